```python
import math
import jax, jax.numpy as jnp
from jax import lax
import numpy as np

D_MODEL = 1024
BATCH = 8
SEQ = 8192
DEPTH = 1

SB_HEADS = 8
SB_HEAD_DIM = 64
SB_WIDTH = SB_HEADS * SB_HEAD_DIM
MLA_HEADS = 8
MLA_NOPE_DIM = 64
MLA_ROPE_DIM = 32
MLA_QK_DIM = MLA_NOPE_DIM + MLA_ROPE_DIM
MLA_V_DIM = 64
MLA_WIDTH = MLA_HEADS * MLA_V_DIM
Q_LORA_RANK = 384
KV_LORA_RANK = 256
ROPE_THETA = 10000.0
MIX_WIDTH = SB_WIDTH + MLA_WIDTH
Q_BLOCK = 128
EPS = 1e-6
IN_SIZES = (SB_WIDTH, SB_WIDTH, SB_WIDTH, SB_WIDTH,
            Q_LORA_RANK, KV_LORA_RANK, MLA_ROPE_DIM, MLA_WIDTH)
IN_COLS = sum(IN_SIZES)

kernel_name = "hymba_stickbreaking_mla_adaln"


def _rms_norm(x, w):
    xf = x.astype(jnp.float32)
    y = xf * lax.rsqrt(jnp.mean(xf * xf, axis=-1, keepdims=True) + EPS)
    return (y * w.astype(jnp.float32)).astype(x.dtype)


def _rotate_half(x):
    x1, x2 = jnp.split(x, 2, axis=-1)
    return jnp.concatenate([-x2, x1], axis=-1)


def _to_blocks(q):
    B, H, S, d = q.shape
    return q.reshape(B, H, S // Q_BLOCK, Q_BLOCK, d).transpose(2, 0, 1, 3, 4)


def _from_blocks(o):
    nb, B, H, QB, d = o.shape
    return o.transpose(1, 2, 0, 3, 4).reshape(B, H, nb * QB, d)


def _stick_breaking_attention(q, k, v):
    S, d = q.shape[2], q.shape[3]
    key_pos = jnp.arange(S)
    inv_sqrt_d = 1.0 / math.sqrt(d)

    def block(args):
        i, qi = args
        q_pos = i * Q_BLOCK + jnp.arange(Q_BLOCK)
        z = jnp.einsum('bhqd,bhkd->bhqk', qi, k).astype(jnp.float32) * inv_sqrt_d
        strict = key_pos[None, :] < q_pos[:, None]
        log_keep = jnp.where(strict, jax.nn.log_sigmoid(-z), 0.0)
        after = lax.cumsum(log_keep, axis=3, reverse=True) - log_keep
        w = jnp.where(strict, jnp.exp(jax.nn.log_sigmoid(z) + after), 0.0)
        return jnp.einsum('bhqk,bhkd->bhqd', w.astype(v.dtype), v)

    out = lax.map(block, (jnp.arange(S // Q_BLOCK), _to_blocks(q)))
    return _from_blocks(out)


def _causal_softmax_attention(q, k, v):
    S, d = q.shape[2], q.shape[3]
    key_pos = jnp.arange(S)
    scale = 1.0 / math.sqrt(d)
    neg = jnp.finfo(jnp.float32).min

    def block(args):
        i, qi = args
        q_pos = i * Q_BLOCK + jnp.arange(Q_BLOCK)
        z = jnp.einsum('bhqd,bhkd->bhqk', qi, k).astype(jnp.float32) * scale
        causal = key_pos[None, :] <= q_pos[:, None]
        p = jax.nn.softmax(jnp.where(causal, z, neg), axis=-1)
        return jnp.einsum('bhqk,bhkd->bhqd', p.astype(v.dtype), v)

    out = lax.map(block, (jnp.arange(S // Q_BLOCK), _to_blocks(q)))
    return _from_blocks(out)


def _layer(x, c, cos, sin, w_ada, b_ada, norm_w, w_in, q_lora_norm, w_uq,
           kv_lora_norm, w_ukv, q_head_norm, k_head_norm, w_out):
    B, S, _ = x.shape
    ada = jax.nn.silu(c) @ w_ada + b_ada
    shift, scale, gate = jnp.split(ada[:, None, :], 3, axis=-1)
    h = _rms_norm(x, norm_w) * (1.0 + scale) + shift

    proj = h @ w_in
    q_sb, k_sb, v_sb, g_sb, c_q, c_kv, k_rope, g_mla = jnp.split(
        proj, np.cumsum(IN_SIZES)[:-1].tolist(), axis=-1)

    def heads_sb(t):
        return t.reshape(B, S, SB_HEADS, SB_HEAD_DIM).transpose(0, 2, 1, 3)
    o_sb = _stick_breaking_attention(heads_sb(q_sb), heads_sb(k_sb), heads_sb(v_sb))
    o_sb = o_sb.transpose(0, 2, 1, 3).reshape(B, S, SB_WIDTH) * jax.nn.silu(g_sb)

    q = (_rms_norm(c_q, q_lora_norm) @ w_uq).reshape(B, S, MLA_HEADS, MLA_QK_DIM)
    kv = (_rms_norm(c_kv, kv_lora_norm) @ w_ukv).reshape(
        B, S, MLA_HEADS, MLA_NOPE_DIM + MLA_V_DIM)
    k_nope, v_mla = kv[..., :MLA_NOPE_DIM], kv[..., MLA_NOPE_DIM:]
    k_r = jnp.broadcast_to(k_rope[:, :, None, :], (B, S, MLA_HEADS, MLA_ROPE_DIM))
    k = jnp.concatenate([k_nope, k_r], axis=-1)
    q = _rms_norm(q, q_head_norm)
    k = _rms_norm(k, k_head_norm)
    def rope(t):
        t_n, t_r = t[..., :MLA_NOPE_DIM], t[..., MLA_NOPE_DIM:]
        t_r = t_r * cos + _rotate_half(t_r) * sin
        return jnp.concatenate([t_n, t_r], axis=-1)
    q, k = rope(q), rope(k)
    o_mla = _causal_softmax_attention(q.transpose(0, 2, 1, 3), k.transpose(0, 2, 1, 3),
                                      v_mla.transpose(0, 2, 1, 3))
    o_mla = o_mla.transpose(0, 2, 1, 3).reshape(B, S, MLA_WIDTH) * jax.nn.silu(g_mla)

    mixed = jnp.concatenate([o_sb, o_mla], axis=-1)
    return x + gate * (mixed @ w_out)


def _fwd_setup_inputs(seed: int = 0) -> dict:
    key = jax.random.key(seed)
    ks = jax.random.split(key, 16)
    D = D_MODEL

    def nrm(k, shape, std):
        return jax.random.normal(k, shape, jnp.float32) * std

    def gain(k, n):
        return 1.0 + 0.01 * jax.random.normal(k, (DEPTH, n), jnp.float32)

    return {
        "x": nrm(ks[0], (BATCH, SEQ, D), 1.0),
        "c": nrm(ks[1], (BATCH, D), 1.0),
        "positions": jnp.broadcast_to(jnp.arange(SEQ, dtype=jnp.int32), (BATCH, SEQ)),
        "w_ada": nrm(ks[2], (DEPTH, D, 3 * D), 0.5 * D ** -0.5),
        "b_ada": nrm(ks[3], (DEPTH, 3 * D), 0.01),
        "norm_w": gain(ks[4], D),
        "w_in": nrm(ks[5], (DEPTH, D, IN_COLS), D ** -0.5),
        "q_lora_norm": gain(ks[6], Q_LORA_RANK),
        "w_uq": nrm(ks[7], (DEPTH, Q_LORA_RANK, MLA_HEADS * MLA_QK_DIM), Q_LORA_RANK ** -0.5),
        "kv_lora_norm": gain(ks[8], KV_LORA_RANK),
        "w_ukv": nrm(ks[9], (DEPTH, KV_LORA_RANK, MLA_HEADS * (MLA_NOPE_DIM + MLA_V_DIM)),
                     KV_LORA_RANK ** -0.5),
        "q_head_norm": gain(ks[10], MLA_QK_DIM),
        "k_head_norm": gain(ks[11], MLA_QK_DIM),
        "w_out": nrm(ks[12], (DEPTH, MIX_WIDTH, D), MIX_WIDTH ** -0.5),
    }


def _fwd_reference(x, c, positions, w_ada, b_ada, norm_w, w_in, q_lora_norm, w_uq,
              kv_lora_norm, w_ukv, q_head_norm, k_head_norm, w_out):
    inv_freq = ROPE_THETA ** (-jnp.arange(0, MLA_ROPE_DIM, 2, dtype=jnp.float32) / MLA_ROPE_DIM)
    ang = positions.astype(jnp.float32)[..., None] * inv_freq
    ang = jnp.concatenate([ang, ang], axis=-1)[:, :, None, :]
    cos = jnp.cos(ang).astype(x.dtype)
    sin = jnp.sin(ang).astype(x.dtype)
    for l in range(DEPTH):
        x = _layer(x, c, cos, sin, w_ada[l], b_ada[l], norm_w[l], w_in[l],
                   q_lora_norm[l], w_uq[l], kv_lora_norm[l], w_ukv[l],
                   q_head_norm[l], k_head_norm[l], w_out[l])
    return x


import jax as _jax
import jax.numpy as _jnp

TWIN_FORMAT = 'train_step'
FWD_PARAMS = ['x', 'c', 'positions', 'w_ada', 'b_ada', 'norm_w', 'w_in', 'q_lora_norm', 'w_uq', 'kv_lora_norm', 'w_ukv', 'q_head_norm', 'k_head_norm', 'w_out']
TWIN_WEIGHTS = ['w_ada', 'b_ada', 'norm_w', 'w_in', 'q_lora_norm', 'w_uq', 'kv_lora_norm', 'w_ukv', 'q_head_norm', 'k_head_norm', 'w_out']
TWIN_DIFF_INPUT = 'x'
TWIN_INPUTS = ['x', 'c', 'positions', 'w_ada', 'b_ada', 'norm_w', 'w_in', 'q_lora_norm', 'w_uq', 'kv_lora_norm', 'w_ukv', 'q_head_norm', 'k_head_norm', 'w_out', 'loss_target', 'm_w_ada', 'm_b_ada', 'm_norm_w', 'm_w_in', 'm_q_lora_norm', 'm_w_uq', 'm_kv_lora_norm', 'm_w_ukv', 'm_q_head_norm', 'm_k_head_norm', 'm_w_out', 'v_w_ada', 'v_b_ada', 'v_norm_w', 'v_w_in', 'v_q_lora_norm', 'v_w_uq', 'v_kv_lora_norm', 'v_w_ukv', 'v_q_head_norm', 'v_k_head_norm', 'v_w_out']
TWIN_OUTPUTS = ['loss', 'grad_x', 'grad_w_ada', 'grad_b_ada', 'grad_norm_w', 'grad_w_in', 'grad_q_lora_norm', 'grad_w_uq', 'grad_kv_lora_norm', 'grad_w_ukv', 'grad_q_head_norm', 'grad_k_head_norm', 'grad_w_out', 'delta_w_ada', 'delta_b_ada', 'delta_norm_w', 'delta_w_in', 'delta_q_lora_norm', 'delta_w_uq', 'delta_kv_lora_norm', 'delta_w_ukv', 'delta_q_head_norm', 'delta_k_head_norm', 'delta_w_out', 'new_m_w_ada', 'new_m_b_ada', 'new_m_norm_w', 'new_m_w_in', 'new_m_q_lora_norm', 'new_m_w_uq', 'new_m_kv_lora_norm', 'new_m_w_ukv', 'new_m_q_head_norm', 'new_m_k_head_norm', 'new_m_w_out', 'new_v_w_ada', 'new_v_b_ada', 'new_v_norm_w', 'new_v_w_in', 'new_v_q_lora_norm', 'new_v_w_uq', 'new_v_kv_lora_norm', 'new_v_w_ukv', 'new_v_q_head_norm', 'new_v_k_head_norm', 'new_v_w_out']
TWIN_LEAF_KINDS = {'loss': 'loss', 'grad_x': 'grad_x', 'grad_w_ada': 'grad_w', 'grad_b_ada': 'grad_w', 'grad_norm_w': 'grad_w', 'grad_w_in': 'grad_w', 'grad_q_lora_norm': 'grad_w', 'grad_w_uq': 'grad_w', 'grad_kv_lora_norm': 'grad_w', 'grad_w_ukv': 'grad_w', 'grad_q_head_norm': 'grad_w', 'grad_k_head_norm': 'grad_w', 'grad_w_out': 'grad_w', 'delta_w_ada': 'delta_w', 'delta_b_ada': 'delta_w', 'delta_norm_w': 'delta_w', 'delta_w_in': 'delta_w', 'delta_q_lora_norm': 'delta_w', 'delta_w_uq': 'delta_w', 'delta_kv_lora_norm': 'delta_w', 'delta_w_ukv': 'delta_w', 'delta_q_head_norm': 'delta_w', 'delta_k_head_norm': 'delta_w', 'delta_w_out': 'delta_w', 'new_m_w_ada': 'new_m', 'new_m_b_ada': 'new_m', 'new_m_norm_w': 'new_m', 'new_m_w_in': 'new_m', 'new_m_q_lora_norm': 'new_m', 'new_m_w_uq': 'new_m', 'new_m_kv_lora_norm': 'new_m', 'new_m_w_ukv': 'new_m', 'new_m_q_head_norm': 'new_m', 'new_m_k_head_norm': 'new_m', 'new_m_w_out': 'new_m', 'new_v_w_ada': 'new_v', 'new_v_b_ada': 'new_v', 'new_v_norm_w': 'new_v', 'new_v_w_in': 'new_v', 'new_v_q_lora_norm': 'new_v', 'new_v_w_uq': 'new_v', 'new_v_kv_lora_norm': 'new_v', 'new_v_w_ukv': 'new_v', 'new_v_q_head_norm': 'new_v', 'new_v_k_head_norm': 'new_v', 'new_v_w_out': 'new_v'}


def _forward(args):
    return _fwd_reference(*[args[k] for k in FWD_PARAMS])


def _output_shape():
    def fwd():
        inp = _fwd_setup_inputs(0)
        return _fwd_reference(*[inp[k] for k in FWD_PARAMS])
    out = _jax.eval_shape(fwd)
    return out.shape, out.dtype

N_MICROBATCH = 1
ADAM_LR = 0.001
ADAM_B1 = 0.9
ADAM_B2 = 0.999
ADAM_EPS = 1e-08
ADAM_WD = 0.01
ADAM_STEP = 10
PER_EXAMPLE_BATCH_AXIS = {'x': 0, 'c': 0, 'positions': 0, 'loss_target': 0}
SHARED_INPUTS = []
_WEIGHT_DTYPES = {'w_ada': _jnp.float32, 'b_ada': _jnp.float32, 'norm_w': _jnp.float32, 'w_in': _jnp.float32, 'q_lora_norm': _jnp.float32, 'w_uq': _jnp.float32, 'kv_lora_norm': _jnp.float32, 'w_ukv': _jnp.float32, 'q_head_norm': _jnp.float32, 'k_head_norm': _jnp.float32, 'w_out': _jnp.float32}
MOMENT_SCALE = {'w_ada': 4.222983e-01, 'b_ada': 1.168115e+00, 'norm_w': 1.540952e+00, 'w_in': 7.604319e-02, 'q_lora_norm': 1.358432e-02, 'w_uq': 9.845253e-03, 'kv_lora_norm': 3.328896e-01, 'w_ukv': 3.702255e-02, 'q_head_norm': 6.283917e-02, 'k_head_norm': 6.231891e-02, 'w_out': 5.034991e-02}


def _to_microbatches(a, axis):
    t = _jnp.moveaxis(a, axis, 0)
    t = t.reshape((N_MICROBATCH, t.shape[0] // N_MICROBATCH) + t.shape[1:])
    return _jnp.moveaxis(t, 1, axis + 1)


def setup_inputs(seed: int = 0) -> dict:
    inp = _fwd_setup_inputs(seed)
    key = _jax.random.fold_in(_jax.random.key(seed), 7919)
    shape, _ = _output_shape()
    out = dict(inp)
    out["loss_target"] = _jax.random.normal(_jax.random.fold_in(key, 0), shape, _jnp.float32)
    for i, name in enumerate(TWIN_WEIGHTS):
        w = inp[name].astype(_jnp.float32)
        if MOMENT_SCALE is None:
            s = _jnp.sqrt(_jnp.mean(_jnp.square(w)) + 1e-30)
        else:
            s = MOMENT_SCALE[name]
        km, kv = _jax.random.split(_jax.random.fold_in(key, i + 1))
        out[name] = w
        out["m_" + name] = s * _jax.random.normal(km, w.shape, _jnp.float32)
        out["v_" + name] = (s * s) * _jax.random.uniform(kv, w.shape, _jnp.float32, 0.5, 1.5)
    if N_MICROBATCH > 1:
        for name, axis in PER_EXAMPLE_BATCH_AXIS.items():
            out[name] = _to_microbatches(out[name], axis)
    return {'x': out['x'], 'c': out['c'], 'positions': out['positions'], 'w_ada': out['w_ada'], 'b_ada': out['b_ada'], 'norm_w': out['norm_w'], 'w_in': out['w_in'], 'q_lora_norm': out['q_lora_norm'], 'w_uq': out['w_uq'], 'kv_lora_norm': out['kv_lora_norm'], 'w_ukv': out['w_ukv'], 'q_head_norm': out['q_head_norm'], 'k_head_norm': out['k_head_norm'], 'w_out': out['w_out'], 'loss_target': out['loss_target'], 'm_w_ada': out['m_w_ada'], 'm_b_ada': out['m_b_ada'], 'm_norm_w': out['m_norm_w'], 'm_w_in': out['m_w_in'], 'm_q_lora_norm': out['m_q_lora_norm'], 'm_w_uq': out['m_w_uq'], 'm_kv_lora_norm': out['m_kv_lora_norm'], 'm_w_ukv': out['m_w_ukv'], 'm_q_head_norm': out['m_q_head_norm'], 'm_k_head_norm': out['m_k_head_norm'], 'm_w_out': out['m_w_out'], 'v_w_ada': out['v_w_ada'], 'v_b_ada': out['v_b_ada'], 'v_norm_w': out['v_norm_w'], 'v_w_in': out['v_w_in'], 'v_q_lora_norm': out['v_q_lora_norm'], 'v_w_uq': out['v_w_uq'], 'v_kv_lora_norm': out['v_kv_lora_norm'], 'v_w_ukv': out['v_w_ukv'], 'v_q_head_norm': out['v_q_head_norm'], 'v_k_head_norm': out['v_k_head_norm'], 'v_w_out': out['v_w_out']}


def _loss(weights, diff, rest, loss_target):
    with _jax.named_scope("forward"):
        args = {**rest, TWIN_DIFF_INPUT: diff, **{k: w.astype(_WEIGHT_DTYPES[k]) for k, w in weights.items()}}
        y = _forward(args)
    with _jax.named_scope("loss_head"):
        err = _jnp.square(y.astype(_jnp.float32) - loss_target)
        return 0.5 * _jnp.sum(_jnp.mean(err, axis=-1)) if err.ndim else 0.5 * err


def _adamw(w, g, m, v):
    m = ADAM_B1 * m + (1.0 - ADAM_B1) * g
    v = ADAM_B2 * v + (1.0 - ADAM_B2) * _jnp.square(g)
    m_hat = m / (1.0 - ADAM_B1 ** ADAM_STEP)
    v_hat = v / (1.0 - ADAM_B2 ** ADAM_STEP)
    delta = -ADAM_LR * (m_hat / (_jnp.sqrt(v_hat) + ADAM_EPS) + ADAM_WD * w)
    return delta, m, v


def reference(x, c, positions, w_ada, b_ada, norm_w, w_in, q_lora_norm, w_uq, kv_lora_norm, w_ukv, q_head_norm, k_head_norm, w_out, loss_target, m_w_ada, m_b_ada, m_norm_w, m_w_in, m_q_lora_norm, m_w_uq, m_kv_lora_norm, m_w_ukv, m_q_head_norm, m_k_head_norm, m_w_out, v_w_ada, v_b_ada, v_norm_w, v_w_in, v_q_lora_norm, v_w_uq, v_kv_lora_norm, v_w_ukv, v_q_head_norm, v_k_head_norm, v_w_out):
    given = dict(x=x, c=c, positions=positions, w_ada=w_ada, b_ada=b_ada, norm_w=norm_w, w_in=w_in, q_lora_norm=q_lora_norm, w_uq=w_uq, kv_lora_norm=kv_lora_norm, w_ukv=w_ukv, q_head_norm=q_head_norm, k_head_norm=k_head_norm, w_out=w_out, loss_target=loss_target, m_w_ada=m_w_ada, m_b_ada=m_b_ada, m_norm_w=m_norm_w, m_w_in=m_w_in, m_q_lora_norm=m_q_lora_norm, m_w_uq=m_w_uq, m_kv_lora_norm=m_kv_lora_norm, m_w_ukv=m_w_ukv, m_q_head_norm=m_q_head_norm, m_k_head_norm=m_k_head_norm, m_w_out=m_w_out, v_w_ada=v_w_ada, v_b_ada=v_b_ada, v_norm_w=v_norm_w, v_w_in=v_w_in, v_q_lora_norm=v_q_lora_norm, v_w_uq=v_w_uq, v_kv_lora_norm=v_kv_lora_norm, v_w_ukv=v_w_ukv, v_q_head_norm=v_q_head_norm, v_k_head_norm=v_k_head_norm, v_w_out=v_w_out)
    weights = {n: given[n] for n in TWIN_WEIGHTS}
    shared = {n: given[n] for n in SHARED_INPUTS}
    per_example = {n: given[n] for n in ['x', 'c', 'positions']}
    grad_fn = _jax.value_and_grad(_loss, argnums=(0, 1))

    def one_microbatch(ex, loss_target):
        ex = dict(ex)
        diff = ex.pop(TWIN_DIFF_INPUT)
        return grad_fn(weights, diff, {**shared, **ex}, loss_target)

    if N_MICROBATCH == 1:
        loss, (grad_w, grad_x) = one_microbatch(per_example, given["loss_target"])
    else:
        def body(carry, xs):
            loss_sum, grad_sum = carry
            l_k, (gw_k, gx_k) = one_microbatch(xs[0], xs[1])
            with _jax.named_scope("update"):
                return (loss_sum + l_k, _jax.tree.map(_jnp.add, grad_sum, gw_k)), gx_k

        init = (_jnp.zeros((), _jnp.float32), _jax.tree.map(_jnp.zeros_like, weights))
        (loss, grad_w), grad_x = _jax.lax.scan(body, init, (per_example, given["loss_target"]))
    with _jax.named_scope("update"):
        delta_w, new_m, new_v = {}, {}, {}
        for n in TWIN_WEIGHTS:
            delta_w[n], new_m[n], new_v[n] = _adamw(weights[n], grad_w[n], given["m_" + n], given["v_" + n])
    return (loss, grad_x, *[grad_w[n] for n in TWIN_WEIGHTS], *[delta_w[n] for n in TWIN_WEIGHTS],
            *[new_m[n] for n in TWIN_WEIGHTS], *[new_v[n] for n in TWIN_WEIGHTS])
```

```python
import functools
import math

import jax
import jax.numpy as jnp
from jax import lax
from jax.experimental import pallas as pl
from jax.experimental.pallas import tpu as pltpu

F32 = jnp.float32
BF16 = jnp.bfloat16
I32 = jnp.int32

D_MODEL = 1024
SB_HEADS = 8
SB_WIDTH = 512
MLA_HEADS = 8
MLA_QK_DIM = 96
MLA_NOPE_DIM = 64
MLA_ROPE_DIM = 32
MLA_WIDTH = 512
Q_LORA_RANK = 384
KV_LORA_RANK = 256
ROPE_THETA = 10000.0
EPS = 1e-6
LANES = 128
HEAD_PAD = 128
MLA_PAD_WIDTH = MLA_HEADS * HEAD_PAD

C_Q, C_K, C_V, C_G = 0, 512, 1024, 1536
C_CQ, C_CKV, C_GM, C_KR = 2048, 2432, 2688, 3200
IN_COLS_PAD = 3328
KR_LANE = 64

ADAM_LR = 0.001
ADAM_B1 = 0.9
ADAM_B2 = 0.999
ADAM_EPS = 1e-08
ADAM_WD = 0.01
ADAM_STEP = 10

SB_SCALE = 0.125
MLA_SCALE = 1.0 / math.sqrt(MLA_QK_DIM)
SB_DEAD = -104.0
MASK_NEG = -1e30

VMEM_LIMIT = 56 * 1024 * 1024
MESH = pl.DeviceIdType.MESH


def _dot(a, b):
    return jnp.dot(a, b, preferred_element_type=F32)


def _dot_nt(a, b):
    return lax.dot_general(a, b, (((1,), (1,)), ((), ())), preferred_element_type=F32)


def _dot_tn(a, b):
    return lax.dot_general(a, b, (((0,), (0,)), ((), ())), preferred_element_type=F32)


def _sigmoid(x):
    return 1.0 / (1.0 + jnp.exp(-x))


def _split_dot(a, m):
    hi = a.astype(BF16)
    lo = (a - hi.astype(F32)).astype(BF16)
    return _dot(hi, m) + _dot(lo, m)


def _params(sem, vmem=None):
    return pltpu.CompilerParams(dimension_semantics=sem, vmem_limit_bytes=vmem)


def _row_tile(s, want):
    return min(want, s)


def _hbm_spec():
    return pl.BlockSpec(memory_space=pltpu.HBM)


def _gather_call(shards):
    n = len(shards)

    def body(*refs):
        ins, outs = refs[:n], refs[n:2 * n]
        send_sems, recv_sems, loc_sems = refs[2 * n:]
        x, y, c = lax.axis_index("x"), lax.axis_index("y"), lax.axis_index("c")
        me = 2 * x + y
        peers = [(1 - x, y), (x, 1 - y), (1 - x, 1 - y)]

        def copy(a, j, slot):
            px, py = peers[j]
            return pltpu.make_async_remote_copy(
                src_ref=ins[a].at[0], dst_ref=outs[a].at[slot],
                send_sem=send_sems.at[3 * a + j], recv_sem=recv_sems.at[3 * a + j],
                device_id=(px, py, c), device_id_type=MESH)

        local = [pltpu.make_async_copy(ins[a].at[0], outs[a].at[me], loc_sems.at[a]) for a in range(n)]
        for cp in local:
            cp.start()
        sends = [copy(a, j, me) for a in range(n) for j in range(3)]
        for cp in sends:
            cp.start()
        for a in range(n):
            for j in range(3):
                px, py = peers[j]
                copy(a, j, 2 * px + py).wait_recv()
        for cp in sends:
            cp.wait_send()
        for cp in local:
            cp.wait()

    return pl.pallas_call(
        body, name="gather_weights",
        out_shape=[jax.ShapeDtypeStruct((4,) + s.shape[1:], s.dtype) for s in shards],
        in_specs=[_hbm_spec() for _ in shards],
        out_specs=[_hbm_spec() for _ in shards],
        scratch_shapes=[pltpu.SemaphoreType.DMA((3 * n,)), pltpu.SemaphoreType.DMA((3 * n,)),
                        pltpu.SemaphoreType.DMA((n,))],
    )(*shards)


def _exchange_call(grads, small):
    n = len(grads)

    def body(*refs):
        g_in, small_in = refs[:n], refs[n]
        own, sib, packs = refs[n + 1:2 * n + 1], refs[2 * n + 1:3 * n + 1], refs[3 * n + 1]
        ici_send, ici_recv, d2d_send, d2d_recv, sm_send, sm_recv, loc_sems = refs[3 * n + 2:]
        x, y, c = lax.axis_index("x"), lax.axis_index("y"), lax.axis_index("c")
        me = 2 * x + y
        me8 = 4 * x + 2 * y + c
        sibling = (x, y, 1 - c)
        peers = [(1 - x, y), (x, 1 - y), (1 - x, 1 - y)]
        flips = [(fx, fy, fc) for fx in (0, 1) for fy in (0, 1) for fc in (0, 1)][1:]

        def ici(a, j, src_slot, dst_slot):
            px, py = peers[j]
            return pltpu.make_async_remote_copy(
                src_ref=g_in[a].at[src_slot], dst_ref=own[a].at[dst_slot],
                send_sem=ici_send.at[3 * a + j], recv_sem=ici_recv.at[3 * a + j],
                device_id=(px, py, c), device_id_type=MESH)

        def d2d(a, rel, chip, src):
            return pltpu.make_async_remote_copy(
                src_ref=src, dst_ref=sib[a].at[chip],
                send_sem=d2d_send.at[4 * a + rel], recv_sem=d2d_recv.at[4 * a + rel],
                device_id=sibling, device_id_type=MESH)

        def flipped(r):
            fx, fy, fc = flips[r]
            return ((1 - x) if fx else x, (1 - y) if fy else y, (1 - c) if fc else c)

        def sm(r, slot):
            return pltpu.make_async_remote_copy(
                src_ref=small_in, dst_ref=packs.at[slot],
                send_sem=sm_send.at[r], recv_sem=sm_recv.at[r],
                device_id=flipped(r), device_id_type=MESH)

        def peer8(r):
            px, py, pc = flipped(r)
            return 4 * px + 2 * py + pc

        local = [pltpu.make_async_copy(g_in[a].at[me], own[a].at[me], loc_sems.at[a]) for a in range(n)]
        local.append(pltpu.make_async_copy(small_in, packs.at[me8], loc_sems.at[n]))
        for cp in local:
            cp.start()
        sends = []
        for r in range(7):
            sends.append(sm(r, me8))
        for a in range(n):
            for j in range(3):
                px, py = peers[j]
                sends.append(ici(a, j, 2 * px + py, me))
        for cp in sends:
            cp.start()
        for a in range(n):
            cp = d2d(a, 0, me, g_in[a].at[me])
            cp.start()
            sends.append(cp)
        for a in range(n):
            for j in range(3):
                px, py = peers[j]
                ici(a, j, me, 2 * px + py).wait_recv()
                cp = d2d(a, 1 + j, 2 * px + py, own[a].at[2 * px + py])
                cp.start()
                sends.append(cp)
        for a in range(n):
            d2d(a, 0, me, g_in[a].at[me]).wait_recv()
            for j in range(3):
                px, py = peers[j]
                d2d(a, 1 + j, 2 * px + py, g_in[a].at[me]).wait_recv()
        for r in range(7):
            sm(r, peer8(r)).wait_recv()
        for cp in sends:
            cp.wait_send()
        for cp in local:
            cp.wait()

    out_shape = ([jax.ShapeDtypeStruct(g.shape, g.dtype) for g in grads] * 2
                 + [jax.ShapeDtypeStruct((8,) + small.shape, small.dtype)])
    res = pl.pallas_call(
        body, name="exchange_grads",
        out_shape=out_shape,
        in_specs=[_hbm_spec() for _ in range(n + 1)],
        out_specs=[_hbm_spec() for _ in range(2 * n + 1)],
        scratch_shapes=[pltpu.SemaphoreType.DMA((3 * n,)), pltpu.SemaphoreType.DMA((3 * n,)),
                        pltpu.SemaphoreType.DMA((4 * n,)), pltpu.SemaphoreType.DMA((4 * n,)),
                        pltpu.SemaphoreType.DMA((7,)), pltpu.SemaphoreType.DMA((7,)),
                        pltpu.SemaphoreType.DMA((n + 1,))],
    )(*grads, small)
    return res[:n], res[n:2 * n], res[2 * n]


def _ada_call(c, w_ada_bf, b_ada):
    def body(c_ref, w_ref, b_ref, o_ref):
        cc = c_ref[...]
        sc = jnp.broadcast_to(cc * _sigmoid(cc), (8, D_MODEL)).astype(BF16)
        o_ref[...] = _dot(sc, w_ref[...]) + b_ref[...]

    out = pl.pallas_call(
        body, name="ada_fwd",
        out_shape=jax.ShapeDtypeStruct((8, 3 * D_MODEL), F32),
        compiler_params=pltpu.CompilerParams(vmem_limit_bytes=VMEM_LIMIT),
    )(c, w_ada_bf, b_ada)
    return out[0:1]


def _full(shape):
    return pl.BlockSpec(shape, lambda i: (0,) * len(shape))


def _rows(tm, width):
    return pl.BlockSpec((tm, width), lambda i: (i, 0))


def _pre_call(x, shift, scale, norm_w, w_in_bf):
    s = x.shape[0]
    tm = _row_tile(s, 512)
    groups = [(C_Q, 512, BF16), (C_K, 512, BF16), (C_V, 512, BF16), (C_G, 512, F32),
              (C_CQ, Q_LORA_RANK, F32), (C_CKV, KV_LORA_RANK, F32), (C_GM, 512, F32), (C_KR, LANES, F32)]

    def body(x_ref, sh_ref, sc_ref, nw_ref, w_ref, hb_ref, *outs):
        xx = x_ref[...]
        r0 = lax.rsqrt(jnp.mean(xx * xx, axis=-1, keepdims=True) + EPS)
        h = (xx * r0 * nw_ref[...]) * (1.0 + sc_ref[...]) + sh_ref[...]
        hb = h.astype(BF16)
        hb_ref[...] = hb
        for (c0, width, dt), o_ref in zip(groups, outs):
            o_ref[...] = _dot(hb, w_ref[:, c0:c0 + width]).astype(dt)

    return pl.pallas_call(
        body, name="pre_proj",
        grid=(s // tm,),
        in_specs=[_rows(tm, D_MODEL), _full((1, D_MODEL)), _full((1, D_MODEL)), _full((1, D_MODEL)),
                  _full((D_MODEL, IN_COLS_PAD))],
        out_specs=[_rows(tm, D_MODEL)] + [_rows(tm, w) for _, w, _ in groups],
        out_shape=[jax.ShapeDtypeStruct((s, D_MODEL), BF16)]
        + [jax.ShapeDtypeStruct((s, w), dt) for _, w, dt in groups],
        compiler_params=_params(("arbitrary",), VMEM_LIMIT),
    )(x, shift, scale, norm_w, w_in_bf)


def _rope(t, cos_t, sin_a, sin_b):
    return t * cos_t + pltpu.roll(t, 112, 1) * sin_a + pltpu.roll(t, 16, 1) * sin_b


def _rope_adjoint(d, cos_t, sin_a, sin_b):
    return d * cos_t + pltpu.roll(d * sin_a, 16, 1) + pltpu.roll(d * sin_b, 112, 1)


def _mla_prep_call(c_q, c_kv, k_rope, cos_t, sin_a, sin_b, q_lora_norm, kv_lora_norm, qhn_pad, khn_pad,
                   w_uq_bf, w_uk_bf, w_uv_bf):
    s = c_q.shape[0]
    tm = _row_tile(s, 256)

    def body(cq_ref, ckv_ref, kr_ref, cos_ref, sa_ref, sb_ref, qln_ref, kvln_ref, qhn_ref, khn_ref,
             wuq_ref, wuk_ref, wuv_ref, q_ref, k_ref, v_ref, cqn_ref, ckvn_ref, q0_ref, k0_ref):
        cq = cq_ref[...]
        cqn = (cq * lax.rsqrt(jnp.mean(cq * cq, axis=-1, keepdims=True) + EPS) * qln_ref[...]).astype(BF16)
        cqn_ref[...] = cqn
        ckv = ckv_ref[...]
        ckvn = (ckv * lax.rsqrt(jnp.mean(ckv * ckv, axis=-1, keepdims=True) + EPS) * kvln_ref[...]).astype(BF16)
        ckvn_ref[...] = ckvn
        v_ref[...] = _dot(ckvn, wuv_ref[...]).astype(BF16)
        cos_t, sin_a, sin_b = cos_ref[...], sa_ref[...], sb_ref[...]
        kr = kr_ref[...]
        for h in range(MLA_HEADS):
            cols = slice(h * HEAD_PAD, (h + 1) * HEAD_PAD)
            q0 = _dot(cqn, wuq_ref[:, cols])
            q0_ref[:, cols] = q0
            rq = lax.rsqrt(jnp.sum(q0 * q0, axis=-1, keepdims=True) * (1.0 / MLA_QK_DIM) + EPS)
            q_ref[:, cols] = _rope(q0 * rq * qhn_ref[...], cos_t, sin_a, sin_b).astype(BF16)
            k0 = _dot(ckvn, wuk_ref[:, cols]) + kr
            k0_ref[:, cols] = k0
            rk = lax.rsqrt(jnp.sum(k0 * k0, axis=-1, keepdims=True) * (1.0 / MLA_QK_DIM) + EPS)
            k_ref[:, cols] = _rope(k0 * rk * khn_ref[...], cos_t, sin_a, sin_b).astype(BF16)

    return pl.pallas_call(
        body, name="mla_prep",
        grid=(s // tm,),
        in_specs=[_rows(tm, Q_LORA_RANK), _rows(tm, KV_LORA_RANK), _rows(tm, LANES),
                  _rows(tm, LANES), _rows(tm, LANES), _rows(tm, LANES),
                  _full((1, Q_LORA_RANK)), _full((1, KV_LORA_RANK)), _full((1, LANES)), _full((1, LANES)),
                  _full((Q_LORA_RANK, MLA_PAD_WIDTH)), _full((KV_LORA_RANK, MLA_PAD_WIDTH)),
                  _full((KV_LORA_RANK, MLA_WIDTH))],
        out_specs=[_rows(tm, MLA_PAD_WIDTH), _rows(tm, MLA_PAD_WIDTH), _rows(tm, MLA_WIDTH),
                   _rows(tm, Q_LORA_RANK), _rows(tm, KV_LORA_RANK),
                   _rows(tm, MLA_PAD_WIDTH), _rows(tm, MLA_PAD_WIDTH)],
        out_shape=[jax.ShapeDtypeStruct((s, MLA_PAD_WIDTH), BF16), jax.ShapeDtypeStruct((s, MLA_PAD_WIDTH), BF16),
                   jax.ShapeDtypeStruct((s, MLA_WIDTH), BF16),
                   jax.ShapeDtypeStruct((s, Q_LORA_RANK), BF16), jax.ShapeDtypeStruct((s, KV_LORA_RANK), BF16),
                   jax.ShapeDtypeStruct((s, MLA_PAD_WIDTH), F32), jax.ShapeDtypeStruct((s, MLA_PAD_WIDTH), F32)],
        compiler_params=_params(("arbitrary",), VMEM_LIMIT),
    )(c_q, c_kv, k_rope, cos_t, sin_a, sin_b, q_lora_norm, kv_lora_norm, qhn_pad, khn_pad,
      w_uq_bf, w_uk_bf, w_uv_bf)


def _log_sigmoid_pair(z):
    ls = jnp.minimum(z, 0.0) - jnp.log1p(jnp.exp(-jnp.abs(z)))
    return ls, ls - z


def _sb_fwd_call(q, k, v):
    s = q.shape[0]
    bq = _row_tile(s, 256)
    nq = s // bq

    def body(q_ref, k_ref, v_ref, o_ref, r_ref, ks_ref):
        hp, i = pl.program_id(0), pl.program_id(1)
        lane = lax.broadcasted_iota(I32, (bq, LANES), 1)
        row = lax.broadcasted_iota(I32, (bq, bq), 0)
        col = lax.broadcasted_iota(I32, (bq, bq), 1)
        strict = col < row
        later = jnp.where(row > col, 1.0, 0.0).astype(BF16)
        qs = q_ref[...] * jnp.asarray(SB_SCALE, BF16)
        accs, finals = [], []
        for hh in range(2):
            qm = qs * _head_mask(lane, hh).astype(BF16)

            def tile(kb, run, acc, masked, qm=qm):
                rows = pl.ds(pl.multiple_of(kb * bq, bq), bq)
                z = _dot_nt(qm, k_ref[rows, :])
                ls, lk = _log_sigmoid_pair(z)
                if masked:
                    lk = jnp.where(strict, lk, 0.0)
                w = jnp.exp(ls + _split_dot(lk, later) + run)
                if masked:
                    w = jnp.where(strict, w, 0.0)
                acc = acc + _dot(w.astype(BF16), v_ref[rows, :])
                return run + jnp.sum(lk, axis=1, keepdims=True), acc

            run, acc = tile(i, jnp.zeros((bq, 1), F32), jnp.zeros((bq, LANES), F32), True)

            def cond(carry):
                kb, alive, _, _ = carry
                return jnp.logical_and(kb >= 0, alive > 0)

            def step(carry, tile=tile):
                kb, _, run, acc = carry
                run, acc = tile(kb, run, acc, False)
                return kb - 1, (jnp.max(run) > SB_DEAD).astype(I32), run, acc

            kb, _, run, acc = lax.while_loop(
                cond, step, (i - 1, (jnp.max(run) > SB_DEAD).astype(I32), run, acc))
            ks_ref[2 * hp + hh, i] = kb + 1
            accs.append(acc)
            finals.append(run)
        o_ref[...] = jnp.where(lane < 64, accs[0], accs[1])
        r_ref[...] = jnp.where(lane < 64, finals[0], finals[1])

    return pl.pallas_call(
        body, name="sb_fwd",
        grid=(4, nq),
        in_specs=[pl.BlockSpec((bq, LANES), lambda h, i: (i, h)),
                  pl.BlockSpec((s, LANES), lambda h, i: (0, h)),
                  pl.BlockSpec((s, LANES), lambda h, i: (0, h))],
        out_specs=[pl.BlockSpec((bq, LANES), lambda h, i: (i, h)),
                   pl.BlockSpec((bq, LANES), lambda h, i: (i, h)),
                   pl.BlockSpec(memory_space=pltpu.SMEM)],
        out_shape=[jax.ShapeDtypeStruct((s, SB_WIDTH), F32), jax.ShapeDtypeStruct((s, SB_WIDTH), F32),
                   jax.ShapeDtypeStruct((SB_HEADS, nq), I32)],
        compiler_params=_params(("arbitrary", "arbitrary"), VMEM_LIMIT),
    )(q, k, v)


def _mla_fwd_call(q, k, v):
    s = q.shape[0]
    bq = _row_tile(s, 256)
    nq = s // bq

    def body(q_ref, k_ref, v_ref, o_ref, lse_ref):
        i = pl.program_id(1)
        lane = lax.broadcasted_iota(I32, (bq, LANES), 1)
        row = lax.broadcasted_iota(I32, (bq, bq), 0)
        col = lax.broadcasted_iota(I32, (bq, bq), 1)
        causal = col <= row
        outs, lses = [], []
        for hh in range(2):
            cols = slice(hh * HEAD_PAD, (hh + 1) * HEAD_PAD)
            qh = q_ref[:, cols]

            def tile(kb, carry, masked, qh=qh, cols=cols):
                m, l, acc = carry
                rows = pl.ds(pl.multiple_of(kb * bq, bq), bq)
                sc = _dot_nt(qh, k_ref[rows, cols]) * MLA_SCALE
                if masked:
                    sc = jnp.where(causal, sc, MASK_NEG)
                m_new = jnp.maximum(m, jnp.max(sc, axis=1, keepdims=True))
                p = jnp.exp(sc - m_new)
                alpha = jnp.exp(m - m_new)
                l = alpha * l + jnp.sum(p, axis=1, keepdims=True)
                acc = alpha * acc + _dot(p.astype(BF16), v_ref[rows, :])
                return m_new, l, acc

            init = (jnp.full((bq, 1), MASK_NEG, F32), jnp.zeros((bq, 1), F32), jnp.zeros((bq, LANES), F32))
            carry = lax.fori_loop(0, i, lambda kb, cr, tile=tile: tile(kb, cr, False), init)
            m, l, acc = tile(i, carry, True)
            outs.append(acc / l)
            lses.append(m + jnp.log(l))
        o_ref[...] = jnp.where(lane < 64, outs[0], outs[1])
        lse_ref[...] = jnp.where(lane < 64, lses[0], lses[1])

    return pl.pallas_call(
        body, name="mla_fwd",
        grid=(4, nq),
        in_specs=[pl.BlockSpec((bq, 2 * HEAD_PAD), lambda h, i: (i, h)),
                  pl.BlockSpec((s, 2 * HEAD_PAD), lambda h, i: (0, h)),
                  pl.BlockSpec((s, LANES), lambda h, i: (0, h))],
        out_specs=[pl.BlockSpec((bq, LANES), lambda h, i: (i, h)),
                   pl.BlockSpec((bq, LANES), lambda h, i: (i, h))],
        out_shape=[jax.ShapeDtypeStruct((s, MLA_WIDTH), F32), jax.ShapeDtypeStruct((s, MLA_WIDTH), F32)],
        compiler_params=_params(("arbitrary", "arbitrary"), VMEM_LIMIT),
    )(q, k, v)


def _out_call(o_sb, g_sb, o_mla, g_mla, x, target, gate, w_out_bf):
    s = x.shape[0]
    tm = _row_tile(s, 256)

    def body(osb_ref, gsb_ref, oml_ref, gml_ref, x_ref, t_ref, gate_ref, w_ref,
             dosb_ref, doml_ref, dgsb_ref, dgml_ref, dy_ref, gw_ref, dgate_ref, sq_ref):
        @pl.when(pl.program_id(0) == 0)
        def _():
            gw_ref[...] = jnp.zeros_like(gw_ref)
            dgate_ref[...] = jnp.zeros_like(dgate_ref)
            sq_ref[...] = jnp.zeros_like(sq_ref)

        g_s, g_m = gsb_ref[...], gml_ref[...]
        sig_s, sig_m = _sigmoid(g_s), _sigmoid(g_m)
        silu_s, silu_m = g_s * sig_s, g_m * sig_m
        o_s, o_m = osb_ref[...], oml_ref[...]
        mixed = jnp.concatenate([o_s * silu_s, o_m * silu_m], axis=1).astype(BF16)
        u = _dot(mixed, w_ref[...])
        gate_v = gate_ref[...]
        err = x_ref[...] + gate_v * u - t_ref[...]
        sq_ref[...] += jnp.sum(err * err, axis=0, keepdims=True)
        dy = err * (1.0 / D_MODEL)
        dy_ref[...] = dy
        dgate_ref[...] += jnp.sum(dy * u, axis=0, keepdims=True)
        du = (dy * gate_v).astype(BF16)
        gw_ref[...] += _dot_tn(mixed, du)
        dmix = _dot_nt(du, w_ref[...])
        dm_s, dm_m = dmix[:, :SB_WIDTH], dmix[:, SB_WIDTH:]
        dosb_ref[...] = (dm_s * silu_s).astype(BF16)
        doml_ref[...] = (dm_m * silu_m).astype(BF16)
        dgsb_ref[...] = (dm_s * o_s * (sig_s * (1.0 + g_s * (1.0 - sig_s)))).astype(BF16)
        dgml_ref[...] = (dm_m * o_m * (sig_m * (1.0 + g_m * (1.0 - sig_m)))).astype(BF16)

    return pl.pallas_call(
        body, name="out_proj_loss",
        grid=(s // tm,),
        in_specs=[_rows(tm, 512), _rows(tm, 512), _rows(tm, 512), _rows(tm, 512),
                  _rows(tm, D_MODEL), _rows(tm, D_MODEL), _full((1, D_MODEL)), _full((D_MODEL, D_MODEL))],
        out_specs=[_rows(tm, 512), _rows(tm, 512), _rows(tm, 512), _rows(tm, 512), _rows(tm, D_MODEL),
                   _full((D_MODEL, D_MODEL)), _full((1, D_MODEL)), _full((1, D_MODEL))],
        out_shape=[jax.ShapeDtypeStruct((s, 512), BF16)] * 4
        + [jax.ShapeDtypeStruct((s, D_MODEL), F32), jax.ShapeDtypeStruct((D_MODEL, D_MODEL), F32),
           jax.ShapeDtypeStruct((1, D_MODEL), F32), jax.ShapeDtypeStruct((1, D_MODEL), F32)],
        compiler_params=_params(("arbitrary",), VMEM_LIMIT),
    )(o_sb, g_sb, o_mla, g_mla, x, target, gate, w_out_bf)


def _head_mask(lane, hh):
    return jnp.where((lane >= 64) if hh else (lane < 64), 1.0, 0.0)


def _pick_lane(packed, lane, which):
    return jnp.sum(jnp.where(lane == which, packed, 0.0), axis=1, keepdims=True)


def _sb_bwd_call(kstart, q, k, v, do, rfin):
    s = q.shape[0]
    bq = _row_tile(s, 256)
    nq = s // bq

    def body(ks_ref, q_ref, k_ref, v_ref, do_ref, r_ref, dq_ref, dk_ref, dv_ref):
        hp, i = pl.program_id(0), pl.program_id(1)

        @pl.when(i == 0)
        def _():
            dk_ref[...] = jnp.zeros_like(dk_ref)
            dv_ref[...] = jnp.zeros_like(dv_ref)

        lane = lax.broadcasted_iota(I32, (bq, LANES), 1)
        row = lax.broadcasted_iota(I32, (bq, bq), 0)
        col = lax.broadcasted_iota(I32, (bq, bq), 1)
        strict = col < row
        upto = jnp.where(row <= col, 1.0, 0.0).astype(BF16)
        before = jnp.where(row < col, 1.0, 0.0).astype(BF16)
        qs = q_ref[...] * jnp.asarray(SB_SCALE, BF16)
        do2 = do_ref[...]
        r2 = r_ref[...]
        dqs = []
        for hh in range(2):
            head = _head_mask(lane, hh).astype(BF16)
            qm = qs * head
            dom = do2 * head
            total = _pick_lane(r2, lane, 64 * hh)

            def tile(kb, carry, masked, qm=qm, dom=dom, total=total):
                pre, gpre, dq = carry
                rows = pl.ds(pl.multiple_of(kb * bq, bq), bq)
                kblk, vblk = k_ref[rows, :], v_ref[rows, :]
                z = _dot_nt(qm, kblk)
                ls, lk = _log_sigmoid_pair(z)
                if masked:
                    lk = jnp.where(strict, lk, 0.0)
                w = jnp.exp(ls + ((total - pre) - _split_dot(lk, upto)))
                if masked:
                    w = jnp.where(strict, w, 0.0)
                g = w * _dot_nt(dom, vblk)
                gsum = gpre + _split_dot(g, before)
                dz = g - jnp.exp(ls) * (g + gsum)
                if masked:
                    dz = jnp.where(strict, dz, 0.0)
                dzb = dz.astype(BF16)
                dk_ref[rows, :] += _dot_tn(dzb, qm)
                dv_ref[rows, :] += _dot_tn(w.astype(BF16), dom)
                return (pre + jnp.sum(lk, axis=1, keepdims=True), gpre + jnp.sum(g, axis=1, keepdims=True),
                        dq + _dot(dzb, kblk))

            init = (jnp.zeros((bq, 1), F32), jnp.zeros((bq, 1), F32), jnp.zeros((bq, LANES), F32))
            carry = lax.fori_loop(ks_ref[2 * hp + hh, i], i, lambda kb, cr, tile=tile: tile(kb, cr, False), init)
            dqs.append(tile(i, carry, True)[2])
        dq_ref[...] = (jnp.where(lane < 64, dqs[0], dqs[1]) * SB_SCALE).astype(BF16)

    return pl.pallas_call(
        body, name="sb_bwd",
        grid_spec=pltpu.PrefetchScalarGridSpec(
            num_scalar_prefetch=1, grid=(4, nq),
            in_specs=[pl.BlockSpec((bq, LANES), lambda h, i, ks: (i, h)),
                      pl.BlockSpec((s, LANES), lambda h, i, ks: (0, h)),
                      pl.BlockSpec((s, LANES), lambda h, i, ks: (0, h)),
                      pl.BlockSpec((bq, LANES), lambda h, i, ks: (i, h)),
                      pl.BlockSpec((bq, LANES), lambda h, i, ks: (i, h))],
            out_specs=[pl.BlockSpec((bq, LANES), lambda h, i, ks: (i, h)),
                       pl.BlockSpec((s, LANES), lambda h, i, ks: (0, h)),
                       pl.BlockSpec((s, LANES), lambda h, i, ks: (0, h))]),
        out_shape=[jax.ShapeDtypeStruct((s, SB_WIDTH), BF16), jax.ShapeDtypeStruct((s, SB_WIDTH), F32),
                   jax.ShapeDtypeStruct((s, SB_WIDTH), F32)],
        compiler_params=_params(("arbitrary", "arbitrary"), VMEM_LIMIT),
    )(kstart, q, k, v, do, rfin)


def _mla_bwd_call(q, k, v, do, o, lse):
    s = q.shape[0]
    bq = _row_tile(s, 256)
    nq = s // bq

    def body(q_ref, k_ref, v_ref, do_ref, o_ref, lse_ref, dq_ref, dk_ref, dv_ref):
        i = pl.program_id(1)

        @pl.when(i == 0)
        def _():
            dk_ref[...] = jnp.zeros_like(dk_ref)
            dv_ref[...] = jnp.zeros_like(dv_ref)

        lane = lax.broadcasted_iota(I32, (bq, LANES), 1)
        row = lax.broadcasted_iota(I32, (bq, bq), 0)
        col = lax.broadcasted_iota(I32, (bq, bq), 1)
        causal = col <= row
        do2 = do_ref[...]
        prod = do2.astype(F32) * o_ref[...]
        lse2 = lse_ref[...]
        for hh in range(2):
            head = _head_mask(lane, hh)
            cols = slice(hh * HEAD_PAD, (hh + 1) * HEAD_PAD)
            qh = q_ref[:, cols]
            dom = do2 * head.astype(BF16)
            delta = jnp.sum(prod * head, axis=1, keepdims=True)
            lse_h = _pick_lane(lse2, lane, 64 * hh)

            def tile(kb, dq, masked, qh=qh, dom=dom, delta=delta, lse_h=lse_h, cols=cols):
                rows = pl.ds(pl.multiple_of(kb * bq, bq), bq)
                kblk, vblk = k_ref[rows, cols], v_ref[rows, :]
                p = jnp.exp(_dot_nt(qh, kblk) * MLA_SCALE - lse_h)
                if masked:
                    p = jnp.where(causal, p, 0.0)
                ds = (p * (_dot_nt(dom, vblk) - delta) * MLA_SCALE).astype(BF16)
                dk_ref[rows, cols] += _dot_tn(ds, qh)
                dv_ref[rows, :] += _dot_tn(p.astype(BF16), dom)
                return dq + _dot(ds, kblk)

            dq = lax.fori_loop(0, i, lambda kb, dq, tile=tile: tile(kb, dq, False), jnp.zeros((bq, HEAD_PAD), F32))
            dq_ref[:, cols] = tile(i, dq, True)

    return pl.pallas_call(
        body, name="mla_bwd",
        grid=(4, nq),
        in_specs=[pl.BlockSpec((bq, 2 * HEAD_PAD), lambda h, i: (i, h)),
                  pl.BlockSpec((s, 2 * HEAD_PAD), lambda h, i: (0, h)),
                  pl.BlockSpec((s, LANES), lambda h, i: (0, h)),
                  pl.BlockSpec((bq, LANES), lambda h, i: (i, h)),
                  pl.BlockSpec((bq, LANES), lambda h, i: (i, h)),
                  pl.BlockSpec((bq, LANES), lambda h, i: (i, h))],
        out_specs=[pl.BlockSpec((bq, 2 * HEAD_PAD), lambda h, i: (i, h)),
                   pl.BlockSpec((s, 2 * HEAD_PAD), lambda h, i: (0, h)),
                   pl.BlockSpec((s, LANES), lambda h, i: (0, h))],
        out_shape=[jax.ShapeDtypeStruct((s, MLA_PAD_WIDTH), F32), jax.ShapeDtypeStruct((s, MLA_PAD_WIDTH), F32),
                   jax.ShapeDtypeStruct((s, MLA_WIDTH), F32)],
        compiler_params=_params(("arbitrary", "arbitrary"), VMEM_LIMIT),
    )(q, k, v, do, o, lse)


def _rms_bwd(d_out, inp, r, weight, n):
    normed = inp * r
    gw = d_out * weight
    d_in = r * (gw - normed * (jnp.sum(gw * normed, axis=-1, keepdims=True) * (1.0 / n)))
    return d_in, d_out * normed


def _mla_prep_bwd_call(dq, dk, dv, q0, k0, cqn, ckvn, c_q, c_kv, cos_t, sin_a, sin_b,
                       q_lora_norm, kv_lora_norm, qhn_pad, khn_pad, w_uq_bf, w_uk_bf, w_uv_bf):
    s = dq.shape[0]
    tm = _row_tile(s, 256)

    def body(dq_ref, dk_ref, dv_ref, q0_ref, k0_ref, cqn_ref, ckvn_ref, cq_ref, ckv_ref,
             cos_ref, sa_ref, sb_ref, qln_ref, kvln_ref, qhn_ref, khn_ref, wuq_ref, wuk_ref, wuv_ref,
             dcq_ref, dckv_ref, dkr_ref, gwuq_ref, gwuk_ref, gwuv_ref, gqln_ref, gkvln_ref, gqhn_ref, gkhn_ref):
        @pl.when(pl.program_id(0) == 0)
        def _():
            for ref in (gwuq_ref, gwuk_ref, gwuv_ref, gqln_ref, gkvln_ref, gqhn_ref, gkhn_ref):
                ref[...] = jnp.zeros_like(ref)

        cos_t, sin_a, sin_b = cos_ref[...], sa_ref[...], sb_ref[...]
        lane = lax.broadcasted_iota(I32, (tm, LANES), 1)
        rope_lanes = jnp.logical_and(lane >= KR_LANE, lane < KR_LANE + MLA_ROPE_DIM)
        cqn, ckvn = cqn_ref[...], ckvn_ref[...]
        d_cqn = jnp.zeros((tm, Q_LORA_RANK), F32)
        d_ckvn = jnp.zeros((tm, KV_LORA_RANK), F32)
        d_kr = jnp.zeros((tm, LANES), F32)
        g_qhn = jnp.zeros((1, LANES), F32)
        g_khn = jnp.zeros((1, LANES), F32)
        for h in range(MLA_HEADS):
            cols = slice(h * HEAD_PAD, (h + 1) * HEAD_PAD)
            q0 = q0_ref[:, cols]
            rq = lax.rsqrt(jnp.sum(q0 * q0, axis=-1, keepdims=True) * (1.0 / MLA_QK_DIM) + EPS)
            d_q0, gq = _rms_bwd(_rope_adjoint(dq_ref[:, cols], cos_t, sin_a, sin_b), q0, rq, qhn_ref[...],
                                MLA_QK_DIM)
            g_qhn += jnp.sum(gq, axis=0, keepdims=True)
            d_q0b = d_q0.astype(BF16)
            d_cqn += _dot_nt(d_q0b, wuq_ref[:, cols])
            gwuq_ref[:, cols] += _dot_tn(cqn, d_q0b)
            k0 = k0_ref[:, cols]
            rk = lax.rsqrt(jnp.sum(k0 * k0, axis=-1, keepdims=True) * (1.0 / MLA_QK_DIM) + EPS)
            d_k0, gk = _rms_bwd(_rope_adjoint(dk_ref[:, cols], cos_t, sin_a, sin_b), k0, rk, khn_ref[...],
                                MLA_QK_DIM)
            g_khn += jnp.sum(gk, axis=0, keepdims=True)
            d_kr += jnp.where(rope_lanes, d_k0, 0.0)
            d_k0b = d_k0.astype(BF16)
            d_ckvn += _dot_nt(d_k0b, wuk_ref[:, cols])
            gwuk_ref[:, cols] += _dot_tn(ckvn, d_k0b)
        dvb = dv_ref[...].astype(BF16)
        d_ckvn += _dot_nt(dvb, wuv_ref[...])
        gwuv_ref[...] += _dot_tn(ckvn, dvb)
        gqhn_ref[...] += g_qhn
        gkhn_ref[...] += g_khn
        dkr_ref[...] = d_kr.astype(BF16)
        cq = cq_ref[...]
        rcq = lax.rsqrt(jnp.mean(cq * cq, axis=-1, keepdims=True) + EPS)
        d_cq, gl = _rms_bwd(d_cqn, cq, rcq, qln_ref[...], Q_LORA_RANK)
        dcq_ref[...] = d_cq.astype(BF16)
        gqln_ref[...] += jnp.sum(gl, axis=0, keepdims=True)
        ckv = ckv_ref[...]
        rckv = lax.rsqrt(jnp.mean(ckv * ckv, axis=-1, keepdims=True) + EPS)
        d_ckv, gl = _rms_bwd(d_ckvn, ckv, rckv, kvln_ref[...], KV_LORA_RANK)
        dckv_ref[...] = d_ckv.astype(BF16)
        gkvln_ref[...] += jnp.sum(gl, axis=0, keepdims=True)

    return pl.pallas_call(
        body, name="mla_prep_bwd",
        grid=(s // tm,),
        in_specs=[_rows(tm, MLA_PAD_WIDTH), _rows(tm, MLA_PAD_WIDTH), _rows(tm, MLA_WIDTH),
                  _rows(tm, MLA_PAD_WIDTH), _rows(tm, MLA_PAD_WIDTH),
                  _rows(tm, Q_LORA_RANK), _rows(tm, KV_LORA_RANK), _rows(tm, Q_LORA_RANK), _rows(tm, KV_LORA_RANK),
                  _rows(tm, LANES), _rows(tm, LANES), _rows(tm, LANES),
                  _full((1, Q_LORA_RANK)), _full((1, KV_LORA_RANK)), _full((1, LANES)), _full((1, LANES)),
                  _full((Q_LORA_RANK, MLA_PAD_WIDTH)), _full((KV_LORA_RANK, MLA_PAD_WIDTH)),
                  _full((KV_LORA_RANK, MLA_WIDTH))],
        out_specs=[_rows(tm, Q_LORA_RANK), _rows(tm, KV_LORA_RANK), _rows(tm, LANES),
                   _full((Q_LORA_RANK, MLA_PAD_WIDTH)), _full((KV_LORA_RANK, MLA_PAD_WIDTH)),
                   _full((KV_LORA_RANK, MLA_WIDTH)),
                   _full((1, Q_LORA_RANK)), _full((1, KV_LORA_RANK)), _full((1, LANES)), _full((1, LANES))],
        out_shape=[jax.ShapeDtypeStruct((s, Q_LORA_RANK), BF16), jax.ShapeDtypeStruct((s, KV_LORA_RANK), BF16),
                   jax.ShapeDtypeStruct((s, LANES), BF16),
                   jax.ShapeDtypeStruct((Q_LORA_RANK, MLA_PAD_WIDTH), F32),
                   jax.ShapeDtypeStruct((KV_LORA_RANK, MLA_PAD_WIDTH), F32),
                   jax.ShapeDtypeStruct((KV_LORA_RANK, MLA_WIDTH), F32),
                   jax.ShapeDtypeStruct((1, Q_LORA_RANK), F32), jax.ShapeDtypeStruct((1, KV_LORA_RANK), F32),
                   jax.ShapeDtypeStruct((1, LANES), F32), jax.ShapeDtypeStruct((1, LANES), F32)],
        compiler_params=_params(("arbitrary",), VMEM_LIMIT),
    )(dq, dk, dv, q0, k0, cqn, ckvn, c_q, c_kv, cos_t, sin_a, sin_b,
      q_lora_norm, kv_lora_norm, qhn_pad, khn_pad, w_uq_bf, w_uk_bf, w_uv_bf)


def _dh_call(pieces, x, dy, shift, scale, norm_w, w_in_bf):
    s = x.shape[0]
    tm = _row_tile(s, 256)
    widths = [p.shape[1] for p in pieces]
    offsets = [sum(widths[:j]) for j in range(len(widths))]
    assert offsets[-1] + widths[-1] == IN_COLS_PAD
    n = len(pieces)

    def body(*refs):
        p_refs = refs[:n]
        x_ref, dy_ref, sh_ref, sc_ref, nw_ref, w_ref, gx_ref, dp_ref, dsh_ref, dsc_ref, gnw_ref = refs[n:]

        @pl.when(pl.program_id(0) == 0)
        def _():
            dsh_ref[...] = jnp.zeros_like(dsh_ref)
            dsc_ref[...] = jnp.zeros_like(dsc_ref)
            gnw_ref[...] = jnp.zeros_like(gnw_ref)

        for p_ref, c0, width in zip(p_refs, offsets, widths):
            dp_ref[:, c0:c0 + width] = p_ref[...].astype(BF16)
        dh = _dot_nt(dp_ref[...], w_ref[...])
        xx = x_ref[...]
        r0 = lax.rsqrt(jnp.mean(xx * xx, axis=-1, keepdims=True) + EPS)
        xn = xx * r0
        nw = nw_ref[...]
        dsh_ref[...] += jnp.sum(dh, axis=0, keepdims=True)
        dsc_ref[...] += jnp.sum(dh * (xn * nw), axis=0, keepdims=True)
        dn = dh * (1.0 + sc_ref[...])
        gnw_ref[...] += jnp.sum(dn * xn, axis=0, keepdims=True)
        dxn = dn * nw
        gx_ref[...] = dy_ref[...] + r0 * (dxn - xn * jnp.mean(dxn * xn, axis=-1, keepdims=True))

    return pl.pallas_call(
        body, name="in_proj_bwd",
        grid=(s // tm,),
        in_specs=[_rows(tm, w) for w in widths]
        + [_rows(tm, D_MODEL), _rows(tm, D_MODEL), _full((1, D_MODEL)), _full((1, D_MODEL)), _full((1, D_MODEL)),
           _full((D_MODEL, IN_COLS_PAD))],
        out_specs=[_rows(tm, D_MODEL), _rows(tm, IN_COLS_PAD),
                   _full((1, D_MODEL)), _full((1, D_MODEL)), _full((1, D_MODEL))],
        out_shape=[jax.ShapeDtypeStruct((s, D_MODEL), F32), jax.ShapeDtypeStruct((s, IN_COLS_PAD), BF16),
                   jax.ShapeDtypeStruct((1, D_MODEL), F32), jax.ShapeDtypeStruct((1, D_MODEL), F32),
                   jax.ShapeDtypeStruct((1, D_MODEL), F32)],
        compiler_params=_params(("arbitrary",), VMEM_LIMIT),
    )(*pieces, x, dy, shift, scale, norm_w, w_in_bf)


def _gw_in_call(hb, dproj):
    s = hb.shape[0]
    tk = _row_tile(s, 512)
    tn = IN_COLS_PAD // 2

    def body(h_ref, d_ref, g_ref):
        @pl.when(pl.program_id(1) == 0)
        def _():
            g_ref[...] = jnp.zeros_like(g_ref)

        g_ref[...] += _dot_tn(h_ref[...], d_ref[...])

    return pl.pallas_call(
        body, name="in_proj_wgrad",
        grid=(2, s // tk),
        in_specs=[pl.BlockSpec((tk, D_MODEL), lambda j, t: (t, 0)), pl.BlockSpec((tk, tn), lambda j, t: (t, j))],
        out_specs=pl.BlockSpec((D_MODEL, tn), lambda j, t: (0, j)),
        out_shape=jax.ShapeDtypeStruct((D_MODEL, IN_COLS_PAD), F32),
        compiler_params=_params(("arbitrary", "arbitrary"), VMEM_LIMIT),
    )(hb, dproj)


def _adamw(g, w, m, v):
    m = ADAM_B1 * m + (1.0 - ADAM_B1) * g
    v = ADAM_B2 * v + (1.0 - ADAM_B2) * (g * g)
    m_hat = m / (1.0 - ADAM_B1 ** ADAM_STEP)
    v_hat = v / (1.0 - ADAM_B2 ** ADAM_STEP)
    delta = -ADAM_LR * (m_hat / (jnp.sqrt(v_hat) + ADAM_EPS) + ADAM_WD * w)
    return delta, m, v


def _adam_shard_call(name, own, sib, w, m, v):
    r, c = w.shape
    tr = r if r <= 512 else 256

    def body(own_ref, sib_ref, w_ref, m_ref, v_ref, g_ref, d_ref, nm_ref, nv_ref):
        a = ((own_ref[0] + own_ref[1]) + own_ref[2]) + own_ref[3]
        b = ((sib_ref[0] + sib_ref[1]) + sib_ref[2]) + sib_ref[3]
        g = a + b
        g_ref[...] = g
        d_ref[...], nm_ref[...], nv_ref[...] = _adamw(g, w_ref[...], m_ref[...], v_ref[...])

    part = pl.BlockSpec((4, tr, c), lambda i: (0, i, 0))
    blk = pl.BlockSpec((tr, c), lambda i: (i, 0))
    return pl.pallas_call(
        body, name=name,
        grid=(r // tr,),
        in_specs=[part, part, blk, blk, blk],
        out_specs=[blk] * 4,
        out_shape=[jax.ShapeDtypeStruct((r, c), F32)] * 4,
        compiler_params=_params(("arbitrary",), VMEM_LIMIT),
    )(own, sib, w, m, v)


def _adam_ada_call(c_all, d_all, w, m, v):
    r, c = w.shape
    tr = 256

    def body(c_ref, d_ref, w_ref, m_ref, v_ref, g_ref, dl_ref, nm_ref, nv_ref):
        cc = c_ref[...]
        sc = cc * _sigmoid(cc)
        dd = d_ref[...]
        sc_hi = sc.astype(BF16)
        sc_lo = (sc - sc_hi.astype(F32)).astype(BF16)
        dd_hi = dd.astype(BF16)
        dd_lo = (dd - dd_hi.astype(F32)).astype(BF16)
        g = _dot_tn(sc_hi, dd_hi) + (_dot_tn(sc_hi, dd_lo) + _dot_tn(sc_lo, dd_hi))
        g_ref[...] = g
        dl_ref[...], nm_ref[...], nv_ref[...] = _adamw(g, w_ref[...], m_ref[...], v_ref[...])

    blk = pl.BlockSpec((tr, c), lambda i: (i, 0))
    return pl.pallas_call(
        body, name="adam_w_ada",
        grid=(r // tr,),
        in_specs=[pl.BlockSpec((16, tr), lambda i: (0, i)), pl.BlockSpec((16, c), lambda i: (0, 0)), blk, blk, blk],
        out_specs=[blk] * 4,
        out_shape=[jax.ShapeDtypeStruct((r, c), F32)] * 4,
        compiler_params=_params(("arbitrary",), VMEM_LIMIT),
    )(c_all, d_all, w, m, v)


def _adam_small_call(packs, w, m, v, first_row):
    rows = w.shape[0]

    def body(p_ref, w_ref, m_ref, v_ref, g_ref, d_ref, nm_ref, nv_ref):
        g = p_ref[0, first_row:first_row + rows, :]
        for b in range(1, 8):
            g = g + p_ref[b, first_row:first_row + rows, :]
        g_ref[...] = g
        d_ref[...], nm_ref[...], nv_ref[...] = _adamw(g, w_ref[...], m_ref[...], v_ref[...])

    return pl.pallas_call(
        body, name="adam_vectors",
        out_shape=[jax.ShapeDtypeStruct((rows, LANES), F32)] * 4,
    )(packs, w, m, v)


def _rope_tables(positions):
    inv_freq = ROPE_THETA ** (-jnp.arange(0, MLA_ROPE_DIM, 2, dtype=F32) / MLA_ROPE_DIM)
    ang = positions.astype(F32)[:, None] * inv_freq
    cos, sin = jnp.cos(ang), jnp.sin(ang)
    s = positions.shape[0]
    half = MLA_ROPE_DIM // 2
    zeros = lambda n: jnp.zeros((s, n), F32)
    cos_t = jnp.concatenate([jnp.ones((s, MLA_NOPE_DIM), F32), cos, cos, zeros(HEAD_PAD - MLA_QK_DIM)], axis=1)
    sin_a = jnp.concatenate([zeros(MLA_NOPE_DIM), -sin, zeros(half + HEAD_PAD - MLA_QK_DIM)], axis=1)
    sin_b = jnp.concatenate([zeros(MLA_NOPE_DIM + half), sin, zeros(HEAD_PAD - MLA_QK_DIM)], axis=1)
    return cos_t, sin_a, sin_b


def _unshard_cols(g):
    return jnp.transpose(g, (1, 0, 2)).reshape(g.shape[1], 4 * g.shape[2])


def _shard_cols(g):
    r, c4 = g.shape
    return jnp.transpose(g.reshape(r, 4, c4 // 4), (1, 0, 2))


def _pad_heads(w, width):
    r = w.shape[0]
    w = w.reshape(r, MLA_HEADS, width)
    return jnp.pad(w, ((0, 0), (0, 0), (0, HEAD_PAD - width))).reshape(r, MLA_PAD_WIDTH)


def _pad_lanes(vec):
    return jnp.pad(vec, ((0, 0), (0, LANES - vec.shape[1])))


def _vector_rows(vecs):
    rows = []
    for vec in vecs:
        n = vec.shape[1]
        pad = (-n) % LANES
        rows.append(jnp.pad(vec, ((0, 0), (0, pad))).reshape((n + pad) // LANES, LANES))
    return jnp.concatenate(rows, axis=0)


def kernel(x, c, positions, w_ada, b_ada, norm_w, w_in, q_lora_norm, w_uq, kv_lora_norm, w_ukv, q_head_norm, k_head_norm, w_out, loss_target, m_w_ada, m_b_ada, m_norm_w, m_w_in, m_q_lora_norm, m_w_uq, m_kv_lora_norm, m_w_ukv, m_q_head_norm, m_k_head_norm, m_w_out, v_w_ada, v_b_ada, v_norm_w, v_w_in, v_q_lora_norm, v_w_uq, v_kv_lora_norm, v_w_ukv, v_q_head_norm, v_k_head_norm, v_w_out):
    wa_g, win_g, wuq_g, wukv_g, wout_g = _gather_call([w_ada, w_in, w_uq, w_ukv, w_out])
    (sq_sum, grad_x, g_w_in, g_w_uq, g_w_ukv, g_w_out, d_ada, g_norm_w, g_qln, g_kvln, g_qhn, g_khn) = _local_step(
        x[0], c, positions[0], loss_target[0], _unshard_cols(wa_g), b_ada, norm_w, _unshard_cols(win_g),
        q_lora_norm, _unshard_cols(wuq_g), kv_lora_norm, _unshard_cols(wukv_g), q_head_norm, k_head_norm,
        wout_g.reshape(D_MODEL, D_MODEL))
    loss = lax.psum(0.5 * sq_sum / D_MODEL, ("x", "y", "c"))

    grads = [_shard_cols(g_w_in), _shard_cols(g_w_uq), _shard_cols(g_w_ukv),
             g_w_out.reshape(4, D_MODEL // 4, D_MODEL)]
    small = _vector_rows([c, d_ada, g_norm_w, g_qln, g_kvln, g_qhn, g_khn])
    small = jnp.pad(small, ((0, (-small.shape[0]) % 8), (0, 0)))
    own, sib, packs = _exchange_call(grads, small)

    names = ["adam_w_in", "adam_w_uq", "adam_w_ukv", "adam_w_out"]
    shard_w = [(w_in, m_w_in, v_w_in), (w_uq, m_w_uq, v_w_uq), (w_ukv, m_w_ukv, v_w_ukv),
               (w_out, m_w_out, v_w_out)]
    res = {}
    for name, o_g, s_g, (w, m, v) in zip(names, own, sib, shard_w):
        res[name] = _adam_shard_call(name, o_g, s_g, w[0], m[0], v[0])
    chip = 2 * lax.axis_index("x") + lax.axis_index("y")
    ada_cols = w_ada.shape[2]
    c_rows = D_MODEL // LANES
    c_all = jnp.pad(packs[:, :c_rows, :].reshape(8, D_MODEL), ((0, 8), (0, 0)))
    d_rows = packs[:, c_rows:c_rows + 3 * c_rows, :].reshape(8, 3 * D_MODEL)
    d_all = jnp.pad(lax.dynamic_slice_in_dim(d_rows, chip * ada_cols, ada_cols, axis=1), ((0, 8), (0, 0)))
    res_ada = _adam_ada_call(c_all, d_all, w_ada[0], m_w_ada[0], v_w_ada[0])
    vec_names = [(b_ada, m_b_ada, v_b_ada), (norm_w, m_norm_w, v_norm_w), (q_lora_norm, m_q_lora_norm, v_q_lora_norm),
                 (kv_lora_norm, m_kv_lora_norm, v_kv_lora_norm), (q_head_norm, m_q_head_norm, v_q_head_norm),
                 (k_head_norm, m_k_head_norm, v_k_head_norm)]
    packed = [_vector_rows([t[j] for t in vec_names]) for j in range(3)]
    n_rows = packed[0].shape[0]
    pad_rows = (-n_rows) % 8
    packed = [jnp.pad(p, ((0, pad_rows), (0, 0))) for p in packed]
    res_vec = _adam_small_call(packs, packed[0], packed[1], packed[2], c_rows)

    def unpack(arr):
        outs, r0 = [], 0
        for t in vec_names:
            n = t[0].shape[1]
            nr = -(-n // LANES)
            outs.append(arr[r0:r0 + nr].reshape(1, nr * LANES)[:, :n])
            r0 += nr
        return outs

    vec_out = [unpack(a) for a in res_vec]

    def ordered(kind):
        big = lambda name: res[name][kind][None]
        return [res_ada[kind][None], vec_out[kind][0], vec_out[kind][1], big("adam_w_in"), vec_out[kind][2],
                big("adam_w_uq"), vec_out[kind][3], big("adam_w_ukv"), vec_out[kind][4], vec_out[kind][5],
                big("adam_w_out")]

    return (loss, grad_x[None], *ordered(0), *ordered(1), *ordered(2), *ordered(3))


def _local_step(x2, c, positions, tgt, w_ada_full, b_ada, norm_w, w_in_full, q_lora_norm, w_uq_full,
                kv_lora_norm, w_ukv_full, q_head_norm, k_head_norm, w_out_full):
    w_ada_bf = w_ada_full.astype(BF16)
    kr_block = jnp.pad(w_in_full[:, 2688:2720], ((0, 0), (KR_LANE, LANES - KR_LANE - MLA_ROPE_DIM)))
    w_in_bf = jnp.concatenate([w_in_full[:, :2688], w_in_full[:, 2720:], kr_block], axis=1).astype(BF16)
    w_uq_bf = _pad_heads(w_uq_full, MLA_QK_DIM).astype(BF16)
    w_ukv_heads = w_ukv_full.reshape(KV_LORA_RANK, MLA_HEADS, 2 * MLA_NOPE_DIM)
    w_uk_bf = _pad_heads(w_ukv_heads[:, :, :MLA_NOPE_DIM].reshape(KV_LORA_RANK, -1), MLA_NOPE_DIM).astype(BF16)
    w_uv_bf = w_ukv_heads[:, :, MLA_NOPE_DIM:].reshape(KV_LORA_RANK, MLA_WIDTH).astype(BF16)
    w_out_bf = w_out_full.astype(BF16)
    qhn_pad, khn_pad = _pad_lanes(q_head_norm), _pad_lanes(k_head_norm)
    cos_t, sin_a, sin_b = _rope_tables(positions)

    ada = _ada_call(c, w_ada_bf, b_ada)
    shift, scale, gate = ada[:, :D_MODEL], ada[:, D_MODEL:2 * D_MODEL], ada[:, 2 * D_MODEL:]
    hb, q_sb, k_sb, v_sb, g_sb, c_q, c_kv, g_mla, k_rope = _pre_call(x2, shift, scale, norm_w, w_in_bf)
    q_m, k_m, v_m, cqn, ckvn, q0, k0 = _mla_prep_call(
        c_q, c_kv, k_rope, cos_t, sin_a, sin_b, q_lora_norm, kv_lora_norm, qhn_pad, khn_pad,
        w_uq_bf, w_uk_bf, w_uv_bf)
    o_sb, r_sb, kstart = _sb_fwd_call(q_sb, k_sb, v_sb)
    o_mla, lse = _mla_fwd_call(q_m, k_m, v_m)
    do_sb, do_mla, dg_sb, dg_mla, dy, g_w_out, d_gate, sq = _out_call(
        o_sb, g_sb, o_mla, g_mla, x2, tgt, gate, w_out_bf)

    dq_sb, dk_sb, dv_sb = _sb_bwd_call(kstart, q_sb, k_sb, v_sb, do_sb, r_sb)
    dq_m, dk_m, dv_m = _mla_bwd_call(q_m, k_m, v_m, do_mla, o_mla, lse)
    (d_cq, d_ckv, d_kr, g_wuq_pad, g_wuk_pad, g_wuv, g_qln, g_kvln, g_qhn, g_khn) = _mla_prep_bwd_call(
        dq_m, dk_m, dv_m, q0, k0, cqn, ckvn, c_q, c_kv, cos_t, sin_a, sin_b,
        q_lora_norm, kv_lora_norm, qhn_pad, khn_pad, w_uq_bf, w_uk_bf, w_uv_bf)
    grad_x, dproj, d_shift, d_scale, g_norm_w = _dh_call(
        [dq_sb, dk_sb, dv_sb, dg_sb, d_cq, d_ckv, dg_mla, d_kr], x2, dy, shift, scale, norm_w, w_in_bf)
    g_win_pad = _gw_in_call(hb, dproj)

    g_w_in = jnp.concatenate([g_win_pad[:, :2688],
                              g_win_pad[:, C_KR + KR_LANE:C_KR + KR_LANE + MLA_ROPE_DIM],
                              g_win_pad[:, 2688:3200]], axis=1)
    g_w_uq = g_wuq_pad.reshape(Q_LORA_RANK, MLA_HEADS, HEAD_PAD)[:, :, :MLA_QK_DIM].reshape(Q_LORA_RANK, -1)
    g_w_ukv = jnp.concatenate(
        [g_wuk_pad.reshape(KV_LORA_RANK, MLA_HEADS, HEAD_PAD)[:, :, :MLA_NOPE_DIM],
         g_wuv.reshape(KV_LORA_RANK, MLA_HEADS, MLA_NOPE_DIM)], axis=2).reshape(KV_LORA_RANK, -1)
    d_ada = jnp.concatenate([d_shift, d_scale, d_gate], axis=1)
    return (jnp.sum(sq), grad_x, g_w_in, g_w_uq, g_w_ukv, g_w_out, d_ada, g_norm_w, g_qln, g_kvln,
            g_qhn[:, :MLA_QK_DIM], g_khn[:, :MLA_QK_DIM])
```

```python
import functools
import math

import jax
import jax.numpy as jnp
from jax import lax
from jax.experimental import pallas as pl
from jax.experimental.pallas import tpu as pltpu

F32 = jnp.float32
BF16 = jnp.bfloat16
I32 = jnp.int32

D_MODEL = 1024
SB_HEADS = 8
SB_WIDTH = 512
MLA_HEADS = 8
MLA_QK_DIM = 96
MLA_NOPE_DIM = 64
MLA_ROPE_DIM = 32
MLA_WIDTH = 512
Q_LORA_RANK = 384
KV_LORA_RANK = 256
ROPE_THETA = 10000.0
EPS = 1e-6
LANES = 128
HEAD_PAD = 128
MLA_PAD_WIDTH = MLA_HEADS * HEAD_PAD

C_Q, C_K, C_V, C_G = 0, 512, 1024, 1536
C_CQ, C_CKV, C_GM, C_KR = 2048, 2432, 2688, 3200
IN_COLS_PAD = 3328
KR_LANE = 64

ADAM_LR = 0.001
ADAM_B1 = 0.9
ADAM_B2 = 0.999
ADAM_EPS = 1e-08
ADAM_WD = 0.01
ADAM_STEP = 10

SB_SCALE = 0.125
MLA_SCALE = 1.0 / math.sqrt(MLA_QK_DIM)
LN2 = math.log(2.0)
MLA_SCALE_LOG2 = MLA_SCALE / LN2
MLA_BQ = 256
MLA_BK = 1024
MLA_BWD_BK = 512
SB_DEAD = -104.0
MASK_NEG = -1e30

VMEM_LIMIT = 56 * 1024 * 1024
MESH = pl.DeviceIdType.MESH


def _dot(a, b):
    return jnp.dot(a, b, preferred_element_type=F32)


def _dot_nt(a, b):
    return lax.dot_general(a, b, (((1,), (1,)), ((), ())), preferred_element_type=F32)


def _dot_tn(a, b):
    return lax.dot_general(a, b, (((0,), (0,)), ((), ())), preferred_element_type=F32)


def _sigmoid(x):
    return 1.0 / (1.0 + jnp.exp(-x))


def _split_dot(a, m):
    hi = a.astype(BF16)
    lo = (a - hi.astype(F32)).astype(BF16)
    return _dot(hi, m) + _dot(lo, m)


def _params(sem, vmem=None):
    return pltpu.CompilerParams(dimension_semantics=sem, vmem_limit_bytes=vmem)


def _row_tile(s, want):
    return min(want, s)


def _hbm_spec():
    return pl.BlockSpec(memory_space=pltpu.HBM)


def _gather_call(shards):
    n = len(shards)

    def body(*refs):
        ins, outs = refs[:n], refs[n:2 * n]
        send_sems, recv_sems, loc_sems = refs[2 * n:]
        x, y, c = lax.axis_index("x"), lax.axis_index("y"), lax.axis_index("c")
        me = 2 * x + y
        peers = [(1 - x, y), (x, 1 - y), (1 - x, 1 - y)]

        def copy(a, j, slot):
            px, py = peers[j]
            return pltpu.make_async_remote_copy(
                src_ref=ins[a].at[0], dst_ref=outs[a].at[slot],
                send_sem=send_sems.at[3 * a + j], recv_sem=recv_sems.at[3 * a + j],
                device_id=(px, py, c), device_id_type=MESH)

        local = [pltpu.make_async_copy(ins[a].at[0], outs[a].at[me], loc_sems.at[a]) for a in range(n)]
        for cp in local:
            cp.start()
        sends = [copy(a, j, me) for a in range(n) for j in range(3)]
        for cp in sends:
            cp.start()
        for a in range(n):
            for j in range(3):
                px, py = peers[j]
                copy(a, j, 2 * px + py).wait_recv()
        for cp in sends:
            cp.wait_send()
        for cp in local:
            cp.wait()

    return pl.pallas_call(
        body, name="gather_weights",
        out_shape=[jax.ShapeDtypeStruct((4,) + s.shape[1:], s.dtype) for s in shards],
        in_specs=[_hbm_spec() for _ in shards],
        out_specs=[_hbm_spec() for _ in shards],
        scratch_shapes=[pltpu.SemaphoreType.DMA((3 * n,)), pltpu.SemaphoreType.DMA((3 * n,)),
                        pltpu.SemaphoreType.DMA((n,))],
    )(*shards)


def _exchange_call(grads, small):
    n = len(grads)

    def body(*refs):
        g_in, small_in = refs[:n], refs[n]
        own, sib, packs = refs[n + 1:2 * n + 1], refs[2 * n + 1:3 * n + 1], refs[3 * n + 1]
        ici_send, ici_recv, d2d_send, d2d_recv, sm_send, sm_recv, loc_sems = refs[3 * n + 2:]
        x, y, c = lax.axis_index("x"), lax.axis_index("y"), lax.axis_index("c")
        me = 2 * x + y
        me8 = 4 * x + 2 * y + c
        sibling = (x, y, 1 - c)
        peers = [(1 - x, y), (x, 1 - y), (1 - x, 1 - y)]
        flips = [(fx, fy, fc) for fx in (0, 1) for fy in (0, 1) for fc in (0, 1)][1:]

        def ici(a, j, src_slot, dst_slot):
            px, py = peers[j]
            return pltpu.make_async_remote_copy(
                src_ref=g_in[a].at[src_slot], dst_ref=own[a].at[dst_slot],
                send_sem=ici_send.at[3 * a + j], recv_sem=ici_recv.at[3 * a + j],
                device_id=(px, py, c), device_id_type=MESH)

        def d2d(a, rel, chip, src):
            return pltpu.make_async_remote_copy(
                src_ref=src, dst_ref=sib[a].at[chip],
                send_sem=d2d_send.at[4 * a + rel], recv_sem=d2d_recv.at[4 * a + rel],
                device_id=sibling, device_id_type=MESH)

        def flipped(r):
            fx, fy, fc = flips[r]
            return ((1 - x) if fx else x, (1 - y) if fy else y, (1 - c) if fc else c)

        def sm(r, slot):
            return pltpu.make_async_remote_copy(
                src_ref=small_in, dst_ref=packs.at[slot],
                send_sem=sm_send.at[r], recv_sem=sm_recv.at[r],
                device_id=flipped(r), device_id_type=MESH)

        def peer8(r):
            px, py, pc = flipped(r)
            return 4 * px + 2 * py + pc

        local = [pltpu.make_async_copy(g_in[a].at[me], own[a].at[me], loc_sems.at[a]) for a in range(n)]
        local.append(pltpu.make_async_copy(small_in, packs.at[me8], loc_sems.at[n]))
        for cp in local:
            cp.start()
        sends = []
        for r in range(7):
            sends.append(sm(r, me8))
        for a in range(n):
            for j in range(3):
                px, py = peers[j]
                sends.append(ici(a, j, 2 * px + py, me))
        for cp in sends:
            cp.start()
        for a in range(n):
            cp = d2d(a, 0, me, g_in[a].at[me])
            cp.start()
            sends.append(cp)
        for a in range(n):
            for j in range(3):
                px, py = peers[j]
                ici(a, j, me, 2 * px + py).wait_recv()
                cp = d2d(a, 1 + j, 2 * px + py, own[a].at[2 * px + py])
                cp.start()
                sends.append(cp)
        for a in range(n):
            d2d(a, 0, me, g_in[a].at[me]).wait_recv()
            for j in range(3):
                px, py = peers[j]
                d2d(a, 1 + j, 2 * px + py, g_in[a].at[me]).wait_recv()
        for r in range(7):
            sm(r, peer8(r)).wait_recv()
        for cp in sends:
            cp.wait_send()
        for cp in local:
            cp.wait()

    out_shape = ([jax.ShapeDtypeStruct(g.shape, g.dtype) for g in grads] * 2
                 + [jax.ShapeDtypeStruct((8,) + small.shape, small.dtype)])
    res = pl.pallas_call(
        body, name="exchange_grads",
        out_shape=out_shape,
        in_specs=[_hbm_spec() for _ in range(n + 1)],
        out_specs=[_hbm_spec() for _ in range(2 * n + 1)],
        scratch_shapes=[pltpu.SemaphoreType.DMA((3 * n,)), pltpu.SemaphoreType.DMA((3 * n,)),
                        pltpu.SemaphoreType.DMA((4 * n,)), pltpu.SemaphoreType.DMA((4 * n,)),
                        pltpu.SemaphoreType.DMA((7,)), pltpu.SemaphoreType.DMA((7,)),
                        pltpu.SemaphoreType.DMA((n + 1,))],
    )(*grads, small)
    return res[:n], res[n:2 * n], res[2 * n]


def _ada_call(c, w_ada_bf, b_ada):
    def body(c_ref, w_ref, b_ref, o_ref):
        cc = c_ref[...]
        sc = jnp.broadcast_to(cc * _sigmoid(cc), (8, D_MODEL)).astype(BF16)
        o_ref[...] = _dot(sc, w_ref[...]) + b_ref[...]

    out = pl.pallas_call(
        body, name="ada_fwd",
        out_shape=jax.ShapeDtypeStruct((8, 3 * D_MODEL), F32),
        compiler_params=pltpu.CompilerParams(vmem_limit_bytes=VMEM_LIMIT),
    )(c, w_ada_bf, b_ada)
    return out[0:1]


def _full(shape):
    return pl.BlockSpec(shape, lambda i: (0,) * len(shape))


def _rows(tm, width):
    return pl.BlockSpec((tm, width), lambda i: (i, 0))


def _pre_call(x, shift, scale, norm_w, w_in_bf):
    s = x.shape[0]
    tm = _row_tile(s, 512)
    groups = [(C_Q, 512, BF16), (C_K, 512, BF16), (C_V, 512, BF16), (C_G, 512, F32),
              (C_CQ, Q_LORA_RANK, F32), (C_CKV, KV_LORA_RANK, F32), (C_GM, 512, F32), (C_KR, LANES, F32)]

    def body(x_ref, sh_ref, sc_ref, nw_ref, w_ref, hb_ref, *outs):
        xx = x_ref[...]
        r0 = lax.rsqrt(jnp.mean(xx * xx, axis=-1, keepdims=True) + EPS)
        h = (xx * r0 * nw_ref[...]) * (1.0 + sc_ref[...]) + sh_ref[...]
        hb = h.astype(BF16)
        hb_ref[...] = hb
        for (c0, width, dt), o_ref in zip(groups, outs):
            o_ref[...] = _dot(hb, w_ref[:, c0:c0 + width]).astype(dt)

    return pl.pallas_call(
        body, name="pre_proj",
        grid=(s // tm,),
        in_specs=[_rows(tm, D_MODEL), _full((1, D_MODEL)), _full((1, D_MODEL)), _full((1, D_MODEL)),
                  _full((D_MODEL, IN_COLS_PAD))],
        out_specs=[_rows(tm, D_MODEL)] + [_rows(tm, w) for _, w, _ in groups],
        out_shape=[jax.ShapeDtypeStruct((s, D_MODEL), BF16)]
        + [jax.ShapeDtypeStruct((s, w), dt) for _, w, dt in groups],
        compiler_params=_params(("arbitrary",), VMEM_LIMIT),
    )(x, shift, scale, norm_w, w_in_bf)


def _rope(t, cos_t, sin_a, sin_b):
    return t * cos_t + pltpu.roll(t, 112, 1) * sin_a + pltpu.roll(t, 16, 1) * sin_b


def _rope_adjoint(d, cos_t, sin_a, sin_b):
    return d * cos_t + pltpu.roll(d * sin_a, 16, 1) + pltpu.roll(d * sin_b, 112, 1)


def _mla_prep_call(c_q, c_kv, k_rope, cos_t, sin_a, sin_b, q_lora_norm, kv_lora_norm, qhn_pad, khn_pad,
                   w_uq_bf, w_uk_bf, w_uv_bf):
    s = c_q.shape[0]
    tm = _row_tile(s, 256)

    def body(cq_ref, ckv_ref, kr_ref, cos_ref, sa_ref, sb_ref, qln_ref, kvln_ref, qhn_ref, khn_ref,
             wuq_ref, wuk_ref, wuv_ref, q_ref, k_ref, v_ref, cqn_ref, ckvn_ref, q0_ref, k0_ref):
        cq = cq_ref[...]
        cqn = (cq * lax.rsqrt(jnp.mean(cq * cq, axis=-1, keepdims=True) + EPS) * qln_ref[...]).astype(BF16)
        cqn_ref[...] = cqn
        ckv = ckv_ref[...]
        ckvn = (ckv * lax.rsqrt(jnp.mean(ckv * ckv, axis=-1, keepdims=True) + EPS) * kvln_ref[...]).astype(BF16)
        ckvn_ref[...] = ckvn
        v_ref[...] = _dot(ckvn, wuv_ref[...]).astype(BF16)
        cos_t, sin_a, sin_b = cos_ref[...], sa_ref[...], sb_ref[...]
        kr = kr_ref[...]
        for h in range(MLA_HEADS):
            cols = slice(h * HEAD_PAD, (h + 1) * HEAD_PAD)
            q0 = _dot(cqn, wuq_ref[:, cols])
            q0_ref[:, cols] = q0
            rq = lax.rsqrt(jnp.sum(q0 * q0, axis=-1, keepdims=True) * (1.0 / MLA_QK_DIM) + EPS)
            q_ref[:, cols] = _rope(q0 * rq * qhn_ref[...], cos_t, sin_a, sin_b).astype(BF16)
            k0 = _dot(ckvn, wuk_ref[:, cols]) + kr
            k0_ref[:, cols] = k0
            rk = lax.rsqrt(jnp.sum(k0 * k0, axis=-1, keepdims=True) * (1.0 / MLA_QK_DIM) + EPS)
            k_ref[:, cols] = _rope(k0 * rk * khn_ref[...], cos_t, sin_a, sin_b).astype(BF16)

    return pl.pallas_call(
        body, name="mla_prep",
        grid=(s // tm,),
        in_specs=[_rows(tm, Q_LORA_RANK), _rows(tm, KV_LORA_RANK), _rows(tm, LANES),
                  _rows(tm, LANES), _rows(tm, LANES), _rows(tm, LANES),
                  _full((1, Q_LORA_RANK)), _full((1, KV_LORA_RANK)), _full((1, LANES)), _full((1, LANES)),
                  _full((Q_LORA_RANK, MLA_PAD_WIDTH)), _full((KV_LORA_RANK, MLA_PAD_WIDTH)),
                  _full((KV_LORA_RANK, MLA_WIDTH))],
        out_specs=[_rows(tm, MLA_PAD_WIDTH), _rows(tm, MLA_PAD_WIDTH), _rows(tm, MLA_WIDTH),
                   _rows(tm, Q_LORA_RANK), _rows(tm, KV_LORA_RANK),
                   _rows(tm, MLA_PAD_WIDTH), _rows(tm, MLA_PAD_WIDTH)],
        out_shape=[jax.ShapeDtypeStruct((s, MLA_PAD_WIDTH), BF16), jax.ShapeDtypeStruct((s, MLA_PAD_WIDTH), BF16),
                   jax.ShapeDtypeStruct((s, MLA_WIDTH), BF16),
                   jax.ShapeDtypeStruct((s, Q_LORA_RANK), BF16), jax.ShapeDtypeStruct((s, KV_LORA_RANK), BF16),
                   jax.ShapeDtypeStruct((s, MLA_PAD_WIDTH), F32), jax.ShapeDtypeStruct((s, MLA_PAD_WIDTH), F32)],
        compiler_params=_params(("arbitrary",), VMEM_LIMIT),
    )(c_q, c_kv, k_rope, cos_t, sin_a, sin_b, q_lora_norm, kv_lora_norm, qhn_pad, khn_pad,
      w_uq_bf, w_uk_bf, w_uv_bf)


def _log_sigmoid_pair(z):
    ls = jnp.minimum(z, 0.0) - jnp.log1p(jnp.exp(-jnp.abs(z)))
    return ls, ls - z


def _sb_fwd_call(q, k, v):
    s = q.shape[0]
    bq = _row_tile(s, 256)
    nq = s // bq

    def body(q_ref, k_ref, v_ref, o_ref, r_ref, ks_ref):
        hp, i = pl.program_id(0), pl.program_id(1)
        lane = lax.broadcasted_iota(I32, (bq, LANES), 1)
        row = lax.broadcasted_iota(I32, (bq, bq), 0)
        col = lax.broadcasted_iota(I32, (bq, bq), 1)
        strict = col < row
        later = jnp.where(row > col, 1.0, 0.0).astype(BF16)
        qs = q_ref[...] * jnp.asarray(SB_SCALE, BF16)
        accs, finals = [], []
        for hh in range(2):
            qm = qs * _head_mask(lane, hh).astype(BF16)

            def tile(kb, run, acc, masked, qm=qm):
                rows = pl.ds(pl.multiple_of(kb * bq, bq), bq)
                z = _dot_nt(qm, k_ref[rows, :])
                ls, lk = _log_sigmoid_pair(z)
                if masked:
                    lk = jnp.where(strict, lk, 0.0)
                w = jnp.exp(ls + _split_dot(lk, later) + run)
                if masked:
                    w = jnp.where(strict, w, 0.0)
                acc = acc + _dot(w.astype(BF16), v_ref[rows, :])
                return run + jnp.sum(lk, axis=1, keepdims=True), acc

            run, acc = tile(i, jnp.zeros((bq, 1), F32), jnp.zeros((bq, LANES), F32), True)

            def cond(carry):
                kb, alive, _, _ = carry
                return jnp.logical_and(kb >= 0, alive > 0)

            def step(carry, tile=tile):
                kb, _, run, acc = carry
                run, acc = tile(kb, run, acc, False)
                return kb - 1, (jnp.max(run) > SB_DEAD).astype(I32), run, acc

            kb, _, run, acc = lax.while_loop(
                cond, step, (i - 1, (jnp.max(run) > SB_DEAD).astype(I32), run, acc))
            ks_ref[2 * hp + hh, i] = kb + 1
            accs.append(acc)
            finals.append(run)
        o_ref[...] = jnp.where(lane < 64, accs[0], accs[1])
        r_ref[...] = jnp.where(lane < 64, finals[0], finals[1])

    return pl.pallas_call(
        body, name="sb_fwd",
        grid=(4, nq),
        in_specs=[pl.BlockSpec((bq, LANES), lambda h, i: (i, h)),
                  pl.BlockSpec((s, LANES), lambda h, i: (0, h)),
                  pl.BlockSpec((s, LANES), lambda h, i: (0, h))],
        out_specs=[pl.BlockSpec((bq, LANES), lambda h, i: (i, h)),
                   pl.BlockSpec((bq, LANES), lambda h, i: (i, h)),
                   pl.BlockSpec(memory_space=pltpu.SMEM)],
        out_shape=[jax.ShapeDtypeStruct((s, SB_WIDTH), F32), jax.ShapeDtypeStruct((s, SB_WIDTH), F32),
                   jax.ShapeDtypeStruct((SB_HEADS, nq), I32)],
        compiler_params=_params(("arbitrary", "arbitrary"), VMEM_LIMIT),
    )(q, k, v)


def _mla_fwd_call(q, k, v):
    s = q.shape[0]
    bq = _row_tile(s, MLA_BQ)
    bk = _row_tile(s, MLA_BK)
    nq = s // bq
    n_diag = max(1, bq // bk)

    def body(q_ref, k_ref, v_ref, o_ref, lse_ref):
        i = pl.program_id(1)
        lane = lax.broadcasted_iota(I32, (bq, LANES), 1)
        row = lax.broadcasted_iota(I32, (bq, bk), 0)
        col = lax.broadcasted_iota(I32, (bq, bk), 1)
        n_full = (i * bq) // bk

        def tile(kb, carry, masked):
            rows = pl.ds(pl.multiple_of(kb * bk, bk), bk)
            vblk = v_ref[rows, :]
            new = []
            for hh in range(2):
                m, l, acc = carry[hh]
                cols = slice(hh * HEAD_PAD, (hh + 1) * HEAD_PAD)
                sc = _dot_nt(q_ref[:, cols], k_ref[rows, cols]) * MLA_SCALE_LOG2
                if masked:
                    sc = jnp.where(col + kb * bk <= row + i * bq, sc, MASK_NEG)
                m_new = jnp.maximum(m, jnp.max(sc, axis=1, keepdims=True))
                p = jnp.exp2(sc - m_new)
                alpha = jnp.exp2(m - m_new)
                l = alpha * l + jnp.sum(p, axis=1, keepdims=True)
                acc = alpha * acc + _dot(p.astype(BF16), vblk)
                new.append((m_new, l, acc))
            return tuple(new)

        one = (jnp.full((bq, 1), MASK_NEG, F32), jnp.zeros((bq, 1), F32), jnp.zeros((bq, LANES), F32))
        carry = lax.fori_loop(0, n_full, lambda kb, cr: tile(kb, cr, False), (one, one))
        for d in range(n_diag):
            carry = tile(n_full + d, carry, True)
        (m0, l0, a0), (m1, l1, a1) = carry
        o_ref[...] = jnp.where(lane < 64, a0 / l0, a1 / l1)
        lse_ref[...] = jnp.where(lane < 64, m0 + jnp.log2(l0), m1 + jnp.log2(l1)) * LN2

    return pl.pallas_call(
        body, name="mla_fwd",
        grid=(4, nq),
        in_specs=[pl.BlockSpec((bq, 2 * HEAD_PAD), lambda h, i: (i, h)),
                  pl.BlockSpec((s, 2 * HEAD_PAD), lambda h, i: (0, h)),
                  pl.BlockSpec((s, LANES), lambda h, i: (0, h))],
        out_specs=[pl.BlockSpec((bq, LANES), lambda h, i: (i, h)),
                   pl.BlockSpec((bq, LANES), lambda h, i: (i, h))],
        out_shape=[jax.ShapeDtypeStruct((s, MLA_WIDTH), F32), jax.ShapeDtypeStruct((s, MLA_WIDTH), F32)],
        compiler_params=_params(("arbitrary", "arbitrary"), VMEM_LIMIT),
    )(q, k, v)


def _out_call(o_sb, g_sb, o_mla, g_mla, x, target, gate, w_out_bf):
    s = x.shape[0]
    tm = _row_tile(s, 256)

    def body(osb_ref, gsb_ref, oml_ref, gml_ref, x_ref, t_ref, gate_ref, w_ref,
             dosb_ref, doml_ref, dgsb_ref, dgml_ref, dy_ref, gw_ref, dgate_ref, sq_ref):
        @pl.when(pl.program_id(0) == 0)
        def _():
            gw_ref[...] = jnp.zeros_like(gw_ref)
            dgate_ref[...] = jnp.zeros_like(dgate_ref)
            sq_ref[...] = jnp.zeros_like(sq_ref)

        g_s, g_m = gsb_ref[...], gml_ref[...]
        sig_s, sig_m = _sigmoid(g_s), _sigmoid(g_m)
        silu_s, silu_m = g_s * sig_s, g_m * sig_m
        o_s, o_m = osb_ref[...], oml_ref[...]
        mixed = jnp.concatenate([o_s * silu_s, o_m * silu_m], axis=1).astype(BF16)
        u = _dot(mixed, w_ref[...])
        gate_v = gate_ref[...]
        err = x_ref[...] + gate_v * u - t_ref[...]
        sq_ref[...] += jnp.sum(err * err, axis=0, keepdims=True)
        dy = err * (1.0 / D_MODEL)
        dy_ref[...] = dy
        dgate_ref[...] += jnp.sum(dy * u, axis=0, keepdims=True)
        du = (dy * gate_v).astype(BF16)
        gw_ref[...] += _dot_tn(mixed, du)
        dmix = _dot_nt(du, w_ref[...])
        dm_s, dm_m = dmix[:, :SB_WIDTH], dmix[:, SB_WIDTH:]
        dosb_ref[...] = (dm_s * silu_s).astype(BF16)
        doml_ref[...] = (dm_m * silu_m).astype(BF16)
        dgsb_ref[...] = (dm_s * o_s * (sig_s * (1.0 + g_s * (1.0 - sig_s)))).astype(BF16)
        dgml_ref[...] = (dm_m * o_m * (sig_m * (1.0 + g_m * (1.0 - sig_m)))).astype(BF16)

    return pl.pallas_call(
        body, name="out_proj_loss",
        grid=(s // tm,),
        in_specs=[_rows(tm, 512), _rows(tm, 512), _rows(tm, 512), _rows(tm, 512),
                  _rows(tm, D_MODEL), _rows(tm, D_MODEL), _full((1, D_MODEL)), _full((D_MODEL, D_MODEL))],
        out_specs=[_rows(tm, 512), _rows(tm, 512), _rows(tm, 512), _rows(tm, 512), _rows(tm, D_MODEL),
                   _full((D_MODEL, D_MODEL)), _full((1, D_MODEL)), _full((1, D_MODEL))],
        out_shape=[jax.ShapeDtypeStruct((s, 512), BF16)] * 4
        + [jax.ShapeDtypeStruct((s, D_MODEL), F32), jax.ShapeDtypeStruct((D_MODEL, D_MODEL), F32),
           jax.ShapeDtypeStruct((1, D_MODEL), F32), jax.ShapeDtypeStruct((1, D_MODEL), F32)],
        compiler_params=_params(("arbitrary",), VMEM_LIMIT),
    )(o_sb, g_sb, o_mla, g_mla, x, target, gate, w_out_bf)


def _head_mask(lane, hh):
    return jnp.where((lane >= 64) if hh else (lane < 64), 1.0, 0.0)


def _pick_lane(packed, lane, which):
    return jnp.sum(jnp.where(lane == which, packed, 0.0), axis=1, keepdims=True)


def _sb_bwd_call(kstart, q, k, v, do, rfin):
    s = q.shape[0]
    bq = _row_tile(s, 256)
    nq = s // bq

    def body(ks_ref, q_ref, k_ref, v_ref, do_ref, r_ref, dq_ref, dk_ref, dv_ref):
        hp, i = pl.program_id(0), pl.program_id(1)

        @pl.when(i == 0)
        def _():
            dk_ref[...] = jnp.zeros_like(dk_ref)
            dv_ref[...] = jnp.zeros_like(dv_ref)

        lane = lax.broadcasted_iota(I32, (bq, LANES), 1)
        row = lax.broadcasted_iota(I32, (bq, bq), 0)
        col = lax.broadcasted_iota(I32, (bq, bq), 1)
        strict = col < row
        upto = jnp.where(row <= col, 1.0, 0.0).astype(BF16)
        before = jnp.where(row < col, 1.0, 0.0).astype(BF16)
        qs = q_ref[...] * jnp.asarray(SB_SCALE, BF16)
        do2 = do_ref[...]
        r2 = r_ref[...]
        dqs = []
        for hh in range(2):
            head = _head_mask(lane, hh).astype(BF16)
            qm = qs * head
            dom = do2 * head
            total = _pick_lane(r2, lane, 64 * hh)

            def tile(kb, carry, masked, qm=qm, dom=dom, total=total):
                pre, gpre, dq = carry
                rows = pl.ds(pl.multiple_of(kb * bq, bq), bq)
                kblk, vblk = k_ref[rows, :], v_ref[rows, :]
                z = _dot_nt(qm, kblk)
                ls, lk = _log_sigmoid_pair(z)
                if masked:
                    lk = jnp.where(strict, lk, 0.0)
                w = jnp.exp(ls + ((total - pre) - _split_dot(lk, upto)))
                if masked:
                    w = jnp.where(strict, w, 0.0)
                g = w * _dot_nt(dom, vblk)
                gsum = gpre + _split_dot(g, before)
                dz = g - jnp.exp(ls) * (g + gsum)
                if masked:
                    dz = jnp.where(strict, dz, 0.0)
                dzb = dz.astype(BF16)
                dk_ref[rows, :] += _dot_tn(dzb, qm)
                dv_ref[rows, :] += _dot_tn(w.astype(BF16), dom)
                return (pre + jnp.sum(lk, axis=1, keepdims=True), gpre + jnp.sum(g, axis=1, keepdims=True),
                        dq + _dot(dzb, kblk))

            init = (jnp.zeros((bq, 1), F32), jnp.zeros((bq, 1), F32), jnp.zeros((bq, LANES), F32))
            carry = lax.fori_loop(ks_ref[2 * hp + hh, i], i, lambda kb, cr, tile=tile: tile(kb, cr, False), init)
            dqs.append(tile(i, carry, True)[2])
        dq_ref[...] = (jnp.where(lane < 64, dqs[0], dqs[1]) * SB_SCALE).astype(BF16)

    return pl.pallas_call(
        body, name="sb_bwd",
        grid_spec=pltpu.PrefetchScalarGridSpec(
            num_scalar_prefetch=1, grid=(4, nq),
            in_specs=[pl.BlockSpec((bq, LANES), lambda h, i, ks: (i, h)),
                      pl.BlockSpec((s, LANES), lambda h, i, ks: (0, h)),
                      pl.BlockSpec((s, LANES), lambda h, i, ks: (0, h)),
                      pl.BlockSpec((bq, LANES), lambda h, i, ks: (i, h)),
                      pl.BlockSpec((bq, LANES), lambda h, i, ks: (i, h))],
            out_specs=[pl.BlockSpec((bq, LANES), lambda h, i, ks: (i, h)),
                       pl.BlockSpec((s, LANES), lambda h, i, ks: (0, h)),
                       pl.BlockSpec((s, LANES), lambda h, i, ks: (0, h))]),
        out_shape=[jax.ShapeDtypeStruct((s, SB_WIDTH), BF16), jax.ShapeDtypeStruct((s, SB_WIDTH), F32),
                   jax.ShapeDtypeStruct((s, SB_WIDTH), F32)],
        compiler_params=_params(("arbitrary", "arbitrary"), VMEM_LIMIT),
    )(kstart, q, k, v, do, rfin)


def _mla_bwd_call(q, k, v, do, o, lse):
    s = q.shape[0]
    bq = _row_tile(s, MLA_BQ)
    bk = _row_tile(s, MLA_BWD_BK)
    nq = s // bq
    n_diag = max(1, bq // bk)

    def body(q_ref, k_ref, v_ref, do_ref, o_ref, lse_ref, dq_ref, dk_ref, dv_ref):
        i = pl.program_id(1)

        @pl.when(i == 0)
        def _():
            dk_ref[...] = jnp.zeros_like(dk_ref)
            dv_ref[...] = jnp.zeros_like(dv_ref)

        lane = lax.broadcasted_iota(I32, (bq, LANES), 1)
        row = lax.broadcasted_iota(I32, (bq, bk), 0)
        col = lax.broadcasted_iota(I32, (bq, bk), 1)
        n_full = (i * bq) // bk
        do2 = do_ref[...]
        prod = do2.astype(F32) * o_ref[...]
        lse2 = lse_ref[...]
        doms, deltas, lses = [], [], []
        for hh in range(2):
            head = _head_mask(lane, hh)
            doms.append(do2 * head.astype(BF16))
            deltas.append(jnp.sum(prod * head, axis=1, keepdims=True))
            lses.append(_pick_lane(lse2, lane, 64 * hh))

        def tile(kb, dqs, masked):
            rows = pl.ds(pl.multiple_of(kb * bk, bk), bk)
            vblk = v_ref[rows, :]
            new, dv_parts = [], []
            for hh in range(2):
                cols = slice(hh * HEAD_PAD, (hh + 1) * HEAD_PAD)
                qh, kblk = q_ref[:, cols], k_ref[rows, cols]
                p = jnp.exp(_dot_nt(qh, kblk) * MLA_SCALE - lses[hh])
                if masked:
                    p = jnp.where(col + kb * bk <= row + i * bq, p, 0.0)
                ds = (p * (_dot_nt(doms[hh], vblk) - deltas[hh]) * MLA_SCALE).astype(BF16)
                dk_ref[rows, cols] += _dot_tn(ds, qh)
                dv_parts.append(_dot_tn(p.astype(BF16), doms[hh]))
                new.append(dqs[hh] + _dot(ds, kblk))
            dv_ref[rows, :] += dv_parts[0] + dv_parts[1]
            return tuple(new)

        zero = jnp.zeros((bq, HEAD_PAD), F32)
        dqs = lax.fori_loop(0, n_full, lambda kb, dqs: tile(kb, dqs, False), (zero, zero))
        for d in range(n_diag):
            dqs = tile(n_full + d, dqs, True)
        dq_ref[:, :HEAD_PAD] = dqs[0]
        dq_ref[:, HEAD_PAD:] = dqs[1]

    return pl.pallas_call(
        body, name="mla_bwd",
        grid=(4, nq),
        in_specs=[pl.BlockSpec((bq, 2 * HEAD_PAD), lambda h, i: (i, h)),
                  pl.BlockSpec((s, 2 * HEAD_PAD), lambda h, i: (0, h)),
                  pl.BlockSpec((s, LANES), lambda h, i: (0, h)),
                  pl.BlockSpec((bq, LANES), lambda h, i: (i, h)),
                  pl.BlockSpec((bq, LANES), lambda h, i: (i, h)),
                  pl.BlockSpec((bq, LANES), lambda h, i: (i, h))],
        out_specs=[pl.BlockSpec((bq, 2 * HEAD_PAD), lambda h, i: (i, h)),
                   pl.BlockSpec((s, 2 * HEAD_PAD), lambda h, i: (0, h)),
                   pl.BlockSpec((s, LANES), lambda h, i: (0, h))],
        out_shape=[jax.ShapeDtypeStruct((s, MLA_PAD_WIDTH), F32), jax.ShapeDtypeStruct((s, MLA_PAD_WIDTH), F32),
                   jax.ShapeDtypeStruct((s, MLA_WIDTH), F32)],
        compiler_params=_params(("arbitrary", "arbitrary"), VMEM_LIMIT),
    )(q, k, v, do, o, lse)


def _rms_bwd(d_out, inp, r, weight, n):
    normed = inp * r
    gw = d_out * weight
    d_in = r * (gw - normed * (jnp.sum(gw * normed, axis=-1, keepdims=True) * (1.0 / n)))
    return d_in, d_out * normed


def _mla_prep_bwd_call(dq, dk, dv, q0, k0, cqn, ckvn, c_q, c_kv, cos_t, sin_a, sin_b,
                       q_lora_norm, kv_lora_norm, qhn_pad, khn_pad, w_uq_bf, w_uk_bf, w_uv_bf):
    s = dq.shape[0]
    tm = _row_tile(s, 256)

    def body(dq_ref, dk_ref, dv_ref, q0_ref, k0_ref, cqn_ref, ckvn_ref, cq_ref, ckv_ref,
             cos_ref, sa_ref, sb_ref, qln_ref, kvln_ref, qhn_ref, khn_ref, wuq_ref, wuk_ref, wuv_ref,
             dcq_ref, dckv_ref, dkr_ref, gwuq_ref, gwuk_ref, gwuv_ref, gqln_ref, gkvln_ref, gqhn_ref, gkhn_ref):
        @pl.when(pl.program_id(0) == 0)
        def _():
            for ref in (gwuq_ref, gwuk_ref, gwuv_ref, gqln_ref, gkvln_ref, gqhn_ref, gkhn_ref):
                ref[...] = jnp.zeros_like(ref)

        cos_t, sin_a, sin_b = cos_ref[...], sa_ref[...], sb_ref[...]
        lane = lax.broadcasted_iota(I32, (tm, LANES), 1)
        rope_lanes = jnp.logical_and(lane >= KR_LANE, lane < KR_LANE + MLA_ROPE_DIM)
        cqn, ckvn = cqn_ref[...], ckvn_ref[...]
        d_cqn = jnp.zeros((tm, Q_LORA_RANK), F32)
        d_ckvn = jnp.zeros((tm, KV_LORA_RANK), F32)
        d_kr = jnp.zeros((tm, LANES), F32)
        g_qhn = jnp.zeros((1, LANES), F32)
        g_khn = jnp.zeros((1, LANES), F32)
        for h in range(MLA_HEADS):
            cols = slice(h * HEAD_PAD, (h + 1) * HEAD_PAD)
            q0 = q0_ref[:, cols]
            rq = lax.rsqrt(jnp.sum(q0 * q0, axis=-1, keepdims=True) * (1.0 / MLA_QK_DIM) + EPS)
            d_q0, gq = _rms_bwd(_rope_adjoint(dq_ref[:, cols], cos_t, sin_a, sin_b), q0, rq, qhn_ref[...],
                                MLA_QK_DIM)
            g_qhn += jnp.sum(gq, axis=0, keepdims=True)
            d_q0b = d_q0.astype(BF16)
            d_cqn += _dot_nt(d_q0b, wuq_ref[:, cols])
            gwuq_ref[:, cols] += _dot_tn(cqn, d_q0b)
            k0 = k0_ref[:, cols]
            rk = lax.rsqrt(jnp.sum(k0 * k0, axis=-1, keepdims=True) * (1.0 / MLA_QK_DIM) + EPS)
            d_k0, gk = _rms_bwd(_rope_adjoint(dk_ref[:, cols], cos_t, sin_a, sin_b), k0, rk, khn_ref[...],
                                MLA_QK_DIM)
            g_khn += jnp.sum(gk, axis=0, keepdims=True)
            d_kr += jnp.where(rope_lanes, d_k0, 0.0)
            d_k0b = d_k0.astype(BF16)
            d_ckvn += _dot_nt(d_k0b, wuk_ref[:, cols])
            gwuk_ref[:, cols] += _dot_tn(ckvn, d_k0b)
        dvb = dv_ref[...].astype(BF16)
        d_ckvn += _dot_nt(dvb, wuv_ref[...])
        gwuv_ref[...] += _dot_tn(ckvn, dvb)
        gqhn_ref[...] += g_qhn
        gkhn_ref[...] += g_khn
        dkr_ref[...] = d_kr.astype(BF16)
        cq = cq_ref[...]
        rcq = lax.rsqrt(jnp.mean(cq * cq, axis=-1, keepdims=True) + EPS)
        d_cq, gl = _rms_bwd(d_cqn, cq, rcq, qln_ref[...], Q_LORA_RANK)
        dcq_ref[...] = d_cq.astype(BF16)
        gqln_ref[...] += jnp.sum(gl, axis=0, keepdims=True)
        ckv = ckv_ref[...]
        rckv = lax.rsqrt(jnp.mean(ckv * ckv, axis=-1, keepdims=True) + EPS)
        d_ckv, gl = _rms_bwd(d_ckvn, ckv, rckv, kvln_ref[...], KV_LORA_RANK)
        dckv_ref[...] = d_ckv.astype(BF16)
        gkvln_ref[...] += jnp.sum(gl, axis=0, keepdims=True)

    return pl.pallas_call(
        body, name="mla_prep_bwd",
        grid=(s // tm,),
        in_specs=[_rows(tm, MLA_PAD_WIDTH), _rows(tm, MLA_PAD_WIDTH), _rows(tm, MLA_WIDTH),
                  _rows(tm, MLA_PAD_WIDTH), _rows(tm, MLA_PAD_WIDTH),
                  _rows(tm, Q_LORA_RANK), _rows(tm, KV_LORA_RANK), _rows(tm, Q_LORA_RANK), _rows(tm, KV_LORA_RANK),
                  _rows(tm, LANES), _rows(tm, LANES), _rows(tm, LANES),
                  _full((1, Q_LORA_RANK)), _full((1, KV_LORA_RANK)), _full((1, LANES)), _full((1, LANES)),
                  _full((Q_LORA_RANK, MLA_PAD_WIDTH)), _full((KV_LORA_RANK, MLA_PAD_WIDTH)),
                  _full((KV_LORA_RANK, MLA_WIDTH))],
        out_specs=[_rows(tm, Q_LORA_RANK), _rows(tm, KV_LORA_RANK), _rows(tm, LANES),
                   _full((Q_LORA_RANK, MLA_PAD_WIDTH)), _full((KV_LORA_RANK, MLA_PAD_WIDTH)),
                   _full((KV_LORA_RANK, MLA_WIDTH)),
                   _full((1, Q_LORA_RANK)), _full((1, KV_LORA_RANK)), _full((1, LANES)), _full((1, LANES))],
        out_shape=[jax.ShapeDtypeStruct((s, Q_LORA_RANK), BF16), jax.ShapeDtypeStruct((s, KV_LORA_RANK), BF16),
                   jax.ShapeDtypeStruct((s, LANES), BF16),
                   jax.ShapeDtypeStruct((Q_LORA_RANK, MLA_PAD_WIDTH), F32),
                   jax.ShapeDtypeStruct((KV_LORA_RANK, MLA_PAD_WIDTH), F32),
                   jax.ShapeDtypeStruct((KV_LORA_RANK, MLA_WIDTH), F32),
                   jax.ShapeDtypeStruct((1, Q_LORA_RANK), F32), jax.ShapeDtypeStruct((1, KV_LORA_RANK), F32),
                   jax.ShapeDtypeStruct((1, LANES), F32), jax.ShapeDtypeStruct((1, LANES), F32)],
        compiler_params=_params(("arbitrary",), VMEM_LIMIT),
    )(dq, dk, dv, q0, k0, cqn, ckvn, c_q, c_kv, cos_t, sin_a, sin_b,
      q_lora_norm, kv_lora_norm, qhn_pad, khn_pad, w_uq_bf, w_uk_bf, w_uv_bf)


def _dh_call(pieces, x, dy, shift, scale, norm_w, w_in_bf):
    s = x.shape[0]
    tm = _row_tile(s, 256)
    widths = [p.shape[1] for p in pieces]
    offsets = [sum(widths[:j]) for j in range(len(widths))]
    assert offsets[-1] + widths[-1] == IN_COLS_PAD
    n = len(pieces)

    def body(*refs):
        p_refs = refs[:n]
        x_ref, dy_ref, sh_ref, sc_ref, nw_ref, w_ref, gx_ref, dp_ref, dsh_ref, dsc_ref, gnw_ref = refs[n:]

        @pl.when(pl.program_id(0) == 0)
        def _():
            dsh_ref[...] = jnp.zeros_like(dsh_ref)
            dsc_ref[...] = jnp.zeros_like(dsc_ref)
            gnw_ref[...] = jnp.zeros_like(gnw_ref)

        for p_ref, c0, width in zip(p_refs, offsets, widths):
            dp_ref[:, c0:c0 + width] = p_ref[...].astype(BF16)
        dh = _dot_nt(dp_ref[...], w_ref[...])
        xx = x_ref[...]
        r0 = lax.rsqrt(jnp.mean(xx * xx, axis=-1, keepdims=True) + EPS)
        xn = xx * r0
        nw = nw_ref[...]
        dsh_ref[...] += jnp.sum(dh, axis=0, keepdims=True)
        dsc_ref[...] += jnp.sum(dh * (xn * nw), axis=0, keepdims=True)
        dn = dh * (1.0 + sc_ref[...])
        gnw_ref[...] += jnp.sum(dn * xn, axis=0, keepdims=True)
        dxn = dn * nw
        gx_ref[...] = dy_ref[...] + r0 * (dxn - xn * jnp.mean(dxn * xn, axis=-1, keepdims=True))

    return pl.pallas_call(
        body, name="in_proj_bwd",
        grid=(s // tm,),
        in_specs=[_rows(tm, w) for w in widths]
        + [_rows(tm, D_MODEL), _rows(tm, D_MODEL), _full((1, D_MODEL)), _full((1, D_MODEL)), _full((1, D_MODEL)),
           _full((D_MODEL, IN_COLS_PAD))],
        out_specs=[_rows(tm, D_MODEL), _rows(tm, IN_COLS_PAD),
                   _full((1, D_MODEL)), _full((1, D_MODEL)), _full((1, D_MODEL))],
        out_shape=[jax.ShapeDtypeStruct((s, D_MODEL), F32), jax.ShapeDtypeStruct((s, IN_COLS_PAD), BF16),
                   jax.ShapeDtypeStruct((1, D_MODEL), F32), jax.ShapeDtypeStruct((1, D_MODEL), F32),
                   jax.ShapeDtypeStruct((1, D_MODEL), F32)],
        compiler_params=_params(("arbitrary",), VMEM_LIMIT),
    )(*pieces, x, dy, shift, scale, norm_w, w_in_bf)


def _gw_in_call(hb, dproj):
    s = hb.shape[0]
    tk = _row_tile(s, 512)
    tn = IN_COLS_PAD // 2

    def body(h_ref, d_ref, g_ref):
        @pl.when(pl.program_id(1) == 0)
        def _():
            g_ref[...] = jnp.zeros_like(g_ref)

        g_ref[...] += _dot_tn(h_ref[...], d_ref[...])

    return pl.pallas_call(
        body, name="in_proj_wgrad",
        grid=(2, s // tk),
        in_specs=[pl.BlockSpec((tk, D_MODEL), lambda j, t: (t, 0)), pl.BlockSpec((tk, tn), lambda j, t: (t, j))],
        out_specs=pl.BlockSpec((D_MODEL, tn), lambda j, t: (0, j)),
        out_shape=jax.ShapeDtypeStruct((D_MODEL, IN_COLS_PAD), F32),
        compiler_params=_params(("arbitrary", "arbitrary"), VMEM_LIMIT),
    )(hb, dproj)


def _adamw(g, w, m, v):
    m = ADAM_B1 * m + (1.0 - ADAM_B1) * g
    v = ADAM_B2 * v + (1.0 - ADAM_B2) * (g * g)
    m_hat = m / (1.0 - ADAM_B1 ** ADAM_STEP)
    v_hat = v / (1.0 - ADAM_B2 ** ADAM_STEP)
    delta = -ADAM_LR * (m_hat / (jnp.sqrt(v_hat) + ADAM_EPS) + ADAM_WD * w)
    return delta, m, v


def _adam_shard_call(name, own, sib, w, m, v):
    r, c = w.shape
    tr = r if r <= 512 else 256

    def body(own_ref, sib_ref, w_ref, m_ref, v_ref, g_ref, d_ref, nm_ref, nv_ref):
        a = ((own_ref[0] + own_ref[1]) + own_ref[2]) + own_ref[3]
        b = ((sib_ref[0] + sib_ref[1]) + sib_ref[2]) + sib_ref[3]
        g = a + b
        g_ref[...] = g
        d_ref[...], nm_ref[...], nv_ref[...] = _adamw(g, w_ref[...], m_ref[...], v_ref[...])

    part = pl.BlockSpec((4, tr, c), lambda i: (0, i, 0))
    blk = pl.BlockSpec((tr, c), lambda i: (i, 0))
    return pl.pallas_call(
        body, name=name,
        grid=(r // tr,),
        in_specs=[part, part, blk, blk, blk],
        out_specs=[blk] * 4,
        out_shape=[jax.ShapeDtypeStruct((r, c), F32)] * 4,
        compiler_params=_params(("arbitrary",), VMEM_LIMIT),
    )(own, sib, w, m, v)


def _adam_ada_call(c_all, d_all, w, m, v):
    r, c = w.shape
    tr = 256

    def body(c_ref, d_ref, w_ref, m_ref, v_ref, g_ref, dl_ref, nm_ref, nv_ref):
        cc = c_ref[...]
        sc = cc * _sigmoid(cc)
        dd = d_ref[...]
        sc_hi = sc.astype(BF16)
        sc_lo = (sc - sc_hi.astype(F32)).astype(BF16)
        dd_hi = dd.astype(BF16)
        dd_lo = (dd - dd_hi.astype(F32)).astype(BF16)
        g = _dot_tn(sc_hi, dd_hi) + (_dot_tn(sc_hi, dd_lo) + _dot_tn(sc_lo, dd_hi))
        g_ref[...] = g
        dl_ref[...], nm_ref[...], nv_ref[...] = _adamw(g, w_ref[...], m_ref[...], v_ref[...])

    blk = pl.BlockSpec((tr, c), lambda i: (i, 0))
    return pl.pallas_call(
        body, name="adam_w_ada",
        grid=(r // tr,),
        in_specs=[pl.BlockSpec((16, tr), lambda i: (0, i)), pl.BlockSpec((16, c), lambda i: (0, 0)), blk, blk, blk],
        out_specs=[blk] * 4,
        out_shape=[jax.ShapeDtypeStruct((r, c), F32)] * 4,
        compiler_params=_params(("arbitrary",), VMEM_LIMIT),
    )(c_all, d_all, w, m, v)


def _adam_small_call(packs, w, m, v, first_row):
    rows = w.shape[0]

    def body(p_ref, w_ref, m_ref, v_ref, g_ref, d_ref, nm_ref, nv_ref):
        g = p_ref[0, first_row:first_row + rows, :]
        for b in range(1, 8):
            g = g + p_ref[b, first_row:first_row + rows, :]
        g_ref[...] = g
        d_ref[...], nm_ref[...], nv_ref[...] = _adamw(g, w_ref[...], m_ref[...], v_ref[...])

    return pl.pallas_call(
        body, name="adam_vectors",
        out_shape=[jax.ShapeDtypeStruct((rows, LANES), F32)] * 4,
    )(packs, w, m, v)


def _rope_tables(positions):
    inv_freq = ROPE_THETA ** (-jnp.arange(0, MLA_ROPE_DIM, 2, dtype=F32) / MLA_ROPE_DIM)
    ang = positions.astype(F32)[:, None] * inv_freq
    cos, sin = jnp.cos(ang), jnp.sin(ang)
    s = positions.shape[0]
    half = MLA_ROPE_DIM // 2
    zeros = lambda n: jnp.zeros((s, n), F32)
    cos_t = jnp.concatenate([jnp.ones((s, MLA_NOPE_DIM), F32), cos, cos, zeros(HEAD_PAD - MLA_QK_DIM)], axis=1)
    sin_a = jnp.concatenate([zeros(MLA_NOPE_DIM), -sin, zeros(half + HEAD_PAD - MLA_QK_DIM)], axis=1)
    sin_b = jnp.concatenate([zeros(MLA_NOPE_DIM + half), sin, zeros(HEAD_PAD - MLA_QK_DIM)], axis=1)
    return cos_t, sin_a, sin_b


def _unshard_cols(g):
    return jnp.transpose(g, (1, 0, 2)).reshape(g.shape[1], 4 * g.shape[2])


def _shard_cols(g):
    r, c4 = g.shape
    return jnp.transpose(g.reshape(r, 4, c4 // 4), (1, 0, 2))


def _pad_heads(w, width):
    r = w.shape[0]
    w = w.reshape(r, MLA_HEADS, width)
    return jnp.pad(w, ((0, 0), (0, 0), (0, HEAD_PAD - width))).reshape(r, MLA_PAD_WIDTH)


def _pad_lanes(vec):
    return jnp.pad(vec, ((0, 0), (0, LANES - vec.shape[1])))


def _vector_rows(vecs):
    rows = []
    for vec in vecs:
        n = vec.shape[1]
        pad = (-n) % LANES
        rows.append(jnp.pad(vec, ((0, 0), (0, pad))).reshape((n + pad) // LANES, LANES))
    return jnp.concatenate(rows, axis=0)


def kernel(x, c, positions, w_ada, b_ada, norm_w, w_in, q_lora_norm, w_uq, kv_lora_norm, w_ukv, q_head_norm, k_head_norm, w_out, loss_target, m_w_ada, m_b_ada, m_norm_w, m_w_in, m_q_lora_norm, m_w_uq, m_kv_lora_norm, m_w_ukv, m_q_head_norm, m_k_head_norm, m_w_out, v_w_ada, v_b_ada, v_norm_w, v_w_in, v_q_lora_norm, v_w_uq, v_kv_lora_norm, v_w_ukv, v_q_head_norm, v_k_head_norm, v_w_out):
    wa_g, win_g, wuq_g, wukv_g, wout_g = _gather_call([w_ada, w_in, w_uq, w_ukv, w_out])
    (sq_sum, grad_x, g_w_in, g_w_uq, g_w_ukv, g_w_out, d_ada, g_norm_w, g_qln, g_kvln, g_qhn, g_khn) = _local_step(
        x[0], c, positions[0], loss_target[0], _unshard_cols(wa_g), b_ada, norm_w, _unshard_cols(win_g),
        q_lora_norm, _unshard_cols(wuq_g), kv_lora_norm, _unshard_cols(wukv_g), q_head_norm, k_head_norm,
        wout_g.reshape(D_MODEL, D_MODEL))
    loss = lax.psum(0.5 * sq_sum / D_MODEL, ("x", "y", "c"))

    grads = [_shard_cols(g_w_in), _shard_cols(g_w_uq), _shard_cols(g_w_ukv),
             g_w_out.reshape(4, D_MODEL // 4, D_MODEL)]
    small = _vector_rows([c, d_ada, g_norm_w, g_qln, g_kvln, g_qhn, g_khn])
    small = jnp.pad(small, ((0, (-small.shape[0]) % 8), (0, 0)))
    own, sib, packs = _exchange_call(grads, small)

    names = ["adam_w_in", "adam_w_uq", "adam_w_ukv", "adam_w_out"]
    shard_w = [(w_in, m_w_in, v_w_in), (w_uq, m_w_uq, v_w_uq), (w_ukv, m_w_ukv, v_w_ukv),
               (w_out, m_w_out, v_w_out)]
    res = {}
    for name, o_g, s_g, (w, m, v) in zip(names, own, sib, shard_w):
        res[name] = _adam_shard_call(name, o_g, s_g, w[0], m[0], v[0])
    chip = 2 * lax.axis_index("x") + lax.axis_index("y")
    ada_cols = w_ada.shape[2]
    c_rows = D_MODEL // LANES
    c_all = jnp.pad(packs[:, :c_rows, :].reshape(8, D_MODEL), ((0, 8), (0, 0)))
    d_rows = packs[:, c_rows:c_rows + 3 * c_rows, :].reshape(8, 3 * D_MODEL)
    d_all = jnp.pad(lax.dynamic_slice_in_dim(d_rows, chip * ada_cols, ada_cols, axis=1), ((0, 8), (0, 0)))
    res_ada = _adam_ada_call(c_all, d_all, w_ada[0], m_w_ada[0], v_w_ada[0])
    vec_names = [(b_ada, m_b_ada, v_b_ada), (norm_w, m_norm_w, v_norm_w), (q_lora_norm, m_q_lora_norm, v_q_lora_norm),
                 (kv_lora_norm, m_kv_lora_norm, v_kv_lora_norm), (q_head_norm, m_q_head_norm, v_q_head_norm),
                 (k_head_norm, m_k_head_norm, v_k_head_norm)]
    packed = [_vector_rows([t[j] for t in vec_names]) for j in range(3)]
    n_rows = packed[0].shape[0]
    pad_rows = (-n_rows) % 8
    packed = [jnp.pad(p, ((0, pad_rows), (0, 0))) for p in packed]
    res_vec = _adam_small_call(packs, packed[0], packed[1], packed[2], c_rows)

    def unpack(arr):
        outs, r0 = [], 0
        for t in vec_names:
            n = t[0].shape[1]
            nr = -(-n // LANES)
            outs.append(arr[r0:r0 + nr].reshape(1, nr * LANES)[:, :n])
            r0 += nr
        return outs

    vec_out = [unpack(a) for a in res_vec]

    def ordered(kind):
        big = lambda name: res[name][kind][None]
        return [res_ada[kind][None], vec_out[kind][0], vec_out[kind][1], big("adam_w_in"), vec_out[kind][2],
                big("adam_w_uq"), vec_out[kind][3], big("adam_w_ukv"), vec_out[kind][4], vec_out[kind][5],
                big("adam_w_out")]

    return (loss, grad_x[None], *ordered(0), *ordered(1), *ordered(2), *ordered(3))


def _local_step(x2, c, positions, tgt, w_ada_full, b_ada, norm_w, w_in_full, q_lora_norm, w_uq_full,
                kv_lora_norm, w_ukv_full, q_head_norm, k_head_norm, w_out_full):
    w_ada_bf = w_ada_full.astype(BF16)
    kr_block = jnp.pad(w_in_full[:, 2688:2720], ((0, 0), (KR_LANE, LANES - KR_LANE - MLA_ROPE_DIM)))
    w_in_bf = jnp.concatenate([w_in_full[:, :2688], w_in_full[:, 2720:], kr_block], axis=1).astype(BF16)
    w_uq_bf = _pad_heads(w_uq_full, MLA_QK_DIM).astype(BF16)
    w_ukv_heads = w_ukv_full.reshape(KV_LORA_RANK, MLA_HEADS, 2 * MLA_NOPE_DIM)
    w_uk_bf = _pad_heads(w_ukv_heads[:, :, :MLA_NOPE_DIM].reshape(KV_LORA_RANK, -1), MLA_NOPE_DIM).astype(BF16)
    w_uv_bf = w_ukv_heads[:, :, MLA_NOPE_DIM:].reshape(KV_LORA_RANK, MLA_WIDTH).astype(BF16)
    w_out_bf = w_out_full.astype(BF16)
    qhn_pad, khn_pad = _pad_lanes(q_head_norm), _pad_lanes(k_head_norm)
    cos_t, sin_a, sin_b = _rope_tables(positions)

    ada = _ada_call(c, w_ada_bf, b_ada)
    shift, scale, gate = ada[:, :D_MODEL], ada[:, D_MODEL:2 * D_MODEL], ada[:, 2 * D_MODEL:]
    hb, q_sb, k_sb, v_sb, g_sb, c_q, c_kv, g_mla, k_rope = _pre_call(x2, shift, scale, norm_w, w_in_bf)
    q_m, k_m, v_m, cqn, ckvn, q0, k0 = _mla_prep_call(
        c_q, c_kv, k_rope, cos_t, sin_a, sin_b, q_lora_norm, kv_lora_norm, qhn_pad, khn_pad,
        w_uq_bf, w_uk_bf, w_uv_bf)
    o_sb, r_sb, kstart = _sb_fwd_call(q_sb, k_sb, v_sb)
    o_mla, lse = _mla_fwd_call(q_m, k_m, v_m)
    do_sb, do_mla, dg_sb, dg_mla, dy, g_w_out, d_gate, sq = _out_call(
        o_sb, g_sb, o_mla, g_mla, x2, tgt, gate, w_out_bf)

    dq_sb, dk_sb, dv_sb = _sb_bwd_call(kstart, q_sb, k_sb, v_sb, do_sb, r_sb)
    dq_m, dk_m, dv_m = _mla_bwd_call(q_m, k_m, v_m, do_mla, o_mla, lse)
    (d_cq, d_ckv, d_kr, g_wuq_pad, g_wuk_pad, g_wuv, g_qln, g_kvln, g_qhn, g_khn) = _mla_prep_bwd_call(
        dq_m, dk_m, dv_m, q0, k0, cqn, ckvn, c_q, c_kv, cos_t, sin_a, sin_b,
        q_lora_norm, kv_lora_norm, qhn_pad, khn_pad, w_uq_bf, w_uk_bf, w_uv_bf)
    grad_x, dproj, d_shift, d_scale, g_norm_w = _dh_call(
        [dq_sb, dk_sb, dv_sb, dg_sb, d_cq, d_ckv, dg_mla, d_kr], x2, dy, shift, scale, norm_w, w_in_bf)
    g_win_pad = _gw_in_call(hb, dproj)

    g_w_in = jnp.concatenate([g_win_pad[:, :2688],
                              g_win_pad[:, C_KR + KR_LANE:C_KR + KR_LANE + MLA_ROPE_DIM],
                              g_win_pad[:, 2688:3200]], axis=1)
    g_w_uq = g_wuq_pad.reshape(Q_LORA_RANK, MLA_HEADS, HEAD_PAD)[:, :, :MLA_QK_DIM].reshape(Q_LORA_RANK, -1)
    g_w_ukv = jnp.concatenate(
        [g_wuk_pad.reshape(KV_LORA_RANK, MLA_HEADS, HEAD_PAD)[:, :, :MLA_NOPE_DIM],
         g_wuv.reshape(KV_LORA_RANK, MLA_HEADS, MLA_NOPE_DIM)], axis=2).reshape(KV_LORA_RANK, -1)
    d_ada = jnp.concatenate([d_shift, d_scale, d_gate], axis=1)
    return (jnp.sum(sq), grad_x, g_w_in, g_w_uq, g_w_ukv, g_w_out, d_ada, g_norm_w, g_qln, g_kvln,
            g_qhn[:, :MLA_QK_DIM], g_khn[:, :MLA_QK_DIM])
```

```python
import functools
import math

import jax
import jax.numpy as jnp
from jax import lax
from jax.experimental import pallas as pl
from jax.experimental.pallas import tpu as pltpu

F32 = jnp.float32
BF16 = jnp.bfloat16
I32 = jnp.int32

D_MODEL = 1024
SB_HEADS = 8
SB_WIDTH = 512
MLA_HEADS = 8
MLA_QK_DIM = 96
MLA_NOPE_DIM = 64
MLA_ROPE_DIM = 32
MLA_WIDTH = 512
Q_LORA_RANK = 384
KV_LORA_RANK = 256
ROPE_THETA = 10000.0
EPS = 1e-6
LANES = 128
HEAD_PAD = 128
MLA_PAD_WIDTH = MLA_HEADS * HEAD_PAD

C_Q, C_K, C_V, C_G = 0, 512, 1024, 1536
C_CQ, C_CKV, C_GM, C_KR = 2048, 2432, 2688, 3200
IN_COLS_PAD = 3328
KR_LANE = 64

ADAM_LR = 0.001
ADAM_B1 = 0.9
ADAM_B2 = 0.999
ADAM_EPS = 1e-08
ADAM_WD = 0.01
ADAM_STEP = 10

SB_SCALE = 0.125
MLA_SCALE = 1.0 / math.sqrt(MLA_QK_DIM)
LN2 = math.log(2.0)
MLA_SCALE_LOG2 = MLA_SCALE / LN2
MLA_BQ = 256
MLA_BK = 1024
MLA_BWD_BK = 512
SB_DEAD = -104.0
MASK_NEG = -1e30

VMEM_LIMIT = 56 * 1024 * 1024
MESH = pl.DeviceIdType.MESH


def _dot(a, b):
    return jnp.dot(a, b, preferred_element_type=F32)


def _dot_nt(a, b):
    return lax.dot_general(a, b, (((1,), (1,)), ((), ())), preferred_element_type=F32)


def _dot_tn(a, b):
    return lax.dot_general(a, b, (((0,), (0,)), ((), ())), preferred_element_type=F32)


def _sigmoid(x):
    return 1.0 / (1.0 + jnp.exp(-x))


def _split_dot(a, m):
    hi = a.astype(BF16)
    lo = (a - hi.astype(F32)).astype(BF16)
    return _dot(hi, m) + _dot(lo, m)


def _params(sem, vmem=None):
    return pltpu.CompilerParams(dimension_semantics=sem, vmem_limit_bytes=vmem)


def _row_tile(s, want):
    return min(want, s)


def _hbm_spec():
    return pl.BlockSpec(memory_space=pltpu.HBM)


def _gather_call(shards):
    n = len(shards)
    halves = [s.shape[1] // 2 for s in shards]

    def body(*refs):
        ins, outs = refs[:n], refs[n:2 * n]
        ici_send, ici_recv, d2d_send, d2d_recv, loc_sems = refs[2 * n:]
        x, y, c = lax.axis_index("x"), lax.axis_index("y"), lax.axis_index("c")
        me = 2 * x + y
        peers = [(1 - x, y), (x, 1 - y), (1 - x, 1 - y)]

        def rows(a, which):
            return pl.ds(pl.multiple_of(which * halves[a], 16), halves[a])

        def ici(a, j, slot):
            px, py = peers[j]
            return pltpu.make_async_remote_copy(
                src_ref=ins[a].at[0, rows(a, c)], dst_ref=outs[a].at[slot, rows(a, c)],
                send_sem=ici_send.at[3 * a + j], recv_sem=ici_recv.at[3 * a + j],
                device_id=(px, py, c), device_id_type=MESH)

        def d2d(a, j, which):
            px, py = peers[j]
            piece = outs[a].at[2 * px + py, rows(a, which)]
            return pltpu.make_async_remote_copy(
                src_ref=piece, dst_ref=piece,
                send_sem=d2d_send.at[3 * a + j], recv_sem=d2d_recv.at[3 * a + j],
                device_id=(x, y, 1 - c), device_id_type=MESH)

        local = [pltpu.make_async_copy(ins[a].at[0], outs[a].at[me], loc_sems.at[a]) for a in range(n)]
        for cp in local:
            cp.start()
        sends = [ici(a, j, me) for a in range(n) for j in range(3)]
        for cp in sends:
            cp.start()
        for a in range(n):
            for j in range(3):
                px, py = peers[j]
                ici(a, j, 2 * px + py).wait_recv()
                cp = d2d(a, j, c)
                cp.start()
                sends.append(cp)
        for a in range(n):
            for j in range(3):
                d2d(a, j, 1 - c).wait_recv()
        for cp in sends:
            cp.wait_send()
        for cp in local:
            cp.wait()

    return pl.pallas_call(
        body, name="gather_weights",
        out_shape=[jax.ShapeDtypeStruct((4,) + s.shape[1:], s.dtype) for s in shards],
        in_specs=[_hbm_spec() for _ in shards],
        out_specs=[_hbm_spec() for _ in shards],
        scratch_shapes=[pltpu.SemaphoreType.DMA((3 * n,)), pltpu.SemaphoreType.DMA((3 * n,)),
                        pltpu.SemaphoreType.DMA((3 * n,)), pltpu.SemaphoreType.DMA((3 * n,)),
                        pltpu.SemaphoreType.DMA((n,))],
    )(*shards)


def _exchange_call(grads, small):
    n = len(grads)

    def body(*refs):
        g_in, small_in = refs[:n], refs[n]
        own, sib, packs = refs[n + 1:2 * n + 1], refs[2 * n + 1:3 * n + 1], refs[3 * n + 1]
        ici_send, ici_recv, d2d_send, d2d_recv, sm_send, sm_recv, loc_sems = refs[3 * n + 2:]
        x, y, c = lax.axis_index("x"), lax.axis_index("y"), lax.axis_index("c")
        me = 2 * x + y
        me8 = 4 * x + 2 * y + c
        sibling = (x, y, 1 - c)
        peers = [(1 - x, y), (x, 1 - y), (1 - x, 1 - y)]
        flips = [(fx, fy, fc) for fx in (0, 1) for fy in (0, 1) for fc in (0, 1)][1:]

        def ici(a, j, src_slot, dst_slot):
            px, py = peers[j]
            return pltpu.make_async_remote_copy(
                src_ref=g_in[a].at[src_slot], dst_ref=own[a].at[dst_slot],
                send_sem=ici_send.at[3 * a + j], recv_sem=ici_recv.at[3 * a + j],
                device_id=(px, py, c), device_id_type=MESH)

        def d2d(a, rel, chip, src):
            return pltpu.make_async_remote_copy(
                src_ref=src, dst_ref=sib[a].at[chip],
                send_sem=d2d_send.at[4 * a + rel], recv_sem=d2d_recv.at[4 * a + rel],
                device_id=sibling, device_id_type=MESH)

        def flipped(r):
            fx, fy, fc = flips[r]
            return ((1 - x) if fx else x, (1 - y) if fy else y, (1 - c) if fc else c)

        def sm(r, slot):
            return pltpu.make_async_remote_copy(
                src_ref=small_in, dst_ref=packs.at[slot],
                send_sem=sm_send.at[r], recv_sem=sm_recv.at[r],
                device_id=flipped(r), device_id_type=MESH)

        def peer8(r):
            px, py, pc = flipped(r)
            return 4 * px + 2 * py + pc

        local = [pltpu.make_async_copy(g_in[a].at[me], own[a].at[me], loc_sems.at[a]) for a in range(n)]
        local.append(pltpu.make_async_copy(small_in, packs.at[me8], loc_sems.at[n]))
        for cp in local:
            cp.start()
        sends = []
        for r in range(7):
            sends.append(sm(r, me8))
        for a in range(n):
            for j in range(3):
                px, py = peers[j]
                sends.append(ici(a, j, 2 * px + py, me))
        for cp in sends:
            cp.start()
        for a in range(n):
            cp = d2d(a, 0, me, g_in[a].at[me])
            cp.start()
            sends.append(cp)
        for a in range(n):
            for j in range(3):
                px, py = peers[j]
                ici(a, j, me, 2 * px + py).wait_recv()
                cp = d2d(a, 1 + j, 2 * px + py, own[a].at[2 * px + py])
                cp.start()
                sends.append(cp)
        for a in range(n):
            d2d(a, 0, me, g_in[a].at[me]).wait_recv()
            for j in range(3):
                px, py = peers[j]
                d2d(a, 1 + j, 2 * px + py, g_in[a].at[me]).wait_recv()
        for r in range(7):
            sm(r, peer8(r)).wait_recv()
        for cp in sends:
            cp.wait_send()
        for cp in local:
            cp.wait()

    out_shape = ([jax.ShapeDtypeStruct(g.shape, g.dtype) for g in grads] * 2
                 + [jax.ShapeDtypeStruct((8,) + small.shape, small.dtype)])
    res = pl.pallas_call(
        body, name="exchange_grads",
        out_shape=out_shape,
        in_specs=[_hbm_spec() for _ in range(n + 1)],
        out_specs=[_hbm_spec() for _ in range(2 * n + 1)],
        scratch_shapes=[pltpu.SemaphoreType.DMA((3 * n,)), pltpu.SemaphoreType.DMA((3 * n,)),
                        pltpu.SemaphoreType.DMA((4 * n,)), pltpu.SemaphoreType.DMA((4 * n,)),
                        pltpu.SemaphoreType.DMA((7,)), pltpu.SemaphoreType.DMA((7,)),
                        pltpu.SemaphoreType.DMA((n + 1,))],
    )(*grads, small)
    return res[:n], res[n:2 * n], res[2 * n]


def _ada_call(c, w_ada_bf, b_ada):
    def body(c_ref, w_ref, b_ref, o_ref):
        cc = c_ref[...]
        sc = jnp.broadcast_to(cc * _sigmoid(cc), (8, D_MODEL)).astype(BF16)
        o_ref[...] = _dot(sc, w_ref[...]) + b_ref[...]

    out = pl.pallas_call(
        body, name="ada_fwd",
        out_shape=jax.ShapeDtypeStruct((8, 3 * D_MODEL), F32),
        compiler_params=pltpu.CompilerParams(vmem_limit_bytes=VMEM_LIMIT),
    )(c, w_ada_bf, b_ada)
    return out[0:1]


def _full(shape):
    return pl.BlockSpec(shape, lambda i: (0,) * len(shape))


def _rows(tm, width):
    return pl.BlockSpec((tm, width), lambda i: (i, 0))


def _pre_call(x, shift, scale, norm_w, w_in_bf):
    s = x.shape[0]
    tm = _row_tile(s, 512)
    groups = [(C_Q, 512, BF16), (C_K, 512, BF16), (C_V, 512, BF16), (C_G, 512, F32),
              (C_CQ, Q_LORA_RANK, F32), (C_CKV, KV_LORA_RANK, F32), (C_GM, 512, F32), (C_KR, LANES, F32)]

    def body(x_ref, sh_ref, sc_ref, nw_ref, w_ref, hb_ref, *outs):
        xx = x_ref[...]
        r0 = lax.rsqrt(jnp.mean(xx * xx, axis=-1, keepdims=True) + EPS)
        h = (xx * r0 * nw_ref[...]) * (1.0 + sc_ref[...]) + sh_ref[...]
        hb = h.astype(BF16)
        hb_ref[...] = hb
        for (c0, width, dt), o_ref in zip(groups, outs):
            o_ref[...] = _dot(hb, w_ref[:, c0:c0 + width]).astype(dt)

    return pl.pallas_call(
        body, name="pre_proj",
        grid=(s // tm,),
        in_specs=[_rows(tm, D_MODEL), _full((1, D_MODEL)), _full((1, D_MODEL)), _full((1, D_MODEL)),
                  _full((D_MODEL, IN_COLS_PAD))],
        out_specs=[_rows(tm, D_MODEL)] + [_rows(tm, w) for _, w, _ in groups],
        out_shape=[jax.ShapeDtypeStruct((s, D_MODEL), BF16)]
        + [jax.ShapeDtypeStruct((s, w), dt) for _, w, dt in groups],
        compiler_params=_params(("arbitrary",), VMEM_LIMIT),
    )(x, shift, scale, norm_w, w_in_bf)


def _rope(t, cos_t, sin_a, sin_b):
    return t * cos_t + pltpu.roll(t, 112, 1) * sin_a + pltpu.roll(t, 16, 1) * sin_b


def _rope_adjoint(d, cos_t, sin_a, sin_b):
    return d * cos_t + pltpu.roll(d * sin_a, 16, 1) + pltpu.roll(d * sin_b, 112, 1)


def _mla_prep_call(c_q, c_kv, k_rope, cos_t, sin_a, sin_b, q_lora_norm, kv_lora_norm, qhn_pad, khn_pad,
                   w_uq_bf, w_uk_bf, w_uv_bf):
    s = c_q.shape[0]
    tm = _row_tile(s, 256)

    def body(cq_ref, ckv_ref, kr_ref, cos_ref, sa_ref, sb_ref, qln_ref, kvln_ref, qhn_ref, khn_ref,
             wuq_ref, wuk_ref, wuv_ref, q_ref, k_ref, v_ref, cqn_ref, ckvn_ref, q0_ref, k0_ref):
        cq = cq_ref[...]
        cqn = (cq * lax.rsqrt(jnp.mean(cq * cq, axis=-1, keepdims=True) + EPS) * qln_ref[...]).astype(BF16)
        cqn_ref[...] = cqn
        ckv = ckv_ref[...]
        ckvn = (ckv * lax.rsqrt(jnp.mean(ckv * ckv, axis=-1, keepdims=True) + EPS) * kvln_ref[...]).astype(BF16)
        ckvn_ref[...] = ckvn
        v_ref[...] = _dot(ckvn, wuv_ref[...]).astype(BF16)
        cos_t, sin_a, sin_b = cos_ref[...], sa_ref[...], sb_ref[...]
        kr = kr_ref[...]
        for h in range(MLA_HEADS):
            cols = slice(h * HEAD_PAD, (h + 1) * HEAD_PAD)
            q0 = _dot(cqn, wuq_ref[:, cols])
            q0_ref[:, cols] = q0
            rq = lax.rsqrt(jnp.sum(q0 * q0, axis=-1, keepdims=True) * (1.0 / MLA_QK_DIM) + EPS)
            q_ref[:, cols] = (_rope(q0 * rq * qhn_ref[...], cos_t, sin_a, sin_b) * MLA_SCALE_LOG2).astype(BF16)
            k0 = _dot(ckvn, wuk_ref[:, cols]) + kr
            k0_ref[:, cols] = k0
            rk = lax.rsqrt(jnp.sum(k0 * k0, axis=-1, keepdims=True) * (1.0 / MLA_QK_DIM) + EPS)
            k_ref[:, cols] = _rope(k0 * rk * khn_ref[...], cos_t, sin_a, sin_b).astype(BF16)

    return pl.pallas_call(
        body, name="mla_prep",
        grid=(s // tm,),
        in_specs=[_rows(tm, Q_LORA_RANK), _rows(tm, KV_LORA_RANK), _rows(tm, LANES),
                  _rows(tm, LANES), _rows(tm, LANES), _rows(tm, LANES),
                  _full((1, Q_LORA_RANK)), _full((1, KV_LORA_RANK)), _full((1, LANES)), _full((1, LANES)),
                  _full((Q_LORA_RANK, MLA_PAD_WIDTH)), _full((KV_LORA_RANK, MLA_PAD_WIDTH)),
                  _full((KV_LORA_RANK, MLA_WIDTH))],
        out_specs=[_rows(tm, MLA_PAD_WIDTH), _rows(tm, MLA_PAD_WIDTH), _rows(tm, MLA_WIDTH),
                   _rows(tm, Q_LORA_RANK), _rows(tm, KV_LORA_RANK),
                   _rows(tm, MLA_PAD_WIDTH), _rows(tm, MLA_PAD_WIDTH)],
        out_shape=[jax.ShapeDtypeStruct((s, MLA_PAD_WIDTH), BF16), jax.ShapeDtypeStruct((s, MLA_PAD_WIDTH), BF16),
                   jax.ShapeDtypeStruct((s, MLA_WIDTH), BF16),
                   jax.ShapeDtypeStruct((s, Q_LORA_RANK), BF16), jax.ShapeDtypeStruct((s, KV_LORA_RANK), BF16),
                   jax.ShapeDtypeStruct((s, MLA_PAD_WIDTH), F32), jax.ShapeDtypeStruct((s, MLA_PAD_WIDTH), F32)],
        compiler_params=_params(("arbitrary",), VMEM_LIMIT),
    )(c_q, c_kv, k_rope, cos_t, sin_a, sin_b, q_lora_norm, kv_lora_norm, qhn_pad, khn_pad,
      w_uq_bf, w_uk_bf, w_uv_bf)


def _log_sigmoid_pair(z):
    ls = jnp.minimum(z, 0.0) - jnp.log1p(jnp.exp(-jnp.abs(z)))
    return ls, ls - z


def _sb_fwd_call(q, k, v):
    s = q.shape[0]
    bq = _row_tile(s, 256)
    nq = s // bq

    def body(q_ref, k_ref, v_ref, o_ref, r_ref, ks_ref):
        hp, i = pl.program_id(0), pl.program_id(1)
        lane = lax.broadcasted_iota(I32, (bq, LANES), 1)
        row = lax.broadcasted_iota(I32, (bq, bq), 0)
        col = lax.broadcasted_iota(I32, (bq, bq), 1)
        strict = col < row
        later = jnp.where(row > col, 1.0, 0.0).astype(BF16)
        qs = q_ref[...] * jnp.asarray(SB_SCALE, BF16)
        accs, finals = [], []
        for hh in range(2):
            qm = qs * _head_mask(lane, hh).astype(BF16)

            def tile(kb, run, acc, masked, qm=qm):
                rows = pl.ds(pl.multiple_of(kb * bq, bq), bq)
                z = _dot_nt(qm, k_ref[rows, :])
                ls, lk = _log_sigmoid_pair(z)
                if masked:
                    lk = jnp.where(strict, lk, 0.0)
                w = jnp.exp(ls + _split_dot(lk, later) + run)
                if masked:
                    w = jnp.where(strict, w, 0.0)
                acc = acc + _dot(w.astype(BF16), v_ref[rows, :])
                return run + jnp.sum(lk, axis=1, keepdims=True), acc

            run, acc = tile(i, jnp.zeros((bq, 1), F32), jnp.zeros((bq, LANES), F32), True)

            def cond(carry):
                kb, alive, _, _ = carry
                return jnp.logical_and(kb >= 0, alive > 0)

            def step(carry, tile=tile):
                kb, _, run, acc = carry
                run, acc = tile(kb, run, acc, False)
                return kb - 1, (jnp.max(run) > SB_DEAD).astype(I32), run, acc

            kb, _, run, acc = lax.while_loop(
                cond, step, (i - 1, (jnp.max(run) > SB_DEAD).astype(I32), run, acc))
            ks_ref[2 * hp + hh, i] = kb + 1
            accs.append(acc)
            finals.append(run)
        o_ref[...] = jnp.where(lane < 64, accs[0], accs[1])
        r_ref[...] = jnp.where(lane < 64, finals[0], finals[1])

    return pl.pallas_call(
        body, name="sb_fwd",
        grid=(4, nq),
        in_specs=[pl.BlockSpec((bq, LANES), lambda h, i: (i, h)),
                  pl.BlockSpec((s, LANES), lambda h, i: (0, h)),
                  pl.BlockSpec((s, LANES), lambda h, i: (0, h))],
        out_specs=[pl.BlockSpec((bq, LANES), lambda h, i: (i, h)),
                   pl.BlockSpec((bq, LANES), lambda h, i: (i, h)),
                   pl.BlockSpec(memory_space=pltpu.SMEM)],
        out_shape=[jax.ShapeDtypeStruct((s, SB_WIDTH), F32), jax.ShapeDtypeStruct((s, SB_WIDTH), F32),
                   jax.ShapeDtypeStruct((SB_HEADS, nq), I32)],
        compiler_params=_params(("arbitrary", "arbitrary"), VMEM_LIMIT),
    )(q, k, v)


def _mla_fwd_call(q, k, v):
    s = q.shape[0]
    bq = _row_tile(s, MLA_BQ)
    bk = _row_tile(s, MLA_BK)
    nq = s // bq
    assert bk % bq == 0
    half = bk // 2

    def body(q_ref, k_ref, v_ref, o_ref, lse_ref, p_ref, s_ref):
        i = pl.program_id(1)
        lane = lax.broadcasted_iota(I32, (bq, LANES), 1)
        row = lax.broadcasted_iota(I32, (bq, half), 0)
        col = lax.broadcasted_iota(I32, (bq, half), 1)
        n_full = (i * bq) // bk

        def keys(g):
            return pl.ds(pl.multiple_of(g * half, half), half)

        def put_scores(g, slot):
            for hh in range(2):
                cols = slice(hh * HEAD_PAD, (hh + 1) * HEAD_PAD)
                s_ref[slot, hh] = _dot_nt(q_ref[:, cols], k_ref[keys(g), cols])

        def add_pv(carry, g, slot):
            vblk = v_ref[keys(g), :]
            return tuple((m, l, alpha * acc + _dot(p_ref[slot, hh], vblk), alpha)
                         for hh, (m, l, acc, alpha) in enumerate(carry))

        def substep(g, slot, carry, masked, prefetch):
            if prefetch:
                put_scores(g + 1, 1 - slot)
            carry = add_pv(carry, jnp.maximum(g - 1, 0), 1 - slot)
            new = []
            for hh in range(2):
                m, l, acc, _ = carry[hh]
                sc = s_ref[slot, hh]
                if masked:
                    sc = jnp.where(col + g * half <= row + i * bq, sc, MASK_NEG)
                m_new = jnp.maximum(m, jnp.max(sc, axis=1, keepdims=True))
                p = jnp.exp2(sc - m_new)
                alpha = jnp.exp2(m - m_new)
                l = alpha * l + jnp.sum(p, axis=1, keepdims=True)
                p_ref[slot, hh] = p.astype(BF16)
                new.append((m_new, l, acc, alpha))
            return tuple(new)

        def chunk(kb, carry, masked):
            carry = substep(2 * kb, 0, carry, masked, True)
            return substep(2 * kb + 1, 1, carry, masked, not masked)

        p_ref[1] = jnp.zeros_like(p_ref[1])
        put_scores(0, 0)
        one = (jnp.full((bq, 1), MASK_NEG, F32), jnp.zeros((bq, 1), F32), jnp.zeros((bq, LANES), F32),
               jnp.ones((bq, 1), F32))
        carry = lax.fori_loop(0, n_full, lambda kb, cr: chunk(kb, cr, False), (one, one))
        carry = chunk(n_full, carry, True)
        (m0, l0, a0, _), (m1, l1, a1, _) = add_pv(carry, 2 * n_full + 1, 1)
        o_ref[...] = jnp.where(lane < 64, a0 / l0, a1 / l1)
        lse_ref[...] = jnp.where(lane < 64, m0 + jnp.log2(l0), m1 + jnp.log2(l1))

    return pl.pallas_call(
        body, name="mla_fwd",
        grid=(4, nq),
        in_specs=[pl.BlockSpec((bq, 2 * HEAD_PAD), lambda h, i: (i, h)),
                  pl.BlockSpec((s, 2 * HEAD_PAD), lambda h, i: (0, h)),
                  pl.BlockSpec((s, LANES), lambda h, i: (0, h))],
        out_specs=[pl.BlockSpec((bq, LANES), lambda h, i: (i, h)),
                   pl.BlockSpec((bq, LANES), lambda h, i: (i, h))],
        out_shape=[jax.ShapeDtypeStruct((s, MLA_WIDTH), F32), jax.ShapeDtypeStruct((s, MLA_WIDTH), F32)],
        scratch_shapes=[pltpu.VMEM((2, 2, bq, half), BF16), pltpu.VMEM((2, 2, bq, half), F32)],
        compiler_params=_params(("arbitrary", "arbitrary"), VMEM_LIMIT),
    )(q, k, v)


def _out_call(o_sb, g_sb, o_mla, g_mla, x, target, gate, w_out_bf):
    s = x.shape[0]
    tm = _row_tile(s, 256)

    def body(osb_ref, gsb_ref, oml_ref, gml_ref, x_ref, t_ref, gate_ref, w_ref,
             dosb_ref, doml_ref, dgsb_ref, dgml_ref, dy_ref, gw_ref, dgate_ref, sq_ref):
        @pl.when(pl.program_id(0) == 0)
        def _():
            gw_ref[...] = jnp.zeros_like(gw_ref)
            dgate_ref[...] = jnp.zeros_like(dgate_ref)
            sq_ref[...] = jnp.zeros_like(sq_ref)

        g_s, g_m = gsb_ref[...], gml_ref[...]
        sig_s, sig_m = _sigmoid(g_s), _sigmoid(g_m)
        silu_s, silu_m = g_s * sig_s, g_m * sig_m
        o_s, o_m = osb_ref[...], oml_ref[...]
        mixed = jnp.concatenate([o_s * silu_s, o_m * silu_m], axis=1).astype(BF16)
        u = _dot(mixed, w_ref[...])
        gate_v = gate_ref[...]
        err = x_ref[...] + gate_v * u - t_ref[...]
        sq_ref[...] += jnp.sum(err * err, axis=0, keepdims=True)
        dy = err * (1.0 / D_MODEL)
        dy_ref[...] = dy
        dgate_ref[...] += jnp.sum(dy * u, axis=0, keepdims=True)
        du = (dy * gate_v).astype(BF16)
        gw_ref[...] += _dot_tn(mixed, du)
        dmix = _dot_nt(du, w_ref[...])
        dm_s, dm_m = dmix[:, :SB_WIDTH], dmix[:, SB_WIDTH:]
        dosb_ref[...] = (dm_s * silu_s).astype(BF16)
        doml_ref[...] = (dm_m * silu_m).astype(BF16)
        dgsb_ref[...] = (dm_s * o_s * (sig_s * (1.0 + g_s * (1.0 - sig_s)))).astype(BF16)
        dgml_ref[...] = (dm_m * o_m * (sig_m * (1.0 + g_m * (1.0 - sig_m)))).astype(BF16)

    return pl.pallas_call(
        body, name="out_proj_loss",
        grid=(s // tm,),
        in_specs=[_rows(tm, 512), _rows(tm, 512), _rows(tm, 512), _rows(tm, 512),
                  _rows(tm, D_MODEL), _rows(tm, D_MODEL), _full((1, D_MODEL)), _full((D_MODEL, D_MODEL))],
        out_specs=[_rows(tm, 512), _rows(tm, 512), _rows(tm, 512), _rows(tm, 512), _rows(tm, D_MODEL),
                   _full((D_MODEL, D_MODEL)), _full((1, D_MODEL)), _full((1, D_MODEL))],
        out_shape=[jax.ShapeDtypeStruct((s, 512), BF16)] * 4
        + [jax.ShapeDtypeStruct((s, D_MODEL), F32), jax.ShapeDtypeStruct((D_MODEL, D_MODEL), F32),
           jax.ShapeDtypeStruct((1, D_MODEL), F32), jax.ShapeDtypeStruct((1, D_MODEL), F32)],
        compiler_params=_params(("arbitrary",), VMEM_LIMIT),
    )(o_sb, g_sb, o_mla, g_mla, x, target, gate, w_out_bf)


def _head_mask(lane, hh):
    return jnp.where((lane >= 64) if hh else (lane < 64), 1.0, 0.0)


def _pick_lane(packed, lane, which):
    return jnp.sum(jnp.where(lane == which, packed, 0.0), axis=1, keepdims=True)


def _sb_bwd_call(kstart, q, k, v, do, rfin):
    s = q.shape[0]
    bq = _row_tile(s, 256)
    nq = s // bq

    def body(ks_ref, q_ref, k_ref, v_ref, do_ref, r_ref, dq_ref, dk_ref, dv_ref):
        hp, i = pl.program_id(0), pl.program_id(1)

        @pl.when(i == 0)
        def _():
            dk_ref[...] = jnp.zeros_like(dk_ref)
            dv_ref[...] = jnp.zeros_like(dv_ref)

        lane = lax.broadcasted_iota(I32, (bq, LANES), 1)
        row = lax.broadcasted_iota(I32, (bq, bq), 0)
        col = lax.broadcasted_iota(I32, (bq, bq), 1)
        strict = col < row
        upto = jnp.where(row <= col, 1.0, 0.0).astype(BF16)
        before = jnp.where(row < col, 1.0, 0.0).astype(BF16)
        qs = q_ref[...] * jnp.asarray(SB_SCALE, BF16)
        do2 = do_ref[...]
        r2 = r_ref[...]
        dqs = []
        for hh in range(2):
            head = _head_mask(lane, hh).astype(BF16)
            qm = qs * head
            dom = do2 * head
            total = _pick_lane(r2, lane, 64 * hh)

            def tile(kb, carry, masked, qm=qm, dom=dom, total=total):
                pre, gpre, dq = carry
                rows = pl.ds(pl.multiple_of(kb * bq, bq), bq)
                kblk, vblk = k_ref[rows, :], v_ref[rows, :]
                z = _dot_nt(qm, kblk)
                ls, lk = _log_sigmoid_pair(z)
                if masked:
                    lk = jnp.where(strict, lk, 0.0)
                w = jnp.exp(ls + ((total - pre) - _split_dot(lk, upto)))
                if masked:
                    w = jnp.where(strict, w, 0.0)
                g = w * _dot_nt(dom, vblk)
                gsum = gpre + _split_dot(g, before)
                dz = g - jnp.exp(ls) * (g + gsum)
                if masked:
                    dz = jnp.where(strict, dz, 0.0)
                dzb = dz.astype(BF16)
                dk_ref[rows, :] += _dot_tn(dzb, qm)
                dv_ref[rows, :] += _dot_tn(w.astype(BF16), dom)
                return (pre + jnp.sum(lk, axis=1, keepdims=True), gpre + jnp.sum(g, axis=1, keepdims=True),
                        dq + _dot(dzb, kblk))

            init = (jnp.zeros((bq, 1), F32), jnp.zeros((bq, 1), F32), jnp.zeros((bq, LANES), F32))
            carry = lax.fori_loop(ks_ref[2 * hp + hh, i], i, lambda kb, cr, tile=tile: tile(kb, cr, False), init)
            dqs.append(tile(i, carry, True)[2])
        dq_ref[...] = (jnp.where(lane < 64, dqs[0], dqs[1]) * SB_SCALE).astype(BF16)

    return pl.pallas_call(
        body, name="sb_bwd",
        grid_spec=pltpu.PrefetchScalarGridSpec(
            num_scalar_prefetch=1, grid=(4, nq),
            in_specs=[pl.BlockSpec((bq, LANES), lambda h, i, ks: (i, h)),
                      pl.BlockSpec((s, LANES), lambda h, i, ks: (0, h)),
                      pl.BlockSpec((s, LANES), lambda h, i, ks: (0, h)),
                      pl.BlockSpec((bq, LANES), lambda h, i, ks: (i, h)),
                      pl.BlockSpec((bq, LANES), lambda h, i, ks: (i, h))],
            out_specs=[pl.BlockSpec((bq, LANES), lambda h, i, ks: (i, h)),
                       pl.BlockSpec((s, LANES), lambda h, i, ks: (0, h)),
                       pl.BlockSpec((s, LANES), lambda h, i, ks: (0, h))]),
        out_shape=[jax.ShapeDtypeStruct((s, SB_WIDTH), BF16), jax.ShapeDtypeStruct((s, SB_WIDTH), F32),
                   jax.ShapeDtypeStruct((s, SB_WIDTH), F32)],
        compiler_params=_params(("arbitrary", "arbitrary"), VMEM_LIMIT),
    )(kstart, q, k, v, do, rfin)


def _mla_bwd_call(q, k, v, do, o, lse):
    s = q.shape[0]
    bq = _row_tile(s, MLA_BQ)
    bk = _row_tile(s, MLA_BWD_BK)
    nq = s // bq
    assert bk % bq == 0
    half = bk // 2

    def body(q_ref, k_ref, v_ref, do_ref, o_ref, lse_ref, dq_ref, dk_ref, dv_ref, dom_ref, s_ref, dp_ref, pb_ref,
             ds_ref):
        i = pl.program_id(1)

        @pl.when(i == 0)
        def _():
            dk_ref[...] = jnp.zeros_like(dk_ref)
            dv_ref[...] = jnp.zeros_like(dv_ref)

        lane = lax.broadcasted_iota(I32, (bq, LANES), 1)
        row = lax.broadcasted_iota(I32, (bq, half), 0)
        col = lax.broadcasted_iota(I32, (bq, half), 1)
        n_full = (i * bq) // bk
        do2 = do_ref[...]
        prod = do2.astype(F32) * o_ref[...]
        lse2 = lse_ref[...]
        deltas, lses = [], []
        for hh in range(2):
            head = _head_mask(lane, hh)
            dom_ref[hh] = do2 * head.astype(BF16)
            deltas.append(jnp.sum(prod * head, axis=1, keepdims=True))
            lses.append(_pick_lane(lse2, lane, 64 * hh))

        def keys(g):
            return pl.ds(pl.multiple_of(g * half, half), half)

        def heads():
            return [(hh, slice(hh * HEAD_PAD, (hh + 1) * HEAD_PAD)) for hh in range(2)]

        def put_products(g, slot):
            vblk = v_ref[keys(g), :]
            for hh, cols in heads():
                s_ref[slot, hh] = _dot_nt(q_ref[:, cols], k_ref[keys(g), cols])
                dp_ref[slot, hh] = _dot_nt(dom_ref[hh], vblk)

        def add_grads(dqs, g, slot):
            rows = keys(g)
            new, dv_parts = [], []
            for hh, cols in heads():
                ds = ds_ref[slot, hh]
                dk_ref[rows, cols] += _dot_tn(ds, q_ref[:, cols])
                dv_parts.append(_dot_tn(pb_ref[slot, hh], dom_ref[hh]))
                new.append(dqs[hh] + _dot(ds, k_ref[rows, cols]))
            dv_ref[rows, :] += dv_parts[0] + dv_parts[1]
            return tuple(new)

        def substep(g, slot, dqs, masked, prefetch):
            if prefetch:
                put_products(g + 1, 1 - slot)
            dqs = add_grads(dqs, jnp.maximum(g - 1, 0), 1 - slot)
            for hh, _ in heads():
                p = jnp.exp2(s_ref[slot, hh] - lses[hh])
                if masked:
                    p = jnp.where(col + g * half <= row + i * bq, p, 0.0)
                ds_ref[slot, hh] = (p * (dp_ref[slot, hh] - deltas[hh])).astype(BF16)
                pb_ref[slot, hh] = p.astype(BF16)
            return dqs

        def chunk(kb, dqs, masked):
            dqs = substep(2 * kb, 0, dqs, masked, True)
            return substep(2 * kb + 1, 1, dqs, masked, not masked)

        ds_ref[1] = jnp.zeros_like(ds_ref[1])
        pb_ref[1] = jnp.zeros_like(pb_ref[1])
        put_products(0, 0)
        zero = jnp.zeros((bq, HEAD_PAD), F32)
        dqs = lax.fori_loop(0, n_full, lambda kb, dqs: chunk(kb, dqs, False), (zero, zero))
        dqs = chunk(n_full, dqs, True)
        dqs = add_grads(dqs, 2 * n_full + 1, 1)
        dq_ref[:, :HEAD_PAD] = dqs[0] * MLA_SCALE
        dq_ref[:, HEAD_PAD:] = dqs[1] * MLA_SCALE

    return pl.pallas_call(
        body, name="mla_bwd",
        grid=(4, nq),
        in_specs=[pl.BlockSpec((bq, 2 * HEAD_PAD), lambda h, i: (i, h)),
                  pl.BlockSpec((s, 2 * HEAD_PAD), lambda h, i: (0, h)),
                  pl.BlockSpec((s, LANES), lambda h, i: (0, h)),
                  pl.BlockSpec((bq, LANES), lambda h, i: (i, h)),
                  pl.BlockSpec((bq, LANES), lambda h, i: (i, h)),
                  pl.BlockSpec((bq, LANES), lambda h, i: (i, h))],
        out_specs=[pl.BlockSpec((bq, 2 * HEAD_PAD), lambda h, i: (i, h)),
                   pl.BlockSpec((s, 2 * HEAD_PAD), lambda h, i: (0, h)),
                   pl.BlockSpec((s, LANES), lambda h, i: (0, h))],
        out_shape=[jax.ShapeDtypeStruct((s, MLA_PAD_WIDTH), F32), jax.ShapeDtypeStruct((s, MLA_PAD_WIDTH), F32),
                   jax.ShapeDtypeStruct((s, MLA_WIDTH), F32)],
        scratch_shapes=[pltpu.VMEM((2, bq, LANES), BF16),
                        pltpu.VMEM((2, 2, bq, half), F32), pltpu.VMEM((2, 2, bq, half), F32),
                        pltpu.VMEM((2, 2, bq, half), BF16), pltpu.VMEM((2, 2, bq, half), BF16)],
        compiler_params=_params(("arbitrary", "arbitrary"), VMEM_LIMIT),
    )(q, k, v, do, o, lse)


def _rms_bwd(d_out, inp, r, weight, n):
    normed = inp * r
    gw = d_out * weight
    d_in = r * (gw - normed * (jnp.sum(gw * normed, axis=-1, keepdims=True) * (1.0 / n)))
    return d_in, d_out * normed


def _mla_prep_bwd_call(dq, dk, dv, q0, k0, cqn, ckvn, c_q, c_kv, cos_t, sin_a, sin_b,
                       q_lora_norm, kv_lora_norm, qhn_pad, khn_pad, w_uq_bf, w_uk_bf, w_uv_bf):
    s = dq.shape[0]
    tm = _row_tile(s, 256)

    def body(dq_ref, dk_ref, dv_ref, q0_ref, k0_ref, cqn_ref, ckvn_ref, cq_ref, ckv_ref,
             cos_ref, sa_ref, sb_ref, qln_ref, kvln_ref, qhn_ref, khn_ref, wuq_ref, wuk_ref, wuv_ref,
             dcq_ref, dckv_ref, dkr_ref, gwuq_ref, gwuk_ref, gwuv_ref, gqln_ref, gkvln_ref, gqhn_ref, gkhn_ref):
        @pl.when(pl.program_id(0) == 0)
        def _():
            for ref in (gwuq_ref, gwuk_ref, gwuv_ref, gqln_ref, gkvln_ref, gqhn_ref, gkhn_ref):
                ref[...] = jnp.zeros_like(ref)

        cos_t, sin_a, sin_b = cos_ref[...], sa_ref[...], sb_ref[...]
        lane = lax.broadcasted_iota(I32, (tm, LANES), 1)
        rope_lanes = jnp.logical_and(lane >= KR_LANE, lane < KR_LANE + MLA_ROPE_DIM)
        cqn, ckvn = cqn_ref[...], ckvn_ref[...]
        d_cqn = jnp.zeros((tm, Q_LORA_RANK), F32)
        d_ckvn = jnp.zeros((tm, KV_LORA_RANK), F32)
        d_kr = jnp.zeros((tm, LANES), F32)
        g_qhn = jnp.zeros((1, LANES), F32)
        g_khn = jnp.zeros((1, LANES), F32)
        for h in range(MLA_HEADS):
            cols = slice(h * HEAD_PAD, (h + 1) * HEAD_PAD)
            q0 = q0_ref[:, cols]
            rq = lax.rsqrt(jnp.sum(q0 * q0, axis=-1, keepdims=True) * (1.0 / MLA_QK_DIM) + EPS)
            d_q0, gq = _rms_bwd(_rope_adjoint(dq_ref[:, cols], cos_t, sin_a, sin_b), q0, rq, qhn_ref[...],
                                MLA_QK_DIM)
            g_qhn += jnp.sum(gq, axis=0, keepdims=True)
            d_q0b = d_q0.astype(BF16)
            d_cqn += _dot_nt(d_q0b, wuq_ref[:, cols])
            gwuq_ref[:, cols] += _dot_tn(cqn, d_q0b)
            k0 = k0_ref[:, cols]
            rk = lax.rsqrt(jnp.sum(k0 * k0, axis=-1, keepdims=True) * (1.0 / MLA_QK_DIM) + EPS)
            d_k0, gk = _rms_bwd(_rope_adjoint(dk_ref[:, cols] * LN2, cos_t, sin_a, sin_b), k0, rk, khn_ref[...],
                                MLA_QK_DIM)
            g_khn += jnp.sum(gk, axis=0, keepdims=True)
            d_kr += jnp.where(rope_lanes, d_k0, 0.0)
            d_k0b = d_k0.astype(BF16)
            d_ckvn += _dot_nt(d_k0b, wuk_ref[:, cols])
            gwuk_ref[:, cols] += _dot_tn(ckvn, d_k0b)
        dvb = dv_ref[...].astype(BF16)
        d_ckvn += _dot_nt(dvb, wuv_ref[...])
        gwuv_ref[...] += _dot_tn(ckvn, dvb)
        gqhn_ref[...] += g_qhn
        gkhn_ref[...] += g_khn
        dkr_ref[...] = d_kr.astype(BF16)
        cq = cq_ref[...]
        rcq = lax.rsqrt(jnp.mean(cq * cq, axis=-1, keepdims=True) + EPS)
        d_cq, gl = _rms_bwd(d_cqn, cq, rcq, qln_ref[...], Q_LORA_RANK)
        dcq_ref[...] = d_cq.astype(BF16)
        gqln_ref[...] += jnp.sum(gl, axis=0, keepdims=True)
        ckv = ckv_ref[...]
        rckv = lax.rsqrt(jnp.mean(ckv * ckv, axis=-1, keepdims=True) + EPS)
        d_ckv, gl = _rms_bwd(d_ckvn, ckv, rckv, kvln_ref[...], KV_LORA_RANK)
        dckv_ref[...] = d_ckv.astype(BF16)
        gkvln_ref[...] += jnp.sum(gl, axis=0, keepdims=True)

    return pl.pallas_call(
        body, name="mla_prep_bwd",
        grid=(s // tm,),
        in_specs=[_rows(tm, MLA_PAD_WIDTH), _rows(tm, MLA_PAD_WIDTH), _rows(tm, MLA_WIDTH),
                  _rows(tm, MLA_PAD_WIDTH), _rows(tm, MLA_PAD_WIDTH),
                  _rows(tm, Q_LORA_RANK), _rows(tm, KV_LORA_RANK), _rows(tm, Q_LORA_RANK), _rows(tm, KV_LORA_RANK),
                  _rows(tm, LANES), _rows(tm, LANES), _rows(tm, LANES),
                  _full((1, Q_LORA_RANK)), _full((1, KV_LORA_RANK)), _full((1, LANES)), _full((1, LANES)),
                  _full((Q_LORA_RANK, MLA_PAD_WIDTH)), _full((KV_LORA_RANK, MLA_PAD_WIDTH)),
                  _full((KV_LORA_RANK, MLA_WIDTH))],
        out_specs=[_rows(tm, Q_LORA_RANK), _rows(tm, KV_LORA_RANK), _rows(tm, LANES),
                   _full((Q_LORA_RANK, MLA_PAD_WIDTH)), _full((KV_LORA_RANK, MLA_PAD_WIDTH)),
                   _full((KV_LORA_RANK, MLA_WIDTH)),
                   _full((1, Q_LORA_RANK)), _full((1, KV_LORA_RANK)), _full((1, LANES)), _full((1, LANES))],
        out_shape=[jax.ShapeDtypeStruct((s, Q_LORA_RANK), BF16), jax.ShapeDtypeStruct((s, KV_LORA_RANK), BF16),
                   jax.ShapeDtypeStruct((s, LANES), BF16),
                   jax.ShapeDtypeStruct((Q_LORA_RANK, MLA_PAD_WIDTH), F32),
                   jax.ShapeDtypeStruct((KV_LORA_RANK, MLA_PAD_WIDTH), F32),
                   jax.ShapeDtypeStruct((KV_LORA_RANK, MLA_WIDTH), F32),
                   jax.ShapeDtypeStruct((1, Q_LORA_RANK), F32), jax.ShapeDtypeStruct((1, KV_LORA_RANK), F32),
                   jax.ShapeDtypeStruct((1, LANES), F32), jax.ShapeDtypeStruct((1, LANES), F32)],
        compiler_params=_params(("arbitrary",), VMEM_LIMIT),
    )(dq, dk, dv, q0, k0, cqn, ckvn, c_q, c_kv, cos_t, sin_a, sin_b,
      q_lora_norm, kv_lora_norm, qhn_pad, khn_pad, w_uq_bf, w_uk_bf, w_uv_bf)


def _dh_call(pieces, x, dy, shift, scale, norm_w, w_in_bf):
    s = x.shape[0]
    tm = _row_tile(s, 256)
    widths = [p.shape[1] for p in pieces]
    offsets = [sum(widths[:j]) for j in range(len(widths))]
    assert offsets[-1] + widths[-1] == IN_COLS_PAD
    n = len(pieces)

    def body(*refs):
        p_refs = refs[:n]
        x_ref, dy_ref, sh_ref, sc_ref, nw_ref, w_ref, gx_ref, dp_ref, dsh_ref, dsc_ref, gnw_ref = refs[n:]

        @pl.when(pl.program_id(0) == 0)
        def _():
            dsh_ref[...] = jnp.zeros_like(dsh_ref)
            dsc_ref[...] = jnp.zeros_like(dsc_ref)
            gnw_ref[...] = jnp.zeros_like(gnw_ref)

        for p_ref, c0, width in zip(p_refs, offsets, widths):
            dp_ref[:, c0:c0 + width] = p_ref[...].astype(BF16)
        dh = _dot_nt(dp_ref[...], w_ref[...])
        xx = x_ref[...]
        r0 = lax.rsqrt(jnp.mean(xx * xx, axis=-1, keepdims=True) + EPS)
        xn = xx * r0
        nw = nw_ref[...]
        dsh_ref[...] += jnp.sum(dh, axis=0, keepdims=True)
        dsc_ref[...] += jnp.sum(dh * (xn * nw), axis=0, keepdims=True)
        dn = dh * (1.0 + sc_ref[...])
        gnw_ref[...] += jnp.sum(dn * xn, axis=0, keepdims=True)
        dxn = dn * nw
        gx_ref[...] = dy_ref[...] + r0 * (dxn - xn * jnp.mean(dxn * xn, axis=-1, keepdims=True))

    return pl.pallas_call(
        body, name="in_proj_bwd",
        grid=(s // tm,),
        in_specs=[_rows(tm, w) for w in widths]
        + [_rows(tm, D_MODEL), _rows(tm, D_MODEL), _full((1, D_MODEL)), _full((1, D_MODEL)), _full((1, D_MODEL)),
           _full((D_MODEL, IN_COLS_PAD))],
        out_specs=[_rows(tm, D_MODEL), _rows(tm, IN_COLS_PAD),
                   _full((1, D_MODEL)), _full((1, D_MODEL)), _full((1, D_MODEL))],
        out_shape=[jax.ShapeDtypeStruct((s, D_MODEL), F32), jax.ShapeDtypeStruct((s, IN_COLS_PAD), BF16),
                   jax.ShapeDtypeStruct((1, D_MODEL), F32), jax.ShapeDtypeStruct((1, D_MODEL), F32),
                   jax.ShapeDtypeStruct((1, D_MODEL), F32)],
        compiler_params=_params(("arbitrary",), VMEM_LIMIT),
    )(*pieces, x, dy, shift, scale, norm_w, w_in_bf)


def _gw_in_call(hb, dproj):
    s = hb.shape[0]
    tk = _row_tile(s, 512)
    tn = IN_COLS_PAD // 2

    def body(h_ref, d_ref, g_ref):
        @pl.when(pl.program_id(1) == 0)
        def _():
            g_ref[...] = jnp.zeros_like(g_ref)

        g_ref[...] += _dot_tn(h_ref[...], d_ref[...])

    return pl.pallas_call(
        body, name="in_proj_wgrad",
        grid=(2, s // tk),
        in_specs=[pl.BlockSpec((tk, D_MODEL), lambda j, t: (t, 0)), pl.BlockSpec((tk, tn), lambda j, t: (t, j))],
        out_specs=pl.BlockSpec((D_MODEL, tn), lambda j, t: (0, j)),
        out_shape=jax.ShapeDtypeStruct((D_MODEL, IN_COLS_PAD), F32),
        compiler_params=_params(("arbitrary", "arbitrary"), VMEM_LIMIT),
    )(hb, dproj)


def _adamw(g, w, m, v):
    m = ADAM_B1 * m + (1.0 - ADAM_B1) * g
    v = ADAM_B2 * v + (1.0 - ADAM_B2) * (g * g)
    m_hat = m / (1.0 - ADAM_B1 ** ADAM_STEP)
    v_hat = v / (1.0 - ADAM_B2 ** ADAM_STEP)
    delta = -ADAM_LR * (m_hat / (jnp.sqrt(v_hat) + ADAM_EPS) + ADAM_WD * w)
    return delta, m, v


def _adam_shard_call(name, own, sib, w, m, v):
    r, c = w.shape
    tr = r if r <= 512 else 256

    def body(own_ref, sib_ref, w_ref, m_ref, v_ref, g_ref, d_ref, nm_ref, nv_ref):
        a = ((own_ref[0] + own_ref[1]) + own_ref[2]) + own_ref[3]
        b = ((sib_ref[0] + sib_ref[1]) + sib_ref[2]) + sib_ref[3]
        g = a + b
        g_ref[...] = g
        d_ref[...], nm_ref[...], nv_ref[...] = _adamw(g, w_ref[...], m_ref[...], v_ref[...])

    part = pl.BlockSpec((4, tr, c), lambda i: (0, i, 0))
    blk = pl.BlockSpec((tr, c), lambda i: (i, 0))
    return pl.pallas_call(
        body, name=name,
        grid=(r // tr,),
        in_specs=[part, part, blk, blk, blk],
        out_specs=[blk] * 4,
        out_shape=[jax.ShapeDtypeStruct((r, c), F32)] * 4,
        compiler_params=_params(("arbitrary",), VMEM_LIMIT),
    )(own, sib, w, m, v)


def _adam_ada_call(c_all, d_all, w, m, v):
    r, c = w.shape
    tr = 256

    def body(c_ref, d_ref, w_ref, m_ref, v_ref, g_ref, dl_ref, nm_ref, nv_ref):
        cc = c_ref[...]
        sc = cc * _sigmoid(cc)
        dd = d_ref[...]
        sc_hi = sc.astype(BF16)
        sc_lo = (sc - sc_hi.astype(F32)).astype(BF16)
        dd_hi = dd.astype(BF16)
        dd_lo = (dd - dd_hi.astype(F32)).astype(BF16)
        g = _dot_tn(sc_hi, dd_hi) + (_dot_tn(sc_hi, dd_lo) + _dot_tn(sc_lo, dd_hi))
        g_ref[...] = g
        dl_ref[...], nm_ref[...], nv_ref[...] = _adamw(g, w_ref[...], m_ref[...], v_ref[...])

    blk = pl.BlockSpec((tr, c), lambda i: (i, 0))
    return pl.pallas_call(
        body, name="adam_w_ada",
        grid=(r // tr,),
        in_specs=[pl.BlockSpec((16, tr), lambda i: (0, i)), pl.BlockSpec((16, c), lambda i: (0, 0)), blk, blk, blk],
        out_specs=[blk] * 4,
        out_shape=[jax.ShapeDtypeStruct((r, c), F32)] * 4,
        compiler_params=_params(("arbitrary",), VMEM_LIMIT),
    )(c_all, d_all, w, m, v)


def _adam_small_call(packs, w, m, v, first_row):
    rows = w.shape[0]

    def body(p_ref, w_ref, m_ref, v_ref, g_ref, d_ref, nm_ref, nv_ref):
        g = p_ref[0, first_row:first_row + rows, :]
        for b in range(1, 8):
            g = g + p_ref[b, first_row:first_row + rows, :]
        g_ref[...] = g
        d_ref[...], nm_ref[...], nv_ref[...] = _adamw(g, w_ref[...], m_ref[...], v_ref[...])

    return pl.pallas_call(
        body, name="adam_vectors",
        out_shape=[jax.ShapeDtypeStruct((rows, LANES), F32)] * 4,
    )(packs, w, m, v)


def _rope_tables(positions):
    inv_freq = ROPE_THETA ** (-jnp.arange(0, MLA_ROPE_DIM, 2, dtype=F32) / MLA_ROPE_DIM)
    ang = positions.astype(F32)[:, None] * inv_freq
    cos, sin = jnp.cos(ang), jnp.sin(ang)
    s = positions.shape[0]
    half = MLA_ROPE_DIM // 2
    zeros = lambda n: jnp.zeros((s, n), F32)
    cos_t = jnp.concatenate([jnp.ones((s, MLA_NOPE_DIM), F32), cos, cos, zeros(HEAD_PAD - MLA_QK_DIM)], axis=1)
    sin_a = jnp.concatenate([zeros(MLA_NOPE_DIM), -sin, zeros(half + HEAD_PAD - MLA_QK_DIM)], axis=1)
    sin_b = jnp.concatenate([zeros(MLA_NOPE_DIM + half), sin, zeros(HEAD_PAD - MLA_QK_DIM)], axis=1)
    return cos_t, sin_a, sin_b


def _unshard_cols(g):
    return jnp.transpose(g, (1, 0, 2)).reshape(g.shape[1], 4 * g.shape[2])


def _shard_cols(g):
    r, c4 = g.shape
    return jnp.transpose(g.reshape(r, 4, c4 // 4), (1, 0, 2))


def _pad_heads(w, width):
    r = w.shape[0]
    w = w.reshape(r, MLA_HEADS, width)
    return jnp.pad(w, ((0, 0), (0, 0), (0, HEAD_PAD - width))).reshape(r, MLA_PAD_WIDTH)


def _pad_lanes(vec):
    return jnp.pad(vec, ((0, 0), (0, LANES - vec.shape[1])))


def _vector_rows(vecs):
    rows = []
    for vec in vecs:
        n = vec.shape[1]
        pad = (-n) % LANES
        rows.append(jnp.pad(vec, ((0, 0), (0, pad))).reshape((n + pad) // LANES, LANES))
    return jnp.concatenate(rows, axis=0)


def kernel(x, c, positions, w_ada, b_ada, norm_w, w_in, q_lora_norm, w_uq, kv_lora_norm, w_ukv, q_head_norm, k_head_norm, w_out, loss_target, m_w_ada, m_b_ada, m_norm_w, m_w_in, m_q_lora_norm, m_w_uq, m_kv_lora_norm, m_w_ukv, m_q_head_norm, m_k_head_norm, m_w_out, v_w_ada, v_b_ada, v_norm_w, v_w_in, v_q_lora_norm, v_w_uq, v_kv_lora_norm, v_w_ukv, v_q_head_norm, v_k_head_norm, v_w_out):
    wa_g, win_g, wuq_g, wukv_g, wout_g = _gather_call([w.astype(BF16) for w in (w_ada, w_in, w_uq, w_ukv, w_out)])
    (sq_sum, grad_x, g_w_in, g_w_uq, g_w_ukv, g_w_out, d_ada, g_norm_w, g_qln, g_kvln, g_qhn, g_khn) = _local_step(
        x[0], c, positions[0], loss_target[0], _unshard_cols(wa_g), b_ada, norm_w, _unshard_cols(win_g),
        q_lora_norm, _unshard_cols(wuq_g), kv_lora_norm, _unshard_cols(wukv_g), q_head_norm, k_head_norm,
        wout_g.reshape(D_MODEL, D_MODEL))
    loss = lax.psum(0.5 * sq_sum / D_MODEL, ("x", "y", "c"))

    grads = [_shard_cols(g_w_in), _shard_cols(g_w_uq), _shard_cols(g_w_ukv),
             g_w_out.reshape(4, D_MODEL // 4, D_MODEL)]
    small = _vector_rows([c, d_ada, g_norm_w, g_qln, g_kvln, g_qhn, g_khn])
    small = jnp.pad(small, ((0, (-small.shape[0]) % 8), (0, 0)))
    own, sib, packs = _exchange_call(grads, small)

    names = ["adam_w_in", "adam_w_uq", "adam_w_ukv", "adam_w_out"]
    shard_w = [(w_in, m_w_in, v_w_in), (w_uq, m_w_uq, v_w_uq), (w_ukv, m_w_ukv, v_w_ukv),
               (w_out, m_w_out, v_w_out)]
    res = {}
    for name, o_g, s_g, (w, m, v) in zip(names, own, sib, shard_w):
        res[name] = _adam_shard_call(name, o_g, s_g, w[0], m[0], v[0])
    chip = 2 * lax.axis_index("x") + lax.axis_index("y")
    ada_cols = w_ada.shape[2]
    c_rows = D_MODEL // LANES
    c_all = jnp.pad(packs[:, :c_rows, :].reshape(8, D_MODEL), ((0, 8), (0, 0)))
    d_rows = packs[:, c_rows:c_rows + 3 * c_rows, :].reshape(8, 3 * D_MODEL)
    d_all = jnp.pad(lax.dynamic_slice_in_dim(d_rows, chip * ada_cols, ada_cols, axis=1), ((0, 8), (0, 0)))
    res_ada = _adam_ada_call(c_all, d_all, w_ada[0], m_w_ada[0], v_w_ada[0])
    vec_names = [(b_ada, m_b_ada, v_b_ada), (norm_w, m_norm_w, v_norm_w), (q_lora_norm, m_q_lora_norm, v_q_lora_norm),
                 (kv_lora_norm, m_kv_lora_norm, v_kv_lora_norm), (q_head_norm, m_q_head_norm, v_q_head_norm),
                 (k_head_norm, m_k_head_norm, v_k_head_norm)]
    packed = [_vector_rows([t[j] for t in vec_names]) for j in range(3)]
    n_rows = packed[0].shape[0]
    pad_rows = (-n_rows) % 8
    packed = [jnp.pad(p, ((0, pad_rows), (0, 0))) for p in packed]
    res_vec = _adam_small_call(packs, packed[0], packed[1], packed[2], c_rows)

    def unpack(arr):
        outs, r0 = [], 0
        for t in vec_names:
            n = t[0].shape[1]
            nr = -(-n // LANES)
            outs.append(arr[r0:r0 + nr].reshape(1, nr * LANES)[:, :n])
            r0 += nr
        return outs

    vec_out = [unpack(a) for a in res_vec]

    def ordered(kind):
        big = lambda name: res[name][kind][None]
        return [res_ada[kind][None], vec_out[kind][0], vec_out[kind][1], big("adam_w_in"), vec_out[kind][2],
                big("adam_w_uq"), vec_out[kind][3], big("adam_w_ukv"), vec_out[kind][4], vec_out[kind][5],
                big("adam_w_out")]

    return (loss, grad_x[None], *ordered(0), *ordered(1), *ordered(2), *ordered(3))


def _local_step(x2, c, positions, tgt, w_ada_full, b_ada, norm_w, w_in_full, q_lora_norm, w_uq_full,
                kv_lora_norm, w_ukv_full, q_head_norm, k_head_norm, w_out_full):
    w_ada_bf = w_ada_full.astype(BF16)
    kr_block = jnp.pad(w_in_full[:, 2688:2720], ((0, 0), (KR_LANE, LANES - KR_LANE - MLA_ROPE_DIM)))
    w_in_bf = jnp.concatenate([w_in_full[:, :2688], w_in_full[:, 2720:], kr_block], axis=1).astype(BF16)
    w_uq_bf = _pad_heads(w_uq_full, MLA_QK_DIM).astype(BF16)
    w_ukv_heads = w_ukv_full.reshape(KV_LORA_RANK, MLA_HEADS, 2 * MLA_NOPE_DIM)
    w_uk_bf = _pad_heads(w_ukv_heads[:, :, :MLA_NOPE_DIM].reshape(KV_LORA_RANK, -1), MLA_NOPE_DIM).astype(BF16)
    w_uv_bf = w_ukv_heads[:, :, MLA_NOPE_DIM:].reshape(KV_LORA_RANK, MLA_WIDTH).astype(BF16)
    w_out_bf = w_out_full.astype(BF16)
    qhn_pad, khn_pad = _pad_lanes(q_head_norm), _pad_lanes(k_head_norm)
    cos_t, sin_a, sin_b = _rope_tables(positions)

    ada = _ada_call(c, w_ada_bf, b_ada)
    shift, scale, gate = ada[:, :D_MODEL], ada[:, D_MODEL:2 * D_MODEL], ada[:, 2 * D_MODEL:]
    hb, q_sb, k_sb, v_sb, g_sb, c_q, c_kv, g_mla, k_rope = _pre_call(x2, shift, scale, norm_w, w_in_bf)
    q_m, k_m, v_m, cqn, ckvn, q0, k0 = _mla_prep_call(
        c_q, c_kv, k_rope, cos_t, sin_a, sin_b, q_lora_norm, kv_lora_norm, qhn_pad, khn_pad,
        w_uq_bf, w_uk_bf, w_uv_bf)
    o_sb, r_sb, kstart = _sb_fwd_call(q_sb, k_sb, v_sb)
    o_mla, lse = _mla_fwd_call(q_m, k_m, v_m)
    do_sb, do_mla, dg_sb, dg_mla, dy, g_w_out, d_gate, sq = _out_call(
        o_sb, g_sb, o_mla, g_mla, x2, tgt, gate, w_out_bf)

    dq_sb, dk_sb, dv_sb = _sb_bwd_call(kstart, q_sb, k_sb, v_sb, do_sb, r_sb)
    dq_m, dk_m, dv_m = _mla_bwd_call(q_m, k_m, v_m, do_mla, o_mla, lse)
    (d_cq, d_ckv, d_kr, g_wuq_pad, g_wuk_pad, g_wuv, g_qln, g_kvln, g_qhn, g_khn) = _mla_prep_bwd_call(
        dq_m, dk_m, dv_m, q0, k0, cqn, ckvn, c_q, c_kv, cos_t, sin_a, sin_b,
        q_lora_norm, kv_lora_norm, qhn_pad, khn_pad, w_uq_bf, w_uk_bf, w_uv_bf)
    grad_x, dproj, d_shift, d_scale, g_norm_w = _dh_call(
        [dq_sb, dk_sb, dv_sb, dg_sb, d_cq, d_ckv, dg_mla, d_kr], x2, dy, shift, scale, norm_w, w_in_bf)
    g_win_pad = _gw_in_call(hb, dproj)

    g_w_in = jnp.concatenate([g_win_pad[:, :2688],
                              g_win_pad[:, C_KR + KR_LANE:C_KR + KR_LANE + MLA_ROPE_DIM],
                              g_win_pad[:, 2688:3200]], axis=1)
    g_w_uq = g_wuq_pad.reshape(Q_LORA_RANK, MLA_HEADS, HEAD_PAD)[:, :, :MLA_QK_DIM].reshape(Q_LORA_RANK, -1)
    g_w_ukv = jnp.concatenate(
        [g_wuk_pad.reshape(KV_LORA_RANK, MLA_HEADS, HEAD_PAD)[:, :, :MLA_NOPE_DIM],
         g_wuv.reshape(KV_LORA_RANK, MLA_HEADS, MLA_NOPE_DIM)], axis=2).reshape(KV_LORA_RANK, -1)
    d_ada = jnp.concatenate([d_shift, d_scale, d_gate], axis=1)
    return (jnp.sum(sq), grad_x, g_w_in, g_w_uq, g_w_ukv, g_w_out, d_ada, g_norm_w, g_qln, g_kvln,
            g_qhn[:, :MLA_QK_DIM], g_khn[:, :MLA_QK_DIM])
```

```python
import functools
import math

import jax
import jax.numpy as jnp
from jax import lax
from jax.experimental import pallas as pl
from jax.experimental.pallas import tpu as pltpu

F32 = jnp.float32
BF16 = jnp.bfloat16
I32 = jnp.int32

D_MODEL = 1024
SB_HEADS = 8
SB_WIDTH = 512
MLA_HEADS = 8
MLA_QK_DIM = 96
MLA_NOPE_DIM = 64
MLA_ROPE_DIM = 32
MLA_WIDTH = 512
Q_LORA_RANK = 384
KV_LORA_RANK = 256
ROPE_THETA = 10000.0
EPS = 1e-6
LANES = 128
HEAD_PAD = 128
MLA_PAD_WIDTH = MLA_HEADS * HEAD_PAD

C_Q, C_K, C_V, C_G = 0, 512, 1024, 1536
C_CQ, C_CKV, C_GM, C_KR = 2048, 2432, 2688, 3200
IN_COLS_PAD = 3328
KR_LANE = 64

ADAM_LR = 0.001
ADAM_B1 = 0.9
ADAM_B2 = 0.999
ADAM_EPS = 1e-08
ADAM_WD = 0.01
ADAM_STEP = 10

SB_SCALE = 0.125
MLA_SCALE = 1.0 / math.sqrt(MLA_QK_DIM)
LN2 = math.log(2.0)
MLA_SCALE_LOG2 = MLA_SCALE / LN2
MLA_BQ = 256
MLA_BK = 1024
MLA_BWD_BK = 512
SB_DEAD = -104.0
MASK_NEG = -1e30

VMEM_LIMIT = 56 * 1024 * 1024
MESH = pl.DeviceIdType.MESH


def _dot(a, b):
    return jnp.dot(a, b, preferred_element_type=F32)


def _dot_nt(a, b):
    return lax.dot_general(a, b, (((1,), (1,)), ((), ())), preferred_element_type=F32)


def _dot_tn(a, b):
    return lax.dot_general(a, b, (((0,), (0,)), ((), ())), preferred_element_type=F32)


def _sigmoid(x):
    return 1.0 / (1.0 + jnp.exp(-x))


def _split_dot(a, m):
    hi = a.astype(BF16)
    lo = (a - hi.astype(F32)).astype(BF16)
    return _dot(hi, m) + _dot(lo, m)


def _params(sem, vmem=None):
    return pltpu.CompilerParams(dimension_semantics=sem, vmem_limit_bytes=vmem)


def _row_tile(s, want):
    return min(want, s)


def _hbm_spec():
    return pl.BlockSpec(memory_space=pltpu.HBM)


def _gather_call(shards):
    n = len(shards)
    halves = [s.shape[1] // 2 for s in shards]

    def body(*refs):
        ins, outs = refs[:n], refs[n:2 * n]
        ici_send, ici_recv, d2d_send, d2d_recv, loc_sems = refs[2 * n:]
        x, y, c = lax.axis_index("x"), lax.axis_index("y"), lax.axis_index("c")
        me = 2 * x + y
        peers = [(1 - x, y), (x, 1 - y), (1 - x, 1 - y)]

        def rows(a, which):
            return pl.ds(pl.multiple_of(which * halves[a], 16), halves[a])

        def ici(a, j, slot):
            px, py = peers[j]
            return pltpu.make_async_remote_copy(
                src_ref=ins[a].at[0, rows(a, c)], dst_ref=outs[a].at[slot, rows(a, c)],
                send_sem=ici_send.at[3 * a + j], recv_sem=ici_recv.at[3 * a + j],
                device_id=(px, py, c), device_id_type=MESH)

        def d2d(a, j, which):
            px, py = peers[j]
            piece = outs[a].at[2 * px + py, rows(a, which)]
            return pltpu.make_async_remote_copy(
                src_ref=piece, dst_ref=piece,
                send_sem=d2d_send.at[3 * a + j], recv_sem=d2d_recv.at[3 * a + j],
                device_id=(x, y, 1 - c), device_id_type=MESH)

        local = [pltpu.make_async_copy(ins[a].at[0], outs[a].at[me], loc_sems.at[a]) for a in range(n)]
        for cp in local:
            cp.start()
        sends = [ici(a, j, me) for a in range(n) for j in range(3)]
        for cp in sends:
            cp.start()
        for a in range(n):
            for j in range(3):
                px, py = peers[j]
                ici(a, j, 2 * px + py).wait_recv()
                cp = d2d(a, j, c)
                cp.start()
                sends.append(cp)
        for a in range(n):
            for j in range(3):
                d2d(a, j, 1 - c).wait_recv()
        for cp in sends:
            cp.wait_send()
        for cp in local:
            cp.wait()

    return pl.pallas_call(
        body, name="gather_weights",
        out_shape=[jax.ShapeDtypeStruct((4,) + s.shape[1:], s.dtype) for s in shards],
        in_specs=[_hbm_spec() for _ in shards],
        out_specs=[_hbm_spec() for _ in shards],
        scratch_shapes=[pltpu.SemaphoreType.DMA((3 * n,)), pltpu.SemaphoreType.DMA((3 * n,)),
                        pltpu.SemaphoreType.DMA((3 * n,)), pltpu.SemaphoreType.DMA((3 * n,)),
                        pltpu.SemaphoreType.DMA((n,))],
    )(*shards)


def _exchange_call(grads, small):
    n = len(grads)

    def body(*refs):
        g_in, small_in = refs[:n], refs[n]
        own, sib, packs = refs[n + 1:2 * n + 1], refs[2 * n + 1:3 * n + 1], refs[3 * n + 1]
        ici_send, ici_recv, d2d_send, d2d_recv, sm_send, sm_recv, loc_sems = refs[3 * n + 2:]
        x, y, c = lax.axis_index("x"), lax.axis_index("y"), lax.axis_index("c")
        me = 2 * x + y
        me8 = 4 * x + 2 * y + c
        sibling = (x, y, 1 - c)
        peers = [(1 - x, y), (x, 1 - y), (1 - x, 1 - y)]
        flips = [(fx, fy, fc) for fx in (0, 1) for fy in (0, 1) for fc in (0, 1)][1:]

        def ici(a, j, src_slot, dst_slot):
            px, py = peers[j]
            return pltpu.make_async_remote_copy(
                src_ref=g_in[a].at[src_slot], dst_ref=own[a].at[dst_slot],
                send_sem=ici_send.at[3 * a + j], recv_sem=ici_recv.at[3 * a + j],
                device_id=(px, py, c), device_id_type=MESH)

        def d2d(a, rel, chip, src):
            return pltpu.make_async_remote_copy(
                src_ref=src, dst_ref=sib[a].at[chip],
                send_sem=d2d_send.at[4 * a + rel], recv_sem=d2d_recv.at[4 * a + rel],
                device_id=sibling, device_id_type=MESH)

        def flipped(r):
            fx, fy, fc = flips[r]
            return ((1 - x) if fx else x, (1 - y) if fy else y, (1 - c) if fc else c)

        def sm(r, slot):
            return pltpu.make_async_remote_copy(
                src_ref=small_in, dst_ref=packs.at[slot],
                send_sem=sm_send.at[r], recv_sem=sm_recv.at[r],
                device_id=flipped(r), device_id_type=MESH)

        def peer8(r):
            px, py, pc = flipped(r)
            return 4 * px + 2 * py + pc

        local = [pltpu.make_async_copy(g_in[a].at[me], own[a].at[me], loc_sems.at[a]) for a in range(n)]
        local.append(pltpu.make_async_copy(small_in, packs.at[me8], loc_sems.at[n]))
        for cp in local:
            cp.start()
        sends = []
        for r in range(7):
            sends.append(sm(r, me8))
        for a in range(n):
            for j in range(3):
                px, py = peers[j]
                sends.append(ici(a, j, 2 * px + py, me))
        for cp in sends:
            cp.start()
        for a in range(n):
            cp = d2d(a, 0, me, g_in[a].at[me])
            cp.start()
            sends.append(cp)
        for a in range(n):
            for j in range(3):
                px, py = peers[j]
                ici(a, j, me, 2 * px + py).wait_recv()
                cp = d2d(a, 1 + j, 2 * px + py, own[a].at[2 * px + py])
                cp.start()
                sends.append(cp)
        for a in range(n):
            d2d(a, 0, me, g_in[a].at[me]).wait_recv()
            for j in range(3):
                px, py = peers[j]
                d2d(a, 1 + j, 2 * px + py, g_in[a].at[me]).wait_recv()
        for r in range(7):
            sm(r, peer8(r)).wait_recv()
        for cp in sends:
            cp.wait_send()
        for cp in local:
            cp.wait()

    out_shape = ([jax.ShapeDtypeStruct(g.shape, g.dtype) for g in grads] * 2
                 + [jax.ShapeDtypeStruct((8,) + small.shape, small.dtype)])
    res = pl.pallas_call(
        body, name="exchange_grads",
        out_shape=out_shape,
        in_specs=[_hbm_spec() for _ in range(n + 1)],
        out_specs=[_hbm_spec() for _ in range(2 * n + 1)],
        scratch_shapes=[pltpu.SemaphoreType.DMA((3 * n,)), pltpu.SemaphoreType.DMA((3 * n,)),
                        pltpu.SemaphoreType.DMA((4 * n,)), pltpu.SemaphoreType.DMA((4 * n,)),
                        pltpu.SemaphoreType.DMA((7,)), pltpu.SemaphoreType.DMA((7,)),
                        pltpu.SemaphoreType.DMA((n + 1,))],
    )(*grads, small)
    return res[:n], res[n:2 * n], res[2 * n]


def _ada_call(c, w_ada_bf, b_ada):
    def body(c_ref, w_ref, b_ref, o_ref):
        cc = c_ref[...]
        sc = jnp.broadcast_to(cc * _sigmoid(cc), (8, D_MODEL)).astype(BF16)
        o_ref[...] = _dot(sc, w_ref[...]) + b_ref[...]

    out = pl.pallas_call(
        body, name="ada_fwd",
        out_shape=jax.ShapeDtypeStruct((8, 3 * D_MODEL), F32),
        compiler_params=pltpu.CompilerParams(vmem_limit_bytes=VMEM_LIMIT),
    )(c, w_ada_bf, b_ada)
    return out[0:1]


def _full(shape):
    return pl.BlockSpec(shape, lambda i: (0,) * len(shape))


def _rows(tm, width):
    return pl.BlockSpec((tm, width), lambda i: (i, 0))


def _pre_call(x, shift, scale, norm_w, w_in_bf):
    s = x.shape[0]
    tm = _row_tile(s, 512)
    groups = [(C_Q, 512, BF16), (C_K, 512, BF16), (C_V, 512, BF16), (C_G, 512, F32),
              (C_CQ, Q_LORA_RANK, F32), (C_CKV, KV_LORA_RANK, F32), (C_GM, 512, F32), (C_KR, LANES, F32)]

    def body(x_ref, sh_ref, sc_ref, nw_ref, w_ref, hb_ref, *outs):
        xx = x_ref[...]
        r0 = lax.rsqrt(jnp.mean(xx * xx, axis=-1, keepdims=True) + EPS)
        h = (xx * r0 * nw_ref[...]) * (1.0 + sc_ref[...]) + sh_ref[...]
        hb = h.astype(BF16)
        hb_ref[...] = hb
        for (c0, width, dt), o_ref in zip(groups, outs):
            o_ref[...] = _dot(hb, w_ref[:, c0:c0 + width]).astype(dt)

    return pl.pallas_call(
        body, name="pre_proj",
        grid=(s // tm,),
        in_specs=[_rows(tm, D_MODEL), _full((1, D_MODEL)), _full((1, D_MODEL)), _full((1, D_MODEL)),
                  _full((D_MODEL, IN_COLS_PAD))],
        out_specs=[_rows(tm, D_MODEL)] + [_rows(tm, w) for _, w, _ in groups],
        out_shape=[jax.ShapeDtypeStruct((s, D_MODEL), BF16)]
        + [jax.ShapeDtypeStruct((s, w), dt) for _, w, dt in groups],
        compiler_params=_params(("arbitrary",), VMEM_LIMIT),
    )(x, shift, scale, norm_w, w_in_bf)


def _rope(t, cos_t, sin_a, sin_b):
    return t * cos_t + pltpu.roll(t, 112, 1) * sin_a + pltpu.roll(t, 16, 1) * sin_b


def _rope_adjoint(d, cos_t, sin_a, sin_b):
    return d * cos_t + pltpu.roll(d * sin_a, 16, 1) + pltpu.roll(d * sin_b, 112, 1)


def _mla_prep_call(c_q, c_kv, k_rope, cos_t, sin_a, sin_b, q_lora_norm, kv_lora_norm, qhn_pad, khn_pad,
                   w_uq_bf, w_uk_bf, w_uv_bf):
    s = c_q.shape[0]
    tm = _row_tile(s, 256)

    def body(cq_ref, ckv_ref, kr_ref, cos_ref, sa_ref, sb_ref, qln_ref, kvln_ref, qhn_ref, khn_ref,
             wuq_ref, wuk_ref, wuv_ref, q_ref, k_ref, v_ref, cqn_ref, ckvn_ref, q0_ref, k0_ref):
        cq = cq_ref[...]
        cqn = (cq * lax.rsqrt(jnp.mean(cq * cq, axis=-1, keepdims=True) + EPS) * qln_ref[...]).astype(BF16)
        cqn_ref[...] = cqn
        ckv = ckv_ref[...]
        ckvn = (ckv * lax.rsqrt(jnp.mean(ckv * ckv, axis=-1, keepdims=True) + EPS) * kvln_ref[...]).astype(BF16)
        ckvn_ref[...] = ckvn
        v_ref[...] = _dot(ckvn, wuv_ref[...]).astype(BF16)
        cos_t, sin_a, sin_b = cos_ref[...], sa_ref[...], sb_ref[...]
        kr = kr_ref[...]
        for h in range(MLA_HEADS):
            cols = slice(h * HEAD_PAD, (h + 1) * HEAD_PAD)
            q0 = _dot(cqn, wuq_ref[:, cols])
            q0_ref[:, cols] = q0
            rq = lax.rsqrt(jnp.sum(q0 * q0, axis=-1, keepdims=True) * (1.0 / MLA_QK_DIM) + EPS)
            q_ref[:, cols] = (_rope(q0 * rq * qhn_ref[...], cos_t, sin_a, sin_b) * MLA_SCALE_LOG2).astype(BF16)
            k0 = _dot(ckvn, wuk_ref[:, cols]) + kr
            k0_ref[:, cols] = k0
            rk = lax.rsqrt(jnp.sum(k0 * k0, axis=-1, keepdims=True) * (1.0 / MLA_QK_DIM) + EPS)
            k_ref[:, cols] = _rope(k0 * rk * khn_ref[...], cos_t, sin_a, sin_b).astype(BF16)

    return pl.pallas_call(
        body, name="mla_prep",
        grid=(s // tm,),
        in_specs=[_rows(tm, Q_LORA_RANK), _rows(tm, KV_LORA_RANK), _rows(tm, LANES),
                  _rows(tm, LANES), _rows(tm, LANES), _rows(tm, LANES),
                  _full((1, Q_LORA_RANK)), _full((1, KV_LORA_RANK)), _full((1, LANES)), _full((1, LANES)),
                  _full((Q_LORA_RANK, MLA_PAD_WIDTH)), _full((KV_LORA_RANK, MLA_PAD_WIDTH)),
                  _full((KV_LORA_RANK, MLA_WIDTH))],
        out_specs=[_rows(tm, MLA_PAD_WIDTH), _rows(tm, MLA_PAD_WIDTH), _rows(tm, MLA_WIDTH),
                   _rows(tm, Q_LORA_RANK), _rows(tm, KV_LORA_RANK),
                   _rows(tm, MLA_PAD_WIDTH), _rows(tm, MLA_PAD_WIDTH)],
        out_shape=[jax.ShapeDtypeStruct((s, MLA_PAD_WIDTH), BF16), jax.ShapeDtypeStruct((s, MLA_PAD_WIDTH), BF16),
                   jax.ShapeDtypeStruct((s, MLA_WIDTH), BF16),
                   jax.ShapeDtypeStruct((s, Q_LORA_RANK), BF16), jax.ShapeDtypeStruct((s, KV_LORA_RANK), BF16),
                   jax.ShapeDtypeStruct((s, MLA_PAD_WIDTH), F32), jax.ShapeDtypeStruct((s, MLA_PAD_WIDTH), F32)],
        compiler_params=_params(("arbitrary",), VMEM_LIMIT),
    )(c_q, c_kv, k_rope, cos_t, sin_a, sin_b, q_lora_norm, kv_lora_norm, qhn_pad, khn_pad,
      w_uq_bf, w_uk_bf, w_uv_bf)


def _log_sigmoid_pair(z):
    ls = jnp.minimum(z, 0.0) - jnp.log(1.0 + jnp.exp(-jnp.abs(z)))
    return ls, ls - z


def _sb_fwd_call(q, k, v):
    s = q.shape[0]
    bq = _row_tile(s, 256)
    nq = s // bq
    assert s >= 2 * bq

    def body(q_ref, k_ref, v_ref, o_ref, r_ref, ks_ref):
        hp, i = pl.program_id(0), pl.program_id(1)
        lane = lax.broadcasted_iota(I32, (bq, LANES), 1)
        row = lax.broadcasted_iota(I32, (bq, bq), 0)
        col = lax.broadcasted_iota(I32, (bq, bq), 1)
        strict = col < row
        later = jnp.where(row > col, 1.0, 0.0).astype(BF16)
        qs = q_ref[...] * jnp.asarray(SB_SCALE, BF16)
        qms = [qs * _head_mask(lane, hh).astype(BF16) for hh in range(2)]

        def walk(blocks, state):
            for kb, diagonal in blocks:
                rows = pl.ds(pl.multiple_of(kb * bq, bq), bq)
                kblk, vblk = k_ref[rows, :], v_ref[rows, :]
                new = []
                for hh in range(2):
                    run, acc = state[hh]
                    ls, lk = _log_sigmoid_pair(_dot_nt(qms[hh], kblk))
                    if diagonal:
                        lk = jnp.where(strict, lk, 0.0)
                    w = jnp.exp(ls + (_split_dot(lk, later) + run))
                    if diagonal:
                        w = jnp.where(strict, w, 0.0)
                    new.append((run + jnp.sum(lk, axis=1, keepdims=True), acc + _dot(w.astype(BF16), vblk)))
                state = tuple(new)
            return state

        def alive(state):
            return (jnp.maximum(jnp.max(state[0][0]), jnp.max(state[1][0])) > SB_DEAD).astype(I32)

        def finish(state, first):
            ks_ref[hp, i] = first
            o_ref[...] = jnp.where(lane < 64, state[0][1], state[1][1])
            r_ref[...] = jnp.where(lane < 64, state[0][0], state[1][0])

        zero = (jnp.zeros((bq, 1), F32), jnp.zeros((bq, LANES), F32))

        @pl.when(i == 0)
        def _():
            finish(walk([(0, True)], (zero, zero)), 0)

        @pl.when(i > 0)
        def _():
            state = walk([(i, True), (i - 1, False)], (zero, zero))

            def cond(carry):
                return jnp.logical_and(carry[0] >= 0, carry[1] > 0)

            def step(carry):
                state = walk([(carry[0], False)], carry[2])
                return carry[0] - 1, alive(state), state

            kb, _, state = lax.while_loop(cond, step, (i - 2, alive(state), state))
            finish(state, kb + 1)

    return pl.pallas_call(
        body, name="sb_fwd",
        grid=(4, nq),
        in_specs=[pl.BlockSpec((bq, LANES), lambda h, i: (i, h)),
                  pl.BlockSpec((s, LANES), lambda h, i: (0, h)),
                  pl.BlockSpec((s, LANES), lambda h, i: (0, h))],
        out_specs=[pl.BlockSpec((bq, LANES), lambda h, i: (i, h)),
                   pl.BlockSpec((bq, LANES), lambda h, i: (i, h)),
                   pl.BlockSpec(memory_space=pltpu.SMEM)],
        out_shape=[jax.ShapeDtypeStruct((s, SB_WIDTH), F32), jax.ShapeDtypeStruct((s, SB_WIDTH), F32),
                   jax.ShapeDtypeStruct((SB_HEADS // 2, nq), I32)],
        compiler_params=_params(("arbitrary", "arbitrary"), VMEM_LIMIT),
    )(q, k, v)


def _mla_fwd_call(q, k, v):
    s = q.shape[0]
    bq = _row_tile(s, MLA_BQ)
    bk = _row_tile(s, MLA_BK)
    nq = s // bq
    assert bk % bq == 0
    half = bk // 2

    def body(q_ref, k_ref, v_ref, o_ref, lse_ref, p_ref, s_ref):
        i = pl.program_id(1)
        lane = lax.broadcasted_iota(I32, (bq, LANES), 1)
        row = lax.broadcasted_iota(I32, (bq, half), 0)
        col = lax.broadcasted_iota(I32, (bq, half), 1)
        n_full = (i * bq) // bk

        def keys(g):
            return pl.ds(pl.multiple_of(g * half, half), half)

        def put_scores(g, slot):
            for hh in range(2):
                cols = slice(hh * HEAD_PAD, (hh + 1) * HEAD_PAD)
                s_ref[slot, hh] = _dot_nt(q_ref[:, cols], k_ref[keys(g), cols])

        def add_pv(carry, g, slot):
            vblk = v_ref[keys(g), :]
            return tuple((m, l, alpha * acc + _dot(p_ref[slot, hh], vblk), alpha)
                         for hh, (m, l, acc, alpha) in enumerate(carry))

        def substep(g, slot, carry, masked, prefetch):
            if prefetch:
                put_scores(g + 1, 1 - slot)
            carry = add_pv(carry, jnp.maximum(g - 1, 0), 1 - slot)
            new = []
            for hh in range(2):
                m, l, acc, _ = carry[hh]
                sc = s_ref[slot, hh]
                if masked:
                    sc = jnp.where(col + g * half <= row + i * bq, sc, MASK_NEG)
                m_new = jnp.maximum(m, jnp.max(sc, axis=1, keepdims=True))
                p = jnp.exp2(sc - m_new)
                alpha = jnp.exp2(m - m_new)
                l = alpha * l + jnp.sum(p, axis=1, keepdims=True)
                p_ref[slot, hh] = p.astype(BF16)
                new.append((m_new, l, acc, alpha))
            return tuple(new)

        def chunk(kb, carry, masked):
            carry = substep(2 * kb, 0, carry, masked, True)
            return substep(2 * kb + 1, 1, carry, masked, not masked)

        p_ref[1] = jnp.zeros_like(p_ref[1])
        put_scores(0, 0)
        one = (jnp.full((bq, 1), MASK_NEG, F32), jnp.zeros((bq, 1), F32), jnp.zeros((bq, LANES), F32),
               jnp.ones((bq, 1), F32))
        carry = lax.fori_loop(0, n_full, lambda kb, cr: chunk(kb, cr, False), (one, one))
        carry = chunk(n_full, carry, True)
        (m0, l0, a0, _), (m1, l1, a1, _) = add_pv(carry, 2 * n_full + 1, 1)
        o_ref[...] = jnp.where(lane < 64, a0 / l0, a1 / l1)
        lse_ref[...] = jnp.where(lane < 64, m0 + jnp.log2(l0), m1 + jnp.log2(l1))

    return pl.pallas_call(
        body, name="mla_fwd",
        grid=(4, nq),
        in_specs=[pl.BlockSpec((bq, 2 * HEAD_PAD), lambda h, i: (i, h)),
                  pl.BlockSpec((s, 2 * HEAD_PAD), lambda h, i: (0, h)),
                  pl.BlockSpec((s, LANES), lambda h, i: (0, h))],
        out_specs=[pl.BlockSpec((bq, LANES), lambda h, i: (i, h)),
                   pl.BlockSpec((bq, LANES), lambda h, i: (i, h))],
        out_shape=[jax.ShapeDtypeStruct((s, MLA_WIDTH), F32), jax.ShapeDtypeStruct((s, MLA_WIDTH), F32)],
        scratch_shapes=[pltpu.VMEM((2, 2, bq, half), BF16), pltpu.VMEM((2, 2, bq, half), F32)],
        compiler_params=_params(("arbitrary", "arbitrary"), VMEM_LIMIT),
    )(q, k, v)


def _out_call(o_sb, g_sb, o_mla, g_mla, x, target, gate, w_out_bf):
    s = x.shape[0]
    tm = _row_tile(s, 256)

    def body(osb_ref, gsb_ref, oml_ref, gml_ref, x_ref, t_ref, gate_ref, w_ref,
             dosb_ref, doml_ref, dgsb_ref, dgml_ref, dy_ref, gw_ref, dgate_ref, sq_ref):
        @pl.when(pl.program_id(0) == 0)
        def _():
            gw_ref[...] = jnp.zeros_like(gw_ref)
            dgate_ref[...] = jnp.zeros_like(dgate_ref)
            sq_ref[...] = jnp.zeros_like(sq_ref)

        g_s, g_m = gsb_ref[...], gml_ref[...]
        sig_s, sig_m = _sigmoid(g_s), _sigmoid(g_m)
        silu_s, silu_m = g_s * sig_s, g_m * sig_m
        o_s, o_m = osb_ref[...], oml_ref[...]
        mixed = jnp.concatenate([o_s * silu_s, o_m * silu_m], axis=1).astype(BF16)
        u = _dot(mixed, w_ref[...])
        gate_v = gate_ref[...]
        err = x_ref[...] + gate_v * u - t_ref[...]
        sq_ref[...] += jnp.sum(err * err, axis=0, keepdims=True)
        dy = err * (1.0 / D_MODEL)
        dy_ref[...] = dy
        dgate_ref[...] += jnp.sum(dy * u, axis=0, keepdims=True)
        du = (dy * gate_v).astype(BF16)
        gw_ref[...] += _dot_tn(mixed, du)
        dmix = _dot_nt(du, w_ref[...])
        dm_s, dm_m = dmix[:, :SB_WIDTH], dmix[:, SB_WIDTH:]
        dosb_ref[...] = (dm_s * silu_s).astype(BF16)
        doml_ref[...] = (dm_m * silu_m).astype(BF16)
        dgsb_ref[...] = (dm_s * o_s * (sig_s * (1.0 + g_s * (1.0 - sig_s)))).astype(BF16)
        dgml_ref[...] = (dm_m * o_m * (sig_m * (1.0 + g_m * (1.0 - sig_m)))).astype(BF16)

    return pl.pallas_call(
        body, name="out_proj_loss",
        grid=(s // tm,),
        in_specs=[_rows(tm, 512), _rows(tm, 512), _rows(tm, 512), _rows(tm, 512),
                  _rows(tm, D_MODEL), _rows(tm, D_MODEL), _full((1, D_MODEL)), _full((D_MODEL, D_MODEL))],
        out_specs=[_rows(tm, 512), _rows(tm, 512), _rows(tm, 512), _rows(tm, 512), _rows(tm, D_MODEL),
                   _full((D_MODEL, D_MODEL)), _full((1, D_MODEL)), _full((1, D_MODEL))],
        out_shape=[jax.ShapeDtypeStruct((s, 512), BF16)] * 4
        + [jax.ShapeDtypeStruct((s, D_MODEL), F32), jax.ShapeDtypeStruct((D_MODEL, D_MODEL), F32),
           jax.ShapeDtypeStruct((1, D_MODEL), F32), jax.ShapeDtypeStruct((1, D_MODEL), F32)],
        compiler_params=_params(("arbitrary",), VMEM_LIMIT),
    )(o_sb, g_sb, o_mla, g_mla, x, target, gate, w_out_bf)


def _head_mask(lane, hh):
    return jnp.where((lane >= 64) if hh else (lane < 64), 1.0, 0.0)


def _pick_lane(packed, lane, which):
    return jnp.sum(jnp.where(lane == which, packed, 0.0), axis=1, keepdims=True)


def _sb_bwd_call(kstart, q, k, v, do, rfin):
    s = q.shape[0]
    bq = _row_tile(s, 256)
    nq = s // bq

    def body(ks_ref, q_ref, k_ref, v_ref, do_ref, r_ref, dq_ref, dk_ref, dv_ref):
        hp, i = pl.program_id(0), pl.program_id(1)

        @pl.when(i == 0)
        def _():
            dk_ref[...] = jnp.zeros_like(dk_ref)
            dv_ref[...] = jnp.zeros_like(dv_ref)

        lane = lax.broadcasted_iota(I32, (bq, LANES), 1)
        row = lax.broadcasted_iota(I32, (bq, bq), 0)
        col = lax.broadcasted_iota(I32, (bq, bq), 1)
        upto = jnp.where(row <= col, 1.0, 0.0).astype(BF16)
        before = jnp.where(row < col, 1.0, 0.0).astype(BF16)
        qs = q_ref[...] * jnp.asarray(SB_SCALE, BF16)
        do2 = do_ref[...]
        r2 = r_ref[...]
        heads = [_head_mask(lane, hh).astype(BF16) for hh in range(2)]
        qms = [qs * head for head in heads]
        doms = [do2 * head for head in heads]
        totals = [_pick_lane(r2, lane, 64 * hh) for hh in range(2)]

        strict = col < row

        def walk(blocks, state):
            for kb, diagonal in blocks:
                rows = pl.ds(pl.multiple_of(kb * bq, bq), bq)
                kblk, vblk = k_ref[rows, :], v_ref[rows, :]
                new, dk_parts, dv_parts = [], [], []
                for hh in range(2):
                    pre, gpre, dq = state[hh]
                    ls, lk = _log_sigmoid_pair(_dot_nt(qms[hh], kblk))
                    if diagonal:
                        lk = jnp.where(strict, lk, 0.0)
                    w = jnp.exp(ls + ((totals[hh] - pre) - _split_dot(lk, upto)))
                    if diagonal:
                        w = jnp.where(strict, w, 0.0)
                    g = w * _dot_nt(doms[hh], vblk)
                    dz = g - jnp.exp(ls) * (g + (gpre + _split_dot(g, before)))
                    if diagonal:
                        dz = jnp.where(strict, dz, 0.0)
                    dzb = dz.astype(BF16)
                    dk_parts.append(_dot_tn(dzb, qms[hh]))
                    dv_parts.append(_dot_tn(w.astype(BF16), doms[hh]))
                    new.append((pre + jnp.sum(lk, axis=1, keepdims=True), gpre + jnp.sum(g, axis=1, keepdims=True),
                                dq + _dot(dzb, kblk)))
                dk_ref[rows, :] += dk_parts[0] + dk_parts[1]
                dv_ref[rows, :] += dv_parts[0] + dv_parts[1]
                state = tuple(new)
            return state

        def finish(state):
            dq_ref[...] = (jnp.where(lane < 64, state[0][2], state[1][2]) * SB_SCALE).astype(BF16)

        zero = (jnp.zeros((bq, 1), F32), jnp.zeros((bq, 1), F32), jnp.zeros((bq, LANES), F32))

        @pl.when(i == 0)
        def _():
            finish(walk([(0, True)], (zero, zero)))

        @pl.when(i > 0)
        def _():
            state = lax.fori_loop(ks_ref[hp, i], i - 1, lambda kb, st: walk([(kb, False)], st), (zero, zero))
            finish(walk([(i - 1, False), (i, True)], state))

    return pl.pallas_call(
        body, name="sb_bwd",
        grid_spec=pltpu.PrefetchScalarGridSpec(
            num_scalar_prefetch=1, grid=(4, nq),
            in_specs=[pl.BlockSpec((bq, LANES), lambda h, i, ks: (i, h)),
                      pl.BlockSpec((s, LANES), lambda h, i, ks: (0, h)),
                      pl.BlockSpec((s, LANES), lambda h, i, ks: (0, h)),
                      pl.BlockSpec((bq, LANES), lambda h, i, ks: (i, h)),
                      pl.BlockSpec((bq, LANES), lambda h, i, ks: (i, h))],
            out_specs=[pl.BlockSpec((bq, LANES), lambda h, i, ks: (i, h)),
                       pl.BlockSpec((s, LANES), lambda h, i, ks: (0, h)),
                       pl.BlockSpec((s, LANES), lambda h, i, ks: (0, h))]),
        out_shape=[jax.ShapeDtypeStruct((s, SB_WIDTH), BF16), jax.ShapeDtypeStruct((s, SB_WIDTH), F32),
                   jax.ShapeDtypeStruct((s, SB_WIDTH), F32)],
        compiler_params=_params(("arbitrary", "arbitrary"), VMEM_LIMIT),
    )(kstart, q, k, v, do, rfin)


def _mla_bwd_call(q, k, v, do, o, lse):
    s = q.shape[0]
    bq = _row_tile(s, MLA_BQ)
    bk = _row_tile(s, MLA_BWD_BK)
    nq = s // bq
    assert bk % bq == 0
    half = bk // 2

    def body(q_ref, k_ref, v_ref, do_ref, o_ref, lse_ref, dq_ref, dk_ref, dv_ref, dom_ref, s_ref, dp_ref, pb_ref,
             ds_ref):
        i = pl.program_id(1)

        @pl.when(i == 0)
        def _():
            dk_ref[...] = jnp.zeros_like(dk_ref)
            dv_ref[...] = jnp.zeros_like(dv_ref)

        lane = lax.broadcasted_iota(I32, (bq, LANES), 1)
        row = lax.broadcasted_iota(I32, (bq, half), 0)
        col = lax.broadcasted_iota(I32, (bq, half), 1)
        n_full = (i * bq) // bk
        do2 = do_ref[...]
        prod = do2.astype(F32) * o_ref[...]
        lse2 = lse_ref[...]
        deltas, lses = [], []
        for hh in range(2):
            head = _head_mask(lane, hh)
            dom_ref[hh] = do2 * head.astype(BF16)
            deltas.append(jnp.sum(prod * head, axis=1, keepdims=True))
            lses.append(_pick_lane(lse2, lane, 64 * hh))

        def keys(g):
            return pl.ds(pl.multiple_of(g * half, half), half)

        def heads():
            return [(hh, slice(hh * HEAD_PAD, (hh + 1) * HEAD_PAD)) for hh in range(2)]

        def put_products(g, slot):
            vblk = v_ref[keys(g), :]
            for hh, cols in heads():
                s_ref[slot, hh] = _dot_nt(q_ref[:, cols], k_ref[keys(g), cols])
                dp_ref[slot, hh] = _dot_nt(dom_ref[hh], vblk)

        def add_grads(dqs, g, slot):
            rows = keys(g)
            new, dv_parts = [], []
            for hh, cols in heads():
                ds = ds_ref[slot, hh]
                dk_ref[rows, cols] += _dot_tn(ds, q_ref[:, cols])
                dv_parts.append(_dot_tn(pb_ref[slot, hh], dom_ref[hh]))
                new.append(dqs[hh] + _dot(ds, k_ref[rows, cols]))
            dv_ref[rows, :] += dv_parts[0] + dv_parts[1]
            return tuple(new)

        def substep(g, slot, dqs, masked, prefetch):
            if prefetch:
                put_products(g + 1, 1 - slot)
            dqs = add_grads(dqs, jnp.maximum(g - 1, 0), 1 - slot)
            for hh, _ in heads():
                p = jnp.exp2(s_ref[slot, hh] - lses[hh])
                if masked:
                    p = jnp.where(col + g * half <= row + i * bq, p, 0.0)
                ds_ref[slot, hh] = (p * (dp_ref[slot, hh] - deltas[hh])).astype(BF16)
                pb_ref[slot, hh] = p.astype(BF16)
            return dqs

        def chunk(kb, dqs, masked):
            dqs = substep(2 * kb, 0, dqs, masked, True)
            return substep(2 * kb + 1, 1, dqs, masked, not masked)

        ds_ref[1] = jnp.zeros_like(ds_ref[1])
        pb_ref[1] = jnp.zeros_like(pb_ref[1])
        put_products(0, 0)
        zero = jnp.zeros((bq, HEAD_PAD), F32)
        dqs = lax.fori_loop(0, n_full, lambda kb, dqs: chunk(kb, dqs, False), (zero, zero))
        dqs = chunk(n_full, dqs, True)
        dqs = add_grads(dqs, 2 * n_full + 1, 1)
        dq_ref[:, :HEAD_PAD] = dqs[0] * MLA_SCALE
        dq_ref[:, HEAD_PAD:] = dqs[1] * MLA_SCALE

    return pl.pallas_call(
        body, name="mla_bwd",
        grid=(4, nq),
        in_specs=[pl.BlockSpec((bq, 2 * HEAD_PAD), lambda h, i: (i, h)),
                  pl.BlockSpec((s, 2 * HEAD_PAD), lambda h, i: (0, h)),
                  pl.BlockSpec((s, LANES), lambda h, i: (0, h)),
                  pl.BlockSpec((bq, LANES), lambda h, i: (i, h)),
                  pl.BlockSpec((bq, LANES), lambda h, i: (i, h)),
                  pl.BlockSpec((bq, LANES), lambda h, i: (i, h))],
        out_specs=[pl.BlockSpec((bq, 2 * HEAD_PAD), lambda h, i: (i, h)),
                   pl.BlockSpec((s, 2 * HEAD_PAD), lambda h, i: (0, h)),
                   pl.BlockSpec((s, LANES), lambda h, i: (0, h))],
        out_shape=[jax.ShapeDtypeStruct((s, MLA_PAD_WIDTH), F32), jax.ShapeDtypeStruct((s, MLA_PAD_WIDTH), F32),
                   jax.ShapeDtypeStruct((s, MLA_WIDTH), F32)],
        scratch_shapes=[pltpu.VMEM((2, bq, LANES), BF16),
                        pltpu.VMEM((2, 2, bq, half), F32), pltpu.VMEM((2, 2, bq, half), F32),
                        pltpu.VMEM((2, 2, bq, half), BF16), pltpu.VMEM((2, 2, bq, half), BF16)],
        compiler_params=_params(("arbitrary", "arbitrary"), VMEM_LIMIT),
    )(q, k, v, do, o, lse)


def _rms_bwd(d_out, inp, r, weight, n):
    normed = inp * r
    gw = d_out * weight
    d_in = r * (gw - normed * (jnp.sum(gw * normed, axis=-1, keepdims=True) * (1.0 / n)))
    return d_in, d_out * normed


def _mla_prep_bwd_call(dq, dk, dv, q0, k0, cqn, ckvn, c_q, c_kv, cos_t, sin_a, sin_b,
                       q_lora_norm, kv_lora_norm, qhn_pad, khn_pad, w_uq_bf, w_uk_bf, w_uv_bf):
    s = dq.shape[0]
    tm = _row_tile(s, 256)

    def body(dq_ref, dk_ref, dv_ref, q0_ref, k0_ref, cqn_ref, ckvn_ref, cq_ref, ckv_ref,
             cos_ref, sa_ref, sb_ref, qln_ref, kvln_ref, qhn_ref, khn_ref, wuq_ref, wuk_ref, wuv_ref,
             dcq_ref, dckv_ref, dkr_ref, gwuq_ref, gwuk_ref, gwuv_ref, gqln_ref, gkvln_ref, gqhn_ref, gkhn_ref):
        @pl.when(pl.program_id(0) == 0)
        def _():
            for ref in (gwuq_ref, gwuk_ref, gwuv_ref, gqln_ref, gkvln_ref, gqhn_ref, gkhn_ref):
                ref[...] = jnp.zeros_like(ref)

        cos_t, sin_a, sin_b = cos_ref[...], sa_ref[...], sb_ref[...]
        lane = lax.broadcasted_iota(I32, (tm, LANES), 1)
        rope_lanes = jnp.logical_and(lane >= KR_LANE, lane < KR_LANE + MLA_ROPE_DIM)
        cqn, ckvn = cqn_ref[...], ckvn_ref[...]
        d_cqn = jnp.zeros((tm, Q_LORA_RANK), F32)
        d_ckvn = jnp.zeros((tm, KV_LORA_RANK), F32)
        d_kr = jnp.zeros((tm, LANES), F32)
        g_qhn = jnp.zeros((1, LANES), F32)
        g_khn = jnp.zeros((1, LANES), F32)
        for h in range(MLA_HEADS):
            cols = slice(h * HEAD_PAD, (h + 1) * HEAD_PAD)
            q0 = q0_ref[:, cols]
            rq = lax.rsqrt(jnp.sum(q0 * q0, axis=-1, keepdims=True) * (1.0 / MLA_QK_DIM) + EPS)
            d_q0, gq = _rms_bwd(_rope_adjoint(dq_ref[:, cols], cos_t, sin_a, sin_b), q0, rq, qhn_ref[...],
                                MLA_QK_DIM)
            g_qhn += jnp.sum(gq, axis=0, keepdims=True)
            d_q0b = d_q0.astype(BF16)
            d_cqn += _dot_nt(d_q0b, wuq_ref[:, cols])
            gwuq_ref[:, cols] += _dot_tn(cqn, d_q0b)
            k0 = k0_ref[:, cols]
            rk = lax.rsqrt(jnp.sum(k0 * k0, axis=-1, keepdims=True) * (1.0 / MLA_QK_DIM) + EPS)
            d_k0, gk = _rms_bwd(_rope_adjoint(dk_ref[:, cols] * LN2, cos_t, sin_a, sin_b), k0, rk, khn_ref[...],
                                MLA_QK_DIM)
            g_khn += jnp.sum(gk, axis=0, keepdims=True)
            d_kr += jnp.where(rope_lanes, d_k0, 0.0)
            d_k0b = d_k0.astype(BF16)
            d_ckvn += _dot_nt(d_k0b, wuk_ref[:, cols])
            gwuk_ref[:, cols] += _dot_tn(ckvn, d_k0b)
        dvb = dv_ref[...].astype(BF16)
        d_ckvn += _dot_nt(dvb, wuv_ref[...])
        gwuv_ref[...] += _dot_tn(ckvn, dvb)
        gqhn_ref[...] += g_qhn
        gkhn_ref[...] += g_khn
        dkr_ref[...] = d_kr.astype(BF16)
        cq = cq_ref[...]
        rcq = lax.rsqrt(jnp.mean(cq * cq, axis=-1, keepdims=True) + EPS)
        d_cq, gl = _rms_bwd(d_cqn, cq, rcq, qln_ref[...], Q_LORA_RANK)
        dcq_ref[...] = d_cq.astype(BF16)
        gqln_ref[...] += jnp.sum(gl, axis=0, keepdims=True)
        ckv = ckv_ref[...]
        rckv = lax.rsqrt(jnp.mean(ckv * ckv, axis=-1, keepdims=True) + EPS)
        d_ckv, gl = _rms_bwd(d_ckvn, ckv, rckv, kvln_ref[...], KV_LORA_RANK)
        dckv_ref[...] = d_ckv.astype(BF16)
        gkvln_ref[...] += jnp.sum(gl, axis=0, keepdims=True)

    return pl.pallas_call(
        body, name="mla_prep_bwd",
        grid=(s // tm,),
        in_specs=[_rows(tm, MLA_PAD_WIDTH), _rows(tm, MLA_PAD_WIDTH), _rows(tm, MLA_WIDTH),
                  _rows(tm, MLA_PAD_WIDTH), _rows(tm, MLA_PAD_WIDTH),
                  _rows(tm, Q_LORA_RANK), _rows(tm, KV_LORA_RANK), _rows(tm, Q_LORA_RANK), _rows(tm, KV_LORA_RANK),
                  _rows(tm, LANES), _rows(tm, LANES), _rows(tm, LANES),
                  _full((1, Q_LORA_RANK)), _full((1, KV_LORA_RANK)), _full((1, LANES)), _full((1, LANES)),
                  _full((Q_LORA_RANK, MLA_PAD_WIDTH)), _full((KV_LORA_RANK, MLA_PAD_WIDTH)),
                  _full((KV_LORA_RANK, MLA_WIDTH))],
        out_specs=[_rows(tm, Q_LORA_RANK), _rows(tm, KV_LORA_RANK), _rows(tm, LANES),
                   _full((Q_LORA_RANK, MLA_PAD_WIDTH)), _full((KV_LORA_RANK, MLA_PAD_WIDTH)),
                   _full((KV_LORA_RANK, MLA_WIDTH)),
                   _full((1, Q_LORA_RANK)), _full((1, KV_LORA_RANK)), _full((1, LANES)), _full((1, LANES))],
        out_shape=[jax.ShapeDtypeStruct((s, Q_LORA_RANK), BF16), jax.ShapeDtypeStruct((s, KV_LORA_RANK), BF16),
                   jax.ShapeDtypeStruct((s, LANES), BF16),
                   jax.ShapeDtypeStruct((Q_LORA_RANK, MLA_PAD_WIDTH), F32),
                   jax.ShapeDtypeStruct((KV_LORA_RANK, MLA_PAD_WIDTH), F32),
                   jax.ShapeDtypeStruct((KV_LORA_RANK, MLA_WIDTH), F32),
                   jax.ShapeDtypeStruct((1, Q_LORA_RANK), F32), jax.ShapeDtypeStruct((1, KV_LORA_RANK), F32),
                   jax.ShapeDtypeStruct((1, LANES), F32), jax.ShapeDtypeStruct((1, LANES), F32)],
        compiler_params=_params(("arbitrary",), VMEM_LIMIT),
    )(dq, dk, dv, q0, k0, cqn, ckvn, c_q, c_kv, cos_t, sin_a, sin_b,
      q_lora_norm, kv_lora_norm, qhn_pad, khn_pad, w_uq_bf, w_uk_bf, w_uv_bf)


def _dh_call(pieces, x, dy, shift, scale, norm_w, w_in_bf):
    s = x.shape[0]
    tm = _row_tile(s, 256)
    widths = [p.shape[1] for p in pieces]
    offsets = [sum(widths[:j]) for j in range(len(widths))]
    assert offsets[-1] + widths[-1] == IN_COLS_PAD
    n = len(pieces)

    def body(*refs):
        p_refs = refs[:n]
        x_ref, dy_ref, sh_ref, sc_ref, nw_ref, w_ref, gx_ref, dp_ref, dsh_ref, dsc_ref, gnw_ref = refs[n:]

        @pl.when(pl.program_id(0) == 0)
        def _():
            dsh_ref[...] = jnp.zeros_like(dsh_ref)
            dsc_ref[...] = jnp.zeros_like(dsc_ref)
            gnw_ref[...] = jnp.zeros_like(gnw_ref)

        for p_ref, c0, width in zip(p_refs, offsets, widths):
            dp_ref[:, c0:c0 + width] = p_ref[...].astype(BF16)
        dh = _dot_nt(dp_ref[...], w_ref[...])
        xx = x_ref[...]
        r0 = lax.rsqrt(jnp.mean(xx * xx, axis=-1, keepdims=True) + EPS)
        xn = xx * r0
        nw = nw_ref[...]
        dsh_ref[...] += jnp.sum(dh, axis=0, keepdims=True)
        dsc_ref[...] += jnp.sum(dh * (xn * nw), axis=0, keepdims=True)
        dn = dh * (1.0 + sc_ref[...])
        gnw_ref[...] += jnp.sum(dn * xn, axis=0, keepdims=True)
        dxn = dn * nw
        gx_ref[...] = dy_ref[...] + r0 * (dxn - xn * jnp.mean(dxn * xn, axis=-1, keepdims=True))

    return pl.pallas_call(
        body, name="in_proj_bwd",
        grid=(s // tm,),
        in_specs=[_rows(tm, w) for w in widths]
        + [_rows(tm, D_MODEL), _rows(tm, D_MODEL), _full((1, D_MODEL)), _full((1, D_MODEL)), _full((1, D_MODEL)),
           _full((D_MODEL, IN_COLS_PAD))],
        out_specs=[_rows(tm, D_MODEL), _rows(tm, IN_COLS_PAD),
                   _full((1, D_MODEL)), _full((1, D_MODEL)), _full((1, D_MODEL))],
        out_shape=[jax.ShapeDtypeStruct((s, D_MODEL), F32), jax.ShapeDtypeStruct((s, IN_COLS_PAD), BF16),
                   jax.ShapeDtypeStruct((1, D_MODEL), F32), jax.ShapeDtypeStruct((1, D_MODEL), F32),
                   jax.ShapeDtypeStruct((1, D_MODEL), F32)],
        compiler_params=_params(("arbitrary",), VMEM_LIMIT),
    )(*pieces, x, dy, shift, scale, norm_w, w_in_bf)


def _gw_in_call(hb, dproj):
    s = hb.shape[0]
    tk = _row_tile(s, 512)
    tn = IN_COLS_PAD // 2

    def body(h_ref, d_ref, g_ref):
        @pl.when(pl.program_id(1) == 0)
        def _():
            g_ref[...] = jnp.zeros_like(g_ref)

        g_ref[...] += _dot_tn(h_ref[...], d_ref[...])

    return pl.pallas_call(
        body, name="in_proj_wgrad",
        grid=(2, s // tk),
        in_specs=[pl.BlockSpec((tk, D_MODEL), lambda j, t: (t, 0)), pl.BlockSpec((tk, tn), lambda j, t: (t, j))],
        out_specs=pl.BlockSpec((D_MODEL, tn), lambda j, t: (0, j)),
        out_shape=jax.ShapeDtypeStruct((D_MODEL, IN_COLS_PAD), F32),
        compiler_params=_params(("arbitrary", "arbitrary"), VMEM_LIMIT),
    )(hb, dproj)


def _adamw(g, w, m, v):
    m = ADAM_B1 * m + (1.0 - ADAM_B1) * g
    v = ADAM_B2 * v + (1.0 - ADAM_B2) * (g * g)
    m_hat = m / (1.0 - ADAM_B1 ** ADAM_STEP)
    v_hat = v / (1.0 - ADAM_B2 ** ADAM_STEP)
    delta = -ADAM_LR * (m_hat / (jnp.sqrt(v_hat) + ADAM_EPS) + ADAM_WD * w)
    return delta, m, v


def _adam_shard_call(name, own, sib, w, m, v):
    r, c = w.shape
    tr = r if r <= 512 else 256

    def body(own_ref, sib_ref, w_ref, m_ref, v_ref, g_ref, d_ref, nm_ref, nv_ref):
        a = ((own_ref[0] + own_ref[1]) + own_ref[2]) + own_ref[3]
        b = ((sib_ref[0] + sib_ref[1]) + sib_ref[2]) + sib_ref[3]
        g = a + b
        g_ref[...] = g
        d_ref[...], nm_ref[...], nv_ref[...] = _adamw(g, w_ref[...], m_ref[...], v_ref[...])

    part = pl.BlockSpec((4, tr, c), lambda i: (0, i, 0))
    blk = pl.BlockSpec((tr, c), lambda i: (i, 0))
    return pl.pallas_call(
        body, name=name,
        grid=(r // tr,),
        in_specs=[part, part, blk, blk, blk],
        out_specs=[blk] * 4,
        out_shape=[jax.ShapeDtypeStruct((r, c), F32)] * 4,
        compiler_params=_params(("arbitrary",), VMEM_LIMIT),
    )(own, sib, w, m, v)


def _adam_ada_call(c_all, d_all, w, m, v):
    r, c = w.shape
    tr = 256

    def body(c_ref, d_ref, w_ref, m_ref, v_ref, g_ref, dl_ref, nm_ref, nv_ref):
        cc = c_ref[...]
        sc = cc * _sigmoid(cc)
        dd = d_ref[...]
        sc_hi = sc.astype(BF16)
        sc_lo = (sc - sc_hi.astype(F32)).astype(BF16)
        dd_hi = dd.astype(BF16)
        dd_lo = (dd - dd_hi.astype(F32)).astype(BF16)
        g = _dot_tn(sc_hi, dd_hi) + (_dot_tn(sc_hi, dd_lo) + _dot_tn(sc_lo, dd_hi))
        g_ref[...] = g
        dl_ref[...], nm_ref[...], nv_ref[...] = _adamw(g, w_ref[...], m_ref[...], v_ref[...])

    blk = pl.BlockSpec((tr, c), lambda i: (i, 0))
    return pl.pallas_call(
        body, name="adam_w_ada",
        grid=(r // tr,),
        in_specs=[pl.BlockSpec((16, tr), lambda i: (0, i)), pl.BlockSpec((16, c), lambda i: (0, 0)), blk, blk, blk],
        out_specs=[blk] * 4,
        out_shape=[jax.ShapeDtypeStruct((r, c), F32)] * 4,
        compiler_params=_params(("arbitrary",), VMEM_LIMIT),
    )(c_all, d_all, w, m, v)


def _adam_small_call(packs, w, m, v, first_row):
    rows = w.shape[0]

    def body(p_ref, w_ref, m_ref, v_ref, g_ref, d_ref, nm_ref, nv_ref):
        g = p_ref[0, first_row:first_row + rows, :]
        for b in range(1, 8):
            g = g + p_ref[b, first_row:first_row + rows, :]
        g_ref[...] = g
        d_ref[...], nm_ref[...], nv_ref[...] = _adamw(g, w_ref[...], m_ref[...], v_ref[...])

    return pl.pallas_call(
        body, name="adam_vectors",
        out_shape=[jax.ShapeDtypeStruct((rows, LANES), F32)] * 4,
    )(packs, w, m, v)


def _rope_tables(positions):
    inv_freq = ROPE_THETA ** (-jnp.arange(0, MLA_ROPE_DIM, 2, dtype=F32) / MLA_ROPE_DIM)
    ang = positions.astype(F32)[:, None] * inv_freq
    cos, sin = jnp.cos(ang), jnp.sin(ang)
    s = positions.shape[0]
    half = MLA_ROPE_DIM // 2
    zeros = lambda n: jnp.zeros((s, n), F32)
    cos_t = jnp.concatenate([jnp.ones((s, MLA_NOPE_DIM), F32), cos, cos, zeros(HEAD_PAD - MLA_QK_DIM)], axis=1)
    sin_a = jnp.concatenate([zeros(MLA_NOPE_DIM), -sin, zeros(half + HEAD_PAD - MLA_QK_DIM)], axis=1)
    sin_b = jnp.concatenate([zeros(MLA_NOPE_DIM + half), sin, zeros(HEAD_PAD - MLA_QK_DIM)], axis=1)
    return cos_t, sin_a, sin_b


def _unshard_cols(g):
    return jnp.transpose(g, (1, 0, 2)).reshape(g.shape[1], 4 * g.shape[2])


def _shard_cols(g):
    r, c4 = g.shape
    return jnp.transpose(g.reshape(r, 4, c4 // 4), (1, 0, 2))


def _pad_heads(w, width):
    r = w.shape[0]
    w = w.reshape(r, MLA_HEADS, width)
    return jnp.pad(w, ((0, 0), (0, 0), (0, HEAD_PAD - width))).reshape(r, MLA_PAD_WIDTH)


def _pad_lanes(vec):
    return jnp.pad(vec, ((0, 0), (0, LANES - vec.shape[1])))


def _vector_rows(vecs):
    rows = []
    for vec in vecs:
        n = vec.shape[1]
        pad = (-n) % LANES
        rows.append(jnp.pad(vec, ((0, 0), (0, pad))).reshape((n + pad) // LANES, LANES))
    return jnp.concatenate(rows, axis=0)


def kernel(x, c, positions, w_ada, b_ada, norm_w, w_in, q_lora_norm, w_uq, kv_lora_norm, w_ukv, q_head_norm, k_head_norm, w_out, loss_target, m_w_ada, m_b_ada, m_norm_w, m_w_in, m_q_lora_norm, m_w_uq, m_kv_lora_norm, m_w_ukv, m_q_head_norm, m_k_head_norm, m_w_out, v_w_ada, v_b_ada, v_norm_w, v_w_in, v_q_lora_norm, v_w_uq, v_kv_lora_norm, v_w_ukv, v_q_head_norm, v_k_head_norm, v_w_out):
    wa_g, win_g, wuq_g, wukv_g, wout_g = _gather_call([w.astype(BF16) for w in (w_ada, w_in, w_uq, w_ukv, w_out)])
    (sq_sum, grad_x, g_w_in, g_w_uq, g_w_ukv, g_w_out, d_ada, g_norm_w, g_qln, g_kvln, g_qhn, g_khn) = _local_step(
        x[0], c, positions[0], loss_target[0], _unshard_cols(wa_g), b_ada, norm_w, _unshard_cols(win_g),
        q_lora_norm, _unshard_cols(wuq_g), kv_lora_norm, _unshard_cols(wukv_g), q_head_norm, k_head_norm,
        wout_g.reshape(D_MODEL, D_MODEL))
    loss = lax.psum(0.5 * sq_sum / D_MODEL, ("x", "y", "c"))

    grads = [_shard_cols(g_w_in), _shard_cols(g_w_uq), _shard_cols(g_w_ukv),
             g_w_out.reshape(4, D_MODEL // 4, D_MODEL)]
    small = _vector_rows([c, d_ada, g_norm_w, g_qln, g_kvln, g_qhn, g_khn])
    small = jnp.pad(small, ((0, (-small.shape[0]) % 8), (0, 0)))
    own, sib, packs = _exchange_call(grads, small)

    names = ["adam_w_in", "adam_w_uq", "adam_w_ukv", "adam_w_out"]
    shard_w = [(w_in, m_w_in, v_w_in), (w_uq, m_w_uq, v_w_uq), (w_ukv, m_w_ukv, v_w_ukv),
               (w_out, m_w_out, v_w_out)]
    res = {}
    for name, o_g, s_g, (w, m, v) in zip(names, own, sib, shard_w):
        res[name] = _adam_shard_call(name, o_g, s_g, w[0], m[0], v[0])
    chip = 2 * lax.axis_index("x") + lax.axis_index("y")
    ada_cols = w_ada.shape[2]
    c_rows = D_MODEL // LANES
    c_all = jnp.pad(packs[:, :c_rows, :].reshape(8, D_MODEL), ((0, 8), (0, 0)))
    d_rows = packs[:, c_rows:c_rows + 3 * c_rows, :].reshape(8, 3 * D_MODEL)
    d_all = jnp.pad(lax.dynamic_slice_in_dim(d_rows, chip * ada_cols, ada_cols, axis=1), ((0, 8), (0, 0)))
    res_ada = _adam_ada_call(c_all, d_all, w_ada[0], m_w_ada[0], v_w_ada[0])
    vec_names = [(b_ada, m_b_ada, v_b_ada), (norm_w, m_norm_w, v_norm_w), (q_lora_norm, m_q_lora_norm, v_q_lora_norm),
                 (kv_lora_norm, m_kv_lora_norm, v_kv_lora_norm), (q_head_norm, m_q_head_norm, v_q_head_norm),
                 (k_head_norm, m_k_head_norm, v_k_head_norm)]
    packed = [_vector_rows([t[j] for t in vec_names]) for j in range(3)]
    n_rows = packed[0].shape[0]
    pad_rows = (-n_rows) % 8
    packed = [jnp.pad(p, ((0, pad_rows), (0, 0))) for p in packed]
    res_vec = _adam_small_call(packs, packed[0], packed[1], packed[2], c_rows)

    def unpack(arr):
        outs, r0 = [], 0
        for t in vec_names:
            n = t[0].shape[1]
            nr = -(-n // LANES)
            outs.append(arr[r0:r0 + nr].reshape(1, nr * LANES)[:, :n])
            r0 += nr
        return outs

    vec_out = [unpack(a) for a in res_vec]

    def ordered(kind):
        big = lambda name: res[name][kind][None]
        return [res_ada[kind][None], vec_out[kind][0], vec_out[kind][1], big("adam_w_in"), vec_out[kind][2],
                big("adam_w_uq"), vec_out[kind][3], big("adam_w_ukv"), vec_out[kind][4], vec_out[kind][5],
                big("adam_w_out")]

    return (loss, grad_x[None], *ordered(0), *ordered(1), *ordered(2), *ordered(3))


def _local_step(x2, c, positions, tgt, w_ada_full, b_ada, norm_w, w_in_full, q_lora_norm, w_uq_full,
                kv_lora_norm, w_ukv_full, q_head_norm, k_head_norm, w_out_full):
    w_ada_bf = w_ada_full.astype(BF16)
    kr_block = jnp.pad(w_in_full[:, 2688:2720], ((0, 0), (KR_LANE, LANES - KR_LANE - MLA_ROPE_DIM)))
    w_in_bf = jnp.concatenate([w_in_full[:, :2688], w_in_full[:, 2720:], kr_block], axis=1).astype(BF16)
    w_uq_bf = _pad_heads(w_uq_full, MLA_QK_DIM).astype(BF16)
    w_ukv_heads = w_ukv_full.reshape(KV_LORA_RANK, MLA_HEADS, 2 * MLA_NOPE_DIM)
    w_uk_bf = _pad_heads(w_ukv_heads[:, :, :MLA_NOPE_DIM].reshape(KV_LORA_RANK, -1), MLA_NOPE_DIM).astype(BF16)
    w_uv_bf = w_ukv_heads[:, :, MLA_NOPE_DIM:].reshape(KV_LORA_RANK, MLA_WIDTH).astype(BF16)
    w_out_bf = w_out_full.astype(BF16)
    qhn_pad, khn_pad = _pad_lanes(q_head_norm), _pad_lanes(k_head_norm)
    cos_t, sin_a, sin_b = _rope_tables(positions)

    ada = _ada_call(c, w_ada_bf, b_ada)
    shift, scale, gate = ada[:, :D_MODEL], ada[:, D_MODEL:2 * D_MODEL], ada[:, 2 * D_MODEL:]
    hb, q_sb, k_sb, v_sb, g_sb, c_q, c_kv, g_mla, k_rope = _pre_call(x2, shift, scale, norm_w, w_in_bf)
    q_m, k_m, v_m, cqn, ckvn, q0, k0 = _mla_prep_call(
        c_q, c_kv, k_rope, cos_t, sin_a, sin_b, q_lora_norm, kv_lora_norm, qhn_pad, khn_pad,
        w_uq_bf, w_uk_bf, w_uv_bf)
    o_sb, r_sb, kstart = _sb_fwd_call(q_sb, k_sb, v_sb)
    o_mla, lse = _mla_fwd_call(q_m, k_m, v_m)
    do_sb, do_mla, dg_sb, dg_mla, dy, g_w_out, d_gate, sq = _out_call(
        o_sb, g_sb, o_mla, g_mla, x2, tgt, gate, w_out_bf)

    dq_sb, dk_sb, dv_sb = _sb_bwd_call(kstart, q_sb, k_sb, v_sb, do_sb, r_sb)
    dq_m, dk_m, dv_m = _mla_bwd_call(q_m, k_m, v_m, do_mla, o_mla, lse)
    (d_cq, d_ckv, d_kr, g_wuq_pad, g_wuk_pad, g_wuv, g_qln, g_kvln, g_qhn, g_khn) = _mla_prep_bwd_call(
        dq_m, dk_m, dv_m, q0, k0, cqn, ckvn, c_q, c_kv, cos_t, sin_a, sin_b,
        q_lora_norm, kv_lora_norm, qhn_pad, khn_pad, w_uq_bf, w_uk_bf, w_uv_bf)
    grad_x, dproj, d_shift, d_scale, g_norm_w = _dh_call(
        [dq_sb, dk_sb, dv_sb, dg_sb, d_cq, d_ckv, dg_mla, d_kr], x2, dy, shift, scale, norm_w, w_in_bf)
    g_win_pad = _gw_in_call(hb, dproj)

    g_w_in = jnp.concatenate([g_win_pad[:, :2688],
                              g_win_pad[:, C_KR + KR_LANE:C_KR + KR_LANE + MLA_ROPE_DIM],
                              g_win_pad[:, 2688:3200]], axis=1)
    g_w_uq = g_wuq_pad.reshape(Q_LORA_RANK, MLA_HEADS, HEAD_PAD)[:, :, :MLA_QK_DIM].reshape(Q_LORA_RANK, -1)
    g_w_ukv = jnp.concatenate(
        [g_wuk_pad.reshape(KV_LORA_RANK, MLA_HEADS, HEAD_PAD)[:, :, :MLA_NOPE_DIM],
         g_wuv.reshape(KV_LORA_RANK, MLA_HEADS, MLA_NOPE_DIM)], axis=2).reshape(KV_LORA_RANK, -1)
    d_ada = jnp.concatenate([d_shift, d_scale, d_gate], axis=1)
    return (jnp.sum(sq), grad_x, g_w_in, g_w_uq, g_w_ukv, g_w_out, d_ada, g_norm_w, g_qln, g_kvln,
            g_qhn[:, :MLA_QK_DIM], g_khn[:, :MLA_QK_DIM])
```

```python
import functools
import math

import jax
import jax.numpy as jnp
from jax import lax
from jax.experimental import pallas as pl
from jax.experimental.pallas import tpu as pltpu

F32 = jnp.float32
BF16 = jnp.bfloat16
I32 = jnp.int32

D_MODEL = 1024
SB_HEADS = 8
SB_WIDTH = 512
MLA_HEADS = 8
MLA_QK_DIM = 96
MLA_NOPE_DIM = 64
MLA_ROPE_DIM = 32
MLA_WIDTH = 512
Q_LORA_RANK = 384
KV_LORA_RANK = 256
ROPE_THETA = 10000.0
EPS = 1e-6
LANES = 128
HEAD_PAD = 128
MLA_PAD_WIDTH = MLA_HEADS * HEAD_PAD

C_Q, C_K, C_V, C_G = 0, 512, 1024, 1536
C_CQ, C_CKV, C_GM, C_KR = 2048, 2432, 2688, 3200
IN_COLS_PAD = 3328
KR_LANE = 64

ADAM_LR = 0.001
ADAM_B1 = 0.9
ADAM_B2 = 0.999
ADAM_EPS = 1e-08
ADAM_WD = 0.01
ADAM_STEP = 10

SB_SCALE = 0.125
MLA_SCALE = 1.0 / math.sqrt(MLA_QK_DIM)
LN2 = math.log(2.0)
MLA_SCALE_LOG2 = MLA_SCALE / LN2
MLA_BQ = 512
MLA_BWD_BQ = 512
MLA_BK = 1024
MLA_BWD_BK = 512
SB_DEAD = -104.0
MASK_NEG = -1e30

VMEM_LIMIT = 56 * 1024 * 1024
MESH = pl.DeviceIdType.MESH


def _dot(a, b):
    return jnp.dot(a, b, preferred_element_type=F32)


def _dot_nt(a, b):
    return lax.dot_general(a, b, (((1,), (1,)), ((), ())), preferred_element_type=F32)


def _dot_tn(a, b):
    return lax.dot_general(a, b, (((0,), (0,)), ((), ())), preferred_element_type=F32)


def _sigmoid(x):
    return 1.0 / (1.0 + jnp.exp(-x))


def _split_dot(a, m):
    hi = a.astype(BF16)
    lo = (a - hi.astype(F32)).astype(BF16)
    return _dot(hi, m) + _dot(lo, m)


def _params(sem, vmem=None):
    return pltpu.CompilerParams(dimension_semantics=sem, vmem_limit_bytes=vmem)


def _row_tile(s, want):
    return min(want, s)


def _hbm_spec():
    return pl.BlockSpec(memory_space=pltpu.HBM)


def _gather_call(shards):
    n = len(shards)
    halves = [s.shape[1] // 2 for s in shards]

    def body(*refs):
        ins, outs = refs[:n], refs[n:2 * n]
        ici_send, ici_recv, d2d_send, d2d_recv, loc_sems = refs[2 * n:]
        x, y, c = lax.axis_index("x"), lax.axis_index("y"), lax.axis_index("c")
        me = 2 * x + y
        peers = [(1 - x, y), (x, 1 - y), (1 - x, 1 - y)]

        def rows(a, which):
            return pl.ds(pl.multiple_of(which * halves[a], 16), halves[a])

        def ici(a, j, slot):
            px, py = peers[j]
            return pltpu.make_async_remote_copy(
                src_ref=ins[a].at[0, rows(a, c)], dst_ref=outs[a].at[slot, rows(a, c)],
                send_sem=ici_send.at[3 * a + j], recv_sem=ici_recv.at[3 * a + j],
                device_id=(px, py, c), device_id_type=MESH)

        def d2d(a, j, which):
            px, py = peers[j]
            piece = outs[a].at[2 * px + py, rows(a, which)]
            return pltpu.make_async_remote_copy(
                src_ref=piece, dst_ref=piece,
                send_sem=d2d_send.at[3 * a + j], recv_sem=d2d_recv.at[3 * a + j],
                device_id=(x, y, 1 - c), device_id_type=MESH)

        local = [pltpu.make_async_copy(ins[a].at[0], outs[a].at[me], loc_sems.at[a]) for a in range(n)]
        for cp in local:
            cp.start()
        sends = [ici(a, j, me) for a in range(n) for j in range(3)]
        for cp in sends:
            cp.start()
        for a in range(n):
            for j in range(3):
                px, py = peers[j]
                ici(a, j, 2 * px + py).wait_recv()
                cp = d2d(a, j, c)
                cp.start()
                sends.append(cp)
        for a in range(n):
            for j in range(3):
                d2d(a, j, 1 - c).wait_recv()
        for cp in sends:
            cp.wait_send()
        for cp in local:
            cp.wait()

    return pl.pallas_call(
        body, name="gather_weights",
        out_shape=[jax.ShapeDtypeStruct((4,) + s.shape[1:], s.dtype) for s in shards],
        in_specs=[_hbm_spec() for _ in shards],
        out_specs=[_hbm_spec() for _ in shards],
        scratch_shapes=[pltpu.SemaphoreType.DMA((3 * n,)), pltpu.SemaphoreType.DMA((3 * n,)),
                        pltpu.SemaphoreType.DMA((3 * n,)), pltpu.SemaphoreType.DMA((3 * n,)),
                        pltpu.SemaphoreType.DMA((n,))],
    )(*shards)


def _exchange_call(grads, small):
    n = len(grads)

    def body(*refs):
        g_in, small_in = refs[:n], refs[n]
        own, sib, packs = refs[n + 1:2 * n + 1], refs[2 * n + 1:3 * n + 1], refs[3 * n + 1]
        ici_send, ici_recv, d2d_send, d2d_recv, sm_send, sm_recv, loc_sems = refs[3 * n + 2:]
        x, y, c = lax.axis_index("x"), lax.axis_index("y"), lax.axis_index("c")
        me = 2 * x + y
        me8 = 4 * x + 2 * y + c
        sibling = (x, y, 1 - c)
        peers = [(1 - x, y), (x, 1 - y), (1 - x, 1 - y)]
        flips = [(fx, fy, fc) for fx in (0, 1) for fy in (0, 1) for fc in (0, 1)][1:]

        def ici(a, j, src_slot, dst_slot):
            px, py = peers[j]
            return pltpu.make_async_remote_copy(
                src_ref=g_in[a].at[src_slot], dst_ref=own[a].at[dst_slot],
                send_sem=ici_send.at[3 * a + j], recv_sem=ici_recv.at[3 * a + j],
                device_id=(px, py, c), device_id_type=MESH)

        def d2d(a, rel, chip, src):
            return pltpu.make_async_remote_copy(
                src_ref=src, dst_ref=sib[a].at[chip],
                send_sem=d2d_send.at[4 * a + rel], recv_sem=d2d_recv.at[4 * a + rel],
                device_id=sibling, device_id_type=MESH)

        def flipped(r):
            fx, fy, fc = flips[r]
            return ((1 - x) if fx else x, (1 - y) if fy else y, (1 - c) if fc else c)

        def sm(r, slot):
            return pltpu.make_async_remote_copy(
                src_ref=small_in, dst_ref=packs.at[slot],
                send_sem=sm_send.at[r], recv_sem=sm_recv.at[r],
                device_id=flipped(r), device_id_type=MESH)

        def peer8(r):
            px, py, pc = flipped(r)
            return 4 * px + 2 * py + pc

        local = [pltpu.make_async_copy(g_in[a].at[me], own[a].at[me], loc_sems.at[a]) for a in range(n)]
        local.append(pltpu.make_async_copy(small_in, packs.at[me8], loc_sems.at[n]))
        for cp in local:
            cp.start()
        sends = []
        for r in range(7):
            sends.append(sm(r, me8))
        for a in range(n):
            for j in range(3):
                px, py = peers[j]
                sends.append(ici(a, j, 2 * px + py, me))
        for cp in sends:
            cp.start()
        for a in range(n):
            cp = d2d(a, 0, me, g_in[a].at[me])
            cp.start()
            sends.append(cp)
        for a in range(n):
            for j in range(3):
                px, py = peers[j]
                ici(a, j, me, 2 * px + py).wait_recv()
                cp = d2d(a, 1 + j, 2 * px + py, own[a].at[2 * px + py])
                cp.start()
                sends.append(cp)
        for a in range(n):
            d2d(a, 0, me, g_in[a].at[me]).wait_recv()
            for j in range(3):
                px, py = peers[j]
                d2d(a, 1 + j, 2 * px + py, g_in[a].at[me]).wait_recv()
        for r in range(7):
            sm(r, peer8(r)).wait_recv()
        for cp in sends:
            cp.wait_send()
        for cp in local:
            cp.wait()

    out_shape = ([jax.ShapeDtypeStruct(g.shape, g.dtype) for g in grads] * 2
                 + [jax.ShapeDtypeStruct((8,) + small.shape, small.dtype)])
    res = pl.pallas_call(
        body, name="exchange_grads",
        out_shape=out_shape,
        in_specs=[_hbm_spec() for _ in range(n + 1)],
        out_specs=[_hbm_spec() for _ in range(2 * n + 1)],
        scratch_shapes=[pltpu.SemaphoreType.DMA((3 * n,)), pltpu.SemaphoreType.DMA((3 * n,)),
                        pltpu.SemaphoreType.DMA((4 * n,)), pltpu.SemaphoreType.DMA((4 * n,)),
                        pltpu.SemaphoreType.DMA((7,)), pltpu.SemaphoreType.DMA((7,)),
                        pltpu.SemaphoreType.DMA((n + 1,))],
    )(*grads, small)
    return res[:n], res[n:2 * n], res[2 * n]


def _ada_call(c, w_ada_bf, b_ada):
    def body(c_ref, w_ref, b_ref, o_ref):
        cc = c_ref[...]
        sc = jnp.broadcast_to(cc * _sigmoid(cc), (8, D_MODEL)).astype(BF16)
        o_ref[...] = _dot(sc, w_ref[...]) + b_ref[...]

    out = pl.pallas_call(
        body, name="ada_fwd",
        out_shape=jax.ShapeDtypeStruct((8, 3 * D_MODEL), F32),
        compiler_params=pltpu.CompilerParams(vmem_limit_bytes=VMEM_LIMIT),
    )(c, w_ada_bf, b_ada)
    return out[0:1]


def _full(shape):
    return pl.BlockSpec(shape, lambda i: (0,) * len(shape))


def _rows(tm, width):
    return pl.BlockSpec((tm, width), lambda i: (i, 0))


def _pre_call(x, shift, scale, norm_w, w_in_bf):
    s = x.shape[0]
    tm = _row_tile(s, 512)
    groups = [(C_Q, 512, BF16), (C_K, 512, BF16), (C_V, 512, BF16), (C_G, 512, F32),
              (C_CQ, Q_LORA_RANK, F32), (C_CKV, KV_LORA_RANK, F32), (C_GM, 512, F32), (C_KR, LANES, F32)]

    def body(x_ref, sh_ref, sc_ref, nw_ref, w_ref, hb_ref, *outs):
        xx = x_ref[...]
        r0 = lax.rsqrt(jnp.mean(xx * xx, axis=-1, keepdims=True) + EPS)
        h = (xx * r0 * nw_ref[...]) * (1.0 + sc_ref[...]) + sh_ref[...]
        hb = h.astype(BF16)
        hb_ref[...] = hb
        for (c0, width, dt), o_ref in zip(groups, outs):
            o_ref[...] = _dot(hb, w_ref[:, c0:c0 + width]).astype(dt)

    return pl.pallas_call(
        body, name="pre_proj",
        grid=(s // tm,),
        in_specs=[_rows(tm, D_MODEL), _full((1, D_MODEL)), _full((1, D_MODEL)), _full((1, D_MODEL)),
                  _full((D_MODEL, IN_COLS_PAD))],
        out_specs=[_rows(tm, D_MODEL)] + [_rows(tm, w) for _, w, _ in groups],
        out_shape=[jax.ShapeDtypeStruct((s, D_MODEL), BF16)]
        + [jax.ShapeDtypeStruct((s, w), dt) for _, w, dt in groups],
        compiler_params=_params(("arbitrary",), VMEM_LIMIT),
    )(x, shift, scale, norm_w, w_in_bf)


def _rope(t, cos_t, sin_a, sin_b):
    return t * cos_t + pltpu.roll(t, 112, 1) * sin_a + pltpu.roll(t, 16, 1) * sin_b


def _rope_adjoint(d, cos_t, sin_a, sin_b):
    return d * cos_t + pltpu.roll(d * sin_a, 16, 1) + pltpu.roll(d * sin_b, 112, 1)


def _mla_prep_call(c_q, c_kv, k_rope, cos_t, sin_a, sin_b, q_lora_norm, kv_lora_norm, qhn_pad, khn_pad,
                   w_uq_bf, w_uk_bf, w_uv_bf):
    s = c_q.shape[0]
    tm = _row_tile(s, 256)

    def body(cq_ref, ckv_ref, kr_ref, cos_ref, sa_ref, sb_ref, qln_ref, kvln_ref, qhn_ref, khn_ref,
             wuq_ref, wuk_ref, wuv_ref, q_ref, k_ref, v_ref, cqn_ref, ckvn_ref, q0_ref, k0_ref):
        cq = cq_ref[...]
        cqn = (cq * lax.rsqrt(jnp.mean(cq * cq, axis=-1, keepdims=True) + EPS) * qln_ref[...]).astype(BF16)
        cqn_ref[...] = cqn
        ckv = ckv_ref[...]
        ckvn = (ckv * lax.rsqrt(jnp.mean(ckv * ckv, axis=-1, keepdims=True) + EPS) * kvln_ref[...]).astype(BF16)
        ckvn_ref[...] = ckvn
        v_ref[...] = _dot(ckvn, wuv_ref[...]).astype(BF16)
        cos_t, sin_a, sin_b = cos_ref[...], sa_ref[...], sb_ref[...]
        kr = kr_ref[...]
        for h in range(MLA_HEADS):
            cols = slice(h * HEAD_PAD, (h + 1) * HEAD_PAD)
            q0 = _dot(cqn, wuq_ref[:, cols])
            q0_ref[:, cols] = q0
            rq = lax.rsqrt(jnp.sum(q0 * q0, axis=-1, keepdims=True) * (1.0 / MLA_QK_DIM) + EPS)
            q_ref[:, cols] = (_rope(q0 * rq * qhn_ref[...], cos_t, sin_a, sin_b) * MLA_SCALE_LOG2).astype(BF16)
            k0 = _dot(ckvn, wuk_ref[:, cols]) + kr
            k0_ref[:, cols] = k0
            rk = lax.rsqrt(jnp.sum(k0 * k0, axis=-1, keepdims=True) * (1.0 / MLA_QK_DIM) + EPS)
            k_ref[:, cols] = _rope(k0 * rk * khn_ref[...], cos_t, sin_a, sin_b).astype(BF16)

    return pl.pallas_call(
        body, name="mla_prep",
        grid=(s // tm,),
        in_specs=[_rows(tm, Q_LORA_RANK), _rows(tm, KV_LORA_RANK), _rows(tm, LANES),
                  _rows(tm, LANES), _rows(tm, LANES), _rows(tm, LANES),
                  _full((1, Q_LORA_RANK)), _full((1, KV_LORA_RANK)), _full((1, LANES)), _full((1, LANES)),
                  _full((Q_LORA_RANK, MLA_PAD_WIDTH)), _full((KV_LORA_RANK, MLA_PAD_WIDTH)),
                  _full((KV_LORA_RANK, MLA_WIDTH))],
        out_specs=[_rows(tm, MLA_PAD_WIDTH), _rows(tm, MLA_PAD_WIDTH), _rows(tm, MLA_WIDTH),
                   _rows(tm, Q_LORA_RANK), _rows(tm, KV_LORA_RANK),
                   _rows(tm, MLA_PAD_WIDTH), _rows(tm, MLA_PAD_WIDTH)],
        out_shape=[jax.ShapeDtypeStruct((s, MLA_PAD_WIDTH), BF16), jax.ShapeDtypeStruct((s, MLA_PAD_WIDTH), BF16),
                   jax.ShapeDtypeStruct((s, MLA_WIDTH), BF16),
                   jax.ShapeDtypeStruct((s, Q_LORA_RANK), BF16), jax.ShapeDtypeStruct((s, KV_LORA_RANK), BF16),
                   jax.ShapeDtypeStruct((s, MLA_PAD_WIDTH), F32), jax.ShapeDtypeStruct((s, MLA_PAD_WIDTH), F32)],
        compiler_params=_params(("arbitrary",), VMEM_LIMIT),
    )(c_q, c_kv, k_rope, cos_t, sin_a, sin_b, q_lora_norm, kv_lora_norm, qhn_pad, khn_pad,
      w_uq_bf, w_uk_bf, w_uv_bf)


def _log_sigmoid_pair(z):
    ls = jnp.minimum(z, 0.0) - jnp.log(1.0 + jnp.exp(-jnp.abs(z)))
    return ls, ls - z


def _sb_fwd_call(q, k, v):
    s = q.shape[0]
    bq = _row_tile(s, 256)
    nq = s // bq
    assert s >= 2 * bq

    def body(q_ref, k_ref, v_ref, o_ref, r_ref, ks_ref):
        hp, i = pl.program_id(0), pl.program_id(1)
        lane = lax.broadcasted_iota(I32, (bq, LANES), 1)
        row = lax.broadcasted_iota(I32, (bq, bq), 0)
        col = lax.broadcasted_iota(I32, (bq, bq), 1)
        strict = col < row
        later = jnp.where(row > col, 1.0, 0.0).astype(BF16)
        qs = q_ref[...] * jnp.asarray(SB_SCALE, BF16)
        qms = [qs * _head_mask(lane, hh).astype(BF16) for hh in range(2)]

        def walk(blocks, state):
            for kb, diagonal in blocks:
                rows = pl.ds(pl.multiple_of(kb * bq, bq), bq)
                kblk, vblk = k_ref[rows, :], v_ref[rows, :]
                new = []
                for hh in range(2):
                    run, acc = state[hh]
                    ls, lk = _log_sigmoid_pair(_dot_nt(qms[hh], kblk))
                    if diagonal:
                        lk = jnp.where(strict, lk, 0.0)
                    w = jnp.exp(ls + (_split_dot(lk, later) + run))
                    if diagonal:
                        w = jnp.where(strict, w, 0.0)
                    new.append((run + jnp.sum(lk, axis=1, keepdims=True), acc + _dot(w.astype(BF16), vblk)))
                state = tuple(new)
            return state

        def alive(state):
            return (jnp.maximum(jnp.max(state[0][0]), jnp.max(state[1][0])) > SB_DEAD).astype(I32)

        def finish(state, first):
            ks_ref[hp, i] = first
            o_ref[...] = jnp.where(lane < 64, state[0][1], state[1][1])
            r_ref[...] = jnp.where(lane < 64, state[0][0], state[1][0])

        zero = (jnp.zeros((bq, 1), F32), jnp.zeros((bq, LANES), F32))

        @pl.when(i == 0)
        def _():
            finish(walk([(0, True)], (zero, zero)), 0)

        @pl.when(i > 0)
        def _():
            state = walk([(i, True), (i - 1, False)], (zero, zero))

            def cond(carry):
                return jnp.logical_and(carry[0] >= 0, carry[1] > 0)

            def step(carry):
                state = walk([(carry[0], False)], carry[2])
                return carry[0] - 1, alive(state), state

            kb, _, state = lax.while_loop(cond, step, (i - 2, alive(state), state))
            finish(state, kb + 1)

    return pl.pallas_call(
        body, name="sb_fwd",
        grid=(4, nq),
        in_specs=[pl.BlockSpec((bq, LANES), lambda h, i: (i, h)),
                  pl.BlockSpec((s, LANES), lambda h, i: (0, h)),
                  pl.BlockSpec((s, LANES), lambda h, i: (0, h))],
        out_specs=[pl.BlockSpec((bq, LANES), lambda h, i: (i, h)),
                   pl.BlockSpec((bq, LANES), lambda h, i: (i, h)),
                   pl.BlockSpec(memory_space=pltpu.SMEM)],
        out_shape=[jax.ShapeDtypeStruct((s, SB_WIDTH), F32), jax.ShapeDtypeStruct((s, SB_WIDTH), F32),
                   jax.ShapeDtypeStruct((SB_HEADS // 2, nq), I32)],
        compiler_params=_params(("arbitrary", "arbitrary"), VMEM_LIMIT),
    )(q, k, v)


def _mla_fwd_call(q, k, v):
    s = q.shape[0]
    bq = _row_tile(s, MLA_BQ)
    bk = _row_tile(s, MLA_BK)
    nq = s // bq
    assert bk % bq == 0
    half = bk // 2

    def body(q_ref, k_ref, v_ref, o_ref, lse_ref, p_ref, s_ref):
        i = pl.program_id(1)
        lane = lax.broadcasted_iota(I32, (bq, LANES), 1)
        row = lax.broadcasted_iota(I32, (half, bq), 1)
        col = lax.broadcasted_iota(I32, (half, bq), 0)
        n_full = (i * bq) // bk

        def keys(g):
            return pl.ds(pl.multiple_of(g * half, half), half)

        def put_scores(g, slot):
            for hh in range(2):
                cols = slice(hh * HEAD_PAD, (hh + 1) * HEAD_PAD)
                s_ref[slot, hh] = _dot_nt(k_ref[keys(g), cols], q_ref[:, cols])

        def add_pv(carry, g, slot):
            vblk = v_ref[keys(g), :]
            return tuple((m, l, alpha * acc + _dot_tn(vblk, p_ref[slot, hh]), alpha)
                         for hh, (m, l, acc, alpha) in enumerate(carry))

        def substep(g, slot, carry, masked, prefetch):
            if prefetch:
                put_scores(g + 1, 1 - slot)
            carry = add_pv(carry, jnp.maximum(g - 1, 0), 1 - slot)
            new = []
            for hh in range(2):
                m, l, acc, _ = carry[hh]
                sc = s_ref[slot, hh]
                if masked:
                    sc = jnp.where(col + g * half <= row + i * bq, sc, MASK_NEG)
                m_new = jnp.maximum(m, jnp.max(sc, axis=0, keepdims=True))
                p = jnp.exp2(sc - m_new)
                alpha = jnp.exp2(m - m_new)
                l = alpha * l + jnp.sum(p, axis=0, keepdims=True)
                p_ref[slot, hh] = p.astype(BF16)
                new.append((m_new, l, acc, alpha))
            return tuple(new)

        def chunk(kb, carry, masked):
            carry = substep(2 * kb, 0, carry, masked, True)
            return substep(2 * kb + 1, 1, carry, masked, not masked)

        p_ref[1] = jnp.zeros_like(p_ref[1])
        put_scores(0, 0)
        one = (jnp.full((1, bq), MASK_NEG, F32), jnp.zeros((1, bq), F32), jnp.zeros((LANES, bq), F32),
               jnp.ones((1, bq), F32))
        carry = lax.fori_loop(0, n_full, lambda kb, cr: chunk(kb, cr, False), (one, one))
        carry = chunk(n_full, carry, True)
        (m0, l0, a0, _), (m1, l1, a1, _) = add_pv(carry, 2 * n_full + 1, 1)
        o_ref[...] = jnp.where(lane < 64, (a0 / l0).T, (a1 / l1).T)
        sub = lax.broadcasted_iota(I32, (8, bq), 0)
        lse_ref[...] = jnp.where(sub == 0, m0 + jnp.log2(l0), jnp.where(sub == 1, m1 + jnp.log2(l1), 0.0))

    return pl.pallas_call(
        body, name="mla_fwd",
        grid=(4, nq),
        in_specs=[pl.BlockSpec((bq, 2 * HEAD_PAD), lambda h, i: (i, h)),
                  pl.BlockSpec((s, 2 * HEAD_PAD), lambda h, i: (0, h)),
                  pl.BlockSpec((s, LANES), lambda h, i: (0, h))],
        out_specs=[pl.BlockSpec((bq, LANES), lambda h, i: (i, h)),
                   pl.BlockSpec((None, 8, bq), lambda h, i: (h, 0, i))],
        out_shape=[jax.ShapeDtypeStruct((s, MLA_WIDTH), F32), jax.ShapeDtypeStruct((4, 8, s), F32)],
        scratch_shapes=[pltpu.VMEM((2, 2, half, bq), BF16), pltpu.VMEM((2, 2, half, bq), F32)],
        compiler_params=_params(("arbitrary", "arbitrary"), VMEM_LIMIT),
    )(q, k, v)


def _out_call(o_sb, g_sb, o_mla, g_mla, x, target, gate, w_out_bf):
    s = x.shape[0]
    tm = _row_tile(s, 256)

    def body(osb_ref, gsb_ref, oml_ref, gml_ref, x_ref, t_ref, gate_ref, w_ref,
             dosb_ref, doml_ref, dgsb_ref, dgml_ref, dy_ref, gw_ref, dgate_ref, sq_ref):
        @pl.when(pl.program_id(0) == 0)
        def _():
            gw_ref[...] = jnp.zeros_like(gw_ref)
            dgate_ref[...] = jnp.zeros_like(dgate_ref)
            sq_ref[...] = jnp.zeros_like(sq_ref)

        g_s, g_m = gsb_ref[...], gml_ref[...]
        sig_s, sig_m = _sigmoid(g_s), _sigmoid(g_m)
        silu_s, silu_m = g_s * sig_s, g_m * sig_m
        o_s, o_m = osb_ref[...], oml_ref[...]
        mixed = jnp.concatenate([o_s * silu_s, o_m * silu_m], axis=1).astype(BF16)
        u = _dot(mixed, w_ref[...])
        gate_v = gate_ref[...]
        err = x_ref[...] + gate_v * u - t_ref[...]
        sq_ref[...] += jnp.sum(err * err, axis=0, keepdims=True)
        dy = err * (1.0 / D_MODEL)
        dy_ref[...] = dy
        dgate_ref[...] += jnp.sum(dy * u, axis=0, keepdims=True)
        du = (dy * gate_v).astype(BF16)
        gw_ref[...] += _dot_tn(mixed, du)
        dmix = _dot_nt(du, w_ref[...])
        dm_s, dm_m = dmix[:, :SB_WIDTH], dmix[:, SB_WIDTH:]
        dosb_ref[...] = (dm_s * silu_s).astype(BF16)
        doml_ref[...] = (dm_m * silu_m).astype(BF16)
        dgsb_ref[...] = (dm_s * o_s * (sig_s * (1.0 + g_s * (1.0 - sig_s)))).astype(BF16)
        dgml_ref[...] = (dm_m * o_m * (sig_m * (1.0 + g_m * (1.0 - sig_m)))).astype(BF16)

    return pl.pallas_call(
        body, name="out_proj_loss",
        grid=(s // tm,),
        in_specs=[_rows(tm, 512), _rows(tm, 512), _rows(tm, 512), _rows(tm, 512),
                  _rows(tm, D_MODEL), _rows(tm, D_MODEL), _full((1, D_MODEL)), _full((D_MODEL, D_MODEL))],
        out_specs=[_rows(tm, 512), _rows(tm, 512), _rows(tm, 512), _rows(tm, 512), _rows(tm, D_MODEL),
                   _full((D_MODEL, D_MODEL)), _full((1, D_MODEL)), _full((1, D_MODEL))],
        out_shape=[jax.ShapeDtypeStruct((s, 512), BF16)] * 4
        + [jax.ShapeDtypeStruct((s, D_MODEL), F32), jax.ShapeDtypeStruct((D_MODEL, D_MODEL), F32),
           jax.ShapeDtypeStruct((1, D_MODEL), F32), jax.ShapeDtypeStruct((1, D_MODEL), F32)],
        compiler_params=_params(("arbitrary",), VMEM_LIMIT),
    )(o_sb, g_sb, o_mla, g_mla, x, target, gate, w_out_bf)


def _head_mask(lane, hh):
    return jnp.where((lane >= 64) if hh else (lane < 64), 1.0, 0.0)


def _pick_lane(packed, lane, which):
    return jnp.sum(jnp.where(lane == which, packed, 0.0), axis=1, keepdims=True)


def _sb_bwd_call(kstart, q, k, v, do, rfin):
    s = q.shape[0]
    bq = _row_tile(s, 256)
    nq = s // bq

    def body(ks_ref, q_ref, k_ref, v_ref, do_ref, r_ref, dq_ref, dk_ref, dv_ref):
        hp, i = pl.program_id(0), pl.program_id(1)

        @pl.when(i == 0)
        def _():
            dk_ref[...] = jnp.zeros_like(dk_ref)
            dv_ref[...] = jnp.zeros_like(dv_ref)

        lane = lax.broadcasted_iota(I32, (bq, LANES), 1)
        row = lax.broadcasted_iota(I32, (bq, bq), 0)
        col = lax.broadcasted_iota(I32, (bq, bq), 1)
        upto = jnp.where(row <= col, 1.0, 0.0).astype(BF16)
        before = jnp.where(row < col, 1.0, 0.0).astype(BF16)
        qs = q_ref[...] * jnp.asarray(SB_SCALE, BF16)
        do2 = do_ref[...]
        r2 = r_ref[...]
        heads = [_head_mask(lane, hh).astype(BF16) for hh in range(2)]
        qms = [qs * head for head in heads]
        doms = [do2 * head for head in heads]
        totals = [_pick_lane(r2, lane, 64 * hh) for hh in range(2)]

        strict = col < row

        def walk(blocks, state):
            for kb, diagonal in blocks:
                rows = pl.ds(pl.multiple_of(kb * bq, bq), bq)
                kblk, vblk = k_ref[rows, :], v_ref[rows, :]
                new, dk_parts, dv_parts = [], [], []
                for hh in range(2):
                    pre, gpre, dq = state[hh]
                    ls, lk = _log_sigmoid_pair(_dot_nt(qms[hh], kblk))
                    if diagonal:
                        lk = jnp.where(strict, lk, 0.0)
                    w = jnp.exp(ls + ((totals[hh] - pre) - _split_dot(lk, upto)))
                    if diagonal:
                        w = jnp.where(strict, w, 0.0)
                    g = w * _dot_nt(doms[hh], vblk)
                    dz = g - jnp.exp(ls) * (g + (gpre + _split_dot(g, before)))
                    if diagonal:
                        dz = jnp.where(strict, dz, 0.0)
                    dzb = dz.astype(BF16)
                    dk_parts.append(_dot_tn(dzb, qms[hh]))
                    dv_parts.append(_dot_tn(w.astype(BF16), doms[hh]))
                    new.append((pre + jnp.sum(lk, axis=1, keepdims=True), gpre + jnp.sum(g, axis=1, keepdims=True),
                                dq + _dot(dzb, kblk)))
                dk_ref[rows, :] += dk_parts[0] + dk_parts[1]
                dv_ref[rows, :] += dv_parts[0] + dv_parts[1]
                state = tuple(new)
            return state

        def finish(state):
            dq_ref[...] = (jnp.where(lane < 64, state[0][2], state[1][2]) * SB_SCALE).astype(BF16)

        zero = (jnp.zeros((bq, 1), F32), jnp.zeros((bq, 1), F32), jnp.zeros((bq, LANES), F32))

        @pl.when(i == 0)
        def _():
            finish(walk([(0, True)], (zero, zero)))

        @pl.when(i > 0)
        def _():
            state = lax.fori_loop(ks_ref[hp, i], i - 1, lambda kb, st: walk([(kb, False)], st), (zero, zero))
            finish(walk([(i - 1, False), (i, True)], state))

    return pl.pallas_call(
        body, name="sb_bwd",
        grid_spec=pltpu.PrefetchScalarGridSpec(
            num_scalar_prefetch=1, grid=(4, nq),
            in_specs=[pl.BlockSpec((bq, LANES), lambda h, i, ks: (i, h)),
                      pl.BlockSpec((s, LANES), lambda h, i, ks: (0, h)),
                      pl.BlockSpec((s, LANES), lambda h, i, ks: (0, h)),
                      pl.BlockSpec((bq, LANES), lambda h, i, ks: (i, h)),
                      pl.BlockSpec((bq, LANES), lambda h, i, ks: (i, h))],
            out_specs=[pl.BlockSpec((bq, LANES), lambda h, i, ks: (i, h)),
                       pl.BlockSpec((s, LANES), lambda h, i, ks: (0, h)),
                       pl.BlockSpec((s, LANES), lambda h, i, ks: (0, h))]),
        out_shape=[jax.ShapeDtypeStruct((s, SB_WIDTH), BF16), jax.ShapeDtypeStruct((s, SB_WIDTH), F32),
                   jax.ShapeDtypeStruct((s, SB_WIDTH), F32)],
        compiler_params=_params(("arbitrary", "arbitrary"), VMEM_LIMIT),
    )(kstart, q, k, v, do, rfin)


def _mla_bwd_call(q, k, v, do, o, lse):
    s = q.shape[0]
    bq = _row_tile(s, MLA_BWD_BQ)
    bk = _row_tile(s, MLA_BWD_BK)
    nq = s // bq
    assert bk % bq == 0
    half = bk // 2

    def body(q_ref, k_ref, v_ref, do_ref, o_ref, lse_ref, dq_ref, dk_ref, dv_ref, dom_ref, s_ref, dp_ref, pb_ref,
             ds_ref):
        i = pl.program_id(1)

        @pl.when(i == 0)
        def _():
            dk_ref[...] = jnp.zeros_like(dk_ref)
            dv_ref[...] = jnp.zeros_like(dv_ref)

        lane = lax.broadcasted_iota(I32, (bq, LANES), 1)
        row = lax.broadcasted_iota(I32, (half, bq), 1)
        col = lax.broadcasted_iota(I32, (half, bq), 0)
        n_full = (i * bq) // bk
        do2 = do_ref[...]
        prod = do2.astype(F32) * o_ref[...]
        ones = jnp.ones((8, LANES), BF16)
        deltas, lses = [], []
        for hh in range(2):
            head = _head_mask(lane, hh)
            dom_ref[hh] = do2 * head.astype(BF16)
            part = prod * head
            hi = part.astype(BF16)
            lo = (part - hi.astype(F32)).astype(BF16)
            deltas.append((_dot_nt(ones, hi) + _dot_nt(ones, lo))[0:1])
            lses.append(lse_ref[hh:hh + 1, :])

        def keys(g):
            return pl.ds(pl.multiple_of(g * half, half), half)

        def heads():
            return [(hh, slice(hh * HEAD_PAD, (hh + 1) * HEAD_PAD)) for hh in range(2)]

        def put_products(g, slot):
            vblk = v_ref[keys(g), :]
            for hh, cols in heads():
                s_ref[slot, hh] = _dot_nt(k_ref[keys(g), cols], q_ref[:, cols])
                dp_ref[slot, hh] = _dot_nt(vblk, dom_ref[hh])

        def add_grads(dqs, g, slot):
            rows = keys(g)
            new, dv_parts = [], []
            for hh, cols in heads():
                ds = ds_ref[slot, hh]
                dk_ref[rows, cols] += _dot(ds, q_ref[:, cols])
                dv_parts.append(_dot(pb_ref[slot, hh], dom_ref[hh]))
                new.append(dqs[hh] + _dot_tn(k_ref[rows, cols], ds))
            dv_ref[rows, :] += dv_parts[0] + dv_parts[1]
            return tuple(new)

        def substep(g, slot, dqs, masked, prefetch):
            if prefetch:
                put_products(g + 1, 1 - slot)
            dqs = add_grads(dqs, jnp.maximum(g - 1, 0), 1 - slot)
            for hh, _ in heads():
                p = jnp.exp2(s_ref[slot, hh] - lses[hh])
                if masked:
                    p = jnp.where(col + g * half <= row + i * bq, p, 0.0)
                ds_ref[slot, hh] = (p * (dp_ref[slot, hh] - deltas[hh])).astype(BF16)
                pb_ref[slot, hh] = p.astype(BF16)
            return dqs

        def chunk(kb, dqs, masked):
            dqs = substep(2 * kb, 0, dqs, masked, True)
            return substep(2 * kb + 1, 1, dqs, masked, not masked)

        ds_ref[1] = jnp.zeros_like(ds_ref[1])
        pb_ref[1] = jnp.zeros_like(pb_ref[1])
        put_products(0, 0)
        zero = jnp.zeros((HEAD_PAD, bq), F32)
        dqs = lax.fori_loop(0, n_full, lambda kb, dqs: chunk(kb, dqs, False), (zero, zero))
        dqs = chunk(n_full, dqs, True)
        dqs = add_grads(dqs, 2 * n_full + 1, 1)
        dq_ref[:, :HEAD_PAD] = dqs[0].T * MLA_SCALE
        dq_ref[:, HEAD_PAD:] = dqs[1].T * MLA_SCALE

    return pl.pallas_call(
        body, name="mla_bwd",
        grid=(4, nq),
        in_specs=[pl.BlockSpec((bq, 2 * HEAD_PAD), lambda h, i: (i, h)),
                  pl.BlockSpec((s, 2 * HEAD_PAD), lambda h, i: (0, h)),
                  pl.BlockSpec((s, LANES), lambda h, i: (0, h)),
                  pl.BlockSpec((bq, LANES), lambda h, i: (i, h)),
                  pl.BlockSpec((bq, LANES), lambda h, i: (i, h)),
                  pl.BlockSpec((None, 8, bq), lambda h, i: (h, 0, i))],
        out_specs=[pl.BlockSpec((bq, 2 * HEAD_PAD), lambda h, i: (i, h)),
                   pl.BlockSpec((s, 2 * HEAD_PAD), lambda h, i: (0, h)),
                   pl.BlockSpec((s, LANES), lambda h, i: (0, h))],
        out_shape=[jax.ShapeDtypeStruct((s, MLA_PAD_WIDTH), F32), jax.ShapeDtypeStruct((s, MLA_PAD_WIDTH), F32),
                   jax.ShapeDtypeStruct((s, MLA_WIDTH), F32)],
        scratch_shapes=[pltpu.VMEM((2, bq, LANES), BF16),
                        pltpu.VMEM((2, 2, half, bq), F32), pltpu.VMEM((2, 2, half, bq), F32),
                        pltpu.VMEM((2, 2, half, bq), BF16), pltpu.VMEM((2, 2, half, bq), BF16)],
        compiler_params=_params(("arbitrary", "arbitrary"), VMEM_LIMIT),
    )(q, k, v, do, o, lse)


def _rms_bwd(d_out, inp, r, weight, n):
    normed = inp * r
    gw = d_out * weight
    d_in = r * (gw - normed * (jnp.sum(gw * normed, axis=-1, keepdims=True) * (1.0 / n)))
    return d_in, d_out * normed


def _mla_prep_bwd_call(dq, dk, dv, q0, k0, cqn, ckvn, c_q, c_kv, cos_t, sin_a, sin_b,
                       q_lora_norm, kv_lora_norm, qhn_pad, khn_pad, w_uq_bf, w_uk_bf, w_uv_bf):
    s = dq.shape[0]
    tm = _row_tile(s, 256)

    def body(dq_ref, dk_ref, dv_ref, q0_ref, k0_ref, cqn_ref, ckvn_ref, cq_ref, ckv_ref,
             cos_ref, sa_ref, sb_ref, qln_ref, kvln_ref, qhn_ref, khn_ref, wuq_ref, wuk_ref, wuv_ref,
             dcq_ref, dckv_ref, dkr_ref, gwuq_ref, gwuk_ref, gwuv_ref, gqln_ref, gkvln_ref, gqhn_ref, gkhn_ref):
        @pl.when(pl.program_id(0) == 0)
        def _():
            for ref in (gwuq_ref, gwuk_ref, gwuv_ref, gqln_ref, gkvln_ref, gqhn_ref, gkhn_ref):
                ref[...] = jnp.zeros_like(ref)

        cos_t, sin_a, sin_b = cos_ref[...], sa_ref[...], sb_ref[...]
        lane = lax.broadcasted_iota(I32, (tm, LANES), 1)
        rope_lanes = jnp.logical_and(lane >= KR_LANE, lane < KR_LANE + MLA_ROPE_DIM)
        cqn, ckvn = cqn_ref[...], ckvn_ref[...]
        d_cqn = jnp.zeros((tm, Q_LORA_RANK), F32)
        d_ckvn = jnp.zeros((tm, KV_LORA_RANK), F32)
        d_kr = jnp.zeros((tm, LANES), F32)
        g_qhn = jnp.zeros((1, LANES), F32)
        g_khn = jnp.zeros((1, LANES), F32)
        for h in range(MLA_HEADS):
            cols = slice(h * HEAD_PAD, (h + 1) * HEAD_PAD)
            q0 = q0_ref[:, cols]
            rq = lax.rsqrt(jnp.sum(q0 * q0, axis=-1, keepdims=True) * (1.0 / MLA_QK_DIM) + EPS)
            d_q0, gq = _rms_bwd(_rope_adjoint(dq_ref[:, cols], cos_t, sin_a, sin_b), q0, rq, qhn_ref[...],
                                MLA_QK_DIM)
            g_qhn += jnp.sum(gq, axis=0, keepdims=True)
            d_q0b = d_q0.astype(BF16)
            d_cqn += _dot_nt(d_q0b, wuq_ref[:, cols])
            gwuq_ref[:, cols] += _dot_tn(cqn, d_q0b)
            k0 = k0_ref[:, cols]
            rk = lax.rsqrt(jnp.sum(k0 * k0, axis=-1, keepdims=True) * (1.0 / MLA_QK_DIM) + EPS)
            d_k0, gk = _rms_bwd(_rope_adjoint(dk_ref[:, cols] * LN2, cos_t, sin_a, sin_b), k0, rk, khn_ref[...],
                                MLA_QK_DIM)
            g_khn += jnp.sum(gk, axis=0, keepdims=True)
            d_kr += jnp.where(rope_lanes, d_k0, 0.0)
            d_k0b = d_k0.astype(BF16)
            d_ckvn += _dot_nt(d_k0b, wuk_ref[:, cols])
            gwuk_ref[:, cols] += _dot_tn(ckvn, d_k0b)
        dvb = dv_ref[...].astype(BF16)
        d_ckvn += _dot_nt(dvb, wuv_ref[...])
        gwuv_ref[...] += _dot_tn(ckvn, dvb)
        gqhn_ref[...] += g_qhn
        gkhn_ref[...] += g_khn
        dkr_ref[...] = d_kr.astype(BF16)
        cq = cq_ref[...]
        rcq = lax.rsqrt(jnp.mean(cq * cq, axis=-1, keepdims=True) + EPS)
        d_cq, gl = _rms_bwd(d_cqn, cq, rcq, qln_ref[...], Q_LORA_RANK)
        dcq_ref[...] = d_cq.astype(BF16)
        gqln_ref[...] += jnp.sum(gl, axis=0, keepdims=True)
        ckv = ckv_ref[...]
        rckv = lax.rsqrt(jnp.mean(ckv * ckv, axis=-1, keepdims=True) + EPS)
        d_ckv, gl = _rms_bwd(d_ckvn, ckv, rckv, kvln_ref[...], KV_LORA_RANK)
        dckv_ref[...] = d_ckv.astype(BF16)
        gkvln_ref[...] += jnp.sum(gl, axis=0, keepdims=True)

    return pl.pallas_call(
        body, name="mla_prep_bwd",
        grid=(s // tm,),
        in_specs=[_rows(tm, MLA_PAD_WIDTH), _rows(tm, MLA_PAD_WIDTH), _rows(tm, MLA_WIDTH),
                  _rows(tm, MLA_PAD_WIDTH), _rows(tm, MLA_PAD_WIDTH),
                  _rows(tm, Q_LORA_RANK), _rows(tm, KV_LORA_RANK), _rows(tm, Q_LORA_RANK), _rows(tm, KV_LORA_RANK),
                  _rows(tm, LANES), _rows(tm, LANES), _rows(tm, LANES),
                  _full((1, Q_LORA_RANK)), _full((1, KV_LORA_RANK)), _full((1, LANES)), _full((1, LANES)),
                  _full((Q_LORA_RANK, MLA_PAD_WIDTH)), _full((KV_LORA_RANK, MLA_PAD_WIDTH)),
                  _full((KV_LORA_RANK, MLA_WIDTH))],
        out_specs=[_rows(tm, Q_LORA_RANK), _rows(tm, KV_LORA_RANK), _rows(tm, LANES),
                   _full((Q_LORA_RANK, MLA_PAD_WIDTH)), _full((KV_LORA_RANK, MLA_PAD_WIDTH)),
                   _full((KV_LORA_RANK, MLA_WIDTH)),
                   _full((1, Q_LORA_RANK)), _full((1, KV_LORA_RANK)), _full((1, LANES)), _full((1, LANES))],
        out_shape=[jax.ShapeDtypeStruct((s, Q_LORA_RANK), BF16), jax.ShapeDtypeStruct((s, KV_LORA_RANK), BF16),
                   jax.ShapeDtypeStruct((s, LANES), BF16),
                   jax.ShapeDtypeStruct((Q_LORA_RANK, MLA_PAD_WIDTH), F32),
                   jax.ShapeDtypeStruct((KV_LORA_RANK, MLA_PAD_WIDTH), F32),
                   jax.ShapeDtypeStruct((KV_LORA_RANK, MLA_WIDTH), F32),
                   jax.ShapeDtypeStruct((1, Q_LORA_RANK), F32), jax.ShapeDtypeStruct((1, KV_LORA_RANK), F32),
                   jax.ShapeDtypeStruct((1, LANES), F32), jax.ShapeDtypeStruct((1, LANES), F32)],
        compiler_params=_params(("arbitrary",), VMEM_LIMIT),
    )(dq, dk, dv, q0, k0, cqn, ckvn, c_q, c_kv, cos_t, sin_a, sin_b,
      q_lora_norm, kv_lora_norm, qhn_pad, khn_pad, w_uq_bf, w_uk_bf, w_uv_bf)


def _dh_call(pieces, x, dy, shift, scale, norm_w, w_in_bf):
    s = x.shape[0]
    tm = _row_tile(s, 256)
    widths = [p.shape[1] for p in pieces]
    offsets = [sum(widths[:j]) for j in range(len(widths))]
    assert offsets[-1] + widths[-1] == IN_COLS_PAD
    n = len(pieces)

    def body(*refs):
        p_refs = refs[:n]
        x_ref, dy_ref, sh_ref, sc_ref, nw_ref, w_ref, gx_ref, dp_ref, dsh_ref, dsc_ref, gnw_ref = refs[n:]

        @pl.when(pl.program_id(0) == 0)
        def _():
            dsh_ref[...] = jnp.zeros_like(dsh_ref)
            dsc_ref[...] = jnp.zeros_like(dsc_ref)
            gnw_ref[...] = jnp.zeros_like(gnw_ref)

        for p_ref, c0, width in zip(p_refs, offsets, widths):
            dp_ref[:, c0:c0 + width] = p_ref[...].astype(BF16)
        dh = _dot_nt(dp_ref[...], w_ref[...])
        xx = x_ref[...]
        r0 = lax.rsqrt(jnp.mean(xx * xx, axis=-1, keepdims=True) + EPS)
        xn = xx * r0
        nw = nw_ref[...]
        dsh_ref[...] += jnp.sum(dh, axis=0, keepdims=True)
        dsc_ref[...] += jnp.sum(dh * (xn * nw), axis=0, keepdims=True)
        dn = dh * (1.0 + sc_ref[...])
        gnw_ref[...] += jnp.sum(dn * xn, axis=0, keepdims=True)
        dxn = dn * nw
        gx_ref[...] = dy_ref[...] + r0 * (dxn - xn * jnp.mean(dxn * xn, axis=-1, keepdims=True))

    return pl.pallas_call(
        body, name="in_proj_bwd",
        grid=(s // tm,),
        in_specs=[_rows(tm, w) for w in widths]
        + [_rows(tm, D_MODEL), _rows(tm, D_MODEL), _full((1, D_MODEL)), _full((1, D_MODEL)), _full((1, D_MODEL)),
           _full((D_MODEL, IN_COLS_PAD))],
        out_specs=[_rows(tm, D_MODEL), _rows(tm, IN_COLS_PAD),
                   _full((1, D_MODEL)), _full((1, D_MODEL)), _full((1, D_MODEL))],
        out_shape=[jax.ShapeDtypeStruct((s, D_MODEL), F32), jax.ShapeDtypeStruct((s, IN_COLS_PAD), BF16),
                   jax.ShapeDtypeStruct((1, D_MODEL), F32), jax.ShapeDtypeStruct((1, D_MODEL), F32),
                   jax.ShapeDtypeStruct((1, D_MODEL), F32)],
        compiler_params=_params(("arbitrary",), VMEM_LIMIT),
    )(*pieces, x, dy, shift, scale, norm_w, w_in_bf)


def _gw_in_call(hb, dproj):
    s = hb.shape[0]
    tk = _row_tile(s, 512)
    tn = IN_COLS_PAD // 2

    def body(h_ref, d_ref, g_ref):
        @pl.when(pl.program_id(1) == 0)
        def _():
            g_ref[...] = jnp.zeros_like(g_ref)

        g_ref[...] += _dot_tn(h_ref[...], d_ref[...])

    return pl.pallas_call(
        body, name="in_proj_wgrad",
        grid=(2, s // tk),
        in_specs=[pl.BlockSpec((tk, D_MODEL), lambda j, t: (t, 0)), pl.BlockSpec((tk, tn), lambda j, t: (t, j))],
        out_specs=pl.BlockSpec((D_MODEL, tn), lambda j, t: (0, j)),
        out_shape=jax.ShapeDtypeStruct((D_MODEL, IN_COLS_PAD), F32),
        compiler_params=_params(("arbitrary", "arbitrary"), VMEM_LIMIT),
    )(hb, dproj)


def _adamw(g, w, m, v):
    m = ADAM_B1 * m + (1.0 - ADAM_B1) * g
    v = ADAM_B2 * v + (1.0 - ADAM_B2) * (g * g)
    m_hat = m / (1.0 - ADAM_B1 ** ADAM_STEP)
    v_hat = v / (1.0 - ADAM_B2 ** ADAM_STEP)
    delta = -ADAM_LR * (m_hat / (jnp.sqrt(v_hat) + ADAM_EPS) + ADAM_WD * w)
    return delta, m, v


def _adam_shard_call(name, own, sib, w, m, v):
    r, c = w.shape
    tr = r if r <= 512 else 256

    def body(own_ref, sib_ref, w_ref, m_ref, v_ref, g_ref, d_ref, nm_ref, nv_ref):
        a = ((own_ref[0].astype(F32) + own_ref[1].astype(F32)) + own_ref[2].astype(F32)) + own_ref[3].astype(F32)
        b = ((sib_ref[0].astype(F32) + sib_ref[1].astype(F32)) + sib_ref[2].astype(F32)) + sib_ref[3].astype(F32)
        g = a + b
        g_ref[...] = g
        d_ref[...], nm_ref[...], nv_ref[...] = _adamw(g, w_ref[...], m_ref[...], v_ref[...])

    part = pl.BlockSpec((4, tr, c), lambda i: (0, i, 0))
    blk = pl.BlockSpec((tr, c), lambda i: (i, 0))
    return pl.pallas_call(
        body, name=name,
        grid=(r // tr,),
        in_specs=[part, part, blk, blk, blk],
        out_specs=[blk] * 4,
        out_shape=[jax.ShapeDtypeStruct((r, c), F32)] * 4,
        compiler_params=_params(("arbitrary",), VMEM_LIMIT),
    )(own, sib, w, m, v)


def _adam_ada_call(c_all, d_all, w, m, v):
    r, c = w.shape
    tr = 256

    def body(c_ref, d_ref, w_ref, m_ref, v_ref, g_ref, dl_ref, nm_ref, nv_ref):
        cc = c_ref[...]
        sc = cc * _sigmoid(cc)
        dd = d_ref[...]
        sc_hi = sc.astype(BF16)
        sc_lo = (sc - sc_hi.astype(F32)).astype(BF16)
        dd_hi = dd.astype(BF16)
        dd_lo = (dd - dd_hi.astype(F32)).astype(BF16)
        g = _dot_tn(sc_hi, dd_hi) + (_dot_tn(sc_hi, dd_lo) + _dot_tn(sc_lo, dd_hi))
        g_ref[...] = g
        dl_ref[...], nm_ref[...], nv_ref[...] = _adamw(g, w_ref[...], m_ref[...], v_ref[...])

    blk = pl.BlockSpec((tr, c), lambda i: (i, 0))
    return pl.pallas_call(
        body, name="adam_w_ada",
        grid=(r // tr,),
        in_specs=[pl.BlockSpec((16, tr), lambda i: (0, i)), pl.BlockSpec((16, c), lambda i: (0, 0)), blk, blk, blk],
        out_specs=[blk] * 4,
        out_shape=[jax.ShapeDtypeStruct((r, c), F32)] * 4,
        compiler_params=_params(("arbitrary",), VMEM_LIMIT),
    )(c_all, d_all, w, m, v)


def _adam_small_call(packs, w, m, v, first_row):
    rows = w.shape[0]

    def body(p_ref, w_ref, m_ref, v_ref, g_ref, d_ref, nm_ref, nv_ref):
        g = p_ref[0, first_row:first_row + rows, :]
        for b in range(1, 8):
            g = g + p_ref[b, first_row:first_row + rows, :]
        g_ref[...] = g
        d_ref[...], nm_ref[...], nv_ref[...] = _adamw(g, w_ref[...], m_ref[...], v_ref[...])

    return pl.pallas_call(
        body, name="adam_vectors",
        out_shape=[jax.ShapeDtypeStruct((rows, LANES), F32)] * 4,
    )(packs, w, m, v)


def _rope_tables(positions):
    inv_freq = ROPE_THETA ** (-jnp.arange(0, MLA_ROPE_DIM, 2, dtype=F32) / MLA_ROPE_DIM)
    ang = positions.astype(F32)[:, None] * inv_freq
    cos, sin = jnp.cos(ang), jnp.sin(ang)
    s = positions.shape[0]
    half = MLA_ROPE_DIM // 2
    zeros = lambda n: jnp.zeros((s, n), F32)
    cos_t = jnp.concatenate([jnp.ones((s, MLA_NOPE_DIM), F32), cos, cos, zeros(HEAD_PAD - MLA_QK_DIM)], axis=1)
    sin_a = jnp.concatenate([zeros(MLA_NOPE_DIM), -sin, zeros(half + HEAD_PAD - MLA_QK_DIM)], axis=1)
    sin_b = jnp.concatenate([zeros(MLA_NOPE_DIM + half), sin, zeros(HEAD_PAD - MLA_QK_DIM)], axis=1)
    return cos_t, sin_a, sin_b


def _unshard_cols(g):
    return jnp.transpose(g, (1, 0, 2)).reshape(g.shape[1], 4 * g.shape[2])


def _shard_cols(g):
    r, c4 = g.shape
    return jnp.transpose(g.reshape(r, 4, c4 // 4), (1, 0, 2))


def _pad_heads(w, width):
    r = w.shape[0]
    w = w.reshape(r, MLA_HEADS, width)
    return jnp.pad(w, ((0, 0), (0, 0), (0, HEAD_PAD - width))).reshape(r, MLA_PAD_WIDTH)


def _pad_lanes(vec):
    return jnp.pad(vec, ((0, 0), (0, LANES - vec.shape[1])))


def _vector_rows(vecs):
    rows = []
    for vec in vecs:
        n = vec.shape[1]
        pad = (-n) % LANES
        rows.append(jnp.pad(vec, ((0, 0), (0, pad))).reshape((n + pad) // LANES, LANES))
    return jnp.concatenate(rows, axis=0)


def kernel(x, c, positions, w_ada, b_ada, norm_w, w_in, q_lora_norm, w_uq, kv_lora_norm, w_ukv, q_head_norm, k_head_norm, w_out, loss_target, m_w_ada, m_b_ada, m_norm_w, m_w_in, m_q_lora_norm, m_w_uq, m_kv_lora_norm, m_w_ukv, m_q_head_norm, m_k_head_norm, m_w_out, v_w_ada, v_b_ada, v_norm_w, v_w_in, v_q_lora_norm, v_w_uq, v_kv_lora_norm, v_w_ukv, v_q_head_norm, v_k_head_norm, v_w_out):
    wa_g, win_g, wuq_g, wukv_g, wout_g = _gather_call([w.astype(BF16) for w in (w_ada, w_in, w_uq, w_ukv, w_out)])
    (sq_sum, grad_x, g_w_in, g_w_uq, g_w_ukv, g_w_out, d_ada, g_norm_w, g_qln, g_kvln, g_qhn, g_khn) = _local_step(
        x[0], c, positions[0], loss_target[0], _unshard_cols(wa_g), b_ada, norm_w, _unshard_cols(win_g),
        q_lora_norm, _unshard_cols(wuq_g), kv_lora_norm, _unshard_cols(wukv_g), q_head_norm, k_head_norm,
        wout_g.reshape(D_MODEL, D_MODEL))
    loss = lax.psum(0.5 * sq_sum / D_MODEL, ("x", "y", "c"))

    grads = [g.astype(BF16) for g in (_shard_cols(g_w_in), _shard_cols(g_w_uq), _shard_cols(g_w_ukv),
                                      g_w_out.reshape(4, D_MODEL // 4, D_MODEL))]
    small = _vector_rows([c, d_ada, g_norm_w, g_qln, g_kvln, g_qhn, g_khn])
    small = jnp.pad(small, ((0, (-small.shape[0]) % 8), (0, 0)))
    own, sib, packs = _exchange_call(grads, small)

    names = ["adam_w_in", "adam_w_uq", "adam_w_ukv", "adam_w_out"]
    shard_w = [(w_in, m_w_in, v_w_in), (w_uq, m_w_uq, v_w_uq), (w_ukv, m_w_ukv, v_w_ukv),
               (w_out, m_w_out, v_w_out)]
    res = {}
    for name, o_g, s_g, (w, m, v) in zip(names, own, sib, shard_w):
        res[name] = _adam_shard_call(name, o_g, s_g, w[0], m[0], v[0])
    chip = 2 * lax.axis_index("x") + lax.axis_index("y")
    ada_cols = w_ada.shape[2]
    c_rows = D_MODEL // LANES
    c_all = jnp.pad(packs[:, :c_rows, :].reshape(8, D_MODEL), ((0, 8), (0, 0)))
    d_rows = packs[:, c_rows:c_rows + 3 * c_rows, :].reshape(8, 3 * D_MODEL)
    d_all = jnp.pad(lax.dynamic_slice_in_dim(d_rows, chip * ada_cols, ada_cols, axis=1), ((0, 8), (0, 0)))
    res_ada = _adam_ada_call(c_all, d_all, w_ada[0], m_w_ada[0], v_w_ada[0])
    vec_names = [(b_ada, m_b_ada, v_b_ada), (norm_w, m_norm_w, v_norm_w), (q_lora_norm, m_q_lora_norm, v_q_lora_norm),
                 (kv_lora_norm, m_kv_lora_norm, v_kv_lora_norm), (q_head_norm, m_q_head_norm, v_q_head_norm),
                 (k_head_norm, m_k_head_norm, v_k_head_norm)]
    packed = [_vector_rows([t[j] for t in vec_names]) for j in range(3)]
    n_rows = packed[0].shape[0]
    pad_rows = (-n_rows) % 8
    packed = [jnp.pad(p, ((0, pad_rows), (0, 0))) for p in packed]
    res_vec = _adam_small_call(packs, packed[0], packed[1], packed[2], c_rows)

    def unpack(arr):
        outs, r0 = [], 0
        for t in vec_names:
            n = t[0].shape[1]
            nr = -(-n // LANES)
            outs.append(arr[r0:r0 + nr].reshape(1, nr * LANES)[:, :n])
            r0 += nr
        return outs

    vec_out = [unpack(a) for a in res_vec]

    def ordered(kind):
        big = lambda name: res[name][kind][None]
        return [res_ada[kind][None], vec_out[kind][0], vec_out[kind][1], big("adam_w_in"), vec_out[kind][2],
                big("adam_w_uq"), vec_out[kind][3], big("adam_w_ukv"), vec_out[kind][4], vec_out[kind][5],
                big("adam_w_out")]

    return (loss, grad_x[None], *ordered(0), *ordered(1), *ordered(2), *ordered(3))


def _local_step(x2, c, positions, tgt, w_ada_full, b_ada, norm_w, w_in_full, q_lora_norm, w_uq_full,
                kv_lora_norm, w_ukv_full, q_head_norm, k_head_norm, w_out_full):
    w_ada_bf = w_ada_full.astype(BF16)
    kr_block = jnp.pad(w_in_full[:, 2688:2720], ((0, 0), (KR_LANE, LANES - KR_LANE - MLA_ROPE_DIM)))
    w_in_bf = jnp.concatenate([w_in_full[:, :2688], w_in_full[:, 2720:], kr_block], axis=1).astype(BF16)
    w_uq_bf = _pad_heads(w_uq_full, MLA_QK_DIM).astype(BF16)
    w_ukv_heads = w_ukv_full.reshape(KV_LORA_RANK, MLA_HEADS, 2 * MLA_NOPE_DIM)
    w_uk_bf = _pad_heads(w_ukv_heads[:, :, :MLA_NOPE_DIM].reshape(KV_LORA_RANK, -1), MLA_NOPE_DIM).astype(BF16)
    w_uv_bf = w_ukv_heads[:, :, MLA_NOPE_DIM:].reshape(KV_LORA_RANK, MLA_WIDTH).astype(BF16)
    w_out_bf = w_out_full.astype(BF16)
    qhn_pad, khn_pad = _pad_lanes(q_head_norm), _pad_lanes(k_head_norm)
    cos_t, sin_a, sin_b = _rope_tables(positions)

    ada = _ada_call(c, w_ada_bf, b_ada)
    shift, scale, gate = ada[:, :D_MODEL], ada[:, D_MODEL:2 * D_MODEL], ada[:, 2 * D_MODEL:]
    hb, q_sb, k_sb, v_sb, g_sb, c_q, c_kv, g_mla, k_rope = _pre_call(x2, shift, scale, norm_w, w_in_bf)
    q_m, k_m, v_m, cqn, ckvn, q0, k0 = _mla_prep_call(
        c_q, c_kv, k_rope, cos_t, sin_a, sin_b, q_lora_norm, kv_lora_norm, qhn_pad, khn_pad,
        w_uq_bf, w_uk_bf, w_uv_bf)
    o_sb, r_sb, kstart = _sb_fwd_call(q_sb, k_sb, v_sb)
    o_mla, lse = _mla_fwd_call(q_m, k_m, v_m)
    do_sb, do_mla, dg_sb, dg_mla, dy, g_w_out, d_gate, sq = _out_call(
        o_sb, g_sb, o_mla, g_mla, x2, tgt, gate, w_out_bf)

    dq_sb, dk_sb, dv_sb = _sb_bwd_call(kstart, q_sb, k_sb, v_sb, do_sb, r_sb)
    dq_m, dk_m, dv_m = _mla_bwd_call(q_m, k_m, v_m, do_mla, o_mla, lse)
    (d_cq, d_ckv, d_kr, g_wuq_pad, g_wuk_pad, g_wuv, g_qln, g_kvln, g_qhn, g_khn) = _mla_prep_bwd_call(
        dq_m, dk_m, dv_m, q0, k0, cqn, ckvn, c_q, c_kv, cos_t, sin_a, sin_b,
        q_lora_norm, kv_lora_norm, qhn_pad, khn_pad, w_uq_bf, w_uk_bf, w_uv_bf)
    grad_x, dproj, d_shift, d_scale, g_norm_w = _dh_call(
        [dq_sb, dk_sb, dv_sb, dg_sb, d_cq, d_ckv, dg_mla, d_kr], x2, dy, shift, scale, norm_w, w_in_bf)
    g_win_pad = _gw_in_call(hb, dproj)

    g_w_in = jnp.concatenate([g_win_pad[:, :2688],
                              g_win_pad[:, C_KR + KR_LANE:C_KR + KR_LANE + MLA_ROPE_DIM],
                              g_win_pad[:, 2688:3200]], axis=1)
    g_w_uq = g_wuq_pad.reshape(Q_LORA_RANK, MLA_HEADS, HEAD_PAD)[:, :, :MLA_QK_DIM].reshape(Q_LORA_RANK, -1)
    g_w_ukv = jnp.concatenate(
        [g_wuk_pad.reshape(KV_LORA_RANK, MLA_HEADS, HEAD_PAD)[:, :, :MLA_NOPE_DIM],
         g_wuv.reshape(KV_LORA_RANK, MLA_HEADS, MLA_NOPE_DIM)], axis=2).reshape(KV_LORA_RANK, -1)
    d_ada = jnp.concatenate([d_shift, d_scale, d_gate], axis=1)
    return (jnp.sum(sq), grad_x, g_w_in, g_w_uq, g_w_ukv, g_w_out, d_ada, g_norm_w, g_qln, g_kvln,
            g_qhn[:, :MLA_QK_DIM], g_khn[:, :MLA_QK_DIM])
```

```python
import functools
import math

import jax
import jax.numpy as jnp
from jax import lax
from jax.experimental import pallas as pl
from jax.experimental.pallas import tpu as pltpu

F32 = jnp.float32
BF16 = jnp.bfloat16
I32 = jnp.int32

D_MODEL = 1024
SB_HEADS = 8
SB_WIDTH = 512
MLA_HEADS = 8
MLA_QK_DIM = 96
MLA_NOPE_DIM = 64
MLA_ROPE_DIM = 32
MLA_WIDTH = 512
Q_LORA_RANK = 384
KV_LORA_RANK = 256
ROPE_THETA = 10000.0
EPS = 1e-6
LANES = 128
HEAD_PAD = 128
MLA_PAD_WIDTH = MLA_HEADS * HEAD_PAD

C_Q, C_K, C_V, C_G = 0, 512, 1024, 1536
C_CQ, C_CKV, C_GM, C_KR = 2048, 2432, 2688, 3200
IN_COLS_PAD = 3328
KR_LANE = 64

ADAM_LR = 0.001
ADAM_B1 = 0.9
ADAM_B2 = 0.999
ADAM_EPS = 1e-08
ADAM_WD = 0.01
ADAM_STEP = 10

SB_SCALE = 0.125
SB_GROUP = 4
MLA_SCALE = 1.0 / math.sqrt(MLA_QK_DIM)
LN2 = math.log(2.0)
MLA_SCALE_LOG2 = MLA_SCALE / LN2
MLA_BQ = 512
MLA_BWD_BQ = 512
MLA_BK = 1024
MLA_BWD_BK = 512
SB_DEAD = -104.0
MASK_NEG = -1e30

VMEM_LIMIT = 56 * 1024 * 1024
MESH = pl.DeviceIdType.MESH


def _dot(a, b):
    return jnp.dot(a, b, preferred_element_type=F32)


def _dot_nt(a, b):
    return lax.dot_general(a, b, (((1,), (1,)), ((), ())), preferred_element_type=F32)


def _dot_tn(a, b):
    return lax.dot_general(a, b, (((0,), (0,)), ((), ())), preferred_element_type=F32)


def _sigmoid(x):
    return 1.0 / (1.0 + jnp.exp(-x))


def _split_dot(a, m):
    hi = a.astype(BF16)
    lo = (a - hi.astype(F32)).astype(BF16)
    return _dot(hi, m) + _dot(lo, m)


def _params(sem, vmem=None):
    return pltpu.CompilerParams(dimension_semantics=sem, vmem_limit_bytes=vmem)


def _row_tile(s, want):
    return min(want, s)


def _hbm_spec():
    return pl.BlockSpec(memory_space=pltpu.HBM)


def _gather_call(shards):
    n = len(shards)
    halves = [s.shape[1] // 2 for s in shards]

    def body(*refs):
        ins, outs = refs[:n], refs[n:2 * n]
        ici_send, ici_recv, d2d_send, d2d_recv, loc_sems = refs[2 * n:]
        x, y, c = lax.axis_index("x"), lax.axis_index("y"), lax.axis_index("c")
        me = 2 * x + y
        peers = [(1 - x, y), (x, 1 - y), (1 - x, 1 - y)]

        def rows(a, which):
            return pl.ds(pl.multiple_of(which * halves[a], 16), halves[a])

        def ici(a, j, slot):
            px, py = peers[j]
            return pltpu.make_async_remote_copy(
                src_ref=ins[a].at[0, rows(a, c)], dst_ref=outs[a].at[slot, rows(a, c)],
                send_sem=ici_send.at[3 * a + j], recv_sem=ici_recv.at[3 * a + j],
                device_id=(px, py, c), device_id_type=MESH)

        def d2d(a, j, which):
            px, py = peers[j]
            piece = outs[a].at[2 * px + py, rows(a, which)]
            return pltpu.make_async_remote_copy(
                src_ref=piece, dst_ref=piece,
                send_sem=d2d_send.at[3 * a + j], recv_sem=d2d_recv.at[3 * a + j],
                device_id=(x, y, 1 - c), device_id_type=MESH)

        local = [pltpu.make_async_copy(ins[a].at[0], outs[a].at[me], loc_sems.at[a]) for a in range(n)]
        for cp in local:
            cp.start()
        sends = [ici(a, j, me) for a in range(n) for j in range(3)]
        for cp in sends:
            cp.start()
        for a in range(n):
            for j in range(3):
                px, py = peers[j]
                ici(a, j, 2 * px + py).wait_recv()
                cp = d2d(a, j, c)
                cp.start()
                sends.append(cp)
        for a in range(n):
            for j in range(3):
                d2d(a, j, 1 - c).wait_recv()
        for cp in sends:
            cp.wait_send()
        for cp in local:
            cp.wait()

    return pl.pallas_call(
        body, name="gather_weights",
        out_shape=[jax.ShapeDtypeStruct((4,) + s.shape[1:], s.dtype) for s in shards],
        in_specs=[_hbm_spec() for _ in shards],
        out_specs=[_hbm_spec() for _ in shards],
        scratch_shapes=[pltpu.SemaphoreType.DMA((3 * n,)), pltpu.SemaphoreType.DMA((3 * n,)),
                        pltpu.SemaphoreType.DMA((3 * n,)), pltpu.SemaphoreType.DMA((3 * n,)),
                        pltpu.SemaphoreType.DMA((n,))],
    )(*shards)


def _exchange_call(grads, small):
    n = len(grads)

    def body(*refs):
        g_in, small_in = refs[:n], refs[n]
        own, sib, packs = refs[n + 1:2 * n + 1], refs[2 * n + 1:3 * n + 1], refs[3 * n + 1]
        ici_send, ici_recv, d2d_send, d2d_recv, sm_send, sm_recv, loc_sems = refs[3 * n + 2:]
        x, y, c = lax.axis_index("x"), lax.axis_index("y"), lax.axis_index("c")
        me = 2 * x + y
        me8 = 4 * x + 2 * y + c
        sibling = (x, y, 1 - c)
        peers = [(1 - x, y), (x, 1 - y), (1 - x, 1 - y)]
        flips = [(fx, fy, fc) for fx in (0, 1) for fy in (0, 1) for fc in (0, 1)][1:]

        def ici(a, j, src_slot, dst_slot):
            px, py = peers[j]
            return pltpu.make_async_remote_copy(
                src_ref=g_in[a].at[src_slot], dst_ref=own[a].at[dst_slot],
                send_sem=ici_send.at[3 * a + j], recv_sem=ici_recv.at[3 * a + j],
                device_id=(px, py, c), device_id_type=MESH)

        def d2d(a, rel, chip, src):
            return pltpu.make_async_remote_copy(
                src_ref=src, dst_ref=sib[a].at[chip],
                send_sem=d2d_send.at[4 * a + rel], recv_sem=d2d_recv.at[4 * a + rel],
                device_id=sibling, device_id_type=MESH)

        def flipped(r):
            fx, fy, fc = flips[r]
            return ((1 - x) if fx else x, (1 - y) if fy else y, (1 - c) if fc else c)

        def sm(r, slot):
            return pltpu.make_async_remote_copy(
                src_ref=small_in, dst_ref=packs.at[slot],
                send_sem=sm_send.at[r], recv_sem=sm_recv.at[r],
                device_id=flipped(r), device_id_type=MESH)

        def peer8(r):
            px, py, pc = flipped(r)
            return 4 * px + 2 * py + pc

        local = [pltpu.make_async_copy(g_in[a].at[me], own[a].at[me], loc_sems.at[a]) for a in range(n)]
        local.append(pltpu.make_async_copy(small_in, packs.at[me8], loc_sems.at[n]))
        for cp in local:
            cp.start()
        sends = []
        for r in range(7):
            sends.append(sm(r, me8))
        for a in range(n):
            for j in range(3):
                px, py = peers[j]
                sends.append(ici(a, j, 2 * px + py, me))
        for cp in sends:
            cp.start()
        for a in range(n):
            cp = d2d(a, 0, me, g_in[a].at[me])
            cp.start()
            sends.append(cp)
        for a in range(n):
            for j in range(3):
                px, py = peers[j]
                ici(a, j, me, 2 * px + py).wait_recv()
                cp = d2d(a, 1 + j, 2 * px + py, own[a].at[2 * px + py])
                cp.start()
                sends.append(cp)
        for a in range(n):
            d2d(a, 0, me, g_in[a].at[me]).wait_recv()
            for j in range(3):
                px, py = peers[j]
                d2d(a, 1 + j, 2 * px + py, g_in[a].at[me]).wait_recv()
        for r in range(7):
            sm(r, peer8(r)).wait_recv()
        for cp in sends:
            cp.wait_send()
        for cp in local:
            cp.wait()

    out_shape = ([jax.ShapeDtypeStruct(g.shape, g.dtype) for g in grads] * 2
                 + [jax.ShapeDtypeStruct((8,) + small.shape, small.dtype)])
    res = pl.pallas_call(
        body, name="exchange_grads",
        out_shape=out_shape,
        in_specs=[_hbm_spec() for _ in range(n + 1)],
        out_specs=[_hbm_spec() for _ in range(2 * n + 1)],
        scratch_shapes=[pltpu.SemaphoreType.DMA((3 * n,)), pltpu.SemaphoreType.DMA((3 * n,)),
                        pltpu.SemaphoreType.DMA((4 * n,)), pltpu.SemaphoreType.DMA((4 * n,)),
                        pltpu.SemaphoreType.DMA((7,)), pltpu.SemaphoreType.DMA((7,)),
                        pltpu.SemaphoreType.DMA((n + 1,))],
    )(*grads, small)
    return res[:n], res[n:2 * n], res[2 * n]


def _ada_call(c, w_ada_bf, b_ada):
    def body(c_ref, w_ref, b_ref, o_ref):
        cc = c_ref[...]
        sc = jnp.broadcast_to(cc * _sigmoid(cc), (8, D_MODEL)).astype(BF16)
        o_ref[...] = _dot(sc, w_ref[...]) + b_ref[...]

    out = pl.pallas_call(
        body, name="ada_fwd",
        out_shape=jax.ShapeDtypeStruct((8, 3 * D_MODEL), F32),
        compiler_params=pltpu.CompilerParams(vmem_limit_bytes=VMEM_LIMIT),
    )(c, w_ada_bf, b_ada)
    return out[0:1]


def _full(shape):
    return pl.BlockSpec(shape, lambda i: (0,) * len(shape))


def _rows(tm, width):
    return pl.BlockSpec((tm, width), lambda i: (i, 0))


def _pre_call(x, shift, scale, norm_w, w_in_bf):
    s = x.shape[0]
    tm = _row_tile(s, 512)
    groups = [(C_Q, 512, BF16), (C_K, 512, BF16), (C_V, 512, BF16), (C_G, 512, F32),
              (C_CQ, Q_LORA_RANK, F32), (C_CKV, KV_LORA_RANK, F32), (C_GM, 512, F32), (C_KR, LANES, F32)]

    def body(x_ref, sh_ref, sc_ref, nw_ref, w_ref, hb_ref, *outs):
        xx = x_ref[...]
        r0 = lax.rsqrt(jnp.mean(xx * xx, axis=-1, keepdims=True) + EPS)
        h = (xx * r0 * nw_ref[...]) * (1.0 + sc_ref[...]) + sh_ref[...]
        hb = h.astype(BF16)
        hb_ref[...] = hb
        for (c0, width, dt), o_ref in zip(groups, outs):
            o_ref[...] = _dot(hb, w_ref[:, c0:c0 + width]).astype(dt)

    return pl.pallas_call(
        body, name="pre_proj",
        grid=(s // tm,),
        in_specs=[_rows(tm, D_MODEL), _full((1, D_MODEL)), _full((1, D_MODEL)), _full((1, D_MODEL)),
                  _full((D_MODEL, IN_COLS_PAD))],
        out_specs=[_rows(tm, D_MODEL)] + [_rows(tm, w) for _, w, _ in groups],
        out_shape=[jax.ShapeDtypeStruct((s, D_MODEL), BF16)]
        + [jax.ShapeDtypeStruct((s, w), dt) for _, w, dt in groups],
        compiler_params=_params(("arbitrary",), VMEM_LIMIT),
    )(x, shift, scale, norm_w, w_in_bf)


def _rope(t, cos_t, sin_a, sin_b):
    return t * cos_t + pltpu.roll(t, 112, 1) * sin_a + pltpu.roll(t, 16, 1) * sin_b


def _rope_adjoint(d, cos_t, sin_a, sin_b):
    return d * cos_t + pltpu.roll(d * sin_a, 16, 1) + pltpu.roll(d * sin_b, 112, 1)


def _mla_prep_call(c_q, c_kv, k_rope, cos_t, sin_a, sin_b, q_lora_norm, kv_lora_norm, qhn_pad, khn_pad,
                   w_uq_bf, w_uk_bf, w_uv_bf):
    s = c_q.shape[0]
    tm = _row_tile(s, 256)

    def body(cq_ref, ckv_ref, kr_ref, cos_ref, sa_ref, sb_ref, qln_ref, kvln_ref, qhn_ref, khn_ref,
             wuq_ref, wuk_ref, wuv_ref, q_ref, k_ref, v_ref, cqn_ref, ckvn_ref, q0_ref, k0_ref):
        cq = cq_ref[...]
        cqn = (cq * lax.rsqrt(jnp.mean(cq * cq, axis=-1, keepdims=True) + EPS) * qln_ref[...]).astype(BF16)
        cqn_ref[...] = cqn
        ckv = ckv_ref[...]
        ckvn = (ckv * lax.rsqrt(jnp.mean(ckv * ckv, axis=-1, keepdims=True) + EPS) * kvln_ref[...]).astype(BF16)
        ckvn_ref[...] = ckvn
        v_ref[...] = _dot(ckvn, wuv_ref[...]).astype(BF16)
        cos_t, sin_a, sin_b = cos_ref[...], sa_ref[...], sb_ref[...]
        kr = kr_ref[...]
        for h in range(MLA_HEADS):
            cols = slice(h * HEAD_PAD, (h + 1) * HEAD_PAD)
            q0 = _dot(cqn, wuq_ref[:, cols])
            q0_ref[:, cols] = q0
            rq = lax.rsqrt(jnp.sum(q0 * q0, axis=-1, keepdims=True) * (1.0 / MLA_QK_DIM) + EPS)
            q_ref[:, cols] = (_rope(q0 * rq * qhn_ref[...], cos_t, sin_a, sin_b) * MLA_SCALE_LOG2).astype(BF16)
            k0 = _dot(ckvn, wuk_ref[:, cols]) + kr
            k0_ref[:, cols] = k0
            rk = lax.rsqrt(jnp.sum(k0 * k0, axis=-1, keepdims=True) * (1.0 / MLA_QK_DIM) + EPS)
            k_ref[:, cols] = _rope(k0 * rk * khn_ref[...], cos_t, sin_a, sin_b).astype(BF16)

    return pl.pallas_call(
        body, name="mla_prep",
        grid=(s // tm,),
        in_specs=[_rows(tm, Q_LORA_RANK), _rows(tm, KV_LORA_RANK), _rows(tm, LANES),
                  _rows(tm, LANES), _rows(tm, LANES), _rows(tm, LANES),
                  _full((1, Q_LORA_RANK)), _full((1, KV_LORA_RANK)), _full((1, LANES)), _full((1, LANES)),
                  _full((Q_LORA_RANK, MLA_PAD_WIDTH)), _full((KV_LORA_RANK, MLA_PAD_WIDTH)),
                  _full((KV_LORA_RANK, MLA_WIDTH))],
        out_specs=[_rows(tm, MLA_PAD_WIDTH), _rows(tm, MLA_PAD_WIDTH), _rows(tm, MLA_WIDTH),
                   _rows(tm, Q_LORA_RANK), _rows(tm, KV_LORA_RANK),
                   _rows(tm, MLA_PAD_WIDTH), _rows(tm, MLA_PAD_WIDTH)],
        out_shape=[jax.ShapeDtypeStruct((s, MLA_PAD_WIDTH), BF16), jax.ShapeDtypeStruct((s, MLA_PAD_WIDTH), BF16),
                   jax.ShapeDtypeStruct((s, MLA_WIDTH), BF16),
                   jax.ShapeDtypeStruct((s, Q_LORA_RANK), BF16), jax.ShapeDtypeStruct((s, KV_LORA_RANK), BF16),
                   jax.ShapeDtypeStruct((s, MLA_PAD_WIDTH), F32), jax.ShapeDtypeStruct((s, MLA_PAD_WIDTH), F32)],
        compiler_params=_params(("arbitrary",), VMEM_LIMIT),
    )(c_q, c_kv, k_rope, cos_t, sin_a, sin_b, q_lora_norm, kv_lora_norm, qhn_pad, khn_pad,
      w_uq_bf, w_uk_bf, w_uv_bf)


def _log_sigmoid_pair(z):
    ls = jnp.minimum(z, 0.0) - jnp.log(1.0 + jnp.exp(-jnp.abs(z)))
    return ls, ls - z


def _pair(hh):
    return slice((hh // 2) * LANES, (hh // 2 + 1) * LANES)


def _sb_fwd_call(q, k, v):
    s = q.shape[0]
    bq = _row_tile(s, 256)
    nq = s // bq
    nh = SB_GROUP
    width = nh * 64

    def body(q_ref, k_ref, v_ref, o_ref, r_ref, ks_ref):
        hp, i = pl.program_id(0), pl.program_id(1)
        lane = lax.broadcasted_iota(I32, (bq, LANES), 1)
        row = lax.broadcasted_iota(I32, (bq, bq), 0)
        col = lax.broadcasted_iota(I32, (bq, bq), 1)
        strict = col < row
        later = jnp.where(row > col, 1.0, 0.0).astype(BF16)
        masks = [_head_mask(lane, hh).astype(BF16) for hh in range(2)]
        qms = [q_ref[:, _pair(hh)] * jnp.asarray(SB_SCALE, BF16) * masks[hh % 2] for hh in range(nh)]

        def walk(blocks, state):
            chains = [(kb, diagonal, hh) for kb, diagonal in blocks for hh in range(nh)]
            keys = lambda kb: pl.ds(pl.multiple_of(kb * bq, bq), bq)
            zs = [_dot_nt(qms[hh], k_ref[keys(kb), _pair(hh)]) for kb, _, hh in chains]
            pairs = []
            for z, (_, diagonal, _) in zip(zs, chains):
                ls, lk = _log_sigmoid_pair(z)
                pairs.append((ls, jnp.where(strict, lk, 0.0) if diagonal else lk))
            sums = [_split_dot(lk, later) for _, lk in pairs]
            runs = [st[0] for st in state]
            ws = []
            for (ls, lk), after, (_, diagonal, hh) in zip(pairs, sums, chains):
                w = jnp.exp(ls + (after + runs[hh]))
                ws.append((jnp.where(strict, w, 0.0) if diagonal else w).astype(BF16))
                runs[hh] = runs[hh] + jnp.sum(lk, axis=1, keepdims=True)
            accs = [st[1] for st in state]
            for w, (kb, _, hh) in zip(ws, chains):
                accs[hh] = accs[hh] + _dot(w, v_ref[keys(kb), _pair(hh)])
            return tuple(zip(runs, accs))

        def alive(state):
            top = jnp.max(state[0][0])
            for st in state[1:]:
                top = jnp.maximum(top, jnp.max(st[0]))
            return (top > SB_DEAD).astype(I32)

        def finish(state, first):
            ks_ref[hp, i] = first
            for pair in range(nh // 2):
                o_ref[:, _pair(2 * pair)] = jnp.where(lane < 64, state[2 * pair][1], state[2 * pair + 1][1])
                r_ref[:, _pair(2 * pair)] = jnp.where(lane < 64, state[2 * pair][0], state[2 * pair + 1][0])

        zero = ((jnp.zeros((bq, 1), F32), jnp.zeros((bq, LANES), F32)),) * nh

        @pl.when(i == 0)
        def _():
            finish(walk([(0, True)], zero), 0)

        @pl.when(i > 0)
        def _():
            state = walk([(i, True), (i - 1, False)], zero)

            def cond(carry):
                return jnp.logical_and(carry[0] >= 0, carry[1] > 0)

            def step(carry):
                state = walk([(carry[0], False)], carry[2])
                return carry[0] - 1, alive(state), state

            kb, _, state = lax.while_loop(cond, step, (i - 2, alive(state), state))
            finish(state, kb + 1)

    return pl.pallas_call(
        body, name="sb_fwd",
        grid=(SB_HEADS // nh, nq),
        in_specs=[pl.BlockSpec((bq, width), lambda h, i: (i, h)),
                  pl.BlockSpec((s, width), lambda h, i: (0, h)),
                  pl.BlockSpec((s, width), lambda h, i: (0, h))],
        out_specs=[pl.BlockSpec((bq, width), lambda h, i: (i, h)),
                   pl.BlockSpec((bq, width), lambda h, i: (i, h)),
                   pl.BlockSpec(memory_space=pltpu.SMEM)],
        out_shape=[jax.ShapeDtypeStruct((s, SB_WIDTH), F32), jax.ShapeDtypeStruct((s, SB_WIDTH), F32),
                   jax.ShapeDtypeStruct((SB_HEADS // nh, nq), I32)],
        compiler_params=_params(("arbitrary", "arbitrary"), VMEM_LIMIT),
    )(q, k, v)


def _mla_fwd_call(q, k, v):
    s = q.shape[0]
    bq = _row_tile(s, MLA_BQ)
    bk = _row_tile(s, MLA_BK)
    nq = s // bq
    assert bk % bq == 0
    half = bk // 2

    def body(q_ref, k_ref, v_ref, o_ref, lse_ref, p_ref, s_ref):
        i = pl.program_id(1)
        lane = lax.broadcasted_iota(I32, (bq, LANES), 1)
        row = lax.broadcasted_iota(I32, (half, bq), 1)
        col = lax.broadcasted_iota(I32, (half, bq), 0)
        n_full = (i * bq) // bk

        def keys(g):
            return pl.ds(pl.multiple_of(g * half, half), half)

        def put_scores(g, slot):
            for hh in range(2):
                cols = slice(hh * HEAD_PAD, (hh + 1) * HEAD_PAD)
                s_ref[slot, hh] = _dot_nt(k_ref[keys(g), cols], q_ref[:, cols])

        def add_pv(carry, g, slot):
            vblk = v_ref[keys(g), :]
            return tuple((m, l, alpha * acc + _dot_tn(vblk, p_ref[slot, hh]), alpha)
                         for hh, (m, l, acc, alpha) in enumerate(carry))

        def substep(g, slot, carry, masked, prefetch):
            if prefetch:
                put_scores(g + 1, 1 - slot)
            carry = add_pv(carry, jnp.maximum(g - 1, 0), 1 - slot)
            new = []
            for hh in range(2):
                m, l, acc, _ = carry[hh]
                sc = s_ref[slot, hh]
                if masked:
                    sc = jnp.where(col + g * half <= row + i * bq, sc, MASK_NEG)
                m_new = jnp.maximum(m, jnp.max(sc, axis=0, keepdims=True))
                p = jnp.exp2(sc - m_new)
                alpha = jnp.exp2(m - m_new)
                l = alpha * l + jnp.sum(p, axis=0, keepdims=True)
                p_ref[slot, hh] = p.astype(BF16)
                new.append((m_new, l, acc, alpha))
            return tuple(new)

        def chunk(kb, carry, masked):
            carry = substep(2 * kb, 0, carry, masked, True)
            return substep(2 * kb + 1, 1, carry, masked, not masked)

        p_ref[1] = jnp.zeros_like(p_ref[1])
        put_scores(0, 0)
        one = (jnp.full((1, bq), MASK_NEG, F32), jnp.zeros((1, bq), F32), jnp.zeros((LANES, bq), F32),
               jnp.ones((1, bq), F32))
        carry = lax.fori_loop(0, n_full, lambda kb, cr: chunk(kb, cr, False), (one, one))
        carry = chunk(n_full, carry, True)
        (m0, l0, a0, _), (m1, l1, a1, _) = add_pv(carry, 2 * n_full + 1, 1)
        o_ref[...] = jnp.where(lane < 64, (a0 / l0).T, (a1 / l1).T)
        sub = lax.broadcasted_iota(I32, (8, bq), 0)
        lse_ref[...] = jnp.where(sub == 0, m0 + jnp.log2(l0), jnp.where(sub == 1, m1 + jnp.log2(l1), 0.0))

    return pl.pallas_call(
        body, name="mla_fwd",
        grid=(4, nq),
        in_specs=[pl.BlockSpec((bq, 2 * HEAD_PAD), lambda h, i: (i, h)),
                  pl.BlockSpec((s, 2 * HEAD_PAD), lambda h, i: (0, h)),
                  pl.BlockSpec((s, LANES), lambda h, i: (0, h))],
        out_specs=[pl.BlockSpec((bq, LANES), lambda h, i: (i, h)),
                   pl.BlockSpec((None, 8, bq), lambda h, i: (h, 0, i))],
        out_shape=[jax.ShapeDtypeStruct((s, MLA_WIDTH), F32), jax.ShapeDtypeStruct((4, 8, s), F32)],
        scratch_shapes=[pltpu.VMEM((2, 2, half, bq), BF16), pltpu.VMEM((2, 2, half, bq), F32)],
        compiler_params=_params(("arbitrary", "arbitrary"), VMEM_LIMIT),
    )(q, k, v)


def _out_call(o_sb, g_sb, o_mla, g_mla, x, target, gate, w_out_bf):
    s = x.shape[0]
    tm = _row_tile(s, 256)

    def body(osb_ref, gsb_ref, oml_ref, gml_ref, x_ref, t_ref, gate_ref, w_ref,
             dosb_ref, doml_ref, dgsb_ref, dgml_ref, dy_ref, gw_ref, dgate_ref, sq_ref):
        @pl.when(pl.program_id(0) == 0)
        def _():
            gw_ref[...] = jnp.zeros_like(gw_ref)
            dgate_ref[...] = jnp.zeros_like(dgate_ref)
            sq_ref[...] = jnp.zeros_like(sq_ref)

        g_s, g_m = gsb_ref[...], gml_ref[...]
        sig_s, sig_m = _sigmoid(g_s), _sigmoid(g_m)
        silu_s, silu_m = g_s * sig_s, g_m * sig_m
        o_s, o_m = osb_ref[...], oml_ref[...]
        mixed = jnp.concatenate([o_s * silu_s, o_m * silu_m], axis=1).astype(BF16)
        u = _dot(mixed, w_ref[...])
        gate_v = gate_ref[...]
        err = x_ref[...] + gate_v * u - t_ref[...]
        sq_ref[...] += jnp.sum(err * err, axis=0, keepdims=True)
        dy = err * (1.0 / D_MODEL)
        dy_ref[...] = dy
        dgate_ref[...] += jnp.sum(dy * u, axis=0, keepdims=True)
        du = (dy * gate_v).astype(BF16)
        gw_ref[...] += _dot_tn(mixed, du)
        dmix = _dot_nt(du, w_ref[...])
        dm_s, dm_m = dmix[:, :SB_WIDTH], dmix[:, SB_WIDTH:]
        dosb_ref[...] = (dm_s * silu_s).astype(BF16)
        doml_ref[...] = (dm_m * silu_m).astype(BF16)
        dgsb_ref[...] = (dm_s * o_s * (sig_s * (1.0 + g_s * (1.0 - sig_s)))).astype(BF16)
        dgml_ref[...] = (dm_m * o_m * (sig_m * (1.0 + g_m * (1.0 - sig_m)))).astype(BF16)

    return pl.pallas_call(
        body, name="out_proj_loss",
        grid=(s // tm,),
        in_specs=[_rows(tm, 512), _rows(tm, 512), _rows(tm, 512), _rows(tm, 512),
                  _rows(tm, D_MODEL), _rows(tm, D_MODEL), _full((1, D_MODEL)), _full((D_MODEL, D_MODEL))],
        out_specs=[_rows(tm, 512), _rows(tm, 512), _rows(tm, 512), _rows(tm, 512), _rows(tm, D_MODEL),
                   _full((D_MODEL, D_MODEL)), _full((1, D_MODEL)), _full((1, D_MODEL))],
        out_shape=[jax.ShapeDtypeStruct((s, 512), BF16)] * 4
        + [jax.ShapeDtypeStruct((s, D_MODEL), F32), jax.ShapeDtypeStruct((D_MODEL, D_MODEL), F32),
           jax.ShapeDtypeStruct((1, D_MODEL), F32), jax.ShapeDtypeStruct((1, D_MODEL), F32)],
        compiler_params=_params(("arbitrary",), VMEM_LIMIT),
    )(o_sb, g_sb, o_mla, g_mla, x, target, gate, w_out_bf)


def _head_mask(lane, hh):
    return jnp.where((lane >= 64) if hh else (lane < 64), 1.0, 0.0)


def _pick_lane(packed, lane, which):
    return jnp.sum(jnp.where(lane == which, packed, 0.0), axis=1, keepdims=True)


def _sb_bwd_call(kstart, q, k, v, do, rfin):
    s = q.shape[0]
    bq = _row_tile(s, 256)
    nq = s // bq
    nh = SB_GROUP
    width = nh * 64

    def body(ks_ref, q_ref, k_ref, v_ref, do_ref, r_ref, dq_ref, dk_ref, dv_ref):
        hp, i = pl.program_id(0), pl.program_id(1)

        @pl.when(i == 0)
        def _():
            dk_ref[...] = jnp.zeros_like(dk_ref)
            dv_ref[...] = jnp.zeros_like(dv_ref)

        lane = lax.broadcasted_iota(I32, (bq, LANES), 1)
        row = lax.broadcasted_iota(I32, (bq, bq), 0)
        col = lax.broadcasted_iota(I32, (bq, bq), 1)
        upto = jnp.where(row <= col, 1.0, 0.0).astype(BF16)
        before = jnp.where(row < col, 1.0, 0.0).astype(BF16)
        masks = [_head_mask(lane, hh).astype(BF16) for hh in range(2)]
        qms = [q_ref[:, _pair(hh)] * jnp.asarray(SB_SCALE, BF16) * masks[hh % 2] for hh in range(nh)]
        doms = [do_ref[:, _pair(hh)] * masks[hh % 2] for hh in range(nh)]
        totals = [_pick_lane(r_ref[:, _pair(hh)], lane, 64 * (hh % 2)) for hh in range(nh)]
        strict = col < row

        def walk(blocks, state):
            chains = [(kb, diagonal, hh) for kb, diagonal in blocks for hh in range(nh)]
            keys = lambda kb: pl.ds(pl.multiple_of(kb * bq, bq), bq)
            cut = lambda x, diagonal: jnp.where(strict, x, 0.0) if diagonal else x
            zs = [_dot_nt(qms[hh], k_ref[keys(kb), _pair(hh)]) for kb, _, hh in chains]
            dws = [_dot_nt(doms[hh], v_ref[keys(kb), _pair(hh)]) for kb, _, hh in chains]
            pairs = []
            for z, (_, diagonal, _) in zip(zs, chains):
                ls, lk = _log_sigmoid_pair(z)
                pairs.append((ls, cut(lk, diagonal)))
            incls = [_split_dot(lk, upto) for _, lk in pairs]
            pres = [st[0] for st in state]
            ws, gs = [], []
            for (ls, lk), incl, dw, (_, diagonal, hh) in zip(pairs, incls, dws, chains):
                w = cut(jnp.exp(ls + ((totals[hh] - pres[hh]) - incl)), diagonal)
                ws.append(w.astype(BF16))
                gs.append(w * dw)
                pres[hh] = pres[hh] + jnp.sum(lk, axis=1, keepdims=True)
            gsums = [_split_dot(g, before) for g in gs]
            gpres = [st[1] for st in state]
            dzs = []
            for (ls, _), g, gsum, (_, diagonal, hh) in zip(pairs, gs, gsums, chains):
                dzs.append(cut(g - jnp.exp(ls) * (g + (gpres[hh] + gsum)), diagonal).astype(BF16))
                gpres[hh] = gpres[hh] + jnp.sum(g, axis=1, keepdims=True)
            dqs = [st[2] for st in state]
            dk_parts, dv_parts = [], []
            for dzb, w, (kb, _, hh) in zip(dzs, ws, chains):
                dk_parts.append(_dot_tn(dzb, qms[hh]))
                dv_parts.append(_dot_tn(w, doms[hh]))
                dqs[hh] = dqs[hh] + _dot(dzb, k_ref[keys(kb), _pair(hh)])
            for b, (kb, _) in enumerate(blocks):
                for pair in range(nh // 2):
                    c0 = b * nh + 2 * pair
                    dk_ref[keys(kb), _pair(2 * pair)] += dk_parts[c0] + dk_parts[c0 + 1]
                    dv_ref[keys(kb), _pair(2 * pair)] += dv_parts[c0] + dv_parts[c0 + 1]
            return tuple(zip(pres, gpres, dqs))

        def finish(state):
            for pair in range(nh // 2):
                both = jnp.where(lane < 64, state[2 * pair][2], state[2 * pair + 1][2])
                dq_ref[:, _pair(2 * pair)] = (both * SB_SCALE).astype(BF16)

        zero = ((jnp.zeros((bq, 1), F32), jnp.zeros((bq, 1), F32), jnp.zeros((bq, LANES), F32)),) * nh

        @pl.when(i == 0)
        def _():
            finish(walk([(0, True)], zero))

        @pl.when(i > 0)
        def _():
            state = lax.fori_loop(ks_ref[hp, i], i - 1, lambda kb, st: walk([(kb, False)], st), zero)
            finish(walk([(i - 1, False), (i, True)], state))

    return pl.pallas_call(
        body, name="sb_bwd",
        grid_spec=pltpu.PrefetchScalarGridSpec(
            num_scalar_prefetch=1, grid=(SB_HEADS // nh, nq),
            in_specs=[pl.BlockSpec((bq, width), lambda h, i, ks: (i, h)),
                      pl.BlockSpec((s, width), lambda h, i, ks: (0, h), pipeline_mode=pl.Buffered(1)),
                      pl.BlockSpec((s, width), lambda h, i, ks: (0, h), pipeline_mode=pl.Buffered(1)),
                      pl.BlockSpec((bq, width), lambda h, i, ks: (i, h)),
                      pl.BlockSpec((bq, width), lambda h, i, ks: (i, h))],
            out_specs=[pl.BlockSpec((bq, width), lambda h, i, ks: (i, h)),
                       pl.BlockSpec((s, width), lambda h, i, ks: (0, h), pipeline_mode=pl.Buffered(1)),
                       pl.BlockSpec((s, width), lambda h, i, ks: (0, h), pipeline_mode=pl.Buffered(1))]),
        out_shape=[jax.ShapeDtypeStruct((s, SB_WIDTH), BF16), jax.ShapeDtypeStruct((s, SB_WIDTH), F32),
                   jax.ShapeDtypeStruct((s, SB_WIDTH), F32)],
        compiler_params=_params(("arbitrary", "arbitrary"), VMEM_LIMIT),
    )(kstart, q, k, v, do, rfin)


def _mla_bwd_call(q, k, v, do, o, lse):
    s = q.shape[0]
    bq = _row_tile(s, MLA_BWD_BQ)
    bk = _row_tile(s, MLA_BWD_BK)
    nq = s // bq
    assert bk % bq == 0
    half = bk // 2

    def body(q_ref, k_ref, v_ref, do_ref, o_ref, lse_ref, dq_ref, dk_ref, dv_ref, dom_ref, s_ref, dp_ref, pb_ref,
             ds_ref):
        i = pl.program_id(1)

        @pl.when(i == 0)
        def _():
            dk_ref[...] = jnp.zeros_like(dk_ref)
            dv_ref[...] = jnp.zeros_like(dv_ref)

        lane = lax.broadcasted_iota(I32, (bq, LANES), 1)
        row = lax.broadcasted_iota(I32, (half, bq), 1)
        col = lax.broadcasted_iota(I32, (half, bq), 0)
        n_full = (i * bq) // bk
        do2 = do_ref[...]
        prod = do2.astype(F32) * o_ref[...]
        ones = jnp.ones((8, LANES), BF16)
        deltas, lses = [], []
        for hh in range(2):
            head = _head_mask(lane, hh)
            dom_ref[hh] = do2 * head.astype(BF16)
            part = prod * head
            hi = part.astype(BF16)
            lo = (part - hi.astype(F32)).astype(BF16)
            deltas.append((_dot_nt(ones, hi) + _dot_nt(ones, lo))[0:1])
            lses.append(lse_ref[hh:hh + 1, :])

        def keys(g):
            return pl.ds(pl.multiple_of(g * half, half), half)

        def heads():
            return [(hh, slice(hh * HEAD_PAD, (hh + 1) * HEAD_PAD)) for hh in range(2)]

        def put_products(g, slot):
            vblk = v_ref[keys(g), :]
            for hh, cols in heads():
                s_ref[slot, hh] = _dot_nt(k_ref[keys(g), cols], q_ref[:, cols])
                dp_ref[slot, hh] = _dot_nt(vblk, dom_ref[hh])

        def add_grads(dqs, g, slot):
            rows = keys(g)
            new, dv_parts = [], []
            for hh, cols in heads():
                ds = ds_ref[slot, hh]
                dk_ref[rows, cols] += _dot(ds, q_ref[:, cols])
                dv_parts.append(_dot(pb_ref[slot, hh], dom_ref[hh]))
                new.append(dqs[hh] + _dot_tn(k_ref[rows, cols], ds))
            dv_ref[rows, :] += dv_parts[0] + dv_parts[1]
            return tuple(new)

        def substep(g, slot, dqs, masked, prefetch):
            if prefetch:
                put_products(g + 1, 1 - slot)
            dqs = add_grads(dqs, jnp.maximum(g - 1, 0), 1 - slot)
            for hh, _ in heads():
                p = jnp.exp2(s_ref[slot, hh] - lses[hh])
                if masked:
                    p = jnp.where(col + g * half <= row + i * bq, p, 0.0)
                ds_ref[slot, hh] = (p * (dp_ref[slot, hh] - deltas[hh])).astype(BF16)
                pb_ref[slot, hh] = p.astype(BF16)
            return dqs

        def chunk(kb, dqs, masked):
            dqs = substep(2 * kb, 0, dqs, masked, True)
            return substep(2 * kb + 1, 1, dqs, masked, not masked)

        ds_ref[1] = jnp.zeros_like(ds_ref[1])
        pb_ref[1] = jnp.zeros_like(pb_ref[1])
        put_products(0, 0)
        zero = jnp.zeros((HEAD_PAD, bq), F32)
        dqs = lax.fori_loop(0, n_full, lambda kb, dqs: chunk(kb, dqs, False), (zero, zero))
        dqs = chunk(n_full, dqs, True)
        dqs = add_grads(dqs, 2 * n_full + 1, 1)
        dq_ref[:, :HEAD_PAD] = dqs[0].T * MLA_SCALE
        dq_ref[:, HEAD_PAD:] = dqs[1].T * MLA_SCALE

    return pl.pallas_call(
        body, name="mla_bwd",
        grid=(4, nq),
        in_specs=[pl.BlockSpec((bq, 2 * HEAD_PAD), lambda h, i: (i, h)),
                  pl.BlockSpec((s, 2 * HEAD_PAD), lambda h, i: (0, h)),
                  pl.BlockSpec((s, LANES), lambda h, i: (0, h)),
                  pl.BlockSpec((bq, LANES), lambda h, i: (i, h)),
                  pl.BlockSpec((bq, LANES), lambda h, i: (i, h)),
                  pl.BlockSpec((None, 8, bq), lambda h, i: (h, 0, i))],
        out_specs=[pl.BlockSpec((bq, 2 * HEAD_PAD), lambda h, i: (i, h)),
                   pl.BlockSpec((s, 2 * HEAD_PAD), lambda h, i: (0, h)),
                   pl.BlockSpec((s, LANES), lambda h, i: (0, h))],
        out_shape=[jax.ShapeDtypeStruct((s, MLA_PAD_WIDTH), F32), jax.ShapeDtypeStruct((s, MLA_PAD_WIDTH), F32),
                   jax.ShapeDtypeStruct((s, MLA_WIDTH), F32)],
        scratch_shapes=[pltpu.VMEM((2, bq, LANES), BF16),
                        pltpu.VMEM((2, 2, half, bq), F32), pltpu.VMEM((2, 2, half, bq), F32),
                        pltpu.VMEM((2, 2, half, bq), BF16), pltpu.VMEM((2, 2, half, bq), BF16)],
        compiler_params=_params(("arbitrary", "arbitrary"), VMEM_LIMIT),
    )(q, k, v, do, o, lse)


def _rms_bwd(d_out, inp, r, weight, n):
    normed = inp * r
    gw = d_out * weight
    d_in = r * (gw - normed * (jnp.sum(gw * normed, axis=-1, keepdims=True) * (1.0 / n)))
    return d_in, d_out * normed


def _mla_prep_bwd_call(dq, dk, dv, q0, k0, cqn, ckvn, c_q, c_kv, cos_t, sin_a, sin_b,
                       q_lora_norm, kv_lora_norm, qhn_pad, khn_pad, w_uq_bf, w_uk_bf, w_uv_bf):
    s = dq.shape[0]
    tm = _row_tile(s, 256)

    def body(dq_ref, dk_ref, dv_ref, q0_ref, k0_ref, cqn_ref, ckvn_ref, cq_ref, ckv_ref,
             cos_ref, sa_ref, sb_ref, qln_ref, kvln_ref, qhn_ref, khn_ref, wuq_ref, wuk_ref, wuv_ref,
             dcq_ref, dckv_ref, dkr_ref, gwuq_ref, gwuk_ref, gwuv_ref, gqln_ref, gkvln_ref, gqhn_ref, gkhn_ref):
        @pl.when(pl.program_id(0) == 0)
        def _():
            for ref in (gwuq_ref, gwuk_ref, gwuv_ref, gqln_ref, gkvln_ref, gqhn_ref, gkhn_ref):
                ref[...] = jnp.zeros_like(ref)

        cos_t, sin_a, sin_b = cos_ref[...], sa_ref[...], sb_ref[...]
        lane = lax.broadcasted_iota(I32, (tm, LANES), 1)
        rope_lanes = jnp.logical_and(lane >= KR_LANE, lane < KR_LANE + MLA_ROPE_DIM)
        cqn, ckvn = cqn_ref[...], ckvn_ref[...]
        d_cqn = jnp.zeros((tm, Q_LORA_RANK), F32)
        d_ckvn = jnp.zeros((tm, KV_LORA_RANK), F32)
        d_kr = jnp.zeros((tm, LANES), F32)
        g_qhn = jnp.zeros((1, LANES), F32)
        g_khn = jnp.zeros((1, LANES), F32)
        for h in range(MLA_HEADS):
            cols = slice(h * HEAD_PAD, (h + 1) * HEAD_PAD)
            q0 = q0_ref[:, cols]
            rq = lax.rsqrt(jnp.sum(q0 * q0, axis=-1, keepdims=True) * (1.0 / MLA_QK_DIM) + EPS)
            d_q0, gq = _rms_bwd(_rope_adjoint(dq_ref[:, cols], cos_t, sin_a, sin_b), q0, rq, qhn_ref[...],
                                MLA_QK_DIM)
            g_qhn += jnp.sum(gq, axis=0, keepdims=True)
            d_q0b = d_q0.astype(BF16)
            d_cqn += _dot_nt(d_q0b, wuq_ref[:, cols])
            gwuq_ref[:, cols] += _dot_tn(cqn, d_q0b)
            k0 = k0_ref[:, cols]
            rk = lax.rsqrt(jnp.sum(k0 * k0, axis=-1, keepdims=True) * (1.0 / MLA_QK_DIM) + EPS)
            d_k0, gk = _rms_bwd(_rope_adjoint(dk_ref[:, cols] * LN2, cos_t, sin_a, sin_b), k0, rk, khn_ref[...],
                                MLA_QK_DIM)
            g_khn += jnp.sum(gk, axis=0, keepdims=True)
            d_kr += jnp.where(rope_lanes, d_k0, 0.0)
            d_k0b = d_k0.astype(BF16)
            d_ckvn += _dot_nt(d_k0b, wuk_ref[:, cols])
            gwuk_ref[:, cols] += _dot_tn(ckvn, d_k0b)
        dvb = dv_ref[...].astype(BF16)
        d_ckvn += _dot_nt(dvb, wuv_ref[...])
        gwuv_ref[...] += _dot_tn(ckvn, dvb)
        gqhn_ref[...] += g_qhn
        gkhn_ref[...] += g_khn
        dkr_ref[...] = d_kr.astype(BF16)
        cq = cq_ref[...]
        rcq = lax.rsqrt(jnp.mean(cq * cq, axis=-1, keepdims=True) + EPS)
        d_cq, gl = _rms_bwd(d_cqn, cq, rcq, qln_ref[...], Q_LORA_RANK)
        dcq_ref[...] = d_cq.astype(BF16)
        gqln_ref[...] += jnp.sum(gl, axis=0, keepdims=True)
        ckv = ckv_ref[...]
        rckv = lax.rsqrt(jnp.mean(ckv * ckv, axis=-1, keepdims=True) + EPS)
        d_ckv, gl = _rms_bwd(d_ckvn, ckv, rckv, kvln_ref[...], KV_LORA_RANK)
        dckv_ref[...] = d_ckv.astype(BF16)
        gkvln_ref[...] += jnp.sum(gl, axis=0, keepdims=True)

    return pl.pallas_call(
        body, name="mla_prep_bwd",
        grid=(s // tm,),
        in_specs=[_rows(tm, MLA_PAD_WIDTH), _rows(tm, MLA_PAD_WIDTH), _rows(tm, MLA_WIDTH),
                  _rows(tm, MLA_PAD_WIDTH), _rows(tm, MLA_PAD_WIDTH),
                  _rows(tm, Q_LORA_RANK), _rows(tm, KV_LORA_RANK), _rows(tm, Q_LORA_RANK), _rows(tm, KV_LORA_RANK),
                  _rows(tm, LANES), _rows(tm, LANES), _rows(tm, LANES),
                  _full((1, Q_LORA_RANK)), _full((1, KV_LORA_RANK)), _full((1, LANES)), _full((1, LANES)),
                  _full((Q_LORA_RANK, MLA_PAD_WIDTH)), _full((KV_LORA_RANK, MLA_PAD_WIDTH)),
                  _full((KV_LORA_RANK, MLA_WIDTH))],
        out_specs=[_rows(tm, Q_LORA_RANK), _rows(tm, KV_LORA_RANK), _rows(tm, LANES),
                   _full((Q_LORA_RANK, MLA_PAD_WIDTH)), _full((KV_LORA_RANK, MLA_PAD_WIDTH)),
                   _full((KV_LORA_RANK, MLA_WIDTH)),
                   _full((1, Q_LORA_RANK)), _full((1, KV_LORA_RANK)), _full((1, LANES)), _full((1, LANES))],
        out_shape=[jax.ShapeDtypeStruct((s, Q_LORA_RANK), BF16), jax.ShapeDtypeStruct((s, KV_LORA_RANK), BF16),
                   jax.ShapeDtypeStruct((s, LANES), BF16),
                   jax.ShapeDtypeStruct((Q_LORA_RANK, MLA_PAD_WIDTH), F32),
                   jax.ShapeDtypeStruct((KV_LORA_RANK, MLA_PAD_WIDTH), F32),
                   jax.ShapeDtypeStruct((KV_LORA_RANK, MLA_WIDTH), F32),
                   jax.ShapeDtypeStruct((1, Q_LORA_RANK), F32), jax.ShapeDtypeStruct((1, KV_LORA_RANK), F32),
                   jax.ShapeDtypeStruct((1, LANES), F32), jax.ShapeDtypeStruct((1, LANES), F32)],
        compiler_params=_params(("arbitrary",), VMEM_LIMIT),
    )(dq, dk, dv, q0, k0, cqn, ckvn, c_q, c_kv, cos_t, sin_a, sin_b,
      q_lora_norm, kv_lora_norm, qhn_pad, khn_pad, w_uq_bf, w_uk_bf, w_uv_bf)


def _dh_call(pieces, x, dy, shift, scale, norm_w, w_in_bf):
    s = x.shape[0]
    tm = _row_tile(s, 256)
    widths = [p.shape[1] for p in pieces]
    offsets = [sum(widths[:j]) for j in range(len(widths))]
    assert offsets[-1] + widths[-1] == IN_COLS_PAD
    n = len(pieces)

    def body(*refs):
        p_refs = refs[:n]
        x_ref, dy_ref, sh_ref, sc_ref, nw_ref, w_ref, gx_ref, dp_ref, dsh_ref, dsc_ref, gnw_ref = refs[n:]

        @pl.when(pl.program_id(0) == 0)
        def _():
            dsh_ref[...] = jnp.zeros_like(dsh_ref)
            dsc_ref[...] = jnp.zeros_like(dsc_ref)
            gnw_ref[...] = jnp.zeros_like(gnw_ref)

        for p_ref, c0, width in zip(p_refs, offsets, widths):
            dp_ref[:, c0:c0 + width] = p_ref[...].astype(BF16)
        dh = _dot_nt(dp_ref[...], w_ref[...])
        xx = x_ref[...]
        r0 = lax.rsqrt(jnp.mean(xx * xx, axis=-1, keepdims=True) + EPS)
        xn = xx * r0
        nw = nw_ref[...]
        dsh_ref[...] += jnp.sum(dh, axis=0, keepdims=True)
        dsc_ref[...] += jnp.sum(dh * (xn * nw), axis=0, keepdims=True)
        dn = dh * (1.0 + sc_ref[...])
        gnw_ref[...] += jnp.sum(dn * xn, axis=0, keepdims=True)
        dxn = dn * nw
        gx_ref[...] = dy_ref[...] + r0 * (dxn - xn * jnp.mean(dxn * xn, axis=-1, keepdims=True))

    return pl.pallas_call(
        body, name="in_proj_bwd",
        grid=(s // tm,),
        in_specs=[_rows(tm, w) for w in widths]
        + [_rows(tm, D_MODEL), _rows(tm, D_MODEL), _full((1, D_MODEL)), _full((1, D_MODEL)), _full((1, D_MODEL)),
           _full((D_MODEL, IN_COLS_PAD))],
        out_specs=[_rows(tm, D_MODEL), _rows(tm, IN_COLS_PAD),
                   _full((1, D_MODEL)), _full((1, D_MODEL)), _full((1, D_MODEL))],
        out_shape=[jax.ShapeDtypeStruct((s, D_MODEL), F32), jax.ShapeDtypeStruct((s, IN_COLS_PAD), BF16),
                   jax.ShapeDtypeStruct((1, D_MODEL), F32), jax.ShapeDtypeStruct((1, D_MODEL), F32),
                   jax.ShapeDtypeStruct((1, D_MODEL), F32)],
        compiler_params=_params(("arbitrary",), VMEM_LIMIT),
    )(*pieces, x, dy, shift, scale, norm_w, w_in_bf)


def _gw_in_call(hb, dproj):
    s = hb.shape[0]
    tk = _row_tile(s, 512)
    tn = IN_COLS_PAD // 2

    def body(h_ref, d_ref, g_ref):
        @pl.when(pl.program_id(1) == 0)
        def _():
            g_ref[...] = jnp.zeros_like(g_ref)

        g_ref[...] += _dot_tn(h_ref[...], d_ref[...])

    return pl.pallas_call(
        body, name="in_proj_wgrad",
        grid=(2, s // tk),
        in_specs=[pl.BlockSpec((tk, D_MODEL), lambda j, t: (t, 0)), pl.BlockSpec((tk, tn), lambda j, t: (t, j))],
        out_specs=pl.BlockSpec((D_MODEL, tn), lambda j, t: (0, j)),
        out_shape=jax.ShapeDtypeStruct((D_MODEL, IN_COLS_PAD), F32),
        compiler_params=_params(("arbitrary", "arbitrary"), VMEM_LIMIT),
    )(hb, dproj)


def _adamw(g, w, m, v):
    m = ADAM_B1 * m + (1.0 - ADAM_B1) * g
    v = ADAM_B2 * v + (1.0 - ADAM_B2) * (g * g)
    m_hat = m / (1.0 - ADAM_B1 ** ADAM_STEP)
    v_hat = v / (1.0 - ADAM_B2 ** ADAM_STEP)
    delta = -ADAM_LR * (m_hat / (jnp.sqrt(v_hat) + ADAM_EPS) + ADAM_WD * w)
    return delta, m, v


def _adam_shard_call(name, own, sib, w, m, v):
    r, c = w.shape
    tr = r if r <= 512 else 256

    def body(own_ref, sib_ref, w_ref, m_ref, v_ref, g_ref, d_ref, nm_ref, nv_ref):
        a = ((own_ref[0].astype(F32) + own_ref[1].astype(F32)) + own_ref[2].astype(F32)) + own_ref[3].astype(F32)
        b = ((sib_ref[0].astype(F32) + sib_ref[1].astype(F32)) + sib_ref[2].astype(F32)) + sib_ref[3].astype(F32)
        g = a + b
        g_ref[...] = g
        d_ref[...], nm_ref[...], nv_ref[...] = _adamw(g, w_ref[...], m_ref[...], v_ref[...])

    part = pl.BlockSpec((4, tr, c), lambda i: (0, i, 0))
    blk = pl.BlockSpec((tr, c), lambda i: (i, 0))
    return pl.pallas_call(
        body, name=name,
        grid=(r // tr,),
        in_specs=[part, part, blk, blk, blk],
        out_specs=[blk] * 4,
        out_shape=[jax.ShapeDtypeStruct((r, c), F32)] * 4,
        compiler_params=_params(("arbitrary",), VMEM_LIMIT),
    )(own, sib, w, m, v)


def _adam_ada_call(c_all, d_all, w, m, v):
    r, c = w.shape
    tr = 256

    def body(c_ref, d_ref, w_ref, m_ref, v_ref, g_ref, dl_ref, nm_ref, nv_ref):
        cc = c_ref[...]
        sc = cc * _sigmoid(cc)
        dd = d_ref[...]
        sc_hi = sc.astype(BF16)
        sc_lo = (sc - sc_hi.astype(F32)).astype(BF16)
        dd_hi = dd.astype(BF16)
        dd_lo = (dd - dd_hi.astype(F32)).astype(BF16)
        g = _dot_tn(sc_hi, dd_hi) + (_dot_tn(sc_hi, dd_lo) + _dot_tn(sc_lo, dd_hi))
        g_ref[...] = g
        dl_ref[...], nm_ref[...], nv_ref[...] = _adamw(g, w_ref[...], m_ref[...], v_ref[...])

    blk = pl.BlockSpec((tr, c), lambda i: (i, 0))
    return pl.pallas_call(
        body, name="adam_w_ada",
        grid=(r // tr,),
        in_specs=[pl.BlockSpec((16, tr), lambda i: (0, i)), pl.BlockSpec((16, c), lambda i: (0, 0)), blk, blk, blk],
        out_specs=[blk] * 4,
        out_shape=[jax.ShapeDtypeStruct((r, c), F32)] * 4,
        compiler_params=_params(("arbitrary",), VMEM_LIMIT),
    )(c_all, d_all, w, m, v)


def _adam_small_call(packs, w, m, v, first_row):
    rows = w.shape[0]

    def body(p_ref, w_ref, m_ref, v_ref, g_ref, d_ref, nm_ref, nv_ref):
        g = p_ref[0, first_row:first_row + rows, :]
        for b in range(1, 8):
            g = g + p_ref[b, first_row:first_row + rows, :]
        g_ref[...] = g
        d_ref[...], nm_ref[...], nv_ref[...] = _adamw(g, w_ref[...], m_ref[...], v_ref[...])

    return pl.pallas_call(
        body, name="adam_vectors",
        out_shape=[jax.ShapeDtypeStruct((rows, LANES), F32)] * 4,
    )(packs, w, m, v)


def _rope_tables(positions):
    inv_freq = ROPE_THETA ** (-jnp.arange(0, MLA_ROPE_DIM, 2, dtype=F32) / MLA_ROPE_DIM)
    ang = positions.astype(F32)[:, None] * inv_freq
    cos, sin = jnp.cos(ang), jnp.sin(ang)
    s = positions.shape[0]
    half = MLA_ROPE_DIM // 2
    zeros = lambda n: jnp.zeros((s, n), F32)
    cos_t = jnp.concatenate([jnp.ones((s, MLA_NOPE_DIM), F32), cos, cos, zeros(HEAD_PAD - MLA_QK_DIM)], axis=1)
    sin_a = jnp.concatenate([zeros(MLA_NOPE_DIM), -sin, zeros(half + HEAD_PAD - MLA_QK_DIM)], axis=1)
    sin_b = jnp.concatenate([zeros(MLA_NOPE_DIM + half), sin, zeros(HEAD_PAD - MLA_QK_DIM)], axis=1)
    return cos_t, sin_a, sin_b


def _unshard_cols(g):
    return jnp.transpose(g, (1, 0, 2)).reshape(g.shape[1], 4 * g.shape[2])


def _shard_cols(g):
    r, c4 = g.shape
    return jnp.transpose(g.reshape(r, 4, c4 // 4), (1, 0, 2))


def _pad_heads(w, width):
    r = w.shape[0]
    w = w.reshape(r, MLA_HEADS, width)
    return jnp.pad(w, ((0, 0), (0, 0), (0, HEAD_PAD - width))).reshape(r, MLA_PAD_WIDTH)


def _pad_lanes(vec):
    return jnp.pad(vec, ((0, 0), (0, LANES - vec.shape[1])))


def _vector_rows(vecs):
    rows = []
    for vec in vecs:
        n = vec.shape[1]
        pad = (-n) % LANES
        rows.append(jnp.pad(vec, ((0, 0), (0, pad))).reshape((n + pad) // LANES, LANES))
    return jnp.concatenate(rows, axis=0)


def kernel(x, c, positions, w_ada, b_ada, norm_w, w_in, q_lora_norm, w_uq, kv_lora_norm, w_ukv, q_head_norm, k_head_norm, w_out, loss_target, m_w_ada, m_b_ada, m_norm_w, m_w_in, m_q_lora_norm, m_w_uq, m_kv_lora_norm, m_w_ukv, m_q_head_norm, m_k_head_norm, m_w_out, v_w_ada, v_b_ada, v_norm_w, v_w_in, v_q_lora_norm, v_w_uq, v_kv_lora_norm, v_w_ukv, v_q_head_norm, v_k_head_norm, v_w_out):
    wa_g, win_g, wuq_g, wukv_g, wout_g = _gather_call([w.astype(BF16) for w in (w_ada, w_in, w_uq, w_ukv, w_out)])
    (sq_sum, grad_x, g_w_in, g_w_uq, g_w_ukv, g_w_out, d_ada, g_norm_w, g_qln, g_kvln, g_qhn, g_khn) = _local_step(
        x[0], c, positions[0], loss_target[0], _unshard_cols(wa_g), b_ada, norm_w, _unshard_cols(win_g),
        q_lora_norm, _unshard_cols(wuq_g), kv_lora_norm, _unshard_cols(wukv_g), q_head_norm, k_head_norm,
        wout_g.reshape(D_MODEL, D_MODEL))
    loss = lax.psum(0.5 * sq_sum / D_MODEL, ("x", "y", "c"))

    grads = [g.astype(BF16) for g in (_shard_cols(g_w_in), _shard_cols(g_w_uq), _shard_cols(g_w_ukv),
                                      g_w_out.reshape(4, D_MODEL // 4, D_MODEL))]
    small = _vector_rows([c, d_ada, g_norm_w, g_qln, g_kvln, g_qhn, g_khn])
    small = jnp.pad(small, ((0, (-small.shape[0]) % 8), (0, 0)))
    own, sib, packs = _exchange_call(grads, small)

    names = ["adam_w_in", "adam_w_uq", "adam_w_ukv", "adam_w_out"]
    shard_w = [(w_in, m_w_in, v_w_in), (w_uq, m_w_uq, v_w_uq), (w_ukv, m_w_ukv, v_w_ukv),
               (w_out, m_w_out, v_w_out)]
    res = {}
    for name, o_g, s_g, (w, m, v) in zip(names, own, sib, shard_w):
        res[name] = _adam_shard_call(name, o_g, s_g, w[0], m[0], v[0])
    chip = 2 * lax.axis_index("x") + lax.axis_index("y")
    ada_cols = w_ada.shape[2]
    c_rows = D_MODEL // LANES
    c_all = jnp.pad(packs[:, :c_rows, :].reshape(8, D_MODEL), ((0, 8), (0, 0)))
    d_rows = packs[:, c_rows:c_rows + 3 * c_rows, :].reshape(8, 3 * D_MODEL)
    d_all = jnp.pad(lax.dynamic_slice_in_dim(d_rows, chip * ada_cols, ada_cols, axis=1), ((0, 8), (0, 0)))
    res_ada = _adam_ada_call(c_all, d_all, w_ada[0], m_w_ada[0], v_w_ada[0])
    vec_names = [(b_ada, m_b_ada, v_b_ada), (norm_w, m_norm_w, v_norm_w), (q_lora_norm, m_q_lora_norm, v_q_lora_norm),
                 (kv_lora_norm, m_kv_lora_norm, v_kv_lora_norm), (q_head_norm, m_q_head_norm, v_q_head_norm),
                 (k_head_norm, m_k_head_norm, v_k_head_norm)]
    packed = [_vector_rows([t[j] for t in vec_names]) for j in range(3)]
    n_rows = packed[0].shape[0]
    pad_rows = (-n_rows) % 8
    packed = [jnp.pad(p, ((0, pad_rows), (0, 0))) for p in packed]
    res_vec = _adam_small_call(packs, packed[0], packed[1], packed[2], c_rows)

    def unpack(arr):
        outs, r0 = [], 0
        for t in vec_names:
            n = t[0].shape[1]
            nr = -(-n // LANES)
            outs.append(arr[r0:r0 + nr].reshape(1, nr * LANES)[:, :n])
            r0 += nr
        return outs

    vec_out = [unpack(a) for a in res_vec]

    def ordered(kind):
        big = lambda name: res[name][kind][None]
        return [res_ada[kind][None], vec_out[kind][0], vec_out[kind][1], big("adam_w_in"), vec_out[kind][2],
                big("adam_w_uq"), vec_out[kind][3], big("adam_w_ukv"), vec_out[kind][4], vec_out[kind][5],
                big("adam_w_out")]

    return (loss, grad_x[None], *ordered(0), *ordered(1), *ordered(2), *ordered(3))


def _local_step(x2, c, positions, tgt, w_ada_full, b_ada, norm_w, w_in_full, q_lora_norm, w_uq_full,
                kv_lora_norm, w_ukv_full, q_head_norm, k_head_norm, w_out_full):
    w_ada_bf = w_ada_full.astype(BF16)
    kr_block = jnp.pad(w_in_full[:, 2688:2720], ((0, 0), (KR_LANE, LANES - KR_LANE - MLA_ROPE_DIM)))
    w_in_bf = jnp.concatenate([w_in_full[:, :2688], w_in_full[:, 2720:], kr_block], axis=1).astype(BF16)
    w_uq_bf = _pad_heads(w_uq_full, MLA_QK_DIM).astype(BF16)
    w_ukv_heads = w_ukv_full.reshape(KV_LORA_RANK, MLA_HEADS, 2 * MLA_NOPE_DIM)
    w_uk_bf = _pad_heads(w_ukv_heads[:, :, :MLA_NOPE_DIM].reshape(KV_LORA_RANK, -1), MLA_NOPE_DIM).astype(BF16)
    w_uv_bf = w_ukv_heads[:, :, MLA_NOPE_DIM:].reshape(KV_LORA_RANK, MLA_WIDTH).astype(BF16)
    w_out_bf = w_out_full.astype(BF16)
    qhn_pad, khn_pad = _pad_lanes(q_head_norm), _pad_lanes(k_head_norm)
    cos_t, sin_a, sin_b = _rope_tables(positions)

    ada = _ada_call(c, w_ada_bf, b_ada)
    shift, scale, gate = ada[:, :D_MODEL], ada[:, D_MODEL:2 * D_MODEL], ada[:, 2 * D_MODEL:]
    hb, q_sb, k_sb, v_sb, g_sb, c_q, c_kv, g_mla, k_rope = _pre_call(x2, shift, scale, norm_w, w_in_bf)
    q_m, k_m, v_m, cqn, ckvn, q0, k0 = _mla_prep_call(
        c_q, c_kv, k_rope, cos_t, sin_a, sin_b, q_lora_norm, kv_lora_norm, qhn_pad, khn_pad,
        w_uq_bf, w_uk_bf, w_uv_bf)
    o_sb, r_sb, kstart = _sb_fwd_call(q_sb, k_sb, v_sb)
    o_mla, lse = _mla_fwd_call(q_m, k_m, v_m)
    do_sb, do_mla, dg_sb, dg_mla, dy, g_w_out, d_gate, sq = _out_call(
        o_sb, g_sb, o_mla, g_mla, x2, tgt, gate, w_out_bf)

    dq_sb, dk_sb, dv_sb = _sb_bwd_call(kstart, q_sb, k_sb, v_sb, do_sb, r_sb)
    dq_m, dk_m, dv_m = _mla_bwd_call(q_m, k_m, v_m, do_mla, o_mla, lse)
    (d_cq, d_ckv, d_kr, g_wuq_pad, g_wuk_pad, g_wuv, g_qln, g_kvln, g_qhn, g_khn) = _mla_prep_bwd_call(
        dq_m, dk_m, dv_m, q0, k0, cqn, ckvn, c_q, c_kv, cos_t, sin_a, sin_b,
        q_lora_norm, kv_lora_norm, qhn_pad, khn_pad, w_uq_bf, w_uk_bf, w_uv_bf)
    grad_x, dproj, d_shift, d_scale, g_norm_w = _dh_call(
        [dq_sb, dk_sb, dv_sb, dg_sb, d_cq, d_ckv, dg_mla, d_kr], x2, dy, shift, scale, norm_w, w_in_bf)
    g_win_pad = _gw_in_call(hb, dproj)

    g_w_in = jnp.concatenate([g_win_pad[:, :2688],
                              g_win_pad[:, C_KR + KR_LANE:C_KR + KR_LANE + MLA_ROPE_DIM],
                              g_win_pad[:, 2688:3200]], axis=1)
    g_w_uq = g_wuq_pad.reshape(Q_LORA_RANK, MLA_HEADS, HEAD_PAD)[:, :, :MLA_QK_DIM].reshape(Q_LORA_RANK, -1)
    g_w_ukv = jnp.concatenate(
        [g_wuk_pad.reshape(KV_LORA_RANK, MLA_HEADS, HEAD_PAD)[:, :, :MLA_NOPE_DIM],
         g_wuv.reshape(KV_LORA_RANK, MLA_HEADS, MLA_NOPE_DIM)], axis=2).reshape(KV_LORA_RANK, -1)
    d_ada = jnp.concatenate([d_shift, d_scale, d_gate], axis=1)
    return (jnp.sum(sq), grad_x, g_w_in, g_w_uq, g_w_ukv, g_w_out, d_ada, g_norm_w, g_qln, g_kvln,
            g_qhn[:, :MLA_QK_DIM], g_khn[:, :MLA_QK_DIM])
```

```python
import functools
import math

import jax
import jax.numpy as jnp
from jax import lax
from jax.experimental import pallas as pl
from jax.experimental.pallas import tpu as pltpu

F32 = jnp.float32
BF16 = jnp.bfloat16
I32 = jnp.int32

D_MODEL = 1024
SB_HEADS = 8
SB_WIDTH = 512
MLA_HEADS = 8
MLA_QK_DIM = 96
MLA_NOPE_DIM = 64
MLA_ROPE_DIM = 32
MLA_WIDTH = 512
Q_LORA_RANK = 384
KV_LORA_RANK = 256
ROPE_THETA = 10000.0
EPS = 1e-6
LANES = 128
HEAD_PAD = 128
MLA_PAD_WIDTH = MLA_HEADS * HEAD_PAD

C_Q, C_K, C_V, C_G = 0, 512, 1024, 1536
C_CQ, C_CKV, C_GM, C_KR = 2048, 2432, 2688, 3200
IN_COLS_PAD = 3328

ADAM_LR = 0.001
ADAM_B1 = 0.9
ADAM_B2 = 0.999
ADAM_EPS = 1e-08
ADAM_WD = 0.01
ADAM_STEP = 10

SB_SCALE = 0.125
SB_GROUP = 4
MLA_SCALE = 1.0 / math.sqrt(MLA_QK_DIM)
LN2 = math.log(2.0)
MLA_SCALE_LOG2 = MLA_SCALE / LN2
MLA_BQ = 512
MLA_BWD_BQ = 512
MLA_BK = 1024
MLA_BWD_BK = 512
SB_DEAD = -104.0
MASK_NEG = -1e30

VMEM_LIMIT = 56 * 1024 * 1024
MESH = pl.DeviceIdType.MESH


def _dot(a, b):
    return jnp.dot(a, b, preferred_element_type=F32)


def _dot_nt(a, b):
    return lax.dot_general(a, b, (((1,), (1,)), ((), ())), preferred_element_type=F32)


def _dot_tn(a, b):
    return lax.dot_general(a, b, (((0,), (0,)), ((), ())), preferred_element_type=F32)


def _sigmoid(x):
    return 1.0 / (1.0 + jnp.exp(-x))


def _split_dot(a, m):
    hi = a.astype(BF16)
    lo = (a - hi.astype(F32)).astype(BF16)
    return _dot(hi, m) + _dot(lo, m)


def _params(sem, vmem=None):
    return pltpu.CompilerParams(dimension_semantics=sem, vmem_limit_bytes=vmem)


def _row_tile(s, want):
    return min(want, s)


def _hbm_spec():
    return pl.BlockSpec(memory_space=pltpu.HBM)


def _gather_call(shards):
    n = len(shards)
    halves = [s.shape[1] // 2 for s in shards]

    def body(*refs):
        ins, outs = refs[:n], refs[n:2 * n]
        ici_send, ici_recv, d2d_send, d2d_recv, loc_sems = refs[2 * n:]
        x, y, c = lax.axis_index("x"), lax.axis_index("y"), lax.axis_index("c")
        me = 2 * x + y
        peers = [(1 - x, y), (x, 1 - y), (1 - x, 1 - y)]

        def rows(a, which):
            return pl.ds(pl.multiple_of(which * halves[a], 16), halves[a])

        def ici(a, j, slot):
            px, py = peers[j]
            return pltpu.make_async_remote_copy(
                src_ref=ins[a].at[0, rows(a, c)], dst_ref=outs[a].at[slot, rows(a, c)],
                send_sem=ici_send.at[3 * a + j], recv_sem=ici_recv.at[3 * a + j],
                device_id=(px, py, c), device_id_type=MESH)

        def d2d(a, j, which):
            px, py = peers[j]
            piece = outs[a].at[2 * px + py, rows(a, which)]
            return pltpu.make_async_remote_copy(
                src_ref=piece, dst_ref=piece,
                send_sem=d2d_send.at[3 * a + j], recv_sem=d2d_recv.at[3 * a + j],
                device_id=(x, y, 1 - c), device_id_type=MESH)

        local = [pltpu.make_async_copy(ins[a].at[0], outs[a].at[me], loc_sems.at[a]) for a in range(n)]
        for cp in local:
            cp.start()
        sends = [ici(a, j, me) for a in range(n) for j in range(3)]
        for cp in sends:
            cp.start()
        for a in range(n):
            for j in range(3):
                px, py = peers[j]
                ici(a, j, 2 * px + py).wait_recv()
                cp = d2d(a, j, c)
                cp.start()
                sends.append(cp)
        for a in range(n):
            for j in range(3):
                d2d(a, j, 1 - c).wait_recv()
        for cp in sends:
            cp.wait_send()
        for cp in local:
            cp.wait()

    return pl.pallas_call(
        body, name="gather_weights",
        out_shape=[jax.ShapeDtypeStruct((4,) + s.shape[1:], s.dtype) for s in shards],
        in_specs=[_hbm_spec() for _ in shards],
        out_specs=[_hbm_spec() for _ in shards],
        scratch_shapes=[pltpu.SemaphoreType.DMA((3 * n,)), pltpu.SemaphoreType.DMA((3 * n,)),
                        pltpu.SemaphoreType.DMA((3 * n,)), pltpu.SemaphoreType.DMA((3 * n,)),
                        pltpu.SemaphoreType.DMA((n,))],
    )(*shards)


def _exchange_call(grads, small):
    n = len(grads)

    def body(*refs):
        g_in, small_in = refs[:n], refs[n]
        own, sib, packs = refs[n + 1:2 * n + 1], refs[2 * n + 1:3 * n + 1], refs[3 * n + 1]
        ici_send, ici_recv, d2d_send, d2d_recv, sm_send, sm_recv, loc_sems = refs[3 * n + 2:]
        x, y, c = lax.axis_index("x"), lax.axis_index("y"), lax.axis_index("c")
        me = 2 * x + y
        me8 = 4 * x + 2 * y + c
        sibling = (x, y, 1 - c)
        peers = [(1 - x, y), (x, 1 - y), (1 - x, 1 - y)]
        flips = [(fx, fy, fc) for fx in (0, 1) for fy in (0, 1) for fc in (0, 1)][1:]

        def ici(a, j, src_slot, dst_slot):
            px, py = peers[j]
            return pltpu.make_async_remote_copy(
                src_ref=g_in[a].at[src_slot], dst_ref=own[a].at[dst_slot],
                send_sem=ici_send.at[3 * a + j], recv_sem=ici_recv.at[3 * a + j],
                device_id=(px, py, c), device_id_type=MESH)

        def d2d(a, rel, chip, src):
            return pltpu.make_async_remote_copy(
                src_ref=src, dst_ref=sib[a].at[chip],
                send_sem=d2d_send.at[4 * a + rel], recv_sem=d2d_recv.at[4 * a + rel],
                device_id=sibling, device_id_type=MESH)

        def flipped(r):
            fx, fy, fc = flips[r]
            return ((1 - x) if fx else x, (1 - y) if fy else y, (1 - c) if fc else c)

        def sm(r, slot):
            return pltpu.make_async_remote_copy(
                src_ref=small_in, dst_ref=packs.at[slot],
                send_sem=sm_send.at[r], recv_sem=sm_recv.at[r],
                device_id=flipped(r), device_id_type=MESH)

        def peer8(r):
            px, py, pc = flipped(r)
            return 4 * px + 2 * py + pc

        local = [pltpu.make_async_copy(g_in[a].at[me], own[a].at[me], loc_sems.at[a]) for a in range(n)]
        local.append(pltpu.make_async_copy(small_in, packs.at[me8], loc_sems.at[n]))
        for cp in local:
            cp.start()
        sends = []
        for r in range(7):
            sends.append(sm(r, me8))
        for a in range(n):
            for j in range(3):
                px, py = peers[j]
                sends.append(ici(a, j, 2 * px + py, me))
        for cp in sends:
            cp.start()
        for a in range(n):
            cp = d2d(a, 0, me, g_in[a].at[me])
            cp.start()
            sends.append(cp)
        for a in range(n):
            for j in range(3):
                px, py = peers[j]
                ici(a, j, me, 2 * px + py).wait_recv()
                cp = d2d(a, 1 + j, 2 * px + py, own[a].at[2 * px + py])
                cp.start()
                sends.append(cp)
        for a in range(n):
            d2d(a, 0, me, g_in[a].at[me]).wait_recv()
            for j in range(3):
                px, py = peers[j]
                d2d(a, 1 + j, 2 * px + py, g_in[a].at[me]).wait_recv()
        for r in range(7):
            sm(r, peer8(r)).wait_recv()
        for cp in sends:
            cp.wait_send()
        for cp in local:
            cp.wait()

    out_shape = ([jax.ShapeDtypeStruct(g.shape, g.dtype) for g in grads] * 2
                 + [jax.ShapeDtypeStruct((8,) + small.shape, small.dtype)])
    res = pl.pallas_call(
        body, name="exchange_grads",
        out_shape=out_shape,
        in_specs=[_hbm_spec() for _ in range(n + 1)],
        out_specs=[_hbm_spec() for _ in range(2 * n + 1)],
        scratch_shapes=[pltpu.SemaphoreType.DMA((3 * n,)), pltpu.SemaphoreType.DMA((3 * n,)),
                        pltpu.SemaphoreType.DMA((4 * n,)), pltpu.SemaphoreType.DMA((4 * n,)),
                        pltpu.SemaphoreType.DMA((7,)), pltpu.SemaphoreType.DMA((7,)),
                        pltpu.SemaphoreType.DMA((n + 1,))],
    )(*grads, small)
    return res[:n], res[n:2 * n], res[2 * n]


def _ada_call(c, w_ada_bf, b_ada):
    def body(c_ref, w_ref, b_ref, o_ref):
        cc = c_ref[...]
        sc = jnp.broadcast_to(cc * _sigmoid(cc), (8, D_MODEL)).astype(BF16)
        o_ref[...] = _dot(sc, w_ref[...])[0:1] + b_ref[...]

    return pl.pallas_call(
        body, name="ada_fwd",
        out_shape=jax.ShapeDtypeStruct((1, 3 * D_MODEL), F32),
        compiler_params=pltpu.CompilerParams(vmem_limit_bytes=VMEM_LIMIT),
    )(c, w_ada_bf, b_ada)


def _ada_part(j):
    return pl.BlockSpec((1, D_MODEL), lambda i: (0, j))


def _full(shape):
    return pl.BlockSpec(shape, lambda i: (0,) * len(shape))


def _rows(tm, width):
    return pl.BlockSpec((tm, width), lambda i: (i, 0))


def _pre_call(x, ada, norm_w, w_in_bf):
    s = x.shape[0]
    tm = _row_tile(s, 512)
    groups = [(C_Q, 512, BF16), (C_K, 512, BF16), (C_V, 512, BF16), (C_G, 512, F32),
              (C_CQ, Q_LORA_RANK, F32), (C_CKV, KV_LORA_RANK, F32), (C_GM, 512, F32), (C_KR, LANES, F32)]

    def body(x_ref, sh_ref, sc_ref, nw_ref, w_ref, hb_ref, *outs):
        xx = x_ref[...]
        r0 = lax.rsqrt(jnp.mean(xx * xx, axis=-1, keepdims=True) + EPS)
        h = (xx * r0 * nw_ref[...]) * (1.0 + sc_ref[...]) + sh_ref[...]
        hb = h.astype(BF16)
        hb_ref[...] = hb
        for (c0, width, dt), o_ref in zip(groups, outs):
            o_ref[...] = _dot(hb, w_ref[:, c0:c0 + width]).astype(dt)

    return pl.pallas_call(
        body, name="pre_proj",
        grid=(s // tm,),
        in_specs=[_rows(tm, D_MODEL), _ada_part(0), _ada_part(1), _full((1, D_MODEL)),
                  _full((D_MODEL, IN_COLS_PAD))],
        out_specs=[_rows(tm, D_MODEL)] + [_rows(tm, w) for _, w, _ in groups],
        out_shape=[jax.ShapeDtypeStruct((s, D_MODEL), BF16)]
        + [jax.ShapeDtypeStruct((s, w), dt) for _, w, dt in groups],
        compiler_params=_params(("arbitrary",), VMEM_LIMIT),
    )(x, ada, ada, norm_w, w_in_bf)


def _rope(t, cos_t, sin_t):
    return t * cos_t + pltpu.roll(t, 64, 1) * sin_t


def _rope_adjoint(d, cos_t, sin_t):
    return d * cos_t + pltpu.roll(d * sin_t, 64, 1)


def _mla_prep_call(c_q, c_kv, k_rope, cos_t, sin_t, q_lora_norm, kv_lora_norm, qhn_pad, khn_pad,
                   w_uq_bf, w_uk_bf, w_uv_bf):
    s = c_q.shape[0]
    tm = _row_tile(s, 256)

    def body(cq_ref, ckv_ref, kr_ref, cos_ref, sin_ref, qln_ref, kvln_ref, qhn_ref, khn_ref,
             wuq_ref, wuk_ref, wuv_ref, q_ref, k_ref, v_ref, cqn_ref, ckvn_ref, q0_ref, k0_ref):
        cq = cq_ref[...]
        cqn = (cq * lax.rsqrt(jnp.mean(cq * cq, axis=-1, keepdims=True) + EPS) * qln_ref[...]).astype(BF16)
        cqn_ref[...] = cqn
        ckv = ckv_ref[...]
        ckvn = (ckv * lax.rsqrt(jnp.mean(ckv * ckv, axis=-1, keepdims=True) + EPS) * kvln_ref[...]).astype(BF16)
        ckvn_ref[...] = ckvn
        v_ref[...] = _dot(ckvn, wuv_ref[...]).astype(BF16)
        q0_ref[...] = _dot(cqn, wuq_ref[...])
        k0_ref[...] = _dot(ckvn, wuk_ref[...])
        cos_t, sin_t = cos_ref[...], sin_ref[...]
        kr = kr_ref[...]
        for h in range(MLA_HEADS):
            cols = slice(h * HEAD_PAD, (h + 1) * HEAD_PAD)
            q0 = q0_ref[:, cols]
            rq = lax.rsqrt(jnp.sum(q0 * q0, axis=-1, keepdims=True) * (1.0 / MLA_QK_DIM) + EPS)
            q_ref[:, cols] = (_rope(q0 * rq * qhn_ref[...], cos_t, sin_t) * MLA_SCALE_LOG2).astype(BF16)
            k0 = k0_ref[:, cols] + kr
            k0_ref[:, cols] = k0
            rk = lax.rsqrt(jnp.sum(k0 * k0, axis=-1, keepdims=True) * (1.0 / MLA_QK_DIM) + EPS)
            k_ref[:, cols] = _rope(k0 * rk * khn_ref[...], cos_t, sin_t).astype(BF16)

    return pl.pallas_call(
        body, name="mla_prep",
        grid=(s // tm,),
        in_specs=[_rows(tm, Q_LORA_RANK), _rows(tm, KV_LORA_RANK), _rows(tm, LANES),
                  _rows(tm, LANES), _rows(tm, LANES),
                  _full((1, Q_LORA_RANK)), _full((1, KV_LORA_RANK)), _full((1, LANES)), _full((1, LANES)),
                  _full((Q_LORA_RANK, MLA_PAD_WIDTH)), _full((KV_LORA_RANK, MLA_PAD_WIDTH)),
                  _full((KV_LORA_RANK, MLA_WIDTH))],
        out_specs=[_rows(tm, MLA_PAD_WIDTH), _rows(tm, MLA_PAD_WIDTH), _rows(tm, MLA_WIDTH),
                   _rows(tm, Q_LORA_RANK), _rows(tm, KV_LORA_RANK),
                   _rows(tm, MLA_PAD_WIDTH), _rows(tm, MLA_PAD_WIDTH)],
        out_shape=[jax.ShapeDtypeStruct((s, MLA_PAD_WIDTH), BF16), jax.ShapeDtypeStruct((s, MLA_PAD_WIDTH), BF16),
                   jax.ShapeDtypeStruct((s, MLA_WIDTH), BF16),
                   jax.ShapeDtypeStruct((s, Q_LORA_RANK), BF16), jax.ShapeDtypeStruct((s, KV_LORA_RANK), BF16),
                   jax.ShapeDtypeStruct((s, MLA_PAD_WIDTH), F32), jax.ShapeDtypeStruct((s, MLA_PAD_WIDTH), F32)],
        compiler_params=_params(("arbitrary",), VMEM_LIMIT),
    )(c_q, c_kv, k_rope, cos_t, sin_t, q_lora_norm, kv_lora_norm, qhn_pad, khn_pad,
      w_uq_bf, w_uk_bf, w_uv_bf)


def _log_sigmoid_pair(z):
    ls = jnp.minimum(z, 0.0) - jnp.log(1.0 + jnp.exp(-jnp.abs(z)))
    return ls, ls - z


def _pair(hh):
    return slice((hh // 2) * LANES, (hh // 2 + 1) * LANES)


def _sb_fwd_call(q, k, v):
    s = q.shape[0]
    bq = _row_tile(s, 256)
    nq = s // bq
    nh = SB_GROUP
    width = nh * 64

    def body(q_ref, k_ref, v_ref, o_ref, r_ref, ks_ref):
        hp, i = pl.program_id(0), pl.program_id(1)
        lane = lax.broadcasted_iota(I32, (bq, LANES), 1)
        row = lax.broadcasted_iota(I32, (bq, bq), 0)
        col = lax.broadcasted_iota(I32, (bq, bq), 1)
        strict = col < row
        later = jnp.where(row > col, 1.0, 0.0).astype(BF16)
        masks = [_head_mask(lane, hh).astype(BF16) for hh in range(2)]
        qms = [q_ref[:, _pair(hh)] * jnp.asarray(SB_SCALE, BF16) * masks[hh % 2] for hh in range(nh)]

        def walk(blocks, state):
            chains = [(kb, diagonal, hh) for kb, diagonal in blocks for hh in range(nh)]
            keys = lambda kb: pl.ds(pl.multiple_of(kb * bq, bq), bq)
            zs = [_dot_nt(qms[hh], k_ref[keys(kb), _pair(hh)]) for kb, _, hh in chains]
            pairs = []
            for z, (_, diagonal, _) in zip(zs, chains):
                ls, lk = _log_sigmoid_pair(z)
                pairs.append((ls, jnp.where(strict, lk, 0.0) if diagonal else lk))
            sums = [_split_dot(lk, later) for _, lk in pairs]
            runs = [st[0] for st in state]
            ws = []
            for (ls, lk), after, (_, diagonal, hh) in zip(pairs, sums, chains):
                w = jnp.exp(ls + (after + runs[hh]))
                ws.append((jnp.where(strict, w, 0.0) if diagonal else w).astype(BF16))
                runs[hh] = runs[hh] + jnp.sum(lk, axis=1, keepdims=True)
            accs = [st[1] for st in state]
            for w, (kb, _, hh) in zip(ws, chains):
                accs[hh] = accs[hh] + _dot(w, v_ref[keys(kb), _pair(hh)])
            return tuple(zip(runs, accs))

        def alive(state):
            top = jnp.max(state[0][0])
            for st in state[1:]:
                top = jnp.maximum(top, jnp.max(st[0]))
            return (top > SB_DEAD).astype(I32)

        def finish(state, first):
            ks_ref[hp, i] = first
            for pair in range(nh // 2):
                o_ref[:, _pair(2 * pair)] = jnp.where(lane < 64, state[2 * pair][1], state[2 * pair + 1][1])
                r_ref[:, _pair(2 * pair)] = jnp.where(lane < 64, state[2 * pair][0], state[2 * pair + 1][0])

        zero = ((jnp.zeros((bq, 1), F32), jnp.zeros((bq, LANES), F32)),) * nh

        @pl.when(i == 0)
        def _():
            finish(walk([(0, True)], zero), 0)

        @pl.when(i > 0)
        def _():
            state = walk([(i, True), (i - 1, False)], zero)

            def cond(carry):
                return jnp.logical_and(carry[0] >= 0, carry[1] > 0)

            def step(carry):
                state = walk([(carry[0], False)], carry[2])
                return carry[0] - 1, alive(state), state

            kb, _, state = lax.while_loop(cond, step, (i - 2, alive(state), state))
            finish(state, kb + 1)

    return pl.pallas_call(
        body, name="sb_fwd",
        grid=(SB_HEADS // nh, nq),
        in_specs=[pl.BlockSpec((bq, width), lambda h, i: (i, h)),
                  pl.BlockSpec((s, width), lambda h, i: (0, h)),
                  pl.BlockSpec((s, width), lambda h, i: (0, h))],
        out_specs=[pl.BlockSpec((bq, width), lambda h, i: (i, h)),
                   pl.BlockSpec((bq, width), lambda h, i: (i, h)),
                   pl.BlockSpec(memory_space=pltpu.SMEM)],
        out_shape=[jax.ShapeDtypeStruct((s, SB_WIDTH), F32), jax.ShapeDtypeStruct((s, SB_WIDTH), F32),
                   jax.ShapeDtypeStruct((SB_HEADS // nh, nq), I32)],
        compiler_params=_params(("arbitrary", "arbitrary"), VMEM_LIMIT),
    )(q, k, v)


def _mla_fwd_call(q, k, v):
    s = q.shape[0]
    bq = _row_tile(s, MLA_BQ)
    bk = _row_tile(s, MLA_BK)
    nq = s // bq
    assert bk % bq == 0
    half = bk // 2

    def body(q_ref, k_ref, v_ref, o_ref, lse_ref, p_ref, s_ref):
        i = pl.program_id(1)
        lane = lax.broadcasted_iota(I32, (bq, LANES), 1)
        row = lax.broadcasted_iota(I32, (half, bq), 1)
        col = lax.broadcasted_iota(I32, (half, bq), 0)
        n_full = (i * bq) // bk

        def keys(g):
            return pl.ds(pl.multiple_of(g * half, half), half)

        def put_scores(g, slot):
            for hh in range(2):
                cols = slice(hh * HEAD_PAD, (hh + 1) * HEAD_PAD)
                s_ref[slot, hh] = _dot_nt(k_ref[keys(g), cols], q_ref[:, cols])

        def add_pv(carry, g, slot):
            vblk = v_ref[keys(g), :]
            return tuple((m, l, alpha * acc + _dot_tn(vblk, p_ref[slot, hh]), alpha)
                         for hh, (m, l, acc, alpha) in enumerate(carry))

        def substep(g, slot, carry, masked, prefetch):
            if prefetch:
                put_scores(g + 1, 1 - slot)
            carry = add_pv(carry, jnp.maximum(g - 1, 0), 1 - slot)
            new = []
            for hh in range(2):
                m, l, acc, _ = carry[hh]
                sc = s_ref[slot, hh]
                if masked:
                    sc = jnp.where(col + g * half <= row + i * bq, sc, MASK_NEG)
                m_new = jnp.maximum(m, jnp.max(sc, axis=0, keepdims=True))
                p = jnp.exp2(sc - m_new)
                alpha = jnp.exp2(m - m_new)
                l = alpha * l + jnp.sum(p, axis=0, keepdims=True)
                p_ref[slot, hh] = p.astype(BF16)
                new.append((m_new, l, acc, alpha))
            return tuple(new)

        def chunk(kb, carry, masked):
            carry = substep(2 * kb, 0, carry, masked, True)
            return substep(2 * kb + 1, 1, carry, masked, not masked)

        p_ref[1] = jnp.zeros_like(p_ref[1])
        put_scores(0, 0)
        one = (jnp.full((1, bq), MASK_NEG, F32), jnp.zeros((1, bq), F32), jnp.zeros((LANES, bq), F32),
               jnp.ones((1, bq), F32))
        carry = lax.fori_loop(0, n_full, lambda kb, cr: chunk(kb, cr, False), (one, one))
        carry = chunk(n_full, carry, True)
        (m0, l0, a0, _), (m1, l1, a1, _) = add_pv(carry, 2 * n_full + 1, 1)
        o_ref[...] = jnp.where(lane < 64, (a0 / l0).T, (a1 / l1).T)
        sub = lax.broadcasted_iota(I32, (8, bq), 0)
        lse_ref[...] = jnp.where(sub == 0, m0 + jnp.log2(l0), jnp.where(sub == 1, m1 + jnp.log2(l1), 0.0))

    return pl.pallas_call(
        body, name="mla_fwd",
        grid=(4, nq),
        in_specs=[pl.BlockSpec((bq, 2 * HEAD_PAD), lambda h, i: (i, h)),
                  pl.BlockSpec((s, 2 * HEAD_PAD), lambda h, i: (0, h)),
                  pl.BlockSpec((s, LANES), lambda h, i: (0, h))],
        out_specs=[pl.BlockSpec((bq, LANES), lambda h, i: (i, h)),
                   pl.BlockSpec((None, 8, bq), lambda h, i: (h, 0, i))],
        out_shape=[jax.ShapeDtypeStruct((s, MLA_WIDTH), F32), jax.ShapeDtypeStruct((4, 8, s), F32)],
        scratch_shapes=[pltpu.VMEM((2, 2, half, bq), BF16), pltpu.VMEM((2, 2, half, bq), F32)],
        compiler_params=_params(("arbitrary", "arbitrary"), VMEM_LIMIT),
    )(q, k, v)


def _out_call(o_sb, g_sb, o_mla, g_mla, x, target, ada, w_out_bf):
    s = x.shape[0]
    tm = _row_tile(s, 256)

    def body(osb_ref, gsb_ref, oml_ref, gml_ref, x_ref, t_ref, gate_ref, w_ref,
             dosb_ref, doml_ref, dgsb_ref, dgml_ref, dy_ref, gw_ref, dgate_ref, sq_ref):
        @pl.when(pl.program_id(0) == 0)
        def _():
            gw_ref[...] = jnp.zeros_like(gw_ref)
            dgate_ref[...] = jnp.zeros_like(dgate_ref)
            sq_ref[...] = jnp.zeros_like(sq_ref)

        g_s, g_m = gsb_ref[...], gml_ref[...]
        sig_s, sig_m = _sigmoid(g_s), _sigmoid(g_m)
        silu_s, silu_m = g_s * sig_s, g_m * sig_m
        o_s, o_m = osb_ref[...], oml_ref[...]
        mixed = jnp.concatenate([o_s * silu_s, o_m * silu_m], axis=1).astype(BF16)
        u = _dot(mixed, w_ref[...])
        gate_v = gate_ref[...]
        err = x_ref[...] + gate_v * u - t_ref[...]
        sq_ref[...] += jnp.sum(err * err, axis=0, keepdims=True)
        dy = err * (1.0 / D_MODEL)
        dy_ref[...] = dy
        dgate_ref[...] += jnp.sum(dy * u, axis=0, keepdims=True)
        du = (dy * gate_v).astype(BF16)
        gw_ref[...] += _dot_tn(mixed, du)
        dmix = _dot_nt(du, w_ref[...])
        dm_s, dm_m = dmix[:, :SB_WIDTH], dmix[:, SB_WIDTH:]
        dosb_ref[...] = (dm_s * silu_s).astype(BF16)
        doml_ref[...] = (dm_m * silu_m).astype(BF16)
        dgsb_ref[...] = (dm_s * o_s * (sig_s * (1.0 + g_s * (1.0 - sig_s)))).astype(BF16)
        dgml_ref[...] = (dm_m * o_m * (sig_m * (1.0 + g_m * (1.0 - sig_m)))).astype(BF16)

    return pl.pallas_call(
        body, name="out_proj_loss",
        grid=(s // tm,),
        in_specs=[_rows(tm, 512), _rows(tm, 512), _rows(tm, 512), _rows(tm, 512),
                  _rows(tm, D_MODEL), _rows(tm, D_MODEL), _ada_part(2), _full((D_MODEL, D_MODEL))],
        out_specs=[_rows(tm, 512), _rows(tm, 512), _rows(tm, 512), _rows(tm, 512), _rows(tm, D_MODEL),
                   _full((D_MODEL, D_MODEL)), _full((1, D_MODEL)), _full((1, D_MODEL))],
        out_shape=[jax.ShapeDtypeStruct((s, 512), BF16)] * 4
        + [jax.ShapeDtypeStruct((s, D_MODEL), F32), jax.ShapeDtypeStruct((D_MODEL, D_MODEL), F32),
           jax.ShapeDtypeStruct((1, D_MODEL), F32), jax.ShapeDtypeStruct((1, D_MODEL), F32)],
        compiler_params=_params(("arbitrary",), VMEM_LIMIT),
    )(o_sb, g_sb, o_mla, g_mla, x, target, ada, w_out_bf)


def _head_mask(lane, hh):
    return jnp.where((lane >= 64) if hh else (lane < 64), 1.0, 0.0)


def _pick_lane(packed, lane, which):
    return jnp.sum(jnp.where(lane == which, packed, 0.0), axis=1, keepdims=True)


def _sb_bwd_call(kstart, q, k, v, do, rfin):
    s = q.shape[0]
    bq = _row_tile(s, 256)
    nq = s // bq
    nh = SB_GROUP
    width = nh * 64

    def body(ks_ref, q_ref, k_ref, v_ref, do_ref, r_ref, dq_ref, dk_ref, dv_ref):
        hp, i = pl.program_id(0), pl.program_id(1)

        @pl.when(i == 0)
        def _():
            dk_ref[...] = jnp.zeros_like(dk_ref)
            dv_ref[...] = jnp.zeros_like(dv_ref)

        lane = lax.broadcasted_iota(I32, (bq, LANES), 1)
        row = lax.broadcasted_iota(I32, (bq, bq), 0)
        col = lax.broadcasted_iota(I32, (bq, bq), 1)
        upto = jnp.where(row <= col, 1.0, 0.0).astype(BF16)
        before = jnp.where(row < col, 1.0, 0.0).astype(BF16)
        masks = [_head_mask(lane, hh).astype(BF16) for hh in range(2)]
        qms = [q_ref[:, _pair(hh)] * jnp.asarray(SB_SCALE, BF16) * masks[hh % 2] for hh in range(nh)]
        doms = [do_ref[:, _pair(hh)] * masks[hh % 2] for hh in range(nh)]
        totals = [_pick_lane(r_ref[:, _pair(hh)], lane, 64 * (hh % 2)) for hh in range(nh)]
        strict = col < row

        def walk(blocks, state):
            chains = [(kb, diagonal, hh) for kb, diagonal in blocks for hh in range(nh)]
            keys = lambda kb: pl.ds(pl.multiple_of(kb * bq, bq), bq)
            cut = lambda x, diagonal: jnp.where(strict, x, 0.0) if diagonal else x
            zs = [_dot_nt(qms[hh], k_ref[keys(kb), _pair(hh)]) for kb, _, hh in chains]
            dws = [_dot_nt(doms[hh], v_ref[keys(kb), _pair(hh)]) for kb, _, hh in chains]
            pairs = []
            for z, (_, diagonal, _) in zip(zs, chains):
                ls, lk = _log_sigmoid_pair(z)
                pairs.append((ls, cut(lk, diagonal)))
            incls = [_split_dot(lk, upto) for _, lk in pairs]
            pres = [st[0] for st in state]
            ws, gs = [], []
            for (ls, lk), incl, dw, (_, diagonal, hh) in zip(pairs, incls, dws, chains):
                w = cut(jnp.exp(ls + ((totals[hh] - pres[hh]) - incl)), diagonal)
                ws.append(w.astype(BF16))
                gs.append(w * dw)
                pres[hh] = pres[hh] + jnp.sum(lk, axis=1, keepdims=True)
            gsums = [_split_dot(g, before) for g in gs]
            gpres = [st[1] for st in state]
            dzs = []
            for (ls, _), g, gsum, (_, diagonal, hh) in zip(pairs, gs, gsums, chains):
                dzs.append(cut(g - jnp.exp(ls) * (g + (gpres[hh] + gsum)), diagonal).astype(BF16))
                gpres[hh] = gpres[hh] + jnp.sum(g, axis=1, keepdims=True)
            dqs = [st[2] for st in state]
            dk_parts, dv_parts = [], []
            for dzb, w, (kb, _, hh) in zip(dzs, ws, chains):
                dk_parts.append(_dot_tn(dzb, qms[hh]))
                dv_parts.append(_dot_tn(w, doms[hh]))
                dqs[hh] = dqs[hh] + _dot(dzb, k_ref[keys(kb), _pair(hh)])
            for b, (kb, _) in enumerate(blocks):
                for pair in range(nh // 2):
                    c0 = b * nh + 2 * pair
                    dk_ref[keys(kb), _pair(2 * pair)] += dk_parts[c0] + dk_parts[c0 + 1]
                    dv_ref[keys(kb), _pair(2 * pair)] += dv_parts[c0] + dv_parts[c0 + 1]
            return tuple(zip(pres, gpres, dqs))

        def finish(state):
            for pair in range(nh // 2):
                both = jnp.where(lane < 64, state[2 * pair][2], state[2 * pair + 1][2])
                dq_ref[:, _pair(2 * pair)] = (both * SB_SCALE).astype(BF16)

        zero = ((jnp.zeros((bq, 1), F32), jnp.zeros((bq, 1), F32), jnp.zeros((bq, LANES), F32)),) * nh

        @pl.when(i == 0)
        def _():
            finish(walk([(0, True)], zero))

        @pl.when(i > 0)
        def _():
            state = lax.fori_loop(ks_ref[hp, i], i - 1, lambda kb, st: walk([(kb, False)], st), zero)
            finish(walk([(i - 1, False), (i, True)], state))

    return pl.pallas_call(
        body, name="sb_bwd",
        grid_spec=pltpu.PrefetchScalarGridSpec(
            num_scalar_prefetch=1, grid=(SB_HEADS // nh, nq),
            in_specs=[pl.BlockSpec((bq, width), lambda h, i, ks: (i, h)),
                      pl.BlockSpec((s, width), lambda h, i, ks: (0, h), pipeline_mode=pl.Buffered(1)),
                      pl.BlockSpec((s, width), lambda h, i, ks: (0, h), pipeline_mode=pl.Buffered(1)),
                      pl.BlockSpec((bq, width), lambda h, i, ks: (i, h)),
                      pl.BlockSpec((bq, width), lambda h, i, ks: (i, h))],
            out_specs=[pl.BlockSpec((bq, width), lambda h, i, ks: (i, h)),
                       pl.BlockSpec((s, width), lambda h, i, ks: (0, h), pipeline_mode=pl.Buffered(1)),
                       pl.BlockSpec((s, width), lambda h, i, ks: (0, h), pipeline_mode=pl.Buffered(1))]),
        out_shape=[jax.ShapeDtypeStruct((s, SB_WIDTH), BF16), jax.ShapeDtypeStruct((s, SB_WIDTH), F32),
                   jax.ShapeDtypeStruct((s, SB_WIDTH), F32)],
        compiler_params=_params(("arbitrary", "arbitrary"), VMEM_LIMIT),
    )(kstart, q, k, v, do, rfin)


def _mla_bwd_call(q, k, v, do, o, lse):
    s = q.shape[0]
    bq = _row_tile(s, MLA_BWD_BQ)
    bk = _row_tile(s, MLA_BWD_BK)
    nq = s // bq
    assert bk % bq == 0
    half = bk // 2

    def body(q_ref, k_ref, v_ref, do_ref, o_ref, lse_ref, dq_ref, dk_ref, dv_ref, dom_ref, s_ref, dp_ref, pb_ref,
             ds_ref):
        i = pl.program_id(1)

        @pl.when(i == 0)
        def _():
            dk_ref[...] = jnp.zeros_like(dk_ref)
            dv_ref[...] = jnp.zeros_like(dv_ref)

        lane = lax.broadcasted_iota(I32, (bq, LANES), 1)
        row = lax.broadcasted_iota(I32, (half, bq), 1)
        col = lax.broadcasted_iota(I32, (half, bq), 0)
        n_full = (i * bq) // bk
        do2 = do_ref[...]
        prod = do2.astype(F32) * o_ref[...]
        ones = jnp.ones((8, LANES), BF16)
        deltas, lses = [], []
        for hh in range(2):
            head = _head_mask(lane, hh)
            dom_ref[hh] = do2 * head.astype(BF16)
            part = prod * head
            hi = part.astype(BF16)
            lo = (part - hi.astype(F32)).astype(BF16)
            deltas.append((_dot_nt(ones, hi) + _dot_nt(ones, lo))[0:1])
            lses.append(lse_ref[hh:hh + 1, :])

        def keys(g):
            return pl.ds(pl.multiple_of(g * half, half), half)

        def heads():
            return [(hh, slice(hh * HEAD_PAD, (hh + 1) * HEAD_PAD)) for hh in range(2)]

        def put_products(g, slot):
            vblk = v_ref[keys(g), :]
            for hh, cols in heads():
                s_ref[slot, hh] = _dot_nt(k_ref[keys(g), cols], q_ref[:, cols])
                dp_ref[slot, hh] = _dot_nt(vblk, dom_ref[hh])

        def add_grads(dqs, g, slot):
            rows = keys(g)
            new, dv_parts = [], []
            for hh, cols in heads():
                ds = ds_ref[slot, hh]
                dk_ref[rows, cols] += _dot(ds, q_ref[:, cols])
                dv_parts.append(_dot(pb_ref[slot, hh], dom_ref[hh]))
                new.append(dqs[hh] + _dot_tn(k_ref[rows, cols], ds))
            dv_ref[rows, :] += dv_parts[0] + dv_parts[1]
            return tuple(new)

        def substep(g, slot, dqs, masked, prefetch):
            if prefetch:
                put_products(g + 1, 1 - slot)
            dqs = add_grads(dqs, jnp.maximum(g - 1, 0), 1 - slot)
            for hh, _ in heads():
                p = jnp.exp2(s_ref[slot, hh] - lses[hh])
                if masked:
                    p = jnp.where(col + g * half <= row + i * bq, p, 0.0)
                ds_ref[slot, hh] = (p * (dp_ref[slot, hh] - deltas[hh])).astype(BF16)
                pb_ref[slot, hh] = p.astype(BF16)
            return dqs

        def chunk(kb, dqs, masked):
            dqs = substep(2 * kb, 0, dqs, masked, True)
            return substep(2 * kb + 1, 1, dqs, masked, not masked)

        ds_ref[1] = jnp.zeros_like(ds_ref[1])
        pb_ref[1] = jnp.zeros_like(pb_ref[1])
        put_products(0, 0)
        zero = jnp.zeros((HEAD_PAD, bq), F32)
        dqs = lax.fori_loop(0, n_full, lambda kb, dqs: chunk(kb, dqs, False), (zero, zero))
        dqs = chunk(n_full, dqs, True)
        dqs = add_grads(dqs, 2 * n_full + 1, 1)
        dq_ref[:, :HEAD_PAD] = dqs[0].T * MLA_SCALE
        dq_ref[:, HEAD_PAD:] = dqs[1].T * MLA_SCALE

    return pl.pallas_call(
        body, name="mla_bwd",
        grid=(4, nq),
        in_specs=[pl.BlockSpec((bq, 2 * HEAD_PAD), lambda h, i: (i, h)),
                  pl.BlockSpec((s, 2 * HEAD_PAD), lambda h, i: (0, h)),
                  pl.BlockSpec((s, LANES), lambda h, i: (0, h)),
                  pl.BlockSpec((bq, LANES), lambda h, i: (i, h)),
                  pl.BlockSpec((bq, LANES), lambda h, i: (i, h)),
                  pl.BlockSpec((None, 8, bq), lambda h, i: (h, 0, i))],
        out_specs=[pl.BlockSpec((bq, 2 * HEAD_PAD), lambda h, i: (i, h)),
                   pl.BlockSpec((s, 2 * HEAD_PAD), lambda h, i: (0, h)),
                   pl.BlockSpec((s, LANES), lambda h, i: (0, h))],
        out_shape=[jax.ShapeDtypeStruct((s, MLA_PAD_WIDTH), F32), jax.ShapeDtypeStruct((s, MLA_PAD_WIDTH), F32),
                   jax.ShapeDtypeStruct((s, MLA_WIDTH), F32)],
        scratch_shapes=[pltpu.VMEM((2, bq, LANES), BF16),
                        pltpu.VMEM((2, 2, half, bq), F32), pltpu.VMEM((2, 2, half, bq), F32),
                        pltpu.VMEM((2, 2, half, bq), BF16), pltpu.VMEM((2, 2, half, bq), BF16)],
        compiler_params=_params(("arbitrary", "arbitrary"), VMEM_LIMIT),
    )(q, k, v, do, o, lse)


def _rms_bwd(d_out, inp, r, weight, n):
    normed = inp * r
    gw = d_out * weight
    d_in = r * (gw - normed * (jnp.sum(gw * normed, axis=-1, keepdims=True) * (1.0 / n)))
    return d_in, d_out * normed


def _mla_prep_bwd_call(dq, dk, dv, q0, k0, cqn, ckvn, c_q, c_kv, cos_t, sin_t,
                       q_lora_norm, kv_lora_norm, qhn_pad, khn_pad, w_uq_bf, w_uk_bf, w_uv_bf):
    s = dq.shape[0]
    tm = _row_tile(s, 256)

    def body(dq_ref, dk_ref, dv_ref, q0_ref, k0_ref, cqn_ref, ckvn_ref, cq_ref, ckv_ref,
             cos_ref, sin_ref, qln_ref, kvln_ref, qhn_ref, khn_ref, wuq_ref, wuk_ref, wuv_ref,
             dcq_ref, dckv_ref, dkr_ref, gwuq_ref, gwuk_ref, gwuv_ref, gqln_ref, gkvln_ref, gqhn_ref, gkhn_ref,
             dq0_ref, dk0_ref):
        @pl.when(pl.program_id(0) == 0)
        def _():
            for ref in (gwuq_ref, gwuk_ref, gwuv_ref, gqln_ref, gkvln_ref, gqhn_ref, gkhn_ref):
                ref[...] = jnp.zeros_like(ref)

        cos_t, sin_t = cos_ref[...], sin_ref[...]
        lane = lax.broadcasted_iota(I32, (tm, LANES), 1)
        rope_lanes = jnp.logical_or(lane < ROPE_HALF, jnp.logical_and(lane >= 64, lane < 64 + ROPE_HALF))
        d_kr = jnp.zeros((tm, LANES), F32)
        g_qhn = jnp.zeros((1, LANES), F32)
        g_khn = jnp.zeros((1, LANES), F32)
        for h in range(MLA_HEADS):
            cols = slice(h * HEAD_PAD, (h + 1) * HEAD_PAD)
            q0 = q0_ref[:, cols]
            rq = lax.rsqrt(jnp.sum(q0 * q0, axis=-1, keepdims=True) * (1.0 / MLA_QK_DIM) + EPS)
            d_q0, gq = _rms_bwd(_rope_adjoint(dq_ref[:, cols], cos_t, sin_t), q0, rq, qhn_ref[...], MLA_QK_DIM)
            g_qhn += jnp.sum(gq, axis=0, keepdims=True)
            dq0_ref[:, cols] = d_q0.astype(BF16)
            k0 = k0_ref[:, cols]
            rk = lax.rsqrt(jnp.sum(k0 * k0, axis=-1, keepdims=True) * (1.0 / MLA_QK_DIM) + EPS)
            d_k0, gk = _rms_bwd(_rope_adjoint(dk_ref[:, cols] * LN2, cos_t, sin_t), k0, rk, khn_ref[...],
                                MLA_QK_DIM)
            g_khn += jnp.sum(gk, axis=0, keepdims=True)
            d_kr += jnp.where(rope_lanes, d_k0, 0.0)
            dk0_ref[:, cols] = d_k0.astype(BF16)
        cqn, ckvn = cqn_ref[...], ckvn_ref[...]
        d_q0b, d_k0b, dvb = dq0_ref[...], dk0_ref[...], dv_ref[...].astype(BF16)
        d_cqn = _dot_nt(d_q0b, wuq_ref[...])
        gwuq_ref[...] += _dot_tn(cqn, d_q0b)
        d_ckvn = _dot_nt(d_k0b, wuk_ref[...]) + _dot_nt(dvb, wuv_ref[...])
        gwuk_ref[...] += _dot_tn(ckvn, d_k0b)
        gwuv_ref[...] += _dot_tn(ckvn, dvb)
        gqhn_ref[...] += g_qhn
        gkhn_ref[...] += g_khn
        dkr_ref[...] = d_kr.astype(BF16)
        cq = cq_ref[...]
        rcq = lax.rsqrt(jnp.mean(cq * cq, axis=-1, keepdims=True) + EPS)
        d_cq, gl = _rms_bwd(d_cqn, cq, rcq, qln_ref[...], Q_LORA_RANK)
        dcq_ref[...] = d_cq.astype(BF16)
        gqln_ref[...] += jnp.sum(gl, axis=0, keepdims=True)
        ckv = ckv_ref[...]
        rckv = lax.rsqrt(jnp.mean(ckv * ckv, axis=-1, keepdims=True) + EPS)
        d_ckv, gl = _rms_bwd(d_ckvn, ckv, rckv, kvln_ref[...], KV_LORA_RANK)
        dckv_ref[...] = d_ckv.astype(BF16)
        gkvln_ref[...] += jnp.sum(gl, axis=0, keepdims=True)

    return pl.pallas_call(
        body, name="mla_prep_bwd",
        grid=(s // tm,),
        in_specs=[_rows(tm, MLA_PAD_WIDTH), _rows(tm, MLA_PAD_WIDTH), _rows(tm, MLA_WIDTH),
                  _rows(tm, MLA_PAD_WIDTH), _rows(tm, MLA_PAD_WIDTH),
                  _rows(tm, Q_LORA_RANK), _rows(tm, KV_LORA_RANK), _rows(tm, Q_LORA_RANK), _rows(tm, KV_LORA_RANK),
                  _rows(tm, LANES), _rows(tm, LANES),
                  _full((1, Q_LORA_RANK)), _full((1, KV_LORA_RANK)), _full((1, LANES)), _full((1, LANES)),
                  _full((Q_LORA_RANK, MLA_PAD_WIDTH)), _full((KV_LORA_RANK, MLA_PAD_WIDTH)),
                  _full((KV_LORA_RANK, MLA_WIDTH))],
        out_specs=[_rows(tm, Q_LORA_RANK), _rows(tm, KV_LORA_RANK), _rows(tm, LANES),
                   _full((Q_LORA_RANK, MLA_PAD_WIDTH)), _full((KV_LORA_RANK, MLA_PAD_WIDTH)),
                   _full((KV_LORA_RANK, MLA_WIDTH)),
                   _full((1, Q_LORA_RANK)), _full((1, KV_LORA_RANK)), _full((1, LANES)), _full((1, LANES))],
        out_shape=[jax.ShapeDtypeStruct((s, Q_LORA_RANK), BF16), jax.ShapeDtypeStruct((s, KV_LORA_RANK), BF16),
                   jax.ShapeDtypeStruct((s, LANES), BF16),
                   jax.ShapeDtypeStruct((Q_LORA_RANK, MLA_PAD_WIDTH), F32),
                   jax.ShapeDtypeStruct((KV_LORA_RANK, MLA_PAD_WIDTH), F32),
                   jax.ShapeDtypeStruct((KV_LORA_RANK, MLA_WIDTH), F32),
                   jax.ShapeDtypeStruct((1, Q_LORA_RANK), F32), jax.ShapeDtypeStruct((1, KV_LORA_RANK), F32),
                   jax.ShapeDtypeStruct((1, LANES), F32), jax.ShapeDtypeStruct((1, LANES), F32)],
        scratch_shapes=[pltpu.VMEM((tm, MLA_PAD_WIDTH), BF16), pltpu.VMEM((tm, MLA_PAD_WIDTH), BF16)],
        compiler_params=_params(("arbitrary",), VMEM_LIMIT),
    )(dq, dk, dv, q0, k0, cqn, ckvn, c_q, c_kv, cos_t, sin_t,
      q_lora_norm, kv_lora_norm, qhn_pad, khn_pad, w_uq_bf, w_uk_bf, w_uv_bf)


def _dh_call(pieces, hb, x, dy, ada, norm_w, w_in_bf):
    s = x.shape[0]
    tm = _row_tile(s, 256)
    widths = [p.shape[1] for p in pieces]
    offsets = [sum(widths[:j]) for j in range(len(widths))]
    assert offsets[-1] + widths[-1] == IN_COLS_PAD
    n = len(pieces)

    def body(*refs):
        p_refs = refs[:n]
        (hb_ref, x_ref, dy_ref, sh_ref, sc_ref, nw_ref, w_ref, gx_ref, gw_ref, dsh_ref, dsc_ref, gnw_ref,
         dp_ref) = refs[n:]

        @pl.when(pl.program_id(0) == 0)
        def _():
            gw_ref[...] = jnp.zeros_like(gw_ref)
            dsh_ref[...] = jnp.zeros_like(dsh_ref)
            dsc_ref[...] = jnp.zeros_like(dsc_ref)
            gnw_ref[...] = jnp.zeros_like(gnw_ref)

        for p_ref, c0, width in zip(p_refs, offsets, widths):
            dp_ref[:, c0:c0 + width] = p_ref[...].astype(BF16)
        gw_ref[...] += _dot_tn(hb_ref[...], dp_ref[...])
        dh = _dot_nt(dp_ref[...], w_ref[...])
        xx = x_ref[...]
        r0 = lax.rsqrt(jnp.mean(xx * xx, axis=-1, keepdims=True) + EPS)
        xn = xx * r0
        nw = nw_ref[...]
        dsh_ref[...] += jnp.sum(dh, axis=0, keepdims=True)
        dsc_ref[...] += jnp.sum(dh * (xn * nw), axis=0, keepdims=True)
        dn = dh * (1.0 + sc_ref[...])
        gnw_ref[...] += jnp.sum(dn * xn, axis=0, keepdims=True)
        dxn = dn * nw
        gx_ref[...] = dy_ref[...] + r0 * (dxn - xn * jnp.mean(dxn * xn, axis=-1, keepdims=True))

    return pl.pallas_call(
        body, name="in_proj_bwd",
        grid=(s // tm,),
        in_specs=[_rows(tm, w) for w in widths]
        + [_rows(tm, D_MODEL), _rows(tm, D_MODEL), _rows(tm, D_MODEL),
           _ada_part(0), _ada_part(1), _full((1, D_MODEL)),
           pl.BlockSpec((D_MODEL, IN_COLS_PAD), lambda i: (0, 0), pipeline_mode=pl.Buffered(1))],
        out_specs=[_rows(tm, D_MODEL),
                   pl.BlockSpec((D_MODEL, IN_COLS_PAD), lambda i: (0, 0), pipeline_mode=pl.Buffered(1)),
                   _full((1, D_MODEL)), _full((1, D_MODEL)), _full((1, D_MODEL))],
        out_shape=[jax.ShapeDtypeStruct((s, D_MODEL), F32), jax.ShapeDtypeStruct((D_MODEL, IN_COLS_PAD), F32),
                   jax.ShapeDtypeStruct((1, D_MODEL), F32), jax.ShapeDtypeStruct((1, D_MODEL), F32),
                   jax.ShapeDtypeStruct((1, D_MODEL), F32)],
        scratch_shapes=[pltpu.VMEM((tm, IN_COLS_PAD), BF16)],
        compiler_params=_params(("arbitrary",), VMEM_LIMIT),
    )(*pieces, hb, x, dy, ada, ada, norm_w, w_in_bf)


def _adamw(g, w, m, v):
    m = ADAM_B1 * m + (1.0 - ADAM_B1) * g
    v = ADAM_B2 * v + (1.0 - ADAM_B2) * (g * g)
    m_hat = m / (1.0 - ADAM_B1 ** ADAM_STEP)
    v_hat = v / (1.0 - ADAM_B2 ** ADAM_STEP)
    delta = -ADAM_LR * (m_hat / (jnp.sqrt(v_hat) + ADAM_EPS) + ADAM_WD * w)
    return delta, m, v


def _adam_shard_call(name, own, sib, w, m, v):
    r, c = w.shape
    tr = r if r <= 512 else 256

    def body(own_ref, sib_ref, w_ref, m_ref, v_ref, g_ref, d_ref, nm_ref, nv_ref):
        a = ((own_ref[0].astype(F32) + own_ref[1].astype(F32)) + own_ref[2].astype(F32)) + own_ref[3].astype(F32)
        b = ((sib_ref[0].astype(F32) + sib_ref[1].astype(F32)) + sib_ref[2].astype(F32)) + sib_ref[3].astype(F32)
        g = a + b
        g_ref[...] = g
        d_ref[...], nm_ref[...], nv_ref[...] = _adamw(g, w_ref[...], m_ref[...], v_ref[...])

    part = pl.BlockSpec((4, tr, c), lambda i: (0, i, 0))
    blk = pl.BlockSpec((tr, c), lambda i: (i, 0))
    return pl.pallas_call(
        body, name=name,
        grid=(r // tr,),
        in_specs=[part, part, blk, blk, blk],
        out_specs=[blk] * 4,
        out_shape=[jax.ShapeDtypeStruct((r, c), F32)] * 4,
        compiler_params=_params(("arbitrary",), VMEM_LIMIT),
    )(own, sib, w, m, v)


def _adam_ada_call(c_all, d_all, w, m, v):
    r, c = w.shape
    tr = 256

    def body(c_ref, d_ref, w_ref, m_ref, v_ref, g_ref, dl_ref, nm_ref, nv_ref):
        cc = c_ref[...]
        sc = cc * _sigmoid(cc)
        dd = d_ref[...]
        sc_hi = sc.astype(BF16)
        sc_lo = (sc - sc_hi.astype(F32)).astype(BF16)
        dd_hi = dd.astype(BF16)
        dd_lo = (dd - dd_hi.astype(F32)).astype(BF16)
        g = _dot_tn(sc_hi, dd_hi) + (_dot_tn(sc_hi, dd_lo) + _dot_tn(sc_lo, dd_hi))
        g_ref[...] = g
        dl_ref[...], nm_ref[...], nv_ref[...] = _adamw(g, w_ref[...], m_ref[...], v_ref[...])

    blk = pl.BlockSpec((tr, c), lambda i: (i, 0))
    return pl.pallas_call(
        body, name="adam_w_ada",
        grid=(r // tr,),
        in_specs=[pl.BlockSpec((16, tr), lambda i: (0, i)), pl.BlockSpec((16, c), lambda i: (0, 0)), blk, blk, blk],
        out_specs=[blk] * 4,
        out_shape=[jax.ShapeDtypeStruct((r, c), F32)] * 4,
        compiler_params=_params(("arbitrary",), VMEM_LIMIT),
    )(c_all, d_all, w, m, v)


def _adam_vectors_call(packs, offsets, vectors):
    nv = len(vectors)

    def body(*refs):
        p_ref, ins, outs = refs[0], refs[1:1 + 3 * nv], refs[1 + 3 * nv:]
        for j, off in enumerate(offsets):
            n = ins[3 * j].shape[1]
            span = -(-n // LANES) * LANES
            g = p_ref[0, :, off:off + span]
            for b in range(1, 8):
                g = g + p_ref[b, :, off:off + span]
            g = g[:, :n]
            outs[j][...] = g
            outs[nv + j][...], outs[2 * nv + j][...], outs[3 * nv + j][...] = _adamw(
                g, ins[3 * j][...], ins[3 * j + 1][...], ins[3 * j + 2][...])

    flat = [a for t in vectors for a in t]
    res = pl.pallas_call(
        body, name="adam_vectors",
        out_shape=[jax.ShapeDtypeStruct(t[0].shape, F32) for _ in range(4) for t in vectors],
    )(packs, *flat)
    return [res[k * nv:(k + 1) * nv] for k in range(4)]


ROPE_HALF = MLA_ROPE_DIM // 2
NOPE_A = MLA_NOPE_DIM - ROPE_HALF


def _zeros_like_lanes(t, n):
    return jnp.zeros(t.shape[:-1] + (n,), t.dtype)


def _to_head_lanes(t):
    nope, rope = t[..., :MLA_NOPE_DIM], t[..., MLA_NOPE_DIM:]
    return jnp.concatenate([rope[..., :ROPE_HALF], nope[..., :NOPE_A], rope[..., ROPE_HALF:], nope[..., NOPE_A:],
                            _zeros_like_lanes(t, HEAD_PAD - MLA_QK_DIM)], axis=-1)


def _from_head_lanes(t):
    return jnp.concatenate([t[..., ROPE_HALF:MLA_NOPE_DIM], t[..., 64 + ROPE_HALF:MLA_QK_DIM],
                            t[..., :ROPE_HALF], t[..., 64:64 + ROPE_HALF]], axis=-1)


def _nope_to_head_lanes(t):
    return jnp.concatenate([_zeros_like_lanes(t, ROPE_HALF), t[..., :NOPE_A], _zeros_like_lanes(t, ROPE_HALF),
                            t[..., NOPE_A:], _zeros_like_lanes(t, HEAD_PAD - MLA_QK_DIM)], axis=-1)


def _rope_to_head_lanes(t):
    return jnp.concatenate([t[..., :ROPE_HALF], _zeros_like_lanes(t, 64 - ROPE_HALF), t[..., ROPE_HALF:],
                            _zeros_like_lanes(t, 64 - ROPE_HALF)], axis=-1)


def _rope_tables(positions):
    inv_freq = (ROPE_THETA ** (-jnp.arange(0, MLA_ROPE_DIM, 2, dtype=F32) / MLA_ROPE_DIM))[None]
    signed = _rope_to_head_lanes(jnp.concatenate([-inv_freq, inv_freq], axis=1))
    ang = positions.astype(F32)[:, None] * signed
    return jnp.cos(ang), jnp.sin(ang)


def _unshard_cols(g):
    return jnp.transpose(g, (1, 0, 2)).reshape(g.shape[1], 4 * g.shape[2])


def _shard_cols(g):
    r, c4 = g.shape
    return jnp.transpose(g.reshape(r, 4, c4 // 4), (1, 0, 2))


def kernel(x, c, positions, w_ada, b_ada, norm_w, w_in, q_lora_norm, w_uq, kv_lora_norm, w_ukv, q_head_norm, k_head_norm, w_out, loss_target, m_w_ada, m_b_ada, m_norm_w, m_w_in, m_q_lora_norm, m_w_uq, m_kv_lora_norm, m_w_ukv, m_q_head_norm, m_k_head_norm, m_w_out, v_w_ada, v_b_ada, v_norm_w, v_w_in, v_q_lora_norm, v_w_uq, v_kv_lora_norm, v_w_ukv, v_q_head_norm, v_k_head_norm, v_w_out):
    wa_g, win_g, wuq_g, wukv_g, wout_g = _gather_call([w.astype(BF16) for w in (w_ada, w_in, w_uq, w_ukv, w_out)])
    (sq_sum, grad_x, g_w_in, g_w_uq, g_w_ukv, g_w_out, d_ada, g_norm_w, g_qln, g_kvln, g_qhn, g_khn) = _local_step(
        x[0], c, positions[0], loss_target[0], _unshard_cols(wa_g), b_ada, norm_w, _unshard_cols(win_g),
        q_lora_norm, _unshard_cols(wuq_g), kv_lora_norm, _unshard_cols(wukv_g), q_head_norm, k_head_norm,
        wout_g.reshape(D_MODEL, D_MODEL))
    loss = lax.psum(0.5 * sq_sum / D_MODEL, ("x", "y", "c"))

    grads = [g.astype(BF16) for g in (_shard_cols(g_w_in), _shard_cols(g_w_uq), _shard_cols(g_w_ukv),
                                      g_w_out.reshape(4, D_MODEL // 4, D_MODEL))]
    pieces = [c, d_ada, g_norm_w, g_qln, g_kvln, g_qhn, g_khn]
    spans = [-(-p.shape[1] // LANES) * LANES for p in pieces]
    starts = [sum(spans[:j]) for j in range(len(spans))]
    small = jnp.concatenate([jnp.pad(p, ((0, 0), (0, sp - p.shape[1]))) for p, sp in zip(pieces, spans)], axis=1)
    own, sib, packs = _exchange_call(grads, small)

    names = ["adam_w_in", "adam_w_uq", "adam_w_ukv", "adam_w_out"]
    shard_w = [(w_in, m_w_in, v_w_in), (w_uq, m_w_uq, v_w_uq), (w_ukv, m_w_ukv, v_w_ukv),
               (w_out, m_w_out, v_w_out)]
    res = {}
    for name, o_g, s_g, (w, m, v) in zip(names, own, sib, shard_w):
        res[name] = _adam_shard_call(name, o_g, s_g, w[0], m[0], v[0])
    chip = 2 * lax.axis_index("x") + lax.axis_index("y")
    ada_cols = w_ada.shape[2]
    rows = packs[:, 0, :]
    c_all = jnp.pad(rows[:, :D_MODEL], ((0, 8), (0, 0)))
    d_all = jnp.pad(lax.dynamic_slice_in_dim(rows, starts[1] + chip * ada_cols, ada_cols, axis=1), ((0, 8), (0, 0)))
    res_ada = _adam_ada_call(c_all, d_all, w_ada[0], m_w_ada[0], v_w_ada[0])
    vectors = [(b_ada, m_b_ada, v_b_ada), (norm_w, m_norm_w, v_norm_w), (q_lora_norm, m_q_lora_norm, v_q_lora_norm),
               (kv_lora_norm, m_kv_lora_norm, v_kv_lora_norm), (q_head_norm, m_q_head_norm, v_q_head_norm),
               (k_head_norm, m_k_head_norm, v_k_head_norm)]
    vec_out = _adam_vectors_call(packs, starts[1:], vectors)

    def ordered(kind):
        big = lambda name: res[name][kind][None]
        return [res_ada[kind][None], vec_out[kind][0], vec_out[kind][1], big("adam_w_in"), vec_out[kind][2],
                big("adam_w_uq"), vec_out[kind][3], big("adam_w_ukv"), vec_out[kind][4], vec_out[kind][5],
                big("adam_w_out")]

    return (loss, grad_x[None], *ordered(0), *ordered(1), *ordered(2), *ordered(3))


def _local_step(x2, c, positions, tgt, w_ada_full, b_ada, norm_w, w_in_full, q_lora_norm, w_uq_full,
                kv_lora_norm, w_ukv_full, q_head_norm, k_head_norm, w_out_full):
    w_ada_bf = w_ada_full.astype(BF16)
    kr_block = _rope_to_head_lanes(w_in_full[:, 2688:2720])
    w_in_bf = jnp.concatenate([w_in_full[:, :2688], w_in_full[:, 2720:], kr_block], axis=1).astype(BF16)
    w_uq_bf = _to_head_lanes(w_uq_full.reshape(Q_LORA_RANK, MLA_HEADS, MLA_QK_DIM)).reshape(
        Q_LORA_RANK, MLA_PAD_WIDTH).astype(BF16)
    w_ukv_heads = w_ukv_full.reshape(KV_LORA_RANK, MLA_HEADS, 2 * MLA_NOPE_DIM)
    w_uk_bf = _nope_to_head_lanes(w_ukv_heads[:, :, :MLA_NOPE_DIM]).reshape(KV_LORA_RANK, MLA_PAD_WIDTH).astype(BF16)
    w_uv_bf = w_ukv_heads[:, :, MLA_NOPE_DIM:].reshape(KV_LORA_RANK, MLA_WIDTH).astype(BF16)
    w_out_bf = w_out_full.astype(BF16)
    qhn_pad, khn_pad = _to_head_lanes(q_head_norm), _to_head_lanes(k_head_norm)
    cos_t, sin_t = _rope_tables(positions)

    ada = _ada_call(c, w_ada_bf, b_ada)
    hb, q_sb, k_sb, v_sb, g_sb, c_q, c_kv, g_mla, k_rope = _pre_call(x2, ada, norm_w, w_in_bf)
    q_m, k_m, v_m, cqn, ckvn, q0, k0 = _mla_prep_call(
        c_q, c_kv, k_rope, cos_t, sin_t, q_lora_norm, kv_lora_norm, qhn_pad, khn_pad,
        w_uq_bf, w_uk_bf, w_uv_bf)
    o_sb, r_sb, kstart = _sb_fwd_call(q_sb, k_sb, v_sb)
    o_mla, lse = _mla_fwd_call(q_m, k_m, v_m)
    do_sb, do_mla, dg_sb, dg_mla, dy, g_w_out, d_gate, sq = _out_call(
        o_sb, g_sb, o_mla, g_mla, x2, tgt, ada, w_out_bf)

    dq_sb, dk_sb, dv_sb = _sb_bwd_call(kstart, q_sb, k_sb, v_sb, do_sb, r_sb)
    dq_m, dk_m, dv_m = _mla_bwd_call(q_m, k_m, v_m, do_mla, o_mla, lse)
    (d_cq, d_ckv, d_kr, g_wuq_pad, g_wuk_pad, g_wuv, g_qln, g_kvln, g_qhn, g_khn) = _mla_prep_bwd_call(
        dq_m, dk_m, dv_m, q0, k0, cqn, ckvn, c_q, c_kv, cos_t, sin_t,
        q_lora_norm, kv_lora_norm, qhn_pad, khn_pad, w_uq_bf, w_uk_bf, w_uv_bf)
    grad_x, g_win_pad, d_shift, d_scale, g_norm_w = _dh_call(
        [dq_sb, dk_sb, dv_sb, dg_sb, d_cq, d_ckv, dg_mla, d_kr], hb, x2, dy, ada, norm_w, w_in_bf)

    g_kr = g_win_pad[:, C_KR:]
    g_w_in = jnp.concatenate([g_win_pad[:, :2688], g_kr[:, :ROPE_HALF], g_kr[:, 64:64 + ROPE_HALF],
                              g_win_pad[:, 2688:3200]], axis=1)
    g_w_uq = _from_head_lanes(g_wuq_pad.reshape(Q_LORA_RANK, MLA_HEADS, HEAD_PAD)).reshape(Q_LORA_RANK, -1)
    g_w_ukv = jnp.concatenate(
        [_from_head_lanes(g_wuk_pad.reshape(KV_LORA_RANK, MLA_HEADS, HEAD_PAD))[:, :, :MLA_NOPE_DIM],
         g_wuv.reshape(KV_LORA_RANK, MLA_HEADS, MLA_NOPE_DIM)], axis=2).reshape(KV_LORA_RANK, -1)
    d_ada = jnp.concatenate([d_shift, d_scale, d_gate], axis=1)
    return (jnp.sum(sq), grad_x, g_w_in, g_w_uq, g_w_ukv, g_w_out, d_ada, g_norm_w, g_qln, g_kvln,
            _from_head_lanes(g_qhn), _from_head_lanes(g_khn))
```

```python
import functools
import math

import jax
import jax.numpy as jnp
from jax import lax
from jax.experimental import pallas as pl
from jax.experimental.pallas import tpu as pltpu

F32 = jnp.float32
BF16 = jnp.bfloat16
I32 = jnp.int32

D_MODEL = 1024
SB_HEADS = 8
SB_WIDTH = 512
MLA_HEADS = 8
MLA_QK_DIM = 96
MLA_NOPE_DIM = 64
MLA_ROPE_DIM = 32
MLA_WIDTH = 512
Q_LORA_RANK = 384
KV_LORA_RANK = 256
ROPE_THETA = 10000.0
EPS = 1e-6
LANES = 128
HEAD_PAD = 128
MLA_PAD_WIDTH = MLA_HEADS * HEAD_PAD

C_Q, C_K, C_V, C_G = 0, 512, 1024, 1536
C_CQ, C_CKV, C_GM, C_KR = 2048, 2432, 2688, 3200
IN_COLS_PAD = 3328

ADAM_LR = 0.001
ADAM_B1 = 0.9
ADAM_B2 = 0.999
ADAM_EPS = 1e-08
ADAM_WD = 0.01
ADAM_STEP = 10

SB_SCALE = 0.125
SB_GROUP = 4
MLA_SCALE = 1.0 / math.sqrt(MLA_QK_DIM)
LN2 = math.log(2.0)
MLA_SCALE_LOG2 = MLA_SCALE / LN2
MLA_BQ = 1024
MLA_BWD_BQ = 512
MLA_BK = 1024
MLA_BWD_BK = 512
SB_DEAD = -104.0
MASK_NEG = -1e30

VMEM_LIMIT = 56 * 1024 * 1024
MESH = pl.DeviceIdType.MESH


def _dot(a, b):
    return jnp.dot(a, b, preferred_element_type=F32)


def _dot_nt(a, b):
    return lax.dot_general(a, b, (((1,), (1,)), ((), ())), preferred_element_type=F32)


def _dot_tn(a, b):
    return lax.dot_general(a, b, (((0,), (0,)), ((), ())), preferred_element_type=F32)


def _sigmoid(x):
    return 1.0 / (1.0 + jnp.exp(-x))


def _split_dot(a, m):
    hi = a.astype(BF16)
    lo = (a - hi.astype(F32)).astype(BF16)
    return _dot(hi, m) + _dot(lo, m)


def _params(sem, vmem=None):
    return pltpu.CompilerParams(dimension_semantics=sem, vmem_limit_bytes=vmem)


def _row_tile(s, want):
    return min(want, s)


def _hbm_spec():
    return pl.BlockSpec(memory_space=pltpu.HBM)


def _allgather_rows_call(row):
    def body(in_ref, out_ref, send_sems, recv_sems, loc_sem):
        x, y, c = lax.axis_index("x"), lax.axis_index("y"), lax.axis_index("c")
        flips = [(fx, fy, fc) for fx in (0, 1) for fy in (0, 1) for fc in (0, 1)][1:]

        def peer(r):
            fx, fy, fc = flips[r]
            return ((1 - x) if fx else x, (1 - y) if fy else y, (1 - c) if fc else c)

        def copy(r, slot):
            return pltpu.make_async_remote_copy(
                src_ref=in_ref, dst_ref=out_ref.at[slot], send_sem=send_sems.at[r], recv_sem=recv_sems.at[r],
                device_id=peer(r), device_id_type=MESH)

        local = pltpu.make_async_copy(in_ref, out_ref.at[4 * x + 2 * y + c], loc_sem)
        local.start()
        sends = [copy(r, 4 * x + 2 * y + c) for r in range(7)]
        for cp in sends:
            cp.start()
        for r in range(7):
            px, py, pc = peer(r)
            copy(r, 4 * px + 2 * py + pc).wait_recv()
        for cp in sends:
            cp.wait_send()
        local.wait()

    return pl.pallas_call(
        body, name="gather_rows",
        out_shape=jax.ShapeDtypeStruct((8,) + row.shape, row.dtype),
        in_specs=[_hbm_spec()], out_specs=_hbm_spec(),
        scratch_shapes=[pltpu.SemaphoreType.DMA((7,)), pltpu.SemaphoreType.DMA((7,)), pltpu.SemaphoreType.DMA],
    )(row)


def _gather_call(shards, split):
    n = len(shards)
    halves = [s.shape[1] // 2 for s in shards]

    def body(*refs):
        ins, outs = refs[:n], refs[n:2 * n]
        ici_send, ici_recv, d2d_send, d2d_recv, loc_sems = refs[2 * n:]
        x, y, c = lax.axis_index("x"), lax.axis_index("y"), lax.axis_index("c")
        me = 2 * x + y
        peers = [(1 - x, y), (x, 1 - y), (1 - x, 1 - y)]

        def rows(a, which):
            return pl.ds(pl.multiple_of(which * halves[a], 16), halves[a])

        def ici(a, j, slot):
            px, py = peers[j]
            src, dst = ins[a].at[0], outs[a].at[slot]
            if split[a]:
                src, dst = src.at[rows(a, c)], dst.at[rows(a, c)]
            return pltpu.make_async_remote_copy(
                src_ref=src, dst_ref=dst,
                send_sem=ici_send.at[3 * a + j], recv_sem=ici_recv.at[3 * a + j],
                device_id=(px, py, c), device_id_type=MESH)

        def d2d(a, j, which):
            px, py = peers[j]
            piece = outs[a].at[2 * px + py, rows(a, which)]
            return pltpu.make_async_remote_copy(
                src_ref=piece, dst_ref=piece,
                send_sem=d2d_send.at[3 * a + j], recv_sem=d2d_recv.at[3 * a + j],
                device_id=(x, y, 1 - c), device_id_type=MESH)

        local = [pltpu.make_async_copy(ins[a].at[0], outs[a].at[me], loc_sems.at[a]) for a in range(n)]
        for cp in local:
            cp.start()
        sends = [ici(a, j, me) for a in range(n) for j in range(3)]
        for cp in sends:
            cp.start()
        for a in range(n):
            for j in range(3):
                px, py = peers[j]
                ici(a, j, 2 * px + py).wait_recv()
                if split[a]:
                    cp = d2d(a, j, c)
                    cp.start()
                    sends.append(cp)
        for a in range(n):
            for j in range(3):
                if split[a]:
                    d2d(a, j, 1 - c).wait_recv()
        for cp in sends:
            cp.wait_send()
        for cp in local:
            cp.wait()

    return pl.pallas_call(
        body, name="gather_weights",
        out_shape=[jax.ShapeDtypeStruct((4,) + s.shape[1:], s.dtype) for s in shards],
        in_specs=[_hbm_spec() for _ in shards],
        out_specs=[_hbm_spec() for _ in shards],
        scratch_shapes=[pltpu.SemaphoreType.DMA((3 * n,)), pltpu.SemaphoreType.DMA((3 * n,)),
                        pltpu.SemaphoreType.DMA((3 * n,)), pltpu.SemaphoreType.DMA((3 * n,)),
                        pltpu.SemaphoreType.DMA((n,))],
    )(*shards)


def _exchange_call(grads, small):
    n = len(grads)

    def body(*refs):
        g_in, small_in = refs[:n], refs[n]
        own, sib, packs = refs[n + 1:2 * n + 1], refs[2 * n + 1:3 * n + 1], refs[3 * n + 1]
        ici_send, ici_recv, d2d_send, d2d_recv, sm_send, sm_recv, loc_sems = refs[3 * n + 2:]
        x, y, c = lax.axis_index("x"), lax.axis_index("y"), lax.axis_index("c")
        me = 2 * x + y
        me8 = 4 * x + 2 * y + c
        sibling = (x, y, 1 - c)
        peers = [(1 - x, y), (x, 1 - y), (1 - x, 1 - y)]
        flips = [(fx, fy, fc) for fx in (0, 1) for fy in (0, 1) for fc in (0, 1)][1:]

        def ici(a, j, src_slot, dst_slot):
            px, py = peers[j]
            return pltpu.make_async_remote_copy(
                src_ref=g_in[a].at[src_slot], dst_ref=own[a].at[dst_slot],
                send_sem=ici_send.at[3 * a + j], recv_sem=ici_recv.at[3 * a + j],
                device_id=(px, py, c), device_id_type=MESH)

        def d2d(a, rel, chip, src):
            return pltpu.make_async_remote_copy(
                src_ref=src, dst_ref=sib[a].at[chip],
                send_sem=d2d_send.at[4 * a + rel], recv_sem=d2d_recv.at[4 * a + rel],
                device_id=sibling, device_id_type=MESH)

        def flipped(r):
            fx, fy, fc = flips[r]
            return ((1 - x) if fx else x, (1 - y) if fy else y, (1 - c) if fc else c)

        def sm(r, slot):
            return pltpu.make_async_remote_copy(
                src_ref=small_in, dst_ref=packs.at[slot],
                send_sem=sm_send.at[r], recv_sem=sm_recv.at[r],
                device_id=flipped(r), device_id_type=MESH)

        def peer8(r):
            px, py, pc = flipped(r)
            return 4 * px + 2 * py + pc

        local = [pltpu.make_async_copy(g_in[a].at[me], own[a].at[me], loc_sems.at[a]) for a in range(n)]
        local.append(pltpu.make_async_copy(small_in, packs.at[me8], loc_sems.at[n]))
        for cp in local:
            cp.start()
        sends = []
        for r in range(7):
            sends.append(sm(r, me8))
        for a in range(n):
            for j in range(3):
                px, py = peers[j]
                sends.append(ici(a, j, 2 * px + py, me))
        for cp in sends:
            cp.start()
        for a in range(n):
            cp = d2d(a, 0, me, g_in[a].at[me])
            cp.start()
            sends.append(cp)
        for a in range(n):
            for j in range(3):
                px, py = peers[j]
                ici(a, j, me, 2 * px + py).wait_recv()
                cp = d2d(a, 1 + j, 2 * px + py, own[a].at[2 * px + py])
                cp.start()
                sends.append(cp)
        for a in range(n):
            d2d(a, 0, me, g_in[a].at[me]).wait_recv()
            for j in range(3):
                px, py = peers[j]
                d2d(a, 1 + j, 2 * px + py, g_in[a].at[me]).wait_recv()
        for r in range(7):
            sm(r, peer8(r)).wait_recv()
        for cp in sends:
            cp.wait_send()
        for cp in local:
            cp.wait()

    out_shape = ([jax.ShapeDtypeStruct(g.shape, g.dtype) for g in grads] * 2
                 + [jax.ShapeDtypeStruct((8,) + small.shape, small.dtype)])
    res = pl.pallas_call(
        body, name="exchange_grads",
        out_shape=out_shape,
        in_specs=[_hbm_spec() for _ in range(n + 1)],
        out_specs=[_hbm_spec() for _ in range(2 * n + 1)],
        scratch_shapes=[pltpu.SemaphoreType.DMA((3 * n,)), pltpu.SemaphoreType.DMA((3 * n,)),
                        pltpu.SemaphoreType.DMA((4 * n,)), pltpu.SemaphoreType.DMA((4 * n,)),
                        pltpu.SemaphoreType.DMA((7,)), pltpu.SemaphoreType.DMA((7,)),
                        pltpu.SemaphoreType.DMA((n + 1,))],
    )(*grads, small)
    return res[:n], res[n:2 * n], res[2 * n]


def _ada_call(c_all, w_ada_cols, b_ada_cols):
    def body(c_ref, w_ref, b_ref, o_ref):
        cc = c_ref[...]
        o_ref[...] = _dot((cc * _sigmoid(cc)).astype(BF16), w_ref[...].astype(BF16)) + b_ref[...]

    return pl.pallas_call(
        body, name="ada_fwd",
        out_shape=jax.ShapeDtypeStruct((c_all.shape[0], w_ada_cols.shape[1]), F32),
        compiler_params=pltpu.CompilerParams(vmem_limit_bytes=VMEM_LIMIT),
    )(c_all, w_ada_cols, b_ada_cols)


def _ada_part(j):
    return pl.BlockSpec((1, D_MODEL), lambda i: (0, j))


def _full(shape):
    return pl.BlockSpec(shape, lambda i: (0,) * len(shape))


def _rows(tm, width):
    return pl.BlockSpec((tm, width), lambda i: (i, 0))


def _pre_call(x, ada, norm_w, w_in_bf):
    s = x.shape[0]
    tm = _row_tile(s, 512)
    groups = [(C_Q, 512, BF16), (C_K, 512, BF16), (C_V, 512, BF16), (C_G, 512, F32),
              (C_CQ, Q_LORA_RANK, F32), (C_CKV, KV_LORA_RANK, F32), (C_GM, 512, F32), (C_KR, LANES, F32)]

    def body(x_ref, sh_ref, sc_ref, nw_ref, w_ref, hb_ref, *outs):
        xx = x_ref[...]
        r0 = lax.rsqrt(jnp.mean(xx * xx, axis=-1, keepdims=True) + EPS)
        h = (xx * r0 * nw_ref[...]) * (1.0 + sc_ref[...]) + sh_ref[...]
        hb = h.astype(BF16)
        hb_ref[...] = hb
        for (c0, width, dt), o_ref in zip(groups, outs):
            o_ref[...] = _dot(hb, w_ref[:, c0:c0 + width]).astype(dt)

    return pl.pallas_call(
        body, name="pre_proj",
        grid=(s // tm,),
        in_specs=[_rows(tm, D_MODEL), _ada_part(0), _ada_part(1), _full((1, D_MODEL)),
                  _full((D_MODEL, IN_COLS_PAD))],
        out_specs=[_rows(tm, D_MODEL)] + [_rows(tm, w) for _, w, _ in groups],
        out_shape=[jax.ShapeDtypeStruct((s, D_MODEL), BF16)]
        + [jax.ShapeDtypeStruct((s, w), dt) for _, w, dt in groups],
        compiler_params=_params(("arbitrary",), VMEM_LIMIT),
    )(x, ada, ada, norm_w, w_in_bf)


def _rope(t, cos_t, sin_t):
    return t * cos_t + pltpu.roll(t, 64, 1) * sin_t


def _rope_adjoint(d, cos_t, sin_t):
    return d * cos_t + pltpu.roll(d * sin_t, 64, 1)


def _mla_prep_call(c_q, c_kv, k_rope, cos_t, sin_t, q_lora_norm, kv_lora_norm, qhn_pad, khn_pad,
                   w_uq_bf, w_uk_bf, w_uv_bf):
    s = c_q.shape[0]
    tm = _row_tile(s, 256)

    def body(cq_ref, ckv_ref, kr_ref, cos_ref, sin_ref, qln_ref, kvln_ref, qhn_ref, khn_ref,
             wuq_ref, wuk_ref, wuv_ref, q_ref, k_ref, v_ref, cqn_ref, ckvn_ref, q0_ref, k0_ref):
        cq = cq_ref[...]
        cqn = (cq * lax.rsqrt(jnp.mean(cq * cq, axis=-1, keepdims=True) + EPS) * qln_ref[...]).astype(BF16)
        cqn_ref[...] = cqn
        ckv = ckv_ref[...]
        ckvn = (ckv * lax.rsqrt(jnp.mean(ckv * ckv, axis=-1, keepdims=True) + EPS) * kvln_ref[...]).astype(BF16)
        ckvn_ref[...] = ckvn
        v_ref[...] = _dot(ckvn, wuv_ref[...]).astype(BF16)
        q0_ref[...] = _dot(cqn, wuq_ref[...])
        k0_ref[...] = _dot(ckvn, wuk_ref[...])
        cos_t, sin_t = cos_ref[...], sin_ref[...]
        kr = kr_ref[...]
        for h in range(MLA_HEADS):
            cols = slice(h * HEAD_PAD, (h + 1) * HEAD_PAD)
            q0 = q0_ref[:, cols]
            rq = lax.rsqrt(jnp.sum(q0 * q0, axis=-1, keepdims=True) * (1.0 / MLA_QK_DIM) + EPS)
            q_ref[:, cols] = (_rope(q0 * rq * qhn_ref[...], cos_t, sin_t) * MLA_SCALE_LOG2).astype(BF16)
            k0 = k0_ref[:, cols] + kr
            k0_ref[:, cols] = k0
            rk = lax.rsqrt(jnp.sum(k0 * k0, axis=-1, keepdims=True) * (1.0 / MLA_QK_DIM) + EPS)
            k_ref[:, cols] = _rope(k0 * rk * khn_ref[...], cos_t, sin_t).astype(BF16)

    return pl.pallas_call(
        body, name="mla_prep",
        grid=(s // tm,),
        in_specs=[_rows(tm, Q_LORA_RANK), _rows(tm, KV_LORA_RANK), _rows(tm, LANES),
                  _rows(tm, LANES), _rows(tm, LANES),
                  _full((1, Q_LORA_RANK)), _full((1, KV_LORA_RANK)), _full((1, LANES)), _full((1, LANES)),
                  _full((Q_LORA_RANK, MLA_PAD_WIDTH)), _full((KV_LORA_RANK, MLA_PAD_WIDTH)),
                  _full((KV_LORA_RANK, MLA_WIDTH))],
        out_specs=[_rows(tm, MLA_PAD_WIDTH), _rows(tm, MLA_PAD_WIDTH), _rows(tm, MLA_WIDTH),
                   _rows(tm, Q_LORA_RANK), _rows(tm, KV_LORA_RANK),
                   _rows(tm, MLA_PAD_WIDTH), _rows(tm, MLA_PAD_WIDTH)],
        out_shape=[jax.ShapeDtypeStruct((s, MLA_PAD_WIDTH), BF16), jax.ShapeDtypeStruct((s, MLA_PAD_WIDTH), BF16),
                   jax.ShapeDtypeStruct((s, MLA_WIDTH), BF16),
                   jax.ShapeDtypeStruct((s, Q_LORA_RANK), BF16), jax.ShapeDtypeStruct((s, KV_LORA_RANK), BF16),
                   jax.ShapeDtypeStruct((s, MLA_PAD_WIDTH), F32), jax.ShapeDtypeStruct((s, MLA_PAD_WIDTH), F32)],
        compiler_params=_params(("arbitrary",), VMEM_LIMIT),
    )(c_q, c_kv, k_rope, cos_t, sin_t, q_lora_norm, kv_lora_norm, qhn_pad, khn_pad,
      w_uq_bf, w_uk_bf, w_uv_bf)


def _log_sigmoid_pair(z):
    ls = jnp.minimum(z, 0.0) - jnp.log(1.0 + jnp.exp(-jnp.abs(z)))
    return ls, ls - z


def _pair(hh):
    return slice((hh // 2) * LANES, (hh // 2 + 1) * LANES)


def _sb_fwd_call(q, k, v):
    s = q.shape[0]
    bq = _row_tile(s, 256)
    nq = s // bq
    nh = SB_GROUP
    width = nh * 64

    def body(q_ref, k_ref, v_ref, o_ref, r_ref, ks_ref):
        hp, i = pl.program_id(0), pl.program_id(1)
        lane = lax.broadcasted_iota(I32, (bq, LANES), 1)
        row = lax.broadcasted_iota(I32, (bq, bq), 0)
        col = lax.broadcasted_iota(I32, (bq, bq), 1)
        strict = col < row
        later = jnp.where(row > col, 1.0, 0.0).astype(BF16)
        masks = [_head_mask(lane, hh).astype(BF16) for hh in range(2)]
        qms = [q_ref[:, _pair(hh)] * jnp.asarray(SB_SCALE, BF16) * masks[hh % 2] for hh in range(nh)]

        def walk(blocks, state):
            chains = [(kb, diagonal, hh) for kb, diagonal in blocks for hh in range(nh)]
            keys = lambda kb: pl.ds(pl.multiple_of(kb * bq, bq), bq)
            zs = [_dot_nt(qms[hh], k_ref[keys(kb), _pair(hh)]) for kb, _, hh in chains]
            pairs = []
            for z, (_, diagonal, _) in zip(zs, chains):
                ls, lk = _log_sigmoid_pair(z)
                pairs.append((ls, jnp.where(strict, lk, 0.0) if diagonal else lk))
            sums = [_split_dot(lk, later) for _, lk in pairs]
            runs = [st[0] for st in state]
            ws = []
            for (ls, lk), after, (_, diagonal, hh) in zip(pairs, sums, chains):
                w = jnp.exp(ls + (after + runs[hh]))
                ws.append((jnp.where(strict, w, 0.0) if diagonal else w).astype(BF16))
                runs[hh] = runs[hh] + jnp.sum(lk, axis=1, keepdims=True)
            accs = [st[1] for st in state]
            for w, (kb, _, hh) in zip(ws, chains):
                accs[hh] = accs[hh] + _dot(w, v_ref[keys(kb), _pair(hh)])
            return tuple(zip(runs, accs))

        def alive(state):
            top = jnp.max(state[0][0])
            for st in state[1:]:
                top = jnp.maximum(top, jnp.max(st[0]))
            return (top > SB_DEAD).astype(I32)

        def finish(state, first):
            ks_ref[hp, i] = first
            for pair in range(nh // 2):
                o_ref[:, _pair(2 * pair)] = jnp.where(lane < 64, state[2 * pair][1], state[2 * pair + 1][1])
                r_ref[:, _pair(2 * pair)] = jnp.where(lane < 64, state[2 * pair][0], state[2 * pair + 1][0])

        zero = ((jnp.zeros((bq, 1), F32), jnp.zeros((bq, LANES), F32)),) * nh

        @pl.when(i == 0)
        def _():
            finish(walk([(0, True)], zero), 0)

        @pl.when(i > 0)
        def _():
            state = walk([(i, True), (i - 1, False)], zero)

            def cond(carry):
                return jnp.logical_and(carry[0] >= 0, carry[1] > 0)

            def step(carry):
                state = walk([(carry[0], False)], carry[2])
                return carry[0] - 1, alive(state), state

            kb, _, state = lax.while_loop(cond, step, (i - 2, alive(state), state))
            finish(state, kb + 1)

    return pl.pallas_call(
        body, name="sb_fwd",
        grid=(SB_HEADS // nh, nq),
        in_specs=[pl.BlockSpec((bq, width), lambda h, i: (i, h)),
                  pl.BlockSpec((s, width), lambda h, i: (0, h)),
                  pl.BlockSpec((s, width), lambda h, i: (0, h))],
        out_specs=[pl.BlockSpec((bq, width), lambda h, i: (i, h)),
                   pl.BlockSpec((bq, width), lambda h, i: (i, h)),
                   pl.BlockSpec(memory_space=pltpu.SMEM)],
        out_shape=[jax.ShapeDtypeStruct((s, SB_WIDTH), F32), jax.ShapeDtypeStruct((s, SB_WIDTH), F32),
                   jax.ShapeDtypeStruct((SB_HEADS // nh, nq), I32)],
        compiler_params=_params(("arbitrary", "arbitrary"), VMEM_LIMIT),
    )(q, k, v)


def _mla_fwd_call(q, k, v):
    s = q.shape[0]
    bq = _row_tile(s, MLA_BQ)
    bk = _row_tile(s, MLA_BK)
    nq = s // bq
    assert bk % bq == 0
    half = bk // 2

    def body(q_ref, k_ref, v_ref, o_ref, lse_ref, p_ref, s_ref):
        i = pl.program_id(1)
        lane = lax.broadcasted_iota(I32, (bq, LANES), 1)
        row = lax.broadcasted_iota(I32, (half, bq), 1)
        col = lax.broadcasted_iota(I32, (half, bq), 0)
        n_full = (i * bq) // bk

        def keys(g):
            return pl.ds(pl.multiple_of(g * half, half), half)

        def put_scores(g, slot):
            for hh in range(2):
                cols = slice(hh * HEAD_PAD, (hh + 1) * HEAD_PAD)
                s_ref[slot, hh] = _dot_nt(k_ref[keys(g), cols], q_ref[:, cols])

        def add_pv(carry, g, slot):
            vblk = v_ref[keys(g), :]
            return tuple((m, l, alpha * acc + _dot_tn(vblk, p_ref[slot, hh]), alpha)
                         for hh, (m, l, acc, alpha) in enumerate(carry))

        def substep(g, slot, carry, masked, prefetch):
            if prefetch:
                put_scores(g + 1, 1 - slot)
            carry = add_pv(carry, jnp.maximum(g - 1, 0), 1 - slot)
            new = []
            for hh in range(2):
                m, l, acc, _ = carry[hh]
                sc = s_ref[slot, hh]
                if masked:
                    sc = jnp.where(col + g * half <= row + i * bq, sc, MASK_NEG)
                m_new = jnp.maximum(m, jnp.max(sc, axis=0, keepdims=True))
                p = jnp.exp2(sc - m_new)
                alpha = jnp.exp2(m - m_new)
                l = alpha * l + jnp.sum(p, axis=0, keepdims=True)
                p_ref[slot, hh] = p.astype(BF16)
                new.append((m_new, l, acc, alpha))
            return tuple(new)

        def chunk(kb, carry, masked):
            carry = substep(2 * kb, 0, carry, masked, True)
            return substep(2 * kb + 1, 1, carry, masked, not masked)

        p_ref[1] = jnp.zeros_like(p_ref[1])
        put_scores(0, 0)
        one = (jnp.full((1, bq), MASK_NEG, F32), jnp.zeros((1, bq), F32), jnp.zeros((LANES, bq), F32),
               jnp.ones((1, bq), F32))
        carry = lax.fori_loop(0, n_full, lambda kb, cr: chunk(kb, cr, False), (one, one))
        carry = chunk(n_full, carry, True)
        (m0, l0, a0, _), (m1, l1, a1, _) = add_pv(carry, 2 * n_full + 1, 1)
        o_ref[...] = jnp.where(lane < 64, (a0 / l0).T, (a1 / l1).T)
        sub = lax.broadcasted_iota(I32, (8, bq), 0)
        lse_ref[...] = jnp.where(sub == 0, m0 + jnp.log2(l0), jnp.where(sub == 1, m1 + jnp.log2(l1), 0.0))

    return pl.pallas_call(
        body, name="mla_fwd",
        grid=(4, nq),
        in_specs=[pl.BlockSpec((bq, 2 * HEAD_PAD), lambda h, i: (i, h)),
                  pl.BlockSpec((s, 2 * HEAD_PAD), lambda h, i: (0, h)),
                  pl.BlockSpec((s, LANES), lambda h, i: (0, h))],
        out_specs=[pl.BlockSpec((bq, LANES), lambda h, i: (i, h)),
                   pl.BlockSpec((None, 8, bq), lambda h, i: (h, 0, i))],
        out_shape=[jax.ShapeDtypeStruct((s, MLA_WIDTH), F32), jax.ShapeDtypeStruct((4, 8, s), F32)],
        scratch_shapes=[pltpu.VMEM((2, 2, half, bq), BF16), pltpu.VMEM((2, 2, half, bq), F32)],
        compiler_params=_params(("arbitrary", "arbitrary"), VMEM_LIMIT),
    )(q, k, v)


def _out_call(o_sb, g_sb, o_mla, g_mla, x, target, ada, w_out_bf):
    s = x.shape[0]
    tm = _row_tile(s, 256)

    def body(osb_ref, gsb_ref, oml_ref, gml_ref, x_ref, t_ref, gate_ref, w_ref,
             dosb_ref, doml_ref, dgsb_ref, dgml_ref, dy_ref, gw_ref, dgate_ref, sq_ref):
        @pl.when(pl.program_id(0) == 0)
        def _():
            gw_ref[...] = jnp.zeros_like(gw_ref)
            dgate_ref[...] = jnp.zeros_like(dgate_ref)
            sq_ref[...] = jnp.zeros_like(sq_ref)

        g_s, g_m = gsb_ref[...], gml_ref[...]
        sig_s, sig_m = _sigmoid(g_s), _sigmoid(g_m)
        silu_s, silu_m = g_s * sig_s, g_m * sig_m
        o_s, o_m = osb_ref[...], oml_ref[...]
        mixed = jnp.concatenate([o_s * silu_s, o_m * silu_m], axis=1).astype(BF16)
        u = _dot(mixed, w_ref[...])
        gate_v = gate_ref[...]
        err = x_ref[...] + gate_v * u - t_ref[...]
        sq_ref[...] += jnp.sum(err * err, axis=0, keepdims=True)
        dy = err * (1.0 / D_MODEL)
        dy_ref[...] = dy
        dgate_ref[...] += jnp.sum(dy * u, axis=0, keepdims=True)
        du = (dy * gate_v).astype(BF16)
        gw_ref[...] += _dot_tn(mixed, du)
        dmix = _dot_nt(du, w_ref[...])
        dm_s, dm_m = dmix[:, :SB_WIDTH], dmix[:, SB_WIDTH:]
        dosb_ref[...] = (dm_s * silu_s).astype(BF16)
        doml_ref[...] = (dm_m * silu_m).astype(BF16)
        dgsb_ref[...] = (dm_s * o_s * (sig_s * (1.0 + g_s * (1.0 - sig_s)))).astype(BF16)
        dgml_ref[...] = (dm_m * o_m * (sig_m * (1.0 + g_m * (1.0 - sig_m)))).astype(BF16)

    return pl.pallas_call(
        body, name="out_proj_loss",
        grid=(s // tm,),
        in_specs=[_rows(tm, 512), _rows(tm, 512), _rows(tm, 512), _rows(tm, 512),
                  _rows(tm, D_MODEL), _rows(tm, D_MODEL), _ada_part(2), _full((D_MODEL, D_MODEL))],
        out_specs=[_rows(tm, 512), _rows(tm, 512), _rows(tm, 512), _rows(tm, 512), _rows(tm, D_MODEL),
                   _full((D_MODEL, D_MODEL)), _full((1, D_MODEL)), _full((1, D_MODEL))],
        out_shape=[jax.ShapeDtypeStruct((s, 512), BF16)] * 4
        + [jax.ShapeDtypeStruct((s, D_MODEL), F32), jax.ShapeDtypeStruct((D_MODEL, D_MODEL), F32),
           jax.ShapeDtypeStruct((1, D_MODEL), F32), jax.ShapeDtypeStruct((1, D_MODEL), F32)],
        compiler_params=_params(("arbitrary",), VMEM_LIMIT),
    )(o_sb, g_sb, o_mla, g_mla, x, target, ada, w_out_bf)


def _head_mask(lane, hh):
    return jnp.where((lane >= 64) if hh else (lane < 64), 1.0, 0.0)


def _pick_lane(packed, lane, which):
    return jnp.sum(jnp.where(lane == which, packed, 0.0), axis=1, keepdims=True)


def _sb_bwd_call(kstart, q, k, v, do, rfin):
    s = q.shape[0]
    bq = _row_tile(s, 256)
    nq = s // bq
    nh = SB_GROUP
    width = nh * 64

    def body(ks_ref, q_ref, k_ref, v_ref, do_ref, r_ref, dq_ref, dk_ref, dv_ref):
        hp, i = pl.program_id(0), pl.program_id(1)

        @pl.when(i == 0)
        def _():
            dk_ref[...] = jnp.zeros_like(dk_ref)
            dv_ref[...] = jnp.zeros_like(dv_ref)

        lane = lax.broadcasted_iota(I32, (bq, LANES), 1)
        row = lax.broadcasted_iota(I32, (bq, bq), 0)
        col = lax.broadcasted_iota(I32, (bq, bq), 1)
        upto = jnp.where(row <= col, 1.0, 0.0).astype(BF16)
        before = jnp.where(row < col, 1.0, 0.0).astype(BF16)
        masks = [_head_mask(lane, hh).astype(BF16) for hh in range(2)]
        qms = [q_ref[:, _pair(hh)] * jnp.asarray(SB_SCALE, BF16) * masks[hh % 2] for hh in range(nh)]
        doms = [do_ref[:, _pair(hh)] * masks[hh % 2] for hh in range(nh)]
        totals = [_pick_lane(r_ref[:, _pair(hh)], lane, 64 * (hh % 2)) for hh in range(nh)]
        strict = col < row

        def walk(blocks, state):
            chains = [(kb, diagonal, hh) for kb, diagonal in blocks for hh in range(nh)]
            keys = lambda kb: pl.ds(pl.multiple_of(kb * bq, bq), bq)
            cut = lambda x, diagonal: jnp.where(strict, x, 0.0) if diagonal else x
            zs = [_dot_nt(qms[hh], k_ref[keys(kb), _pair(hh)]) for kb, _, hh in chains]
            dws = [_dot_nt(doms[hh], v_ref[keys(kb), _pair(hh)]) for kb, _, hh in chains]
            pairs = []
            for z, (_, diagonal, _) in zip(zs, chains):
                ls, lk = _log_sigmoid_pair(z)
                pairs.append((ls, cut(lk, diagonal)))
            incls = [_split_dot(lk, upto) for _, lk in pairs]
            pres = [st[0] for st in state]
            ws, gs = [], []
            for (ls, lk), incl, dw, (_, diagonal, hh) in zip(pairs, incls, dws, chains):
                w = cut(jnp.exp(ls + ((totals[hh] - pres[hh]) - incl)), diagonal)
                ws.append(w.astype(BF16))
                gs.append(w * dw)
                pres[hh] = pres[hh] + jnp.sum(lk, axis=1, keepdims=True)
            gsums = [_split_dot(g, before) for g in gs]
            gpres = [st[1] for st in state]
            dzs = []
            for (ls, _), g, gsum, (_, diagonal, hh) in zip(pairs, gs, gsums, chains):
                dzs.append(cut(g - jnp.exp(ls) * (g + (gpres[hh] + gsum)), diagonal).astype(BF16))
                gpres[hh] = gpres[hh] + jnp.sum(g, axis=1, keepdims=True)
            dqs = [st[2] for st in state]
            dk_parts, dv_parts = [], []
            for dzb, w, (kb, _, hh) in zip(dzs, ws, chains):
                dk_parts.append(_dot_tn(dzb, qms[hh]))
                dv_parts.append(_dot_tn(w, doms[hh]))
                dqs[hh] = dqs[hh] + _dot(dzb, k_ref[keys(kb), _pair(hh)])
            for b, (kb, _) in enumerate(blocks):
                for pair in range(nh // 2):
                    c0 = b * nh + 2 * pair
                    dk_ref[keys(kb), _pair(2 * pair)] += dk_parts[c0] + dk_parts[c0 + 1]
                    dv_ref[keys(kb), _pair(2 * pair)] += dv_parts[c0] + dv_parts[c0 + 1]
            return tuple(zip(pres, gpres, dqs))

        def finish(state):
            for pair in range(nh // 2):
                both = jnp.where(lane < 64, state[2 * pair][2], state[2 * pair + 1][2])
                dq_ref[:, _pair(2 * pair)] = (both * SB_SCALE).astype(BF16)

        zero = ((jnp.zeros((bq, 1), F32), jnp.zeros((bq, 1), F32), jnp.zeros((bq, LANES), F32)),) * nh

        @pl.when(i == 0)
        def _():
            finish(walk([(0, True)], zero))

        @pl.when(i > 0)
        def _():
            state = lax.fori_loop(ks_ref[hp, i], i - 1, lambda kb, st: walk([(kb, False)], st), zero)
            finish(walk([(i - 1, False), (i, True)], state))

    return pl.pallas_call(
        body, name="sb_bwd",
        grid_spec=pltpu.PrefetchScalarGridSpec(
            num_scalar_prefetch=1, grid=(SB_HEADS // nh, nq),
            in_specs=[pl.BlockSpec((bq, width), lambda h, i, ks: (i, h)),
                      pl.BlockSpec((s, width), lambda h, i, ks: (0, h), pipeline_mode=pl.Buffered(1)),
                      pl.BlockSpec((s, width), lambda h, i, ks: (0, h), pipeline_mode=pl.Buffered(1)),
                      pl.BlockSpec((bq, width), lambda h, i, ks: (i, h)),
                      pl.BlockSpec((bq, width), lambda h, i, ks: (i, h))],
            out_specs=[pl.BlockSpec((bq, width), lambda h, i, ks: (i, h)),
                       pl.BlockSpec((s, width), lambda h, i, ks: (0, h), pipeline_mode=pl.Buffered(1)),
                       pl.BlockSpec((s, width), lambda h, i, ks: (0, h), pipeline_mode=pl.Buffered(1))]),
        out_shape=[jax.ShapeDtypeStruct((s, SB_WIDTH), BF16), jax.ShapeDtypeStruct((s, SB_WIDTH), F32),
                   jax.ShapeDtypeStruct((s, SB_WIDTH), F32)],
        compiler_params=_params(("arbitrary", "arbitrary"), VMEM_LIMIT),
    )(kstart, q, k, v, do, rfin)


def _mla_bwd_call(q, k, v, do, o, lse):
    s = q.shape[0]
    bq = _row_tile(s, MLA_BWD_BQ)
    bk = _row_tile(s, MLA_BWD_BK)
    nq = s // bq
    assert bk % bq == 0
    half = bk // 2

    def body(q_ref, k_ref, v_ref, do_ref, o_ref, lse_ref, dq_ref, dk_ref, dv_ref, dom_ref, s_ref, dp_ref, pb_ref,
             ds_ref):
        i = pl.program_id(1)

        @pl.when(i == 0)
        def _():
            dk_ref[...] = jnp.zeros_like(dk_ref)
            dv_ref[...] = jnp.zeros_like(dv_ref)

        lane = lax.broadcasted_iota(I32, (bq, LANES), 1)
        row = lax.broadcasted_iota(I32, (half, bq), 1)
        col = lax.broadcasted_iota(I32, (half, bq), 0)
        n_full = (i * bq) // bk
        do2 = do_ref[...]
        prod = do2.astype(F32) * o_ref[...]
        ones = jnp.ones((8, LANES), BF16)
        deltas, lses = [], []
        for hh in range(2):
            head = _head_mask(lane, hh)
            dom_ref[hh] = do2 * head.astype(BF16)
            part = prod * head
            hi = part.astype(BF16)
            lo = (part - hi.astype(F32)).astype(BF16)
            deltas.append((_dot_nt(ones, hi) + _dot_nt(ones, lo))[0:1])
            lses.append(lse_ref[hh:hh + 1, :])

        def keys(g):
            return pl.ds(pl.multiple_of(g * half, half), half)

        def heads():
            return [(hh, slice(hh * HEAD_PAD, (hh + 1) * HEAD_PAD)) for hh in range(2)]

        def put_products(g, slot):
            vblk = v_ref[keys(g), :]
            for hh, cols in heads():
                s_ref[slot, hh] = _dot_nt(k_ref[keys(g), cols], q_ref[:, cols])
                dp_ref[slot, hh] = _dot_nt(vblk, dom_ref[hh])

        def add_grads(dqs, g, slot):
            rows = keys(g)
            new, dv_parts = [], []
            for hh, cols in heads():
                ds = ds_ref[slot, hh]
                dk_ref[rows, cols] += _dot(ds, q_ref[:, cols])
                dv_parts.append(_dot(pb_ref[slot, hh], dom_ref[hh]))
                new.append(dqs[hh] + _dot_tn(k_ref[rows, cols], ds))
            dv_ref[rows, :] += dv_parts[0] + dv_parts[1]
            return tuple(new)

        def substep(g, slot, dqs, masked, prefetch):
            if prefetch:
                put_products(g + 1, 1 - slot)
            dqs = add_grads(dqs, jnp.maximum(g - 1, 0), 1 - slot)
            for hh, _ in heads():
                p = jnp.exp2(s_ref[slot, hh] - lses[hh])
                if masked:
                    p = jnp.where(col + g * half <= row + i * bq, p, 0.0)
                ds_ref[slot, hh] = (p * (dp_ref[slot, hh] - deltas[hh])).astype(BF16)
                pb_ref[slot, hh] = p.astype(BF16)
            return dqs

        def chunk(kb, dqs, masked):
            dqs = substep(2 * kb, 0, dqs, masked, True)
            return substep(2 * kb + 1, 1, dqs, masked, not masked)

        ds_ref[1] = jnp.zeros_like(ds_ref[1])
        pb_ref[1] = jnp.zeros_like(pb_ref[1])
        put_products(0, 0)
        zero = jnp.zeros((HEAD_PAD, bq), F32)
        dqs = lax.fori_loop(0, n_full, lambda kb, dqs: chunk(kb, dqs, False), (zero, zero))
        dqs = chunk(n_full, dqs, True)
        dqs = add_grads(dqs, 2 * n_full + 1, 1)
        dq_ref[:, :HEAD_PAD] = dqs[0].T * MLA_SCALE
        dq_ref[:, HEAD_PAD:] = dqs[1].T * MLA_SCALE

    return pl.pallas_call(
        body, name="mla_bwd",
        grid=(4, nq),
        in_specs=[pl.BlockSpec((bq, 2 * HEAD_PAD), lambda h, i: (i, h)),
                  pl.BlockSpec((s, 2 * HEAD_PAD), lambda h, i: (0, h), pipeline_mode=pl.Buffered(1)),
                  pl.BlockSpec((s, LANES), lambda h, i: (0, h), pipeline_mode=pl.Buffered(1)),
                  pl.BlockSpec((bq, LANES), lambda h, i: (i, h)),
                  pl.BlockSpec((bq, LANES), lambda h, i: (i, h)),
                  pl.BlockSpec((None, 8, bq), lambda h, i: (h, 0, i))],
        out_specs=[pl.BlockSpec((bq, 2 * HEAD_PAD), lambda h, i: (i, h)),
                   pl.BlockSpec((s, 2 * HEAD_PAD), lambda h, i: (0, h), pipeline_mode=pl.Buffered(1)),
                   pl.BlockSpec((s, LANES), lambda h, i: (0, h), pipeline_mode=pl.Buffered(1))],
        out_shape=[jax.ShapeDtypeStruct((s, MLA_PAD_WIDTH), F32), jax.ShapeDtypeStruct((s, MLA_PAD_WIDTH), F32),
                   jax.ShapeDtypeStruct((s, MLA_WIDTH), F32)],
        scratch_shapes=[pltpu.VMEM((2, bq, LANES), BF16),
                        pltpu.VMEM((2, 2, half, bq), F32), pltpu.VMEM((2, 2, half, bq), F32),
                        pltpu.VMEM((2, 2, half, bq), BF16), pltpu.VMEM((2, 2, half, bq), BF16)],
        compiler_params=_params(("arbitrary", "arbitrary"), VMEM_LIMIT),
    )(q, k, v, do, o, lse)


def _rms_bwd(d_out, inp, r, weight, n):
    normed = inp * r
    gw = d_out * weight
    d_in = r * (gw - normed * (jnp.sum(gw * normed, axis=-1, keepdims=True) * (1.0 / n)))
    return d_in, d_out * normed


def _mla_prep_bwd_call(dq, dk, dv, q0, k0, cqn, ckvn, c_q, c_kv, cos_t, sin_t,
                       q_lora_norm, kv_lora_norm, qhn_pad, khn_pad, w_uq_bf, w_uk_bf, w_uv_bf):
    s = dq.shape[0]
    tm = _row_tile(s, 256)

    def body(dq_ref, dk_ref, dv_ref, q0_ref, k0_ref, cqn_ref, ckvn_ref, cq_ref, ckv_ref,
             cos_ref, sin_ref, qln_ref, kvln_ref, qhn_ref, khn_ref, wuq_ref, wuk_ref, wuv_ref,
             dcq_ref, dckv_ref, dkr_ref, gwuq_ref, gwuk_ref, gwuv_ref, gqln_ref, gkvln_ref, gqhn_ref, gkhn_ref,
             dq0_ref, dk0_ref):
        @pl.when(pl.program_id(0) == 0)
        def _():
            for ref in (gwuq_ref, gwuk_ref, gwuv_ref, gqln_ref, gkvln_ref, gqhn_ref, gkhn_ref):
                ref[...] = jnp.zeros_like(ref)

        cos_t, sin_t = cos_ref[...], sin_ref[...]
        lane = lax.broadcasted_iota(I32, (tm, LANES), 1)
        rope_lanes = jnp.logical_or(lane < ROPE_HALF, jnp.logical_and(lane >= 64, lane < 64 + ROPE_HALF))
        d_kr = jnp.zeros((tm, LANES), F32)
        g_qhn = jnp.zeros((1, LANES), F32)
        g_khn = jnp.zeros((1, LANES), F32)
        for h in range(MLA_HEADS):
            cols = slice(h * HEAD_PAD, (h + 1) * HEAD_PAD)
            q0 = q0_ref[:, cols]
            rq = lax.rsqrt(jnp.sum(q0 * q0, axis=-1, keepdims=True) * (1.0 / MLA_QK_DIM) + EPS)
            d_q0, gq = _rms_bwd(_rope_adjoint(dq_ref[:, cols], cos_t, sin_t), q0, rq, qhn_ref[...], MLA_QK_DIM)
            g_qhn += jnp.sum(gq, axis=0, keepdims=True)
            dq0_ref[:, cols] = d_q0.astype(BF16)
            k0 = k0_ref[:, cols]
            rk = lax.rsqrt(jnp.sum(k0 * k0, axis=-1, keepdims=True) * (1.0 / MLA_QK_DIM) + EPS)
            d_k0, gk = _rms_bwd(_rope_adjoint(dk_ref[:, cols] * LN2, cos_t, sin_t), k0, rk, khn_ref[...],
                                MLA_QK_DIM)
            g_khn += jnp.sum(gk, axis=0, keepdims=True)
            d_kr += jnp.where(rope_lanes, d_k0, 0.0)
            dk0_ref[:, cols] = d_k0.astype(BF16)
        cqn, ckvn = cqn_ref[...], ckvn_ref[...]
        d_q0b, d_k0b, dvb = dq0_ref[...], dk0_ref[...], dv_ref[...].astype(BF16)
        d_cqn = _dot_nt(d_q0b, wuq_ref[...])
        gwuq_ref[...] += _dot_tn(cqn, d_q0b)
        d_ckvn = _dot_nt(d_k0b, wuk_ref[...]) + _dot_nt(dvb, wuv_ref[...])
        gwuk_ref[...] += _dot_tn(ckvn, d_k0b)
        gwuv_ref[...] += _dot_tn(ckvn, dvb)
        gqhn_ref[...] += g_qhn
        gkhn_ref[...] += g_khn
        dkr_ref[...] = d_kr.astype(BF16)
        cq = cq_ref[...]
        rcq = lax.rsqrt(jnp.mean(cq * cq, axis=-1, keepdims=True) + EPS)
        d_cq, gl = _rms_bwd(d_cqn, cq, rcq, qln_ref[...], Q_LORA_RANK)
        dcq_ref[...] = d_cq.astype(BF16)
        gqln_ref[...] += jnp.sum(gl, axis=0, keepdims=True)
        ckv = ckv_ref[...]
        rckv = lax.rsqrt(jnp.mean(ckv * ckv, axis=-1, keepdims=True) + EPS)
        d_ckv, gl = _rms_bwd(d_ckvn, ckv, rckv, kvln_ref[...], KV_LORA_RANK)
        dckv_ref[...] = d_ckv.astype(BF16)
        gkvln_ref[...] += jnp.sum(gl, axis=0, keepdims=True)

    return pl.pallas_call(
        body, name="mla_prep_bwd",
        grid=(s // tm,),
        in_specs=[_rows(tm, MLA_PAD_WIDTH), _rows(tm, MLA_PAD_WIDTH), _rows(tm, MLA_WIDTH),
                  _rows(tm, MLA_PAD_WIDTH), _rows(tm, MLA_PAD_WIDTH),
                  _rows(tm, Q_LORA_RANK), _rows(tm, KV_LORA_RANK), _rows(tm, Q_LORA_RANK), _rows(tm, KV_LORA_RANK),
                  _rows(tm, LANES), _rows(tm, LANES),
                  _full((1, Q_LORA_RANK)), _full((1, KV_LORA_RANK)), _full((1, LANES)), _full((1, LANES)),
                  _full((Q_LORA_RANK, MLA_PAD_WIDTH)), _full((KV_LORA_RANK, MLA_PAD_WIDTH)),
                  _full((KV_LORA_RANK, MLA_WIDTH))],
        out_specs=[_rows(tm, Q_LORA_RANK), _rows(tm, KV_LORA_RANK), _rows(tm, LANES),
                   _full((Q_LORA_RANK, MLA_PAD_WIDTH)), _full((KV_LORA_RANK, MLA_PAD_WIDTH)),
                   _full((KV_LORA_RANK, MLA_WIDTH)),
                   _full((1, Q_LORA_RANK)), _full((1, KV_LORA_RANK)), _full((1, LANES)), _full((1, LANES))],
        out_shape=[jax.ShapeDtypeStruct((s, Q_LORA_RANK), BF16), jax.ShapeDtypeStruct((s, KV_LORA_RANK), BF16),
                   jax.ShapeDtypeStruct((s, LANES), BF16),
                   jax.ShapeDtypeStruct((Q_LORA_RANK, MLA_PAD_WIDTH), F32),
                   jax.ShapeDtypeStruct((KV_LORA_RANK, MLA_PAD_WIDTH), F32),
                   jax.ShapeDtypeStruct((KV_LORA_RANK, MLA_WIDTH), F32),
                   jax.ShapeDtypeStruct((1, Q_LORA_RANK), F32), jax.ShapeDtypeStruct((1, KV_LORA_RANK), F32),
                   jax.ShapeDtypeStruct((1, LANES), F32), jax.ShapeDtypeStruct((1, LANES), F32)],
        scratch_shapes=[pltpu.VMEM((tm, MLA_PAD_WIDTH), BF16), pltpu.VMEM((tm, MLA_PAD_WIDTH), BF16)],
        compiler_params=_params(("arbitrary",), VMEM_LIMIT),
    )(dq, dk, dv, q0, k0, cqn, ckvn, c_q, c_kv, cos_t, sin_t,
      q_lora_norm, kv_lora_norm, qhn_pad, khn_pad, w_uq_bf, w_uk_bf, w_uv_bf)


def _dh_call(pieces, hb, x, dy, ada, norm_w, w_in_bf):
    s = x.shape[0]
    tm = _row_tile(s, 256)
    widths = [p.shape[1] for p in pieces]
    offsets = [sum(widths[:j]) for j in range(len(widths))]
    assert offsets[-1] + widths[-1] == IN_COLS_PAD
    n = len(pieces)

    def body(*refs):
        p_refs = refs[:n]
        (hb_ref, x_ref, dy_ref, sh_ref, sc_ref, nw_ref, w_ref, gx_ref, gw_ref, dsh_ref, dsc_ref, gnw_ref,
         dp_ref) = refs[n:]

        @pl.when(pl.program_id(0) == 0)
        def _():
            gw_ref[...] = jnp.zeros_like(gw_ref)
            dsh_ref[...] = jnp.zeros_like(dsh_ref)
            dsc_ref[...] = jnp.zeros_like(dsc_ref)
            gnw_ref[...] = jnp.zeros_like(gnw_ref)

        for p_ref, c0, width in zip(p_refs, offsets, widths):
            dp_ref[:, c0:c0 + width] = p_ref[...].astype(BF16)
        gw_ref[...] += _dot_tn(hb_ref[...], dp_ref[...])
        dh = _dot_nt(dp_ref[...], w_ref[...])
        xx = x_ref[...]
        r0 = lax.rsqrt(jnp.mean(xx * xx, axis=-1, keepdims=True) + EPS)
        xn = xx * r0
        nw = nw_ref[...]
        dsh_ref[...] += jnp.sum(dh, axis=0, keepdims=True)
        dsc_ref[...] += jnp.sum(dh * (xn * nw), axis=0, keepdims=True)
        dn = dh * (1.0 + sc_ref[...])
        gnw_ref[...] += jnp.sum(dn * xn, axis=0, keepdims=True)
        dxn = dn * nw
        gx_ref[...] = dy_ref[...] + r0 * (dxn - xn * jnp.mean(dxn * xn, axis=-1, keepdims=True))

    return pl.pallas_call(
        body, name="in_proj_bwd",
        grid=(s // tm,),
        in_specs=[_rows(tm, w) for w in widths]
        + [_rows(tm, D_MODEL), _rows(tm, D_MODEL), _rows(tm, D_MODEL),
           _ada_part(0), _ada_part(1), _full((1, D_MODEL)),
           pl.BlockSpec((D_MODEL, IN_COLS_PAD), lambda i: (0, 0), pipeline_mode=pl.Buffered(1))],
        out_specs=[_rows(tm, D_MODEL),
                   pl.BlockSpec((D_MODEL, IN_COLS_PAD), lambda i: (0, 0), pipeline_mode=pl.Buffered(1)),
                   _full((1, D_MODEL)), _full((1, D_MODEL)), _full((1, D_MODEL))],
        out_shape=[jax.ShapeDtypeStruct((s, D_MODEL), F32), jax.ShapeDtypeStruct((D_MODEL, IN_COLS_PAD), F32),
                   jax.ShapeDtypeStruct((1, D_MODEL), F32), jax.ShapeDtypeStruct((1, D_MODEL), F32),
                   jax.ShapeDtypeStruct((1, D_MODEL), F32)],
        scratch_shapes=[pltpu.VMEM((tm, IN_COLS_PAD), BF16)],
        compiler_params=_params(("arbitrary",), VMEM_LIMIT),
    )(*pieces, hb, x, dy, ada, ada, norm_w, w_in_bf)


def _adamw(g, w, m, v):
    m = ADAM_B1 * m + (1.0 - ADAM_B1) * g
    v = ADAM_B2 * v + (1.0 - ADAM_B2) * (g * g)
    m_hat = m / (1.0 - ADAM_B1 ** ADAM_STEP)
    v_hat = v / (1.0 - ADAM_B2 ** ADAM_STEP)
    delta = -ADAM_LR * (m_hat / (jnp.sqrt(v_hat) + ADAM_EPS) + ADAM_WD * w)
    return delta, m, v


def _adam_shard_call(name, own, sib, w, m, v):
    r, c = w.shape
    tr = r if r <= 512 else 256

    def body(own_ref, sib_ref, w_ref, m_ref, v_ref, g_ref, d_ref, nm_ref, nv_ref):
        a = ((own_ref[0].astype(F32) + own_ref[1].astype(F32)) + own_ref[2].astype(F32)) + own_ref[3].astype(F32)
        b = ((sib_ref[0].astype(F32) + sib_ref[1].astype(F32)) + sib_ref[2].astype(F32)) + sib_ref[3].astype(F32)
        g = a + b
        g_ref[...] = g
        d_ref[...], nm_ref[...], nv_ref[...] = _adamw(g, w_ref[...], m_ref[...], v_ref[...])

    part = pl.BlockSpec((4, tr, c), lambda i: (0, i, 0))
    blk = pl.BlockSpec((tr, c), lambda i: (i, 0))
    return pl.pallas_call(
        body, name=name,
        grid=(r // tr,),
        in_specs=[part, part, blk, blk, blk],
        out_specs=[blk] * 4,
        out_shape=[jax.ShapeDtypeStruct((r, c), F32)] * 4,
        compiler_params=_params(("arbitrary",), VMEM_LIMIT),
    )(own, sib, w, m, v)


def _adam_ada_call(c_all, d_all, w, m, v):
    r, c = w.shape
    tr = 256

    def body(c_ref, d_ref, w_ref, m_ref, v_ref, g_ref, dl_ref, nm_ref, nv_ref):
        cc = c_ref[...]
        sc = cc * _sigmoid(cc)
        dd = d_ref[...]
        sc_hi = sc.astype(BF16)
        sc_lo = (sc - sc_hi.astype(F32)).astype(BF16)
        dd_hi = dd.astype(BF16)
        dd_lo = (dd - dd_hi.astype(F32)).astype(BF16)
        g = _dot_tn(sc_hi, dd_hi) + (_dot_tn(sc_hi, dd_lo) + _dot_tn(sc_lo, dd_hi))
        g_ref[...] = g
        dl_ref[...], nm_ref[...], nv_ref[...] = _adamw(g, w_ref[...], m_ref[...], v_ref[...])

    blk = pl.BlockSpec((tr, c), lambda i: (i, 0))
    return pl.pallas_call(
        body, name="adam_w_ada",
        grid=(r // tr,),
        in_specs=[pl.BlockSpec((16, tr), lambda i: (0, i)), pl.BlockSpec((16, c), lambda i: (0, 0)), blk, blk, blk],
        out_specs=[blk] * 4,
        out_shape=[jax.ShapeDtypeStruct((r, c), F32)] * 4,
        compiler_params=_params(("arbitrary",), VMEM_LIMIT),
    )(c_all, d_all, w, m, v)


def _adam_vectors_call(packs, offsets, vectors):
    nv = len(vectors)

    def body(*refs):
        p_ref, ins, outs = refs[0], refs[1:1 + 3 * nv], refs[1 + 3 * nv:]
        for j, off in enumerate(offsets):
            n = ins[3 * j].shape[1]
            span = -(-n // LANES) * LANES
            g = p_ref[0, :, off:off + span]
            for b in range(1, 8):
                g = g + p_ref[b, :, off:off + span]
            g = g[:, :n]
            outs[j][...] = g
            outs[nv + j][...], outs[2 * nv + j][...], outs[3 * nv + j][...] = _adamw(
                g, ins[3 * j][...], ins[3 * j + 1][...], ins[3 * j + 2][...])

    flat = [a for t in vectors for a in t]
    res = pl.pallas_call(
        body, name="adam_vectors",
        out_shape=[jax.ShapeDtypeStruct(t[0].shape, F32) for _ in range(4) for t in vectors],
    )(packs, *flat)
    return [res[k * nv:(k + 1) * nv] for k in range(4)]


ROPE_HALF = MLA_ROPE_DIM // 2
NOPE_A = MLA_NOPE_DIM - ROPE_HALF


def _zeros_like_lanes(t, n):
    return jnp.zeros(t.shape[:-1] + (n,), t.dtype)


def _to_head_lanes(t):
    nope, rope = t[..., :MLA_NOPE_DIM], t[..., MLA_NOPE_DIM:]
    return jnp.concatenate([rope[..., :ROPE_HALF], nope[..., :NOPE_A], rope[..., ROPE_HALF:], nope[..., NOPE_A:],
                            _zeros_like_lanes(t, HEAD_PAD - MLA_QK_DIM)], axis=-1)


def _from_head_lanes(t):
    return jnp.concatenate([t[..., ROPE_HALF:MLA_NOPE_DIM], t[..., 64 + ROPE_HALF:MLA_QK_DIM],
                            t[..., :ROPE_HALF], t[..., 64:64 + ROPE_HALF]], axis=-1)


def _nope_to_head_lanes(t):
    return jnp.concatenate([_zeros_like_lanes(t, ROPE_HALF), t[..., :NOPE_A], _zeros_like_lanes(t, ROPE_HALF),
                            t[..., NOPE_A:], _zeros_like_lanes(t, HEAD_PAD - MLA_QK_DIM)], axis=-1)


def _rope_to_head_lanes(t):
    return jnp.concatenate([t[..., :ROPE_HALF], _zeros_like_lanes(t, 64 - ROPE_HALF), t[..., ROPE_HALF:],
                            _zeros_like_lanes(t, 64 - ROPE_HALF)], axis=-1)


def _rope_tables(positions):
    inv_freq = (ROPE_THETA ** (-jnp.arange(0, MLA_ROPE_DIM, 2, dtype=F32) / MLA_ROPE_DIM))[None]
    signed = _rope_to_head_lanes(jnp.concatenate([-inv_freq, inv_freq], axis=1))
    ang = positions.astype(F32)[:, None] * signed
    return jnp.cos(ang), jnp.sin(ang)


def _unshard_cols(g):
    return jnp.transpose(g, (1, 0, 2)).reshape(g.shape[1], 4 * g.shape[2])


def _shard_cols(g):
    r, c4 = g.shape
    return jnp.transpose(g.reshape(r, 4, c4 // 4), (1, 0, 2))


def kernel(x, c, positions, w_ada, b_ada, norm_w, w_in, q_lora_norm, w_uq, kv_lora_norm, w_ukv, q_head_norm, k_head_norm, w_out, loss_target, m_w_ada, m_b_ada, m_norm_w, m_w_in, m_q_lora_norm, m_w_uq, m_kv_lora_norm, m_w_ukv, m_q_head_norm, m_k_head_norm, m_w_out, v_w_ada, v_b_ada, v_norm_w, v_w_in, v_q_lora_norm, v_w_uq, v_kv_lora_norm, v_w_ukv, v_q_head_norm, v_k_head_norm, v_w_out):
    chip = 2 * lax.axis_index("x") + lax.axis_index("y")
    me8 = 2 * chip + lax.axis_index("c")
    ada_cols = w_ada.shape[2]
    c_all = _allgather_rows_call(c)[:, 0, :]
    ada_part = _ada_call(c_all, w_ada[0], lax.dynamic_slice_in_dim(b_ada, chip * ada_cols, ada_cols, axis=1))
    ada_g, win_g, wuq_g, wukv_g, wout_g = _gather_call(
        [ada_part[None]] + [w.astype(BF16) for w in (w_in, w_uq, w_ukv, w_out)], [False, True, True, True, True])
    ada = lax.dynamic_slice_in_dim(ada_g, me8, 1, axis=1).reshape(1, 4 * ada_cols)
    (sq_sum, grad_x, g_w_in, g_w_uq, g_w_ukv, g_w_out, d_ada, g_norm_w, g_qln, g_kvln, g_qhn, g_khn) = _local_step(
        x[0], ada, positions[0], loss_target[0], norm_w, _unshard_cols(win_g),
        q_lora_norm, _unshard_cols(wuq_g), kv_lora_norm, _unshard_cols(wukv_g), q_head_norm, k_head_norm,
        wout_g.reshape(D_MODEL, D_MODEL))
    loss = lax.psum(0.5 * sq_sum / D_MODEL, ("x", "y", "c"))

    grads = [g.astype(BF16) for g in (_shard_cols(g_w_in), _shard_cols(g_w_uq), _shard_cols(g_w_ukv),
                                      g_w_out.reshape(4, D_MODEL // 4, D_MODEL))]
    pieces = [d_ada, g_norm_w, g_qln, g_kvln, g_qhn, g_khn]
    spans = [-(-p.shape[1] // LANES) * LANES for p in pieces]
    starts = [sum(spans[:j]) for j in range(len(spans))]
    small = jnp.concatenate([jnp.pad(p, ((0, 0), (0, sp - p.shape[1]))) for p, sp in zip(pieces, spans)], axis=1)
    own, sib, packs = _exchange_call(grads, small)

    names = ["adam_w_in", "adam_w_uq", "adam_w_ukv", "adam_w_out"]
    shard_w = [(w_in, m_w_in, v_w_in), (w_uq, m_w_uq, v_w_uq), (w_ukv, m_w_ukv, v_w_ukv),
               (w_out, m_w_out, v_w_out)]
    res = {}
    for name, o_g, s_g, (w, m, v) in zip(names, own, sib, shard_w):
        res[name] = _adam_shard_call(name, o_g, s_g, w[0], m[0], v[0])
    d_all = lax.dynamic_slice_in_dim(packs[:, 0, :], starts[0] + chip * ada_cols, ada_cols, axis=1)
    res_ada = _adam_ada_call(jnp.pad(c_all, ((0, 8), (0, 0))), jnp.pad(d_all, ((0, 8), (0, 0))),
                             w_ada[0], m_w_ada[0], v_w_ada[0])
    vectors = [(b_ada, m_b_ada, v_b_ada), (norm_w, m_norm_w, v_norm_w), (q_lora_norm, m_q_lora_norm, v_q_lora_norm),
               (kv_lora_norm, m_kv_lora_norm, v_kv_lora_norm), (q_head_norm, m_q_head_norm, v_q_head_norm),
               (k_head_norm, m_k_head_norm, v_k_head_norm)]
    vec_out = _adam_vectors_call(packs, starts, vectors)

    def ordered(kind):
        big = lambda name: res[name][kind][None]
        return [res_ada[kind][None], vec_out[kind][0], vec_out[kind][1], big("adam_w_in"), vec_out[kind][2],
                big("adam_w_uq"), vec_out[kind][3], big("adam_w_ukv"), vec_out[kind][4], vec_out[kind][5],
                big("adam_w_out")]

    return (loss, grad_x[None], *ordered(0), *ordered(1), *ordered(2), *ordered(3))


def _local_step(x2, ada, positions, tgt, norm_w, w_in_full, q_lora_norm, w_uq_full,
                kv_lora_norm, w_ukv_full, q_head_norm, k_head_norm, w_out_full):
    kr_block = _rope_to_head_lanes(w_in_full[:, 2688:2720])
    w_in_bf = jnp.concatenate([w_in_full[:, :2688], w_in_full[:, 2720:], kr_block], axis=1).astype(BF16)
    w_uq_bf = _to_head_lanes(w_uq_full.reshape(Q_LORA_RANK, MLA_HEADS, MLA_QK_DIM)).reshape(
        Q_LORA_RANK, MLA_PAD_WIDTH).astype(BF16)
    w_ukv_heads = w_ukv_full.reshape(KV_LORA_RANK, MLA_HEADS, 2 * MLA_NOPE_DIM)
    w_uk_bf = _nope_to_head_lanes(w_ukv_heads[:, :, :MLA_NOPE_DIM]).reshape(KV_LORA_RANK, MLA_PAD_WIDTH).astype(BF16)
    w_uv_bf = w_ukv_heads[:, :, MLA_NOPE_DIM:].reshape(KV_LORA_RANK, MLA_WIDTH).astype(BF16)
    w_out_bf = w_out_full.astype(BF16)
    qhn_pad, khn_pad = _to_head_lanes(q_head_norm), _to_head_lanes(k_head_norm)
    cos_t, sin_t = _rope_tables(positions)

    hb, q_sb, k_sb, v_sb, g_sb, c_q, c_kv, g_mla, k_rope = _pre_call(x2, ada, norm_w, w_in_bf)
    q_m, k_m, v_m, cqn, ckvn, q0, k0 = _mla_prep_call(
        c_q, c_kv, k_rope, cos_t, sin_t, q_lora_norm, kv_lora_norm, qhn_pad, khn_pad,
        w_uq_bf, w_uk_bf, w_uv_bf)
    o_sb, r_sb, kstart = _sb_fwd_call(q_sb, k_sb, v_sb)
    o_mla, lse = _mla_fwd_call(q_m, k_m, v_m)
    do_sb, do_mla, dg_sb, dg_mla, dy, g_w_out, d_gate, sq = _out_call(
        o_sb, g_sb, o_mla, g_mla, x2, tgt, ada, w_out_bf)

    dq_sb, dk_sb, dv_sb = _sb_bwd_call(kstart, q_sb, k_sb, v_sb, do_sb, r_sb)
    dq_m, dk_m, dv_m = _mla_bwd_call(q_m, k_m, v_m, do_mla, o_mla, lse)
    (d_cq, d_ckv, d_kr, g_wuq_pad, g_wuk_pad, g_wuv, g_qln, g_kvln, g_qhn, g_khn) = _mla_prep_bwd_call(
        dq_m, dk_m, dv_m, q0, k0, cqn, ckvn, c_q, c_kv, cos_t, sin_t,
        q_lora_norm, kv_lora_norm, qhn_pad, khn_pad, w_uq_bf, w_uk_bf, w_uv_bf)
    grad_x, g_win_pad, d_shift, d_scale, g_norm_w = _dh_call(
        [dq_sb, dk_sb, dv_sb, dg_sb, d_cq, d_ckv, dg_mla, d_kr], hb, x2, dy, ada, norm_w, w_in_bf)

    g_kr = g_win_pad[:, C_KR:]
    g_w_in = jnp.concatenate([g_win_pad[:, :2688], g_kr[:, :ROPE_HALF], g_kr[:, 64:64 + ROPE_HALF],
                              g_win_pad[:, 2688:3200]], axis=1)
    g_w_uq = _from_head_lanes(g_wuq_pad.reshape(Q_LORA_RANK, MLA_HEADS, HEAD_PAD)).reshape(Q_LORA_RANK, -1)
    g_w_ukv = jnp.concatenate(
        [_from_head_lanes(g_wuk_pad.reshape(KV_LORA_RANK, MLA_HEADS, HEAD_PAD))[:, :, :MLA_NOPE_DIM],
         g_wuv.reshape(KV_LORA_RANK, MLA_HEADS, MLA_NOPE_DIM)], axis=2).reshape(KV_LORA_RANK, -1)
    d_ada = jnp.concatenate([d_shift, d_scale, d_gate], axis=1)
    return (jnp.sum(sq), grad_x, g_w_in, g_w_uq, g_w_ukv, g_w_out, d_ada, g_norm_w, g_qln, g_kvln,
            _from_head_lanes(g_qhn), _from_head_lanes(g_khn))
```

```python
import functools
import math

import jax
import jax.numpy as jnp
from jax import lax
from jax.experimental import pallas as pl
from jax.experimental.pallas import tpu as pltpu

F32 = jnp.float32
BF16 = jnp.bfloat16
I32 = jnp.int32

D_MODEL = 1024
SB_HEADS = 8
SB_WIDTH = 512
MLA_HEADS = 8
MLA_QK_DIM = 96
MLA_NOPE_DIM = 64
MLA_ROPE_DIM = 32
MLA_WIDTH = 512
Q_LORA_RANK = 384
KV_LORA_RANK = 256
ROPE_THETA = 10000.0
EPS = 1e-6
LANES = 128
HEAD_PAD = 128
MLA_PAD_WIDTH = MLA_HEADS * HEAD_PAD

C_Q, C_K, C_V, C_G = 0, 512, 1024, 1536
C_CQ, C_CKV, C_GM, C_KR = 2048, 2432, 2688, 3200
IN_COLS_PAD = 3328

ADAM_LR = 0.001
ADAM_B1 = 0.9
ADAM_B2 = 0.999
ADAM_EPS = 1e-08
ADAM_WD = 0.01
ADAM_STEP = 10

SB_SCALE = 0.125
SB_GROUP = 4
MLA_SCALE = 1.0 / math.sqrt(MLA_QK_DIM)
LN2 = math.log(2.0)
MLA_SCALE_LOG2 = MLA_SCALE / LN2
MLA_BQ = 1024
MLA_BWD_BQ = 512
MLA_BK = 1024
MLA_BWD_BK = 512
SB_DEAD = -104.0
MASK_NEG = -1e30

VMEM_LIMIT = 56 * 1024 * 1024
MESH = pl.DeviceIdType.MESH


def _dot(a, b):
    return jnp.dot(a, b, preferred_element_type=F32)


def _dot_nt(a, b):
    return lax.dot_general(a, b, (((1,), (1,)), ((), ())), preferred_element_type=F32)


def _dot_tn(a, b):
    return lax.dot_general(a, b, (((0,), (0,)), ((), ())), preferred_element_type=F32)


def _sigmoid(x):
    return 1.0 / (1.0 + jnp.exp(-x))


def _split_dot(a, m):
    hi = a.astype(BF16)
    lo = (a - hi.astype(F32)).astype(BF16)
    return _dot(hi, m) + _dot(lo, m)


def _params(sem, vmem=None):
    return pltpu.CompilerParams(dimension_semantics=sem, vmem_limit_bytes=vmem)


def _row_tile(s, want):
    return min(want, s)


def _hbm_spec():
    return pl.BlockSpec(memory_space=pltpu.HBM)


def _allgather_rows_call(row):
    def body(in_ref, out_ref, send_sems, recv_sems, loc_sem):
        x, y, c = lax.axis_index("x"), lax.axis_index("y"), lax.axis_index("c")
        flips = [(fx, fy, fc) for fx in (0, 1) for fy in (0, 1) for fc in (0, 1)][1:]

        def peer(r):
            fx, fy, fc = flips[r]
            return ((1 - x) if fx else x, (1 - y) if fy else y, (1 - c) if fc else c)

        def copy(r, slot):
            return pltpu.make_async_remote_copy(
                src_ref=in_ref, dst_ref=out_ref.at[slot], send_sem=send_sems.at[r], recv_sem=recv_sems.at[r],
                device_id=peer(r), device_id_type=MESH)

        local = pltpu.make_async_copy(in_ref, out_ref.at[4 * x + 2 * y + c], loc_sem)
        local.start()
        sends = [copy(r, 4 * x + 2 * y + c) for r in range(7)]
        for cp in sends:
            cp.start()
        for r in range(7):
            px, py, pc = peer(r)
            copy(r, 4 * px + 2 * py + pc).wait_recv()
        for cp in sends:
            cp.wait_send()
        local.wait()

    return pl.pallas_call(
        body, name="gather_rows",
        out_shape=jax.ShapeDtypeStruct((8,) + row.shape, row.dtype),
        in_specs=[_hbm_spec()], out_specs=_hbm_spec(),
        scratch_shapes=[pltpu.SemaphoreType.DMA((7,)), pltpu.SemaphoreType.DMA((7,)), pltpu.SemaphoreType.DMA],
    )(row)


def _gather_call(shards, split):
    n = len(shards)
    halves = [s.shape[1] // 2 for s in shards]

    def body(*refs):
        ins, outs = refs[:n], refs[n:2 * n]
        ici_send, ici_recv, d2d_send, d2d_recv, loc_sems = refs[2 * n:]
        x, y, c = lax.axis_index("x"), lax.axis_index("y"), lax.axis_index("c")
        me = 2 * x + y
        peers = [(1 - x, y), (x, 1 - y), (1 - x, 1 - y)]

        def rows(a, which):
            return pl.ds(pl.multiple_of(which * halves[a], 16), halves[a])

        def ici(a, j, slot):
            px, py = peers[j]
            src, dst = ins[a].at[0], outs[a].at[slot]
            if split[a]:
                src, dst = src.at[rows(a, c)], dst.at[rows(a, c)]
            return pltpu.make_async_remote_copy(
                src_ref=src, dst_ref=dst,
                send_sem=ici_send.at[3 * a + j], recv_sem=ici_recv.at[3 * a + j],
                device_id=(px, py, c), device_id_type=MESH)

        def d2d(a, j, which):
            px, py = peers[j]
            piece = outs[a].at[2 * px + py, rows(a, which)]
            return pltpu.make_async_remote_copy(
                src_ref=piece, dst_ref=piece,
                send_sem=d2d_send.at[3 * a + j], recv_sem=d2d_recv.at[3 * a + j],
                device_id=(x, y, 1 - c), device_id_type=MESH)

        local = [pltpu.make_async_copy(ins[a].at[0], outs[a].at[me], loc_sems.at[a]) for a in range(n)]
        for cp in local:
            cp.start()
        sends = [ici(a, j, me) for a in range(n) for j in range(3)]
        for cp in sends:
            cp.start()
        for a in range(n):
            for j in range(3):
                px, py = peers[j]
                ici(a, j, 2 * px + py).wait_recv()
                if split[a]:
                    cp = d2d(a, j, c)
                    cp.start()
                    sends.append(cp)
        for a in range(n):
            for j in range(3):
                if split[a]:
                    d2d(a, j, 1 - c).wait_recv()
        for cp in sends:
            cp.wait_send()
        for cp in local:
            cp.wait()

    return pl.pallas_call(
        body, name="gather_weights",
        out_shape=[jax.ShapeDtypeStruct((4,) + s.shape[1:], s.dtype) for s in shards],
        in_specs=[_hbm_spec() for _ in shards],
        out_specs=[_hbm_spec() for _ in shards],
        scratch_shapes=[pltpu.SemaphoreType.DMA((3 * n,)), pltpu.SemaphoreType.DMA((3 * n,)),
                        pltpu.SemaphoreType.DMA((3 * n,)), pltpu.SemaphoreType.DMA((3 * n,)),
                        pltpu.SemaphoreType.DMA((n,))],
    )(*shards)


def _exchange_call(grads, small):
    n = len(grads)

    def body(*refs):
        g_in, small_in = refs[:n], refs[n]
        own, sib, packs = refs[n + 1:2 * n + 1], refs[2 * n + 1:3 * n + 1], refs[3 * n + 1]
        ici_send, ici_recv, d2d_send, d2d_recv, sm_send, sm_recv, loc_sems = refs[3 * n + 2:]
        x, y, c = lax.axis_index("x"), lax.axis_index("y"), lax.axis_index("c")
        me = 2 * x + y
        me8 = 4 * x + 2 * y + c
        sibling = (x, y, 1 - c)
        peers = [(1 - x, y), (x, 1 - y), (1 - x, 1 - y)]
        flips = [(fx, fy, fc) for fx in (0, 1) for fy in (0, 1) for fc in (0, 1)][1:]

        def ici(a, j, src_slot, dst_slot):
            px, py = peers[j]
            return pltpu.make_async_remote_copy(
                src_ref=g_in[a].at[src_slot], dst_ref=own[a].at[dst_slot],
                send_sem=ici_send.at[3 * a + j], recv_sem=ici_recv.at[3 * a + j],
                device_id=(px, py, c), device_id_type=MESH)

        def d2d(a, rel, chip, src):
            return pltpu.make_async_remote_copy(
                src_ref=src, dst_ref=sib[a].at[chip],
                send_sem=d2d_send.at[4 * a + rel], recv_sem=d2d_recv.at[4 * a + rel],
                device_id=sibling, device_id_type=MESH)

        def flipped(r):
            fx, fy, fc = flips[r]
            return ((1 - x) if fx else x, (1 - y) if fy else y, (1 - c) if fc else c)

        def sm(r, slot):
            return pltpu.make_async_remote_copy(
                src_ref=small_in, dst_ref=packs.at[slot],
                send_sem=sm_send.at[r], recv_sem=sm_recv.at[r],
                device_id=flipped(r), device_id_type=MESH)

        def peer8(r):
            px, py, pc = flipped(r)
            return 4 * px + 2 * py + pc

        local = [pltpu.make_async_copy(g_in[a].at[me], own[a].at[me], loc_sems.at[a]) for a in range(n)]
        local.append(pltpu.make_async_copy(small_in, packs.at[me8], loc_sems.at[n]))
        for cp in local:
            cp.start()
        sends = []
        for r in range(7):
            sends.append(sm(r, me8))
        for a in range(n):
            for j in range(3):
                px, py = peers[j]
                sends.append(ici(a, j, 2 * px + py, me))
        for cp in sends:
            cp.start()
        for a in range(n):
            cp = d2d(a, 0, me, g_in[a].at[me])
            cp.start()
            sends.append(cp)
        for a in range(n):
            for j in range(3):
                px, py = peers[j]
                ici(a, j, me, 2 * px + py).wait_recv()
                cp = d2d(a, 1 + j, 2 * px + py, own[a].at[2 * px + py])
                cp.start()
                sends.append(cp)
        for a in range(n):
            d2d(a, 0, me, g_in[a].at[me]).wait_recv()
            for j in range(3):
                px, py = peers[j]
                d2d(a, 1 + j, 2 * px + py, g_in[a].at[me]).wait_recv()
        for r in range(7):
            sm(r, peer8(r)).wait_recv()
        for cp in sends:
            cp.wait_send()
        for cp in local:
            cp.wait()

    out_shape = ([jax.ShapeDtypeStruct(g.shape, g.dtype) for g in grads] * 2
                 + [jax.ShapeDtypeStruct((8,) + small.shape, small.dtype)])
    res = pl.pallas_call(
        body, name="exchange_grads",
        out_shape=out_shape,
        in_specs=[_hbm_spec() for _ in range(n + 1)],
        out_specs=[_hbm_spec() for _ in range(2 * n + 1)],
        scratch_shapes=[pltpu.SemaphoreType.DMA((3 * n,)), pltpu.SemaphoreType.DMA((3 * n,)),
                        pltpu.SemaphoreType.DMA((4 * n,)), pltpu.SemaphoreType.DMA((4 * n,)),
                        pltpu.SemaphoreType.DMA((7,)), pltpu.SemaphoreType.DMA((7,)),
                        pltpu.SemaphoreType.DMA((n + 1,))],
    )(*grads, small)
    return res[:n], res[n:2 * n], res[2 * n]


def _ada_call(c_all, w_ada_cols, b_ada_cols):
    def body(c_ref, w_ref, b_ref, o_ref):
        cc = c_ref[...]
        o_ref[...] = _dot((cc * _sigmoid(cc)).astype(BF16), w_ref[...].astype(BF16)) + b_ref[...]

    return pl.pallas_call(
        body, name="ada_fwd",
        out_shape=jax.ShapeDtypeStruct((c_all.shape[0], w_ada_cols.shape[1]), F32),
        compiler_params=pltpu.CompilerParams(vmem_limit_bytes=VMEM_LIMIT),
    )(c_all, w_ada_cols, b_ada_cols)


def _ada_part(j):
    return pl.BlockSpec((1, D_MODEL), lambda i: (0, j))


def _full(shape):
    return pl.BlockSpec(shape, lambda i: (0,) * len(shape))


def _rows(tm, width):
    return pl.BlockSpec((tm, width), lambda i: (i, 0))


def _pre_call(x, ada, norm_w, w_in_bf):
    s = x.shape[0]
    tm = _row_tile(s, 512)
    groups = [(C_Q, 512, BF16), (C_K, 512, BF16), (C_V, 512, BF16), (C_G, 512, F32),
              (C_CQ, Q_LORA_RANK, F32), (C_CKV, KV_LORA_RANK, F32), (C_GM, 512, F32), (C_KR, LANES, F32)]

    def body(x_ref, sh_ref, sc_ref, nw_ref, w_ref, hb_ref, *outs):
        xx = x_ref[...]
        r0 = lax.rsqrt(jnp.mean(xx * xx, axis=-1, keepdims=True) + EPS)
        h = (xx * r0 * nw_ref[...]) * (1.0 + sc_ref[...]) + sh_ref[...]
        hb = h.astype(BF16)
        hb_ref[...] = hb
        for (c0, width, dt), o_ref in zip(groups, outs):
            o_ref[...] = _dot(hb, w_ref[:, c0:c0 + width]).astype(dt)

    return pl.pallas_call(
        body, name="pre_proj",
        grid=(s // tm,),
        in_specs=[_rows(tm, D_MODEL), _ada_part(0), _ada_part(1), _full((1, D_MODEL)),
                  _full((D_MODEL, IN_COLS_PAD))],
        out_specs=[_rows(tm, D_MODEL)] + [_rows(tm, w) for _, w, _ in groups],
        out_shape=[jax.ShapeDtypeStruct((s, D_MODEL), BF16)]
        + [jax.ShapeDtypeStruct((s, w), dt) for _, w, dt in groups],
        compiler_params=_params(("arbitrary",), VMEM_LIMIT),
    )(x, ada, ada, norm_w, w_in_bf)


def _rope(t, cos_t, sin_t):
    return t * cos_t + pltpu.roll(t, 64, 1) * sin_t


def _rope_adjoint(d, cos_t, sin_t):
    return d * cos_t + pltpu.roll(d * sin_t, 64, 1)


def _mla_prep_call(c_q, c_kv, k_rope, cos_t, sin_t, q_lora_norm, kv_lora_norm, qhn_pad, khn_pad,
                   w_uq_bf, w_uk_bf, w_uv_bf):
    s = c_q.shape[0]
    tm = _row_tile(s, 256)

    def body(cq_ref, ckv_ref, kr_ref, cos_ref, sin_ref, qln_ref, kvln_ref, qhn_ref, khn_ref,
             wuq_ref, wuk_ref, wuv_ref, q_ref, k_ref, v_ref, cqn_ref, ckvn_ref, q0_ref, k0_ref):
        cq = cq_ref[...]
        cqn = (cq * lax.rsqrt(jnp.mean(cq * cq, axis=-1, keepdims=True) + EPS) * qln_ref[...]).astype(BF16)
        cqn_ref[...] = cqn
        ckv = ckv_ref[...]
        ckvn = (ckv * lax.rsqrt(jnp.mean(ckv * ckv, axis=-1, keepdims=True) + EPS) * kvln_ref[...]).astype(BF16)
        ckvn_ref[...] = ckvn
        v_ref[...] = _dot(ckvn, wuv_ref[...]).astype(BF16)
        q0_ref[...] = _dot(cqn, wuq_ref[...])
        k0_ref[...] = _dot(ckvn, wuk_ref[...])
        cos_t, sin_t = cos_ref[...], sin_ref[...]
        kr = kr_ref[...]
        for h in range(MLA_HEADS):
            cols = slice(h * HEAD_PAD, (h + 1) * HEAD_PAD)
            q0 = q0_ref[:, cols]
            rq = lax.rsqrt(jnp.sum(q0 * q0, axis=-1, keepdims=True) * (1.0 / MLA_QK_DIM) + EPS)
            q_ref[:, cols] = (_rope(q0 * rq * qhn_ref[...], cos_t, sin_t) * MLA_SCALE_LOG2).astype(BF16)
            k0 = k0_ref[:, cols] + kr
            k0_ref[:, cols] = k0
            rk = lax.rsqrt(jnp.sum(k0 * k0, axis=-1, keepdims=True) * (1.0 / MLA_QK_DIM) + EPS)
            k_ref[:, cols] = _rope(k0 * rk * khn_ref[...], cos_t, sin_t).astype(BF16)

    return pl.pallas_call(
        body, name="mla_prep",
        grid=(s // tm,),
        in_specs=[_rows(tm, Q_LORA_RANK), _rows(tm, KV_LORA_RANK), _rows(tm, LANES),
                  _rows(tm, LANES), _rows(tm, LANES),
                  _full((1, Q_LORA_RANK)), _full((1, KV_LORA_RANK)), _full((1, LANES)), _full((1, LANES)),
                  _full((Q_LORA_RANK, MLA_PAD_WIDTH)), _full((KV_LORA_RANK, MLA_PAD_WIDTH)),
                  _full((KV_LORA_RANK, MLA_WIDTH))],
        out_specs=[_rows(tm, MLA_PAD_WIDTH), _rows(tm, MLA_PAD_WIDTH), _rows(tm, MLA_WIDTH),
                   _rows(tm, Q_LORA_RANK), _rows(tm, KV_LORA_RANK),
                   _rows(tm, MLA_PAD_WIDTH), _rows(tm, MLA_PAD_WIDTH)],
        out_shape=[jax.ShapeDtypeStruct((s, MLA_PAD_WIDTH), BF16), jax.ShapeDtypeStruct((s, MLA_PAD_WIDTH), BF16),
                   jax.ShapeDtypeStruct((s, MLA_WIDTH), BF16),
                   jax.ShapeDtypeStruct((s, Q_LORA_RANK), BF16), jax.ShapeDtypeStruct((s, KV_LORA_RANK), BF16),
                   jax.ShapeDtypeStruct((s, MLA_PAD_WIDTH), F32), jax.ShapeDtypeStruct((s, MLA_PAD_WIDTH), F32)],
        compiler_params=_params(("arbitrary",), VMEM_LIMIT),
    )(c_q, c_kv, k_rope, cos_t, sin_t, q_lora_norm, kv_lora_norm, qhn_pad, khn_pad,
      w_uq_bf, w_uk_bf, w_uv_bf)


def _log_sigmoid_pair(z):
    ls = jnp.minimum(z, 0.0) - jnp.log(1.0 + jnp.exp(-jnp.abs(z)))
    return ls, ls - z


def _pair(hh):
    return slice((hh // 2) * LANES, (hh // 2 + 1) * LANES)


def _sb_fwd_call(q, k, v):
    s = q.shape[0]
    bq = _row_tile(s, 256)
    nq = s // bq
    nh = SB_GROUP
    width = nh * 64

    def body(q_ref, k_ref, v_ref, o_ref, r_ref, ks_ref):
        hp, i = pl.program_id(0), pl.program_id(1)
        lane = lax.broadcasted_iota(I32, (bq, LANES), 1)
        row = lax.broadcasted_iota(I32, (bq, bq), 0)
        col = lax.broadcasted_iota(I32, (bq, bq), 1)
        strict = col < row
        later = jnp.where(row > col, 1.0, 0.0).astype(BF16)
        masks = [_head_mask(lane, hh).astype(BF16) for hh in range(2)]
        qms = [q_ref[:, _pair(hh)] * jnp.asarray(SB_SCALE, BF16) * masks[hh % 2] for hh in range(nh)]

        def walk(blocks, state):
            chains = [(kb, diagonal, hh) for kb, diagonal in blocks for hh in range(nh)]
            keys = lambda kb: pl.ds(pl.multiple_of(kb * bq, bq), bq)
            zs = [_dot_nt(qms[hh], k_ref[keys(kb), _pair(hh)]) for kb, _, hh in chains]
            pairs = []
            for z, (_, diagonal, _) in zip(zs, chains):
                ls, lk = _log_sigmoid_pair(z)
                pairs.append((ls, jnp.where(strict, lk, 0.0) if diagonal else lk))
            sums = [_split_dot(lk, later) for _, lk in pairs]
            runs = [st[0] for st in state]
            ws = []
            for (ls, lk), after, (_, diagonal, hh) in zip(pairs, sums, chains):
                w = jnp.exp(ls + (after + runs[hh]))
                ws.append((jnp.where(strict, w, 0.0) if diagonal else w).astype(BF16))
                runs[hh] = runs[hh] + jnp.sum(lk, axis=1, keepdims=True)
            accs = [st[1] for st in state]
            for w, (kb, _, hh) in zip(ws, chains):
                accs[hh] = accs[hh] + _dot(w, v_ref[keys(kb), _pair(hh)])
            return tuple(zip(runs, accs))

        def alive(state):
            top = jnp.max(state[0][0])
            for st in state[1:]:
                top = jnp.maximum(top, jnp.max(st[0]))
            return (top > SB_DEAD).astype(I32)

        def finish(state, first):
            ks_ref[hp, i] = first
            for pair in range(nh // 2):
                o_ref[:, _pair(2 * pair)] = jnp.where(lane < 64, state[2 * pair][1], state[2 * pair + 1][1])
                r_ref[:, _pair(2 * pair)] = jnp.where(lane < 64, state[2 * pair][0], state[2 * pair + 1][0])

        zero = ((jnp.zeros((bq, 1), F32), jnp.zeros((bq, LANES), F32)),) * nh

        @pl.when(i == 0)
        def _():
            finish(walk([(0, True)], zero), 0)

        @pl.when(i > 0)
        def _():
            state = walk([(i, True), (i - 1, False)], zero)

            def cond(carry):
                return jnp.logical_and(carry[0] >= 0, carry[1] > 0)

            def step(carry):
                state = walk([(carry[0], False)], carry[2])
                return carry[0] - 1, alive(state), state

            kb, _, state = lax.while_loop(cond, step, (i - 2, alive(state), state))
            finish(state, kb + 1)

    return pl.pallas_call(
        body, name="sb_fwd",
        grid=(SB_HEADS // nh, nq),
        in_specs=[pl.BlockSpec((bq, width), lambda h, i: (i, h)),
                  pl.BlockSpec((s, width), lambda h, i: (0, h)),
                  pl.BlockSpec((s, width), lambda h, i: (0, h))],
        out_specs=[pl.BlockSpec((bq, width), lambda h, i: (i, h)),
                   pl.BlockSpec((bq, width), lambda h, i: (i, h)),
                   pl.BlockSpec(memory_space=pltpu.SMEM)],
        out_shape=[jax.ShapeDtypeStruct((s, SB_WIDTH), F32), jax.ShapeDtypeStruct((s, SB_WIDTH), F32),
                   jax.ShapeDtypeStruct((SB_HEADS // nh, nq), I32)],
        compiler_params=_params(("arbitrary", "arbitrary"), VMEM_LIMIT),
    )(q, k, v)


def _mla_fwd_call(q, k, v):
    s = q.shape[0]
    bq = _row_tile(s, MLA_BQ)
    bk = _row_tile(s, MLA_BK)
    nq = s // bq
    assert bk % bq == 0
    half = bk // 2

    def body(q_ref, k_ref, v_ref, o_ref, lse_ref, p_ref, s_ref):
        i = pl.program_id(1)
        lane = lax.broadcasted_iota(I32, (bq, LANES), 1)
        row = lax.broadcasted_iota(I32, (half, bq), 1)
        col = lax.broadcasted_iota(I32, (half, bq), 0)
        n_full = (i * bq) // bk

        def keys(g):
            return pl.ds(pl.multiple_of(g * half, half), half)

        def join(left, right, qlo):
            return right if qlo == 0 else jnp.concatenate([left[:, :qlo], right], axis=1)

        def put_scores(g, slot, qlo=0):
            for hh in range(2):
                cols = slice(hh * HEAD_PAD, (hh + 1) * HEAD_PAD)
                s_ref[slot, hh, :, qlo:] = _dot_nt(k_ref[keys(g), cols], q_ref[qlo:, cols])

        def add_pv(carry, g, slot, qlo=0):
            vblk = v_ref[keys(g), :]
            out = []
            for hh, (m, l, acc, alpha) in enumerate(carry):
                upd = alpha[:, qlo:] * acc[:, qlo:] + _dot_tn(vblk, p_ref[slot, hh, :, qlo:])
                out.append((m, l, join(acc, upd, qlo), alpha))
            return tuple(out)

        def substep(g, slot, carry, masked, prefetch, qlo=0, next_qlo=0):
            if prefetch:
                put_scores(g + 1, 1 - slot, next_qlo)
            carry = add_pv(carry, jnp.maximum(g - 1, 0), 1 - slot)
            new = []
            for hh in range(2):
                m, l, acc, _ = carry[hh]
                sc = s_ref[slot, hh, :, qlo:]
                if masked:
                    sc = jnp.where(col[:, qlo:] + g * half <= row[:, qlo:] + i * bq, sc, MASK_NEG)
                m_new = jnp.maximum(m[:, qlo:], jnp.max(sc, axis=0, keepdims=True))
                p = jnp.exp2(sc - m_new)
                alpha = jnp.exp2(m[:, qlo:] - m_new)
                l_new = alpha * l[:, qlo:] + jnp.sum(p, axis=0, keepdims=True)
                p_ref[slot, hh, :, qlo:] = p.astype(BF16)
                new.append((join(m, m_new, qlo), join(l, l_new, qlo), acc, join(jnp.ones_like(m), alpha, qlo)))
            return tuple(new)

        skip = half if bk == bq else 0

        def chunk(kb, carry, masked):
            qlo = skip if masked else 0
            carry = substep(2 * kb, 0, carry, masked, True, 0, qlo)
            return substep(2 * kb + 1, 1, carry, masked, not masked, qlo, 0)

        p_ref[1] = jnp.zeros_like(p_ref[1])
        put_scores(0, 0)
        one = (jnp.full((1, bq), MASK_NEG, F32), jnp.zeros((1, bq), F32), jnp.zeros((LANES, bq), F32),
               jnp.ones((1, bq), F32))
        carry = lax.fori_loop(0, n_full, lambda kb, cr: chunk(kb, cr, False), (one, one))
        carry = chunk(n_full, carry, True)
        (m0, l0, a0, _), (m1, l1, a1, _) = add_pv(carry, 2 * n_full + 1, 1, skip)
        o_ref[...] = jnp.where(lane < 64, (a0 / l0).T, (a1 / l1).T)
        sub = lax.broadcasted_iota(I32, (8, bq), 0)
        lse_ref[...] = jnp.where(sub == 0, m0 + jnp.log2(l0), jnp.where(sub == 1, m1 + jnp.log2(l1), 0.0))

    return pl.pallas_call(
        body, name="mla_fwd",
        grid=(4, nq),
        in_specs=[pl.BlockSpec((bq, 2 * HEAD_PAD), lambda h, i: (i, h)),
                  pl.BlockSpec((s, 2 * HEAD_PAD), lambda h, i: (0, h)),
                  pl.BlockSpec((s, LANES), lambda h, i: (0, h))],
        out_specs=[pl.BlockSpec((bq, LANES), lambda h, i: (i, h)),
                   pl.BlockSpec((None, 8, bq), lambda h, i: (h, 0, i))],
        out_shape=[jax.ShapeDtypeStruct((s, MLA_WIDTH), F32), jax.ShapeDtypeStruct((4, 8, s), F32)],
        scratch_shapes=[pltpu.VMEM((2, 2, half, bq), BF16), pltpu.VMEM((2, 2, half, bq), F32)],
        compiler_params=_params(("arbitrary", "arbitrary"), VMEM_LIMIT),
    )(q, k, v)


def _out_call(o_sb, g_sb, o_mla, g_mla, x, target, ada, w_out_bf):
    s = x.shape[0]
    tm = _row_tile(s, 256)

    def body(osb_ref, gsb_ref, oml_ref, gml_ref, x_ref, t_ref, gate_ref, w_ref,
             dosb_ref, doml_ref, dgsb_ref, dgml_ref, dy_ref, gw_ref, dgate_ref, sq_ref):
        @pl.when(pl.program_id(0) == 0)
        def _():
            gw_ref[...] = jnp.zeros_like(gw_ref)
            dgate_ref[...] = jnp.zeros_like(dgate_ref)
            sq_ref[...] = jnp.zeros_like(sq_ref)

        g_s, g_m = gsb_ref[...], gml_ref[...]
        sig_s, sig_m = _sigmoid(g_s), _sigmoid(g_m)
        silu_s, silu_m = g_s * sig_s, g_m * sig_m
        o_s, o_m = osb_ref[...], oml_ref[...]
        mixed = jnp.concatenate([o_s * silu_s, o_m * silu_m], axis=1).astype(BF16)
        u = _dot(mixed, w_ref[...])
        gate_v = gate_ref[...]
        err = x_ref[...] + gate_v * u - t_ref[...]
        sq_ref[...] += jnp.sum(err * err, axis=0, keepdims=True)
        dy = err * (1.0 / D_MODEL)
        dy_ref[...] = dy
        dgate_ref[...] += jnp.sum(dy * u, axis=0, keepdims=True)
        du = (dy * gate_v).astype(BF16)
        gw_ref[...] += _dot_tn(mixed, du)
        dmix = _dot_nt(du, w_ref[...])
        dm_s, dm_m = dmix[:, :SB_WIDTH], dmix[:, SB_WIDTH:]
        dosb_ref[...] = (dm_s * silu_s).astype(BF16)
        doml_ref[...] = (dm_m * silu_m).astype(BF16)
        dgsb_ref[...] = (dm_s * o_s * (sig_s * (1.0 + g_s * (1.0 - sig_s)))).astype(BF16)
        dgml_ref[...] = (dm_m * o_m * (sig_m * (1.0 + g_m * (1.0 - sig_m)))).astype(BF16)

    return pl.pallas_call(
        body, name="out_proj_loss",
        grid=(s // tm,),
        in_specs=[_rows(tm, 512), _rows(tm, 512), _rows(tm, 512), _rows(tm, 512),
                  _rows(tm, D_MODEL), _rows(tm, D_MODEL), _ada_part(2), _full((D_MODEL, D_MODEL))],
        out_specs=[_rows(tm, 512), _rows(tm, 512), _rows(tm, 512), _rows(tm, 512), _rows(tm, D_MODEL),
                   _full((D_MODEL, D_MODEL)), _full((1, D_MODEL)), _full((1, D_MODEL))],
        out_shape=[jax.ShapeDtypeStruct((s, 512), BF16)] * 4
        + [jax.ShapeDtypeStruct((s, D_MODEL), F32), jax.ShapeDtypeStruct((D_MODEL, D_MODEL), F32),
           jax.ShapeDtypeStruct((1, D_MODEL), F32), jax.ShapeDtypeStruct((1, D_MODEL), F32)],
        compiler_params=_params(("arbitrary",), VMEM_LIMIT),
    )(o_sb, g_sb, o_mla, g_mla, x, target, ada, w_out_bf)


def _head_mask(lane, hh):
    return jnp.where((lane >= 64) if hh else (lane < 64), 1.0, 0.0)


def _pick_lane(packed, lane, which):
    return jnp.sum(jnp.where(lane == which, packed, 0.0), axis=1, keepdims=True)


def _sb_bwd_call(kstart, q, k, v, do, rfin):
    s = q.shape[0]
    bq = _row_tile(s, 256)
    nq = s // bq
    nh = SB_GROUP
    width = nh * 64

    def body(ks_ref, q_ref, k_ref, v_ref, do_ref, r_ref, dq_ref, dk_ref, dv_ref):
        hp, i = pl.program_id(0), pl.program_id(1)

        @pl.when(i == 0)
        def _():
            dk_ref[...] = jnp.zeros_like(dk_ref)
            dv_ref[...] = jnp.zeros_like(dv_ref)

        lane = lax.broadcasted_iota(I32, (bq, LANES), 1)
        row = lax.broadcasted_iota(I32, (bq, bq), 0)
        col = lax.broadcasted_iota(I32, (bq, bq), 1)
        upto = jnp.where(row <= col, 1.0, 0.0).astype(BF16)
        before = jnp.where(row < col, 1.0, 0.0).astype(BF16)
        masks = [_head_mask(lane, hh).astype(BF16) for hh in range(2)]
        qms = [q_ref[:, _pair(hh)] * jnp.asarray(SB_SCALE, BF16) * masks[hh % 2] for hh in range(nh)]
        doms = [do_ref[:, _pair(hh)] * masks[hh % 2] for hh in range(nh)]
        totals = [_pick_lane(r_ref[:, _pair(hh)], lane, 64 * (hh % 2)) for hh in range(nh)]
        strict = col < row

        def walk(blocks, state):
            chains = [(kb, diagonal, hh) for kb, diagonal in blocks for hh in range(nh)]
            keys = lambda kb: pl.ds(pl.multiple_of(kb * bq, bq), bq)
            cut = lambda x, diagonal: jnp.where(strict, x, 0.0) if diagonal else x
            zs = [_dot_nt(qms[hh], k_ref[keys(kb), _pair(hh)]) for kb, _, hh in chains]
            dws = [_dot_nt(doms[hh], v_ref[keys(kb), _pair(hh)]) for kb, _, hh in chains]
            pairs = []
            for z, (_, diagonal, _) in zip(zs, chains):
                ls, lk = _log_sigmoid_pair(z)
                pairs.append((ls, cut(lk, diagonal)))
            incls = [_split_dot(lk, upto) for _, lk in pairs]
            pres = [st[0] for st in state]
            ws, gs = [], []
            for (ls, lk), incl, dw, (_, diagonal, hh) in zip(pairs, incls, dws, chains):
                w = cut(jnp.exp(ls + ((totals[hh] - pres[hh]) - incl)), diagonal)
                ws.append(w.astype(BF16))
                gs.append(w * dw)
                pres[hh] = pres[hh] + jnp.sum(lk, axis=1, keepdims=True)
            gsums = [_dot(g.astype(BF16), before) for g in gs]
            gpres = [st[1] for st in state]
            dzs = []
            for (ls, _), g, gsum, (_, diagonal, hh) in zip(pairs, gs, gsums, chains):
                dzs.append(cut(g - jnp.exp(ls) * (g + (gpres[hh] + gsum)), diagonal).astype(BF16))
                gpres[hh] = gpres[hh] + jnp.sum(g, axis=1, keepdims=True)
            dqs = [st[2] for st in state]
            dk_parts, dv_parts = [], []
            for dzb, w, (kb, _, hh) in zip(dzs, ws, chains):
                dk_parts.append(_dot_tn(dzb, qms[hh]))
                dv_parts.append(_dot_tn(w, doms[hh]))
                dqs[hh] = dqs[hh] + _dot(dzb, k_ref[keys(kb), _pair(hh)])
            for b, (kb, _) in enumerate(blocks):
                for pair in range(nh // 2):
                    c0 = b * nh + 2 * pair
                    dk_ref[keys(kb), _pair(2 * pair)] += dk_parts[c0] + dk_parts[c0 + 1]
                    dv_ref[keys(kb), _pair(2 * pair)] += dv_parts[c0] + dv_parts[c0 + 1]
            return tuple(zip(pres, gpres, dqs))

        def finish(state):
            for pair in range(nh // 2):
                both = jnp.where(lane < 64, state[2 * pair][2], state[2 * pair + 1][2])
                dq_ref[:, _pair(2 * pair)] = (both * SB_SCALE).astype(BF16)

        zero = ((jnp.zeros((bq, 1), F32), jnp.zeros((bq, 1), F32), jnp.zeros((bq, LANES), F32)),) * nh

        @pl.when(i == 0)
        def _():
            finish(walk([(0, True)], zero))

        @pl.when(i > 0)
        def _():
            state = lax.fori_loop(ks_ref[hp, i], i - 1, lambda kb, st: walk([(kb, False)], st), zero)
            finish(walk([(i - 1, False), (i, True)], state))

    return pl.pallas_call(
        body, name="sb_bwd",
        grid_spec=pltpu.PrefetchScalarGridSpec(
            num_scalar_prefetch=1, grid=(SB_HEADS // nh, nq),
            in_specs=[pl.BlockSpec((bq, width), lambda h, i, ks: (i, h)),
                      pl.BlockSpec((s, width), lambda h, i, ks: (0, h), pipeline_mode=pl.Buffered(1)),
                      pl.BlockSpec((s, width), lambda h, i, ks: (0, h), pipeline_mode=pl.Buffered(1)),
                      pl.BlockSpec((bq, width), lambda h, i, ks: (i, h)),
                      pl.BlockSpec((bq, width), lambda h, i, ks: (i, h))],
            out_specs=[pl.BlockSpec((bq, width), lambda h, i, ks: (i, h)),
                       pl.BlockSpec((s, width), lambda h, i, ks: (0, h), pipeline_mode=pl.Buffered(1)),
                       pl.BlockSpec((s, width), lambda h, i, ks: (0, h), pipeline_mode=pl.Buffered(1))]),
        out_shape=[jax.ShapeDtypeStruct((s, SB_WIDTH), BF16), jax.ShapeDtypeStruct((s, SB_WIDTH), F32),
                   jax.ShapeDtypeStruct((s, SB_WIDTH), F32)],
        compiler_params=_params(("arbitrary", "arbitrary"), VMEM_LIMIT),
    )(kstart, q, k, v, do, rfin)


def _mla_bwd_call(q, k, v, do, o, lse):
    s = q.shape[0]
    bq = _row_tile(s, MLA_BWD_BQ)
    bk = _row_tile(s, MLA_BWD_BK)
    nq = s // bq
    assert bk % bq == 0
    half = bk // 2

    def body(q_ref, k_ref, v_ref, do_ref, o_ref, lse_ref, dq_ref, dk_ref, dv_ref, dom_ref, s_ref, dp_ref, pb_ref,
             ds_ref):
        i = pl.program_id(1)

        @pl.when(i == 0)
        def _():
            dk_ref[...] = jnp.zeros_like(dk_ref)
            dv_ref[...] = jnp.zeros_like(dv_ref)

        lane = lax.broadcasted_iota(I32, (bq, LANES), 1)
        row = lax.broadcasted_iota(I32, (half, bq), 1)
        col = lax.broadcasted_iota(I32, (half, bq), 0)
        n_full = (i * bq) // bk
        do2 = do_ref[...]
        prod = do2.astype(F32) * o_ref[...]
        ones = jnp.ones((8, LANES), BF16)
        deltas, lses = [], []
        for hh in range(2):
            head = _head_mask(lane, hh)
            dom_ref[hh] = do2 * head.astype(BF16)
            part = prod * head
            hi = part.astype(BF16)
            lo = (part - hi.astype(F32)).astype(BF16)
            deltas.append((_dot_nt(ones, hi) + _dot_nt(ones, lo))[0:1])
            lses.append(lse_ref[hh:hh + 1, :])

        def keys(g):
            return pl.ds(pl.multiple_of(g * half, half), half)

        def heads():
            return [(hh, slice(hh * HEAD_PAD, (hh + 1) * HEAD_PAD)) for hh in range(2)]

        def put_products(g, slot, qlo=0):
            vblk = v_ref[keys(g), :]
            for hh, cols in heads():
                s_ref[slot, hh, :, qlo:] = _dot_nt(k_ref[keys(g), cols], q_ref[qlo:, cols])
                dp_ref[slot, hh, :, qlo:] = _dot_nt(vblk, dom_ref[hh, qlo:, :])

        def add_grads(dqs, g, slot, qlo=0):
            rows = keys(g)
            new, dv_parts = [], []
            for hh, cols in heads():
                ds = ds_ref[slot, hh, :, qlo:]
                dk_ref[rows, cols] += _dot(ds, q_ref[qlo:, cols])
                dv_parts.append(_dot(pb_ref[slot, hh, :, qlo:], dom_ref[hh, qlo:, :]))
                upd = dqs[hh][:, qlo:] + _dot_tn(k_ref[rows, cols], ds)
                new.append(upd if qlo == 0 else jnp.concatenate([dqs[hh][:, :qlo], upd], axis=1))
            dv_ref[rows, :] += dv_parts[0] + dv_parts[1]
            return tuple(new)

        def substep(g, slot, dqs, masked, prefetch, qlo=0, next_qlo=0):
            if prefetch:
                put_products(g + 1, 1 - slot, next_qlo)
            dqs = add_grads(dqs, jnp.maximum(g - 1, 0), 1 - slot)
            for hh, _ in heads():
                p = jnp.exp2(s_ref[slot, hh, :, qlo:] - lses[hh][:, qlo:])
                if masked:
                    p = jnp.where(col[:, qlo:] + g * half <= row[:, qlo:] + i * bq, p, 0.0)
                ds_ref[slot, hh, :, qlo:] = (p * (dp_ref[slot, hh, :, qlo:] - deltas[hh][:, qlo:])).astype(BF16)
                pb_ref[slot, hh, :, qlo:] = p.astype(BF16)
            return dqs

        skip = half if bk == bq else 0

        def chunk(kb, dqs, masked):
            qlo = skip if masked else 0
            dqs = substep(2 * kb, 0, dqs, masked, True, 0, qlo)
            return substep(2 * kb + 1, 1, dqs, masked, not masked, qlo, 0)

        ds_ref[1] = jnp.zeros_like(ds_ref[1])
        pb_ref[1] = jnp.zeros_like(pb_ref[1])
        put_products(0, 0)
        zero = jnp.zeros((HEAD_PAD, bq), F32)
        dqs = lax.fori_loop(0, n_full, lambda kb, dqs: chunk(kb, dqs, False), (zero, zero))
        dqs = chunk(n_full, dqs, True)
        dqs = add_grads(dqs, 2 * n_full + 1, 1, skip)
        dq_ref[:, :HEAD_PAD] = dqs[0].T * MLA_SCALE
        dq_ref[:, HEAD_PAD:] = dqs[1].T * MLA_SCALE

    return pl.pallas_call(
        body, name="mla_bwd",
        grid=(4, nq),
        in_specs=[pl.BlockSpec((bq, 2 * HEAD_PAD), lambda h, i: (i, h)),
                  pl.BlockSpec((s, 2 * HEAD_PAD), lambda h, i: (0, h)),
                  pl.BlockSpec((s, LANES), lambda h, i: (0, h)),
                  pl.BlockSpec((bq, LANES), lambda h, i: (i, h)),
                  pl.BlockSpec((bq, LANES), lambda h, i: (i, h)),
                  pl.BlockSpec((None, 8, bq), lambda h, i: (h, 0, i))],
        out_specs=[pl.BlockSpec((bq, 2 * HEAD_PAD), lambda h, i: (i, h)),
                   pl.BlockSpec((s, 2 * HEAD_PAD), lambda h, i: (0, h)),
                   pl.BlockSpec((s, LANES), lambda h, i: (0, h))],
        out_shape=[jax.ShapeDtypeStruct((s, MLA_PAD_WIDTH), F32), jax.ShapeDtypeStruct((s, MLA_PAD_WIDTH), F32),
                   jax.ShapeDtypeStruct((s, MLA_WIDTH), F32)],
        scratch_shapes=[pltpu.VMEM((2, bq, LANES), BF16),
                        pltpu.VMEM((2, 2, half, bq), F32), pltpu.VMEM((2, 2, half, bq), F32),
                        pltpu.VMEM((2, 2, half, bq), BF16), pltpu.VMEM((2, 2, half, bq), BF16)],
        compiler_params=_params(("arbitrary", "arbitrary"), VMEM_LIMIT),
    )(q, k, v, do, o, lse)


def _rms_bwd(d_out, inp, r, weight, n):
    normed = inp * r
    gw = d_out * weight
    d_in = r * (gw - normed * (jnp.sum(gw * normed, axis=-1, keepdims=True) * (1.0 / n)))
    return d_in, d_out * normed


def _mla_prep_bwd_call(dq, dk, dv, q0, k0, cqn, ckvn, c_q, c_kv, cos_t, sin_t,
                       q_lora_norm, kv_lora_norm, qhn_pad, khn_pad, w_uq_bf, w_uk_bf, w_uv_bf):
    s = dq.shape[0]
    tm = _row_tile(s, 256)

    def body(dq_ref, dk_ref, dv_ref, q0_ref, k0_ref, cqn_ref, ckvn_ref, cq_ref, ckv_ref,
             cos_ref, sin_ref, qln_ref, kvln_ref, qhn_ref, khn_ref, wuq_ref, wuk_ref, wuv_ref,
             dcq_ref, dckv_ref, dkr_ref, gwuq_ref, gwuk_ref, gwuv_ref, gqln_ref, gkvln_ref, gqhn_ref, gkhn_ref,
             dq0_ref, dk0_ref):
        @pl.when(pl.program_id(0) == 0)
        def _():
            for ref in (gwuq_ref, gwuk_ref, gwuv_ref, gqln_ref, gkvln_ref, gqhn_ref, gkhn_ref):
                ref[...] = jnp.zeros_like(ref)

        cos_t, sin_t = cos_ref[...], sin_ref[...]
        lane = lax.broadcasted_iota(I32, (tm, LANES), 1)
        rope_lanes = jnp.logical_or(lane < ROPE_HALF, jnp.logical_and(lane >= 64, lane < 64 + ROPE_HALF))
        d_kr = jnp.zeros((tm, LANES), F32)
        g_qhn = jnp.zeros((1, LANES), F32)
        g_khn = jnp.zeros((1, LANES), F32)
        for h in range(MLA_HEADS):
            cols = slice(h * HEAD_PAD, (h + 1) * HEAD_PAD)
            q0 = q0_ref[:, cols]
            rq = lax.rsqrt(jnp.sum(q0 * q0, axis=-1, keepdims=True) * (1.0 / MLA_QK_DIM) + EPS)
            d_q0, gq = _rms_bwd(_rope_adjoint(dq_ref[:, cols], cos_t, sin_t), q0, rq, qhn_ref[...], MLA_QK_DIM)
            g_qhn += jnp.sum(gq, axis=0, keepdims=True)
            dq0_ref[:, cols] = d_q0.astype(BF16)
            k0 = k0_ref[:, cols]
            rk = lax.rsqrt(jnp.sum(k0 * k0, axis=-1, keepdims=True) * (1.0 / MLA_QK_DIM) + EPS)
            d_k0, gk = _rms_bwd(_rope_adjoint(dk_ref[:, cols] * LN2, cos_t, sin_t), k0, rk, khn_ref[...],
                                MLA_QK_DIM)
            g_khn += jnp.sum(gk, axis=0, keepdims=True)
            d_kr += jnp.where(rope_lanes, d_k0, 0.0)
            dk0_ref[:, cols] = d_k0.astype(BF16)
        cqn, ckvn = cqn_ref[...], ckvn_ref[...]
        d_q0b, d_k0b, dvb = dq0_ref[...], dk0_ref[...], dv_ref[...].astype(BF16)
        d_cqn = _dot_nt(d_q0b, wuq_ref[...])
        gwuq_ref[...] += _dot_tn(cqn, d_q0b)
        d_ckvn = _dot_nt(d_k0b, wuk_ref[...]) + _dot_nt(dvb, wuv_ref[...])
        gwuk_ref[...] += _dot_tn(ckvn, d_k0b)
        gwuv_ref[...] += _dot_tn(ckvn, dvb)
        gqhn_ref[...] += g_qhn
        gkhn_ref[...] += g_khn
        dkr_ref[...] = d_kr.astype(BF16)
        cq = cq_ref[...]
        rcq = lax.rsqrt(jnp.mean(cq * cq, axis=-1, keepdims=True) + EPS)
        d_cq, gl = _rms_bwd(d_cqn, cq, rcq, qln_ref[...], Q_LORA_RANK)
        dcq_ref[...] = d_cq.astype(BF16)
        gqln_ref[...] += jnp.sum(gl, axis=0, keepdims=True)
        ckv = ckv_ref[...]
        rckv = lax.rsqrt(jnp.mean(ckv * ckv, axis=-1, keepdims=True) + EPS)
        d_ckv, gl = _rms_bwd(d_ckvn, ckv, rckv, kvln_ref[...], KV_LORA_RANK)
        dckv_ref[...] = d_ckv.astype(BF16)
        gkvln_ref[...] += jnp.sum(gl, axis=0, keepdims=True)

    return pl.pallas_call(
        body, name="mla_prep_bwd",
        grid=(s // tm,),
        in_specs=[_rows(tm, MLA_PAD_WIDTH), _rows(tm, MLA_PAD_WIDTH), _rows(tm, MLA_WIDTH),
                  _rows(tm, MLA_PAD_WIDTH), _rows(tm, MLA_PAD_WIDTH),
                  _rows(tm, Q_LORA_RANK), _rows(tm, KV_LORA_RANK), _rows(tm, Q_LORA_RANK), _rows(tm, KV_LORA_RANK),
                  _rows(tm, LANES), _rows(tm, LANES),
                  _full((1, Q_LORA_RANK)), _full((1, KV_LORA_RANK)), _full((1, LANES)), _full((1, LANES)),
                  _full((Q_LORA_RANK, MLA_PAD_WIDTH)), _full((KV_LORA_RANK, MLA_PAD_WIDTH)),
                  _full((KV_LORA_RANK, MLA_WIDTH))],
        out_specs=[_rows(tm, Q_LORA_RANK), _rows(tm, KV_LORA_RANK), _rows(tm, LANES),
                   _full((Q_LORA_RANK, MLA_PAD_WIDTH)), _full((KV_LORA_RANK, MLA_PAD_WIDTH)),
                   _full((KV_LORA_RANK, MLA_WIDTH)),
                   _full((1, Q_LORA_RANK)), _full((1, KV_LORA_RANK)), _full((1, LANES)), _full((1, LANES))],
        out_shape=[jax.ShapeDtypeStruct((s, Q_LORA_RANK), BF16), jax.ShapeDtypeStruct((s, KV_LORA_RANK), BF16),
                   jax.ShapeDtypeStruct((s, LANES), BF16),
                   jax.ShapeDtypeStruct((Q_LORA_RANK, MLA_PAD_WIDTH), F32),
                   jax.ShapeDtypeStruct((KV_LORA_RANK, MLA_PAD_WIDTH), F32),
                   jax.ShapeDtypeStruct((KV_LORA_RANK, MLA_WIDTH), F32),
                   jax.ShapeDtypeStruct((1, Q_LORA_RANK), F32), jax.ShapeDtypeStruct((1, KV_LORA_RANK), F32),
                   jax.ShapeDtypeStruct((1, LANES), F32), jax.ShapeDtypeStruct((1, LANES), F32)],
        scratch_shapes=[pltpu.VMEM((tm, MLA_PAD_WIDTH), BF16), pltpu.VMEM((tm, MLA_PAD_WIDTH), BF16)],
        compiler_params=_params(("arbitrary",), VMEM_LIMIT),
    )(dq, dk, dv, q0, k0, cqn, ckvn, c_q, c_kv, cos_t, sin_t,
      q_lora_norm, kv_lora_norm, qhn_pad, khn_pad, w_uq_bf, w_uk_bf, w_uv_bf)


def _dh_call(pieces, hb, x, dy, ada, norm_w, w_in_bf):
    s = x.shape[0]
    tm = _row_tile(s, 256)
    widths = [p.shape[1] for p in pieces]
    offsets = [sum(widths[:j]) for j in range(len(widths))]
    assert offsets[-1] + widths[-1] == IN_COLS_PAD
    n = len(pieces)

    def body(*refs):
        p_refs = refs[:n]
        (hb_ref, x_ref, dy_ref, sh_ref, sc_ref, nw_ref, w_ref, gx_ref, gw_ref, dsh_ref, dsc_ref, gnw_ref,
         dp_ref) = refs[n:]

        @pl.when(pl.program_id(0) == 0)
        def _():
            gw_ref[...] = jnp.zeros_like(gw_ref)
            dsh_ref[...] = jnp.zeros_like(dsh_ref)
            dsc_ref[...] = jnp.zeros_like(dsc_ref)
            gnw_ref[...] = jnp.zeros_like(gnw_ref)

        for p_ref, c0, width in zip(p_refs, offsets, widths):
            dp_ref[:, c0:c0 + width] = p_ref[...].astype(BF16)
        gw_ref[...] += _dot_tn(hb_ref[...], dp_ref[...])
        dh = _dot_nt(dp_ref[...], w_ref[...])
        xx = x_ref[...]
        r0 = lax.rsqrt(jnp.mean(xx * xx, axis=-1, keepdims=True) + EPS)
        xn = xx * r0
        nw = nw_ref[...]
        dsh_ref[...] += jnp.sum(dh, axis=0, keepdims=True)
        dsc_ref[...] += jnp.sum(dh * (xn * nw), axis=0, keepdims=True)
        dn = dh * (1.0 + sc_ref[...])
        gnw_ref[...] += jnp.sum(dn * xn, axis=0, keepdims=True)
        dxn = dn * nw
        gx_ref[...] = dy_ref[...] + r0 * (dxn - xn * jnp.mean(dxn * xn, axis=-1, keepdims=True))

    return pl.pallas_call(
        body, name="in_proj_bwd",
        grid=(s // tm,),
        in_specs=[_rows(tm, w) for w in widths]
        + [_rows(tm, D_MODEL), _rows(tm, D_MODEL), _rows(tm, D_MODEL),
           _ada_part(0), _ada_part(1), _full((1, D_MODEL)),
           pl.BlockSpec((D_MODEL, IN_COLS_PAD), lambda i: (0, 0), pipeline_mode=pl.Buffered(1))],
        out_specs=[_rows(tm, D_MODEL),
                   pl.BlockSpec((D_MODEL, IN_COLS_PAD), lambda i: (0, 0), pipeline_mode=pl.Buffered(1)),
                   _full((1, D_MODEL)), _full((1, D_MODEL)), _full((1, D_MODEL))],
        out_shape=[jax.ShapeDtypeStruct((s, D_MODEL), F32), jax.ShapeDtypeStruct((D_MODEL, IN_COLS_PAD), F32),
                   jax.ShapeDtypeStruct((1, D_MODEL), F32), jax.ShapeDtypeStruct((1, D_MODEL), F32),
                   jax.ShapeDtypeStruct((1, D_MODEL), F32)],
        scratch_shapes=[pltpu.VMEM((tm, IN_COLS_PAD), BF16)],
        compiler_params=_params(("arbitrary",), VMEM_LIMIT),
    )(*pieces, hb, x, dy, ada, ada, norm_w, w_in_bf)


def _adamw(g, w, m, v):
    m = ADAM_B1 * m + (1.0 - ADAM_B1) * g
    v = ADAM_B2 * v + (1.0 - ADAM_B2) * (g * g)
    m_hat = m / (1.0 - ADAM_B1 ** ADAM_STEP)
    v_hat = v / (1.0 - ADAM_B2 ** ADAM_STEP)
    delta = -ADAM_LR * (m_hat / (jnp.sqrt(v_hat) + ADAM_EPS) + ADAM_WD * w)
    return delta, m, v


def _adam_shard_call(name, own, sib, w, m, v):
    r, c = w.shape
    tr = r if r <= 512 else 256

    def body(own_ref, sib_ref, w_ref, m_ref, v_ref, g_ref, d_ref, nm_ref, nv_ref):
        a = ((own_ref[0].astype(F32) + own_ref[1].astype(F32)) + own_ref[2].astype(F32)) + own_ref[3].astype(F32)
        b = ((sib_ref[0].astype(F32) + sib_ref[1].astype(F32)) + sib_ref[2].astype(F32)) + sib_ref[3].astype(F32)
        g = a + b
        g_ref[...] = g
        d_ref[...], nm_ref[...], nv_ref[...] = _adamw(g, w_ref[...], m_ref[...], v_ref[...])

    part = pl.BlockSpec((4, tr, c), lambda i: (0, i, 0))
    blk = pl.BlockSpec((tr, c), lambda i: (i, 0))
    return pl.pallas_call(
        body, name=name,
        grid=(r // tr,),
        in_specs=[part, part, blk, blk, blk],
        out_specs=[blk] * 4,
        out_shape=[jax.ShapeDtypeStruct((r, c), F32)] * 4,
        compiler_params=_params(("arbitrary",), VMEM_LIMIT),
    )(own, sib, w, m, v)


def _adam_ada_call(c_all, d_all, w, m, v):
    r, c = w.shape
    tr = 256

    def body(c_ref, d_ref, w_ref, m_ref, v_ref, g_ref, dl_ref, nm_ref, nv_ref):
        cc = c_ref[...]
        sc = cc * _sigmoid(cc)
        dd = d_ref[...]
        sc_hi = sc.astype(BF16)
        sc_lo = (sc - sc_hi.astype(F32)).astype(BF16)
        dd_hi = dd.astype(BF16)
        dd_lo = (dd - dd_hi.astype(F32)).astype(BF16)
        g = _dot_tn(sc_hi, dd_hi) + (_dot_tn(sc_hi, dd_lo) + _dot_tn(sc_lo, dd_hi))
        g_ref[...] = g
        dl_ref[...], nm_ref[...], nv_ref[...] = _adamw(g, w_ref[...], m_ref[...], v_ref[...])

    blk = pl.BlockSpec((tr, c), lambda i: (i, 0))
    return pl.pallas_call(
        body, name="adam_w_ada",
        grid=(r // tr,),
        in_specs=[pl.BlockSpec((16, tr), lambda i: (0, i)), pl.BlockSpec((16, c), lambda i: (0, 0)), blk, blk, blk],
        out_specs=[blk] * 4,
        out_shape=[jax.ShapeDtypeStruct((r, c), F32)] * 4,
        compiler_params=_params(("arbitrary",), VMEM_LIMIT),
    )(c_all, d_all, w, m, v)


def _adam_vectors_call(packs, offsets, vectors):
    nv = len(vectors)

    def body(*refs):
        p_ref, ins, outs = refs[0], refs[1:1 + 3 * nv], refs[1 + 3 * nv:]
        for j, off in enumerate(offsets):
            n = ins[3 * j].shape[1]
            span = -(-n // LANES) * LANES
            g = p_ref[0, :, off:off + span]
            for b in range(1, 8):
                g = g + p_ref[b, :, off:off + span]
            g = g[:, :n]
            outs[j][...] = g
            outs[nv + j][...], outs[2 * nv + j][...], outs[3 * nv + j][...] = _adamw(
                g, ins[3 * j][...], ins[3 * j + 1][...], ins[3 * j + 2][...])

    flat = [a for t in vectors for a in t]
    res = pl.pallas_call(
        body, name="adam_vectors",
        out_shape=[jax.ShapeDtypeStruct(t[0].shape, F32) for _ in range(4) for t in vectors],
    )(packs, *flat)
    return [res[k * nv:(k + 1) * nv] for k in range(4)]


ROPE_HALF = MLA_ROPE_DIM // 2
NOPE_A = MLA_NOPE_DIM - ROPE_HALF


def _zeros_like_lanes(t, n):
    return jnp.zeros(t.shape[:-1] + (n,), t.dtype)


def _to_head_lanes(t):
    nope, rope = t[..., :MLA_NOPE_DIM], t[..., MLA_NOPE_DIM:]
    return jnp.concatenate([rope[..., :ROPE_HALF], nope[..., :NOPE_A], rope[..., ROPE_HALF:], nope[..., NOPE_A:],
                            _zeros_like_lanes(t, HEAD_PAD - MLA_QK_DIM)], axis=-1)


def _from_head_lanes(t):
    return jnp.concatenate([t[..., ROPE_HALF:MLA_NOPE_DIM], t[..., 64 + ROPE_HALF:MLA_QK_DIM],
                            t[..., :ROPE_HALF], t[..., 64:64 + ROPE_HALF]], axis=-1)


def _nope_to_head_lanes(t):
    return jnp.concatenate([_zeros_like_lanes(t, ROPE_HALF), t[..., :NOPE_A], _zeros_like_lanes(t, ROPE_HALF),
                            t[..., NOPE_A:], _zeros_like_lanes(t, HEAD_PAD - MLA_QK_DIM)], axis=-1)


def _rope_to_head_lanes(t):
    return jnp.concatenate([t[..., :ROPE_HALF], _zeros_like_lanes(t, 64 - ROPE_HALF), t[..., ROPE_HALF:],
                            _zeros_like_lanes(t, 64 - ROPE_HALF)], axis=-1)


def _rope_tables(positions):
    inv_freq = (ROPE_THETA ** (-jnp.arange(0, MLA_ROPE_DIM, 2, dtype=F32) / MLA_ROPE_DIM))[None]
    signed = _rope_to_head_lanes(jnp.concatenate([-inv_freq, inv_freq], axis=1))
    ang = positions.astype(F32)[:, None] * signed
    return jnp.cos(ang), jnp.sin(ang)


def _unshard_cols(g):
    return jnp.transpose(g, (1, 0, 2)).reshape(g.shape[1], 4 * g.shape[2])


def _shard_cols(g):
    r, c4 = g.shape
    return jnp.transpose(g.reshape(r, 4, c4 // 4), (1, 0, 2))


def kernel(x, c, positions, w_ada, b_ada, norm_w, w_in, q_lora_norm, w_uq, kv_lora_norm, w_ukv, q_head_norm, k_head_norm, w_out, loss_target, m_w_ada, m_b_ada, m_norm_w, m_w_in, m_q_lora_norm, m_w_uq, m_kv_lora_norm, m_w_ukv, m_q_head_norm, m_k_head_norm, m_w_out, v_w_ada, v_b_ada, v_norm_w, v_w_in, v_q_lora_norm, v_w_uq, v_kv_lora_norm, v_w_ukv, v_q_head_norm, v_k_head_norm, v_w_out):
    chip = 2 * lax.axis_index("x") + lax.axis_index("y")
    me8 = 2 * chip + lax.axis_index("c")
    ada_cols = w_ada.shape[2]
    c_all = _allgather_rows_call(c)[:, 0, :]
    ada_part = _ada_call(c_all, w_ada[0], lax.dynamic_slice_in_dim(b_ada, chip * ada_cols, ada_cols, axis=1))
    ada_g, win_g, wuq_g, wukv_g, wout_g = _gather_call(
        [ada_part[None]] + [w.astype(BF16) for w in (w_in, w_uq, w_ukv, w_out)], [False, True, True, True, True])
    ada = lax.dynamic_slice_in_dim(ada_g, me8, 1, axis=1).reshape(1, 4 * ada_cols)
    (sq_sum, grad_x, g_w_in, g_w_uq, g_w_ukv, g_w_out, d_ada, g_norm_w, g_qln, g_kvln, g_qhn, g_khn) = _local_step(
        x[0], ada, positions[0], loss_target[0], norm_w, _unshard_cols(win_g),
        q_lora_norm, _unshard_cols(wuq_g), kv_lora_norm, _unshard_cols(wukv_g), q_head_norm, k_head_norm,
        wout_g.reshape(D_MODEL, D_MODEL))
    loss = lax.psum(0.5 * sq_sum / D_MODEL, ("x", "y", "c"))

    grads = [g.astype(BF16) for g in (_shard_cols(g_w_in), _shard_cols(g_w_uq), _shard_cols(g_w_ukv),
                                      g_w_out.reshape(4, D_MODEL // 4, D_MODEL))]
    pieces = [d_ada, g_norm_w, g_qln, g_kvln, g_qhn, g_khn]
    spans = [-(-p.shape[1] // LANES) * LANES for p in pieces]
    starts = [sum(spans[:j]) for j in range(len(spans))]
    small = jnp.concatenate([jnp.pad(p, ((0, 0), (0, sp - p.shape[1]))) for p, sp in zip(pieces, spans)], axis=1)
    own, sib, packs = _exchange_call(grads, small)

    names = ["adam_w_in", "adam_w_uq", "adam_w_ukv", "adam_w_out"]
    shard_w = [(w_in, m_w_in, v_w_in), (w_uq, m_w_uq, v_w_uq), (w_ukv, m_w_ukv, v_w_ukv),
               (w_out, m_w_out, v_w_out)]
    res = {}
    for name, o_g, s_g, (w, m, v) in zip(names, own, sib, shard_w):
        res[name] = _adam_shard_call(name, o_g, s_g, w[0], m[0], v[0])
    d_all = lax.dynamic_slice_in_dim(packs[:, 0, :], starts[0] + chip * ada_cols, ada_cols, axis=1)
    res_ada = _adam_ada_call(jnp.pad(c_all, ((0, 8), (0, 0))), jnp.pad(d_all, ((0, 8), (0, 0))),
                             w_ada[0], m_w_ada[0], v_w_ada[0])
    vectors = [(b_ada, m_b_ada, v_b_ada), (norm_w, m_norm_w, v_norm_w), (q_lora_norm, m_q_lora_norm, v_q_lora_norm),
               (kv_lora_norm, m_kv_lora_norm, v_kv_lora_norm), (q_head_norm, m_q_head_norm, v_q_head_norm),
               (k_head_norm, m_k_head_norm, v_k_head_norm)]
    vec_out = _adam_vectors_call(packs, starts, vectors)

    def ordered(kind):
        big = lambda name: res[name][kind][None]
        return [res_ada[kind][None], vec_out[kind][0], vec_out[kind][1], big("adam_w_in"), vec_out[kind][2],
                big("adam_w_uq"), vec_out[kind][3], big("adam_w_ukv"), vec_out[kind][4], vec_out[kind][5],
                big("adam_w_out")]

    return (loss, grad_x[None], *ordered(0), *ordered(1), *ordered(2), *ordered(3))


def _local_step(x2, ada, positions, tgt, norm_w, w_in_full, q_lora_norm, w_uq_full,
                kv_lora_norm, w_ukv_full, q_head_norm, k_head_norm, w_out_full):
    kr_block = _rope_to_head_lanes(w_in_full[:, 2688:2720])
    w_in_bf = jnp.concatenate([w_in_full[:, :2688], w_in_full[:, 2720:], kr_block], axis=1).astype(BF16)
    w_uq_bf = _to_head_lanes(w_uq_full.reshape(Q_LORA_RANK, MLA_HEADS, MLA_QK_DIM)).reshape(
        Q_LORA_RANK, MLA_PAD_WIDTH).astype(BF16)
    w_ukv_heads = w_ukv_full.reshape(KV_LORA_RANK, MLA_HEADS, 2 * MLA_NOPE_DIM)
    w_uk_bf = _nope_to_head_lanes(w_ukv_heads[:, :, :MLA_NOPE_DIM]).reshape(KV_LORA_RANK, MLA_PAD_WIDTH).astype(BF16)
    w_uv_bf = w_ukv_heads[:, :, MLA_NOPE_DIM:].reshape(KV_LORA_RANK, MLA_WIDTH).astype(BF16)
    w_out_bf = w_out_full.astype(BF16)
    qhn_pad, khn_pad = _to_head_lanes(q_head_norm), _to_head_lanes(k_head_norm)
    cos_t, sin_t = _rope_tables(positions)

    hb, q_sb, k_sb, v_sb, g_sb, c_q, c_kv, g_mla, k_rope = _pre_call(x2, ada, norm_w, w_in_bf)
    q_m, k_m, v_m, cqn, ckvn, q0, k0 = _mla_prep_call(
        c_q, c_kv, k_rope, cos_t, sin_t, q_lora_norm, kv_lora_norm, qhn_pad, khn_pad,
        w_uq_bf, w_uk_bf, w_uv_bf)
    o_sb, r_sb, kstart = _sb_fwd_call(q_sb, k_sb, v_sb)
    o_mla, lse = _mla_fwd_call(q_m, k_m, v_m)
    do_sb, do_mla, dg_sb, dg_mla, dy, g_w_out, d_gate, sq = _out_call(
        o_sb, g_sb, o_mla, g_mla, x2, tgt, ada, w_out_bf)

    dq_sb, dk_sb, dv_sb = _sb_bwd_call(kstart, q_sb, k_sb, v_sb, do_sb, r_sb)
    dq_m, dk_m, dv_m = _mla_bwd_call(q_m, k_m, v_m, do_mla, o_mla, lse)
    (d_cq, d_ckv, d_kr, g_wuq_pad, g_wuk_pad, g_wuv, g_qln, g_kvln, g_qhn, g_khn) = _mla_prep_bwd_call(
        dq_m, dk_m, dv_m, q0, k0, cqn, ckvn, c_q, c_kv, cos_t, sin_t,
        q_lora_norm, kv_lora_norm, qhn_pad, khn_pad, w_uq_bf, w_uk_bf, w_uv_bf)
    grad_x, g_win_pad, d_shift, d_scale, g_norm_w = _dh_call(
        [dq_sb, dk_sb, dv_sb, dg_sb, d_cq, d_ckv, dg_mla, d_kr], hb, x2, dy, ada, norm_w, w_in_bf)

    g_kr = g_win_pad[:, C_KR:]
    g_w_in = jnp.concatenate([g_win_pad[:, :2688], g_kr[:, :ROPE_HALF], g_kr[:, 64:64 + ROPE_HALF],
                              g_win_pad[:, 2688:3200]], axis=1)
    g_w_uq = _from_head_lanes(g_wuq_pad.reshape(Q_LORA_RANK, MLA_HEADS, HEAD_PAD)).reshape(Q_LORA_RANK, -1)
    g_w_ukv = jnp.concatenate(
        [_from_head_lanes(g_wuk_pad.reshape(KV_LORA_RANK, MLA_HEADS, HEAD_PAD))[:, :, :MLA_NOPE_DIM],
         g_wuv.reshape(KV_LORA_RANK, MLA_HEADS, MLA_NOPE_DIM)], axis=2).reshape(KV_LORA_RANK, -1)
    d_ada = jnp.concatenate([d_shift, d_scale, d_gate], axis=1)
    return (jnp.sum(sq), grad_x, g_w_in, g_w_uq, g_w_ukv, g_w_out, d_ada, g_norm_w, g_qln, g_kvln,
            _from_head_lanes(g_qhn), _from_head_lanes(g_khn))
```

```python
import functools
import math

import jax
import jax.numpy as jnp
from jax import lax
from jax.experimental import pallas as pl
from jax.experimental.pallas import tpu as pltpu

F32 = jnp.float32
BF16 = jnp.bfloat16
I32 = jnp.int32

D_MODEL = 1024
SB_HEADS = 8
SB_WIDTH = 512
MLA_HEADS = 8
MLA_QK_DIM = 96
MLA_NOPE_DIM = 64
MLA_ROPE_DIM = 32
MLA_WIDTH = 512
Q_LORA_RANK = 384
KV_LORA_RANK = 256
ROPE_THETA = 10000.0
EPS = 1e-6
LANES = 128
HEAD_PAD = 128
MLA_PAD_WIDTH = MLA_HEADS * HEAD_PAD

C_Q, C_K, C_V, C_G = 0, 512, 1024, 1536
C_CQ, C_CKV, C_GM, C_KR = 2048, 2432, 2688, 3200
IN_COLS_PAD = 3328

ADAM_LR = 0.001
ADAM_B1 = 0.9
ADAM_B2 = 0.999
ADAM_EPS = 1e-08
ADAM_WD = 0.01
ADAM_STEP = 10

SB_SCALE = 0.125
SB_GROUP = 4
MLA_SCALE = 1.0 / math.sqrt(MLA_QK_DIM)
LN2 = math.log(2.0)
MLA_SCALE_LOG2 = MLA_SCALE / LN2
MLA_BQ = 1024
MLA_BWD_BQ = 512
MLA_BK = 1024
MLA_BWD_BK = 512
SB_DEAD = -104.0
MASK_NEG = -1e30

VMEM_LIMIT = 56 * 1024 * 1024
MESH = pl.DeviceIdType.MESH


def _dot(a, b):
    return jnp.dot(a, b, preferred_element_type=F32)


def _dot_nt(a, b):
    return lax.dot_general(a, b, (((1,), (1,)), ((), ())), preferred_element_type=F32)


def _dot_tn(a, b):
    return lax.dot_general(a, b, (((0,), (0,)), ((), ())), preferred_element_type=F32)


def _sigmoid(x):
    return 1.0 / (1.0 + jnp.exp(-x))


def _split_dot(a, m):
    hi = a.astype(BF16)
    lo = (a - hi.astype(F32)).astype(BF16)
    return _dot(hi, m) + _dot(lo, m)


def _params(sem, vmem=None):
    return pltpu.CompilerParams(dimension_semantics=sem, vmem_limit_bytes=vmem)


def _row_tile(s, want):
    return min(want, s)


def _hbm_spec():
    return pl.BlockSpec(memory_space=pltpu.HBM)


def _allgather_rows_call(row):
    def body(in_ref, out_ref, send_sems, recv_sems, loc_sem):
        x, y, c = lax.axis_index("x"), lax.axis_index("y"), lax.axis_index("c")
        flips = [(fx, fy, fc) for fx in (0, 1) for fy in (0, 1) for fc in (0, 1)][1:]

        def peer(r):
            fx, fy, fc = flips[r]
            return ((1 - x) if fx else x, (1 - y) if fy else y, (1 - c) if fc else c)

        def copy(r, slot):
            return pltpu.make_async_remote_copy(
                src_ref=in_ref, dst_ref=out_ref.at[slot], send_sem=send_sems.at[r], recv_sem=recv_sems.at[r],
                device_id=peer(r), device_id_type=MESH)

        local = pltpu.make_async_copy(in_ref, out_ref.at[4 * x + 2 * y + c], loc_sem)
        local.start()
        sends = [copy(r, 4 * x + 2 * y + c) for r in range(7)]
        for cp in sends:
            cp.start()
        for r in range(7):
            px, py, pc = peer(r)
            copy(r, 4 * px + 2 * py + pc).wait_recv()
        for cp in sends:
            cp.wait_send()
        local.wait()

    return pl.pallas_call(
        body, name="gather_rows",
        out_shape=jax.ShapeDtypeStruct((8,) + row.shape, row.dtype),
        in_specs=[_hbm_spec()], out_specs=_hbm_spec(),
        scratch_shapes=[pltpu.SemaphoreType.DMA((7,)), pltpu.SemaphoreType.DMA((7,)), pltpu.SemaphoreType.DMA],
    )(row)


def _gather_call(shards, split):
    n = len(shards)
    halves = [s.shape[1] // 2 for s in shards]

    def body(*refs):
        ins, outs = refs[:n], refs[n:2 * n]
        ici_send, ici_recv, d2d_send, d2d_recv, loc_sems = refs[2 * n:]
        x, y, c = lax.axis_index("x"), lax.axis_index("y"), lax.axis_index("c")
        me = 2 * x + y
        peers = [(1 - x, y), (x, 1 - y), (1 - x, 1 - y)]

        def rows(a, which):
            return pl.ds(pl.multiple_of(which * halves[a], 16), halves[a])

        def ici(a, j, slot):
            px, py = peers[j]
            src, dst = ins[a].at[0], outs[a].at[slot]
            if split[a]:
                src, dst = src.at[rows(a, c)], dst.at[rows(a, c)]
            return pltpu.make_async_remote_copy(
                src_ref=src, dst_ref=dst,
                send_sem=ici_send.at[3 * a + j], recv_sem=ici_recv.at[3 * a + j],
                device_id=(px, py, c), device_id_type=MESH)

        def d2d(a, j, which):
            px, py = peers[j]
            piece = outs[a].at[2 * px + py, rows(a, which)]
            return pltpu.make_async_remote_copy(
                src_ref=piece, dst_ref=piece,
                send_sem=d2d_send.at[3 * a + j], recv_sem=d2d_recv.at[3 * a + j],
                device_id=(x, y, 1 - c), device_id_type=MESH)

        local = [pltpu.make_async_copy(ins[a].at[0], outs[a].at[me], loc_sems.at[a]) for a in range(n)]
        for cp in local:
            cp.start()
        sends = [ici(a, j, me) for a in range(n) for j in range(3)]
        for cp in sends:
            cp.start()
        for a in range(n):
            for j in range(3):
                px, py = peers[j]
                ici(a, j, 2 * px + py).wait_recv()
                if split[a]:
                    cp = d2d(a, j, c)
                    cp.start()
                    sends.append(cp)
        for a in range(n):
            for j in range(3):
                if split[a]:
                    d2d(a, j, 1 - c).wait_recv()
        for cp in sends:
            cp.wait_send()
        for cp in local:
            cp.wait()

    return pl.pallas_call(
        body, name="gather_weights",
        out_shape=[jax.ShapeDtypeStruct((4,) + s.shape[1:], s.dtype) for s in shards],
        in_specs=[_hbm_spec() for _ in shards],
        out_specs=[_hbm_spec() for _ in shards],
        scratch_shapes=[pltpu.SemaphoreType.DMA((3 * n,)), pltpu.SemaphoreType.DMA((3 * n,)),
                        pltpu.SemaphoreType.DMA((3 * n,)), pltpu.SemaphoreType.DMA((3 * n,)),
                        pltpu.SemaphoreType.DMA((n,))],
    )(*shards)


def _exchange_call(grads, small):
    n = len(grads)

    def body(*refs):
        g_in, small_in = refs[:n], refs[n]
        own, sib, packs = refs[n + 1:2 * n + 1], refs[2 * n + 1:3 * n + 1], refs[3 * n + 1]
        ici_send, ici_recv, d2d_send, d2d_recv, sm_send, sm_recv, loc_sems = refs[3 * n + 2:]
        x, y, c = lax.axis_index("x"), lax.axis_index("y"), lax.axis_index("c")
        me = 2 * x + y
        me8 = 4 * x + 2 * y + c
        sibling = (x, y, 1 - c)
        peers = [(1 - x, y), (x, 1 - y), (1 - x, 1 - y)]
        flips = [(fx, fy, fc) for fx in (0, 1) for fy in (0, 1) for fc in (0, 1)][1:]

        def ici(a, j, src_slot, dst_slot):
            px, py = peers[j]
            return pltpu.make_async_remote_copy(
                src_ref=g_in[a].at[src_slot], dst_ref=own[a].at[dst_slot],
                send_sem=ici_send.at[3 * a + j], recv_sem=ici_recv.at[3 * a + j],
                device_id=(px, py, c), device_id_type=MESH)

        def d2d(a, rel, chip, src):
            return pltpu.make_async_remote_copy(
                src_ref=src, dst_ref=sib[a].at[chip],
                send_sem=d2d_send.at[4 * a + rel], recv_sem=d2d_recv.at[4 * a + rel],
                device_id=sibling, device_id_type=MESH)

        def flipped(r):
            fx, fy, fc = flips[r]
            return ((1 - x) if fx else x, (1 - y) if fy else y, (1 - c) if fc else c)

        def sm(r, slot):
            return pltpu.make_async_remote_copy(
                src_ref=small_in, dst_ref=packs.at[slot],
                send_sem=sm_send.at[r], recv_sem=sm_recv.at[r],
                device_id=flipped(r), device_id_type=MESH)

        def peer8(r):
            px, py, pc = flipped(r)
            return 4 * px + 2 * py + pc

        local = [pltpu.make_async_copy(g_in[a].at[me], own[a].at[me], loc_sems.at[a]) for a in range(n)]
        local.append(pltpu.make_async_copy(small_in, packs.at[me8], loc_sems.at[n]))
        for cp in local:
            cp.start()
        sends = []
        for r in range(7):
            sends.append(sm(r, me8))
        for a in range(n):
            for j in range(3):
                px, py = peers[j]
                sends.append(ici(a, j, 2 * px + py, me))
        for cp in sends:
            cp.start()
        for a in range(n):
            cp = d2d(a, 0, me, g_in[a].at[me])
            cp.start()
            sends.append(cp)
        for a in range(n):
            for j in range(3):
                px, py = peers[j]
                ici(a, j, me, 2 * px + py).wait_recv()
                cp = d2d(a, 1 + j, 2 * px + py, own[a].at[2 * px + py])
                cp.start()
                sends.append(cp)
        for a in range(n):
            d2d(a, 0, me, g_in[a].at[me]).wait_recv()
            for j in range(3):
                px, py = peers[j]
                d2d(a, 1 + j, 2 * px + py, g_in[a].at[me]).wait_recv()
        for r in range(7):
            sm(r, peer8(r)).wait_recv()
        for cp in sends:
            cp.wait_send()
        for cp in local:
            cp.wait()

    out_shape = ([jax.ShapeDtypeStruct(g.shape, g.dtype) for g in grads] * 2
                 + [jax.ShapeDtypeStruct((8,) + small.shape, small.dtype)])
    res = pl.pallas_call(
        body, name="exchange_grads",
        out_shape=out_shape,
        in_specs=[_hbm_spec() for _ in range(n + 1)],
        out_specs=[_hbm_spec() for _ in range(2 * n + 1)],
        scratch_shapes=[pltpu.SemaphoreType.DMA((3 * n,)), pltpu.SemaphoreType.DMA((3 * n,)),
                        pltpu.SemaphoreType.DMA((4 * n,)), pltpu.SemaphoreType.DMA((4 * n,)),
                        pltpu.SemaphoreType.DMA((7,)), pltpu.SemaphoreType.DMA((7,)),
                        pltpu.SemaphoreType.DMA((n + 1,))],
    )(*grads, small)
    return res[:n], res[n:2 * n], res[2 * n]


def _ada_call(c_all, w_ada_cols, b_ada_cols):
    def body(c_ref, w_ref, b_ref, o_ref):
        cc = c_ref[...]
        o_ref[...] = _dot((cc * _sigmoid(cc)).astype(BF16), w_ref[...].astype(BF16)) + b_ref[...]

    return pl.pallas_call(
        body, name="ada_fwd",
        out_shape=jax.ShapeDtypeStruct((c_all.shape[0], w_ada_cols.shape[1]), F32),
        compiler_params=pltpu.CompilerParams(vmem_limit_bytes=VMEM_LIMIT),
    )(c_all, w_ada_cols, b_ada_cols)


def _ada_part(j):
    return pl.BlockSpec((1, D_MODEL), lambda i: (0, j))


def _full(shape):
    return pl.BlockSpec(shape, lambda i: (0,) * len(shape))


def _rows(tm, width):
    return pl.BlockSpec((tm, width), lambda i: (i, 0))


def _pre_call(x, ada, norm_w, w_in_bf):
    s = x.shape[0]
    tm = _row_tile(s, 512)
    groups = [(C_Q, 512, BF16), (C_K, 512, BF16), (C_V, 512, BF16), (C_G, 512, F32),
              (C_CQ, Q_LORA_RANK, F32), (C_CKV, KV_LORA_RANK, F32), (C_GM, 512, F32), (C_KR, LANES, F32)]

    def body(x_ref, sh_ref, sc_ref, nw_ref, w_ref, hb_ref, *outs):
        xx = x_ref[...]
        r0 = lax.rsqrt(jnp.mean(xx * xx, axis=-1, keepdims=True) + EPS)
        h = (xx * r0 * nw_ref[...]) * (1.0 + sc_ref[...]) + sh_ref[...]
        hb = h.astype(BF16)
        hb_ref[...] = hb
        for (c0, width, dt), o_ref in zip(groups, outs):
            o_ref[...] = _dot(hb, w_ref[:, c0:c0 + width]).astype(dt)

    return pl.pallas_call(
        body, name="pre_proj",
        grid=(s // tm,),
        in_specs=[_rows(tm, D_MODEL), _ada_part(0), _ada_part(1), _full((1, D_MODEL)),
                  _full((D_MODEL, IN_COLS_PAD))],
        out_specs=[_rows(tm, D_MODEL)] + [_rows(tm, w) for _, w, _ in groups],
        out_shape=[jax.ShapeDtypeStruct((s, D_MODEL), BF16)]
        + [jax.ShapeDtypeStruct((s, w), dt) for _, w, dt in groups],
        compiler_params=_params(("arbitrary",), VMEM_LIMIT),
    )(x, ada, ada, norm_w, w_in_bf)


def _rope(t, cos_t, sin_t):
    return t * cos_t + pltpu.roll(t, 64, 1) * sin_t


def _rope_adjoint(d, cos_t, sin_t):
    return d * cos_t + pltpu.roll(d * sin_t, 64, 1)


def _mla_prep_call(c_q, c_kv, k_rope, cos_t, sin_t, q_lora_norm, kv_lora_norm, qhn_pad, khn_pad,
                   w_uq_bf, w_uk_bf, w_uv_bf):
    s = c_q.shape[0]
    tm = _row_tile(s, 256)

    def body(cq_ref, ckv_ref, kr_ref, cos_ref, sin_ref, qln_ref, kvln_ref, qhn_ref, khn_ref,
             wuq_ref, wuk_ref, wuv_ref, q_ref, k_ref, v_ref, cqn_ref, ckvn_ref, q0_ref, k0_ref):
        cq = cq_ref[...]
        cqn = (cq * lax.rsqrt(jnp.mean(cq * cq, axis=-1, keepdims=True) + EPS) * qln_ref[...]).astype(BF16)
        cqn_ref[...] = cqn
        ckv = ckv_ref[...]
        ckvn = (ckv * lax.rsqrt(jnp.mean(ckv * ckv, axis=-1, keepdims=True) + EPS) * kvln_ref[...]).astype(BF16)
        ckvn_ref[...] = ckvn
        v_ref[...] = _dot(ckvn, wuv_ref[...]).astype(BF16)
        q0_ref[...] = _dot(cqn, wuq_ref[...])
        k0_ref[...] = _dot(ckvn, wuk_ref[...])
        cos_t, sin_t = cos_ref[...], sin_ref[...]
        kr = kr_ref[...]
        heads = [slice(h * HEAD_PAD, (h + 1) * HEAD_PAD) for h in range(MLA_HEADS)]
        for cols in heads:
            k0_ref[:, cols] = k0_ref[:, cols] + kr

        def inv_rms(ref):
            sums = [jnp.sum(ref[:, cols] * ref[:, cols], axis=-1, keepdims=True) for cols in heads]
            return [lax.rsqrt(t * (1.0 / MLA_QK_DIM) + EPS) for t in sums]

        rqs, rks = inv_rms(q0_ref), inv_rms(k0_ref)
        for cols, rq, rk in zip(heads, rqs, rks):
            q_ref[:, cols] = (_rope(q0_ref[:, cols] * rq * qhn_ref[...], cos_t, sin_t) * MLA_SCALE_LOG2).astype(BF16)
            k_ref[:, cols] = _rope(k0_ref[:, cols] * rk * khn_ref[...], cos_t, sin_t).astype(BF16)

    return pl.pallas_call(
        body, name="mla_prep",
        grid=(s // tm,),
        in_specs=[_rows(tm, Q_LORA_RANK), _rows(tm, KV_LORA_RANK), _rows(tm, LANES),
                  _rows(tm, LANES), _rows(tm, LANES),
                  _full((1, Q_LORA_RANK)), _full((1, KV_LORA_RANK)), _full((1, LANES)), _full((1, LANES)),
                  _full((Q_LORA_RANK, MLA_PAD_WIDTH)), _full((KV_LORA_RANK, MLA_PAD_WIDTH)),
                  _full((KV_LORA_RANK, MLA_WIDTH))],
        out_specs=[_rows(tm, MLA_PAD_WIDTH), _rows(tm, MLA_PAD_WIDTH), _rows(tm, MLA_WIDTH),
                   _rows(tm, Q_LORA_RANK), _rows(tm, KV_LORA_RANK),
                   _rows(tm, MLA_PAD_WIDTH), _rows(tm, MLA_PAD_WIDTH)],
        out_shape=[jax.ShapeDtypeStruct((s, MLA_PAD_WIDTH), BF16), jax.ShapeDtypeStruct((s, MLA_PAD_WIDTH), BF16),
                   jax.ShapeDtypeStruct((s, MLA_WIDTH), BF16),
                   jax.ShapeDtypeStruct((s, Q_LORA_RANK), BF16), jax.ShapeDtypeStruct((s, KV_LORA_RANK), BF16),
                   jax.ShapeDtypeStruct((s, MLA_PAD_WIDTH), F32), jax.ShapeDtypeStruct((s, MLA_PAD_WIDTH), F32)],
        compiler_params=_params(("arbitrary",), VMEM_LIMIT),
    )(c_q, c_kv, k_rope, cos_t, sin_t, q_lora_norm, kv_lora_norm, qhn_pad, khn_pad,
      w_uq_bf, w_uk_bf, w_uv_bf)


def _log_sigmoid_pair(z):
    ls = jnp.minimum(z, 0.0) - jnp.log(1.0 + jnp.exp(-jnp.abs(z)))
    return ls, ls - z


def _pair(hh):
    return slice((hh // 2) * LANES, (hh // 2 + 1) * LANES)


def _sb_fwd_call(q, k, v):
    s = q.shape[0]
    bq = _row_tile(s, 256)
    nq = s // bq
    nh = SB_GROUP
    width = nh * 64

    def body(q_ref, k_ref, v_ref, o_ref, r_ref, ks_ref):
        hp, i = pl.program_id(0), pl.program_id(1)
        lane = lax.broadcasted_iota(I32, (bq, LANES), 1)
        row = lax.broadcasted_iota(I32, (bq, bq), 0)
        col = lax.broadcasted_iota(I32, (bq, bq), 1)
        strict = col < row
        later = jnp.where(row > col, 1.0, 0.0).astype(BF16)
        masks = [_head_mask(lane, hh).astype(BF16) for hh in range(2)]
        qms = [q_ref[:, _pair(hh)] * jnp.asarray(SB_SCALE, BF16) * masks[hh % 2] for hh in range(nh)]

        def walk(blocks, state):
            chains = [(kb, diagonal, hh) for kb, diagonal in blocks for hh in range(nh)]
            keys = lambda kb: pl.ds(pl.multiple_of(kb * bq, bq), bq)
            zs = [_dot_nt(qms[hh], k_ref[keys(kb), _pair(hh)]) for kb, _, hh in chains]
            pairs = []
            for z, (_, diagonal, _) in zip(zs, chains):
                ls, lk = _log_sigmoid_pair(z)
                pairs.append((ls, jnp.where(strict, lk, 0.0) if diagonal else lk))
            sums = [_split_dot(lk, later) for _, lk in pairs]
            runs = [st[0] for st in state]
            ws = []
            for (ls, lk), after, (_, diagonal, hh) in zip(pairs, sums, chains):
                w = jnp.exp(ls + (after + runs[hh]))
                ws.append((jnp.where(strict, w, 0.0) if diagonal else w).astype(BF16))
                runs[hh] = runs[hh] + jnp.sum(lk, axis=1, keepdims=True)
            accs = [st[1] for st in state]
            for w, (kb, _, hh) in zip(ws, chains):
                accs[hh] = accs[hh] + _dot(w, v_ref[keys(kb), _pair(hh)])
            return tuple(zip(runs, accs))

        def alive(state):
            top = jnp.max(state[0][0])
            for st in state[1:]:
                top = jnp.maximum(top, jnp.max(st[0]))
            return (top > SB_DEAD).astype(I32)

        def finish(state, first):
            ks_ref[hp, i] = first
            for pair in range(nh // 2):
                o_ref[:, _pair(2 * pair)] = jnp.where(lane < 64, state[2 * pair][1], state[2 * pair + 1][1])
                r_ref[:, _pair(2 * pair)] = jnp.where(lane < 64, state[2 * pair][0], state[2 * pair + 1][0])

        zero = ((jnp.zeros((bq, 1), F32), jnp.zeros((bq, LANES), F32)),) * nh

        @pl.when(i == 0)
        def _():
            finish(walk([(0, True)], zero), 0)

        @pl.when(i > 0)
        def _():
            state = walk([(i, True), (i - 1, False)], zero)

            def cond(carry):
                return jnp.logical_and(carry[0] >= 0, carry[1] > 0)

            def step(carry):
                state = walk([(carry[0], False)], carry[2])
                return carry[0] - 1, alive(state), state

            kb, _, state = lax.while_loop(cond, step, (i - 2, alive(state), state))
            finish(state, kb + 1)

    return pl.pallas_call(
        body, name="sb_fwd",
        grid=(SB_HEADS // nh, nq),
        in_specs=[pl.BlockSpec((bq, width), lambda h, i: (i, h)),
                  pl.BlockSpec((s, width), lambda h, i: (0, h)),
                  pl.BlockSpec((s, width), lambda h, i: (0, h))],
        out_specs=[pl.BlockSpec((bq, width), lambda h, i: (i, h)),
                   pl.BlockSpec((bq, width), lambda h, i: (i, h)),
                   pl.BlockSpec(memory_space=pltpu.SMEM)],
        out_shape=[jax.ShapeDtypeStruct((s, SB_WIDTH), F32), jax.ShapeDtypeStruct((s, SB_WIDTH), F32),
                   jax.ShapeDtypeStruct((SB_HEADS // nh, nq), I32)],
        compiler_params=_params(("arbitrary", "arbitrary"), VMEM_LIMIT),
    )(q, k, v)


def _mla_fwd_call(q, k, v):
    s = q.shape[0]
    bq = _row_tile(s, MLA_BQ)
    bk = _row_tile(s, MLA_BK)
    nq = s // bq
    assert bk % bq == 0
    half = bk // 2

    def body(q_ref, k_ref, v_ref, o_ref, lse_ref, p_ref, s_ref):
        i = pl.program_id(1)
        lane = lax.broadcasted_iota(I32, (bq, LANES), 1)
        row = lax.broadcasted_iota(I32, (half, bq), 1)
        col = lax.broadcasted_iota(I32, (half, bq), 0)
        n_full = (i * bq) // bk

        def keys(g):
            return pl.ds(pl.multiple_of(g * half, half), half)

        def join(left, right, qlo):
            return right if qlo == 0 else jnp.concatenate([left[:, :qlo], right], axis=1)

        def put_scores(g, slot, qlo=0):
            for hh in range(2):
                cols = slice(hh * HEAD_PAD, (hh + 1) * HEAD_PAD)
                s_ref[slot, hh, :, qlo:] = _dot_nt(k_ref[keys(g), cols], q_ref[qlo:, cols])

        def add_pv(carry, g, slot, qlo=0):
            vblk = v_ref[keys(g), :]
            out = []
            for hh, (m, l, acc, alpha) in enumerate(carry):
                upd = alpha[:, qlo:] * acc[:, qlo:] + _dot_tn(vblk, p_ref[slot, hh, :, qlo:])
                out.append((m, l, join(acc, upd, qlo), alpha))
            return tuple(out)

        def substep(g, slot, carry, masked, prefetch, qlo=0, next_qlo=0):
            if prefetch:
                put_scores(g + 1, 1 - slot, next_qlo)
            carry = add_pv(carry, jnp.maximum(g - 1, 0), 1 - slot)
            new = []
            for hh in range(2):
                m, l, acc, _ = carry[hh]
                sc = s_ref[slot, hh, :, qlo:]
                if masked:
                    sc = jnp.where(col[:, qlo:] + g * half <= row[:, qlo:] + i * bq, sc, MASK_NEG)
                m_new = jnp.maximum(m[:, qlo:], jnp.max(sc, axis=0, keepdims=True))
                p = jnp.exp2(sc - m_new)
                alpha = jnp.exp2(m[:, qlo:] - m_new)
                l_new = alpha * l[:, qlo:] + jnp.sum(p, axis=0, keepdims=True)
                p_ref[slot, hh, :, qlo:] = p.astype(BF16)
                new.append((join(m, m_new, qlo), join(l, l_new, qlo), acc, join(jnp.ones_like(m), alpha, qlo)))
            return tuple(new)

        skip = half if bk == bq else 0

        def chunk(kb, carry, masked):
            qlo = skip if masked else 0
            carry = substep(2 * kb, 0, carry, masked, True, 0, qlo)
            return substep(2 * kb + 1, 1, carry, masked, not masked, qlo, 0)

        p_ref[1] = jnp.zeros_like(p_ref[1])
        put_scores(0, 0)
        one = (jnp.full((1, bq), MASK_NEG, F32), jnp.zeros((1, bq), F32), jnp.zeros((LANES, bq), F32),
               jnp.ones((1, bq), F32))
        carry = lax.fori_loop(0, n_full, lambda kb, cr: chunk(kb, cr, False), (one, one))
        carry = chunk(n_full, carry, True)
        (m0, l0, a0, _), (m1, l1, a1, _) = add_pv(carry, 2 * n_full + 1, 1, skip)
        o_ref[...] = jnp.where(lane < 64, (a0 / l0).T, (a1 / l1).T)
        sub = lax.broadcasted_iota(I32, (8, bq), 0)
        lse_ref[...] = jnp.where(sub == 0, m0 + jnp.log2(l0), jnp.where(sub == 1, m1 + jnp.log2(l1), 0.0))

    return pl.pallas_call(
        body, name="mla_fwd",
        grid=(4, nq),
        in_specs=[pl.BlockSpec((bq, 2 * HEAD_PAD), lambda h, i: (i, h)),
                  pl.BlockSpec((s, 2 * HEAD_PAD), lambda h, i: (0, h)),
                  pl.BlockSpec((s, LANES), lambda h, i: (0, h))],
        out_specs=[pl.BlockSpec((bq, LANES), lambda h, i: (i, h)),
                   pl.BlockSpec((None, 8, bq), lambda h, i: (h, 0, i))],
        out_shape=[jax.ShapeDtypeStruct((s, MLA_WIDTH), F32), jax.ShapeDtypeStruct((4, 8, s), F32)],
        scratch_shapes=[pltpu.VMEM((2, 2, half, bq), BF16), pltpu.VMEM((2, 2, half, bq), F32)],
        compiler_params=_params(("arbitrary", "arbitrary"), VMEM_LIMIT),
    )(q, k, v)


def _out_call(o_sb, g_sb, o_mla, g_mla, x, target, ada, w_out_bf):
    s = x.shape[0]
    tm = _row_tile(s, 256)

    def body(osb_ref, gsb_ref, oml_ref, gml_ref, x_ref, t_ref, gate_ref, w_ref,
             dosb_ref, doml_ref, dgsb_ref, dgml_ref, dy_ref, gw_ref, dgate_ref, sq_ref):
        @pl.when(pl.program_id(0) == 0)
        def _():
            gw_ref[...] = jnp.zeros_like(gw_ref)
            dgate_ref[...] = jnp.zeros_like(dgate_ref)
            sq_ref[...] = jnp.zeros_like(sq_ref)

        g_s, g_m = gsb_ref[...], gml_ref[...]
        sig_s, sig_m = _sigmoid(g_s), _sigmoid(g_m)
        silu_s, silu_m = g_s * sig_s, g_m * sig_m
        o_s, o_m = osb_ref[...], oml_ref[...]
        mixed = jnp.concatenate([o_s * silu_s, o_m * silu_m], axis=1).astype(BF16)
        u = _dot(mixed, w_ref[...])
        gate_v = gate_ref[...]
        err = x_ref[...] + gate_v * u - t_ref[...]
        sq_ref[...] += jnp.sum(err * err, axis=0, keepdims=True)
        dy = err * (1.0 / D_MODEL)
        dy_ref[...] = dy
        dgate_ref[...] += jnp.sum(dy * u, axis=0, keepdims=True)
        du = (dy * gate_v).astype(BF16)
        gw_ref[...] += _dot_tn(mixed, du)
        dmix = _dot_nt(du, w_ref[...])
        dm_s, dm_m = dmix[:, :SB_WIDTH], dmix[:, SB_WIDTH:]
        dosb_ref[...] = (dm_s * silu_s).astype(BF16)
        doml_ref[...] = (dm_m * silu_m).astype(BF16)
        dgsb_ref[...] = (dm_s * o_s * (sig_s * (1.0 + g_s * (1.0 - sig_s)))).astype(BF16)
        dgml_ref[...] = (dm_m * o_m * (sig_m * (1.0 + g_m * (1.0 - sig_m)))).astype(BF16)

    return pl.pallas_call(
        body, name="out_proj_loss",
        grid=(s // tm,),
        in_specs=[_rows(tm, 512), _rows(tm, 512), _rows(tm, 512), _rows(tm, 512),
                  _rows(tm, D_MODEL), _rows(tm, D_MODEL), _ada_part(2), _full((D_MODEL, D_MODEL))],
        out_specs=[_rows(tm, 512), _rows(tm, 512), _rows(tm, 512), _rows(tm, 512), _rows(tm, D_MODEL),
                   _full((D_MODEL, D_MODEL)), _full((1, D_MODEL)), _full((1, D_MODEL))],
        out_shape=[jax.ShapeDtypeStruct((s, 512), BF16)] * 4
        + [jax.ShapeDtypeStruct((s, D_MODEL), F32), jax.ShapeDtypeStruct((D_MODEL, D_MODEL), F32),
           jax.ShapeDtypeStruct((1, D_MODEL), F32), jax.ShapeDtypeStruct((1, D_MODEL), F32)],
        compiler_params=_params(("arbitrary",), VMEM_LIMIT),
    )(o_sb, g_sb, o_mla, g_mla, x, target, ada, w_out_bf)


def _head_mask(lane, hh):
    return jnp.where((lane >= 64) if hh else (lane < 64), 1.0, 0.0)


def _pick_lane(packed, lane, which):
    return jnp.sum(jnp.where(lane == which, packed, 0.0), axis=1, keepdims=True)


def _sb_bwd_call(kstart, q, k, v, do, rfin):
    s = q.shape[0]
    bq = _row_tile(s, 256)
    nq = s // bq
    nh = SB_GROUP
    width = nh * 64

    def body(ks_ref, q_ref, k_ref, v_ref, do_ref, r_ref, dq_ref, dk_ref, dv_ref):
        hp, i = pl.program_id(0), pl.program_id(1)

        @pl.when(i == 0)
        def _():
            dk_ref[...] = jnp.zeros_like(dk_ref)
            dv_ref[...] = jnp.zeros_like(dv_ref)

        lane = lax.broadcasted_iota(I32, (bq, LANES), 1)
        row = lax.broadcasted_iota(I32, (bq, bq), 0)
        col = lax.broadcasted_iota(I32, (bq, bq), 1)
        upto = jnp.where(row <= col, 1.0, 0.0).astype(BF16)
        before = jnp.where(row < col, 1.0, 0.0).astype(BF16)
        masks = [_head_mask(lane, hh).astype(BF16) for hh in range(2)]
        qms = [q_ref[:, _pair(hh)] * jnp.asarray(SB_SCALE, BF16) * masks[hh % 2] for hh in range(nh)]
        doms = [do_ref[:, _pair(hh)] * masks[hh % 2] for hh in range(nh)]
        totals = [_pick_lane(r_ref[:, _pair(hh)], lane, 64 * (hh % 2)) for hh in range(nh)]
        strict = col < row

        def walk(blocks, state):
            chains = [(kb, diagonal, hh) for kb, diagonal in blocks for hh in range(nh)]
            keys = lambda kb: pl.ds(pl.multiple_of(kb * bq, bq), bq)
            cut = lambda x, diagonal: jnp.where(strict, x, 0.0) if diagonal else x
            zs = [_dot_nt(qms[hh], k_ref[keys(kb), _pair(hh)]) for kb, _, hh in chains]
            dws = [_dot_nt(doms[hh], v_ref[keys(kb), _pair(hh)]) for kb, _, hh in chains]
            pairs = []
            for z, (_, diagonal, _) in zip(zs, chains):
                ls, lk = _log_sigmoid_pair(z)
                pairs.append((ls, cut(lk, diagonal)))
            incls = [_split_dot(lk, upto) for _, lk in pairs]
            pres = [st[0] for st in state]
            ws, gs = [], []
            for (ls, lk), incl, dw, (_, diagonal, hh) in zip(pairs, incls, dws, chains):
                w = cut(jnp.exp(ls + ((totals[hh] - pres[hh]) - incl)), diagonal)
                ws.append(w.astype(BF16))
                gs.append(w * dw)
                pres[hh] = pres[hh] + jnp.sum(lk, axis=1, keepdims=True)
            gsums = [_dot(g.astype(BF16), before) for g in gs]
            gpres = [st[1] for st in state]
            dzs = []
            for (ls, _), g, gsum, (_, diagonal, hh) in zip(pairs, gs, gsums, chains):
                dzs.append(cut(g - jnp.exp(ls) * (g + (gpres[hh] + gsum)), diagonal).astype(BF16))
                gpres[hh] = gpres[hh] + jnp.sum(g, axis=1, keepdims=True)
            dqs = [st[2] for st in state]
            dk_parts, dv_parts = [], []
            for dzb, w, (kb, _, hh) in zip(dzs, ws, chains):
                dk_parts.append(_dot_tn(dzb, qms[hh]))
                dv_parts.append(_dot_tn(w, doms[hh]))
                dqs[hh] = dqs[hh] + _dot(dzb, k_ref[keys(kb), _pair(hh)])
            for b, (kb, _) in enumerate(blocks):
                for pair in range(nh // 2):
                    c0 = b * nh + 2 * pair
                    dk_ref[keys(kb), _pair(2 * pair)] += dk_parts[c0] + dk_parts[c0 + 1]
                    dv_ref[keys(kb), _pair(2 * pair)] += dv_parts[c0] + dv_parts[c0 + 1]
            return tuple(zip(pres, gpres, dqs))

        def finish(state):
            for pair in range(nh // 2):
                both = jnp.where(lane < 64, state[2 * pair][2], state[2 * pair + 1][2])
                dq_ref[:, _pair(2 * pair)] = (both * SB_SCALE).astype(BF16)

        zero = ((jnp.zeros((bq, 1), F32), jnp.zeros((bq, 1), F32), jnp.zeros((bq, LANES), F32)),) * nh

        @pl.when(i == 0)
        def _():
            finish(walk([(0, True)], zero))

        @pl.when(i > 0)
        def _():
            state = lax.fori_loop(ks_ref[hp, i], i - 1, lambda kb, st: walk([(kb, False)], st), zero)
            finish(walk([(i - 1, False), (i, True)], state))

    return pl.pallas_call(
        body, name="sb_bwd",
        grid_spec=pltpu.PrefetchScalarGridSpec(
            num_scalar_prefetch=1, grid=(SB_HEADS // nh, nq),
            in_specs=[pl.BlockSpec((bq, width), lambda h, i, ks: (i, h)),
                      pl.BlockSpec((s, width), lambda h, i, ks: (0, h), pipeline_mode=pl.Buffered(1)),
                      pl.BlockSpec((s, width), lambda h, i, ks: (0, h), pipeline_mode=pl.Buffered(1)),
                      pl.BlockSpec((bq, width), lambda h, i, ks: (i, h)),
                      pl.BlockSpec((bq, width), lambda h, i, ks: (i, h))],
            out_specs=[pl.BlockSpec((bq, width), lambda h, i, ks: (i, h)),
                       pl.BlockSpec((s, width), lambda h, i, ks: (0, h), pipeline_mode=pl.Buffered(1)),
                       pl.BlockSpec((s, width), lambda h, i, ks: (0, h), pipeline_mode=pl.Buffered(1))]),
        out_shape=[jax.ShapeDtypeStruct((s, SB_WIDTH), BF16), jax.ShapeDtypeStruct((s, SB_WIDTH), F32),
                   jax.ShapeDtypeStruct((s, SB_WIDTH), F32)],
        compiler_params=_params(("arbitrary", "arbitrary"), VMEM_LIMIT),
    )(kstart, q, k, v, do, rfin)


def _mla_bwd_call(q, k, v, do, o, lse):
    s = q.shape[0]
    bq = _row_tile(s, MLA_BWD_BQ)
    bk = _row_tile(s, MLA_BWD_BK)
    nq = s // bq
    assert bk % bq == 0
    half = bk // 2

    def body(q_ref, k_ref, v_ref, do_ref, o_ref, lse_ref, dq_ref, dk_ref, dv_ref, dom_ref, s_ref, dp_ref, pb_ref,
             ds_ref):
        i = pl.program_id(1)

        @pl.when(i == 0)
        def _():
            dk_ref[...] = jnp.zeros_like(dk_ref)
            dv_ref[...] = jnp.zeros_like(dv_ref)

        lane = lax.broadcasted_iota(I32, (bq, LANES), 1)
        row = lax.broadcasted_iota(I32, (half, bq), 1)
        col = lax.broadcasted_iota(I32, (half, bq), 0)
        n_full = (i * bq) // bk
        do2 = do_ref[...]
        prod = do2.astype(F32) * o_ref[...]
        ones = jnp.ones((8, LANES), BF16)
        deltas, lses = [], []
        for hh in range(2):
            head = _head_mask(lane, hh)
            dom_ref[hh] = do2 * head.astype(BF16)
            part = prod * head
            hi = part.astype(BF16)
            lo = (part - hi.astype(F32)).astype(BF16)
            deltas.append((_dot_nt(ones, hi) + _dot_nt(ones, lo))[0:1])
            lses.append(lse_ref[hh:hh + 1, :])

        def keys(g):
            return pl.ds(pl.multiple_of(g * half, half), half)

        def heads():
            return [(hh, slice(hh * HEAD_PAD, (hh + 1) * HEAD_PAD)) for hh in range(2)]

        def put_products(g, slot, qlo=0):
            vblk = v_ref[keys(g), :]
            for hh, cols in heads():
                s_ref[slot, hh, :, qlo:] = _dot_nt(k_ref[keys(g), cols], q_ref[qlo:, cols])
                dp_ref[slot, hh, :, qlo:] = _dot_nt(vblk, dom_ref[hh, qlo:, :])

        def add_grads(dqs, g, slot, qlo=0):
            rows = keys(g)
            new, dv_parts = [], []
            for hh, cols in heads():
                ds = ds_ref[slot, hh, :, qlo:]
                dk_ref[rows, cols] += _dot(ds, q_ref[qlo:, cols])
                dv_parts.append(_dot(pb_ref[slot, hh, :, qlo:], dom_ref[hh, qlo:, :]))
                upd = dqs[hh][:, qlo:] + _dot_tn(k_ref[rows, cols], ds)
                new.append(upd if qlo == 0 else jnp.concatenate([dqs[hh][:, :qlo], upd], axis=1))
            dv_ref[rows, :] += dv_parts[0] + dv_parts[1]
            return tuple(new)

        def substep(g, slot, dqs, masked, prefetch, qlo=0, next_qlo=0):
            if prefetch:
                put_products(g + 1, 1 - slot, next_qlo)
            dqs = add_grads(dqs, jnp.maximum(g - 1, 0), 1 - slot)
            for hh, _ in heads():
                p = jnp.exp2(s_ref[slot, hh, :, qlo:] - lses[hh][:, qlo:])
                if masked:
                    p = jnp.where(col[:, qlo:] + g * half <= row[:, qlo:] + i * bq, p, 0.0)
                ds_ref[slot, hh, :, qlo:] = (p * (dp_ref[slot, hh, :, qlo:] - deltas[hh][:, qlo:])).astype(BF16)
                pb_ref[slot, hh, :, qlo:] = p.astype(BF16)
            return dqs

        skip = half if bk == bq else 0

        def chunk(kb, dqs, masked):
            qlo = skip if masked else 0
            dqs = substep(2 * kb, 0, dqs, masked, True, 0, qlo)
            return substep(2 * kb + 1, 1, dqs, masked, not masked, qlo, 0)

        ds_ref[1] = jnp.zeros_like(ds_ref[1])
        pb_ref[1] = jnp.zeros_like(pb_ref[1])
        put_products(0, 0)
        zero = jnp.zeros((HEAD_PAD, bq), F32)
        dqs = lax.fori_loop(0, n_full, lambda kb, dqs: chunk(kb, dqs, False), (zero, zero))
        dqs = chunk(n_full, dqs, True)
        dqs = add_grads(dqs, 2 * n_full + 1, 1, skip)
        dq_ref[:, :HEAD_PAD] = dqs[0].T * MLA_SCALE
        dq_ref[:, HEAD_PAD:] = dqs[1].T * MLA_SCALE

    return pl.pallas_call(
        body, name="mla_bwd",
        grid=(4, nq),
        in_specs=[pl.BlockSpec((bq, 2 * HEAD_PAD), lambda h, i: (i, h)),
                  pl.BlockSpec((s, 2 * HEAD_PAD), lambda h, i: (0, h)),
                  pl.BlockSpec((s, LANES), lambda h, i: (0, h)),
                  pl.BlockSpec((bq, LANES), lambda h, i: (i, h)),
                  pl.BlockSpec((bq, LANES), lambda h, i: (i, h)),
                  pl.BlockSpec((None, 8, bq), lambda h, i: (h, 0, i))],
        out_specs=[pl.BlockSpec((bq, 2 * HEAD_PAD), lambda h, i: (i, h)),
                   pl.BlockSpec((s, 2 * HEAD_PAD), lambda h, i: (0, h)),
                   pl.BlockSpec((s, LANES), lambda h, i: (0, h))],
        out_shape=[jax.ShapeDtypeStruct((s, MLA_PAD_WIDTH), F32), jax.ShapeDtypeStruct((s, MLA_PAD_WIDTH), F32),
                   jax.ShapeDtypeStruct((s, MLA_WIDTH), F32)],
        scratch_shapes=[pltpu.VMEM((2, bq, LANES), BF16),
                        pltpu.VMEM((2, 2, half, bq), F32), pltpu.VMEM((2, 2, half, bq), F32),
                        pltpu.VMEM((2, 2, half, bq), BF16), pltpu.VMEM((2, 2, half, bq), BF16)],
        compiler_params=_params(("arbitrary", "arbitrary"), VMEM_LIMIT),
    )(q, k, v, do, o, lse)


def _rms_bwd(d_out, inp, r, weight, n):
    normed = inp * r
    gw = d_out * weight
    d_in = r * (gw - normed * (jnp.sum(gw * normed, axis=-1, keepdims=True) * (1.0 / n)))
    return d_in, d_out * normed


def _mla_prep_bwd_call(dq, dk, dv, q0, k0, cqn, ckvn, c_q, c_kv, cos_t, sin_t,
                       q_lora_norm, kv_lora_norm, qhn_pad, khn_pad, w_uq_bf, w_uk_bf, w_uv_bf):
    s = dq.shape[0]
    tm = _row_tile(s, 256)

    def body(dq_ref, dk_ref, dv_ref, q0_ref, k0_ref, cqn_ref, ckvn_ref, cq_ref, ckv_ref,
             cos_ref, sin_ref, qln_ref, kvln_ref, qhn_ref, khn_ref, wuq_ref, wuk_ref, wuv_ref,
             dcq_ref, dckv_ref, dkr_ref, gwuq_ref, gwuk_ref, gwuv_ref, gqln_ref, gkvln_ref, gqhn_ref, gkhn_ref,
             dq0_ref, dk0_ref, tmp_ref):
        @pl.when(pl.program_id(0) == 0)
        def _():
            for ref in (gwuq_ref, gwuk_ref, gwuv_ref, gqln_ref, gkvln_ref, gqhn_ref, gkhn_ref):
                ref[...] = jnp.zeros_like(ref)

        cos_t, sin_t = cos_ref[...], sin_ref[...]
        lane = lax.broadcasted_iota(I32, (tm, LANES), 1)
        rope_lanes = jnp.logical_or(lane < ROPE_HALF, jnp.logical_and(lane >= 64, lane < 64 + ROPE_HALF))
        heads = [slice(h * HEAD_PAD, (h + 1) * HEAD_PAD) for h in range(MLA_HEADS)]

        def head_norm_bwd(d_ref, x0_ref, w_ref, out_ref, scale):
            w = w_ref[...]
            inv = [lax.rsqrt(jnp.sum(x0_ref[:, cols] * x0_ref[:, cols], axis=-1, keepdims=True)
                             * (1.0 / MLA_QK_DIM) + EPS) for cols in heads]
            for cols in heads:
                tmp_ref[:, cols] = _rope_adjoint(d_ref[:, cols] * scale, cos_t, sin_t)
            dots = [jnp.sum(tmp_ref[:, cols] * w * (x0_ref[:, cols] * r), axis=-1, keepdims=True)
                    for cols, r in zip(heads, inv)]
            g_w = jnp.zeros((1, LANES), F32)
            rope_sum = jnp.zeros((tm, LANES), F32)
            for cols, r, dot in zip(heads, inv, dots):
                normed = x0_ref[:, cols] * r
                d_n = tmp_ref[:, cols]
                d_x0 = r * (d_n * w - normed * (dot * (1.0 / MLA_QK_DIM)))
                out_ref[:, cols] = d_x0.astype(BF16)
                g_w = g_w + jnp.sum(d_n * normed, axis=0, keepdims=True)
                rope_sum = rope_sum + jnp.where(rope_lanes, d_x0, 0.0)
            return g_w, rope_sum

        g_qhn, _ = head_norm_bwd(dq_ref, q0_ref, qhn_ref, dq0_ref, 1.0)
        g_khn, d_kr = head_norm_bwd(dk_ref, k0_ref, khn_ref, dk0_ref, LN2)
        cqn, ckvn = cqn_ref[...], ckvn_ref[...]
        d_q0b, d_k0b, dvb = dq0_ref[...], dk0_ref[...], dv_ref[...].astype(BF16)
        d_cqn = _dot_nt(d_q0b, wuq_ref[...])
        gwuq_ref[...] += _dot_tn(cqn, d_q0b)
        d_ckvn = _dot_nt(d_k0b, wuk_ref[...]) + _dot_nt(dvb, wuv_ref[...])
        gwuk_ref[...] += _dot_tn(ckvn, d_k0b)
        gwuv_ref[...] += _dot_tn(ckvn, dvb)
        gqhn_ref[...] += g_qhn
        gkhn_ref[...] += g_khn
        dkr_ref[...] = d_kr.astype(BF16)
        cq = cq_ref[...]
        rcq = lax.rsqrt(jnp.mean(cq * cq, axis=-1, keepdims=True) + EPS)
        d_cq, gl = _rms_bwd(d_cqn, cq, rcq, qln_ref[...], Q_LORA_RANK)
        dcq_ref[...] = d_cq.astype(BF16)
        gqln_ref[...] += jnp.sum(gl, axis=0, keepdims=True)
        ckv = ckv_ref[...]
        rckv = lax.rsqrt(jnp.mean(ckv * ckv, axis=-1, keepdims=True) + EPS)
        d_ckv, gl = _rms_bwd(d_ckvn, ckv, rckv, kvln_ref[...], KV_LORA_RANK)
        dckv_ref[...] = d_ckv.astype(BF16)
        gkvln_ref[...] += jnp.sum(gl, axis=0, keepdims=True)

    return pl.pallas_call(
        body, name="mla_prep_bwd",
        grid=(s // tm,),
        in_specs=[_rows(tm, MLA_PAD_WIDTH), _rows(tm, MLA_PAD_WIDTH), _rows(tm, MLA_WIDTH),
                  _rows(tm, MLA_PAD_WIDTH), _rows(tm, MLA_PAD_WIDTH),
                  _rows(tm, Q_LORA_RANK), _rows(tm, KV_LORA_RANK), _rows(tm, Q_LORA_RANK), _rows(tm, KV_LORA_RANK),
                  _rows(tm, LANES), _rows(tm, LANES),
                  _full((1, Q_LORA_RANK)), _full((1, KV_LORA_RANK)), _full((1, LANES)), _full((1, LANES)),
                  _full((Q_LORA_RANK, MLA_PAD_WIDTH)), _full((KV_LORA_RANK, MLA_PAD_WIDTH)),
                  _full((KV_LORA_RANK, MLA_WIDTH))],
        out_specs=[_rows(tm, Q_LORA_RANK), _rows(tm, KV_LORA_RANK), _rows(tm, LANES),
                   _full((Q_LORA_RANK, MLA_PAD_WIDTH)), _full((KV_LORA_RANK, MLA_PAD_WIDTH)),
                   _full((KV_LORA_RANK, MLA_WIDTH)),
                   _full((1, Q_LORA_RANK)), _full((1, KV_LORA_RANK)), _full((1, LANES)), _full((1, LANES))],
        out_shape=[jax.ShapeDtypeStruct((s, Q_LORA_RANK), BF16), jax.ShapeDtypeStruct((s, KV_LORA_RANK), BF16),
                   jax.ShapeDtypeStruct((s, LANES), BF16),
                   jax.ShapeDtypeStruct((Q_LORA_RANK, MLA_PAD_WIDTH), F32),
                   jax.ShapeDtypeStruct((KV_LORA_RANK, MLA_PAD_WIDTH), F32),
                   jax.ShapeDtypeStruct((KV_LORA_RANK, MLA_WIDTH), F32),
                   jax.ShapeDtypeStruct((1, Q_LORA_RANK), F32), jax.ShapeDtypeStruct((1, KV_LORA_RANK), F32),
                   jax.ShapeDtypeStruct((1, LANES), F32), jax.ShapeDtypeStruct((1, LANES), F32)],
        scratch_shapes=[pltpu.VMEM((tm, MLA_PAD_WIDTH), BF16), pltpu.VMEM((tm, MLA_PAD_WIDTH), BF16),
                        pltpu.VMEM((tm, MLA_PAD_WIDTH), F32)],
        compiler_params=_params(("arbitrary",), VMEM_LIMIT),
    )(dq, dk, dv, q0, k0, cqn, ckvn, c_q, c_kv, cos_t, sin_t,
      q_lora_norm, kv_lora_norm, qhn_pad, khn_pad, w_uq_bf, w_uk_bf, w_uv_bf)


def _dh_call(pieces, hb, x, dy, ada, norm_w, w_in_bf):
    s = x.shape[0]
    tm = _row_tile(s, 256)
    widths = [p.shape[1] for p in pieces]
    offsets = [sum(widths[:j]) for j in range(len(widths))]
    assert offsets[-1] + widths[-1] == IN_COLS_PAD
    n = len(pieces)

    def body(*refs):
        p_refs = refs[:n]
        (hb_ref, x_ref, dy_ref, sh_ref, sc_ref, nw_ref, w_ref, gx_ref, gw_ref, dsh_ref, dsc_ref, gnw_ref,
         dp_ref) = refs[n:]

        @pl.when(pl.program_id(0) == 0)
        def _():
            gw_ref[...] = jnp.zeros_like(gw_ref)
            dsh_ref[...] = jnp.zeros_like(dsh_ref)
            dsc_ref[...] = jnp.zeros_like(dsc_ref)
            gnw_ref[...] = jnp.zeros_like(gnw_ref)

        for p_ref, c0, width in zip(p_refs, offsets, widths):
            dp_ref[:, c0:c0 + width] = p_ref[...].astype(BF16)
        gw_ref[...] += _dot_tn(hb_ref[...], dp_ref[...])
        dh = _dot_nt(dp_ref[...], w_ref[...])
        xx = x_ref[...]
        r0 = lax.rsqrt(jnp.mean(xx * xx, axis=-1, keepdims=True) + EPS)
        xn = xx * r0
        nw = nw_ref[...]
        dsh_ref[...] += jnp.sum(dh, axis=0, keepdims=True)
        dsc_ref[...] += jnp.sum(dh * (xn * nw), axis=0, keepdims=True)
        dn = dh * (1.0 + sc_ref[...])
        gnw_ref[...] += jnp.sum(dn * xn, axis=0, keepdims=True)
        dxn = dn * nw
        gx_ref[...] = dy_ref[...] + r0 * (dxn - xn * jnp.mean(dxn * xn, axis=-1, keepdims=True))

    return pl.pallas_call(
        body, name="in_proj_bwd",
        grid=(s // tm,),
        in_specs=[_rows(tm, w) for w in widths]
        + [_rows(tm, D_MODEL), _rows(tm, D_MODEL), _rows(tm, D_MODEL),
           _ada_part(0), _ada_part(1), _full((1, D_MODEL)),
           pl.BlockSpec((D_MODEL, IN_COLS_PAD), lambda i: (0, 0), pipeline_mode=pl.Buffered(1))],
        out_specs=[_rows(tm, D_MODEL),
                   pl.BlockSpec((D_MODEL, IN_COLS_PAD), lambda i: (0, 0), pipeline_mode=pl.Buffered(1)),
                   _full((1, D_MODEL)), _full((1, D_MODEL)), _full((1, D_MODEL))],
        out_shape=[jax.ShapeDtypeStruct((s, D_MODEL), F32), jax.ShapeDtypeStruct((D_MODEL, IN_COLS_PAD), F32),
                   jax.ShapeDtypeStruct((1, D_MODEL), F32), jax.ShapeDtypeStruct((1, D_MODEL), F32),
                   jax.ShapeDtypeStruct((1, D_MODEL), F32)],
        scratch_shapes=[pltpu.VMEM((tm, IN_COLS_PAD), BF16)],
        compiler_params=_params(("arbitrary",), VMEM_LIMIT),
    )(*pieces, hb, x, dy, ada, ada, norm_w, w_in_bf)


def _adamw(g, w, m, v):
    m = ADAM_B1 * m + (1.0 - ADAM_B1) * g
    v = ADAM_B2 * v + (1.0 - ADAM_B2) * (g * g)
    m_hat = m / (1.0 - ADAM_B1 ** ADAM_STEP)
    v_hat = v / (1.0 - ADAM_B2 ** ADAM_STEP)
    delta = -ADAM_LR * (m_hat / (jnp.sqrt(v_hat) + ADAM_EPS) + ADAM_WD * w)
    return delta, m, v


def _adam_shard_call(name, own, sib, w, m, v):
    r, c = w.shape
    tr = r if r <= 512 else 256

    def body(own_ref, sib_ref, w_ref, m_ref, v_ref, g_ref, d_ref, nm_ref, nv_ref):
        a = ((own_ref[0].astype(F32) + own_ref[1].astype(F32)) + own_ref[2].astype(F32)) + own_ref[3].astype(F32)
        b = ((sib_ref[0].astype(F32) + sib_ref[1].astype(F32)) + sib_ref[2].astype(F32)) + sib_ref[3].astype(F32)
        g = a + b
        g_ref[...] = g
        d_ref[...], nm_ref[...], nv_ref[...] = _adamw(g, w_ref[...], m_ref[...], v_ref[...])

    part = pl.BlockSpec((4, tr, c), lambda i: (0, i, 0))
    blk = pl.BlockSpec((tr, c), lambda i: (i, 0))
    return pl.pallas_call(
        body, name=name,
        grid=(r // tr,),
        in_specs=[part, part, blk, blk, blk],
        out_specs=[blk] * 4,
        out_shape=[jax.ShapeDtypeStruct((r, c), F32)] * 4,
        compiler_params=_params(("arbitrary",), VMEM_LIMIT),
    )(own, sib, w, m, v)


def _adam_ada_call(c_all, d_all, w, m, v):
    r, c = w.shape
    tr = 256

    def body(c_ref, d_ref, w_ref, m_ref, v_ref, g_ref, dl_ref, nm_ref, nv_ref):
        cc = c_ref[...]
        sc = cc * _sigmoid(cc)
        dd = d_ref[...]
        sc_hi = sc.astype(BF16)
        sc_lo = (sc - sc_hi.astype(F32)).astype(BF16)
        dd_hi = dd.astype(BF16)
        dd_lo = (dd - dd_hi.astype(F32)).astype(BF16)
        g = _dot_tn(sc_hi, dd_hi) + (_dot_tn(sc_hi, dd_lo) + _dot_tn(sc_lo, dd_hi))
        g_ref[...] = g
        dl_ref[...], nm_ref[...], nv_ref[...] = _adamw(g, w_ref[...], m_ref[...], v_ref[...])

    blk = pl.BlockSpec((tr, c), lambda i: (i, 0))
    return pl.pallas_call(
        body, name="adam_w_ada",
        grid=(r // tr,),
        in_specs=[pl.BlockSpec((16, tr), lambda i: (0, i)), pl.BlockSpec((16, c), lambda i: (0, 0)), blk, blk, blk],
        out_specs=[blk] * 4,
        out_shape=[jax.ShapeDtypeStruct((r, c), F32)] * 4,
        compiler_params=_params(("arbitrary",), VMEM_LIMIT),
    )(c_all, d_all, w, m, v)


def _adam_vectors_call(packs, offsets, vectors):
    nv = len(vectors)

    def body(*refs):
        p_ref, ins, outs = refs[0], refs[1:1 + 3 * nv], refs[1 + 3 * nv:]
        for j, off in enumerate(offsets):
            n = ins[3 * j].shape[1]
            span = -(-n // LANES) * LANES
            g = p_ref[0, :, off:off + span]
            for b in range(1, 8):
                g = g + p_ref[b, :, off:off + span]
            g = g[:, :n]
            outs[j][...] = g
            outs[nv + j][...], outs[2 * nv + j][...], outs[3 * nv + j][...] = _adamw(
                g, ins[3 * j][...], ins[3 * j + 1][...], ins[3 * j + 2][...])

    flat = [a for t in vectors for a in t]
    res = pl.pallas_call(
        body, name="adam_vectors",
        out_shape=[jax.ShapeDtypeStruct(t[0].shape, F32) for _ in range(4) for t in vectors],
    )(packs, *flat)
    return [res[k * nv:(k + 1) * nv] for k in range(4)]


ROPE_HALF = MLA_ROPE_DIM // 2
NOPE_A = MLA_NOPE_DIM - ROPE_HALF


def _zeros_like_lanes(t, n):
    return jnp.zeros(t.shape[:-1] + (n,), t.dtype)


def _to_head_lanes(t):
    nope, rope = t[..., :MLA_NOPE_DIM], t[..., MLA_NOPE_DIM:]
    return jnp.concatenate([rope[..., :ROPE_HALF], nope[..., :NOPE_A], rope[..., ROPE_HALF:], nope[..., NOPE_A:],
                            _zeros_like_lanes(t, HEAD_PAD - MLA_QK_DIM)], axis=-1)


def _from_head_lanes(t):
    return jnp.concatenate([t[..., ROPE_HALF:MLA_NOPE_DIM], t[..., 64 + ROPE_HALF:MLA_QK_DIM],
                            t[..., :ROPE_HALF], t[..., 64:64 + ROPE_HALF]], axis=-1)


def _nope_to_head_lanes(t):
    return jnp.concatenate([_zeros_like_lanes(t, ROPE_HALF), t[..., :NOPE_A], _zeros_like_lanes(t, ROPE_HALF),
                            t[..., NOPE_A:], _zeros_like_lanes(t, HEAD_PAD - MLA_QK_DIM)], axis=-1)


def _rope_to_head_lanes(t):
    return jnp.concatenate([t[..., :ROPE_HALF], _zeros_like_lanes(t, 64 - ROPE_HALF), t[..., ROPE_HALF:],
                            _zeros_like_lanes(t, 64 - ROPE_HALF)], axis=-1)


def _rope_tables(positions):
    inv_freq = ROPE_THETA ** (-jnp.arange(0, MLA_ROPE_DIM, 2, dtype=F32) / MLA_ROPE_DIM)
    ang = jnp.concatenate([-inv_freq, inv_freq])[:, None] * positions.astype(F32)[None, :]
    n = positions.shape[0]
    ones, zeros = jnp.ones((MLA_NOPE_DIM, n), F32), jnp.zeros((MLA_NOPE_DIM, n), F32)
    cos_t = _to_head_lanes(jnp.concatenate([ones, jnp.cos(ang)], axis=0).T)
    sin_t = _to_head_lanes(jnp.concatenate([zeros, jnp.sin(ang)], axis=0).T)
    return cos_t, sin_t


def _unshard_cols(g):
    return jnp.transpose(g, (1, 0, 2)).reshape(g.shape[1], 4 * g.shape[2])


def _shard_cols(g):
    r, c4 = g.shape
    return jnp.transpose(g.reshape(r, 4, c4 // 4), (1, 0, 2))


def kernel(x, c, positions, w_ada, b_ada, norm_w, w_in, q_lora_norm, w_uq, kv_lora_norm, w_ukv, q_head_norm, k_head_norm, w_out, loss_target, m_w_ada, m_b_ada, m_norm_w, m_w_in, m_q_lora_norm, m_w_uq, m_kv_lora_norm, m_w_ukv, m_q_head_norm, m_k_head_norm, m_w_out, v_w_ada, v_b_ada, v_norm_w, v_w_in, v_q_lora_norm, v_w_uq, v_kv_lora_norm, v_w_ukv, v_q_head_norm, v_k_head_norm, v_w_out):
    chip = 2 * lax.axis_index("x") + lax.axis_index("y")
    me8 = 2 * chip + lax.axis_index("c")
    ada_cols = w_ada.shape[2]
    c_all = _allgather_rows_call(c)[:, 0, :]
    ada_part = _ada_call(c_all, w_ada[0], lax.dynamic_slice_in_dim(b_ada, chip * ada_cols, ada_cols, axis=1))
    ada_g, win_g, wuq_g, wukv_g, wout_g = _gather_call(
        [ada_part[None]] + [w.astype(BF16) for w in (w_in, w_uq, w_ukv, w_out)], [False, True, True, True, True])
    ada = lax.dynamic_slice_in_dim(ada_g, me8, 1, axis=1).reshape(1, 4 * ada_cols)
    (sq_sum, grad_x, g_w_in, g_w_uq, g_w_ukv, g_w_out, d_ada, g_norm_w, g_qln, g_kvln, g_qhn, g_khn) = _local_step(
        x[0], ada, positions[0], loss_target[0], norm_w, win_g,
        q_lora_norm, _unshard_cols(wuq_g), kv_lora_norm, _unshard_cols(wukv_g), q_head_norm, k_head_norm,
        wout_g.reshape(D_MODEL, D_MODEL))
    loss = lax.psum(0.5 * sq_sum / D_MODEL, ("x", "y", "c"))

    grads = [g.astype(BF16) for g in (g_w_in, _shard_cols(g_w_uq), _shard_cols(g_w_ukv),
                                      g_w_out.reshape(4, D_MODEL // 4, D_MODEL))]
    pieces = [d_ada, g_norm_w, g_qln, g_kvln, g_qhn, g_khn]
    spans = [-(-p.shape[1] // LANES) * LANES for p in pieces]
    starts = [sum(spans[:j]) for j in range(len(spans))]
    small = jnp.concatenate([jnp.pad(p, ((0, 0), (0, sp - p.shape[1]))) for p, sp in zip(pieces, spans)], axis=1)
    own, sib, packs = _exchange_call(grads, small)

    names = ["adam_w_in", "adam_w_uq", "adam_w_ukv", "adam_w_out"]
    shard_w = [(w_in, m_w_in, v_w_in), (w_uq, m_w_uq, v_w_uq), (w_ukv, m_w_ukv, v_w_ukv),
               (w_out, m_w_out, v_w_out)]
    res = {}
    for name, o_g, s_g, (w, m, v) in zip(names, own, sib, shard_w):
        res[name] = _adam_shard_call(name, o_g, s_g, w[0], m[0], v[0])
    d_all = lax.dynamic_slice_in_dim(packs[:, 0, :], starts[0] + chip * ada_cols, ada_cols, axis=1)
    res_ada = _adam_ada_call(jnp.pad(c_all, ((0, 8), (0, 0))), jnp.pad(d_all, ((0, 8), (0, 0))),
                             w_ada[0], m_w_ada[0], v_w_ada[0])
    vectors = [(b_ada, m_b_ada, v_b_ada), (norm_w, m_norm_w, v_norm_w), (q_lora_norm, m_q_lora_norm, v_q_lora_norm),
               (kv_lora_norm, m_kv_lora_norm, v_kv_lora_norm), (q_head_norm, m_q_head_norm, v_q_head_norm),
               (k_head_norm, m_k_head_norm, v_k_head_norm)]
    vec_out = _adam_vectors_call(packs, starts, vectors)

    def ordered(kind):
        big = lambda name: res[name][kind][None]
        return [res_ada[kind][None], vec_out[kind][0], vec_out[kind][1], big("adam_w_in"), vec_out[kind][2],
                big("adam_w_uq"), vec_out[kind][3], big("adam_w_ukv"), vec_out[kind][4], vec_out[kind][5],
                big("adam_w_out")]

    return (loss, grad_x[None], *ordered(0), *ordered(1), *ordered(2), *ordered(3))


IN_SHARD = 808
CKV_TAIL = 2688 - 3 * IN_SHARD


def _local_step(x2, ada, positions, tgt, norm_w, w_in_shards, q_lora_norm, w_uq_full,
                kv_lora_norm, w_ukv_full, q_head_norm, k_head_norm, w_out_full):
    last = w_in_shards[3]
    w_in_bf = jnp.concatenate(
        [w_in_shards[0], w_in_shards[1], w_in_shards[2], last[:, :CKV_TAIL], last[:, CKV_TAIL + MLA_ROPE_DIM:],
         _rope_to_head_lanes(last[:, CKV_TAIL:CKV_TAIL + MLA_ROPE_DIM])], axis=1).astype(BF16)
    w_uq_bf = _to_head_lanes(w_uq_full.reshape(Q_LORA_RANK, MLA_HEADS, MLA_QK_DIM)).reshape(
        Q_LORA_RANK, MLA_PAD_WIDTH).astype(BF16)
    w_ukv_heads = w_ukv_full.reshape(KV_LORA_RANK, MLA_HEADS, 2 * MLA_NOPE_DIM)
    w_uk_bf = _nope_to_head_lanes(w_ukv_heads[:, :, :MLA_NOPE_DIM]).reshape(KV_LORA_RANK, MLA_PAD_WIDTH).astype(BF16)
    w_uv_bf = w_ukv_heads[:, :, MLA_NOPE_DIM:].reshape(KV_LORA_RANK, MLA_WIDTH).astype(BF16)
    w_out_bf = w_out_full.astype(BF16)
    qhn_pad, khn_pad = _to_head_lanes(q_head_norm), _to_head_lanes(k_head_norm)
    cos_t, sin_t = _rope_tables(positions)

    hb, q_sb, k_sb, v_sb, g_sb, c_q, c_kv, g_mla, k_rope = _pre_call(x2, ada, norm_w, w_in_bf)
    q_m, k_m, v_m, cqn, ckvn, q0, k0 = _mla_prep_call(
        c_q, c_kv, k_rope, cos_t, sin_t, q_lora_norm, kv_lora_norm, qhn_pad, khn_pad,
        w_uq_bf, w_uk_bf, w_uv_bf)
    o_sb, r_sb, kstart = _sb_fwd_call(q_sb, k_sb, v_sb)
    o_mla, lse = _mla_fwd_call(q_m, k_m, v_m)
    do_sb, do_mla, dg_sb, dg_mla, dy, g_w_out, d_gate, sq = _out_call(
        o_sb, g_sb, o_mla, g_mla, x2, tgt, ada, w_out_bf)

    dq_sb, dk_sb, dv_sb = _sb_bwd_call(kstart, q_sb, k_sb, v_sb, do_sb, r_sb)
    dq_m, dk_m, dv_m = _mla_bwd_call(q_m, k_m, v_m, do_mla, o_mla, lse)
    (d_cq, d_ckv, d_kr, g_wuq_pad, g_wuk_pad, g_wuv, g_qln, g_kvln, g_qhn, g_khn) = _mla_prep_bwd_call(
        dq_m, dk_m, dv_m, q0, k0, cqn, ckvn, c_q, c_kv, cos_t, sin_t,
        q_lora_norm, kv_lora_norm, qhn_pad, khn_pad, w_uq_bf, w_uk_bf, w_uv_bf)
    grad_x, g_win_pad, d_shift, d_scale, g_norm_w = _dh_call(
        [dq_sb, dk_sb, dv_sb, dg_sb, d_cq, d_ckv, dg_mla, d_kr], hb, x2, dy, ada, norm_w, w_in_bf)

    g_kr = g_win_pad[:, C_KR:]
    g_last = jnp.concatenate([g_win_pad[:, 3 * IN_SHARD:C_GM], g_kr[:, :ROPE_HALF], g_kr[:, 64:64 + ROPE_HALF],
                              g_win_pad[:, C_GM:C_KR]], axis=1)
    g_w_in = jnp.stack([g_win_pad[:, j * IN_SHARD:(j + 1) * IN_SHARD] for j in range(3)] + [g_last])
    g_w_uq = _from_head_lanes(g_wuq_pad.reshape(Q_LORA_RANK, MLA_HEADS, HEAD_PAD)).reshape(Q_LORA_RANK, -1)
    g_w_ukv = jnp.concatenate(
        [_from_head_lanes(g_wuk_pad.reshape(KV_LORA_RANK, MLA_HEADS, HEAD_PAD))[:, :, :MLA_NOPE_DIM],
         g_wuv.reshape(KV_LORA_RANK, MLA_HEADS, MLA_NOPE_DIM)], axis=2).reshape(KV_LORA_RANK, -1)
    d_ada = jnp.concatenate([d_shift, d_scale, d_gate], axis=1)
    return (jnp.sum(sq), grad_x, g_w_in, g_w_uq, g_w_ukv, g_w_out, d_ada, g_norm_w, g_qln, g_kvln,
            _from_head_lanes(g_qhn), _from_head_lanes(g_khn))
```

```python
import functools
import math

import jax
import jax.numpy as jnp
from jax import lax
from jax.experimental import pallas as pl
from jax.experimental.pallas import tpu as pltpu

F32 = jnp.float32
BF16 = jnp.bfloat16
I32 = jnp.int32

D_MODEL = 1024
SB_HEADS = 8
SB_WIDTH = 512
MLA_HEADS = 8
MLA_QK_DIM = 96
MLA_NOPE_DIM = 64
MLA_ROPE_DIM = 32
MLA_WIDTH = 512
Q_LORA_RANK = 384
KV_LORA_RANK = 256
ROPE_THETA = 10000.0
EPS = 1e-6
LANES = 128
HEAD_PAD = 128
MLA_PAD_WIDTH = MLA_HEADS * HEAD_PAD

C_Q, C_K, C_V, C_G = 0, 512, 1024, 1536
C_CQ, C_CKV, C_GM, C_KR = 2048, 2432, 2688, 3200
IN_COLS_PAD = 3328

ADAM_LR = 0.001
ADAM_B1 = 0.9
ADAM_B2 = 0.999
ADAM_EPS = 1e-08
ADAM_WD = 0.01
ADAM_STEP = 10

SB_SCALE = 0.125
SB_GROUP = 4
MLA_SCALE = 1.0 / math.sqrt(MLA_QK_DIM)
LN2 = math.log(2.0)
MLA_SCALE_LOG2 = MLA_SCALE / LN2
MLA_BQ = 1024
MLA_BWD_BQ = 1024
MLA_BK = 1024
MLA_BWD_BK = 1024
MLA_STEP = 512
MLA_BWD_STEP = 256
SB_DEAD = -104.0
MASK_NEG = -1e30

VMEM_LIMIT = 56 * 1024 * 1024
MESH = pl.DeviceIdType.MESH


def _dot(a, b):
    return jnp.dot(a, b, preferred_element_type=F32)


def _dot_nt(a, b):
    return lax.dot_general(a, b, (((1,), (1,)), ((), ())), preferred_element_type=F32)


def _dot_tn(a, b):
    return lax.dot_general(a, b, (((0,), (0,)), ((), ())), preferred_element_type=F32)


def _sigmoid(x):
    return 1.0 / (1.0 + jnp.exp(-x))


def _split_dot(a, m):
    hi = a.astype(BF16)
    lo = (a - hi.astype(F32)).astype(BF16)
    return _dot(hi, m) + _dot(lo, m)


def _params(sem, vmem=None):
    return pltpu.CompilerParams(dimension_semantics=sem, vmem_limit_bytes=vmem)


def _row_tile(s, want):
    return min(want, s)


def _hbm_spec():
    return pl.BlockSpec(memory_space=pltpu.HBM)


def _allgather_rows_call(row):
    def body(in_ref, out_ref, send_sems, recv_sems, loc_sem):
        x, y, c = lax.axis_index("x"), lax.axis_index("y"), lax.axis_index("c")
        flips = [(fx, fy, fc) for fx in (0, 1) for fy in (0, 1) for fc in (0, 1)][1:]

        def peer(r):
            fx, fy, fc = flips[r]
            return ((1 - x) if fx else x, (1 - y) if fy else y, (1 - c) if fc else c)

        def copy(r, slot):
            return pltpu.make_async_remote_copy(
                src_ref=in_ref, dst_ref=out_ref.at[slot], send_sem=send_sems.at[r], recv_sem=recv_sems.at[r],
                device_id=peer(r), device_id_type=MESH)

        local = pltpu.make_async_copy(in_ref, out_ref.at[4 * x + 2 * y + c], loc_sem)
        local.start()
        sends = [copy(r, 4 * x + 2 * y + c) for r in range(7)]
        for cp in sends:
            cp.start()
        for r in range(7):
            px, py, pc = peer(r)
            copy(r, 4 * px + 2 * py + pc).wait_recv()
        for cp in sends:
            cp.wait_send()
        local.wait()

    return pl.pallas_call(
        body, name="gather_rows",
        out_shape=jax.ShapeDtypeStruct((8,) + row.shape, row.dtype),
        in_specs=[_hbm_spec()], out_specs=_hbm_spec(),
        scratch_shapes=[pltpu.SemaphoreType.DMA((7,)), pltpu.SemaphoreType.DMA((7,)), pltpu.SemaphoreType.DMA],
    )(row)


def _gather_call(shards, split):
    n = len(shards)
    halves = [s.shape[1] // 2 for s in shards]

    def body(*refs):
        ins, outs = refs[:n], refs[n:2 * n]
        ici_send, ici_recv, d2d_send, d2d_recv, loc_sems = refs[2 * n:]
        x, y, c = lax.axis_index("x"), lax.axis_index("y"), lax.axis_index("c")
        me = 2 * x + y
        peers = [(1 - x, y), (x, 1 - y), (1 - x, 1 - y)]

        def rows(a, which):
            return pl.ds(pl.multiple_of(which * halves[a], 16), halves[a])

        def ici(a, j, slot):
            px, py = peers[j]
            src, dst = ins[a].at[0], outs[a].at[slot]
            if split[a]:
                src, dst = src.at[rows(a, c)], dst.at[rows(a, c)]
            return pltpu.make_async_remote_copy(
                src_ref=src, dst_ref=dst,
                send_sem=ici_send.at[3 * a + j], recv_sem=ici_recv.at[3 * a + j],
                device_id=(px, py, c), device_id_type=MESH)

        def d2d(a, j, which):
            px, py = peers[j]
            piece = outs[a].at[2 * px + py, rows(a, which)]
            return pltpu.make_async_remote_copy(
                src_ref=piece, dst_ref=piece,
                send_sem=d2d_send.at[3 * a + j], recv_sem=d2d_recv.at[3 * a + j],
                device_id=(x, y, 1 - c), device_id_type=MESH)

        local = [pltpu.make_async_copy(ins[a].at[0], outs[a].at[me], loc_sems.at[a]) for a in range(n)]
        for cp in local:
            cp.start()
        sends = [ici(a, j, me) for a in range(n) for j in range(3)]
        for cp in sends:
            cp.start()
        for a in range(n):
            for j in range(3):
                px, py = peers[j]
                ici(a, j, 2 * px + py).wait_recv()
                if split[a]:
                    cp = d2d(a, j, c)
                    cp.start()
                    sends.append(cp)
        for a in range(n):
            for j in range(3):
                if split[a]:
                    d2d(a, j, 1 - c).wait_recv()
        for cp in sends:
            cp.wait_send()
        for cp in local:
            cp.wait()

    return pl.pallas_call(
        body, name="gather_weights",
        out_shape=[jax.ShapeDtypeStruct((4,) + s.shape[1:], s.dtype) for s in shards],
        in_specs=[_hbm_spec() for _ in shards],
        out_specs=[_hbm_spec() for _ in shards],
        scratch_shapes=[pltpu.SemaphoreType.DMA((3 * n,)), pltpu.SemaphoreType.DMA((3 * n,)),
                        pltpu.SemaphoreType.DMA((3 * n,)), pltpu.SemaphoreType.DMA((3 * n,)),
                        pltpu.SemaphoreType.DMA((n,))],
    )(*shards)


def _exchange_call(grads, small):
    n = len(grads)

    def body(*refs):
        g_in, small_in = refs[:n], refs[n]
        own, sib, packs = refs[n + 1:2 * n + 1], refs[2 * n + 1:3 * n + 1], refs[3 * n + 1]
        ici_send, ici_recv, d2d_send, d2d_recv, sm_send, sm_recv, loc_sems = refs[3 * n + 2:]
        x, y, c = lax.axis_index("x"), lax.axis_index("y"), lax.axis_index("c")
        me = 2 * x + y
        me8 = 4 * x + 2 * y + c
        sibling = (x, y, 1 - c)
        peers = [(1 - x, y), (x, 1 - y), (1 - x, 1 - y)]
        flips = [(fx, fy, fc) for fx in (0, 1) for fy in (0, 1) for fc in (0, 1)][1:]

        def ici(a, j, src_slot, dst_slot):
            px, py = peers[j]
            return pltpu.make_async_remote_copy(
                src_ref=g_in[a].at[src_slot], dst_ref=own[a].at[dst_slot],
                send_sem=ici_send.at[3 * a + j], recv_sem=ici_recv.at[3 * a + j],
                device_id=(px, py, c), device_id_type=MESH)

        def d2d(a, rel, chip, src):
            return pltpu.make_async_remote_copy(
                src_ref=src, dst_ref=sib[a].at[chip],
                send_sem=d2d_send.at[4 * a + rel], recv_sem=d2d_recv.at[4 * a + rel],
                device_id=sibling, device_id_type=MESH)

        def flipped(r):
            fx, fy, fc = flips[r]
            return ((1 - x) if fx else x, (1 - y) if fy else y, (1 - c) if fc else c)

        def sm(r, slot):
            return pltpu.make_async_remote_copy(
                src_ref=small_in, dst_ref=packs.at[slot],
                send_sem=sm_send.at[r], recv_sem=sm_recv.at[r],
                device_id=flipped(r), device_id_type=MESH)

        def peer8(r):
            px, py, pc = flipped(r)
            return 4 * px + 2 * py + pc

        local = [pltpu.make_async_copy(g_in[a].at[me], own[a].at[me], loc_sems.at[a]) for a in range(n)]
        local.append(pltpu.make_async_copy(small_in, packs.at[me8], loc_sems.at[n]))
        for cp in local:
            cp.start()
        sends = []
        for r in range(7):
            sends.append(sm(r, me8))
        for a in range(n):
            for j in range(3):
                px, py = peers[j]
                sends.append(ici(a, j, 2 * px + py, me))
        for cp in sends:
            cp.start()
        for a in range(n):
            cp = d2d(a, 0, me, g_in[a].at[me])
            cp.start()
            sends.append(cp)
        for a in range(n):
            for j in range(3):
                px, py = peers[j]
                ici(a, j, me, 2 * px + py).wait_recv()
                cp = d2d(a, 1 + j, 2 * px + py, own[a].at[2 * px + py])
                cp.start()
                sends.append(cp)
        for a in range(n):
            d2d(a, 0, me, g_in[a].at[me]).wait_recv()
            for j in range(3):
                px, py = peers[j]
                d2d(a, 1 + j, 2 * px + py, g_in[a].at[me]).wait_recv()
        for r in range(7):
            sm(r, peer8(r)).wait_recv()
        for cp in sends:
            cp.wait_send()
        for cp in local:
            cp.wait()

    out_shape = ([jax.ShapeDtypeStruct(g.shape, g.dtype) for g in grads] * 2
                 + [jax.ShapeDtypeStruct((8,) + small.shape, small.dtype)])
    res = pl.pallas_call(
        body, name="exchange_grads",
        out_shape=out_shape,
        in_specs=[_hbm_spec() for _ in range(n + 1)],
        out_specs=[_hbm_spec() for _ in range(2 * n + 1)],
        scratch_shapes=[pltpu.SemaphoreType.DMA((3 * n,)), pltpu.SemaphoreType.DMA((3 * n,)),
                        pltpu.SemaphoreType.DMA((4 * n,)), pltpu.SemaphoreType.DMA((4 * n,)),
                        pltpu.SemaphoreType.DMA((7,)), pltpu.SemaphoreType.DMA((7,)),
                        pltpu.SemaphoreType.DMA((n + 1,))],
    )(*grads, small)
    return res[:n], res[n:2 * n], res[2 * n]


def _ada_call(c_all, w_ada_cols, b_ada_cols):
    def body(c_ref, w_ref, b_ref, o_ref):
        cc = c_ref[...]
        o_ref[...] = _dot((cc * _sigmoid(cc)).astype(BF16), w_ref[...].astype(BF16)) + b_ref[...]

    return pl.pallas_call(
        body, name="ada_fwd",
        out_shape=jax.ShapeDtypeStruct((c_all.shape[0], w_ada_cols.shape[1]), F32),
        compiler_params=pltpu.CompilerParams(vmem_limit_bytes=VMEM_LIMIT),
    )(c_all, w_ada_cols, b_ada_cols)


def _ada_part(j):
    return pl.BlockSpec((1, D_MODEL), lambda i: (0, j))


def _full(shape):
    return pl.BlockSpec(shape, lambda i: (0,) * len(shape))


def _rows(tm, width):
    return pl.BlockSpec((tm, width), lambda i: (i, 0))


def _pre_call(x, ada, norm_w, w_in_bf):
    s = x.shape[0]
    tm = _row_tile(s, 512)
    groups = [(C_Q, 512, BF16), (C_K, 512, BF16), (C_V, 512, BF16), (C_G, 512, F32),
              (C_CQ, Q_LORA_RANK, F32), (C_CKV, KV_LORA_RANK, F32), (C_GM, 512, F32), (C_KR, LANES, F32)]

    def body(x_ref, sh_ref, sc_ref, nw_ref, w_ref, hb_ref, *outs):
        xx = x_ref[...]
        r0 = lax.rsqrt(jnp.mean(xx * xx, axis=-1, keepdims=True) + EPS)
        h = (xx * r0 * nw_ref[...]) * (1.0 + sc_ref[...]) + sh_ref[...]
        hb = h.astype(BF16)
        hb_ref[...] = hb
        for (c0, width, dt), o_ref in zip(groups, outs):
            o_ref[...] = _dot(hb, w_ref[:, c0:c0 + width]).astype(dt)

    return pl.pallas_call(
        body, name="pre_proj",
        grid=(s // tm,),
        in_specs=[_rows(tm, D_MODEL), _ada_part(0), _ada_part(1), _full((1, D_MODEL)),
                  _full((D_MODEL, IN_COLS_PAD))],
        out_specs=[_rows(tm, D_MODEL)] + [_rows(tm, w) for _, w, _ in groups],
        out_shape=[jax.ShapeDtypeStruct((s, D_MODEL), BF16)]
        + [jax.ShapeDtypeStruct((s, w), dt) for _, w, dt in groups],
        compiler_params=_params(("arbitrary",), VMEM_LIMIT),
    )(x, ada, ada, norm_w, w_in_bf)


def _rope(t, cos_t, sin_t):
    return t * cos_t + pltpu.roll(t, 64, 1) * sin_t


def _rope_adjoint(d, cos_t, sin_t):
    return d * cos_t + pltpu.roll(d * sin_t, 64, 1)


def _mla_prep_call(c_q, c_kv, k_rope, cos_t, sin_t, q_lora_norm, kv_lora_norm, qhn_pad, khn_pad,
                   w_uq_bf, w_uk_bf, w_uv_bf):
    s = c_q.shape[0]
    tm = _row_tile(s, 256)

    def body(cq_ref, ckv_ref, kr_ref, cos_ref, sin_ref, qln_ref, kvln_ref, qhn_ref, khn_ref,
             wuq_ref, wuk_ref, wuv_ref, q_ref, k_ref, v_ref, cqn_ref, ckvn_ref, q0_ref, k0_ref):
        cq = cq_ref[...]
        cqn = (cq * lax.rsqrt(jnp.mean(cq * cq, axis=-1, keepdims=True) + EPS) * qln_ref[...]).astype(BF16)
        cqn_ref[...] = cqn
        ckv = ckv_ref[...]
        ckvn = (ckv * lax.rsqrt(jnp.mean(ckv * ckv, axis=-1, keepdims=True) + EPS) * kvln_ref[...]).astype(BF16)
        ckvn_ref[...] = ckvn
        v_ref[...] = _dot(ckvn, wuv_ref[...]).astype(BF16)
        q0_ref[...] = _dot(cqn, wuq_ref[...])
        k0_ref[...] = _dot(ckvn, wuk_ref[...])
        cos_t, sin_t = cos_ref[...], sin_ref[...]
        kr = kr_ref[...]
        heads = [slice(h * HEAD_PAD, (h + 1) * HEAD_PAD) for h in range(MLA_HEADS)]
        for cols in heads:
            k0_ref[:, cols] = k0_ref[:, cols] + kr

        def inv_rms(ref):
            sums = [jnp.sum(ref[:, cols] * ref[:, cols], axis=-1, keepdims=True) for cols in heads]
            return [lax.rsqrt(t * (1.0 / MLA_QK_DIM) + EPS) for t in sums]

        rqs, rks = inv_rms(q0_ref), inv_rms(k0_ref)
        for cols, rq, rk in zip(heads, rqs, rks):
            q_ref[:, cols] = (_rope(q0_ref[:, cols] * rq * qhn_ref[...], cos_t, sin_t) * MLA_SCALE_LOG2).astype(BF16)
            k_ref[:, cols] = _rope(k0_ref[:, cols] * rk * khn_ref[...], cos_t, sin_t).astype(BF16)

    return pl.pallas_call(
        body, name="mla_prep",
        grid=(s // tm,),
        in_specs=[_rows(tm, Q_LORA_RANK), _rows(tm, KV_LORA_RANK), _rows(tm, LANES),
                  _rows(tm, LANES), _rows(tm, LANES),
                  _full((1, Q_LORA_RANK)), _full((1, KV_LORA_RANK)), _full((1, LANES)), _full((1, LANES)),
                  _full((Q_LORA_RANK, MLA_PAD_WIDTH)), _full((KV_LORA_RANK, MLA_PAD_WIDTH)),
                  _full((KV_LORA_RANK, MLA_WIDTH))],
        out_specs=[_rows(tm, MLA_PAD_WIDTH), _rows(tm, MLA_PAD_WIDTH), _rows(tm, MLA_WIDTH),
                   _rows(tm, Q_LORA_RANK), _rows(tm, KV_LORA_RANK),
                   _rows(tm, MLA_PAD_WIDTH), _rows(tm, MLA_PAD_WIDTH)],
        out_shape=[jax.ShapeDtypeStruct((s, MLA_PAD_WIDTH), BF16), jax.ShapeDtypeStruct((s, MLA_PAD_WIDTH), BF16),
                   jax.ShapeDtypeStruct((s, MLA_WIDTH), BF16),
                   jax.ShapeDtypeStruct((s, Q_LORA_RANK), BF16), jax.ShapeDtypeStruct((s, KV_LORA_RANK), BF16),
                   jax.ShapeDtypeStruct((s, MLA_PAD_WIDTH), F32), jax.ShapeDtypeStruct((s, MLA_PAD_WIDTH), F32)],
        compiler_params=_params(("arbitrary",), VMEM_LIMIT),
    )(c_q, c_kv, k_rope, cos_t, sin_t, q_lora_norm, kv_lora_norm, qhn_pad, khn_pad,
      w_uq_bf, w_uk_bf, w_uv_bf)


def _log_sigmoid_pair(z):
    ls = jnp.minimum(z, 0.0) - jnp.log(1.0 + jnp.exp(-jnp.abs(z)))
    return ls, ls - z


def _pair(hh):
    return slice((hh // 2) * LANES, (hh // 2 + 1) * LANES)


def _sb_fwd_call(q, k, v):
    s = q.shape[0]
    bq = _row_tile(s, 256)
    nq = s // bq
    nh = SB_GROUP
    width = nh * 64

    def body(q_ref, k_ref, v_ref, o_ref, r_ref, ks_ref):
        hp, i = pl.program_id(0), pl.program_id(1)
        lane = lax.broadcasted_iota(I32, (bq, LANES), 1)
        row = lax.broadcasted_iota(I32, (bq, bq), 0)
        col = lax.broadcasted_iota(I32, (bq, bq), 1)
        strict = col < row
        later = jnp.where(row > col, 1.0, 0.0).astype(BF16)
        masks = [_head_mask(lane, hh).astype(BF16) for hh in range(2)]
        qms = [q_ref[:, _pair(hh)] * jnp.asarray(SB_SCALE, BF16) * masks[hh % 2] for hh in range(nh)]

        def walk(blocks, state):
            chains = [(kb, diagonal, hh) for kb, diagonal in blocks for hh in range(nh)]
            keys = lambda kb: pl.ds(pl.multiple_of(kb * bq, bq), bq)
            zs = [_dot_nt(qms[hh], k_ref[keys(kb), _pair(hh)]) for kb, _, hh in chains]
            pairs = []
            for z, (_, diagonal, _) in zip(zs, chains):
                ls, lk = _log_sigmoid_pair(z)
                pairs.append((ls, jnp.where(strict, lk, 0.0) if diagonal else lk))
            sums = [_split_dot(lk, later) for _, lk in pairs]
            runs = [st[0] for st in state]
            ws = []
            for (ls, lk), after, (_, diagonal, hh) in zip(pairs, sums, chains):
                w = jnp.exp(ls + (after + runs[hh]))
                ws.append((jnp.where(strict, w, 0.0) if diagonal else w).astype(BF16))
                runs[hh] = runs[hh] + jnp.sum(lk, axis=1, keepdims=True)
            accs = [st[1] for st in state]
            for w, (kb, _, hh) in zip(ws, chains):
                accs[hh] = accs[hh] + _dot(w, v_ref[keys(kb), _pair(hh)])
            return tuple(zip(runs, accs))

        def alive(state):
            top = jnp.max(state[0][0])
            for st in state[1:]:
                top = jnp.maximum(top, jnp.max(st[0]))
            return (top > SB_DEAD).astype(I32)

        def finish(state, first):
            ks_ref[hp, i] = first
            for pair in range(nh // 2):
                o_ref[:, _pair(2 * pair)] = jnp.where(lane < 64, state[2 * pair][1], state[2 * pair + 1][1])
                r_ref[:, _pair(2 * pair)] = jnp.where(lane < 64, state[2 * pair][0], state[2 * pair + 1][0])

        zero = ((jnp.zeros((bq, 1), F32), jnp.zeros((bq, LANES), F32)),) * nh

        @pl.when(i == 0)
        def _():
            finish(walk([(0, True)], zero), 0)

        @pl.when(i > 0)
        def _():
            state = walk([(i, True), (i - 1, False)], zero)

            def cond(carry):
                return jnp.logical_and(carry[0] >= 0, carry[1] > 0)

            def step(carry):
                state = walk([(carry[0], False)], carry[2])
                return carry[0] - 1, alive(state), state

            kb, _, state = lax.while_loop(cond, step, (i - 2, alive(state), state))
            finish(state, kb + 1)

    return pl.pallas_call(
        body, name="sb_fwd",
        grid=(SB_HEADS // nh, nq),
        in_specs=[pl.BlockSpec((bq, width), lambda h, i: (i, h)),
                  pl.BlockSpec((s, width), lambda h, i: (0, h)),
                  pl.BlockSpec((s, width), lambda h, i: (0, h))],
        out_specs=[pl.BlockSpec((bq, width), lambda h, i: (i, h)),
                   pl.BlockSpec((bq, width), lambda h, i: (i, h)),
                   pl.BlockSpec(memory_space=pltpu.SMEM)],
        out_shape=[jax.ShapeDtypeStruct((s, SB_WIDTH), F32), jax.ShapeDtypeStruct((s, SB_WIDTH), F32),
                   jax.ShapeDtypeStruct((SB_HEADS // nh, nq), I32)],
        compiler_params=_params(("arbitrary", "arbitrary"), VMEM_LIMIT),
    )(q, k, v)


def _mla_fwd_call(q, k, v):
    s = q.shape[0]
    bq = _row_tile(s, MLA_BQ)
    bk = _row_tile(s, MLA_BK)
    nq = s // bq
    assert bk % bq == 0
    half = min(bk // 2, MLA_STEP)
    nsub = bk // half
    assert nsub % 2 == 0

    def body(q_ref, k_ref, v_ref, o_ref, lse_ref, p_ref, s_ref):
        i = pl.program_id(1)
        lane = lax.broadcasted_iota(I32, (bq, LANES), 1)
        row = lax.broadcasted_iota(I32, (half, bq), 1)
        col = lax.broadcasted_iota(I32, (half, bq), 0)
        n_full = (i * bq) // bk

        def keys(g):
            return pl.ds(pl.multiple_of(g * half, half), half)

        def join(left, right, qlo):
            return right if qlo == 0 else jnp.concatenate([left[:, :qlo], right], axis=1)

        def put_scores(g, slot, qlo=0):
            for hh in range(2):
                cols = slice(hh * HEAD_PAD, (hh + 1) * HEAD_PAD)
                s_ref[slot, hh, :, qlo:] = _dot_nt(k_ref[keys(g), cols], q_ref[qlo:, cols])

        def add_pv(carry, g, slot, qlo=0):
            vblk = v_ref[keys(g), :]
            out = []
            for hh, (m, l, acc, alpha) in enumerate(carry):
                upd = alpha[:, qlo:] * acc[:, qlo:] + _dot_tn(vblk, p_ref[slot, hh, :, qlo:])
                out.append((m, l, join(acc, upd, qlo), alpha))
            return tuple(out)

        def substep(g, slot, carry, masked, prefetch, qlo=0, next_qlo=0, prev_qlo=0):
            if prefetch:
                put_scores(g + 1, 1 - slot, next_qlo)
            carry = add_pv(carry, jnp.maximum(g - 1, 0), 1 - slot, prev_qlo)
            new = []
            for hh in range(2):
                m, l, acc, _ = carry[hh]
                sc = s_ref[slot, hh, :, qlo:]
                if masked:
                    sc = jnp.where(col[:, qlo:] + g * half <= row[:, qlo:] + i * bq, sc, MASK_NEG)
                m_new = jnp.maximum(m[:, qlo:], jnp.max(sc, axis=0, keepdims=True))
                p = jnp.exp2(sc - m_new)
                alpha = jnp.exp2(m[:, qlo:] - m_new)
                l_new = alpha * l[:, qlo:] + jnp.sum(p, axis=0, keepdims=True)
                p_ref[slot, hh, :, qlo:] = p.astype(BF16)
                new.append((join(m, m_new, qlo), join(l, l_new, qlo), acc, join(jnp.ones_like(m), alpha, qlo)))
            return tuple(new)

        def first_query(t, masked):
            return t * half if (masked and bk == bq and 0 <= t < nsub) else 0

        def chunk(kb, carry, masked):
            for t in range(nsub):
                last = masked and t == nsub - 1
                carry = substep(nsub * kb + t, t % 2, carry, masked, not last, first_query(t, masked),
                                first_query(t + 1, masked), first_query(t - 1, masked))
            return carry

        p_ref[1] = jnp.zeros_like(p_ref[1])
        put_scores(0, 0)
        one = (jnp.full((1, bq), MASK_NEG, F32), jnp.zeros((1, bq), F32), jnp.zeros((LANES, bq), F32),
               jnp.ones((1, bq), F32))
        carry = lax.fori_loop(0, n_full, lambda kb, cr: chunk(kb, cr, False), (one, one))
        carry = chunk(n_full, carry, True)
        (m0, l0, a0, _), (m1, l1, a1, _) = add_pv(carry, nsub * n_full + nsub - 1, 1, first_query(nsub - 1, True))
        o_ref[...] = jnp.where(lane < 64, (a0 / l0).T, (a1 / l1).T)
        sub = lax.broadcasted_iota(I32, (8, bq), 0)
        lse_ref[...] = jnp.where(sub == 0, m0 + jnp.log2(l0), jnp.where(sub == 1, m1 + jnp.log2(l1), 0.0))

    return pl.pallas_call(
        body, name="mla_fwd",
        grid=(4, nq),
        in_specs=[pl.BlockSpec((bq, 2 * HEAD_PAD), lambda h, i: (i, h)),
                  pl.BlockSpec((s, 2 * HEAD_PAD), lambda h, i: (0, h)),
                  pl.BlockSpec((s, LANES), lambda h, i: (0, h))],
        out_specs=[pl.BlockSpec((bq, LANES), lambda h, i: (i, h)),
                   pl.BlockSpec((None, 8, bq), lambda h, i: (h, 0, i))],
        out_shape=[jax.ShapeDtypeStruct((s, MLA_WIDTH), F32), jax.ShapeDtypeStruct((4, 8, s), F32)],
        scratch_shapes=[pltpu.VMEM((2, 2, half, bq), BF16), pltpu.VMEM((2, 2, half, bq), F32)],
        compiler_params=_params(("arbitrary", "arbitrary"), VMEM_LIMIT),
    )(q, k, v)


def _out_call(o_sb, g_sb, o_mla, g_mla, x, target, ada, w_out_bf):
    s = x.shape[0]
    tm = _row_tile(s, 256)

    def body(osb_ref, gsb_ref, oml_ref, gml_ref, x_ref, t_ref, gate_ref, w_ref,
             dosb_ref, doml_ref, dgsb_ref, dgml_ref, dy_ref, gw_ref, dgate_ref, sq_ref):
        @pl.when(pl.program_id(0) == 0)
        def _():
            gw_ref[...] = jnp.zeros_like(gw_ref)
            dgate_ref[...] = jnp.zeros_like(dgate_ref)
            sq_ref[...] = jnp.zeros_like(sq_ref)

        g_s, g_m = gsb_ref[...], gml_ref[...]
        sig_s, sig_m = _sigmoid(g_s), _sigmoid(g_m)
        silu_s, silu_m = g_s * sig_s, g_m * sig_m
        o_s, o_m = osb_ref[...], oml_ref[...]
        mixed = jnp.concatenate([o_s * silu_s, o_m * silu_m], axis=1).astype(BF16)
        u = _dot(mixed, w_ref[...])
        gate_v = gate_ref[...]
        err = x_ref[...] + gate_v * u - t_ref[...]
        sq_ref[...] += jnp.sum(err * err, axis=0, keepdims=True)
        dy = err * (1.0 / D_MODEL)
        dy_ref[...] = dy
        dgate_ref[...] += jnp.sum(dy * u, axis=0, keepdims=True)
        du = (dy * gate_v).astype(BF16)
        gw_ref[...] += _dot_tn(mixed, du)
        dmix = _dot_nt(du, w_ref[...])
        dm_s, dm_m = dmix[:, :SB_WIDTH], dmix[:, SB_WIDTH:]
        dosb_ref[...] = (dm_s * silu_s).astype(BF16)
        doml_ref[...] = (dm_m * silu_m).astype(BF16)
        dgsb_ref[...] = (dm_s * o_s * (sig_s * (1.0 + g_s * (1.0 - sig_s)))).astype(BF16)
        dgml_ref[...] = (dm_m * o_m * (sig_m * (1.0 + g_m * (1.0 - sig_m)))).astype(BF16)

    return pl.pallas_call(
        body, name="out_proj_loss",
        grid=(s // tm,),
        in_specs=[_rows(tm, 512), _rows(tm, 512), _rows(tm, 512), _rows(tm, 512),
                  _rows(tm, D_MODEL), _rows(tm, D_MODEL), _ada_part(2), _full((D_MODEL, D_MODEL))],
        out_specs=[_rows(tm, 512), _rows(tm, 512), _rows(tm, 512), _rows(tm, 512), _rows(tm, D_MODEL),
                   _full((D_MODEL, D_MODEL)), _full((1, D_MODEL)), _full((1, D_MODEL))],
        out_shape=[jax.ShapeDtypeStruct((s, 512), BF16)] * 4
        + [jax.ShapeDtypeStruct((s, D_MODEL), F32), jax.ShapeDtypeStruct((D_MODEL, D_MODEL), F32),
           jax.ShapeDtypeStruct((1, D_MODEL), F32), jax.ShapeDtypeStruct((1, D_MODEL), F32)],
        compiler_params=_params(("arbitrary",), VMEM_LIMIT),
    )(o_sb, g_sb, o_mla, g_mla, x, target, ada, w_out_bf)


def _head_mask(lane, hh):
    return jnp.where((lane >= 64) if hh else (lane < 64), 1.0, 0.0)


def _pick_lane(packed, lane, which):
    return jnp.sum(jnp.where(lane == which, packed, 0.0), axis=1, keepdims=True)


def _sb_bwd_call(kstart, q, k, v, do, rfin):
    s = q.shape[0]
    bq = _row_tile(s, 256)
    nq = s // bq
    nh = SB_GROUP
    width = nh * 64

    def body(ks_ref, q_ref, k_ref, v_ref, do_ref, r_ref, dq_ref, dk_ref, dv_ref):
        hp, i = pl.program_id(0), pl.program_id(1)

        @pl.when(i == 0)
        def _():
            dk_ref[...] = jnp.zeros_like(dk_ref)
            dv_ref[...] = jnp.zeros_like(dv_ref)

        lane = lax.broadcasted_iota(I32, (bq, LANES), 1)
        row = lax.broadcasted_iota(I32, (bq, bq), 0)
        col = lax.broadcasted_iota(I32, (bq, bq), 1)
        upto = jnp.where(row <= col, 1.0, 0.0).astype(BF16)
        before = jnp.where(row < col, 1.0, 0.0).astype(BF16)
        masks = [_head_mask(lane, hh).astype(BF16) for hh in range(2)]
        qms = [q_ref[:, _pair(hh)] * jnp.asarray(SB_SCALE, BF16) * masks[hh % 2] for hh in range(nh)]
        doms = [do_ref[:, _pair(hh)] * masks[hh % 2] for hh in range(nh)]
        totals = [_pick_lane(r_ref[:, _pair(hh)], lane, 64 * (hh % 2)) for hh in range(nh)]
        strict = col < row

        def walk(blocks, state):
            chains = [(kb, diagonal, hh) for kb, diagonal in blocks for hh in range(nh)]
            keys = lambda kb: pl.ds(pl.multiple_of(kb * bq, bq), bq)
            cut = lambda x, diagonal: jnp.where(strict, x, 0.0) if diagonal else x
            zs = [_dot_nt(qms[hh], k_ref[keys(kb), _pair(hh)]) for kb, _, hh in chains]
            dws = [_dot_nt(doms[hh], v_ref[keys(kb), _pair(hh)]) for kb, _, hh in chains]
            pairs = []
            for z, (_, diagonal, _) in zip(zs, chains):
                ls, lk = _log_sigmoid_pair(z)
                pairs.append((ls, cut(lk, diagonal)))
            incls = [_split_dot(lk, upto) for _, lk in pairs]
            pres = [st[0] for st in state]
            ws, gs = [], []
            for (ls, lk), incl, dw, (_, diagonal, hh) in zip(pairs, incls, dws, chains):
                w = cut(jnp.exp(ls + ((totals[hh] - pres[hh]) - incl)), diagonal)
                ws.append(w.astype(BF16))
                gs.append(w * dw)
                pres[hh] = pres[hh] + jnp.sum(lk, axis=1, keepdims=True)
            gsums = [_dot(g.astype(BF16), before) for g in gs]
            gpres = [st[1] for st in state]
            dzs = []
            for (ls, _), g, gsum, (_, diagonal, hh) in zip(pairs, gs, gsums, chains):
                dzs.append(cut(g - jnp.exp(ls) * (g + (gpres[hh] + gsum)), diagonal).astype(BF16))
                gpres[hh] = gpres[hh] + jnp.sum(g, axis=1, keepdims=True)
            dqs = [st[2] for st in state]
            dk_parts, dv_parts = [], []
            for dzb, w, (kb, _, hh) in zip(dzs, ws, chains):
                dk_parts.append(_dot_tn(dzb, qms[hh]))
                dv_parts.append(_dot_tn(w, doms[hh]))
                dqs[hh] = dqs[hh] + _dot(dzb, k_ref[keys(kb), _pair(hh)])
            for b, (kb, _) in enumerate(blocks):
                for pair in range(nh // 2):
                    c0 = b * nh + 2 * pair
                    dk_ref[keys(kb), _pair(2 * pair)] += dk_parts[c0] + dk_parts[c0 + 1]
                    dv_ref[keys(kb), _pair(2 * pair)] += dv_parts[c0] + dv_parts[c0 + 1]
            return tuple(zip(pres, gpres, dqs))

        def finish(state):
            for pair in range(nh // 2):
                both = jnp.where(lane < 64, state[2 * pair][2], state[2 * pair + 1][2])
                dq_ref[:, _pair(2 * pair)] = (both * SB_SCALE).astype(BF16)

        zero = ((jnp.zeros((bq, 1), F32), jnp.zeros((bq, 1), F32), jnp.zeros((bq, LANES), F32)),) * nh

        @pl.when(i == 0)
        def _():
            finish(walk([(0, True)], zero))

        @pl.when(i > 0)
        def _():
            state = lax.fori_loop(ks_ref[hp, i], i - 1, lambda kb, st: walk([(kb, False)], st), zero)
            finish(walk([(i - 1, False), (i, True)], state))

    return pl.pallas_call(
        body, name="sb_bwd",
        grid_spec=pltpu.PrefetchScalarGridSpec(
            num_scalar_prefetch=1, grid=(SB_HEADS // nh, nq),
            in_specs=[pl.BlockSpec((bq, width), lambda h, i, ks: (i, h)),
                      pl.BlockSpec((s, width), lambda h, i, ks: (0, h), pipeline_mode=pl.Buffered(1)),
                      pl.BlockSpec((s, width), lambda h, i, ks: (0, h), pipeline_mode=pl.Buffered(1)),
                      pl.BlockSpec((bq, width), lambda h, i, ks: (i, h)),
                      pl.BlockSpec((bq, width), lambda h, i, ks: (i, h))],
            out_specs=[pl.BlockSpec((bq, width), lambda h, i, ks: (i, h)),
                       pl.BlockSpec((s, width), lambda h, i, ks: (0, h), pipeline_mode=pl.Buffered(1)),
                       pl.BlockSpec((s, width), lambda h, i, ks: (0, h), pipeline_mode=pl.Buffered(1))]),
        out_shape=[jax.ShapeDtypeStruct((s, SB_WIDTH), BF16), jax.ShapeDtypeStruct((s, SB_WIDTH), F32),
                   jax.ShapeDtypeStruct((s, SB_WIDTH), F32)],
        compiler_params=_params(("arbitrary", "arbitrary"), VMEM_LIMIT),
    )(kstart, q, k, v, do, rfin)


def _mla_bwd_call(q, k, v, do, o, lse):
    s = q.shape[0]
    bq = _row_tile(s, MLA_BWD_BQ)
    bk = _row_tile(s, MLA_BWD_BK)
    nq = s // bq
    assert bk % bq == 0
    half = min(bk // 2, MLA_BWD_STEP)
    nsub = bk // half
    assert nsub % 2 == 0

    def body(q_ref, k_ref, v_ref, do_ref, o_ref, lse_ref, dq_ref, dk_ref, dv_ref, dom_ref, s_ref, dp_ref, pb_ref,
             ds_ref):
        i = pl.program_id(1)

        @pl.when(i == 0)
        def _():
            dk_ref[...] = jnp.zeros_like(dk_ref)
            dv_ref[...] = jnp.zeros_like(dv_ref)

        lane = lax.broadcasted_iota(I32, (bq, LANES), 1)
        row = lax.broadcasted_iota(I32, (half, bq), 1)
        col = lax.broadcasted_iota(I32, (half, bq), 0)
        n_full = (i * bq) // bk
        do2 = do_ref[...]
        prod = do2.astype(F32) * o_ref[...]
        ones = jnp.ones((8, LANES), BF16)
        deltas, lses = [], []
        for hh in range(2):
            head = _head_mask(lane, hh)
            dom_ref[hh] = do2 * head.astype(BF16)
            part = prod * head
            hi = part.astype(BF16)
            lo = (part - hi.astype(F32)).astype(BF16)
            deltas.append((_dot_nt(ones, hi) + _dot_nt(ones, lo))[0:1])
            lses.append(lse_ref[hh:hh + 1, :])

        def keys(g):
            return pl.ds(pl.multiple_of(g * half, half), half)

        def heads():
            return [(hh, slice(hh * HEAD_PAD, (hh + 1) * HEAD_PAD)) for hh in range(2)]

        def put_products(g, slot, qlo=0):
            vblk = v_ref[keys(g), :]
            for hh, cols in heads():
                s_ref[slot, hh, :, qlo:] = _dot_nt(k_ref[keys(g), cols], q_ref[qlo:, cols])
                dp_ref[slot, hh, :, qlo:] = _dot_nt(vblk, dom_ref[hh, qlo:, :])

        def add_grads(dqs, g, slot, qlo=0):
            rows = keys(g)
            new, dv_parts = [], []
            for hh, cols in heads():
                ds = ds_ref[slot, hh, :, qlo:]
                dk_ref[rows, cols] += _dot(ds, q_ref[qlo:, cols])
                dv_parts.append(_dot(pb_ref[slot, hh, :, qlo:], dom_ref[hh, qlo:, :]))
                upd = dqs[hh][:, qlo:] + _dot_tn(k_ref[rows, cols], ds)
                new.append(upd if qlo == 0 else jnp.concatenate([dqs[hh][:, :qlo], upd], axis=1))
            dv_ref[rows, :] += dv_parts[0] + dv_parts[1]
            return tuple(new)

        def substep(g, slot, dqs, masked, prefetch, qlo=0, next_qlo=0, prev_qlo=0):
            if prefetch:
                put_products(g + 1, 1 - slot, next_qlo)
            dqs = add_grads(dqs, jnp.maximum(g - 1, 0), 1 - slot, prev_qlo)
            for hh, _ in heads():
                p = jnp.exp2(s_ref[slot, hh, :, qlo:] - lses[hh][:, qlo:])
                if masked:
                    p = jnp.where(col[:, qlo:] + g * half <= row[:, qlo:] + i * bq, p, 0.0)
                ds_ref[slot, hh, :, qlo:] = (p * (dp_ref[slot, hh, :, qlo:] - deltas[hh][:, qlo:])).astype(BF16)
                pb_ref[slot, hh, :, qlo:] = p.astype(BF16)
            return dqs

        def first_query(t, masked):
            return t * half if (masked and bk == bq and 0 <= t < nsub) else 0

        def chunk(kb, dqs, masked):
            for t in range(nsub):
                last = masked and t == nsub - 1
                dqs = substep(nsub * kb + t, t % 2, dqs, masked, not last, first_query(t, masked),
                              first_query(t + 1, masked), first_query(t - 1, masked))
            return dqs

        ds_ref[1] = jnp.zeros_like(ds_ref[1])
        pb_ref[1] = jnp.zeros_like(pb_ref[1])
        put_products(0, 0)
        zero = jnp.zeros((HEAD_PAD, bq), F32)
        dqs = lax.fori_loop(0, n_full, lambda kb, dqs: chunk(kb, dqs, False), (zero, zero))
        dqs = chunk(n_full, dqs, True)
        dqs = add_grads(dqs, nsub * n_full + nsub - 1, 1, first_query(nsub - 1, True))
        dq_ref[:, :HEAD_PAD] = dqs[0].T * MLA_SCALE
        dq_ref[:, HEAD_PAD:] = dqs[1].T * MLA_SCALE

    return pl.pallas_call(
        body, name="mla_bwd",
        grid=(4, nq),
        in_specs=[pl.BlockSpec((bq, 2 * HEAD_PAD), lambda h, i: (i, h)),
                  pl.BlockSpec((s, 2 * HEAD_PAD), lambda h, i: (0, h)),
                  pl.BlockSpec((s, LANES), lambda h, i: (0, h)),
                  pl.BlockSpec((bq, LANES), lambda h, i: (i, h)),
                  pl.BlockSpec((bq, LANES), lambda h, i: (i, h)),
                  pl.BlockSpec((None, 8, bq), lambda h, i: (h, 0, i))],
        out_specs=[pl.BlockSpec((bq, 2 * HEAD_PAD), lambda h, i: (i, h)),
                   pl.BlockSpec((s, 2 * HEAD_PAD), lambda h, i: (0, h), pipeline_mode=pl.Buffered(1)),
                   pl.BlockSpec((s, LANES), lambda h, i: (0, h), pipeline_mode=pl.Buffered(1))],
        out_shape=[jax.ShapeDtypeStruct((s, MLA_PAD_WIDTH), F32), jax.ShapeDtypeStruct((s, MLA_PAD_WIDTH), F32),
                   jax.ShapeDtypeStruct((s, MLA_WIDTH), F32)],
        scratch_shapes=[pltpu.VMEM((2, bq, LANES), BF16),
                        pltpu.VMEM((2, 2, half, bq), F32), pltpu.VMEM((2, 2, half, bq), F32),
                        pltpu.VMEM((2, 2, half, bq), BF16), pltpu.VMEM((2, 2, half, bq), BF16)],
        compiler_params=_params(("arbitrary", "arbitrary"), VMEM_LIMIT),
    )(q, k, v, do, o, lse)


def _rms_bwd(d_out, inp, r, weight, n):
    normed = inp * r
    gw = d_out * weight
    d_in = r * (gw - normed * (jnp.sum(gw * normed, axis=-1, keepdims=True) * (1.0 / n)))
    return d_in, d_out * normed


def _mla_prep_bwd_call(dq, dk, dv, q0, k0, cqn, ckvn, c_q, c_kv, cos_t, sin_t,
                       q_lora_norm, kv_lora_norm, qhn_pad, khn_pad, w_uq_bf, w_uk_bf, w_uv_bf):
    s = dq.shape[0]
    tm = _row_tile(s, 256)

    def body(dq_ref, dk_ref, dv_ref, q0_ref, k0_ref, cqn_ref, ckvn_ref, cq_ref, ckv_ref,
             cos_ref, sin_ref, qln_ref, kvln_ref, qhn_ref, khn_ref, wuq_ref, wuk_ref, wuv_ref,
             dcq_ref, dckv_ref, dkr_ref, gwuq_ref, gwuk_ref, gwuv_ref, gqln_ref, gkvln_ref, gqhn_ref, gkhn_ref,
             dq0_ref, dk0_ref, tmp_ref):
        @pl.when(pl.program_id(0) == 0)
        def _():
            for ref in (gwuq_ref, gwuk_ref, gwuv_ref, gqln_ref, gkvln_ref, gqhn_ref, gkhn_ref):
                ref[...] = jnp.zeros_like(ref)

        cos_t, sin_t = cos_ref[...], sin_ref[...]
        lane = lax.broadcasted_iota(I32, (tm, LANES), 1)
        rope_lanes = jnp.logical_or(lane < ROPE_HALF, jnp.logical_and(lane >= 64, lane < 64 + ROPE_HALF))
        heads = [slice(h * HEAD_PAD, (h + 1) * HEAD_PAD) for h in range(MLA_HEADS)]

        def head_norm_bwd(d_ref, x0_ref, w_ref, out_ref, scale):
            w = w_ref[...]
            inv = [lax.rsqrt(jnp.sum(x0_ref[:, cols] * x0_ref[:, cols], axis=-1, keepdims=True)
                             * (1.0 / MLA_QK_DIM) + EPS) for cols in heads]
            for cols in heads:
                tmp_ref[:, cols] = _rope_adjoint(d_ref[:, cols] * scale, cos_t, sin_t)
            dots = [jnp.sum(tmp_ref[:, cols] * w * (x0_ref[:, cols] * r), axis=-1, keepdims=True)
                    for cols, r in zip(heads, inv)]
            g_w = jnp.zeros((1, LANES), F32)
            rope_sum = jnp.zeros((tm, LANES), F32)
            for cols, r, dot in zip(heads, inv, dots):
                normed = x0_ref[:, cols] * r
                d_n = tmp_ref[:, cols]
                d_x0 = r * (d_n * w - normed * (dot * (1.0 / MLA_QK_DIM)))
                out_ref[:, cols] = d_x0.astype(BF16)
                g_w = g_w + jnp.sum(d_n * normed, axis=0, keepdims=True)
                rope_sum = rope_sum + jnp.where(rope_lanes, d_x0, 0.0)
            return g_w, rope_sum

        g_qhn, _ = head_norm_bwd(dq_ref, q0_ref, qhn_ref, dq0_ref, 1.0)
        g_khn, d_kr = head_norm_bwd(dk_ref, k0_ref, khn_ref, dk0_ref, LN2)
        cqn, ckvn = cqn_ref[...], ckvn_ref[...]
        d_q0b, d_k0b, dvb = dq0_ref[...], dk0_ref[...], dv_ref[...].astype(BF16)
        d_cqn = _dot_nt(d_q0b, wuq_ref[...])
        gwuq_ref[...] += _dot_tn(cqn, d_q0b)
        d_ckvn = _dot_nt(d_k0b, wuk_ref[...]) + _dot_nt(dvb, wuv_ref[...])
        gwuk_ref[...] += _dot_tn(ckvn, d_k0b)
        gwuv_ref[...] += _dot_tn(ckvn, dvb)
        gqhn_ref[...] += g_qhn
        gkhn_ref[...] += g_khn
        dkr_ref[...] = d_kr.astype(BF16)
        cq = cq_ref[...]
        rcq = lax.rsqrt(jnp.mean(cq * cq, axis=-1, keepdims=True) + EPS)
        d_cq, gl = _rms_bwd(d_cqn, cq, rcq, qln_ref[...], Q_LORA_RANK)
        dcq_ref[...] = d_cq.astype(BF16)
        gqln_ref[...] += jnp.sum(gl, axis=0, keepdims=True)
        ckv = ckv_ref[...]
        rckv = lax.rsqrt(jnp.mean(ckv * ckv, axis=-1, keepdims=True) + EPS)
        d_ckv, gl = _rms_bwd(d_ckvn, ckv, rckv, kvln_ref[...], KV_LORA_RANK)
        dckv_ref[...] = d_ckv.astype(BF16)
        gkvln_ref[...] += jnp.sum(gl, axis=0, keepdims=True)

    return pl.pallas_call(
        body, name="mla_prep_bwd",
        grid=(s // tm,),
        in_specs=[_rows(tm, MLA_PAD_WIDTH), _rows(tm, MLA_PAD_WIDTH), _rows(tm, MLA_WIDTH),
                  _rows(tm, MLA_PAD_WIDTH), _rows(tm, MLA_PAD_WIDTH),
                  _rows(tm, Q_LORA_RANK), _rows(tm, KV_LORA_RANK), _rows(tm, Q_LORA_RANK), _rows(tm, KV_LORA_RANK),
                  _rows(tm, LANES), _rows(tm, LANES),
                  _full((1, Q_LORA_RANK)), _full((1, KV_LORA_RANK)), _full((1, LANES)), _full((1, LANES)),
                  _full((Q_LORA_RANK, MLA_PAD_WIDTH)), _full((KV_LORA_RANK, MLA_PAD_WIDTH)),
                  _full((KV_LORA_RANK, MLA_WIDTH))],
        out_specs=[_rows(tm, Q_LORA_RANK), _rows(tm, KV_LORA_RANK), _rows(tm, LANES),
                   _full((Q_LORA_RANK, MLA_PAD_WIDTH)), _full((KV_LORA_RANK, MLA_PAD_WIDTH)),
                   _full((KV_LORA_RANK, MLA_WIDTH)),
                   _full((1, Q_LORA_RANK)), _full((1, KV_LORA_RANK)), _full((1, LANES)), _full((1, LANES))],
        out_shape=[jax.ShapeDtypeStruct((s, Q_LORA_RANK), BF16), jax.ShapeDtypeStruct((s, KV_LORA_RANK), BF16),
                   jax.ShapeDtypeStruct((s, LANES), BF16),
                   jax.ShapeDtypeStruct((Q_LORA_RANK, MLA_PAD_WIDTH), F32),
                   jax.ShapeDtypeStruct((KV_LORA_RANK, MLA_PAD_WIDTH), F32),
                   jax.ShapeDtypeStruct((KV_LORA_RANK, MLA_WIDTH), F32),
                   jax.ShapeDtypeStruct((1, Q_LORA_RANK), F32), jax.ShapeDtypeStruct((1, KV_LORA_RANK), F32),
                   jax.ShapeDtypeStruct((1, LANES), F32), jax.ShapeDtypeStruct((1, LANES), F32)],
        scratch_shapes=[pltpu.VMEM((tm, MLA_PAD_WIDTH), BF16), pltpu.VMEM((tm, MLA_PAD_WIDTH), BF16),
                        pltpu.VMEM((tm, MLA_PAD_WIDTH), F32)],
        compiler_params=_params(("arbitrary",), VMEM_LIMIT),
    )(dq, dk, dv, q0, k0, cqn, ckvn, c_q, c_kv, cos_t, sin_t,
      q_lora_norm, kv_lora_norm, qhn_pad, khn_pad, w_uq_bf, w_uk_bf, w_uv_bf)


def _dh_call(pieces, hb, x, dy, ada, norm_w, w_in_bf):
    s = x.shape[0]
    tm = _row_tile(s, 256)
    widths = [p.shape[1] for p in pieces]
    offsets = [sum(widths[:j]) for j in range(len(widths))]
    assert offsets[-1] + widths[-1] == IN_COLS_PAD
    n = len(pieces)

    def body(*refs):
        p_refs = refs[:n]
        (hb_ref, x_ref, dy_ref, sh_ref, sc_ref, nw_ref, w_ref, gx_ref, gw_ref, dsh_ref, dsc_ref, gnw_ref,
         dp_ref) = refs[n:]

        @pl.when(pl.program_id(0) == 0)
        def _():
            gw_ref[...] = jnp.zeros_like(gw_ref)
            dsh_ref[...] = jnp.zeros_like(dsh_ref)
            dsc_ref[...] = jnp.zeros_like(dsc_ref)
            gnw_ref[...] = jnp.zeros_like(gnw_ref)

        for p_ref, c0, width in zip(p_refs, offsets, widths):
            dp_ref[:, c0:c0 + width] = p_ref[...].astype(BF16)
        gw_ref[...] += _dot_tn(hb_ref[...], dp_ref[...])
        dh = _dot_nt(dp_ref[...], w_ref[...])
        xx = x_ref[...]
        r0 = lax.rsqrt(jnp.mean(xx * xx, axis=-1, keepdims=True) + EPS)
        xn = xx * r0
        nw = nw_ref[...]
        dsh_ref[...] += jnp.sum(dh, axis=0, keepdims=True)
        dsc_ref[...] += jnp.sum(dh * (xn * nw), axis=0, keepdims=True)
        dn = dh * (1.0 + sc_ref[...])
        gnw_ref[...] += jnp.sum(dn * xn, axis=0, keepdims=True)
        dxn = dn * nw
        gx_ref[...] = dy_ref[...] + r0 * (dxn - xn * jnp.mean(dxn * xn, axis=-1, keepdims=True))

    return pl.pallas_call(
        body, name="in_proj_bwd",
        grid=(s // tm,),
        in_specs=[_rows(tm, w) for w in widths]
        + [_rows(tm, D_MODEL), _rows(tm, D_MODEL), _rows(tm, D_MODEL),
           _ada_part(0), _ada_part(1), _full((1, D_MODEL)),
           pl.BlockSpec((D_MODEL, IN_COLS_PAD), lambda i: (0, 0), pipeline_mode=pl.Buffered(1))],
        out_specs=[_rows(tm, D_MODEL),
                   pl.BlockSpec((D_MODEL, IN_COLS_PAD), lambda i: (0, 0), pipeline_mode=pl.Buffered(1)),
                   _full((1, D_MODEL)), _full((1, D_MODEL)), _full((1, D_MODEL))],
        out_shape=[jax.ShapeDtypeStruct((s, D_MODEL), F32), jax.ShapeDtypeStruct((D_MODEL, IN_COLS_PAD), F32),
                   jax.ShapeDtypeStruct((1, D_MODEL), F32), jax.ShapeDtypeStruct((1, D_MODEL), F32),
                   jax.ShapeDtypeStruct((1, D_MODEL), F32)],
        scratch_shapes=[pltpu.VMEM((tm, IN_COLS_PAD), BF16)],
        compiler_params=_params(("arbitrary",), VMEM_LIMIT),
    )(*pieces, hb, x, dy, ada, ada, norm_w, w_in_bf)


def _adamw(g, w, m, v):
    m = ADAM_B1 * m + (1.0 - ADAM_B1) * g
    v = ADAM_B2 * v + (1.0 - ADAM_B2) * (g * g)
    m_hat = m / (1.0 - ADAM_B1 ** ADAM_STEP)
    v_hat = v / (1.0 - ADAM_B2 ** ADAM_STEP)
    delta = -ADAM_LR * (m_hat / (jnp.sqrt(v_hat) + ADAM_EPS) + ADAM_WD * w)
    return delta, m, v


def _adam_shard_call(name, own, sib, w, m, v):
    r, c = w.shape
    tr = r if r <= 512 else 256

    def body(own_ref, sib_ref, w_ref, m_ref, v_ref, g_ref, d_ref, nm_ref, nv_ref):
        a = ((own_ref[0].astype(F32) + own_ref[1].astype(F32)) + own_ref[2].astype(F32)) + own_ref[3].astype(F32)
        b = ((sib_ref[0].astype(F32) + sib_ref[1].astype(F32)) + sib_ref[2].astype(F32)) + sib_ref[3].astype(F32)
        g = a + b
        g_ref[...] = g
        d_ref[...], nm_ref[...], nv_ref[...] = _adamw(g, w_ref[...], m_ref[...], v_ref[...])

    part = pl.BlockSpec((4, tr, c), lambda i: (0, i, 0))
    blk = pl.BlockSpec((tr, c), lambda i: (i, 0))
    return pl.pallas_call(
        body, name=name,
        grid=(r // tr,),
        in_specs=[part, part, blk, blk, blk],
        out_specs=[blk] * 4,
        out_shape=[jax.ShapeDtypeStruct((r, c), F32)] * 4,
        compiler_params=_params(("arbitrary",), VMEM_LIMIT),
    )(own, sib, w, m, v)


def _adam_ada_call(c_all, d_all, w, m, v):
    r, c = w.shape
    tr = 256

    def body(c_ref, d_ref, w_ref, m_ref, v_ref, g_ref, dl_ref, nm_ref, nv_ref):
        cc = c_ref[...]
        sc = cc * _sigmoid(cc)
        dd = d_ref[...]
        sc_hi = sc.astype(BF16)
        sc_lo = (sc - sc_hi.astype(F32)).astype(BF16)
        dd_hi = dd.astype(BF16)
        dd_lo = (dd - dd_hi.astype(F32)).astype(BF16)
        g = _dot_tn(sc_hi, dd_hi) + (_dot_tn(sc_hi, dd_lo) + _dot_tn(sc_lo, dd_hi))
        g_ref[...] = g
        dl_ref[...], nm_ref[...], nv_ref[...] = _adamw(g, w_ref[...], m_ref[...], v_ref[...])

    blk = pl.BlockSpec((tr, c), lambda i: (i, 0))
    return pl.pallas_call(
        body, name="adam_w_ada",
        grid=(r // tr,),
        in_specs=[pl.BlockSpec((16, tr), lambda i: (0, i)), pl.BlockSpec((16, c), lambda i: (0, 0)), blk, blk, blk],
        out_specs=[blk] * 4,
        out_shape=[jax.ShapeDtypeStruct((r, c), F32)] * 4,
        compiler_params=_params(("arbitrary",), VMEM_LIMIT),
    )(c_all, d_all, w, m, v)


def _adam_vectors_call(packs, offsets, vectors):
    nv = len(vectors)

    def body(*refs):
        p_ref, ins, outs = refs[0], refs[1:1 + 3 * nv], refs[1 + 3 * nv:]
        for j, off in enumerate(offsets):
            n = ins[3 * j].shape[1]
            span = -(-n // LANES) * LANES
            g = p_ref[0, :, off:off + span]
            for b in range(1, 8):
                g = g + p_ref[b, :, off:off + span]
            g = g[:, :n]
            outs[j][...] = g
            outs[nv + j][...], outs[2 * nv + j][...], outs[3 * nv + j][...] = _adamw(
                g, ins[3 * j][...], ins[3 * j + 1][...], ins[3 * j + 2][...])

    flat = [a for t in vectors for a in t]
    res = pl.pallas_call(
        body, name="adam_vectors",
        out_shape=[jax.ShapeDtypeStruct(t[0].shape, F32) for _ in range(4) for t in vectors],
    )(packs, *flat)
    return [res[k * nv:(k + 1) * nv] for k in range(4)]


ROPE_HALF = MLA_ROPE_DIM // 2
NOPE_A = MLA_NOPE_DIM - ROPE_HALF


def _zeros_like_lanes(t, n):
    return jnp.zeros(t.shape[:-1] + (n,), t.dtype)


def _to_head_lanes(t):
    nope, rope = t[..., :MLA_NOPE_DIM], t[..., MLA_NOPE_DIM:]
    return jnp.concatenate([rope[..., :ROPE_HALF], nope[..., :NOPE_A], rope[..., ROPE_HALF:], nope[..., NOPE_A:],
                            _zeros_like_lanes(t, HEAD_PAD - MLA_QK_DIM)], axis=-1)


def _from_head_lanes(t):
    return jnp.concatenate([t[..., ROPE_HALF:MLA_NOPE_DIM], t[..., 64 + ROPE_HALF:MLA_QK_DIM],
                            t[..., :ROPE_HALF], t[..., 64:64 + ROPE_HALF]], axis=-1)


def _nope_to_head_lanes(t):
    return jnp.concatenate([_zeros_like_lanes(t, ROPE_HALF), t[..., :NOPE_A], _zeros_like_lanes(t, ROPE_HALF),
                            t[..., NOPE_A:], _zeros_like_lanes(t, HEAD_PAD - MLA_QK_DIM)], axis=-1)


def _rope_to_head_lanes(t):
    return jnp.concatenate([t[..., :ROPE_HALF], _zeros_like_lanes(t, 64 - ROPE_HALF), t[..., ROPE_HALF:],
                            _zeros_like_lanes(t, 64 - ROPE_HALF)], axis=-1)


def _rope_tables(positions):
    inv_freq = (ROPE_THETA ** (-jnp.arange(0, MLA_ROPE_DIM, 2, dtype=F32) / MLA_ROPE_DIM))[None]
    signed = _rope_to_head_lanes(jnp.concatenate([-inv_freq, inv_freq], axis=1))
    ang = positions.astype(F32)[:, None] * signed
    return jnp.cos(ang), jnp.sin(ang)


def _unshard_cols(g):
    return jnp.transpose(g, (1, 0, 2)).reshape(g.shape[1], 4 * g.shape[2])


def _shard_cols(g):
    r, c4 = g.shape
    return jnp.transpose(g.reshape(r, 4, c4 // 4), (1, 0, 2))


def kernel(x, c, positions, w_ada, b_ada, norm_w, w_in, q_lora_norm, w_uq, kv_lora_norm, w_ukv, q_head_norm, k_head_norm, w_out, loss_target, m_w_ada, m_b_ada, m_norm_w, m_w_in, m_q_lora_norm, m_w_uq, m_kv_lora_norm, m_w_ukv, m_q_head_norm, m_k_head_norm, m_w_out, v_w_ada, v_b_ada, v_norm_w, v_w_in, v_q_lora_norm, v_w_uq, v_kv_lora_norm, v_w_ukv, v_q_head_norm, v_k_head_norm, v_w_out):
    chip = 2 * lax.axis_index("x") + lax.axis_index("y")
    me8 = 2 * chip + lax.axis_index("c")
    ada_cols = w_ada.shape[2]
    c_all = _allgather_rows_call(c)[:, 0, :]
    ada_part = _ada_call(c_all, w_ada[0], lax.dynamic_slice_in_dim(b_ada, chip * ada_cols, ada_cols, axis=1))
    ada_g, win_g, wuq_g, wukv_g, wout_g = _gather_call(
        [ada_part[None]] + [w.astype(BF16) for w in (w_in, w_uq, w_ukv, w_out)], [False, True, True, True, True])
    ada = lax.dynamic_slice_in_dim(ada_g, me8, 1, axis=1).reshape(1, 4 * ada_cols)
    (sq_sum, grad_x, g_w_in, g_w_uq, g_w_ukv, g_w_out, d_ada, g_norm_w, g_qln, g_kvln, g_qhn, g_khn) = _local_step(
        x[0], ada, positions[0], loss_target[0], norm_w, win_g,
        q_lora_norm, _unshard_cols(wuq_g), kv_lora_norm, _unshard_cols(wukv_g), q_head_norm, k_head_norm,
        wout_g.reshape(D_MODEL, D_MODEL))
    loss = lax.psum(0.5 * sq_sum / D_MODEL, ("x", "y", "c"))

    grads = [g.astype(BF16) for g in (g_w_in, _shard_cols(g_w_uq), _shard_cols(g_w_ukv),
                                      g_w_out.reshape(4, D_MODEL // 4, D_MODEL))]
    pieces = [d_ada, g_norm_w, g_qln, g_kvln, g_qhn, g_khn]
    spans = [-(-p.shape[1] // LANES) * LANES for p in pieces]
    starts = [sum(spans[:j]) for j in range(len(spans))]
    small = jnp.concatenate([jnp.pad(p, ((0, 0), (0, sp - p.shape[1]))) for p, sp in zip(pieces, spans)], axis=1)
    own, sib, packs = _exchange_call(grads, small)

    names = ["adam_w_in", "adam_w_uq", "adam_w_ukv", "adam_w_out"]
    shard_w = [(w_in, m_w_in, v_w_in), (w_uq, m_w_uq, v_w_uq), (w_ukv, m_w_ukv, v_w_ukv),
               (w_out, m_w_out, v_w_out)]
    res = {}
    for name, o_g, s_g, (w, m, v) in zip(names, own, sib, shard_w):
        res[name] = _adam_shard_call(name, o_g, s_g, w[0], m[0], v[0])
    d_all = lax.dynamic_slice_in_dim(packs[:, 0, :], starts[0] + chip * ada_cols, ada_cols, axis=1)
    res_ada = _adam_ada_call(jnp.pad(c_all, ((0, 8), (0, 0))), jnp.pad(d_all, ((0, 8), (0, 0))),
                             w_ada[0], m_w_ada[0], v_w_ada[0])
    vectors = [(b_ada, m_b_ada, v_b_ada), (norm_w, m_norm_w, v_norm_w), (q_lora_norm, m_q_lora_norm, v_q_lora_norm),
               (kv_lora_norm, m_kv_lora_norm, v_kv_lora_norm), (q_head_norm, m_q_head_norm, v_q_head_norm),
               (k_head_norm, m_k_head_norm, v_k_head_norm)]
    vec_out = _adam_vectors_call(packs, starts, vectors)

    def ordered(kind):
        big = lambda name: res[name][kind][None]
        return [res_ada[kind][None], vec_out[kind][0], vec_out[kind][1], big("adam_w_in"), vec_out[kind][2],
                big("adam_w_uq"), vec_out[kind][3], big("adam_w_ukv"), vec_out[kind][4], vec_out[kind][5],
                big("adam_w_out")]

    return (loss, grad_x[None], *ordered(0), *ordered(1), *ordered(2), *ordered(3))


IN_SHARD = 808
CKV_TAIL = 2688 - 3 * IN_SHARD


def _local_step(x2, ada, positions, tgt, norm_w, w_in_shards, q_lora_norm, w_uq_full,
                kv_lora_norm, w_ukv_full, q_head_norm, k_head_norm, w_out_full):
    last = w_in_shards[3]
    w_in_bf = jnp.concatenate(
        [w_in_shards[0], w_in_shards[1], w_in_shards[2], last[:, :CKV_TAIL], last[:, CKV_TAIL + MLA_ROPE_DIM:],
         _rope_to_head_lanes(last[:, CKV_TAIL:CKV_TAIL + MLA_ROPE_DIM])], axis=1).astype(BF16)
    w_uq_bf = _to_head_lanes(w_uq_full.reshape(Q_LORA_RANK, MLA_HEADS, MLA_QK_DIM)).reshape(
        Q_LORA_RANK, MLA_PAD_WIDTH).astype(BF16)
    w_ukv_heads = w_ukv_full.reshape(KV_LORA_RANK, MLA_HEADS, 2 * MLA_NOPE_DIM)
    w_uk_bf = _nope_to_head_lanes(w_ukv_heads[:, :, :MLA_NOPE_DIM]).reshape(KV_LORA_RANK, MLA_PAD_WIDTH).astype(BF16)
    w_uv_bf = w_ukv_heads[:, :, MLA_NOPE_DIM:].reshape(KV_LORA_RANK, MLA_WIDTH).astype(BF16)
    w_out_bf = w_out_full.astype(BF16)
    qhn_pad, khn_pad = _to_head_lanes(q_head_norm), _to_head_lanes(k_head_norm)
    cos_t, sin_t = _rope_tables(positions)

    hb, q_sb, k_sb, v_sb, g_sb, c_q, c_kv, g_mla, k_rope = _pre_call(x2, ada, norm_w, w_in_bf)
    q_m, k_m, v_m, cqn, ckvn, q0, k0 = _mla_prep_call(
        c_q, c_kv, k_rope, cos_t, sin_t, q_lora_norm, kv_lora_norm, qhn_pad, khn_pad,
        w_uq_bf, w_uk_bf, w_uv_bf)
    o_sb, r_sb, kstart = _sb_fwd_call(q_sb, k_sb, v_sb)
    o_mla, lse = _mla_fwd_call(q_m, k_m, v_m)
    do_sb, do_mla, dg_sb, dg_mla, dy, g_w_out, d_gate, sq = _out_call(
        o_sb, g_sb, o_mla, g_mla, x2, tgt, ada, w_out_bf)

    dq_sb, dk_sb, dv_sb = _sb_bwd_call(kstart, q_sb, k_sb, v_sb, do_sb, r_sb)
    dq_m, dk_m, dv_m = _mla_bwd_call(q_m, k_m, v_m, do_mla, o_mla, lse)
    (d_cq, d_ckv, d_kr, g_wuq_pad, g_wuk_pad, g_wuv, g_qln, g_kvln, g_qhn, g_khn) = _mla_prep_bwd_call(
        dq_m, dk_m, dv_m, q0, k0, cqn, ckvn, c_q, c_kv, cos_t, sin_t,
        q_lora_norm, kv_lora_norm, qhn_pad, khn_pad, w_uq_bf, w_uk_bf, w_uv_bf)
    grad_x, g_win_pad, d_shift, d_scale, g_norm_w = _dh_call(
        [dq_sb, dk_sb, dv_sb, dg_sb, d_cq, d_ckv, dg_mla, d_kr], hb, x2, dy, ada, norm_w, w_in_bf)

    g_kr = g_win_pad[:, C_KR:]
    g_last = jnp.concatenate([g_win_pad[:, 3 * IN_SHARD:C_GM], g_kr[:, :ROPE_HALF], g_kr[:, 64:64 + ROPE_HALF],
                              g_win_pad[:, C_GM:C_KR]], axis=1)
    g_w_in = jnp.stack([g_win_pad[:, j * IN_SHARD:(j + 1) * IN_SHARD] for j in range(3)] + [g_last])
    g_w_uq = _from_head_lanes(g_wuq_pad.reshape(Q_LORA_RANK, MLA_HEADS, HEAD_PAD)).reshape(Q_LORA_RANK, -1)
    g_w_ukv = jnp.concatenate(
        [_from_head_lanes(g_wuk_pad.reshape(KV_LORA_RANK, MLA_HEADS, HEAD_PAD))[:, :, :MLA_NOPE_DIM],
         g_wuv.reshape(KV_LORA_RANK, MLA_HEADS, MLA_NOPE_DIM)], axis=2).reshape(KV_LORA_RANK, -1)
    d_ada = jnp.concatenate([d_shift, d_scale, d_gate], axis=1)
    return (jnp.sum(sq), grad_x, g_w_in, g_w_uq, g_w_ukv, g_w_out, d_ada, g_norm_w, g_qln, g_kvln,
            _from_head_lanes(g_qhn), _from_head_lanes(g_khn))
```

```python
import math

import jax
import jax.numpy as jnp
from jax import lax
from jax.experimental import pallas as pl
from jax.experimental.pallas import tpu as pltpu

F32 = jnp.float32
BF16 = jnp.bfloat16
I32 = jnp.int32

D_MODEL = 1024
SB_HEADS = 8
SB_WIDTH = 512
MLA_HEADS = 8
MLA_QK_DIM = 96
MLA_NOPE_DIM = 64
MLA_ROPE_DIM = 32
MLA_WIDTH = 512
Q_LORA_RANK = 384
KV_LORA_RANK = 256
ROPE_THETA = 10000.0
EPS = 1e-6
LANES = 128
HALF_LANES = LANES // 2
HEAD_PAD = 128
MLA_PAD_WIDTH = MLA_HEADS * HEAD_PAD

C_Q, C_K, C_V, C_G = 0, 512, 1024, 1536
C_CQ, C_CKV, C_GM, C_KR = 2048, 2432, 2688, 3200
IN_COLS_PAD = 3328

ADAM_LR = 0.001
ADAM_B1 = 0.9
ADAM_B2 = 0.999
ADAM_EPS = 1e-08
ADAM_WD = 0.01
ADAM_STEP = 10

SB_SCALE = 0.125
SB_GROUP = 4
MLA_SCALE = 1.0 / math.sqrt(MLA_QK_DIM)
LN2 = math.log(2.0)
MLA_SCALE_LOG2 = MLA_SCALE / LN2
MLA_BQ = 1024
MLA_BWD_BQ = 1024
MLA_BK = 1024
MLA_BWD_BK = 1024
MLA_STEP = 512
MLA_BWD_STEP = 256
SB_DEAD = -104.0
MASK_NEG = -1e30

VMEM_LIMIT = 56 * 1024 * 1024
MESH = pl.DeviceIdType.MESH


def _dot(a, b):
    return jnp.dot(a, b, preferred_element_type=F32)


def _dot_nt(a, b):
    return lax.dot_general(a, b, (((1,), (1,)), ((), ())), preferred_element_type=F32)


def _dot_tn(a, b):
    return lax.dot_general(a, b, (((0,), (0,)), ((), ())), preferred_element_type=F32)


def _sigmoid(x):
    return 1.0 / (1.0 + jnp.exp(-x))


def _split_dot(a, m):
    hi = a.astype(BF16)
    lo = (a - hi.astype(F32)).astype(BF16)
    return _dot(hi, m) + _dot(lo, m)


def _params(sem, vmem=None):
    return pltpu.CompilerParams(dimension_semantics=sem, vmem_limit_bytes=vmem)


def _row_tile(s, want):
    return min(want, s)


def _hbm_spec():
    return pl.BlockSpec(memory_space=pltpu.HBM)


def _allgather_rows_call(row):
    def body(in_ref, out_ref, send_sems, recv_sems, loc_sem):
        x, y, c = lax.axis_index("x"), lax.axis_index("y"), lax.axis_index("c")
        flips = [(fx, fy, fc) for fx in (0, 1) for fy in (0, 1) for fc in (0, 1)][1:]

        def peer(r):
            fx, fy, fc = flips[r]
            return ((1 - x) if fx else x, (1 - y) if fy else y, (1 - c) if fc else c)

        def copy(r, slot):
            return pltpu.make_async_remote_copy(
                src_ref=in_ref, dst_ref=out_ref.at[slot], send_sem=send_sems.at[r], recv_sem=recv_sems.at[r],
                device_id=peer(r), device_id_type=MESH)

        local = pltpu.make_async_copy(in_ref, out_ref.at[4 * x + 2 * y + c], loc_sem)
        local.start()
        sends = [copy(r, 4 * x + 2 * y + c) for r in range(7)]
        for cp in sends:
            cp.start()
        for r in range(7):
            px, py, pc = peer(r)
            copy(r, 4 * px + 2 * py + pc).wait_recv()
        for cp in sends:
            cp.wait_send()
        local.wait()

    return pl.pallas_call(
        body, name="gather_rows",
        out_shape=jax.ShapeDtypeStruct((8,) + row.shape, row.dtype),
        in_specs=[_hbm_spec()], out_specs=_hbm_spec(),
        scratch_shapes=[pltpu.SemaphoreType.DMA((7,)), pltpu.SemaphoreType.DMA((7,)), pltpu.SemaphoreType.DMA],
    )(row)


def _gather_call(shards, split):
    n = len(shards)
    halves = [s.shape[1] // 2 for s in shards]

    def body(*refs):
        ins, outs = refs[:n], refs[n:2 * n]
        ici_send, ici_recv, d2d_send, d2d_recv, loc_sems = refs[2 * n:]
        x, y, c = lax.axis_index("x"), lax.axis_index("y"), lax.axis_index("c")
        me = 2 * x + y
        peers = [(1 - x, y), (x, 1 - y), (1 - x, 1 - y)]

        def rows(a, which):
            return pl.ds(pl.multiple_of(which * halves[a], 16), halves[a])

        def ici(a, j, slot):
            px, py = peers[j]
            src, dst = ins[a].at[0], outs[a].at[slot]
            if split[a]:
                src, dst = src.at[rows(a, c)], dst.at[rows(a, c)]
            return pltpu.make_async_remote_copy(
                src_ref=src, dst_ref=dst,
                send_sem=ici_send.at[3 * a + j], recv_sem=ici_recv.at[3 * a + j],
                device_id=(px, py, c), device_id_type=MESH)

        def d2d(a, j, which):
            px, py = peers[j]
            piece = outs[a].at[2 * px + py, rows(a, which)]
            return pltpu.make_async_remote_copy(
                src_ref=piece, dst_ref=piece,
                send_sem=d2d_send.at[3 * a + j], recv_sem=d2d_recv.at[3 * a + j],
                device_id=(x, y, 1 - c), device_id_type=MESH)

        local = [pltpu.make_async_copy(ins[a].at[0], outs[a].at[me], loc_sems.at[a]) for a in range(n)]
        for cp in local:
            cp.start()
        sends = [ici(a, j, me) for a in range(n) for j in range(3)]
        for cp in sends:
            cp.start()
        for a in range(n):
            for j in range(3):
                px, py = peers[j]
                ici(a, j, 2 * px + py).wait_recv()
                if split[a]:
                    cp = d2d(a, j, c)
                    cp.start()
                    sends.append(cp)
        for a in range(n):
            for j in range(3):
                if split[a]:
                    d2d(a, j, 1 - c).wait_recv()
        for cp in sends:
            cp.wait_send()
        for cp in local:
            cp.wait()

    return pl.pallas_call(
        body, name="gather_weights",
        out_shape=[jax.ShapeDtypeStruct((4,) + s.shape[1:], s.dtype) for s in shards],
        in_specs=[_hbm_spec() for _ in shards],
        out_specs=[_hbm_spec() for _ in shards],
        scratch_shapes=[pltpu.SemaphoreType.DMA((3 * n,)), pltpu.SemaphoreType.DMA((3 * n,)),
                        pltpu.SemaphoreType.DMA((3 * n,)), pltpu.SemaphoreType.DMA((3 * n,)),
                        pltpu.SemaphoreType.DMA((n,))],
    )(*shards)


def _exchange_call(grads, small):
    n = len(grads)

    def body(*refs):
        g_in, small_in = refs[:n], refs[n]
        own, sib, packs = refs[n + 1:2 * n + 1], refs[2 * n + 1:3 * n + 1], refs[3 * n + 1]
        ici_send, ici_recv, d2d_send, d2d_recv, sm_send, sm_recv, loc_sems = refs[3 * n + 2:]
        x, y, c = lax.axis_index("x"), lax.axis_index("y"), lax.axis_index("c")
        me = 2 * x + y
        me8 = 4 * x + 2 * y + c
        sibling = (x, y, 1 - c)
        peers = [(1 - x, y), (x, 1 - y), (1 - x, 1 - y)]
        flips = [(fx, fy, fc) for fx in (0, 1) for fy in (0, 1) for fc in (0, 1)][1:]

        def ici(a, j, src_slot, dst_slot):
            px, py = peers[j]
            return pltpu.make_async_remote_copy(
                src_ref=g_in[a].at[src_slot], dst_ref=own[a].at[dst_slot],
                send_sem=ici_send.at[3 * a + j], recv_sem=ici_recv.at[3 * a + j],
                device_id=(px, py, c), device_id_type=MESH)

        def d2d(a, rel, chip, src):
            return pltpu.make_async_remote_copy(
                src_ref=src, dst_ref=sib[a].at[chip],
                send_sem=d2d_send.at[4 * a + rel], recv_sem=d2d_recv.at[4 * a + rel],
                device_id=sibling, device_id_type=MESH)

        def flipped(r):
            fx, fy, fc = flips[r]
            return ((1 - x) if fx else x, (1 - y) if fy else y, (1 - c) if fc else c)

        def sm(r, slot):
            return pltpu.make_async_remote_copy(
                src_ref=small_in, dst_ref=packs.at[slot],
                send_sem=sm_send.at[r], recv_sem=sm_recv.at[r],
                device_id=flipped(r), device_id_type=MESH)

        def peer8(r):
            px, py, pc = flipped(r)
            return 4 * px + 2 * py + pc

        local = [pltpu.make_async_copy(g_in[a].at[me], own[a].at[me], loc_sems.at[a]) for a in range(n)]
        local.append(pltpu.make_async_copy(small_in, packs.at[me8], loc_sems.at[n]))
        for cp in local:
            cp.start()
        sends = []
        for r in range(7):
            sends.append(sm(r, me8))
        for a in range(n):
            for j in range(3):
                px, py = peers[j]
                sends.append(ici(a, j, 2 * px + py, me))
        for cp in sends:
            cp.start()
        for a in range(n):
            cp = d2d(a, 0, me, g_in[a].at[me])
            cp.start()
            sends.append(cp)
        for a in range(n):
            for j in range(3):
                px, py = peers[j]
                ici(a, j, me, 2 * px + py).wait_recv()
                cp = d2d(a, 1 + j, 2 * px + py, own[a].at[2 * px + py])
                cp.start()
                sends.append(cp)
        for a in range(n):
            d2d(a, 0, me, g_in[a].at[me]).wait_recv()
            for j in range(3):
                px, py = peers[j]
                d2d(a, 1 + j, 2 * px + py, g_in[a].at[me]).wait_recv()
        for r in range(7):
            sm(r, peer8(r)).wait_recv()
        for cp in sends:
            cp.wait_send()
        for cp in local:
            cp.wait()

    out_shape = ([jax.ShapeDtypeStruct(g.shape, g.dtype) for g in grads] * 2
                 + [jax.ShapeDtypeStruct((8,) + small.shape, small.dtype)])
    res = pl.pallas_call(
        body, name="exchange_grads",
        out_shape=out_shape,
        in_specs=[_hbm_spec() for _ in range(n + 1)],
        out_specs=[_hbm_spec() for _ in range(2 * n + 1)],
        scratch_shapes=[pltpu.SemaphoreType.DMA((3 * n,)), pltpu.SemaphoreType.DMA((3 * n,)),
                        pltpu.SemaphoreType.DMA((4 * n,)), pltpu.SemaphoreType.DMA((4 * n,)),
                        pltpu.SemaphoreType.DMA((7,)), pltpu.SemaphoreType.DMA((7,)),
                        pltpu.SemaphoreType.DMA((n + 1,))],
    )(*grads, small)
    return res[:n], res[n:2 * n], res[2 * n]


def _ada_call(c_all, w_ada_cols, b_ada_cols):
    def body(c_ref, w_ref, b_ref, o_ref):
        cc = c_ref[...]
        o_ref[...] = _dot((cc * _sigmoid(cc)).astype(BF16), w_ref[...].astype(BF16)) + b_ref[...]

    return pl.pallas_call(
        body, name="ada_fwd",
        out_shape=jax.ShapeDtypeStruct((c_all.shape[0], w_ada_cols.shape[1]), F32),
        compiler_params=pltpu.CompilerParams(vmem_limit_bytes=VMEM_LIMIT),
    )(c_all, w_ada_cols, b_ada_cols)


def _ada_part(j):
    return pl.BlockSpec((1, D_MODEL), lambda i: (0, j))


def _full(shape):
    return pl.BlockSpec(shape, lambda i: (0,) * len(shape))


def _rows(tm, width):
    return pl.BlockSpec((tm, width), lambda i: (i, 0))


def _pre_call(x, ada, norm_w, w_in_bf):
    s = x.shape[0]
    tm = _row_tile(s, 512)
    groups = [(C_Q, 512, BF16), (C_K, 512, BF16), (C_V, 512, BF16), (C_G, 512, F32),
              (C_CQ, Q_LORA_RANK, F32), (C_CKV, KV_LORA_RANK, F32), (C_GM, 512, F32), (C_KR, LANES, F32)]

    def body(x_ref, sh_ref, sc_ref, nw_ref, w_ref, hb_ref, *outs):
        xx = x_ref[...]
        r0 = lax.rsqrt(jnp.mean(xx * xx, axis=-1, keepdims=True) + EPS)
        h = (xx * r0 * nw_ref[...]) * (1.0 + sc_ref[...]) + sh_ref[...]
        hb = h.astype(BF16)
        hb_ref[...] = hb
        for (c0, width, dt), o_ref in zip(groups, outs):
            o_ref[...] = _dot(hb, w_ref[:, c0:c0 + width]).astype(dt)

    return pl.pallas_call(
        body, name="pre_proj",
        grid=(s // tm,),
        in_specs=[_rows(tm, D_MODEL), _ada_part(0), _ada_part(1), _full((1, D_MODEL)),
                  _full((D_MODEL, IN_COLS_PAD))],
        out_specs=[_rows(tm, D_MODEL)] + [_rows(tm, w) for _, w, _ in groups],
        out_shape=[jax.ShapeDtypeStruct((s, D_MODEL), BF16)]
        + [jax.ShapeDtypeStruct((s, w), dt) for _, w, dt in groups],
        compiler_params=_params(("arbitrary",), VMEM_LIMIT),
    )(x, ada, ada, norm_w, w_in_bf)


def _rope(t, cos_t, sin_t):
    return t * cos_t + pltpu.roll(t, HALF_LANES, 1) * sin_t


def _rope_adjoint(d, cos_t, sin_t):
    return d * cos_t + pltpu.roll(d * sin_t, HALF_LANES, 1)


def _mla_prep_call(c_q, c_kv, k_rope, cos_t, sin_t, q_lora_norm, kv_lora_norm, qhn_pad, khn_pad,
                   w_uq_bf, w_uk_bf, w_uv_bf):
    s = c_q.shape[0]
    tm = _row_tile(s, 256)

    def body(cq_ref, ckv_ref, kr_ref, cos_ref, sin_ref, qln_ref, kvln_ref, qhn_ref, khn_ref,
             wuq_ref, wuk_ref, wuv_ref, q_ref, k_ref, v_ref, cqn_ref, ckvn_ref, q0_ref, k0_ref):
        cq = cq_ref[...]
        cqn = (cq * lax.rsqrt(jnp.mean(cq * cq, axis=-1, keepdims=True) + EPS) * qln_ref[...]).astype(BF16)
        cqn_ref[...] = cqn
        ckv = ckv_ref[...]
        ckvn = (ckv * lax.rsqrt(jnp.mean(ckv * ckv, axis=-1, keepdims=True) + EPS) * kvln_ref[...]).astype(BF16)
        ckvn_ref[...] = ckvn
        v_ref[...] = _dot(ckvn, wuv_ref[...]).astype(BF16)
        q0_ref[...] = _dot(cqn, wuq_ref[...])
        k0_ref[...] = _dot(ckvn, wuk_ref[...])
        cos_t, sin_t = cos_ref[...], sin_ref[...]
        kr = kr_ref[...]
        heads = [slice(h * HEAD_PAD, (h + 1) * HEAD_PAD) for h in range(MLA_HEADS)]
        for cols in heads:
            k0_ref[:, cols] = k0_ref[:, cols] + kr

        def inv_rms(ref):
            sums = [jnp.sum(ref[:, cols] * ref[:, cols], axis=-1, keepdims=True) for cols in heads]
            return [lax.rsqrt(t * (1.0 / MLA_QK_DIM) + EPS) for t in sums]

        rqs, rks = inv_rms(q0_ref), inv_rms(k0_ref)
        for cols, rq, rk in zip(heads, rqs, rks):
            q_ref[:, cols] = (_rope(q0_ref[:, cols] * rq * qhn_ref[...], cos_t, sin_t) * MLA_SCALE_LOG2).astype(BF16)
            k_ref[:, cols] = _rope(k0_ref[:, cols] * rk * khn_ref[...], cos_t, sin_t).astype(BF16)

    return pl.pallas_call(
        body, name="mla_prep",
        grid=(s // tm,),
        in_specs=[_rows(tm, Q_LORA_RANK), _rows(tm, KV_LORA_RANK), _rows(tm, LANES),
                  _rows(tm, LANES), _rows(tm, LANES),
                  _full((1, Q_LORA_RANK)), _full((1, KV_LORA_RANK)), _full((1, LANES)), _full((1, LANES)),
                  _full((Q_LORA_RANK, MLA_PAD_WIDTH)), _full((KV_LORA_RANK, MLA_PAD_WIDTH)),
                  _full((KV_LORA_RANK, MLA_WIDTH))],
        out_specs=[_rows(tm, MLA_PAD_WIDTH), _rows(tm, MLA_PAD_WIDTH), _rows(tm, MLA_WIDTH),
                   _rows(tm, Q_LORA_RANK), _rows(tm, KV_LORA_RANK),
                   _rows(tm, MLA_PAD_WIDTH), _rows(tm, MLA_PAD_WIDTH)],
        out_shape=[jax.ShapeDtypeStruct((s, MLA_PAD_WIDTH), BF16), jax.ShapeDtypeStruct((s, MLA_PAD_WIDTH), BF16),
                   jax.ShapeDtypeStruct((s, MLA_WIDTH), BF16),
                   jax.ShapeDtypeStruct((s, Q_LORA_RANK), BF16), jax.ShapeDtypeStruct((s, KV_LORA_RANK), BF16),
                   jax.ShapeDtypeStruct((s, MLA_PAD_WIDTH), F32), jax.ShapeDtypeStruct((s, MLA_PAD_WIDTH), F32)],
        compiler_params=_params(("arbitrary",), VMEM_LIMIT),
    )(c_q, c_kv, k_rope, cos_t, sin_t, q_lora_norm, kv_lora_norm, qhn_pad, khn_pad,
      w_uq_bf, w_uk_bf, w_uv_bf)


def _log_sigmoid_pair(z):
    ls = jnp.minimum(z, 0.0) - jnp.log(1.0 + jnp.exp(-jnp.abs(z)))
    return ls, ls - z


def _pair(hh):
    return slice((hh // 2) * LANES, (hh // 2 + 1) * LANES)


def _sb_fwd_call(q, k, v):
    s = q.shape[0]
    bq = _row_tile(s, 256)
    nq = s // bq
    nh = SB_GROUP
    width = nh * HALF_LANES

    def body(q_ref, k_ref, v_ref, o_ref, r_ref, ks_ref):
        hp, i = pl.program_id(0), pl.program_id(1)
        lane = lax.broadcasted_iota(I32, (bq, LANES), 1)
        row = lax.broadcasted_iota(I32, (bq, bq), 0)
        col = lax.broadcasted_iota(I32, (bq, bq), 1)
        strict = col < row
        later = jnp.where(row > col, 1.0, 0.0).astype(BF16)
        masks = [_head_mask(lane, hh).astype(BF16) for hh in range(2)]
        qms = [q_ref[:, _pair(hh)] * jnp.asarray(SB_SCALE, BF16) * masks[hh % 2] for hh in range(nh)]

        def walk(blocks, state):
            chains = [(kb, diagonal, hh) for kb, diagonal in blocks for hh in range(nh)]
            keys = lambda kb: pl.ds(pl.multiple_of(kb * bq, bq), bq)
            zs = [_dot_nt(qms[hh], k_ref[keys(kb), _pair(hh)]) for kb, _, hh in chains]
            pairs = []
            for z, (_, diagonal, _) in zip(zs, chains):
                ls, lk = _log_sigmoid_pair(z)
                pairs.append((ls, jnp.where(strict, lk, 0.0) if diagonal else lk))
            sums = [_split_dot(lk, later) for _, lk in pairs]
            runs = [st[0] for st in state]
            ws = []
            for (ls, lk), after, (_, diagonal, hh) in zip(pairs, sums, chains):
                w = jnp.exp(ls + (after + runs[hh]))
                ws.append((jnp.where(strict, w, 0.0) if diagonal else w).astype(BF16))
                runs[hh] = runs[hh] + jnp.sum(lk, axis=1, keepdims=True)
            accs = [st[1] for st in state]
            for w, (kb, _, hh) in zip(ws, chains):
                accs[hh] = accs[hh] + _dot(w, v_ref[keys(kb), _pair(hh)])
            return tuple(zip(runs, accs))

        def alive(state):
            top = jnp.max(state[0][0])
            for st in state[1:]:
                top = jnp.maximum(top, jnp.max(st[0]))
            return (top > SB_DEAD).astype(I32)

        def finish(state, first):
            ks_ref[hp, i] = first
            for pair in range(nh // 2):
                o_ref[:, _pair(2 * pair)] = jnp.where(lane < HALF_LANES, state[2 * pair][1], state[2 * pair + 1][1])
                r_ref[:, _pair(2 * pair)] = jnp.where(lane < HALF_LANES, state[2 * pair][0], state[2 * pair + 1][0])

        zero = ((jnp.zeros((bq, 1), F32), jnp.zeros((bq, LANES), F32)),) * nh

        @pl.when(i == 0)
        def _():
            finish(walk([(0, True)], zero), 0)

        @pl.when(i > 0)
        def _():
            state = walk([(i, True), (i - 1, False)], zero)

            def cond(carry):
                return jnp.logical_and(carry[0] >= 0, carry[1] > 0)

            def step(carry):
                state = walk([(carry[0], False)], carry[2])
                return carry[0] - 1, alive(state), state

            kb, _, state = lax.while_loop(cond, step, (i - 2, alive(state), state))
            finish(state, kb + 1)

    return pl.pallas_call(
        body, name="sb_fwd",
        grid=(SB_HEADS // nh, nq),
        in_specs=[pl.BlockSpec((bq, width), lambda h, i: (i, h)),
                  pl.BlockSpec((s, width), lambda h, i: (0, h)),
                  pl.BlockSpec((s, width), lambda h, i: (0, h))],
        out_specs=[pl.BlockSpec((bq, width), lambda h, i: (i, h)),
                   pl.BlockSpec((bq, width), lambda h, i: (i, h)),
                   pl.BlockSpec(memory_space=pltpu.SMEM)],
        out_shape=[jax.ShapeDtypeStruct((s, SB_WIDTH), F32), jax.ShapeDtypeStruct((s, SB_WIDTH), F32),
                   jax.ShapeDtypeStruct((SB_HEADS // nh, nq), I32)],
        compiler_params=_params(("arbitrary", "arbitrary"), VMEM_LIMIT),
    )(q, k, v)


def _mla_fwd_call(q, k, v):
    s = q.shape[0]
    bq = _row_tile(s, MLA_BQ)
    bk = _row_tile(s, MLA_BK)
    nq = s // bq
    assert bk % bq == 0
    step = min(bk // 2, MLA_STEP)
    nsub = bk // step
    assert nsub % 2 == 0

    def body(q_ref, k_ref, v_ref, o_ref, lse_ref, p_ref, s_ref):
        i = pl.program_id(1)
        lane = lax.broadcasted_iota(I32, (bq, LANES), 1)
        row = lax.broadcasted_iota(I32, (step, bq), 1)
        col = lax.broadcasted_iota(I32, (step, bq), 0)
        n_full = (i * bq) // bk

        def keys(g):
            return pl.ds(pl.multiple_of(g * step, step), step)

        def join(left, right, qlo):
            return right if qlo == 0 else jnp.concatenate([left[:, :qlo], right], axis=1)

        def put_scores(g, slot, qlo=0):
            for hh in range(2):
                cols = slice(hh * HEAD_PAD, (hh + 1) * HEAD_PAD)
                s_ref[slot, hh, :, qlo:] = _dot_nt(k_ref[keys(g), cols], q_ref[qlo:, cols])

        def add_pv(carry, g, slot, qlo=0):
            vblk = v_ref[keys(g), :]
            out = []
            for hh, (m, l, acc, alpha) in enumerate(carry):
                upd = alpha[:, qlo:] * acc[:, qlo:] + _dot_tn(vblk, p_ref[slot, hh, :, qlo:])
                out.append((m, l, join(acc, upd, qlo), alpha))
            return tuple(out)

        def substep(g, slot, carry, masked, prefetch, qlo=0, next_qlo=0, prev_qlo=0):
            if prefetch:
                put_scores(g + 1, 1 - slot, next_qlo)
            carry = add_pv(carry, jnp.maximum(g - 1, 0), 1 - slot, prev_qlo)
            new = []
            for hh in range(2):
                m, l, acc, _ = carry[hh]
                sc = s_ref[slot, hh, :, qlo:]
                if masked:
                    sc = jnp.where(col[:, qlo:] + g * step <= row[:, qlo:] + i * bq, sc, MASK_NEG)
                m_new = jnp.maximum(m[:, qlo:], jnp.max(sc, axis=0, keepdims=True))
                p = jnp.exp2(sc - m_new)
                alpha = jnp.exp2(m[:, qlo:] - m_new)
                l_new = alpha * l[:, qlo:] + jnp.sum(p, axis=0, keepdims=True)
                p_ref[slot, hh, :, qlo:] = p.astype(BF16)
                new.append((join(m, m_new, qlo), join(l, l_new, qlo), acc, join(jnp.ones_like(m), alpha, qlo)))
            return tuple(new)

        def first_query(t, masked):
            return t * step if (masked and bk == bq and 0 <= t < nsub) else 0

        def chunk(kb, carry, masked):
            for t in range(nsub):
                last = masked and t == nsub - 1
                carry = substep(nsub * kb + t, t % 2, carry, masked, not last, first_query(t, masked),
                                first_query(t + 1, masked), first_query(t - 1, masked))
            return carry

        p_ref[1] = jnp.zeros_like(p_ref[1])
        put_scores(0, 0)
        one = (jnp.full((1, bq), MASK_NEG, F32), jnp.zeros((1, bq), F32), jnp.zeros((LANES, bq), F32),
               jnp.ones((1, bq), F32))
        carry = lax.fori_loop(0, n_full, lambda kb, cr: chunk(kb, cr, False), (one, one))
        carry = chunk(n_full, carry, True)
        (m0, l0, a0, _), (m1, l1, a1, _) = add_pv(carry, nsub * n_full + nsub - 1, 1, first_query(nsub - 1, True))
        o_ref[...] = jnp.where(lane < HALF_LANES, (a0 / l0).T, (a1 / l1).T)
        sub = lax.broadcasted_iota(I32, (8, bq), 0)
        lse_ref[...] = jnp.where(sub == 0, m0 + jnp.log2(l0), jnp.where(sub == 1, m1 + jnp.log2(l1), 0.0))

    return pl.pallas_call(
        body, name="mla_fwd",
        grid=(4, nq),
        in_specs=[pl.BlockSpec((bq, 2 * HEAD_PAD), lambda h, i: (i, h)),
                  pl.BlockSpec((s, 2 * HEAD_PAD), lambda h, i: (0, h)),
                  pl.BlockSpec((s, LANES), lambda h, i: (0, h))],
        out_specs=[pl.BlockSpec((bq, LANES), lambda h, i: (i, h)),
                   pl.BlockSpec((None, 8, bq), lambda h, i: (h, 0, i))],
        out_shape=[jax.ShapeDtypeStruct((s, MLA_WIDTH), F32), jax.ShapeDtypeStruct((4, 8, s), F32)],
        scratch_shapes=[pltpu.VMEM((2, 2, step, bq), BF16), pltpu.VMEM((2, 2, step, bq), F32)],
        compiler_params=_params(("arbitrary", "arbitrary"), VMEM_LIMIT),
    )(q, k, v)


def _out_call(o_sb, g_sb, o_mla, g_mla, x, target, ada, w_out_bf):
    s = x.shape[0]
    tm = _row_tile(s, 256)

    def body(osb_ref, gsb_ref, oml_ref, gml_ref, x_ref, t_ref, gate_ref, w_ref,
             dosb_ref, doml_ref, dgsb_ref, dgml_ref, dy_ref, gw_ref, dgate_ref, sq_ref):
        @pl.when(pl.program_id(0) == 0)
        def _():
            gw_ref[...] = jnp.zeros_like(gw_ref)
            dgate_ref[...] = jnp.zeros_like(dgate_ref)
            sq_ref[...] = jnp.zeros_like(sq_ref)

        g_s, g_m = gsb_ref[...], gml_ref[...]
        sig_s, sig_m = _sigmoid(g_s), _sigmoid(g_m)
        silu_s, silu_m = g_s * sig_s, g_m * sig_m
        o_s, o_m = osb_ref[...], oml_ref[...]
        mixed = jnp.concatenate([o_s * silu_s, o_m * silu_m], axis=1).astype(BF16)
        u = _dot(mixed, w_ref[...])
        gate_v = gate_ref[...]
        err = x_ref[...] + gate_v * u - t_ref[...]
        sq_ref[...] += jnp.sum(err * err, axis=0, keepdims=True)
        dy = err * (1.0 / D_MODEL)
        dy_ref[...] = dy
        dgate_ref[...] += jnp.sum(dy * u, axis=0, keepdims=True)
        du = (dy * gate_v).astype(BF16)
        gw_ref[...] += _dot_tn(mixed, du)
        dmix = _dot_nt(du, w_ref[...])
        dm_s, dm_m = dmix[:, :SB_WIDTH], dmix[:, SB_WIDTH:]
        dosb_ref[...] = (dm_s * silu_s).astype(BF16)
        doml_ref[...] = (dm_m * silu_m).astype(BF16)
        dgsb_ref[...] = (dm_s * o_s * (sig_s * (1.0 + g_s * (1.0 - sig_s)))).astype(BF16)
        dgml_ref[...] = (dm_m * o_m * (sig_m * (1.0 + g_m * (1.0 - sig_m)))).astype(BF16)

    return pl.pallas_call(
        body, name="out_proj_loss",
        grid=(s // tm,),
        in_specs=[_rows(tm, 512), _rows(tm, 512), _rows(tm, 512), _rows(tm, 512),
                  _rows(tm, D_MODEL), _rows(tm, D_MODEL), _ada_part(2), _full((D_MODEL, D_MODEL))],
        out_specs=[_rows(tm, 512), _rows(tm, 512), _rows(tm, 512), _rows(tm, 512), _rows(tm, D_MODEL),
                   _full((D_MODEL, D_MODEL)), _full((1, D_MODEL)), _full((1, D_MODEL))],
        out_shape=[jax.ShapeDtypeStruct((s, 512), BF16)] * 4
        + [jax.ShapeDtypeStruct((s, D_MODEL), F32), jax.ShapeDtypeStruct((D_MODEL, D_MODEL), F32),
           jax.ShapeDtypeStruct((1, D_MODEL), F32), jax.ShapeDtypeStruct((1, D_MODEL), F32)],
        compiler_params=_params(("arbitrary",), VMEM_LIMIT),
    )(o_sb, g_sb, o_mla, g_mla, x, target, ada, w_out_bf)


def _head_mask(lane, hh):
    return jnp.where((lane >= HALF_LANES) if hh else (lane < HALF_LANES), 1.0, 0.0)


def _pick_lane(packed, lane, which):
    return jnp.sum(jnp.where(lane == which, packed, 0.0), axis=1, keepdims=True)


def _sb_bwd_call(kstart, q, k, v, do, rfin):
    s = q.shape[0]
    bq = _row_tile(s, 256)
    nq = s // bq
    nh = SB_GROUP
    width = nh * HALF_LANES

    def body(ks_ref, q_ref, k_ref, v_ref, do_ref, r_ref, dq_ref, dk_ref, dv_ref):
        hp, i = pl.program_id(0), pl.program_id(1)

        @pl.when(i == 0)
        def _():
            dk_ref[...] = jnp.zeros_like(dk_ref)
            dv_ref[...] = jnp.zeros_like(dv_ref)

        lane = lax.broadcasted_iota(I32, (bq, LANES), 1)
        row = lax.broadcasted_iota(I32, (bq, bq), 0)
        col = lax.broadcasted_iota(I32, (bq, bq), 1)
        upto = jnp.where(row <= col, 1.0, 0.0).astype(BF16)
        before = jnp.where(row < col, 1.0, 0.0).astype(BF16)
        masks = [_head_mask(lane, hh).astype(BF16) for hh in range(2)]
        qms = [q_ref[:, _pair(hh)] * jnp.asarray(SB_SCALE, BF16) * masks[hh % 2] for hh in range(nh)]
        doms = [do_ref[:, _pair(hh)] * masks[hh % 2] for hh in range(nh)]
        totals = [_pick_lane(r_ref[:, _pair(hh)], lane, HALF_LANES * (hh % 2)) for hh in range(nh)]
        strict = col < row

        def walk(blocks, state):
            chains = [(kb, diagonal, hh) for kb, diagonal in blocks for hh in range(nh)]
            keys = lambda kb: pl.ds(pl.multiple_of(kb * bq, bq), bq)
            cut = lambda x, diagonal: jnp.where(strict, x, 0.0) if diagonal else x
            zs = [_dot_nt(qms[hh], k_ref[keys(kb), _pair(hh)]) for kb, _, hh in chains]
            dws = [_dot_nt(doms[hh], v_ref[keys(kb), _pair(hh)]) for kb, _, hh in chains]
            pairs = []
            for z, (_, diagonal, _) in zip(zs, chains):
                ls, lk = _log_sigmoid_pair(z)
                pairs.append((ls, cut(lk, diagonal)))
            incls = [_split_dot(lk, upto) for _, lk in pairs]
            pres = [st[0] for st in state]
            ws, gs = [], []
            for (ls, lk), incl, dw, (_, diagonal, hh) in zip(pairs, incls, dws, chains):
                w = cut(jnp.exp(ls + ((totals[hh] - pres[hh]) - incl)), diagonal)
                ws.append(w.astype(BF16))
                gs.append(w * dw)
                pres[hh] = pres[hh] + jnp.sum(lk, axis=1, keepdims=True)
            gsums = [_dot(g.astype(BF16), before) for g in gs]
            gpres = [st[1] for st in state]
            dzs = []
            for (ls, _), g, gsum, (_, diagonal, hh) in zip(pairs, gs, gsums, chains):
                dzs.append(cut(g - jnp.exp(ls) * (g + (gpres[hh] + gsum)), diagonal).astype(BF16))
                gpres[hh] = gpres[hh] + jnp.sum(g, axis=1, keepdims=True)
            dqs = [st[2] for st in state]
            dk_parts, dv_parts = [], []
            for dzb, w, (kb, _, hh) in zip(dzs, ws, chains):
                dk_parts.append(_dot_tn(dzb, qms[hh]))
                dv_parts.append(_dot_tn(w, doms[hh]))
                dqs[hh] = dqs[hh] + _dot(dzb, k_ref[keys(kb), _pair(hh)])
            for b, (kb, _) in enumerate(blocks):
                for pair in range(nh // 2):
                    c0 = b * nh + 2 * pair
                    dk_ref[keys(kb), _pair(2 * pair)] += dk_parts[c0] + dk_parts[c0 + 1]
                    dv_ref[keys(kb), _pair(2 * pair)] += dv_parts[c0] + dv_parts[c0 + 1]
            return tuple(zip(pres, gpres, dqs))

        def finish(state):
            for pair in range(nh // 2):
                both = jnp.where(lane < HALF_LANES, state[2 * pair][2], state[2 * pair + 1][2])
                dq_ref[:, _pair(2 * pair)] = (both * SB_SCALE).astype(BF16)

        zero = ((jnp.zeros((bq, 1), F32), jnp.zeros((bq, 1), F32), jnp.zeros((bq, LANES), F32)),) * nh

        @pl.when(i == 0)
        def _():
            finish(walk([(0, True)], zero))

        @pl.when(i > 0)
        def _():
            state = lax.fori_loop(ks_ref[hp, i], i - 1, lambda kb, st: walk([(kb, False)], st), zero)
            finish(walk([(i - 1, False), (i, True)], state))

    return pl.pallas_call(
        body, name="sb_bwd",
        grid_spec=pltpu.PrefetchScalarGridSpec(
            num_scalar_prefetch=1, grid=(SB_HEADS // nh, nq),
            in_specs=[pl.BlockSpec((bq, width), lambda h, i, ks: (i, h)),
                      pl.BlockSpec((s, width), lambda h, i, ks: (0, h), pipeline_mode=pl.Buffered(1)),
                      pl.BlockSpec((s, width), lambda h, i, ks: (0, h), pipeline_mode=pl.Buffered(1)),
                      pl.BlockSpec((bq, width), lambda h, i, ks: (i, h)),
                      pl.BlockSpec((bq, width), lambda h, i, ks: (i, h))],
            out_specs=[pl.BlockSpec((bq, width), lambda h, i, ks: (i, h)),
                       pl.BlockSpec((s, width), lambda h, i, ks: (0, h), pipeline_mode=pl.Buffered(1)),
                       pl.BlockSpec((s, width), lambda h, i, ks: (0, h), pipeline_mode=pl.Buffered(1))]),
        out_shape=[jax.ShapeDtypeStruct((s, SB_WIDTH), BF16), jax.ShapeDtypeStruct((s, SB_WIDTH), F32),
                   jax.ShapeDtypeStruct((s, SB_WIDTH), F32)],
        compiler_params=_params(("arbitrary", "arbitrary"), VMEM_LIMIT),
    )(kstart, q, k, v, do, rfin)


def _mla_bwd_call(q, k, v, do, o, lse):
    s = q.shape[0]
    bq = _row_tile(s, MLA_BWD_BQ)
    bk = _row_tile(s, MLA_BWD_BK)
    nq = s // bq
    assert bk % bq == 0
    step = min(bk // 2, MLA_BWD_STEP)
    nsub = bk // step
    assert nsub % 2 == 0

    def body(q_ref, k_ref, v_ref, do_ref, o_ref, lse_ref, dq_ref, dk_ref, dv_ref, dom_ref, s_ref, dp_ref, pb_ref,
             ds_ref):
        i = pl.program_id(1)

        @pl.when(i == 0)
        def _():
            dk_ref[...] = jnp.zeros_like(dk_ref)
            dv_ref[...] = jnp.zeros_like(dv_ref)

        lane = lax.broadcasted_iota(I32, (bq, LANES), 1)
        row = lax.broadcasted_iota(I32, (step, bq), 1)
        col = lax.broadcasted_iota(I32, (step, bq), 0)
        n_full = (i * bq) // bk
        do2 = do_ref[...]
        prod = do2.astype(F32) * o_ref[...]
        ones = jnp.ones((8, LANES), BF16)
        deltas, lses = [], []
        for hh in range(2):
            head = _head_mask(lane, hh)
            dom_ref[hh] = do2 * head.astype(BF16)
            part = prod * head
            hi = part.astype(BF16)
            lo = (part - hi.astype(F32)).astype(BF16)
            deltas.append((_dot_nt(ones, hi) + _dot_nt(ones, lo))[0:1])
            lses.append(lse_ref[hh:hh + 1, :])

        def keys(g):
            return pl.ds(pl.multiple_of(g * step, step), step)

        def heads():
            return [(hh, slice(hh * HEAD_PAD, (hh + 1) * HEAD_PAD)) for hh in range(2)]

        def put_products(g, slot, qlo=0):
            vblk = v_ref[keys(g), :]
            for hh, cols in heads():
                s_ref[slot, hh, :, qlo:] = _dot_nt(k_ref[keys(g), cols], q_ref[qlo:, cols])
                dp_ref[slot, hh, :, qlo:] = _dot_nt(vblk, dom_ref[hh, qlo:, :])

        def add_grads(dqs, g, slot, qlo=0):
            rows = keys(g)
            new, dv_parts = [], []
            for hh, cols in heads():
                ds = ds_ref[slot, hh, :, qlo:]
                dk_ref[rows, cols] += _dot(ds, q_ref[qlo:, cols])
                dv_parts.append(_dot(pb_ref[slot, hh, :, qlo:], dom_ref[hh, qlo:, :]))
                upd = dqs[hh][:, qlo:] + _dot_tn(k_ref[rows, cols], ds)
                new.append(upd if qlo == 0 else jnp.concatenate([dqs[hh][:, :qlo], upd], axis=1))
            dv_ref[rows, :] += dv_parts[0] + dv_parts[1]
            return tuple(new)

        def substep(g, slot, dqs, masked, prefetch, qlo=0, next_qlo=0, prev_qlo=0):
            if prefetch:
                put_products(g + 1, 1 - slot, next_qlo)
            dqs = add_grads(dqs, jnp.maximum(g - 1, 0), 1 - slot, prev_qlo)
            for hh, _ in heads():
                p = jnp.exp2(s_ref[slot, hh, :, qlo:] - lses[hh][:, qlo:])
                if masked:
                    p = jnp.where(col[:, qlo:] + g * step <= row[:, qlo:] + i * bq, p, 0.0)
                ds_ref[slot, hh, :, qlo:] = (p * (dp_ref[slot, hh, :, qlo:] - deltas[hh][:, qlo:])).astype(BF16)
                pb_ref[slot, hh, :, qlo:] = p.astype(BF16)
            return dqs

        def first_query(t, masked):
            return t * step if (masked and bk == bq and 0 <= t < nsub) else 0

        def chunk(kb, dqs, masked):
            for t in range(nsub):
                last = masked and t == nsub - 1
                dqs = substep(nsub * kb + t, t % 2, dqs, masked, not last, first_query(t, masked),
                              first_query(t + 1, masked), first_query(t - 1, masked))
            return dqs

        ds_ref[1] = jnp.zeros_like(ds_ref[1])
        pb_ref[1] = jnp.zeros_like(pb_ref[1])
        put_products(0, 0)
        zero = jnp.zeros((HEAD_PAD, bq), F32)
        dqs = lax.fori_loop(0, n_full, lambda kb, dqs: chunk(kb, dqs, False), (zero, zero))
        dqs = chunk(n_full, dqs, True)
        dqs = add_grads(dqs, nsub * n_full + nsub - 1, 1, first_query(nsub - 1, True))
        dq_ref[:, :HEAD_PAD] = dqs[0].T * MLA_SCALE
        dq_ref[:, HEAD_PAD:] = dqs[1].T * MLA_SCALE

    return pl.pallas_call(
        body, name="mla_bwd",
        grid=(4, nq),
        in_specs=[pl.BlockSpec((bq, 2 * HEAD_PAD), lambda h, i: (i, h)),
                  pl.BlockSpec((s, 2 * HEAD_PAD), lambda h, i: (0, h)),
                  pl.BlockSpec((s, LANES), lambda h, i: (0, h)),
                  pl.BlockSpec((bq, LANES), lambda h, i: (i, h)),
                  pl.BlockSpec((bq, LANES), lambda h, i: (i, h)),
                  pl.BlockSpec((None, 8, bq), lambda h, i: (h, 0, i))],
        out_specs=[pl.BlockSpec((bq, 2 * HEAD_PAD), lambda h, i: (i, h)),
                   pl.BlockSpec((s, 2 * HEAD_PAD), lambda h, i: (0, h), pipeline_mode=pl.Buffered(1)),
                   pl.BlockSpec((s, LANES), lambda h, i: (0, h), pipeline_mode=pl.Buffered(1))],
        out_shape=[jax.ShapeDtypeStruct((s, MLA_PAD_WIDTH), F32), jax.ShapeDtypeStruct((s, MLA_PAD_WIDTH), F32),
                   jax.ShapeDtypeStruct((s, MLA_WIDTH), F32)],
        scratch_shapes=[pltpu.VMEM((2, bq, LANES), BF16),
                        pltpu.VMEM((2, 2, step, bq), F32), pltpu.VMEM((2, 2, step, bq), F32),
                        pltpu.VMEM((2, 2, step, bq), BF16), pltpu.VMEM((2, 2, step, bq), BF16)],
        compiler_params=_params(("arbitrary", "arbitrary"), VMEM_LIMIT),
    )(q, k, v, do, o, lse)


def _rms_bwd(d_out, inp, r, weight, n):
    normed = inp * r
    gw = d_out * weight
    d_in = r * (gw - normed * (jnp.sum(gw * normed, axis=-1, keepdims=True) * (1.0 / n)))
    return d_in, d_out * normed


def _mla_prep_bwd_call(dq, dk, dv, q0, k0, cqn, ckvn, c_q, c_kv, cos_t, sin_t,
                       q_lora_norm, kv_lora_norm, qhn_pad, khn_pad, w_uq_bf, w_uk_bf, w_uv_bf):
    s = dq.shape[0]
    tm = _row_tile(s, 256)

    def body(dq_ref, dk_ref, dv_ref, q0_ref, k0_ref, cqn_ref, ckvn_ref, cq_ref, ckv_ref,
             cos_ref, sin_ref, qln_ref, kvln_ref, qhn_ref, khn_ref, wuq_ref, wuk_ref, wuv_ref,
             dcq_ref, dckv_ref, dkr_ref, gwuq_ref, gwuk_ref, gwuv_ref, gqln_ref, gkvln_ref, gqhn_ref, gkhn_ref,
             dq0_ref, dk0_ref, tmp_ref):
        @pl.when(pl.program_id(0) == 0)
        def _():
            for ref in (gwuq_ref, gwuk_ref, gwuv_ref, gqln_ref, gkvln_ref, gqhn_ref, gkhn_ref):
                ref[...] = jnp.zeros_like(ref)

        cos_t, sin_t = cos_ref[...], sin_ref[...]
        lane = lax.broadcasted_iota(I32, (tm, LANES), 1)
        rope_lanes = jnp.logical_or(lane < ROPE_HALF,
                                    jnp.logical_and(lane >= HALF_LANES, lane < HALF_LANES + ROPE_HALF))
        heads = [slice(h * HEAD_PAD, (h + 1) * HEAD_PAD) for h in range(MLA_HEADS)]

        def head_norm_bwd(d_ref, x0_ref, w_ref, out_ref, scale):
            w = w_ref[...]
            inv = [lax.rsqrt(jnp.sum(x0_ref[:, cols] * x0_ref[:, cols], axis=-1, keepdims=True)
                             * (1.0 / MLA_QK_DIM) + EPS) for cols in heads]
            for cols in heads:
                tmp_ref[:, cols] = _rope_adjoint(d_ref[:, cols] * scale, cos_t, sin_t)
            dots = [jnp.sum(tmp_ref[:, cols] * w * (x0_ref[:, cols] * r), axis=-1, keepdims=True)
                    for cols, r in zip(heads, inv)]
            g_w = jnp.zeros((1, LANES), F32)
            rope_sum = jnp.zeros((tm, LANES), F32)
            for cols, r, dot in zip(heads, inv, dots):
                normed = x0_ref[:, cols] * r
                d_n = tmp_ref[:, cols]
                d_x0 = r * (d_n * w - normed * (dot * (1.0 / MLA_QK_DIM)))
                out_ref[:, cols] = d_x0.astype(BF16)
                g_w = g_w + jnp.sum(d_n * normed, axis=0, keepdims=True)
                rope_sum = rope_sum + jnp.where(rope_lanes, d_x0, 0.0)
            return g_w, rope_sum

        g_qhn, _ = head_norm_bwd(dq_ref, q0_ref, qhn_ref, dq0_ref, 1.0)
        g_khn, d_kr = head_norm_bwd(dk_ref, k0_ref, khn_ref, dk0_ref, LN2)
        cqn, ckvn = cqn_ref[...], ckvn_ref[...]
        d_q0b, d_k0b, dvb = dq0_ref[...], dk0_ref[...], dv_ref[...].astype(BF16)
        d_cqn = _dot_nt(d_q0b, wuq_ref[...])
        gwuq_ref[...] += _dot_tn(cqn, d_q0b)
        d_ckvn = _dot_nt(d_k0b, wuk_ref[...]) + _dot_nt(dvb, wuv_ref[...])
        gwuk_ref[...] += _dot_tn(ckvn, d_k0b)
        gwuv_ref[...] += _dot_tn(ckvn, dvb)
        gqhn_ref[...] += g_qhn
        gkhn_ref[...] += g_khn
        dkr_ref[...] = d_kr.astype(BF16)
        cq = cq_ref[...]
        rcq = lax.rsqrt(jnp.mean(cq * cq, axis=-1, keepdims=True) + EPS)
        d_cq, gl = _rms_bwd(d_cqn, cq, rcq, qln_ref[...], Q_LORA_RANK)
        dcq_ref[...] = d_cq.astype(BF16)
        gqln_ref[...] += jnp.sum(gl, axis=0, keepdims=True)
        ckv = ckv_ref[...]
        rckv = lax.rsqrt(jnp.mean(ckv * ckv, axis=-1, keepdims=True) + EPS)
        d_ckv, gl = _rms_bwd(d_ckvn, ckv, rckv, kvln_ref[...], KV_LORA_RANK)
        dckv_ref[...] = d_ckv.astype(BF16)
        gkvln_ref[...] += jnp.sum(gl, axis=0, keepdims=True)

    return pl.pallas_call(
        body, name="mla_prep_bwd",
        grid=(s // tm,),
        in_specs=[_rows(tm, MLA_PAD_WIDTH), _rows(tm, MLA_PAD_WIDTH), _rows(tm, MLA_WIDTH),
                  _rows(tm, MLA_PAD_WIDTH), _rows(tm, MLA_PAD_WIDTH),
                  _rows(tm, Q_LORA_RANK), _rows(tm, KV_LORA_RANK), _rows(tm, Q_LORA_RANK), _rows(tm, KV_LORA_RANK),
                  _rows(tm, LANES), _rows(tm, LANES),
                  _full((1, Q_LORA_RANK)), _full((1, KV_LORA_RANK)), _full((1, LANES)), _full((1, LANES)),
                  _full((Q_LORA_RANK, MLA_PAD_WIDTH)), _full((KV_LORA_RANK, MLA_PAD_WIDTH)),
                  _full((KV_LORA_RANK, MLA_WIDTH))],
        out_specs=[_rows(tm, Q_LORA_RANK), _rows(tm, KV_LORA_RANK), _rows(tm, LANES),
                   _full((Q_LORA_RANK, MLA_PAD_WIDTH)), _full((KV_LORA_RANK, MLA_PAD_WIDTH)),
                   _full((KV_LORA_RANK, MLA_WIDTH)),
                   _full((1, Q_LORA_RANK)), _full((1, KV_LORA_RANK)), _full((1, LANES)), _full((1, LANES))],
        out_shape=[jax.ShapeDtypeStruct((s, Q_LORA_RANK), BF16), jax.ShapeDtypeStruct((s, KV_LORA_RANK), BF16),
                   jax.ShapeDtypeStruct((s, LANES), BF16),
                   jax.ShapeDtypeStruct((Q_LORA_RANK, MLA_PAD_WIDTH), F32),
                   jax.ShapeDtypeStruct((KV_LORA_RANK, MLA_PAD_WIDTH), F32),
                   jax.ShapeDtypeStruct((KV_LORA_RANK, MLA_WIDTH), F32),
                   jax.ShapeDtypeStruct((1, Q_LORA_RANK), F32), jax.ShapeDtypeStruct((1, KV_LORA_RANK), F32),
                   jax.ShapeDtypeStruct((1, LANES), F32), jax.ShapeDtypeStruct((1, LANES), F32)],
        scratch_shapes=[pltpu.VMEM((tm, MLA_PAD_WIDTH), BF16), pltpu.VMEM((tm, MLA_PAD_WIDTH), BF16),
                        pltpu.VMEM((tm, MLA_PAD_WIDTH), F32)],
        compiler_params=_params(("arbitrary",), VMEM_LIMIT),
    )(dq, dk, dv, q0, k0, cqn, ckvn, c_q, c_kv, cos_t, sin_t,
      q_lora_norm, kv_lora_norm, qhn_pad, khn_pad, w_uq_bf, w_uk_bf, w_uv_bf)


def _dh_call(pieces, hb, x, dy, ada, norm_w, w_in_bf):
    s = x.shape[0]
    tm = _row_tile(s, 256)
    widths = [p.shape[1] for p in pieces]
    offsets = [sum(widths[:j]) for j in range(len(widths))]
    assert offsets[-1] + widths[-1] == IN_COLS_PAD
    n = len(pieces)

    def body(*refs):
        p_refs = refs[:n]
        (hb_ref, x_ref, dy_ref, sh_ref, sc_ref, nw_ref, w_ref, gx_ref, gw_ref, dsh_ref, dsc_ref, gnw_ref,
         dp_ref) = refs[n:]

        @pl.when(pl.program_id(0) == 0)
        def _():
            gw_ref[...] = jnp.zeros_like(gw_ref)
            dsh_ref[...] = jnp.zeros_like(dsh_ref)
            dsc_ref[...] = jnp.zeros_like(dsc_ref)
            gnw_ref[...] = jnp.zeros_like(gnw_ref)

        for p_ref, c0, width in zip(p_refs, offsets, widths):
            dp_ref[:, c0:c0 + width] = p_ref[...].astype(BF16)
        gw_ref[...] += _dot_tn(hb_ref[...], dp_ref[...])
        dh = _dot_nt(dp_ref[...], w_ref[...])
        xx = x_ref[...]
        r0 = lax.rsqrt(jnp.mean(xx * xx, axis=-1, keepdims=True) + EPS)
        xn = xx * r0
        nw = nw_ref[...]
        dsh_ref[...] += jnp.sum(dh, axis=0, keepdims=True)
        dsc_ref[...] += jnp.sum(dh * (xn * nw), axis=0, keepdims=True)
        dn = dh * (1.0 + sc_ref[...])
        gnw_ref[...] += jnp.sum(dn * xn, axis=0, keepdims=True)
        dxn = dn * nw
        gx_ref[...] = dy_ref[...] + r0 * (dxn - xn * jnp.mean(dxn * xn, axis=-1, keepdims=True))

    return pl.pallas_call(
        body, name="in_proj_bwd",
        grid=(s // tm,),
        in_specs=[_rows(tm, w) for w in widths]
        + [_rows(tm, D_MODEL), _rows(tm, D_MODEL), _rows(tm, D_MODEL),
           _ada_part(0), _ada_part(1), _full((1, D_MODEL)),
           pl.BlockSpec((D_MODEL, IN_COLS_PAD), lambda i: (0, 0), pipeline_mode=pl.Buffered(1))],
        out_specs=[_rows(tm, D_MODEL),
                   pl.BlockSpec((D_MODEL, IN_COLS_PAD), lambda i: (0, 0), pipeline_mode=pl.Buffered(1)),
                   _full((1, D_MODEL)), _full((1, D_MODEL)), _full((1, D_MODEL))],
        out_shape=[jax.ShapeDtypeStruct((s, D_MODEL), F32), jax.ShapeDtypeStruct((D_MODEL, IN_COLS_PAD), F32),
                   jax.ShapeDtypeStruct((1, D_MODEL), F32), jax.ShapeDtypeStruct((1, D_MODEL), F32),
                   jax.ShapeDtypeStruct((1, D_MODEL), F32)],
        scratch_shapes=[pltpu.VMEM((tm, IN_COLS_PAD), BF16)],
        compiler_params=_params(("arbitrary",), VMEM_LIMIT),
    )(*pieces, hb, x, dy, ada, ada, norm_w, w_in_bf)


def _adamw(g, w, m, v):
    m = ADAM_B1 * m + (1.0 - ADAM_B1) * g
    v = ADAM_B2 * v + (1.0 - ADAM_B2) * (g * g)
    m_hat = m / (1.0 - ADAM_B1 ** ADAM_STEP)
    v_hat = v / (1.0 - ADAM_B2 ** ADAM_STEP)
    delta = -ADAM_LR * (m_hat / (jnp.sqrt(v_hat) + ADAM_EPS) + ADAM_WD * w)
    return delta, m, v


def _adam_shard_call(name, own, sib, w, m, v):
    r, c = w.shape
    tr = r if r <= 512 else 256

    def body(own_ref, sib_ref, w_ref, m_ref, v_ref, g_ref, d_ref, nm_ref, nv_ref):
        a = ((own_ref[0].astype(F32) + own_ref[1].astype(F32)) + own_ref[2].astype(F32)) + own_ref[3].astype(F32)
        b = ((sib_ref[0].astype(F32) + sib_ref[1].astype(F32)) + sib_ref[2].astype(F32)) + sib_ref[3].astype(F32)
        g = a + b
        g_ref[...] = g
        d_ref[...], nm_ref[...], nv_ref[...] = _adamw(g, w_ref[...], m_ref[...], v_ref[...])

    part = pl.BlockSpec((4, tr, c), lambda i: (0, i, 0))
    blk = pl.BlockSpec((tr, c), lambda i: (i, 0))
    return pl.pallas_call(
        body, name=name,
        grid=(r // tr,),
        in_specs=[part, part, blk, blk, blk],
        out_specs=[blk] * 4,
        out_shape=[jax.ShapeDtypeStruct((r, c), F32)] * 4,
        compiler_params=_params(("arbitrary",), VMEM_LIMIT),
    )(own, sib, w, m, v)


def _adam_ada_call(c_all, d_all, w, m, v):
    r, c = w.shape
    tr = 256

    def body(c_ref, d_ref, w_ref, m_ref, v_ref, g_ref, dl_ref, nm_ref, nv_ref):
        cc = c_ref[...]
        sc = cc * _sigmoid(cc)
        dd = d_ref[...]
        sc_hi = sc.astype(BF16)
        sc_lo = (sc - sc_hi.astype(F32)).astype(BF16)
        dd_hi = dd.astype(BF16)
        dd_lo = (dd - dd_hi.astype(F32)).astype(BF16)
        g = _dot_tn(sc_hi, dd_hi) + (_dot_tn(sc_hi, dd_lo) + _dot_tn(sc_lo, dd_hi))
        g_ref[...] = g
        dl_ref[...], nm_ref[...], nv_ref[...] = _adamw(g, w_ref[...], m_ref[...], v_ref[...])

    blk = pl.BlockSpec((tr, c), lambda i: (i, 0))
    return pl.pallas_call(
        body, name="adam_w_ada",
        grid=(r // tr,),
        in_specs=[pl.BlockSpec((16, tr), lambda i: (0, i)), pl.BlockSpec((16, c), lambda i: (0, 0)), blk, blk, blk],
        out_specs=[blk] * 4,
        out_shape=[jax.ShapeDtypeStruct((r, c), F32)] * 4,
        compiler_params=_params(("arbitrary",), VMEM_LIMIT),
    )(c_all, d_all, w, m, v)


def _adam_vectors_call(packs, offsets, vectors):
    nv = len(vectors)

    def body(*refs):
        p_ref, ins, outs = refs[0], refs[1:1 + 3 * nv], refs[1 + 3 * nv:]
        for j, off in enumerate(offsets):
            n = ins[3 * j].shape[1]
            span = -(-n // LANES) * LANES
            g = p_ref[0, :, off:off + span]
            for b in range(1, 8):
                g = g + p_ref[b, :, off:off + span]
            g = g[:, :n]
            outs[j][...] = g
            outs[nv + j][...], outs[2 * nv + j][...], outs[3 * nv + j][...] = _adamw(
                g, ins[3 * j][...], ins[3 * j + 1][...], ins[3 * j + 2][...])

    flat = [a for t in vectors for a in t]
    res = pl.pallas_call(
        body, name="adam_vectors",
        out_shape=[jax.ShapeDtypeStruct(t[0].shape, F32) for _ in range(4) for t in vectors],
    )(packs, *flat)
    return [res[k * nv:(k + 1) * nv] for k in range(4)]


ROPE_HALF = MLA_ROPE_DIM // 2
NOPE_A = MLA_NOPE_DIM - ROPE_HALF


def _zeros_like_lanes(t, n):
    return jnp.zeros(t.shape[:-1] + (n,), t.dtype)


def _to_head_lanes(t):
    nope, rope = t[..., :MLA_NOPE_DIM], t[..., MLA_NOPE_DIM:]
    return jnp.concatenate([rope[..., :ROPE_HALF], nope[..., :NOPE_A], rope[..., ROPE_HALF:], nope[..., NOPE_A:],
                            _zeros_like_lanes(t, HEAD_PAD - MLA_QK_DIM)], axis=-1)


def _from_head_lanes(t):
    return jnp.concatenate([t[..., ROPE_HALF:HALF_LANES], t[..., HALF_LANES + ROPE_HALF:MLA_QK_DIM],
                            t[..., :ROPE_HALF], t[..., HALF_LANES:HALF_LANES + ROPE_HALF]], axis=-1)


def _nope_to_head_lanes(t):
    return jnp.concatenate([_zeros_like_lanes(t, ROPE_HALF), t[..., :NOPE_A], _zeros_like_lanes(t, ROPE_HALF),
                            t[..., NOPE_A:], _zeros_like_lanes(t, HEAD_PAD - MLA_QK_DIM)], axis=-1)


def _rope_to_head_lanes(t):
    return jnp.concatenate([t[..., :ROPE_HALF], _zeros_like_lanes(t, HALF_LANES - ROPE_HALF), t[..., ROPE_HALF:],
                            _zeros_like_lanes(t, HALF_LANES - ROPE_HALF)], axis=-1)


def _rope_tables(positions):
    inv_freq = (ROPE_THETA ** (-jnp.arange(0, MLA_ROPE_DIM, 2, dtype=F32) / MLA_ROPE_DIM))[None]
    signed = _rope_to_head_lanes(jnp.concatenate([-inv_freq, inv_freq], axis=1))
    ang = positions.astype(F32)[:, None] * signed
    return jnp.cos(ang), jnp.sin(ang)


def _unshard_cols(g):
    return jnp.transpose(g, (1, 0, 2)).reshape(g.shape[1], 4 * g.shape[2])


def _shard_cols(g):
    r, c4 = g.shape
    return jnp.transpose(g.reshape(r, 4, c4 // 4), (1, 0, 2))


def kernel(x, c, positions, w_ada, b_ada, norm_w, w_in, q_lora_norm, w_uq, kv_lora_norm, w_ukv, q_head_norm, k_head_norm, w_out, loss_target, m_w_ada, m_b_ada, m_norm_w, m_w_in, m_q_lora_norm, m_w_uq, m_kv_lora_norm, m_w_ukv, m_q_head_norm, m_k_head_norm, m_w_out, v_w_ada, v_b_ada, v_norm_w, v_w_in, v_q_lora_norm, v_w_uq, v_kv_lora_norm, v_w_ukv, v_q_head_norm, v_k_head_norm, v_w_out):
    chip = 2 * lax.axis_index("x") + lax.axis_index("y")
    me8 = 2 * chip + lax.axis_index("c")
    ada_cols = w_ada.shape[2]
    c_all = _allgather_rows_call(c)[:, 0, :]
    ada_part = _ada_call(c_all, w_ada[0], lax.dynamic_slice_in_dim(b_ada, chip * ada_cols, ada_cols, axis=1))
    ada_g, win_g, wuq_g, wukv_g, wout_g = _gather_call(
        [ada_part[None]] + [w.astype(BF16) for w in (w_in, w_uq, w_ukv, w_out)], [False, True, True, True, True])
    ada = lax.dynamic_slice_in_dim(ada_g, me8, 1, axis=1).reshape(1, 4 * ada_cols)
    (sq_sum, grad_x, g_w_in, g_w_uq, g_w_ukv, g_w_out, d_ada, g_norm_w, g_qln, g_kvln, g_qhn, g_khn) = _local_step(
        x[0], ada, positions[0], loss_target[0], norm_w, win_g,
        q_lora_norm, _unshard_cols(wuq_g), kv_lora_norm, _unshard_cols(wukv_g), q_head_norm, k_head_norm,
        wout_g.reshape(D_MODEL, D_MODEL))

    grads = [g.astype(BF16) for g in (g_w_in, _shard_cols(g_w_uq), _shard_cols(g_w_ukv),
                                      g_w_out.reshape(4, D_MODEL // 4, D_MODEL))]
    pieces = [d_ada, g_norm_w, g_qln, g_kvln, g_qhn, g_khn, (0.5 * sq_sum / D_MODEL).reshape(1, 1)]
    spans = [-(-p.shape[1] // LANES) * LANES for p in pieces]
    starts = [sum(spans[:j]) for j in range(len(spans))]
    small = jnp.concatenate([jnp.pad(p, ((0, 0), (0, sp - p.shape[1]))) for p, sp in zip(pieces, spans)], axis=1)
    own, sib, packs = _exchange_call(grads, small)
    loss = jnp.sum(packs[:, 0, starts[-1]])

    names = ["adam_w_in", "adam_w_uq", "adam_w_ukv", "adam_w_out"]
    shard_w = [(w_in, m_w_in, v_w_in), (w_uq, m_w_uq, v_w_uq), (w_ukv, m_w_ukv, v_w_ukv),
               (w_out, m_w_out, v_w_out)]
    res = {}
    for name, o_g, s_g, (w, m, v) in zip(names, own, sib, shard_w):
        res[name] = _adam_shard_call(name, o_g, s_g, w[0], m[0], v[0])
    d_all = lax.dynamic_slice_in_dim(packs[:, 0, :], starts[0] + chip * ada_cols, ada_cols, axis=1)
    res_ada = _adam_ada_call(jnp.pad(c_all, ((0, 8), (0, 0))), jnp.pad(d_all, ((0, 8), (0, 0))),
                             w_ada[0], m_w_ada[0], v_w_ada[0])
    vectors = [(b_ada, m_b_ada, v_b_ada), (norm_w, m_norm_w, v_norm_w), (q_lora_norm, m_q_lora_norm, v_q_lora_norm),
               (kv_lora_norm, m_kv_lora_norm, v_kv_lora_norm), (q_head_norm, m_q_head_norm, v_q_head_norm),
               (k_head_norm, m_k_head_norm, v_k_head_norm)]
    vec_out = _adam_vectors_call(packs, starts[:len(vectors)], vectors)

    def ordered(kind):
        big = lambda name: res[name][kind][None]
        return [res_ada[kind][None], vec_out[kind][0], vec_out[kind][1], big("adam_w_in"), vec_out[kind][2],
                big("adam_w_uq"), vec_out[kind][3], big("adam_w_ukv"), vec_out[kind][4], vec_out[kind][5],
                big("adam_w_out")]

    return (loss, grad_x[None], *ordered(0), *ordered(1), *ordered(2), *ordered(3))


def _local_step(x2, ada, positions, tgt, norm_w, w_in_shards, q_lora_norm, w_uq_full,
                kv_lora_norm, w_ukv_full, q_head_norm, k_head_norm, w_out_full):
    in_shard = w_in_shards.shape[2]
    ckv_tail = C_GM - 3 * in_shard
    assert 0 <= ckv_tail and ckv_tail + MLA_ROPE_DIM + MLA_WIDTH == in_shard
    last = w_in_shards[3]
    w_in_bf = jnp.concatenate(
        [w_in_shards[0], w_in_shards[1], w_in_shards[2], last[:, :ckv_tail], last[:, ckv_tail + MLA_ROPE_DIM:],
         _rope_to_head_lanes(last[:, ckv_tail:ckv_tail + MLA_ROPE_DIM])], axis=1).astype(BF16)
    w_uq_bf = _to_head_lanes(w_uq_full.reshape(Q_LORA_RANK, MLA_HEADS, MLA_QK_DIM)).reshape(
        Q_LORA_RANK, MLA_PAD_WIDTH).astype(BF16)
    w_ukv_heads = w_ukv_full.reshape(KV_LORA_RANK, MLA_HEADS, 2 * MLA_NOPE_DIM)
    w_uk_bf = _nope_to_head_lanes(w_ukv_heads[:, :, :MLA_NOPE_DIM]).reshape(KV_LORA_RANK, MLA_PAD_WIDTH).astype(BF16)
    w_uv_bf = w_ukv_heads[:, :, MLA_NOPE_DIM:].reshape(KV_LORA_RANK, MLA_WIDTH).astype(BF16)
    w_out_bf = w_out_full.astype(BF16)
    qhn_pad, khn_pad = _to_head_lanes(q_head_norm), _to_head_lanes(k_head_norm)
    cos_t, sin_t = _rope_tables(positions)

    hb, q_sb, k_sb, v_sb, g_sb, c_q, c_kv, g_mla, k_rope = _pre_call(x2, ada, norm_w, w_in_bf)
    q_m, k_m, v_m, cqn, ckvn, q0, k0 = _mla_prep_call(
        c_q, c_kv, k_rope, cos_t, sin_t, q_lora_norm, kv_lora_norm, qhn_pad, khn_pad,
        w_uq_bf, w_uk_bf, w_uv_bf)
    o_sb, r_sb, kstart = _sb_fwd_call(q_sb, k_sb, v_sb)
    o_mla, lse = _mla_fwd_call(q_m, k_m, v_m)
    do_sb, do_mla, dg_sb, dg_mla, dy, g_w_out, d_gate, sq = _out_call(
        o_sb, g_sb, o_mla, g_mla, x2, tgt, ada, w_out_bf)

    dq_sb, dk_sb, dv_sb = _sb_bwd_call(kstart, q_sb, k_sb, v_sb, do_sb, r_sb)
    dq_m, dk_m, dv_m = _mla_bwd_call(q_m, k_m, v_m, do_mla, o_mla, lse)
    (d_cq, d_ckv, d_kr, g_wuq_pad, g_wuk_pad, g_wuv, g_qln, g_kvln, g_qhn, g_khn) = _mla_prep_bwd_call(
        dq_m, dk_m, dv_m, q0, k0, cqn, ckvn, c_q, c_kv, cos_t, sin_t,
        q_lora_norm, kv_lora_norm, qhn_pad, khn_pad, w_uq_bf, w_uk_bf, w_uv_bf)
    grad_x, g_win_pad, d_shift, d_scale, g_norm_w = _dh_call(
        [dq_sb, dk_sb, dv_sb, dg_sb, d_cq, d_ckv, dg_mla, d_kr], hb, x2, dy, ada, norm_w, w_in_bf)

    g_kr = g_win_pad[:, C_KR:]
    g_last = jnp.concatenate([g_win_pad[:, 3 * in_shard:C_GM], g_kr[:, :ROPE_HALF],
                              g_kr[:, HALF_LANES:HALF_LANES + ROPE_HALF], g_win_pad[:, C_GM:C_KR]], axis=1)
    g_w_in = jnp.stack([g_win_pad[:, j * in_shard:(j + 1) * in_shard] for j in range(3)] + [g_last])
    g_w_uq = _from_head_lanes(g_wuq_pad.reshape(Q_LORA_RANK, MLA_HEADS, HEAD_PAD)).reshape(Q_LORA_RANK, -1)
    g_w_ukv = jnp.concatenate(
        [_from_head_lanes(g_wuk_pad.reshape(KV_LORA_RANK, MLA_HEADS, HEAD_PAD))[:, :, :MLA_NOPE_DIM],
         g_wuv.reshape(KV_LORA_RANK, MLA_HEADS, MLA_NOPE_DIM)], axis=2).reshape(KV_LORA_RANK, -1)
    d_ada = jnp.concatenate([d_shift, d_scale, d_gate], axis=1)
    return (jnp.sum(sq), grad_x, g_w_in, g_w_uq, g_w_ukv, g_w_out, d_ada, g_norm_w, g_qln, g_kvln,
            _from_head_lanes(g_qhn), _from_head_lanes(g_khn))
```

```python
import math

import jax
import jax.numpy as jnp
from jax import lax
from jax.experimental import pallas as pl
from jax.experimental.pallas import tpu as pltpu

F32 = jnp.float32
BF16 = jnp.bfloat16
I32 = jnp.int32

D_MODEL = 1024
SB_HEADS = 8
SB_WIDTH = 512
MLA_HEADS = 8
MLA_QK_DIM = 96
MLA_NOPE_DIM = 64
MLA_ROPE_DIM = 32
MLA_WIDTH = 512
Q_LORA_RANK = 384
KV_LORA_RANK = 256
ROPE_THETA = 10000.0
EPS = 1e-6
LANES = 128
HALF_LANES = LANES // 2
HEAD_PAD = 128
MLA_PAD_WIDTH = MLA_HEADS * HEAD_PAD

C_Q, C_K, C_V, C_G = 0, 512, 1024, 1536
C_CQ, C_CKV, C_GM, C_KR = 2048, 2432, 2688, 3200
IN_COLS_PAD = 3328

ADAM_LR = 0.001
ADAM_B1 = 0.9
ADAM_B2 = 0.999
ADAM_EPS = 1e-08
ADAM_WD = 0.01
ADAM_STEP = 10

SB_SCALE = 0.125
SB_GROUP = 4
SB_TOP_ROWS = 160
MLA_SCALE = 1.0 / math.sqrt(MLA_QK_DIM)
LN2 = math.log(2.0)
MLA_SCALE_LOG2 = MLA_SCALE / LN2
MLA_BQ = 1024
MLA_BWD_BQ = 1024
MLA_BK = 1024
MLA_BWD_BK = 1024
MLA_STEP = 512
MLA_BWD_STEP = 256
SB_DEAD = -104.0
MASK_NEG = -1e30

VMEM_LIMIT = 56 * 1024 * 1024
MESH = pl.DeviceIdType.MESH


def _dot(a, b):
    return jnp.dot(a, b, preferred_element_type=F32)


def _dot_nt(a, b):
    return lax.dot_general(a, b, (((1,), (1,)), ((), ())), preferred_element_type=F32)


def _dot_tn(a, b):
    return lax.dot_general(a, b, (((0,), (0,)), ((), ())), preferred_element_type=F32)


def _sigmoid(x):
    return 1.0 / (1.0 + jnp.exp(-x))


def _split_dot(a, m):
    hi = a.astype(BF16)
    lo = (a - hi.astype(F32)).astype(BF16)
    return _dot(hi, m) + _dot(lo, m)


def _params(sem, vmem=None):
    return pltpu.CompilerParams(dimension_semantics=sem, vmem_limit_bytes=vmem)


def _row_tile(s, want):
    return min(want, s)


def _hbm_spec():
    return pl.BlockSpec(memory_space=pltpu.HBM)


def _allgather_rows_call(row):
    def body(in_ref, out_ref, send_sems, recv_sems, loc_sem):
        x, y, c = lax.axis_index("x"), lax.axis_index("y"), lax.axis_index("c")
        flips = [(fx, fy, fc) for fx in (0, 1) for fy in (0, 1) for fc in (0, 1)][1:]

        def peer(r):
            fx, fy, fc = flips[r]
            return ((1 - x) if fx else x, (1 - y) if fy else y, (1 - c) if fc else c)

        def copy(r, slot):
            return pltpu.make_async_remote_copy(
                src_ref=in_ref, dst_ref=out_ref.at[slot], send_sem=send_sems.at[r], recv_sem=recv_sems.at[r],
                device_id=peer(r), device_id_type=MESH)

        local = pltpu.make_async_copy(in_ref, out_ref.at[4 * x + 2 * y + c], loc_sem)
        local.start()
        sends = [copy(r, 4 * x + 2 * y + c) for r in range(7)]
        for cp in sends:
            cp.start()
        for r in range(7):
            px, py, pc = peer(r)
            copy(r, 4 * px + 2 * py + pc).wait_recv()
        for cp in sends:
            cp.wait_send()
        local.wait()

    return pl.pallas_call(
        body, name="gather_rows",
        out_shape=jax.ShapeDtypeStruct((8,) + row.shape, row.dtype),
        in_specs=[_hbm_spec()], out_specs=_hbm_spec(),
        scratch_shapes=[pltpu.SemaphoreType.DMA((7,)), pltpu.SemaphoreType.DMA((7,)), pltpu.SemaphoreType.DMA],
    )(row)


def _gather_call(shards, split):
    n = len(shards)
    halves = [s.shape[1] // 2 for s in shards]

    def body(*refs):
        ins, outs = refs[:n], refs[n:2 * n]
        ici_send, ici_recv, d2d_send, d2d_recv, loc_sems = refs[2 * n:]
        x, y, c = lax.axis_index("x"), lax.axis_index("y"), lax.axis_index("c")
        me = 2 * x + y
        peers = [(1 - x, y), (x, 1 - y), (1 - x, 1 - y)]

        def rows(a, which):
            return pl.ds(pl.multiple_of(which * halves[a], 16), halves[a])

        def ici(a, j, slot):
            px, py = peers[j]
            src, dst = ins[a].at[0], outs[a].at[slot]
            if split[a]:
                src, dst = src.at[rows(a, c)], dst.at[rows(a, c)]
            return pltpu.make_async_remote_copy(
                src_ref=src, dst_ref=dst,
                send_sem=ici_send.at[3 * a + j], recv_sem=ici_recv.at[3 * a + j],
                device_id=(px, py, c), device_id_type=MESH)

        def d2d(a, j, which):
            px, py = peers[j]
            piece = outs[a].at[2 * px + py, rows(a, which)]
            return pltpu.make_async_remote_copy(
                src_ref=piece, dst_ref=piece,
                send_sem=d2d_send.at[3 * a + j], recv_sem=d2d_recv.at[3 * a + j],
                device_id=(x, y, 1 - c), device_id_type=MESH)

        local = [pltpu.make_async_copy(ins[a].at[0], outs[a].at[me], loc_sems.at[a]) for a in range(n)]
        for cp in local:
            cp.start()
        sends = [ici(a, j, me) for a in range(n) for j in range(3)]
        for cp in sends:
            cp.start()
        for a in range(n):
            for j in range(3):
                px, py = peers[j]
                ici(a, j, 2 * px + py).wait_recv()
                if split[a]:
                    cp = d2d(a, j, c)
                    cp.start()
                    sends.append(cp)
        for a in range(n):
            for j in range(3):
                if split[a]:
                    d2d(a, j, 1 - c).wait_recv()
        for cp in sends:
            cp.wait_send()
        for cp in local:
            cp.wait()

    return pl.pallas_call(
        body, name="gather_weights",
        out_shape=[jax.ShapeDtypeStruct((4,) + s.shape[1:], s.dtype) for s in shards],
        in_specs=[_hbm_spec() for _ in shards],
        out_specs=[_hbm_spec() for _ in shards],
        scratch_shapes=[pltpu.SemaphoreType.DMA((3 * n,)), pltpu.SemaphoreType.DMA((3 * n,)),
                        pltpu.SemaphoreType.DMA((3 * n,)), pltpu.SemaphoreType.DMA((3 * n,)),
                        pltpu.SemaphoreType.DMA((n,))],
    )(*shards)


def _exchange_call(grads, small):
    n = len(grads)

    def body(*refs):
        g_in, small_in = refs[:n], refs[n]
        own, sib, packs = refs[n + 1:2 * n + 1], refs[2 * n + 1:3 * n + 1], refs[3 * n + 1]
        ici_send, ici_recv, d2d_send, d2d_recv, sm_send, sm_recv, loc_sems = refs[3 * n + 2:]
        x, y, c = lax.axis_index("x"), lax.axis_index("y"), lax.axis_index("c")
        me = 2 * x + y
        me8 = 4 * x + 2 * y + c
        sibling = (x, y, 1 - c)
        peers = [(1 - x, y), (x, 1 - y), (1 - x, 1 - y)]
        flips = [(fx, fy, fc) for fx in (0, 1) for fy in (0, 1) for fc in (0, 1)][1:]

        def ici(a, j, src_slot, dst_slot):
            px, py = peers[j]
            return pltpu.make_async_remote_copy(
                src_ref=g_in[a].at[src_slot], dst_ref=own[a].at[dst_slot],
                send_sem=ici_send.at[3 * a + j], recv_sem=ici_recv.at[3 * a + j],
                device_id=(px, py, c), device_id_type=MESH)

        def d2d(a, rel, chip, src):
            return pltpu.make_async_remote_copy(
                src_ref=src, dst_ref=sib[a].at[chip],
                send_sem=d2d_send.at[4 * a + rel], recv_sem=d2d_recv.at[4 * a + rel],
                device_id=sibling, device_id_type=MESH)

        def flipped(r):
            fx, fy, fc = flips[r]
            return ((1 - x) if fx else x, (1 - y) if fy else y, (1 - c) if fc else c)

        def sm(r, slot):
            return pltpu.make_async_remote_copy(
                src_ref=small_in, dst_ref=packs.at[slot],
                send_sem=sm_send.at[r], recv_sem=sm_recv.at[r],
                device_id=flipped(r), device_id_type=MESH)

        def peer8(r):
            px, py, pc = flipped(r)
            return 4 * px + 2 * py + pc

        local = [pltpu.make_async_copy(g_in[a].at[me], own[a].at[me], loc_sems.at[a]) for a in range(n)]
        local.append(pltpu.make_async_copy(small_in, packs.at[me8], loc_sems.at[n]))
        for cp in local:
            cp.start()
        sends = []
        for r in range(7):
            sends.append(sm(r, me8))
        for a in range(n):
            for j in range(3):
                px, py = peers[j]
                sends.append(ici(a, j, 2 * px + py, me))
        for cp in sends:
            cp.start()
        for a in range(n):
            cp = d2d(a, 0, me, g_in[a].at[me])
            cp.start()
            sends.append(cp)
        for a in range(n):
            for j in range(3):
                px, py = peers[j]
                ici(a, j, me, 2 * px + py).wait_recv()
                cp = d2d(a, 1 + j, 2 * px + py, own[a].at[2 * px + py])
                cp.start()
                sends.append(cp)
        for a in range(n):
            d2d(a, 0, me, g_in[a].at[me]).wait_recv()
            for j in range(3):
                px, py = peers[j]
                d2d(a, 1 + j, 2 * px + py, g_in[a].at[me]).wait_recv()
        for r in range(7):
            sm(r, peer8(r)).wait_recv()
        for cp in sends:
            cp.wait_send()
        for cp in local:
            cp.wait()

    out_shape = ([jax.ShapeDtypeStruct(g.shape, g.dtype) for g in grads] * 2
                 + [jax.ShapeDtypeStruct((8,) + small.shape, small.dtype)])
    res = pl.pallas_call(
        body, name="exchange_grads",
        out_shape=out_shape,
        in_specs=[_hbm_spec() for _ in range(n + 1)],
        out_specs=[_hbm_spec() for _ in range(2 * n + 1)],
        scratch_shapes=[pltpu.SemaphoreType.DMA((3 * n,)), pltpu.SemaphoreType.DMA((3 * n,)),
                        pltpu.SemaphoreType.DMA((4 * n,)), pltpu.SemaphoreType.DMA((4 * n,)),
                        pltpu.SemaphoreType.DMA((7,)), pltpu.SemaphoreType.DMA((7,)),
                        pltpu.SemaphoreType.DMA((n + 1,))],
    )(*grads, small)
    return res[:n], res[n:2 * n], res[2 * n]


def _ada_call(c_all, w_ada_cols, b_ada_cols):
    def body(c_ref, w_ref, b_ref, o_ref):
        cc = c_ref[...]
        o_ref[...] = _dot((cc * _sigmoid(cc)).astype(BF16), w_ref[...].astype(BF16)) + b_ref[...]

    return pl.pallas_call(
        body, name="ada_fwd",
        out_shape=jax.ShapeDtypeStruct((c_all.shape[0], w_ada_cols.shape[1]), F32),
        compiler_params=pltpu.CompilerParams(vmem_limit_bytes=VMEM_LIMIT),
    )(c_all, w_ada_cols, b_ada_cols)


def _ada_part(j):
    return pl.BlockSpec((1, D_MODEL), lambda i: (0, j))


def _full(shape):
    return pl.BlockSpec(shape, lambda i: (0,) * len(shape))


def _rows(tm, width):
    return pl.BlockSpec((tm, width), lambda i: (i, 0))


def _pre_call(x, ada, norm_w, w_in_bf):
    s = x.shape[0]
    tm = _row_tile(s, 512)
    groups = [(C_Q, 512, BF16), (C_K, 512, BF16), (C_V, 512, BF16), (C_G, 512, F32),
              (C_CQ, Q_LORA_RANK, F32), (C_CKV, KV_LORA_RANK, F32), (C_GM, 512, F32), (C_KR, LANES, F32)]

    def body(x_ref, sh_ref, sc_ref, nw_ref, w_ref, hb_ref, *outs):
        xx = x_ref[...]
        r0 = lax.rsqrt(jnp.mean(xx * xx, axis=-1, keepdims=True) + EPS)
        h = (xx * r0 * nw_ref[...]) * (1.0 + sc_ref[...]) + sh_ref[...]
        hb = h.astype(BF16)
        hb_ref[...] = hb
        for (c0, width, dt), o_ref in zip(groups, outs):
            o_ref[...] = _dot(hb, w_ref[:, c0:c0 + width]).astype(dt)

    return pl.pallas_call(
        body, name="pre_proj",
        grid=(s // tm,),
        in_specs=[_rows(tm, D_MODEL), _ada_part(0), _ada_part(1), _full((1, D_MODEL)),
                  _full((D_MODEL, IN_COLS_PAD))],
        out_specs=[_rows(tm, D_MODEL)] + [_rows(tm, w) for _, w, _ in groups],
        out_shape=[jax.ShapeDtypeStruct((s, D_MODEL), BF16)]
        + [jax.ShapeDtypeStruct((s, w), dt) for _, w, dt in groups],
        compiler_params=_params(("arbitrary",), VMEM_LIMIT),
    )(x, ada, ada, norm_w, w_in_bf)


def _rope(t, cos_t, sin_t):
    return t * cos_t + pltpu.roll(t, HALF_LANES, 1) * sin_t


def _rope_adjoint(d, cos_t, sin_t):
    return d * cos_t + pltpu.roll(d * sin_t, HALF_LANES, 1)


def _mla_prep_call(c_q, c_kv, k_rope, cos_t, sin_t, q_lora_norm, kv_lora_norm, qhn_pad, khn_pad,
                   w_uq_bf, w_uk_bf, w_uv_bf):
    s = c_q.shape[0]
    tm = _row_tile(s, 256)

    def body(cq_ref, ckv_ref, kr_ref, cos_ref, sin_ref, qln_ref, kvln_ref, qhn_ref, khn_ref,
             wuq_ref, wuk_ref, wuv_ref, q_ref, k_ref, v_ref, cqn_ref, ckvn_ref, q0_ref, k0_ref):
        cq = cq_ref[...]
        cqn = (cq * lax.rsqrt(jnp.mean(cq * cq, axis=-1, keepdims=True) + EPS) * qln_ref[...]).astype(BF16)
        cqn_ref[...] = cqn
        ckv = ckv_ref[...]
        ckvn = (ckv * lax.rsqrt(jnp.mean(ckv * ckv, axis=-1, keepdims=True) + EPS) * kvln_ref[...]).astype(BF16)
        ckvn_ref[...] = ckvn
        v_ref[...] = _dot(ckvn, wuv_ref[...]).astype(BF16)
        q0_ref[...] = _dot(cqn, wuq_ref[...])
        k0_ref[...] = _dot(ckvn, wuk_ref[...])
        cos_t, sin_t = cos_ref[...], sin_ref[...]
        kr = kr_ref[...]
        heads = [slice(h * HEAD_PAD, (h + 1) * HEAD_PAD) for h in range(MLA_HEADS)]
        for cols in heads:
            k0_ref[:, cols] = k0_ref[:, cols] + kr

        def inv_rms(ref):
            sums = [jnp.sum(ref[:, cols] * ref[:, cols], axis=-1, keepdims=True) for cols in heads]
            return [lax.rsqrt(t * (1.0 / MLA_QK_DIM) + EPS) for t in sums]

        rqs, rks = inv_rms(q0_ref), inv_rms(k0_ref)
        for cols, rq, rk in zip(heads, rqs, rks):
            q_ref[:, cols] = (_rope(q0_ref[:, cols] * rq * qhn_ref[...], cos_t, sin_t) * MLA_SCALE_LOG2).astype(BF16)
            k_ref[:, cols] = _rope(k0_ref[:, cols] * rk * khn_ref[...], cos_t, sin_t).astype(BF16)

    return pl.pallas_call(
        body, name="mla_prep",
        grid=(s // tm,),
        in_specs=[_rows(tm, Q_LORA_RANK), _rows(tm, KV_LORA_RANK), _rows(tm, LANES),
                  _rows(tm, LANES), _rows(tm, LANES),
                  _full((1, Q_LORA_RANK)), _full((1, KV_LORA_RANK)), _full((1, LANES)), _full((1, LANES)),
                  _full((Q_LORA_RANK, MLA_PAD_WIDTH)), _full((KV_LORA_RANK, MLA_PAD_WIDTH)),
                  _full((KV_LORA_RANK, MLA_WIDTH))],
        out_specs=[_rows(tm, MLA_PAD_WIDTH), _rows(tm, MLA_PAD_WIDTH), _rows(tm, MLA_WIDTH),
                   _rows(tm, Q_LORA_RANK), _rows(tm, KV_LORA_RANK),
                   _rows(tm, MLA_PAD_WIDTH), _rows(tm, MLA_PAD_WIDTH)],
        out_shape=[jax.ShapeDtypeStruct((s, MLA_PAD_WIDTH), BF16), jax.ShapeDtypeStruct((s, MLA_PAD_WIDTH), BF16),
                   jax.ShapeDtypeStruct((s, MLA_WIDTH), BF16),
                   jax.ShapeDtypeStruct((s, Q_LORA_RANK), BF16), jax.ShapeDtypeStruct((s, KV_LORA_RANK), BF16),
                   jax.ShapeDtypeStruct((s, MLA_PAD_WIDTH), F32), jax.ShapeDtypeStruct((s, MLA_PAD_WIDTH), F32)],
        compiler_params=_params(("arbitrary",), VMEM_LIMIT),
    )(c_q, c_kv, k_rope, cos_t, sin_t, q_lora_norm, kv_lora_norm, qhn_pad, khn_pad,
      w_uq_bf, w_uk_bf, w_uv_bf)


def _log_sigmoid_pair(z):
    ls = jnp.minimum(z, 0.0) - jnp.log(1.0 + jnp.exp(-jnp.abs(z)))
    return ls, ls - z


def _pair(hh):
    return slice((hh // 2) * LANES, (hh // 2 + 1) * LANES)


def _set_rows(full, rows, part):
    pieces = ([full[:rows.start]] if rows.start else []) + [part]
    if rows.stop < full.shape[0]:
        pieces.append(full[rows.stop:])
    return pieces[0] if len(pieces) == 1 else jnp.concatenate(pieces, axis=0)


def _sb_fwd_call(q, k, v):
    s = q.shape[0]
    bq = _row_tile(s, 256)
    nq = s // bq
    nh = SB_GROUP
    width = nh * HALF_LANES
    top_rows = min(bq, SB_TOP_ROWS)

    def body(q_ref, k_ref, v_ref, o_ref, r_ref, ks_ref):
        hp, i = pl.program_id(0), pl.program_id(1)
        lane = lax.broadcasted_iota(I32, (bq, LANES), 1)
        row = lax.broadcasted_iota(I32, (bq, bq), 0)
        col = lax.broadcasted_iota(I32, (bq, bq), 1)
        strict = col < row
        later = jnp.where(row > col, 1.0, 0.0).astype(BF16)
        masks = [_head_mask(lane, hh).astype(BF16) for hh in range(2)]
        qms = [q_ref[:, _pair(hh)] * jnp.asarray(SB_SCALE, BF16) * masks[hh % 2] for hh in range(nh)]

        def walk(blocks, state):
            chains = [(kb, diagonal, slice(r0, r1), hh) for kb, diagonal, r0, r1 in blocks for hh in range(nh)]
            keys = lambda kb: pl.ds(pl.multiple_of(kb * bq, bq), bq)
            zs = [_dot_nt(qms[hh][rs], k_ref[keys(kb), _pair(hh)]) for kb, _, rs, hh in chains]
            pairs = []
            for z, (_, diagonal, rs, _) in zip(zs, chains):
                ls, lk = _log_sigmoid_pair(z)
                pairs.append((ls, jnp.where(strict[rs], lk, 0.0) if diagonal else lk))
            sums = [_split_dot(lk, later) for _, lk in pairs]
            runs = [st[0] for st in state]
            ws = []
            for (ls, lk), after, (_, diagonal, rs, hh) in zip(pairs, sums, chains):
                w = jnp.exp(ls + (after + runs[hh][rs]))
                ws.append((jnp.where(strict[rs], w, 0.0) if diagonal else w).astype(BF16))
                runs[hh] = _set_rows(runs[hh], rs, runs[hh][rs] + jnp.sum(lk, axis=1, keepdims=True))
            accs = [st[1] for st in state]
            for w, (kb, _, rs, hh) in zip(ws, chains):
                accs[hh] = _set_rows(accs[hh], rs, accs[hh][rs] + _dot(w, v_ref[keys(kb), _pair(hh)]))
            return tuple(zip(runs, accs))

        def alive(state, r0=0):
            top = jnp.max(state[0][0][r0:])
            for st in state[1:]:
                top = jnp.maximum(top, jnp.max(st[0][r0:]))
            return (top > SB_DEAD).astype(I32)

        def finish(state, first, low_rows_took_left):
            ks_ref[hp, i] = 2 * first + low_rows_took_left
            for pair in range(nh // 2):
                o_ref[:, _pair(2 * pair)] = jnp.where(lane < HALF_LANES, state[2 * pair][1], state[2 * pair + 1][1])
                r_ref[:, _pair(2 * pair)] = jnp.where(lane < HALF_LANES, state[2 * pair][0], state[2 * pair + 1][0])

        zero = ((jnp.zeros((bq, 1), F32), jnp.zeros((bq, LANES), F32)),) * nh

        @pl.when(i == 0)
        def _():
            finish(walk([(0, True, 0, bq)], zero), 0, 0)

        @pl.when(i > 0)
        def _():
            state = walk([(i, True, 0, bq), (i - 1, False, 0, top_rows)], zero)
            low_alive = alive(state, top_rows)
            if top_rows < bq:
                state = lax.cond(low_alive > 0, lambda st: walk([(i - 1, False, top_rows, bq)], st), lambda st: st,
                                 state)

            def cond(carry):
                return jnp.logical_and(carry[0] >= 0, carry[1] > 0)

            def step(carry):
                state = walk([(carry[0], False, 0, bq)], carry[2])
                return carry[0] - 1, alive(state), state

            kb, _, state = lax.while_loop(cond, step, (i - 2, alive(state), state))
            finish(state, kb + 1, low_alive)

    return pl.pallas_call(
        body, name="sb_fwd",
        grid=(SB_HEADS // nh, nq),
        in_specs=[pl.BlockSpec((bq, width), lambda h, i: (i, h)),
                  pl.BlockSpec((s, width), lambda h, i: (0, h)),
                  pl.BlockSpec((s, width), lambda h, i: (0, h))],
        out_specs=[pl.BlockSpec((bq, width), lambda h, i: (i, h)),
                   pl.BlockSpec((bq, width), lambda h, i: (i, h)),
                   pl.BlockSpec(memory_space=pltpu.SMEM)],
        out_shape=[jax.ShapeDtypeStruct((s, SB_WIDTH), F32), jax.ShapeDtypeStruct((s, SB_WIDTH), F32),
                   jax.ShapeDtypeStruct((SB_HEADS // nh, nq), I32)],
        compiler_params=_params(("arbitrary", "arbitrary"), VMEM_LIMIT),
    )(q, k, v)


def _mla_fwd_call(q, k, v):
    s = q.shape[0]
    bq = _row_tile(s, MLA_BQ)
    bk = _row_tile(s, MLA_BK)
    nq = s // bq
    assert bk % bq == 0
    step = min(bk // 2, MLA_STEP)
    nsub = bk // step
    assert nsub % 2 == 0

    def body(q_ref, k_ref, v_ref, o_ref, lse_ref, p_ref, s_ref):
        i = pl.program_id(1)
        lane = lax.broadcasted_iota(I32, (bq, LANES), 1)
        row = lax.broadcasted_iota(I32, (step, bq), 1)
        col = lax.broadcasted_iota(I32, (step, bq), 0)
        n_full = (i * bq) // bk

        def keys(g):
            return pl.ds(pl.multiple_of(g * step, step), step)

        def join(left, right, qlo):
            return right if qlo == 0 else jnp.concatenate([left[:, :qlo], right], axis=1)

        def put_scores(g, slot, qlo=0):
            for hh in range(2):
                cols = slice(hh * HEAD_PAD, (hh + 1) * HEAD_PAD)
                s_ref[slot, hh, :, qlo:] = _dot_nt(k_ref[keys(g), cols], q_ref[qlo:, cols])

        def add_pv(carry, g, slot, qlo=0):
            vblk = v_ref[keys(g), :]
            out = []
            for hh, (m, l, acc, alpha) in enumerate(carry):
                upd = alpha[:, qlo:] * acc[:, qlo:] + _dot_tn(vblk, p_ref[slot, hh, :, qlo:])
                out.append((m, l, join(acc, upd, qlo), alpha))
            return tuple(out)

        def substep(g, slot, carry, masked, prefetch, qlo=0, next_qlo=0, prev_qlo=0):
            if prefetch:
                put_scores(g + 1, 1 - slot, next_qlo)
            carry = add_pv(carry, jnp.maximum(g - 1, 0), 1 - slot, prev_qlo)
            new = []
            for hh in range(2):
                m, l, acc, _ = carry[hh]
                sc = s_ref[slot, hh, :, qlo:]
                if masked:
                    sc = jnp.where(col[:, qlo:] + g * step <= row[:, qlo:] + i * bq, sc, MASK_NEG)
                m_new = jnp.maximum(m[:, qlo:], jnp.max(sc, axis=0, keepdims=True))
                p = jnp.exp2(sc - m_new)
                alpha = jnp.exp2(m[:, qlo:] - m_new)
                l_new = alpha * l[:, qlo:] + jnp.sum(p, axis=0, keepdims=True)
                p_ref[slot, hh, :, qlo:] = p.astype(BF16)
                new.append((join(m, m_new, qlo), join(l, l_new, qlo), acc, join(jnp.ones_like(m), alpha, qlo)))
            return tuple(new)

        def first_query(t, masked):
            return t * step if (masked and bk == bq and 0 <= t < nsub) else 0

        def chunk(kb, carry, masked):
            for t in range(nsub):
                last = masked and t == nsub - 1
                carry = substep(nsub * kb + t, t % 2, carry, masked, not last, first_query(t, masked),
                                first_query(t + 1, masked), first_query(t - 1, masked))
            return carry

        p_ref[1] = jnp.zeros_like(p_ref[1])
        put_scores(0, 0)
        one = (jnp.full((1, bq), MASK_NEG, F32), jnp.zeros((1, bq), F32), jnp.zeros((LANES, bq), F32),
               jnp.ones((1, bq), F32))
        carry = lax.fori_loop(0, n_full, lambda kb, cr: chunk(kb, cr, False), (one, one))
        carry = chunk(n_full, carry, True)
        (m0, l0, a0, _), (m1, l1, a1, _) = add_pv(carry, nsub * n_full + nsub - 1, 1, first_query(nsub - 1, True))
        o_ref[...] = jnp.where(lane < HALF_LANES, (a0 / l0).T, (a1 / l1).T)
        sub = lax.broadcasted_iota(I32, (8, bq), 0)
        lse_ref[...] = jnp.where(sub == 0, m0 + jnp.log2(l0), jnp.where(sub == 1, m1 + jnp.log2(l1), 0.0))

    return pl.pallas_call(
        body, name="mla_fwd",
        grid=(4, nq),
        in_specs=[pl.BlockSpec((bq, 2 * HEAD_PAD), lambda h, i: (i, h)),
                  pl.BlockSpec((s, 2 * HEAD_PAD), lambda h, i: (0, h)),
                  pl.BlockSpec((s, LANES), lambda h, i: (0, h))],
        out_specs=[pl.BlockSpec((bq, LANES), lambda h, i: (i, h)),
                   pl.BlockSpec((None, 8, bq), lambda h, i: (h, 0, i))],
        out_shape=[jax.ShapeDtypeStruct((s, MLA_WIDTH), F32), jax.ShapeDtypeStruct((4, 8, s), F32)],
        scratch_shapes=[pltpu.VMEM((2, 2, step, bq), BF16), pltpu.VMEM((2, 2, step, bq), F32)],
        compiler_params=_params(("arbitrary", "arbitrary"), VMEM_LIMIT),
    )(q, k, v)


def _out_call(o_sb, g_sb, o_mla, g_mla, x, target, ada, w_out_bf):
    s = x.shape[0]
    tm = _row_tile(s, 256)

    def body(osb_ref, gsb_ref, oml_ref, gml_ref, x_ref, t_ref, gate_ref, w_ref,
             dosb_ref, doml_ref, dgsb_ref, dgml_ref, dy_ref, gw_ref, dgate_ref, sq_ref):
        @pl.when(pl.program_id(0) == 0)
        def _():
            gw_ref[...] = jnp.zeros_like(gw_ref)
            dgate_ref[...] = jnp.zeros_like(dgate_ref)
            sq_ref[...] = jnp.zeros_like(sq_ref)

        g_s, g_m = gsb_ref[...], gml_ref[...]
        sig_s, sig_m = _sigmoid(g_s), _sigmoid(g_m)
        silu_s, silu_m = g_s * sig_s, g_m * sig_m
        o_s, o_m = osb_ref[...], oml_ref[...]
        mixed = jnp.concatenate([o_s * silu_s, o_m * silu_m], axis=1).astype(BF16)
        u = _dot(mixed, w_ref[...])
        gate_v = gate_ref[...]
        err = x_ref[...] + gate_v * u - t_ref[...]
        sq_ref[...] += jnp.sum(err * err, axis=0, keepdims=True)
        dy = err * (1.0 / D_MODEL)
        dy_ref[...] = dy
        dgate_ref[...] += jnp.sum(dy * u, axis=0, keepdims=True)
        du = (dy * gate_v).astype(BF16)
        gw_ref[...] += _dot_tn(mixed, du)
        dmix = _dot_nt(du, w_ref[...])
        dm_s, dm_m = dmix[:, :SB_WIDTH], dmix[:, SB_WIDTH:]
        dosb_ref[...] = (dm_s * silu_s).astype(BF16)
        doml_ref[...] = (dm_m * silu_m).astype(BF16)
        dgsb_ref[...] = (dm_s * o_s * (sig_s * (1.0 + g_s * (1.0 - sig_s)))).astype(BF16)
        dgml_ref[...] = (dm_m * o_m * (sig_m * (1.0 + g_m * (1.0 - sig_m)))).astype(BF16)

    return pl.pallas_call(
        body, name="out_proj_loss",
        grid=(s // tm,),
        in_specs=[_rows(tm, 512), _rows(tm, 512), _rows(tm, 512), _rows(tm, 512),
                  _rows(tm, D_MODEL), _rows(tm, D_MODEL), _ada_part(2), _full((D_MODEL, D_MODEL))],
        out_specs=[_rows(tm, 512), _rows(tm, 512), _rows(tm, 512), _rows(tm, 512), _rows(tm, D_MODEL),
                   _full((D_MODEL, D_MODEL)), _full((1, D_MODEL)), _full((1, D_MODEL))],
        out_shape=[jax.ShapeDtypeStruct((s, 512), BF16)] * 4
        + [jax.ShapeDtypeStruct((s, D_MODEL), F32), jax.ShapeDtypeStruct((D_MODEL, D_MODEL), F32),
           jax.ShapeDtypeStruct((1, D_MODEL), F32), jax.ShapeDtypeStruct((1, D_MODEL), F32)],
        compiler_params=_params(("arbitrary",), VMEM_LIMIT),
    )(o_sb, g_sb, o_mla, g_mla, x, target, ada, w_out_bf)


def _head_mask(lane, hh):
    return jnp.where((lane >= HALF_LANES) if hh else (lane < HALF_LANES), 1.0, 0.0)


def _pick_lane(packed, lane, which):
    return jnp.sum(jnp.where(lane == which, packed, 0.0), axis=1, keepdims=True)


def _sb_bwd_call(kstart, q, k, v, do, rfin):
    s = q.shape[0]
    bq = _row_tile(s, 256)
    nq = s // bq
    nh = SB_GROUP
    width = nh * HALF_LANES
    top_rows = min(bq, SB_TOP_ROWS)

    def body(ks_ref, q_ref, k_ref, v_ref, do_ref, r_ref, dq_ref, dk_ref, dv_ref):
        hp, i = pl.program_id(0), pl.program_id(1)

        @pl.when(i == 0)
        def _():
            dk_ref[...] = jnp.zeros_like(dk_ref)
            dv_ref[...] = jnp.zeros_like(dv_ref)

        lane = lax.broadcasted_iota(I32, (bq, LANES), 1)
        row = lax.broadcasted_iota(I32, (bq, bq), 0)
        col = lax.broadcasted_iota(I32, (bq, bq), 1)
        upto = jnp.where(row <= col, 1.0, 0.0).astype(BF16)
        before = jnp.where(row < col, 1.0, 0.0).astype(BF16)
        masks = [_head_mask(lane, hh).astype(BF16) for hh in range(2)]
        qms = [q_ref[:, _pair(hh)] * jnp.asarray(SB_SCALE, BF16) * masks[hh % 2] for hh in range(nh)]
        doms = [do_ref[:, _pair(hh)] * masks[hh % 2] for hh in range(nh)]
        totals = [_pick_lane(r_ref[:, _pair(hh)], lane, HALF_LANES * (hh % 2)) for hh in range(nh)]
        strict = col < row

        def walk(blocks, state):
            chains = [(kb, diagonal, slice(r0, r1), hh) for kb, diagonal, r0, r1 in blocks for hh in range(nh)]
            keys = lambda kb: pl.ds(pl.multiple_of(kb * bq, bq), bq)
            cut = lambda x, diagonal, rs: jnp.where(strict[rs], x, 0.0) if diagonal else x
            zs = [_dot_nt(qms[hh][rs], k_ref[keys(kb), _pair(hh)]) for kb, _, rs, hh in chains]
            dws = [_dot_nt(doms[hh][rs], v_ref[keys(kb), _pair(hh)]) for kb, _, rs, hh in chains]
            pairs = []
            for z, (_, diagonal, rs, _) in zip(zs, chains):
                ls, lk = _log_sigmoid_pair(z)
                pairs.append((ls, cut(lk, diagonal, rs)))
            incls = [_split_dot(lk, upto) for _, lk in pairs]
            pres = [st[0] for st in state]
            ws, gs = [], []
            for (ls, lk), incl, dw, (_, diagonal, rs, hh) in zip(pairs, incls, dws, chains):
                w = cut(jnp.exp(ls + ((totals[hh][rs] - pres[hh][rs]) - incl)), diagonal, rs)
                ws.append(w.astype(BF16))
                gs.append(w * dw)
                pres[hh] = _set_rows(pres[hh], rs, pres[hh][rs] + jnp.sum(lk, axis=1, keepdims=True))
            gsums = [_dot(g.astype(BF16), before) for g in gs]
            gpres = [st[1] for st in state]
            dzs = []
            for (ls, _), g, gsum, (_, diagonal, rs, hh) in zip(pairs, gs, gsums, chains):
                dzs.append(cut(g - jnp.exp(ls) * (g + (gpres[hh][rs] + gsum)), diagonal, rs).astype(BF16))
                gpres[hh] = _set_rows(gpres[hh], rs, gpres[hh][rs] + jnp.sum(g, axis=1, keepdims=True))
            dqs = [st[2] for st in state]
            dk_parts, dv_parts = [], []
            for dzb, w, (kb, _, rs, hh) in zip(dzs, ws, chains):
                dk_parts.append(_dot_tn(dzb, qms[hh][rs]))
                dv_parts.append(_dot_tn(w, doms[hh][rs]))
                dqs[hh] = _set_rows(dqs[hh], rs, dqs[hh][rs] + _dot(dzb, k_ref[keys(kb), _pair(hh)]))
            for b, (kb, _, _, _) in enumerate(blocks):
                for pair in range(nh // 2):
                    c0 = b * nh + 2 * pair
                    dk_ref[keys(kb), _pair(2 * pair)] += dk_parts[c0] + dk_parts[c0 + 1]
                    dv_ref[keys(kb), _pair(2 * pair)] += dv_parts[c0] + dv_parts[c0 + 1]
            return tuple(zip(pres, gpres, dqs))

        def finish(state):
            for pair in range(nh // 2):
                both = jnp.where(lane < HALF_LANES, state[2 * pair][2], state[2 * pair + 1][2])
                dq_ref[:, _pair(2 * pair)] = (both * SB_SCALE).astype(BF16)

        zero = ((jnp.zeros((bq, 1), F32), jnp.zeros((bq, 1), F32), jnp.zeros((bq, LANES), F32)),) * nh

        @pl.when(i == 0)
        def _():
            finish(walk([(0, True, 0, bq)], zero))

        @pl.when(i > 0)
        def _():
            first, low_rows_took_left = ks_ref[hp, i] // 2, ks_ref[hp, i] % 2
            state = lax.fori_loop(first, i - 1, lambda kb, st: walk([(kb, False, 0, bq)], st), zero)
            if top_rows < bq:
                state = lax.cond(low_rows_took_left > 0, lambda st: walk([(i - 1, False, top_rows, bq)], st),
                                 lambda st: st, state)
            finish(walk([(i - 1, False, 0, top_rows), (i, True, 0, bq)], state))

    return pl.pallas_call(
        body, name="sb_bwd",
        grid_spec=pltpu.PrefetchScalarGridSpec(
            num_scalar_prefetch=1, grid=(SB_HEADS // nh, nq),
            in_specs=[pl.BlockSpec((bq, width), lambda h, i, ks: (i, h)),
                      pl.BlockSpec((s, width), lambda h, i, ks: (0, h), pipeline_mode=pl.Buffered(1)),
                      pl.BlockSpec((s, width), lambda h, i, ks: (0, h), pipeline_mode=pl.Buffered(1)),
                      pl.BlockSpec((bq, width), lambda h, i, ks: (i, h)),
                      pl.BlockSpec((bq, width), lambda h, i, ks: (i, h))],
            out_specs=[pl.BlockSpec((bq, width), lambda h, i, ks: (i, h)),
                       pl.BlockSpec((s, width), lambda h, i, ks: (0, h), pipeline_mode=pl.Buffered(1)),
                       pl.BlockSpec((s, width), lambda h, i, ks: (0, h), pipeline_mode=pl.Buffered(1))]),
        out_shape=[jax.ShapeDtypeStruct((s, SB_WIDTH), BF16), jax.ShapeDtypeStruct((s, SB_WIDTH), F32),
                   jax.ShapeDtypeStruct((s, SB_WIDTH), F32)],
        compiler_params=_params(("arbitrary", "arbitrary"), VMEM_LIMIT),
    )(kstart, q, k, v, do, rfin)


def _mla_bwd_call(q, k, v, do, o, lse):
    s = q.shape[0]
    bq = _row_tile(s, MLA_BWD_BQ)
    bk = _row_tile(s, MLA_BWD_BK)
    nq = s // bq
    assert bk % bq == 0
    step = min(bk // 2, MLA_BWD_STEP)
    nsub = bk // step
    assert nsub % 2 == 0

    def body(q_ref, k_ref, v_ref, do_ref, o_ref, lse_ref, dq_ref, dk_ref, dv_ref, dom_ref, s_ref, dp_ref, pb_ref,
             ds_ref):
        i = pl.program_id(1)

        @pl.when(i == 0)
        def _():
            dk_ref[...] = jnp.zeros_like(dk_ref)
            dv_ref[...] = jnp.zeros_like(dv_ref)

        lane = lax.broadcasted_iota(I32, (bq, LANES), 1)
        row = lax.broadcasted_iota(I32, (step, bq), 1)
        col = lax.broadcasted_iota(I32, (step, bq), 0)
        n_full = (i * bq) // bk
        do2 = do_ref[...]
        prod = do2.astype(F32) * o_ref[...]
        ones = jnp.ones((8, LANES), BF16)
        deltas, lses = [], []
        for hh in range(2):
            head = _head_mask(lane, hh)
            dom_ref[hh] = do2 * head.astype(BF16)
            part = prod * head
            hi = part.astype(BF16)
            lo = (part - hi.astype(F32)).astype(BF16)
            deltas.append((_dot_nt(ones, hi) + _dot_nt(ones, lo))[0:1])
            lses.append(lse_ref[hh:hh + 1, :])

        def keys(g):
            return pl.ds(pl.multiple_of(g * step, step), step)

        def heads():
            return [(hh, slice(hh * HEAD_PAD, (hh + 1) * HEAD_PAD)) for hh in range(2)]

        def put_products(g, slot, qlo=0):
            vblk = v_ref[keys(g), :]
            for hh, cols in heads():
                s_ref[slot, hh, :, qlo:] = _dot_nt(k_ref[keys(g), cols], q_ref[qlo:, cols])
                dp_ref[slot, hh, :, qlo:] = _dot_nt(vblk, dom_ref[hh, qlo:, :])

        def add_grads(dqs, g, slot, qlo=0):
            rows = keys(g)
            new, dv_parts = [], []
            for hh, cols in heads():
                ds = ds_ref[slot, hh, :, qlo:]
                dk_ref[rows, cols] += _dot(ds, q_ref[qlo:, cols])
                dv_parts.append(_dot(pb_ref[slot, hh, :, qlo:], dom_ref[hh, qlo:, :]))
                upd = dqs[hh][:, qlo:] + _dot_tn(k_ref[rows, cols], ds)
                new.append(upd if qlo == 0 else jnp.concatenate([dqs[hh][:, :qlo], upd], axis=1))
            dv_ref[rows, :] += dv_parts[0] + dv_parts[1]
            return tuple(new)

        def substep(g, slot, dqs, masked, prefetch, qlo=0, next_qlo=0, prev_qlo=0):
            if prefetch:
                put_products(g + 1, 1 - slot, next_qlo)
            dqs = add_grads(dqs, jnp.maximum(g - 1, 0), 1 - slot, prev_qlo)
            for hh, _ in heads():
                p = jnp.exp2(s_ref[slot, hh, :, qlo:] - lses[hh][:, qlo:])
                if masked:
                    p = jnp.where(col[:, qlo:] + g * step <= row[:, qlo:] + i * bq, p, 0.0)
                ds_ref[slot, hh, :, qlo:] = (p * (dp_ref[slot, hh, :, qlo:] - deltas[hh][:, qlo:])).astype(BF16)
                pb_ref[slot, hh, :, qlo:] = p.astype(BF16)
            return dqs

        def first_query(t, masked):
            return t * step if (masked and bk == bq and 0 <= t < nsub) else 0

        def chunk(kb, dqs, masked):
            for t in range(nsub):
                last = masked and t == nsub - 1
                dqs = substep(nsub * kb + t, t % 2, dqs, masked, not last, first_query(t, masked),
                              first_query(t + 1, masked), first_query(t - 1, masked))
            return dqs

        ds_ref[1] = jnp.zeros_like(ds_ref[1])
        pb_ref[1] = jnp.zeros_like(pb_ref[1])
        put_products(0, 0)
        zero = jnp.zeros((HEAD_PAD, bq), F32)
        dqs = lax.fori_loop(0, n_full, lambda kb, dqs: chunk(kb, dqs, False), (zero, zero))
        dqs = chunk(n_full, dqs, True)
        dqs = add_grads(dqs, nsub * n_full + nsub - 1, 1, first_query(nsub - 1, True))
        dq_ref[:, :HEAD_PAD] = dqs[0].T * MLA_SCALE
        dq_ref[:, HEAD_PAD:] = dqs[1].T * MLA_SCALE

    return pl.pallas_call(
        body, name="mla_bwd",
        grid=(4, nq),
        in_specs=[pl.BlockSpec((bq, 2 * HEAD_PAD), lambda h, i: (i, h)),
                  pl.BlockSpec((s, 2 * HEAD_PAD), lambda h, i: (0, h)),
                  pl.BlockSpec((s, LANES), lambda h, i: (0, h)),
                  pl.BlockSpec((bq, LANES), lambda h, i: (i, h)),
                  pl.BlockSpec((bq, LANES), lambda h, i: (i, h)),
                  pl.BlockSpec((None, 8, bq), lambda h, i: (h, 0, i))],
        out_specs=[pl.BlockSpec((bq, 2 * HEAD_PAD), lambda h, i: (i, h)),
                   pl.BlockSpec((s, 2 * HEAD_PAD), lambda h, i: (0, h), pipeline_mode=pl.Buffered(1)),
                   pl.BlockSpec((s, LANES), lambda h, i: (0, h), pipeline_mode=pl.Buffered(1))],
        out_shape=[jax.ShapeDtypeStruct((s, MLA_PAD_WIDTH), F32), jax.ShapeDtypeStruct((s, MLA_PAD_WIDTH), F32),
                   jax.ShapeDtypeStruct((s, MLA_WIDTH), F32)],
        scratch_shapes=[pltpu.VMEM((2, bq, LANES), BF16),
                        pltpu.VMEM((2, 2, step, bq), F32), pltpu.VMEM((2, 2, step, bq), F32),
                        pltpu.VMEM((2, 2, step, bq), BF16), pltpu.VMEM((2, 2, step, bq), BF16)],
        compiler_params=_params(("arbitrary", "arbitrary"), VMEM_LIMIT),
    )(q, k, v, do, o, lse)


def _rms_bwd(d_out, inp, r, weight, n):
    normed = inp * r
    gw = d_out * weight
    d_in = r * (gw - normed * (jnp.sum(gw * normed, axis=-1, keepdims=True) * (1.0 / n)))
    return d_in, d_out * normed


def _mla_prep_bwd_call(dq, dk, dv, q0, k0, cqn, ckvn, c_q, c_kv, cos_t, sin_t,
                       q_lora_norm, kv_lora_norm, qhn_pad, khn_pad, w_uq_bf, w_uk_bf, w_uv_bf):
    s = dq.shape[0]
    tm = _row_tile(s, 256)

    def body(dq_ref, dk_ref, dv_ref, q0_ref, k0_ref, cqn_ref, ckvn_ref, cq_ref, ckv_ref,
             cos_ref, sin_ref, qln_ref, kvln_ref, qhn_ref, khn_ref, wuq_ref, wuk_ref, wuv_ref,
             dcq_ref, dckv_ref, dkr_ref, gwuq_ref, gwuk_ref, gwuv_ref, gqln_ref, gkvln_ref, gqhn_ref, gkhn_ref,
             dq0_ref, dk0_ref, tmp_ref):
        @pl.when(pl.program_id(0) == 0)
        def _():
            for ref in (gwuq_ref, gwuk_ref, gwuv_ref, gqln_ref, gkvln_ref, gqhn_ref, gkhn_ref):
                ref[...] = jnp.zeros_like(ref)

        cos_t, sin_t = cos_ref[...], sin_ref[...]
        lane = lax.broadcasted_iota(I32, (tm, LANES), 1)
        rope_lanes = jnp.logical_or(lane < ROPE_HALF,
                                    jnp.logical_and(lane >= HALF_LANES, lane < HALF_LANES + ROPE_HALF))
        heads = [slice(h * HEAD_PAD, (h + 1) * HEAD_PAD) for h in range(MLA_HEADS)]

        def head_norm_bwd(d_ref, x0_ref, w_ref, out_ref, scale):
            w = w_ref[...]
            inv = [lax.rsqrt(jnp.sum(x0_ref[:, cols] * x0_ref[:, cols], axis=-1, keepdims=True)
                             * (1.0 / MLA_QK_DIM) + EPS) for cols in heads]
            for cols in heads:
                tmp_ref[:, cols] = _rope_adjoint(d_ref[:, cols] * scale, cos_t, sin_t)
            dots = [jnp.sum(tmp_ref[:, cols] * w * (x0_ref[:, cols] * r), axis=-1, keepdims=True)
                    for cols, r in zip(heads, inv)]
            g_w = jnp.zeros((1, LANES), F32)
            rope_sum = jnp.zeros((tm, LANES), F32)
            for cols, r, dot in zip(heads, inv, dots):
                normed = x0_ref[:, cols] * r
                d_n = tmp_ref[:, cols]
                d_x0 = r * (d_n * w - normed * (dot * (1.0 / MLA_QK_DIM)))
                out_ref[:, cols] = d_x0.astype(BF16)
                g_w = g_w + jnp.sum(d_n * normed, axis=0, keepdims=True)
                rope_sum = rope_sum + jnp.where(rope_lanes, d_x0, 0.0)
            return g_w, rope_sum

        g_qhn, _ = head_norm_bwd(dq_ref, q0_ref, qhn_ref, dq0_ref, 1.0)
        g_khn, d_kr = head_norm_bwd(dk_ref, k0_ref, khn_ref, dk0_ref, LN2)
        cqn, ckvn = cqn_ref[...], ckvn_ref[...]
        d_q0b, d_k0b, dvb = dq0_ref[...], dk0_ref[...], dv_ref[...].astype(BF16)
        d_cqn = _dot_nt(d_q0b, wuq_ref[...])
        gwuq_ref[...] += _dot_tn(cqn, d_q0b)
        d_ckvn = _dot_nt(d_k0b, wuk_ref[...]) + _dot_nt(dvb, wuv_ref[...])
        gwuk_ref[...] += _dot_tn(ckvn, d_k0b)
        gwuv_ref[...] += _dot_tn(ckvn, dvb)
        gqhn_ref[...] += g_qhn
        gkhn_ref[...] += g_khn
        dkr_ref[...] = d_kr.astype(BF16)
        cq = cq_ref[...]
        rcq = lax.rsqrt(jnp.mean(cq * cq, axis=-1, keepdims=True) + EPS)
        d_cq, gl = _rms_bwd(d_cqn, cq, rcq, qln_ref[...], Q_LORA_RANK)
        dcq_ref[...] = d_cq.astype(BF16)
        gqln_ref[...] += jnp.sum(gl, axis=0, keepdims=True)
        ckv = ckv_ref[...]
        rckv = lax.rsqrt(jnp.mean(ckv * ckv, axis=-1, keepdims=True) + EPS)
        d_ckv, gl = _rms_bwd(d_ckvn, ckv, rckv, kvln_ref[...], KV_LORA_RANK)
        dckv_ref[...] = d_ckv.astype(BF16)
        gkvln_ref[...] += jnp.sum(gl, axis=0, keepdims=True)

    return pl.pallas_call(
        body, name="mla_prep_bwd",
        grid=(s // tm,),
        in_specs=[_rows(tm, MLA_PAD_WIDTH), _rows(tm, MLA_PAD_WIDTH), _rows(tm, MLA_WIDTH),
                  _rows(tm, MLA_PAD_WIDTH), _rows(tm, MLA_PAD_WIDTH),
                  _rows(tm, Q_LORA_RANK), _rows(tm, KV_LORA_RANK), _rows(tm, Q_LORA_RANK), _rows(tm, KV_LORA_RANK),
                  _rows(tm, LANES), _rows(tm, LANES),
                  _full((1, Q_LORA_RANK)), _full((1, KV_LORA_RANK)), _full((1, LANES)), _full((1, LANES)),
                  _full((Q_LORA_RANK, MLA_PAD_WIDTH)), _full((KV_LORA_RANK, MLA_PAD_WIDTH)),
                  _full((KV_LORA_RANK, MLA_WIDTH))],
        out_specs=[_rows(tm, Q_LORA_RANK), _rows(tm, KV_LORA_RANK), _rows(tm, LANES),
                   _full((Q_LORA_RANK, MLA_PAD_WIDTH)), _full((KV_LORA_RANK, MLA_PAD_WIDTH)),
                   _full((KV_LORA_RANK, MLA_WIDTH)),
                   _full((1, Q_LORA_RANK)), _full((1, KV_LORA_RANK)), _full((1, LANES)), _full((1, LANES))],
        out_shape=[jax.ShapeDtypeStruct((s, Q_LORA_RANK), BF16), jax.ShapeDtypeStruct((s, KV_LORA_RANK), BF16),
                   jax.ShapeDtypeStruct((s, LANES), BF16),
                   jax.ShapeDtypeStruct((Q_LORA_RANK, MLA_PAD_WIDTH), F32),
                   jax.ShapeDtypeStruct((KV_LORA_RANK, MLA_PAD_WIDTH), F32),
                   jax.ShapeDtypeStruct((KV_LORA_RANK, MLA_WIDTH), F32),
                   jax.ShapeDtypeStruct((1, Q_LORA_RANK), F32), jax.ShapeDtypeStruct((1, KV_LORA_RANK), F32),
                   jax.ShapeDtypeStruct((1, LANES), F32), jax.ShapeDtypeStruct((1, LANES), F32)],
        scratch_shapes=[pltpu.VMEM((tm, MLA_PAD_WIDTH), BF16), pltpu.VMEM((tm, MLA_PAD_WIDTH), BF16),
                        pltpu.VMEM((tm, MLA_PAD_WIDTH), F32)],
        compiler_params=_params(("arbitrary",), VMEM_LIMIT),
    )(dq, dk, dv, q0, k0, cqn, ckvn, c_q, c_kv, cos_t, sin_t,
      q_lora_norm, kv_lora_norm, qhn_pad, khn_pad, w_uq_bf, w_uk_bf, w_uv_bf)


def _dh_call(pieces, hb, x, dy, ada, norm_w, w_in_bf):
    s = x.shape[0]
    tm = _row_tile(s, 256)
    widths = [p.shape[1] for p in pieces]
    offsets = [sum(widths[:j]) for j in range(len(widths))]
    assert offsets[-1] + widths[-1] == IN_COLS_PAD
    n = len(pieces)

    def body(*refs):
        p_refs = refs[:n]
        (hb_ref, x_ref, dy_ref, sh_ref, sc_ref, nw_ref, w_ref, gx_ref, gw_ref, dsh_ref, dsc_ref, gnw_ref,
         dp_ref) = refs[n:]

        @pl.when(pl.program_id(0) == 0)
        def _():
            gw_ref[...] = jnp.zeros_like(gw_ref)
            dsh_ref[...] = jnp.zeros_like(dsh_ref)
            dsc_ref[...] = jnp.zeros_like(dsc_ref)
            gnw_ref[...] = jnp.zeros_like(gnw_ref)

        for p_ref, c0, width in zip(p_refs, offsets, widths):
            dp_ref[:, c0:c0 + width] = p_ref[...].astype(BF16)
        gw_ref[...] += _dot_tn(hb_ref[...], dp_ref[...])
        dh = _dot_nt(dp_ref[...], w_ref[...])
        xx = x_ref[...]
        r0 = lax.rsqrt(jnp.mean(xx * xx, axis=-1, keepdims=True) + EPS)
        xn = xx * r0
        nw = nw_ref[...]
        dsh_ref[...] += jnp.sum(dh, axis=0, keepdims=True)
        dsc_ref[...] += jnp.sum(dh * (xn * nw), axis=0, keepdims=True)
        dn = dh * (1.0 + sc_ref[...])
        gnw_ref[...] += jnp.sum(dn * xn, axis=0, keepdims=True)
        dxn = dn * nw
        gx_ref[...] = dy_ref[...] + r0 * (dxn - xn * jnp.mean(dxn * xn, axis=-1, keepdims=True))

    return pl.pallas_call(
        body, name="in_proj_bwd",
        grid=(s // tm,),
        in_specs=[_rows(tm, w) for w in widths]
        + [_rows(tm, D_MODEL), _rows(tm, D_MODEL), _rows(tm, D_MODEL),
           _ada_part(0), _ada_part(1), _full((1, D_MODEL)),
           pl.BlockSpec((D_MODEL, IN_COLS_PAD), lambda i: (0, 0), pipeline_mode=pl.Buffered(1))],
        out_specs=[_rows(tm, D_MODEL),
                   pl.BlockSpec((D_MODEL, IN_COLS_PAD), lambda i: (0, 0), pipeline_mode=pl.Buffered(1)),
                   _full((1, D_MODEL)), _full((1, D_MODEL)), _full((1, D_MODEL))],
        out_shape=[jax.ShapeDtypeStruct((s, D_MODEL), F32), jax.ShapeDtypeStruct((D_MODEL, IN_COLS_PAD), F32),
                   jax.ShapeDtypeStruct((1, D_MODEL), F32), jax.ShapeDtypeStruct((1, D_MODEL), F32),
                   jax.ShapeDtypeStruct((1, D_MODEL), F32)],
        scratch_shapes=[pltpu.VMEM((tm, IN_COLS_PAD), BF16)],
        compiler_params=_params(("arbitrary",), VMEM_LIMIT),
    )(*pieces, hb, x, dy, ada, ada, norm_w, w_in_bf)


def _adamw(g, w, m, v):
    m = ADAM_B1 * m + (1.0 - ADAM_B1) * g
    v = ADAM_B2 * v + (1.0 - ADAM_B2) * (g * g)
    m_hat = m / (1.0 - ADAM_B1 ** ADAM_STEP)
    v_hat = v / (1.0 - ADAM_B2 ** ADAM_STEP)
    delta = -ADAM_LR * (m_hat / (jnp.sqrt(v_hat) + ADAM_EPS) + ADAM_WD * w)
    return delta, m, v


def _adam_shard_call(name, own, sib, w, m, v):
    r, c = w.shape
    tr = r if r <= 512 else 256

    def body(own_ref, sib_ref, w_ref, m_ref, v_ref, g_ref, d_ref, nm_ref, nv_ref):
        a = ((own_ref[0].astype(F32) + own_ref[1].astype(F32)) + own_ref[2].astype(F32)) + own_ref[3].astype(F32)
        b = ((sib_ref[0].astype(F32) + sib_ref[1].astype(F32)) + sib_ref[2].astype(F32)) + sib_ref[3].astype(F32)
        g = a + b
        g_ref[...] = g
        d_ref[...], nm_ref[...], nv_ref[...] = _adamw(g, w_ref[...], m_ref[...], v_ref[...])

    part = pl.BlockSpec((4, tr, c), lambda i: (0, i, 0))
    blk = pl.BlockSpec((tr, c), lambda i: (i, 0))
    return pl.pallas_call(
        body, name=name,
        grid=(r // tr,),
        in_specs=[part, part, blk, blk, blk],
        out_specs=[blk] * 4,
        out_shape=[jax.ShapeDtypeStruct((r, c), F32)] * 4,
        compiler_params=_params(("arbitrary",), VMEM_LIMIT),
    )(own, sib, w, m, v)


def _adam_ada_call(c_all, d_all, w, m, v):
    r, c = w.shape
    tr = 256

    def body(c_ref, d_ref, w_ref, m_ref, v_ref, g_ref, dl_ref, nm_ref, nv_ref):
        cc = c_ref[...]
        sc = cc * _sigmoid(cc)
        dd = d_ref[...]
        sc_hi = sc.astype(BF16)
        sc_lo = (sc - sc_hi.astype(F32)).astype(BF16)
        dd_hi = dd.astype(BF16)
        dd_lo = (dd - dd_hi.astype(F32)).astype(BF16)
        g = _dot_tn(sc_hi, dd_hi) + (_dot_tn(sc_hi, dd_lo) + _dot_tn(sc_lo, dd_hi))
        g_ref[...] = g
        dl_ref[...], nm_ref[...], nv_ref[...] = _adamw(g, w_ref[...], m_ref[...], v_ref[...])

    blk = pl.BlockSpec((tr, c), lambda i: (i, 0))
    return pl.pallas_call(
        body, name="adam_w_ada",
        grid=(r // tr,),
        in_specs=[pl.BlockSpec((16, tr), lambda i: (0, i)), pl.BlockSpec((16, c), lambda i: (0, 0)), blk, blk, blk],
        out_specs=[blk] * 4,
        out_shape=[jax.ShapeDtypeStruct((r, c), F32)] * 4,
        compiler_params=_params(("arbitrary",), VMEM_LIMIT),
    )(c_all, d_all, w, m, v)


def _adam_vectors_call(packs, offsets, vectors):
    nv = len(vectors)

    def body(*refs):
        p_ref, ins, outs = refs[0], refs[1:1 + 3 * nv], refs[1 + 3 * nv:]
        for j, off in enumerate(offsets):
            n = ins[3 * j].shape[1]
            span = -(-n // LANES) * LANES
            g = p_ref[0, :, off:off + span]
            for b in range(1, 8):
                g = g + p_ref[b, :, off:off + span]
            g = g[:, :n]
            outs[j][...] = g
            outs[nv + j][...], outs[2 * nv + j][...], outs[3 * nv + j][...] = _adamw(
                g, ins[3 * j][...], ins[3 * j + 1][...], ins[3 * j + 2][...])

    flat = [a for t in vectors for a in t]
    res = pl.pallas_call(
        body, name="adam_vectors",
        out_shape=[jax.ShapeDtypeStruct(t[0].shape, F32) for _ in range(4) for t in vectors],
    )(packs, *flat)
    return [res[k * nv:(k + 1) * nv] for k in range(4)]


ROPE_HALF = MLA_ROPE_DIM // 2
NOPE_A = MLA_NOPE_DIM - ROPE_HALF


def _zeros_like_lanes(t, n):
    return jnp.zeros(t.shape[:-1] + (n,), t.dtype)


def _to_head_lanes(t):
    nope, rope = t[..., :MLA_NOPE_DIM], t[..., MLA_NOPE_DIM:]
    return jnp.concatenate([rope[..., :ROPE_HALF], nope[..., :NOPE_A], rope[..., ROPE_HALF:], nope[..., NOPE_A:],
                            _zeros_like_lanes(t, HEAD_PAD - MLA_QK_DIM)], axis=-1)


def _from_head_lanes(t):
    return jnp.concatenate([t[..., ROPE_HALF:HALF_LANES], t[..., HALF_LANES + ROPE_HALF:MLA_QK_DIM],
                            t[..., :ROPE_HALF], t[..., HALF_LANES:HALF_LANES + ROPE_HALF]], axis=-1)


def _nope_to_head_lanes(t):
    return jnp.concatenate([_zeros_like_lanes(t, ROPE_HALF), t[..., :NOPE_A], _zeros_like_lanes(t, ROPE_HALF),
                            t[..., NOPE_A:], _zeros_like_lanes(t, HEAD_PAD - MLA_QK_DIM)], axis=-1)


def _rope_to_head_lanes(t):
    return jnp.concatenate([t[..., :ROPE_HALF], _zeros_like_lanes(t, HALF_LANES - ROPE_HALF), t[..., ROPE_HALF:],
                            _zeros_like_lanes(t, HALF_LANES - ROPE_HALF)], axis=-1)


def _rope_tables(positions):
    inv_freq = (ROPE_THETA ** (-jnp.arange(0, MLA_ROPE_DIM, 2, dtype=F32) / MLA_ROPE_DIM))[None]
    signed = _rope_to_head_lanes(jnp.concatenate([-inv_freq, inv_freq], axis=1))
    ang = positions.astype(F32)[:, None] * signed
    return jnp.cos(ang), jnp.sin(ang)


def _unshard_cols(g):
    return jnp.transpose(g, (1, 0, 2)).reshape(g.shape[1], 4 * g.shape[2])


def _shard_cols(g):
    r, c4 = g.shape
    return jnp.transpose(g.reshape(r, 4, c4 // 4), (1, 0, 2))


def kernel(x, c, positions, w_ada, b_ada, norm_w, w_in, q_lora_norm, w_uq, kv_lora_norm, w_ukv, q_head_norm, k_head_norm, w_out, loss_target, m_w_ada, m_b_ada, m_norm_w, m_w_in, m_q_lora_norm, m_w_uq, m_kv_lora_norm, m_w_ukv, m_q_head_norm, m_k_head_norm, m_w_out, v_w_ada, v_b_ada, v_norm_w, v_w_in, v_q_lora_norm, v_w_uq, v_kv_lora_norm, v_w_ukv, v_q_head_norm, v_k_head_norm, v_w_out):
    chip = 2 * lax.axis_index("x") + lax.axis_index("y")
    me8 = 2 * chip + lax.axis_index("c")
    ada_cols = w_ada.shape[2]
    c_all = _allgather_rows_call(c)[:, 0, :]
    ada_part = _ada_call(c_all, w_ada[0], lax.dynamic_slice_in_dim(b_ada, chip * ada_cols, ada_cols, axis=1))
    ada_g, win_g, wuq_g, wukv_g, wout_g = _gather_call(
        [ada_part[None]] + [w.astype(BF16) for w in (w_in, w_uq, w_ukv, w_out)], [False, True, True, True, True])
    ada = lax.dynamic_slice_in_dim(ada_g, me8, 1, axis=1).reshape(1, 4 * ada_cols)
    (sq_sum, grad_x, g_w_in, g_w_uq, g_w_ukv, g_w_out, d_ada, g_norm_w, g_qln, g_kvln, g_qhn, g_khn) = _local_step(
        x[0], ada, positions[0], loss_target[0], norm_w, win_g,
        q_lora_norm, _unshard_cols(wuq_g), kv_lora_norm, _unshard_cols(wukv_g), q_head_norm, k_head_norm,
        wout_g.reshape(D_MODEL, D_MODEL))

    grads = [g.astype(BF16) for g in (g_w_in, _shard_cols(g_w_uq), _shard_cols(g_w_ukv),
                                      g_w_out.reshape(4, D_MODEL // 4, D_MODEL))]
    pieces = [d_ada, g_norm_w, g_qln, g_kvln, g_qhn, g_khn, (0.5 * sq_sum / D_MODEL).reshape(1, 1)]
    spans = [-(-p.shape[1] // LANES) * LANES for p in pieces]
    starts = [sum(spans[:j]) for j in range(len(spans))]
    small = jnp.concatenate([jnp.pad(p, ((0, 0), (0, sp - p.shape[1]))) for p, sp in zip(pieces, spans)], axis=1)
    own, sib, packs = _exchange_call(grads, small)
    loss = jnp.sum(packs[:, 0, starts[-1]])

    names = ["adam_w_in", "adam_w_uq", "adam_w_ukv", "adam_w_out"]
    shard_w = [(w_in, m_w_in, v_w_in), (w_uq, m_w_uq, v_w_uq), (w_ukv, m_w_ukv, v_w_ukv),
               (w_out, m_w_out, v_w_out)]
    res = {}
    for name, o_g, s_g, (w, m, v) in zip(names, own, sib, shard_w):
        res[name] = _adam_shard_call(name, o_g, s_g, w[0], m[0], v[0])
    d_all = lax.dynamic_slice_in_dim(packs[:, 0, :], starts[0] + chip * ada_cols, ada_cols, axis=1)
    res_ada = _adam_ada_call(jnp.pad(c_all, ((0, 8), (0, 0))), jnp.pad(d_all, ((0, 8), (0, 0))),
                             w_ada[0], m_w_ada[0], v_w_ada[0])
    vectors = [(b_ada, m_b_ada, v_b_ada), (norm_w, m_norm_w, v_norm_w), (q_lora_norm, m_q_lora_norm, v_q_lora_norm),
               (kv_lora_norm, m_kv_lora_norm, v_kv_lora_norm), (q_head_norm, m_q_head_norm, v_q_head_norm),
               (k_head_norm, m_k_head_norm, v_k_head_norm)]
    vec_out = _adam_vectors_call(packs, starts[:len(vectors)], vectors)

    def ordered(kind):
        big = lambda name: res[name][kind][None]
        return [res_ada[kind][None], vec_out[kind][0], vec_out[kind][1], big("adam_w_in"), vec_out[kind][2],
                big("adam_w_uq"), vec_out[kind][3], big("adam_w_ukv"), vec_out[kind][4], vec_out[kind][5],
                big("adam_w_out")]

    return (loss, grad_x[None], *ordered(0), *ordered(1), *ordered(2), *ordered(3))


def _local_step(x2, ada, positions, tgt, norm_w, w_in_shards, q_lora_norm, w_uq_full,
                kv_lora_norm, w_ukv_full, q_head_norm, k_head_norm, w_out_full):
    in_shard = w_in_shards.shape[2]
    ckv_tail = C_GM - 3 * in_shard
    assert 0 <= ckv_tail and ckv_tail + MLA_ROPE_DIM + MLA_WIDTH == in_shard
    last = w_in_shards[3]
    w_in_bf = jnp.concatenate(
        [w_in_shards[0], w_in_shards[1], w_in_shards[2], last[:, :ckv_tail], last[:, ckv_tail + MLA_ROPE_DIM:],
         _rope_to_head_lanes(last[:, ckv_tail:ckv_tail + MLA_ROPE_DIM])], axis=1).astype(BF16)
    w_uq_bf = _to_head_lanes(w_uq_full.reshape(Q_LORA_RANK, MLA_HEADS, MLA_QK_DIM)).reshape(
        Q_LORA_RANK, MLA_PAD_WIDTH).astype(BF16)
    w_ukv_heads = w_ukv_full.reshape(KV_LORA_RANK, MLA_HEADS, 2 * MLA_NOPE_DIM)
    w_uk_bf = _nope_to_head_lanes(w_ukv_heads[:, :, :MLA_NOPE_DIM]).reshape(KV_LORA_RANK, MLA_PAD_WIDTH).astype(BF16)
    w_uv_bf = w_ukv_heads[:, :, MLA_NOPE_DIM:].reshape(KV_LORA_RANK, MLA_WIDTH).astype(BF16)
    w_out_bf = w_out_full.astype(BF16)
    qhn_pad, khn_pad = _to_head_lanes(q_head_norm), _to_head_lanes(k_head_norm)
    cos_t, sin_t = _rope_tables(positions)

    hb, q_sb, k_sb, v_sb, g_sb, c_q, c_kv, g_mla, k_rope = _pre_call(x2, ada, norm_w, w_in_bf)
    q_m, k_m, v_m, cqn, ckvn, q0, k0 = _mla_prep_call(
        c_q, c_kv, k_rope, cos_t, sin_t, q_lora_norm, kv_lora_norm, qhn_pad, khn_pad,
        w_uq_bf, w_uk_bf, w_uv_bf)
    o_sb, r_sb, kstart = _sb_fwd_call(q_sb, k_sb, v_sb)
    o_mla, lse = _mla_fwd_call(q_m, k_m, v_m)
    do_sb, do_mla, dg_sb, dg_mla, dy, g_w_out, d_gate, sq = _out_call(
        o_sb, g_sb, o_mla, g_mla, x2, tgt, ada, w_out_bf)

    dq_sb, dk_sb, dv_sb = _sb_bwd_call(kstart, q_sb, k_sb, v_sb, do_sb, r_sb)
    dq_m, dk_m, dv_m = _mla_bwd_call(q_m, k_m, v_m, do_mla, o_mla, lse)
    (d_cq, d_ckv, d_kr, g_wuq_pad, g_wuk_pad, g_wuv, g_qln, g_kvln, g_qhn, g_khn) = _mla_prep_bwd_call(
        dq_m, dk_m, dv_m, q0, k0, cqn, ckvn, c_q, c_kv, cos_t, sin_t,
        q_lora_norm, kv_lora_norm, qhn_pad, khn_pad, w_uq_bf, w_uk_bf, w_uv_bf)
    grad_x, g_win_pad, d_shift, d_scale, g_norm_w = _dh_call(
        [dq_sb, dk_sb, dv_sb, dg_sb, d_cq, d_ckv, dg_mla, d_kr], hb, x2, dy, ada, norm_w, w_in_bf)

    g_kr = g_win_pad[:, C_KR:]
    g_last = jnp.concatenate([g_win_pad[:, 3 * in_shard:C_GM], g_kr[:, :ROPE_HALF],
                              g_kr[:, HALF_LANES:HALF_LANES + ROPE_HALF], g_win_pad[:, C_GM:C_KR]], axis=1)
    g_w_in = jnp.stack([g_win_pad[:, j * in_shard:(j + 1) * in_shard] for j in range(3)] + [g_last])
    g_w_uq = _from_head_lanes(g_wuq_pad.reshape(Q_LORA_RANK, MLA_HEADS, HEAD_PAD)).reshape(Q_LORA_RANK, -1)
    g_w_ukv = jnp.concatenate(
        [_from_head_lanes(g_wuk_pad.reshape(KV_LORA_RANK, MLA_HEADS, HEAD_PAD))[:, :, :MLA_NOPE_DIM],
         g_wuv.reshape(KV_LORA_RANK, MLA_HEADS, MLA_NOPE_DIM)], axis=2).reshape(KV_LORA_RANK, -1)
    d_ada = jnp.concatenate([d_shift, d_scale, d_gate], axis=1)
    return (jnp.sum(sq), grad_x, g_w_in, g_w_uq, g_w_ukv, g_w_out, d_ada, g_norm_w, g_qln, g_kvln,
            _from_head_lanes(g_qhn), _from_head_lanes(g_khn))
```

```python
import math

import jax
import jax.numpy as jnp
from jax import lax
from jax.experimental import pallas as pl
from jax.experimental.pallas import tpu as pltpu

F32 = jnp.float32
BF16 = jnp.bfloat16
I32 = jnp.int32

D_MODEL = 1024
SB_HEADS = 8
SB_WIDTH = 512
MLA_HEADS = 8
MLA_QK_DIM = 96
MLA_NOPE_DIM = 64
MLA_ROPE_DIM = 32
MLA_WIDTH = 512
Q_LORA_RANK = 384
KV_LORA_RANK = 256
ROPE_THETA = 10000.0
EPS = 1e-6
LANES = 128
HALF_LANES = LANES // 2
HEAD_PAD = 128
MLA_PAD_WIDTH = MLA_HEADS * HEAD_PAD

C_Q, C_K, C_V, C_G = 0, 512, 1024, 1536
C_CQ, C_CKV, C_GM, C_KR = 2048, 2432, 2688, 3200
IN_COLS_PAD = 3328

ADAM_LR = 0.001
ADAM_B1 = 0.9
ADAM_B2 = 0.999
ADAM_EPS = 1e-08
ADAM_WD = 0.01
ADAM_STEP = 10

SB_SCALE = 0.125
SB_GROUP = 4
SB_TOP_ROWS = 192
MLA_SCALE = 1.0 / math.sqrt(MLA_QK_DIM)
LN2 = math.log(2.0)
MLA_SCALE_LOG2 = MLA_SCALE / LN2
MLA_BQ = 1024
MLA_BWD_BQ = 1024
MLA_BK = 1024
MLA_BWD_BK = 1024
MLA_STEP = 512
MLA_BWD_STEP = 256
SB_DEAD = -104.0
MASK_NEG = -1e30

VMEM_LIMIT = 56 * 1024 * 1024
MESH = pl.DeviceIdType.MESH


def _dot(a, b):
    return jnp.dot(a, b, preferred_element_type=F32)


def _dot_nt(a, b):
    return lax.dot_general(a, b, (((1,), (1,)), ((), ())), preferred_element_type=F32)


def _dot_tn(a, b):
    return lax.dot_general(a, b, (((0,), (0,)), ((), ())), preferred_element_type=F32)


def _sigmoid(x):
    return 1.0 / (1.0 + jnp.exp(-x))


def _split_dot(a, m):
    hi = a.astype(BF16)
    lo = (a - hi.astype(F32)).astype(BF16)
    return _dot(hi, m) + _dot(lo, m)


def _params(sem, vmem=None):
    return pltpu.CompilerParams(dimension_semantics=sem, vmem_limit_bytes=vmem)


def _row_tile(s, want):
    return min(want, s)


def _hbm_spec():
    return pl.BlockSpec(memory_space=pltpu.HBM)


def _allgather_rows_call(row):
    def body(in_ref, out_ref, send_sems, recv_sems, loc_sem):
        x, y, c = lax.axis_index("x"), lax.axis_index("y"), lax.axis_index("c")
        flips = [(fx, fy, fc) for fx in (0, 1) for fy in (0, 1) for fc in (0, 1)][1:]

        def peer(r):
            fx, fy, fc = flips[r]
            return ((1 - x) if fx else x, (1 - y) if fy else y, (1 - c) if fc else c)

        def copy(r, slot):
            return pltpu.make_async_remote_copy(
                src_ref=in_ref, dst_ref=out_ref.at[slot], send_sem=send_sems.at[r], recv_sem=recv_sems.at[r],
                device_id=peer(r), device_id_type=MESH)

        local = pltpu.make_async_copy(in_ref, out_ref.at[4 * x + 2 * y + c], loc_sem)
        local.start()
        sends = [copy(r, 4 * x + 2 * y + c) for r in range(7)]
        for cp in sends:
            cp.start()
        for r in range(7):
            px, py, pc = peer(r)
            copy(r, 4 * px + 2 * py + pc).wait_recv()
        for cp in sends:
            cp.wait_send()
        local.wait()

    return pl.pallas_call(
        body, name="gather_rows",
        out_shape=jax.ShapeDtypeStruct((8,) + row.shape, row.dtype),
        in_specs=[_hbm_spec()], out_specs=_hbm_spec(),
        scratch_shapes=[pltpu.SemaphoreType.DMA((7,)), pltpu.SemaphoreType.DMA((7,)), pltpu.SemaphoreType.DMA],
    )(row)


def _gather_call(shards, split):
    n = len(shards)
    halves = [s.shape[1] // 2 for s in shards]

    def body(*refs):
        ins, outs = refs[:n], refs[n:2 * n]
        ici_send, ici_recv, d2d_send, d2d_recv, loc_sems = refs[2 * n:]
        x, y, c = lax.axis_index("x"), lax.axis_index("y"), lax.axis_index("c")
        me = 2 * x + y
        peers = [(1 - x, y), (x, 1 - y), (1 - x, 1 - y)]

        def rows(a, which):
            return pl.ds(pl.multiple_of(which * halves[a], 16), halves[a])

        def ici(a, j, slot):
            px, py = peers[j]
            src, dst = ins[a].at[0], outs[a].at[slot]
            if split[a]:
                src, dst = src.at[rows(a, c)], dst.at[rows(a, c)]
            return pltpu.make_async_remote_copy(
                src_ref=src, dst_ref=dst,
                send_sem=ici_send.at[3 * a + j], recv_sem=ici_recv.at[3 * a + j],
                device_id=(px, py, c), device_id_type=MESH)

        def d2d(a, j, which):
            px, py = peers[j]
            piece = outs[a].at[2 * px + py, rows(a, which)]
            return pltpu.make_async_remote_copy(
                src_ref=piece, dst_ref=piece,
                send_sem=d2d_send.at[3 * a + j], recv_sem=d2d_recv.at[3 * a + j],
                device_id=(x, y, 1 - c), device_id_type=MESH)

        local = [pltpu.make_async_copy(ins[a].at[0], outs[a].at[me], loc_sems.at[a]) for a in range(n)]
        for cp in local:
            cp.start()
        sends = [ici(a, j, me) for a in range(n) for j in range(3)]
        for cp in sends:
            cp.start()
        for a in range(n):
            for j in range(3):
                px, py = peers[j]
                ici(a, j, 2 * px + py).wait_recv()
                if split[a]:
                    cp = d2d(a, j, c)
                    cp.start()
                    sends.append(cp)
        for a in range(n):
            for j in range(3):
                if split[a]:
                    d2d(a, j, 1 - c).wait_recv()
        for cp in sends:
            cp.wait_send()
        for cp in local:
            cp.wait()

    return pl.pallas_call(
        body, name="gather_weights",
        out_shape=[jax.ShapeDtypeStruct((4,) + s.shape[1:], s.dtype) for s in shards],
        in_specs=[_hbm_spec() for _ in shards],
        out_specs=[_hbm_spec() for _ in shards],
        scratch_shapes=[pltpu.SemaphoreType.DMA((3 * n,)), pltpu.SemaphoreType.DMA((3 * n,)),
                        pltpu.SemaphoreType.DMA((3 * n,)), pltpu.SemaphoreType.DMA((3 * n,)),
                        pltpu.SemaphoreType.DMA((n,))],
    )(*shards)


def _exchange_call(grads, small):
    n = len(grads)

    def body(*refs):
        g_in, small_in = refs[:n], refs[n]
        own, sib, packs = refs[n + 1:2 * n + 1], refs[2 * n + 1:3 * n + 1], refs[3 * n + 1]
        ici_send, ici_recv, d2d_send, d2d_recv, sm_send, sm_recv, loc_sems = refs[3 * n + 2:]
        x, y, c = lax.axis_index("x"), lax.axis_index("y"), lax.axis_index("c")
        me = 2 * x + y
        me8 = 4 * x + 2 * y + c
        sibling = (x, y, 1 - c)
        peers = [(1 - x, y), (x, 1 - y), (1 - x, 1 - y)]
        flips = [(fx, fy, fc) for fx in (0, 1) for fy in (0, 1) for fc in (0, 1)][1:]

        def ici(a, j, src_slot, dst_slot):
            px, py = peers[j]
            return pltpu.make_async_remote_copy(
                src_ref=g_in[a].at[src_slot], dst_ref=own[a].at[dst_slot],
                send_sem=ici_send.at[3 * a + j], recv_sem=ici_recv.at[3 * a + j],
                device_id=(px, py, c), device_id_type=MESH)

        def d2d(a, rel, chip, src):
            return pltpu.make_async_remote_copy(
                src_ref=src, dst_ref=sib[a].at[chip],
                send_sem=d2d_send.at[4 * a + rel], recv_sem=d2d_recv.at[4 * a + rel],
                device_id=sibling, device_id_type=MESH)

        def flipped(r):
            fx, fy, fc = flips[r]
            return ((1 - x) if fx else x, (1 - y) if fy else y, (1 - c) if fc else c)

        def sm(r, slot):
            return pltpu.make_async_remote_copy(
                src_ref=small_in, dst_ref=packs.at[slot],
                send_sem=sm_send.at[r], recv_sem=sm_recv.at[r],
                device_id=flipped(r), device_id_type=MESH)

        def peer8(r):
            px, py, pc = flipped(r)
            return 4 * px + 2 * py + pc

        local = [pltpu.make_async_copy(g_in[a].at[me], own[a].at[me], loc_sems.at[a]) for a in range(n)]
        local.append(pltpu.make_async_copy(small_in, packs.at[me8], loc_sems.at[n]))
        for cp in local:
            cp.start()
        sends = []
        for r in range(7):
            sends.append(sm(r, me8))
        for a in range(n):
            for j in range(3):
                px, py = peers[j]
                sends.append(ici(a, j, 2 * px + py, me))
        for cp in sends:
            cp.start()
        for a in range(n):
            cp = d2d(a, 0, me, g_in[a].at[me])
            cp.start()
            sends.append(cp)
        for a in range(n):
            for j in range(3):
                px, py = peers[j]
                ici(a, j, me, 2 * px + py).wait_recv()
                cp = d2d(a, 1 + j, 2 * px + py, own[a].at[2 * px + py])
                cp.start()
                sends.append(cp)
        for a in range(n):
            d2d(a, 0, me, g_in[a].at[me]).wait_recv()
            for j in range(3):
                px, py = peers[j]
                d2d(a, 1 + j, 2 * px + py, g_in[a].at[me]).wait_recv()
        for r in range(7):
            sm(r, peer8(r)).wait_recv()
        for cp in sends:
            cp.wait_send()
        for cp in local:
            cp.wait()

    out_shape = ([jax.ShapeDtypeStruct(g.shape, g.dtype) for g in grads] * 2
                 + [jax.ShapeDtypeStruct((8,) + small.shape, small.dtype)])
    res = pl.pallas_call(
        body, name="exchange_grads",
        out_shape=out_shape,
        in_specs=[_hbm_spec() for _ in range(n + 1)],
        out_specs=[_hbm_spec() for _ in range(2 * n + 1)],
        scratch_shapes=[pltpu.SemaphoreType.DMA((3 * n,)), pltpu.SemaphoreType.DMA((3 * n,)),
                        pltpu.SemaphoreType.DMA((4 * n,)), pltpu.SemaphoreType.DMA((4 * n,)),
                        pltpu.SemaphoreType.DMA((7,)), pltpu.SemaphoreType.DMA((7,)),
                        pltpu.SemaphoreType.DMA((n + 1,))],
    )(*grads, small)
    return res[:n], res[n:2 * n], res[2 * n]


def _ada_call(c_all, w_ada_cols, b_ada_cols):
    def body(c_ref, w_ref, b_ref, o_ref):
        cc = c_ref[...]
        o_ref[...] = _dot((cc * _sigmoid(cc)).astype(BF16), w_ref[...].astype(BF16)) + b_ref[...]

    return pl.pallas_call(
        body, name="ada_fwd",
        out_shape=jax.ShapeDtypeStruct((c_all.shape[0], w_ada_cols.shape[1]), F32),
        compiler_params=pltpu.CompilerParams(vmem_limit_bytes=VMEM_LIMIT),
    )(c_all, w_ada_cols, b_ada_cols)


def _ada_part(j):
    return pl.BlockSpec((1, D_MODEL), lambda i: (0, j))


def _full(shape):
    return pl.BlockSpec(shape, lambda i: (0,) * len(shape))


def _rows(tm, width):
    return pl.BlockSpec((tm, width), lambda i: (i, 0))


def _pre_call(x, ada, norm_w, w_in_bf):
    s = x.shape[0]
    tm = _row_tile(s, 512)
    groups = [(C_Q, 512, BF16), (C_K, 512, BF16), (C_V, 512, BF16), (C_G, 512, F32),
              (C_CQ, Q_LORA_RANK, F32), (C_CKV, KV_LORA_RANK, F32), (C_GM, 512, F32), (C_KR, LANES, F32)]

    def body(x_ref, sh_ref, sc_ref, nw_ref, w_ref, hb_ref, *outs):
        xx = x_ref[...]
        r0 = lax.rsqrt(jnp.mean(xx * xx, axis=-1, keepdims=True) + EPS)
        h = (xx * r0 * nw_ref[...]) * (1.0 + sc_ref[...]) + sh_ref[...]
        hb = h.astype(BF16)
        hb_ref[...] = hb
        for (c0, width, dt), o_ref in zip(groups, outs):
            o_ref[...] = _dot(hb, w_ref[:, c0:c0 + width]).astype(dt)

    return pl.pallas_call(
        body, name="pre_proj",
        grid=(s // tm,),
        in_specs=[_rows(tm, D_MODEL), _ada_part(0), _ada_part(1), _full((1, D_MODEL)),
                  _full((D_MODEL, IN_COLS_PAD))],
        out_specs=[_rows(tm, D_MODEL)] + [_rows(tm, w) for _, w, _ in groups],
        out_shape=[jax.ShapeDtypeStruct((s, D_MODEL), BF16)]
        + [jax.ShapeDtypeStruct((s, w), dt) for _, w, dt in groups],
        compiler_params=_params(("arbitrary",), VMEM_LIMIT),
    )(x, ada, ada, norm_w, w_in_bf)


def _rope(t, cos_t, sin_t):
    return t * cos_t + pltpu.roll(t, HALF_LANES, 1) * sin_t


def _rope_adjoint(d, cos_t, sin_t):
    return d * cos_t + pltpu.roll(d * sin_t, HALF_LANES, 1)


def _mla_prep_call(c_q, c_kv, k_rope, cos_t, sin_t, q_lora_norm, kv_lora_norm, qhn_pad, khn_pad,
                   w_uq_bf, w_uk_bf, w_uv_bf):
    s = c_q.shape[0]
    tm = _row_tile(s, 256)

    def body(cq_ref, ckv_ref, kr_ref, cos_ref, sin_ref, qln_ref, kvln_ref, qhn_ref, khn_ref,
             wuq_ref, wuk_ref, wuv_ref, q_ref, k_ref, v_ref, cqn_ref, ckvn_ref, q0_ref, k0_ref):
        cq = cq_ref[...]
        cqn = (cq * lax.rsqrt(jnp.mean(cq * cq, axis=-1, keepdims=True) + EPS) * qln_ref[...]).astype(BF16)
        cqn_ref[...] = cqn
        ckv = ckv_ref[...]
        ckvn = (ckv * lax.rsqrt(jnp.mean(ckv * ckv, axis=-1, keepdims=True) + EPS) * kvln_ref[...]).astype(BF16)
        ckvn_ref[...] = ckvn
        v_ref[...] = _dot(ckvn, wuv_ref[...]).astype(BF16)
        q0_ref[...] = _dot(cqn, wuq_ref[...])
        k0_ref[...] = _dot(ckvn, wuk_ref[...])
        cos_t, sin_t = cos_ref[...], sin_ref[...]
        kr = kr_ref[...]
        heads = [slice(h * HEAD_PAD, (h + 1) * HEAD_PAD) for h in range(MLA_HEADS)]
        for cols in heads:
            k0_ref[:, cols] = k0_ref[:, cols] + kr

        def inv_rms(ref):
            sums = [jnp.sum(ref[:, cols] * ref[:, cols], axis=-1, keepdims=True) for cols in heads]
            return [lax.rsqrt(t * (1.0 / MLA_QK_DIM) + EPS) for t in sums]

        rqs, rks = inv_rms(q0_ref), inv_rms(k0_ref)
        for cols, rq, rk in zip(heads, rqs, rks):
            q_ref[:, cols] = (_rope(q0_ref[:, cols] * rq * qhn_ref[...], cos_t, sin_t) * MLA_SCALE_LOG2).astype(BF16)
            k_ref[:, cols] = _rope(k0_ref[:, cols] * rk * khn_ref[...], cos_t, sin_t).astype(BF16)

    return pl.pallas_call(
        body, name="mla_prep",
        grid=(s // tm,),
        in_specs=[_rows(tm, Q_LORA_RANK), _rows(tm, KV_LORA_RANK), _rows(tm, LANES),
                  _rows(tm, LANES), _rows(tm, LANES),
                  _full((1, Q_LORA_RANK)), _full((1, KV_LORA_RANK)), _full((1, LANES)), _full((1, LANES)),
                  _full((Q_LORA_RANK, MLA_PAD_WIDTH)), _full((KV_LORA_RANK, MLA_PAD_WIDTH)),
                  _full((KV_LORA_RANK, MLA_WIDTH))],
        out_specs=[_rows(tm, MLA_PAD_WIDTH), _rows(tm, MLA_PAD_WIDTH), _rows(tm, MLA_WIDTH),
                   _rows(tm, Q_LORA_RANK), _rows(tm, KV_LORA_RANK),
                   _rows(tm, MLA_PAD_WIDTH), _rows(tm, MLA_PAD_WIDTH)],
        out_shape=[jax.ShapeDtypeStruct((s, MLA_PAD_WIDTH), BF16), jax.ShapeDtypeStruct((s, MLA_PAD_WIDTH), BF16),
                   jax.ShapeDtypeStruct((s, MLA_WIDTH), BF16),
                   jax.ShapeDtypeStruct((s, Q_LORA_RANK), BF16), jax.ShapeDtypeStruct((s, KV_LORA_RANK), BF16),
                   jax.ShapeDtypeStruct((s, MLA_PAD_WIDTH), F32), jax.ShapeDtypeStruct((s, MLA_PAD_WIDTH), F32)],
        compiler_params=_params(("arbitrary",), VMEM_LIMIT),
    )(c_q, c_kv, k_rope, cos_t, sin_t, q_lora_norm, kv_lora_norm, qhn_pad, khn_pad,
      w_uq_bf, w_uk_bf, w_uv_bf)


def _log_sigmoid_pair(z):
    ls = jnp.minimum(z, 0.0) - jnp.log(1.0 + jnp.exp(-jnp.abs(z)))
    return ls, ls - z


def _pair(hh):
    return slice((hh // 2) * LANES, (hh // 2 + 1) * LANES)


def _set_rows(full, rows, part):
    pieces = ([full[:rows.start]] if rows.start else []) + [part]
    if rows.stop < full.shape[0]:
        pieces.append(full[rows.stop:])
    return pieces[0] if len(pieces) == 1 else jnp.concatenate(pieces, axis=0)


def _sb_fwd_call(q, k, v):
    s = q.shape[0]
    bq = _row_tile(s, 256)
    nq = s // bq
    nh = SB_GROUP
    width = nh * HALF_LANES
    top_rows = min(bq, SB_TOP_ROWS)

    def body(q_ref, k_ref, v_ref, o_ref, r_ref, ks_ref):
        hp, i = pl.program_id(0), pl.program_id(1)
        lane = lax.broadcasted_iota(I32, (bq, LANES), 1)
        row = lax.broadcasted_iota(I32, (bq, bq), 0)
        col = lax.broadcasted_iota(I32, (bq, bq), 1)
        strict = col < row
        later = jnp.where(row > col, 1.0, 0.0).astype(BF16)
        masks = [_head_mask(lane, hh).astype(BF16) for hh in range(2)]
        qms = [q_ref[:, _pair(hh)] * jnp.asarray(SB_SCALE, BF16) * masks[hh % 2] for hh in range(nh)]

        def walk(blocks, state):
            chains = [(kb, diagonal, slice(r0, r1), hh) for kb, diagonal, r0, r1 in blocks for hh in range(nh)]
            keys = lambda kb: pl.ds(pl.multiple_of(kb * bq, bq), bq)
            zs = [_dot_nt(qms[hh][rs], k_ref[keys(kb), _pair(hh)]) for kb, _, rs, hh in chains]
            pairs = []
            for z, (_, diagonal, rs, _) in zip(zs, chains):
                ls, lk = _log_sigmoid_pair(z)
                pairs.append((ls, jnp.where(strict[rs], lk, 0.0) if diagonal else lk))
            sums = [_split_dot(lk, later) for _, lk in pairs]
            runs = [st[0] for st in state]
            ws = []
            for (ls, lk), after, (_, diagonal, rs, hh) in zip(pairs, sums, chains):
                w = jnp.exp(ls + (after + runs[hh][rs]))
                ws.append((jnp.where(strict[rs], w, 0.0) if diagonal else w).astype(BF16))
                runs[hh] = _set_rows(runs[hh], rs, runs[hh][rs] + jnp.sum(lk, axis=1, keepdims=True))
            accs = [st[1] for st in state]
            for w, (kb, _, rs, hh) in zip(ws, chains):
                accs[hh] = _set_rows(accs[hh], rs, accs[hh][rs] + _dot(w, v_ref[keys(kb), _pair(hh)]))
            return tuple(zip(runs, accs))

        def alive(state, r0=0):
            top = jnp.max(state[0][0][r0:])
            for st in state[1:]:
                top = jnp.maximum(top, jnp.max(st[0][r0:]))
            return (top > SB_DEAD).astype(I32)

        def finish(state, first, low_rows_took_left):
            ks_ref[hp, i] = 2 * first + low_rows_took_left
            for pair in range(nh // 2):
                o_ref[:, _pair(2 * pair)] = jnp.where(lane < HALF_LANES, state[2 * pair][1], state[2 * pair + 1][1])
                r_ref[:, _pair(2 * pair)] = jnp.where(lane < HALF_LANES, state[2 * pair][0], state[2 * pair + 1][0])

        zero = ((jnp.zeros((bq, 1), F32), jnp.zeros((bq, LANES), F32)),) * nh

        @pl.when(i == 0)
        def _():
            finish(walk([(0, True, 0, bq)], zero), 0, 0)

        @pl.when(i > 0)
        def _():
            state = walk([(i, True, 0, bq), (i - 1, False, 0, top_rows)], zero)
            low_alive = alive(state, top_rows)
            if top_rows < bq:
                state = lax.cond(low_alive > 0, lambda st: walk([(i - 1, False, top_rows, bq)], st), lambda st: st,
                                 state)

            def cond(carry):
                return jnp.logical_and(carry[0] >= 0, carry[1] > 0)

            def step(carry):
                state = walk([(carry[0], False, 0, bq)], carry[2])
                return carry[0] - 1, alive(state), state

            kb, _, state = lax.while_loop(cond, step, (i - 2, alive(state), state))
            finish(state, kb + 1, low_alive)

    return pl.pallas_call(
        body, name="sb_fwd",
        grid=(SB_HEADS // nh, nq),
        in_specs=[pl.BlockSpec((bq, width), lambda h, i: (i, h)),
                  pl.BlockSpec((s, width), lambda h, i: (0, h)),
                  pl.BlockSpec((s, width), lambda h, i: (0, h))],
        out_specs=[pl.BlockSpec((bq, width), lambda h, i: (i, h)),
                   pl.BlockSpec((bq, width), lambda h, i: (i, h)),
                   pl.BlockSpec(memory_space=pltpu.SMEM)],
        out_shape=[jax.ShapeDtypeStruct((s, SB_WIDTH), F32), jax.ShapeDtypeStruct((s, SB_WIDTH), F32),
                   jax.ShapeDtypeStruct((SB_HEADS // nh, nq), I32)],
        compiler_params=_params(("arbitrary", "arbitrary"), VMEM_LIMIT),
    )(q, k, v)


def _mla_fwd_call(q, k, v):
    s = q.shape[0]
    bq = _row_tile(s, MLA_BQ)
    bk = _row_tile(s, MLA_BK)
    nq = s // bq
    assert bk % bq == 0
    step = min(bk // 2, MLA_STEP)
    nsub = bk // step
    assert nsub % 2 == 0

    def body(q_ref, k_ref, v_ref, o_ref, lse_ref, p_ref, s_ref):
        i = pl.program_id(1)
        lane = lax.broadcasted_iota(I32, (bq, LANES), 1)
        row = lax.broadcasted_iota(I32, (step, bq), 1)
        col = lax.broadcasted_iota(I32, (step, bq), 0)
        n_full = (i * bq) // bk

        def keys(g):
            return pl.ds(pl.multiple_of(g * step, step), step)

        def join(left, right, qlo):
            return right if qlo == 0 else jnp.concatenate([left[:, :qlo], right], axis=1)

        def put_scores(g, slot, qlo=0):
            for hh in range(2):
                cols = slice(hh * HEAD_PAD, (hh + 1) * HEAD_PAD)
                s_ref[slot, hh, :, qlo:] = _dot_nt(k_ref[keys(g), cols], q_ref[qlo:, cols])

        def add_pv(carry, g, slot, qlo=0):
            vblk = v_ref[keys(g), :]
            out = []
            for hh, (m, l, acc, alpha) in enumerate(carry):
                upd = alpha[:, qlo:] * acc[:, qlo:] + _dot_tn(vblk, p_ref[slot, hh, :, qlo:])
                out.append((m, l, join(acc, upd, qlo), alpha))
            return tuple(out)

        def substep(g, slot, carry, masked, prefetch, qlo=0, next_qlo=0, prev_qlo=0):
            if prefetch:
                put_scores(g + 1, 1 - slot, next_qlo)
            carry = add_pv(carry, jnp.maximum(g - 1, 0), 1 - slot, prev_qlo)
            new = []
            for hh in range(2):
                m, l, acc, _ = carry[hh]
                sc = s_ref[slot, hh, :, qlo:]
                if masked:
                    sc = jnp.where(col[:, qlo:] + g * step <= row[:, qlo:] + i * bq, sc, MASK_NEG)
                m_new = jnp.maximum(m[:, qlo:], jnp.max(sc, axis=0, keepdims=True))
                p = jnp.exp2(sc - m_new)
                alpha = jnp.exp2(m[:, qlo:] - m_new)
                l_new = alpha * l[:, qlo:] + jnp.sum(p, axis=0, keepdims=True)
                p_ref[slot, hh, :, qlo:] = p.astype(BF16)
                new.append((join(m, m_new, qlo), join(l, l_new, qlo), acc, join(jnp.ones_like(m), alpha, qlo)))
            return tuple(new)

        def first_query(t, masked):
            return t * step if (masked and bk == bq and 0 <= t < nsub) else 0

        def chunk(kb, carry, masked):
            for t in range(nsub):
                last = masked and t == nsub - 1
                carry = substep(nsub * kb + t, t % 2, carry, masked, not last, first_query(t, masked),
                                first_query(t + 1, masked), first_query(t - 1, masked))
            return carry

        p_ref[1] = jnp.zeros_like(p_ref[1])
        put_scores(0, 0)
        one = (jnp.full((1, bq), MASK_NEG, F32), jnp.zeros((1, bq), F32), jnp.zeros((LANES, bq), F32),
               jnp.ones((1, bq), F32))
        carry = lax.fori_loop(0, n_full, lambda kb, cr: chunk(kb, cr, False), (one, one))
        carry = chunk(n_full, carry, True)
        (m0, l0, a0, _), (m1, l1, a1, _) = add_pv(carry, nsub * n_full + nsub - 1, 1, first_query(nsub - 1, True))
        o_ref[...] = jnp.where(lane < HALF_LANES, (a0 / l0).T, (a1 / l1).T)
        sub = lax.broadcasted_iota(I32, (8, bq), 0)
        lse_ref[...] = jnp.where(sub == 0, m0 + jnp.log2(l0), jnp.where(sub == 1, m1 + jnp.log2(l1), 0.0))

    return pl.pallas_call(
        body, name="mla_fwd",
        grid=(4, nq),
        in_specs=[pl.BlockSpec((bq, 2 * HEAD_PAD), lambda h, i: (i, h)),
                  pl.BlockSpec((s, 2 * HEAD_PAD), lambda h, i: (0, h)),
                  pl.BlockSpec((s, LANES), lambda h, i: (0, h))],
        out_specs=[pl.BlockSpec((bq, LANES), lambda h, i: (i, h)),
                   pl.BlockSpec((None, 8, bq), lambda h, i: (h, 0, i))],
        out_shape=[jax.ShapeDtypeStruct((s, MLA_WIDTH), F32), jax.ShapeDtypeStruct((4, 8, s), F32)],
        scratch_shapes=[pltpu.VMEM((2, 2, step, bq), BF16), pltpu.VMEM((2, 2, step, bq), F32)],
        compiler_params=_params(("arbitrary", "arbitrary"), VMEM_LIMIT),
    )(q, k, v)


def _out_call(o_sb, g_sb, o_mla, g_mla, x, target, ada, w_out_bf):
    s = x.shape[0]
    tm = _row_tile(s, 256)

    def body(osb_ref, gsb_ref, oml_ref, gml_ref, x_ref, t_ref, gate_ref, w_ref,
             dosb_ref, doml_ref, dgsb_ref, dgml_ref, dy_ref, gw_ref, dgate_ref, sq_ref):
        @pl.when(pl.program_id(0) == 0)
        def _():
            gw_ref[...] = jnp.zeros_like(gw_ref)
            dgate_ref[...] = jnp.zeros_like(dgate_ref)
            sq_ref[...] = jnp.zeros_like(sq_ref)

        g_s, g_m = gsb_ref[...], gml_ref[...]
        sig_s, sig_m = _sigmoid(g_s), _sigmoid(g_m)
        silu_s, silu_m = g_s * sig_s, g_m * sig_m
        o_s, o_m = osb_ref[...], oml_ref[...]
        mixed = jnp.concatenate([o_s * silu_s, o_m * silu_m], axis=1).astype(BF16)
        u = _dot(mixed, w_ref[...])
        gate_v = gate_ref[...]
        err = x_ref[...] + gate_v * u - t_ref[...]
        sq_ref[...] += jnp.sum(err * err, axis=0, keepdims=True)
        dy = err * (1.0 / D_MODEL)
        dy_ref[...] = dy
        dgate_ref[...] += jnp.sum(dy * u, axis=0, keepdims=True)
        du = (dy * gate_v).astype(BF16)
        gw_ref[...] += _dot_tn(mixed, du)
        dmix = _dot_nt(du, w_ref[...])
        dm_s, dm_m = dmix[:, :SB_WIDTH], dmix[:, SB_WIDTH:]
        dosb_ref[...] = (dm_s * silu_s).astype(BF16)
        doml_ref[...] = (dm_m * silu_m).astype(BF16)
        dgsb_ref[...] = (dm_s * o_s * (sig_s * (1.0 + g_s * (1.0 - sig_s)))).astype(BF16)
        dgml_ref[...] = (dm_m * o_m * (sig_m * (1.0 + g_m * (1.0 - sig_m)))).astype(BF16)

    return pl.pallas_call(
        body, name="out_proj_loss",
        grid=(s // tm,),
        in_specs=[_rows(tm, 512), _rows(tm, 512), _rows(tm, 512), _rows(tm, 512),
                  _rows(tm, D_MODEL), _rows(tm, D_MODEL), _ada_part(2), _full((D_MODEL, D_MODEL))],
        out_specs=[_rows(tm, 512), _rows(tm, 512), _rows(tm, 512), _rows(tm, 512), _rows(tm, D_MODEL),
                   _full((D_MODEL, D_MODEL)), _full((1, D_MODEL)), _full((1, D_MODEL))],
        out_shape=[jax.ShapeDtypeStruct((s, 512), BF16)] * 4
        + [jax.ShapeDtypeStruct((s, D_MODEL), F32), jax.ShapeDtypeStruct((D_MODEL, D_MODEL), F32),
           jax.ShapeDtypeStruct((1, D_MODEL), F32), jax.ShapeDtypeStruct((1, D_MODEL), F32)],
        compiler_params=_params(("arbitrary",), VMEM_LIMIT),
    )(o_sb, g_sb, o_mla, g_mla, x, target, ada, w_out_bf)


def _head_mask(lane, hh):
    return jnp.where((lane >= HALF_LANES) if hh else (lane < HALF_LANES), 1.0, 0.0)


def _pick_lane(packed, lane, which):
    return jnp.sum(jnp.where(lane == which, packed, 0.0), axis=1, keepdims=True)


def _sb_bwd_call(kstart, q, k, v, do, rfin):
    s = q.shape[0]
    bq = _row_tile(s, 256)
    nq = s // bq
    nh = SB_GROUP
    width = nh * HALF_LANES
    top_rows = min(bq, SB_TOP_ROWS)

    def body(ks_ref, q_ref, k_ref, v_ref, do_ref, r_ref, dq_ref, dk_ref, dv_ref):
        hp, i = pl.program_id(0), pl.program_id(1)

        @pl.when(i == 0)
        def _():
            dk_ref[...] = jnp.zeros_like(dk_ref)
            dv_ref[...] = jnp.zeros_like(dv_ref)

        lane = lax.broadcasted_iota(I32, (bq, LANES), 1)
        row = lax.broadcasted_iota(I32, (bq, bq), 0)
        col = lax.broadcasted_iota(I32, (bq, bq), 1)
        upto = jnp.where(row <= col, 1.0, 0.0).astype(BF16)
        before = jnp.where(row < col, 1.0, 0.0).astype(BF16)
        masks = [_head_mask(lane, hh).astype(BF16) for hh in range(2)]
        qms = [q_ref[:, _pair(hh)] * jnp.asarray(SB_SCALE, BF16) * masks[hh % 2] for hh in range(nh)]
        doms = [do_ref[:, _pair(hh)] * masks[hh % 2] for hh in range(nh)]
        totals = [_pick_lane(r_ref[:, _pair(hh)], lane, HALF_LANES * (hh % 2)) for hh in range(nh)]
        strict = col < row

        def walk(blocks, state):
            chains = [(kb, diagonal, slice(r0, r1), hh) for kb, diagonal, r0, r1 in blocks for hh in range(nh)]
            keys = lambda kb: pl.ds(pl.multiple_of(kb * bq, bq), bq)
            cut = lambda x, diagonal, rs: jnp.where(strict[rs], x, 0.0) if diagonal else x
            zs = [_dot_nt(qms[hh][rs], k_ref[keys(kb), _pair(hh)]) for kb, _, rs, hh in chains]
            dws = [_dot_nt(doms[hh][rs], v_ref[keys(kb), _pair(hh)]) for kb, _, rs, hh in chains]
            pairs = []
            for z, (_, diagonal, rs, _) in zip(zs, chains):
                ls, lk = _log_sigmoid_pair(z)
                pairs.append((ls, cut(lk, diagonal, rs)))
            incls = [_split_dot(lk, upto) for _, lk in pairs]
            pres = [st[0] for st in state]
            ws, gs = [], []
            for (ls, lk), incl, dw, (_, diagonal, rs, hh) in zip(pairs, incls, dws, chains):
                w = cut(jnp.exp(ls + ((totals[hh][rs] - pres[hh][rs]) - incl)), diagonal, rs)
                ws.append(w.astype(BF16))
                gs.append(w * dw)
                pres[hh] = _set_rows(pres[hh], rs, pres[hh][rs] + jnp.sum(lk, axis=1, keepdims=True))
            gsums = [_dot(g.astype(BF16), before) for g in gs]
            gpres = [st[1] for st in state]
            dzs = []
            for (ls, _), g, gsum, (_, diagonal, rs, hh) in zip(pairs, gs, gsums, chains):
                dzs.append(cut(g - jnp.exp(ls) * (g + (gpres[hh][rs] + gsum)), diagonal, rs).astype(BF16))
                gpres[hh] = _set_rows(gpres[hh], rs, gpres[hh][rs] + jnp.sum(g, axis=1, keepdims=True))
            dqs = [st[2] for st in state]
            dk_parts, dv_parts = [], []
            for dzb, w, (kb, _, rs, hh) in zip(dzs, ws, chains):
                dk_parts.append(_dot_tn(dzb, qms[hh][rs]))
                dv_parts.append(_dot_tn(w, doms[hh][rs]))
                dqs[hh] = _set_rows(dqs[hh], rs, dqs[hh][rs] + _dot(dzb, k_ref[keys(kb), _pair(hh)]))
            for b, (kb, _, _, _) in enumerate(blocks):
                for pair in range(nh // 2):
                    c0 = b * nh + 2 * pair
                    dk_ref[keys(kb), _pair(2 * pair)] += dk_parts[c0] + dk_parts[c0 + 1]
                    dv_ref[keys(kb), _pair(2 * pair)] += dv_parts[c0] + dv_parts[c0 + 1]
            return tuple(zip(pres, gpres, dqs))

        def finish(state):
            for pair in range(nh // 2):
                both = jnp.where(lane < HALF_LANES, state[2 * pair][2], state[2 * pair + 1][2])
                dq_ref[:, _pair(2 * pair)] = (both * SB_SCALE).astype(BF16)

        zero = ((jnp.zeros((bq, 1), F32), jnp.zeros((bq, 1), F32), jnp.zeros((bq, LANES), F32)),) * nh

        @pl.when(i == 0)
        def _():
            finish(walk([(0, True, 0, bq)], zero))

        @pl.when(i > 0)
        def _():
            first, low_rows_took_left = ks_ref[hp, i] // 2, ks_ref[hp, i] % 2
            state = lax.fori_loop(first, i - 1, lambda kb, st: walk([(kb, False, 0, bq)], st), zero)
            if top_rows < bq:
                state = lax.cond(low_rows_took_left > 0, lambda st: walk([(i - 1, False, top_rows, bq)], st),
                                 lambda st: st, state)
            finish(walk([(i - 1, False, 0, top_rows), (i, True, 0, bq)], state))

    return pl.pallas_call(
        body, name="sb_bwd",
        grid_spec=pltpu.PrefetchScalarGridSpec(
            num_scalar_prefetch=1, grid=(SB_HEADS // nh, nq),
            in_specs=[pl.BlockSpec((bq, width), lambda h, i, ks: (i, h)),
                      pl.BlockSpec((s, width), lambda h, i, ks: (0, h), pipeline_mode=pl.Buffered(1)),
                      pl.BlockSpec((s, width), lambda h, i, ks: (0, h), pipeline_mode=pl.Buffered(1)),
                      pl.BlockSpec((bq, width), lambda h, i, ks: (i, h)),
                      pl.BlockSpec((bq, width), lambda h, i, ks: (i, h))],
            out_specs=[pl.BlockSpec((bq, width), lambda h, i, ks: (i, h)),
                       pl.BlockSpec((s, width), lambda h, i, ks: (0, h), pipeline_mode=pl.Buffered(1)),
                       pl.BlockSpec((s, width), lambda h, i, ks: (0, h), pipeline_mode=pl.Buffered(1))]),
        out_shape=[jax.ShapeDtypeStruct((s, SB_WIDTH), BF16), jax.ShapeDtypeStruct((s, SB_WIDTH), F32),
                   jax.ShapeDtypeStruct((s, SB_WIDTH), F32)],
        compiler_params=_params(("arbitrary", "arbitrary"), VMEM_LIMIT),
    )(kstart, q, k, v, do, rfin)


def _mla_bwd_call(q, k, v, do, o, lse):
    s = q.shape[0]
    bq = _row_tile(s, MLA_BWD_BQ)
    bk = _row_tile(s, MLA_BWD_BK)
    nq = s // bq
    assert bk % bq == 0
    step = min(bk // 2, MLA_BWD_STEP)
    nsub = bk // step
    assert nsub % 2 == 0

    def body(q_ref, k_ref, v_ref, do_ref, o_ref, lse_ref, dq_ref, dk_ref, dv_ref, dom_ref, s_ref, dp_ref, pb_ref,
             ds_ref):
        i = pl.program_id(1)

        @pl.when(i == 0)
        def _():
            dk_ref[...] = jnp.zeros_like(dk_ref)
            dv_ref[...] = jnp.zeros_like(dv_ref)

        lane = lax.broadcasted_iota(I32, (bq, LANES), 1)
        row = lax.broadcasted_iota(I32, (step, bq), 1)
        col = lax.broadcasted_iota(I32, (step, bq), 0)
        n_full = (i * bq) // bk
        do2 = do_ref[...]
        prod = do2.astype(F32) * o_ref[...]
        ones = jnp.ones((8, LANES), BF16)
        deltas, lses = [], []
        for hh in range(2):
            head = _head_mask(lane, hh)
            dom_ref[hh] = do2 * head.astype(BF16)
            part = prod * head
            hi = part.astype(BF16)
            lo = (part - hi.astype(F32)).astype(BF16)
            deltas.append((_dot_nt(ones, hi) + _dot_nt(ones, lo))[0:1])
            lses.append(lse_ref[hh:hh + 1, :])

        def keys(g):
            return pl.ds(pl.multiple_of(g * step, step), step)

        def heads():
            return [(hh, slice(hh * HEAD_PAD, (hh + 1) * HEAD_PAD)) for hh in range(2)]

        def put_products(g, slot, qlo=0):
            vblk = v_ref[keys(g), :]
            for hh, cols in heads():
                s_ref[slot, hh, :, qlo:] = _dot_nt(k_ref[keys(g), cols], q_ref[qlo:, cols])
                dp_ref[slot, hh, :, qlo:] = _dot_nt(vblk, dom_ref[hh, qlo:, :])

        def add_grads(dqs, g, slot, qlo=0):
            rows = keys(g)
            new, dv_parts = [], []
            for hh, cols in heads():
                ds = ds_ref[slot, hh, :, qlo:]
                dk_ref[rows, cols] += _dot(ds, q_ref[qlo:, cols])
                dv_parts.append(_dot(pb_ref[slot, hh, :, qlo:], dom_ref[hh, qlo:, :]))
                upd = dqs[hh][:, qlo:] + _dot_tn(k_ref[rows, cols], ds)
                new.append(upd if qlo == 0 else jnp.concatenate([dqs[hh][:, :qlo], upd], axis=1))
            dv_ref[rows, :] += dv_parts[0] + dv_parts[1]
            return tuple(new)

        def substep(g, slot, dqs, masked, prefetch, qlo=0, next_qlo=0, prev_qlo=0):
            if prefetch:
                put_products(g + 1, 1 - slot, next_qlo)
            dqs = add_grads(dqs, jnp.maximum(g - 1, 0), 1 - slot, prev_qlo)
            for hh, _ in heads():
                p = jnp.exp2(s_ref[slot, hh, :, qlo:] - lses[hh][:, qlo:])
                if masked:
                    p = jnp.where(col[:, qlo:] + g * step <= row[:, qlo:] + i * bq, p, 0.0)
                ds_ref[slot, hh, :, qlo:] = (p * (dp_ref[slot, hh, :, qlo:] - deltas[hh][:, qlo:])).astype(BF16)
                pb_ref[slot, hh, :, qlo:] = p.astype(BF16)
            return dqs

        def first_query(t, masked):
            return t * step if (masked and bk == bq and 0 <= t < nsub) else 0

        def chunk(kb, dqs, masked):
            for t in range(nsub):
                last = masked and t == nsub - 1
                dqs = substep(nsub * kb + t, t % 2, dqs, masked, not last, first_query(t, masked),
                              first_query(t + 1, masked), first_query(t - 1, masked))
            return dqs

        ds_ref[1] = jnp.zeros_like(ds_ref[1])
        pb_ref[1] = jnp.zeros_like(pb_ref[1])
        put_products(0, 0)
        zero = jnp.zeros((HEAD_PAD, bq), F32)
        dqs = lax.fori_loop(0, n_full, lambda kb, dqs: chunk(kb, dqs, False), (zero, zero))
        dqs = chunk(n_full, dqs, True)
        dqs = add_grads(dqs, nsub * n_full + nsub - 1, 1, first_query(nsub - 1, True))
        dq_ref[:, :HEAD_PAD] = dqs[0].T * MLA_SCALE
        dq_ref[:, HEAD_PAD:] = dqs[1].T * MLA_SCALE

    return pl.pallas_call(
        body, name="mla_bwd",
        grid=(4, nq),
        in_specs=[pl.BlockSpec((bq, 2 * HEAD_PAD), lambda h, i: (i, h)),
                  pl.BlockSpec((s, 2 * HEAD_PAD), lambda h, i: (0, h)),
                  pl.BlockSpec((s, LANES), lambda h, i: (0, h)),
                  pl.BlockSpec((bq, LANES), lambda h, i: (i, h)),
                  pl.BlockSpec((bq, LANES), lambda h, i: (i, h)),
                  pl.BlockSpec((None, 8, bq), lambda h, i: (h, 0, i))],
        out_specs=[pl.BlockSpec((bq, 2 * HEAD_PAD), lambda h, i: (i, h)),
                   pl.BlockSpec((s, 2 * HEAD_PAD), lambda h, i: (0, h), pipeline_mode=pl.Buffered(1)),
                   pl.BlockSpec((s, LANES), lambda h, i: (0, h), pipeline_mode=pl.Buffered(1))],
        out_shape=[jax.ShapeDtypeStruct((s, MLA_PAD_WIDTH), F32), jax.ShapeDtypeStruct((s, MLA_PAD_WIDTH), F32),
                   jax.ShapeDtypeStruct((s, MLA_WIDTH), F32)],
        scratch_shapes=[pltpu.VMEM((2, bq, LANES), BF16),
                        pltpu.VMEM((2, 2, step, bq), F32), pltpu.VMEM((2, 2, step, bq), F32),
                        pltpu.VMEM((2, 2, step, bq), BF16), pltpu.VMEM((2, 2, step, bq), BF16)],
        compiler_params=_params(("arbitrary", "arbitrary"), VMEM_LIMIT),
    )(q, k, v, do, o, lse)


def _rms_bwd(d_out, inp, r, weight, n):
    normed = inp * r
    gw = d_out * weight
    d_in = r * (gw - normed * (jnp.sum(gw * normed, axis=-1, keepdims=True) * (1.0 / n)))
    return d_in, d_out * normed


def _mla_prep_bwd_call(dq, dk, dv, q0, k0, cqn, ckvn, c_q, c_kv, cos_t, sin_t,
                       q_lora_norm, kv_lora_norm, qhn_pad, khn_pad, w_uq_bf, w_uk_bf, w_uv_bf):
    s = dq.shape[0]
    tm = _row_tile(s, 256)

    def body(dq_ref, dk_ref, dv_ref, q0_ref, k0_ref, cqn_ref, ckvn_ref, cq_ref, ckv_ref,
             cos_ref, sin_ref, qln_ref, kvln_ref, qhn_ref, khn_ref, wuq_ref, wuk_ref, wuv_ref,
             dcq_ref, dckv_ref, dkr_ref, gwuq_ref, gwuk_ref, gwuv_ref, gqln_ref, gkvln_ref, gqhn_ref, gkhn_ref,
             dq0_ref, dk0_ref, tmp_ref):
        @pl.when(pl.program_id(0) == 0)
        def _():
            for ref in (gwuq_ref, gwuk_ref, gwuv_ref, gqln_ref, gkvln_ref, gqhn_ref, gkhn_ref):
                ref[...] = jnp.zeros_like(ref)

        cos_t, sin_t = cos_ref[...], sin_ref[...]
        lane = lax.broadcasted_iota(I32, (tm, LANES), 1)
        rope_lanes = jnp.logical_or(lane < ROPE_HALF,
                                    jnp.logical_and(lane >= HALF_LANES, lane < HALF_LANES + ROPE_HALF))
        heads = [slice(h * HEAD_PAD, (h + 1) * HEAD_PAD) for h in range(MLA_HEADS)]

        def head_norm_bwd(d_ref, x0_ref, w_ref, out_ref, scale):
            w = w_ref[...]
            inv = [lax.rsqrt(jnp.sum(x0_ref[:, cols] * x0_ref[:, cols], axis=-1, keepdims=True)
                             * (1.0 / MLA_QK_DIM) + EPS) for cols in heads]
            for cols in heads:
                tmp_ref[:, cols] = _rope_adjoint(d_ref[:, cols] * scale, cos_t, sin_t)
            dots = [jnp.sum(tmp_ref[:, cols] * w * (x0_ref[:, cols] * r), axis=-1, keepdims=True)
                    for cols, r in zip(heads, inv)]
            g_w = jnp.zeros((1, LANES), F32)
            rope_sum = jnp.zeros((tm, LANES), F32)
            for cols, r, dot in zip(heads, inv, dots):
                normed = x0_ref[:, cols] * r
                d_n = tmp_ref[:, cols]
                d_x0 = r * (d_n * w - normed * (dot * (1.0 / MLA_QK_DIM)))
                out_ref[:, cols] = d_x0.astype(BF16)
                g_w = g_w + jnp.sum(d_n * normed, axis=0, keepdims=True)
                rope_sum = rope_sum + jnp.where(rope_lanes, d_x0, 0.0)
            return g_w, rope_sum

        g_qhn, _ = head_norm_bwd(dq_ref, q0_ref, qhn_ref, dq0_ref, 1.0)
        g_khn, d_kr = head_norm_bwd(dk_ref, k0_ref, khn_ref, dk0_ref, LN2)
        cqn, ckvn = cqn_ref[...], ckvn_ref[...]
        d_q0b, d_k0b, dvb = dq0_ref[...], dk0_ref[...], dv_ref[...].astype(BF16)
        d_cqn = _dot_nt(d_q0b, wuq_ref[...])
        gwuq_ref[...] += _dot_tn(cqn, d_q0b)
        d_ckvn = _dot_nt(d_k0b, wuk_ref[...]) + _dot_nt(dvb, wuv_ref[...])
        gwuk_ref[...] += _dot_tn(ckvn, d_k0b)
        gwuv_ref[...] += _dot_tn(ckvn, dvb)
        gqhn_ref[...] += g_qhn
        gkhn_ref[...] += g_khn
        dkr_ref[...] = d_kr.astype(BF16)
        cq = cq_ref[...]
        rcq = lax.rsqrt(jnp.mean(cq * cq, axis=-1, keepdims=True) + EPS)
        d_cq, gl = _rms_bwd(d_cqn, cq, rcq, qln_ref[...], Q_LORA_RANK)
        dcq_ref[...] = d_cq.astype(BF16)
        gqln_ref[...] += jnp.sum(gl, axis=0, keepdims=True)
        ckv = ckv_ref[...]
        rckv = lax.rsqrt(jnp.mean(ckv * ckv, axis=-1, keepdims=True) + EPS)
        d_ckv, gl = _rms_bwd(d_ckvn, ckv, rckv, kvln_ref[...], KV_LORA_RANK)
        dckv_ref[...] = d_ckv.astype(BF16)
        gkvln_ref[...] += jnp.sum(gl, axis=0, keepdims=True)

    return pl.pallas_call(
        body, name="mla_prep_bwd",
        grid=(s // tm,),
        in_specs=[_rows(tm, MLA_PAD_WIDTH), _rows(tm, MLA_PAD_WIDTH), _rows(tm, MLA_WIDTH),
                  _rows(tm, MLA_PAD_WIDTH), _rows(tm, MLA_PAD_WIDTH),
                  _rows(tm, Q_LORA_RANK), _rows(tm, KV_LORA_RANK), _rows(tm, Q_LORA_RANK), _rows(tm, KV_LORA_RANK),
                  _rows(tm, LANES), _rows(tm, LANES),
                  _full((1, Q_LORA_RANK)), _full((1, KV_LORA_RANK)), _full((1, LANES)), _full((1, LANES)),
                  _full((Q_LORA_RANK, MLA_PAD_WIDTH)), _full((KV_LORA_RANK, MLA_PAD_WIDTH)),
                  _full((KV_LORA_RANK, MLA_WIDTH))],
        out_specs=[_rows(tm, Q_LORA_RANK), _rows(tm, KV_LORA_RANK), _rows(tm, LANES),
                   _full((Q_LORA_RANK, MLA_PAD_WIDTH)), _full((KV_LORA_RANK, MLA_PAD_WIDTH)),
                   _full((KV_LORA_RANK, MLA_WIDTH)),
                   _full((1, Q_LORA_RANK)), _full((1, KV_LORA_RANK)), _full((1, LANES)), _full((1, LANES))],
        out_shape=[jax.ShapeDtypeStruct((s, Q_LORA_RANK), BF16), jax.ShapeDtypeStruct((s, KV_LORA_RANK), BF16),
                   jax.ShapeDtypeStruct((s, LANES), BF16),
                   jax.ShapeDtypeStruct((Q_LORA_RANK, MLA_PAD_WIDTH), F32),
                   jax.ShapeDtypeStruct((KV_LORA_RANK, MLA_PAD_WIDTH), F32),
                   jax.ShapeDtypeStruct((KV_LORA_RANK, MLA_WIDTH), F32),
                   jax.ShapeDtypeStruct((1, Q_LORA_RANK), F32), jax.ShapeDtypeStruct((1, KV_LORA_RANK), F32),
                   jax.ShapeDtypeStruct((1, LANES), F32), jax.ShapeDtypeStruct((1, LANES), F32)],
        scratch_shapes=[pltpu.VMEM((tm, MLA_PAD_WIDTH), BF16), pltpu.VMEM((tm, MLA_PAD_WIDTH), BF16),
                        pltpu.VMEM((tm, MLA_PAD_WIDTH), F32)],
        compiler_params=_params(("arbitrary",), VMEM_LIMIT),
    )(dq, dk, dv, q0, k0, cqn, ckvn, c_q, c_kv, cos_t, sin_t,
      q_lora_norm, kv_lora_norm, qhn_pad, khn_pad, w_uq_bf, w_uk_bf, w_uv_bf)


def _dh_call(pieces, hb, x, dy, ada, norm_w, w_in_bf):
    s = x.shape[0]
    tm = _row_tile(s, 256)
    widths = [p.shape[1] for p in pieces]
    offsets = [sum(widths[:j]) for j in range(len(widths))]
    assert offsets[-1] + widths[-1] == IN_COLS_PAD
    n = len(pieces)

    def body(*refs):
        p_refs = refs[:n]
        (hb_ref, x_ref, dy_ref, sh_ref, sc_ref, nw_ref, w_ref, gx_ref, gw_ref, dsh_ref, dsc_ref, gnw_ref,
         dp_ref) = refs[n:]

        @pl.when(pl.program_id(0) == 0)
        def _():
            gw_ref[...] = jnp.zeros_like(gw_ref)
            dsh_ref[...] = jnp.zeros_like(dsh_ref)
            dsc_ref[...] = jnp.zeros_like(dsc_ref)
            gnw_ref[...] = jnp.zeros_like(gnw_ref)

        for p_ref, c0, width in zip(p_refs, offsets, widths):
            dp_ref[:, c0:c0 + width] = p_ref[...].astype(BF16)
        gw_ref[...] += _dot_tn(hb_ref[...], dp_ref[...])
        dh = _dot_nt(dp_ref[...], w_ref[...])
        xx = x_ref[...]
        r0 = lax.rsqrt(jnp.mean(xx * xx, axis=-1, keepdims=True) + EPS)
        xn = xx * r0
        nw = nw_ref[...]
        dsh_ref[...] += jnp.sum(dh, axis=0, keepdims=True)
        dsc_ref[...] += jnp.sum(dh * (xn * nw), axis=0, keepdims=True)
        dn = dh * (1.0 + sc_ref[...])
        gnw_ref[...] += jnp.sum(dn * xn, axis=0, keepdims=True)
        dxn = dn * nw
        gx_ref[...] = dy_ref[...] + r0 * (dxn - xn * jnp.mean(dxn * xn, axis=-1, keepdims=True))

    return pl.pallas_call(
        body, name="in_proj_bwd",
        grid=(s // tm,),
        in_specs=[_rows(tm, w) for w in widths]
        + [_rows(tm, D_MODEL), _rows(tm, D_MODEL), _rows(tm, D_MODEL),
           _ada_part(0), _ada_part(1), _full((1, D_MODEL)),
           pl.BlockSpec((D_MODEL, IN_COLS_PAD), lambda i: (0, 0), pipeline_mode=pl.Buffered(1))],
        out_specs=[_rows(tm, D_MODEL),
                   pl.BlockSpec((D_MODEL, IN_COLS_PAD), lambda i: (0, 0), pipeline_mode=pl.Buffered(1)),
                   _full((1, D_MODEL)), _full((1, D_MODEL)), _full((1, D_MODEL))],
        out_shape=[jax.ShapeDtypeStruct((s, D_MODEL), F32), jax.ShapeDtypeStruct((D_MODEL, IN_COLS_PAD), F32),
                   jax.ShapeDtypeStruct((1, D_MODEL), F32), jax.ShapeDtypeStruct((1, D_MODEL), F32),
                   jax.ShapeDtypeStruct((1, D_MODEL), F32)],
        scratch_shapes=[pltpu.VMEM((tm, IN_COLS_PAD), BF16)],
        compiler_params=_params(("arbitrary",), VMEM_LIMIT),
    )(*pieces, hb, x, dy, ada, ada, norm_w, w_in_bf)


def _adamw(g, w, m, v):
    m = ADAM_B1 * m + (1.0 - ADAM_B1) * g
    v = ADAM_B2 * v + (1.0 - ADAM_B2) * (g * g)
    m_hat = m / (1.0 - ADAM_B1 ** ADAM_STEP)
    v_hat = v / (1.0 - ADAM_B2 ** ADAM_STEP)
    delta = -ADAM_LR * (m_hat / (jnp.sqrt(v_hat) + ADAM_EPS) + ADAM_WD * w)
    return delta, m, v


def _adam_shard_call(name, own, sib, w, m, v):
    r, c = w.shape
    tr = r if r <= 512 else 256

    def body(own_ref, sib_ref, w_ref, m_ref, v_ref, g_ref, d_ref, nm_ref, nv_ref):
        a = ((own_ref[0].astype(F32) + own_ref[1].astype(F32)) + own_ref[2].astype(F32)) + own_ref[3].astype(F32)
        b = ((sib_ref[0].astype(F32) + sib_ref[1].astype(F32)) + sib_ref[2].astype(F32)) + sib_ref[3].astype(F32)
        g = a + b
        g_ref[...] = g
        d_ref[...], nm_ref[...], nv_ref[...] = _adamw(g, w_ref[...], m_ref[...], v_ref[...])

    part = pl.BlockSpec((4, tr, c), lambda i: (0, i, 0))
    blk = pl.BlockSpec((tr, c), lambda i: (i, 0))
    return pl.pallas_call(
        body, name=name,
        grid=(r // tr,),
        in_specs=[part, part, blk, blk, blk],
        out_specs=[blk] * 4,
        out_shape=[jax.ShapeDtypeStruct((r, c), F32)] * 4,
        compiler_params=_params(("arbitrary",), VMEM_LIMIT),
    )(own, sib, w, m, v)


def _adam_ada_call(c_all, d_all, w, m, v):
    r, c = w.shape
    tr = 256

    def body(c_ref, d_ref, w_ref, m_ref, v_ref, g_ref, dl_ref, nm_ref, nv_ref):
        cc = c_ref[...]
        sc = cc * _sigmoid(cc)
        dd = d_ref[...]
        sc_hi = sc.astype(BF16)
        sc_lo = (sc - sc_hi.astype(F32)).astype(BF16)
        dd_hi = dd.astype(BF16)
        dd_lo = (dd - dd_hi.astype(F32)).astype(BF16)
        g = _dot_tn(sc_hi, dd_hi) + (_dot_tn(sc_hi, dd_lo) + _dot_tn(sc_lo, dd_hi))
        g_ref[...] = g
        dl_ref[...], nm_ref[...], nv_ref[...] = _adamw(g, w_ref[...], m_ref[...], v_ref[...])

    blk = pl.BlockSpec((tr, c), lambda i: (i, 0))
    return pl.pallas_call(
        body, name="adam_w_ada",
        grid=(r // tr,),
        in_specs=[pl.BlockSpec((16, tr), lambda i: (0, i)), pl.BlockSpec((16, c), lambda i: (0, 0)), blk, blk, blk],
        out_specs=[blk] * 4,
        out_shape=[jax.ShapeDtypeStruct((r, c), F32)] * 4,
        compiler_params=_params(("arbitrary",), VMEM_LIMIT),
    )(c_all, d_all, w, m, v)


def _adam_vectors_call(packs, offsets, vectors):
    nv = len(vectors)

    def body(*refs):
        p_ref, ins, outs = refs[0], refs[1:1 + 3 * nv], refs[1 + 3 * nv:]
        for j, off in enumerate(offsets):
            n = ins[3 * j].shape[1]
            span = -(-n // LANES) * LANES
            g = p_ref[0, :, off:off + span]
            for b in range(1, 8):
                g = g + p_ref[b, :, off:off + span]
            g = g[:, :n]
            outs[j][...] = g
            outs[nv + j][...], outs[2 * nv + j][...], outs[3 * nv + j][...] = _adamw(
                g, ins[3 * j][...], ins[3 * j + 1][...], ins[3 * j + 2][...])

    flat = [a for t in vectors for a in t]
    res = pl.pallas_call(
        body, name="adam_vectors",
        out_shape=[jax.ShapeDtypeStruct(t[0].shape, F32) for _ in range(4) for t in vectors],
    )(packs, *flat)
    return [res[k * nv:(k + 1) * nv] for k in range(4)]


ROPE_HALF = MLA_ROPE_DIM // 2
NOPE_A = MLA_NOPE_DIM - ROPE_HALF


def _zeros_like_lanes(t, n):
    return jnp.zeros(t.shape[:-1] + (n,), t.dtype)


def _to_head_lanes(t):
    nope, rope = t[..., :MLA_NOPE_DIM], t[..., MLA_NOPE_DIM:]
    return jnp.concatenate([rope[..., :ROPE_HALF], nope[..., :NOPE_A], rope[..., ROPE_HALF:], nope[..., NOPE_A:],
                            _zeros_like_lanes(t, HEAD_PAD - MLA_QK_DIM)], axis=-1)


def _from_head_lanes(t):
    return jnp.concatenate([t[..., ROPE_HALF:HALF_LANES], t[..., HALF_LANES + ROPE_HALF:MLA_QK_DIM],
                            t[..., :ROPE_HALF], t[..., HALF_LANES:HALF_LANES + ROPE_HALF]], axis=-1)


def _nope_to_head_lanes(t):
    return jnp.concatenate([_zeros_like_lanes(t, ROPE_HALF), t[..., :NOPE_A], _zeros_like_lanes(t, ROPE_HALF),
                            t[..., NOPE_A:], _zeros_like_lanes(t, HEAD_PAD - MLA_QK_DIM)], axis=-1)


def _rope_to_head_lanes(t):
    return jnp.concatenate([t[..., :ROPE_HALF], _zeros_like_lanes(t, HALF_LANES - ROPE_HALF), t[..., ROPE_HALF:],
                            _zeros_like_lanes(t, HALF_LANES - ROPE_HALF)], axis=-1)


def _rope_tables(positions):
    inv_freq = (ROPE_THETA ** (-jnp.arange(0, MLA_ROPE_DIM, 2, dtype=F32) / MLA_ROPE_DIM))[None]
    signed = _rope_to_head_lanes(jnp.concatenate([-inv_freq, inv_freq], axis=1))
    ang = positions.astype(F32)[:, None] * signed
    return jnp.cos(ang), jnp.sin(ang)


def _unshard_cols(g):
    return jnp.transpose(g, (1, 0, 2)).reshape(g.shape[1], 4 * g.shape[2])


def _shard_cols(g):
    r, c4 = g.shape
    return jnp.transpose(g.reshape(r, 4, c4 // 4), (1, 0, 2))


def kernel(x, c, positions, w_ada, b_ada, norm_w, w_in, q_lora_norm, w_uq, kv_lora_norm, w_ukv, q_head_norm, k_head_norm, w_out, loss_target, m_w_ada, m_b_ada, m_norm_w, m_w_in, m_q_lora_norm, m_w_uq, m_kv_lora_norm, m_w_ukv, m_q_head_norm, m_k_head_norm, m_w_out, v_w_ada, v_b_ada, v_norm_w, v_w_in, v_q_lora_norm, v_w_uq, v_kv_lora_norm, v_w_ukv, v_q_head_norm, v_k_head_norm, v_w_out):
    chip = 2 * lax.axis_index("x") + lax.axis_index("y")
    me8 = 2 * chip + lax.axis_index("c")
    ada_cols = w_ada.shape[2]
    c_all = _allgather_rows_call(c)[:, 0, :]
    ada_part = _ada_call(c_all, w_ada[0], lax.dynamic_slice_in_dim(b_ada, chip * ada_cols, ada_cols, axis=1))
    ada_g, win_g, wuq_g, wukv_g, wout_g = _gather_call(
        [ada_part[None]] + [w.astype(BF16) for w in (w_in, w_uq, w_ukv, w_out)], [False, True, True, True, True])
    ada = lax.dynamic_slice_in_dim(ada_g, me8, 1, axis=1).reshape(1, 4 * ada_cols)
    (sq_sum, grad_x, g_w_in, g_w_uq, g_w_ukv, g_w_out, d_ada, g_norm_w, g_qln, g_kvln, g_qhn, g_khn) = _local_step(
        x[0], ada, positions[0], loss_target[0], norm_w, win_g,
        q_lora_norm, _unshard_cols(wuq_g), kv_lora_norm, _unshard_cols(wukv_g), q_head_norm, k_head_norm,
        wout_g.reshape(D_MODEL, D_MODEL))

    grads = [g.astype(BF16) for g in (g_w_in, _shard_cols(g_w_uq), _shard_cols(g_w_ukv),
                                      g_w_out.reshape(4, D_MODEL // 4, D_MODEL))]
    pieces = [d_ada, g_norm_w, g_qln, g_kvln, g_qhn, g_khn, (0.5 * sq_sum / D_MODEL).reshape(1, 1)]
    spans = [-(-p.shape[1] // LANES) * LANES for p in pieces]
    starts = [sum(spans[:j]) for j in range(len(spans))]
    small = jnp.concatenate([jnp.pad(p, ((0, 0), (0, sp - p.shape[1]))) for p, sp in zip(pieces, spans)], axis=1)
    own, sib, packs = _exchange_call(grads, small)
    loss = jnp.sum(packs[:, 0, starts[-1]])

    names = ["adam_w_in", "adam_w_uq", "adam_w_ukv", "adam_w_out"]
    shard_w = [(w_in, m_w_in, v_w_in), (w_uq, m_w_uq, v_w_uq), (w_ukv, m_w_ukv, v_w_ukv),
               (w_out, m_w_out, v_w_out)]
    res = {}
    for name, o_g, s_g, (w, m, v) in zip(names, own, sib, shard_w):
        res[name] = _adam_shard_call(name, o_g, s_g, w[0], m[0], v[0])
    d_all = lax.dynamic_slice_in_dim(packs[:, 0, :], starts[0] + chip * ada_cols, ada_cols, axis=1)
    res_ada = _adam_ada_call(jnp.pad(c_all, ((0, 8), (0, 0))), jnp.pad(d_all, ((0, 8), (0, 0))),
                             w_ada[0], m_w_ada[0], v_w_ada[0])
    vectors = [(b_ada, m_b_ada, v_b_ada), (norm_w, m_norm_w, v_norm_w), (q_lora_norm, m_q_lora_norm, v_q_lora_norm),
               (kv_lora_norm, m_kv_lora_norm, v_kv_lora_norm), (q_head_norm, m_q_head_norm, v_q_head_norm),
               (k_head_norm, m_k_head_norm, v_k_head_norm)]
    vec_out = _adam_vectors_call(packs, starts[:len(vectors)], vectors)

    def ordered(kind):
        big = lambda name: res[name][kind][None]
        return [res_ada[kind][None], vec_out[kind][0], vec_out[kind][1], big("adam_w_in"), vec_out[kind][2],
                big("adam_w_uq"), vec_out[kind][3], big("adam_w_ukv"), vec_out[kind][4], vec_out[kind][5],
                big("adam_w_out")]

    return (loss, grad_x[None], *ordered(0), *ordered(1), *ordered(2), *ordered(3))


def _local_step(x2, ada, positions, tgt, norm_w, w_in_shards, q_lora_norm, w_uq_full,
                kv_lora_norm, w_ukv_full, q_head_norm, k_head_norm, w_out_full):
    in_shard = w_in_shards.shape[2]
    ckv_tail = C_GM - 3 * in_shard
    assert 0 <= ckv_tail and ckv_tail + MLA_ROPE_DIM + MLA_WIDTH == in_shard
    last = w_in_shards[3]
    w_in_bf = jnp.concatenate(
        [w_in_shards[0], w_in_shards[1], w_in_shards[2], last[:, :ckv_tail], last[:, ckv_tail + MLA_ROPE_DIM:],
         _rope_to_head_lanes(last[:, ckv_tail:ckv_tail + MLA_ROPE_DIM])], axis=1).astype(BF16)
    w_uq_bf = _to_head_lanes(w_uq_full.reshape(Q_LORA_RANK, MLA_HEADS, MLA_QK_DIM)).reshape(
        Q_LORA_RANK, MLA_PAD_WIDTH).astype(BF16)
    w_ukv_heads = w_ukv_full.reshape(KV_LORA_RANK, MLA_HEADS, 2 * MLA_NOPE_DIM)
    w_uk_bf = _nope_to_head_lanes(w_ukv_heads[:, :, :MLA_NOPE_DIM]).reshape(KV_LORA_RANK, MLA_PAD_WIDTH).astype(BF16)
    w_uv_bf = w_ukv_heads[:, :, MLA_NOPE_DIM:].reshape(KV_LORA_RANK, MLA_WIDTH).astype(BF16)
    w_out_bf = w_out_full.astype(BF16)
    qhn_pad, khn_pad = _to_head_lanes(q_head_norm), _to_head_lanes(k_head_norm)
    cos_t, sin_t = _rope_tables(positions)

    hb, q_sb, k_sb, v_sb, g_sb, c_q, c_kv, g_mla, k_rope = _pre_call(x2, ada, norm_w, w_in_bf)
    q_m, k_m, v_m, cqn, ckvn, q0, k0 = _mla_prep_call(
        c_q, c_kv, k_rope, cos_t, sin_t, q_lora_norm, kv_lora_norm, qhn_pad, khn_pad,
        w_uq_bf, w_uk_bf, w_uv_bf)
    o_sb, r_sb, kstart = _sb_fwd_call(q_sb, k_sb, v_sb)
    o_mla, lse = _mla_fwd_call(q_m, k_m, v_m)
    do_sb, do_mla, dg_sb, dg_mla, dy, g_w_out, d_gate, sq = _out_call(
        o_sb, g_sb, o_mla, g_mla, x2, tgt, ada, w_out_bf)

    dq_sb, dk_sb, dv_sb = _sb_bwd_call(kstart, q_sb, k_sb, v_sb, do_sb, r_sb)
    dq_m, dk_m, dv_m = _mla_bwd_call(q_m, k_m, v_m, do_mla, o_mla, lse)
    (d_cq, d_ckv, d_kr, g_wuq_pad, g_wuk_pad, g_wuv, g_qln, g_kvln, g_qhn, g_khn) = _mla_prep_bwd_call(
        dq_m, dk_m, dv_m, q0, k0, cqn, ckvn, c_q, c_kv, cos_t, sin_t,
        q_lora_norm, kv_lora_norm, qhn_pad, khn_pad, w_uq_bf, w_uk_bf, w_uv_bf)
    grad_x, g_win_pad, d_shift, d_scale, g_norm_w = _dh_call(
        [dq_sb, dk_sb, dv_sb, dg_sb, d_cq, d_ckv, dg_mla, d_kr], hb, x2, dy, ada, norm_w, w_in_bf)

    g_kr = g_win_pad[:, C_KR:]
    g_last = jnp.concatenate([g_win_pad[:, 3 * in_shard:C_GM], g_kr[:, :ROPE_HALF],
                              g_kr[:, HALF_LANES:HALF_LANES + ROPE_HALF], g_win_pad[:, C_GM:C_KR]], axis=1)
    g_w_in = jnp.stack([g_win_pad[:, j * in_shard:(j + 1) * in_shard] for j in range(3)] + [g_last])
    g_w_uq = _from_head_lanes(g_wuq_pad.reshape(Q_LORA_RANK, MLA_HEADS, HEAD_PAD)).reshape(Q_LORA_RANK, -1)
    g_w_ukv = jnp.concatenate(
        [_from_head_lanes(g_wuk_pad.reshape(KV_LORA_RANK, MLA_HEADS, HEAD_PAD))[:, :, :MLA_NOPE_DIM],
         g_wuv.reshape(KV_LORA_RANK, MLA_HEADS, MLA_NOPE_DIM)], axis=2).reshape(KV_LORA_RANK, -1)
    d_ada = jnp.concatenate([d_shift, d_scale, d_gate], axis=1)
    return (jnp.sum(sq), grad_x, g_w_in, g_w_uq, g_w_ukv, g_w_out, d_ada, g_norm_w, g_qln, g_kvln,
            _from_head_lanes(g_qhn), _from_head_lanes(g_khn))
```

```python
import math

import jax
import jax.numpy as jnp
from jax import lax
from jax.experimental import pallas as pl
from jax.experimental.pallas import tpu as pltpu

F32 = jnp.float32
BF16 = jnp.bfloat16
I32 = jnp.int32

D_MODEL = 1024
SB_HEADS = 8
SB_WIDTH = 512
MLA_HEADS = 8
MLA_QK_DIM = 96
MLA_NOPE_DIM = 64
MLA_ROPE_DIM = 32
MLA_WIDTH = 512
Q_LORA_RANK = 384
KV_LORA_RANK = 256
ROPE_THETA = 10000.0
EPS = 1e-6
LANES = 128
HALF_LANES = LANES // 2
HEAD_PAD = 128
MLA_PAD_WIDTH = MLA_HEADS * HEAD_PAD

C_Q, C_K, C_V, C_G = 0, 512, 1024, 1536
C_CQ, C_CKV, C_GM, C_KR = 2048, 2432, 2688, 3200
IN_COLS_PAD = 3328

ADAM_LR = 0.001
ADAM_B1 = 0.9
ADAM_B2 = 0.999
ADAM_EPS = 1e-08
ADAM_WD = 0.01
ADAM_STEP = 10

SB_SCALE = 0.125
SB_GROUP = 4
MLA_SCALE = 1.0 / math.sqrt(MLA_QK_DIM)
LN2 = math.log(2.0)
MLA_SCALE_LOG2 = MLA_SCALE / LN2
MLA_BQ = 1024
MLA_BWD_BQ = 1024
MLA_BK = 1024
MLA_BWD_BK = 1024
MLA_STEP = 512
MLA_BWD_STEP = 256
SB_DEAD = -104.0
MASK_NEG = -1e30

VMEM_LIMIT = 56 * 1024 * 1024
MESH = pl.DeviceIdType.MESH


def _dot(a, b):
    return jnp.dot(a, b, preferred_element_type=F32)


def _dot_nt(a, b):
    return lax.dot_general(a, b, (((1,), (1,)), ((), ())), preferred_element_type=F32)


def _dot_tn(a, b):
    return lax.dot_general(a, b, (((0,), (0,)), ((), ())), preferred_element_type=F32)


def _sigmoid(x):
    return 1.0 / (1.0 + jnp.exp(-x))


def _split_dot(a, m):
    hi = a.astype(BF16)
    lo = (a - hi.astype(F32)).astype(BF16)
    return _dot(hi, m) + _dot(lo, m)


def _params(sem, vmem=None):
    return pltpu.CompilerParams(dimension_semantics=sem, vmem_limit_bytes=vmem)


def _row_tile(s, want):
    return min(want, s)


def _hbm_spec():
    return pl.BlockSpec(memory_space=pltpu.HBM)


def _allgather_rows_call(row):
    def body(in_ref, out_ref, send_sems, recv_sems, loc_sem):
        x, y, c = lax.axis_index("x"), lax.axis_index("y"), lax.axis_index("c")
        flips = [(fx, fy, fc) for fx in (0, 1) for fy in (0, 1) for fc in (0, 1)][1:]

        def peer(r):
            fx, fy, fc = flips[r]
            return ((1 - x) if fx else x, (1 - y) if fy else y, (1 - c) if fc else c)

        def copy(r, slot):
            return pltpu.make_async_remote_copy(
                src_ref=in_ref, dst_ref=out_ref.at[slot], send_sem=send_sems.at[r], recv_sem=recv_sems.at[r],
                device_id=peer(r), device_id_type=MESH)

        local = pltpu.make_async_copy(in_ref, out_ref.at[4 * x + 2 * y + c], loc_sem)
        local.start()
        sends = [copy(r, 4 * x + 2 * y + c) for r in range(7)]
        for cp in sends:
            cp.start()
        for r in range(7):
            px, py, pc = peer(r)
            copy(r, 4 * px + 2 * py + pc).wait_recv()
        for cp in sends:
            cp.wait_send()
        local.wait()

    return pl.pallas_call(
        body, name="gather_rows",
        out_shape=jax.ShapeDtypeStruct((8,) + row.shape, row.dtype),
        in_specs=[_hbm_spec()], out_specs=_hbm_spec(),
        scratch_shapes=[pltpu.SemaphoreType.DMA((7,)), pltpu.SemaphoreType.DMA((7,)), pltpu.SemaphoreType.DMA],
    )(row)


def _gather_call(shards, split):
    n = len(shards)
    halves = [s.shape[1] // 2 for s in shards]

    def body(*refs):
        ins, outs = refs[:n], refs[n:2 * n]
        ici_send, ici_recv, d2d_send, d2d_recv, loc_sems = refs[2 * n:]
        x, y, c = lax.axis_index("x"), lax.axis_index("y"), lax.axis_index("c")
        me = 2 * x + y
        peers = [(1 - x, y), (x, 1 - y), (1 - x, 1 - y)]

        def rows(a, which):
            return pl.ds(pl.multiple_of(which * halves[a], 16), halves[a])

        def ici(a, j, slot):
            px, py = peers[j]
            src, dst = ins[a].at[0], outs[a].at[slot]
            if split[a]:
                src, dst = src.at[rows(a, c)], dst.at[rows(a, c)]
            return pltpu.make_async_remote_copy(
                src_ref=src, dst_ref=dst,
                send_sem=ici_send.at[3 * a + j], recv_sem=ici_recv.at[3 * a + j],
                device_id=(px, py, c), device_id_type=MESH)

        def d2d(a, j, which):
            px, py = peers[j]
            piece = outs[a].at[2 * px + py, rows(a, which)]
            return pltpu.make_async_remote_copy(
                src_ref=piece, dst_ref=piece,
                send_sem=d2d_send.at[3 * a + j], recv_sem=d2d_recv.at[3 * a + j],
                device_id=(x, y, 1 - c), device_id_type=MESH)

        local = [pltpu.make_async_copy(ins[a].at[0], outs[a].at[me], loc_sems.at[a]) for a in range(n)]
        for cp in local:
            cp.start()
        sends = [ici(a, j, me) for a in range(n) for j in range(3)]
        for cp in sends:
            cp.start()
        for a in range(n):
            for j in range(3):
                px, py = peers[j]
                ici(a, j, 2 * px + py).wait_recv()
                if split[a]:
                    cp = d2d(a, j, c)
                    cp.start()
                    sends.append(cp)
        for a in range(n):
            for j in range(3):
                if split[a]:
                    d2d(a, j, 1 - c).wait_recv()
        for cp in sends:
            cp.wait_send()
        for cp in local:
            cp.wait()

    return pl.pallas_call(
        body, name="gather_weights",
        out_shape=[jax.ShapeDtypeStruct((4,) + s.shape[1:], s.dtype) for s in shards],
        in_specs=[_hbm_spec() for _ in shards],
        out_specs=[_hbm_spec() for _ in shards],
        scratch_shapes=[pltpu.SemaphoreType.DMA((3 * n,)), pltpu.SemaphoreType.DMA((3 * n,)),
                        pltpu.SemaphoreType.DMA((3 * n,)), pltpu.SemaphoreType.DMA((3 * n,)),
                        pltpu.SemaphoreType.DMA((n,))],
    )(*shards)


def _exchange_call(grads, small):
    n = len(grads)

    def body(*refs):
        g_in, small_in = refs[:n], refs[n]
        own, sib, packs = refs[n + 1:2 * n + 1], refs[2 * n + 1:3 * n + 1], refs[3 * n + 1]
        ici_send, ici_recv, d2d_send, d2d_recv, sm_send, sm_recv, loc_sems = refs[3 * n + 2:]
        x, y, c = lax.axis_index("x"), lax.axis_index("y"), lax.axis_index("c")
        me = 2 * x + y
        me8 = 4 * x + 2 * y + c
        sibling = (x, y, 1 - c)
        peers = [(1 - x, y), (x, 1 - y), (1 - x, 1 - y)]
        flips = [(fx, fy, fc) for fx in (0, 1) for fy in (0, 1) for fc in (0, 1)][1:]

        def ici(a, j, src_slot, dst_slot):
            px, py = peers[j]
            return pltpu.make_async_remote_copy(
                src_ref=g_in[a].at[src_slot], dst_ref=own[a].at[dst_slot],
                send_sem=ici_send.at[3 * a + j], recv_sem=ici_recv.at[3 * a + j],
                device_id=(px, py, c), device_id_type=MESH)

        def d2d(a, rel, chip, src):
            return pltpu.make_async_remote_copy(
                src_ref=src, dst_ref=sib[a].at[chip],
                send_sem=d2d_send.at[4 * a + rel], recv_sem=d2d_recv.at[4 * a + rel],
                device_id=sibling, device_id_type=MESH)

        def flipped(r):
            fx, fy, fc = flips[r]
            return ((1 - x) if fx else x, (1 - y) if fy else y, (1 - c) if fc else c)

        def sm(r, slot):
            return pltpu.make_async_remote_copy(
                src_ref=small_in, dst_ref=packs.at[slot],
                send_sem=sm_send.at[r], recv_sem=sm_recv.at[r],
                device_id=flipped(r), device_id_type=MESH)

        def peer8(r):
            px, py, pc = flipped(r)
            return 4 * px + 2 * py + pc

        local = [pltpu.make_async_copy(g_in[a].at[me], own[a].at[me], loc_sems.at[a]) for a in range(n)]
        local.append(pltpu.make_async_copy(small_in, packs.at[me8], loc_sems.at[n]))
        for cp in local:
            cp.start()
        sends = []
        for r in range(7):
            sends.append(sm(r, me8))
        for a in range(n):
            for j in range(3):
                px, py = peers[j]
                sends.append(ici(a, j, 2 * px + py, me))
        for cp in sends:
            cp.start()
        for a in range(n):
            cp = d2d(a, 0, me, g_in[a].at[me])
            cp.start()
            sends.append(cp)
        for a in range(n):
            for j in range(3):
                px, py = peers[j]
                ici(a, j, me, 2 * px + py).wait_recv()
                cp = d2d(a, 1 + j, 2 * px + py, own[a].at[2 * px + py])
                cp.start()
                sends.append(cp)
        for a in range(n):
            d2d(a, 0, me, g_in[a].at[me]).wait_recv()
            for j in range(3):
                px, py = peers[j]
                d2d(a, 1 + j, 2 * px + py, g_in[a].at[me]).wait_recv()
        for r in range(7):
            sm(r, peer8(r)).wait_recv()
        for cp in sends:
            cp.wait_send()
        for cp in local:
            cp.wait()

    out_shape = ([jax.ShapeDtypeStruct(g.shape, g.dtype) for g in grads] * 2
                 + [jax.ShapeDtypeStruct((8,) + small.shape, small.dtype)])
    res = pl.pallas_call(
        body, name="exchange_grads",
        out_shape=out_shape,
        in_specs=[_hbm_spec() for _ in range(n + 1)],
        out_specs=[_hbm_spec() for _ in range(2 * n + 1)],
        scratch_shapes=[pltpu.SemaphoreType.DMA((3 * n,)), pltpu.SemaphoreType.DMA((3 * n,)),
                        pltpu.SemaphoreType.DMA((4 * n,)), pltpu.SemaphoreType.DMA((4 * n,)),
                        pltpu.SemaphoreType.DMA((7,)), pltpu.SemaphoreType.DMA((7,)),
                        pltpu.SemaphoreType.DMA((n + 1,))],
    )(*grads, small)
    return res[:n], res[n:2 * n], res[2 * n]


def _ada_call(c_all, w_ada_cols, b_ada_cols):
    def body(c_ref, w_ref, b_ref, o_ref):
        cc = c_ref[...]
        o_ref[...] = _dot((cc * _sigmoid(cc)).astype(BF16), w_ref[...].astype(BF16)) + b_ref[...]

    return pl.pallas_call(
        body, name="ada_fwd",
        out_shape=jax.ShapeDtypeStruct((c_all.shape[0], w_ada_cols.shape[1]), F32),
        compiler_params=pltpu.CompilerParams(vmem_limit_bytes=VMEM_LIMIT),
    )(c_all, w_ada_cols, b_ada_cols)


def _ada_part(j):
    return pl.BlockSpec((1, D_MODEL), lambda i: (0, j))


def _full(shape):
    return pl.BlockSpec(shape, lambda i: (0,) * len(shape))


def _rows(tm, width):
    return pl.BlockSpec((tm, width), lambda i: (i, 0))


def _pre_call(x, ada, norm_w, w_in_bf):
    s = x.shape[0]
    tm = _row_tile(s, 512)
    groups = [(C_Q, 512, BF16), (C_K, 512, BF16), (C_V, 512, BF16), (C_G, 512, F32),
              (C_CQ, Q_LORA_RANK, F32), (C_CKV, KV_LORA_RANK, F32), (C_GM, 512, F32), (C_KR, LANES, F32)]

    def body(x_ref, sh_ref, sc_ref, nw_ref, w_ref, hb_ref, *outs):
        xx = x_ref[...]
        r0 = lax.rsqrt(jnp.mean(xx * xx, axis=-1, keepdims=True) + EPS)
        h = (xx * r0 * nw_ref[...]) * (1.0 + sc_ref[...]) + sh_ref[...]
        hb = h.astype(BF16)
        hb_ref[...] = hb
        for (c0, width, dt), o_ref in zip(groups, outs):
            o_ref[...] = _dot(hb, w_ref[:, c0:c0 + width]).astype(dt)

    return pl.pallas_call(
        body, name="pre_proj",
        grid=(s // tm,),
        in_specs=[_rows(tm, D_MODEL), _ada_part(0), _ada_part(1), _full((1, D_MODEL)),
                  _full((D_MODEL, IN_COLS_PAD))],
        out_specs=[_rows(tm, D_MODEL)] + [_rows(tm, w) for _, w, _ in groups],
        out_shape=[jax.ShapeDtypeStruct((s, D_MODEL), BF16)]
        + [jax.ShapeDtypeStruct((s, w), dt) for _, w, dt in groups],
        compiler_params=_params(("arbitrary",), VMEM_LIMIT),
    )(x, ada, ada, norm_w, w_in_bf)


def _rope(t, cos_t, sin_t):
    return t * cos_t + pltpu.roll(t, HALF_LANES, 1) * sin_t


def _rope_adjoint(d, cos_t, sin_t):
    return d * cos_t + pltpu.roll(d * sin_t, HALF_LANES, 1)


def _mla_prep_call(c_q, c_kv, k_rope, cos_t, sin_t, q_lora_norm, kv_lora_norm, qhn_pad, khn_pad,
                   w_uq_bf, w_uk_bf, w_uv_bf):
    s = c_q.shape[0]
    tm = _row_tile(s, 512)

    def body(cq_ref, ckv_ref, kr_ref, cos_ref, sin_ref, qln_ref, kvln_ref, qhn_ref, khn_ref,
             wuq_ref, wuk_ref, wuv_ref, q_ref, k_ref, v_ref, cqn_ref, ckvn_ref, q0_ref, k0_ref):
        cq = cq_ref[...]
        cqn = (cq * lax.rsqrt(jnp.mean(cq * cq, axis=-1, keepdims=True) + EPS) * qln_ref[...]).astype(BF16)
        cqn_ref[...] = cqn
        ckv = ckv_ref[...]
        ckvn = (ckv * lax.rsqrt(jnp.mean(ckv * ckv, axis=-1, keepdims=True) + EPS) * kvln_ref[...]).astype(BF16)
        ckvn_ref[...] = ckvn
        v_ref[...] = _dot(ckvn, wuv_ref[...]).astype(BF16)
        q0_ref[...] = _dot(cqn, wuq_ref[...])
        k0_ref[...] = _dot(ckvn, wuk_ref[...])
        cos_t, sin_t = cos_ref[...], sin_ref[...]
        kr = kr_ref[...]
        heads = [slice(h * HEAD_PAD, (h + 1) * HEAD_PAD) for h in range(MLA_HEADS)]
        for cols in heads:
            k0_ref[:, cols] = k0_ref[:, cols] + kr

        def inv_rms(ref):
            sums = [jnp.sum(ref[:, cols] * ref[:, cols], axis=-1, keepdims=True) for cols in heads]
            return [lax.rsqrt(t * (1.0 / MLA_QK_DIM) + EPS) for t in sums]

        rqs, rks = inv_rms(q0_ref), inv_rms(k0_ref)
        for cols, rq, rk in zip(heads, rqs, rks):
            q_ref[:, cols] = (_rope(q0_ref[:, cols] * rq * qhn_ref[...], cos_t, sin_t) * MLA_SCALE_LOG2).astype(BF16)
            k_ref[:, cols] = _rope(k0_ref[:, cols] * rk * khn_ref[...], cos_t, sin_t).astype(BF16)

    return pl.pallas_call(
        body, name="mla_prep",
        grid=(s // tm,),
        in_specs=[_rows(tm, Q_LORA_RANK), _rows(tm, KV_LORA_RANK), _rows(tm, LANES),
                  _rows(tm, LANES), _rows(tm, LANES),
                  _full((1, Q_LORA_RANK)), _full((1, KV_LORA_RANK)), _full((1, LANES)), _full((1, LANES)),
                  _full((Q_LORA_RANK, MLA_PAD_WIDTH)), _full((KV_LORA_RANK, MLA_PAD_WIDTH)),
                  _full((KV_LORA_RANK, MLA_WIDTH))],
        out_specs=[_rows(tm, MLA_PAD_WIDTH), _rows(tm, MLA_PAD_WIDTH), _rows(tm, MLA_WIDTH),
                   _rows(tm, Q_LORA_RANK), _rows(tm, KV_LORA_RANK),
                   _rows(tm, MLA_PAD_WIDTH), _rows(tm, MLA_PAD_WIDTH)],
        out_shape=[jax.ShapeDtypeStruct((s, MLA_PAD_WIDTH), BF16), jax.ShapeDtypeStruct((s, MLA_PAD_WIDTH), BF16),
                   jax.ShapeDtypeStruct((s, MLA_WIDTH), BF16),
                   jax.ShapeDtypeStruct((s, Q_LORA_RANK), BF16), jax.ShapeDtypeStruct((s, KV_LORA_RANK), BF16),
                   jax.ShapeDtypeStruct((s, MLA_PAD_WIDTH), F32), jax.ShapeDtypeStruct((s, MLA_PAD_WIDTH), F32)],
        compiler_params=_params(("arbitrary",), VMEM_LIMIT),
    )(c_q, c_kv, k_rope, cos_t, sin_t, q_lora_norm, kv_lora_norm, qhn_pad, khn_pad,
      w_uq_bf, w_uk_bf, w_uv_bf)


def _log_sigmoid_pair(z):
    ls = jnp.minimum(z, 0.0) - jnp.log(1.0 + jnp.exp(-jnp.abs(z)))
    return ls, ls - z


def _pair(hh):
    return slice((hh // 2) * LANES, (hh // 2 + 1) * LANES)


def _sb_fwd_call(q, k, v):
    s = q.shape[0]
    bq = _row_tile(s, 256)
    nq = s // bq
    nh = SB_GROUP
    width = nh * HALF_LANES

    def body(q_ref, k_ref, v_ref, o_ref, r_ref, ks_ref):
        hp, i = pl.program_id(0), pl.program_id(1)
        lane = lax.broadcasted_iota(I32, (bq, LANES), 1)
        row = lax.broadcasted_iota(I32, (bq, bq), 0)
        col = lax.broadcasted_iota(I32, (bq, bq), 1)
        strict = col < row
        later = jnp.where(row > col, 1.0, 0.0).astype(BF16)
        masks = [_head_mask(lane, hh).astype(BF16) for hh in range(2)]
        qms = [q_ref[:, _pair(hh)] * jnp.asarray(SB_SCALE, BF16) * masks[hh % 2] for hh in range(nh)]

        def walk(blocks, state):
            chains = [(kb, diagonal, hh) for kb, diagonal in blocks for hh in range(nh)]
            keys = lambda kb: pl.ds(pl.multiple_of(kb * bq, bq), bq)
            zs = [_dot_nt(qms[hh], k_ref[keys(kb), _pair(hh)]) for kb, _, hh in chains]
            pairs = []
            for z, (_, diagonal, _) in zip(zs, chains):
                ls, lk = _log_sigmoid_pair(z)
                pairs.append((ls, jnp.where(strict, lk, 0.0) if diagonal else lk))
            sums = [_split_dot(lk, later) for _, lk in pairs]
            runs = [st[0] for st in state]
            ws = []
            for (ls, lk), after, (_, diagonal, hh) in zip(pairs, sums, chains):
                w = jnp.exp(ls + (after + runs[hh]))
                ws.append((jnp.where(strict, w, 0.0) if diagonal else w).astype(BF16))
                runs[hh] = runs[hh] + jnp.sum(lk, axis=1, keepdims=True)
            accs = [st[1] for st in state]
            for w, (kb, _, hh) in zip(ws, chains):
                accs[hh] = accs[hh] + _dot(w, v_ref[keys(kb), _pair(hh)])
            return tuple(zip(runs, accs))

        def alive(state):
            top = jnp.max(state[0][0])
            for st in state[1:]:
                top = jnp.maximum(top, jnp.max(st[0]))
            return (top > SB_DEAD).astype(I32)

        def finish(state, first):
            ks_ref[hp, i] = first
            for pair in range(nh // 2):
                o_ref[:, _pair(2 * pair)] = jnp.where(lane < HALF_LANES, state[2 * pair][1], state[2 * pair + 1][1])
                r_ref[:, _pair(2 * pair)] = jnp.where(lane < HALF_LANES, state[2 * pair][0], state[2 * pair + 1][0])

        zero = ((jnp.zeros((bq, 1), F32), jnp.zeros((bq, LANES), F32)),) * nh

        @pl.when(i == 0)
        def _():
            finish(walk([(0, True)], zero), 0)

        @pl.when(i > 0)
        def _():
            state = walk([(i, True), (i - 1, False)], zero)

            def cond(carry):
                return jnp.logical_and(carry[0] >= 0, carry[1] > 0)

            def step(carry):
                state = walk([(carry[0], False)], carry[2])
                return carry[0] - 1, alive(state), state

            kb, _, state = lax.while_loop(cond, step, (i - 2, alive(state), state))
            finish(state, kb + 1)

    return pl.pallas_call(
        body, name="sb_fwd",
        grid=(SB_HEADS // nh, nq),
        in_specs=[pl.BlockSpec((bq, width), lambda h, i: (i, h)),
                  pl.BlockSpec((s, width), lambda h, i: (0, h)),
                  pl.BlockSpec((s, width), lambda h, i: (0, h))],
        out_specs=[pl.BlockSpec((bq, width), lambda h, i: (i, h)),
                   pl.BlockSpec((bq, width), lambda h, i: (i, h)),
                   pl.BlockSpec(memory_space=pltpu.SMEM)],
        out_shape=[jax.ShapeDtypeStruct((s, SB_WIDTH), F32), jax.ShapeDtypeStruct((s, SB_WIDTH), F32),
                   jax.ShapeDtypeStruct((SB_HEADS // nh, nq), I32)],
        compiler_params=_params(("arbitrary", "arbitrary"), VMEM_LIMIT),
    )(q, k, v)


def _mla_fwd_call(q, k, v):
    s = q.shape[0]
    bq = _row_tile(s, MLA_BQ)
    bk = _row_tile(s, MLA_BK)
    nq = s // bq
    assert bk % bq == 0
    step = min(bk // 2, MLA_STEP)
    nsub = bk // step
    assert nsub % 2 == 0

    def body(q_ref, k_ref, v_ref, o_ref, lse_ref, p_ref, s_ref):
        i = pl.program_id(1)
        lane = lax.broadcasted_iota(I32, (bq, LANES), 1)
        row = lax.broadcasted_iota(I32, (step, bq), 1)
        col = lax.broadcasted_iota(I32, (step, bq), 0)
        n_full = (i * bq) // bk

        def keys(g):
            return pl.ds(pl.multiple_of(g * step, step), step)

        def join(left, right, qlo):
            return right if qlo == 0 else jnp.concatenate([left[:, :qlo], right], axis=1)

        def put_scores(g, slot, qlo=0):
            for hh in range(2):
                cols = slice(hh * HEAD_PAD, (hh + 1) * HEAD_PAD)
                s_ref[slot, hh, :, qlo:] = _dot_nt(k_ref[keys(g), cols], q_ref[qlo:, cols])

        def add_pv(carry, g, slot, qlo=0):
            vblk = v_ref[keys(g), :]
            out = []
            for hh, (m, l, acc, alpha) in enumerate(carry):
                upd = alpha[:, qlo:] * acc[:, qlo:] + _dot_tn(vblk, p_ref[slot, hh, :, qlo:])
                out.append((m, l, join(acc, upd, qlo), alpha))
            return tuple(out)

        def substep(g, slot, carry, masked, prefetch, qlo=0, next_qlo=0, prev_qlo=0):
            if prefetch:
                put_scores(g + 1, 1 - slot, next_qlo)
            carry = add_pv(carry, jnp.maximum(g - 1, 0), 1 - slot, prev_qlo)
            new = []
            for hh in range(2):
                m, l, acc, _ = carry[hh]
                sc = s_ref[slot, hh, :, qlo:]
                if masked:
                    sc = jnp.where(col[:, qlo:] + g * step <= row[:, qlo:] + i * bq, sc, MASK_NEG)
                m_new = jnp.maximum(m[:, qlo:], jnp.max(sc, axis=0, keepdims=True))
                p = jnp.exp2(sc - m_new)
                alpha = jnp.exp2(m[:, qlo:] - m_new)
                l_new = alpha * l[:, qlo:] + jnp.sum(p, axis=0, keepdims=True)
                p_ref[slot, hh, :, qlo:] = p.astype(BF16)
                new.append((join(m, m_new, qlo), join(l, l_new, qlo), acc, join(jnp.ones_like(m), alpha, qlo)))
            return tuple(new)

        def first_query(t, masked):
            return t * step if (masked and bk == bq and 0 <= t < nsub) else 0

        def chunk(kb, carry, masked):
            for t in range(nsub):
                last = masked and t == nsub - 1
                carry = substep(nsub * kb + t, t % 2, carry, masked, not last, first_query(t, masked),
                                first_query(t + 1, masked), first_query(t - 1, masked))
            return carry

        p_ref[1] = jnp.zeros_like(p_ref[1])
        put_scores(0, 0)
        one = (jnp.full((1, bq), MASK_NEG, F32), jnp.zeros((1, bq), F32), jnp.zeros((LANES, bq), F32),
               jnp.ones((1, bq), F32))
        carry = lax.fori_loop(0, n_full, lambda kb, cr: chunk(kb, cr, False), (one, one))
        carry = chunk(n_full, carry, True)
        (m0, l0, a0, _), (m1, l1, a1, _) = add_pv(carry, nsub * n_full + nsub - 1, 1, first_query(nsub - 1, True))
        o_ref[...] = jnp.where(lane < HALF_LANES, (a0 / l0).T, (a1 / l1).T)
        sub = lax.broadcasted_iota(I32, (8, bq), 0)
        lse_ref[...] = jnp.where(sub == 0, m0 + jnp.log2(l0), jnp.where(sub == 1, m1 + jnp.log2(l1), 0.0))

    return pl.pallas_call(
        body, name="mla_fwd",
        grid=(4, nq),
        in_specs=[pl.BlockSpec((bq, 2 * HEAD_PAD), lambda h, i: (i, h)),
                  pl.BlockSpec((s, 2 * HEAD_PAD), lambda h, i: (0, h)),
                  pl.BlockSpec((s, LANES), lambda h, i: (0, h))],
        out_specs=[pl.BlockSpec((bq, LANES), lambda h, i: (i, h)),
                   pl.BlockSpec((None, 8, bq), lambda h, i: (h, 0, i))],
        out_shape=[jax.ShapeDtypeStruct((s, MLA_WIDTH), F32), jax.ShapeDtypeStruct((4, 8, s), F32)],
        scratch_shapes=[pltpu.VMEM((2, 2, step, bq), BF16), pltpu.VMEM((2, 2, step, bq), F32)],
        compiler_params=_params(("arbitrary", "arbitrary"), VMEM_LIMIT),
    )(q, k, v)


def _out_call(o_sb, g_sb, o_mla, g_mla, x, target, ada, w_out_bf):
    s = x.shape[0]
    tm = _row_tile(s, 512)

    def body(osb_ref, gsb_ref, oml_ref, gml_ref, x_ref, t_ref, gate_ref, w_ref,
             dosb_ref, doml_ref, dgsb_ref, dgml_ref, dy_ref, gw_ref, dgate_ref, sq_ref):
        @pl.when(pl.program_id(0) == 0)
        def _():
            gw_ref[...] = jnp.zeros_like(gw_ref)
            dgate_ref[...] = jnp.zeros_like(dgate_ref)
            sq_ref[...] = jnp.zeros_like(sq_ref)

        g_s, g_m = gsb_ref[...], gml_ref[...]
        sig_s, sig_m = _sigmoid(g_s), _sigmoid(g_m)
        silu_s, silu_m = g_s * sig_s, g_m * sig_m
        o_s, o_m = osb_ref[...], oml_ref[...]
        mixed = jnp.concatenate([o_s * silu_s, o_m * silu_m], axis=1).astype(BF16)
        u = _dot(mixed, w_ref[...])
        gate_v = gate_ref[...]
        err = x_ref[...] + gate_v * u - t_ref[...]
        sq_ref[...] += jnp.sum(err * err, axis=0, keepdims=True)
        dy = err * (1.0 / D_MODEL)
        dy_ref[...] = dy
        dgate_ref[...] += jnp.sum(dy * u, axis=0, keepdims=True)
        du = (dy * gate_v).astype(BF16)
        gw_ref[...] += _dot_tn(mixed, du)
        dmix = _dot_nt(du, w_ref[...])
        dm_s, dm_m = dmix[:, :SB_WIDTH], dmix[:, SB_WIDTH:]
        dosb_ref[...] = (dm_s * silu_s).astype(BF16)
        doml_ref[...] = (dm_m * silu_m).astype(BF16)
        dgsb_ref[...] = (dm_s * o_s * (sig_s * (1.0 + g_s * (1.0 - sig_s)))).astype(BF16)
        dgml_ref[...] = (dm_m * o_m * (sig_m * (1.0 + g_m * (1.0 - sig_m)))).astype(BF16)

    return pl.pallas_call(
        body, name="out_proj_loss",
        grid=(s // tm,),
        in_specs=[_rows(tm, 512), _rows(tm, 512), _rows(tm, 512), _rows(tm, 512),
                  _rows(tm, D_MODEL), _rows(tm, D_MODEL), _ada_part(2), _full((D_MODEL, D_MODEL))],
        out_specs=[_rows(tm, 512), _rows(tm, 512), _rows(tm, 512), _rows(tm, 512), _rows(tm, D_MODEL),
                   _full((D_MODEL, D_MODEL)), _full((1, D_MODEL)), _full((1, D_MODEL))],
        out_shape=[jax.ShapeDtypeStruct((s, 512), BF16)] * 4
        + [jax.ShapeDtypeStruct((s, D_MODEL), F32), jax.ShapeDtypeStruct((D_MODEL, D_MODEL), F32),
           jax.ShapeDtypeStruct((1, D_MODEL), F32), jax.ShapeDtypeStruct((1, D_MODEL), F32)],
        compiler_params=_params(("arbitrary",), VMEM_LIMIT),
    )(o_sb, g_sb, o_mla, g_mla, x, target, ada, w_out_bf)


def _head_mask(lane, hh):
    return jnp.where((lane >= HALF_LANES) if hh else (lane < HALF_LANES), 1.0, 0.0)


def _pick_lane(packed, lane, which):
    return jnp.sum(jnp.where(lane == which, packed, 0.0), axis=1, keepdims=True)


def _sb_bwd_call(kstart, q, k, v, do, rfin):
    s = q.shape[0]
    bq = _row_tile(s, 256)
    nq = s // bq
    nh = SB_GROUP
    width = nh * HALF_LANES

    def body(ks_ref, q_ref, k_ref, v_ref, do_ref, r_ref, dq_ref, dk_ref, dv_ref):
        hp, i = pl.program_id(0), pl.program_id(1)

        @pl.when(i == 0)
        def _():
            dk_ref[...] = jnp.zeros_like(dk_ref)
            dv_ref[...] = jnp.zeros_like(dv_ref)

        lane = lax.broadcasted_iota(I32, (bq, LANES), 1)
        row = lax.broadcasted_iota(I32, (bq, bq), 0)
        col = lax.broadcasted_iota(I32, (bq, bq), 1)
        upto = jnp.where(row <= col, 1.0, 0.0).astype(BF16)
        before = jnp.where(row < col, 1.0, 0.0).astype(BF16)
        masks = [_head_mask(lane, hh).astype(BF16) for hh in range(2)]
        qms = [q_ref[:, _pair(hh)] * jnp.asarray(SB_SCALE, BF16) * masks[hh % 2] for hh in range(nh)]
        doms = [do_ref[:, _pair(hh)] * masks[hh % 2] for hh in range(nh)]
        totals = [_pick_lane(r_ref[:, _pair(hh)], lane, HALF_LANES * (hh % 2)) for hh in range(nh)]
        strict = col < row

        def walk(blocks, state):
            chains = [(kb, diagonal, hh) for kb, diagonal in blocks for hh in range(nh)]
            keys = lambda kb: pl.ds(pl.multiple_of(kb * bq, bq), bq)
            cut = lambda x, diagonal: jnp.where(strict, x, 0.0) if diagonal else x
            zs = [_dot_nt(qms[hh], k_ref[keys(kb), _pair(hh)]) for kb, _, hh in chains]
            dws = [_dot_nt(doms[hh], v_ref[keys(kb), _pair(hh)]) for kb, _, hh in chains]
            pairs = []
            for z, (_, diagonal, _) in zip(zs, chains):
                ls, lk = _log_sigmoid_pair(z)
                pairs.append((ls, cut(lk, diagonal)))
            incls = [_split_dot(lk, upto) for _, lk in pairs]
            pres = [st[0] for st in state]
            ws, gs = [], []
            for (ls, lk), incl, dw, (_, diagonal, hh) in zip(pairs, incls, dws, chains):
                w = cut(jnp.exp(ls + ((totals[hh] - pres[hh]) - incl)), diagonal)
                ws.append(w.astype(BF16))
                gs.append(w * dw)
                pres[hh] = pres[hh] + jnp.sum(lk, axis=1, keepdims=True)
            gsums = [_dot(g.astype(BF16), before) for g in gs]
            gpres = [st[1] for st in state]
            dzs = []
            for (ls, _), g, gsum, (_, diagonal, hh) in zip(pairs, gs, gsums, chains):
                dzs.append(cut(g - jnp.exp(ls) * (g + (gpres[hh] + gsum)), diagonal).astype(BF16))
                gpres[hh] = gpres[hh] + jnp.sum(g, axis=1, keepdims=True)
            dqs = [st[2] for st in state]
            dk_parts, dv_parts = [], []
            for dzb, w, (kb, _, hh) in zip(dzs, ws, chains):
                dk_parts.append(_dot_tn(dzb, qms[hh]))
                dv_parts.append(_dot_tn(w, doms[hh]))
                dqs[hh] = dqs[hh] + _dot(dzb, k_ref[keys(kb), _pair(hh)])
            for b, (kb, _) in enumerate(blocks):
                for pair in range(nh // 2):
                    c0 = b * nh + 2 * pair
                    dk_ref[keys(kb), _pair(2 * pair)] += dk_parts[c0] + dk_parts[c0 + 1]
                    dv_ref[keys(kb), _pair(2 * pair)] += dv_parts[c0] + dv_parts[c0 + 1]
            return tuple(zip(pres, gpres, dqs))

        def finish(state):
            for pair in range(nh // 2):
                both = jnp.where(lane < HALF_LANES, state[2 * pair][2], state[2 * pair + 1][2])
                dq_ref[:, _pair(2 * pair)] = (both * SB_SCALE).astype(BF16)

        zero = ((jnp.zeros((bq, 1), F32), jnp.zeros((bq, 1), F32), jnp.zeros((bq, LANES), F32)),) * nh

        @pl.when(i == 0)
        def _():
            finish(walk([(0, True)], zero))

        @pl.when(i > 0)
        def _():
            state = lax.fori_loop(ks_ref[hp, i], i - 1, lambda kb, st: walk([(kb, False)], st), zero)
            finish(walk([(i - 1, False), (i, True)], state))

    return pl.pallas_call(
        body, name="sb_bwd",
        grid_spec=pltpu.PrefetchScalarGridSpec(
            num_scalar_prefetch=1, grid=(SB_HEADS // nh, nq),
            in_specs=[pl.BlockSpec((bq, width), lambda h, i, ks: (i, h)),
                      pl.BlockSpec((s, width), lambda h, i, ks: (0, h), pipeline_mode=pl.Buffered(1)),
                      pl.BlockSpec((s, width), lambda h, i, ks: (0, h), pipeline_mode=pl.Buffered(1)),
                      pl.BlockSpec((bq, width), lambda h, i, ks: (i, h)),
                      pl.BlockSpec((bq, width), lambda h, i, ks: (i, h))],
            out_specs=[pl.BlockSpec((bq, width), lambda h, i, ks: (i, h)),
                       pl.BlockSpec((s, width), lambda h, i, ks: (0, h), pipeline_mode=pl.Buffered(1)),
                       pl.BlockSpec((s, width), lambda h, i, ks: (0, h), pipeline_mode=pl.Buffered(1))]),
        out_shape=[jax.ShapeDtypeStruct((s, SB_WIDTH), BF16), jax.ShapeDtypeStruct((s, SB_WIDTH), F32),
                   jax.ShapeDtypeStruct((s, SB_WIDTH), F32)],
        compiler_params=_params(("arbitrary", "arbitrary"), VMEM_LIMIT),
    )(kstart, q, k, v, do, rfin)


def _mla_bwd_call(q, k, v, do, o, lse):
    s = q.shape[0]
    bq = _row_tile(s, MLA_BWD_BQ)
    bk = _row_tile(s, MLA_BWD_BK)
    nq = s // bq
    assert bk % bq == 0
    step = min(bk // 2, MLA_BWD_STEP)
    nsub = bk // step
    assert nsub % 2 == 0

    def body(q_ref, k_ref, v_ref, do_ref, o_ref, lse_ref, dq_ref, dk_ref, dv_ref, dom_ref, s_ref, dp_ref, pb_ref,
             ds_ref):
        i = pl.program_id(1)

        @pl.when(i == 0)
        def _():
            dk_ref[...] = jnp.zeros_like(dk_ref)
            dv_ref[...] = jnp.zeros_like(dv_ref)

        lane = lax.broadcasted_iota(I32, (bq, LANES), 1)
        row = lax.broadcasted_iota(I32, (step, bq), 1)
        col = lax.broadcasted_iota(I32, (step, bq), 0)
        n_full = (i * bq) // bk
        do2 = do_ref[...]
        prod = do2.astype(F32) * o_ref[...]
        ones = jnp.ones((8, LANES), BF16)
        deltas, lses = [], []
        for hh in range(2):
            head = _head_mask(lane, hh)
            dom_ref[hh] = do2 * head.astype(BF16)
            part = prod * head
            hi = part.astype(BF16)
            lo = (part - hi.astype(F32)).astype(BF16)
            deltas.append((_dot_nt(ones, hi) + _dot_nt(ones, lo))[0:1])
            lses.append(lse_ref[hh:hh + 1, :])

        def keys(g):
            return pl.ds(pl.multiple_of(g * step, step), step)

        def heads():
            return [(hh, slice(hh * HEAD_PAD, (hh + 1) * HEAD_PAD)) for hh in range(2)]

        def put_products(g, slot, qlo=0):
            vblk = v_ref[keys(g), :]
            for hh, cols in heads():
                s_ref[slot, hh, :, qlo:] = _dot_nt(k_ref[keys(g), cols], q_ref[qlo:, cols])
                dp_ref[slot, hh, :, qlo:] = _dot_nt(vblk, dom_ref[hh, qlo:, :])

        def add_grads(dqs, g, slot, qlo=0):
            rows = keys(g)
            new, dv_parts = [], []
            for hh, cols in heads():
                ds = ds_ref[slot, hh, :, qlo:]
                dk_ref[rows, cols] += _dot(ds, q_ref[qlo:, cols])
                dv_parts.append(_dot(pb_ref[slot, hh, :, qlo:], dom_ref[hh, qlo:, :]))
                upd = dqs[hh][:, qlo:] + _dot_tn(k_ref[rows, cols], ds)
                new.append(upd if qlo == 0 else jnp.concatenate([dqs[hh][:, :qlo], upd], axis=1))
            dv_ref[rows, :] += dv_parts[0] + dv_parts[1]
            return tuple(new)

        def substep(g, slot, dqs, masked, prefetch, qlo=0, next_qlo=0, prev_qlo=0):
            if prefetch:
                put_products(g + 1, 1 - slot, next_qlo)
            dqs = add_grads(dqs, jnp.maximum(g - 1, 0), 1 - slot, prev_qlo)
            for hh, _ in heads():
                p = jnp.exp2(s_ref[slot, hh, :, qlo:] - lses[hh][:, qlo:])
                if masked:
                    p = jnp.where(col[:, qlo:] + g * step <= row[:, qlo:] + i * bq, p, 0.0)
                ds_ref[slot, hh, :, qlo:] = (p * (dp_ref[slot, hh, :, qlo:] - deltas[hh][:, qlo:])).astype(BF16)
                pb_ref[slot, hh, :, qlo:] = p.astype(BF16)
            return dqs

        def first_query(t, masked):
            return t * step if (masked and bk == bq and 0 <= t < nsub) else 0

        def chunk(kb, dqs, masked):
            for t in range(nsub):
                last = masked and t == nsub - 1
                dqs = substep(nsub * kb + t, t % 2, dqs, masked, not last, first_query(t, masked),
                              first_query(t + 1, masked), first_query(t - 1, masked))
            return dqs

        ds_ref[1] = jnp.zeros_like(ds_ref[1])
        pb_ref[1] = jnp.zeros_like(pb_ref[1])
        put_products(0, 0)
        zero = jnp.zeros((HEAD_PAD, bq), F32)
        dqs = lax.fori_loop(0, n_full, lambda kb, dqs: chunk(kb, dqs, False), (zero, zero))
        dqs = chunk(n_full, dqs, True)
        dqs = add_grads(dqs, nsub * n_full + nsub - 1, 1, first_query(nsub - 1, True))
        dq_ref[:, :HEAD_PAD] = dqs[0].T * MLA_SCALE
        dq_ref[:, HEAD_PAD:] = dqs[1].T * MLA_SCALE

    return pl.pallas_call(
        body, name="mla_bwd",
        grid=(4, nq),
        in_specs=[pl.BlockSpec((bq, 2 * HEAD_PAD), lambda h, i: (i, h)),
                  pl.BlockSpec((s, 2 * HEAD_PAD), lambda h, i: (0, h)),
                  pl.BlockSpec((s, LANES), lambda h, i: (0, h)),
                  pl.BlockSpec((bq, LANES), lambda h, i: (i, h)),
                  pl.BlockSpec((bq, LANES), lambda h, i: (i, h)),
                  pl.BlockSpec((None, 8, bq), lambda h, i: (h, 0, i))],
        out_specs=[pl.BlockSpec((bq, 2 * HEAD_PAD), lambda h, i: (i, h)),
                   pl.BlockSpec((s, 2 * HEAD_PAD), lambda h, i: (0, h), pipeline_mode=pl.Buffered(1)),
                   pl.BlockSpec((s, LANES), lambda h, i: (0, h), pipeline_mode=pl.Buffered(1))],
        out_shape=[jax.ShapeDtypeStruct((s, MLA_PAD_WIDTH), F32), jax.ShapeDtypeStruct((s, MLA_PAD_WIDTH), F32),
                   jax.ShapeDtypeStruct((s, MLA_WIDTH), F32)],
        scratch_shapes=[pltpu.VMEM((2, bq, LANES), BF16),
                        pltpu.VMEM((2, 2, step, bq), F32), pltpu.VMEM((2, 2, step, bq), F32),
                        pltpu.VMEM((2, 2, step, bq), BF16), pltpu.VMEM((2, 2, step, bq), BF16)],
        compiler_params=_params(("arbitrary", "arbitrary"), VMEM_LIMIT),
    )(q, k, v, do, o, lse)


def _rms_bwd(d_out, inp, r, weight, n):
    normed = inp * r
    gw = d_out * weight
    d_in = r * (gw - normed * (jnp.sum(gw * normed, axis=-1, keepdims=True) * (1.0 / n)))
    return d_in, d_out * normed


def _mla_prep_bwd_call(dq, dk, dv, q0, k0, cqn, ckvn, c_q, c_kv, cos_t, sin_t,
                       q_lora_norm, kv_lora_norm, qhn_pad, khn_pad, w_uq_bf, w_uk_bf, w_uv_bf):
    s = dq.shape[0]
    tm = _row_tile(s, 512)

    def body(dq_ref, dk_ref, dv_ref, q0_ref, k0_ref, cqn_ref, ckvn_ref, cq_ref, ckv_ref,
             cos_ref, sin_ref, qln_ref, kvln_ref, qhn_ref, khn_ref, wuq_ref, wuk_ref, wuv_ref,
             dcq_ref, dckv_ref, dkr_ref, gwuq_ref, gwuk_ref, gwuv_ref, gqln_ref, gkvln_ref, gqhn_ref, gkhn_ref,
             dq0_ref, dk0_ref, tmp_ref):
        @pl.when(pl.program_id(0) == 0)
        def _():
            for ref in (gwuq_ref, gwuk_ref, gwuv_ref, gqln_ref, gkvln_ref, gqhn_ref, gkhn_ref):
                ref[...] = jnp.zeros_like(ref)

        cos_t, sin_t = cos_ref[...], sin_ref[...]
        lane = lax.broadcasted_iota(I32, (tm, LANES), 1)
        rope_lanes = jnp.logical_or(lane < ROPE_HALF,
                                    jnp.logical_and(lane >= HALF_LANES, lane < HALF_LANES + ROPE_HALF))
        heads = [slice(h * HEAD_PAD, (h + 1) * HEAD_PAD) for h in range(MLA_HEADS)]

        def head_norm_bwd(d_ref, x0_ref, w_ref, out_ref, scale):
            w = w_ref[...]
            inv = [lax.rsqrt(jnp.sum(x0_ref[:, cols] * x0_ref[:, cols], axis=-1, keepdims=True)
                             * (1.0 / MLA_QK_DIM) + EPS) for cols in heads]
            for cols in heads:
                tmp_ref[:, cols] = _rope_adjoint(d_ref[:, cols] * scale, cos_t, sin_t)
            dots = [jnp.sum(tmp_ref[:, cols] * w * (x0_ref[:, cols] * r), axis=-1, keepdims=True)
                    for cols, r in zip(heads, inv)]
            g_w = jnp.zeros((1, LANES), F32)
            rope_sum = jnp.zeros((tm, LANES), F32)
            for cols, r, dot in zip(heads, inv, dots):
                normed = x0_ref[:, cols] * r
                d_n = tmp_ref[:, cols]
                d_x0 = r * (d_n * w - normed * (dot * (1.0 / MLA_QK_DIM)))
                out_ref[:, cols] = d_x0.astype(BF16)
                g_w = g_w + jnp.sum(d_n * normed, axis=0, keepdims=True)
                rope_sum = rope_sum + jnp.where(rope_lanes, d_x0, 0.0)
            return g_w, rope_sum

        g_qhn, _ = head_norm_bwd(dq_ref, q0_ref, qhn_ref, dq0_ref, 1.0)
        g_khn, d_kr = head_norm_bwd(dk_ref, k0_ref, khn_ref, dk0_ref, LN2)
        cqn, ckvn = cqn_ref[...], ckvn_ref[...]
        d_q0b, d_k0b, dvb = dq0_ref[...], dk0_ref[...], dv_ref[...].astype(BF16)
        d_cqn = _dot_nt(d_q0b, wuq_ref[...])
        gwuq_ref[...] += _dot_tn(cqn, d_q0b)
        d_ckvn = _dot_nt(d_k0b, wuk_ref[...]) + _dot_nt(dvb, wuv_ref[...])
        gwuk_ref[...] += _dot_tn(ckvn, d_k0b)
        gwuv_ref[...] += _dot_tn(ckvn, dvb)
        gqhn_ref[...] += g_qhn
        gkhn_ref[...] += g_khn
        dkr_ref[...] = d_kr.astype(BF16)
        cq = cq_ref[...]
        rcq = lax.rsqrt(jnp.mean(cq * cq, axis=-1, keepdims=True) + EPS)
        d_cq, gl = _rms_bwd(d_cqn, cq, rcq, qln_ref[...], Q_LORA_RANK)
        dcq_ref[...] = d_cq.astype(BF16)
        gqln_ref[...] += jnp.sum(gl, axis=0, keepdims=True)
        ckv = ckv_ref[...]
        rckv = lax.rsqrt(jnp.mean(ckv * ckv, axis=-1, keepdims=True) + EPS)
        d_ckv, gl = _rms_bwd(d_ckvn, ckv, rckv, kvln_ref[...], KV_LORA_RANK)
        dckv_ref[...] = d_ckv.astype(BF16)
        gkvln_ref[...] += jnp.sum(gl, axis=0, keepdims=True)

    return pl.pallas_call(
        body, name="mla_prep_bwd",
        grid=(s // tm,),
        in_specs=[_rows(tm, MLA_PAD_WIDTH), _rows(tm, MLA_PAD_WIDTH), _rows(tm, MLA_WIDTH),
                  _rows(tm, MLA_PAD_WIDTH), _rows(tm, MLA_PAD_WIDTH),
                  _rows(tm, Q_LORA_RANK), _rows(tm, KV_LORA_RANK), _rows(tm, Q_LORA_RANK), _rows(tm, KV_LORA_RANK),
                  _rows(tm, LANES), _rows(tm, LANES),
                  _full((1, Q_LORA_RANK)), _full((1, KV_LORA_RANK)), _full((1, LANES)), _full((1, LANES)),
                  _full((Q_LORA_RANK, MLA_PAD_WIDTH)), _full((KV_LORA_RANK, MLA_PAD_WIDTH)),
                  _full((KV_LORA_RANK, MLA_WIDTH))],
        out_specs=[_rows(tm, Q_LORA_RANK), _rows(tm, KV_LORA_RANK), _rows(tm, LANES),
                   _full((Q_LORA_RANK, MLA_PAD_WIDTH)), _full((KV_LORA_RANK, MLA_PAD_WIDTH)),
                   _full((KV_LORA_RANK, MLA_WIDTH)),
                   _full((1, Q_LORA_RANK)), _full((1, KV_LORA_RANK)), _full((1, LANES)), _full((1, LANES))],
        out_shape=[jax.ShapeDtypeStruct((s, Q_LORA_RANK), BF16), jax.ShapeDtypeStruct((s, KV_LORA_RANK), BF16),
                   jax.ShapeDtypeStruct((s, LANES), BF16),
                   jax.ShapeDtypeStruct((Q_LORA_RANK, MLA_PAD_WIDTH), F32),
                   jax.ShapeDtypeStruct((KV_LORA_RANK, MLA_PAD_WIDTH), F32),
                   jax.ShapeDtypeStruct((KV_LORA_RANK, MLA_WIDTH), F32),
                   jax.ShapeDtypeStruct((1, Q_LORA_RANK), F32), jax.ShapeDtypeStruct((1, KV_LORA_RANK), F32),
                   jax.ShapeDtypeStruct((1, LANES), F32), jax.ShapeDtypeStruct((1, LANES), F32)],
        scratch_shapes=[pltpu.VMEM((tm, MLA_PAD_WIDTH), BF16), pltpu.VMEM((tm, MLA_PAD_WIDTH), BF16),
                        pltpu.VMEM((tm, MLA_PAD_WIDTH), F32)],
        compiler_params=_params(("arbitrary",), VMEM_LIMIT),
    )(dq, dk, dv, q0, k0, cqn, ckvn, c_q, c_kv, cos_t, sin_t,
      q_lora_norm, kv_lora_norm, qhn_pad, khn_pad, w_uq_bf, w_uk_bf, w_uv_bf)


def _dh_call(pieces, hb, x, dy, ada, norm_w, w_in_bf):
    s = x.shape[0]
    tm = _row_tile(s, 512)
    widths = [p.shape[1] for p in pieces]
    offsets = [sum(widths[:j]) for j in range(len(widths))]
    assert offsets[-1] + widths[-1] == IN_COLS_PAD
    n = len(pieces)

    def body(*refs):
        p_refs = refs[:n]
        (hb_ref, x_ref, dy_ref, sh_ref, sc_ref, nw_ref, w_ref, gx_ref, gw_ref, dsh_ref, dsc_ref, gnw_ref,
         dp_ref) = refs[n:]

        @pl.when(pl.program_id(0) == 0)
        def _():
            gw_ref[...] = jnp.zeros_like(gw_ref)
            dsh_ref[...] = jnp.zeros_like(dsh_ref)
            dsc_ref[...] = jnp.zeros_like(dsc_ref)
            gnw_ref[...] = jnp.zeros_like(gnw_ref)

        for p_ref, c0, width in zip(p_refs, offsets, widths):
            dp_ref[:, c0:c0 + width] = p_ref[...].astype(BF16)
        gw_ref[...] += _dot_tn(hb_ref[...], dp_ref[...])
        dh = _dot_nt(dp_ref[...], w_ref[...])
        xx = x_ref[...]
        r0 = lax.rsqrt(jnp.mean(xx * xx, axis=-1, keepdims=True) + EPS)
        xn = xx * r0
        nw = nw_ref[...]
        dsh_ref[...] += jnp.sum(dh, axis=0, keepdims=True)
        dsc_ref[...] += jnp.sum(dh * (xn * nw), axis=0, keepdims=True)
        dn = dh * (1.0 + sc_ref[...])
        gnw_ref[...] += jnp.sum(dn * xn, axis=0, keepdims=True)
        dxn = dn * nw
        gx_ref[...] = dy_ref[...] + r0 * (dxn - xn * jnp.mean(dxn * xn, axis=-1, keepdims=True))

    return pl.pallas_call(
        body, name="in_proj_bwd",
        grid=(s // tm,),
        in_specs=[_rows(tm, w) for w in widths]
        + [_rows(tm, D_MODEL), _rows(tm, D_MODEL), _rows(tm, D_MODEL),
           _ada_part(0), _ada_part(1), _full((1, D_MODEL)),
           pl.BlockSpec((D_MODEL, IN_COLS_PAD), lambda i: (0, 0), pipeline_mode=pl.Buffered(1))],
        out_specs=[_rows(tm, D_MODEL),
                   pl.BlockSpec((D_MODEL, IN_COLS_PAD), lambda i: (0, 0), pipeline_mode=pl.Buffered(1)),
                   _full((1, D_MODEL)), _full((1, D_MODEL)), _full((1, D_MODEL))],
        out_shape=[jax.ShapeDtypeStruct((s, D_MODEL), F32), jax.ShapeDtypeStruct((D_MODEL, IN_COLS_PAD), F32),
                   jax.ShapeDtypeStruct((1, D_MODEL), F32), jax.ShapeDtypeStruct((1, D_MODEL), F32),
                   jax.ShapeDtypeStruct((1, D_MODEL), F32)],
        scratch_shapes=[pltpu.VMEM((tm, IN_COLS_PAD), BF16)],
        compiler_params=_params(("arbitrary",), VMEM_LIMIT),
    )(*pieces, hb, x, dy, ada, ada, norm_w, w_in_bf)


def _adamw(g, w, m, v):
    m = ADAM_B1 * m + (1.0 - ADAM_B1) * g
    v = ADAM_B2 * v + (1.0 - ADAM_B2) * (g * g)
    m_hat = m / (1.0 - ADAM_B1 ** ADAM_STEP)
    v_hat = v / (1.0 - ADAM_B2 ** ADAM_STEP)
    delta = -ADAM_LR * (m_hat / (jnp.sqrt(v_hat) + ADAM_EPS) + ADAM_WD * w)
    return delta, m, v


def _adam_shard_call(name, own, sib, w, m, v):
    r, c = w.shape
    tr = r if r <= 512 else 256

    def body(own_ref, sib_ref, w_ref, m_ref, v_ref, g_ref, d_ref, nm_ref, nv_ref):
        a = ((own_ref[0].astype(F32) + own_ref[1].astype(F32)) + own_ref[2].astype(F32)) + own_ref[3].astype(F32)
        b = ((sib_ref[0].astype(F32) + sib_ref[1].astype(F32)) + sib_ref[2].astype(F32)) + sib_ref[3].astype(F32)
        g = a + b
        g_ref[...] = g
        d_ref[...], nm_ref[...], nv_ref[...] = _adamw(g, w_ref[...], m_ref[...], v_ref[...])

    part = pl.BlockSpec((4, tr, c), lambda i: (0, i, 0))
    blk = pl.BlockSpec((tr, c), lambda i: (i, 0))
    return pl.pallas_call(
        body, name=name,
        grid=(r // tr,),
        in_specs=[part, part, blk, blk, blk],
        out_specs=[blk] * 4,
        out_shape=[jax.ShapeDtypeStruct((r, c), F32)] * 4,
        compiler_params=_params(("arbitrary",), VMEM_LIMIT),
    )(own, sib, w, m, v)


def _adam_ada_call(c_all, d_all, w, m, v):
    r, c = w.shape
    tr = 256

    def body(c_ref, d_ref, w_ref, m_ref, v_ref, g_ref, dl_ref, nm_ref, nv_ref):
        cc = c_ref[...]
        sc = cc * _sigmoid(cc)
        dd = d_ref[...]
        sc_hi = sc.astype(BF16)
        sc_lo = (sc - sc_hi.astype(F32)).astype(BF16)
        dd_hi = dd.astype(BF16)
        dd_lo = (dd - dd_hi.astype(F32)).astype(BF16)
        g = _dot_tn(sc_hi, dd_hi) + (_dot_tn(sc_hi, dd_lo) + _dot_tn(sc_lo, dd_hi))
        g_ref[...] = g
        dl_ref[...], nm_ref[...], nv_ref[...] = _adamw(g, w_ref[...], m_ref[...], v_ref[...])

    blk = pl.BlockSpec((tr, c), lambda i: (i, 0))
    return pl.pallas_call(
        body, name="adam_w_ada",
        grid=(r // tr,),
        in_specs=[pl.BlockSpec((16, tr), lambda i: (0, i)), pl.BlockSpec((16, c), lambda i: (0, 0)), blk, blk, blk],
        out_specs=[blk] * 4,
        out_shape=[jax.ShapeDtypeStruct((r, c), F32)] * 4,
        compiler_params=_params(("arbitrary",), VMEM_LIMIT),
    )(c_all, d_all, w, m, v)


def _adam_vectors_call(packs, offsets, vectors):
    nv = len(vectors)

    def body(*refs):
        p_ref, ins, outs = refs[0], refs[1:1 + 3 * nv], refs[1 + 3 * nv:]
        for j, off in enumerate(offsets):
            n = ins[3 * j].shape[1]
            span = -(-n // LANES) * LANES
            g = p_ref[0, :, off:off + span]
            for b in range(1, 8):
                g = g + p_ref[b, :, off:off + span]
            g = g[:, :n]
            outs[j][...] = g
            outs[nv + j][...], outs[2 * nv + j][...], outs[3 * nv + j][...] = _adamw(
                g, ins[3 * j][...], ins[3 * j + 1][...], ins[3 * j + 2][...])

    flat = [a for t in vectors for a in t]
    res = pl.pallas_call(
        body, name="adam_vectors",
        out_shape=[jax.ShapeDtypeStruct(t[0].shape, F32) for _ in range(4) for t in vectors],
    )(packs, *flat)
    return [res[k * nv:(k + 1) * nv] for k in range(4)]


ROPE_HALF = MLA_ROPE_DIM // 2
NOPE_A = MLA_NOPE_DIM - ROPE_HALF


def _zeros_like_lanes(t, n):
    return jnp.zeros(t.shape[:-1] + (n,), t.dtype)


def _to_head_lanes(t):
    nope, rope = t[..., :MLA_NOPE_DIM], t[..., MLA_NOPE_DIM:]
    return jnp.concatenate([rope[..., :ROPE_HALF], nope[..., :NOPE_A], rope[..., ROPE_HALF:], nope[..., NOPE_A:],
                            _zeros_like_lanes(t, HEAD_PAD - MLA_QK_DIM)], axis=-1)


def _from_head_lanes(t):
    return jnp.concatenate([t[..., ROPE_HALF:HALF_LANES], t[..., HALF_LANES + ROPE_HALF:MLA_QK_DIM],
                            t[..., :ROPE_HALF], t[..., HALF_LANES:HALF_LANES + ROPE_HALF]], axis=-1)


def _nope_to_head_lanes(t):
    return jnp.concatenate([_zeros_like_lanes(t, ROPE_HALF), t[..., :NOPE_A], _zeros_like_lanes(t, ROPE_HALF),
                            t[..., NOPE_A:], _zeros_like_lanes(t, HEAD_PAD - MLA_QK_DIM)], axis=-1)


def _rope_to_head_lanes(t):
    return jnp.concatenate([t[..., :ROPE_HALF], _zeros_like_lanes(t, HALF_LANES - ROPE_HALF), t[..., ROPE_HALF:],
                            _zeros_like_lanes(t, HALF_LANES - ROPE_HALF)], axis=-1)


def _rope_tables(positions):
    inv_freq = (ROPE_THETA ** (-jnp.arange(0, MLA_ROPE_DIM, 2, dtype=F32) / MLA_ROPE_DIM))[None]
    signed = _rope_to_head_lanes(jnp.concatenate([-inv_freq, inv_freq], axis=1))
    ang = positions.astype(F32)[:, None] * signed
    return jnp.cos(ang), jnp.sin(ang)


def _unshard_cols(g):
    return jnp.transpose(g, (1, 0, 2)).reshape(g.shape[1], 4 * g.shape[2])


def _shard_cols(g):
    r, c4 = g.shape
    return jnp.transpose(g.reshape(r, 4, c4 // 4), (1, 0, 2))


def kernel(x, c, positions, w_ada, b_ada, norm_w, w_in, q_lora_norm, w_uq, kv_lora_norm, w_ukv, q_head_norm, k_head_norm, w_out, loss_target, m_w_ada, m_b_ada, m_norm_w, m_w_in, m_q_lora_norm, m_w_uq, m_kv_lora_norm, m_w_ukv, m_q_head_norm, m_k_head_norm, m_w_out, v_w_ada, v_b_ada, v_norm_w, v_w_in, v_q_lora_norm, v_w_uq, v_kv_lora_norm, v_w_ukv, v_q_head_norm, v_k_head_norm, v_w_out):
    chip = 2 * lax.axis_index("x") + lax.axis_index("y")
    me8 = 2 * chip + lax.axis_index("c")
    ada_cols = w_ada.shape[2]
    c_all = _allgather_rows_call(c)[:, 0, :]
    ada_part = _ada_call(c_all, w_ada[0], lax.dynamic_slice_in_dim(b_ada, chip * ada_cols, ada_cols, axis=1))
    ada_g, win_g, wuq_g, wukv_g, wout_g = _gather_call(
        [ada_part[None]] + [w.astype(BF16) for w in (w_in, w_uq, w_ukv, w_out)], [False, True, True, True, True])
    ada = lax.dynamic_slice_in_dim(ada_g, me8, 1, axis=1).reshape(1, 4 * ada_cols)
    (sq_sum, grad_x, g_w_in, g_w_uq, g_w_ukv, g_w_out, d_ada, g_norm_w, g_qln, g_kvln, g_qhn, g_khn) = _local_step(
        x[0], ada, positions[0], loss_target[0], norm_w, win_g,
        q_lora_norm, _unshard_cols(wuq_g), kv_lora_norm, _unshard_cols(wukv_g), q_head_norm, k_head_norm,
        wout_g.reshape(D_MODEL, D_MODEL))

    grads = [g.astype(BF16) for g in (g_w_in, _shard_cols(g_w_uq), _shard_cols(g_w_ukv),
                                      g_w_out.reshape(4, D_MODEL // 4, D_MODEL))]
    pieces = [d_ada, g_norm_w, g_qln, g_kvln, g_qhn, g_khn, (0.5 * sq_sum / D_MODEL).reshape(1, 1)]
    spans = [-(-p.shape[1] // LANES) * LANES for p in pieces]
    starts = [sum(spans[:j]) for j in range(len(spans))]
    small = jnp.concatenate([jnp.pad(p, ((0, 0), (0, sp - p.shape[1]))) for p, sp in zip(pieces, spans)], axis=1)
    own, sib, packs = _exchange_call(grads, small)
    loss = jnp.sum(packs[:, 0, starts[-1]])

    names = ["adam_w_in", "adam_w_uq", "adam_w_ukv", "adam_w_out"]
    shard_w = [(w_in, m_w_in, v_w_in), (w_uq, m_w_uq, v_w_uq), (w_ukv, m_w_ukv, v_w_ukv),
               (w_out, m_w_out, v_w_out)]
    res = {}
    for name, o_g, s_g, (w, m, v) in zip(names, own, sib, shard_w):
        res[name] = _adam_shard_call(name, o_g, s_g, w[0], m[0], v[0])
    d_all = lax.dynamic_slice_in_dim(packs[:, 0, :], starts[0] + chip * ada_cols, ada_cols, axis=1)
    res_ada = _adam_ada_call(jnp.pad(c_all, ((0, 8), (0, 0))), jnp.pad(d_all, ((0, 8), (0, 0))),
                             w_ada[0], m_w_ada[0], v_w_ada[0])
    vectors = [(b_ada, m_b_ada, v_b_ada), (norm_w, m_norm_w, v_norm_w), (q_lora_norm, m_q_lora_norm, v_q_lora_norm),
               (kv_lora_norm, m_kv_lora_norm, v_kv_lora_norm), (q_head_norm, m_q_head_norm, v_q_head_norm),
               (k_head_norm, m_k_head_norm, v_k_head_norm)]
    vec_out = _adam_vectors_call(packs, starts[:len(vectors)], vectors)

    def ordered(kind):
        big = lambda name: res[name][kind][None]
        return [res_ada[kind][None], vec_out[kind][0], vec_out[kind][1], big("adam_w_in"), vec_out[kind][2],
                big("adam_w_uq"), vec_out[kind][3], big("adam_w_ukv"), vec_out[kind][4], vec_out[kind][5],
                big("adam_w_out")]

    return (loss, grad_x[None], *ordered(0), *ordered(1), *ordered(2), *ordered(3))


def _local_step(x2, ada, positions, tgt, norm_w, w_in_shards, q_lora_norm, w_uq_full,
                kv_lora_norm, w_ukv_full, q_head_norm, k_head_norm, w_out_full):
    in_shard = w_in_shards.shape[2]
    ckv_tail = C_GM - 3 * in_shard
    assert 0 <= ckv_tail and ckv_tail + MLA_ROPE_DIM + MLA_WIDTH == in_shard
    last = w_in_shards[3]
    w_in_bf = jnp.concatenate(
        [w_in_shards[0], w_in_shards[1], w_in_shards[2], last[:, :ckv_tail], last[:, ckv_tail + MLA_ROPE_DIM:],
         _rope_to_head_lanes(last[:, ckv_tail:ckv_tail + MLA_ROPE_DIM])], axis=1).astype(BF16)
    w_uq_bf = _to_head_lanes(w_uq_full.reshape(Q_LORA_RANK, MLA_HEADS, MLA_QK_DIM)).reshape(
        Q_LORA_RANK, MLA_PAD_WIDTH).astype(BF16)
    w_ukv_heads = w_ukv_full.reshape(KV_LORA_RANK, MLA_HEADS, 2 * MLA_NOPE_DIM)
    w_uk_bf = _nope_to_head_lanes(w_ukv_heads[:, :, :MLA_NOPE_DIM]).reshape(KV_LORA_RANK, MLA_PAD_WIDTH).astype(BF16)
    w_uv_bf = w_ukv_heads[:, :, MLA_NOPE_DIM:].reshape(KV_LORA_RANK, MLA_WIDTH).astype(BF16)
    w_out_bf = w_out_full.astype(BF16)
    qhn_pad, khn_pad = _to_head_lanes(q_head_norm), _to_head_lanes(k_head_norm)
    cos_t, sin_t = _rope_tables(positions)

    hb, q_sb, k_sb, v_sb, g_sb, c_q, c_kv, g_mla, k_rope = _pre_call(x2, ada, norm_w, w_in_bf)
    q_m, k_m, v_m, cqn, ckvn, q0, k0 = _mla_prep_call(
        c_q, c_kv, k_rope, cos_t, sin_t, q_lora_norm, kv_lora_norm, qhn_pad, khn_pad,
        w_uq_bf, w_uk_bf, w_uv_bf)
    o_sb, r_sb, kstart = _sb_fwd_call(q_sb, k_sb, v_sb)
    o_mla, lse = _mla_fwd_call(q_m, k_m, v_m)
    do_sb, do_mla, dg_sb, dg_mla, dy, g_w_out, d_gate, sq = _out_call(
        o_sb, g_sb, o_mla, g_mla, x2, tgt, ada, w_out_bf)

    dq_sb, dk_sb, dv_sb = _sb_bwd_call(kstart, q_sb, k_sb, v_sb, do_sb, r_sb)
    dq_m, dk_m, dv_m = _mla_bwd_call(q_m, k_m, v_m, do_mla, o_mla, lse)
    (d_cq, d_ckv, d_kr, g_wuq_pad, g_wuk_pad, g_wuv, g_qln, g_kvln, g_qhn, g_khn) = _mla_prep_bwd_call(
        dq_m, dk_m, dv_m, q0, k0, cqn, ckvn, c_q, c_kv, cos_t, sin_t,
        q_lora_norm, kv_lora_norm, qhn_pad, khn_pad, w_uq_bf, w_uk_bf, w_uv_bf)
    grad_x, g_win_pad, d_shift, d_scale, g_norm_w = _dh_call(
        [dq_sb, dk_sb, dv_sb, dg_sb, d_cq, d_ckv, dg_mla, d_kr], hb, x2, dy, ada, norm_w, w_in_bf)

    g_kr = g_win_pad[:, C_KR:]
    g_last = jnp.concatenate([g_win_pad[:, 3 * in_shard:C_GM], g_kr[:, :ROPE_HALF],
                              g_kr[:, HALF_LANES:HALF_LANES + ROPE_HALF], g_win_pad[:, C_GM:C_KR]], axis=1)
    g_w_in = jnp.stack([g_win_pad[:, j * in_shard:(j + 1) * in_shard] for j in range(3)] + [g_last])
    g_w_uq = _from_head_lanes(g_wuq_pad.reshape(Q_LORA_RANK, MLA_HEADS, HEAD_PAD)).reshape(Q_LORA_RANK, -1)
    g_w_ukv = jnp.concatenate(
        [_from_head_lanes(g_wuk_pad.reshape(KV_LORA_RANK, MLA_HEADS, HEAD_PAD))[:, :, :MLA_NOPE_DIM],
         g_wuv.reshape(KV_LORA_RANK, MLA_HEADS, MLA_NOPE_DIM)], axis=2).reshape(KV_LORA_RANK, -1)
    d_ada = jnp.concatenate([d_shift, d_scale, d_gate], axis=1)
    return (jnp.sum(sq), grad_x, g_w_in, g_w_uq, g_w_ukv, g_w_out, d_ada, g_norm_w, g_qln, g_kvln,
            _from_head_lanes(g_qhn), _from_head_lanes(g_khn))
```

```python
import math

import jax
import jax.numpy as jnp
from jax import lax
from jax.experimental import pallas as pl
from jax.experimental.pallas import tpu as pltpu

F32 = jnp.float32
BF16 = jnp.bfloat16
I32 = jnp.int32

D_MODEL = 1024
SB_HEADS = 8
SB_WIDTH = 512
MLA_HEADS = 8
MLA_QK_DIM = 96
MLA_NOPE_DIM = 64
MLA_ROPE_DIM = 32
MLA_WIDTH = 512
Q_LORA_RANK = 384
KV_LORA_RANK = 256
ROPE_THETA = 10000.0
EPS = 1e-6
LANES = 128
HALF_LANES = LANES // 2
HEAD_PAD = 128
MLA_PAD_WIDTH = MLA_HEADS * HEAD_PAD

C_Q, C_K, C_V, C_G = 0, 512, 1024, 1536
C_CQ, C_CKV, C_GM, C_KR = 2048, 2432, 2688, 3200
IN_COLS_PAD = 3328

ADAM_LR = 0.001
ADAM_B1 = 0.9
ADAM_B2 = 0.999
ADAM_EPS = 1e-08
ADAM_WD = 0.01
ADAM_STEP = 10

SB_SCALE = 0.125
SB_GROUP = 4
MLA_SCALE = 1.0 / math.sqrt(MLA_QK_DIM)
LN2 = math.log(2.0)
MLA_SCALE_LOG2 = MLA_SCALE / LN2
MLA_BQ = 1024
MLA_BWD_BQ = 1024
MLA_BK = 1024
MLA_BWD_BK = 1024
MLA_STEP = 512
MLA_BWD_STEP = 256
SB_DEAD = -104.0
MASK_NEG = -1e30

VMEM_LIMIT = 56 * 1024 * 1024
MESH = pl.DeviceIdType.MESH


def _dot(a, b):
    return jnp.dot(a, b, preferred_element_type=F32)


def _dot_nt(a, b):
    return lax.dot_general(a, b, (((1,), (1,)), ((), ())), preferred_element_type=F32)


def _dot_tn(a, b):
    return lax.dot_general(a, b, (((0,), (0,)), ((), ())), preferred_element_type=F32)


def _sigmoid(x):
    return 1.0 / (1.0 + jnp.exp(-x))


def _split_dot(a, m):
    hi = a.astype(BF16)
    lo = (a - hi.astype(F32)).astype(BF16)
    return _dot(hi, m) + _dot(lo, m)


def _params(sem, vmem=None):
    return pltpu.CompilerParams(dimension_semantics=sem, vmem_limit_bytes=vmem)


def _row_tile(s, want):
    return min(want, s)


def _hbm_spec():
    return pl.BlockSpec(memory_space=pltpu.HBM)


def _allgather_rows_call(row):
    def body(in_ref, out_ref, send_sems, recv_sems, loc_sem):
        x, y, c = lax.axis_index("x"), lax.axis_index("y"), lax.axis_index("c")
        flips = [(fx, fy, fc) for fx in (0, 1) for fy in (0, 1) for fc in (0, 1)][1:]

        def peer(r):
            fx, fy, fc = flips[r]
            return ((1 - x) if fx else x, (1 - y) if fy else y, (1 - c) if fc else c)

        def copy(r, slot):
            return pltpu.make_async_remote_copy(
                src_ref=in_ref, dst_ref=out_ref.at[slot], send_sem=send_sems.at[r], recv_sem=recv_sems.at[r],
                device_id=peer(r), device_id_type=MESH)

        local = pltpu.make_async_copy(in_ref, out_ref.at[4 * x + 2 * y + c], loc_sem)
        local.start()
        sends = [copy(r, 4 * x + 2 * y + c) for r in range(7)]
        for cp in sends:
            cp.start()
        for r in range(7):
            px, py, pc = peer(r)
            copy(r, 4 * px + 2 * py + pc).wait_recv()
        for cp in sends:
            cp.wait_send()
        local.wait()

    return pl.pallas_call(
        body, name="gather_rows",
        out_shape=jax.ShapeDtypeStruct((8,) + row.shape, row.dtype),
        in_specs=[_hbm_spec()], out_specs=_hbm_spec(),
        scratch_shapes=[pltpu.SemaphoreType.DMA((7,)), pltpu.SemaphoreType.DMA((7,)), pltpu.SemaphoreType.DMA],
    )(row)


def _gather_call(shards, split):
    n = len(shards)
    halves = [s.shape[1] // 2 for s in shards]

    def body(*refs):
        ins, outs = refs[:n], refs[n:2 * n]
        ici_send, ici_recv, d2d_send, d2d_recv, loc_sems = refs[2 * n:]
        x, y, c = lax.axis_index("x"), lax.axis_index("y"), lax.axis_index("c")
        me = 2 * x + y
        peers = [(1 - x, y), (x, 1 - y), (1 - x, 1 - y)]

        def rows(a, which):
            return pl.ds(pl.multiple_of(which * halves[a], 16), halves[a])

        def ici(a, j, slot):
            px, py = peers[j]
            src, dst = ins[a].at[0], outs[a].at[slot]
            if split[a]:
                src, dst = src.at[rows(a, c)], dst.at[rows(a, c)]
            return pltpu.make_async_remote_copy(
                src_ref=src, dst_ref=dst,
                send_sem=ici_send.at[3 * a + j], recv_sem=ici_recv.at[3 * a + j],
                device_id=(px, py, c), device_id_type=MESH)

        def d2d(a, j, which):
            px, py = peers[j]
            piece = outs[a].at[2 * px + py, rows(a, which)]
            return pltpu.make_async_remote_copy(
                src_ref=piece, dst_ref=piece,
                send_sem=d2d_send.at[3 * a + j], recv_sem=d2d_recv.at[3 * a + j],
                device_id=(x, y, 1 - c), device_id_type=MESH)

        local = [pltpu.make_async_copy(ins[a].at[0], outs[a].at[me], loc_sems.at[a]) for a in range(n)]
        for cp in local:
            cp.start()
        sends = [ici(a, j, me) for a in range(n) for j in range(3)]
        for cp in sends:
            cp.start()
        for a in range(n):
            for j in range(3):
                px, py = peers[j]
                ici(a, j, 2 * px + py).wait_recv()
                if split[a]:
                    cp = d2d(a, j, c)
                    cp.start()
                    sends.append(cp)
        for a in range(n):
            for j in range(3):
                if split[a]:
                    d2d(a, j, 1 - c).wait_recv()
        for cp in sends:
            cp.wait_send()
        for cp in local:
            cp.wait()

    return pl.pallas_call(
        body, name="gather_weights",
        out_shape=[jax.ShapeDtypeStruct((4,) + s.shape[1:], s.dtype) for s in shards],
        in_specs=[_hbm_spec() for _ in shards],
        out_specs=[_hbm_spec() for _ in shards],
        scratch_shapes=[pltpu.SemaphoreType.DMA((3 * n,)), pltpu.SemaphoreType.DMA((3 * n,)),
                        pltpu.SemaphoreType.DMA((3 * n,)), pltpu.SemaphoreType.DMA((3 * n,)),
                        pltpu.SemaphoreType.DMA((n,))],
    )(*shards)


def _exchange_call(grads, small):
    n = len(grads)

    def body(*refs):
        g_in, small_in = refs[:n], refs[n]
        own, sib, packs = refs[n + 1:2 * n + 1], refs[2 * n + 1:3 * n + 1], refs[3 * n + 1]
        ici_send, ici_recv, d2d_send, d2d_recv, sm_send, sm_recv, loc_sems = refs[3 * n + 2:]
        x, y, c = lax.axis_index("x"), lax.axis_index("y"), lax.axis_index("c")
        me = 2 * x + y
        me8 = 4 * x + 2 * y + c
        sibling = (x, y, 1 - c)
        peers = [(1 - x, y), (x, 1 - y), (1 - x, 1 - y)]
        flips = [(fx, fy, fc) for fx in (0, 1) for fy in (0, 1) for fc in (0, 1)][1:]

        def ici(a, j, src_slot, dst_slot):
            px, py = peers[j]
            return pltpu.make_async_remote_copy(
                src_ref=g_in[a].at[src_slot], dst_ref=own[a].at[dst_slot],
                send_sem=ici_send.at[3 * a + j], recv_sem=ici_recv.at[3 * a + j],
                device_id=(px, py, c), device_id_type=MESH)

        def d2d(a, rel, chip, src):
            return pltpu.make_async_remote_copy(
                src_ref=src, dst_ref=sib[a].at[chip],
                send_sem=d2d_send.at[4 * a + rel], recv_sem=d2d_recv.at[4 * a + rel],
                device_id=sibling, device_id_type=MESH)

        def flipped(r):
            fx, fy, fc = flips[r]
            return ((1 - x) if fx else x, (1 - y) if fy else y, (1 - c) if fc else c)

        def sm(r, slot):
            return pltpu.make_async_remote_copy(
                src_ref=small_in, dst_ref=packs.at[slot],
                send_sem=sm_send.at[r], recv_sem=sm_recv.at[r],
                device_id=flipped(r), device_id_type=MESH)

        def peer8(r):
            px, py, pc = flipped(r)
            return 4 * px + 2 * py + pc

        local = [pltpu.make_async_copy(g_in[a].at[me], own[a].at[me], loc_sems.at[a]) for a in range(n)]
        local.append(pltpu.make_async_copy(small_in, packs.at[me8], loc_sems.at[n]))
        for cp in local:
            cp.start()
        sends = []
        for r in range(7):
            sends.append(sm(r, me8))
        for a in range(n):
            for j in range(3):
                px, py = peers[j]
                sends.append(ici(a, j, 2 * px + py, me))
        for cp in sends:
            cp.start()
        for a in range(n):
            cp = d2d(a, 0, me, g_in[a].at[me])
            cp.start()
            sends.append(cp)
        for a in range(n):
            for j in range(3):
                px, py = peers[j]
                ici(a, j, me, 2 * px + py).wait_recv()
                cp = d2d(a, 1 + j, 2 * px + py, own[a].at[2 * px + py])
                cp.start()
                sends.append(cp)
        for a in range(n):
            d2d(a, 0, me, g_in[a].at[me]).wait_recv()
            for j in range(3):
                px, py = peers[j]
                d2d(a, 1 + j, 2 * px + py, g_in[a].at[me]).wait_recv()
        for r in range(7):
            sm(r, peer8(r)).wait_recv()
        for cp in sends:
            cp.wait_send()
        for cp in local:
            cp.wait()

    out_shape = ([jax.ShapeDtypeStruct(g.shape, g.dtype) for g in grads] * 2
                 + [jax.ShapeDtypeStruct((8,) + small.shape, small.dtype)])
    res = pl.pallas_call(
        body, name="exchange_grads",
        out_shape=out_shape,
        in_specs=[_hbm_spec() for _ in range(n + 1)],
        out_specs=[_hbm_spec() for _ in range(2 * n + 1)],
        scratch_shapes=[pltpu.SemaphoreType.DMA((3 * n,)), pltpu.SemaphoreType.DMA((3 * n,)),
                        pltpu.SemaphoreType.DMA((4 * n,)), pltpu.SemaphoreType.DMA((4 * n,)),
                        pltpu.SemaphoreType.DMA((7,)), pltpu.SemaphoreType.DMA((7,)),
                        pltpu.SemaphoreType.DMA((n + 1,))],
    )(*grads, small)
    return res[:n], res[n:2 * n], res[2 * n]


def _ada_call(c_all, w_ada_cols, b_ada_cols):
    def body(c_ref, w_ref, b_ref, o_ref):
        cc = c_ref[...]
        o_ref[...] = _dot((cc * _sigmoid(cc)).astype(BF16), w_ref[...].astype(BF16)) + b_ref[...]

    return pl.pallas_call(
        body, name="ada_fwd",
        out_shape=jax.ShapeDtypeStruct((c_all.shape[0], w_ada_cols.shape[1]), F32),
        compiler_params=pltpu.CompilerParams(vmem_limit_bytes=VMEM_LIMIT),
    )(c_all, w_ada_cols, b_ada_cols)


def _ada_part(j):
    return pl.BlockSpec((1, D_MODEL), lambda i: (0, j))


def _full(shape):
    return pl.BlockSpec(shape, lambda i: (0,) * len(shape))


def _rows(tm, width):
    return pl.BlockSpec((tm, width), lambda i: (i, 0))


def _rope(t, cos_t, sin_t):
    return t * cos_t + pltpu.roll(t, HALF_LANES, 1) * sin_t


def _rope_adjoint(d, cos_t, sin_t):
    return d * cos_t + pltpu.roll(d * sin_t, HALF_LANES, 1)


def _pre_call(x, ada, norm_w, w_in_bf, cos_t, sin_t, q_lora_norm, kv_lora_norm, qhn_pad, khn_pad,
              w_uq_bf, w_uk_bf, w_uv_bf):
    s = x.shape[0]
    tm = _row_tile(s, 512)
    out_defs = [(D_MODEL, BF16), (512, BF16), (512, BF16), (512, BF16), (512, F32),
                (Q_LORA_RANK, F32), (KV_LORA_RANK, F32), (512, F32),
                (MLA_PAD_WIDTH, BF16), (MLA_PAD_WIDTH, BF16), (MLA_WIDTH, BF16),
                (Q_LORA_RANK, BF16), (KV_LORA_RANK, BF16), (MLA_PAD_WIDTH, F32), (MLA_PAD_WIDTH, F32)]

    def body(x_ref, sh_ref, sc_ref, nw_ref, w_ref, cos_ref, sin_ref, qln_ref, kvln_ref, qhn_ref, khn_ref,
             wuq_ref, wuk_ref, wuv_ref,
             hb_ref, qsb_ref, ksb_ref, vsb_ref, gsb_ref, cq_ref, ckv_ref, gm_ref,
             q_ref, k_ref, v_ref, cqn_ref, ckvn_ref, q0_ref, k0_ref):
        xx = x_ref[...]
        r0 = lax.rsqrt(jnp.mean(xx * xx, axis=-1, keepdims=True) + EPS)
        hb = ((xx * r0 * nw_ref[...]) * (1.0 + sc_ref[...]) + sh_ref[...]).astype(BF16)
        hb_ref[...] = hb

        def proj(c0, width):
            return _dot(hb, w_ref[:, c0:c0 + width])

        cq = proj(C_CQ, Q_LORA_RANK)
        ckv = proj(C_CKV, KV_LORA_RANK)
        kr = proj(C_KR, LANES)
        cq_ref[...] = cq
        ckv_ref[...] = ckv
        cqn = (cq * lax.rsqrt(jnp.mean(cq * cq, axis=-1, keepdims=True) + EPS) * qln_ref[...]).astype(BF16)
        cqn_ref[...] = cqn
        ckvn = (ckv * lax.rsqrt(jnp.mean(ckv * ckv, axis=-1, keepdims=True) + EPS) * kvln_ref[...]).astype(BF16)
        ckvn_ref[...] = ckvn
        qsb_ref[...] = proj(C_Q, 512).astype(BF16)
        v_ref[...] = _dot(ckvn, wuv_ref[...]).astype(BF16)
        q0_ref[...] = _dot(cqn, wuq_ref[...])
        k0_ref[...] = _dot(ckvn, wuk_ref[...])
        ksb_ref[...] = proj(C_K, 512).astype(BF16)
        cos_t, sin_t = cos_ref[...], sin_ref[...]
        heads = [slice(h * HEAD_PAD, (h + 1) * HEAD_PAD) for h in range(MLA_HEADS)]
        for cols in heads:
            k0_ref[:, cols] = k0_ref[:, cols] + kr

        def inv_rms(ref):
            sums = [jnp.sum(ref[:, cols] * ref[:, cols], axis=-1, keepdims=True) for cols in heads]
            return [lax.rsqrt(t * (1.0 / MLA_QK_DIM) + EPS) for t in sums]

        rqs = inv_rms(q0_ref)
        vsb_ref[...] = proj(C_V, 512).astype(BF16)
        rks = inv_rms(k0_ref)
        gsb_ref[...] = proj(C_G, 512)
        for cols, rq, rk in zip(heads, rqs, rks):
            q_ref[:, cols] = (_rope(q0_ref[:, cols] * rq * qhn_ref[...], cos_t, sin_t) * MLA_SCALE_LOG2).astype(BF16)
            k_ref[:, cols] = _rope(k0_ref[:, cols] * rk * khn_ref[...], cos_t, sin_t).astype(BF16)
        gm_ref[...] = proj(C_GM, 512)

    return pl.pallas_call(
        body, name="pre_proj",
        grid=(s // tm,),
        in_specs=[_rows(tm, D_MODEL), _ada_part(0), _ada_part(1), _full((1, D_MODEL)),
                  pl.BlockSpec((D_MODEL, IN_COLS_PAD), lambda i: (0, 0), pipeline_mode=pl.Buffered(1)),
                  _rows(tm, LANES), _rows(tm, LANES),
                  _full((1, Q_LORA_RANK)), _full((1, KV_LORA_RANK)), _full((1, LANES)), _full((1, LANES)),
                  _full((Q_LORA_RANK, MLA_PAD_WIDTH)), _full((KV_LORA_RANK, MLA_PAD_WIDTH)),
                  _full((KV_LORA_RANK, MLA_WIDTH))],
        out_specs=[_rows(tm, w) for w, _ in out_defs],
        out_shape=[jax.ShapeDtypeStruct((s, w), dt) for w, dt in out_defs],
        compiler_params=_params(("arbitrary",), VMEM_LIMIT),
    )(x, ada, ada, norm_w, w_in_bf, cos_t, sin_t, q_lora_norm, kv_lora_norm, qhn_pad, khn_pad,
      w_uq_bf, w_uk_bf, w_uv_bf)


def _log_sigmoid_pair(z):
    ls = jnp.minimum(z, 0.0) - jnp.log(1.0 + jnp.exp(-jnp.abs(z)))
    return ls, ls - z


def _pair(hh):
    return slice((hh // 2) * LANES, (hh // 2 + 1) * LANES)


def _sb_fwd_call(q, k, v):
    s = q.shape[0]
    bq = _row_tile(s, 256)
    nq = s // bq
    nh = SB_GROUP
    width = nh * HALF_LANES

    def body(q_ref, k_ref, v_ref, o_ref, r_ref, ks_ref):
        hp, i = pl.program_id(0), pl.program_id(1)
        lane = lax.broadcasted_iota(I32, (bq, LANES), 1)
        row = lax.broadcasted_iota(I32, (bq, bq), 0)
        col = lax.broadcasted_iota(I32, (bq, bq), 1)
        strict = col < row
        later = jnp.where(row > col, 1.0, 0.0).astype(BF16)
        masks = [_head_mask(lane, hh).astype(BF16) for hh in range(2)]
        qms = [q_ref[:, _pair(hh)] * jnp.asarray(SB_SCALE, BF16) * masks[hh % 2] for hh in range(nh)]

        def walk(blocks, state):
            chains = [(kb, diagonal, hh) for kb, diagonal in blocks for hh in range(nh)]
            keys = lambda kb: pl.ds(pl.multiple_of(kb * bq, bq), bq)
            zs = [_dot_nt(qms[hh], k_ref[keys(kb), _pair(hh)]) for kb, _, hh in chains]
            pairs = []
            for z, (_, diagonal, _) in zip(zs, chains):
                ls, lk = _log_sigmoid_pair(z)
                pairs.append((ls, jnp.where(strict, lk, 0.0) if diagonal else lk))
            sums = [_split_dot(lk, later) for _, lk in pairs]
            runs = [st[0] for st in state]
            ws = []
            for (ls, lk), after, (_, diagonal, hh) in zip(pairs, sums, chains):
                w = jnp.exp(ls + (after + runs[hh]))
                ws.append((jnp.where(strict, w, 0.0) if diagonal else w).astype(BF16))
                runs[hh] = runs[hh] + jnp.sum(lk, axis=1, keepdims=True)
            accs = [st[1] for st in state]
            for w, (kb, _, hh) in zip(ws, chains):
                accs[hh] = accs[hh] + _dot(w, v_ref[keys(kb), _pair(hh)])
            return tuple(zip(runs, accs))

        def alive(state):
            top = jnp.max(state[0][0])
            for st in state[1:]:
                top = jnp.maximum(top, jnp.max(st[0]))
            return (top > SB_DEAD).astype(I32)

        def finish(state, first):
            ks_ref[hp, i] = first
            for pair in range(nh // 2):
                o_ref[:, _pair(2 * pair)] = jnp.where(lane < HALF_LANES, state[2 * pair][1], state[2 * pair + 1][1])
                r_ref[:, _pair(2 * pair)] = jnp.where(lane < HALF_LANES, state[2 * pair][0], state[2 * pair + 1][0])

        zero = ((jnp.zeros((bq, 1), F32), jnp.zeros((bq, LANES), F32)),) * nh

        @pl.when(i == 0)
        def _():
            finish(walk([(0, True)], zero), 0)

        @pl.when(i > 0)
        def _():
            state = walk([(i, True), (i - 1, False)], zero)

            def cond(carry):
                return jnp.logical_and(carry[0] >= 0, carry[1] > 0)

            def step(carry):
                state = walk([(carry[0], False)], carry[2])
                return carry[0] - 1, alive(state), state

            kb, _, state = lax.while_loop(cond, step, (i - 2, alive(state), state))
            finish(state, kb + 1)

    return pl.pallas_call(
        body, name="sb_fwd",
        grid=(SB_HEADS // nh, nq),
        in_specs=[pl.BlockSpec((bq, width), lambda h, i: (i, h)),
                  pl.BlockSpec((s, width), lambda h, i: (0, h)),
                  pl.BlockSpec((s, width), lambda h, i: (0, h))],
        out_specs=[pl.BlockSpec((bq, width), lambda h, i: (i, h)),
                   pl.BlockSpec((bq, width), lambda h, i: (i, h)),
                   pl.BlockSpec(memory_space=pltpu.SMEM)],
        out_shape=[jax.ShapeDtypeStruct((s, SB_WIDTH), F32), jax.ShapeDtypeStruct((s, SB_WIDTH), F32),
                   jax.ShapeDtypeStruct((SB_HEADS // nh, nq), I32)],
        compiler_params=_params(("arbitrary", "arbitrary"), VMEM_LIMIT),
    )(q, k, v)


def _mla_fwd_call(q, k, v):
    s = q.shape[0]
    bq = _row_tile(s, MLA_BQ)
    bk = _row_tile(s, MLA_BK)
    nq = s // bq
    assert bk % bq == 0
    step = min(bk // 2, MLA_STEP)
    nsub = bk // step
    assert nsub % 2 == 0

    def body(q_ref, k_ref, v_ref, o_ref, lse_ref, p_ref, s_ref):
        i = pl.program_id(1)
        lane = lax.broadcasted_iota(I32, (bq, LANES), 1)
        row = lax.broadcasted_iota(I32, (step, bq), 1)
        col = lax.broadcasted_iota(I32, (step, bq), 0)
        n_full = (i * bq) // bk

        def keys(g):
            return pl.ds(pl.multiple_of(g * step, step), step)

        def join(left, right, qlo):
            return right if qlo == 0 else jnp.concatenate([left[:, :qlo], right], axis=1)

        def put_scores(g, slot, qlo=0):
            for hh in range(2):
                cols = slice(hh * HEAD_PAD, (hh + 1) * HEAD_PAD)
                s_ref[slot, hh, :, qlo:] = _dot_nt(k_ref[keys(g), cols], q_ref[qlo:, cols])

        def add_pv(carry, g, slot, qlo=0):
            vblk = v_ref[keys(g), :]
            out = []
            for hh, (m, l, acc, alpha) in enumerate(carry):
                upd = alpha[:, qlo:] * acc[:, qlo:] + _dot_tn(vblk, p_ref[slot, hh, :, qlo:])
                out.append((m, l, join(acc, upd, qlo), alpha))
            return tuple(out)

        def substep(g, slot, carry, masked, prefetch, qlo=0, next_qlo=0, prev_qlo=0):
            if prefetch:
                put_scores(g + 1, 1 - slot, next_qlo)
            carry = add_pv(carry, jnp.maximum(g - 1, 0), 1 - slot, prev_qlo)
            new = []
            for hh in range(2):
                m, l, acc, _ = carry[hh]
                sc = s_ref[slot, hh, :, qlo:]
                if masked:
                    sc = jnp.where(col[:, qlo:] + g * step <= row[:, qlo:] + i * bq, sc, MASK_NEG)
                m_new = jnp.maximum(m[:, qlo:], jnp.max(sc, axis=0, keepdims=True))
                p = jnp.exp2(sc - m_new)
                alpha = jnp.exp2(m[:, qlo:] - m_new)
                l_new = alpha * l[:, qlo:] + jnp.sum(p, axis=0, keepdims=True)
                p_ref[slot, hh, :, qlo:] = p.astype(BF16)
                new.append((join(m, m_new, qlo), join(l, l_new, qlo), acc, join(jnp.ones_like(m), alpha, qlo)))
            return tuple(new)

        def first_query(t, masked):
            return t * step if (masked and bk == bq and 0 <= t < nsub) else 0

        def chunk(kb, carry, masked):
            for t in range(nsub):
                last = masked and t == nsub - 1
                carry = substep(nsub * kb + t, t % 2, carry, masked, not last, first_query(t, masked),
                                first_query(t + 1, masked), first_query(t - 1, masked))
            return carry

        p_ref[1] = jnp.zeros_like(p_ref[1])
        put_scores(0, 0)
        one = (jnp.full((1, bq), MASK_NEG, F32), jnp.zeros((1, bq), F32), jnp.zeros((LANES, bq), F32),
               jnp.ones((1, bq), F32))
        carry = lax.fori_loop(0, n_full, lambda kb, cr: chunk(kb, cr, False), (one, one))
        carry = chunk(n_full, carry, True)
        (m0, l0, a0, _), (m1, l1, a1, _) = add_pv(carry, nsub * n_full + nsub - 1, 1, first_query(nsub - 1, True))
        o_ref[...] = jnp.where(lane < HALF_LANES, (a0 / l0).T, (a1 / l1).T)
        sub = lax.broadcasted_iota(I32, (8, bq), 0)
        lse_ref[...] = jnp.where(sub == 0, m0 + jnp.log2(l0), jnp.where(sub == 1, m1 + jnp.log2(l1), 0.0))

    return pl.pallas_call(
        body, name="mla_fwd",
        grid=(4, nq),
        in_specs=[pl.BlockSpec((bq, 2 * HEAD_PAD), lambda h, i: (i, h)),
                  pl.BlockSpec((s, 2 * HEAD_PAD), lambda h, i: (0, h)),
                  pl.BlockSpec((s, LANES), lambda h, i: (0, h))],
        out_specs=[pl.BlockSpec((bq, LANES), lambda h, i: (i, h)),
                   pl.BlockSpec((None, 8, bq), lambda h, i: (h, 0, i))],
        out_shape=[jax.ShapeDtypeStruct((s, MLA_WIDTH), F32), jax.ShapeDtypeStruct((4, 8, s), F32)],
        scratch_shapes=[pltpu.VMEM((2, 2, step, bq), BF16), pltpu.VMEM((2, 2, step, bq), F32)],
        compiler_params=_params(("arbitrary", "arbitrary"), VMEM_LIMIT),
    )(q, k, v)


def _out_call(o_sb, g_sb, o_mla, g_mla, x, target, ada, w_out_bf):
    s = x.shape[0]
    tm = _row_tile(s, 512)

    def body(osb_ref, gsb_ref, oml_ref, gml_ref, x_ref, t_ref, gate_ref, w_ref,
             dosb_ref, doml_ref, dgsb_ref, dgml_ref, dy_ref, gw_ref, dgate_ref, sq_ref):
        @pl.when(pl.program_id(0) == 0)
        def _():
            gw_ref[...] = jnp.zeros_like(gw_ref)
            dgate_ref[...] = jnp.zeros_like(dgate_ref)
            sq_ref[...] = jnp.zeros_like(sq_ref)

        g_s, g_m = gsb_ref[...], gml_ref[...]
        sig_s, sig_m = _sigmoid(g_s), _sigmoid(g_m)
        silu_s, silu_m = g_s * sig_s, g_m * sig_m
        o_s, o_m = osb_ref[...], oml_ref[...]
        mixed = jnp.concatenate([o_s * silu_s, o_m * silu_m], axis=1).astype(BF16)
        u = _dot(mixed, w_ref[...])
        gate_v = gate_ref[...]
        err = x_ref[...] + gate_v * u - t_ref[...]
        sq_ref[...] += jnp.sum(err * err, axis=0, keepdims=True)
        dy = err * (1.0 / D_MODEL)
        dy_ref[...] = dy
        dgate_ref[...] += jnp.sum(dy * u, axis=0, keepdims=True)
        du = (dy * gate_v).astype(BF16)
        gw_ref[...] += _dot_tn(mixed, du)
        dmix = _dot_nt(du, w_ref[...])
        dm_s, dm_m = dmix[:, :SB_WIDTH], dmix[:, SB_WIDTH:]
        dosb_ref[...] = (dm_s * silu_s).astype(BF16)
        doml_ref[...] = (dm_m * silu_m).astype(BF16)
        dgsb_ref[...] = (dm_s * o_s * (sig_s * (1.0 + g_s * (1.0 - sig_s)))).astype(BF16)
        dgml_ref[...] = (dm_m * o_m * (sig_m * (1.0 + g_m * (1.0 - sig_m)))).astype(BF16)

    return pl.pallas_call(
        body, name="out_proj_loss",
        grid=(s // tm,),
        in_specs=[_rows(tm, 512), _rows(tm, 512), _rows(tm, 512), _rows(tm, 512),
                  _rows(tm, D_MODEL), _rows(tm, D_MODEL), _ada_part(2), _full((D_MODEL, D_MODEL))],
        out_specs=[_rows(tm, 512), _rows(tm, 512), _rows(tm, 512), _rows(tm, 512), _rows(tm, D_MODEL),
                   _full((D_MODEL, D_MODEL)), _full((1, D_MODEL)), _full((1, D_MODEL))],
        out_shape=[jax.ShapeDtypeStruct((s, 512), BF16)] * 4
        + [jax.ShapeDtypeStruct((s, D_MODEL), F32), jax.ShapeDtypeStruct((D_MODEL, D_MODEL), F32),
           jax.ShapeDtypeStruct((1, D_MODEL), F32), jax.ShapeDtypeStruct((1, D_MODEL), F32)],
        compiler_params=_params(("arbitrary",), VMEM_LIMIT),
    )(o_sb, g_sb, o_mla, g_mla, x, target, ada, w_out_bf)


def _head_mask(lane, hh):
    return jnp.where((lane >= HALF_LANES) if hh else (lane < HALF_LANES), 1.0, 0.0)


def _pick_lane(packed, lane, which):
    return jnp.sum(jnp.where(lane == which, packed, 0.0), axis=1, keepdims=True)


def _sb_bwd_call(kstart, q, k, v, do, rfin):
    s = q.shape[0]
    bq = _row_tile(s, 256)
    nq = s // bq
    nh = SB_GROUP
    width = nh * HALF_LANES

    def body(ks_ref, q_ref, k_ref, v_ref, do_ref, r_ref, dq_ref, dk_ref, dv_ref):
        hp, i = pl.program_id(0), pl.program_id(1)

        @pl.when(i == 0)
        def _():
            dk_ref[...] = jnp.zeros_like(dk_ref)
            dv_ref[...] = jnp.zeros_like(dv_ref)

        lane = lax.broadcasted_iota(I32, (bq, LANES), 1)
        row = lax.broadcasted_iota(I32, (bq, bq), 0)
        col = lax.broadcasted_iota(I32, (bq, bq), 1)
        upto = jnp.where(row <= col, 1.0, 0.0).astype(BF16)
        before = jnp.where(row < col, 1.0, 0.0).astype(BF16)
        masks = [_head_mask(lane, hh).astype(BF16) for hh in range(2)]
        qms = [q_ref[:, _pair(hh)] * jnp.asarray(SB_SCALE, BF16) * masks[hh % 2] for hh in range(nh)]
        doms = [do_ref[:, _pair(hh)] * masks[hh % 2] for hh in range(nh)]
        totals = [_pick_lane(r_ref[:, _pair(hh)], lane, HALF_LANES * (hh % 2)) for hh in range(nh)]
        strict = col < row

        def walk(blocks, state):
            chains = [(kb, diagonal, hh) for kb, diagonal in blocks for hh in range(nh)]
            keys = lambda kb: pl.ds(pl.multiple_of(kb * bq, bq), bq)
            cut = lambda x, diagonal: jnp.where(strict, x, 0.0) if diagonal else x
            zs = [_dot_nt(qms[hh], k_ref[keys(kb), _pair(hh)]) for kb, _, hh in chains]
            dws = [_dot_nt(doms[hh], v_ref[keys(kb), _pair(hh)]) for kb, _, hh in chains]
            pairs = []
            for z, (_, diagonal, _) in zip(zs, chains):
                ls, lk = _log_sigmoid_pair(z)
                pairs.append((ls, cut(lk, diagonal)))
            incls = [_split_dot(lk, upto) for _, lk in pairs]
            pres = [st[0] for st in state]
            ws, gs = [], []
            for (ls, lk), incl, dw, (_, diagonal, hh) in zip(pairs, incls, dws, chains):
                w = cut(jnp.exp(ls + ((totals[hh] - pres[hh]) - incl)), diagonal)
                ws.append(w.astype(BF16))
                gs.append(w * dw)
                pres[hh] = pres[hh] + jnp.sum(lk, axis=1, keepdims=True)
            gsums = [_dot(g.astype(BF16), before) for g in gs]
            gpres = [st[1] for st in state]
            dzs = []
            for (ls, _), g, gsum, (_, diagonal, hh) in zip(pairs, gs, gsums, chains):
                dzs.append(cut(g - jnp.exp(ls) * (g + (gpres[hh] + gsum)), diagonal).astype(BF16))
                gpres[hh] = gpres[hh] + jnp.sum(g, axis=1, keepdims=True)
            dqs = [st[2] for st in state]
            dk_parts, dv_parts = [], []
            for dzb, w, (kb, _, hh) in zip(dzs, ws, chains):
                dk_parts.append(_dot_tn(dzb, qms[hh]))
                dv_parts.append(_dot_tn(w, doms[hh]))
                dqs[hh] = dqs[hh] + _dot(dzb, k_ref[keys(kb), _pair(hh)])
            for b, (kb, _) in enumerate(blocks):
                for pair in range(nh // 2):
                    c0 = b * nh + 2 * pair
                    dk_ref[keys(kb), _pair(2 * pair)] += dk_parts[c0] + dk_parts[c0 + 1]
                    dv_ref[keys(kb), _pair(2 * pair)] += dv_parts[c0] + dv_parts[c0 + 1]
            return tuple(zip(pres, gpres, dqs))

        def finish(state):
            for pair in range(nh // 2):
                both = jnp.where(lane < HALF_LANES, state[2 * pair][2], state[2 * pair + 1][2])
                dq_ref[:, _pair(2 * pair)] = (both * SB_SCALE).astype(BF16)

        zero = ((jnp.zeros((bq, 1), F32), jnp.zeros((bq, 1), F32), jnp.zeros((bq, LANES), F32)),) * nh

        @pl.when(i == 0)
        def _():
            finish(walk([(0, True)], zero))

        @pl.when(i > 0)
        def _():
            state = lax.fori_loop(ks_ref[hp, i], i - 1, lambda kb, st: walk([(kb, False)], st), zero)
            finish(walk([(i - 1, False), (i, True)], state))

    return pl.pallas_call(
        body, name="sb_bwd",
        grid_spec=pltpu.PrefetchScalarGridSpec(
            num_scalar_prefetch=1, grid=(SB_HEADS // nh, nq),
            in_specs=[pl.BlockSpec((bq, width), lambda h, i, ks: (i, h)),
                      pl.BlockSpec((s, width), lambda h, i, ks: (0, h), pipeline_mode=pl.Buffered(1)),
                      pl.BlockSpec((s, width), lambda h, i, ks: (0, h), pipeline_mode=pl.Buffered(1)),
                      pl.BlockSpec((bq, width), lambda h, i, ks: (i, h)),
                      pl.BlockSpec((bq, width), lambda h, i, ks: (i, h))],
            out_specs=[pl.BlockSpec((bq, width), lambda h, i, ks: (i, h)),
                       pl.BlockSpec((s, width), lambda h, i, ks: (0, h), pipeline_mode=pl.Buffered(1)),
                       pl.BlockSpec((s, width), lambda h, i, ks: (0, h), pipeline_mode=pl.Buffered(1))]),
        out_shape=[jax.ShapeDtypeStruct((s, SB_WIDTH), BF16), jax.ShapeDtypeStruct((s, SB_WIDTH), F32),
                   jax.ShapeDtypeStruct((s, SB_WIDTH), F32)],
        compiler_params=_params(("arbitrary", "arbitrary"), VMEM_LIMIT),
    )(kstart, q, k, v, do, rfin)


def _mla_bwd_call(q, k, v, do, o, lse):
    s = q.shape[0]
    bq = _row_tile(s, MLA_BWD_BQ)
    bk = _row_tile(s, MLA_BWD_BK)
    nq = s // bq
    assert bk % bq == 0
    step = min(bk // 2, MLA_BWD_STEP)
    nsub = bk // step
    assert nsub % 2 == 0

    def body(q_ref, k_ref, v_ref, do_ref, o_ref, lse_ref, dq_ref, dk_ref, dv_ref, dom_ref, s_ref, dp_ref, pb_ref,
             ds_ref):
        i = pl.program_id(1)

        @pl.when(i == 0)
        def _():
            dk_ref[...] = jnp.zeros_like(dk_ref)
            dv_ref[...] = jnp.zeros_like(dv_ref)

        lane = lax.broadcasted_iota(I32, (bq, LANES), 1)
        row = lax.broadcasted_iota(I32, (step, bq), 1)
        col = lax.broadcasted_iota(I32, (step, bq), 0)
        n_full = (i * bq) // bk
        do2 = do_ref[...]
        prod = do2.astype(F32) * o_ref[...]
        ones = jnp.ones((8, LANES), BF16)
        deltas, lses = [], []
        for hh in range(2):
            head = _head_mask(lane, hh)
            dom_ref[hh] = do2 * head.astype(BF16)
            part = prod * head
            hi = part.astype(BF16)
            lo = (part - hi.astype(F32)).astype(BF16)
            deltas.append((_dot_nt(ones, hi) + _dot_nt(ones, lo))[0:1])
            lses.append(lse_ref[hh:hh + 1, :])

        def keys(g):
            return pl.ds(pl.multiple_of(g * step, step), step)

        def heads():
            return [(hh, slice(hh * HEAD_PAD, (hh + 1) * HEAD_PAD)) for hh in range(2)]

        def put_products(g, slot, qlo=0):
            vblk = v_ref[keys(g), :]
            for hh, cols in heads():
                s_ref[slot, hh, :, qlo:] = _dot_nt(k_ref[keys(g), cols], q_ref[qlo:, cols])
                dp_ref[slot, hh, :, qlo:] = _dot_nt(vblk, dom_ref[hh, qlo:, :])

        def add_grads(dqs, g, slot, qlo=0):
            rows = keys(g)
            new, dv_parts = [], []
            for hh, cols in heads():
                ds = ds_ref[slot, hh, :, qlo:]
                dk_ref[rows, cols] += _dot(ds, q_ref[qlo:, cols])
                dv_parts.append(_dot(pb_ref[slot, hh, :, qlo:], dom_ref[hh, qlo:, :]))
                upd = dqs[hh][:, qlo:] + _dot_tn(k_ref[rows, cols], ds)
                new.append(upd if qlo == 0 else jnp.concatenate([dqs[hh][:, :qlo], upd], axis=1))
            dv_ref[rows, :] += dv_parts[0] + dv_parts[1]
            return tuple(new)

        def substep(g, slot, dqs, masked, prefetch, qlo=0, next_qlo=0, prev_qlo=0):
            if prefetch:
                put_products(g + 1, 1 - slot, next_qlo)
            dqs = add_grads(dqs, jnp.maximum(g - 1, 0), 1 - slot, prev_qlo)
            for hh, _ in heads():
                p = jnp.exp2(s_ref[slot, hh, :, qlo:] - lses[hh][:, qlo:])
                if masked:
                    p = jnp.where(col[:, qlo:] + g * step <= row[:, qlo:] + i * bq, p, 0.0)
                ds_ref[slot, hh, :, qlo:] = (p * (dp_ref[slot, hh, :, qlo:] - deltas[hh][:, qlo:])).astype(BF16)
                pb_ref[slot, hh, :, qlo:] = p.astype(BF16)
            return dqs

        def first_query(t, masked):
            return t * step if (masked and bk == bq and 0 <= t < nsub) else 0

        def chunk(kb, dqs, masked):
            for t in range(nsub):
                last = masked and t == nsub - 1
                dqs = substep(nsub * kb + t, t % 2, dqs, masked, not last, first_query(t, masked),
                              first_query(t + 1, masked), first_query(t - 1, masked))
            return dqs

        ds_ref[1] = jnp.zeros_like(ds_ref[1])
        pb_ref[1] = jnp.zeros_like(pb_ref[1])
        put_products(0, 0)
        zero = jnp.zeros((HEAD_PAD, bq), F32)
        dqs = lax.fori_loop(0, n_full, lambda kb, dqs: chunk(kb, dqs, False), (zero, zero))
        dqs = chunk(n_full, dqs, True)
        dqs = add_grads(dqs, nsub * n_full + nsub - 1, 1, first_query(nsub - 1, True))
        dq_ref[:, :HEAD_PAD] = dqs[0].T * MLA_SCALE
        dq_ref[:, HEAD_PAD:] = dqs[1].T * MLA_SCALE

    return pl.pallas_call(
        body, name="mla_bwd",
        grid=(4, nq),
        in_specs=[pl.BlockSpec((bq, 2 * HEAD_PAD), lambda h, i: (i, h)),
                  pl.BlockSpec((s, 2 * HEAD_PAD), lambda h, i: (0, h)),
                  pl.BlockSpec((s, LANES), lambda h, i: (0, h)),
                  pl.BlockSpec((bq, LANES), lambda h, i: (i, h)),
                  pl.BlockSpec((bq, LANES), lambda h, i: (i, h)),
                  pl.BlockSpec((None, 8, bq), lambda h, i: (h, 0, i))],
        out_specs=[pl.BlockSpec((bq, 2 * HEAD_PAD), lambda h, i: (i, h)),
                   pl.BlockSpec((s, 2 * HEAD_PAD), lambda h, i: (0, h), pipeline_mode=pl.Buffered(1)),
                   pl.BlockSpec((s, LANES), lambda h, i: (0, h), pipeline_mode=pl.Buffered(1))],
        out_shape=[jax.ShapeDtypeStruct((s, MLA_PAD_WIDTH), F32), jax.ShapeDtypeStruct((s, MLA_PAD_WIDTH), F32),
                   jax.ShapeDtypeStruct((s, MLA_WIDTH), F32)],
        scratch_shapes=[pltpu.VMEM((2, bq, LANES), BF16),
                        pltpu.VMEM((2, 2, step, bq), F32), pltpu.VMEM((2, 2, step, bq), F32),
                        pltpu.VMEM((2, 2, step, bq), BF16), pltpu.VMEM((2, 2, step, bq), BF16)],
        compiler_params=_params(("arbitrary", "arbitrary"), VMEM_LIMIT),
    )(q, k, v, do, o, lse)


def _rms_bwd(d_out, inp, r, weight, n):
    normed = inp * r
    gw = d_out * weight
    d_in = r * (gw - normed * (jnp.sum(gw * normed, axis=-1, keepdims=True) * (1.0 / n)))
    return d_in, d_out * normed


def _mla_prep_bwd_call(dq, dk, dv, q0, k0, cqn, ckvn, c_q, c_kv, cos_t, sin_t,
                       q_lora_norm, kv_lora_norm, qhn_pad, khn_pad, w_uq_bf, w_uk_bf, w_uv_bf):
    s = dq.shape[0]
    tm = _row_tile(s, 512)

    def body(dq_ref, dk_ref, dv_ref, q0_ref, k0_ref, cqn_ref, ckvn_ref, cq_ref, ckv_ref,
             cos_ref, sin_ref, qln_ref, kvln_ref, qhn_ref, khn_ref, wuq_ref, wuk_ref, wuv_ref,
             dcq_ref, dckv_ref, dkr_ref, gwuq_ref, gwuk_ref, gwuv_ref, gqln_ref, gkvln_ref, gqhn_ref, gkhn_ref,
             dq0_ref, dk0_ref, tmp_ref):
        @pl.when(pl.program_id(0) == 0)
        def _():
            for ref in (gwuq_ref, gwuk_ref, gwuv_ref, gqln_ref, gkvln_ref, gqhn_ref, gkhn_ref):
                ref[...] = jnp.zeros_like(ref)

        cos_t, sin_t = cos_ref[...], sin_ref[...]
        lane = lax.broadcasted_iota(I32, (tm, LANES), 1)
        rope_lanes = jnp.logical_or(lane < ROPE_HALF,
                                    jnp.logical_and(lane >= HALF_LANES, lane < HALF_LANES + ROPE_HALF))
        heads = [slice(h * HEAD_PAD, (h + 1) * HEAD_PAD) for h in range(MLA_HEADS)]

        def head_norm_bwd(d_ref, x0_ref, w_ref, out_ref, scale):
            w = w_ref[...]
            inv = [lax.rsqrt(jnp.sum(x0_ref[:, cols] * x0_ref[:, cols], axis=-1, keepdims=True)
                             * (1.0 / MLA_QK_DIM) + EPS) for cols in heads]
            for cols in heads:
                tmp_ref[:, cols] = _rope_adjoint(d_ref[:, cols] * scale, cos_t, sin_t)
            dots = [jnp.sum(tmp_ref[:, cols] * w * (x0_ref[:, cols] * r), axis=-1, keepdims=True)
                    for cols, r in zip(heads, inv)]
            g_w = jnp.zeros((1, LANES), F32)
            rope_sum = jnp.zeros((tm, LANES), F32)
            for cols, r, dot in zip(heads, inv, dots):
                normed = x0_ref[:, cols] * r
                d_n = tmp_ref[:, cols]
                d_x0 = r * (d_n * w - normed * (dot * (1.0 / MLA_QK_DIM)))
                out_ref[:, cols] = d_x0.astype(BF16)
                g_w = g_w + jnp.sum(d_n * normed, axis=0, keepdims=True)
                rope_sum = rope_sum + jnp.where(rope_lanes, d_x0, 0.0)
            return g_w, rope_sum

        g_qhn, _ = head_norm_bwd(dq_ref, q0_ref, qhn_ref, dq0_ref, 1.0)
        g_khn, d_kr = head_norm_bwd(dk_ref, k0_ref, khn_ref, dk0_ref, LN2)
        cqn, ckvn = cqn_ref[...], ckvn_ref[...]
        d_q0b, d_k0b, dvb = dq0_ref[...], dk0_ref[...], dv_ref[...].astype(BF16)
        d_cqn = _dot_nt(d_q0b, wuq_ref[...])
        gwuq_ref[...] += _dot_tn(cqn, d_q0b)
        d_ckvn = _dot_nt(d_k0b, wuk_ref[...]) + _dot_nt(dvb, wuv_ref[...])
        gwuk_ref[...] += _dot_tn(ckvn, d_k0b)
        gwuv_ref[...] += _dot_tn(ckvn, dvb)
        gqhn_ref[...] += g_qhn
        gkhn_ref[...] += g_khn
        dkr_ref[...] = d_kr.astype(BF16)
        cq = cq_ref[...]
        rcq = lax.rsqrt(jnp.mean(cq * cq, axis=-1, keepdims=True) + EPS)
        d_cq, gl = _rms_bwd(d_cqn, cq, rcq, qln_ref[...], Q_LORA_RANK)
        dcq_ref[...] = d_cq.astype(BF16)
        gqln_ref[...] += jnp.sum(gl, axis=0, keepdims=True)
        ckv = ckv_ref[...]
        rckv = lax.rsqrt(jnp.mean(ckv * ckv, axis=-1, keepdims=True) + EPS)
        d_ckv, gl = _rms_bwd(d_ckvn, ckv, rckv, kvln_ref[...], KV_LORA_RANK)
        dckv_ref[...] = d_ckv.astype(BF16)
        gkvln_ref[...] += jnp.sum(gl, axis=0, keepdims=True)

    return pl.pallas_call(
        body, name="mla_prep_bwd",
        grid=(s // tm,),
        in_specs=[_rows(tm, MLA_PAD_WIDTH), _rows(tm, MLA_PAD_WIDTH), _rows(tm, MLA_WIDTH),
                  _rows(tm, MLA_PAD_WIDTH), _rows(tm, MLA_PAD_WIDTH),
                  _rows(tm, Q_LORA_RANK), _rows(tm, KV_LORA_RANK), _rows(tm, Q_LORA_RANK), _rows(tm, KV_LORA_RANK),
                  _rows(tm, LANES), _rows(tm, LANES),
                  _full((1, Q_LORA_RANK)), _full((1, KV_LORA_RANK)), _full((1, LANES)), _full((1, LANES)),
                  _full((Q_LORA_RANK, MLA_PAD_WIDTH)), _full((KV_LORA_RANK, MLA_PAD_WIDTH)),
                  _full((KV_LORA_RANK, MLA_WIDTH))],
        out_specs=[_rows(tm, Q_LORA_RANK), _rows(tm, KV_LORA_RANK), _rows(tm, LANES),
                   _full((Q_LORA_RANK, MLA_PAD_WIDTH)), _full((KV_LORA_RANK, MLA_PAD_WIDTH)),
                   _full((KV_LORA_RANK, MLA_WIDTH)),
                   _full((1, Q_LORA_RANK)), _full((1, KV_LORA_RANK)), _full((1, LANES)), _full((1, LANES))],
        out_shape=[jax.ShapeDtypeStruct((s, Q_LORA_RANK), BF16), jax.ShapeDtypeStruct((s, KV_LORA_RANK), BF16),
                   jax.ShapeDtypeStruct((s, LANES), BF16),
                   jax.ShapeDtypeStruct((Q_LORA_RANK, MLA_PAD_WIDTH), F32),
                   jax.ShapeDtypeStruct((KV_LORA_RANK, MLA_PAD_WIDTH), F32),
                   jax.ShapeDtypeStruct((KV_LORA_RANK, MLA_WIDTH), F32),
                   jax.ShapeDtypeStruct((1, Q_LORA_RANK), F32), jax.ShapeDtypeStruct((1, KV_LORA_RANK), F32),
                   jax.ShapeDtypeStruct((1, LANES), F32), jax.ShapeDtypeStruct((1, LANES), F32)],
        scratch_shapes=[pltpu.VMEM((tm, MLA_PAD_WIDTH), BF16), pltpu.VMEM((tm, MLA_PAD_WIDTH), BF16),
                        pltpu.VMEM((tm, MLA_PAD_WIDTH), F32)],
        compiler_params=_params(("arbitrary",), VMEM_LIMIT),
    )(dq, dk, dv, q0, k0, cqn, ckvn, c_q, c_kv, cos_t, sin_t,
      q_lora_norm, kv_lora_norm, qhn_pad, khn_pad, w_uq_bf, w_uk_bf, w_uv_bf)


def _dh_call(pieces, hb, x, dy, ada, norm_w, w_in_bf):
    s = x.shape[0]
    tm = _row_tile(s, 512)
    widths = [p.shape[1] for p in pieces]
    offsets = [sum(widths[:j]) for j in range(len(widths))]
    assert offsets[-1] + widths[-1] == IN_COLS_PAD
    n = len(pieces)

    def body(*refs):
        p_refs = refs[:n]
        (hb_ref, x_ref, dy_ref, sh_ref, sc_ref, nw_ref, w_ref, gx_ref, gw_ref, dsh_ref, dsc_ref, gnw_ref,
         dp_ref) = refs[n:]

        @pl.when(pl.program_id(0) == 0)
        def _():
            gw_ref[...] = jnp.zeros_like(gw_ref)
            dsh_ref[...] = jnp.zeros_like(dsh_ref)
            dsc_ref[...] = jnp.zeros_like(dsc_ref)
            gnw_ref[...] = jnp.zeros_like(gnw_ref)

        for p_ref, c0, width in zip(p_refs, offsets, widths):
            dp_ref[:, c0:c0 + width] = p_ref[...].astype(BF16)
        gw_ref[...] += _dot_tn(hb_ref[...], dp_ref[...])
        dh = _dot_nt(dp_ref[...], w_ref[...])
        xx = x_ref[...]
        r0 = lax.rsqrt(jnp.mean(xx * xx, axis=-1, keepdims=True) + EPS)
        xn = xx * r0
        nw = nw_ref[...]
        dsh_ref[...] += jnp.sum(dh, axis=0, keepdims=True)
        dsc_ref[...] += jnp.sum(dh * (xn * nw), axis=0, keepdims=True)
        dn = dh * (1.0 + sc_ref[...])
        gnw_ref[...] += jnp.sum(dn * xn, axis=0, keepdims=True)
        dxn = dn * nw
        gx_ref[...] = dy_ref[...] + r0 * (dxn - xn * jnp.mean(dxn * xn, axis=-1, keepdims=True))

    return pl.pallas_call(
        body, name="in_proj_bwd",
        grid=(s // tm,),
        in_specs=[_rows(tm, w) for w in widths]
        + [_rows(tm, D_MODEL), _rows(tm, D_MODEL), _rows(tm, D_MODEL),
           _ada_part(0), _ada_part(1), _full((1, D_MODEL)),
           pl.BlockSpec((D_MODEL, IN_COLS_PAD), lambda i: (0, 0), pipeline_mode=pl.Buffered(1))],
        out_specs=[_rows(tm, D_MODEL),
                   pl.BlockSpec((D_MODEL, IN_COLS_PAD), lambda i: (0, 0), pipeline_mode=pl.Buffered(1)),
                   _full((1, D_MODEL)), _full((1, D_MODEL)), _full((1, D_MODEL))],
        out_shape=[jax.ShapeDtypeStruct((s, D_MODEL), F32), jax.ShapeDtypeStruct((D_MODEL, IN_COLS_PAD), F32),
                   jax.ShapeDtypeStruct((1, D_MODEL), F32), jax.ShapeDtypeStruct((1, D_MODEL), F32),
                   jax.ShapeDtypeStruct((1, D_MODEL), F32)],
        scratch_shapes=[pltpu.VMEM((tm, IN_COLS_PAD), BF16)],
        compiler_params=_params(("arbitrary",), VMEM_LIMIT),
    )(*pieces, hb, x, dy, ada, ada, norm_w, w_in_bf)


def _adamw(g, w, m, v):
    m = ADAM_B1 * m + (1.0 - ADAM_B1) * g
    v = ADAM_B2 * v + (1.0 - ADAM_B2) * (g * g)
    m_hat = m / (1.0 - ADAM_B1 ** ADAM_STEP)
    v_hat = v / (1.0 - ADAM_B2 ** ADAM_STEP)
    delta = -ADAM_LR * (m_hat / (jnp.sqrt(v_hat) + ADAM_EPS) + ADAM_WD * w)
    return delta, m, v


def _adam_shard_call(name, own, sib, w, m, v):
    r, c = w.shape
    tr = r if r <= 512 else 256

    def body(own_ref, sib_ref, w_ref, m_ref, v_ref, g_ref, d_ref, nm_ref, nv_ref):
        a = ((own_ref[0].astype(F32) + own_ref[1].astype(F32)) + own_ref[2].astype(F32)) + own_ref[3].astype(F32)
        b = ((sib_ref[0].astype(F32) + sib_ref[1].astype(F32)) + sib_ref[2].astype(F32)) + sib_ref[3].astype(F32)
        g = a + b
        g_ref[...] = g
        d_ref[...], nm_ref[...], nv_ref[...] = _adamw(g, w_ref[...], m_ref[...], v_ref[...])

    part = pl.BlockSpec((4, tr, c), lambda i: (0, i, 0))
    blk = pl.BlockSpec((tr, c), lambda i: (i, 0))
    return pl.pallas_call(
        body, name=name,
        grid=(r // tr,),
        in_specs=[part, part, blk, blk, blk],
        out_specs=[blk] * 4,
        out_shape=[jax.ShapeDtypeStruct((r, c), F32)] * 4,
        compiler_params=_params(("arbitrary",), VMEM_LIMIT),
    )(own, sib, w, m, v)


def _adam_ada_call(c_all, d_all, w, m, v):
    r, c = w.shape
    tr = 256

    def body(c_ref, d_ref, w_ref, m_ref, v_ref, g_ref, dl_ref, nm_ref, nv_ref):
        cc = c_ref[...]
        sc = cc * _sigmoid(cc)
        dd = d_ref[...]
        sc_hi = sc.astype(BF16)
        sc_lo = (sc - sc_hi.astype(F32)).astype(BF16)
        dd_hi = dd.astype(BF16)
        dd_lo = (dd - dd_hi.astype(F32)).astype(BF16)
        g = _dot_tn(sc_hi, dd_hi) + (_dot_tn(sc_hi, dd_lo) + _dot_tn(sc_lo, dd_hi))
        g_ref[...] = g
        dl_ref[...], nm_ref[...], nv_ref[...] = _adamw(g, w_ref[...], m_ref[...], v_ref[...])

    blk = pl.BlockSpec((tr, c), lambda i: (i, 0))
    return pl.pallas_call(
        body, name="adam_w_ada",
        grid=(r // tr,),
        in_specs=[pl.BlockSpec((16, tr), lambda i: (0, i)), pl.BlockSpec((16, c), lambda i: (0, 0)), blk, blk, blk],
        out_specs=[blk] * 4,
        out_shape=[jax.ShapeDtypeStruct((r, c), F32)] * 4,
        compiler_params=_params(("arbitrary",), VMEM_LIMIT),
    )(c_all, d_all, w, m, v)


def _adam_vectors_call(packs, offsets, vectors):
    nv = len(vectors)

    def body(*refs):
        p_ref, ins, outs = refs[0], refs[1:1 + 3 * nv], refs[1 + 3 * nv:]
        for j, off in enumerate(offsets):
            n = ins[3 * j].shape[1]
            span = -(-n // LANES) * LANES
            g = p_ref[0, :, off:off + span]
            for b in range(1, 8):
                g = g + p_ref[b, :, off:off + span]
            g = g[:, :n]
            outs[j][...] = g
            outs[nv + j][...], outs[2 * nv + j][...], outs[3 * nv + j][...] = _adamw(
                g, ins[3 * j][...], ins[3 * j + 1][...], ins[3 * j + 2][...])

    flat = [a for t in vectors for a in t]
    res = pl.pallas_call(
        body, name="adam_vectors",
        out_shape=[jax.ShapeDtypeStruct(t[0].shape, F32) for _ in range(4) for t in vectors],
    )(packs, *flat)
    return [res[k * nv:(k + 1) * nv] for k in range(4)]


ROPE_HALF = MLA_ROPE_DIM // 2
NOPE_A = MLA_NOPE_DIM - ROPE_HALF


def _zeros_like_lanes(t, n):
    return jnp.zeros(t.shape[:-1] + (n,), t.dtype)


def _to_head_lanes(t):
    nope, rope = t[..., :MLA_NOPE_DIM], t[..., MLA_NOPE_DIM:]
    return jnp.concatenate([rope[..., :ROPE_HALF], nope[..., :NOPE_A], rope[..., ROPE_HALF:], nope[..., NOPE_A:],
                            _zeros_like_lanes(t, HEAD_PAD - MLA_QK_DIM)], axis=-1)


def _from_head_lanes(t):
    return jnp.concatenate([t[..., ROPE_HALF:HALF_LANES], t[..., HALF_LANES + ROPE_HALF:MLA_QK_DIM],
                            t[..., :ROPE_HALF], t[..., HALF_LANES:HALF_LANES + ROPE_HALF]], axis=-1)


def _nope_to_head_lanes(t):
    return jnp.concatenate([_zeros_like_lanes(t, ROPE_HALF), t[..., :NOPE_A], _zeros_like_lanes(t, ROPE_HALF),
                            t[..., NOPE_A:], _zeros_like_lanes(t, HEAD_PAD - MLA_QK_DIM)], axis=-1)


def _rope_to_head_lanes(t):
    return jnp.concatenate([t[..., :ROPE_HALF], _zeros_like_lanes(t, HALF_LANES - ROPE_HALF), t[..., ROPE_HALF:],
                            _zeros_like_lanes(t, HALF_LANES - ROPE_HALF)], axis=-1)


def _rope_tables(positions):
    inv_freq = (ROPE_THETA ** (-jnp.arange(0, MLA_ROPE_DIM, 2, dtype=F32) / MLA_ROPE_DIM))[None]
    signed = _rope_to_head_lanes(jnp.concatenate([-inv_freq, inv_freq], axis=1))
    ang = positions.astype(F32)[:, None] * signed
    return jnp.cos(ang), jnp.sin(ang)


def _unshard_cols(g):
    return jnp.transpose(g, (1, 0, 2)).reshape(g.shape[1], 4 * g.shape[2])


def _shard_cols(g):
    r, c4 = g.shape
    return jnp.transpose(g.reshape(r, 4, c4 // 4), (1, 0, 2))


def kernel(x, c, positions, w_ada, b_ada, norm_w, w_in, q_lora_norm, w_uq, kv_lora_norm, w_ukv, q_head_norm, k_head_norm, w_out, loss_target, m_w_ada, m_b_ada, m_norm_w, m_w_in, m_q_lora_norm, m_w_uq, m_kv_lora_norm, m_w_ukv, m_q_head_norm, m_k_head_norm, m_w_out, v_w_ada, v_b_ada, v_norm_w, v_w_in, v_q_lora_norm, v_w_uq, v_kv_lora_norm, v_w_ukv, v_q_head_norm, v_k_head_norm, v_w_out):
    chip = 2 * lax.axis_index("x") + lax.axis_index("y")
    me8 = 2 * chip + lax.axis_index("c")
    ada_cols = w_ada.shape[2]
    c_all = _allgather_rows_call(c)[:, 0, :]
    ada_part = _ada_call(c_all, w_ada[0], lax.dynamic_slice_in_dim(b_ada, chip * ada_cols, ada_cols, axis=1))
    ada_g, win_g, wuq_g, wukv_g, wout_g = _gather_call(
        [ada_part[None]] + [w.astype(BF16) for w in (w_in, w_uq, w_ukv, w_out)], [False, True, True, True, True])
    ada = lax.dynamic_slice_in_dim(ada_g, me8, 1, axis=1).reshape(1, 4 * ada_cols)
    (sq_sum, grad_x, g_w_in, g_w_uq, g_w_ukv, g_w_out, d_ada, g_norm_w, g_qln, g_kvln, g_qhn, g_khn) = _local_step(
        x[0], ada, positions[0], loss_target[0], norm_w, win_g,
        q_lora_norm, _unshard_cols(wuq_g), kv_lora_norm, _unshard_cols(wukv_g), q_head_norm, k_head_norm,
        wout_g.reshape(D_MODEL, D_MODEL))

    grads = [g.astype(BF16) for g in (g_w_in, _shard_cols(g_w_uq), _shard_cols(g_w_ukv),
                                      g_w_out.reshape(4, D_MODEL // 4, D_MODEL))]
    pieces = [d_ada, g_norm_w, g_qln, g_kvln, g_qhn, g_khn, (0.5 * sq_sum / D_MODEL).reshape(1, 1)]
    spans = [-(-p.shape[1] // LANES) * LANES for p in pieces]
    starts = [sum(spans[:j]) for j in range(len(spans))]
    small = jnp.concatenate([jnp.pad(p, ((0, 0), (0, sp - p.shape[1]))) for p, sp in zip(pieces, spans)], axis=1)
    own, sib, packs = _exchange_call(grads, small)
    loss = jnp.sum(packs[:, 0, starts[-1]])

    names = ["adam_w_in", "adam_w_uq", "adam_w_ukv", "adam_w_out"]
    shard_w = [(w_in, m_w_in, v_w_in), (w_uq, m_w_uq, v_w_uq), (w_ukv, m_w_ukv, v_w_ukv),
               (w_out, m_w_out, v_w_out)]
    res = {}
    for name, o_g, s_g, (w, m, v) in zip(names, own, sib, shard_w):
        res[name] = _adam_shard_call(name, o_g, s_g, w[0], m[0], v[0])
    d_all = lax.dynamic_slice_in_dim(packs[:, 0, :], starts[0] + chip * ada_cols, ada_cols, axis=1)
    res_ada = _adam_ada_call(jnp.pad(c_all, ((0, 8), (0, 0))), jnp.pad(d_all, ((0, 8), (0, 0))),
                             w_ada[0], m_w_ada[0], v_w_ada[0])
    vectors = [(b_ada, m_b_ada, v_b_ada), (norm_w, m_norm_w, v_norm_w), (q_lora_norm, m_q_lora_norm, v_q_lora_norm),
               (kv_lora_norm, m_kv_lora_norm, v_kv_lora_norm), (q_head_norm, m_q_head_norm, v_q_head_norm),
               (k_head_norm, m_k_head_norm, v_k_head_norm)]
    vec_out = _adam_vectors_call(packs, starts[:len(vectors)], vectors)

    def ordered(kind):
        big = lambda name: res[name][kind][None]
        return [res_ada[kind][None], vec_out[kind][0], vec_out[kind][1], big("adam_w_in"), vec_out[kind][2],
                big("adam_w_uq"), vec_out[kind][3], big("adam_w_ukv"), vec_out[kind][4], vec_out[kind][5],
                big("adam_w_out")]

    return (loss, grad_x[None], *ordered(0), *ordered(1), *ordered(2), *ordered(3))


def _local_step(x2, ada, positions, tgt, norm_w, w_in_shards, q_lora_norm, w_uq_full,
                kv_lora_norm, w_ukv_full, q_head_norm, k_head_norm, w_out_full):
    in_shard = w_in_shards.shape[2]
    ckv_tail = C_GM - 3 * in_shard
    assert 0 <= ckv_tail and ckv_tail + MLA_ROPE_DIM + MLA_WIDTH == in_shard
    last = w_in_shards[3]
    w_in_bf = jnp.concatenate(
        [w_in_shards[0], w_in_shards[1], w_in_shards[2], last[:, :ckv_tail], last[:, ckv_tail + MLA_ROPE_DIM:],
         _rope_to_head_lanes(last[:, ckv_tail:ckv_tail + MLA_ROPE_DIM])], axis=1).astype(BF16)
    w_uq_bf = _to_head_lanes(w_uq_full.reshape(Q_LORA_RANK, MLA_HEADS, MLA_QK_DIM)).reshape(
        Q_LORA_RANK, MLA_PAD_WIDTH).astype(BF16)
    w_ukv_heads = w_ukv_full.reshape(KV_LORA_RANK, MLA_HEADS, 2 * MLA_NOPE_DIM)
    w_uk_bf = _nope_to_head_lanes(w_ukv_heads[:, :, :MLA_NOPE_DIM]).reshape(KV_LORA_RANK, MLA_PAD_WIDTH).astype(BF16)
    w_uv_bf = w_ukv_heads[:, :, MLA_NOPE_DIM:].reshape(KV_LORA_RANK, MLA_WIDTH).astype(BF16)
    w_out_bf = w_out_full.astype(BF16)
    qhn_pad, khn_pad = _to_head_lanes(q_head_norm), _to_head_lanes(k_head_norm)
    cos_t, sin_t = _rope_tables(positions)

    hb, q_sb, k_sb, v_sb, g_sb, c_q, c_kv, g_mla, q_m, k_m, v_m, cqn, ckvn, q0, k0 = _pre_call(
        x2, ada, norm_w, w_in_bf, cos_t, sin_t, q_lora_norm, kv_lora_norm, qhn_pad, khn_pad,
        w_uq_bf, w_uk_bf, w_uv_bf)
    o_sb, r_sb, kstart = _sb_fwd_call(q_sb, k_sb, v_sb)
    o_mla, lse = _mla_fwd_call(q_m, k_m, v_m)
    do_sb, do_mla, dg_sb, dg_mla, dy, g_w_out, d_gate, sq = _out_call(
        o_sb, g_sb, o_mla, g_mla, x2, tgt, ada, w_out_bf)

    dq_sb, dk_sb, dv_sb = _sb_bwd_call(kstart, q_sb, k_sb, v_sb, do_sb, r_sb)
    dq_m, dk_m, dv_m = _mla_bwd_call(q_m, k_m, v_m, do_mla, o_mla, lse)
    (d_cq, d_ckv, d_kr, g_wuq_pad, g_wuk_pad, g_wuv, g_qln, g_kvln, g_qhn, g_khn) = _mla_prep_bwd_call(
        dq_m, dk_m, dv_m, q0, k0, cqn, ckvn, c_q, c_kv, cos_t, sin_t,
        q_lora_norm, kv_lora_norm, qhn_pad, khn_pad, w_uq_bf, w_uk_bf, w_uv_bf)
    grad_x, g_win_pad, d_shift, d_scale, g_norm_w = _dh_call(
        [dq_sb, dk_sb, dv_sb, dg_sb, d_cq, d_ckv, dg_mla, d_kr], hb, x2, dy, ada, norm_w, w_in_bf)

    g_kr = g_win_pad[:, C_KR:]
    g_last = jnp.concatenate([g_win_pad[:, 3 * in_shard:C_GM], g_kr[:, :ROPE_HALF],
                              g_kr[:, HALF_LANES:HALF_LANES + ROPE_HALF], g_win_pad[:, C_GM:C_KR]], axis=1)
    g_w_in = jnp.stack([g_win_pad[:, j * in_shard:(j + 1) * in_shard] for j in range(3)] + [g_last])
    g_w_uq = _from_head_lanes(g_wuq_pad.reshape(Q_LORA_RANK, MLA_HEADS, HEAD_PAD)).reshape(Q_LORA_RANK, -1)
    g_w_ukv = jnp.concatenate(
        [_from_head_lanes(g_wuk_pad.reshape(KV_LORA_RANK, MLA_HEADS, HEAD_PAD))[:, :, :MLA_NOPE_DIM],
         g_wuv.reshape(KV_LORA_RANK, MLA_HEADS, MLA_NOPE_DIM)], axis=2).reshape(KV_LORA_RANK, -1)
    d_ada = jnp.concatenate([d_shift, d_scale, d_gate], axis=1)
    return (jnp.sum(sq), grad_x, g_w_in, g_w_uq, g_w_ukv, g_w_out, d_ada, g_norm_w, g_qln, g_kvln,
            _from_head_lanes(g_qhn), _from_head_lanes(g_khn))
```

```python
import math

import jax
import jax.numpy as jnp
from jax import lax
from jax.experimental import pallas as pl
from jax.experimental.pallas import tpu as pltpu

F32 = jnp.float32
BF16 = jnp.bfloat16
I32 = jnp.int32

D_MODEL = 1024
SB_HEADS = 8
SB_WIDTH = 512
MLA_HEADS = 8
MLA_QK_DIM = 96
MLA_NOPE_DIM = 64
MLA_ROPE_DIM = 32
MLA_WIDTH = 512
Q_LORA_RANK = 384
KV_LORA_RANK = 256
ROPE_THETA = 10000.0
EPS = 1e-6
LANES = 128
HALF_LANES = LANES // 2
HEAD_PAD = 128
MLA_PAD_WIDTH = MLA_HEADS * HEAD_PAD

C_Q, C_K, C_V, C_G = 0, 512, 1024, 1536
C_CQ, C_CKV, C_GM, C_KR = 2048, 2432, 2688, 3200
IN_COLS_PAD = 3328

ADAM_LR = 0.001
ADAM_B1 = 0.9
ADAM_B2 = 0.999
ADAM_EPS = 1e-08
ADAM_WD = 0.01
ADAM_STEP = 10

SB_SCALE = 0.125
SB_GROUP = 4
MLA_SCALE = 1.0 / math.sqrt(MLA_QK_DIM)
LN2 = math.log(2.0)
MLA_SCALE_LOG2 = MLA_SCALE / LN2
MLA_BQ = 1024
MLA_BWD_BQ = 1024
MLA_BK = 1024
MLA_BWD_BK = 1024
MLA_STEP = 512
MLA_BWD_STEP = 256
SB_DEAD = -104.0
MASK_NEG = -1e30

VMEM_LIMIT = 56 * 1024 * 1024
MESH = pl.DeviceIdType.MESH


def _dot(a, b):
    return jnp.dot(a, b, preferred_element_type=F32)


def _dot_nt(a, b):
    return lax.dot_general(a, b, (((1,), (1,)), ((), ())), preferred_element_type=F32)


def _dot_tn(a, b):
    return lax.dot_general(a, b, (((0,), (0,)), ((), ())), preferred_element_type=F32)


def _sigmoid(x):
    return 1.0 / (1.0 + jnp.exp(-x))


def _split_dot(a, m):
    hi = a.astype(BF16)
    lo = (a - hi.astype(F32)).astype(BF16)
    return _dot(hi, m) + _dot(lo, m)


def _params(sem, vmem=None):
    return pltpu.CompilerParams(dimension_semantics=sem, vmem_limit_bytes=vmem)


def _row_tile(s, want):
    return min(want, s)


def _hbm_spec():
    return pl.BlockSpec(memory_space=pltpu.HBM)


def _allgather_rows_call(row):
    def body(in_ref, out_ref, send_sems, recv_sems, loc_sem):
        x, y, c = lax.axis_index("x"), lax.axis_index("y"), lax.axis_index("c")
        flips = [(fx, fy, fc) for fx in (0, 1) for fy in (0, 1) for fc in (0, 1)][1:]

        def peer(r):
            fx, fy, fc = flips[r]
            return ((1 - x) if fx else x, (1 - y) if fy else y, (1 - c) if fc else c)

        def copy(r, slot):
            return pltpu.make_async_remote_copy(
                src_ref=in_ref, dst_ref=out_ref.at[slot], send_sem=send_sems.at[r], recv_sem=recv_sems.at[r],
                device_id=peer(r), device_id_type=MESH)

        local = pltpu.make_async_copy(in_ref, out_ref.at[4 * x + 2 * y + c], loc_sem)
        local.start()
        sends = [copy(r, 4 * x + 2 * y + c) for r in range(7)]
        for cp in sends:
            cp.start()
        for r in range(7):
            px, py, pc = peer(r)
            copy(r, 4 * px + 2 * py + pc).wait_recv()
        for cp in sends:
            cp.wait_send()
        local.wait()

    return pl.pallas_call(
        body, name="gather_rows",
        out_shape=jax.ShapeDtypeStruct((8,) + row.shape, row.dtype),
        in_specs=[_hbm_spec()], out_specs=_hbm_spec(),
        scratch_shapes=[pltpu.SemaphoreType.DMA((7,)), pltpu.SemaphoreType.DMA((7,)), pltpu.SemaphoreType.DMA],
    )(row)


def _gather_call(shards, split):
    n = len(shards)
    halves = [s.shape[1] // 2 for s in shards]

    def body(*refs):
        ins, outs = refs[:n], refs[n:2 * n]
        ici_send, ici_recv, d2d_send, d2d_recv, loc_sems = refs[2 * n:]
        x, y, c = lax.axis_index("x"), lax.axis_index("y"), lax.axis_index("c")
        me = 2 * x + y
        peers = [(1 - x, y), (x, 1 - y), (1 - x, 1 - y)]

        def rows(a, which):
            return pl.ds(pl.multiple_of(which * halves[a], 16), halves[a])

        def ici(a, j, slot):
            px, py = peers[j]
            src, dst = ins[a].at[0], outs[a].at[slot]
            if split[a]:
                src, dst = src.at[rows(a, c)], dst.at[rows(a, c)]
            return pltpu.make_async_remote_copy(
                src_ref=src, dst_ref=dst,
                send_sem=ici_send.at[3 * a + j], recv_sem=ici_recv.at[3 * a + j],
                device_id=(px, py, c), device_id_type=MESH)

        def d2d(a, j, which):
            px, py = peers[j]
            piece = outs[a].at[2 * px + py, rows(a, which)]
            return pltpu.make_async_remote_copy(
                src_ref=piece, dst_ref=piece,
                send_sem=d2d_send.at[3 * a + j], recv_sem=d2d_recv.at[3 * a + j],
                device_id=(x, y, 1 - c), device_id_type=MESH)

        local = [pltpu.make_async_copy(ins[a].at[0], outs[a].at[me], loc_sems.at[a]) for a in range(n)]
        for cp in local:
            cp.start()
        sends = [ici(a, j, me) for a in range(n) for j in range(3)]
        for cp in sends:
            cp.start()
        for a in range(n):
            for j in range(3):
                px, py = peers[j]
                ici(a, j, 2 * px + py).wait_recv()
                if split[a]:
                    cp = d2d(a, j, c)
                    cp.start()
                    sends.append(cp)
        for a in range(n):
            for j in range(3):
                if split[a]:
                    d2d(a, j, 1 - c).wait_recv()
        for cp in sends:
            cp.wait_send()
        for cp in local:
            cp.wait()

    return pl.pallas_call(
        body, name="gather_weights",
        out_shape=[jax.ShapeDtypeStruct((4,) + s.shape[1:], s.dtype) for s in shards],
        in_specs=[_hbm_spec() for _ in shards],
        out_specs=[_hbm_spec() for _ in shards],
        scratch_shapes=[pltpu.SemaphoreType.DMA((3 * n,)), pltpu.SemaphoreType.DMA((3 * n,)),
                        pltpu.SemaphoreType.DMA((3 * n,)), pltpu.SemaphoreType.DMA((3 * n,)),
                        pltpu.SemaphoreType.DMA((n,))],
    )(*shards)


def _swap_halves_call(grads):
    n = len(grads)
    halves = [g.shape[1] // 2 for g in grads]

    def body(*refs):
        g_in, hs, send_sems, recv_sems = refs[:n], refs[n:2 * n], refs[2 * n], refs[2 * n + 1]
        x, y, c = lax.axis_index("x"), lax.axis_index("y"), lax.axis_index("c")
        copies = []
        for a in range(n):
            theirs = pl.ds(pl.multiple_of((1 - c) * halves[a], 16), halves[a])
            copies.append(pltpu.make_async_remote_copy(
                src_ref=g_in[a].at[:, theirs], dst_ref=hs[a], send_sem=send_sems.at[a], recv_sem=recv_sems.at[a],
                device_id=(x, y, 1 - c), device_id_type=MESH))
        for cp in copies:
            cp.start()
        for cp in copies:
            cp.wait()

    return pl.pallas_call(
        body, name="swap_halves",
        out_shape=[jax.ShapeDtypeStruct((4, h, g.shape[2]), g.dtype) for g, h in zip(grads, halves)],
        in_specs=[_hbm_spec() for _ in grads], out_specs=[_hbm_spec() for _ in grads],
        scratch_shapes=[pltpu.SemaphoreType.DMA((n,)), pltpu.SemaphoreType.DMA((n,))],
    )(*grads)


def _sum_halves_call(core, grads, halves):
    n = len(grads)

    def body(core_ref, *refs):
        for g_ref, h_ref, o_ref in zip(refs[:n], refs[n:2 * n], refs[2 * n:]):
            o_ref[...] = (g_ref[...].astype(F32) + h_ref[...].astype(F32)).astype(o_ref.dtype)

    whole = lambda h: pl.BlockSpec(h.shape, lambda i, core_ref: (0, 0, 0))
    return pl.pallas_call(
        body, name="sum_halves",
        grid_spec=pltpu.PrefetchScalarGridSpec(
            num_scalar_prefetch=1, grid=(1,),
            in_specs=[pl.BlockSpec(h.shape, lambda i, core_ref: (0, core_ref[0], 0)) for h in halves]
            + [whole(h) for h in halves],
            out_specs=[whole(h) for h in halves]),
        out_shape=[jax.ShapeDtypeStruct(h.shape, h.dtype) for h in halves],
        compiler_params=_params(("arbitrary",), VMEM_LIMIT),
    )(core, *grads, *halves)


def _exchange_call(chip_halves, small):
    n = len(chip_halves)
    halves = [h.shape[1] for h in chip_halves]

    def body(*refs):
        g_in, small_in = refs[:n], refs[n]
        parts, packs = refs[n + 1:2 * n + 1], refs[2 * n + 1]
        ici_send, ici_recv, d2d_send, d2d_recv, sm_send, sm_recv, loc_sems = refs[2 * n + 2:]
        x, y, c = lax.axis_index("x"), lax.axis_index("y"), lax.axis_index("c")
        me = 2 * x + y
        me8 = 4 * x + 2 * y + c
        sibling = (x, y, 1 - c)
        peers = [(1 - x, y), (x, 1 - y), (1 - x, 1 - y)]
        flips = [(fx, fy, fc) for fx in (0, 1) for fy in (0, 1) for fc in (0, 1)][1:]

        def rows(a, which):
            return pl.ds(pl.multiple_of(which * halves[a], 16), halves[a])

        def ici(a, j, src_slot, dst_slot):
            px, py = peers[j]
            return pltpu.make_async_remote_copy(
                src_ref=g_in[a].at[src_slot], dst_ref=parts[a].at[dst_slot, rows(a, c)],
                send_sem=ici_send.at[3 * a + j], recv_sem=ici_recv.at[3 * a + j],
                device_id=(px, py, c), device_id_type=MESH)

        def d2d(a, rel, chip, which):
            piece = parts[a].at[chip, rows(a, which)]
            return pltpu.make_async_remote_copy(
                src_ref=piece, dst_ref=piece,
                send_sem=d2d_send.at[4 * a + rel], recv_sem=d2d_recv.at[4 * a + rel],
                device_id=sibling, device_id_type=MESH)

        def flipped(r):
            fx, fy, fc = flips[r]
            return ((1 - x) if fx else x, (1 - y) if fy else y, (1 - c) if fc else c)

        def sm(r, slot):
            return pltpu.make_async_remote_copy(
                src_ref=small_in, dst_ref=packs.at[slot],
                send_sem=sm_send.at[r], recv_sem=sm_recv.at[r],
                device_id=flipped(r), device_id_type=MESH)

        def peer8(r):
            px, py, pc = flipped(r)
            return 4 * px + 2 * py + pc

        local = [pltpu.make_async_copy(g_in[a].at[me], parts[a].at[me, rows(a, c)], loc_sems.at[a])
                 for a in range(n)]
        local.append(pltpu.make_async_copy(small_in, packs.at[me8], loc_sems.at[n]))
        for cp in local:
            cp.start()
        sends = []
        for r in range(7):
            sends.append(sm(r, me8))
        for a in range(n):
            for j in range(3):
                px, py = peers[j]
                sends.append(ici(a, j, 2 * px + py, me))
        for cp in sends:
            cp.start()
        for a in range(n):
            local[a].wait()
            cp = d2d(a, 0, me, c)
            cp.start()
            sends.append(cp)
        for a in range(n):
            for j in range(3):
                px, py = peers[j]
                ici(a, j, me, 2 * px + py).wait_recv()
                cp = d2d(a, 1 + j, 2 * px + py, c)
                cp.start()
                sends.append(cp)
        for a in range(n):
            d2d(a, 0, me, 1 - c).wait_recv()
            for j in range(3):
                px, py = peers[j]
                d2d(a, 1 + j, 2 * px + py, 1 - c).wait_recv()
        for r in range(7):
            sm(r, peer8(r)).wait_recv()
        for cp in sends:
            cp.wait_send()
        local[n].wait()

    out_shape = ([jax.ShapeDtypeStruct((4, 2 * h.shape[1], h.shape[2]), h.dtype) for h in chip_halves]
                 + [jax.ShapeDtypeStruct((8,) + small.shape, small.dtype)])
    res = pl.pallas_call(
        body, name="exchange_grads",
        out_shape=out_shape,
        in_specs=[_hbm_spec() for _ in range(n + 1)],
        out_specs=[_hbm_spec() for _ in range(n + 1)],
        scratch_shapes=[pltpu.SemaphoreType.DMA((3 * n,)), pltpu.SemaphoreType.DMA((3 * n,)),
                        pltpu.SemaphoreType.DMA((4 * n,)), pltpu.SemaphoreType.DMA((4 * n,)),
                        pltpu.SemaphoreType.DMA((7,)), pltpu.SemaphoreType.DMA((7,)),
                        pltpu.SemaphoreType.DMA((n + 1,))],
    )(*chip_halves, small)
    return res[:n], res[n]


def _ada_call(c_all, w_ada_cols, b_ada_cols):
    def body(c_ref, w_ref, b_ref, o_ref):
        cc = c_ref[...]
        o_ref[...] = _dot((cc * _sigmoid(cc)).astype(BF16), w_ref[...].astype(BF16)) + b_ref[...]

    return pl.pallas_call(
        body, name="ada_fwd",
        out_shape=jax.ShapeDtypeStruct((c_all.shape[0], w_ada_cols.shape[1]), F32),
        compiler_params=pltpu.CompilerParams(vmem_limit_bytes=VMEM_LIMIT),
    )(c_all, w_ada_cols, b_ada_cols)


def _ada_part(j):
    return pl.BlockSpec((1, D_MODEL), lambda i: (0, j))


def _full(shape):
    return pl.BlockSpec(shape, lambda i: (0,) * len(shape))


def _rows(tm, width):
    return pl.BlockSpec((tm, width), lambda i: (i, 0))


def _rope(t, cos_t, sin_t):
    return t * cos_t + pltpu.roll(t, HALF_LANES, 1) * sin_t


def _rope_adjoint(d, cos_t, sin_t):
    return d * cos_t + pltpu.roll(d * sin_t, HALF_LANES, 1)


def _pre_call(x, ada, norm_w, w_in_bf, cos_t, sin_t, q_lora_norm, kv_lora_norm, qhn_pad, khn_pad,
              w_uq_bf, w_uk_bf, w_uv_bf):
    s = x.shape[0]
    tm = _row_tile(s, 512)
    out_defs = [(D_MODEL, BF16), (512, BF16), (512, BF16), (512, BF16), (512, F32),
                (Q_LORA_RANK, F32), (KV_LORA_RANK, F32), (512, F32),
                (MLA_PAD_WIDTH, BF16), (MLA_PAD_WIDTH, BF16), (MLA_WIDTH, BF16),
                (Q_LORA_RANK, BF16), (KV_LORA_RANK, BF16), (MLA_PAD_WIDTH, F32), (MLA_PAD_WIDTH, F32)]

    def body(x_ref, sh_ref, sc_ref, nw_ref, w_ref, cos_ref, sin_ref, qln_ref, kvln_ref, qhn_ref, khn_ref,
             wuq_ref, wuk_ref, wuv_ref,
             hb_ref, qsb_ref, ksb_ref, vsb_ref, gsb_ref, cq_ref, ckv_ref, gm_ref,
             q_ref, k_ref, v_ref, cqn_ref, ckvn_ref, q0_ref, k0_ref):
        xx = x_ref[...]
        r0 = lax.rsqrt(jnp.mean(xx * xx, axis=-1, keepdims=True) + EPS)
        hb = ((xx * r0 * nw_ref[...]) * (1.0 + sc_ref[...]) + sh_ref[...]).astype(BF16)
        hb_ref[...] = hb

        def proj(c0, width):
            return _dot(hb, w_ref[:, c0:c0 + width])

        cq = proj(C_CQ, Q_LORA_RANK)
        ckv = proj(C_CKV, KV_LORA_RANK)
        kr = proj(C_KR, LANES)
        cq_ref[...] = cq
        ckv_ref[...] = ckv
        cqn = (cq * lax.rsqrt(jnp.mean(cq * cq, axis=-1, keepdims=True) + EPS) * qln_ref[...]).astype(BF16)
        cqn_ref[...] = cqn
        ckvn = (ckv * lax.rsqrt(jnp.mean(ckv * ckv, axis=-1, keepdims=True) + EPS) * kvln_ref[...]).astype(BF16)
        ckvn_ref[...] = ckvn
        qsb_ref[...] = proj(C_Q, 512).astype(BF16)
        v_ref[...] = _dot(ckvn, wuv_ref[...]).astype(BF16)
        q0_ref[...] = _dot(cqn, wuq_ref[...])
        k0_ref[...] = _dot(ckvn, wuk_ref[...])
        ksb_ref[...] = proj(C_K, 512).astype(BF16)
        cos_t, sin_t = cos_ref[...], sin_ref[...]
        heads = [slice(h * HEAD_PAD, (h + 1) * HEAD_PAD) for h in range(MLA_HEADS)]
        for cols in heads:
            k0_ref[:, cols] = k0_ref[:, cols] + kr

        def inv_rms(ref):
            sums = [jnp.sum(ref[:, cols] * ref[:, cols], axis=-1, keepdims=True) for cols in heads]
            return [lax.rsqrt(t * (1.0 / MLA_QK_DIM) + EPS) for t in sums]

        rqs = inv_rms(q0_ref)
        vsb_ref[...] = proj(C_V, 512).astype(BF16)
        rks = inv_rms(k0_ref)
        gsb_ref[...] = proj(C_G, 512)
        for cols, rq, rk in zip(heads, rqs, rks):
            q_ref[:, cols] = (_rope(q0_ref[:, cols] * rq * qhn_ref[...], cos_t, sin_t) * MLA_SCALE_LOG2).astype(BF16)
            k_ref[:, cols] = _rope(k0_ref[:, cols] * rk * khn_ref[...], cos_t, sin_t).astype(BF16)
        gm_ref[...] = proj(C_GM, 512)

    return pl.pallas_call(
        body, name="pre_proj",
        grid=(s // tm,),
        in_specs=[_rows(tm, D_MODEL), _ada_part(0), _ada_part(1), _full((1, D_MODEL)),
                  pl.BlockSpec((D_MODEL, IN_COLS_PAD), lambda i: (0, 0), pipeline_mode=pl.Buffered(1)),
                  _rows(tm, LANES), _rows(tm, LANES),
                  _full((1, Q_LORA_RANK)), _full((1, KV_LORA_RANK)), _full((1, LANES)), _full((1, LANES)),
                  _full((Q_LORA_RANK, MLA_PAD_WIDTH)), _full((KV_LORA_RANK, MLA_PAD_WIDTH)),
                  _full((KV_LORA_RANK, MLA_WIDTH))],
        out_specs=[_rows(tm, w) for w, _ in out_defs],
        out_shape=[jax.ShapeDtypeStruct((s, w), dt) for w, dt in out_defs],
        compiler_params=_params(("arbitrary",), VMEM_LIMIT),
    )(x, ada, ada, norm_w, w_in_bf, cos_t, sin_t, q_lora_norm, kv_lora_norm, qhn_pad, khn_pad,
      w_uq_bf, w_uk_bf, w_uv_bf)


def _log_sigmoid_pair(z):
    ls = jnp.minimum(z, 0.0) - jnp.log(1.0 + jnp.exp(-jnp.abs(z)))
    return ls, ls - z


def _pair(hh):
    return slice((hh // 2) * LANES, (hh // 2 + 1) * LANES)


def _sb_fwd_call(q, k, v):
    s = q.shape[0]
    bq = _row_tile(s, 256)
    nq = s // bq
    nh = SB_GROUP
    width = nh * HALF_LANES

    def body(q_ref, k_ref, v_ref, o_ref, r_ref, ks_ref):
        hp, i = pl.program_id(0), pl.program_id(1)
        lane = lax.broadcasted_iota(I32, (bq, LANES), 1)
        row = lax.broadcasted_iota(I32, (bq, bq), 0)
        col = lax.broadcasted_iota(I32, (bq, bq), 1)
        strict = col < row
        later = jnp.where(row > col, 1.0, 0.0).astype(BF16)
        masks = [_head_mask(lane, hh).astype(BF16) for hh in range(2)]
        qms = [q_ref[:, _pair(hh)] * jnp.asarray(SB_SCALE, BF16) * masks[hh % 2] for hh in range(nh)]

        def walk(blocks, state):
            chains = [(kb, diagonal, hh) for kb, diagonal in blocks for hh in range(nh)]
            keys = lambda kb: pl.ds(pl.multiple_of(kb * bq, bq), bq)
            zs = [_dot_nt(qms[hh], k_ref[keys(kb), _pair(hh)]) for kb, _, hh in chains]
            pairs = []
            for z, (_, diagonal, _) in zip(zs, chains):
                ls, lk = _log_sigmoid_pair(z)
                pairs.append((ls, jnp.where(strict, lk, 0.0) if diagonal else lk))
            sums = [_split_dot(lk, later) for _, lk in pairs]
            runs = [st[0] for st in state]
            ws = []
            for (ls, lk), after, (_, diagonal, hh) in zip(pairs, sums, chains):
                w = jnp.exp(ls + (after + runs[hh]))
                ws.append((jnp.where(strict, w, 0.0) if diagonal else w).astype(BF16))
                runs[hh] = runs[hh] + jnp.sum(lk, axis=1, keepdims=True)
            accs = [st[1] for st in state]
            for w, (kb, _, hh) in zip(ws, chains):
                accs[hh] = accs[hh] + _dot(w, v_ref[keys(kb), _pair(hh)])
            return tuple(zip(runs, accs))

        def alive(state):
            top = jnp.max(state[0][0])
            for st in state[1:]:
                top = jnp.maximum(top, jnp.max(st[0]))
            return (top > SB_DEAD).astype(I32)

        def finish(state, first):
            ks_ref[hp, i] = first
            for pair in range(nh // 2):
                o_ref[:, _pair(2 * pair)] = jnp.where(lane < HALF_LANES, state[2 * pair][1], state[2 * pair + 1][1])
                r_ref[:, _pair(2 * pair)] = jnp.where(lane < HALF_LANES, state[2 * pair][0], state[2 * pair + 1][0])

        zero = ((jnp.zeros((bq, 1), F32), jnp.zeros((bq, LANES), F32)),) * nh

        @pl.when(i == 0)
        def _():
            finish(walk([(0, True)], zero), 0)

        @pl.when(i > 0)
        def _():
            state = walk([(i, True), (i - 1, False)], zero)

            def cond(carry):
                return jnp.logical_and(carry[0] >= 0, carry[1] > 0)

            def step(carry):
                state = walk([(carry[0], False)], carry[2])
                return carry[0] - 1, alive(state), state

            kb, _, state = lax.while_loop(cond, step, (i - 2, alive(state), state))
            finish(state, kb + 1)

    return pl.pallas_call(
        body, name="sb_fwd",
        grid=(SB_HEADS // nh, nq),
        in_specs=[pl.BlockSpec((bq, width), lambda h, i: (i, h)),
                  pl.BlockSpec((s, width), lambda h, i: (0, h)),
                  pl.BlockSpec((s, width), lambda h, i: (0, h))],
        out_specs=[pl.BlockSpec((bq, width), lambda h, i: (i, h)),
                   pl.BlockSpec((bq, width), lambda h, i: (i, h)),
                   pl.BlockSpec(memory_space=pltpu.SMEM)],
        out_shape=[jax.ShapeDtypeStruct((s, SB_WIDTH), F32), jax.ShapeDtypeStruct((s, SB_WIDTH), F32),
                   jax.ShapeDtypeStruct((SB_HEADS // nh, nq), I32)],
        compiler_params=_params(("arbitrary", "arbitrary"), VMEM_LIMIT),
    )(q, k, v)


def _mla_fwd_call(q, k, v):
    s = q.shape[0]
    bq = _row_tile(s, MLA_BQ)
    bk = _row_tile(s, MLA_BK)
    nq = s // bq
    assert bk % bq == 0
    step = min(bk // 2, MLA_STEP)
    nsub = bk // step
    assert nsub % 2 == 0

    def body(q_ref, k_ref, v_ref, o_ref, lse_ref, p_ref, s_ref):
        i = pl.program_id(1)
        lane = lax.broadcasted_iota(I32, (bq, LANES), 1)
        row = lax.broadcasted_iota(I32, (step, bq), 1)
        col = lax.broadcasted_iota(I32, (step, bq), 0)
        n_full = (i * bq) // bk

        def keys(g):
            return pl.ds(pl.multiple_of(g * step, step), step)

        def join(left, right, qlo):
            return right if qlo == 0 else jnp.concatenate([left[:, :qlo], right], axis=1)

        def put_scores(g, slot, qlo=0):
            for hh in range(2):
                cols = slice(hh * HEAD_PAD, (hh + 1) * HEAD_PAD)
                s_ref[slot, hh, :, qlo:] = _dot_nt(k_ref[keys(g), cols], q_ref[qlo:, cols])

        def add_pv(carry, g, slot, qlo=0):
            vblk = v_ref[keys(g), :]
            out = []
            for hh, (m, l, acc, alpha) in enumerate(carry):
                upd = alpha[:, qlo:] * acc[:, qlo:] + _dot_tn(vblk, p_ref[slot, hh, :, qlo:])
                out.append((m, l, join(acc, upd, qlo), alpha))
            return tuple(out)

        def substep(g, slot, carry, masked, prefetch, qlo=0, next_qlo=0, prev_qlo=0):
            if prefetch:
                put_scores(g + 1, 1 - slot, next_qlo)
            carry = add_pv(carry, jnp.maximum(g - 1, 0), 1 - slot, prev_qlo)
            new = []
            for hh in range(2):
                m, l, acc, _ = carry[hh]
                sc = s_ref[slot, hh, :, qlo:]
                if masked:
                    sc = jnp.where(col[:, qlo:] + g * step <= row[:, qlo:] + i * bq, sc, MASK_NEG)
                m_new = jnp.maximum(m[:, qlo:], jnp.max(sc, axis=0, keepdims=True))
                p = jnp.exp2(sc - m_new)
                alpha = jnp.exp2(m[:, qlo:] - m_new)
                l_new = alpha * l[:, qlo:] + jnp.sum(p, axis=0, keepdims=True)
                p_ref[slot, hh, :, qlo:] = p.astype(BF16)
                new.append((join(m, m_new, qlo), join(l, l_new, qlo), acc, join(jnp.ones_like(m), alpha, qlo)))
            return tuple(new)

        def first_query(t, masked):
            return t * step if (masked and bk == bq and 0 <= t < nsub) else 0

        def chunk(kb, carry, masked):
            for t in range(nsub):
                last = masked and t == nsub - 1
                carry = substep(nsub * kb + t, t % 2, carry, masked, not last, first_query(t, masked),
                                first_query(t + 1, masked), first_query(t - 1, masked))
            return carry

        p_ref[1] = jnp.zeros_like(p_ref[1])
        put_scores(0, 0)
        one = (jnp.full((1, bq), MASK_NEG, F32), jnp.zeros((1, bq), F32), jnp.zeros((LANES, bq), F32),
               jnp.ones((1, bq), F32))
        carry = lax.fori_loop(0, n_full, lambda kb, cr: chunk(kb, cr, False), (one, one))
        carry = chunk(n_full, carry, True)
        (m0, l0, a0, _), (m1, l1, a1, _) = add_pv(carry, nsub * n_full + nsub - 1, 1, first_query(nsub - 1, True))
        o_ref[...] = jnp.where(lane < HALF_LANES, (a0 / l0).T, (a1 / l1).T)
        sub = lax.broadcasted_iota(I32, (8, bq), 0)
        lse_ref[...] = jnp.where(sub == 0, m0 + jnp.log2(l0), jnp.where(sub == 1, m1 + jnp.log2(l1), 0.0))

    return pl.pallas_call(
        body, name="mla_fwd",
        grid=(4, nq),
        in_specs=[pl.BlockSpec((bq, 2 * HEAD_PAD), lambda h, i: (i, h)),
                  pl.BlockSpec((s, 2 * HEAD_PAD), lambda h, i: (0, h)),
                  pl.BlockSpec((s, LANES), lambda h, i: (0, h))],
        out_specs=[pl.BlockSpec((bq, LANES), lambda h, i: (i, h)),
                   pl.BlockSpec((None, 8, bq), lambda h, i: (h, 0, i))],
        out_shape=[jax.ShapeDtypeStruct((s, MLA_WIDTH), F32), jax.ShapeDtypeStruct((4, 8, s), F32)],
        scratch_shapes=[pltpu.VMEM((2, 2, step, bq), BF16), pltpu.VMEM((2, 2, step, bq), F32)],
        compiler_params=_params(("arbitrary", "arbitrary"), VMEM_LIMIT),
    )(q, k, v)


def _out_call(o_sb, g_sb, o_mla, g_mla, x, target, ada, w_out_bf):
    s = x.shape[0]
    tm = _row_tile(s, 512)

    def body(osb_ref, gsb_ref, oml_ref, gml_ref, x_ref, t_ref, gate_ref, w_ref,
             dosb_ref, doml_ref, dgsb_ref, dgml_ref, dy_ref, gw_ref, dgate_ref, sq_ref):
        @pl.when(pl.program_id(0) == 0)
        def _():
            gw_ref[...] = jnp.zeros_like(gw_ref)
            dgate_ref[...] = jnp.zeros_like(dgate_ref)
            sq_ref[...] = jnp.zeros_like(sq_ref)

        g_s, g_m = gsb_ref[...], gml_ref[...]
        sig_s, sig_m = _sigmoid(g_s), _sigmoid(g_m)
        silu_s, silu_m = g_s * sig_s, g_m * sig_m
        o_s, o_m = osb_ref[...], oml_ref[...]
        mixed = jnp.concatenate([o_s * silu_s, o_m * silu_m], axis=1).astype(BF16)
        u = _dot(mixed, w_ref[...])
        gate_v = gate_ref[...]
        err = x_ref[...] + gate_v * u - t_ref[...]
        sq_ref[...] += jnp.sum(err * err, axis=0, keepdims=True)
        dy = err * (1.0 / D_MODEL)
        dy_ref[...] = dy
        dgate_ref[...] += jnp.sum(dy * u, axis=0, keepdims=True)
        du = (dy * gate_v).astype(BF16)
        gw_ref[...] += _dot_tn(mixed, du)
        dmix = _dot_nt(du, w_ref[...])
        dm_s, dm_m = dmix[:, :SB_WIDTH], dmix[:, SB_WIDTH:]
        dosb_ref[...] = (dm_s * silu_s).astype(BF16)
        doml_ref[...] = (dm_m * silu_m).astype(BF16)
        dgsb_ref[...] = (dm_s * o_s * (sig_s * (1.0 + g_s * (1.0 - sig_s)))).astype(BF16)
        dgml_ref[...] = (dm_m * o_m * (sig_m * (1.0 + g_m * (1.0 - sig_m)))).astype(BF16)

    return pl.pallas_call(
        body, name="out_proj_loss",
        grid=(s // tm,),
        in_specs=[_rows(tm, 512), _rows(tm, 512), _rows(tm, 512), _rows(tm, 512),
                  _rows(tm, D_MODEL), _rows(tm, D_MODEL), _ada_part(2), _full((D_MODEL, D_MODEL))],
        out_specs=[_rows(tm, 512), _rows(tm, 512), _rows(tm, 512), _rows(tm, 512), _rows(tm, D_MODEL),
                   _full((D_MODEL, D_MODEL)), _full((1, D_MODEL)), _full((1, D_MODEL))],
        out_shape=[jax.ShapeDtypeStruct((s, 512), BF16)] * 4
        + [jax.ShapeDtypeStruct((s, D_MODEL), F32), jax.ShapeDtypeStruct((D_MODEL, D_MODEL), F32),
           jax.ShapeDtypeStruct((1, D_MODEL), F32), jax.ShapeDtypeStruct((1, D_MODEL), F32)],
        compiler_params=_params(("arbitrary",), VMEM_LIMIT),
    )(o_sb, g_sb, o_mla, g_mla, x, target, ada, w_out_bf)


def _head_mask(lane, hh):
    return jnp.where((lane >= HALF_LANES) if hh else (lane < HALF_LANES), 1.0, 0.0)


def _pick_lane(packed, lane, which):
    return jnp.sum(jnp.where(lane == which, packed, 0.0), axis=1, keepdims=True)


def _sb_bwd_call(kstart, q, k, v, do, rfin):
    s = q.shape[0]
    bq = _row_tile(s, 256)
    nq = s // bq
    nh = SB_GROUP
    width = nh * HALF_LANES

    def body(ks_ref, q_ref, k_ref, v_ref, do_ref, r_ref, dq_ref, dk_ref, dv_ref):
        hp, i = pl.program_id(0), pl.program_id(1)

        @pl.when(i == 0)
        def _():
            dk_ref[...] = jnp.zeros_like(dk_ref)
            dv_ref[...] = jnp.zeros_like(dv_ref)

        lane = lax.broadcasted_iota(I32, (bq, LANES), 1)
        row = lax.broadcasted_iota(I32, (bq, bq), 0)
        col = lax.broadcasted_iota(I32, (bq, bq), 1)
        upto = jnp.where(row <= col, 1.0, 0.0).astype(BF16)
        before = jnp.where(row < col, 1.0, 0.0).astype(BF16)
        masks = [_head_mask(lane, hh).astype(BF16) for hh in range(2)]
        qms = [q_ref[:, _pair(hh)] * jnp.asarray(SB_SCALE, BF16) * masks[hh % 2] for hh in range(nh)]
        doms = [do_ref[:, _pair(hh)] * masks[hh % 2] for hh in range(nh)]
        totals = [_pick_lane(r_ref[:, _pair(hh)], lane, HALF_LANES * (hh % 2)) for hh in range(nh)]
        strict = col < row

        def walk(blocks, state):
            chains = [(kb, diagonal, hh) for kb, diagonal in blocks for hh in range(nh)]
            keys = lambda kb: pl.ds(pl.multiple_of(kb * bq, bq), bq)
            cut = lambda x, diagonal: jnp.where(strict, x, 0.0) if diagonal else x
            zs = [_dot_nt(qms[hh], k_ref[keys(kb), _pair(hh)]) for kb, _, hh in chains]
            dws = [_dot_nt(doms[hh], v_ref[keys(kb), _pair(hh)]) for kb, _, hh in chains]
            pairs = []
            for z, (_, diagonal, _) in zip(zs, chains):
                ls, lk = _log_sigmoid_pair(z)
                pairs.append((ls, cut(lk, diagonal)))
            incls = [_split_dot(lk, upto) for _, lk in pairs]
            pres = [st[0] for st in state]
            ws, gs = [], []
            for (ls, lk), incl, dw, (_, diagonal, hh) in zip(pairs, incls, dws, chains):
                w = cut(jnp.exp(ls + ((totals[hh] - pres[hh]) - incl)), diagonal)
                ws.append(w.astype(BF16))
                gs.append(w * dw)
                pres[hh] = pres[hh] + jnp.sum(lk, axis=1, keepdims=True)
            gsums = [_dot(g.astype(BF16), before) for g in gs]
            gpres = [st[1] for st in state]
            dzs = []
            for (ls, _), g, gsum, (_, diagonal, hh) in zip(pairs, gs, gsums, chains):
                dzs.append(cut(g - jnp.exp(ls) * (g + (gpres[hh] + gsum)), diagonal).astype(BF16))
                gpres[hh] = gpres[hh] + jnp.sum(g, axis=1, keepdims=True)
            dqs = [st[2] for st in state]
            dk_parts, dv_parts = [], []
            for dzb, w, (kb, _, hh) in zip(dzs, ws, chains):
                dk_parts.append(_dot_tn(dzb, qms[hh]))
                dv_parts.append(_dot_tn(w, doms[hh]))
                dqs[hh] = dqs[hh] + _dot(dzb, k_ref[keys(kb), _pair(hh)])
            for b, (kb, _) in enumerate(blocks):
                for pair in range(nh // 2):
                    c0 = b * nh + 2 * pair
                    dk_ref[keys(kb), _pair(2 * pair)] += dk_parts[c0] + dk_parts[c0 + 1]
                    dv_ref[keys(kb), _pair(2 * pair)] += dv_parts[c0] + dv_parts[c0 + 1]
            return tuple(zip(pres, gpres, dqs))

        def finish(state):
            for pair in range(nh // 2):
                both = jnp.where(lane < HALF_LANES, state[2 * pair][2], state[2 * pair + 1][2])
                dq_ref[:, _pair(2 * pair)] = (both * SB_SCALE).astype(BF16)

        zero = ((jnp.zeros((bq, 1), F32), jnp.zeros((bq, 1), F32), jnp.zeros((bq, LANES), F32)),) * nh

        @pl.when(i == 0)
        def _():
            finish(walk([(0, True)], zero))

        @pl.when(i > 0)
        def _():
            state = lax.fori_loop(ks_ref[hp, i], i - 1, lambda kb, st: walk([(kb, False)], st), zero)
            finish(walk([(i - 1, False), (i, True)], state))

    return pl.pallas_call(
        body, name="sb_bwd",
        grid_spec=pltpu.PrefetchScalarGridSpec(
            num_scalar_prefetch=1, grid=(SB_HEADS // nh, nq),
            in_specs=[pl.BlockSpec((bq, width), lambda h, i, ks: (i, h)),
                      pl.BlockSpec((s, width), lambda h, i, ks: (0, h), pipeline_mode=pl.Buffered(1)),
                      pl.BlockSpec((s, width), lambda h, i, ks: (0, h), pipeline_mode=pl.Buffered(1)),
                      pl.BlockSpec((bq, width), lambda h, i, ks: (i, h)),
                      pl.BlockSpec((bq, width), lambda h, i, ks: (i, h))],
            out_specs=[pl.BlockSpec((bq, width), lambda h, i, ks: (i, h)),
                       pl.BlockSpec((s, width), lambda h, i, ks: (0, h), pipeline_mode=pl.Buffered(1)),
                       pl.BlockSpec((s, width), lambda h, i, ks: (0, h), pipeline_mode=pl.Buffered(1))]),
        out_shape=[jax.ShapeDtypeStruct((s, SB_WIDTH), BF16), jax.ShapeDtypeStruct((s, SB_WIDTH), F32),
                   jax.ShapeDtypeStruct((s, SB_WIDTH), F32)],
        compiler_params=_params(("arbitrary", "arbitrary"), VMEM_LIMIT),
    )(kstart, q, k, v, do, rfin)


def _mla_bwd_call(q, k, v, do, o, lse):
    s = q.shape[0]
    bq = _row_tile(s, MLA_BWD_BQ)
    bk = _row_tile(s, MLA_BWD_BK)
    nq = s // bq
    assert bk % bq == 0
    step = min(bk // 2, MLA_BWD_STEP)
    nsub = bk // step
    assert nsub % 2 == 0

    def body(q_ref, k_ref, v_ref, do_ref, o_ref, lse_ref, dq_ref, dk_ref, dv_ref, dom_ref, s_ref, dp_ref, pb_ref,
             ds_ref):
        i = pl.program_id(1)

        @pl.when(i == 0)
        def _():
            dk_ref[...] = jnp.zeros_like(dk_ref)
            dv_ref[...] = jnp.zeros_like(dv_ref)

        lane = lax.broadcasted_iota(I32, (bq, LANES), 1)
        row = lax.broadcasted_iota(I32, (step, bq), 1)
        col = lax.broadcasted_iota(I32, (step, bq), 0)
        n_full = (i * bq) // bk
        do2 = do_ref[...]
        prod = do2.astype(F32) * o_ref[...]
        ones = jnp.ones((8, LANES), BF16)
        deltas, lses = [], []
        for hh in range(2):
            head = _head_mask(lane, hh)
            dom_ref[hh] = do2 * head.astype(BF16)
            part = prod * head
            hi = part.astype(BF16)
            lo = (part - hi.astype(F32)).astype(BF16)
            deltas.append((_dot_nt(ones, hi) + _dot_nt(ones, lo))[0:1])
            lses.append(lse_ref[hh:hh + 1, :])

        def keys(g):
            return pl.ds(pl.multiple_of(g * step, step), step)

        def heads():
            return [(hh, slice(hh * HEAD_PAD, (hh + 1) * HEAD_PAD)) for hh in range(2)]

        def put_products(g, slot, qlo=0):
            vblk = v_ref[keys(g), :]
            for hh, cols in heads():
                s_ref[slot, hh, :, qlo:] = _dot_nt(k_ref[keys(g), cols], q_ref[qlo:, cols])
                dp_ref[slot, hh, :, qlo:] = _dot_nt(vblk, dom_ref[hh, qlo:, :])

        def add_grads(dqs, g, slot, qlo=0):
            rows = keys(g)
            new, dv_parts = [], []
            for hh, cols in heads():
                ds = ds_ref[slot, hh, :, qlo:]
                dk_ref[rows, cols] += _dot(ds, q_ref[qlo:, cols])
                dv_parts.append(_dot(pb_ref[slot, hh, :, qlo:], dom_ref[hh, qlo:, :]))
                upd = dqs[hh][:, qlo:] + _dot_tn(k_ref[rows, cols], ds)
                new.append(upd if qlo == 0 else jnp.concatenate([dqs[hh][:, :qlo], upd], axis=1))
            dv_ref[rows, :] += dv_parts[0] + dv_parts[1]
            return tuple(new)

        def substep(g, slot, dqs, masked, prefetch, qlo=0, next_qlo=0, prev_qlo=0):
            if prefetch:
                put_products(g + 1, 1 - slot, next_qlo)
            dqs = add_grads(dqs, jnp.maximum(g - 1, 0), 1 - slot, prev_qlo)
            for hh, _ in heads():
                p = jnp.exp2(s_ref[slot, hh, :, qlo:] - lses[hh][:, qlo:])
                if masked:
                    p = jnp.where(col[:, qlo:] + g * step <= row[:, qlo:] + i * bq, p, 0.0)
                ds_ref[slot, hh, :, qlo:] = (p * (dp_ref[slot, hh, :, qlo:] - deltas[hh][:, qlo:])).astype(BF16)
                pb_ref[slot, hh, :, qlo:] = p.astype(BF16)
            return dqs

        def first_query(t, masked):
            return t * step if (masked and bk == bq and 0 <= t < nsub) else 0

        def chunk(kb, dqs, masked):
            for t in range(nsub):
                last = masked and t == nsub - 1
                dqs = substep(nsub * kb + t, t % 2, dqs, masked, not last, first_query(t, masked),
                              first_query(t + 1, masked), first_query(t - 1, masked))
            return dqs

        ds_ref[1] = jnp.zeros_like(ds_ref[1])
        pb_ref[1] = jnp.zeros_like(pb_ref[1])
        put_products(0, 0)
        zero = jnp.zeros((HEAD_PAD, bq), F32)
        dqs = lax.fori_loop(0, n_full, lambda kb, dqs: chunk(kb, dqs, False), (zero, zero))
        dqs = chunk(n_full, dqs, True)
        dqs = add_grads(dqs, nsub * n_full + nsub - 1, 1, first_query(nsub - 1, True))
        dq_ref[:, :HEAD_PAD] = dqs[0].T * MLA_SCALE
        dq_ref[:, HEAD_PAD:] = dqs[1].T * MLA_SCALE

    return pl.pallas_call(
        body, name="mla_bwd",
        grid=(4, nq),
        in_specs=[pl.BlockSpec((bq, 2 * HEAD_PAD), lambda h, i: (i, h)),
                  pl.BlockSpec((s, 2 * HEAD_PAD), lambda h, i: (0, h)),
                  pl.BlockSpec((s, LANES), lambda h, i: (0, h)),
                  pl.BlockSpec((bq, LANES), lambda h, i: (i, h)),
                  pl.BlockSpec((bq, LANES), lambda h, i: (i, h)),
                  pl.BlockSpec((None, 8, bq), lambda h, i: (h, 0, i))],
        out_specs=[pl.BlockSpec((bq, 2 * HEAD_PAD), lambda h, i: (i, h)),
                   pl.BlockSpec((s, 2 * HEAD_PAD), lambda h, i: (0, h), pipeline_mode=pl.Buffered(1)),
                   pl.BlockSpec((s, LANES), lambda h, i: (0, h), pipeline_mode=pl.Buffered(1))],
        out_shape=[jax.ShapeDtypeStruct((s, MLA_PAD_WIDTH), F32), jax.ShapeDtypeStruct((s, MLA_PAD_WIDTH), F32),
                   jax.ShapeDtypeStruct((s, MLA_WIDTH), F32)],
        scratch_shapes=[pltpu.VMEM((2, bq, LANES), BF16),
                        pltpu.VMEM((2, 2, step, bq), F32), pltpu.VMEM((2, 2, step, bq), F32),
                        pltpu.VMEM((2, 2, step, bq), BF16), pltpu.VMEM((2, 2, step, bq), BF16)],
        compiler_params=_params(("arbitrary", "arbitrary"), VMEM_LIMIT),
    )(q, k, v, do, o, lse)


def _rms_bwd(d_out, inp, r, weight, n):
    normed = inp * r
    gw = d_out * weight
    d_in = r * (gw - normed * (jnp.sum(gw * normed, axis=-1, keepdims=True) * (1.0 / n)))
    return d_in, d_out * normed


def _mla_prep_bwd_call(dq, dk, dv, q0, k0, cqn, ckvn, c_q, c_kv, cos_t, sin_t,
                       q_lora_norm, kv_lora_norm, qhn_pad, khn_pad, w_uq_bf, w_uk_bf, w_uv_bf):
    s = dq.shape[0]
    tm = _row_tile(s, 512)

    def body(dq_ref, dk_ref, dv_ref, q0_ref, k0_ref, cqn_ref, ckvn_ref, cq_ref, ckv_ref,
             cos_ref, sin_ref, qln_ref, kvln_ref, qhn_ref, khn_ref, wuq_ref, wuk_ref, wuv_ref,
             dcq_ref, dckv_ref, dkr_ref, gwuq_ref, gwuk_ref, gwuv_ref, gqln_ref, gkvln_ref, gqhn_ref, gkhn_ref,
             dq0_ref, dk0_ref, tmp_ref):
        @pl.when(pl.program_id(0) == 0)
        def _():
            for ref in (gwuq_ref, gwuk_ref, gwuv_ref, gqln_ref, gkvln_ref, gqhn_ref, gkhn_ref):
                ref[...] = jnp.zeros_like(ref)

        cos_t, sin_t = cos_ref[...], sin_ref[...]
        lane = lax.broadcasted_iota(I32, (tm, LANES), 1)
        rope_lanes = jnp.logical_or(lane < ROPE_HALF,
                                    jnp.logical_and(lane >= HALF_LANES, lane < HALF_LANES + ROPE_HALF))
        heads = [slice(h * HEAD_PAD, (h + 1) * HEAD_PAD) for h in range(MLA_HEADS)]

        def head_norm_bwd(d_ref, x0_ref, w_ref, out_ref, scale):
            w = w_ref[...]
            inv = [lax.rsqrt(jnp.sum(x0_ref[:, cols] * x0_ref[:, cols], axis=-1, keepdims=True)
                             * (1.0 / MLA_QK_DIM) + EPS) for cols in heads]
            for cols in heads:
                tmp_ref[:, cols] = _rope_adjoint(d_ref[:, cols] * scale, cos_t, sin_t)
            dots = [jnp.sum(tmp_ref[:, cols] * w * (x0_ref[:, cols] * r), axis=-1, keepdims=True)
                    for cols, r in zip(heads, inv)]
            g_w = jnp.zeros((1, LANES), F32)
            rope_sum = jnp.zeros((tm, LANES), F32)
            for cols, r, dot in zip(heads, inv, dots):
                normed = x0_ref[:, cols] * r
                d_n = tmp_ref[:, cols]
                d_x0 = r * (d_n * w - normed * (dot * (1.0 / MLA_QK_DIM)))
                out_ref[:, cols] = d_x0.astype(BF16)
                g_w = g_w + jnp.sum(d_n * normed, axis=0, keepdims=True)
                rope_sum = rope_sum + jnp.where(rope_lanes, d_x0, 0.0)
            return g_w, rope_sum

        g_qhn, _ = head_norm_bwd(dq_ref, q0_ref, qhn_ref, dq0_ref, 1.0)
        g_khn, d_kr = head_norm_bwd(dk_ref, k0_ref, khn_ref, dk0_ref, LN2)
        cqn, ckvn = cqn_ref[...], ckvn_ref[...]
        d_q0b, d_k0b, dvb = dq0_ref[...], dk0_ref[...], dv_ref[...].astype(BF16)
        d_cqn = _dot_nt(d_q0b, wuq_ref[...])
        gwuq_ref[...] += _dot_tn(cqn, d_q0b)
        d_ckvn = _dot_nt(d_k0b, wuk_ref[...]) + _dot_nt(dvb, wuv_ref[...])
        gwuk_ref[...] += _dot_tn(ckvn, d_k0b)
        gwuv_ref[...] += _dot_tn(ckvn, dvb)
        gqhn_ref[...] += g_qhn
        gkhn_ref[...] += g_khn
        dkr_ref[...] = d_kr.astype(BF16)
        cq = cq_ref[...]
        rcq = lax.rsqrt(jnp.mean(cq * cq, axis=-1, keepdims=True) + EPS)
        d_cq, gl = _rms_bwd(d_cqn, cq, rcq, qln_ref[...], Q_LORA_RANK)
        dcq_ref[...] = d_cq.astype(BF16)
        gqln_ref[...] += jnp.sum(gl, axis=0, keepdims=True)
        ckv = ckv_ref[...]
        rckv = lax.rsqrt(jnp.mean(ckv * ckv, axis=-1, keepdims=True) + EPS)
        d_ckv, gl = _rms_bwd(d_ckvn, ckv, rckv, kvln_ref[...], KV_LORA_RANK)
        dckv_ref[...] = d_ckv.astype(BF16)
        gkvln_ref[...] += jnp.sum(gl, axis=0, keepdims=True)

    return pl.pallas_call(
        body, name="mla_prep_bwd",
        grid=(s // tm,),
        in_specs=[_rows(tm, MLA_PAD_WIDTH), _rows(tm, MLA_PAD_WIDTH), _rows(tm, MLA_WIDTH),
                  _rows(tm, MLA_PAD_WIDTH), _rows(tm, MLA_PAD_WIDTH),
                  _rows(tm, Q_LORA_RANK), _rows(tm, KV_LORA_RANK), _rows(tm, Q_LORA_RANK), _rows(tm, KV_LORA_RANK),
                  _rows(tm, LANES), _rows(tm, LANES),
                  _full((1, Q_LORA_RANK)), _full((1, KV_LORA_RANK)), _full((1, LANES)), _full((1, LANES)),
                  _full((Q_LORA_RANK, MLA_PAD_WIDTH)), _full((KV_LORA_RANK, MLA_PAD_WIDTH)),
                  _full((KV_LORA_RANK, MLA_WIDTH))],
        out_specs=[_rows(tm, Q_LORA_RANK), _rows(tm, KV_LORA_RANK), _rows(tm, LANES),
                   _full((Q_LORA_RANK, MLA_PAD_WIDTH)), _full((KV_LORA_RANK, MLA_PAD_WIDTH)),
                   _full((KV_LORA_RANK, MLA_WIDTH)),
                   _full((1, Q_LORA_RANK)), _full((1, KV_LORA_RANK)), _full((1, LANES)), _full((1, LANES))],
        out_shape=[jax.ShapeDtypeStruct((s, Q_LORA_RANK), BF16), jax.ShapeDtypeStruct((s, KV_LORA_RANK), BF16),
                   jax.ShapeDtypeStruct((s, LANES), BF16),
                   jax.ShapeDtypeStruct((Q_LORA_RANK, MLA_PAD_WIDTH), F32),
                   jax.ShapeDtypeStruct((KV_LORA_RANK, MLA_PAD_WIDTH), F32),
                   jax.ShapeDtypeStruct((KV_LORA_RANK, MLA_WIDTH), F32),
                   jax.ShapeDtypeStruct((1, Q_LORA_RANK), F32), jax.ShapeDtypeStruct((1, KV_LORA_RANK), F32),
                   jax.ShapeDtypeStruct((1, LANES), F32), jax.ShapeDtypeStruct((1, LANES), F32)],
        scratch_shapes=[pltpu.VMEM((tm, MLA_PAD_WIDTH), BF16), pltpu.VMEM((tm, MLA_PAD_WIDTH), BF16),
                        pltpu.VMEM((tm, MLA_PAD_WIDTH), F32)],
        compiler_params=_params(("arbitrary",), VMEM_LIMIT),
    )(dq, dk, dv, q0, k0, cqn, ckvn, c_q, c_kv, cos_t, sin_t,
      q_lora_norm, kv_lora_norm, qhn_pad, khn_pad, w_uq_bf, w_uk_bf, w_uv_bf)


def _dh_call(pieces, hb, x, dy, ada, norm_w, w_in_bf):
    s = x.shape[0]
    tm = _row_tile(s, 512)
    widths = [p.shape[1] for p in pieces]
    offsets = [sum(widths[:j]) for j in range(len(widths))]
    assert offsets[-1] + widths[-1] == IN_COLS_PAD
    n = len(pieces)

    def body(*refs):
        p_refs = refs[:n]
        (hb_ref, x_ref, dy_ref, sh_ref, sc_ref, nw_ref, w_ref, gx_ref, gw_ref, dsh_ref, dsc_ref, gnw_ref,
         dp_ref) = refs[n:]

        @pl.when(pl.program_id(0) == 0)
        def _():
            gw_ref[...] = jnp.zeros_like(gw_ref)
            dsh_ref[...] = jnp.zeros_like(dsh_ref)
            dsc_ref[...] = jnp.zeros_like(dsc_ref)
            gnw_ref[...] = jnp.zeros_like(gnw_ref)

        for p_ref, c0, width in zip(p_refs, offsets, widths):
            dp_ref[:, c0:c0 + width] = p_ref[...].astype(BF16)
        gw_ref[...] += _dot_tn(hb_ref[...], dp_ref[...])
        dh = _dot_nt(dp_ref[...], w_ref[...])
        xx = x_ref[...]
        r0 = lax.rsqrt(jnp.mean(xx * xx, axis=-1, keepdims=True) + EPS)
        xn = xx * r0
        nw = nw_ref[...]
        dsh_ref[...] += jnp.sum(dh, axis=0, keepdims=True)
        dsc_ref[...] += jnp.sum(dh * (xn * nw), axis=0, keepdims=True)
        dn = dh * (1.0 + sc_ref[...])
        gnw_ref[...] += jnp.sum(dn * xn, axis=0, keepdims=True)
        dxn = dn * nw
        gx_ref[...] = dy_ref[...] + r0 * (dxn - xn * jnp.mean(dxn * xn, axis=-1, keepdims=True))

    return pl.pallas_call(
        body, name="in_proj_bwd",
        grid=(s // tm,),
        in_specs=[_rows(tm, w) for w in widths]
        + [_rows(tm, D_MODEL), _rows(tm, D_MODEL), _rows(tm, D_MODEL),
           _ada_part(0), _ada_part(1), _full((1, D_MODEL)),
           pl.BlockSpec((D_MODEL, IN_COLS_PAD), lambda i: (0, 0), pipeline_mode=pl.Buffered(1))],
        out_specs=[_rows(tm, D_MODEL),
                   pl.BlockSpec((D_MODEL, IN_COLS_PAD), lambda i: (0, 0), pipeline_mode=pl.Buffered(1)),
                   _full((1, D_MODEL)), _full((1, D_MODEL)), _full((1, D_MODEL))],
        out_shape=[jax.ShapeDtypeStruct((s, D_MODEL), F32), jax.ShapeDtypeStruct((D_MODEL, IN_COLS_PAD), F32),
                   jax.ShapeDtypeStruct((1, D_MODEL), F32), jax.ShapeDtypeStruct((1, D_MODEL), F32),
                   jax.ShapeDtypeStruct((1, D_MODEL), F32)],
        scratch_shapes=[pltpu.VMEM((tm, IN_COLS_PAD), BF16)],
        compiler_params=_params(("arbitrary",), VMEM_LIMIT),
    )(*pieces, hb, x, dy, ada, ada, norm_w, w_in_bf)


def _adamw(g, w, m, v):
    m = ADAM_B1 * m + (1.0 - ADAM_B1) * g
    v = ADAM_B2 * v + (1.0 - ADAM_B2) * (g * g)
    m_hat = m / (1.0 - ADAM_B1 ** ADAM_STEP)
    v_hat = v / (1.0 - ADAM_B2 ** ADAM_STEP)
    delta = -ADAM_LR * (m_hat / (jnp.sqrt(v_hat) + ADAM_EPS) + ADAM_WD * w)
    return delta, m, v


def _adam_shard_call(name, parts, w, m, v):
    r, c = w.shape
    tr = r if r <= 512 else 256

    def body(p_ref, w_ref, m_ref, v_ref, g_ref, d_ref, nm_ref, nv_ref):
        g = ((p_ref[0].astype(F32) + p_ref[1].astype(F32)) + p_ref[2].astype(F32)) + p_ref[3].astype(F32)
        g_ref[...] = g
        d_ref[...], nm_ref[...], nv_ref[...] = _adamw(g, w_ref[...], m_ref[...], v_ref[...])

    blk = pl.BlockSpec((tr, c), lambda i: (i, 0))
    return pl.pallas_call(
        body, name=name,
        grid=(r // tr,),
        in_specs=[pl.BlockSpec((4, tr, c), lambda i: (0, i, 0)), blk, blk, blk],
        out_specs=[blk] * 4,
        out_shape=[jax.ShapeDtypeStruct((r, c), F32)] * 4,
        compiler_params=_params(("arbitrary",), VMEM_LIMIT),
    )(parts, w, m, v)


def _adam_ada_call(c_all, d_all, w, m, v):
    r, c = w.shape
    tr = 256

    def body(c_ref, d_ref, w_ref, m_ref, v_ref, g_ref, dl_ref, nm_ref, nv_ref):
        cc = c_ref[...]
        sc = cc * _sigmoid(cc)
        dd = d_ref[...]
        sc_hi = sc.astype(BF16)
        sc_lo = (sc - sc_hi.astype(F32)).astype(BF16)
        dd_hi = dd.astype(BF16)
        dd_lo = (dd - dd_hi.astype(F32)).astype(BF16)
        g = _dot_tn(sc_hi, dd_hi) + (_dot_tn(sc_hi, dd_lo) + _dot_tn(sc_lo, dd_hi))
        g_ref[...] = g
        dl_ref[...], nm_ref[...], nv_ref[...] = _adamw(g, w_ref[...], m_ref[...], v_ref[...])

    blk = pl.BlockSpec((tr, c), lambda i: (i, 0))
    return pl.pallas_call(
        body, name="adam_w_ada",
        grid=(r // tr,),
        in_specs=[pl.BlockSpec((16, tr), lambda i: (0, i)), pl.BlockSpec((16, c), lambda i: (0, 0)), blk, blk, blk],
        out_specs=[blk] * 4,
        out_shape=[jax.ShapeDtypeStruct((r, c), F32)] * 4,
        compiler_params=_params(("arbitrary",), VMEM_LIMIT),
    )(c_all, d_all, w, m, v)


def _adam_vectors_call(packs, offsets, vectors):
    nv = len(vectors)

    def body(*refs):
        p_ref, ins, outs = refs[0], refs[1:1 + 3 * nv], refs[1 + 3 * nv:]
        for j, off in enumerate(offsets):
            n = ins[3 * j].shape[1]
            span = -(-n // LANES) * LANES
            g = p_ref[0, :, off:off + span]
            for b in range(1, 8):
                g = g + p_ref[b, :, off:off + span]
            g = g[:, :n]
            outs[j][...] = g
            outs[nv + j][...], outs[2 * nv + j][...], outs[3 * nv + j][...] = _adamw(
                g, ins[3 * j][...], ins[3 * j + 1][...], ins[3 * j + 2][...])

    flat = [a for t in vectors for a in t]
    res = pl.pallas_call(
        body, name="adam_vectors",
        out_shape=[jax.ShapeDtypeStruct(t[0].shape, F32) for _ in range(4) for t in vectors],
    )(packs, *flat)
    return [res[k * nv:(k + 1) * nv] for k in range(4)]


ROPE_HALF = MLA_ROPE_DIM // 2
NOPE_A = MLA_NOPE_DIM - ROPE_HALF


def _zeros_like_lanes(t, n):
    return jnp.zeros(t.shape[:-1] + (n,), t.dtype)


def _to_head_lanes(t):
    nope, rope = t[..., :MLA_NOPE_DIM], t[..., MLA_NOPE_DIM:]
    return jnp.concatenate([rope[..., :ROPE_HALF], nope[..., :NOPE_A], rope[..., ROPE_HALF:], nope[..., NOPE_A:],
                            _zeros_like_lanes(t, HEAD_PAD - MLA_QK_DIM)], axis=-1)


def _from_head_lanes(t):
    return jnp.concatenate([t[..., ROPE_HALF:HALF_LANES], t[..., HALF_LANES + ROPE_HALF:MLA_QK_DIM],
                            t[..., :ROPE_HALF], t[..., HALF_LANES:HALF_LANES + ROPE_HALF]], axis=-1)


def _nope_to_head_lanes(t):
    return jnp.concatenate([_zeros_like_lanes(t, ROPE_HALF), t[..., :NOPE_A], _zeros_like_lanes(t, ROPE_HALF),
                            t[..., NOPE_A:], _zeros_like_lanes(t, HEAD_PAD - MLA_QK_DIM)], axis=-1)


def _rope_to_head_lanes(t):
    return jnp.concatenate([t[..., :ROPE_HALF], _zeros_like_lanes(t, HALF_LANES - ROPE_HALF), t[..., ROPE_HALF:],
                            _zeros_like_lanes(t, HALF_LANES - ROPE_HALF)], axis=-1)


def _rope_tables(positions):
    inv_freq = (ROPE_THETA ** (-jnp.arange(0, MLA_ROPE_DIM, 2, dtype=F32) / MLA_ROPE_DIM))[None]
    signed = _rope_to_head_lanes(jnp.concatenate([-inv_freq, inv_freq], axis=1))
    ang = positions.astype(F32)[:, None] * signed
    return jnp.cos(ang), jnp.sin(ang)


def _unshard_cols(g):
    return jnp.transpose(g, (1, 0, 2)).reshape(g.shape[1], 4 * g.shape[2])


def _shard_cols(g):
    r, c4 = g.shape
    return jnp.transpose(g.reshape(r, 4, c4 // 4), (1, 0, 2))


def kernel(x, c, positions, w_ada, b_ada, norm_w, w_in, q_lora_norm, w_uq, kv_lora_norm, w_ukv, q_head_norm, k_head_norm, w_out, loss_target, m_w_ada, m_b_ada, m_norm_w, m_w_in, m_q_lora_norm, m_w_uq, m_kv_lora_norm, m_w_ukv, m_q_head_norm, m_k_head_norm, m_w_out, v_w_ada, v_b_ada, v_norm_w, v_w_in, v_q_lora_norm, v_w_uq, v_kv_lora_norm, v_w_ukv, v_q_head_norm, v_k_head_norm, v_w_out):
    chip = 2 * lax.axis_index("x") + lax.axis_index("y")
    me8 = 2 * chip + lax.axis_index("c")
    ada_cols = w_ada.shape[2]
    c_all = _allgather_rows_call(c)[:, 0, :]
    ada_part = _ada_call(c_all, w_ada[0], lax.dynamic_slice_in_dim(b_ada, chip * ada_cols, ada_cols, axis=1))
    ada_g, win_g, wuq_g, wukv_g, wout_g = _gather_call(
        [ada_part[None]] + [w.astype(BF16) for w in (w_in, w_uq, w_ukv, w_out)], [False, True, True, True, True])
    ada = lax.dynamic_slice_in_dim(ada_g, me8, 1, axis=1).reshape(1, 4 * ada_cols)
    (sq_sum, grad_x, g_w_in, g_w_uq, g_w_ukv, g_w_out, d_ada, g_norm_w, g_qln, g_kvln, g_qhn, g_khn) = _local_step(
        x[0], ada, positions[0], loss_target[0], norm_w, win_g,
        q_lora_norm, _unshard_cols(wuq_g), kv_lora_norm, _unshard_cols(wukv_g), q_head_norm, k_head_norm,
        wout_g.reshape(D_MODEL, D_MODEL))

    grads = [g.astype(BF16) for g in (g_w_in, _shard_cols(g_w_uq), _shard_cols(g_w_ukv),
                                      g_w_out.reshape(4, D_MODEL // 4, D_MODEL))]
    pieces = [d_ada, g_norm_w, g_qln, g_kvln, g_qhn, g_khn, (0.5 * sq_sum / D_MODEL).reshape(1, 1)]
    spans = [-(-p.shape[1] // LANES) * LANES for p in pieces]
    starts = [sum(spans[:j]) for j in range(len(spans))]
    small = jnp.concatenate([jnp.pad(p, ((0, 0), (0, sp - p.shape[1]))) for p, sp in zip(pieces, spans)], axis=1)
    core = lax.axis_index("c").astype(I32).reshape(1)
    chip_halves = _sum_halves_call(core, grads, _swap_halves_call(grads))
    parts, packs = _exchange_call(chip_halves, small)
    loss = jnp.sum(packs[:, 0, starts[-1]])

    names = ["adam_w_in", "adam_w_uq", "adam_w_ukv", "adam_w_out"]
    shard_w = [(w_in, m_w_in, v_w_in), (w_uq, m_w_uq, v_w_uq), (w_ukv, m_w_ukv, v_w_ukv),
               (w_out, m_w_out, v_w_out)]
    res = {}
    for name, p_g, (w, m, v) in zip(names, parts, shard_w):
        res[name] = _adam_shard_call(name, p_g, w[0], m[0], v[0])
    d_all = lax.dynamic_slice_in_dim(packs[:, 0, :], starts[0] + chip * ada_cols, ada_cols, axis=1)
    res_ada = _adam_ada_call(jnp.pad(c_all, ((0, 8), (0, 0))), jnp.pad(d_all, ((0, 8), (0, 0))),
                             w_ada[0], m_w_ada[0], v_w_ada[0])
    vectors = [(b_ada, m_b_ada, v_b_ada), (norm_w, m_norm_w, v_norm_w), (q_lora_norm, m_q_lora_norm, v_q_lora_norm),
               (kv_lora_norm, m_kv_lora_norm, v_kv_lora_norm), (q_head_norm, m_q_head_norm, v_q_head_norm),
               (k_head_norm, m_k_head_norm, v_k_head_norm)]
    vec_out = _adam_vectors_call(packs, starts[:len(vectors)], vectors)

    def ordered(kind):
        big = lambda name: res[name][kind][None]
        return [res_ada[kind][None], vec_out[kind][0], vec_out[kind][1], big("adam_w_in"), vec_out[kind][2],
                big("adam_w_uq"), vec_out[kind][3], big("adam_w_ukv"), vec_out[kind][4], vec_out[kind][5],
                big("adam_w_out")]

    return (loss, grad_x[None], *ordered(0), *ordered(1), *ordered(2), *ordered(3))


def _local_step(x2, ada, positions, tgt, norm_w, w_in_shards, q_lora_norm, w_uq_full,
                kv_lora_norm, w_ukv_full, q_head_norm, k_head_norm, w_out_full):
    in_shard = w_in_shards.shape[2]
    ckv_tail = C_GM - 3 * in_shard
    assert 0 <= ckv_tail and ckv_tail + MLA_ROPE_DIM + MLA_WIDTH == in_shard
    last = w_in_shards[3]
    w_in_bf = jnp.concatenate(
        [w_in_shards[0], w_in_shards[1], w_in_shards[2], last[:, :ckv_tail], last[:, ckv_tail + MLA_ROPE_DIM:],
         _rope_to_head_lanes(last[:, ckv_tail:ckv_tail + MLA_ROPE_DIM])], axis=1).astype(BF16)
    w_uq_bf = _to_head_lanes(w_uq_full.reshape(Q_LORA_RANK, MLA_HEADS, MLA_QK_DIM)).reshape(
        Q_LORA_RANK, MLA_PAD_WIDTH).astype(BF16)
    w_ukv_heads = w_ukv_full.reshape(KV_LORA_RANK, MLA_HEADS, 2 * MLA_NOPE_DIM)
    w_uk_bf = _nope_to_head_lanes(w_ukv_heads[:, :, :MLA_NOPE_DIM]).reshape(KV_LORA_RANK, MLA_PAD_WIDTH).astype(BF16)
    w_uv_bf = w_ukv_heads[:, :, MLA_NOPE_DIM:].reshape(KV_LORA_RANK, MLA_WIDTH).astype(BF16)
    w_out_bf = w_out_full.astype(BF16)
    qhn_pad, khn_pad = _to_head_lanes(q_head_norm), _to_head_lanes(k_head_norm)
    cos_t, sin_t = _rope_tables(positions)

    hb, q_sb, k_sb, v_sb, g_sb, c_q, c_kv, g_mla, q_m, k_m, v_m, cqn, ckvn, q0, k0 = _pre_call(
        x2, ada, norm_w, w_in_bf, cos_t, sin_t, q_lora_norm, kv_lora_norm, qhn_pad, khn_pad,
        w_uq_bf, w_uk_bf, w_uv_bf)
    o_sb, r_sb, kstart = _sb_fwd_call(q_sb, k_sb, v_sb)
    o_mla, lse = _mla_fwd_call(q_m, k_m, v_m)
    do_sb, do_mla, dg_sb, dg_mla, dy, g_w_out, d_gate, sq = _out_call(
        o_sb, g_sb, o_mla, g_mla, x2, tgt, ada, w_out_bf)

    dq_sb, dk_sb, dv_sb = _sb_bwd_call(kstart, q_sb, k_sb, v_sb, do_sb, r_sb)
    dq_m, dk_m, dv_m = _mla_bwd_call(q_m, k_m, v_m, do_mla, o_mla, lse)
    (d_cq, d_ckv, d_kr, g_wuq_pad, g_wuk_pad, g_wuv, g_qln, g_kvln, g_qhn, g_khn) = _mla_prep_bwd_call(
        dq_m, dk_m, dv_m, q0, k0, cqn, ckvn, c_q, c_kv, cos_t, sin_t,
        q_lora_norm, kv_lora_norm, qhn_pad, khn_pad, w_uq_bf, w_uk_bf, w_uv_bf)
    grad_x, g_win_pad, d_shift, d_scale, g_norm_w = _dh_call(
        [dq_sb, dk_sb, dv_sb, dg_sb, d_cq, d_ckv, dg_mla, d_kr], hb, x2, dy, ada, norm_w, w_in_bf)

    g_kr = g_win_pad[:, C_KR:]
    g_last = jnp.concatenate([g_win_pad[:, 3 * in_shard:C_GM], g_kr[:, :ROPE_HALF],
                              g_kr[:, HALF_LANES:HALF_LANES + ROPE_HALF], g_win_pad[:, C_GM:C_KR]], axis=1)
    g_w_in = jnp.stack([g_win_pad[:, j * in_shard:(j + 1) * in_shard] for j in range(3)] + [g_last])
    g_w_uq = _from_head_lanes(g_wuq_pad.reshape(Q_LORA_RANK, MLA_HEADS, HEAD_PAD)).reshape(Q_LORA_RANK, -1)
    g_w_ukv = jnp.concatenate(
        [_from_head_lanes(g_wuk_pad.reshape(KV_LORA_RANK, MLA_HEADS, HEAD_PAD))[:, :, :MLA_NOPE_DIM],
         g_wuv.reshape(KV_LORA_RANK, MLA_HEADS, MLA_NOPE_DIM)], axis=2).reshape(KV_LORA_RANK, -1)
    d_ada = jnp.concatenate([d_shift, d_scale, d_gate], axis=1)
    return (jnp.sum(sq), grad_x, g_w_in, g_w_uq, g_w_ukv, g_w_out, d_ada, g_norm_w, g_qln, g_kvln,
            _from_head_lanes(g_qhn), _from_head_lanes(g_khn))
```

```python
import math

import jax
import jax.numpy as jnp
from jax import lax
from jax.experimental import pallas as pl
from jax.experimental.pallas import tpu as pltpu

F32 = jnp.float32
BF16 = jnp.bfloat16
I32 = jnp.int32

D_MODEL = 1024
SB_HEADS = 8
SB_WIDTH = 512
MLA_HEADS = 8
MLA_QK_DIM = 96
MLA_NOPE_DIM = 64
MLA_ROPE_DIM = 32
MLA_WIDTH = 512
Q_LORA_RANK = 384
KV_LORA_RANK = 256
ROPE_THETA = 10000.0
EPS = 1e-6
LANES = 128
HALF_LANES = LANES // 2
HEAD_PAD = 128
MLA_PAD_WIDTH = MLA_HEADS * HEAD_PAD

C_Q, C_K, C_V, C_G = 0, 512, 1024, 1536
C_CQ, C_CKV, C_GM, C_KR = 2048, 2432, 2688, 3200
IN_COLS_PAD = 3328

ADAM_LR = 0.001
ADAM_B1 = 0.9
ADAM_B2 = 0.999
ADAM_EPS = 1e-08
ADAM_WD = 0.01
ADAM_STEP = 10

SB_SCALE = 0.125
SB_GROUP = 4
MLA_SCALE = 1.0 / math.sqrt(MLA_QK_DIM)
LN2 = math.log(2.0)
MLA_SCALE_LOG2 = MLA_SCALE / LN2
MLA_BQ = 1024
MLA_BWD_BQ = 1024
MLA_BK = 1024
MLA_BWD_BK = 1024
MLA_STEP = 512
MLA_BWD_STEP = 256
SB_DEAD = -104.0
MASK_NEG = -1e30

VMEM_LIMIT = 56 * 1024 * 1024
MESH = pl.DeviceIdType.MESH


def _dot(a, b):
    return jnp.dot(a, b, preferred_element_type=F32)


def _dot_nt(a, b):
    return lax.dot_general(a, b, (((1,), (1,)), ((), ())), preferred_element_type=F32)


def _dot_tn(a, b):
    return lax.dot_general(a, b, (((0,), (0,)), ((), ())), preferred_element_type=F32)


def _sigmoid(x):
    return 1.0 / (1.0 + jnp.exp(-x))


def _split_dot(a, m):
    hi = a.astype(BF16)
    lo = (a - hi.astype(F32)).astype(BF16)
    return _dot(hi, m) + _dot(lo, m)


def _params(sem, vmem=None):
    return pltpu.CompilerParams(dimension_semantics=sem, vmem_limit_bytes=vmem)


def _row_tile(s, want):
    return min(want, s)


def _hbm_spec():
    return pl.BlockSpec(memory_space=pltpu.HBM)


def _allgather_rows_call(row):
    def body(in_ref, out_ref, send_sems, recv_sems, loc_sem):
        x, y, c = lax.axis_index("x"), lax.axis_index("y"), lax.axis_index("c")
        flips = [(fx, fy, fc) for fx in (0, 1) for fy in (0, 1) for fc in (0, 1)][1:]

        def peer(r):
            fx, fy, fc = flips[r]
            return ((1 - x) if fx else x, (1 - y) if fy else y, (1 - c) if fc else c)

        def copy(r, slot):
            return pltpu.make_async_remote_copy(
                src_ref=in_ref, dst_ref=out_ref.at[slot], send_sem=send_sems.at[r], recv_sem=recv_sems.at[r],
                device_id=peer(r), device_id_type=MESH)

        local = pltpu.make_async_copy(in_ref, out_ref.at[4 * x + 2 * y + c], loc_sem)
        local.start()
        sends = [copy(r, 4 * x + 2 * y + c) for r in range(7)]
        for cp in sends:
            cp.start()
        for r in range(7):
            px, py, pc = peer(r)
            copy(r, 4 * px + 2 * py + pc).wait_recv()
        for cp in sends:
            cp.wait_send()
        local.wait()

    return pl.pallas_call(
        body, name="gather_rows",
        out_shape=jax.ShapeDtypeStruct((8,) + row.shape, row.dtype),
        in_specs=[_hbm_spec()], out_specs=_hbm_spec(),
        scratch_shapes=[pltpu.SemaphoreType.DMA((7,)), pltpu.SemaphoreType.DMA((7,)), pltpu.SemaphoreType.DMA],
    )(row)


def _gather_call(shards, split):
    n = len(shards)
    halves = [s.shape[1] // 2 for s in shards]

    def body(*refs):
        ins, outs = refs[:n], refs[n:2 * n]
        ici_send, ici_recv, d2d_send, d2d_recv, loc_sems = refs[2 * n:]
        x, y, c = lax.axis_index("x"), lax.axis_index("y"), lax.axis_index("c")
        me = 2 * x + y
        peers = [(1 - x, y), (x, 1 - y), (1 - x, 1 - y)]

        def rows(a, which):
            return pl.ds(pl.multiple_of(which * halves[a], 16), halves[a])

        def ici(a, j, slot):
            px, py = peers[j]
            src, dst = ins[a].at[0], outs[a].at[slot]
            if split[a]:
                src, dst = src.at[rows(a, c)], dst.at[rows(a, c)]
            return pltpu.make_async_remote_copy(
                src_ref=src, dst_ref=dst,
                send_sem=ici_send.at[3 * a + j], recv_sem=ici_recv.at[3 * a + j],
                device_id=(px, py, c), device_id_type=MESH)

        def d2d(a, j, which):
            px, py = peers[j]
            piece = outs[a].at[2 * px + py, rows(a, which)]
            return pltpu.make_async_remote_copy(
                src_ref=piece, dst_ref=piece,
                send_sem=d2d_send.at[3 * a + j], recv_sem=d2d_recv.at[3 * a + j],
                device_id=(x, y, 1 - c), device_id_type=MESH)

        local = [pltpu.make_async_copy(ins[a].at[0], outs[a].at[me], loc_sems.at[a]) for a in range(n)]
        for cp in local:
            cp.start()
        sends = [ici(a, j, me) for a in range(n) for j in range(3)]
        for cp in sends:
            cp.start()
        for a in range(n):
            for j in range(3):
                px, py = peers[j]
                ici(a, j, 2 * px + py).wait_recv()
                if split[a]:
                    cp = d2d(a, j, c)
                    cp.start()
                    sends.append(cp)
        for a in range(n):
            for j in range(3):
                if split[a]:
                    d2d(a, j, 1 - c).wait_recv()
        for cp in sends:
            cp.wait_send()
        for cp in local:
            cp.wait()

    return pl.pallas_call(
        body, name="gather_weights",
        out_shape=[jax.ShapeDtypeStruct((4,) + s.shape[1:], s.dtype) for s in shards],
        in_specs=[_hbm_spec() for _ in shards],
        out_specs=[_hbm_spec() for _ in shards],
        scratch_shapes=[pltpu.SemaphoreType.DMA((3 * n,)), pltpu.SemaphoreType.DMA((3 * n,)),
                        pltpu.SemaphoreType.DMA((3 * n,)), pltpu.SemaphoreType.DMA((3 * n,)),
                        pltpu.SemaphoreType.DMA((n,))],
    )(*shards)


def _swap_halves_call(grads):
    n = len(grads)
    halves = [g.shape[1] // 2 for g in grads]

    def body(*refs):
        g_in, hs, send_sems, recv_sems = refs[:n], refs[n:2 * n], refs[2 * n], refs[2 * n + 1]
        x, y, c = lax.axis_index("x"), lax.axis_index("y"), lax.axis_index("c")
        copies = []
        for a in range(n):
            theirs = pl.ds(pl.multiple_of((1 - c) * halves[a], 16), halves[a])
            copies.append(pltpu.make_async_remote_copy(
                src_ref=g_in[a].at[:, theirs], dst_ref=hs[a], send_sem=send_sems.at[a], recv_sem=recv_sems.at[a],
                device_id=(x, y, 1 - c), device_id_type=MESH))
        for cp in copies:
            cp.start()
        for cp in copies:
            cp.wait()

    return pl.pallas_call(
        body, name="swap_halves",
        out_shape=[jax.ShapeDtypeStruct((4, h, g.shape[2]), g.dtype) for g, h in zip(grads, halves)],
        in_specs=[_hbm_spec() for _ in grads], out_specs=[_hbm_spec() for _ in grads],
        scratch_shapes=[pltpu.SemaphoreType.DMA((n,)), pltpu.SemaphoreType.DMA((n,))],
    )(*grads)


def _sum_halves_call(core, grads, halves):
    n = len(grads)

    def body(core_ref, *refs):
        for g_ref, h_ref, o_ref in zip(refs[:n], refs[n:2 * n], refs[2 * n:]):
            o_ref[...] = (g_ref[...].astype(F32) + h_ref[...].astype(F32)).astype(o_ref.dtype)

    whole = lambda h: pl.BlockSpec(h.shape, lambda i, core_ref: (0, 0, 0))
    return pl.pallas_call(
        body, name="sum_halves",
        grid_spec=pltpu.PrefetchScalarGridSpec(
            num_scalar_prefetch=1, grid=(1,),
            in_specs=[pl.BlockSpec(h.shape, lambda i, core_ref: (0, core_ref[0], 0)) for h in halves]
            + [whole(h) for h in halves],
            out_specs=[whole(h) for h in halves]),
        out_shape=[jax.ShapeDtypeStruct(h.shape, h.dtype) for h in halves],
        compiler_params=_params(("arbitrary",), VMEM_LIMIT),
    )(core, *grads, *halves)


def _exchange_call(chip_halves, small):
    n = len(chip_halves)
    halves = [h.shape[1] for h in chip_halves]

    def body(*refs):
        g_in, small_in = refs[:n], refs[n]
        parts, packs = refs[n + 1:2 * n + 1], refs[2 * n + 1]
        ici_send, ici_recv, d2d_send, d2d_recv, sm_send, sm_recv, loc_sems = refs[2 * n + 2:]
        x, y, c = lax.axis_index("x"), lax.axis_index("y"), lax.axis_index("c")
        me = 2 * x + y
        me8 = 4 * x + 2 * y + c
        sibling = (x, y, 1 - c)
        peers = [(1 - x, y), (x, 1 - y), (1 - x, 1 - y)]
        flips = [(fx, fy, fc) for fx in (0, 1) for fy in (0, 1) for fc in (0, 1)][1:]

        def rows(a, which):
            return pl.ds(pl.multiple_of(which * halves[a], 16), halves[a])

        def ici(a, j, src_slot, dst_slot):
            px, py = peers[j]
            return pltpu.make_async_remote_copy(
                src_ref=g_in[a].at[src_slot], dst_ref=parts[a].at[dst_slot, rows(a, c)],
                send_sem=ici_send.at[3 * a + j], recv_sem=ici_recv.at[3 * a + j],
                device_id=(px, py, c), device_id_type=MESH)

        def d2d(a, rel, chip, which):
            piece = parts[a].at[chip, rows(a, which)]
            return pltpu.make_async_remote_copy(
                src_ref=piece, dst_ref=piece,
                send_sem=d2d_send.at[4 * a + rel], recv_sem=d2d_recv.at[4 * a + rel],
                device_id=sibling, device_id_type=MESH)

        def flipped(r):
            fx, fy, fc = flips[r]
            return ((1 - x) if fx else x, (1 - y) if fy else y, (1 - c) if fc else c)

        def sm(r, slot):
            return pltpu.make_async_remote_copy(
                src_ref=small_in, dst_ref=packs.at[slot],
                send_sem=sm_send.at[r], recv_sem=sm_recv.at[r],
                device_id=flipped(r), device_id_type=MESH)

        def peer8(r):
            px, py, pc = flipped(r)
            return 4 * px + 2 * py + pc

        local = [pltpu.make_async_copy(g_in[a].at[me], parts[a].at[me, rows(a, c)], loc_sems.at[a])
                 for a in range(n)]
        local.append(pltpu.make_async_copy(small_in, packs.at[me8], loc_sems.at[n]))
        for cp in local:
            cp.start()
        sends = []
        for r in range(7):
            sends.append(sm(r, me8))
        for a in range(n):
            for j in range(3):
                px, py = peers[j]
                sends.append(ici(a, j, 2 * px + py, me))
        for cp in sends:
            cp.start()
        for a in range(n):
            local[a].wait()
            cp = d2d(a, 0, me, c)
            cp.start()
            sends.append(cp)
        for a in range(n):
            for j in range(3):
                px, py = peers[j]
                ici(a, j, me, 2 * px + py).wait_recv()
                cp = d2d(a, 1 + j, 2 * px + py, c)
                cp.start()
                sends.append(cp)
        for a in range(n):
            d2d(a, 0, me, 1 - c).wait_recv()
            for j in range(3):
                px, py = peers[j]
                d2d(a, 1 + j, 2 * px + py, 1 - c).wait_recv()
        for r in range(7):
            sm(r, peer8(r)).wait_recv()
        for cp in sends:
            cp.wait_send()
        local[n].wait()

    out_shape = ([jax.ShapeDtypeStruct((4, 2 * h.shape[1], h.shape[2]), h.dtype) for h in chip_halves]
                 + [jax.ShapeDtypeStruct((8,) + small.shape, small.dtype)])
    res = pl.pallas_call(
        body, name="exchange_grads",
        out_shape=out_shape,
        in_specs=[_hbm_spec() for _ in range(n + 1)],
        out_specs=[_hbm_spec() for _ in range(n + 1)],
        scratch_shapes=[pltpu.SemaphoreType.DMA((3 * n,)), pltpu.SemaphoreType.DMA((3 * n,)),
                        pltpu.SemaphoreType.DMA((4 * n,)), pltpu.SemaphoreType.DMA((4 * n,)),
                        pltpu.SemaphoreType.DMA((7,)), pltpu.SemaphoreType.DMA((7,)),
                        pltpu.SemaphoreType.DMA((n + 1,))],
    )(*chip_halves, small)
    return res[:n], res[n]


def _ada_call(c_all, w_ada_cols, b_ada_cols):
    def body(c_ref, w_ref, b_ref, o_ref):
        cc = c_ref[...]
        o_ref[...] = _dot((cc * _sigmoid(cc)).astype(BF16), w_ref[...].astype(BF16)) + b_ref[...]

    return pl.pallas_call(
        body, name="ada_fwd",
        out_shape=jax.ShapeDtypeStruct((c_all.shape[0], w_ada_cols.shape[1]), F32),
        compiler_params=pltpu.CompilerParams(vmem_limit_bytes=VMEM_LIMIT),
    )(c_all, w_ada_cols, b_ada_cols)


def _ada_part(j):
    return pl.BlockSpec((1, D_MODEL), lambda i: (0, j))


def _full(shape):
    return pl.BlockSpec(shape, lambda i: (0,) * len(shape))


def _rows(tm, width):
    return pl.BlockSpec((tm, width), lambda i: (i, 0))


def _rope(t, cos_t, sin_t):
    return t * cos_t + pltpu.roll(t, HALF_LANES, 1) * sin_t


def _rope_adjoint(d, cos_t, sin_t):
    return d * cos_t + pltpu.roll(d * sin_t, HALF_LANES, 1)


def _pre_call(x, ada, norm_w, w_in_bf, cos_t, sin_t, q_lora_norm, kv_lora_norm, qhn_pad, khn_pad,
              w_uq_bf, w_uk_bf, w_uv_bf):
    s = x.shape[0]
    tm = _row_tile(s, 512)
    out_defs = [(D_MODEL, BF16), (512, BF16), (512, BF16), (512, BF16), (512, F32),
                (Q_LORA_RANK, F32), (KV_LORA_RANK, F32), (512, F32),
                (MLA_PAD_WIDTH, BF16), (MLA_PAD_WIDTH, BF16), (MLA_WIDTH, BF16),
                (Q_LORA_RANK, BF16), (KV_LORA_RANK, BF16), (MLA_PAD_WIDTH, F32), (MLA_PAD_WIDTH, F32)]

    def body(x_ref, sh_ref, sc_ref, nw_ref, w_ref, cos_ref, sin_ref, qln_ref, kvln_ref, qhn_ref, khn_ref,
             wuq_ref, wuk_ref, wuv_ref,
             hb_ref, qsb_ref, ksb_ref, vsb_ref, gsb_ref, cq_ref, ckv_ref, gm_ref,
             q_ref, k_ref, v_ref, cqn_ref, ckvn_ref, q0_ref, k0_ref):
        xx = x_ref[...]
        r0 = lax.rsqrt(jnp.mean(xx * xx, axis=-1, keepdims=True) + EPS)
        hb = ((xx * r0 * nw_ref[...]) * (1.0 + sc_ref[...]) + sh_ref[...]).astype(BF16)
        hb_ref[...] = hb

        def proj(c0, width):
            return _dot(hb, w_ref[:, c0:c0 + width])

        cq = proj(C_CQ, Q_LORA_RANK)
        ckv = proj(C_CKV, KV_LORA_RANK)
        kr = proj(C_KR, LANES)
        cq_ref[...] = cq
        ckv_ref[...] = ckv
        cqn = (cq * lax.rsqrt(jnp.mean(cq * cq, axis=-1, keepdims=True) + EPS) * qln_ref[...]).astype(BF16)
        cqn_ref[...] = cqn
        ckvn = (ckv * lax.rsqrt(jnp.mean(ckv * ckv, axis=-1, keepdims=True) + EPS) * kvln_ref[...]).astype(BF16)
        ckvn_ref[...] = ckvn
        qsb_ref[...] = proj(C_Q, 512).astype(BF16)
        v_ref[...] = _dot(ckvn, wuv_ref[...]).astype(BF16)
        q0_ref[...] = _dot(cqn, wuq_ref[...])
        k0_ref[...] = _dot(ckvn, wuk_ref[...])
        ksb_ref[...] = proj(C_K, 512).astype(BF16)
        cos_t, sin_t = cos_ref[...], sin_ref[...]
        heads = [slice(h * HEAD_PAD, (h + 1) * HEAD_PAD) for h in range(MLA_HEADS)]
        for cols in heads:
            k0_ref[:, cols] = k0_ref[:, cols] + kr

        def inv_rms(ref):
            sums = [jnp.sum(ref[:, cols] * ref[:, cols], axis=-1, keepdims=True) for cols in heads]
            return [lax.rsqrt(t * (1.0 / MLA_QK_DIM) + EPS) for t in sums]

        rqs = inv_rms(q0_ref)
        vsb_ref[...] = proj(C_V, 512).astype(BF16)
        rks = inv_rms(k0_ref)
        gsb_ref[...] = proj(C_G, 512)
        for cols, rq, rk in zip(heads, rqs, rks):
            q_ref[:, cols] = (_rope(q0_ref[:, cols] * rq * qhn_ref[...], cos_t, sin_t) * MLA_SCALE_LOG2).astype(BF16)
            k_ref[:, cols] = _rope(k0_ref[:, cols] * rk * khn_ref[...], cos_t, sin_t).astype(BF16)
        gm_ref[...] = proj(C_GM, 512)

    return pl.pallas_call(
        body, name="pre_proj",
        grid=(s // tm,),
        in_specs=[_rows(tm, D_MODEL), _ada_part(0), _ada_part(1), _full((1, D_MODEL)),
                  pl.BlockSpec((D_MODEL, IN_COLS_PAD), lambda i: (0, 0), pipeline_mode=pl.Buffered(1)),
                  _rows(tm, LANES), _rows(tm, LANES),
                  _full((1, Q_LORA_RANK)), _full((1, KV_LORA_RANK)), _full((1, LANES)), _full((1, LANES)),
                  _full((Q_LORA_RANK, MLA_PAD_WIDTH)), _full((KV_LORA_RANK, MLA_PAD_WIDTH)),
                  _full((KV_LORA_RANK, MLA_WIDTH))],
        out_specs=[_rows(tm, w) for w, _ in out_defs],
        out_shape=[jax.ShapeDtypeStruct((s, w), dt) for w, dt in out_defs],
        compiler_params=_params(("arbitrary",), VMEM_LIMIT),
    )(x, ada, ada, norm_w, w_in_bf, cos_t, sin_t, q_lora_norm, kv_lora_norm, qhn_pad, khn_pad,
      w_uq_bf, w_uk_bf, w_uv_bf)


def _log_sigmoid_pair(z):
    ls = jnp.minimum(z, 0.0) - jnp.log(1.0 + jnp.exp(-jnp.abs(z)))
    return ls, ls - z


def _pair(hh):
    return slice((hh // 2) * LANES, (hh // 2 + 1) * LANES)


def _sb_fwd_call(q, k, v):
    s = q.shape[0]
    bq = _row_tile(s, 256)
    nq = s // bq
    nh = SB_GROUP
    width = nh * HALF_LANES

    def body(q_ref, k_ref, v_ref, o_ref, r_ref, ks_ref):
        hp, i = pl.program_id(0), pl.program_id(1)
        lane = lax.broadcasted_iota(I32, (bq, LANES), 1)
        row = lax.broadcasted_iota(I32, (bq, bq), 0)
        col = lax.broadcasted_iota(I32, (bq, bq), 1)
        strict = col < row
        later = jnp.where(row > col, 1.0, 0.0).astype(BF16)
        masks = [_head_mask(lane, hh).astype(BF16) for hh in range(2)]
        qms = [q_ref[:, _pair(hh)] * jnp.asarray(SB_SCALE, BF16) * masks[hh % 2] for hh in range(nh)]

        def walk(blocks, state):
            chains = [(kb, diagonal, hh) for kb, diagonal in blocks for hh in range(nh)]
            keys = lambda kb: pl.ds(pl.multiple_of(kb * bq, bq), bq)
            zs = [_dot_nt(qms[hh], k_ref[keys(kb), _pair(hh)]) for kb, _, hh in chains]
            pairs = []
            for z, (_, diagonal, _) in zip(zs, chains):
                ls, lk = _log_sigmoid_pair(z)
                pairs.append((ls, jnp.where(strict, lk, 0.0) if diagonal else lk))
            sums = [_split_dot(lk, later) for _, lk in pairs]
            runs = [st[0] for st in state]
            ws = []
            for (ls, lk), after, (_, diagonal, hh) in zip(pairs, sums, chains):
                w = jnp.exp(ls + (after + runs[hh]))
                ws.append((jnp.where(strict, w, 0.0) if diagonal else w).astype(BF16))
                runs[hh] = runs[hh] + jnp.sum(lk, axis=1, keepdims=True)
            accs = [st[1] for st in state]
            for w, (kb, _, hh) in zip(ws, chains):
                accs[hh] = accs[hh] + _dot(w, v_ref[keys(kb), _pair(hh)])
            return tuple(zip(runs, accs))

        def alive(state):
            top = jnp.max(state[0][0])
            for st in state[1:]:
                top = jnp.maximum(top, jnp.max(st[0]))
            return (top > SB_DEAD).astype(I32)

        def finish(state, first):
            ks_ref[hp, i] = first
            for pair in range(nh // 2):
                o_ref[:, _pair(2 * pair)] = jnp.where(lane < HALF_LANES, state[2 * pair][1], state[2 * pair + 1][1])
                r_ref[:, _pair(2 * pair)] = jnp.where(lane < HALF_LANES, state[2 * pair][0], state[2 * pair + 1][0])

        zero = ((jnp.zeros((bq, 1), F32), jnp.zeros((bq, LANES), F32)),) * nh

        @pl.when(i == 0)
        def _():
            finish(walk([(0, True)], zero), 0)

        @pl.when(i > 0)
        def _():
            state = walk([(i, True), (i - 1, False)], zero)

            def cond(carry):
                return jnp.logical_and(carry[0] >= 0, carry[1] > 0)

            def step(carry):
                state = walk([(carry[0], False)], carry[2])
                return carry[0] - 1, alive(state), state

            kb, _, state = lax.while_loop(cond, step, (i - 2, alive(state), state))
            finish(state, kb + 1)

    return pl.pallas_call(
        body, name="sb_fwd",
        grid=(SB_HEADS // nh, nq),
        in_specs=[pl.BlockSpec((bq, width), lambda h, i: (i, h)),
                  pl.BlockSpec((s, width), lambda h, i: (0, h)),
                  pl.BlockSpec((s, width), lambda h, i: (0, h))],
        out_specs=[pl.BlockSpec((bq, width), lambda h, i: (i, h)),
                   pl.BlockSpec((bq, width), lambda h, i: (i, h)),
                   pl.BlockSpec(memory_space=pltpu.SMEM)],
        out_shape=[jax.ShapeDtypeStruct((s, SB_WIDTH), F32), jax.ShapeDtypeStruct((s, SB_WIDTH), F32),
                   jax.ShapeDtypeStruct((SB_HEADS // nh, nq), I32)],
        compiler_params=_params(("arbitrary", "arbitrary"), VMEM_LIMIT),
    )(q, k, v)


def _mla_fwd_call(q, k, v):
    s = q.shape[0]
    bq = _row_tile(s, MLA_BQ)
    bk = _row_tile(s, MLA_BK)
    nq = s // bq
    assert bk % bq == 0
    step = min(bk // 2, MLA_STEP)
    nsub = bk // step
    assert nsub % 2 == 0

    def body(q_ref, k_ref, v_ref, o_ref, lse_ref, p_ref, s_ref):
        i = pl.program_id(1)
        lane = lax.broadcasted_iota(I32, (bq, LANES), 1)
        row = lax.broadcasted_iota(I32, (step, bq), 1)
        col = lax.broadcasted_iota(I32, (step, bq), 0)
        n_full = (i * bq) // bk

        def keys(g):
            return pl.ds(pl.multiple_of(g * step, step), step)

        def join(left, right, qlo):
            return right if qlo == 0 else jnp.concatenate([left[:, :qlo], right], axis=1)

        def put_scores(g, slot, qlo=0):
            for hh in range(2):
                cols = slice(hh * HEAD_PAD, (hh + 1) * HEAD_PAD)
                s_ref[slot, hh, :, qlo:] = _dot_nt(k_ref[keys(g), cols], q_ref[qlo:, cols])

        def add_pv(carry, g, slot, qlo=0):
            vblk = v_ref[keys(g), :]
            out = []
            for hh, (m, l, acc, alpha) in enumerate(carry):
                upd = alpha[:, qlo:] * acc[:, qlo:] + _dot_tn(vblk, p_ref[slot, hh, :, qlo:])
                out.append((m, l, join(acc, upd, qlo), alpha))
            return tuple(out)

        def substep(g, slot, carry, masked, prefetch, qlo=0, next_qlo=0, prev_qlo=0):
            if prefetch:
                put_scores(g + 1, 1 - slot, next_qlo)
            carry = add_pv(carry, jnp.maximum(g - 1, 0), 1 - slot, prev_qlo)
            new = []
            for hh in range(2):
                m, l, acc, _ = carry[hh]
                sc = s_ref[slot, hh, :, qlo:]
                if masked:
                    sc = jnp.where(col[:, qlo:] + g * step <= row[:, qlo:] + i * bq, sc, MASK_NEG)
                m_new = jnp.maximum(m[:, qlo:], jnp.max(sc, axis=0, keepdims=True))
                p = jnp.exp2(sc - m_new)
                alpha = jnp.exp2(m[:, qlo:] - m_new)
                l_new = alpha * l[:, qlo:] + jnp.sum(p, axis=0, keepdims=True)
                p_ref[slot, hh, :, qlo:] = p.astype(BF16)
                new.append((join(m, m_new, qlo), join(l, l_new, qlo), acc, join(jnp.ones_like(m), alpha, qlo)))
            return tuple(new)

        def first_query(t, masked):
            return t * step if (masked and bk == bq and 0 <= t < nsub) else 0

        def chunk(kb, carry, masked):
            for t in range(nsub):
                last = masked and t == nsub - 1
                carry = substep(nsub * kb + t, t % 2, carry, masked, not last, first_query(t, masked),
                                first_query(t + 1, masked), first_query(t - 1, masked))
            return carry

        p_ref[1] = jnp.zeros_like(p_ref[1])
        put_scores(0, 0)
        one = (jnp.full((1, bq), MASK_NEG, F32), jnp.zeros((1, bq), F32), jnp.zeros((LANES, bq), F32),
               jnp.ones((1, bq), F32))
        carry = lax.fori_loop(0, n_full, lambda kb, cr: chunk(kb, cr, False), (one, one))
        carry = chunk(n_full, carry, True)
        (m0, l0, a0, _), (m1, l1, a1, _) = add_pv(carry, nsub * n_full + nsub - 1, 1, first_query(nsub - 1, True))
        o_ref[...] = jnp.where(lane < HALF_LANES, (a0 / l0).T, (a1 / l1).T)
        sub = lax.broadcasted_iota(I32, (8, bq), 0)
        lse_ref[...] = jnp.where(sub == 0, m0 + jnp.log2(l0), jnp.where(sub == 1, m1 + jnp.log2(l1), 0.0))

    return pl.pallas_call(
        body, name="mla_fwd",
        grid=(4, nq),
        in_specs=[pl.BlockSpec((bq, 2 * HEAD_PAD), lambda h, i: (i, h)),
                  pl.BlockSpec((s, 2 * HEAD_PAD), lambda h, i: (0, h)),
                  pl.BlockSpec((s, LANES), lambda h, i: (0, h))],
        out_specs=[pl.BlockSpec((bq, LANES), lambda h, i: (i, h)),
                   pl.BlockSpec((None, 8, bq), lambda h, i: (h, 0, i))],
        out_shape=[jax.ShapeDtypeStruct((s, MLA_WIDTH), F32), jax.ShapeDtypeStruct((4, 8, s), F32)],
        scratch_shapes=[pltpu.VMEM((2, 2, step, bq), BF16), pltpu.VMEM((2, 2, step, bq), F32)],
        compiler_params=_params(("arbitrary", "arbitrary"), VMEM_LIMIT),
    )(q, k, v)


def _out_call(o_sb, g_sb, o_mla, g_mla, x, target, ada, w_out_bf):
    s = x.shape[0]
    tm = _row_tile(s, 512)

    def body(osb_ref, gsb_ref, oml_ref, gml_ref, x_ref, t_ref, gate_ref, w_ref,
             dosb_ref, doml_ref, dgsb_ref, dgml_ref, dy_ref, gw_ref, dgate_ref, sq_ref):
        @pl.when(pl.program_id(0) == 0)
        def _():
            gw_ref[...] = jnp.zeros_like(gw_ref)
            dgate_ref[...] = jnp.zeros_like(dgate_ref)
            sq_ref[...] = jnp.zeros_like(sq_ref)

        g_s, g_m = gsb_ref[...], gml_ref[...]
        sig_s, sig_m = _sigmoid(g_s), _sigmoid(g_m)
        silu_s, silu_m = g_s * sig_s, g_m * sig_m
        o_s, o_m = osb_ref[...], oml_ref[...]
        mixed = jnp.concatenate([o_s * silu_s, o_m * silu_m], axis=1).astype(BF16)
        u = _dot(mixed, w_ref[...])
        gate_v = gate_ref[...]
        err = x_ref[...] + gate_v * u - t_ref[...]
        sq_ref[...] += jnp.sum(err * err, axis=0, keepdims=True)
        dy = err * (1.0 / D_MODEL)
        dy_ref[...] = dy
        dgate_ref[...] += jnp.sum(dy * u, axis=0, keepdims=True)
        du = (dy * gate_v).astype(BF16)
        gw_ref[...] += _dot_tn(mixed, du)
        dmix = _dot_nt(du, w_ref[...])
        dm_s, dm_m = dmix[:, :SB_WIDTH], dmix[:, SB_WIDTH:]
        dosb_ref[...] = (dm_s * silu_s).astype(BF16)
        doml_ref[...] = (dm_m * silu_m).astype(BF16)
        dgsb_ref[...] = (dm_s * o_s * (sig_s * (1.0 + g_s * (1.0 - sig_s)))).astype(BF16)
        dgml_ref[...] = (dm_m * o_m * (sig_m * (1.0 + g_m * (1.0 - sig_m)))).astype(BF16)

    return pl.pallas_call(
        body, name="out_proj_loss",
        grid=(s // tm,),
        in_specs=[_rows(tm, 512), _rows(tm, 512), _rows(tm, 512), _rows(tm, 512),
                  _rows(tm, D_MODEL), _rows(tm, D_MODEL), _ada_part(2), _full((D_MODEL, D_MODEL))],
        out_specs=[_rows(tm, 512), _rows(tm, 512), _rows(tm, 512), _rows(tm, 512), _rows(tm, D_MODEL),
                   _full((D_MODEL, D_MODEL)), _full((1, D_MODEL)), _full((1, D_MODEL))],
        out_shape=[jax.ShapeDtypeStruct((s, 512), BF16)] * 4
        + [jax.ShapeDtypeStruct((s, D_MODEL), F32), jax.ShapeDtypeStruct((D_MODEL, D_MODEL), F32),
           jax.ShapeDtypeStruct((1, D_MODEL), F32), jax.ShapeDtypeStruct((1, D_MODEL), F32)],
        compiler_params=_params(("arbitrary",), VMEM_LIMIT),
    )(o_sb, g_sb, o_mla, g_mla, x, target, ada, w_out_bf)


def _head_mask(lane, hh):
    return jnp.where((lane >= HALF_LANES) if hh else (lane < HALF_LANES), 1.0, 0.0)


def _pick_lane(packed, lane, which):
    return jnp.sum(jnp.where(lane == which, packed, 0.0), axis=1, keepdims=True)


def _sb_bwd_call(kstart, q, k, v, do, rfin):
    s = q.shape[0]
    bq = _row_tile(s, 256)
    nq = s // bq
    nh = SB_GROUP
    width = nh * HALF_LANES

    def body(ks_ref, q_ref, k_ref, v_ref, do_ref, r_ref, dq_ref, dk_ref, dv_ref):
        hp, i = pl.program_id(0), pl.program_id(1)

        @pl.when(i == 0)
        def _():
            dk_ref[...] = jnp.zeros_like(dk_ref)
            dv_ref[...] = jnp.zeros_like(dv_ref)

        lane = lax.broadcasted_iota(I32, (bq, LANES), 1)
        row = lax.broadcasted_iota(I32, (bq, bq), 0)
        col = lax.broadcasted_iota(I32, (bq, bq), 1)
        upto = jnp.where(row <= col, 1.0, 0.0).astype(BF16)
        before = jnp.where(row < col, 1.0, 0.0).astype(BF16)
        masks = [_head_mask(lane, hh).astype(BF16) for hh in range(2)]
        qms = [q_ref[:, _pair(hh)] * jnp.asarray(SB_SCALE, BF16) * masks[hh % 2] for hh in range(nh)]
        doms = [do_ref[:, _pair(hh)] * masks[hh % 2] for hh in range(nh)]
        totals = [_pick_lane(r_ref[:, _pair(hh)], lane, HALF_LANES * (hh % 2)) for hh in range(nh)]
        strict = col < row

        def walk(blocks, state):
            chains = [(kb, diagonal, hh) for kb, diagonal in blocks for hh in range(nh)]
            keys = lambda kb: pl.ds(pl.multiple_of(kb * bq, bq), bq)
            cut = lambda x, diagonal: jnp.where(strict, x, 0.0) if diagonal else x
            zs = [_dot_nt(qms[hh], k_ref[keys(kb), _pair(hh)]) for kb, _, hh in chains]
            dws = [_dot_nt(doms[hh], v_ref[keys(kb), _pair(hh)]) for kb, _, hh in chains]
            pairs = []
            for z, (_, diagonal, _) in zip(zs, chains):
                ls, lk = _log_sigmoid_pair(z)
                pairs.append((ls, cut(lk, diagonal)))
            incls = [_split_dot(lk, upto) for _, lk in pairs]
            pres = [st[0] for st in state]
            ws, gs = [], []
            for (ls, lk), incl, dw, (_, diagonal, hh) in zip(pairs, incls, dws, chains):
                w = cut(jnp.exp(ls + ((totals[hh] - pres[hh]) - incl)), diagonal)
                ws.append(w.astype(BF16))
                gs.append(w * dw)
                pres[hh] = pres[hh] + jnp.sum(lk, axis=1, keepdims=True)
            gsums = [_dot(g.astype(BF16), before) for g in gs]
            gpres = [st[1] for st in state]
            dzs = []
            for (ls, _), g, gsum, (_, diagonal, hh) in zip(pairs, gs, gsums, chains):
                dzs.append(cut(g - jnp.exp(ls) * (g + (gpres[hh] + gsum)), diagonal).astype(BF16))
                gpres[hh] = gpres[hh] + jnp.sum(g, axis=1, keepdims=True)
            dqs = [st[2] for st in state]
            dk_parts, dv_parts = [], []
            for dzb, w, (kb, _, hh) in zip(dzs, ws, chains):
                dk_parts.append(_dot_tn(dzb, qms[hh]))
                dv_parts.append(_dot_tn(w, doms[hh]))
                dqs[hh] = dqs[hh] + _dot(dzb, k_ref[keys(kb), _pair(hh)])
            for b, (kb, _) in enumerate(blocks):
                for pair in range(nh // 2):
                    c0 = b * nh + 2 * pair
                    dk_ref[keys(kb), _pair(2 * pair)] += dk_parts[c0] + dk_parts[c0 + 1]
                    dv_ref[keys(kb), _pair(2 * pair)] += dv_parts[c0] + dv_parts[c0 + 1]
            return tuple(zip(pres, gpres, dqs))

        def finish(state):
            for pair in range(nh // 2):
                both = jnp.where(lane < HALF_LANES, state[2 * pair][2], state[2 * pair + 1][2])
                dq_ref[:, _pair(2 * pair)] = (both * SB_SCALE).astype(BF16)

        zero = ((jnp.zeros((bq, 1), F32), jnp.zeros((bq, 1), F32), jnp.zeros((bq, LANES), F32)),) * nh

        @pl.when(i == 0)
        def _():
            finish(walk([(0, True)], zero))

        @pl.when(i > 0)
        def _():
            state = lax.fori_loop(ks_ref[hp, i], i - 1, lambda kb, st: walk([(kb, False)], st), zero)
            finish(walk([(i - 1, False), (i, True)], state))

    return pl.pallas_call(
        body, name="sb_bwd",
        grid_spec=pltpu.PrefetchScalarGridSpec(
            num_scalar_prefetch=1, grid=(SB_HEADS // nh, nq),
            in_specs=[pl.BlockSpec((bq, width), lambda h, i, ks: (i, h)),
                      pl.BlockSpec((s, width), lambda h, i, ks: (0, h), pipeline_mode=pl.Buffered(1)),
                      pl.BlockSpec((s, width), lambda h, i, ks: (0, h), pipeline_mode=pl.Buffered(1)),
                      pl.BlockSpec((bq, width), lambda h, i, ks: (i, h)),
                      pl.BlockSpec((bq, width), lambda h, i, ks: (i, h))],
            out_specs=[pl.BlockSpec((bq, width), lambda h, i, ks: (i, h)),
                       pl.BlockSpec((s, width), lambda h, i, ks: (0, h), pipeline_mode=pl.Buffered(1)),
                       pl.BlockSpec((s, width), lambda h, i, ks: (0, h), pipeline_mode=pl.Buffered(1))]),
        out_shape=[jax.ShapeDtypeStruct((s, SB_WIDTH), BF16), jax.ShapeDtypeStruct((s, SB_WIDTH), F32),
                   jax.ShapeDtypeStruct((s, SB_WIDTH), F32)],
        compiler_params=_params(("arbitrary", "arbitrary"), VMEM_LIMIT),
    )(kstart, q, k, v, do, rfin)


def _mla_bwd_call(q, k, v, do, o, lse):
    s = q.shape[0]
    bq = _row_tile(s, MLA_BWD_BQ)
    bk = _row_tile(s, MLA_BWD_BK)
    nq = s // bq
    assert bk % bq == 0
    step = min(bk // 2, MLA_BWD_STEP)
    nsub = bk // step
    assert nsub % 2 == 0

    def body(q_ref, k_ref, v_ref, do_ref, o_ref, lse_ref, dq_ref, dk_ref, dv_ref, dom_ref, s_ref, dp_ref, pb_ref,
             ds_ref):
        i = pl.program_id(1)

        @pl.when(i == 0)
        def _():
            dk_ref[...] = jnp.zeros_like(dk_ref)
            dv_ref[...] = jnp.zeros_like(dv_ref)

        lane = lax.broadcasted_iota(I32, (bq, LANES), 1)
        row = lax.broadcasted_iota(I32, (step, bq), 1)
        col = lax.broadcasted_iota(I32, (step, bq), 0)
        n_full = (i * bq) // bk
        do2 = do_ref[...]
        prod = do2.astype(F32) * o_ref[...]
        ones = jnp.ones((8, LANES), BF16)
        deltas, lses = [], []
        for hh in range(2):
            head = _head_mask(lane, hh)
            dom_ref[hh] = do2 * head.astype(BF16)
            part = prod * head
            hi = part.astype(BF16)
            lo = (part - hi.astype(F32)).astype(BF16)
            deltas.append((_dot_nt(ones, hi) + _dot_nt(ones, lo))[0:1])
            lses.append(lse_ref[hh:hh + 1, :])

        def keys(g):
            return pl.ds(pl.multiple_of(g * step, step), step)

        def heads():
            return [(hh, slice(hh * HEAD_PAD, (hh + 1) * HEAD_PAD)) for hh in range(2)]

        def put_products(g, slot, qlo=0):
            vblk = v_ref[keys(g), :]
            for hh, cols in heads():
                s_ref[slot, hh, :, qlo:] = _dot_nt(k_ref[keys(g), cols], q_ref[qlo:, cols])
                dp_ref[slot, hh, :, qlo:] = _dot_nt(vblk, dom_ref[hh, qlo:, :])

        def add_grads(dqs, g, slot, qlo=0):
            rows = keys(g)
            new, dv_parts = [], []
            for hh, cols in heads():
                ds = ds_ref[slot, hh, :, qlo:]
                dk_ref[rows, cols] += _dot(ds, q_ref[qlo:, cols])
                dv_parts.append(_dot(pb_ref[slot, hh, :, qlo:], dom_ref[hh, qlo:, :]))
                upd = dqs[hh][:, qlo:] + _dot_tn(k_ref[rows, cols], ds)
                new.append(upd if qlo == 0 else jnp.concatenate([dqs[hh][:, :qlo], upd], axis=1))
            dv_ref[rows, :] += dv_parts[0] + dv_parts[1]
            return tuple(new)

        def substep(g, slot, dqs, masked, prefetch, qlo=0, next_qlo=0, prev_qlo=0):
            if prefetch:
                put_products(g + 1, 1 - slot, next_qlo)
            dqs = add_grads(dqs, jnp.maximum(g - 1, 0), 1 - slot, prev_qlo)
            for hh, _ in heads():
                p = jnp.exp2(s_ref[slot, hh, :, qlo:] - lses[hh][:, qlo:])
                if masked:
                    p = jnp.where(col[:, qlo:] + g * step <= row[:, qlo:] + i * bq, p, 0.0)
                ds_ref[slot, hh, :, qlo:] = (p * (dp_ref[slot, hh, :, qlo:] - deltas[hh][:, qlo:])).astype(BF16)
                pb_ref[slot, hh, :, qlo:] = p.astype(BF16)
            return dqs

        def first_query(t, masked):
            return t * step if (masked and bk == bq and 0 <= t < nsub) else 0

        def chunk(kb, dqs, masked):
            for t in range(nsub):
                last = masked and t == nsub - 1
                dqs = substep(nsub * kb + t, t % 2, dqs, masked, not last, first_query(t, masked),
                              first_query(t + 1, masked), first_query(t - 1, masked))
            return dqs

        ds_ref[1] = jnp.zeros_like(ds_ref[1])
        pb_ref[1] = jnp.zeros_like(pb_ref[1])
        put_products(0, 0)
        zero = jnp.zeros((HEAD_PAD, bq), F32)
        dqs = lax.fori_loop(0, n_full, lambda kb, dqs: chunk(kb, dqs, False), (zero, zero))
        dqs = chunk(n_full, dqs, True)
        dqs = add_grads(dqs, nsub * n_full + nsub - 1, 1, first_query(nsub - 1, True))
        dq_ref[:, :HEAD_PAD] = dqs[0].T * MLA_SCALE
        dq_ref[:, HEAD_PAD:] = dqs[1].T * MLA_SCALE

    return pl.pallas_call(
        body, name="mla_bwd",
        grid=(4, nq),
        in_specs=[pl.BlockSpec((bq, 2 * HEAD_PAD), lambda h, i: (i, h)),
                  pl.BlockSpec((s, 2 * HEAD_PAD), lambda h, i: (0, h)),
                  pl.BlockSpec((s, LANES), lambda h, i: (0, h)),
                  pl.BlockSpec((bq, LANES), lambda h, i: (i, h)),
                  pl.BlockSpec((bq, LANES), lambda h, i: (i, h)),
                  pl.BlockSpec((None, 8, bq), lambda h, i: (h, 0, i))],
        out_specs=[pl.BlockSpec((bq, 2 * HEAD_PAD), lambda h, i: (i, h)),
                   pl.BlockSpec((s, 2 * HEAD_PAD), lambda h, i: (0, h), pipeline_mode=pl.Buffered(1)),
                   pl.BlockSpec((s, LANES), lambda h, i: (0, h), pipeline_mode=pl.Buffered(1))],
        out_shape=[jax.ShapeDtypeStruct((s, MLA_PAD_WIDTH), F32), jax.ShapeDtypeStruct((s, MLA_PAD_WIDTH), F32),
                   jax.ShapeDtypeStruct((s, MLA_WIDTH), F32)],
        scratch_shapes=[pltpu.VMEM((2, bq, LANES), BF16),
                        pltpu.VMEM((2, 2, step, bq), F32), pltpu.VMEM((2, 2, step, bq), F32),
                        pltpu.VMEM((2, 2, step, bq), BF16), pltpu.VMEM((2, 2, step, bq), BF16)],
        compiler_params=_params(("arbitrary", "arbitrary"), VMEM_LIMIT),
    )(q, k, v, do, o, lse)


def _rms_bwd(d_out, inp, r, weight, n):
    normed = inp * r
    gw = d_out * weight
    d_in = r * (gw - normed * (jnp.sum(gw * normed, axis=-1, keepdims=True) * (1.0 / n)))
    return d_in, d_out * normed


def _mla_prep_bwd_call(dq, dk, dv, q0, k0, cqn, ckvn, c_q, c_kv, cos_t, sin_t,
                       q_lora_norm, kv_lora_norm, qhn_pad, khn_pad, w_uq_bf, w_uk_bf, w_uv_bf):
    s = dq.shape[0]
    tm = _row_tile(s, 512)

    def body(dq_ref, dk_ref, dv_ref, q0_ref, k0_ref, cqn_ref, ckvn_ref, cq_ref, ckv_ref,
             cos_ref, sin_ref, qln_ref, kvln_ref, qhn_ref, khn_ref, wuq_ref, wuk_ref, wuv_ref,
             dcq_ref, dckv_ref, dkr_ref, gwuq_ref, gwuk_ref, gwuv_ref, gqln_ref, gkvln_ref, gqhn_ref, gkhn_ref,
             dq0_ref, dk0_ref, tmp_ref):
        @pl.when(pl.program_id(0) == 0)
        def _():
            for ref in (gwuq_ref, gwuk_ref, gwuv_ref, gqln_ref, gkvln_ref, gqhn_ref, gkhn_ref):
                ref[...] = jnp.zeros_like(ref)

        cos_t, sin_t = cos_ref[...], sin_ref[...]
        lane = lax.broadcasted_iota(I32, (tm, LANES), 1)
        rope_lanes = jnp.logical_or(lane < ROPE_HALF,
                                    jnp.logical_and(lane >= HALF_LANES, lane < HALF_LANES + ROPE_HALF))
        heads = [slice(h * HEAD_PAD, (h + 1) * HEAD_PAD) for h in range(MLA_HEADS)]

        def head_norm_bwd(d_ref, x0_ref, w_ref, out_ref, scale):
            w = w_ref[...]
            inv = [lax.rsqrt(jnp.sum(x0_ref[:, cols] * x0_ref[:, cols], axis=-1, keepdims=True)
                             * (1.0 / MLA_QK_DIM) + EPS) for cols in heads]
            for cols in heads:
                tmp_ref[:, cols] = _rope_adjoint(d_ref[:, cols] * scale, cos_t, sin_t)
            dots = [jnp.sum(tmp_ref[:, cols] * w * (x0_ref[:, cols] * r), axis=-1, keepdims=True)
                    for cols, r in zip(heads, inv)]
            g_w = jnp.zeros((1, LANES), F32)
            rope_sum = jnp.zeros((tm, LANES), F32)
            for cols, r, dot in zip(heads, inv, dots):
                normed = x0_ref[:, cols] * r
                d_n = tmp_ref[:, cols]
                d_x0 = r * (d_n * w - normed * (dot * (1.0 / MLA_QK_DIM)))
                out_ref[:, cols] = d_x0.astype(BF16)
                g_w = g_w + jnp.sum(d_n * normed, axis=0, keepdims=True)
                rope_sum = rope_sum + jnp.where(rope_lanes, d_x0, 0.0)
            return g_w, rope_sum

        g_qhn, _ = head_norm_bwd(dq_ref, q0_ref, qhn_ref, dq0_ref, 1.0)
        g_khn, d_kr = head_norm_bwd(dk_ref, k0_ref, khn_ref, dk0_ref, LN2)
        cqn, ckvn = cqn_ref[...], ckvn_ref[...]
        d_q0b, d_k0b, dvb = dq0_ref[...], dk0_ref[...], dv_ref[...].astype(BF16)
        d_cqn = _dot_nt(d_q0b, wuq_ref[...])
        gwuq_ref[...] += _dot_tn(cqn, d_q0b)
        d_ckvn = _dot_nt(d_k0b, wuk_ref[...]) + _dot_nt(dvb, wuv_ref[...])
        gwuk_ref[...] += _dot_tn(ckvn, d_k0b)
        gwuv_ref[...] += _dot_tn(ckvn, dvb)
        gqhn_ref[...] += g_qhn
        gkhn_ref[...] += g_khn
        dkr_ref[...] = d_kr.astype(BF16)
        cq = cq_ref[...]
        rcq = lax.rsqrt(jnp.mean(cq * cq, axis=-1, keepdims=True) + EPS)
        d_cq, gl = _rms_bwd(d_cqn, cq, rcq, qln_ref[...], Q_LORA_RANK)
        dcq_ref[...] = d_cq.astype(BF16)
        gqln_ref[...] += jnp.sum(gl, axis=0, keepdims=True)
        ckv = ckv_ref[...]
        rckv = lax.rsqrt(jnp.mean(ckv * ckv, axis=-1, keepdims=True) + EPS)
        d_ckv, gl = _rms_bwd(d_ckvn, ckv, rckv, kvln_ref[...], KV_LORA_RANK)
        dckv_ref[...] = d_ckv.astype(BF16)
        gkvln_ref[...] += jnp.sum(gl, axis=0, keepdims=True)

    return pl.pallas_call(
        body, name="mla_prep_bwd",
        grid=(s // tm,),
        in_specs=[_rows(tm, MLA_PAD_WIDTH), _rows(tm, MLA_PAD_WIDTH), _rows(tm, MLA_WIDTH),
                  _rows(tm, MLA_PAD_WIDTH), _rows(tm, MLA_PAD_WIDTH),
                  _rows(tm, Q_LORA_RANK), _rows(tm, KV_LORA_RANK), _rows(tm, Q_LORA_RANK), _rows(tm, KV_LORA_RANK),
                  _rows(tm, LANES), _rows(tm, LANES),
                  _full((1, Q_LORA_RANK)), _full((1, KV_LORA_RANK)), _full((1, LANES)), _full((1, LANES)),
                  _full((Q_LORA_RANK, MLA_PAD_WIDTH)), _full((KV_LORA_RANK, MLA_PAD_WIDTH)),
                  _full((KV_LORA_RANK, MLA_WIDTH))],
        out_specs=[_rows(tm, Q_LORA_RANK), _rows(tm, KV_LORA_RANK), _rows(tm, LANES),
                   _full((Q_LORA_RANK, MLA_PAD_WIDTH)), _full((KV_LORA_RANK, MLA_PAD_WIDTH)),
                   _full((KV_LORA_RANK, MLA_WIDTH)),
                   _full((1, Q_LORA_RANK)), _full((1, KV_LORA_RANK)), _full((1, LANES)), _full((1, LANES))],
        out_shape=[jax.ShapeDtypeStruct((s, Q_LORA_RANK), BF16), jax.ShapeDtypeStruct((s, KV_LORA_RANK), BF16),
                   jax.ShapeDtypeStruct((s, LANES), BF16),
                   jax.ShapeDtypeStruct((Q_LORA_RANK, MLA_PAD_WIDTH), F32),
                   jax.ShapeDtypeStruct((KV_LORA_RANK, MLA_PAD_WIDTH), F32),
                   jax.ShapeDtypeStruct((KV_LORA_RANK, MLA_WIDTH), F32),
                   jax.ShapeDtypeStruct((1, Q_LORA_RANK), F32), jax.ShapeDtypeStruct((1, KV_LORA_RANK), F32),
                   jax.ShapeDtypeStruct((1, LANES), F32), jax.ShapeDtypeStruct((1, LANES), F32)],
        scratch_shapes=[pltpu.VMEM((tm, MLA_PAD_WIDTH), BF16), pltpu.VMEM((tm, MLA_PAD_WIDTH), BF16),
                        pltpu.VMEM((tm, MLA_PAD_WIDTH), F32)],
        compiler_params=_params(("arbitrary",), VMEM_LIMIT),
    )(dq, dk, dv, q0, k0, cqn, ckvn, c_q, c_kv, cos_t, sin_t,
      q_lora_norm, kv_lora_norm, qhn_pad, khn_pad, w_uq_bf, w_uk_bf, w_uv_bf)


def _dh_call(pieces, hb, x, dy, ada, norm_w, w_in_bf):
    s = x.shape[0]
    tm = _row_tile(s, 512)
    widths = [p.shape[1] for p in pieces]
    offsets = [sum(widths[:j]) for j in range(len(widths))]
    assert offsets[-1] + widths[-1] == IN_COLS_PAD
    n = len(pieces)

    def body(*refs):
        p_refs = refs[:n]
        (hb_ref, x_ref, dy_ref, sh_ref, sc_ref, nw_ref, w_ref, gx_ref, gw_ref, dsh_ref, dsc_ref, gnw_ref,
         dp_ref, acc_ref) = refs[n:]

        @pl.when(pl.program_id(0) == 0)
        def _():
            acc_ref[...] = jnp.zeros_like(acc_ref)
            dsh_ref[...] = jnp.zeros_like(dsh_ref)
            dsc_ref[...] = jnp.zeros_like(dsc_ref)
            gnw_ref[...] = jnp.zeros_like(gnw_ref)

        for p_ref, c0, width in zip(p_refs, offsets, widths):
            dp_ref[:, c0:c0 + width] = p_ref[...].astype(BF16)
        acc_ref[...] += _dot_tn(hb_ref[...], dp_ref[...])

        @pl.when(pl.program_id(0) == pl.num_programs(0) - 1)
        def _():
            gw_ref[...] = acc_ref[...].astype(BF16)

        dh = _dot_nt(dp_ref[...], w_ref[...])
        xx = x_ref[...]
        r0 = lax.rsqrt(jnp.mean(xx * xx, axis=-1, keepdims=True) + EPS)
        xn = xx * r0
        nw = nw_ref[...]
        dsh_ref[...] += jnp.sum(dh, axis=0, keepdims=True)
        dsc_ref[...] += jnp.sum(dh * (xn * nw), axis=0, keepdims=True)
        dn = dh * (1.0 + sc_ref[...])
        gnw_ref[...] += jnp.sum(dn * xn, axis=0, keepdims=True)
        dxn = dn * nw
        gx_ref[...] = dy_ref[...] + r0 * (dxn - xn * jnp.mean(dxn * xn, axis=-1, keepdims=True))

    return pl.pallas_call(
        body, name="in_proj_bwd",
        grid=(s // tm,),
        in_specs=[_rows(tm, w) for w in widths]
        + [_rows(tm, D_MODEL), _rows(tm, D_MODEL), _rows(tm, D_MODEL),
           _ada_part(0), _ada_part(1), _full((1, D_MODEL)),
           pl.BlockSpec((D_MODEL, IN_COLS_PAD), lambda i: (0, 0), pipeline_mode=pl.Buffered(1))],
        out_specs=[_rows(tm, D_MODEL),
                   pl.BlockSpec((D_MODEL, IN_COLS_PAD), lambda i: (0, 0), pipeline_mode=pl.Buffered(1)),
                   _full((1, D_MODEL)), _full((1, D_MODEL)), _full((1, D_MODEL))],
        out_shape=[jax.ShapeDtypeStruct((s, D_MODEL), F32), jax.ShapeDtypeStruct((D_MODEL, IN_COLS_PAD), BF16),
                   jax.ShapeDtypeStruct((1, D_MODEL), F32), jax.ShapeDtypeStruct((1, D_MODEL), F32),
                   jax.ShapeDtypeStruct((1, D_MODEL), F32)],
        scratch_shapes=[pltpu.VMEM((tm, IN_COLS_PAD), BF16), pltpu.VMEM((D_MODEL, IN_COLS_PAD), F32)],
        compiler_params=_params(("arbitrary",), VMEM_LIMIT),
    )(*pieces, hb, x, dy, ada, ada, norm_w, w_in_bf)


def _adamw(g, w, m, v):
    m = ADAM_B1 * m + (1.0 - ADAM_B1) * g
    v = ADAM_B2 * v + (1.0 - ADAM_B2) * (g * g)
    m_hat = m / (1.0 - ADAM_B1 ** ADAM_STEP)
    v_hat = v / (1.0 - ADAM_B2 ** ADAM_STEP)
    delta = -ADAM_LR * (m_hat / (jnp.sqrt(v_hat) + ADAM_EPS) + ADAM_WD * w)
    return delta, m, v


def _adam_shard_call(name, parts, w, m, v):
    r, c = w.shape
    tr = r if r <= 512 else 256

    def body(p_ref, w_ref, m_ref, v_ref, g_ref, d_ref, nm_ref, nv_ref):
        g = ((p_ref[0].astype(F32) + p_ref[1].astype(F32)) + p_ref[2].astype(F32)) + p_ref[3].astype(F32)
        g_ref[...] = g
        d_ref[...], nm_ref[...], nv_ref[...] = _adamw(g, w_ref[...], m_ref[...], v_ref[...])

    blk = pl.BlockSpec((tr, c), lambda i: (i, 0))
    return pl.pallas_call(
        body, name=name,
        grid=(r // tr,),
        in_specs=[pl.BlockSpec((4, tr, c), lambda i: (0, i, 0)), blk, blk, blk],
        out_specs=[blk] * 4,
        out_shape=[jax.ShapeDtypeStruct((r, c), F32)] * 4,
        compiler_params=_params(("arbitrary",), VMEM_LIMIT),
    )(parts, w, m, v)


def _adam_shard_transposed_call(name, parts, w_t, m_t, v_t):
    c, r = w_t.shape
    assert parts.shape[1:] == (r, c)
    tr = r if r <= 512 else 256
    c_pad = -(-c // LANES) * LANES

    def body(p_ref, w_ref, m_ref, v_ref, g_ref, d_ref, nm_ref, nv_ref, pad_ref):
        @pl.when(pl.program_id(0) == 0)
        def _():
            pad_ref[...] = jnp.zeros_like(pad_ref)

        pad_ref[:, :c] = ((p_ref[0].astype(F32) + p_ref[1].astype(F32)) + p_ref[2].astype(F32)) + p_ref[3].astype(F32)
        g = pad_ref[...].T[:c]
        g_ref[...] = g
        d_ref[...], nm_ref[...], nv_ref[...] = _adamw(g, w_ref[...], m_ref[...], v_ref[...])

    blk = pl.BlockSpec((c, tr), lambda i: (0, i))
    return pl.pallas_call(
        body, name=name,
        grid=(r // tr,),
        in_specs=[pl.BlockSpec((4, tr, c), lambda i: (0, i, 0)), blk, blk, blk],
        out_specs=[blk] * 4,
        out_shape=[jax.ShapeDtypeStruct((c, r), F32)] * 4,
        scratch_shapes=[pltpu.VMEM((tr, c_pad), F32)],
        compiler_params=_params(("arbitrary",), VMEM_LIMIT),
    )(parts, w_t, m_t, v_t)


def _adam_ada_call(c_all, d_all, w, m, v):
    r, c = w.shape
    tr = 256

    def body(c_ref, d_ref, w_ref, m_ref, v_ref, g_ref, dl_ref, nm_ref, nv_ref):
        cc = c_ref[...]
        sc = cc * _sigmoid(cc)
        dd = d_ref[...]
        sc_hi = sc.astype(BF16)
        sc_lo = (sc - sc_hi.astype(F32)).astype(BF16)
        dd_hi = dd.astype(BF16)
        dd_lo = (dd - dd_hi.astype(F32)).astype(BF16)
        g = _dot_tn(sc_hi, dd_hi) + (_dot_tn(sc_hi, dd_lo) + _dot_tn(sc_lo, dd_hi))
        g_ref[...] = g
        dl_ref[...], nm_ref[...], nv_ref[...] = _adamw(g, w_ref[...], m_ref[...], v_ref[...])

    blk = pl.BlockSpec((tr, c), lambda i: (i, 0))
    return pl.pallas_call(
        body, name="adam_w_ada",
        grid=(r // tr,),
        in_specs=[pl.BlockSpec((16, tr), lambda i: (0, i)), pl.BlockSpec((16, c), lambda i: (0, 0)), blk, blk, blk],
        out_specs=[blk] * 4,
        out_shape=[jax.ShapeDtypeStruct((r, c), F32)] * 4,
        compiler_params=_params(("arbitrary",), VMEM_LIMIT),
    )(c_all, d_all, w, m, v)


def _adam_vectors_call(packs, offsets, vectors):
    nv = len(vectors)

    def body(*refs):
        p_ref, ins, outs = refs[0], refs[1:1 + 3 * nv], refs[1 + 3 * nv:]
        for j, off in enumerate(offsets):
            n = ins[3 * j].shape[1]
            span = -(-n // LANES) * LANES
            g = p_ref[0, :, off:off + span]
            for b in range(1, 8):
                g = g + p_ref[b, :, off:off + span]
            g = g[:, :n]
            outs[j][...] = g
            outs[nv + j][...], outs[2 * nv + j][...], outs[3 * nv + j][...] = _adamw(
                g, ins[3 * j][...], ins[3 * j + 1][...], ins[3 * j + 2][...])

    flat = [a for t in vectors for a in t]
    res = pl.pallas_call(
        body, name="adam_vectors",
        out_shape=[jax.ShapeDtypeStruct(t[0].shape, F32) for _ in range(4) for t in vectors],
    )(packs, *flat)
    return [res[k * nv:(k + 1) * nv] for k in range(4)]


ROPE_HALF = MLA_ROPE_DIM // 2
NOPE_A = MLA_NOPE_DIM - ROPE_HALF


def _zeros_like_lanes(t, n):
    return jnp.zeros(t.shape[:-1] + (n,), t.dtype)


def _to_head_lanes(t):
    nope, rope = t[..., :MLA_NOPE_DIM], t[..., MLA_NOPE_DIM:]
    return jnp.concatenate([rope[..., :ROPE_HALF], nope[..., :NOPE_A], rope[..., ROPE_HALF:], nope[..., NOPE_A:],
                            _zeros_like_lanes(t, HEAD_PAD - MLA_QK_DIM)], axis=-1)


def _from_head_lanes(t):
    return jnp.concatenate([t[..., ROPE_HALF:HALF_LANES], t[..., HALF_LANES + ROPE_HALF:MLA_QK_DIM],
                            t[..., :ROPE_HALF], t[..., HALF_LANES:HALF_LANES + ROPE_HALF]], axis=-1)


def _nope_to_head_lanes(t):
    return jnp.concatenate([_zeros_like_lanes(t, ROPE_HALF), t[..., :NOPE_A], _zeros_like_lanes(t, ROPE_HALF),
                            t[..., NOPE_A:], _zeros_like_lanes(t, HEAD_PAD - MLA_QK_DIM)], axis=-1)


def _rope_to_head_lanes(t):
    return jnp.concatenate([t[..., :ROPE_HALF], _zeros_like_lanes(t, HALF_LANES - ROPE_HALF), t[..., ROPE_HALF:],
                            _zeros_like_lanes(t, HALF_LANES - ROPE_HALF)], axis=-1)


def _rope_tables(positions):
    inv_freq = (ROPE_THETA ** (-jnp.arange(0, MLA_ROPE_DIM, 2, dtype=F32) / MLA_ROPE_DIM))[None]
    signed = _rope_to_head_lanes(jnp.concatenate([-inv_freq, inv_freq], axis=1))
    ang = positions.astype(F32)[:, None] * signed
    return jnp.cos(ang), jnp.sin(ang)


def _unshard_cols(g):
    return jnp.transpose(g, (1, 0, 2)).reshape(g.shape[1], 4 * g.shape[2])


def _shard_cols(g):
    r, c4 = g.shape
    return jnp.transpose(g.reshape(r, 4, c4 // 4), (1, 0, 2))


def kernel(x, c, positions, w_ada, b_ada, norm_w, w_in, q_lora_norm, w_uq, kv_lora_norm, w_ukv, q_head_norm, k_head_norm, w_out, loss_target, m_w_ada, m_b_ada, m_norm_w, m_w_in, m_q_lora_norm, m_w_uq, m_kv_lora_norm, m_w_ukv, m_q_head_norm, m_k_head_norm, m_w_out, v_w_ada, v_b_ada, v_norm_w, v_w_in, v_q_lora_norm, v_w_uq, v_kv_lora_norm, v_w_ukv, v_q_head_norm, v_k_head_norm, v_w_out):
    chip = 2 * lax.axis_index("x") + lax.axis_index("y")
    me8 = 2 * chip + lax.axis_index("c")
    ada_cols = w_ada.shape[2]
    c_all = _allgather_rows_call(c)[:, 0, :]
    ada_part = _ada_call(c_all, w_ada[0], lax.dynamic_slice_in_dim(b_ada, chip * ada_cols, ada_cols, axis=1))
    ada_g, win_g, wuq_g, wukv_g, wout_g = _gather_call(
        [ada_part[None]] + [w.astype(BF16) for w in (w_in, w_uq, w_ukv, w_out)], [False, True, True, True, True])
    ada = lax.dynamic_slice_in_dim(ada_g, me8, 1, axis=1).reshape(1, 4 * ada_cols)
    (sq_sum, grad_x, g_w_in, g_w_uq, g_w_ukv, g_w_out, d_ada, g_norm_w, g_qln, g_kvln, g_qhn, g_khn) = _local_step(
        x[0], ada, positions[0], loss_target[0], norm_w, win_g,
        q_lora_norm, _unshard_cols(wuq_g), kv_lora_norm, _unshard_cols(wukv_g), q_head_norm, k_head_norm,
        wout_g.reshape(D_MODEL, D_MODEL))

    grads = [g.astype(BF16) for g in (g_w_in, _shard_cols(g_w_uq), _shard_cols(g_w_ukv),
                                      g_w_out.reshape(4, D_MODEL // 4, D_MODEL))]
    pieces = [d_ada, g_norm_w, g_qln, g_kvln, g_qhn, g_khn, (0.5 * sq_sum / D_MODEL).reshape(1, 1)]
    spans = [-(-p.shape[1] // LANES) * LANES for p in pieces]
    starts = [sum(spans[:j]) for j in range(len(spans))]
    small = jnp.concatenate([jnp.pad(p, ((0, 0), (0, sp - p.shape[1]))) for p, sp in zip(pieces, spans)], axis=1)
    core = lax.axis_index("c").astype(I32).reshape(1)
    chip_halves = _sum_halves_call(core, grads, _swap_halves_call(grads))
    parts, packs = _exchange_call(chip_halves, small)
    loss = jnp.sum(packs[:, 0, starts[-1]])

    names = ["adam_w_in", "adam_w_uq", "adam_w_ukv", "adam_w_out"]
    shard_w = [(w_in, m_w_in, v_w_in), (w_uq, m_w_uq, v_w_uq), (w_ukv, m_w_ukv, v_w_ukv),
               (w_out, m_w_out, v_w_out)]
    res = {}
    for name, p_g, (w, m, v) in zip(names, parts, shard_w):
        if w.shape[2] % LANES:
            res_t = _adam_shard_transposed_call(name, p_g, w[0].T, m[0].T, v[0].T)
            res[name] = [t.T for t in res_t]
        else:
            res[name] = _adam_shard_call(name, p_g, w[0], m[0], v[0])
    d_all = lax.dynamic_slice_in_dim(packs[:, 0, :], starts[0] + chip * ada_cols, ada_cols, axis=1)
    res_ada = _adam_ada_call(jnp.pad(c_all, ((0, 8), (0, 0))), jnp.pad(d_all, ((0, 8), (0, 0))),
                             w_ada[0], m_w_ada[0], v_w_ada[0])
    vectors = [(b_ada, m_b_ada, v_b_ada), (norm_w, m_norm_w, v_norm_w), (q_lora_norm, m_q_lora_norm, v_q_lora_norm),
               (kv_lora_norm, m_kv_lora_norm, v_kv_lora_norm), (q_head_norm, m_q_head_norm, v_q_head_norm),
               (k_head_norm, m_k_head_norm, v_k_head_norm)]
    vec_out = _adam_vectors_call(packs, starts[:len(vectors)], vectors)

    def ordered(kind):
        big = lambda name: res[name][kind][None]
        return [res_ada[kind][None], vec_out[kind][0], vec_out[kind][1], big("adam_w_in"), vec_out[kind][2],
                big("adam_w_uq"), vec_out[kind][3], big("adam_w_ukv"), vec_out[kind][4], vec_out[kind][5],
                big("adam_w_out")]

    return (loss, grad_x[None], *ordered(0), *ordered(1), *ordered(2), *ordered(3))


def _local_step(x2, ada, positions, tgt, norm_w, w_in_shards, q_lora_norm, w_uq_full,
                kv_lora_norm, w_ukv_full, q_head_norm, k_head_norm, w_out_full):
    in_shard = w_in_shards.shape[2]
    ckv_tail = C_GM - 3 * in_shard
    assert 0 <= ckv_tail and ckv_tail + MLA_ROPE_DIM + MLA_WIDTH == in_shard
    last = w_in_shards[3]
    w_in_bf = jnp.concatenate(
        [w_in_shards[0], w_in_shards[1], w_in_shards[2], last[:, :ckv_tail], last[:, ckv_tail + MLA_ROPE_DIM:],
         _rope_to_head_lanes(last[:, ckv_tail:ckv_tail + MLA_ROPE_DIM])], axis=1).astype(BF16)
    w_uq_bf = _to_head_lanes(w_uq_full.reshape(Q_LORA_RANK, MLA_HEADS, MLA_QK_DIM)).reshape(
        Q_LORA_RANK, MLA_PAD_WIDTH).astype(BF16)
    w_ukv_heads = w_ukv_full.reshape(KV_LORA_RANK, MLA_HEADS, 2 * MLA_NOPE_DIM)
    w_uk_bf = _nope_to_head_lanes(w_ukv_heads[:, :, :MLA_NOPE_DIM]).reshape(KV_LORA_RANK, MLA_PAD_WIDTH).astype(BF16)
    w_uv_bf = w_ukv_heads[:, :, MLA_NOPE_DIM:].reshape(KV_LORA_RANK, MLA_WIDTH).astype(BF16)
    w_out_bf = w_out_full.astype(BF16)
    qhn_pad, khn_pad = _to_head_lanes(q_head_norm), _to_head_lanes(k_head_norm)
    cos_t, sin_t = _rope_tables(positions)

    hb, q_sb, k_sb, v_sb, g_sb, c_q, c_kv, g_mla, q_m, k_m, v_m, cqn, ckvn, q0, k0 = _pre_call(
        x2, ada, norm_w, w_in_bf, cos_t, sin_t, q_lora_norm, kv_lora_norm, qhn_pad, khn_pad,
        w_uq_bf, w_uk_bf, w_uv_bf)
    o_sb, r_sb, kstart = _sb_fwd_call(q_sb, k_sb, v_sb)
    o_mla, lse = _mla_fwd_call(q_m, k_m, v_m)
    do_sb, do_mla, dg_sb, dg_mla, dy, g_w_out, d_gate, sq = _out_call(
        o_sb, g_sb, o_mla, g_mla, x2, tgt, ada, w_out_bf)

    dq_sb, dk_sb, dv_sb = _sb_bwd_call(kstart, q_sb, k_sb, v_sb, do_sb, r_sb)
    dq_m, dk_m, dv_m = _mla_bwd_call(q_m, k_m, v_m, do_mla, o_mla, lse)
    (d_cq, d_ckv, d_kr, g_wuq_pad, g_wuk_pad, g_wuv, g_qln, g_kvln, g_qhn, g_khn) = _mla_prep_bwd_call(
        dq_m, dk_m, dv_m, q0, k0, cqn, ckvn, c_q, c_kv, cos_t, sin_t,
        q_lora_norm, kv_lora_norm, qhn_pad, khn_pad, w_uq_bf, w_uk_bf, w_uv_bf)
    grad_x, g_win_pad, d_shift, d_scale, g_norm_w = _dh_call(
        [dq_sb, dk_sb, dv_sb, dg_sb, d_cq, d_ckv, dg_mla, d_kr], hb, x2, dy, ada, norm_w, w_in_bf)

    g_kr = g_win_pad[:, C_KR:]
    g_last = jnp.concatenate([g_win_pad[:, 3 * in_shard:C_GM], g_kr[:, :ROPE_HALF],
                              g_kr[:, HALF_LANES:HALF_LANES + ROPE_HALF], g_win_pad[:, C_GM:C_KR]], axis=1)
    g_w_in = jnp.stack([g_win_pad[:, j * in_shard:(j + 1) * in_shard] for j in range(3)] + [g_last])
    g_w_uq = _from_head_lanes(g_wuq_pad.reshape(Q_LORA_RANK, MLA_HEADS, HEAD_PAD)).reshape(Q_LORA_RANK, -1)
    g_w_ukv = jnp.concatenate(
        [_from_head_lanes(g_wuk_pad.reshape(KV_LORA_RANK, MLA_HEADS, HEAD_PAD))[:, :, :MLA_NOPE_DIM],
         g_wuv.reshape(KV_LORA_RANK, MLA_HEADS, MLA_NOPE_DIM)], axis=2).reshape(KV_LORA_RANK, -1)
    d_ada = jnp.concatenate([d_shift, d_scale, d_gate], axis=1)
    return (jnp.sum(sq), grad_x, g_w_in, g_w_uq, g_w_ukv, g_w_out, d_ada, g_norm_w, g_qln, g_kvln,
            _from_head_lanes(g_qhn), _from_head_lanes(g_khn))
```

```python
import math

import jax
import jax.numpy as jnp
from jax import lax
from jax.experimental import pallas as pl
from jax.experimental.pallas import tpu as pltpu

F32 = jnp.float32
BF16 = jnp.bfloat16
I32 = jnp.int32

D_MODEL = 1024
SB_HEADS = 8
SB_WIDTH = 512
MLA_HEADS = 8
MLA_QK_DIM = 96
MLA_NOPE_DIM = 64
MLA_ROPE_DIM = 32
MLA_WIDTH = 512
Q_LORA_RANK = 384
KV_LORA_RANK = 256
ROPE_THETA = 10000.0
EPS = 1e-6
LANES = 128
HALF_LANES = LANES // 2
HEAD_PAD = 128
MLA_PAD_WIDTH = MLA_HEADS * HEAD_PAD

C_Q, C_K, C_V, C_G = 0, 512, 1024, 1536
C_CQ, C_CKV, C_GM, C_KR = 2048, 2432, 2688, 3200
IN_COLS_PAD = 3328

ADAM_LR = 0.001
ADAM_B1 = 0.9
ADAM_B2 = 0.999
ADAM_EPS = 1e-08
ADAM_WD = 0.01
ADAM_STEP = 10

SB_SCALE = 0.125
SB_GROUP = 4
MLA_SCALE = 1.0 / math.sqrt(MLA_QK_DIM)
LN2 = math.log(2.0)
MLA_SCALE_LOG2 = MLA_SCALE / LN2
MLA_BQ = 1024
MLA_BWD_BQ = 1024
MLA_BK = 1024
MLA_BWD_BK = 1024
MLA_STEP = 512
MLA_BWD_STEP = 256
SB_DEAD = -104.0
MASK_NEG = -1e30

VMEM_LIMIT = 56 * 1024 * 1024
MESH = pl.DeviceIdType.MESH


def _dot(a, b):
    return jnp.dot(a, b, preferred_element_type=F32)


def _dot_nt(a, b):
    return lax.dot_general(a, b, (((1,), (1,)), ((), ())), preferred_element_type=F32)


def _dot_tn(a, b):
    return lax.dot_general(a, b, (((0,), (0,)), ((), ())), preferred_element_type=F32)


def _sigmoid(x):
    return 1.0 / (1.0 + jnp.exp(-x))


def _split_dot(a, m):
    hi = a.astype(BF16)
    lo = (a - hi.astype(F32)).astype(BF16)
    return _dot(hi, m) + _dot(lo, m)


def _params(sem, vmem=None):
    return pltpu.CompilerParams(dimension_semantics=sem, vmem_limit_bytes=vmem)


def _row_tile(s, want):
    return min(want, s)


def _hbm_spec():
    return pl.BlockSpec(memory_space=pltpu.HBM)


def _allgather_rows_call(row):
    def body(in_ref, out_ref, send_sems, recv_sems, loc_sem):
        x, y, c = lax.axis_index("x"), lax.axis_index("y"), lax.axis_index("c")
        flips = [(fx, fy, fc) for fx in (0, 1) for fy in (0, 1) for fc in (0, 1)][1:]

        def peer(r):
            fx, fy, fc = flips[r]
            return ((1 - x) if fx else x, (1 - y) if fy else y, (1 - c) if fc else c)

        def copy(r, slot):
            return pltpu.make_async_remote_copy(
                src_ref=in_ref, dst_ref=out_ref.at[slot], send_sem=send_sems.at[r], recv_sem=recv_sems.at[r],
                device_id=peer(r), device_id_type=MESH)

        local = pltpu.make_async_copy(in_ref, out_ref.at[4 * x + 2 * y + c], loc_sem)
        local.start()
        sends = [copy(r, 4 * x + 2 * y + c) for r in range(7)]
        for cp in sends:
            cp.start()
        for r in range(7):
            px, py, pc = peer(r)
            copy(r, 4 * px + 2 * py + pc).wait_recv()
        for cp in sends:
            cp.wait_send()
        local.wait()

    return pl.pallas_call(
        body, name="gather_rows",
        out_shape=jax.ShapeDtypeStruct((8,) + row.shape, row.dtype),
        in_specs=[_hbm_spec()], out_specs=_hbm_spec(),
        scratch_shapes=[pltpu.SemaphoreType.DMA((7,)), pltpu.SemaphoreType.DMA((7,)), pltpu.SemaphoreType.DMA],
    )(row)


def _gather_call(shards, split):
    n = len(shards)
    halves = [s.shape[1] // 2 for s in shards]

    def body(*refs):
        ins, outs = refs[:n], refs[n:2 * n]
        ici_send, ici_recv, d2d_send, d2d_recv, loc_sems = refs[2 * n:]
        x, y, c = lax.axis_index("x"), lax.axis_index("y"), lax.axis_index("c")
        me = 2 * x + y
        peers = [(1 - x, y), (x, 1 - y), (1 - x, 1 - y)]

        def rows(a, which):
            return pl.ds(pl.multiple_of(which * halves[a], 16), halves[a])

        def ici(a, j, slot):
            px, py = peers[j]
            src, dst = ins[a].at[0], outs[a].at[slot]
            if split[a]:
                src, dst = src.at[rows(a, c)], dst.at[rows(a, c)]
            return pltpu.make_async_remote_copy(
                src_ref=src, dst_ref=dst,
                send_sem=ici_send.at[3 * a + j], recv_sem=ici_recv.at[3 * a + j],
                device_id=(px, py, c), device_id_type=MESH)

        def d2d(a, j, which):
            px, py = peers[j]
            piece = outs[a].at[2 * px + py, rows(a, which)]
            return pltpu.make_async_remote_copy(
                src_ref=piece, dst_ref=piece,
                send_sem=d2d_send.at[3 * a + j], recv_sem=d2d_recv.at[3 * a + j],
                device_id=(x, y, 1 - c), device_id_type=MESH)

        local = [pltpu.make_async_copy(ins[a].at[0], outs[a].at[me], loc_sems.at[a]) for a in range(n)]
        for cp in local:
            cp.start()
        sends = [ici(a, j, me) for a in range(n) for j in range(3)]
        for cp in sends:
            cp.start()
        for a in range(n):
            for j in range(3):
                px, py = peers[j]
                ici(a, j, 2 * px + py).wait_recv()
                if split[a]:
                    cp = d2d(a, j, c)
                    cp.start()
                    sends.append(cp)
        for a in range(n):
            for j in range(3):
                if split[a]:
                    d2d(a, j, 1 - c).wait_recv()
        for cp in sends:
            cp.wait_send()
        for cp in local:
            cp.wait()

    return pl.pallas_call(
        body, name="gather_weights",
        out_shape=[jax.ShapeDtypeStruct((4,) + s.shape[1:], s.dtype) for s in shards],
        in_specs=[_hbm_spec() for _ in shards],
        out_specs=[_hbm_spec() for _ in shards],
        scratch_shapes=[pltpu.SemaphoreType.DMA((3 * n,)), pltpu.SemaphoreType.DMA((3 * n,)),
                        pltpu.SemaphoreType.DMA((3 * n,)), pltpu.SemaphoreType.DMA((3 * n,)),
                        pltpu.SemaphoreType.DMA((n,))],
    )(*shards)


def _swap_halves_call(grads):
    n = len(grads)
    halves = [g.shape[1] // 2 for g in grads]

    def body(*refs):
        g_in, hs, send_sems, recv_sems = refs[:n], refs[n:2 * n], refs[2 * n], refs[2 * n + 1]
        x, y, c = lax.axis_index("x"), lax.axis_index("y"), lax.axis_index("c")
        copies = []
        for a in range(n):
            theirs = pl.ds(pl.multiple_of((1 - c) * halves[a], 16), halves[a])
            copies.append(pltpu.make_async_remote_copy(
                src_ref=g_in[a].at[:, theirs], dst_ref=hs[a], send_sem=send_sems.at[a], recv_sem=recv_sems.at[a],
                device_id=(x, y, 1 - c), device_id_type=MESH))
        for cp in copies:
            cp.start()
        for cp in copies:
            cp.wait()

    return pl.pallas_call(
        body, name="swap_halves",
        out_shape=[jax.ShapeDtypeStruct((4, h, g.shape[2]), g.dtype) for g, h in zip(grads, halves)],
        in_specs=[_hbm_spec() for _ in grads], out_specs=[_hbm_spec() for _ in grads],
        scratch_shapes=[pltpu.SemaphoreType.DMA((n,)), pltpu.SemaphoreType.DMA((n,))],
    )(*grads)


def _sum_halves_call(core, grads, halves):
    n = len(grads)

    def body(core_ref, *refs):
        for g_ref, h_ref, o_ref in zip(refs[:n], refs[n:2 * n], refs[2 * n:]):
            o_ref[...] = (g_ref[...].astype(F32) + h_ref[...].astype(F32)).astype(o_ref.dtype)

    whole = lambda h: pl.BlockSpec(h.shape, lambda i, core_ref: (0, 0, 0))
    return pl.pallas_call(
        body, name="sum_halves",
        grid_spec=pltpu.PrefetchScalarGridSpec(
            num_scalar_prefetch=1, grid=(1,),
            in_specs=[pl.BlockSpec(h.shape, lambda i, core_ref: (0, core_ref[0], 0)) for h in halves]
            + [whole(h) for h in halves],
            out_specs=[whole(h) for h in halves]),
        out_shape=[jax.ShapeDtypeStruct(h.shape, h.dtype) for h in halves],
        compiler_params=_params(("arbitrary",), VMEM_LIMIT),
    )(core, *grads, *halves)


def _exchange_call(chip_halves, small):
    n = len(chip_halves)
    halves = [h.shape[1] for h in chip_halves]

    def body(*refs):
        g_in, small_in = refs[:n], refs[n]
        parts, packs = refs[n + 1:2 * n + 1], refs[2 * n + 1]
        ici_send, ici_recv, d2d_send, d2d_recv, sm_send, sm_recv, loc_sems = refs[2 * n + 2:]
        x, y, c = lax.axis_index("x"), lax.axis_index("y"), lax.axis_index("c")
        me = 2 * x + y
        me8 = 4 * x + 2 * y + c
        sibling = (x, y, 1 - c)
        peers = [(1 - x, y), (x, 1 - y), (1 - x, 1 - y)]
        flips = [(fx, fy, fc) for fx in (0, 1) for fy in (0, 1) for fc in (0, 1)][1:]

        def rows(a, which):
            return pl.ds(pl.multiple_of(which * halves[a], 16), halves[a])

        def ici(a, j, src_slot, dst_slot):
            px, py = peers[j]
            return pltpu.make_async_remote_copy(
                src_ref=g_in[a].at[src_slot], dst_ref=parts[a].at[dst_slot, rows(a, c)],
                send_sem=ici_send.at[3 * a + j], recv_sem=ici_recv.at[3 * a + j],
                device_id=(px, py, c), device_id_type=MESH)

        def d2d(a, rel, chip, which):
            piece = parts[a].at[chip, rows(a, which)]
            return pltpu.make_async_remote_copy(
                src_ref=piece, dst_ref=piece,
                send_sem=d2d_send.at[4 * a + rel], recv_sem=d2d_recv.at[4 * a + rel],
                device_id=sibling, device_id_type=MESH)

        def flipped(r):
            fx, fy, fc = flips[r]
            return ((1 - x) if fx else x, (1 - y) if fy else y, (1 - c) if fc else c)

        def sm(r, slot):
            return pltpu.make_async_remote_copy(
                src_ref=small_in, dst_ref=packs.at[slot],
                send_sem=sm_send.at[r], recv_sem=sm_recv.at[r],
                device_id=flipped(r), device_id_type=MESH)

        def peer8(r):
            px, py, pc = flipped(r)
            return 4 * px + 2 * py + pc

        local = [pltpu.make_async_copy(g_in[a].at[me], parts[a].at[me, rows(a, c)], loc_sems.at[a])
                 for a in range(n)]
        local.append(pltpu.make_async_copy(small_in, packs.at[me8], loc_sems.at[n]))
        for cp in local:
            cp.start()
        sends = []
        for r in range(7):
            sends.append(sm(r, me8))
        for a in range(n):
            for j in range(3):
                px, py = peers[j]
                sends.append(ici(a, j, 2 * px + py, me))
        for cp in sends:
            cp.start()
        for a in range(n):
            local[a].wait()
            cp = d2d(a, 0, me, c)
            cp.start()
            sends.append(cp)
        for a in range(n):
            for j in range(3):
                px, py = peers[j]
                ici(a, j, me, 2 * px + py).wait_recv()
                cp = d2d(a, 1 + j, 2 * px + py, c)
                cp.start()
                sends.append(cp)
        for a in range(n):
            d2d(a, 0, me, 1 - c).wait_recv()
            for j in range(3):
                px, py = peers[j]
                d2d(a, 1 + j, 2 * px + py, 1 - c).wait_recv()
        for r in range(7):
            sm(r, peer8(r)).wait_recv()
        for cp in sends:
            cp.wait_send()
        local[n].wait()

    out_shape = ([jax.ShapeDtypeStruct((4, 2 * h.shape[1], h.shape[2]), h.dtype) for h in chip_halves]
                 + [jax.ShapeDtypeStruct((8,) + small.shape, small.dtype)])
    res = pl.pallas_call(
        body, name="exchange_grads",
        out_shape=out_shape,
        in_specs=[_hbm_spec() for _ in range(n + 1)],
        out_specs=[_hbm_spec() for _ in range(n + 1)],
        scratch_shapes=[pltpu.SemaphoreType.DMA((3 * n,)), pltpu.SemaphoreType.DMA((3 * n,)),
                        pltpu.SemaphoreType.DMA((4 * n,)), pltpu.SemaphoreType.DMA((4 * n,)),
                        pltpu.SemaphoreType.DMA((7,)), pltpu.SemaphoreType.DMA((7,)),
                        pltpu.SemaphoreType.DMA((n + 1,))],
    )(*chip_halves, small)
    return res[:n], res[n]


def _ada_call(c_all, w_ada_cols, b_ada_cols):
    def body(c_ref, w_ref, b_ref, o_ref):
        cc = c_ref[...]
        o_ref[...] = _dot((cc * _sigmoid(cc)).astype(BF16), w_ref[...].astype(BF16)) + b_ref[...]

    return pl.pallas_call(
        body, name="ada_fwd",
        out_shape=jax.ShapeDtypeStruct((c_all.shape[0], w_ada_cols.shape[1]), F32),
        compiler_params=pltpu.CompilerParams(vmem_limit_bytes=VMEM_LIMIT),
    )(c_all, w_ada_cols, b_ada_cols)


def _ada_part(j):
    return pl.BlockSpec((1, D_MODEL), lambda i: (0, j))


def _full(shape):
    return pl.BlockSpec(shape, lambda i: (0,) * len(shape))


def _rows(tm, width):
    return pl.BlockSpec((tm, width), lambda i: (i, 0))


def _rope(t, cos_t, sin_t):
    return t * cos_t + pltpu.roll(t, HALF_LANES, 1) * sin_t


def _rope_adjoint(d, cos_t, sin_t):
    return d * cos_t + pltpu.roll(d * sin_t, HALF_LANES, 1)


def _pre_call(x, ada, norm_w, w_in_bf, cos_t, sin_t, q_lora_norm, kv_lora_norm, qhn_pad, khn_pad,
              w_uq_bf, w_uk_bf, w_uv_bf):
    s = x.shape[0]
    tm = _row_tile(s, 512)
    out_defs = [(D_MODEL, BF16), (512, BF16), (512, BF16), (512, BF16), (512, F32),
                (Q_LORA_RANK, F32), (KV_LORA_RANK, F32), (512, F32),
                (MLA_PAD_WIDTH, BF16), (MLA_PAD_WIDTH, BF16), (MLA_WIDTH, BF16),
                (Q_LORA_RANK, BF16), (KV_LORA_RANK, BF16), (MLA_PAD_WIDTH, F32), (MLA_PAD_WIDTH, F32)]

    def body(x_ref, sh_ref, sc_ref, nw_ref, w_ref, cos_ref, sin_ref, qln_ref, kvln_ref, qhn_ref, khn_ref,
             wuq_ref, wuk_ref, wuv_ref,
             hb_ref, qsb_ref, ksb_ref, vsb_ref, gsb_ref, cq_ref, ckv_ref, gm_ref,
             q_ref, k_ref, v_ref, cqn_ref, ckvn_ref, q0_ref, k0_ref):
        xx = x_ref[...]
        r0 = lax.rsqrt(jnp.mean(xx * xx, axis=-1, keepdims=True) + EPS)
        hb = ((xx * r0 * nw_ref[...]) * (1.0 + sc_ref[...]) + sh_ref[...]).astype(BF16)
        hb_ref[...] = hb

        def proj(c0, width):
            return _dot(hb, w_ref[:, c0:c0 + width])

        cq = proj(C_CQ, Q_LORA_RANK)
        ckv = proj(C_CKV, KV_LORA_RANK)
        kr = proj(C_KR, LANES)
        cq_ref[...] = cq
        ckv_ref[...] = ckv
        cqn = (cq * lax.rsqrt(jnp.mean(cq * cq, axis=-1, keepdims=True) + EPS) * qln_ref[...]).astype(BF16)
        cqn_ref[...] = cqn
        ckvn = (ckv * lax.rsqrt(jnp.mean(ckv * ckv, axis=-1, keepdims=True) + EPS) * kvln_ref[...]).astype(BF16)
        ckvn_ref[...] = ckvn
        qsb_ref[...] = proj(C_Q, 512).astype(BF16)
        v_ref[...] = _dot(ckvn, wuv_ref[...]).astype(BF16)
        q0_ref[...] = _dot(cqn, wuq_ref[...])
        k0_ref[...] = _dot(ckvn, wuk_ref[...])
        ksb_ref[...] = proj(C_K, 512).astype(BF16)
        cos_t, sin_t = cos_ref[...], sin_ref[...]
        heads = [slice(h * HEAD_PAD, (h + 1) * HEAD_PAD) for h in range(MLA_HEADS)]
        for cols in heads:
            k0_ref[:, cols] = k0_ref[:, cols] + kr

        def inv_rms(ref):
            sums = [jnp.sum(ref[:, cols] * ref[:, cols], axis=-1, keepdims=True) for cols in heads]
            return [lax.rsqrt(t * (1.0 / MLA_QK_DIM) + EPS) for t in sums]

        rqs = inv_rms(q0_ref)
        vsb_ref[...] = proj(C_V, 512).astype(BF16)
        rks = inv_rms(k0_ref)
        gsb_ref[...] = proj(C_G, 512)
        for cols, rq, rk in zip(heads, rqs, rks):
            q_ref[:, cols] = (_rope(q0_ref[:, cols] * rq * qhn_ref[...], cos_t, sin_t) * MLA_SCALE_LOG2).astype(BF16)
            k_ref[:, cols] = _rope(k0_ref[:, cols] * rk * khn_ref[...], cos_t, sin_t).astype(BF16)
        gm_ref[...] = proj(C_GM, 512)

    return pl.pallas_call(
        body, name="pre_proj",
        grid=(s // tm,),
        in_specs=[_rows(tm, D_MODEL), _ada_part(0), _ada_part(1), _full((1, D_MODEL)),
                  pl.BlockSpec((D_MODEL, IN_COLS_PAD), lambda i: (0, 0), pipeline_mode=pl.Buffered(1)),
                  _rows(tm, LANES), _rows(tm, LANES),
                  _full((1, Q_LORA_RANK)), _full((1, KV_LORA_RANK)), _full((1, LANES)), _full((1, LANES)),
                  _full((Q_LORA_RANK, MLA_PAD_WIDTH)), _full((KV_LORA_RANK, MLA_PAD_WIDTH)),
                  _full((KV_LORA_RANK, MLA_WIDTH))],
        out_specs=[_rows(tm, w) for w, _ in out_defs],
        out_shape=[jax.ShapeDtypeStruct((s, w), dt) for w, dt in out_defs],
        compiler_params=_params(("arbitrary",), VMEM_LIMIT),
    )(x, ada, ada, norm_w, w_in_bf, cos_t, sin_t, q_lora_norm, kv_lora_norm, qhn_pad, khn_pad,
      w_uq_bf, w_uk_bf, w_uv_bf)


def _log_sigmoid_pair(z):
    ls = jnp.minimum(z, 0.0) - jnp.log(1.0 + jnp.exp(-jnp.abs(z)))
    return ls, ls - z


def _pair(hh):
    return slice((hh // 2) * LANES, (hh // 2 + 1) * LANES)


def _sb_fwd_call(q, k, v):
    s = q.shape[0]
    bq = _row_tile(s, 256)
    nq = s // bq
    nh = SB_GROUP
    width = nh * HALF_LANES

    def body(q_ref, k_ref, v_ref, o_ref, r_ref, ks_ref):
        hp, i = pl.program_id(0), pl.program_id(1)
        lane = lax.broadcasted_iota(I32, (bq, LANES), 1)
        row = lax.broadcasted_iota(I32, (bq, bq), 0)
        col = lax.broadcasted_iota(I32, (bq, bq), 1)
        strict = col < row
        later = jnp.where(row > col, 1.0, 0.0).astype(BF16)
        masks = [_head_mask(lane, hh).astype(BF16) for hh in range(2)]
        qms = [q_ref[:, _pair(hh)] * jnp.asarray(SB_SCALE, BF16) * masks[hh % 2] for hh in range(nh)]

        def walk(blocks, state):
            chains = [(kb, diagonal, hh) for kb, diagonal in blocks for hh in range(nh)]
            keys = lambda kb: pl.ds(pl.multiple_of(kb * bq, bq), bq)
            zs = [_dot_nt(qms[hh], k_ref[keys(kb), _pair(hh)]) for kb, _, hh in chains]
            pairs = []
            for z, (_, diagonal, _) in zip(zs, chains):
                ls, lk = _log_sigmoid_pair(z)
                pairs.append((ls, jnp.where(strict, lk, 0.0) if diagonal else lk))
            sums = [_split_dot(lk, later) for _, lk in pairs]
            runs = [st[0] for st in state]
            ws = []
            for (ls, lk), after, (_, diagonal, hh) in zip(pairs, sums, chains):
                w = jnp.exp(ls + (after + runs[hh]))
                ws.append((jnp.where(strict, w, 0.0) if diagonal else w).astype(BF16))
                runs[hh] = runs[hh] + jnp.sum(lk, axis=1, keepdims=True)
            accs = [st[1] for st in state]
            for w, (kb, _, hh) in zip(ws, chains):
                accs[hh] = accs[hh] + _dot(w, v_ref[keys(kb), _pair(hh)])
            return tuple(zip(runs, accs))

        def alive(state):
            top = jnp.max(state[0][0])
            for st in state[1:]:
                top = jnp.maximum(top, jnp.max(st[0]))
            return (top > SB_DEAD).astype(I32)

        def finish(state, first):
            ks_ref[hp, i] = first
            for pair in range(nh // 2):
                o_ref[:, _pair(2 * pair)] = jnp.where(lane < HALF_LANES, state[2 * pair][1], state[2 * pair + 1][1])
                r_ref[:, _pair(2 * pair)] = jnp.where(lane < HALF_LANES, state[2 * pair][0], state[2 * pair + 1][0])

        zero = ((jnp.zeros((bq, 1), F32), jnp.zeros((bq, LANES), F32)),) * nh

        @pl.when(i == 0)
        def _():
            finish(walk([(0, True)], zero), 0)

        @pl.when(i > 0)
        def _():
            state = walk([(i, True), (i - 1, False)], zero)

            def cond(carry):
                return jnp.logical_and(carry[0] >= 0, carry[1] > 0)

            def step(carry):
                state = walk([(carry[0], False)], carry[2])
                return carry[0] - 1, alive(state), state

            kb, _, state = lax.while_loop(cond, step, (i - 2, alive(state), state))
            finish(state, kb + 1)

    return pl.pallas_call(
        body, name="sb_fwd",
        grid=(SB_HEADS // nh, nq),
        in_specs=[pl.BlockSpec((bq, width), lambda h, i: (i, h)),
                  pl.BlockSpec((s, width), lambda h, i: (0, h)),
                  pl.BlockSpec((s, width), lambda h, i: (0, h))],
        out_specs=[pl.BlockSpec((bq, width), lambda h, i: (i, h)),
                   pl.BlockSpec((bq, width), lambda h, i: (i, h)),
                   pl.BlockSpec(memory_space=pltpu.SMEM)],
        out_shape=[jax.ShapeDtypeStruct((s, SB_WIDTH), F32), jax.ShapeDtypeStruct((s, SB_WIDTH), F32),
                   jax.ShapeDtypeStruct((SB_HEADS // nh, nq), I32)],
        compiler_params=_params(("arbitrary", "arbitrary"), VMEM_LIMIT),
    )(q, k, v)


def _mla_fwd_call(q, k, v):
    s = q.shape[0]
    bq = _row_tile(s, MLA_BQ)
    bk = _row_tile(s, MLA_BK)
    nq = s // bq
    assert bk % bq == 0
    step = min(bk // 2, MLA_STEP)
    nsub = bk // step
    assert nsub % 2 == 0

    def body(q_ref, k_ref, v_ref, o_ref, lse_ref, p_ref, s_ref):
        i = pl.program_id(1)
        lane = lax.broadcasted_iota(I32, (bq, LANES), 1)
        row = lax.broadcasted_iota(I32, (step, bq), 1)
        col = lax.broadcasted_iota(I32, (step, bq), 0)
        n_full = (i * bq) // bk

        def keys(g):
            return pl.ds(pl.multiple_of(g * step, step), step)

        def join(left, right, qlo):
            return right if qlo == 0 else jnp.concatenate([left[:, :qlo], right], axis=1)

        def put_scores(g, slot, qlo=0):
            for hh in range(2):
                cols = slice(hh * HEAD_PAD, (hh + 1) * HEAD_PAD)
                s_ref[slot, hh, :, qlo:] = _dot_nt(k_ref[keys(g), cols], q_ref[qlo:, cols])

        def add_pv(carry, g, slot, qlo=0):
            vblk = v_ref[keys(g), :]
            out = []
            for hh, (m, l, acc, alpha) in enumerate(carry):
                upd = alpha[:, qlo:] * acc[:, qlo:] + _dot_tn(vblk, p_ref[slot, hh, :, qlo:])
                out.append((m, l, join(acc, upd, qlo), alpha))
            return tuple(out)

        def substep(g, slot, carry, masked, prefetch, qlo=0, next_qlo=0, prev_qlo=0, first=False):
            if prefetch:
                put_scores(g + 1, 1 - slot, next_qlo)
            if not first:
                carry = add_pv(carry, g - 1, 1 - slot, prev_qlo)
            new = []
            for hh in range(2):
                m, l, acc, _ = carry[hh]
                sc = s_ref[slot, hh, :, qlo:]
                if masked:
                    sc = jnp.where(col[:, qlo:] + g * step <= row[:, qlo:] + i * bq, sc, MASK_NEG)
                m_new = jnp.maximum(m[:, qlo:], jnp.max(sc, axis=0, keepdims=True))
                p = jnp.exp2(sc - m_new)
                alpha = jnp.exp2(m[:, qlo:] - m_new)
                l_new = alpha * l[:, qlo:] + jnp.sum(p, axis=0, keepdims=True)
                p_ref[slot, hh, :, qlo:] = p.astype(BF16)
                new.append((join(m, m_new, qlo), join(l, l_new, qlo), acc, join(jnp.ones_like(m), alpha, qlo)))
            return tuple(new)

        def first_query(t, masked):
            return t * step if (masked and bk == bq and 0 <= t < nsub) else 0

        def chunk(kb, carry, masked, first=False):
            for t in range(nsub):
                last = masked and t == nsub - 1
                carry = substep(nsub * kb + t, t % 2, carry, masked, not last, first_query(t, masked),
                                first_query(t + 1, masked), first_query(t - 1, masked), first and t == 0)
            return carry

        def finish(carry):
            (m0, l0, a0, _), (m1, l1, a1, _) = add_pv(carry, nsub * n_full + nsub - 1, 1,
                                                      first_query(nsub - 1, True))
            o_ref[...] = jnp.where(lane < HALF_LANES, (a0 / l0).T, (a1 / l1).T)
            sub = lax.broadcasted_iota(I32, (8, bq), 0)
            lse_ref[...] = jnp.where(sub == 0, m0 + jnp.log2(l0), jnp.where(sub == 1, m1 + jnp.log2(l1), 0.0))

        put_scores(0, 0)
        one = (jnp.full((1, bq), MASK_NEG, F32), jnp.zeros((1, bq), F32), jnp.zeros((LANES, bq), F32),
               jnp.ones((1, bq), F32))

        @pl.when(n_full == 0)
        def _():
            finish(chunk(0, (one, one), True, first=True))

        @pl.when(n_full > 0)
        def _():
            carry = chunk(0, (one, one), False, first=True)
            carry = lax.fori_loop(1, n_full, lambda kb, cr: chunk(kb, cr, False), carry)
            finish(chunk(n_full, carry, True))

    return pl.pallas_call(
        body, name="mla_fwd",
        grid=(4, nq),
        in_specs=[pl.BlockSpec((bq, 2 * HEAD_PAD), lambda h, i: (i, h)),
                  pl.BlockSpec((s, 2 * HEAD_PAD), lambda h, i: (0, h)),
                  pl.BlockSpec((s, LANES), lambda h, i: (0, h))],
        out_specs=[pl.BlockSpec((bq, LANES), lambda h, i: (i, h)),
                   pl.BlockSpec((None, 8, bq), lambda h, i: (h, 0, i))],
        out_shape=[jax.ShapeDtypeStruct((s, MLA_WIDTH), F32), jax.ShapeDtypeStruct((4, 8, s), F32)],
        scratch_shapes=[pltpu.VMEM((2, 2, step, bq), BF16), pltpu.VMEM((2, 2, step, bq), F32)],
        compiler_params=_params(("arbitrary", "arbitrary"), VMEM_LIMIT),
    )(q, k, v)


def _out_call(o_sb, g_sb, o_mla, g_mla, x, target, ada, w_out_bf):
    s = x.shape[0]
    tm = _row_tile(s, 512)

    def body(osb_ref, gsb_ref, oml_ref, gml_ref, x_ref, t_ref, gate_ref, w_ref,
             dosb_ref, doml_ref, dgsb_ref, dgml_ref, dy_ref, gw_ref, dgate_ref, sq_ref):
        @pl.when(pl.program_id(0) == 0)
        def _():
            gw_ref[...] = jnp.zeros_like(gw_ref)
            dgate_ref[...] = jnp.zeros_like(dgate_ref)
            sq_ref[...] = jnp.zeros_like(sq_ref)

        g_s, g_m = gsb_ref[...], gml_ref[...]
        sig_s, sig_m = _sigmoid(g_s), _sigmoid(g_m)
        silu_s, silu_m = g_s * sig_s, g_m * sig_m
        o_s, o_m = osb_ref[...], oml_ref[...]
        mixed = jnp.concatenate([o_s * silu_s, o_m * silu_m], axis=1).astype(BF16)
        u = _dot(mixed, w_ref[...])
        gate_v = gate_ref[...]
        err = x_ref[...] + gate_v * u - t_ref[...]
        sq_ref[...] += jnp.sum(err * err, axis=0, keepdims=True)
        dy = err * (1.0 / D_MODEL)
        dy_ref[...] = dy
        dgate_ref[...] += jnp.sum(dy * u, axis=0, keepdims=True)
        du = (dy * gate_v).astype(BF16)
        gw_ref[...] += _dot_tn(mixed, du)
        dmix = _dot_nt(du, w_ref[...])
        dm_s, dm_m = dmix[:, :SB_WIDTH], dmix[:, SB_WIDTH:]
        dosb_ref[...] = (dm_s * silu_s).astype(BF16)
        doml_ref[...] = (dm_m * silu_m).astype(BF16)
        dgsb_ref[...] = (dm_s * o_s * (sig_s * (1.0 + g_s * (1.0 - sig_s)))).astype(BF16)
        dgml_ref[...] = (dm_m * o_m * (sig_m * (1.0 + g_m * (1.0 - sig_m)))).astype(BF16)

    return pl.pallas_call(
        body, name="out_proj_loss",
        grid=(s // tm,),
        in_specs=[_rows(tm, 512), _rows(tm, 512), _rows(tm, 512), _rows(tm, 512),
                  _rows(tm, D_MODEL), _rows(tm, D_MODEL), _ada_part(2), _full((D_MODEL, D_MODEL))],
        out_specs=[_rows(tm, 512), _rows(tm, 512), _rows(tm, 512), _rows(tm, 512), _rows(tm, D_MODEL),
                   _full((D_MODEL, D_MODEL)), _full((1, D_MODEL)), _full((1, D_MODEL))],
        out_shape=[jax.ShapeDtypeStruct((s, 512), BF16)] * 4
        + [jax.ShapeDtypeStruct((s, D_MODEL), F32), jax.ShapeDtypeStruct((D_MODEL, D_MODEL), F32),
           jax.ShapeDtypeStruct((1, D_MODEL), F32), jax.ShapeDtypeStruct((1, D_MODEL), F32)],
        compiler_params=_params(("arbitrary",), VMEM_LIMIT),
    )(o_sb, g_sb, o_mla, g_mla, x, target, ada, w_out_bf)


def _head_mask(lane, hh):
    return jnp.where((lane >= HALF_LANES) if hh else (lane < HALF_LANES), 1.0, 0.0)


def _pick_lane(packed, lane, which):
    return jnp.sum(jnp.where(lane == which, packed, 0.0), axis=1, keepdims=True)


def _sb_bwd_call(kstart, q, k, v, do, rfin):
    s = q.shape[0]
    bq = _row_tile(s, 256)
    nq = s // bq
    nh = SB_GROUP
    width = nh * HALF_LANES

    def body(ks_ref, q_ref, k_ref, v_ref, do_ref, r_ref, dq_ref, dk_ref, dv_ref):
        hp, i = pl.program_id(0), pl.program_id(1)

        @pl.when(i == 0)
        def _():
            dk_ref[...] = jnp.zeros_like(dk_ref)
            dv_ref[...] = jnp.zeros_like(dv_ref)

        lane = lax.broadcasted_iota(I32, (bq, LANES), 1)
        row = lax.broadcasted_iota(I32, (bq, bq), 0)
        col = lax.broadcasted_iota(I32, (bq, bq), 1)
        upto = jnp.where(row <= col, 1.0, 0.0).astype(BF16)
        before = jnp.where(row < col, 1.0, 0.0).astype(BF16)
        masks = [_head_mask(lane, hh).astype(BF16) for hh in range(2)]
        qms = [q_ref[:, _pair(hh)] * jnp.asarray(SB_SCALE, BF16) * masks[hh % 2] for hh in range(nh)]
        doms = [do_ref[:, _pair(hh)] * masks[hh % 2] for hh in range(nh)]
        totals = [_pick_lane(r_ref[:, _pair(hh)], lane, HALF_LANES * (hh % 2)) for hh in range(nh)]
        strict = col < row

        def walk(blocks, state):
            chains = [(kb, diagonal, hh) for kb, diagonal in blocks for hh in range(nh)]
            keys = lambda kb: pl.ds(pl.multiple_of(kb * bq, bq), bq)
            cut = lambda x, diagonal: jnp.where(strict, x, 0.0) if diagonal else x
            zs = [_dot_nt(qms[hh], k_ref[keys(kb), _pair(hh)]) for kb, _, hh in chains]
            dws = [_dot_nt(doms[hh], v_ref[keys(kb), _pair(hh)]) for kb, _, hh in chains]
            pairs = []
            for z, (_, diagonal, _) in zip(zs, chains):
                ls, lk = _log_sigmoid_pair(z)
                pairs.append((ls, cut(lk, diagonal)))
            incls = [_split_dot(lk, upto) for _, lk in pairs]
            pres = [st[0] for st in state]
            ws, gs = [], []
            for (ls, lk), incl, dw, (_, diagonal, hh) in zip(pairs, incls, dws, chains):
                w = cut(jnp.exp(ls + ((totals[hh] - pres[hh]) - incl)), diagonal)
                ws.append(w.astype(BF16))
                gs.append(w * dw)
                pres[hh] = pres[hh] + jnp.sum(lk, axis=1, keepdims=True)
            gsums = [_dot(g.astype(BF16), before) for g in gs]
            gpres = [st[1] for st in state]
            dzs = []
            for (ls, _), g, gsum, (_, diagonal, hh) in zip(pairs, gs, gsums, chains):
                dzs.append(cut(g - jnp.exp(ls) * (g + (gpres[hh] + gsum)), diagonal).astype(BF16))
                gpres[hh] = gpres[hh] + jnp.sum(g, axis=1, keepdims=True)
            dqs = [st[2] for st in state]
            dk_parts, dv_parts = [], []
            for dzb, w, (kb, _, hh) in zip(dzs, ws, chains):
                dk_parts.append(_dot_tn(dzb, qms[hh]))
                dv_parts.append(_dot_tn(w, doms[hh]))
                dqs[hh] = dqs[hh] + _dot(dzb, k_ref[keys(kb), _pair(hh)])
            for b, (kb, _) in enumerate(blocks):
                for pair in range(nh // 2):
                    c0 = b * nh + 2 * pair
                    dk_ref[keys(kb), _pair(2 * pair)] += dk_parts[c0] + dk_parts[c0 + 1]
                    dv_ref[keys(kb), _pair(2 * pair)] += dv_parts[c0] + dv_parts[c0 + 1]
            return tuple(zip(pres, gpres, dqs))

        def finish(state):
            for pair in range(nh // 2):
                both = jnp.where(lane < HALF_LANES, state[2 * pair][2], state[2 * pair + 1][2])
                dq_ref[:, _pair(2 * pair)] = (both * SB_SCALE).astype(BF16)

        zero = ((jnp.zeros((bq, 1), F32), jnp.zeros((bq, 1), F32), jnp.zeros((bq, LANES), F32)),) * nh

        @pl.when(i == 0)
        def _():
            finish(walk([(0, True)], zero))

        @pl.when(i > 0)
        def _():
            state = lax.fori_loop(ks_ref[hp, i], i - 1, lambda kb, st: walk([(kb, False)], st), zero)
            finish(walk([(i - 1, False), (i, True)], state))

    return pl.pallas_call(
        body, name="sb_bwd",
        grid_spec=pltpu.PrefetchScalarGridSpec(
            num_scalar_prefetch=1, grid=(SB_HEADS // nh, nq),
            in_specs=[pl.BlockSpec((bq, width), lambda h, i, ks: (i, h)),
                      pl.BlockSpec((s, width), lambda h, i, ks: (0, h), pipeline_mode=pl.Buffered(1)),
                      pl.BlockSpec((s, width), lambda h, i, ks: (0, h), pipeline_mode=pl.Buffered(1)),
                      pl.BlockSpec((bq, width), lambda h, i, ks: (i, h)),
                      pl.BlockSpec((bq, width), lambda h, i, ks: (i, h))],
            out_specs=[pl.BlockSpec((bq, width), lambda h, i, ks: (i, h)),
                       pl.BlockSpec((s, width), lambda h, i, ks: (0, h), pipeline_mode=pl.Buffered(1)),
                       pl.BlockSpec((s, width), lambda h, i, ks: (0, h), pipeline_mode=pl.Buffered(1))]),
        out_shape=[jax.ShapeDtypeStruct((s, SB_WIDTH), BF16), jax.ShapeDtypeStruct((s, SB_WIDTH), F32),
                   jax.ShapeDtypeStruct((s, SB_WIDTH), F32)],
        compiler_params=_params(("arbitrary", "arbitrary"), VMEM_LIMIT),
    )(kstart, q, k, v, do, rfin)


def _mla_bwd_call(q, k, v, do, o, lse):
    s = q.shape[0]
    bq = _row_tile(s, MLA_BWD_BQ)
    bk = _row_tile(s, MLA_BWD_BK)
    nq = s // bq
    assert bk % bq == 0
    step = min(bk // 2, MLA_BWD_STEP)
    nsub = bk // step
    assert nsub % 2 == 0

    def body(q_ref, k_ref, v_ref, do_ref, o_ref, lse_ref, dq_ref, dk_ref, dv_ref, dom_ref, s_ref, dp_ref, pb_ref,
             ds_ref):
        i = pl.program_id(1)

        @pl.when(i == 0)
        def _():
            dk_ref[...] = jnp.zeros_like(dk_ref)
            dv_ref[...] = jnp.zeros_like(dv_ref)

        lane = lax.broadcasted_iota(I32, (bq, LANES), 1)
        row = lax.broadcasted_iota(I32, (step, bq), 1)
        col = lax.broadcasted_iota(I32, (step, bq), 0)
        n_full = (i * bq) // bk
        do2 = do_ref[...]
        prod = do2.astype(F32) * o_ref[...]
        ones = jnp.ones((8, LANES), BF16)
        deltas, lses = [], []
        for hh in range(2):
            head = _head_mask(lane, hh)
            dom_ref[hh] = do2 * head.astype(BF16)
            part = prod * head
            hi = part.astype(BF16)
            lo = (part - hi.astype(F32)).astype(BF16)
            deltas.append((_dot_nt(ones, hi) + _dot_nt(ones, lo))[0:1])
            lses.append(lse_ref[hh:hh + 1, :])

        def keys(g):
            return pl.ds(pl.multiple_of(g * step, step), step)

        def heads():
            return [(hh, slice(hh * HEAD_PAD, (hh + 1) * HEAD_PAD)) for hh in range(2)]

        def put_products(g, slot, qlo=0):
            vblk = v_ref[keys(g), :]
            for hh, cols in heads():
                s_ref[slot, hh, :, qlo:] = _dot_nt(k_ref[keys(g), cols], q_ref[qlo:, cols])
                dp_ref[slot, hh, :, qlo:] = _dot_nt(vblk, dom_ref[hh, qlo:, :])

        def add_grads(dqs, g, slot, qlo=0):
            rows = keys(g)
            new, dv_parts = [], []
            for hh, cols in heads():
                ds = ds_ref[slot, hh, :, qlo:]
                dk_ref[rows, cols] += _dot(ds, q_ref[qlo:, cols])
                dv_parts.append(_dot(pb_ref[slot, hh, :, qlo:], dom_ref[hh, qlo:, :]))
                upd = dqs[hh][:, qlo:] + _dot_tn(k_ref[rows, cols], ds)
                new.append(upd if qlo == 0 else jnp.concatenate([dqs[hh][:, :qlo], upd], axis=1))
            dv_ref[rows, :] += dv_parts[0] + dv_parts[1]
            return tuple(new)

        def substep(g, slot, dqs, masked, prefetch, qlo=0, next_qlo=0, prev_qlo=0, first=False):
            if prefetch:
                put_products(g + 1, 1 - slot, next_qlo)
            if not first:
                dqs = add_grads(dqs, g - 1, 1 - slot, prev_qlo)
            for hh, _ in heads():
                p = jnp.exp2(s_ref[slot, hh, :, qlo:] - lses[hh][:, qlo:])
                if masked:
                    p = jnp.where(col[:, qlo:] + g * step <= row[:, qlo:] + i * bq, p, 0.0)
                ds_ref[slot, hh, :, qlo:] = (p * (dp_ref[slot, hh, :, qlo:] - deltas[hh][:, qlo:])).astype(BF16)
                pb_ref[slot, hh, :, qlo:] = p.astype(BF16)
            return dqs

        def first_query(t, masked):
            return t * step if (masked and bk == bq and 0 <= t < nsub) else 0

        def chunk(kb, dqs, masked, first=False):
            for t in range(nsub):
                last = masked and t == nsub - 1
                dqs = substep(nsub * kb + t, t % 2, dqs, masked, not last, first_query(t, masked),
                              first_query(t + 1, masked), first_query(t - 1, masked), first and t == 0)
            return dqs

        def finish(dqs):
            dqs = add_grads(dqs, nsub * n_full + nsub - 1, 1, first_query(nsub - 1, True))
            dq_ref[:, :HEAD_PAD] = dqs[0].T * MLA_SCALE
            dq_ref[:, HEAD_PAD:] = dqs[1].T * MLA_SCALE

        put_products(0, 0)
        zero = jnp.zeros((HEAD_PAD, bq), F32)

        @pl.when(n_full == 0)
        def _():
            finish(chunk(0, (zero, zero), True, first=True))

        @pl.when(n_full > 0)
        def _():
            dqs = chunk(0, (zero, zero), False, first=True)
            dqs = lax.fori_loop(1, n_full, lambda kb, dqs: chunk(kb, dqs, False), dqs)
            finish(chunk(n_full, dqs, True))

    return pl.pallas_call(
        body, name="mla_bwd",
        grid=(4, nq),
        in_specs=[pl.BlockSpec((bq, 2 * HEAD_PAD), lambda h, i: (i, h)),
                  pl.BlockSpec((s, 2 * HEAD_PAD), lambda h, i: (0, h)),
                  pl.BlockSpec((s, LANES), lambda h, i: (0, h)),
                  pl.BlockSpec((bq, LANES), lambda h, i: (i, h)),
                  pl.BlockSpec((bq, LANES), lambda h, i: (i, h)),
                  pl.BlockSpec((None, 8, bq), lambda h, i: (h, 0, i))],
        out_specs=[pl.BlockSpec((bq, 2 * HEAD_PAD), lambda h, i: (i, h)),
                   pl.BlockSpec((s, 2 * HEAD_PAD), lambda h, i: (0, h), pipeline_mode=pl.Buffered(1)),
                   pl.BlockSpec((s, LANES), lambda h, i: (0, h), pipeline_mode=pl.Buffered(1))],
        out_shape=[jax.ShapeDtypeStruct((s, MLA_PAD_WIDTH), F32), jax.ShapeDtypeStruct((s, MLA_PAD_WIDTH), F32),
                   jax.ShapeDtypeStruct((s, MLA_WIDTH), F32)],
        scratch_shapes=[pltpu.VMEM((2, bq, LANES), BF16),
                        pltpu.VMEM((2, 2, step, bq), F32), pltpu.VMEM((2, 2, step, bq), F32),
                        pltpu.VMEM((2, 2, step, bq), BF16), pltpu.VMEM((2, 2, step, bq), BF16)],
        compiler_params=_params(("arbitrary", "arbitrary"), VMEM_LIMIT),
    )(q, k, v, do, o, lse)


def _rms_bwd(d_out, inp, r, weight, n):
    normed = inp * r
    gw = d_out * weight
    d_in = r * (gw - normed * (jnp.sum(gw * normed, axis=-1, keepdims=True) * (1.0 / n)))
    return d_in, d_out * normed


def _mla_prep_bwd_call(dq, dk, dv, q0, k0, cqn, ckvn, c_q, c_kv, cos_t, sin_t,
                       q_lora_norm, kv_lora_norm, qhn_pad, khn_pad, w_uq_bf, w_uk_bf, w_uv_bf):
    s = dq.shape[0]
    tm = _row_tile(s, 512)

    def body(dq_ref, dk_ref, dv_ref, q0_ref, k0_ref, cqn_ref, ckvn_ref, cq_ref, ckv_ref,
             cos_ref, sin_ref, qln_ref, kvln_ref, qhn_ref, khn_ref, wuq_ref, wuk_ref, wuv_ref,
             dcq_ref, dckv_ref, dkr_ref, gwuq_ref, gwuk_ref, gwuv_ref, gqln_ref, gkvln_ref, gqhn_ref, gkhn_ref,
             dq0_ref, dk0_ref, tmp_ref):
        @pl.when(pl.program_id(0) == 0)
        def _():
            for ref in (gwuq_ref, gwuk_ref, gwuv_ref, gqln_ref, gkvln_ref, gqhn_ref, gkhn_ref):
                ref[...] = jnp.zeros_like(ref)

        cos_t, sin_t = cos_ref[...], sin_ref[...]
        lane = lax.broadcasted_iota(I32, (tm, LANES), 1)
        rope_lanes = jnp.logical_or(lane < ROPE_HALF,
                                    jnp.logical_and(lane >= HALF_LANES, lane < HALF_LANES + ROPE_HALF))
        heads = [slice(h * HEAD_PAD, (h + 1) * HEAD_PAD) for h in range(MLA_HEADS)]

        def head_norm_bwd(d_ref, x0_ref, w_ref, out_ref, scale):
            w = w_ref[...]
            inv = [lax.rsqrt(jnp.sum(x0_ref[:, cols] * x0_ref[:, cols], axis=-1, keepdims=True)
                             * (1.0 / MLA_QK_DIM) + EPS) for cols in heads]
            for cols in heads:
                tmp_ref[:, cols] = _rope_adjoint(d_ref[:, cols] * scale, cos_t, sin_t)
            dots = [jnp.sum(tmp_ref[:, cols] * w * (x0_ref[:, cols] * r), axis=-1, keepdims=True)
                    for cols, r in zip(heads, inv)]
            g_w = jnp.zeros((1, LANES), F32)
            rope_sum = jnp.zeros((tm, LANES), F32)
            for cols, r, dot in zip(heads, inv, dots):
                normed = x0_ref[:, cols] * r
                d_n = tmp_ref[:, cols]
                d_x0 = r * (d_n * w - normed * (dot * (1.0 / MLA_QK_DIM)))
                out_ref[:, cols] = d_x0.astype(BF16)
                g_w = g_w + jnp.sum(d_n * normed, axis=0, keepdims=True)
                rope_sum = rope_sum + jnp.where(rope_lanes, d_x0, 0.0)
            return g_w, rope_sum

        g_qhn, _ = head_norm_bwd(dq_ref, q0_ref, qhn_ref, dq0_ref, 1.0)
        g_khn, d_kr = head_norm_bwd(dk_ref, k0_ref, khn_ref, dk0_ref, LN2)
        cqn, ckvn = cqn_ref[...], ckvn_ref[...]
        d_q0b, d_k0b, dvb = dq0_ref[...], dk0_ref[...], dv_ref[...].astype(BF16)
        d_cqn = _dot_nt(d_q0b, wuq_ref[...])
        gwuq_ref[...] += _dot_tn(cqn, d_q0b)
        d_ckvn = _dot_nt(d_k0b, wuk_ref[...]) + _dot_nt(dvb, wuv_ref[...])
        gwuk_ref[...] += _dot_tn(ckvn, d_k0b)
        gwuv_ref[...] += _dot_tn(ckvn, dvb)
        gqhn_ref[...] += g_qhn
        gkhn_ref[...] += g_khn
        dkr_ref[...] = d_kr.astype(BF16)
        cq = cq_ref[...]
        rcq = lax.rsqrt(jnp.mean(cq * cq, axis=-1, keepdims=True) + EPS)
        d_cq, gl = _rms_bwd(d_cqn, cq, rcq, qln_ref[...], Q_LORA_RANK)
        dcq_ref[...] = d_cq.astype(BF16)
        gqln_ref[...] += jnp.sum(gl, axis=0, keepdims=True)
        ckv = ckv_ref[...]
        rckv = lax.rsqrt(jnp.mean(ckv * ckv, axis=-1, keepdims=True) + EPS)
        d_ckv, gl = _rms_bwd(d_ckvn, ckv, rckv, kvln_ref[...], KV_LORA_RANK)
        dckv_ref[...] = d_ckv.astype(BF16)
        gkvln_ref[...] += jnp.sum(gl, axis=0, keepdims=True)

    return pl.pallas_call(
        body, name="mla_prep_bwd",
        grid=(s // tm,),
        in_specs=[_rows(tm, MLA_PAD_WIDTH), _rows(tm, MLA_PAD_WIDTH), _rows(tm, MLA_WIDTH),
                  _rows(tm, MLA_PAD_WIDTH), _rows(tm, MLA_PAD_WIDTH),
                  _rows(tm, Q_LORA_RANK), _rows(tm, KV_LORA_RANK), _rows(tm, Q_LORA_RANK), _rows(tm, KV_LORA_RANK),
                  _rows(tm, LANES), _rows(tm, LANES),
                  _full((1, Q_LORA_RANK)), _full((1, KV_LORA_RANK)), _full((1, LANES)), _full((1, LANES)),
                  _full((Q_LORA_RANK, MLA_PAD_WIDTH)), _full((KV_LORA_RANK, MLA_PAD_WIDTH)),
                  _full((KV_LORA_RANK, MLA_WIDTH))],
        out_specs=[_rows(tm, Q_LORA_RANK), _rows(tm, KV_LORA_RANK), _rows(tm, LANES),
                   _full((Q_LORA_RANK, MLA_PAD_WIDTH)), _full((KV_LORA_RANK, MLA_PAD_WIDTH)),
                   _full((KV_LORA_RANK, MLA_WIDTH)),
                   _full((1, Q_LORA_RANK)), _full((1, KV_LORA_RANK)), _full((1, LANES)), _full((1, LANES))],
        out_shape=[jax.ShapeDtypeStruct((s, Q_LORA_RANK), BF16), jax.ShapeDtypeStruct((s, KV_LORA_RANK), BF16),
                   jax.ShapeDtypeStruct((s, LANES), BF16),
                   jax.ShapeDtypeStruct((Q_LORA_RANK, MLA_PAD_WIDTH), F32),
                   jax.ShapeDtypeStruct((KV_LORA_RANK, MLA_PAD_WIDTH), F32),
                   jax.ShapeDtypeStruct((KV_LORA_RANK, MLA_WIDTH), F32),
                   jax.ShapeDtypeStruct((1, Q_LORA_RANK), F32), jax.ShapeDtypeStruct((1, KV_LORA_RANK), F32),
                   jax.ShapeDtypeStruct((1, LANES), F32), jax.ShapeDtypeStruct((1, LANES), F32)],
        scratch_shapes=[pltpu.VMEM((tm, MLA_PAD_WIDTH), BF16), pltpu.VMEM((tm, MLA_PAD_WIDTH), BF16),
                        pltpu.VMEM((tm, MLA_PAD_WIDTH), F32)],
        compiler_params=_params(("arbitrary",), VMEM_LIMIT),
    )(dq, dk, dv, q0, k0, cqn, ckvn, c_q, c_kv, cos_t, sin_t,
      q_lora_norm, kv_lora_norm, qhn_pad, khn_pad, w_uq_bf, w_uk_bf, w_uv_bf)


def _dh_call(pieces, hb, x, dy, ada, norm_w, w_in_bf):
    s = x.shape[0]
    tm = _row_tile(s, 512)
    widths = [p.shape[1] for p in pieces]
    offsets = [sum(widths[:j]) for j in range(len(widths))]
    assert offsets[-1] + widths[-1] == IN_COLS_PAD
    n = len(pieces)

    def body(*refs):
        p_refs = refs[:n]
        (hb_ref, x_ref, dy_ref, sh_ref, sc_ref, nw_ref, w_ref, gx_ref, gw_ref, dsh_ref, dsc_ref, gnw_ref,
         dp_ref, acc_ref) = refs[n:]

        @pl.when(pl.program_id(0) == 0)
        def _():
            acc_ref[...] = jnp.zeros_like(acc_ref)
            dsh_ref[...] = jnp.zeros_like(dsh_ref)
            dsc_ref[...] = jnp.zeros_like(dsc_ref)
            gnw_ref[...] = jnp.zeros_like(gnw_ref)

        for p_ref, c0, width in zip(p_refs, offsets, widths):
            dp_ref[:, c0:c0 + width] = p_ref[...].astype(BF16)
        acc_ref[...] += _dot_tn(hb_ref[...], dp_ref[...])

        @pl.when(pl.program_id(0) == pl.num_programs(0) - 1)
        def _():
            gw_ref[...] = acc_ref[...].astype(BF16)

        dh = _dot_nt(dp_ref[...], w_ref[...])
        xx = x_ref[...]
        r0 = lax.rsqrt(jnp.mean(xx * xx, axis=-1, keepdims=True) + EPS)
        xn = xx * r0
        nw = nw_ref[...]
        dsh_ref[...] += jnp.sum(dh, axis=0, keepdims=True)
        dsc_ref[...] += jnp.sum(dh * (xn * nw), axis=0, keepdims=True)
        dn = dh * (1.0 + sc_ref[...])
        gnw_ref[...] += jnp.sum(dn * xn, axis=0, keepdims=True)
        dxn = dn * nw
        gx_ref[...] = dy_ref[...] + r0 * (dxn - xn * jnp.mean(dxn * xn, axis=-1, keepdims=True))

    return pl.pallas_call(
        body, name="in_proj_bwd",
        grid=(s // tm,),
        in_specs=[_rows(tm, w) for w in widths]
        + [_rows(tm, D_MODEL), _rows(tm, D_MODEL), _rows(tm, D_MODEL),
           _ada_part(0), _ada_part(1), _full((1, D_MODEL)),
           pl.BlockSpec((D_MODEL, IN_COLS_PAD), lambda i: (0, 0), pipeline_mode=pl.Buffered(1))],
        out_specs=[_rows(tm, D_MODEL),
                   pl.BlockSpec((D_MODEL, IN_COLS_PAD), lambda i: (0, 0), pipeline_mode=pl.Buffered(1)),
                   _full((1, D_MODEL)), _full((1, D_MODEL)), _full((1, D_MODEL))],
        out_shape=[jax.ShapeDtypeStruct((s, D_MODEL), F32), jax.ShapeDtypeStruct((D_MODEL, IN_COLS_PAD), BF16),
                   jax.ShapeDtypeStruct((1, D_MODEL), F32), jax.ShapeDtypeStruct((1, D_MODEL), F32),
                   jax.ShapeDtypeStruct((1, D_MODEL), F32)],
        scratch_shapes=[pltpu.VMEM((tm, IN_COLS_PAD), BF16), pltpu.VMEM((D_MODEL, IN_COLS_PAD), F32)],
        compiler_params=_params(("arbitrary",), VMEM_LIMIT),
    )(*pieces, hb, x, dy, ada, ada, norm_w, w_in_bf)


def _adamw(g, w, m, v):
    m = ADAM_B1 * m + (1.0 - ADAM_B1) * g
    v = ADAM_B2 * v + (1.0 - ADAM_B2) * (g * g)
    m_hat = m / (1.0 - ADAM_B1 ** ADAM_STEP)
    v_hat = v / (1.0 - ADAM_B2 ** ADAM_STEP)
    delta = -ADAM_LR * (m_hat / (jnp.sqrt(v_hat) + ADAM_EPS) + ADAM_WD * w)
    return delta, m, v


def _adam_shard_call(name, parts, w, m, v):
    r, c = w.shape
    tr = r if r <= 512 else 256

    def body(p_ref, w_ref, m_ref, v_ref, g_ref, d_ref, nm_ref, nv_ref):
        g = ((p_ref[0].astype(F32) + p_ref[1].astype(F32)) + p_ref[2].astype(F32)) + p_ref[3].astype(F32)
        g_ref[...] = g
        d_ref[...], nm_ref[...], nv_ref[...] = _adamw(g, w_ref[...], m_ref[...], v_ref[...])

    blk = pl.BlockSpec((tr, c), lambda i: (i, 0))
    return pl.pallas_call(
        body, name=name,
        grid=(r // tr,),
        in_specs=[pl.BlockSpec((4, tr, c), lambda i: (0, i, 0)), blk, blk, blk],
        out_specs=[blk] * 4,
        out_shape=[jax.ShapeDtypeStruct((r, c), F32)] * 4,
        compiler_params=_params(("arbitrary",), VMEM_LIMIT),
    )(parts, w, m, v)


def _adam_shard_transposed_call(name, parts, w_t, m_t, v_t):
    c, r = w_t.shape
    assert parts.shape[1:] == (r, c)
    tr = r if r <= 512 else 256
    c_pad = -(-c // LANES) * LANES

    def body(p_ref, w_ref, m_ref, v_ref, g_ref, d_ref, nm_ref, nv_ref, pad_ref):
        @pl.when(pl.program_id(0) == 0)
        def _():
            pad_ref[...] = jnp.zeros_like(pad_ref)

        pad_ref[:, :c] = ((p_ref[0].astype(F32) + p_ref[1].astype(F32)) + p_ref[2].astype(F32)) + p_ref[3].astype(F32)
        g = pad_ref[...].T[:c]
        g_ref[...] = g
        d_ref[...], nm_ref[...], nv_ref[...] = _adamw(g, w_ref[...], m_ref[...], v_ref[...])

    blk = pl.BlockSpec((c, tr), lambda i: (0, i))
    return pl.pallas_call(
        body, name=name,
        grid=(r // tr,),
        in_specs=[pl.BlockSpec((4, tr, c), lambda i: (0, i, 0)), blk, blk, blk],
        out_specs=[blk] * 4,
        out_shape=[jax.ShapeDtypeStruct((c, r), F32)] * 4,
        scratch_shapes=[pltpu.VMEM((tr, c_pad), F32)],
        compiler_params=_params(("arbitrary",), VMEM_LIMIT),
    )(parts, w_t, m_t, v_t)


def _adam_ada_call(c_all, d_all, w, m, v):
    r, c = w.shape
    tr = 256

    def body(c_ref, d_ref, w_ref, m_ref, v_ref, g_ref, dl_ref, nm_ref, nv_ref):
        cc = c_ref[...]
        sc = cc * _sigmoid(cc)
        dd = d_ref[...]
        sc_hi = sc.astype(BF16)
        sc_lo = (sc - sc_hi.astype(F32)).astype(BF16)
        dd_hi = dd.astype(BF16)
        dd_lo = (dd - dd_hi.astype(F32)).astype(BF16)
        g = _dot_tn(sc_hi, dd_hi) + (_dot_tn(sc_hi, dd_lo) + _dot_tn(sc_lo, dd_hi))
        g_ref[...] = g
        dl_ref[...], nm_ref[...], nv_ref[...] = _adamw(g, w_ref[...], m_ref[...], v_ref[...])

    blk = pl.BlockSpec((tr, c), lambda i: (i, 0))
    return pl.pallas_call(
        body, name="adam_w_ada",
        grid=(r // tr,),
        in_specs=[pl.BlockSpec((16, tr), lambda i: (0, i)), pl.BlockSpec((16, c), lambda i: (0, 0)), blk, blk, blk],
        out_specs=[blk] * 4,
        out_shape=[jax.ShapeDtypeStruct((r, c), F32)] * 4,
        compiler_params=_params(("arbitrary",), VMEM_LIMIT),
    )(c_all, d_all, w, m, v)


def _adam_vectors_call(packs, offsets, vectors):
    nv = len(vectors)

    def body(*refs):
        p_ref, ins, outs = refs[0], refs[1:1 + 3 * nv], refs[1 + 3 * nv:]
        for j, off in enumerate(offsets):
            n = ins[3 * j].shape[1]
            span = -(-n // LANES) * LANES
            g = p_ref[0, :, off:off + span]
            for b in range(1, 8):
                g = g + p_ref[b, :, off:off + span]
            g = g[:, :n]
            outs[j][...] = g
            outs[nv + j][...], outs[2 * nv + j][...], outs[3 * nv + j][...] = _adamw(
                g, ins[3 * j][...], ins[3 * j + 1][...], ins[3 * j + 2][...])

    flat = [a for t in vectors for a in t]
    res = pl.pallas_call(
        body, name="adam_vectors",
        out_shape=[jax.ShapeDtypeStruct(t[0].shape, F32) for _ in range(4) for t in vectors],
    )(packs, *flat)
    return [res[k * nv:(k + 1) * nv] for k in range(4)]


ROPE_HALF = MLA_ROPE_DIM // 2
NOPE_A = MLA_NOPE_DIM - ROPE_HALF


def _zeros_like_lanes(t, n):
    return jnp.zeros(t.shape[:-1] + (n,), t.dtype)


def _to_head_lanes(t):
    nope, rope = t[..., :MLA_NOPE_DIM], t[..., MLA_NOPE_DIM:]
    return jnp.concatenate([rope[..., :ROPE_HALF], nope[..., :NOPE_A], rope[..., ROPE_HALF:], nope[..., NOPE_A:],
                            _zeros_like_lanes(t, HEAD_PAD - MLA_QK_DIM)], axis=-1)


def _from_head_lanes(t):
    return jnp.concatenate([t[..., ROPE_HALF:HALF_LANES], t[..., HALF_LANES + ROPE_HALF:MLA_QK_DIM],
                            t[..., :ROPE_HALF], t[..., HALF_LANES:HALF_LANES + ROPE_HALF]], axis=-1)


def _nope_to_head_lanes(t):
    return jnp.concatenate([_zeros_like_lanes(t, ROPE_HALF), t[..., :NOPE_A], _zeros_like_lanes(t, ROPE_HALF),
                            t[..., NOPE_A:], _zeros_like_lanes(t, HEAD_PAD - MLA_QK_DIM)], axis=-1)


def _rope_to_head_lanes(t):
    return jnp.concatenate([t[..., :ROPE_HALF], _zeros_like_lanes(t, HALF_LANES - ROPE_HALF), t[..., ROPE_HALF:],
                            _zeros_like_lanes(t, HALF_LANES - ROPE_HALF)], axis=-1)


def _rope_tables(positions):
    inv_freq = (ROPE_THETA ** (-jnp.arange(0, MLA_ROPE_DIM, 2, dtype=F32) / MLA_ROPE_DIM))[None]
    signed = _rope_to_head_lanes(jnp.concatenate([-inv_freq, inv_freq], axis=1))
    ang = positions.astype(F32)[:, None] * signed
    return jnp.cos(ang), jnp.sin(ang)


def _unshard_cols(g):
    return jnp.transpose(g, (1, 0, 2)).reshape(g.shape[1], 4 * g.shape[2])


def _shard_cols(g):
    r, c4 = g.shape
    return jnp.transpose(g.reshape(r, 4, c4 // 4), (1, 0, 2))


def kernel(x, c, positions, w_ada, b_ada, norm_w, w_in, q_lora_norm, w_uq, kv_lora_norm, w_ukv, q_head_norm, k_head_norm, w_out, loss_target, m_w_ada, m_b_ada, m_norm_w, m_w_in, m_q_lora_norm, m_w_uq, m_kv_lora_norm, m_w_ukv, m_q_head_norm, m_k_head_norm, m_w_out, v_w_ada, v_b_ada, v_norm_w, v_w_in, v_q_lora_norm, v_w_uq, v_kv_lora_norm, v_w_ukv, v_q_head_norm, v_k_head_norm, v_w_out):
    chip = 2 * lax.axis_index("x") + lax.axis_index("y")
    me8 = 2 * chip + lax.axis_index("c")
    ada_cols = w_ada.shape[2]
    c_all = _allgather_rows_call(c)[:, 0, :]
    ada_part = _ada_call(c_all, w_ada[0], lax.dynamic_slice_in_dim(b_ada, chip * ada_cols, ada_cols, axis=1))
    ada_g, win_g, wuq_g, wukv_g, wout_g = _gather_call(
        [ada_part[None]] + [w.astype(BF16) for w in (w_in, w_uq, w_ukv, w_out)], [False, True, True, True, True])
    ada = lax.dynamic_slice_in_dim(ada_g, me8, 1, axis=1).reshape(1, 4 * ada_cols)
    (sq_sum, grad_x, g_w_in, g_w_uq, g_w_ukv, g_w_out, d_ada, g_norm_w, g_qln, g_kvln, g_qhn, g_khn) = _local_step(
        x[0], ada, positions[0], loss_target[0], norm_w, win_g,
        q_lora_norm, _unshard_cols(wuq_g), kv_lora_norm, _unshard_cols(wukv_g), q_head_norm, k_head_norm,
        wout_g.reshape(D_MODEL, D_MODEL))

    grads = [g.astype(BF16) for g in (g_w_in, _shard_cols(g_w_uq), _shard_cols(g_w_ukv),
                                      g_w_out.reshape(4, D_MODEL // 4, D_MODEL))]
    pieces = [d_ada, g_norm_w, g_qln, g_kvln, g_qhn, g_khn, (0.5 * sq_sum / D_MODEL).reshape(1, 1)]
    spans = [-(-p.shape[1] // LANES) * LANES for p in pieces]
    starts = [sum(spans[:j]) for j in range(len(spans))]
    small = jnp.concatenate([jnp.pad(p, ((0, 0), (0, sp - p.shape[1]))) for p, sp in zip(pieces, spans)], axis=1)
    core = lax.axis_index("c").astype(I32).reshape(1)
    chip_halves = _sum_halves_call(core, grads, _swap_halves_call(grads))
    parts, packs = _exchange_call(chip_halves, small)
    loss = jnp.sum(packs[:, 0, starts[-1]])

    names = ["adam_w_in", "adam_w_uq", "adam_w_ukv", "adam_w_out"]
    shard_w = [(w_in, m_w_in, v_w_in), (w_uq, m_w_uq, v_w_uq), (w_ukv, m_w_ukv, v_w_ukv),
               (w_out, m_w_out, v_w_out)]
    res = {}
    for name, p_g, (w, m, v) in zip(names, parts, shard_w):
        if w.shape[2] % LANES:
            res_t = _adam_shard_transposed_call(name, p_g, w[0].T, m[0].T, v[0].T)
            res[name] = [t.T for t in res_t]
        else:
            res[name] = _adam_shard_call(name, p_g, w[0], m[0], v[0])
    d_all = lax.dynamic_slice_in_dim(packs[:, 0, :], starts[0] + chip * ada_cols, ada_cols, axis=1)
    res_ada = _adam_ada_call(jnp.pad(c_all, ((0, 8), (0, 0))), jnp.pad(d_all, ((0, 8), (0, 0))),
                             w_ada[0], m_w_ada[0], v_w_ada[0])
    vectors = [(b_ada, m_b_ada, v_b_ada), (norm_w, m_norm_w, v_norm_w), (q_lora_norm, m_q_lora_norm, v_q_lora_norm),
               (kv_lora_norm, m_kv_lora_norm, v_kv_lora_norm), (q_head_norm, m_q_head_norm, v_q_head_norm),
               (k_head_norm, m_k_head_norm, v_k_head_norm)]
    vec_out = _adam_vectors_call(packs, starts[:len(vectors)], vectors)

    def ordered(kind):
        big = lambda name: res[name][kind][None]
        return [res_ada[kind][None], vec_out[kind][0], vec_out[kind][1], big("adam_w_in"), vec_out[kind][2],
                big("adam_w_uq"), vec_out[kind][3], big("adam_w_ukv"), vec_out[kind][4], vec_out[kind][5],
                big("adam_w_out")]

    return (loss, grad_x[None], *ordered(0), *ordered(1), *ordered(2), *ordered(3))


def _local_step(x2, ada, positions, tgt, norm_w, w_in_shards, q_lora_norm, w_uq_full,
                kv_lora_norm, w_ukv_full, q_head_norm, k_head_norm, w_out_full):
    in_shard = w_in_shards.shape[2]
    ckv_tail = C_GM - 3 * in_shard
    assert 0 <= ckv_tail and ckv_tail + MLA_ROPE_DIM + MLA_WIDTH == in_shard
    last = w_in_shards[3]
    w_in_bf = jnp.concatenate(
        [w_in_shards[0], w_in_shards[1], w_in_shards[2], last[:, :ckv_tail], last[:, ckv_tail + MLA_ROPE_DIM:],
         _rope_to_head_lanes(last[:, ckv_tail:ckv_tail + MLA_ROPE_DIM])], axis=1).astype(BF16)
    w_uq_bf = _to_head_lanes(w_uq_full.reshape(Q_LORA_RANK, MLA_HEADS, MLA_QK_DIM)).reshape(
        Q_LORA_RANK, MLA_PAD_WIDTH).astype(BF16)
    w_ukv_heads = w_ukv_full.reshape(KV_LORA_RANK, MLA_HEADS, 2 * MLA_NOPE_DIM)
    w_uk_bf = _nope_to_head_lanes(w_ukv_heads[:, :, :MLA_NOPE_DIM]).reshape(KV_LORA_RANK, MLA_PAD_WIDTH).astype(BF16)
    w_uv_bf = w_ukv_heads[:, :, MLA_NOPE_DIM:].reshape(KV_LORA_RANK, MLA_WIDTH).astype(BF16)
    w_out_bf = w_out_full.astype(BF16)
    qhn_pad, khn_pad = _to_head_lanes(q_head_norm), _to_head_lanes(k_head_norm)
    cos_t, sin_t = _rope_tables(positions)

    hb, q_sb, k_sb, v_sb, g_sb, c_q, c_kv, g_mla, q_m, k_m, v_m, cqn, ckvn, q0, k0 = _pre_call(
        x2, ada, norm_w, w_in_bf, cos_t, sin_t, q_lora_norm, kv_lora_norm, qhn_pad, khn_pad,
        w_uq_bf, w_uk_bf, w_uv_bf)
    o_sb, r_sb, kstart = _sb_fwd_call(q_sb, k_sb, v_sb)
    o_mla, lse = _mla_fwd_call(q_m, k_m, v_m)
    do_sb, do_mla, dg_sb, dg_mla, dy, g_w_out, d_gate, sq = _out_call(
        o_sb, g_sb, o_mla, g_mla, x2, tgt, ada, w_out_bf)

    dq_sb, dk_sb, dv_sb = _sb_bwd_call(kstart, q_sb, k_sb, v_sb, do_sb, r_sb)
    dq_m, dk_m, dv_m = _mla_bwd_call(q_m, k_m, v_m, do_mla, o_mla, lse)
    (d_cq, d_ckv, d_kr, g_wuq_pad, g_wuk_pad, g_wuv, g_qln, g_kvln, g_qhn, g_khn) = _mla_prep_bwd_call(
        dq_m, dk_m, dv_m, q0, k0, cqn, ckvn, c_q, c_kv, cos_t, sin_t,
        q_lora_norm, kv_lora_norm, qhn_pad, khn_pad, w_uq_bf, w_uk_bf, w_uv_bf)
    grad_x, g_win_pad, d_shift, d_scale, g_norm_w = _dh_call(
        [dq_sb, dk_sb, dv_sb, dg_sb, d_cq, d_ckv, dg_mla, d_kr], hb, x2, dy, ada, norm_w, w_in_bf)

    g_kr = g_win_pad[:, C_KR:]
    g_last = jnp.concatenate([g_win_pad[:, 3 * in_shard:C_GM], g_kr[:, :ROPE_HALF],
                              g_kr[:, HALF_LANES:HALF_LANES + ROPE_HALF], g_win_pad[:, C_GM:C_KR]], axis=1)
    g_w_in = jnp.stack([g_win_pad[:, j * in_shard:(j + 1) * in_shard] for j in range(3)] + [g_last])
    g_w_uq = _from_head_lanes(g_wuq_pad.reshape(Q_LORA_RANK, MLA_HEADS, HEAD_PAD)).reshape(Q_LORA_RANK, -1)
    g_w_ukv = jnp.concatenate(
        [_from_head_lanes(g_wuk_pad.reshape(KV_LORA_RANK, MLA_HEADS, HEAD_PAD))[:, :, :MLA_NOPE_DIM],
         g_wuv.reshape(KV_LORA_RANK, MLA_HEADS, MLA_NOPE_DIM)], axis=2).reshape(KV_LORA_RANK, -1)
    d_ada = jnp.concatenate([d_shift, d_scale, d_gate], axis=1)
    return (jnp.sum(sq), grad_x, g_w_in, g_w_uq, g_w_ukv, g_w_out, d_ada, g_norm_w, g_qln, g_kvln,
            _from_head_lanes(g_qhn), _from_head_lanes(g_khn))
```

```python
import math

import jax
import jax.numpy as jnp
from jax import lax
from jax.experimental import pallas as pl
from jax.experimental.pallas import tpu as pltpu

F32 = jnp.float32
BF16 = jnp.bfloat16
I32 = jnp.int32

D_MODEL = 1024
SB_HEADS = 8
SB_WIDTH = 512
MLA_HEADS = 8
MLA_QK_DIM = 96
MLA_NOPE_DIM = 64
MLA_ROPE_DIM = 32
MLA_WIDTH = 512
Q_LORA_RANK = 384
KV_LORA_RANK = 256
ROPE_THETA = 10000.0
EPS = 1e-6
LANES = 128
HALF_LANES = LANES // 2
HEAD_PAD = 128
MLA_PAD_WIDTH = MLA_HEADS * HEAD_PAD

C_Q, C_K, C_V, C_G = 0, 512, 1024, 1536
C_CQ, C_CKV, C_GM, C_KR = 2048, 2432, 2688, 3200
IN_COLS_PAD = 3328

ADAM_LR = 0.001
ADAM_B1 = 0.9
ADAM_B2 = 0.999
ADAM_EPS = 1e-08
ADAM_WD = 0.01
ADAM_STEP = 10

SB_SCALE = 0.125
SB_GROUP = 4
MLA_SCALE = 1.0 / math.sqrt(MLA_QK_DIM)
LN2 = math.log(2.0)
MLA_SCALE_LOG2 = MLA_SCALE / LN2
MLA_BQ = 1024
MLA_BWD_BQ = 1024
MLA_BK = 1024
MLA_BWD_BK = 1024
MLA_STEP = 512
MLA_BWD_STEP = 256
SB_DEAD = -104.0
MASK_NEG = -1e30

VMEM_LIMIT = 56 * 1024 * 1024
MESH = pl.DeviceIdType.MESH


def _dot(a, b):
    return jnp.dot(a, b, preferred_element_type=F32)


def _dot_nt(a, b):
    return lax.dot_general(a, b, (((1,), (1,)), ((), ())), preferred_element_type=F32)


def _dot_tn(a, b):
    return lax.dot_general(a, b, (((0,), (0,)), ((), ())), preferred_element_type=F32)


def _sigmoid(x):
    return 1.0 / (1.0 + jnp.exp(-x))


def _split_dot(a, m):
    hi = a.astype(BF16)
    lo = (a - hi.astype(F32)).astype(BF16)
    return _dot(hi, m) + _dot(lo, m)


def _params(sem, vmem=None):
    return pltpu.CompilerParams(dimension_semantics=sem, vmem_limit_bytes=vmem)


def _row_tile(s, want):
    return min(want, s)


def _hbm_spec():
    return pl.BlockSpec(memory_space=pltpu.HBM)


def _allgather_rows_call(row):
    def body(in_ref, out_ref, send_sems, recv_sems, loc_sem):
        x, y, c = lax.axis_index("x"), lax.axis_index("y"), lax.axis_index("c")
        flips = [(fx, fy, fc) for fx in (0, 1) for fy in (0, 1) for fc in (0, 1)][1:]

        def peer(r):
            fx, fy, fc = flips[r]
            return ((1 - x) if fx else x, (1 - y) if fy else y, (1 - c) if fc else c)

        def copy(r, slot):
            return pltpu.make_async_remote_copy(
                src_ref=in_ref, dst_ref=out_ref.at[slot], send_sem=send_sems.at[r], recv_sem=recv_sems.at[r],
                device_id=peer(r), device_id_type=MESH)

        local = pltpu.make_async_copy(in_ref, out_ref.at[4 * x + 2 * y + c], loc_sem)
        local.start()
        sends = [copy(r, 4 * x + 2 * y + c) for r in range(7)]
        for cp in sends:
            cp.start()
        for r in range(7):
            px, py, pc = peer(r)
            copy(r, 4 * px + 2 * py + pc).wait_recv()
        for cp in sends:
            cp.wait_send()
        local.wait()

    return pl.pallas_call(
        body, name="gather_rows",
        out_shape=jax.ShapeDtypeStruct((8,) + row.shape, row.dtype),
        in_specs=[_hbm_spec()], out_specs=_hbm_spec(),
        scratch_shapes=[pltpu.SemaphoreType.DMA((7,)), pltpu.SemaphoreType.DMA((7,)), pltpu.SemaphoreType.DMA],
    )(row)


def _gather_call(shards, split, angles):
    n = len(shards)
    halves = [s.shape[1] // 2 for s in shards]
    table_rows = _row_tile(angles.shape[0], 512)

    def body(*refs):
        ins, ang_ref, outs, cos_ref, sin_ref = refs[:n], refs[n], refs[n + 1:2 * n + 1], refs[2 * n + 1], refs[2 * n + 2]
        ici_send, ici_recv, d2d_send, d2d_recv, loc_sems = refs[2 * n + 3:]
        x, y, c = lax.axis_index("x"), lax.axis_index("y"), lax.axis_index("c")
        me = 2 * x + y
        peers = [(1 - x, y), (x, 1 - y), (1 - x, 1 - y)]

        def rows(a, which):
            return pl.ds(pl.multiple_of(which * halves[a], 16), halves[a])

        def ici(a, j, slot):
            px, py = peers[j]
            src, dst = ins[a].at[0], outs[a].at[slot]
            if split[a]:
                src, dst = src.at[rows(a, c)], dst.at[rows(a, c)]
            return pltpu.make_async_remote_copy(
                src_ref=src, dst_ref=dst,
                send_sem=ici_send.at[3 * a + j], recv_sem=ici_recv.at[3 * a + j],
                device_id=(px, py, c), device_id_type=MESH)

        def d2d(a, j, which):
            px, py = peers[j]
            piece = outs[a].at[2 * px + py, rows(a, which)]
            return pltpu.make_async_remote_copy(
                src_ref=piece, dst_ref=piece,
                send_sem=d2d_send.at[3 * a + j], recv_sem=d2d_recv.at[3 * a + j],
                device_id=(x, y, 1 - c), device_id_type=MESH)

        local = [pltpu.make_async_copy(ins[a].at[0], outs[a].at[me], loc_sems.at[a]) for a in range(n)]
        for cp in local:
            cp.start()
        sends = [ici(a, j, me) for a in range(n) for j in range(3)]
        for cp in sends:
            cp.start()

        def tables(t, _):
            rows = pl.ds(pl.multiple_of(t * table_rows, table_rows), table_rows)
            cos_ref[rows, :] = jnp.cos(ang_ref[rows, :])
            sin_ref[rows, :] = jnp.sin(ang_ref[rows, :])
            return 0

        lax.fori_loop(0, angles.shape[0] // table_rows, tables, 0)
        for a in range(n):
            for j in range(3):
                px, py = peers[j]
                ici(a, j, 2 * px + py).wait_recv()
                if split[a]:
                    cp = d2d(a, j, c)
                    cp.start()
                    sends.append(cp)
        for a in range(n):
            for j in range(3):
                if split[a]:
                    d2d(a, j, 1 - c).wait_recv()
        for cp in sends:
            cp.wait_send()
        for cp in local:
            cp.wait()

    return pl.pallas_call(
        body, name="gather_weights",
        out_shape=[jax.ShapeDtypeStruct((4,) + s.shape[1:], s.dtype) for s in shards]
        + [jax.ShapeDtypeStruct(angles.shape, F32)] * 2,
        in_specs=[_hbm_spec() for _ in shards] + [pl.BlockSpec(memory_space=pltpu.VMEM)],
        out_specs=[_hbm_spec() for _ in shards] + [pl.BlockSpec(memory_space=pltpu.VMEM)] * 2,
        scratch_shapes=[pltpu.SemaphoreType.DMA((3 * n,)), pltpu.SemaphoreType.DMA((3 * n,)),
                        pltpu.SemaphoreType.DMA((3 * n,)), pltpu.SemaphoreType.DMA((3 * n,)),
                        pltpu.SemaphoreType.DMA((n,))],
        compiler_params=pltpu.CompilerParams(vmem_limit_bytes=VMEM_LIMIT),
    )(*shards, angles)


def _swap_halves_call(grads):
    n = len(grads)
    halves = [g.shape[1] // 2 for g in grads]

    def body(*refs):
        g_in, hs, send_sems, recv_sems = refs[:n], refs[n:2 * n], refs[2 * n], refs[2 * n + 1]
        x, y, c = lax.axis_index("x"), lax.axis_index("y"), lax.axis_index("c")
        copies = []
        for a in range(n):
            theirs = pl.ds(pl.multiple_of((1 - c) * halves[a], 16), halves[a])
            copies.append(pltpu.make_async_remote_copy(
                src_ref=g_in[a].at[:, theirs], dst_ref=hs[a], send_sem=send_sems.at[a], recv_sem=recv_sems.at[a],
                device_id=(x, y, 1 - c), device_id_type=MESH))
        for cp in copies:
            cp.start()
        for cp in copies:
            cp.wait()

    return pl.pallas_call(
        body, name="swap_halves",
        out_shape=[jax.ShapeDtypeStruct((4, h, g.shape[2]), g.dtype) for g, h in zip(grads, halves)],
        in_specs=[_hbm_spec() for _ in grads], out_specs=[_hbm_spec() for _ in grads],
        scratch_shapes=[pltpu.SemaphoreType.DMA((n,)), pltpu.SemaphoreType.DMA((n,))],
    )(*grads)


def _sum_halves_call(core, grads, halves):
    n = len(grads)

    def body(core_ref, *refs):
        for g_ref, h_ref, o_ref in zip(refs[:n], refs[n:2 * n], refs[2 * n:]):
            o_ref[...] = (g_ref[...].astype(F32) + h_ref[...].astype(F32)).astype(o_ref.dtype)

    whole = lambda h: pl.BlockSpec(h.shape, lambda i, core_ref: (0, 0, 0))
    return pl.pallas_call(
        body, name="sum_halves",
        grid_spec=pltpu.PrefetchScalarGridSpec(
            num_scalar_prefetch=1, grid=(1,),
            in_specs=[pl.BlockSpec(h.shape, lambda i, core_ref: (0, core_ref[0], 0)) for h in halves]
            + [whole(h) for h in halves],
            out_specs=[whole(h) for h in halves]),
        out_shape=[jax.ShapeDtypeStruct(h.shape, h.dtype) for h in halves],
        compiler_params=_params(("arbitrary",), VMEM_LIMIT),
    )(core, *grads, *halves)


def _exchange_call(chip_halves, small):
    n = len(chip_halves)
    halves = [h.shape[1] for h in chip_halves]

    def body(*refs):
        g_in, small_in = refs[:n], refs[n]
        parts, packs = refs[n + 1:2 * n + 1], refs[2 * n + 1]
        ici_send, ici_recv, d2d_send, d2d_recv, sm_send, sm_recv, loc_sems = refs[2 * n + 2:]
        x, y, c = lax.axis_index("x"), lax.axis_index("y"), lax.axis_index("c")
        me = 2 * x + y
        me8 = 4 * x + 2 * y + c
        sibling = (x, y, 1 - c)
        peers = [(1 - x, y), (x, 1 - y), (1 - x, 1 - y)]
        flips = [(fx, fy, fc) for fx in (0, 1) for fy in (0, 1) for fc in (0, 1)][1:]

        def rows(a, which):
            return pl.ds(pl.multiple_of(which * halves[a], 16), halves[a])

        def ici(a, j, src_slot, dst_slot):
            px, py = peers[j]
            return pltpu.make_async_remote_copy(
                src_ref=g_in[a].at[src_slot], dst_ref=parts[a].at[dst_slot, rows(a, c)],
                send_sem=ici_send.at[3 * a + j], recv_sem=ici_recv.at[3 * a + j],
                device_id=(px, py, c), device_id_type=MESH)

        def d2d(a, rel, chip, which):
            piece = parts[a].at[chip, rows(a, which)]
            return pltpu.make_async_remote_copy(
                src_ref=piece, dst_ref=piece,
                send_sem=d2d_send.at[4 * a + rel], recv_sem=d2d_recv.at[4 * a + rel],
                device_id=sibling, device_id_type=MESH)

        def flipped(r):
            fx, fy, fc = flips[r]
            return ((1 - x) if fx else x, (1 - y) if fy else y, (1 - c) if fc else c)

        def sm(r, slot):
            return pltpu.make_async_remote_copy(
                src_ref=small_in, dst_ref=packs.at[slot],
                send_sem=sm_send.at[r], recv_sem=sm_recv.at[r],
                device_id=flipped(r), device_id_type=MESH)

        def peer8(r):
            px, py, pc = flipped(r)
            return 4 * px + 2 * py + pc

        local = [pltpu.make_async_copy(g_in[a].at[me], parts[a].at[me, rows(a, c)], loc_sems.at[a])
                 for a in range(n)]
        local.append(pltpu.make_async_copy(small_in, packs.at[me8], loc_sems.at[n]))
        for cp in local:
            cp.start()
        sends = []
        for r in range(7):
            sends.append(sm(r, me8))
        for a in range(n):
            for j in range(3):
                px, py = peers[j]
                sends.append(ici(a, j, 2 * px + py, me))
        for cp in sends:
            cp.start()
        for a in range(n):
            local[a].wait()
            cp = d2d(a, 0, me, c)
            cp.start()
            sends.append(cp)
        for a in range(n):
            for j in range(3):
                px, py = peers[j]
                ici(a, j, me, 2 * px + py).wait_recv()
                cp = d2d(a, 1 + j, 2 * px + py, c)
                cp.start()
                sends.append(cp)
        for a in range(n):
            d2d(a, 0, me, 1 - c).wait_recv()
            for j in range(3):
                px, py = peers[j]
                d2d(a, 1 + j, 2 * px + py, 1 - c).wait_recv()
        for r in range(7):
            sm(r, peer8(r)).wait_recv()
        for cp in sends:
            cp.wait_send()
        local[n].wait()

    out_shape = ([jax.ShapeDtypeStruct((4, 2 * h.shape[1], h.shape[2]), h.dtype) for h in chip_halves]
                 + [jax.ShapeDtypeStruct((8,) + small.shape, small.dtype)])
    res = pl.pallas_call(
        body, name="exchange_grads",
        out_shape=out_shape,
        in_specs=[_hbm_spec() for _ in range(n + 1)],
        out_specs=[_hbm_spec() for _ in range(n + 1)],
        scratch_shapes=[pltpu.SemaphoreType.DMA((3 * n,)), pltpu.SemaphoreType.DMA((3 * n,)),
                        pltpu.SemaphoreType.DMA((4 * n,)), pltpu.SemaphoreType.DMA((4 * n,)),
                        pltpu.SemaphoreType.DMA((7,)), pltpu.SemaphoreType.DMA((7,)),
                        pltpu.SemaphoreType.DMA((n + 1,))],
    )(*chip_halves, small)
    return res[:n], res[n]


def _ada_call(c_all, w_ada_cols, b_ada_cols):
    def body(c_ref, w_ref, b_ref, o_ref):
        cc = c_ref[...]
        o_ref[...] = _dot((cc * _sigmoid(cc)).astype(BF16), w_ref[...].astype(BF16)) + b_ref[...]

    return pl.pallas_call(
        body, name="ada_fwd",
        out_shape=jax.ShapeDtypeStruct((c_all.shape[0], w_ada_cols.shape[1]), F32),
        compiler_params=pltpu.CompilerParams(vmem_limit_bytes=VMEM_LIMIT),
    )(c_all, w_ada_cols, b_ada_cols)


def _ada_part(j):
    return pl.BlockSpec((1, D_MODEL), lambda i: (0, j))


def _full(shape):
    return pl.BlockSpec(shape, lambda i: (0,) * len(shape))


def _rows(tm, width):
    return pl.BlockSpec((tm, width), lambda i: (i, 0))


def _rope(t, cos_t, sin_t):
    return t * cos_t + pltpu.roll(t, HALF_LANES, 1) * sin_t


def _rope_adjoint(d, cos_t, sin_t):
    return d * cos_t + pltpu.roll(d * sin_t, HALF_LANES, 1)


def _pre_call(x, ada, norm_w, w_in_bf, cos_t, sin_t, q_lora_norm, kv_lora_norm, qhn_pad, khn_pad,
              w_uq_bf, w_uk_bf, w_uv_bf):
    s = x.shape[0]
    tm = _row_tile(s, 512)
    out_defs = [(D_MODEL, BF16), (512, BF16), (512, BF16), (512, BF16), (512, F32),
                (Q_LORA_RANK, F32), (KV_LORA_RANK, F32), (512, F32),
                (MLA_PAD_WIDTH, BF16), (MLA_PAD_WIDTH, BF16), (MLA_WIDTH, BF16),
                (Q_LORA_RANK, BF16), (KV_LORA_RANK, BF16), (MLA_PAD_WIDTH, F32), (MLA_PAD_WIDTH, F32)]

    def body(x_ref, sh_ref, sc_ref, nw_ref, w_ref, cos_ref, sin_ref, qln_ref, kvln_ref, qhn_ref, khn_ref,
             wuq_ref, wuk_ref, wuv_ref,
             hb_ref, qsb_ref, ksb_ref, vsb_ref, gsb_ref, cq_ref, ckv_ref, gm_ref,
             q_ref, k_ref, v_ref, cqn_ref, ckvn_ref, q0_ref, k0_ref):
        xx = x_ref[...]
        r0 = lax.rsqrt(jnp.mean(xx * xx, axis=-1, keepdims=True) + EPS)
        hb = ((xx * r0 * nw_ref[...]) * (1.0 + sc_ref[...]) + sh_ref[...]).astype(BF16)
        hb_ref[...] = hb

        def proj(c0, width):
            return _dot(hb, w_ref[:, c0:c0 + width])

        cq = proj(C_CQ, Q_LORA_RANK)
        ckv = proj(C_CKV, KV_LORA_RANK)
        kr = proj(C_KR, LANES)
        cq_ref[...] = cq
        ckv_ref[...] = ckv
        cqn = (cq * lax.rsqrt(jnp.mean(cq * cq, axis=-1, keepdims=True) + EPS) * qln_ref[...]).astype(BF16)
        cqn_ref[...] = cqn
        ckvn = (ckv * lax.rsqrt(jnp.mean(ckv * ckv, axis=-1, keepdims=True) + EPS) * kvln_ref[...]).astype(BF16)
        ckvn_ref[...] = ckvn
        qsb_ref[...] = proj(C_Q, 512).astype(BF16)
        v_ref[...] = _dot(ckvn, wuv_ref[...]).astype(BF16)
        q0_ref[...] = _dot(cqn, wuq_ref[...])
        k0_ref[...] = _dot(ckvn, wuk_ref[...])
        ksb_ref[...] = proj(C_K, 512).astype(BF16)
        cos_t, sin_t = cos_ref[...], sin_ref[...]
        heads = [slice(h * HEAD_PAD, (h + 1) * HEAD_PAD) for h in range(MLA_HEADS)]
        for cols in heads:
            k0_ref[:, cols] = k0_ref[:, cols] + kr

        def inv_rms(ref):
            sums = [jnp.sum(ref[:, cols] * ref[:, cols], axis=-1, keepdims=True) for cols in heads]
            return [lax.rsqrt(t * (1.0 / MLA_QK_DIM) + EPS) for t in sums]

        rqs = inv_rms(q0_ref)
        vsb_ref[...] = proj(C_V, 512).astype(BF16)
        rks = inv_rms(k0_ref)
        gsb_ref[...] = proj(C_G, 512)
        for cols, rq, rk in zip(heads, rqs, rks):
            q_ref[:, cols] = (_rope(q0_ref[:, cols] * rq * qhn_ref[...], cos_t, sin_t) * MLA_SCALE_LOG2).astype(BF16)
            k_ref[:, cols] = _rope(k0_ref[:, cols] * rk * khn_ref[...], cos_t, sin_t).astype(BF16)
        gm_ref[...] = proj(C_GM, 512)

    return pl.pallas_call(
        body, name="pre_proj",
        grid=(s // tm,),
        in_specs=[_rows(tm, D_MODEL), _ada_part(0), _ada_part(1), _full((1, D_MODEL)),
                  pl.BlockSpec((D_MODEL, IN_COLS_PAD), lambda i: (0, 0), pipeline_mode=pl.Buffered(1)),
                  _rows(tm, LANES), _rows(tm, LANES),
                  _full((1, Q_LORA_RANK)), _full((1, KV_LORA_RANK)), _full((1, LANES)), _full((1, LANES)),
                  _full((Q_LORA_RANK, MLA_PAD_WIDTH)), _full((KV_LORA_RANK, MLA_PAD_WIDTH)),
                  _full((KV_LORA_RANK, MLA_WIDTH))],
        out_specs=[_rows(tm, w) for w, _ in out_defs],
        out_shape=[jax.ShapeDtypeStruct((s, w), dt) for w, dt in out_defs],
        compiler_params=_params(("arbitrary",), VMEM_LIMIT),
    )(x, ada, ada, norm_w, w_in_bf, cos_t, sin_t, q_lora_norm, kv_lora_norm, qhn_pad, khn_pad,
      w_uq_bf, w_uk_bf, w_uv_bf)


def _log_sigmoid_pair(z):
    ls = jnp.minimum(z, 0.0) - jnp.log(1.0 + jnp.exp(-jnp.abs(z)))
    return ls, ls - z


def _pair(hh):
    return slice((hh // 2) * LANES, (hh // 2 + 1) * LANES)


def _sb_fwd_call(q, k, v):
    s = q.shape[0]
    bq = _row_tile(s, 256)
    nq = s // bq
    nh = SB_GROUP
    width = nh * HALF_LANES

    def body(q_ref, k_ref, v_ref, o_ref, r_ref, ks_ref):
        hp, i = pl.program_id(0), pl.program_id(1)
        lane = lax.broadcasted_iota(I32, (bq, LANES), 1)
        row = lax.broadcasted_iota(I32, (bq, bq), 0)
        col = lax.broadcasted_iota(I32, (bq, bq), 1)
        strict = col < row
        later = jnp.where(row > col, 1.0, 0.0).astype(BF16)
        masks = [_head_mask(lane, hh).astype(BF16) for hh in range(2)]
        qms = [q_ref[:, _pair(hh)] * jnp.asarray(SB_SCALE, BF16) * masks[hh % 2] for hh in range(nh)]

        def walk(blocks, state):
            chains = [(kb, diagonal, hh) for kb, diagonal in blocks for hh in range(nh)]
            keys = lambda kb: pl.ds(pl.multiple_of(kb * bq, bq), bq)
            zs = [_dot_nt(qms[hh], k_ref[keys(kb), _pair(hh)]) for kb, _, hh in chains]
            pairs = []
            for z, (_, diagonal, _) in zip(zs, chains):
                ls, lk = _log_sigmoid_pair(z)
                pairs.append((ls, jnp.where(strict, lk, 0.0) if diagonal else lk))
            sums = [_split_dot(lk, later) for _, lk in pairs]
            runs = [st[0] for st in state]
            ws = []
            for (ls, lk), after, (_, diagonal, hh) in zip(pairs, sums, chains):
                w = jnp.exp(ls + (after + runs[hh]))
                ws.append((jnp.where(strict, w, 0.0) if diagonal else w).astype(BF16))
                runs[hh] = runs[hh] + jnp.sum(lk, axis=1, keepdims=True)
            accs = [st[1] for st in state]
            for w, (kb, _, hh) in zip(ws, chains):
                accs[hh] = accs[hh] + _dot(w, v_ref[keys(kb), _pair(hh)])
            return tuple(zip(runs, accs))

        def alive(state):
            top = jnp.max(state[0][0])
            for st in state[1:]:
                top = jnp.maximum(top, jnp.max(st[0]))
            return (top > SB_DEAD).astype(I32)

        def finish(state, first):
            ks_ref[hp, i] = first
            for pair in range(nh // 2):
                o_ref[:, _pair(2 * pair)] = jnp.where(lane < HALF_LANES, state[2 * pair][1], state[2 * pair + 1][1])
                r_ref[:, _pair(2 * pair)] = jnp.where(lane < HALF_LANES, state[2 * pair][0], state[2 * pair + 1][0])

        zero = ((jnp.zeros((bq, 1), F32), jnp.zeros((bq, LANES), F32)),) * nh

        @pl.when(i == 0)
        def _():
            finish(walk([(0, True)], zero), 0)

        @pl.when(i > 0)
        def _():
            state = walk([(i, True), (i - 1, False)], zero)

            def cond(carry):
                return jnp.logical_and(carry[0] >= 0, carry[1] > 0)

            def step(carry):
                state = walk([(carry[0], False)], carry[2])
                return carry[0] - 1, alive(state), state

            kb, _, state = lax.while_loop(cond, step, (i - 2, alive(state), state))
            finish(state, kb + 1)

    return pl.pallas_call(
        body, name="sb_fwd",
        grid=(SB_HEADS // nh, nq),
        in_specs=[pl.BlockSpec((bq, width), lambda h, i: (i, h)),
                  pl.BlockSpec((s, width), lambda h, i: (0, h)),
                  pl.BlockSpec((s, width), lambda h, i: (0, h))],
        out_specs=[pl.BlockSpec((bq, width), lambda h, i: (i, h)),
                   pl.BlockSpec((bq, width), lambda h, i: (i, h)),
                   pl.BlockSpec(memory_space=pltpu.SMEM)],
        out_shape=[jax.ShapeDtypeStruct((s, SB_WIDTH), F32), jax.ShapeDtypeStruct((s, SB_WIDTH), F32),
                   jax.ShapeDtypeStruct((SB_HEADS // nh, nq), I32)],
        compiler_params=_params(("arbitrary", "arbitrary"), VMEM_LIMIT),
    )(q, k, v)


def _mla_fwd_call(q, k, v):
    s = q.shape[0]
    bq = _row_tile(s, MLA_BQ)
    bk = _row_tile(s, MLA_BK)
    nq = s // bq
    assert bk % bq == 0
    step = min(bk // 2, MLA_STEP)
    nsub = bk // step
    assert nsub % 2 == 0

    def body(q_ref, k_ref, v_ref, o_ref, lse_ref, p_ref, s_ref):
        i = pl.program_id(1)
        lane = lax.broadcasted_iota(I32, (bq, LANES), 1)
        row = lax.broadcasted_iota(I32, (step, bq), 1)
        col = lax.broadcasted_iota(I32, (step, bq), 0)
        n_full = (i * bq) // bk

        def keys(g):
            return pl.ds(pl.multiple_of(g * step, step), step)

        def join(left, right, qlo):
            return right if qlo == 0 else jnp.concatenate([left[:, :qlo], right], axis=1)

        def put_scores(g, slot, qlo=0):
            for hh in range(2):
                cols = slice(hh * HEAD_PAD, (hh + 1) * HEAD_PAD)
                s_ref[slot, hh, :, qlo:] = _dot_nt(k_ref[keys(g), cols], q_ref[qlo:, cols])

        def add_pv(carry, g, slot, qlo=0):
            vblk = v_ref[keys(g), :]
            out = []
            for hh, (m, l, acc, alpha) in enumerate(carry):
                upd = alpha[:, qlo:] * acc[:, qlo:] + _dot_tn(vblk, p_ref[slot, hh, :, qlo:])
                out.append((m, l, join(acc, upd, qlo), alpha))
            return tuple(out)

        def substep(g, slot, carry, masked, prefetch, qlo=0, next_qlo=0, prev_qlo=0, first=False):
            if prefetch:
                put_scores(g + 1, 1 - slot, next_qlo)
            if not first:
                carry = add_pv(carry, g - 1, 1 - slot, prev_qlo)
            new = []
            for hh in range(2):
                m, l, acc, _ = carry[hh]
                sc = s_ref[slot, hh, :, qlo:]
                if masked:
                    sc = jnp.where(col[:, qlo:] + g * step <= row[:, qlo:] + i * bq, sc, MASK_NEG)
                m_new = jnp.maximum(m[:, qlo:], jnp.max(sc, axis=0, keepdims=True))
                p = jnp.exp2(sc - m_new)
                alpha = jnp.exp2(m[:, qlo:] - m_new)
                l_new = alpha * l[:, qlo:] + jnp.sum(p, axis=0, keepdims=True)
                p_ref[slot, hh, :, qlo:] = p.astype(BF16)
                new.append((join(m, m_new, qlo), join(l, l_new, qlo), acc, join(jnp.ones_like(m), alpha, qlo)))
            return tuple(new)

        def first_query(t, masked):
            return t * step if (masked and bk == bq and 0 <= t < nsub) else 0

        def chunk(kb, carry, masked, first=False):
            for t in range(nsub):
                last = masked and t == nsub - 1
                carry = substep(nsub * kb + t, t % 2, carry, masked, not last, first_query(t, masked),
                                first_query(t + 1, masked), first_query(t - 1, masked), first and t == 0)
            return carry

        def finish(carry):
            (m0, l0, a0, _), (m1, l1, a1, _) = add_pv(carry, nsub * n_full + nsub - 1, 1,
                                                      first_query(nsub - 1, True))
            o_ref[...] = jnp.where(lane < HALF_LANES, (a0 / l0).T, (a1 / l1).T)
            sub = lax.broadcasted_iota(I32, (8, bq), 0)
            lse_ref[...] = jnp.where(sub == 0, m0 + jnp.log2(l0), jnp.where(sub == 1, m1 + jnp.log2(l1), 0.0))

        put_scores(0, 0)
        one = (jnp.full((1, bq), MASK_NEG, F32), jnp.zeros((1, bq), F32), jnp.zeros((LANES, bq), F32),
               jnp.ones((1, bq), F32))

        @pl.when(n_full == 0)
        def _():
            finish(chunk(0, (one, one), True, first=True))

        @pl.when(n_full > 0)
        def _():
            carry = chunk(0, (one, one), False, first=True)
            carry = lax.fori_loop(1, n_full, lambda kb, cr: chunk(kb, cr, False), carry)
            finish(chunk(n_full, carry, True))

    return pl.pallas_call(
        body, name="mla_fwd",
        grid=(4, nq),
        in_specs=[pl.BlockSpec((bq, 2 * HEAD_PAD), lambda h, i: (i, h)),
                  pl.BlockSpec((s, 2 * HEAD_PAD), lambda h, i: (0, h)),
                  pl.BlockSpec((s, LANES), lambda h, i: (0, h))],
        out_specs=[pl.BlockSpec((bq, LANES), lambda h, i: (i, h)),
                   pl.BlockSpec((None, 8, bq), lambda h, i: (h, 0, i))],
        out_shape=[jax.ShapeDtypeStruct((s, MLA_WIDTH), F32), jax.ShapeDtypeStruct((4, 8, s), F32)],
        scratch_shapes=[pltpu.VMEM((2, 2, step, bq), BF16), pltpu.VMEM((2, 2, step, bq), F32)],
        compiler_params=_params(("arbitrary", "arbitrary"), VMEM_LIMIT),
    )(q, k, v)


def _out_call(o_sb, g_sb, o_mla, g_mla, x, target, ada, w_out_bf):
    s = x.shape[0]
    tm = _row_tile(s, 512)

    def body(osb_ref, gsb_ref, oml_ref, gml_ref, x_ref, t_ref, gate_ref, w_ref,
             dosb_ref, doml_ref, dgsb_ref, dgml_ref, dy_ref, gw_ref, dgate_ref, sq_ref):
        @pl.when(pl.program_id(0) == 0)
        def _():
            gw_ref[...] = jnp.zeros_like(gw_ref)
            dgate_ref[...] = jnp.zeros_like(dgate_ref)
            sq_ref[...] = jnp.zeros_like(sq_ref)

        g_s, g_m = gsb_ref[...], gml_ref[...]
        sig_s, sig_m = _sigmoid(g_s), _sigmoid(g_m)
        silu_s, silu_m = g_s * sig_s, g_m * sig_m
        o_s, o_m = osb_ref[...], oml_ref[...]
        mixed = jnp.concatenate([o_s * silu_s, o_m * silu_m], axis=1).astype(BF16)
        u = _dot(mixed, w_ref[...])
        gate_v = gate_ref[...]
        err = x_ref[...] + gate_v * u - t_ref[...]
        sq_ref[...] += jnp.sum(err * err, axis=0, keepdims=True)
        dy = err * (1.0 / D_MODEL)
        dy_ref[...] = dy
        dgate_ref[...] += jnp.sum(dy * u, axis=0, keepdims=True)
        du = (dy * gate_v).astype(BF16)
        gw_ref[...] += _dot_tn(mixed, du)
        dmix = _dot_nt(du, w_ref[...])
        dm_s, dm_m = dmix[:, :SB_WIDTH], dmix[:, SB_WIDTH:]
        dosb_ref[...] = (dm_s * silu_s).astype(BF16)
        doml_ref[...] = (dm_m * silu_m).astype(BF16)
        dgsb_ref[...] = (dm_s * o_s * (sig_s * (1.0 + g_s * (1.0 - sig_s)))).astype(BF16)
        dgml_ref[...] = (dm_m * o_m * (sig_m * (1.0 + g_m * (1.0 - sig_m)))).astype(BF16)

    return pl.pallas_call(
        body, name="out_proj_loss",
        grid=(s // tm,),
        in_specs=[_rows(tm, 512), _rows(tm, 512), _rows(tm, 512), _rows(tm, 512),
                  _rows(tm, D_MODEL), _rows(tm, D_MODEL), _ada_part(2), _full((D_MODEL, D_MODEL))],
        out_specs=[_rows(tm, 512), _rows(tm, 512), _rows(tm, 512), _rows(tm, 512), _rows(tm, D_MODEL),
                   _full((D_MODEL, D_MODEL)), _full((1, D_MODEL)), _full((1, D_MODEL))],
        out_shape=[jax.ShapeDtypeStruct((s, 512), BF16)] * 4
        + [jax.ShapeDtypeStruct((s, D_MODEL), F32), jax.ShapeDtypeStruct((D_MODEL, D_MODEL), F32),
           jax.ShapeDtypeStruct((1, D_MODEL), F32), jax.ShapeDtypeStruct((1, D_MODEL), F32)],
        compiler_params=_params(("arbitrary",), VMEM_LIMIT),
    )(o_sb, g_sb, o_mla, g_mla, x, target, ada, w_out_bf)


def _head_mask(lane, hh):
    return jnp.where((lane >= HALF_LANES) if hh else (lane < HALF_LANES), 1.0, 0.0)


def _pick_lane(packed, lane, which):
    return jnp.sum(jnp.where(lane == which, packed, 0.0), axis=1, keepdims=True)


def _sb_bwd_call(kstart, q, k, v, do, rfin):
    s = q.shape[0]
    bq = _row_tile(s, 256)
    nq = s // bq
    nh = SB_GROUP
    width = nh * HALF_LANES

    def body(ks_ref, q_ref, k_ref, v_ref, do_ref, r_ref, dq_ref, dk_ref, dv_ref):
        hp, i = pl.program_id(0), pl.program_id(1)

        @pl.when(i == 0)
        def _():
            dk_ref[...] = jnp.zeros_like(dk_ref)
            dv_ref[...] = jnp.zeros_like(dv_ref)

        lane = lax.broadcasted_iota(I32, (bq, LANES), 1)
        row = lax.broadcasted_iota(I32, (bq, bq), 0)
        col = lax.broadcasted_iota(I32, (bq, bq), 1)
        upto = jnp.where(row <= col, 1.0, 0.0).astype(BF16)
        before = jnp.where(row < col, 1.0, 0.0).astype(BF16)
        masks = [_head_mask(lane, hh).astype(BF16) for hh in range(2)]
        qms = [q_ref[:, _pair(hh)] * jnp.asarray(SB_SCALE, BF16) * masks[hh % 2] for hh in range(nh)]
        doms = [do_ref[:, _pair(hh)] * masks[hh % 2] for hh in range(nh)]
        totals = [_pick_lane(r_ref[:, _pair(hh)], lane, HALF_LANES * (hh % 2)) for hh in range(nh)]
        strict = col < row

        def walk(blocks, state):
            chains = [(kb, diagonal, hh) for kb, diagonal in blocks for hh in range(nh)]
            keys = lambda kb: pl.ds(pl.multiple_of(kb * bq, bq), bq)
            cut = lambda x, diagonal: jnp.where(strict, x, 0.0) if diagonal else x
            zs = [_dot_nt(qms[hh], k_ref[keys(kb), _pair(hh)]) for kb, _, hh in chains]
            dws = [_dot_nt(doms[hh], v_ref[keys(kb), _pair(hh)]) for kb, _, hh in chains]
            pairs = []
            for z, (_, diagonal, _) in zip(zs, chains):
                ls, lk = _log_sigmoid_pair(z)
                pairs.append((ls, cut(lk, diagonal)))
            incls = [_split_dot(lk, upto) for _, lk in pairs]
            pres = [st[0] for st in state]
            ws, gs = [], []
            for (ls, lk), incl, dw, (_, diagonal, hh) in zip(pairs, incls, dws, chains):
                w = cut(jnp.exp(ls + ((totals[hh] - pres[hh]) - incl)), diagonal)
                ws.append(w.astype(BF16))
                gs.append(w * dw)
                pres[hh] = pres[hh] + jnp.sum(lk, axis=1, keepdims=True)
            gsums = [_dot(g.astype(BF16), before) for g in gs]
            gpres = [st[1] for st in state]
            dzs = []
            for (ls, _), g, gsum, (_, diagonal, hh) in zip(pairs, gs, gsums, chains):
                dzs.append(cut(g - jnp.exp(ls) * (g + (gpres[hh] + gsum)), diagonal).astype(BF16))
                gpres[hh] = gpres[hh] + jnp.sum(g, axis=1, keepdims=True)
            dqs = [st[2] for st in state]
            dk_parts, dv_parts = [], []
            for dzb, w, (kb, _, hh) in zip(dzs, ws, chains):
                dk_parts.append(_dot_tn(dzb, qms[hh]))
                dv_parts.append(_dot_tn(w, doms[hh]))
                dqs[hh] = dqs[hh] + _dot(dzb, k_ref[keys(kb), _pair(hh)])
            for b, (kb, _) in enumerate(blocks):
                for pair in range(nh // 2):
                    c0 = b * nh + 2 * pair
                    dk_ref[keys(kb), _pair(2 * pair)] += dk_parts[c0] + dk_parts[c0 + 1]
                    dv_ref[keys(kb), _pair(2 * pair)] += dv_parts[c0] + dv_parts[c0 + 1]
            return tuple(zip(pres, gpres, dqs))

        def finish(state):
            for pair in range(nh // 2):
                both = jnp.where(lane < HALF_LANES, state[2 * pair][2], state[2 * pair + 1][2])
                dq_ref[:, _pair(2 * pair)] = (both * SB_SCALE).astype(BF16)

        zero = ((jnp.zeros((bq, 1), F32), jnp.zeros((bq, 1), F32), jnp.zeros((bq, LANES), F32)),) * nh

        @pl.when(i == 0)
        def _():
            finish(walk([(0, True)], zero))

        @pl.when(i > 0)
        def _():
            state = lax.fori_loop(ks_ref[hp, i], i - 1, lambda kb, st: walk([(kb, False)], st), zero)
            finish(walk([(i - 1, False), (i, True)], state))

    return pl.pallas_call(
        body, name="sb_bwd",
        grid_spec=pltpu.PrefetchScalarGridSpec(
            num_scalar_prefetch=1, grid=(SB_HEADS // nh, nq),
            in_specs=[pl.BlockSpec((bq, width), lambda h, i, ks: (i, h)),
                      pl.BlockSpec((s, width), lambda h, i, ks: (0, h), pipeline_mode=pl.Buffered(1)),
                      pl.BlockSpec((s, width), lambda h, i, ks: (0, h), pipeline_mode=pl.Buffered(1)),
                      pl.BlockSpec((bq, width), lambda h, i, ks: (i, h)),
                      pl.BlockSpec((bq, width), lambda h, i, ks: (i, h))],
            out_specs=[pl.BlockSpec((bq, width), lambda h, i, ks: (i, h)),
                       pl.BlockSpec((s, width), lambda h, i, ks: (0, h), pipeline_mode=pl.Buffered(1)),
                       pl.BlockSpec((s, width), lambda h, i, ks: (0, h), pipeline_mode=pl.Buffered(1))]),
        out_shape=[jax.ShapeDtypeStruct((s, SB_WIDTH), BF16), jax.ShapeDtypeStruct((s, SB_WIDTH), F32),
                   jax.ShapeDtypeStruct((s, SB_WIDTH), F32)],
        compiler_params=_params(("arbitrary", "arbitrary"), VMEM_LIMIT),
    )(kstart, q, k, v, do, rfin)


def _mla_bwd_call(q, k, v, do, o, lse):
    s = q.shape[0]
    bq = _row_tile(s, MLA_BWD_BQ)
    bk = _row_tile(s, MLA_BWD_BK)
    nq = s // bq
    assert bk % bq == 0
    step = min(bk // 2, MLA_BWD_STEP)
    nsub = bk // step
    assert nsub % 2 == 0

    def body(q_ref, k_ref, v_ref, do_ref, o_ref, lse_ref, dq_ref, dk_ref, dv_ref, dom_ref, s_ref, dp_ref, pb_ref,
             ds_ref):
        i = pl.program_id(1)

        @pl.when(i == 0)
        def _():
            dk_ref[...] = jnp.zeros_like(dk_ref)
            dv_ref[...] = jnp.zeros_like(dv_ref)

        lane = lax.broadcasted_iota(I32, (bq, LANES), 1)
        row = lax.broadcasted_iota(I32, (step, bq), 1)
        col = lax.broadcasted_iota(I32, (step, bq), 0)
        n_full = (i * bq) // bk
        do2 = do_ref[...]
        prod = do2.astype(F32) * o_ref[...]
        ones = jnp.ones((8, LANES), BF16)
        deltas, lses = [], []
        for hh in range(2):
            head = _head_mask(lane, hh)
            dom_ref[hh] = do2 * head.astype(BF16)
            part = prod * head
            hi = part.astype(BF16)
            lo = (part - hi.astype(F32)).astype(BF16)
            deltas.append((_dot_nt(ones, hi) + _dot_nt(ones, lo))[0:1])
            lses.append(lse_ref[hh:hh + 1, :])

        def keys(g):
            return pl.ds(pl.multiple_of(g * step, step), step)

        def heads():
            return [(hh, slice(hh * HEAD_PAD, (hh + 1) * HEAD_PAD)) for hh in range(2)]

        def put_products(g, slot, qlo=0):
            vblk = v_ref[keys(g), :]
            for hh, cols in heads():
                s_ref[slot, hh, :, qlo:] = _dot_nt(k_ref[keys(g), cols], q_ref[qlo:, cols])
                dp_ref[slot, hh, :, qlo:] = _dot_nt(vblk, dom_ref[hh, qlo:, :])

        def add_grads(dqs, g, slot, qlo=0):
            rows = keys(g)
            new, dv_parts = [], []
            for hh, cols in heads():
                ds = ds_ref[slot, hh, :, qlo:]
                dk_ref[rows, cols] += _dot(ds, q_ref[qlo:, cols])
                dv_parts.append(_dot(pb_ref[slot, hh, :, qlo:], dom_ref[hh, qlo:, :]))
                upd = dqs[hh][:, qlo:] + _dot_tn(k_ref[rows, cols], ds)
                new.append(upd if qlo == 0 else jnp.concatenate([dqs[hh][:, :qlo], upd], axis=1))
            dv_ref[rows, :] += dv_parts[0] + dv_parts[1]
            return tuple(new)

        def substep(g, slot, dqs, masked, prefetch, qlo=0, next_qlo=0, prev_qlo=0, first=False):
            if prefetch:
                put_products(g + 1, 1 - slot, next_qlo)
            if not first:
                dqs = add_grads(dqs, g - 1, 1 - slot, prev_qlo)
            for hh, _ in heads():
                p = jnp.exp2(s_ref[slot, hh, :, qlo:] - lses[hh][:, qlo:])
                if masked:
                    p = jnp.where(col[:, qlo:] + g * step <= row[:, qlo:] + i * bq, p, 0.0)
                ds_ref[slot, hh, :, qlo:] = (p * (dp_ref[slot, hh, :, qlo:] - deltas[hh][:, qlo:])).astype(BF16)
                pb_ref[slot, hh, :, qlo:] = p.astype(BF16)
            return dqs

        def first_query(t, masked):
            return t * step if (masked and bk == bq and 0 <= t < nsub) else 0

        def chunk(kb, dqs, masked, first=False):
            for t in range(nsub):
                last = masked and t == nsub - 1
                dqs = substep(nsub * kb + t, t % 2, dqs, masked, not last, first_query(t, masked),
                              first_query(t + 1, masked), first_query(t - 1, masked), first and t == 0)
            return dqs

        def finish(dqs):
            dqs = add_grads(dqs, nsub * n_full + nsub - 1, 1, first_query(nsub - 1, True))
            dq_ref[:, :HEAD_PAD] = dqs[0].T * MLA_SCALE
            dq_ref[:, HEAD_PAD:] = dqs[1].T * MLA_SCALE

        put_products(0, 0)
        zero = jnp.zeros((HEAD_PAD, bq), F32)

        @pl.when(n_full == 0)
        def _():
            finish(chunk(0, (zero, zero), True, first=True))

        @pl.when(n_full > 0)
        def _():
            dqs = chunk(0, (zero, zero), False, first=True)
            dqs = lax.fori_loop(1, n_full, lambda kb, dqs: chunk(kb, dqs, False), dqs)
            finish(chunk(n_full, dqs, True))

    return pl.pallas_call(
        body, name="mla_bwd",
        grid=(4, nq),
        in_specs=[pl.BlockSpec((bq, 2 * HEAD_PAD), lambda h, i: (i, h)),
                  pl.BlockSpec((s, 2 * HEAD_PAD), lambda h, i: (0, h)),
                  pl.BlockSpec((s, LANES), lambda h, i: (0, h)),
                  pl.BlockSpec((bq, LANES), lambda h, i: (i, h)),
                  pl.BlockSpec((bq, LANES), lambda h, i: (i, h)),
                  pl.BlockSpec((None, 8, bq), lambda h, i: (h, 0, i))],
        out_specs=[pl.BlockSpec((bq, 2 * HEAD_PAD), lambda h, i: (i, h)),
                   pl.BlockSpec((s, 2 * HEAD_PAD), lambda h, i: (0, h), pipeline_mode=pl.Buffered(1)),
                   pl.BlockSpec((s, LANES), lambda h, i: (0, h), pipeline_mode=pl.Buffered(1))],
        out_shape=[jax.ShapeDtypeStruct((s, MLA_PAD_WIDTH), F32), jax.ShapeDtypeStruct((s, MLA_PAD_WIDTH), F32),
                   jax.ShapeDtypeStruct((s, MLA_WIDTH), F32)],
        scratch_shapes=[pltpu.VMEM((2, bq, LANES), BF16),
                        pltpu.VMEM((2, 2, step, bq), F32), pltpu.VMEM((2, 2, step, bq), F32),
                        pltpu.VMEM((2, 2, step, bq), BF16), pltpu.VMEM((2, 2, step, bq), BF16)],
        compiler_params=_params(("arbitrary", "arbitrary"), VMEM_LIMIT),
    )(q, k, v, do, o, lse)


def _rms_bwd(d_out, inp, r, weight, n):
    normed = inp * r
    gw = d_out * weight
    d_in = r * (gw - normed * (jnp.sum(gw * normed, axis=-1, keepdims=True) * (1.0 / n)))
    return d_in, d_out * normed


def _mla_prep_bwd_call(dq, dk, dv, q0, k0, cqn, ckvn, c_q, c_kv, cos_t, sin_t,
                       q_lora_norm, kv_lora_norm, qhn_pad, khn_pad, w_uq_bf, w_uk_bf, w_uv_bf):
    s = dq.shape[0]
    tm = _row_tile(s, 512)

    def body(dq_ref, dk_ref, dv_ref, q0_ref, k0_ref, cqn_ref, ckvn_ref, cq_ref, ckv_ref,
             cos_ref, sin_ref, qln_ref, kvln_ref, qhn_ref, khn_ref, wuq_ref, wuk_ref, wuv_ref,
             dcq_ref, dckv_ref, dkr_ref, gwuq_ref, gwuk_ref, gwuv_ref, gqln_ref, gkvln_ref, gqhn_ref, gkhn_ref,
             dq0_ref, dk0_ref, tmp_ref):
        @pl.when(pl.program_id(0) == 0)
        def _():
            for ref in (gwuq_ref, gwuk_ref, gwuv_ref, gqln_ref, gkvln_ref, gqhn_ref, gkhn_ref):
                ref[...] = jnp.zeros_like(ref)

        cos_t, sin_t = cos_ref[...], sin_ref[...]
        lane = lax.broadcasted_iota(I32, (tm, LANES), 1)
        rope_lanes = jnp.logical_or(lane < ROPE_HALF,
                                    jnp.logical_and(lane >= HALF_LANES, lane < HALF_LANES + ROPE_HALF))
        heads = [slice(h * HEAD_PAD, (h + 1) * HEAD_PAD) for h in range(MLA_HEADS)]

        def head_norm_bwd(d_ref, x0_ref, w_ref, out_ref, scale):
            w = w_ref[...]
            inv = [lax.rsqrt(jnp.sum(x0_ref[:, cols] * x0_ref[:, cols], axis=-1, keepdims=True)
                             * (1.0 / MLA_QK_DIM) + EPS) for cols in heads]
            for cols in heads:
                tmp_ref[:, cols] = _rope_adjoint(d_ref[:, cols] * scale, cos_t, sin_t)
            dots = [jnp.sum(tmp_ref[:, cols] * w * (x0_ref[:, cols] * r), axis=-1, keepdims=True)
                    for cols, r in zip(heads, inv)]
            g_w = jnp.zeros((1, LANES), F32)
            rope_sum = jnp.zeros((tm, LANES), F32)
            for cols, r, dot in zip(heads, inv, dots):
                normed = x0_ref[:, cols] * r
                d_n = tmp_ref[:, cols]
                d_x0 = r * (d_n * w - normed * (dot * (1.0 / MLA_QK_DIM)))
                out_ref[:, cols] = d_x0.astype(BF16)
                g_w = g_w + jnp.sum(d_n * normed, axis=0, keepdims=True)
                rope_sum = rope_sum + jnp.where(rope_lanes, d_x0, 0.0)
            return g_w, rope_sum

        g_qhn, _ = head_norm_bwd(dq_ref, q0_ref, qhn_ref, dq0_ref, 1.0)
        g_khn, d_kr = head_norm_bwd(dk_ref, k0_ref, khn_ref, dk0_ref, LN2)
        cqn, ckvn = cqn_ref[...], ckvn_ref[...]
        d_q0b, d_k0b, dvb = dq0_ref[...], dk0_ref[...], dv_ref[...].astype(BF16)
        d_cqn = _dot_nt(d_q0b, wuq_ref[...])
        gwuq_ref[...] += _dot_tn(cqn, d_q0b)
        d_ckvn = _dot_nt(d_k0b, wuk_ref[...]) + _dot_nt(dvb, wuv_ref[...])
        gwuk_ref[...] += _dot_tn(ckvn, d_k0b)
        gwuv_ref[...] += _dot_tn(ckvn, dvb)
        gqhn_ref[...] += g_qhn
        gkhn_ref[...] += g_khn
        dkr_ref[...] = d_kr.astype(BF16)
        cq = cq_ref[...]
        rcq = lax.rsqrt(jnp.mean(cq * cq, axis=-1, keepdims=True) + EPS)
        d_cq, gl = _rms_bwd(d_cqn, cq, rcq, qln_ref[...], Q_LORA_RANK)
        dcq_ref[...] = d_cq.astype(BF16)
        gqln_ref[...] += jnp.sum(gl, axis=0, keepdims=True)
        ckv = ckv_ref[...]
        rckv = lax.rsqrt(jnp.mean(ckv * ckv, axis=-1, keepdims=True) + EPS)
        d_ckv, gl = _rms_bwd(d_ckvn, ckv, rckv, kvln_ref[...], KV_LORA_RANK)
        dckv_ref[...] = d_ckv.astype(BF16)
        gkvln_ref[...] += jnp.sum(gl, axis=0, keepdims=True)

    return pl.pallas_call(
        body, name="mla_prep_bwd",
        grid=(s // tm,),
        in_specs=[_rows(tm, MLA_PAD_WIDTH), _rows(tm, MLA_PAD_WIDTH), _rows(tm, MLA_WIDTH),
                  _rows(tm, MLA_PAD_WIDTH), _rows(tm, MLA_PAD_WIDTH),
                  _rows(tm, Q_LORA_RANK), _rows(tm, KV_LORA_RANK), _rows(tm, Q_LORA_RANK), _rows(tm, KV_LORA_RANK),
                  _rows(tm, LANES), _rows(tm, LANES),
                  _full((1, Q_LORA_RANK)), _full((1, KV_LORA_RANK)), _full((1, LANES)), _full((1, LANES)),
                  _full((Q_LORA_RANK, MLA_PAD_WIDTH)), _full((KV_LORA_RANK, MLA_PAD_WIDTH)),
                  _full((KV_LORA_RANK, MLA_WIDTH))],
        out_specs=[_rows(tm, Q_LORA_RANK), _rows(tm, KV_LORA_RANK), _rows(tm, LANES),
                   _full((Q_LORA_RANK, MLA_PAD_WIDTH)), _full((KV_LORA_RANK, MLA_PAD_WIDTH)),
                   _full((KV_LORA_RANK, MLA_WIDTH)),
                   _full((1, Q_LORA_RANK)), _full((1, KV_LORA_RANK)), _full((1, LANES)), _full((1, LANES))],
        out_shape=[jax.ShapeDtypeStruct((s, Q_LORA_RANK), BF16), jax.ShapeDtypeStruct((s, KV_LORA_RANK), BF16),
                   jax.ShapeDtypeStruct((s, LANES), BF16),
                   jax.ShapeDtypeStruct((Q_LORA_RANK, MLA_PAD_WIDTH), F32),
                   jax.ShapeDtypeStruct((KV_LORA_RANK, MLA_PAD_WIDTH), F32),
                   jax.ShapeDtypeStruct((KV_LORA_RANK, MLA_WIDTH), F32),
                   jax.ShapeDtypeStruct((1, Q_LORA_RANK), F32), jax.ShapeDtypeStruct((1, KV_LORA_RANK), F32),
                   jax.ShapeDtypeStruct((1, LANES), F32), jax.ShapeDtypeStruct((1, LANES), F32)],
        scratch_shapes=[pltpu.VMEM((tm, MLA_PAD_WIDTH), BF16), pltpu.VMEM((tm, MLA_PAD_WIDTH), BF16),
                        pltpu.VMEM((tm, MLA_PAD_WIDTH), F32)],
        compiler_params=_params(("arbitrary",), VMEM_LIMIT),
    )(dq, dk, dv, q0, k0, cqn, ckvn, c_q, c_kv, cos_t, sin_t,
      q_lora_norm, kv_lora_norm, qhn_pad, khn_pad, w_uq_bf, w_uk_bf, w_uv_bf)


def _dh_call(pieces, hb, x, dy, ada, norm_w, w_in_bf):
    s = x.shape[0]
    tm = _row_tile(s, 512)
    widths = [p.shape[1] for p in pieces]
    offsets = [sum(widths[:j]) for j in range(len(widths))]
    assert offsets[-1] + widths[-1] == IN_COLS_PAD
    n = len(pieces)

    def body(*refs):
        p_refs = refs[:n]
        (hb_ref, x_ref, dy_ref, sh_ref, sc_ref, nw_ref, w_ref, gx_ref, gw_ref, dsh_ref, dsc_ref, gnw_ref,
         dp_ref, acc_ref) = refs[n:]

        @pl.when(pl.program_id(0) == 0)
        def _():
            acc_ref[...] = jnp.zeros_like(acc_ref)
            dsh_ref[...] = jnp.zeros_like(dsh_ref)
            dsc_ref[...] = jnp.zeros_like(dsc_ref)
            gnw_ref[...] = jnp.zeros_like(gnw_ref)

        for p_ref, c0, width in zip(p_refs, offsets, widths):
            dp_ref[:, c0:c0 + width] = p_ref[...].astype(BF16)
        acc_ref[...] += _dot_tn(hb_ref[...], dp_ref[...])

        @pl.when(pl.program_id(0) == pl.num_programs(0) - 1)
        def _():
            gw_ref[...] = acc_ref[...].astype(BF16)

        dh = _dot_nt(dp_ref[...], w_ref[...])
        xx = x_ref[...]
        r0 = lax.rsqrt(jnp.mean(xx * xx, axis=-1, keepdims=True) + EPS)
        xn = xx * r0
        nw = nw_ref[...]
        dsh_ref[...] += jnp.sum(dh, axis=0, keepdims=True)
        dsc_ref[...] += jnp.sum(dh * (xn * nw), axis=0, keepdims=True)
        dn = dh * (1.0 + sc_ref[...])
        gnw_ref[...] += jnp.sum(dn * xn, axis=0, keepdims=True)
        dxn = dn * nw
        gx_ref[...] = dy_ref[...] + r0 * (dxn - xn * jnp.mean(dxn * xn, axis=-1, keepdims=True))

    return pl.pallas_call(
        body, name="in_proj_bwd",
        grid=(s // tm,),
        in_specs=[_rows(tm, w) for w in widths]
        + [_rows(tm, D_MODEL), _rows(tm, D_MODEL), _rows(tm, D_MODEL),
           _ada_part(0), _ada_part(1), _full((1, D_MODEL)),
           pl.BlockSpec((D_MODEL, IN_COLS_PAD), lambda i: (0, 0), pipeline_mode=pl.Buffered(1))],
        out_specs=[_rows(tm, D_MODEL),
                   pl.BlockSpec((D_MODEL, IN_COLS_PAD), lambda i: (0, 0), pipeline_mode=pl.Buffered(1)),
                   _full((1, D_MODEL)), _full((1, D_MODEL)), _full((1, D_MODEL))],
        out_shape=[jax.ShapeDtypeStruct((s, D_MODEL), F32), jax.ShapeDtypeStruct((D_MODEL, IN_COLS_PAD), BF16),
                   jax.ShapeDtypeStruct((1, D_MODEL), F32), jax.ShapeDtypeStruct((1, D_MODEL), F32),
                   jax.ShapeDtypeStruct((1, D_MODEL), F32)],
        scratch_shapes=[pltpu.VMEM((tm, IN_COLS_PAD), BF16), pltpu.VMEM((D_MODEL, IN_COLS_PAD), F32)],
        compiler_params=_params(("arbitrary",), VMEM_LIMIT),
    )(*pieces, hb, x, dy, ada, ada, norm_w, w_in_bf)


def _adamw(g, w, m, v):
    m = ADAM_B1 * m + (1.0 - ADAM_B1) * g
    v = ADAM_B2 * v + (1.0 - ADAM_B2) * (g * g)
    m_hat = m / (1.0 - ADAM_B1 ** ADAM_STEP)
    v_hat = v / (1.0 - ADAM_B2 ** ADAM_STEP)
    delta = -ADAM_LR * (m_hat / (jnp.sqrt(v_hat) + ADAM_EPS) + ADAM_WD * w)
    return delta, m, v


def _adam_shard_call(name, parts, w, m, v):
    r, c = w.shape
    tr = r if r <= 512 else 256

    def body(p_ref, w_ref, m_ref, v_ref, g_ref, d_ref, nm_ref, nv_ref):
        g = ((p_ref[0].astype(F32) + p_ref[1].astype(F32)) + p_ref[2].astype(F32)) + p_ref[3].astype(F32)
        g_ref[...] = g
        d_ref[...], nm_ref[...], nv_ref[...] = _adamw(g, w_ref[...], m_ref[...], v_ref[...])

    blk = pl.BlockSpec((tr, c), lambda i: (i, 0))
    return pl.pallas_call(
        body, name=name,
        grid=(r // tr,),
        in_specs=[pl.BlockSpec((4, tr, c), lambda i: (0, i, 0)), blk, blk, blk],
        out_specs=[blk] * 4,
        out_shape=[jax.ShapeDtypeStruct((r, c), F32)] * 4,
        compiler_params=_params(("arbitrary",), VMEM_LIMIT),
    )(parts, w, m, v)


def _adam_shard_transposed_call(name, parts, w_t, m_t, v_t):
    c, r = w_t.shape
    assert parts.shape[1:] == (r, c)
    tr = r if r <= 512 else 256
    c_pad = -(-c // LANES) * LANES

    def body(p_ref, w_ref, m_ref, v_ref, g_ref, d_ref, nm_ref, nv_ref, pad_ref):
        @pl.when(pl.program_id(0) == 0)
        def _():
            pad_ref[...] = jnp.zeros_like(pad_ref)

        pad_ref[:, :c] = ((p_ref[0].astype(F32) + p_ref[1].astype(F32)) + p_ref[2].astype(F32)) + p_ref[3].astype(F32)
        g = pad_ref[...].T[:c]
        g_ref[...] = g
        d_ref[...], nm_ref[...], nv_ref[...] = _adamw(g, w_ref[...], m_ref[...], v_ref[...])

    blk = pl.BlockSpec((c, tr), lambda i: (0, i))
    return pl.pallas_call(
        body, name=name,
        grid=(r // tr,),
        in_specs=[pl.BlockSpec((4, tr, c), lambda i: (0, i, 0)), blk, blk, blk],
        out_specs=[blk] * 4,
        out_shape=[jax.ShapeDtypeStruct((c, r), F32)] * 4,
        scratch_shapes=[pltpu.VMEM((tr, c_pad), F32)],
        compiler_params=_params(("arbitrary",), VMEM_LIMIT),
    )(parts, w_t, m_t, v_t)


def _adam_ada_call(c_all, d_all, w, m, v):
    r, c = w.shape
    tr = 256

    def body(c_ref, d_ref, w_ref, m_ref, v_ref, g_ref, dl_ref, nm_ref, nv_ref):
        cc = c_ref[...]
        sc = cc * _sigmoid(cc)
        dd = d_ref[...]
        sc_hi = sc.astype(BF16)
        sc_lo = (sc - sc_hi.astype(F32)).astype(BF16)
        dd_hi = dd.astype(BF16)
        dd_lo = (dd - dd_hi.astype(F32)).astype(BF16)
        g = _dot_tn(sc_hi, dd_hi) + (_dot_tn(sc_hi, dd_lo) + _dot_tn(sc_lo, dd_hi))
        g_ref[...] = g
        dl_ref[...], nm_ref[...], nv_ref[...] = _adamw(g, w_ref[...], m_ref[...], v_ref[...])

    blk = pl.BlockSpec((tr, c), lambda i: (i, 0))
    return pl.pallas_call(
        body, name="adam_w_ada",
        grid=(r // tr,),
        in_specs=[pl.BlockSpec((16, tr), lambda i: (0, i)), pl.BlockSpec((16, c), lambda i: (0, 0)), blk, blk, blk],
        out_specs=[blk] * 4,
        out_shape=[jax.ShapeDtypeStruct((r, c), F32)] * 4,
        compiler_params=_params(("arbitrary",), VMEM_LIMIT),
    )(c_all, d_all, w, m, v)


def _adam_vectors_call(packs, offsets, vectors):
    nv = len(vectors)

    def body(*refs):
        p_ref, ins, outs = refs[0], refs[1:1 + 3 * nv], refs[1 + 3 * nv:]
        for j, off in enumerate(offsets):
            n = ins[3 * j].shape[1]
            span = -(-n // LANES) * LANES
            g = p_ref[0, :, off:off + span]
            for b in range(1, 8):
                g = g + p_ref[b, :, off:off + span]
            g = g[:, :n]
            outs[j][...] = g
            outs[nv + j][...], outs[2 * nv + j][...], outs[3 * nv + j][...] = _adamw(
                g, ins[3 * j][...], ins[3 * j + 1][...], ins[3 * j + 2][...])

    flat = [a for t in vectors for a in t]
    res = pl.pallas_call(
        body, name="adam_vectors",
        out_shape=[jax.ShapeDtypeStruct(t[0].shape, F32) for _ in range(4) for t in vectors],
    )(packs, *flat)
    return [res[k * nv:(k + 1) * nv] for k in range(4)]


ROPE_HALF = MLA_ROPE_DIM // 2
NOPE_A = MLA_NOPE_DIM - ROPE_HALF


def _zeros_like_lanes(t, n):
    return jnp.zeros(t.shape[:-1] + (n,), t.dtype)


def _to_head_lanes(t):
    nope, rope = t[..., :MLA_NOPE_DIM], t[..., MLA_NOPE_DIM:]
    return jnp.concatenate([rope[..., :ROPE_HALF], nope[..., :NOPE_A], rope[..., ROPE_HALF:], nope[..., NOPE_A:],
                            _zeros_like_lanes(t, HEAD_PAD - MLA_QK_DIM)], axis=-1)


def _from_head_lanes(t):
    return jnp.concatenate([t[..., ROPE_HALF:HALF_LANES], t[..., HALF_LANES + ROPE_HALF:MLA_QK_DIM],
                            t[..., :ROPE_HALF], t[..., HALF_LANES:HALF_LANES + ROPE_HALF]], axis=-1)


def _nope_to_head_lanes(t):
    return jnp.concatenate([_zeros_like_lanes(t, ROPE_HALF), t[..., :NOPE_A], _zeros_like_lanes(t, ROPE_HALF),
                            t[..., NOPE_A:], _zeros_like_lanes(t, HEAD_PAD - MLA_QK_DIM)], axis=-1)


def _rope_to_head_lanes(t):
    return jnp.concatenate([t[..., :ROPE_HALF], _zeros_like_lanes(t, HALF_LANES - ROPE_HALF), t[..., ROPE_HALF:],
                            _zeros_like_lanes(t, HALF_LANES - ROPE_HALF)], axis=-1)


def _rope_angles(positions):
    inv_freq = (ROPE_THETA ** (-jnp.arange(0, MLA_ROPE_DIM, 2, dtype=F32) / MLA_ROPE_DIM))[None]
    signed = _rope_to_head_lanes(jnp.concatenate([-inv_freq, inv_freq], axis=1))
    return positions.astype(F32)[:, None] * signed


def _unshard_cols(g):
    return jnp.transpose(g, (1, 0, 2)).reshape(g.shape[1], 4 * g.shape[2])


def _shard_cols(g):
    r, c4 = g.shape
    return jnp.transpose(g.reshape(r, 4, c4 // 4), (1, 0, 2))


def kernel(x, c, positions, w_ada, b_ada, norm_w, w_in, q_lora_norm, w_uq, kv_lora_norm, w_ukv, q_head_norm, k_head_norm, w_out, loss_target, m_w_ada, m_b_ada, m_norm_w, m_w_in, m_q_lora_norm, m_w_uq, m_kv_lora_norm, m_w_ukv, m_q_head_norm, m_k_head_norm, m_w_out, v_w_ada, v_b_ada, v_norm_w, v_w_in, v_q_lora_norm, v_w_uq, v_kv_lora_norm, v_w_ukv, v_q_head_norm, v_k_head_norm, v_w_out):
    chip = 2 * lax.axis_index("x") + lax.axis_index("y")
    me8 = 2 * chip + lax.axis_index("c")
    ada_cols = w_ada.shape[2]
    c_all = _allgather_rows_call(c)[:, 0, :]
    ada_part = _ada_call(c_all, w_ada[0], lax.dynamic_slice_in_dim(b_ada, chip * ada_cols, ada_cols, axis=1))
    ada_g, win_g, wuq_g, wukv_g, wout_g, cos_t, sin_t = _gather_call(
        [ada_part[None]] + [w.astype(BF16) for w in (w_in, w_uq, w_ukv, w_out)], [False, True, True, True, True],
        _rope_angles(positions[0]))
    ada = lax.dynamic_slice_in_dim(ada_g, me8, 1, axis=1).reshape(1, 4 * ada_cols)
    (sq_sum, grad_x, g_w_in, g_w_uq, g_w_ukv, g_w_out, d_ada, g_norm_w, g_qln, g_kvln, g_qhn, g_khn) = _local_step(
        x[0], ada, (cos_t, sin_t), loss_target[0], norm_w, win_g,
        q_lora_norm, _unshard_cols(wuq_g), kv_lora_norm, _unshard_cols(wukv_g), q_head_norm, k_head_norm,
        wout_g.reshape(D_MODEL, D_MODEL))

    grads = [g.astype(BF16) for g in (g_w_in, _shard_cols(g_w_uq), _shard_cols(g_w_ukv),
                                      g_w_out.reshape(4, D_MODEL // 4, D_MODEL))]
    pieces = [d_ada, g_norm_w, g_qln, g_kvln, g_qhn, g_khn, (0.5 * sq_sum / D_MODEL).reshape(1, 1)]
    spans = [-(-p.shape[1] // LANES) * LANES for p in pieces]
    starts = [sum(spans[:j]) for j in range(len(spans))]
    small = jnp.concatenate([jnp.pad(p, ((0, 0), (0, sp - p.shape[1]))) for p, sp in zip(pieces, spans)], axis=1)
    core = lax.axis_index("c").astype(I32).reshape(1)
    chip_halves = _sum_halves_call(core, grads, _swap_halves_call(grads))
    parts, packs = _exchange_call(chip_halves, small)
    loss = jnp.sum(packs[:, 0, starts[-1]])

    names = ["adam_w_in", "adam_w_uq", "adam_w_ukv", "adam_w_out"]
    shard_w = [(w_in, m_w_in, v_w_in), (w_uq, m_w_uq, v_w_uq), (w_ukv, m_w_ukv, v_w_ukv),
               (w_out, m_w_out, v_w_out)]
    res = {}
    for name, p_g, (w, m, v) in zip(names, parts, shard_w):
        if w.shape[2] % LANES:
            res_t = _adam_shard_transposed_call(name, p_g, w[0].T, m[0].T, v[0].T)
            res[name] = [t.T for t in res_t]
        else:
            res[name] = _adam_shard_call(name, p_g, w[0], m[0], v[0])
    d_all = lax.dynamic_slice_in_dim(packs[:, 0, :], starts[0] + chip * ada_cols, ada_cols, axis=1)
    res_ada = _adam_ada_call(jnp.pad(c_all, ((0, 8), (0, 0))), jnp.pad(d_all, ((0, 8), (0, 0))),
                             w_ada[0], m_w_ada[0], v_w_ada[0])
    vectors = [(b_ada, m_b_ada, v_b_ada), (norm_w, m_norm_w, v_norm_w), (q_lora_norm, m_q_lora_norm, v_q_lora_norm),
               (kv_lora_norm, m_kv_lora_norm, v_kv_lora_norm), (q_head_norm, m_q_head_norm, v_q_head_norm),
               (k_head_norm, m_k_head_norm, v_k_head_norm)]
    vec_out = _adam_vectors_call(packs, starts[:len(vectors)], vectors)

    def ordered(kind):
        big = lambda name: res[name][kind][None]
        return [res_ada[kind][None], vec_out[kind][0], vec_out[kind][1], big("adam_w_in"), vec_out[kind][2],
                big("adam_w_uq"), vec_out[kind][3], big("adam_w_ukv"), vec_out[kind][4], vec_out[kind][5],
                big("adam_w_out")]

    return (loss, grad_x[None], *ordered(0), *ordered(1), *ordered(2), *ordered(3))


def _local_step(x2, ada, rope_tables, tgt, norm_w, w_in_shards, q_lora_norm, w_uq_full,
                kv_lora_norm, w_ukv_full, q_head_norm, k_head_norm, w_out_full):
    in_shard = w_in_shards.shape[2]
    ckv_tail = C_GM - 3 * in_shard
    assert 0 <= ckv_tail and ckv_tail + MLA_ROPE_DIM + MLA_WIDTH == in_shard
    last = w_in_shards[3]
    w_in_bf = jnp.concatenate(
        [w_in_shards[0], w_in_shards[1], w_in_shards[2], last[:, :ckv_tail], last[:, ckv_tail + MLA_ROPE_DIM:],
         _rope_to_head_lanes(last[:, ckv_tail:ckv_tail + MLA_ROPE_DIM])], axis=1).astype(BF16)
    w_uq_bf = _to_head_lanes(w_uq_full.reshape(Q_LORA_RANK, MLA_HEADS, MLA_QK_DIM)).reshape(
        Q_LORA_RANK, MLA_PAD_WIDTH).astype(BF16)
    w_ukv_heads = w_ukv_full.reshape(KV_LORA_RANK, MLA_HEADS, 2 * MLA_NOPE_DIM)
    w_uk_bf = _nope_to_head_lanes(w_ukv_heads[:, :, :MLA_NOPE_DIM]).reshape(KV_LORA_RANK, MLA_PAD_WIDTH).astype(BF16)
    w_uv_bf = w_ukv_heads[:, :, MLA_NOPE_DIM:].reshape(KV_LORA_RANK, MLA_WIDTH).astype(BF16)
    w_out_bf = w_out_full.astype(BF16)
    qhn_pad, khn_pad = _to_head_lanes(q_head_norm), _to_head_lanes(k_head_norm)
    cos_t, sin_t = rope_tables

    hb, q_sb, k_sb, v_sb, g_sb, c_q, c_kv, g_mla, q_m, k_m, v_m, cqn, ckvn, q0, k0 = _pre_call(
        x2, ada, norm_w, w_in_bf, cos_t, sin_t, q_lora_norm, kv_lora_norm, qhn_pad, khn_pad,
        w_uq_bf, w_uk_bf, w_uv_bf)
    o_sb, r_sb, kstart = _sb_fwd_call(q_sb, k_sb, v_sb)
    o_mla, lse = _mla_fwd_call(q_m, k_m, v_m)
    do_sb, do_mla, dg_sb, dg_mla, dy, g_w_out, d_gate, sq = _out_call(
        o_sb, g_sb, o_mla, g_mla, x2, tgt, ada, w_out_bf)

    dq_sb, dk_sb, dv_sb = _sb_bwd_call(kstart, q_sb, k_sb, v_sb, do_sb, r_sb)
    dq_m, dk_m, dv_m = _mla_bwd_call(q_m, k_m, v_m, do_mla, o_mla, lse)
    (d_cq, d_ckv, d_kr, g_wuq_pad, g_wuk_pad, g_wuv, g_qln, g_kvln, g_qhn, g_khn) = _mla_prep_bwd_call(
        dq_m, dk_m, dv_m, q0, k0, cqn, ckvn, c_q, c_kv, cos_t, sin_t,
        q_lora_norm, kv_lora_norm, qhn_pad, khn_pad, w_uq_bf, w_uk_bf, w_uv_bf)
    grad_x, g_win_pad, d_shift, d_scale, g_norm_w = _dh_call(
        [dq_sb, dk_sb, dv_sb, dg_sb, d_cq, d_ckv, dg_mla, d_kr], hb, x2, dy, ada, norm_w, w_in_bf)

    g_kr = g_win_pad[:, C_KR:]
    g_last = jnp.concatenate([g_win_pad[:, 3 * in_shard:C_GM], g_kr[:, :ROPE_HALF],
                              g_kr[:, HALF_LANES:HALF_LANES + ROPE_HALF], g_win_pad[:, C_GM:C_KR]], axis=1)
    g_w_in = jnp.stack([g_win_pad[:, j * in_shard:(j + 1) * in_shard] for j in range(3)] + [g_last])
    g_w_uq = _from_head_lanes(g_wuq_pad.reshape(Q_LORA_RANK, MLA_HEADS, HEAD_PAD)).reshape(Q_LORA_RANK, -1)
    g_w_ukv = jnp.concatenate(
        [_from_head_lanes(g_wuk_pad.reshape(KV_LORA_RANK, MLA_HEADS, HEAD_PAD))[:, :, :MLA_NOPE_DIM],
         g_wuv.reshape(KV_LORA_RANK, MLA_HEADS, MLA_NOPE_DIM)], axis=2).reshape(KV_LORA_RANK, -1)
    d_ada = jnp.concatenate([d_shift, d_scale, d_gate], axis=1)
    return (jnp.sum(sq), grad_x, g_w_in, g_w_uq, g_w_ukv, g_w_out, d_ada, g_norm_w, g_qln, g_kvln,
            _from_head_lanes(g_qhn), _from_head_lanes(g_khn))
```

```python
import math

import jax
import jax.numpy as jnp
from jax import lax
from jax.experimental import pallas as pl
from jax.experimental.pallas import tpu as pltpu

F32 = jnp.float32
BF16 = jnp.bfloat16
I32 = jnp.int32

D_MODEL = 1024
SB_HEADS = 8
SB_WIDTH = 512
MLA_HEADS = 8
MLA_QK_DIM = 96
MLA_NOPE_DIM = 64
MLA_ROPE_DIM = 32
MLA_WIDTH = 512
Q_LORA_RANK = 384
KV_LORA_RANK = 256
ROPE_THETA = 10000.0
EPS = 1e-6
LANES = 128
HALF_LANES = LANES // 2
HEAD_PAD = 128
MLA_PAD_WIDTH = MLA_HEADS * HEAD_PAD

C_Q, C_K, C_V, C_G = 0, 512, 1024, 1536
C_CQ, C_CKV, C_GM, C_KR = 2048, 2432, 2688, 3200
IN_COLS_PAD = 3328

ADAM_LR = 0.001
ADAM_B1 = 0.9
ADAM_B2 = 0.999
ADAM_EPS = 1e-08
ADAM_WD = 0.01
ADAM_STEP = 10

SB_SCALE = 0.125
SB_GROUP = 4
MLA_SCALE = 1.0 / math.sqrt(MLA_QK_DIM)
LN2 = math.log(2.0)
MLA_SCALE_LOG2 = MLA_SCALE / LN2
MLA_BQ = 1024
MLA_BWD_BQ = 1024
MLA_BK = 1024
MLA_BWD_BK = 1024
MLA_STEP = 512
MLA_BWD_STEP = 256
SB_DEAD = -104.0
MASK_NEG = -1e30

VMEM_LIMIT = 56 * 1024 * 1024
MESH = pl.DeviceIdType.MESH


def _dot(a, b):
    return jnp.dot(a, b, preferred_element_type=F32)


def _dot_nt(a, b):
    return lax.dot_general(a, b, (((1,), (1,)), ((), ())), preferred_element_type=F32)


def _dot_tn(a, b):
    return lax.dot_general(a, b, (((0,), (0,)), ((), ())), preferred_element_type=F32)


def _sigmoid(x):
    return 1.0 / (1.0 + jnp.exp(-x))


def _split_dot(a, m):
    hi = a.astype(BF16)
    lo = (a - hi.astype(F32)).astype(BF16)
    return _dot(hi, m) + _dot(lo, m)


def _params(sem, vmem=None):
    return pltpu.CompilerParams(dimension_semantics=sem, vmem_limit_bytes=vmem)


def _row_tile(s, want):
    return min(want, s)


def _hbm_spec():
    return pl.BlockSpec(memory_space=pltpu.HBM)


def _allgather_rows_call(row):
    def body(in_ref, out_ref, send_sems, recv_sems, loc_sem):
        x, y, c = lax.axis_index("x"), lax.axis_index("y"), lax.axis_index("c")
        flips = [(fx, fy, fc) for fx in (0, 1) for fy in (0, 1) for fc in (0, 1)][1:]

        def peer(r):
            fx, fy, fc = flips[r]
            return ((1 - x) if fx else x, (1 - y) if fy else y, (1 - c) if fc else c)

        def copy(r, slot):
            return pltpu.make_async_remote_copy(
                src_ref=in_ref, dst_ref=out_ref.at[slot], send_sem=send_sems.at[r], recv_sem=recv_sems.at[r],
                device_id=peer(r), device_id_type=MESH)

        local = pltpu.make_async_copy(in_ref, out_ref.at[4 * x + 2 * y + c], loc_sem)
        local.start()
        sends = [copy(r, 4 * x + 2 * y + c) for r in range(7)]
        for cp in sends:
            cp.start()
        for r in range(7):
            px, py, pc = peer(r)
            copy(r, 4 * px + 2 * py + pc).wait_recv()
        for cp in sends:
            cp.wait_send()
        local.wait()

    return pl.pallas_call(
        body, name="gather_rows",
        out_shape=jax.ShapeDtypeStruct((8,) + row.shape, row.dtype),
        in_specs=[_hbm_spec()], out_specs=_hbm_spec(),
        scratch_shapes=[pltpu.SemaphoreType.DMA((7,)), pltpu.SemaphoreType.DMA((7,)), pltpu.SemaphoreType.DMA],
    )(row)


def _gather_call(shards, split, positions, lane_freq):
    n = len(shards)
    halves = [s.shape[1] // 2 for s in shards]
    s_len = positions.shape[0] * LANES
    table_rows = 4 * LANES
    n_chunks = s_len // table_rows
    assert n_chunks * table_rows == s_len and n_chunks >= 2

    def body(*refs):
        ins, pos_ref, frq_ref = refs[:n], refs[n], refs[n + 1]
        outs, cos_out, sin_out = refs[n + 2:2 * n + 2], refs[2 * n + 2], refs[2 * n + 3]
        ici_send, ici_recv, d2d_send, d2d_recv, loc_sems, cos_buf, sin_buf, tab_sems = refs[2 * n + 4:]
        x, y, c = lax.axis_index("x"), lax.axis_index("y"), lax.axis_index("c")
        me = 2 * x + y
        peers = [(1 - x, y), (x, 1 - y), (1 - x, 1 - y)]

        def rows(a, which):
            return pl.ds(pl.multiple_of(which * halves[a], 16), halves[a])

        def ici(a, j, slot):
            px, py = peers[j]
            src, dst = ins[a].at[0], outs[a].at[slot]
            if split[a]:
                src, dst = src.at[rows(a, c)], dst.at[rows(a, c)]
            return pltpu.make_async_remote_copy(
                src_ref=src, dst_ref=dst,
                send_sem=ici_send.at[3 * a + j], recv_sem=ici_recv.at[3 * a + j],
                device_id=(px, py, c), device_id_type=MESH)

        def d2d(a, j, which):
            px, py = peers[j]
            piece = outs[a].at[2 * px + py, rows(a, which)]
            return pltpu.make_async_remote_copy(
                src_ref=piece, dst_ref=piece,
                send_sem=d2d_send.at[3 * a + j], recv_sem=d2d_recv.at[3 * a + j],
                device_id=(x, y, 1 - c), device_id_type=MESH)

        local = [pltpu.make_async_copy(ins[a].at[0], outs[a].at[me], loc_sems.at[a]) for a in range(n)]
        for cp in local:
            cp.start()
        sends = [ici(a, j, me) for a in range(n) for j in range(3)]
        for cp in sends:
            cp.start()

        def table_copies(t, slot):
            rows = pl.ds(pl.multiple_of(t * table_rows, table_rows), table_rows)
            return [pltpu.make_async_copy(buf.at[slot], out.at[rows], tab_sems.at[k, slot])
                    for k, (buf, out) in enumerate(((cos_buf, cos_out), (sin_buf, sin_out)))]

        def tables(t, _):
            slot = t % 2

            @pl.when(t >= 2)
            def _():
                for cp in table_copies(t - 2, slot):
                    cp.wait()

            for b in range(table_rows // LANES):
                lane_pos = pos_ref[pl.ds(t * (table_rows // LANES) + b, 1), :]
                ang = jnp.broadcast_to(lane_pos, (LANES, LANES)).T * frq_ref[...]
                cos_buf[slot, b * LANES:(b + 1) * LANES, :] = jnp.cos(ang)
                sin_buf[slot, b * LANES:(b + 1) * LANES, :] = jnp.sin(ang)
            for cp in table_copies(t, slot):
                cp.start()
            return 0

        lax.fori_loop(0, n_chunks, tables, 0)
        for t in (n_chunks - 2, n_chunks - 1):
            for cp in table_copies(t, t % 2):
                cp.wait()
        for a in range(n):
            for j in range(3):
                px, py = peers[j]
                ici(a, j, 2 * px + py).wait_recv()
                if split[a]:
                    cp = d2d(a, j, c)
                    cp.start()
                    sends.append(cp)
        for a in range(n):
            for j in range(3):
                if split[a]:
                    d2d(a, j, 1 - c).wait_recv()
        for cp in sends:
            cp.wait_send()
        for cp in local:
            cp.wait()

    return pl.pallas_call(
        body, name="gather_weights",
        out_shape=[jax.ShapeDtypeStruct((4,) + s.shape[1:], s.dtype) for s in shards]
        + [jax.ShapeDtypeStruct((s_len, LANES), F32)] * 2,
        in_specs=[_hbm_spec() for _ in shards] + [pl.BlockSpec(memory_space=pltpu.VMEM)] * 2,
        out_specs=[_hbm_spec() for _ in range(n + 2)],
        scratch_shapes=[pltpu.SemaphoreType.DMA((3 * n,)), pltpu.SemaphoreType.DMA((3 * n,)),
                        pltpu.SemaphoreType.DMA((3 * n,)), pltpu.SemaphoreType.DMA((3 * n,)),
                        pltpu.SemaphoreType.DMA((n,)),
                        pltpu.VMEM((2, table_rows, LANES), F32), pltpu.VMEM((2, table_rows, LANES), F32),
                        pltpu.SemaphoreType.DMA((2, 2))],
    )(*shards, positions, lane_freq)


def _swap_halves_call(grads):
    n = len(grads)
    halves = [g.shape[1] // 2 for g in grads]

    def body(*refs):
        g_in, hs, send_sems, recv_sems = refs[:n], refs[n:2 * n], refs[2 * n], refs[2 * n + 1]
        x, y, c = lax.axis_index("x"), lax.axis_index("y"), lax.axis_index("c")
        copies = []
        for a in range(n):
            theirs = pl.ds(pl.multiple_of((1 - c) * halves[a], 16), halves[a])
            copies.append(pltpu.make_async_remote_copy(
                src_ref=g_in[a].at[:, theirs], dst_ref=hs[a], send_sem=send_sems.at[a], recv_sem=recv_sems.at[a],
                device_id=(x, y, 1 - c), device_id_type=MESH))
        for cp in copies:
            cp.start()
        for cp in copies:
            cp.wait()

    return pl.pallas_call(
        body, name="swap_halves",
        out_shape=[jax.ShapeDtypeStruct((4, h, g.shape[2]), g.dtype) for g, h in zip(grads, halves)],
        in_specs=[_hbm_spec() for _ in grads], out_specs=[_hbm_spec() for _ in grads],
        scratch_shapes=[pltpu.SemaphoreType.DMA((n,)), pltpu.SemaphoreType.DMA((n,))],
    )(*grads)


def _sum_halves_call(core, grads, halves):
    n = len(grads)

    def body(core_ref, *refs):
        for g_ref, h_ref, o_ref in zip(refs[:n], refs[n:2 * n], refs[2 * n:]):
            o_ref[...] = (g_ref[...].astype(F32) + h_ref[...].astype(F32)).astype(o_ref.dtype)

    whole = lambda h: pl.BlockSpec(h.shape, lambda i, core_ref: (0, 0, 0))
    return pl.pallas_call(
        body, name="sum_halves",
        grid_spec=pltpu.PrefetchScalarGridSpec(
            num_scalar_prefetch=1, grid=(1,),
            in_specs=[pl.BlockSpec(h.shape, lambda i, core_ref: (0, core_ref[0], 0)) for h in halves]
            + [whole(h) for h in halves],
            out_specs=[whole(h) for h in halves]),
        out_shape=[jax.ShapeDtypeStruct(h.shape, h.dtype) for h in halves],
        compiler_params=_params(("arbitrary",), VMEM_LIMIT),
    )(core, *grads, *halves)


def _exchange_call(chip_halves, small):
    n = len(chip_halves)
    halves = [h.shape[1] for h in chip_halves]

    def body(*refs):
        g_in, small_in = refs[:n], refs[n]
        parts, packs = refs[n + 1:2 * n + 1], refs[2 * n + 1]
        ici_send, ici_recv, d2d_send, d2d_recv, sm_send, sm_recv, loc_sems = refs[2 * n + 2:]
        x, y, c = lax.axis_index("x"), lax.axis_index("y"), lax.axis_index("c")
        me = 2 * x + y
        me8 = 4 * x + 2 * y + c
        sibling = (x, y, 1 - c)
        peers = [(1 - x, y), (x, 1 - y), (1 - x, 1 - y)]
        flips = [(fx, fy, fc) for fx in (0, 1) for fy in (0, 1) for fc in (0, 1)][1:]

        def rows(a, which):
            return pl.ds(pl.multiple_of(which * halves[a], 16), halves[a])

        def ici(a, j, src_slot, dst_slot):
            px, py = peers[j]
            return pltpu.make_async_remote_copy(
                src_ref=g_in[a].at[src_slot], dst_ref=parts[a].at[dst_slot, rows(a, c)],
                send_sem=ici_send.at[3 * a + j], recv_sem=ici_recv.at[3 * a + j],
                device_id=(px, py, c), device_id_type=MESH)

        def d2d(a, rel, chip, which):
            piece = parts[a].at[chip, rows(a, which)]
            return pltpu.make_async_remote_copy(
                src_ref=piece, dst_ref=piece,
                send_sem=d2d_send.at[4 * a + rel], recv_sem=d2d_recv.at[4 * a + rel],
                device_id=sibling, device_id_type=MESH)

        def flipped(r):
            fx, fy, fc = flips[r]
            return ((1 - x) if fx else x, (1 - y) if fy else y, (1 - c) if fc else c)

        def sm(r, slot):
            return pltpu.make_async_remote_copy(
                src_ref=small_in, dst_ref=packs.at[slot],
                send_sem=sm_send.at[r], recv_sem=sm_recv.at[r],
                device_id=flipped(r), device_id_type=MESH)

        def peer8(r):
            px, py, pc = flipped(r)
            return 4 * px + 2 * py + pc

        local = [pltpu.make_async_copy(g_in[a].at[me], parts[a].at[me, rows(a, c)], loc_sems.at[a])
                 for a in range(n)]
        local.append(pltpu.make_async_copy(small_in, packs.at[me8], loc_sems.at[n]))
        for cp in local:
            cp.start()
        sends = []
        for r in range(7):
            sends.append(sm(r, me8))
        for a in range(n):
            for j in range(3):
                px, py = peers[j]
                sends.append(ici(a, j, 2 * px + py, me))
        for cp in sends:
            cp.start()
        for a in range(n):
            local[a].wait()
            cp = d2d(a, 0, me, c)
            cp.start()
            sends.append(cp)
        for a in range(n):
            for j in range(3):
                px, py = peers[j]
                ici(a, j, me, 2 * px + py).wait_recv()
                cp = d2d(a, 1 + j, 2 * px + py, c)
                cp.start()
                sends.append(cp)
        for a in range(n):
            d2d(a, 0, me, 1 - c).wait_recv()
            for j in range(3):
                px, py = peers[j]
                d2d(a, 1 + j, 2 * px + py, 1 - c).wait_recv()
        for r in range(7):
            sm(r, peer8(r)).wait_recv()
        for cp in sends:
            cp.wait_send()
        local[n].wait()

    out_shape = ([jax.ShapeDtypeStruct((4, 2 * h.shape[1], h.shape[2]), h.dtype) for h in chip_halves]
                 + [jax.ShapeDtypeStruct((8,) + small.shape, small.dtype)])
    res = pl.pallas_call(
        body, name="exchange_grads",
        out_shape=out_shape,
        in_specs=[_hbm_spec() for _ in range(n + 1)],
        out_specs=[_hbm_spec() for _ in range(n + 1)],
        scratch_shapes=[pltpu.SemaphoreType.DMA((3 * n,)), pltpu.SemaphoreType.DMA((3 * n,)),
                        pltpu.SemaphoreType.DMA((4 * n,)), pltpu.SemaphoreType.DMA((4 * n,)),
                        pltpu.SemaphoreType.DMA((7,)), pltpu.SemaphoreType.DMA((7,)),
                        pltpu.SemaphoreType.DMA((n + 1,))],
    )(*chip_halves, small)
    return res[:n], res[n]


def _ada_call(c_all, w_ada_cols, b_ada_cols):
    def body(c_ref, w_ref, b_ref, o_ref):
        cc = c_ref[...]
        o_ref[...] = _dot((cc * _sigmoid(cc)).astype(BF16), w_ref[...].astype(BF16)) + b_ref[...]

    return pl.pallas_call(
        body, name="ada_fwd",
        out_shape=jax.ShapeDtypeStruct((c_all.shape[0], w_ada_cols.shape[1]), F32),
        compiler_params=pltpu.CompilerParams(vmem_limit_bytes=VMEM_LIMIT),
    )(c_all, w_ada_cols, b_ada_cols)


def _ada_part(j):
    return pl.BlockSpec((1, D_MODEL), lambda i: (0, j))


def _full(shape):
    return pl.BlockSpec(shape, lambda i: (0,) * len(shape))


def _rows(tm, width):
    return pl.BlockSpec((tm, width), lambda i: (i, 0))


def _rope(t, cos_t, sin_t):
    return t * cos_t + pltpu.roll(t, HALF_LANES, 1) * sin_t


def _rope_adjoint(d, cos_t, sin_t):
    return d * cos_t + pltpu.roll(d * sin_t, HALF_LANES, 1)


def _pre_call(x, ada, norm_w, w_in_bf, cos_t, sin_t, q_lora_norm, kv_lora_norm, qhn_pad, khn_pad,
              w_uq_bf, w_uk_bf, w_uv_bf):
    s = x.shape[0]
    tm = _row_tile(s, 512)
    out_defs = [(D_MODEL, BF16), (512, BF16), (512, BF16), (512, BF16), (512, F32),
                (Q_LORA_RANK, F32), (KV_LORA_RANK, F32), (512, F32),
                (MLA_PAD_WIDTH, BF16), (MLA_PAD_WIDTH, BF16), (MLA_WIDTH, BF16),
                (Q_LORA_RANK, BF16), (KV_LORA_RANK, BF16), (MLA_PAD_WIDTH, F32), (MLA_PAD_WIDTH, F32)]

    def body(x_ref, sh_ref, sc_ref, nw_ref, w_ref, cos_ref, sin_ref, qln_ref, kvln_ref, qhn_ref, khn_ref,
             wuq_ref, wuk_ref, wuv_ref,
             hb_ref, qsb_ref, ksb_ref, vsb_ref, gsb_ref, cq_ref, ckv_ref, gm_ref,
             q_ref, k_ref, v_ref, cqn_ref, ckvn_ref, q0_ref, k0_ref):
        xx = x_ref[...]
        r0 = lax.rsqrt(jnp.mean(xx * xx, axis=-1, keepdims=True) + EPS)
        hb = ((xx * r0 * nw_ref[...]) * (1.0 + sc_ref[...]) + sh_ref[...]).astype(BF16)
        hb_ref[...] = hb

        def proj(c0, width):
            return _dot(hb, w_ref[:, c0:c0 + width])

        cq = proj(C_CQ, Q_LORA_RANK)
        ckv = proj(C_CKV, KV_LORA_RANK)
        kr = proj(C_KR, LANES)
        cq_ref[...] = cq
        ckv_ref[...] = ckv
        cqn = (cq * lax.rsqrt(jnp.mean(cq * cq, axis=-1, keepdims=True) + EPS) * qln_ref[...]).astype(BF16)
        cqn_ref[...] = cqn
        ckvn = (ckv * lax.rsqrt(jnp.mean(ckv * ckv, axis=-1, keepdims=True) + EPS) * kvln_ref[...]).astype(BF16)
        ckvn_ref[...] = ckvn
        qsb_ref[...] = proj(C_Q, 512).astype(BF16)
        v_ref[...] = _dot(ckvn, wuv_ref[...]).astype(BF16)
        q0_ref[...] = _dot(cqn, wuq_ref[...])
        k0_ref[...] = _dot(ckvn, wuk_ref[...])
        ksb_ref[...] = proj(C_K, 512).astype(BF16)
        cos_t, sin_t = cos_ref[...], sin_ref[...]
        heads = [slice(h * HEAD_PAD, (h + 1) * HEAD_PAD) for h in range(MLA_HEADS)]
        for cols in heads:
            k0_ref[:, cols] = k0_ref[:, cols] + kr

        def inv_rms(ref):
            sums = [jnp.sum(ref[:, cols] * ref[:, cols], axis=-1, keepdims=True) for cols in heads]
            return [lax.rsqrt(t * (1.0 / MLA_QK_DIM) + EPS) for t in sums]

        rqs = inv_rms(q0_ref)
        vsb_ref[...] = proj(C_V, 512).astype(BF16)
        rks = inv_rms(k0_ref)
        gsb_ref[...] = proj(C_G, 512)
        for cols, rq, rk in zip(heads, rqs, rks):
            q_ref[:, cols] = (_rope(q0_ref[:, cols] * rq * qhn_ref[...], cos_t, sin_t) * MLA_SCALE_LOG2).astype(BF16)
            k_ref[:, cols] = _rope(k0_ref[:, cols] * rk * khn_ref[...], cos_t, sin_t).astype(BF16)
        gm_ref[...] = proj(C_GM, 512)

    return pl.pallas_call(
        body, name="pre_proj",
        grid=(s // tm,),
        in_specs=[_rows(tm, D_MODEL), _ada_part(0), _ada_part(1), _full((1, D_MODEL)),
                  pl.BlockSpec((D_MODEL, IN_COLS_PAD), lambda i: (0, 0), pipeline_mode=pl.Buffered(1)),
                  _rows(tm, LANES), _rows(tm, LANES),
                  _full((1, Q_LORA_RANK)), _full((1, KV_LORA_RANK)), _full((1, LANES)), _full((1, LANES)),
                  _full((Q_LORA_RANK, MLA_PAD_WIDTH)), _full((KV_LORA_RANK, MLA_PAD_WIDTH)),
                  _full((KV_LORA_RANK, MLA_WIDTH))],
        out_specs=[_rows(tm, w) for w, _ in out_defs],
        out_shape=[jax.ShapeDtypeStruct((s, w), dt) for w, dt in out_defs],
        compiler_params=_params(("arbitrary",), VMEM_LIMIT),
    )(x, ada, ada, norm_w, w_in_bf, cos_t, sin_t, q_lora_norm, kv_lora_norm, qhn_pad, khn_pad,
      w_uq_bf, w_uk_bf, w_uv_bf)


def _log_sigmoid_pair(z):
    ls = jnp.minimum(z, 0.0) - jnp.log(1.0 + jnp.exp(-jnp.abs(z)))
    return ls, ls - z


def _pair(hh):
    return slice((hh // 2) * LANES, (hh // 2 + 1) * LANES)


def _sb_fwd_call(q, k, v):
    s = q.shape[0]
    bq = _row_tile(s, 256)
    nq = s // bq
    nh = SB_GROUP
    width = nh * HALF_LANES

    def body(q_ref, k_ref, v_ref, o_ref, r_ref, ks_ref):
        hp, i = pl.program_id(0), pl.program_id(1)
        lane = lax.broadcasted_iota(I32, (bq, LANES), 1)
        row = lax.broadcasted_iota(I32, (bq, bq), 0)
        col = lax.broadcasted_iota(I32, (bq, bq), 1)
        strict = col < row
        later = jnp.where(row > col, 1.0, 0.0).astype(BF16)
        masks = [_head_mask(lane, hh).astype(BF16) for hh in range(2)]
        qms = [q_ref[:, _pair(hh)] * jnp.asarray(SB_SCALE, BF16) * masks[hh % 2] for hh in range(nh)]

        def walk(blocks, state):
            chains = [(kb, diagonal, hh) for kb, diagonal in blocks for hh in range(nh)]
            keys = lambda kb: pl.ds(pl.multiple_of(kb * bq, bq), bq)
            zs = [_dot_nt(qms[hh], k_ref[keys(kb), _pair(hh)]) for kb, _, hh in chains]
            pairs = []
            for z, (_, diagonal, _) in zip(zs, chains):
                ls, lk = _log_sigmoid_pair(z)
                pairs.append((ls, jnp.where(strict, lk, 0.0) if diagonal else lk))
            sums = [_split_dot(lk, later) for _, lk in pairs]
            runs = [st[0] for st in state]
            ws = []
            for (ls, lk), after, (_, diagonal, hh) in zip(pairs, sums, chains):
                w = jnp.exp(ls + (after + runs[hh]))
                ws.append((jnp.where(strict, w, 0.0) if diagonal else w).astype(BF16))
                runs[hh] = runs[hh] + jnp.sum(lk, axis=1, keepdims=True)
            accs = [st[1] for st in state]
            for w, (kb, _, hh) in zip(ws, chains):
                accs[hh] = accs[hh] + _dot(w, v_ref[keys(kb), _pair(hh)])
            return tuple(zip(runs, accs))

        def alive(state):
            top = jnp.max(state[0][0])
            for st in state[1:]:
                top = jnp.maximum(top, jnp.max(st[0]))
            return (top > SB_DEAD).astype(I32)

        def finish(state, first):
            ks_ref[hp, i] = first
            for pair in range(nh // 2):
                o_ref[:, _pair(2 * pair)] = jnp.where(lane < HALF_LANES, state[2 * pair][1], state[2 * pair + 1][1])
                r_ref[:, _pair(2 * pair)] = jnp.where(lane < HALF_LANES, state[2 * pair][0], state[2 * pair + 1][0])

        zero = ((jnp.zeros((bq, 1), F32), jnp.zeros((bq, LANES), F32)),) * nh

        @pl.when(i == 0)
        def _():
            finish(walk([(0, True)], zero), 0)

        @pl.when(i > 0)
        def _():
            state = walk([(i, True), (i - 1, False)], zero)

            def cond(carry):
                return jnp.logical_and(carry[0] >= 0, carry[1] > 0)

            def step(carry):
                state = walk([(carry[0], False)], carry[2])
                return carry[0] - 1, alive(state), state

            kb, _, state = lax.while_loop(cond, step, (i - 2, alive(state), state))
            finish(state, kb + 1)

    return pl.pallas_call(
        body, name="sb_fwd",
        grid=(SB_HEADS // nh, nq),
        in_specs=[pl.BlockSpec((bq, width), lambda h, i: (i, h)),
                  pl.BlockSpec((s, width), lambda h, i: (0, h)),
                  pl.BlockSpec((s, width), lambda h, i: (0, h))],
        out_specs=[pl.BlockSpec((bq, width), lambda h, i: (i, h)),
                   pl.BlockSpec((bq, width), lambda h, i: (i, h)),
                   pl.BlockSpec(memory_space=pltpu.SMEM)],
        out_shape=[jax.ShapeDtypeStruct((s, SB_WIDTH), F32), jax.ShapeDtypeStruct((s, SB_WIDTH), F32),
                   jax.ShapeDtypeStruct((SB_HEADS // nh, nq), I32)],
        compiler_params=_params(("arbitrary", "arbitrary"), VMEM_LIMIT),
    )(q, k, v)


def _mla_fwd_call(q, k, v):
    s = q.shape[0]
    bq = _row_tile(s, MLA_BQ)
    bk = _row_tile(s, MLA_BK)
    nq = s // bq
    assert bk % bq == 0
    step = min(bk // 2, MLA_STEP)
    nsub = bk // step
    assert nsub % 2 == 0

    def body(q_ref, k_ref, v_ref, o_ref, lse_ref, p_ref, s_ref):
        i = pl.program_id(1)
        lane = lax.broadcasted_iota(I32, (bq, LANES), 1)
        row = lax.broadcasted_iota(I32, (step, bq), 1)
        col = lax.broadcasted_iota(I32, (step, bq), 0)
        n_full = (i * bq) // bk

        def keys(g):
            return pl.ds(pl.multiple_of(g * step, step), step)

        def join(left, right, qlo):
            return right if qlo == 0 else jnp.concatenate([left[:, :qlo], right], axis=1)

        def put_scores(g, slot, qlo=0):
            for hh in range(2):
                cols = slice(hh * HEAD_PAD, (hh + 1) * HEAD_PAD)
                s_ref[slot, hh, :, qlo:] = _dot_nt(k_ref[keys(g), cols], q_ref[qlo:, cols])

        def add_pv(carry, g, slot, qlo=0):
            vblk = v_ref[keys(g), :]
            out = []
            for hh, (m, l, acc, alpha) in enumerate(carry):
                upd = alpha[:, qlo:] * acc[:, qlo:] + _dot_tn(vblk, p_ref[slot, hh, :, qlo:])
                out.append((m, l, join(acc, upd, qlo), alpha))
            return tuple(out)

        def substep(g, slot, carry, masked, prefetch, qlo=0, next_qlo=0, prev_qlo=0, first=False):
            if prefetch:
                put_scores(g + 1, 1 - slot, next_qlo)
            if not first:
                carry = add_pv(carry, g - 1, 1 - slot, prev_qlo)
            new = []
            for hh in range(2):
                m, l, acc, _ = carry[hh]
                sc = s_ref[slot, hh, :, qlo:]
                if masked:
                    sc = jnp.where(col[:, qlo:] + g * step <= row[:, qlo:] + i * bq, sc, MASK_NEG)
                m_new = jnp.maximum(m[:, qlo:], jnp.max(sc, axis=0, keepdims=True))
                p = jnp.exp2(sc - m_new)
                alpha = jnp.exp2(m[:, qlo:] - m_new)
                l_new = alpha * l[:, qlo:] + jnp.sum(p, axis=0, keepdims=True)
                p_ref[slot, hh, :, qlo:] = p.astype(BF16)
                new.append((join(m, m_new, qlo), join(l, l_new, qlo), acc, join(jnp.ones_like(m), alpha, qlo)))
            return tuple(new)

        def first_query(t, masked):
            return t * step if (masked and bk == bq and 0 <= t < nsub) else 0

        def chunk(kb, carry, masked, first=False):
            for t in range(nsub):
                last = masked and t == nsub - 1
                carry = substep(nsub * kb + t, t % 2, carry, masked, not last, first_query(t, masked),
                                first_query(t + 1, masked), first_query(t - 1, masked), first and t == 0)
            return carry

        def finish(carry):
            (m0, l0, a0, _), (m1, l1, a1, _) = add_pv(carry, nsub * n_full + nsub - 1, 1,
                                                      first_query(nsub - 1, True))
            o_ref[...] = jnp.where(lane < HALF_LANES, (a0 / l0).T, (a1 / l1).T)
            sub = lax.broadcasted_iota(I32, (8, bq), 0)
            lse_ref[...] = jnp.where(sub == 0, m0 + jnp.log2(l0), jnp.where(sub == 1, m1 + jnp.log2(l1), 0.0))

        put_scores(0, 0)
        one = (jnp.full((1, bq), MASK_NEG, F32), jnp.zeros((1, bq), F32), jnp.zeros((LANES, bq), F32),
               jnp.ones((1, bq), F32))

        @pl.when(n_full == 0)
        def _():
            finish(chunk(0, (one, one), True, first=True))

        @pl.when(n_full > 0)
        def _():
            carry = chunk(0, (one, one), False, first=True)
            carry = lax.fori_loop(1, n_full, lambda kb, cr: chunk(kb, cr, False), carry)
            finish(chunk(n_full, carry, True))

    return pl.pallas_call(
        body, name="mla_fwd",
        grid=(4, nq),
        in_specs=[pl.BlockSpec((bq, 2 * HEAD_PAD), lambda h, i: (i, h)),
                  pl.BlockSpec((s, 2 * HEAD_PAD), lambda h, i: (0, h)),
                  pl.BlockSpec((s, LANES), lambda h, i: (0, h))],
        out_specs=[pl.BlockSpec((bq, LANES), lambda h, i: (i, h)),
                   pl.BlockSpec((None, 8, bq), lambda h, i: (h, 0, i))],
        out_shape=[jax.ShapeDtypeStruct((s, MLA_WIDTH), F32), jax.ShapeDtypeStruct((4, 8, s), F32)],
        scratch_shapes=[pltpu.VMEM((2, 2, step, bq), BF16), pltpu.VMEM((2, 2, step, bq), F32)],
        compiler_params=_params(("arbitrary", "arbitrary"), VMEM_LIMIT),
    )(q, k, v)


def _out_call(o_sb, g_sb, o_mla, g_mla, x, target, ada, w_out_bf):
    s = x.shape[0]
    tm = _row_tile(s, 512)

    def body(osb_ref, gsb_ref, oml_ref, gml_ref, x_ref, t_ref, gate_ref, w_ref,
             dosb_ref, doml_ref, dgsb_ref, dgml_ref, dy_ref, gw_ref, dgate_ref, sq_ref):
        @pl.when(pl.program_id(0) == 0)
        def _():
            gw_ref[...] = jnp.zeros_like(gw_ref)
            dgate_ref[...] = jnp.zeros_like(dgate_ref)
            sq_ref[...] = jnp.zeros_like(sq_ref)

        g_s, g_m = gsb_ref[...], gml_ref[...]
        sig_s, sig_m = _sigmoid(g_s), _sigmoid(g_m)
        silu_s, silu_m = g_s * sig_s, g_m * sig_m
        o_s, o_m = osb_ref[...], oml_ref[...]
        mixed = jnp.concatenate([o_s * silu_s, o_m * silu_m], axis=1).astype(BF16)
        u = _dot(mixed, w_ref[...])
        gate_v = gate_ref[...]
        err = x_ref[...] + gate_v * u - t_ref[...]
        sq_ref[...] += jnp.sum(err * err, axis=0, keepdims=True)
        dy = err * (1.0 / D_MODEL)
        dy_ref[...] = dy
        dgate_ref[...] += jnp.sum(dy * u, axis=0, keepdims=True)
        du = (dy * gate_v).astype(BF16)
        gw_ref[...] += _dot_tn(mixed, du)
        dmix = _dot_nt(du, w_ref[...])
        dm_s, dm_m = dmix[:, :SB_WIDTH], dmix[:, SB_WIDTH:]
        dosb_ref[...] = (dm_s * silu_s).astype(BF16)
        doml_ref[...] = (dm_m * silu_m).astype(BF16)
        dgsb_ref[...] = (dm_s * o_s * (sig_s * (1.0 + g_s * (1.0 - sig_s)))).astype(BF16)
        dgml_ref[...] = (dm_m * o_m * (sig_m * (1.0 + g_m * (1.0 - sig_m)))).astype(BF16)

    return pl.pallas_call(
        body, name="out_proj_loss",
        grid=(s // tm,),
        in_specs=[_rows(tm, 512), _rows(tm, 512), _rows(tm, 512), _rows(tm, 512),
                  _rows(tm, D_MODEL), _rows(tm, D_MODEL), _ada_part(2), _full((D_MODEL, D_MODEL))],
        out_specs=[_rows(tm, 512), _rows(tm, 512), _rows(tm, 512), _rows(tm, 512), _rows(tm, D_MODEL),
                   _full((D_MODEL, D_MODEL)), _full((1, D_MODEL)), _full((1, D_MODEL))],
        out_shape=[jax.ShapeDtypeStruct((s, 512), BF16)] * 4
        + [jax.ShapeDtypeStruct((s, D_MODEL), F32), jax.ShapeDtypeStruct((D_MODEL, D_MODEL), F32),
           jax.ShapeDtypeStruct((1, D_MODEL), F32), jax.ShapeDtypeStruct((1, D_MODEL), F32)],
        compiler_params=_params(("arbitrary",), VMEM_LIMIT),
    )(o_sb, g_sb, o_mla, g_mla, x, target, ada, w_out_bf)


def _head_mask(lane, hh):
    return jnp.where((lane >= HALF_LANES) if hh else (lane < HALF_LANES), 1.0, 0.0)


def _pick_lane(packed, lane, which):
    return jnp.sum(jnp.where(lane == which, packed, 0.0), axis=1, keepdims=True)


def _sb_bwd_call(kstart, q, k, v, do, rfin):
    s = q.shape[0]
    bq = _row_tile(s, 256)
    nq = s // bq
    nh = SB_GROUP
    width = nh * HALF_LANES

    def body(ks_ref, q_ref, k_ref, v_ref, do_ref, r_ref, dq_ref, dk_ref, dv_ref):
        hp, i = pl.program_id(0), pl.program_id(1)

        @pl.when(i == 0)
        def _():
            dk_ref[...] = jnp.zeros_like(dk_ref)
            dv_ref[...] = jnp.zeros_like(dv_ref)

        lane = lax.broadcasted_iota(I32, (bq, LANES), 1)
        row = lax.broadcasted_iota(I32, (bq, bq), 0)
        col = lax.broadcasted_iota(I32, (bq, bq), 1)
        upto = jnp.where(row <= col, 1.0, 0.0).astype(BF16)
        before = jnp.where(row < col, 1.0, 0.0).astype(BF16)
        masks = [_head_mask(lane, hh).astype(BF16) for hh in range(2)]
        qms = [q_ref[:, _pair(hh)] * jnp.asarray(SB_SCALE, BF16) * masks[hh % 2] for hh in range(nh)]
        doms = [do_ref[:, _pair(hh)] * masks[hh % 2] for hh in range(nh)]
        totals = [_pick_lane(r_ref[:, _pair(hh)], lane, HALF_LANES * (hh % 2)) for hh in range(nh)]
        strict = col < row

        def walk(blocks, state):
            chains = [(kb, diagonal, hh) for kb, diagonal in blocks for hh in range(nh)]
            keys = lambda kb: pl.ds(pl.multiple_of(kb * bq, bq), bq)
            cut = lambda x, diagonal: jnp.where(strict, x, 0.0) if diagonal else x
            zs = [_dot_nt(qms[hh], k_ref[keys(kb), _pair(hh)]) for kb, _, hh in chains]
            dws = [_dot_nt(doms[hh], v_ref[keys(kb), _pair(hh)]) for kb, _, hh in chains]
            pairs = []
            for z, (_, diagonal, _) in zip(zs, chains):
                ls, lk = _log_sigmoid_pair(z)
                pairs.append((ls, cut(lk, diagonal)))
            incls = [_split_dot(lk, upto) for _, lk in pairs]
            pres = [st[0] for st in state]
            ws, gs = [], []
            for (ls, lk), incl, dw, (_, diagonal, hh) in zip(pairs, incls, dws, chains):
                w = cut(jnp.exp(ls + ((totals[hh] - pres[hh]) - incl)), diagonal)
                ws.append(w.astype(BF16))
                gs.append(w * dw)
                pres[hh] = pres[hh] + jnp.sum(lk, axis=1, keepdims=True)
            gsums = [_dot(g.astype(BF16), before) for g in gs]
            gpres = [st[1] for st in state]
            dzs = []
            for (ls, _), g, gsum, (_, diagonal, hh) in zip(pairs, gs, gsums, chains):
                dzs.append(cut(g - jnp.exp(ls) * (g + (gpres[hh] + gsum)), diagonal).astype(BF16))
                gpres[hh] = gpres[hh] + jnp.sum(g, axis=1, keepdims=True)
            dqs = [st[2] for st in state]
            dk_parts, dv_parts = [], []
            for dzb, w, (kb, _, hh) in zip(dzs, ws, chains):
                dk_parts.append(_dot_tn(dzb, qms[hh]))
                dv_parts.append(_dot_tn(w, doms[hh]))
                dqs[hh] = dqs[hh] + _dot(dzb, k_ref[keys(kb), _pair(hh)])
            for b, (kb, _) in enumerate(blocks):
                for pair in range(nh // 2):
                    c0 = b * nh + 2 * pair
                    dk_ref[keys(kb), _pair(2 * pair)] += dk_parts[c0] + dk_parts[c0 + 1]
                    dv_ref[keys(kb), _pair(2 * pair)] += dv_parts[c0] + dv_parts[c0 + 1]
            return tuple(zip(pres, gpres, dqs))

        def finish(state):
            for pair in range(nh // 2):
                both = jnp.where(lane < HALF_LANES, state[2 * pair][2], state[2 * pair + 1][2])
                dq_ref[:, _pair(2 * pair)] = (both * SB_SCALE).astype(BF16)

        zero = ((jnp.zeros((bq, 1), F32), jnp.zeros((bq, 1), F32), jnp.zeros((bq, LANES), F32)),) * nh

        @pl.when(i == 0)
        def _():
            finish(walk([(0, True)], zero))

        @pl.when(i > 0)
        def _():
            state = lax.fori_loop(ks_ref[hp, i], i - 1, lambda kb, st: walk([(kb, False)], st), zero)
            finish(walk([(i - 1, False), (i, True)], state))

    return pl.pallas_call(
        body, name="sb_bwd",
        grid_spec=pltpu.PrefetchScalarGridSpec(
            num_scalar_prefetch=1, grid=(SB_HEADS // nh, nq),
            in_specs=[pl.BlockSpec((bq, width), lambda h, i, ks: (i, h)),
                      pl.BlockSpec((s, width), lambda h, i, ks: (0, h), pipeline_mode=pl.Buffered(1)),
                      pl.BlockSpec((s, width), lambda h, i, ks: (0, h), pipeline_mode=pl.Buffered(1)),
                      pl.BlockSpec((bq, width), lambda h, i, ks: (i, h)),
                      pl.BlockSpec((bq, width), lambda h, i, ks: (i, h))],
            out_specs=[pl.BlockSpec((bq, width), lambda h, i, ks: (i, h)),
                       pl.BlockSpec((s, width), lambda h, i, ks: (0, h), pipeline_mode=pl.Buffered(1)),
                       pl.BlockSpec((s, width), lambda h, i, ks: (0, h), pipeline_mode=pl.Buffered(1))]),
        out_shape=[jax.ShapeDtypeStruct((s, SB_WIDTH), BF16), jax.ShapeDtypeStruct((s, SB_WIDTH), F32),
                   jax.ShapeDtypeStruct((s, SB_WIDTH), F32)],
        compiler_params=_params(("arbitrary", "arbitrary"), VMEM_LIMIT),
    )(kstart, q, k, v, do, rfin)


def _mla_bwd_call(q, k, v, do, o, lse):
    s = q.shape[0]
    bq = _row_tile(s, MLA_BWD_BQ)
    bk = _row_tile(s, MLA_BWD_BK)
    nq = s // bq
    assert bk % bq == 0
    step = min(bk // 2, MLA_BWD_STEP)
    nsub = bk // step
    assert nsub % 2 == 0

    def body(q_ref, k_ref, v_ref, do_ref, o_ref, lse_ref, dq_ref, dk_ref, dv_ref, dom_ref, s_ref, dp_ref, pb_ref,
             ds_ref):
        i = pl.program_id(1)

        @pl.when(i == 0)
        def _():
            dk_ref[...] = jnp.zeros_like(dk_ref)
            dv_ref[...] = jnp.zeros_like(dv_ref)

        lane = lax.broadcasted_iota(I32, (bq, LANES), 1)
        row = lax.broadcasted_iota(I32, (step, bq), 1)
        col = lax.broadcasted_iota(I32, (step, bq), 0)
        n_full = (i * bq) // bk
        do2 = do_ref[...]
        prod = do2.astype(F32) * o_ref[...]
        ones = jnp.ones((8, LANES), BF16)
        deltas, lses = [], []
        for hh in range(2):
            head = _head_mask(lane, hh)
            dom_ref[hh] = do2 * head.astype(BF16)
            part = prod * head
            hi = part.astype(BF16)
            lo = (part - hi.astype(F32)).astype(BF16)
            deltas.append((_dot_nt(ones, hi) + _dot_nt(ones, lo))[0:1])
            lses.append(lse_ref[hh:hh + 1, :])

        def keys(g):
            return pl.ds(pl.multiple_of(g * step, step), step)

        def heads():
            return [(hh, slice(hh * HEAD_PAD, (hh + 1) * HEAD_PAD)) for hh in range(2)]

        def put_products(g, slot, qlo=0):
            vblk = v_ref[keys(g), :]
            for hh, cols in heads():
                s_ref[slot, hh, :, qlo:] = _dot_nt(k_ref[keys(g), cols], q_ref[qlo:, cols])
                dp_ref[slot, hh, :, qlo:] = _dot_nt(vblk, dom_ref[hh, qlo:, :])

        def add_grads(dqs, g, slot, qlo=0):
            rows = keys(g)
            new, dv_parts = [], []
            for hh, cols in heads():
                ds = ds_ref[slot, hh, :, qlo:]
                dk_ref[rows, cols] += _dot(ds, q_ref[qlo:, cols])
                dv_parts.append(_dot(pb_ref[slot, hh, :, qlo:], dom_ref[hh, qlo:, :]))
                upd = dqs[hh][:, qlo:] + _dot_tn(k_ref[rows, cols], ds)
                new.append(upd if qlo == 0 else jnp.concatenate([dqs[hh][:, :qlo], upd], axis=1))
            dv_ref[rows, :] += dv_parts[0] + dv_parts[1]
            return tuple(new)

        def substep(g, slot, dqs, masked, prefetch, qlo=0, next_qlo=0, prev_qlo=0, first=False):
            if prefetch:
                put_products(g + 1, 1 - slot, next_qlo)
            if not first:
                dqs = add_grads(dqs, g - 1, 1 - slot, prev_qlo)
            for hh, _ in heads():
                p = jnp.exp2(s_ref[slot, hh, :, qlo:] - lses[hh][:, qlo:])
                if masked:
                    p = jnp.where(col[:, qlo:] + g * step <= row[:, qlo:] + i * bq, p, 0.0)
                ds_ref[slot, hh, :, qlo:] = (p * (dp_ref[slot, hh, :, qlo:] - deltas[hh][:, qlo:])).astype(BF16)
                pb_ref[slot, hh, :, qlo:] = p.astype(BF16)
            return dqs

        def first_query(t, masked):
            return t * step if (masked and bk == bq and 0 <= t < nsub) else 0

        def chunk(kb, dqs, masked, first=False):
            for t in range(nsub):
                last = masked and t == nsub - 1
                dqs = substep(nsub * kb + t, t % 2, dqs, masked, not last, first_query(t, masked),
                              first_query(t + 1, masked), first_query(t - 1, masked), first and t == 0)
            return dqs

        def finish(dqs):
            dqs = add_grads(dqs, nsub * n_full + nsub - 1, 1, first_query(nsub - 1, True))
            dq_ref[:, :HEAD_PAD] = dqs[0].T * MLA_SCALE
            dq_ref[:, HEAD_PAD:] = dqs[1].T * MLA_SCALE

        put_products(0, 0)
        zero = jnp.zeros((HEAD_PAD, bq), F32)

        @pl.when(n_full == 0)
        def _():
            finish(chunk(0, (zero, zero), True, first=True))

        @pl.when(n_full > 0)
        def _():
            dqs = chunk(0, (zero, zero), False, first=True)
            dqs = lax.fori_loop(1, n_full, lambda kb, dqs: chunk(kb, dqs, False), dqs)
            finish(chunk(n_full, dqs, True))

    return pl.pallas_call(
        body, name="mla_bwd",
        grid=(4, nq),
        in_specs=[pl.BlockSpec((bq, 2 * HEAD_PAD), lambda h, i: (i, h)),
                  pl.BlockSpec((s, 2 * HEAD_PAD), lambda h, i: (0, h)),
                  pl.BlockSpec((s, LANES), lambda h, i: (0, h)),
                  pl.BlockSpec((bq, LANES), lambda h, i: (i, h)),
                  pl.BlockSpec((bq, LANES), lambda h, i: (i, h)),
                  pl.BlockSpec((None, 8, bq), lambda h, i: (h, 0, i))],
        out_specs=[pl.BlockSpec((bq, 2 * HEAD_PAD), lambda h, i: (i, h)),
                   pl.BlockSpec((s, 2 * HEAD_PAD), lambda h, i: (0, h), pipeline_mode=pl.Buffered(1)),
                   pl.BlockSpec((s, LANES), lambda h, i: (0, h), pipeline_mode=pl.Buffered(1))],
        out_shape=[jax.ShapeDtypeStruct((s, MLA_PAD_WIDTH), F32), jax.ShapeDtypeStruct((s, MLA_PAD_WIDTH), F32),
                   jax.ShapeDtypeStruct((s, MLA_WIDTH), F32)],
        scratch_shapes=[pltpu.VMEM((2, bq, LANES), BF16),
                        pltpu.VMEM((2, 2, step, bq), F32), pltpu.VMEM((2, 2, step, bq), F32),
                        pltpu.VMEM((2, 2, step, bq), BF16), pltpu.VMEM((2, 2, step, bq), BF16)],
        compiler_params=_params(("arbitrary", "arbitrary"), VMEM_LIMIT),
    )(q, k, v, do, o, lse)


def _rms_bwd(d_out, inp, r, weight, n):
    normed = inp * r
    gw = d_out * weight
    d_in = r * (gw - normed * (jnp.sum(gw * normed, axis=-1, keepdims=True) * (1.0 / n)))
    return d_in, d_out * normed


def _mla_prep_bwd_call(dq, dk, dv, q0, k0, cqn, ckvn, c_q, c_kv, cos_t, sin_t,
                       q_lora_norm, kv_lora_norm, qhn_pad, khn_pad, w_uq_bf, w_uk_bf, w_uv_bf):
    s = dq.shape[0]
    tm = _row_tile(s, 512)

    def body(dq_ref, dk_ref, dv_ref, q0_ref, k0_ref, cqn_ref, ckvn_ref, cq_ref, ckv_ref,
             cos_ref, sin_ref, qln_ref, kvln_ref, qhn_ref, khn_ref, wuq_ref, wuk_ref, wuv_ref,
             dcq_ref, dckv_ref, dkr_ref, gwuq_ref, gwuk_ref, gwuv_ref, gqln_ref, gkvln_ref, gqhn_ref, gkhn_ref,
             dq0_ref, dk0_ref, tmp_ref):
        @pl.when(pl.program_id(0) == 0)
        def _():
            for ref in (gwuq_ref, gwuk_ref, gwuv_ref, gqln_ref, gkvln_ref, gqhn_ref, gkhn_ref):
                ref[...] = jnp.zeros_like(ref)

        cos_t, sin_t = cos_ref[...], sin_ref[...]
        lane = lax.broadcasted_iota(I32, (tm, LANES), 1)
        rope_lanes = jnp.logical_or(lane < ROPE_HALF,
                                    jnp.logical_and(lane >= HALF_LANES, lane < HALF_LANES + ROPE_HALF))
        heads = [slice(h * HEAD_PAD, (h + 1) * HEAD_PAD) for h in range(MLA_HEADS)]

        def head_norm_bwd(d_ref, x0_ref, w_ref, out_ref, scale):
            w = w_ref[...]
            inv = [lax.rsqrt(jnp.sum(x0_ref[:, cols] * x0_ref[:, cols], axis=-1, keepdims=True)
                             * (1.0 / MLA_QK_DIM) + EPS) for cols in heads]
            for cols in heads:
                tmp_ref[:, cols] = _rope_adjoint(d_ref[:, cols] * scale, cos_t, sin_t)
            dots = [jnp.sum(tmp_ref[:, cols] * w * (x0_ref[:, cols] * r), axis=-1, keepdims=True)
                    for cols, r in zip(heads, inv)]
            g_w = jnp.zeros((1, LANES), F32)
            rope_sum = jnp.zeros((tm, LANES), F32)
            for cols, r, dot in zip(heads, inv, dots):
                normed = x0_ref[:, cols] * r
                d_n = tmp_ref[:, cols]
                d_x0 = r * (d_n * w - normed * (dot * (1.0 / MLA_QK_DIM)))
                out_ref[:, cols] = d_x0.astype(BF16)
                g_w = g_w + jnp.sum(d_n * normed, axis=0, keepdims=True)
                rope_sum = rope_sum + jnp.where(rope_lanes, d_x0, 0.0)
            return g_w, rope_sum

        g_qhn, _ = head_norm_bwd(dq_ref, q0_ref, qhn_ref, dq0_ref, 1.0)
        g_khn, d_kr = head_norm_bwd(dk_ref, k0_ref, khn_ref, dk0_ref, LN2)
        cqn, ckvn = cqn_ref[...], ckvn_ref[...]
        d_q0b, d_k0b, dvb = dq0_ref[...], dk0_ref[...], dv_ref[...].astype(BF16)
        d_cqn = _dot_nt(d_q0b, wuq_ref[...])
        gwuq_ref[...] += _dot_tn(cqn, d_q0b)
        d_ckvn = _dot_nt(d_k0b, wuk_ref[...]) + _dot_nt(dvb, wuv_ref[...])
        gwuk_ref[...] += _dot_tn(ckvn, d_k0b)
        gwuv_ref[...] += _dot_tn(ckvn, dvb)
        gqhn_ref[...] += g_qhn
        gkhn_ref[...] += g_khn
        dkr_ref[...] = d_kr.astype(BF16)
        cq = cq_ref[...]
        rcq = lax.rsqrt(jnp.mean(cq * cq, axis=-1, keepdims=True) + EPS)
        d_cq, gl = _rms_bwd(d_cqn, cq, rcq, qln_ref[...], Q_LORA_RANK)
        dcq_ref[...] = d_cq.astype(BF16)
        gqln_ref[...] += jnp.sum(gl, axis=0, keepdims=True)
        ckv = ckv_ref[...]
        rckv = lax.rsqrt(jnp.mean(ckv * ckv, axis=-1, keepdims=True) + EPS)
        d_ckv, gl = _rms_bwd(d_ckvn, ckv, rckv, kvln_ref[...], KV_LORA_RANK)
        dckv_ref[...] = d_ckv.astype(BF16)
        gkvln_ref[...] += jnp.sum(gl, axis=0, keepdims=True)

    return pl.pallas_call(
        body, name="mla_prep_bwd",
        grid=(s // tm,),
        in_specs=[_rows(tm, MLA_PAD_WIDTH), _rows(tm, MLA_PAD_WIDTH), _rows(tm, MLA_WIDTH),
                  _rows(tm, MLA_PAD_WIDTH), _rows(tm, MLA_PAD_WIDTH),
                  _rows(tm, Q_LORA_RANK), _rows(tm, KV_LORA_RANK), _rows(tm, Q_LORA_RANK), _rows(tm, KV_LORA_RANK),
                  _rows(tm, LANES), _rows(tm, LANES),
                  _full((1, Q_LORA_RANK)), _full((1, KV_LORA_RANK)), _full((1, LANES)), _full((1, LANES)),
                  _full((Q_LORA_RANK, MLA_PAD_WIDTH)), _full((KV_LORA_RANK, MLA_PAD_WIDTH)),
                  _full((KV_LORA_RANK, MLA_WIDTH))],
        out_specs=[_rows(tm, Q_LORA_RANK), _rows(tm, KV_LORA_RANK), _rows(tm, LANES),
                   _full((Q_LORA_RANK, MLA_PAD_WIDTH)), _full((KV_LORA_RANK, MLA_PAD_WIDTH)),
                   _full((KV_LORA_RANK, MLA_WIDTH)),
                   _full((1, Q_LORA_RANK)), _full((1, KV_LORA_RANK)), _full((1, LANES)), _full((1, LANES))],
        out_shape=[jax.ShapeDtypeStruct((s, Q_LORA_RANK), BF16), jax.ShapeDtypeStruct((s, KV_LORA_RANK), BF16),
                   jax.ShapeDtypeStruct((s, LANES), BF16),
                   jax.ShapeDtypeStruct((Q_LORA_RANK, MLA_PAD_WIDTH), F32),
                   jax.ShapeDtypeStruct((KV_LORA_RANK, MLA_PAD_WIDTH), F32),
                   jax.ShapeDtypeStruct((KV_LORA_RANK, MLA_WIDTH), F32),
                   jax.ShapeDtypeStruct((1, Q_LORA_RANK), F32), jax.ShapeDtypeStruct((1, KV_LORA_RANK), F32),
                   jax.ShapeDtypeStruct((1, LANES), F32), jax.ShapeDtypeStruct((1, LANES), F32)],
        scratch_shapes=[pltpu.VMEM((tm, MLA_PAD_WIDTH), BF16), pltpu.VMEM((tm, MLA_PAD_WIDTH), BF16),
                        pltpu.VMEM((tm, MLA_PAD_WIDTH), F32)],
        compiler_params=_params(("arbitrary",), VMEM_LIMIT),
    )(dq, dk, dv, q0, k0, cqn, ckvn, c_q, c_kv, cos_t, sin_t,
      q_lora_norm, kv_lora_norm, qhn_pad, khn_pad, w_uq_bf, w_uk_bf, w_uv_bf)


def _dh_call(pieces, hb, x, dy, ada, norm_w, w_in_bf):
    s = x.shape[0]
    tm = _row_tile(s, 512)
    widths = [p.shape[1] for p in pieces]
    offsets = [sum(widths[:j]) for j in range(len(widths))]
    assert offsets[-1] + widths[-1] == IN_COLS_PAD
    n = len(pieces)

    def body(*refs):
        p_refs = refs[:n]
        (hb_ref, x_ref, dy_ref, sh_ref, sc_ref, nw_ref, w_ref, gx_ref, gw_ref, dsh_ref, dsc_ref, gnw_ref,
         dp_ref, acc_ref) = refs[n:]

        @pl.when(pl.program_id(0) == 0)
        def _():
            acc_ref[...] = jnp.zeros_like(acc_ref)
            dsh_ref[...] = jnp.zeros_like(dsh_ref)
            dsc_ref[...] = jnp.zeros_like(dsc_ref)
            gnw_ref[...] = jnp.zeros_like(gnw_ref)

        for p_ref, c0, width in zip(p_refs, offsets, widths):
            dp_ref[:, c0:c0 + width] = p_ref[...].astype(BF16)
        acc_ref[...] += _dot_tn(hb_ref[...], dp_ref[...])

        @pl.when(pl.program_id(0) == pl.num_programs(0) - 1)
        def _():
            gw_ref[...] = acc_ref[...].astype(BF16)

        dh = _dot_nt(dp_ref[...], w_ref[...])
        xx = x_ref[...]
        r0 = lax.rsqrt(jnp.mean(xx * xx, axis=-1, keepdims=True) + EPS)
        xn = xx * r0
        nw = nw_ref[...]
        dsh_ref[...] += jnp.sum(dh, axis=0, keepdims=True)
        dsc_ref[...] += jnp.sum(dh * (xn * nw), axis=0, keepdims=True)
        dn = dh * (1.0 + sc_ref[...])
        gnw_ref[...] += jnp.sum(dn * xn, axis=0, keepdims=True)
        dxn = dn * nw
        gx_ref[...] = dy_ref[...] + r0 * (dxn - xn * jnp.mean(dxn * xn, axis=-1, keepdims=True))

    return pl.pallas_call(
        body, name="in_proj_bwd",
        grid=(s // tm,),
        in_specs=[_rows(tm, w) for w in widths]
        + [_rows(tm, D_MODEL), _rows(tm, D_MODEL), _rows(tm, D_MODEL),
           _ada_part(0), _ada_part(1), _full((1, D_MODEL)),
           pl.BlockSpec((D_MODEL, IN_COLS_PAD), lambda i: (0, 0), pipeline_mode=pl.Buffered(1))],
        out_specs=[_rows(tm, D_MODEL),
                   pl.BlockSpec((D_MODEL, IN_COLS_PAD), lambda i: (0, 0), pipeline_mode=pl.Buffered(1)),
                   _full((1, D_MODEL)), _full((1, D_MODEL)), _full((1, D_MODEL))],
        out_shape=[jax.ShapeDtypeStruct((s, D_MODEL), F32), jax.ShapeDtypeStruct((D_MODEL, IN_COLS_PAD), BF16),
                   jax.ShapeDtypeStruct((1, D_MODEL), F32), jax.ShapeDtypeStruct((1, D_MODEL), F32),
                   jax.ShapeDtypeStruct((1, D_MODEL), F32)],
        scratch_shapes=[pltpu.VMEM((tm, IN_COLS_PAD), BF16), pltpu.VMEM((D_MODEL, IN_COLS_PAD), F32)],
        compiler_params=_params(("arbitrary",), VMEM_LIMIT),
    )(*pieces, hb, x, dy, ada, ada, norm_w, w_in_bf)


def _adamw(g, w, m, v):
    m = ADAM_B1 * m + (1.0 - ADAM_B1) * g
    v = ADAM_B2 * v + (1.0 - ADAM_B2) * (g * g)
    m_hat = m / (1.0 - ADAM_B1 ** ADAM_STEP)
    v_hat = v / (1.0 - ADAM_B2 ** ADAM_STEP)
    delta = -ADAM_LR * (m_hat / (jnp.sqrt(v_hat) + ADAM_EPS) + ADAM_WD * w)
    return delta, m, v


def _adam_shard_call(name, parts, w, m, v):
    r, c = w.shape
    tr = r if r <= 512 else 256

    def body(p_ref, w_ref, m_ref, v_ref, g_ref, d_ref, nm_ref, nv_ref):
        g = ((p_ref[0].astype(F32) + p_ref[1].astype(F32)) + p_ref[2].astype(F32)) + p_ref[3].astype(F32)
        g_ref[...] = g
        d_ref[...], nm_ref[...], nv_ref[...] = _adamw(g, w_ref[...], m_ref[...], v_ref[...])

    blk = pl.BlockSpec((tr, c), lambda i: (i, 0))
    return pl.pallas_call(
        body, name=name,
        grid=(r // tr,),
        in_specs=[pl.BlockSpec((4, tr, c), lambda i: (0, i, 0)), blk, blk, blk],
        out_specs=[blk] * 4,
        out_shape=[jax.ShapeDtypeStruct((r, c), F32)] * 4,
        compiler_params=_params(("arbitrary",), VMEM_LIMIT),
    )(parts, w, m, v)


def _adam_shard_transposed_call(name, parts, w_t, m_t, v_t):
    c, r = w_t.shape
    assert parts.shape[1:] == (r, c)
    tr = r if r <= 512 else 256
    c_pad = -(-c // LANES) * LANES

    def body(p_ref, w_ref, m_ref, v_ref, g_ref, d_ref, nm_ref, nv_ref, pad_ref):
        @pl.when(pl.program_id(0) == 0)
        def _():
            pad_ref[...] = jnp.zeros_like(pad_ref)

        pad_ref[:, :c] = ((p_ref[0].astype(F32) + p_ref[1].astype(F32)) + p_ref[2].astype(F32)) + p_ref[3].astype(F32)
        g = pad_ref[...].T[:c]
        g_ref[...] = g
        d_ref[...], nm_ref[...], nv_ref[...] = _adamw(g, w_ref[...], m_ref[...], v_ref[...])

    blk = pl.BlockSpec((c, tr), lambda i: (0, i))
    return pl.pallas_call(
        body, name=name,
        grid=(r // tr,),
        in_specs=[pl.BlockSpec((4, tr, c), lambda i: (0, i, 0)), blk, blk, blk],
        out_specs=[blk] * 4,
        out_shape=[jax.ShapeDtypeStruct((c, r), F32)] * 4,
        scratch_shapes=[pltpu.VMEM((tr, c_pad), F32)],
        compiler_params=_params(("arbitrary",), VMEM_LIMIT),
    )(parts, w_t, m_t, v_t)


def _adam_ada_call(c_all, d_all, w, m, v):
    r, c = w.shape
    tr = 256

    def body(c_ref, d_ref, w_ref, m_ref, v_ref, g_ref, dl_ref, nm_ref, nv_ref):
        cc = c_ref[...]
        sc = cc * _sigmoid(cc)
        dd = d_ref[...]
        sc_hi = sc.astype(BF16)
        sc_lo = (sc - sc_hi.astype(F32)).astype(BF16)
        dd_hi = dd.astype(BF16)
        dd_lo = (dd - dd_hi.astype(F32)).astype(BF16)
        g = _dot_tn(sc_hi, dd_hi) + (_dot_tn(sc_hi, dd_lo) + _dot_tn(sc_lo, dd_hi))
        g_ref[...] = g
        dl_ref[...], nm_ref[...], nv_ref[...] = _adamw(g, w_ref[...], m_ref[...], v_ref[...])

    blk = pl.BlockSpec((tr, c), lambda i: (i, 0))
    return pl.pallas_call(
        body, name="adam_w_ada",
        grid=(r // tr,),
        in_specs=[pl.BlockSpec((16, tr), lambda i: (0, i)), pl.BlockSpec((16, c), lambda i: (0, 0)), blk, blk, blk],
        out_specs=[blk] * 4,
        out_shape=[jax.ShapeDtypeStruct((r, c), F32)] * 4,
        compiler_params=_params(("arbitrary",), VMEM_LIMIT),
    )(c_all, d_all, w, m, v)


def _adam_vectors_call(packs, offsets, vectors):
    nv = len(vectors)

    def body(*refs):
        p_ref, ins, outs = refs[0], refs[1:1 + 3 * nv], refs[1 + 3 * nv:]
        for j, off in enumerate(offsets):
            n = ins[3 * j].shape[1]
            span = -(-n // LANES) * LANES
            g = p_ref[0, :, off:off + span]
            for b in range(1, 8):
                g = g + p_ref[b, :, off:off + span]
            g = g[:, :n]
            outs[j][...] = g
            outs[nv + j][...], outs[2 * nv + j][...], outs[3 * nv + j][...] = _adamw(
                g, ins[3 * j][...], ins[3 * j + 1][...], ins[3 * j + 2][...])

    flat = [a for t in vectors for a in t]
    res = pl.pallas_call(
        body, name="adam_vectors",
        out_shape=[jax.ShapeDtypeStruct(t[0].shape, F32) for _ in range(4) for t in vectors],
    )(packs, *flat)
    return [res[k * nv:(k + 1) * nv] for k in range(4)]


ROPE_HALF = MLA_ROPE_DIM // 2
NOPE_A = MLA_NOPE_DIM - ROPE_HALF


def _zeros_like_lanes(t, n):
    return jnp.zeros(t.shape[:-1] + (n,), t.dtype)


def _to_head_lanes(t):
    nope, rope = t[..., :MLA_NOPE_DIM], t[..., MLA_NOPE_DIM:]
    return jnp.concatenate([rope[..., :ROPE_HALF], nope[..., :NOPE_A], rope[..., ROPE_HALF:], nope[..., NOPE_A:],
                            _zeros_like_lanes(t, HEAD_PAD - MLA_QK_DIM)], axis=-1)


def _from_head_lanes(t):
    return jnp.concatenate([t[..., ROPE_HALF:HALF_LANES], t[..., HALF_LANES + ROPE_HALF:MLA_QK_DIM],
                            t[..., :ROPE_HALF], t[..., HALF_LANES:HALF_LANES + ROPE_HALF]], axis=-1)


def _nope_to_head_lanes(t):
    return jnp.concatenate([_zeros_like_lanes(t, ROPE_HALF), t[..., :NOPE_A], _zeros_like_lanes(t, ROPE_HALF),
                            t[..., NOPE_A:], _zeros_like_lanes(t, HEAD_PAD - MLA_QK_DIM)], axis=-1)


def _rope_to_head_lanes(t):
    return jnp.concatenate([t[..., :ROPE_HALF], _zeros_like_lanes(t, HALF_LANES - ROPE_HALF), t[..., ROPE_HALF:],
                            _zeros_like_lanes(t, HALF_LANES - ROPE_HALF)], axis=-1)


def _rope_lane_freq():
    inv_freq = (ROPE_THETA ** (-jnp.arange(0, MLA_ROPE_DIM, 2, dtype=F32) / MLA_ROPE_DIM))[None]
    return _rope_to_head_lanes(jnp.concatenate([-inv_freq, inv_freq], axis=1))


def _unshard_cols(g):
    return jnp.transpose(g, (1, 0, 2)).reshape(g.shape[1], 4 * g.shape[2])


def _shard_cols(g):
    r, c4 = g.shape
    return jnp.transpose(g.reshape(r, 4, c4 // 4), (1, 0, 2))


def kernel(x, c, positions, w_ada, b_ada, norm_w, w_in, q_lora_norm, w_uq, kv_lora_norm, w_ukv, q_head_norm, k_head_norm, w_out, loss_target, m_w_ada, m_b_ada, m_norm_w, m_w_in, m_q_lora_norm, m_w_uq, m_kv_lora_norm, m_w_ukv, m_q_head_norm, m_k_head_norm, m_w_out, v_w_ada, v_b_ada, v_norm_w, v_w_in, v_q_lora_norm, v_w_uq, v_kv_lora_norm, v_w_ukv, v_q_head_norm, v_k_head_norm, v_w_out):
    chip = 2 * lax.axis_index("x") + lax.axis_index("y")
    me8 = 2 * chip + lax.axis_index("c")
    ada_cols = w_ada.shape[2]
    c_all = _allgather_rows_call(c)[:, 0, :]
    ada_part = _ada_call(c_all, w_ada[0], lax.dynamic_slice_in_dim(b_ada, chip * ada_cols, ada_cols, axis=1))
    ada_g, win_g, wuq_g, wukv_g, wout_g, cos_t, sin_t = _gather_call(
        [ada_part[None]] + [w.astype(BF16) for w in (w_in, w_uq, w_ukv, w_out)], [False, True, True, True, True],
        positions[0].astype(F32).reshape(-1, LANES), _rope_lane_freq())
    ada = lax.dynamic_slice_in_dim(ada_g, me8, 1, axis=1).reshape(1, 4 * ada_cols)
    (sq_sum, grad_x, g_w_in, g_w_uq, g_w_ukv, g_w_out, d_ada, g_norm_w, g_qln, g_kvln, g_qhn, g_khn) = _local_step(
        x[0], ada, (cos_t, sin_t), loss_target[0], norm_w, win_g,
        q_lora_norm, _unshard_cols(wuq_g), kv_lora_norm, _unshard_cols(wukv_g), q_head_norm, k_head_norm,
        wout_g.reshape(D_MODEL, D_MODEL))

    grads = [g.astype(BF16) for g in (g_w_in, _shard_cols(g_w_uq), _shard_cols(g_w_ukv),
                                      g_w_out.reshape(4, D_MODEL // 4, D_MODEL))]
    pieces = [d_ada, g_norm_w, g_qln, g_kvln, g_qhn, g_khn, (0.5 * sq_sum / D_MODEL).reshape(1, 1)]
    spans = [-(-p.shape[1] // LANES) * LANES for p in pieces]
    starts = [sum(spans[:j]) for j in range(len(spans))]
    small = jnp.concatenate([jnp.pad(p, ((0, 0), (0, sp - p.shape[1]))) for p, sp in zip(pieces, spans)], axis=1)
    core = lax.axis_index("c").astype(I32).reshape(1)
    chip_halves = _sum_halves_call(core, grads, _swap_halves_call(grads))
    parts, packs = _exchange_call(chip_halves, small)
    loss = jnp.sum(packs[:, 0, starts[-1]])

    names = ["adam_w_in", "adam_w_uq", "adam_w_ukv", "adam_w_out"]
    shard_w = [(w_in, m_w_in, v_w_in), (w_uq, m_w_uq, v_w_uq), (w_ukv, m_w_ukv, v_w_ukv),
               (w_out, m_w_out, v_w_out)]
    res = {}
    for name, p_g, (w, m, v) in zip(names, parts, shard_w):
        if w.shape[2] % LANES:
            res_t = _adam_shard_transposed_call(name, p_g, w[0].T, m[0].T, v[0].T)
            res[name] = [t.T for t in res_t]
        else:
            res[name] = _adam_shard_call(name, p_g, w[0], m[0], v[0])
    d_all = lax.dynamic_slice_in_dim(packs[:, 0, :], starts[0] + chip * ada_cols, ada_cols, axis=1)
    res_ada = _adam_ada_call(jnp.pad(c_all, ((0, 8), (0, 0))), jnp.pad(d_all, ((0, 8), (0, 0))),
                             w_ada[0], m_w_ada[0], v_w_ada[0])
    vectors = [(b_ada, m_b_ada, v_b_ada), (norm_w, m_norm_w, v_norm_w), (q_lora_norm, m_q_lora_norm, v_q_lora_norm),
               (kv_lora_norm, m_kv_lora_norm, v_kv_lora_norm), (q_head_norm, m_q_head_norm, v_q_head_norm),
               (k_head_norm, m_k_head_norm, v_k_head_norm)]
    vec_out = _adam_vectors_call(packs, starts[:len(vectors)], vectors)

    def ordered(kind):
        big = lambda name: res[name][kind][None]
        return [res_ada[kind][None], vec_out[kind][0], vec_out[kind][1], big("adam_w_in"), vec_out[kind][2],
                big("adam_w_uq"), vec_out[kind][3], big("adam_w_ukv"), vec_out[kind][4], vec_out[kind][5],
                big("adam_w_out")]

    return (loss, grad_x[None], *ordered(0), *ordered(1), *ordered(2), *ordered(3))


def _local_step(x2, ada, rope_tables, tgt, norm_w, w_in_shards, q_lora_norm, w_uq_full,
                kv_lora_norm, w_ukv_full, q_head_norm, k_head_norm, w_out_full):
    in_shard = w_in_shards.shape[2]
    ckv_tail = C_GM - 3 * in_shard
    assert 0 <= ckv_tail and ckv_tail + MLA_ROPE_DIM + MLA_WIDTH == in_shard
    last = w_in_shards[3]
    w_in_bf = jnp.concatenate(
        [w_in_shards[0], w_in_shards[1], w_in_shards[2], last[:, :ckv_tail], last[:, ckv_tail + MLA_ROPE_DIM:],
         _rope_to_head_lanes(last[:, ckv_tail:ckv_tail + MLA_ROPE_DIM])], axis=1).astype(BF16)
    w_uq_bf = _to_head_lanes(w_uq_full.reshape(Q_LORA_RANK, MLA_HEADS, MLA_QK_DIM)).reshape(
        Q_LORA_RANK, MLA_PAD_WIDTH).astype(BF16)
    w_ukv_heads = w_ukv_full.reshape(KV_LORA_RANK, MLA_HEADS, 2 * MLA_NOPE_DIM)
    w_uk_bf = _nope_to_head_lanes(w_ukv_heads[:, :, :MLA_NOPE_DIM]).reshape(KV_LORA_RANK, MLA_PAD_WIDTH).astype(BF16)
    w_uv_bf = w_ukv_heads[:, :, MLA_NOPE_DIM:].reshape(KV_LORA_RANK, MLA_WIDTH).astype(BF16)
    w_out_bf = w_out_full.astype(BF16)
    qhn_pad, khn_pad = _to_head_lanes(q_head_norm), _to_head_lanes(k_head_norm)
    cos_t, sin_t = rope_tables

    hb, q_sb, k_sb, v_sb, g_sb, c_q, c_kv, g_mla, q_m, k_m, v_m, cqn, ckvn, q0, k0 = _pre_call(
        x2, ada, norm_w, w_in_bf, cos_t, sin_t, q_lora_norm, kv_lora_norm, qhn_pad, khn_pad,
        w_uq_bf, w_uk_bf, w_uv_bf)
    o_sb, r_sb, kstart = _sb_fwd_call(q_sb, k_sb, v_sb)
    o_mla, lse = _mla_fwd_call(q_m, k_m, v_m)
    do_sb, do_mla, dg_sb, dg_mla, dy, g_w_out, d_gate, sq = _out_call(
        o_sb, g_sb, o_mla, g_mla, x2, tgt, ada, w_out_bf)

    dq_sb, dk_sb, dv_sb = _sb_bwd_call(kstart, q_sb, k_sb, v_sb, do_sb, r_sb)
    dq_m, dk_m, dv_m = _mla_bwd_call(q_m, k_m, v_m, do_mla, o_mla, lse)
    (d_cq, d_ckv, d_kr, g_wuq_pad, g_wuk_pad, g_wuv, g_qln, g_kvln, g_qhn, g_khn) = _mla_prep_bwd_call(
        dq_m, dk_m, dv_m, q0, k0, cqn, ckvn, c_q, c_kv, cos_t, sin_t,
        q_lora_norm, kv_lora_norm, qhn_pad, khn_pad, w_uq_bf, w_uk_bf, w_uv_bf)
    grad_x, g_win_pad, d_shift, d_scale, g_norm_w = _dh_call(
        [dq_sb, dk_sb, dv_sb, dg_sb, d_cq, d_ckv, dg_mla, d_kr], hb, x2, dy, ada, norm_w, w_in_bf)

    g_kr = g_win_pad[:, C_KR:]
    g_last = jnp.concatenate([g_win_pad[:, 3 * in_shard:C_GM], g_kr[:, :ROPE_HALF],
                              g_kr[:, HALF_LANES:HALF_LANES + ROPE_HALF], g_win_pad[:, C_GM:C_KR]], axis=1)
    g_w_in = jnp.stack([g_win_pad[:, j * in_shard:(j + 1) * in_shard] for j in range(3)] + [g_last])
    g_w_uq = _from_head_lanes(g_wuq_pad.reshape(Q_LORA_RANK, MLA_HEADS, HEAD_PAD)).reshape(Q_LORA_RANK, -1)
    g_w_ukv = jnp.concatenate(
        [_from_head_lanes(g_wuk_pad.reshape(KV_LORA_RANK, MLA_HEADS, HEAD_PAD))[:, :, :MLA_NOPE_DIM],
         g_wuv.reshape(KV_LORA_RANK, MLA_HEADS, MLA_NOPE_DIM)], axis=2).reshape(KV_LORA_RANK, -1)
    d_ada = jnp.concatenate([d_shift, d_scale, d_gate], axis=1)
    return (jnp.sum(sq), grad_x, g_w_in, g_w_uq, g_w_ukv, g_w_out, d_ada, g_norm_w, g_qln, g_kvln,
            _from_head_lanes(g_qhn), _from_head_lanes(g_khn))
```

```python
import math

import jax
import jax.numpy as jnp
from jax import lax
from jax.experimental import pallas as pl
from jax.experimental.pallas import tpu as pltpu

F32 = jnp.float32
BF16 = jnp.bfloat16
I32 = jnp.int32

D_MODEL = 1024
SB_HEADS = 8
SB_WIDTH = 512
MLA_HEADS = 8
MLA_QK_DIM = 96
MLA_NOPE_DIM = 64
MLA_ROPE_DIM = 32
MLA_WIDTH = 512
Q_LORA_RANK = 384
KV_LORA_RANK = 256
ROPE_THETA = 10000.0
EPS = 1e-6
LANES = 128
HALF_LANES = LANES // 2
HEAD_PAD = 128
MLA_PAD_WIDTH = MLA_HEADS * HEAD_PAD

C_Q, C_K, C_V, C_G = 0, 512, 1024, 1536
C_CQ, C_CKV, C_GM, C_KR = 2048, 2432, 2688, 3200
IN_COLS_PAD = 3328

ADAM_LR = 0.001
ADAM_B1 = 0.9
ADAM_B2 = 0.999
ADAM_EPS = 1e-08
ADAM_WD = 0.01
ADAM_STEP = 10

SB_SCALE = 0.125
SB_GROUP = 4
MLA_SCALE = 1.0 / math.sqrt(MLA_QK_DIM)
LN2 = math.log(2.0)
MLA_SCALE_LOG2 = MLA_SCALE / LN2
MLA_BQ = 1024
MLA_BWD_BQ = 1024
MLA_BK = 1024
MLA_BWD_BK = 1024
MLA_STEP = 512
MLA_BWD_STEP = 256
SB_DEAD = -104.0
MASK_NEG = -1e30

VMEM_LIMIT = 56 * 1024 * 1024
MESH = pl.DeviceIdType.MESH


def _dot(a, b):
    return jnp.dot(a, b, preferred_element_type=F32)


def _dot_nt(a, b):
    return lax.dot_general(a, b, (((1,), (1,)), ((), ())), preferred_element_type=F32)


def _dot_tn(a, b):
    return lax.dot_general(a, b, (((0,), (0,)), ((), ())), preferred_element_type=F32)


def _sigmoid(x):
    return 1.0 / (1.0 + jnp.exp(-x))


def _split_dot(a, m):
    hi = a.astype(BF16)
    lo = (a - hi.astype(F32)).astype(BF16)
    return _dot(hi, m) + _dot(lo, m)


def _params(sem, vmem=None):
    return pltpu.CompilerParams(dimension_semantics=sem, vmem_limit_bytes=vmem)


def _row_tile(s, want):
    return min(want, s)


def _hbm_spec():
    return pl.BlockSpec(memory_space=pltpu.HBM)


def _allgather_rows_call(row):
    def body(in_ref, out_ref, send_sems, recv_sems, loc_sem):
        x, y, c = lax.axis_index("x"), lax.axis_index("y"), lax.axis_index("c")
        flips = [(fx, fy, fc) for fx in (0, 1) for fy in (0, 1) for fc in (0, 1)][1:]

        def peer(r):
            fx, fy, fc = flips[r]
            return ((1 - x) if fx else x, (1 - y) if fy else y, (1 - c) if fc else c)

        def copy(r, slot):
            return pltpu.make_async_remote_copy(
                src_ref=in_ref, dst_ref=out_ref.at[slot], send_sem=send_sems.at[r], recv_sem=recv_sems.at[r],
                device_id=peer(r), device_id_type=MESH)

        local = pltpu.make_async_copy(in_ref, out_ref.at[4 * x + 2 * y + c], loc_sem)
        local.start()
        sends = [copy(r, 4 * x + 2 * y + c) for r in range(7)]
        for cp in sends:
            cp.start()
        for r in range(7):
            px, py, pc = peer(r)
            copy(r, 4 * px + 2 * py + pc).wait_recv()
        for cp in sends:
            cp.wait_send()
        local.wait()

    return pl.pallas_call(
        body, name="gather_rows",
        out_shape=jax.ShapeDtypeStruct((8,) + row.shape, row.dtype),
        in_specs=[_hbm_spec()], out_specs=_hbm_spec(),
        scratch_shapes=[pltpu.SemaphoreType.DMA((7,)), pltpu.SemaphoreType.DMA((7,)), pltpu.SemaphoreType.DMA],
    )(row)


def _gather_call(shards, split, positions, lane_freq):
    n = len(shards)
    halves = [s.shape[1] // 2 for s in shards]
    s_len = positions.shape[0] * LANES
    table_rows = 4 * LANES
    n_chunks = s_len // table_rows
    assert n_chunks * table_rows == s_len and n_chunks >= 2

    def body(*refs):
        ins, pos_ref, frq_ref = refs[:n], refs[n], refs[n + 1]
        outs, cos_out, sin_out = refs[n + 2:2 * n + 2], refs[2 * n + 2], refs[2 * n + 3]
        ici_send, ici_recv, d2d_send, d2d_recv, loc_sems, cos_buf, sin_buf, tab_sems = refs[2 * n + 4:]
        x, y, c = lax.axis_index("x"), lax.axis_index("y"), lax.axis_index("c")
        me = 2 * x + y
        peers = [(1 - x, y), (x, 1 - y), (1 - x, 1 - y)]

        def rows(a, which):
            return pl.ds(pl.multiple_of(which * halves[a], 16), halves[a])

        def ici(a, j, slot):
            px, py = peers[j]
            src, dst = ins[a].at[0], outs[a].at[slot]
            if split[a]:
                src, dst = src.at[rows(a, c)], dst.at[rows(a, c)]
            return pltpu.make_async_remote_copy(
                src_ref=src, dst_ref=dst,
                send_sem=ici_send.at[3 * a + j], recv_sem=ici_recv.at[3 * a + j],
                device_id=(px, py, c), device_id_type=MESH)

        def d2d(a, j, which):
            px, py = peers[j]
            piece = outs[a].at[2 * px + py, rows(a, which)]
            return pltpu.make_async_remote_copy(
                src_ref=piece, dst_ref=piece,
                send_sem=d2d_send.at[3 * a + j], recv_sem=d2d_recv.at[3 * a + j],
                device_id=(x, y, 1 - c), device_id_type=MESH)

        local = [pltpu.make_async_copy(ins[a].at[0], outs[a].at[me], loc_sems.at[a]) for a in range(n)]
        for cp in local:
            cp.start()
        sends = [ici(a, j, me) for a in range(n) for j in range(3)]
        for cp in sends:
            cp.start()

        def table_copies(t, slot):
            rows = pl.ds(pl.multiple_of(t * table_rows, table_rows), table_rows)
            return [pltpu.make_async_copy(buf.at[slot], out.at[rows], tab_sems.at[k, slot])
                    for k, (buf, out) in enumerate(((cos_buf, cos_out), (sin_buf, sin_out)))]

        def tables(t, _):
            slot = t % 2

            @pl.when(t >= 2)
            def _():
                for cp in table_copies(t - 2, slot):
                    cp.wait()

            for b in range(table_rows // LANES):
                lane_pos = pos_ref[pl.ds(t * (table_rows // LANES) + b, 1), :]
                ang = jnp.broadcast_to(lane_pos, (LANES, LANES)).T * frq_ref[...]
                cos_buf[slot, b * LANES:(b + 1) * LANES, :] = jnp.cos(ang)
                sin_buf[slot, b * LANES:(b + 1) * LANES, :] = jnp.sin(ang)
            for cp in table_copies(t, slot):
                cp.start()
            return 0

        lax.fori_loop(0, n_chunks, tables, 0)
        for t in (n_chunks - 2, n_chunks - 1):
            for cp in table_copies(t, t % 2):
                cp.wait()
        for a in range(n):
            for j in range(3):
                px, py = peers[j]
                ici(a, j, 2 * px + py).wait_recv()
                if split[a]:
                    cp = d2d(a, j, c)
                    cp.start()
                    sends.append(cp)
        for a in range(n):
            for j in range(3):
                if split[a]:
                    d2d(a, j, 1 - c).wait_recv()
        for cp in sends:
            cp.wait_send()
        for cp in local:
            cp.wait()

    return pl.pallas_call(
        body, name="gather_weights",
        out_shape=[jax.ShapeDtypeStruct((4,) + s.shape[1:], s.dtype) for s in shards]
        + [jax.ShapeDtypeStruct((s_len, LANES), F32)] * 2,
        in_specs=[_hbm_spec() for _ in shards] + [pl.BlockSpec(memory_space=pltpu.VMEM)] * 2,
        out_specs=[_hbm_spec() for _ in range(n + 2)],
        scratch_shapes=[pltpu.SemaphoreType.DMA((3 * n,)), pltpu.SemaphoreType.DMA((3 * n,)),
                        pltpu.SemaphoreType.DMA((3 * n,)), pltpu.SemaphoreType.DMA((3 * n,)),
                        pltpu.SemaphoreType.DMA((n,)),
                        pltpu.VMEM((2, table_rows, LANES), F32), pltpu.VMEM((2, table_rows, LANES), F32),
                        pltpu.SemaphoreType.DMA((2, 2))],
    )(*shards, positions, lane_freq)


def _swap_halves_call(grads):
    n = len(grads)
    halves = [g.shape[1] // 2 for g in grads]

    def body(*refs):
        g_in, hs, send_sems, recv_sems = refs[:n], refs[n:2 * n], refs[2 * n], refs[2 * n + 1]
        x, y, c = lax.axis_index("x"), lax.axis_index("y"), lax.axis_index("c")
        copies = []
        for a in range(n):
            theirs = pl.ds(pl.multiple_of((1 - c) * halves[a], 16), halves[a])
            copies.append(pltpu.make_async_remote_copy(
                src_ref=g_in[a].at[:, theirs], dst_ref=hs[a], send_sem=send_sems.at[a], recv_sem=recv_sems.at[a],
                device_id=(x, y, 1 - c), device_id_type=MESH))
        for cp in copies:
            cp.start()
        for cp in copies:
            cp.wait()

    return pl.pallas_call(
        body, name="swap_halves",
        out_shape=[jax.ShapeDtypeStruct((4, h, g.shape[2]), g.dtype) for g, h in zip(grads, halves)],
        in_specs=[_hbm_spec() for _ in grads], out_specs=[_hbm_spec() for _ in grads],
        scratch_shapes=[pltpu.SemaphoreType.DMA((n,)), pltpu.SemaphoreType.DMA((n,))],
    )(*grads)


def _sum_halves_call(core, grads, halves):
    n = len(grads)

    def body(core_ref, *refs):
        for g_ref, h_ref, o_ref in zip(refs[:n], refs[n:2 * n], refs[2 * n:]):
            o_ref[...] = (g_ref[...].astype(F32) + h_ref[...].astype(F32)).astype(o_ref.dtype)

    whole = lambda h: pl.BlockSpec(h.shape, lambda i, core_ref: (0, 0, 0))
    return pl.pallas_call(
        body, name="sum_halves",
        grid_spec=pltpu.PrefetchScalarGridSpec(
            num_scalar_prefetch=1, grid=(1,),
            in_specs=[pl.BlockSpec(h.shape, lambda i, core_ref: (0, core_ref[0], 0)) for h in halves]
            + [whole(h) for h in halves],
            out_specs=[whole(h) for h in halves]),
        out_shape=[jax.ShapeDtypeStruct(h.shape, h.dtype) for h in halves],
        compiler_params=_params(("arbitrary",), VMEM_LIMIT),
    )(core, *grads, *halves)


def _exchange_call(chip_halves, small, ada_lane0, c_all, w_ada, m_ada, v_ada):
    n = len(chip_halves)
    halves = [h.shape[1] for h in chip_halves]
    d_model, cols = w_ada.shape
    tr = 256

    def body(*refs):
        g_in, small_in, c_in, wmv_in = refs[:n], refs[n], refs[n + 1], refs[n + 2:n + 5]
        parts, packs, ada_out = refs[n + 5:2 * n + 5], refs[2 * n + 5], refs[2 * n + 6:2 * n + 10]
        (ici_send, ici_recv, d2d_send, d2d_recv, sm_send, sm_recv, loc_sems,
         c_buf, d_buf, wmv_buf, res_buf, ada_sems) = refs[2 * n + 10:]
        x, y, c = lax.axis_index("x"), lax.axis_index("y"), lax.axis_index("c")
        me = 2 * x + y
        me8 = 4 * x + 2 * y + c
        sibling = (x, y, 1 - c)
        peers = [(1 - x, y), (x, 1 - y), (1 - x, 1 - y)]
        flips = [(fx, fy, fc) for fx in (0, 1) for fy in (0, 1) for fc in (0, 1)][1:]

        def rows(a, which):
            return pl.ds(pl.multiple_of(which * halves[a], 16), halves[a])

        def ici(a, j, src_slot, dst_slot):
            px, py = peers[j]
            return pltpu.make_async_remote_copy(
                src_ref=g_in[a].at[src_slot], dst_ref=parts[a].at[dst_slot, rows(a, c)],
                send_sem=ici_send.at[3 * a + j], recv_sem=ici_recv.at[3 * a + j],
                device_id=(px, py, c), device_id_type=MESH)

        def d2d(a, rel, chip, which):
            piece = parts[a].at[chip, rows(a, which)]
            return pltpu.make_async_remote_copy(
                src_ref=piece, dst_ref=piece,
                send_sem=d2d_send.at[4 * a + rel], recv_sem=d2d_recv.at[4 * a + rel],
                device_id=sibling, device_id_type=MESH)

        def flipped(r):
            fx, fy, fc = flips[r]
            return ((1 - x) if fx else x, (1 - y) if fy else y, (1 - c) if fc else c)

        def sm(r, slot):
            return pltpu.make_async_remote_copy(
                src_ref=small_in, dst_ref=packs.at[slot],
                send_sem=sm_send.at[r], recv_sem=sm_recv.at[r],
                device_id=flipped(r), device_id_type=MESH)

        def peer8(r):
            px, py, pc = flipped(r)
            return 4 * px + 2 * py + pc

        def adam_ada():
            lanes = pl.ds(pl.multiple_of(ada_lane0 + me * cols, LANES), cols)
            d_rows = pltpu.make_async_copy(packs.at[:, 0, lanes], d_buf.at[pl.ds(0, 8), :], ada_sems.at[4])
            d_buf[8:, :] = jnp.zeros((8, cols), F32)
            d_rows.start()
            for cp in [d_rows] + ada_loads:
                cp.wait()
            cc = c_buf[...]
            sc = cc * _sigmoid(cc)
            dd = d_buf[...]
            sc_hi = sc.astype(BF16)
            sc_lo = (sc - sc_hi.astype(F32)).astype(BF16)
            dd_hi = dd.astype(BF16)
            dd_lo = (dd - dd_hi.astype(F32)).astype(BF16)
            for t in range(d_model // tr):
                rws = slice(t * tr, (t + 1) * tr)
                g = _dot_tn(sc_hi[:, rws], dd_hi) + (_dot_tn(sc_hi[:, rws], dd_lo) + _dot_tn(sc_lo[:, rws], dd_hi))
                res_buf[0, rws, :] = g
                res_buf[1, rws, :], res_buf[2, rws, :], res_buf[3, rws, :] = _adamw(
                    g, wmv_buf[0, rws, :], wmv_buf[1, rws, :], wmv_buf[2, rws, :])
            stores = [pltpu.make_async_copy(res_buf.at[k], ada_out[k], ada_sems.at[5 + k]) for k in range(4)]
            for cp in stores:
                cp.start()
            return stores

        ada_loads = [pltpu.make_async_copy(c_in, c_buf, ada_sems.at[0])]
        ada_loads += [pltpu.make_async_copy(wmv_in[k], wmv_buf.at[k], ada_sems.at[1 + k]) for k in range(3)]
        for cp in ada_loads:
            cp.start()
        local = [pltpu.make_async_copy(g_in[a].at[me], parts[a].at[me, rows(a, c)], loc_sems.at[a])
                 for a in range(n)]
        local.append(pltpu.make_async_copy(small_in, packs.at[me8], loc_sems.at[n]))
        for cp in local:
            cp.start()
        sends = []
        for r in range(7):
            sends.append(sm(r, me8))
        for a in range(n):
            for j in range(3):
                px, py = peers[j]
                sends.append(ici(a, j, 2 * px + py, me))
        for cp in sends:
            cp.start()
        for a in range(n):
            local[a].wait()
            cp = d2d(a, 0, me, c)
            cp.start()
            sends.append(cp)
        for r in range(7):
            sm(r, peer8(r)).wait_recv()
        local[n].wait()
        ada_stores = adam_ada()
        for a in range(n):
            for j in range(3):
                px, py = peers[j]
                ici(a, j, me, 2 * px + py).wait_recv()
                cp = d2d(a, 1 + j, 2 * px + py, c)
                cp.start()
                sends.append(cp)
        for a in range(n):
            d2d(a, 0, me, 1 - c).wait_recv()
            for j in range(3):
                px, py = peers[j]
                d2d(a, 1 + j, 2 * px + py, 1 - c).wait_recv()
        for cp in sends:
            cp.wait_send()
        for cp in ada_stores:
            cp.wait()

    out_shape = ([jax.ShapeDtypeStruct((4, 2 * h.shape[1], h.shape[2]), h.dtype) for h in chip_halves]
                 + [jax.ShapeDtypeStruct((8,) + small.shape, small.dtype)]
                 + [jax.ShapeDtypeStruct((d_model, cols), F32)] * 4)
    res = pl.pallas_call(
        body, name="exchange_grads",
        out_shape=out_shape,
        in_specs=[_hbm_spec() for _ in range(n + 5)],
        out_specs=[_hbm_spec() for _ in range(n + 5)],
        scratch_shapes=[pltpu.SemaphoreType.DMA((3 * n,)), pltpu.SemaphoreType.DMA((3 * n,)),
                        pltpu.SemaphoreType.DMA((4 * n,)), pltpu.SemaphoreType.DMA((4 * n,)),
                        pltpu.SemaphoreType.DMA((7,)), pltpu.SemaphoreType.DMA((7,)),
                        pltpu.SemaphoreType.DMA((n + 1,)),
                        pltpu.VMEM(c_all.shape, F32), pltpu.VMEM((16, cols), F32),
                        pltpu.VMEM((3, d_model, cols), F32), pltpu.VMEM((4, d_model, cols), F32),
                        pltpu.SemaphoreType.DMA((9,))],
        compiler_params=pltpu.CompilerParams(vmem_limit_bytes=VMEM_LIMIT),
    )(*chip_halves, small, c_all, w_ada, m_ada, v_ada)
    return res[:n], res[n], res[n + 1:]


def _ada_call(c_all, w_ada_cols, b_ada_cols):
    def body(c_ref, w_ref, b_ref, o_ref):
        cc = c_ref[...]
        o_ref[...] = _dot((cc * _sigmoid(cc)).astype(BF16), w_ref[...].astype(BF16)) + b_ref[...]

    return pl.pallas_call(
        body, name="ada_fwd",
        out_shape=jax.ShapeDtypeStruct((c_all.shape[0], w_ada_cols.shape[1]), F32),
        compiler_params=pltpu.CompilerParams(vmem_limit_bytes=VMEM_LIMIT),
    )(c_all, w_ada_cols, b_ada_cols)


def _ada_part(j):
    return pl.BlockSpec((1, D_MODEL), lambda i: (0, j))


def _full(shape):
    return pl.BlockSpec(shape, lambda i: (0,) * len(shape))


def _rows(tm, width):
    return pl.BlockSpec((tm, width), lambda i: (i, 0))


def _rope(t, cos_t, sin_t):
    return t * cos_t + pltpu.roll(t, HALF_LANES, 1) * sin_t


def _rope_adjoint(d, cos_t, sin_t):
    return d * cos_t + pltpu.roll(d * sin_t, HALF_LANES, 1)


def _pre_call(x, ada, norm_w, w_in_bf, cos_t, sin_t, q_lora_norm, kv_lora_norm, qhn_pad, khn_pad,
              w_uq_bf, w_uk_bf, w_uv_bf):
    s = x.shape[0]
    tm = _row_tile(s, 512)
    out_defs = [(D_MODEL, BF16), (512, BF16), (512, BF16), (512, BF16), (512, F32),
                (Q_LORA_RANK, F32), (KV_LORA_RANK, F32), (512, F32),
                (MLA_PAD_WIDTH, BF16), (MLA_PAD_WIDTH, BF16), (MLA_WIDTH, BF16),
                (Q_LORA_RANK, BF16), (KV_LORA_RANK, BF16), (MLA_PAD_WIDTH, F32), (MLA_PAD_WIDTH, F32)]

    def body(x_ref, sh_ref, sc_ref, nw_ref, w_ref, cos_ref, sin_ref, qln_ref, kvln_ref, qhn_ref, khn_ref,
             wuq_ref, wuk_ref, wuv_ref,
             hb_ref, qsb_ref, ksb_ref, vsb_ref, gsb_ref, cq_ref, ckv_ref, gm_ref,
             q_ref, k_ref, v_ref, cqn_ref, ckvn_ref, q0_ref, k0_ref):
        xx = x_ref[...]
        r0 = lax.rsqrt(jnp.mean(xx * xx, axis=-1, keepdims=True) + EPS)
        hb = ((xx * r0 * nw_ref[...]) * (1.0 + sc_ref[...]) + sh_ref[...]).astype(BF16)
        hb_ref[...] = hb

        def proj(c0, width):
            return _dot(hb, w_ref[:, c0:c0 + width])

        cq = proj(C_CQ, Q_LORA_RANK)
        ckv = proj(C_CKV, KV_LORA_RANK)
        kr = proj(C_KR, LANES)
        cq_ref[...] = cq
        ckv_ref[...] = ckv
        cqn = (cq * lax.rsqrt(jnp.mean(cq * cq, axis=-1, keepdims=True) + EPS) * qln_ref[...]).astype(BF16)
        cqn_ref[...] = cqn
        ckvn = (ckv * lax.rsqrt(jnp.mean(ckv * ckv, axis=-1, keepdims=True) + EPS) * kvln_ref[...]).astype(BF16)
        ckvn_ref[...] = ckvn
        qsb_ref[...] = proj(C_Q, 512).astype(BF16)
        v_ref[...] = _dot(ckvn, wuv_ref[...]).astype(BF16)
        q0_ref[...] = _dot(cqn, wuq_ref[...])
        k0_ref[...] = _dot(ckvn, wuk_ref[...])
        ksb_ref[...] = proj(C_K, 512).astype(BF16)
        cos_t, sin_t = cos_ref[...], sin_ref[...]
        heads = [slice(h * HEAD_PAD, (h + 1) * HEAD_PAD) for h in range(MLA_HEADS)]
        for cols in heads:
            k0_ref[:, cols] = k0_ref[:, cols] + kr

        def inv_rms(ref):
            sums = [jnp.sum(ref[:, cols] * ref[:, cols], axis=-1, keepdims=True) for cols in heads]
            return [lax.rsqrt(t * (1.0 / MLA_QK_DIM) + EPS) for t in sums]

        rqs = inv_rms(q0_ref)
        vsb_ref[...] = proj(C_V, 512).astype(BF16)
        rks = inv_rms(k0_ref)
        gsb_ref[...] = proj(C_G, 512)
        for cols, rq, rk in zip(heads, rqs, rks):
            q_ref[:, cols] = (_rope(q0_ref[:, cols] * rq * qhn_ref[...], cos_t, sin_t) * MLA_SCALE_LOG2).astype(BF16)
            k_ref[:, cols] = _rope(k0_ref[:, cols] * rk * khn_ref[...], cos_t, sin_t).astype(BF16)
        gm_ref[...] = proj(C_GM, 512)

    return pl.pallas_call(
        body, name="pre_proj",
        grid=(s // tm,),
        in_specs=[_rows(tm, D_MODEL), _ada_part(0), _ada_part(1), _full((1, D_MODEL)),
                  pl.BlockSpec((D_MODEL, IN_COLS_PAD), lambda i: (0, 0), pipeline_mode=pl.Buffered(1)),
                  _rows(tm, LANES), _rows(tm, LANES),
                  _full((1, Q_LORA_RANK)), _full((1, KV_LORA_RANK)), _full((1, LANES)), _full((1, LANES)),
                  _full((Q_LORA_RANK, MLA_PAD_WIDTH)), _full((KV_LORA_RANK, MLA_PAD_WIDTH)),
                  _full((KV_LORA_RANK, MLA_WIDTH))],
        out_specs=[_rows(tm, w) for w, _ in out_defs],
        out_shape=[jax.ShapeDtypeStruct((s, w), dt) for w, dt in out_defs],
        compiler_params=_params(("arbitrary",), VMEM_LIMIT),
    )(x, ada, ada, norm_w, w_in_bf, cos_t, sin_t, q_lora_norm, kv_lora_norm, qhn_pad, khn_pad,
      w_uq_bf, w_uk_bf, w_uv_bf)


def _log_sigmoid_pair(z):
    ls = jnp.minimum(z, 0.0) - jnp.log(1.0 + jnp.exp(-jnp.abs(z)))
    return ls, ls - z


def _pair(hh):
    return slice((hh // 2) * LANES, (hh // 2 + 1) * LANES)


def _sb_fwd_call(q, k, v):
    s = q.shape[0]
    bq = _row_tile(s, 256)
    nq = s // bq
    nh = SB_GROUP
    width = nh * HALF_LANES

    def body(q_ref, k_ref, v_ref, o_ref, r_ref, ks_ref):
        hp, i = pl.program_id(0), pl.program_id(1)
        lane = lax.broadcasted_iota(I32, (bq, LANES), 1)
        row = lax.broadcasted_iota(I32, (bq, bq), 0)
        col = lax.broadcasted_iota(I32, (bq, bq), 1)
        strict = col < row
        later = jnp.where(row > col, 1.0, 0.0).astype(BF16)
        masks = [_head_mask(lane, hh).astype(BF16) for hh in range(2)]
        qms = [q_ref[:, _pair(hh)] * jnp.asarray(SB_SCALE, BF16) * masks[hh % 2] for hh in range(nh)]

        def walk(blocks, state):
            chains = [(kb, diagonal, hh) for kb, diagonal in blocks for hh in range(nh)]
            keys = lambda kb: pl.ds(pl.multiple_of(kb * bq, bq), bq)
            zs = [_dot_nt(qms[hh], k_ref[keys(kb), _pair(hh)]) for kb, _, hh in chains]
            pairs = []
            for z, (_, diagonal, _) in zip(zs, chains):
                ls, lk = _log_sigmoid_pair(z)
                pairs.append((ls, jnp.where(strict, lk, 0.0) if diagonal else lk))
            sums = [_split_dot(lk, later) for _, lk in pairs]
            runs = [st[0] for st in state]
            ws = []
            for (ls, lk), after, (_, diagonal, hh) in zip(pairs, sums, chains):
                w = jnp.exp(ls + (after + runs[hh]))
                ws.append((jnp.where(strict, w, 0.0) if diagonal else w).astype(BF16))
                runs[hh] = runs[hh] + jnp.sum(lk, axis=1, keepdims=True)
            accs = [st[1] for st in state]
            for w, (kb, _, hh) in zip(ws, chains):
                accs[hh] = accs[hh] + _dot(w, v_ref[keys(kb), _pair(hh)])
            return tuple(zip(runs, accs))

        def alive(state):
            top = jnp.max(state[0][0])
            for st in state[1:]:
                top = jnp.maximum(top, jnp.max(st[0]))
            return (top > SB_DEAD).astype(I32)

        def finish(state, first):
            ks_ref[hp, i] = first
            for pair in range(nh // 2):
                o_ref[:, _pair(2 * pair)] = jnp.where(lane < HALF_LANES, state[2 * pair][1], state[2 * pair + 1][1])
                r_ref[:, _pair(2 * pair)] = jnp.where(lane < HALF_LANES, state[2 * pair][0], state[2 * pair + 1][0])

        zero = ((jnp.zeros((bq, 1), F32), jnp.zeros((bq, LANES), F32)),) * nh

        @pl.when(i == 0)
        def _():
            finish(walk([(0, True)], zero), 0)

        @pl.when(i > 0)
        def _():
            state = walk([(i, True), (i - 1, False)], zero)

            def cond(carry):
                return jnp.logical_and(carry[0] >= 0, carry[1] > 0)

            def step(carry):
                state = walk([(carry[0], False)], carry[2])
                return carry[0] - 1, alive(state), state

            kb, _, state = lax.while_loop(cond, step, (i - 2, alive(state), state))
            finish(state, kb + 1)

    return pl.pallas_call(
        body, name="sb_fwd",
        grid=(SB_HEADS // nh, nq),
        in_specs=[pl.BlockSpec((bq, width), lambda h, i: (i, h)),
                  pl.BlockSpec((s, width), lambda h, i: (0, h)),
                  pl.BlockSpec((s, width), lambda h, i: (0, h))],
        out_specs=[pl.BlockSpec((bq, width), lambda h, i: (i, h)),
                   pl.BlockSpec((bq, width), lambda h, i: (i, h)),
                   pl.BlockSpec(memory_space=pltpu.SMEM)],
        out_shape=[jax.ShapeDtypeStruct((s, SB_WIDTH), F32), jax.ShapeDtypeStruct((s, SB_WIDTH), F32),
                   jax.ShapeDtypeStruct((SB_HEADS // nh, nq), I32)],
        compiler_params=_params(("arbitrary", "arbitrary"), VMEM_LIMIT),
    )(q, k, v)


def _mla_fwd_call(q, k, v):
    s = q.shape[0]
    bq = _row_tile(s, MLA_BQ)
    bk = _row_tile(s, MLA_BK)
    nq = s // bq
    assert bk % bq == 0
    step = min(bk // 2, MLA_STEP)
    nsub = bk // step
    assert nsub % 2 == 0

    def body(q_ref, k_ref, v_ref, o_ref, lse_ref, p_ref, s_ref):
        i = pl.program_id(1)
        lane = lax.broadcasted_iota(I32, (bq, LANES), 1)
        row = lax.broadcasted_iota(I32, (step, bq), 1)
        col = lax.broadcasted_iota(I32, (step, bq), 0)
        n_full = (i * bq) // bk

        def keys(g):
            return pl.ds(pl.multiple_of(g * step, step), step)

        def join(left, right, qlo):
            return right if qlo == 0 else jnp.concatenate([left[:, :qlo], right], axis=1)

        def put_scores(g, slot, qlo=0):
            for hh in range(2):
                cols = slice(hh * HEAD_PAD, (hh + 1) * HEAD_PAD)
                s_ref[slot, hh, :, qlo:] = _dot_nt(k_ref[keys(g), cols], q_ref[qlo:, cols])

        def add_pv(carry, g, slot, qlo=0):
            vblk = v_ref[keys(g), :]
            out = []
            for hh, (m, l, acc, alpha) in enumerate(carry):
                upd = alpha[:, qlo:] * acc[:, qlo:] + _dot_tn(vblk, p_ref[slot, hh, :, qlo:])
                out.append((m, l, join(acc, upd, qlo), alpha))
            return tuple(out)

        def substep(g, slot, carry, masked, prefetch, qlo=0, next_qlo=0, prev_qlo=0, first=False):
            if prefetch:
                put_scores(g + 1, 1 - slot, next_qlo)
            if not first:
                carry = add_pv(carry, g - 1, 1 - slot, prev_qlo)
            new = []
            for hh in range(2):
                m, l, acc, _ = carry[hh]
                sc = s_ref[slot, hh, :, qlo:]
                if masked:
                    sc = jnp.where(col[:, qlo:] + g * step <= row[:, qlo:] + i * bq, sc, MASK_NEG)
                m_new = jnp.maximum(m[:, qlo:], jnp.max(sc, axis=0, keepdims=True))
                p = jnp.exp2(sc - m_new)
                alpha = jnp.exp2(m[:, qlo:] - m_new)
                l_new = alpha * l[:, qlo:] + jnp.sum(p, axis=0, keepdims=True)
                p_ref[slot, hh, :, qlo:] = p.astype(BF16)
                new.append((join(m, m_new, qlo), join(l, l_new, qlo), acc, join(jnp.ones_like(m), alpha, qlo)))
            return tuple(new)

        def first_query(t, masked):
            return t * step if (masked and bk == bq and 0 <= t < nsub) else 0

        def chunk(kb, carry, masked, first=False):
            for t in range(nsub):
                last = masked and t == nsub - 1
                carry = substep(nsub * kb + t, t % 2, carry, masked, not last, first_query(t, masked),
                                first_query(t + 1, masked), first_query(t - 1, masked), first and t == 0)
            return carry

        def finish(carry):
            (m0, l0, a0, _), (m1, l1, a1, _) = add_pv(carry, nsub * n_full + nsub - 1, 1,
                                                      first_query(nsub - 1, True))
            o_ref[...] = jnp.where(lane < HALF_LANES, (a0 / l0).T, (a1 / l1).T)
            sub = lax.broadcasted_iota(I32, (8, bq), 0)
            lse_ref[...] = jnp.where(sub == 0, m0 + jnp.log2(l0), jnp.where(sub == 1, m1 + jnp.log2(l1), 0.0))

        put_scores(0, 0)
        one = (jnp.full((1, bq), MASK_NEG, F32), jnp.zeros((1, bq), F32), jnp.zeros((LANES, bq), F32),
               jnp.ones((1, bq), F32))

        @pl.when(n_full == 0)
        def _():
            finish(chunk(0, (one, one), True, first=True))

        @pl.when(n_full > 0)
        def _():
            carry = chunk(0, (one, one), False, first=True)
            carry = lax.fori_loop(1, n_full, lambda kb, cr: chunk(kb, cr, False), carry)
            finish(chunk(n_full, carry, True))

    return pl.pallas_call(
        body, name="mla_fwd",
        grid=(4, nq),
        in_specs=[pl.BlockSpec((bq, 2 * HEAD_PAD), lambda h, i: (i, h)),
                  pl.BlockSpec((s, 2 * HEAD_PAD), lambda h, i: (0, h)),
                  pl.BlockSpec((s, LANES), lambda h, i: (0, h))],
        out_specs=[pl.BlockSpec((bq, LANES), lambda h, i: (i, h)),
                   pl.BlockSpec((None, 8, bq), lambda h, i: (h, 0, i))],
        out_shape=[jax.ShapeDtypeStruct((s, MLA_WIDTH), F32), jax.ShapeDtypeStruct((4, 8, s), F32)],
        scratch_shapes=[pltpu.VMEM((2, 2, step, bq), BF16), pltpu.VMEM((2, 2, step, bq), F32)],
        compiler_params=_params(("arbitrary", "arbitrary"), VMEM_LIMIT),
    )(q, k, v)


def _out_call(o_sb, g_sb, o_mla, g_mla, x, target, ada, w_out_bf):
    s = x.shape[0]
    tm = _row_tile(s, 512)

    def body(osb_ref, gsb_ref, oml_ref, gml_ref, x_ref, t_ref, gate_ref, w_ref,
             dosb_ref, doml_ref, dgsb_ref, dgml_ref, dy_ref, gw_ref, dgate_ref, sq_ref):
        @pl.when(pl.program_id(0) == 0)
        def _():
            gw_ref[...] = jnp.zeros_like(gw_ref)
            dgate_ref[...] = jnp.zeros_like(dgate_ref)
            sq_ref[...] = jnp.zeros_like(sq_ref)

        g_s, g_m = gsb_ref[...], gml_ref[...]
        sig_s, sig_m = _sigmoid(g_s), _sigmoid(g_m)
        silu_s, silu_m = g_s * sig_s, g_m * sig_m
        o_s, o_m = osb_ref[...], oml_ref[...]
        mixed = jnp.concatenate([o_s * silu_s, o_m * silu_m], axis=1).astype(BF16)
        u = _dot(mixed, w_ref[...])
        gate_v = gate_ref[...]
        err = x_ref[...] + gate_v * u - t_ref[...]
        sq_ref[...] += jnp.sum(err * err, axis=0, keepdims=True)
        dy = err * (1.0 / D_MODEL)
        dy_ref[...] = dy
        dgate_ref[...] += jnp.sum(dy * u, axis=0, keepdims=True)
        du = (dy * gate_v).astype(BF16)
        gw_ref[...] += _dot_tn(mixed, du)
        dmix = _dot_nt(du, w_ref[...])
        dm_s, dm_m = dmix[:, :SB_WIDTH], dmix[:, SB_WIDTH:]
        dosb_ref[...] = (dm_s * silu_s).astype(BF16)
        doml_ref[...] = (dm_m * silu_m).astype(BF16)
        dgsb_ref[...] = (dm_s * o_s * (sig_s * (1.0 + g_s * (1.0 - sig_s)))).astype(BF16)
        dgml_ref[...] = (dm_m * o_m * (sig_m * (1.0 + g_m * (1.0 - sig_m)))).astype(BF16)

    return pl.pallas_call(
        body, name="out_proj_loss",
        grid=(s // tm,),
        in_specs=[_rows(tm, 512), _rows(tm, 512), _rows(tm, 512), _rows(tm, 512),
                  _rows(tm, D_MODEL), _rows(tm, D_MODEL), _ada_part(2), _full((D_MODEL, D_MODEL))],
        out_specs=[_rows(tm, 512), _rows(tm, 512), _rows(tm, 512), _rows(tm, 512), _rows(tm, D_MODEL),
                   _full((D_MODEL, D_MODEL)), _full((1, D_MODEL)), _full((1, D_MODEL))],
        out_shape=[jax.ShapeDtypeStruct((s, 512), BF16)] * 4
        + [jax.ShapeDtypeStruct((s, D_MODEL), F32), jax.ShapeDtypeStruct((D_MODEL, D_MODEL), F32),
           jax.ShapeDtypeStruct((1, D_MODEL), F32), jax.ShapeDtypeStruct((1, D_MODEL), F32)],
        compiler_params=_params(("arbitrary",), VMEM_LIMIT),
    )(o_sb, g_sb, o_mla, g_mla, x, target, ada, w_out_bf)


def _head_mask(lane, hh):
    return jnp.where((lane >= HALF_LANES) if hh else (lane < HALF_LANES), 1.0, 0.0)


def _pick_lane(packed, lane, which):
    return jnp.sum(jnp.where(lane == which, packed, 0.0), axis=1, keepdims=True)


def _sb_bwd_call(kstart, q, k, v, do, rfin):
    s = q.shape[0]
    bq = _row_tile(s, 256)
    nq = s // bq
    nh = SB_GROUP
    width = nh * HALF_LANES

    def body(ks_ref, q_ref, k_ref, v_ref, do_ref, r_ref, dq_ref, dk_ref, dv_ref):
        hp, i = pl.program_id(0), pl.program_id(1)

        @pl.when(i == 0)
        def _():
            dk_ref[...] = jnp.zeros_like(dk_ref)
            dv_ref[...] = jnp.zeros_like(dv_ref)

        lane = lax.broadcasted_iota(I32, (bq, LANES), 1)
        row = lax.broadcasted_iota(I32, (bq, bq), 0)
        col = lax.broadcasted_iota(I32, (bq, bq), 1)
        upto = jnp.where(row <= col, 1.0, 0.0).astype(BF16)
        before = jnp.where(row < col, 1.0, 0.0).astype(BF16)
        masks = [_head_mask(lane, hh).astype(BF16) for hh in range(2)]
        qms = [q_ref[:, _pair(hh)] * jnp.asarray(SB_SCALE, BF16) * masks[hh % 2] for hh in range(nh)]
        doms = [do_ref[:, _pair(hh)] * masks[hh % 2] for hh in range(nh)]
        totals = [_pick_lane(r_ref[:, _pair(hh)], lane, HALF_LANES * (hh % 2)) for hh in range(nh)]
        strict = col < row

        def walk(blocks, state):
            chains = [(kb, diagonal, hh) for kb, diagonal in blocks for hh in range(nh)]
            keys = lambda kb: pl.ds(pl.multiple_of(kb * bq, bq), bq)
            cut = lambda x, diagonal: jnp.where(strict, x, 0.0) if diagonal else x
            zs = [_dot_nt(qms[hh], k_ref[keys(kb), _pair(hh)]) for kb, _, hh in chains]
            dws = [_dot_nt(doms[hh], v_ref[keys(kb), _pair(hh)]) for kb, _, hh in chains]
            pairs = []
            for z, (_, diagonal, _) in zip(zs, chains):
                ls, lk = _log_sigmoid_pair(z)
                pairs.append((ls, cut(lk, diagonal)))
            incls = [_split_dot(lk, upto) for _, lk in pairs]
            pres = [st[0] for st in state]
            ws, gs = [], []
            for (ls, lk), incl, dw, (_, diagonal, hh) in zip(pairs, incls, dws, chains):
                w = cut(jnp.exp(ls + ((totals[hh] - pres[hh]) - incl)), diagonal)
                ws.append(w.astype(BF16))
                gs.append(w * dw)
                pres[hh] = pres[hh] + jnp.sum(lk, axis=1, keepdims=True)
            gsums = [_dot(g.astype(BF16), before) for g in gs]
            gpres = [st[1] for st in state]
            dzs = []
            for (ls, _), g, gsum, (_, diagonal, hh) in zip(pairs, gs, gsums, chains):
                dzs.append(cut(g - jnp.exp(ls) * (g + (gpres[hh] + gsum)), diagonal).astype(BF16))
                gpres[hh] = gpres[hh] + jnp.sum(g, axis=1, keepdims=True)
            dqs = [st[2] for st in state]
            dk_parts, dv_parts = [], []
            for dzb, w, (kb, _, hh) in zip(dzs, ws, chains):
                dk_parts.append(_dot_tn(dzb, qms[hh]))
                dv_parts.append(_dot_tn(w, doms[hh]))
                dqs[hh] = dqs[hh] + _dot(dzb, k_ref[keys(kb), _pair(hh)])
            for b, (kb, _) in enumerate(blocks):
                for pair in range(nh // 2):
                    c0 = b * nh + 2 * pair
                    dk_ref[keys(kb), _pair(2 * pair)] += dk_parts[c0] + dk_parts[c0 + 1]
                    dv_ref[keys(kb), _pair(2 * pair)] += dv_parts[c0] + dv_parts[c0 + 1]
            return tuple(zip(pres, gpres, dqs))

        def finish(state):
            for pair in range(nh // 2):
                both = jnp.where(lane < HALF_LANES, state[2 * pair][2], state[2 * pair + 1][2])
                dq_ref[:, _pair(2 * pair)] = (both * SB_SCALE).astype(BF16)

        zero = ((jnp.zeros((bq, 1), F32), jnp.zeros((bq, 1), F32), jnp.zeros((bq, LANES), F32)),) * nh

        @pl.when(i == 0)
        def _():
            finish(walk([(0, True)], zero))

        @pl.when(i > 0)
        def _():
            state = lax.fori_loop(ks_ref[hp, i], i - 1, lambda kb, st: walk([(kb, False)], st), zero)
            finish(walk([(i - 1, False), (i, True)], state))

    return pl.pallas_call(
        body, name="sb_bwd",
        grid_spec=pltpu.PrefetchScalarGridSpec(
            num_scalar_prefetch=1, grid=(SB_HEADS // nh, nq),
            in_specs=[pl.BlockSpec((bq, width), lambda h, i, ks: (i, h)),
                      pl.BlockSpec((s, width), lambda h, i, ks: (0, h), pipeline_mode=pl.Buffered(1)),
                      pl.BlockSpec((s, width), lambda h, i, ks: (0, h), pipeline_mode=pl.Buffered(1)),
                      pl.BlockSpec((bq, width), lambda h, i, ks: (i, h)),
                      pl.BlockSpec((bq, width), lambda h, i, ks: (i, h))],
            out_specs=[pl.BlockSpec((bq, width), lambda h, i, ks: (i, h)),
                       pl.BlockSpec((s, width), lambda h, i, ks: (0, h), pipeline_mode=pl.Buffered(1)),
                       pl.BlockSpec((s, width), lambda h, i, ks: (0, h), pipeline_mode=pl.Buffered(1))]),
        out_shape=[jax.ShapeDtypeStruct((s, SB_WIDTH), BF16), jax.ShapeDtypeStruct((s, SB_WIDTH), F32),
                   jax.ShapeDtypeStruct((s, SB_WIDTH), F32)],
        compiler_params=_params(("arbitrary", "arbitrary"), VMEM_LIMIT),
    )(kstart, q, k, v, do, rfin)


def _mla_bwd_call(q, k, v, do, o, lse):
    s = q.shape[0]
    bq = _row_tile(s, MLA_BWD_BQ)
    bk = _row_tile(s, MLA_BWD_BK)
    nq = s // bq
    assert bk % bq == 0
    step = min(bk // 2, MLA_BWD_STEP)
    nsub = bk // step
    assert nsub % 2 == 0

    def body(q_ref, k_ref, v_ref, do_ref, o_ref, lse_ref, dq_ref, dk_ref, dv_ref, dom_ref, s_ref, dp_ref, pb_ref,
             ds_ref):
        i = pl.program_id(1)

        @pl.when(i == 0)
        def _():
            dk_ref[...] = jnp.zeros_like(dk_ref)
            dv_ref[...] = jnp.zeros_like(dv_ref)

        lane = lax.broadcasted_iota(I32, (bq, LANES), 1)
        row = lax.broadcasted_iota(I32, (step, bq), 1)
        col = lax.broadcasted_iota(I32, (step, bq), 0)
        n_full = (i * bq) // bk
        do2 = do_ref[...]
        prod = do2.astype(F32) * o_ref[...]
        ones = jnp.ones((8, LANES), BF16)
        deltas, lses = [], []
        for hh in range(2):
            head = _head_mask(lane, hh)
            dom_ref[hh] = do2 * head.astype(BF16)
            part = prod * head
            hi = part.astype(BF16)
            lo = (part - hi.astype(F32)).astype(BF16)
            deltas.append((_dot_nt(ones, hi) + _dot_nt(ones, lo))[0:1])
            lses.append(lse_ref[hh:hh + 1, :])

        def keys(g):
            return pl.ds(pl.multiple_of(g * step, step), step)

        def heads():
            return [(hh, slice(hh * HEAD_PAD, (hh + 1) * HEAD_PAD)) for hh in range(2)]

        def put_products(g, slot, qlo=0):
            vblk = v_ref[keys(g), :]
            for hh, cols in heads():
                s_ref[slot, hh, :, qlo:] = _dot_nt(k_ref[keys(g), cols], q_ref[qlo:, cols])
                dp_ref[slot, hh, :, qlo:] = _dot_nt(vblk, dom_ref[hh, qlo:, :])

        def add_grads(dqs, g, slot, qlo=0):
            rows = keys(g)
            new, dv_parts = [], []
            for hh, cols in heads():
                ds = ds_ref[slot, hh, :, qlo:]
                dk_ref[rows, cols] += _dot(ds, q_ref[qlo:, cols])
                dv_parts.append(_dot(pb_ref[slot, hh, :, qlo:], dom_ref[hh, qlo:, :]))
                upd = dqs[hh][:, qlo:] + _dot_tn(k_ref[rows, cols], ds)
                new.append(upd if qlo == 0 else jnp.concatenate([dqs[hh][:, :qlo], upd], axis=1))
            dv_ref[rows, :] += dv_parts[0] + dv_parts[1]
            return tuple(new)

        def substep(g, slot, dqs, masked, prefetch, qlo=0, next_qlo=0, prev_qlo=0, first=False):
            if prefetch:
                put_products(g + 1, 1 - slot, next_qlo)
            if not first:
                dqs = add_grads(dqs, g - 1, 1 - slot, prev_qlo)
            for hh, _ in heads():
                p = jnp.exp2(s_ref[slot, hh, :, qlo:] - lses[hh][:, qlo:])
                if masked:
                    p = jnp.where(col[:, qlo:] + g * step <= row[:, qlo:] + i * bq, p, 0.0)
                ds_ref[slot, hh, :, qlo:] = (p * (dp_ref[slot, hh, :, qlo:] - deltas[hh][:, qlo:])).astype(BF16)
                pb_ref[slot, hh, :, qlo:] = p.astype(BF16)
            return dqs

        def first_query(t, masked):
            return t * step if (masked and bk == bq and 0 <= t < nsub) else 0

        def chunk(kb, dqs, masked, first=False):
            for t in range(nsub):
                last = masked and t == nsub - 1
                dqs = substep(nsub * kb + t, t % 2, dqs, masked, not last, first_query(t, masked),
                              first_query(t + 1, masked), first_query(t - 1, masked), first and t == 0)
            return dqs

        def finish(dqs):
            dqs = add_grads(dqs, nsub * n_full + nsub - 1, 1, first_query(nsub - 1, True))
            dq_ref[:, :HEAD_PAD] = dqs[0].T * MLA_SCALE
            dq_ref[:, HEAD_PAD:] = dqs[1].T * MLA_SCALE

        put_products(0, 0)
        zero = jnp.zeros((HEAD_PAD, bq), F32)

        @pl.when(n_full == 0)
        def _():
            finish(chunk(0, (zero, zero), True, first=True))

        @pl.when(n_full > 0)
        def _():
            dqs = chunk(0, (zero, zero), False, first=True)
            dqs = lax.fori_loop(1, n_full, lambda kb, dqs: chunk(kb, dqs, False), dqs)
            finish(chunk(n_full, dqs, True))

    return pl.pallas_call(
        body, name="mla_bwd",
        grid=(4, nq),
        in_specs=[pl.BlockSpec((bq, 2 * HEAD_PAD), lambda h, i: (i, h)),
                  pl.BlockSpec((s, 2 * HEAD_PAD), lambda h, i: (0, h)),
                  pl.BlockSpec((s, LANES), lambda h, i: (0, h)),
                  pl.BlockSpec((bq, LANES), lambda h, i: (i, h)),
                  pl.BlockSpec((bq, LANES), lambda h, i: (i, h)),
                  pl.BlockSpec((None, 8, bq), lambda h, i: (h, 0, i))],
        out_specs=[pl.BlockSpec((bq, 2 * HEAD_PAD), lambda h, i: (i, h)),
                   pl.BlockSpec((s, 2 * HEAD_PAD), lambda h, i: (0, h), pipeline_mode=pl.Buffered(1)),
                   pl.BlockSpec((s, LANES), lambda h, i: (0, h), pipeline_mode=pl.Buffered(1))],
        out_shape=[jax.ShapeDtypeStruct((s, MLA_PAD_WIDTH), F32), jax.ShapeDtypeStruct((s, MLA_PAD_WIDTH), F32),
                   jax.ShapeDtypeStruct((s, MLA_WIDTH), F32)],
        scratch_shapes=[pltpu.VMEM((2, bq, LANES), BF16),
                        pltpu.VMEM((2, 2, step, bq), F32), pltpu.VMEM((2, 2, step, bq), F32),
                        pltpu.VMEM((2, 2, step, bq), BF16), pltpu.VMEM((2, 2, step, bq), BF16)],
        compiler_params=_params(("arbitrary", "arbitrary"), VMEM_LIMIT),
    )(q, k, v, do, o, lse)


def _rms_bwd(d_out, inp, r, weight, n):
    normed = inp * r
    gw = d_out * weight
    d_in = r * (gw - normed * (jnp.sum(gw * normed, axis=-1, keepdims=True) * (1.0 / n)))
    return d_in, d_out * normed


def _mla_prep_bwd_call(dq, dk, dv, q0, k0, cqn, ckvn, c_q, c_kv, cos_t, sin_t,
                       q_lora_norm, kv_lora_norm, qhn_pad, khn_pad, w_uq_bf, w_uk_bf, w_uv_bf):
    s = dq.shape[0]
    tm = _row_tile(s, 512)

    def body(dq_ref, dk_ref, dv_ref, q0_ref, k0_ref, cqn_ref, ckvn_ref, cq_ref, ckv_ref,
             cos_ref, sin_ref, qln_ref, kvln_ref, qhn_ref, khn_ref, wuq_ref, wuk_ref, wuv_ref,
             dcq_ref, dckv_ref, dkr_ref, gwuq_ref, gwuk_ref, gwuv_ref, gqln_ref, gkvln_ref, gqhn_ref, gkhn_ref,
             dq0_ref, dk0_ref, tmp_ref):
        @pl.when(pl.program_id(0) == 0)
        def _():
            for ref in (gwuq_ref, gwuk_ref, gwuv_ref, gqln_ref, gkvln_ref, gqhn_ref, gkhn_ref):
                ref[...] = jnp.zeros_like(ref)

        cos_t, sin_t = cos_ref[...], sin_ref[...]
        lane = lax.broadcasted_iota(I32, (tm, LANES), 1)
        rope_lanes = jnp.logical_or(lane < ROPE_HALF,
                                    jnp.logical_and(lane >= HALF_LANES, lane < HALF_LANES + ROPE_HALF))
        heads = [slice(h * HEAD_PAD, (h + 1) * HEAD_PAD) for h in range(MLA_HEADS)]

        def head_norm_bwd(d_ref, x0_ref, w_ref, out_ref, scale):
            w = w_ref[...]
            inv = [lax.rsqrt(jnp.sum(x0_ref[:, cols] * x0_ref[:, cols], axis=-1, keepdims=True)
                             * (1.0 / MLA_QK_DIM) + EPS) for cols in heads]
            for cols in heads:
                tmp_ref[:, cols] = _rope_adjoint(d_ref[:, cols] * scale, cos_t, sin_t)
            dots = [jnp.sum(tmp_ref[:, cols] * w * (x0_ref[:, cols] * r), axis=-1, keepdims=True)
                    for cols, r in zip(heads, inv)]
            g_w = jnp.zeros((1, LANES), F32)
            rope_sum = jnp.zeros((tm, LANES), F32)
            for cols, r, dot in zip(heads, inv, dots):
                normed = x0_ref[:, cols] * r
                d_n = tmp_ref[:, cols]
                d_x0 = r * (d_n * w - normed * (dot * (1.0 / MLA_QK_DIM)))
                out_ref[:, cols] = d_x0.astype(BF16)
                g_w = g_w + jnp.sum(d_n * normed, axis=0, keepdims=True)
                rope_sum = rope_sum + jnp.where(rope_lanes, d_x0, 0.0)
            return g_w, rope_sum

        g_qhn, _ = head_norm_bwd(dq_ref, q0_ref, qhn_ref, dq0_ref, 1.0)
        g_khn, d_kr = head_norm_bwd(dk_ref, k0_ref, khn_ref, dk0_ref, LN2)
        cqn, ckvn = cqn_ref[...], ckvn_ref[...]
        d_q0b, d_k0b, dvb = dq0_ref[...], dk0_ref[...], dv_ref[...].astype(BF16)
        d_cqn = _dot_nt(d_q0b, wuq_ref[...])
        gwuq_ref[...] += _dot_tn(cqn, d_q0b)
        d_ckvn = _dot_nt(d_k0b, wuk_ref[...]) + _dot_nt(dvb, wuv_ref[...])
        gwuk_ref[...] += _dot_tn(ckvn, d_k0b)
        gwuv_ref[...] += _dot_tn(ckvn, dvb)
        gqhn_ref[...] += g_qhn
        gkhn_ref[...] += g_khn
        dkr_ref[...] = d_kr.astype(BF16)
        cq = cq_ref[...]
        rcq = lax.rsqrt(jnp.mean(cq * cq, axis=-1, keepdims=True) + EPS)
        d_cq, gl = _rms_bwd(d_cqn, cq, rcq, qln_ref[...], Q_LORA_RANK)
        dcq_ref[...] = d_cq.astype(BF16)
        gqln_ref[...] += jnp.sum(gl, axis=0, keepdims=True)
        ckv = ckv_ref[...]
        rckv = lax.rsqrt(jnp.mean(ckv * ckv, axis=-1, keepdims=True) + EPS)
        d_ckv, gl = _rms_bwd(d_ckvn, ckv, rckv, kvln_ref[...], KV_LORA_RANK)
        dckv_ref[...] = d_ckv.astype(BF16)
        gkvln_ref[...] += jnp.sum(gl, axis=0, keepdims=True)

    return pl.pallas_call(
        body, name="mla_prep_bwd",
        grid=(s // tm,),
        in_specs=[_rows(tm, MLA_PAD_WIDTH), _rows(tm, MLA_PAD_WIDTH), _rows(tm, MLA_WIDTH),
                  _rows(tm, MLA_PAD_WIDTH), _rows(tm, MLA_PAD_WIDTH),
                  _rows(tm, Q_LORA_RANK), _rows(tm, KV_LORA_RANK), _rows(tm, Q_LORA_RANK), _rows(tm, KV_LORA_RANK),
                  _rows(tm, LANES), _rows(tm, LANES),
                  _full((1, Q_LORA_RANK)), _full((1, KV_LORA_RANK)), _full((1, LANES)), _full((1, LANES)),
                  _full((Q_LORA_RANK, MLA_PAD_WIDTH)), _full((KV_LORA_RANK, MLA_PAD_WIDTH)),
                  _full((KV_LORA_RANK, MLA_WIDTH))],
        out_specs=[_rows(tm, Q_LORA_RANK), _rows(tm, KV_LORA_RANK), _rows(tm, LANES),
                   _full((Q_LORA_RANK, MLA_PAD_WIDTH)), _full((KV_LORA_RANK, MLA_PAD_WIDTH)),
                   _full((KV_LORA_RANK, MLA_WIDTH)),
                   _full((1, Q_LORA_RANK)), _full((1, KV_LORA_RANK)), _full((1, LANES)), _full((1, LANES))],
        out_shape=[jax.ShapeDtypeStruct((s, Q_LORA_RANK), BF16), jax.ShapeDtypeStruct((s, KV_LORA_RANK), BF16),
                   jax.ShapeDtypeStruct((s, LANES), BF16),
                   jax.ShapeDtypeStruct((Q_LORA_RANK, MLA_PAD_WIDTH), F32),
                   jax.ShapeDtypeStruct((KV_LORA_RANK, MLA_PAD_WIDTH), F32),
                   jax.ShapeDtypeStruct((KV_LORA_RANK, MLA_WIDTH), F32),
                   jax.ShapeDtypeStruct((1, Q_LORA_RANK), F32), jax.ShapeDtypeStruct((1, KV_LORA_RANK), F32),
                   jax.ShapeDtypeStruct((1, LANES), F32), jax.ShapeDtypeStruct((1, LANES), F32)],
        scratch_shapes=[pltpu.VMEM((tm, MLA_PAD_WIDTH), BF16), pltpu.VMEM((tm, MLA_PAD_WIDTH), BF16),
                        pltpu.VMEM((tm, MLA_PAD_WIDTH), F32)],
        compiler_params=_params(("arbitrary",), VMEM_LIMIT),
    )(dq, dk, dv, q0, k0, cqn, ckvn, c_q, c_kv, cos_t, sin_t,
      q_lora_norm, kv_lora_norm, qhn_pad, khn_pad, w_uq_bf, w_uk_bf, w_uv_bf)


def _dh_call(pieces, hb, x, dy, ada, norm_w, w_in_bf):
    s = x.shape[0]
    tm = _row_tile(s, 512)
    widths = [p.shape[1] for p in pieces]
    offsets = [sum(widths[:j]) for j in range(len(widths))]
    assert offsets[-1] + widths[-1] == IN_COLS_PAD
    n = len(pieces)

    def body(*refs):
        p_refs = refs[:n]
        (hb_ref, x_ref, dy_ref, sh_ref, sc_ref, nw_ref, w_ref, gx_ref, gw_ref, dsh_ref, dsc_ref, gnw_ref,
         dp_ref, acc_ref) = refs[n:]

        @pl.when(pl.program_id(0) == 0)
        def _():
            acc_ref[...] = jnp.zeros_like(acc_ref)
            dsh_ref[...] = jnp.zeros_like(dsh_ref)
            dsc_ref[...] = jnp.zeros_like(dsc_ref)
            gnw_ref[...] = jnp.zeros_like(gnw_ref)

        for p_ref, c0, width in zip(p_refs, offsets, widths):
            dp_ref[:, c0:c0 + width] = p_ref[...].astype(BF16)
        acc_ref[...] += _dot_tn(hb_ref[...], dp_ref[...])

        @pl.when(pl.program_id(0) == pl.num_programs(0) - 1)
        def _():
            gw_ref[...] = acc_ref[...].astype(BF16)

        dh = _dot_nt(dp_ref[...], w_ref[...])
        xx = x_ref[...]
        r0 = lax.rsqrt(jnp.mean(xx * xx, axis=-1, keepdims=True) + EPS)
        xn = xx * r0
        nw = nw_ref[...]
        dsh_ref[...] += jnp.sum(dh, axis=0, keepdims=True)
        dsc_ref[...] += jnp.sum(dh * (xn * nw), axis=0, keepdims=True)
        dn = dh * (1.0 + sc_ref[...])
        gnw_ref[...] += jnp.sum(dn * xn, axis=0, keepdims=True)
        dxn = dn * nw
        gx_ref[...] = dy_ref[...] + r0 * (dxn - xn * jnp.mean(dxn * xn, axis=-1, keepdims=True))

    return pl.pallas_call(
        body, name="in_proj_bwd",
        grid=(s // tm,),
        in_specs=[_rows(tm, w) for w in widths]
        + [_rows(tm, D_MODEL), _rows(tm, D_MODEL), _rows(tm, D_MODEL),
           _ada_part(0), _ada_part(1), _full((1, D_MODEL)),
           pl.BlockSpec((D_MODEL, IN_COLS_PAD), lambda i: (0, 0), pipeline_mode=pl.Buffered(1))],
        out_specs=[_rows(tm, D_MODEL),
                   pl.BlockSpec((D_MODEL, IN_COLS_PAD), lambda i: (0, 0), pipeline_mode=pl.Buffered(1)),
                   _full((1, D_MODEL)), _full((1, D_MODEL)), _full((1, D_MODEL))],
        out_shape=[jax.ShapeDtypeStruct((s, D_MODEL), F32), jax.ShapeDtypeStruct((D_MODEL, IN_COLS_PAD), BF16),
                   jax.ShapeDtypeStruct((1, D_MODEL), F32), jax.ShapeDtypeStruct((1, D_MODEL), F32),
                   jax.ShapeDtypeStruct((1, D_MODEL), F32)],
        scratch_shapes=[pltpu.VMEM((tm, IN_COLS_PAD), BF16), pltpu.VMEM((D_MODEL, IN_COLS_PAD), F32)],
        compiler_params=_params(("arbitrary",), VMEM_LIMIT),
    )(*pieces, hb, x, dy, ada, ada, norm_w, w_in_bf)


def _adamw(g, w, m, v):
    m = ADAM_B1 * m + (1.0 - ADAM_B1) * g
    v = ADAM_B2 * v + (1.0 - ADAM_B2) * (g * g)
    m_hat = m / (1.0 - ADAM_B1 ** ADAM_STEP)
    v_hat = v / (1.0 - ADAM_B2 ** ADAM_STEP)
    delta = -ADAM_LR * (m_hat / (jnp.sqrt(v_hat) + ADAM_EPS) + ADAM_WD * w)
    return delta, m, v


def _adam_shard_call(name, parts, w, m, v):
    r, c = w.shape
    tr = r if r <= 512 else 256

    def body(p_ref, w_ref, m_ref, v_ref, g_ref, d_ref, nm_ref, nv_ref):
        g = ((p_ref[0].astype(F32) + p_ref[1].astype(F32)) + p_ref[2].astype(F32)) + p_ref[3].astype(F32)
        g_ref[...] = g
        d_ref[...], nm_ref[...], nv_ref[...] = _adamw(g, w_ref[...], m_ref[...], v_ref[...])

    blk = pl.BlockSpec((tr, c), lambda i: (i, 0))
    return pl.pallas_call(
        body, name=name,
        grid=(r // tr,),
        in_specs=[pl.BlockSpec((4, tr, c), lambda i: (0, i, 0)), blk, blk, blk],
        out_specs=[blk] * 4,
        out_shape=[jax.ShapeDtypeStruct((r, c), F32)] * 4,
        compiler_params=_params(("arbitrary",), VMEM_LIMIT),
    )(parts, w, m, v)


def _adam_shard_transposed_call(name, parts, w_t, m_t, v_t):
    c, r = w_t.shape
    assert parts.shape[1:] == (r, c)
    tr = r if r <= 512 else 256
    c_pad = -(-c // LANES) * LANES

    def body(p_ref, w_ref, m_ref, v_ref, g_ref, d_ref, nm_ref, nv_ref, pad_ref):
        @pl.when(pl.program_id(0) == 0)
        def _():
            pad_ref[...] = jnp.zeros_like(pad_ref)

        pad_ref[:, :c] = ((p_ref[0].astype(F32) + p_ref[1].astype(F32)) + p_ref[2].astype(F32)) + p_ref[3].astype(F32)
        g = pad_ref[...].T[:c]
        g_ref[...] = g
        d_ref[...], nm_ref[...], nv_ref[...] = _adamw(g, w_ref[...], m_ref[...], v_ref[...])

    blk = pl.BlockSpec((c, tr), lambda i: (0, i))
    return pl.pallas_call(
        body, name=name,
        grid=(r // tr,),
        in_specs=[pl.BlockSpec((4, tr, c), lambda i: (0, i, 0)), blk, blk, blk],
        out_specs=[blk] * 4,
        out_shape=[jax.ShapeDtypeStruct((c, r), F32)] * 4,
        scratch_shapes=[pltpu.VMEM((tr, c_pad), F32)],
        compiler_params=_params(("arbitrary",), VMEM_LIMIT),
    )(parts, w_t, m_t, v_t)


def _adam_vectors_call(packs, offsets, vectors):
    nv = len(vectors)

    def body(*refs):
        p_ref, ins, outs = refs[0], refs[1:1 + 3 * nv], refs[1 + 3 * nv:]
        for j, off in enumerate(offsets):
            n = ins[3 * j].shape[1]
            span = -(-n // LANES) * LANES
            g = p_ref[0, :, off:off + span]
            for b in range(1, 8):
                g = g + p_ref[b, :, off:off + span]
            g = g[:, :n]
            outs[j][...] = g
            outs[nv + j][...], outs[2 * nv + j][...], outs[3 * nv + j][...] = _adamw(
                g, ins[3 * j][...], ins[3 * j + 1][...], ins[3 * j + 2][...])

    flat = [a for t in vectors for a in t]
    res = pl.pallas_call(
        body, name="adam_vectors",
        out_shape=[jax.ShapeDtypeStruct(t[0].shape, F32) for _ in range(4) for t in vectors],
    )(packs, *flat)
    return [res[k * nv:(k + 1) * nv] for k in range(4)]


ROPE_HALF = MLA_ROPE_DIM // 2
NOPE_A = MLA_NOPE_DIM - ROPE_HALF


def _zeros_like_lanes(t, n):
    return jnp.zeros(t.shape[:-1] + (n,), t.dtype)


def _to_head_lanes(t):
    nope, rope = t[..., :MLA_NOPE_DIM], t[..., MLA_NOPE_DIM:]
    return jnp.concatenate([rope[..., :ROPE_HALF], nope[..., :NOPE_A], rope[..., ROPE_HALF:], nope[..., NOPE_A:],
                            _zeros_like_lanes(t, HEAD_PAD - MLA_QK_DIM)], axis=-1)


def _from_head_lanes(t):
    return jnp.concatenate([t[..., ROPE_HALF:HALF_LANES], t[..., HALF_LANES + ROPE_HALF:MLA_QK_DIM],
                            t[..., :ROPE_HALF], t[..., HALF_LANES:HALF_LANES + ROPE_HALF]], axis=-1)


def _nope_to_head_lanes(t):
    return jnp.concatenate([_zeros_like_lanes(t, ROPE_HALF), t[..., :NOPE_A], _zeros_like_lanes(t, ROPE_HALF),
                            t[..., NOPE_A:], _zeros_like_lanes(t, HEAD_PAD - MLA_QK_DIM)], axis=-1)


def _rope_to_head_lanes(t):
    return jnp.concatenate([t[..., :ROPE_HALF], _zeros_like_lanes(t, HALF_LANES - ROPE_HALF), t[..., ROPE_HALF:],
                            _zeros_like_lanes(t, HALF_LANES - ROPE_HALF)], axis=-1)


def _rope_lane_freq():
    inv_freq = (ROPE_THETA ** (-jnp.arange(0, MLA_ROPE_DIM, 2, dtype=F32) / MLA_ROPE_DIM))[None]
    return _rope_to_head_lanes(jnp.concatenate([-inv_freq, inv_freq], axis=1))


def _unshard_cols(g):
    return jnp.transpose(g, (1, 0, 2)).reshape(g.shape[1], 4 * g.shape[2])


def _shard_cols(g):
    r, c4 = g.shape
    return jnp.transpose(g.reshape(r, 4, c4 // 4), (1, 0, 2))


def kernel(x, c, positions, w_ada, b_ada, norm_w, w_in, q_lora_norm, w_uq, kv_lora_norm, w_ukv, q_head_norm, k_head_norm, w_out, loss_target, m_w_ada, m_b_ada, m_norm_w, m_w_in, m_q_lora_norm, m_w_uq, m_kv_lora_norm, m_w_ukv, m_q_head_norm, m_k_head_norm, m_w_out, v_w_ada, v_b_ada, v_norm_w, v_w_in, v_q_lora_norm, v_w_uq, v_kv_lora_norm, v_w_ukv, v_q_head_norm, v_k_head_norm, v_w_out):
    chip = 2 * lax.axis_index("x") + lax.axis_index("y")
    me8 = 2 * chip + lax.axis_index("c")
    ada_cols = w_ada.shape[2]
    c_all = _allgather_rows_call(c)[:, 0, :]
    ada_part = _ada_call(c_all, w_ada[0], lax.dynamic_slice_in_dim(b_ada, chip * ada_cols, ada_cols, axis=1))
    ada_g, win_g, wuq_g, wukv_g, wout_g, cos_t, sin_t = _gather_call(
        [ada_part[None]] + [w.astype(BF16) for w in (w_in, w_uq, w_ukv, w_out)], [False, True, True, True, True],
        positions[0].astype(F32).reshape(-1, LANES), _rope_lane_freq())
    ada = lax.dynamic_slice_in_dim(ada_g, me8, 1, axis=1).reshape(1, 4 * ada_cols)
    (sq_sum, grad_x, g_w_in, g_w_uq, g_w_ukv, g_w_out, d_ada, g_norm_w, g_qln, g_kvln, g_qhn, g_khn) = _local_step(
        x[0], ada, (cos_t, sin_t), loss_target[0], norm_w, win_g,
        q_lora_norm, _unshard_cols(wuq_g), kv_lora_norm, _unshard_cols(wukv_g), q_head_norm, k_head_norm,
        wout_g.reshape(D_MODEL, D_MODEL))

    grads = [g.astype(BF16) for g in (g_w_in, _shard_cols(g_w_uq), _shard_cols(g_w_ukv),
                                      g_w_out.reshape(4, D_MODEL // 4, D_MODEL))]
    pieces = [d_ada, g_norm_w, g_qln, g_kvln, g_qhn, g_khn, (0.5 * sq_sum / D_MODEL).reshape(1, 1)]
    spans = [-(-p.shape[1] // LANES) * LANES for p in pieces]
    starts = [sum(spans[:j]) for j in range(len(spans))]
    small = jnp.concatenate([jnp.pad(p, ((0, 0), (0, sp - p.shape[1]))) for p, sp in zip(pieces, spans)], axis=1)
    core = lax.axis_index("c").astype(I32).reshape(1)
    chip_halves = _sum_halves_call(core, grads, _swap_halves_call(grads))
    parts, packs, res_ada = _exchange_call(chip_halves, small, starts[0], jnp.pad(c_all, ((0, 8), (0, 0))),
                                           w_ada[0], m_w_ada[0], v_w_ada[0])
    loss = jnp.sum(packs[:, 0, starts[-1]])

    names = ["adam_w_in", "adam_w_uq", "adam_w_ukv", "adam_w_out"]
    shard_w = [(w_in, m_w_in, v_w_in), (w_uq, m_w_uq, v_w_uq), (w_ukv, m_w_ukv, v_w_ukv),
               (w_out, m_w_out, v_w_out)]
    res = {}
    for name, p_g, (w, m, v) in zip(names, parts, shard_w):
        if w.shape[2] % LANES:
            res_t = _adam_shard_transposed_call(name, p_g, w[0].T, m[0].T, v[0].T)
            res[name] = [t.T for t in res_t]
        else:
            res[name] = _adam_shard_call(name, p_g, w[0], m[0], v[0])
    vectors = [(b_ada, m_b_ada, v_b_ada), (norm_w, m_norm_w, v_norm_w), (q_lora_norm, m_q_lora_norm, v_q_lora_norm),
               (kv_lora_norm, m_kv_lora_norm, v_kv_lora_norm), (q_head_norm, m_q_head_norm, v_q_head_norm),
               (k_head_norm, m_k_head_norm, v_k_head_norm)]
    vec_out = _adam_vectors_call(packs, starts[:len(vectors)], vectors)

    def ordered(kind):
        big = lambda name: res[name][kind][None]
        return [res_ada[kind][None], vec_out[kind][0], vec_out[kind][1], big("adam_w_in"), vec_out[kind][2],
                big("adam_w_uq"), vec_out[kind][3], big("adam_w_ukv"), vec_out[kind][4], vec_out[kind][5],
                big("adam_w_out")]

    return (loss, grad_x[None], *ordered(0), *ordered(1), *ordered(2), *ordered(3))


def _local_step(x2, ada, rope_tables, tgt, norm_w, w_in_shards, q_lora_norm, w_uq_full,
                kv_lora_norm, w_ukv_full, q_head_norm, k_head_norm, w_out_full):
    in_shard = w_in_shards.shape[2]
    ckv_tail = C_GM - 3 * in_shard
    assert 0 <= ckv_tail and ckv_tail + MLA_ROPE_DIM + MLA_WIDTH == in_shard
    last = w_in_shards[3]
    w_in_bf = jnp.concatenate(
        [w_in_shards[0], w_in_shards[1], w_in_shards[2], last[:, :ckv_tail], last[:, ckv_tail + MLA_ROPE_DIM:],
         _rope_to_head_lanes(last[:, ckv_tail:ckv_tail + MLA_ROPE_DIM])], axis=1).astype(BF16)
    w_uq_bf = _to_head_lanes(w_uq_full.reshape(Q_LORA_RANK, MLA_HEADS, MLA_QK_DIM)).reshape(
        Q_LORA_RANK, MLA_PAD_WIDTH).astype(BF16)
    w_ukv_heads = w_ukv_full.reshape(KV_LORA_RANK, MLA_HEADS, 2 * MLA_NOPE_DIM)
    w_uk_bf = _nope_to_head_lanes(w_ukv_heads[:, :, :MLA_NOPE_DIM]).reshape(KV_LORA_RANK, MLA_PAD_WIDTH).astype(BF16)
    w_uv_bf = w_ukv_heads[:, :, MLA_NOPE_DIM:].reshape(KV_LORA_RANK, MLA_WIDTH).astype(BF16)
    w_out_bf = w_out_full.astype(BF16)
    qhn_pad, khn_pad = _to_head_lanes(q_head_norm), _to_head_lanes(k_head_norm)
    cos_t, sin_t = rope_tables

    hb, q_sb, k_sb, v_sb, g_sb, c_q, c_kv, g_mla, q_m, k_m, v_m, cqn, ckvn, q0, k0 = _pre_call(
        x2, ada, norm_w, w_in_bf, cos_t, sin_t, q_lora_norm, kv_lora_norm, qhn_pad, khn_pad,
        w_uq_bf, w_uk_bf, w_uv_bf)
    o_sb, r_sb, kstart = _sb_fwd_call(q_sb, k_sb, v_sb)
    o_mla, lse = _mla_fwd_call(q_m, k_m, v_m)
    do_sb, do_mla, dg_sb, dg_mla, dy, g_w_out, d_gate, sq = _out_call(
        o_sb, g_sb, o_mla, g_mla, x2, tgt, ada, w_out_bf)

    dq_sb, dk_sb, dv_sb = _sb_bwd_call(kstart, q_sb, k_sb, v_sb, do_sb, r_sb)
    dq_m, dk_m, dv_m = _mla_bwd_call(q_m, k_m, v_m, do_mla, o_mla, lse)
    (d_cq, d_ckv, d_kr, g_wuq_pad, g_wuk_pad, g_wuv, g_qln, g_kvln, g_qhn, g_khn) = _mla_prep_bwd_call(
        dq_m, dk_m, dv_m, q0, k0, cqn, ckvn, c_q, c_kv, cos_t, sin_t,
        q_lora_norm, kv_lora_norm, qhn_pad, khn_pad, w_uq_bf, w_uk_bf, w_uv_bf)
    grad_x, g_win_pad, d_shift, d_scale, g_norm_w = _dh_call(
        [dq_sb, dk_sb, dv_sb, dg_sb, d_cq, d_ckv, dg_mla, d_kr], hb, x2, dy, ada, norm_w, w_in_bf)

    g_kr = g_win_pad[:, C_KR:]
    g_last = jnp.concatenate([g_win_pad[:, 3 * in_shard:C_GM], g_kr[:, :ROPE_HALF],
                              g_kr[:, HALF_LANES:HALF_LANES + ROPE_HALF], g_win_pad[:, C_GM:C_KR]], axis=1)
    g_w_in = jnp.stack([g_win_pad[:, j * in_shard:(j + 1) * in_shard] for j in range(3)] + [g_last])
    g_w_uq = _from_head_lanes(g_wuq_pad.reshape(Q_LORA_RANK, MLA_HEADS, HEAD_PAD)).reshape(Q_LORA_RANK, -1)
    g_w_ukv = jnp.concatenate(
        [_from_head_lanes(g_wuk_pad.reshape(KV_LORA_RANK, MLA_HEADS, HEAD_PAD))[:, :, :MLA_NOPE_DIM],
         g_wuv.reshape(KV_LORA_RANK, MLA_HEADS, MLA_NOPE_DIM)], axis=2).reshape(KV_LORA_RANK, -1)
    d_ada = jnp.concatenate([d_shift, d_scale, d_gate], axis=1)
    return (jnp.sum(sq), grad_x, g_w_in, g_w_uq, g_w_ukv, g_w_out, d_ada, g_norm_w, g_qln, g_kvln,
            _from_head_lanes(g_qhn), _from_head_lanes(g_khn))
```

```python
import math

import jax
import jax.numpy as jnp
from jax import lax
from jax.experimental import pallas as pl
from jax.experimental.pallas import tpu as pltpu

F32 = jnp.float32
BF16 = jnp.bfloat16
I32 = jnp.int32

D_MODEL = 1024
SB_HEADS = 8
SB_WIDTH = 512
MLA_HEADS = 8
MLA_QK_DIM = 96
MLA_NOPE_DIM = 64
MLA_ROPE_DIM = 32
MLA_WIDTH = 512
Q_LORA_RANK = 384
KV_LORA_RANK = 256
ROPE_THETA = 10000.0
EPS = 1e-6
LANES = 128
HALF_LANES = LANES // 2
HEAD_PAD = 128
MLA_PAD_WIDTH = MLA_HEADS * HEAD_PAD

C_Q, C_K, C_V, C_G = 0, 512, 1024, 1536
C_CQ, C_CKV, C_GM, C_KR = 2048, 2432, 2688, 3200
IN_COLS_PAD = 3328

ADAM_LR = 0.001
ADAM_B1 = 0.9
ADAM_B2 = 0.999
ADAM_EPS = 1e-08
ADAM_WD = 0.01
ADAM_STEP = 10

SB_SCALE = 0.125
SB_GROUP = 4
MLA_SCALE = 1.0 / math.sqrt(MLA_QK_DIM)
LN2 = math.log(2.0)
MLA_SCALE_LOG2 = MLA_SCALE / LN2
MLA_BQ = 1024
MLA_BWD_BQ = 1024
MLA_BK = 1024
MLA_BWD_BK = 1024
MLA_STEP = 512
MLA_BWD_STEP = 256
SB_DEAD = -104.0
MASK_NEG = -1e30

VMEM_LIMIT = 56 * 1024 * 1024
MESH = pl.DeviceIdType.MESH


def _dot(a, b):
    return jnp.dot(a, b, preferred_element_type=F32)


def _dot_nt(a, b):
    return lax.dot_general(a, b, (((1,), (1,)), ((), ())), preferred_element_type=F32)


def _dot_tn(a, b):
    return lax.dot_general(a, b, (((0,), (0,)), ((), ())), preferred_element_type=F32)


def _sigmoid(x):
    return 1.0 / (1.0 + jnp.exp(-x))


def _split_dot(a, m):
    hi = a.astype(BF16)
    lo = (a - hi.astype(F32)).astype(BF16)
    return _dot(hi, m) + _dot(lo, m)


def _params(sem, vmem=None):
    return pltpu.CompilerParams(dimension_semantics=sem, vmem_limit_bytes=vmem)


def _row_tile(s, want):
    return min(want, s)


def _hbm_spec():
    return pl.BlockSpec(memory_space=pltpu.HBM)


def _allgather_rows_call(row):
    def body(in_ref, out_ref, send_sems, recv_sems, loc_sem):
        x, y, c = lax.axis_index("x"), lax.axis_index("y"), lax.axis_index("c")
        flips = [(fx, fy, fc) for fx in (0, 1) for fy in (0, 1) for fc in (0, 1)][1:]

        def peer(r):
            fx, fy, fc = flips[r]
            return ((1 - x) if fx else x, (1 - y) if fy else y, (1 - c) if fc else c)

        def copy(r, slot):
            return pltpu.make_async_remote_copy(
                src_ref=in_ref, dst_ref=out_ref.at[slot], send_sem=send_sems.at[r], recv_sem=recv_sems.at[r],
                device_id=peer(r), device_id_type=MESH)

        local = pltpu.make_async_copy(in_ref, out_ref.at[4 * x + 2 * y + c], loc_sem)
        local.start()
        sends = [copy(r, 4 * x + 2 * y + c) for r in range(7)]
        for cp in sends:
            cp.start()
        for r in range(7):
            px, py, pc = peer(r)
            copy(r, 4 * px + 2 * py + pc).wait_recv()
        for cp in sends:
            cp.wait_send()
        local.wait()

    return pl.pallas_call(
        body, name="gather_rows",
        out_shape=jax.ShapeDtypeStruct((8,) + row.shape, row.dtype),
        in_specs=[_hbm_spec()], out_specs=_hbm_spec(),
        scratch_shapes=[pltpu.SemaphoreType.DMA((7,)), pltpu.SemaphoreType.DMA((7,)), pltpu.SemaphoreType.DMA],
    )(row)


def _gather_call(shards, split, positions, lane_freq):
    n = len(shards)
    halves = [s.shape[1] // 2 for s in shards]
    s_len = positions.shape[0] * LANES
    table_rows = 4 * LANES
    n_chunks = s_len // table_rows
    assert n_chunks * table_rows == s_len and n_chunks >= 2

    def body(*refs):
        ins, pos_ref, frq_ref = refs[:n], refs[n], refs[n + 1]
        outs, cos_out, sin_out = refs[n + 2:2 * n + 2], refs[2 * n + 2], refs[2 * n + 3]
        ici_send, ici_recv, d2d_send, d2d_recv, loc_sems, cos_buf, sin_buf, tab_sems = refs[2 * n + 4:]
        x, y, c = lax.axis_index("x"), lax.axis_index("y"), lax.axis_index("c")
        me = 2 * x + y
        peers = [(1 - x, y), (x, 1 - y), (1 - x, 1 - y)]

        def rows(a, which):
            return pl.ds(pl.multiple_of(which * halves[a], 16), halves[a])

        def ici(a, j, slot):
            px, py = peers[j]
            src, dst = ins[a].at[0], outs[a].at[slot]
            if split[a]:
                src, dst = src.at[rows(a, c)], dst.at[rows(a, c)]
            return pltpu.make_async_remote_copy(
                src_ref=src, dst_ref=dst,
                send_sem=ici_send.at[3 * a + j], recv_sem=ici_recv.at[3 * a + j],
                device_id=(px, py, c), device_id_type=MESH)

        def d2d(a, j, which):
            px, py = peers[j]
            piece = outs[a].at[2 * px + py, rows(a, which)]
            return pltpu.make_async_remote_copy(
                src_ref=piece, dst_ref=piece,
                send_sem=d2d_send.at[3 * a + j], recv_sem=d2d_recv.at[3 * a + j],
                device_id=(x, y, 1 - c), device_id_type=MESH)

        local = [pltpu.make_async_copy(ins[a].at[0], outs[a].at[me], loc_sems.at[a]) for a in range(n)]
        for cp in local:
            cp.start()
        sends = [ici(a, j, me) for a in range(n) for j in range(3)]
        for cp in sends:
            cp.start()

        def table_copies(t, slot):
            rows = pl.ds(pl.multiple_of(t * table_rows, table_rows), table_rows)
            return [pltpu.make_async_copy(buf.at[slot], out.at[rows], tab_sems.at[k, slot])
                    for k, (buf, out) in enumerate(((cos_buf, cos_out), (sin_buf, sin_out)))]

        def tables(t, _):
            slot = t % 2

            @pl.when(t >= 2)
            def _():
                for cp in table_copies(t - 2, slot):
                    cp.wait()

            for b in range(table_rows // LANES):
                lane_pos = pos_ref[pl.ds(t * (table_rows // LANES) + b, 1), :]
                ang = jnp.broadcast_to(lane_pos, (LANES, LANES)).T * frq_ref[...]
                cos_buf[slot, b * LANES:(b + 1) * LANES, :] = jnp.cos(ang)
                sin_buf[slot, b * LANES:(b + 1) * LANES, :] = jnp.sin(ang)
            for cp in table_copies(t, slot):
                cp.start()
            return 0

        lax.fori_loop(0, n_chunks, tables, 0)
        for t in (n_chunks - 2, n_chunks - 1):
            for cp in table_copies(t, t % 2):
                cp.wait()
        for a in range(n):
            for j in range(3):
                px, py = peers[j]
                ici(a, j, 2 * px + py).wait_recv()
                if split[a]:
                    cp = d2d(a, j, c)
                    cp.start()
                    sends.append(cp)
        for a in range(n):
            for j in range(3):
                if split[a]:
                    d2d(a, j, 1 - c).wait_recv()
        for cp in sends:
            cp.wait_send()
        for cp in local:
            cp.wait()

    return pl.pallas_call(
        body, name="gather_weights",
        out_shape=[jax.ShapeDtypeStruct((4,) + s.shape[1:], s.dtype) for s in shards]
        + [jax.ShapeDtypeStruct((s_len, LANES), F32)] * 2,
        in_specs=[_hbm_spec() for _ in shards] + [pl.BlockSpec(memory_space=pltpu.VMEM)] * 2,
        out_specs=[_hbm_spec() for _ in range(n + 2)],
        scratch_shapes=[pltpu.SemaphoreType.DMA((3 * n,)), pltpu.SemaphoreType.DMA((3 * n,)),
                        pltpu.SemaphoreType.DMA((3 * n,)), pltpu.SemaphoreType.DMA((3 * n,)),
                        pltpu.SemaphoreType.DMA((n,)),
                        pltpu.VMEM((2, table_rows, LANES), F32), pltpu.VMEM((2, table_rows, LANES), F32),
                        pltpu.SemaphoreType.DMA((2, 2))],
    )(*shards, positions, lane_freq)


def _swap_halves_call(grads):
    n = len(grads)
    halves = [g.shape[1] // 2 for g in grads]

    def body(*refs):
        g_in, hs, send_sems, recv_sems = refs[:n], refs[n:2 * n], refs[2 * n], refs[2 * n + 1]
        x, y, c = lax.axis_index("x"), lax.axis_index("y"), lax.axis_index("c")
        copies = []
        for a in range(n):
            theirs = pl.ds(pl.multiple_of((1 - c) * halves[a], 16), halves[a])
            copies.append(pltpu.make_async_remote_copy(
                src_ref=g_in[a].at[:, theirs], dst_ref=hs[a], send_sem=send_sems.at[a], recv_sem=recv_sems.at[a],
                device_id=(x, y, 1 - c), device_id_type=MESH))
        for cp in copies:
            cp.start()
        for cp in copies:
            cp.wait()

    return pl.pallas_call(
        body, name="swap_halves",
        out_shape=[jax.ShapeDtypeStruct((4, h, g.shape[2]), g.dtype) for g, h in zip(grads, halves)],
        in_specs=[_hbm_spec() for _ in grads], out_specs=[_hbm_spec() for _ in grads],
        scratch_shapes=[pltpu.SemaphoreType.DMA((n,)), pltpu.SemaphoreType.DMA((n,))],
    )(*grads)


def _sum_halves_call(core, grads, halves):
    n = len(grads)

    def body(core_ref, *refs):
        for g_ref, h_ref, o_ref in zip(refs[:n], refs[n:2 * n], refs[2 * n:]):
            o_ref[...] = (g_ref[...].astype(F32) + h_ref[...].astype(F32)).astype(o_ref.dtype)

    whole = lambda h: pl.BlockSpec(h.shape, lambda i, core_ref: (0, 0, 0))
    return pl.pallas_call(
        body, name="sum_halves",
        grid_spec=pltpu.PrefetchScalarGridSpec(
            num_scalar_prefetch=1, grid=(1,),
            in_specs=[pl.BlockSpec(h.shape, lambda i, core_ref: (0, core_ref[0], 0)) for h in halves]
            + [whole(h) for h in halves],
            out_specs=[whole(h) for h in halves]),
        out_shape=[jax.ShapeDtypeStruct(h.shape, h.dtype) for h in halves],
        compiler_params=_params(("arbitrary",), VMEM_LIMIT),
    )(core, *grads, *halves)


def _exchange_call(chip_halves, small, ada_lane0, c_all, w_ada, m_ada, v_ada, early):
    n = len(chip_halves)
    halves = [h.shape[1] for h in chip_halves]
    d_model, cols = w_ada.shape
    tr = 256
    ne = len(early)
    first = [a for a, _, _, _ in early]
    order = first + [a for a in range(n) if a not in first]
    n_in, n_out = n + 5 + 3 * ne, n + 5 + 4 * ne

    def body(*refs):
        ins, outs_, scratch = refs[:n_in], refs[n_in:n_in + n_out], refs[n_in + n_out:]
        g_in, small_in, c_in, wmv_in, early_in = ins[:n], ins[n], ins[n + 1], ins[n + 2:n + 5], ins[n + 5:]
        parts, packs, ada_out, early_out = outs_[:n], outs_[n], outs_[n + 1:n + 5], outs_[n + 5:]
        (ici_send, ici_recv, d2d_send, d2d_recv, sm_send, sm_recv, loc_sems,
         c_buf, d_buf, wmv_buf, res_buf, ada_sems) = scratch[:12]
        early_scratch = scratch[12:]
        x, y, c = lax.axis_index("x"), lax.axis_index("y"), lax.axis_index("c")
        me = 2 * x + y
        me8 = 4 * x + 2 * y + c
        sibling = (x, y, 1 - c)
        peers = [(1 - x, y), (x, 1 - y), (1 - x, 1 - y)]
        flips = [(fx, fy, fc) for fx in (0, 1) for fy in (0, 1) for fc in (0, 1)][1:]

        def rows(a, which):
            return pl.ds(pl.multiple_of(which * halves[a], 16), halves[a])

        def ici(a, j, src_slot, dst_slot):
            px, py = peers[j]
            return pltpu.make_async_remote_copy(
                src_ref=g_in[a].at[src_slot], dst_ref=parts[a].at[dst_slot, rows(a, c)],
                send_sem=ici_send.at[3 * a + j], recv_sem=ici_recv.at[3 * a + j],
                device_id=(px, py, c), device_id_type=MESH)

        def d2d(a, rel, chip, which):
            piece = parts[a].at[chip, rows(a, which)]
            return pltpu.make_async_remote_copy(
                src_ref=piece, dst_ref=piece,
                send_sem=d2d_send.at[4 * a + rel], recv_sem=d2d_recv.at[4 * a + rel],
                device_id=sibling, device_id_type=MESH)

        def flipped(r):
            fx, fy, fc = flips[r]
            return ((1 - x) if fx else x, (1 - y) if fy else y, (1 - c) if fc else c)

        def sm(r, slot):
            return pltpu.make_async_remote_copy(
                src_ref=small_in, dst_ref=packs.at[slot],
                send_sem=sm_send.at[r], recv_sem=sm_recv.at[r],
                device_id=flipped(r), device_id_type=MESH)

        def peer8(r):
            px, py, pc = flipped(r)
            return 4 * px + 2 * py + pc

        def adam_ada():
            lanes = pl.ds(pl.multiple_of(ada_lane0 + me * cols, LANES), cols)
            d_rows = pltpu.make_async_copy(packs.at[:, 0, lanes], d_buf.at[pl.ds(0, 8), :], ada_sems.at[4])
            d_buf[8:, :] = jnp.zeros((8, cols), F32)
            d_rows.start()
            for cp in [d_rows] + ada_loads:
                cp.wait()
            cc = c_buf[...]
            sc = cc * _sigmoid(cc)
            dd = d_buf[...]
            sc_hi = sc.astype(BF16)
            sc_lo = (sc - sc_hi.astype(F32)).astype(BF16)
            dd_hi = dd.astype(BF16)
            dd_lo = (dd - dd_hi.astype(F32)).astype(BF16)
            for t in range(d_model // tr):
                rws = slice(t * tr, (t + 1) * tr)
                g = _dot_tn(sc_hi[:, rws], dd_hi) + (_dot_tn(sc_hi[:, rws], dd_lo) + _dot_tn(sc_lo[:, rws], dd_hi))
                res_buf[0, rws, :] = g
                res_buf[1, rws, :], res_buf[2, rws, :], res_buf[3, rws, :] = _adamw(
                    g, wmv_buf[0, rws, :], wmv_buf[1, rws, :], wmv_buf[2, rws, :])
            stores = [pltpu.make_async_copy(res_buf.at[k], ada_out[k], ada_sems.at[5 + k]) for k in range(4)]
            for cp in stores:
                cp.start()
            return stores

        def early_adam(e):
            p_buf, ewmv_buf, eres_buf, e_sems = early_scratch[4 * e:4 * e + 4]
            load = pltpu.make_async_copy(parts[first[e]], p_buf, e_sems.at[3])
            load.start()
            for cp in [load] + early_loads[e]:
                cp.wait()
            g = ((p_buf[0].astype(F32) + p_buf[1].astype(F32)) + p_buf[2].astype(F32)) + p_buf[3].astype(F32)
            eres_buf[0] = g
            eres_buf[1], eres_buf[2], eres_buf[3] = _adamw(g, ewmv_buf[0], ewmv_buf[1], ewmv_buf[2])
            stores = [pltpu.make_async_copy(eres_buf.at[k], early_out[4 * e + k], e_sems.at[4 + k]) for k in range(4)]
            for cp in stores:
                cp.start()
            return stores

        ada_loads = [pltpu.make_async_copy(c_in, c_buf, ada_sems.at[0])]
        ada_loads += [pltpu.make_async_copy(wmv_in[k], wmv_buf.at[k], ada_sems.at[1 + k]) for k in range(3)]
        early_loads = [[pltpu.make_async_copy(early_in[3 * e + k], early_scratch[4 * e + 1].at[k],
                                              early_scratch[4 * e + 3].at[k]) for k in range(3)] for e in range(ne)]
        for cp in ada_loads + [cp for loads in early_loads for cp in loads]:
            cp.start()
        local = [pltpu.make_async_copy(g_in[a].at[me], parts[a].at[me, rows(a, c)], loc_sems.at[a])
                 for a in range(n)]
        local.append(pltpu.make_async_copy(small_in, packs.at[me8], loc_sems.at[n]))
        for cp in local:
            cp.start()
        sends = []
        for r in range(7):
            sends.append(sm(r, me8))
        for a in order:
            for j in range(3):
                px, py = peers[j]
                sends.append(ici(a, j, 2 * px + py, me))
        for cp in sends:
            cp.start()
        for a in order:
            local[a].wait()
            cp = d2d(a, 0, me, c)
            cp.start()
            sends.append(cp)
        for r in range(7):
            sm(r, peer8(r)).wait_recv()
        local[n].wait()
        stores = adam_ada()
        for a in order:
            for j in range(3):
                px, py = peers[j]
                ici(a, j, me, 2 * px + py).wait_recv()
                cp = d2d(a, 1 + j, 2 * px + py, c)
                cp.start()
                sends.append(cp)
            if a in first:
                for rel, chip in enumerate([me] + [2 * px + py for px, py in peers]):
                    d2d(a, rel, chip, 1 - c).wait_recv()
                stores += early_adam(first.index(a))
        for a in order:
            if a not in first:
                d2d(a, 0, me, 1 - c).wait_recv()
                for j in range(3):
                    px, py = peers[j]
                    d2d(a, 1 + j, 2 * px + py, 1 - c).wait_recv()
        for cp in sends:
            cp.wait_send()
        for cp in stores:
            cp.wait()

    out_shape = ([jax.ShapeDtypeStruct((4, 2 * h.shape[1], h.shape[2]), h.dtype) for h in chip_halves]
                 + [jax.ShapeDtypeStruct((8,) + small.shape, small.dtype)]
                 + [jax.ShapeDtypeStruct((d_model, cols), F32)] * 4
                 + [jax.ShapeDtypeStruct(w.shape, F32) for _, w, _, _ in early for _ in range(4)])
    early_scratch_shapes = []
    for a, w, _, _ in early:
        assert chip_halves[a].shape[2] == w.shape[1] and 2 * halves[a] == w.shape[0]
        early_scratch_shapes += [pltpu.VMEM((4,) + w.shape, chip_halves[a].dtype), pltpu.VMEM((3,) + w.shape, F32),
                                 pltpu.VMEM((4,) + w.shape, F32), pltpu.SemaphoreType.DMA((8,))]
    res = pl.pallas_call(
        body, name="exchange_grads",
        out_shape=out_shape,
        in_specs=[_hbm_spec() for _ in range(n_in)],
        out_specs=[_hbm_spec() for _ in range(n_out)],
        scratch_shapes=[pltpu.SemaphoreType.DMA((3 * n,)), pltpu.SemaphoreType.DMA((3 * n,)),
                        pltpu.SemaphoreType.DMA((4 * n,)), pltpu.SemaphoreType.DMA((4 * n,)),
                        pltpu.SemaphoreType.DMA((7,)), pltpu.SemaphoreType.DMA((7,)),
                        pltpu.SemaphoreType.DMA((n + 1,)),
                        pltpu.VMEM(c_all.shape, F32), pltpu.VMEM((16, cols), F32),
                        pltpu.VMEM((3, d_model, cols), F32), pltpu.VMEM((4, d_model, cols), F32),
                        pltpu.SemaphoreType.DMA((9,))] + early_scratch_shapes,
        compiler_params=pltpu.CompilerParams(vmem_limit_bytes=VMEM_LIMIT),
    )(*chip_halves, small, c_all, w_ada, m_ada, v_ada, *[t for _, w, m, v in early for t in (w, m, v)])
    early_res = res[n + 5:]
    return res[:n], res[n], res[n + 1:n + 5], [early_res[4 * e:4 * e + 4] for e in range(ne)]


def _ada_call(c_all, w_ada_cols, b_ada_cols):
    def body(c_ref, w_ref, b_ref, o_ref):
        cc = c_ref[...]
        o_ref[...] = _dot((cc * _sigmoid(cc)).astype(BF16), w_ref[...].astype(BF16)) + b_ref[...]

    return pl.pallas_call(
        body, name="ada_fwd",
        out_shape=jax.ShapeDtypeStruct((c_all.shape[0], w_ada_cols.shape[1]), F32),
        compiler_params=pltpu.CompilerParams(vmem_limit_bytes=VMEM_LIMIT),
    )(c_all, w_ada_cols, b_ada_cols)


def _ada_part(j):
    return pl.BlockSpec((1, D_MODEL), lambda i: (0, j))


def _full(shape):
    return pl.BlockSpec(shape, lambda i: (0,) * len(shape))


def _rows(tm, width):
    return pl.BlockSpec((tm, width), lambda i: (i, 0))


def _rope(t, cos_t, sin_t):
    return t * cos_t + pltpu.roll(t, HALF_LANES, 1) * sin_t


def _rope_adjoint(d, cos_t, sin_t):
    return d * cos_t + pltpu.roll(d * sin_t, HALF_LANES, 1)


def _pre_call(x, ada, norm_w, w_in_bf, cos_t, sin_t, q_lora_norm, kv_lora_norm, qhn_pad, khn_pad,
              w_uq_bf, w_uk_bf, w_uv_bf):
    s = x.shape[0]
    tm = _row_tile(s, 512)
    out_defs = [(D_MODEL, BF16), (512, BF16), (512, BF16), (512, BF16), (512, F32),
                (Q_LORA_RANK, F32), (KV_LORA_RANK, F32), (512, F32),
                (MLA_PAD_WIDTH, BF16), (MLA_PAD_WIDTH, BF16), (MLA_WIDTH, BF16),
                (Q_LORA_RANK, BF16), (KV_LORA_RANK, BF16), (MLA_PAD_WIDTH, F32), (MLA_PAD_WIDTH, F32)]

    def body(x_ref, sh_ref, sc_ref, nw_ref, w_ref, cos_ref, sin_ref, qln_ref, kvln_ref, qhn_ref, khn_ref,
             wuq_ref, wuk_ref, wuv_ref,
             hb_ref, qsb_ref, ksb_ref, vsb_ref, gsb_ref, cq_ref, ckv_ref, gm_ref,
             q_ref, k_ref, v_ref, cqn_ref, ckvn_ref, q0_ref, k0_ref):
        xx = x_ref[...]
        r0 = lax.rsqrt(jnp.mean(xx * xx, axis=-1, keepdims=True) + EPS)
        hb = ((xx * r0 * nw_ref[...]) * (1.0 + sc_ref[...]) + sh_ref[...]).astype(BF16)
        hb_ref[...] = hb

        def proj(c0, width):
            return _dot(hb, w_ref[:, c0:c0 + width])

        cq = proj(C_CQ, Q_LORA_RANK)
        ckv = proj(C_CKV, KV_LORA_RANK)
        kr = proj(C_KR, LANES)
        cq_ref[...] = cq
        ckv_ref[...] = ckv
        cqn = (cq * lax.rsqrt(jnp.mean(cq * cq, axis=-1, keepdims=True) + EPS) * qln_ref[...]).astype(BF16)
        cqn_ref[...] = cqn
        ckvn = (ckv * lax.rsqrt(jnp.mean(ckv * ckv, axis=-1, keepdims=True) + EPS) * kvln_ref[...]).astype(BF16)
        ckvn_ref[...] = ckvn
        qsb_ref[...] = proj(C_Q, 512).astype(BF16)
        v_ref[...] = _dot(ckvn, wuv_ref[...]).astype(BF16)
        q0_ref[...] = _dot(cqn, wuq_ref[...])
        k0_ref[...] = _dot(ckvn, wuk_ref[...])
        ksb_ref[...] = proj(C_K, 512).astype(BF16)
        cos_t, sin_t = cos_ref[...], sin_ref[...]
        heads = [slice(h * HEAD_PAD, (h + 1) * HEAD_PAD) for h in range(MLA_HEADS)]
        for cols in heads:
            k0_ref[:, cols] = k0_ref[:, cols] + kr

        def inv_rms(ref):
            sums = [jnp.sum(ref[:, cols] * ref[:, cols], axis=-1, keepdims=True) for cols in heads]
            return [lax.rsqrt(t * (1.0 / MLA_QK_DIM) + EPS) for t in sums]

        rqs = inv_rms(q0_ref)
        vsb_ref[...] = proj(C_V, 512).astype(BF16)
        rks = inv_rms(k0_ref)
        gsb_ref[...] = proj(C_G, 512)
        for cols, rq, rk in zip(heads, rqs, rks):
            q_ref[:, cols] = (_rope(q0_ref[:, cols] * rq * qhn_ref[...], cos_t, sin_t) * MLA_SCALE_LOG2).astype(BF16)
            k_ref[:, cols] = _rope(k0_ref[:, cols] * rk * khn_ref[...], cos_t, sin_t).astype(BF16)
        gm_ref[...] = proj(C_GM, 512)

    return pl.pallas_call(
        body, name="pre_proj",
        grid=(s // tm,),
        in_specs=[_rows(tm, D_MODEL), _ada_part(0), _ada_part(1), _full((1, D_MODEL)),
                  pl.BlockSpec((D_MODEL, IN_COLS_PAD), lambda i: (0, 0), pipeline_mode=pl.Buffered(1)),
                  _rows(tm, LANES), _rows(tm, LANES),
                  _full((1, Q_LORA_RANK)), _full((1, KV_LORA_RANK)), _full((1, LANES)), _full((1, LANES)),
                  _full((Q_LORA_RANK, MLA_PAD_WIDTH)), _full((KV_LORA_RANK, MLA_PAD_WIDTH)),
                  _full((KV_LORA_RANK, MLA_WIDTH))],
        out_specs=[_rows(tm, w) for w, _ in out_defs],
        out_shape=[jax.ShapeDtypeStruct((s, w), dt) for w, dt in out_defs],
        compiler_params=_params(("arbitrary",), VMEM_LIMIT),
    )(x, ada, ada, norm_w, w_in_bf, cos_t, sin_t, q_lora_norm, kv_lora_norm, qhn_pad, khn_pad,
      w_uq_bf, w_uk_bf, w_uv_bf)


def _log_sigmoid_pair(z):
    ls = jnp.minimum(z, 0.0) - jnp.log(1.0 + jnp.exp(-jnp.abs(z)))
    return ls, ls - z


def _pair(hh):
    return slice((hh // 2) * LANES, (hh // 2 + 1) * LANES)


def _sb_fwd_call(q, k, v):
    s = q.shape[0]
    bq = _row_tile(s, 256)
    nq = s // bq
    nh = SB_GROUP
    width = nh * HALF_LANES

    def body(q_ref, k_ref, v_ref, o_ref, r_ref, ks_ref):
        hp, i = pl.program_id(0), pl.program_id(1)
        lane = lax.broadcasted_iota(I32, (bq, LANES), 1)
        row = lax.broadcasted_iota(I32, (bq, bq), 0)
        col = lax.broadcasted_iota(I32, (bq, bq), 1)
        strict = col < row
        later = jnp.where(row > col, 1.0, 0.0).astype(BF16)
        masks = [_head_mask(lane, hh).astype(BF16) for hh in range(2)]
        qms = [q_ref[:, _pair(hh)] * jnp.asarray(SB_SCALE, BF16) * masks[hh % 2] for hh in range(nh)]

        def walk(blocks, state):
            chains = [(kb, diagonal, hh) for kb, diagonal in blocks for hh in range(nh)]
            keys = lambda kb: pl.ds(pl.multiple_of(kb * bq, bq), bq)
            zs = [_dot_nt(qms[hh], k_ref[keys(kb), _pair(hh)]) for kb, _, hh in chains]
            pairs = []
            for z, (_, diagonal, _) in zip(zs, chains):
                ls, lk = _log_sigmoid_pair(z)
                pairs.append((ls, jnp.where(strict, lk, 0.0) if diagonal else lk))
            sums = [_split_dot(lk, later) for _, lk in pairs]
            runs = [st[0] for st in state]
            ws = []
            for (ls, lk), after, (_, diagonal, hh) in zip(pairs, sums, chains):
                w = jnp.exp(ls + (after + runs[hh]))
                ws.append((jnp.where(strict, w, 0.0) if diagonal else w).astype(BF16))
                runs[hh] = runs[hh] + jnp.sum(lk, axis=1, keepdims=True)
            accs = [st[1] for st in state]
            for w, (kb, _, hh) in zip(ws, chains):
                accs[hh] = accs[hh] + _dot(w, v_ref[keys(kb), _pair(hh)])
            return tuple(zip(runs, accs))

        def alive(state):
            top = jnp.max(state[0][0])
            for st in state[1:]:
                top = jnp.maximum(top, jnp.max(st[0]))
            return (top > SB_DEAD).astype(I32)

        def finish(state, first):
            ks_ref[hp, i] = first
            for pair in range(nh // 2):
                o_ref[:, _pair(2 * pair)] = jnp.where(lane < HALF_LANES, state[2 * pair][1], state[2 * pair + 1][1])
                r_ref[:, _pair(2 * pair)] = jnp.where(lane < HALF_LANES, state[2 * pair][0], state[2 * pair + 1][0])

        zero = ((jnp.zeros((bq, 1), F32), jnp.zeros((bq, LANES), F32)),) * nh

        @pl.when(i == 0)
        def _():
            finish(walk([(0, True)], zero), 0)

        @pl.when(i > 0)
        def _():
            state = walk([(i, True), (i - 1, False)], zero)

            def cond(carry):
                return jnp.logical_and(carry[0] >= 0, carry[1] > 0)

            def step(carry):
                state = walk([(carry[0], False)], carry[2])
                return carry[0] - 1, alive(state), state

            kb, _, state = lax.while_loop(cond, step, (i - 2, alive(state), state))
            finish(state, kb + 1)

    return pl.pallas_call(
        body, name="sb_fwd",
        grid=(SB_HEADS // nh, nq),
        in_specs=[pl.BlockSpec((bq, width), lambda h, i: (i, h)),
                  pl.BlockSpec((s, width), lambda h, i: (0, h)),
                  pl.BlockSpec((s, width), lambda h, i: (0, h))],
        out_specs=[pl.BlockSpec((bq, width), lambda h, i: (i, h)),
                   pl.BlockSpec((bq, width), lambda h, i: (i, h)),
                   pl.BlockSpec(memory_space=pltpu.SMEM)],
        out_shape=[jax.ShapeDtypeStruct((s, SB_WIDTH), F32), jax.ShapeDtypeStruct((s, SB_WIDTH), F32),
                   jax.ShapeDtypeStruct((SB_HEADS // nh, nq), I32)],
        compiler_params=_params(("arbitrary", "arbitrary"), VMEM_LIMIT),
    )(q, k, v)


def _mla_fwd_call(q, k, v):
    s = q.shape[0]
    bq = _row_tile(s, MLA_BQ)
    bk = _row_tile(s, MLA_BK)
    nq = s // bq
    assert bk % bq == 0
    step = min(bk // 2, MLA_STEP)
    nsub = bk // step
    assert nsub % 2 == 0

    def body(q_ref, k_ref, v_ref, o_ref, lse_ref, p_ref, s_ref):
        i = pl.program_id(1)
        lane = lax.broadcasted_iota(I32, (bq, LANES), 1)
        row = lax.broadcasted_iota(I32, (step, bq), 1)
        col = lax.broadcasted_iota(I32, (step, bq), 0)
        n_full = (i * bq) // bk

        def keys(g):
            return pl.ds(pl.multiple_of(g * step, step), step)

        def join(left, right, qlo):
            return right if qlo == 0 else jnp.concatenate([left[:, :qlo], right], axis=1)

        def put_scores(g, slot, qlo=0):
            for hh in range(2):
                cols = slice(hh * HEAD_PAD, (hh + 1) * HEAD_PAD)
                s_ref[slot, hh, :, qlo:] = _dot_nt(k_ref[keys(g), cols], q_ref[qlo:, cols])

        def add_pv(carry, g, slot, qlo=0):
            vblk = v_ref[keys(g), :]
            out = []
            for hh, (m, l, acc, alpha) in enumerate(carry):
                upd = alpha[:, qlo:] * acc[:, qlo:] + _dot_tn(vblk, p_ref[slot, hh, :, qlo:])
                out.append((m, l, join(acc, upd, qlo), alpha))
            return tuple(out)

        def substep(g, slot, carry, masked, prefetch, qlo=0, next_qlo=0, prev_qlo=0, first=False):
            if prefetch:
                put_scores(g + 1, 1 - slot, next_qlo)
            if not first:
                carry = add_pv(carry, g - 1, 1 - slot, prev_qlo)
            new = []
            for hh in range(2):
                m, l, acc, _ = carry[hh]
                sc = s_ref[slot, hh, :, qlo:]
                if masked:
                    sc = jnp.where(col[:, qlo:] + g * step <= row[:, qlo:] + i * bq, sc, MASK_NEG)
                m_new = jnp.maximum(m[:, qlo:], jnp.max(sc, axis=0, keepdims=True))
                p = jnp.exp2(sc - m_new)
                alpha = jnp.exp2(m[:, qlo:] - m_new)
                l_new = alpha * l[:, qlo:] + jnp.sum(p, axis=0, keepdims=True)
                p_ref[slot, hh, :, qlo:] = p.astype(BF16)
                new.append((join(m, m_new, qlo), join(l, l_new, qlo), acc, join(jnp.ones_like(m), alpha, qlo)))
            return tuple(new)

        def first_query(t, masked):
            return t * step if (masked and bk == bq and 0 <= t < nsub) else 0

        def chunk(kb, carry, masked, first=False):
            for t in range(nsub):
                last = masked and t == nsub - 1
                carry = substep(nsub * kb + t, t % 2, carry, masked, not last, first_query(t, masked),
                                first_query(t + 1, masked), first_query(t - 1, masked), first and t == 0)
            return carry

        def finish(carry):
            (m0, l0, a0, _), (m1, l1, a1, _) = add_pv(carry, nsub * n_full + nsub - 1, 1,
                                                      first_query(nsub - 1, True))
            o_ref[...] = jnp.where(lane < HALF_LANES, (a0 / l0).T, (a1 / l1).T)
            sub = lax.broadcasted_iota(I32, (8, bq), 0)
            lse_ref[...] = jnp.where(sub == 0, m0 + jnp.log2(l0), jnp.where(sub == 1, m1 + jnp.log2(l1), 0.0))

        put_scores(0, 0)
        one = (jnp.full((1, bq), MASK_NEG, F32), jnp.zeros((1, bq), F32), jnp.zeros((LANES, bq), F32),
               jnp.ones((1, bq), F32))

        @pl.when(n_full == 0)
        def _():
            finish(chunk(0, (one, one), True, first=True))

        @pl.when(n_full > 0)
        def _():
            carry = chunk(0, (one, one), False, first=True)
            carry = lax.fori_loop(1, n_full, lambda kb, cr: chunk(kb, cr, False), carry)
            finish(chunk(n_full, carry, True))

    return pl.pallas_call(
        body, name="mla_fwd",
        grid=(4, nq),
        in_specs=[pl.BlockSpec((bq, 2 * HEAD_PAD), lambda h, i: (i, h)),
                  pl.BlockSpec((s, 2 * HEAD_PAD), lambda h, i: (0, h)),
                  pl.BlockSpec((s, LANES), lambda h, i: (0, h))],
        out_specs=[pl.BlockSpec((bq, LANES), lambda h, i: (i, h)),
                   pl.BlockSpec((None, 8, bq), lambda h, i: (h, 0, i))],
        out_shape=[jax.ShapeDtypeStruct((s, MLA_WIDTH), F32), jax.ShapeDtypeStruct((4, 8, s), F32)],
        scratch_shapes=[pltpu.VMEM((2, 2, step, bq), BF16), pltpu.VMEM((2, 2, step, bq), F32)],
        compiler_params=_params(("arbitrary", "arbitrary"), VMEM_LIMIT),
    )(q, k, v)


def _out_call(o_sb, g_sb, o_mla, g_mla, x, target, ada, w_out_bf):
    s = x.shape[0]
    tm = _row_tile(s, 512)

    def body(osb_ref, gsb_ref, oml_ref, gml_ref, x_ref, t_ref, gate_ref, w_ref,
             dosb_ref, doml_ref, dgsb_ref, dgml_ref, dy_ref, gw_ref, dgate_ref, sq_ref):
        @pl.when(pl.program_id(0) == 0)
        def _():
            gw_ref[...] = jnp.zeros_like(gw_ref)
            dgate_ref[...] = jnp.zeros_like(dgate_ref)
            sq_ref[...] = jnp.zeros_like(sq_ref)

        g_s, g_m = gsb_ref[...], gml_ref[...]
        sig_s, sig_m = _sigmoid(g_s), _sigmoid(g_m)
        silu_s, silu_m = g_s * sig_s, g_m * sig_m
        o_s, o_m = osb_ref[...], oml_ref[...]
        mixed = jnp.concatenate([o_s * silu_s, o_m * silu_m], axis=1).astype(BF16)
        u = _dot(mixed, w_ref[...])
        gate_v = gate_ref[...]
        err = x_ref[...] + gate_v * u - t_ref[...]
        sq_ref[...] += jnp.sum(err * err, axis=0, keepdims=True)
        dy = err * (1.0 / D_MODEL)
        dy_ref[...] = dy
        dgate_ref[...] += jnp.sum(dy * u, axis=0, keepdims=True)
        du = (dy * gate_v).astype(BF16)
        gw_ref[...] += _dot_tn(mixed, du)
        dmix = _dot_nt(du, w_ref[...])
        dm_s, dm_m = dmix[:, :SB_WIDTH], dmix[:, SB_WIDTH:]
        dosb_ref[...] = (dm_s * silu_s).astype(BF16)
        doml_ref[...] = (dm_m * silu_m).astype(BF16)
        dgsb_ref[...] = (dm_s * o_s * (sig_s * (1.0 + g_s * (1.0 - sig_s)))).astype(BF16)
        dgml_ref[...] = (dm_m * o_m * (sig_m * (1.0 + g_m * (1.0 - sig_m)))).astype(BF16)

    return pl.pallas_call(
        body, name="out_proj_loss",
        grid=(s // tm,),
        in_specs=[_rows(tm, 512), _rows(tm, 512), _rows(tm, 512), _rows(tm, 512),
                  _rows(tm, D_MODEL), _rows(tm, D_MODEL), _ada_part(2), _full((D_MODEL, D_MODEL))],
        out_specs=[_rows(tm, 512), _rows(tm, 512), _rows(tm, 512), _rows(tm, 512), _rows(tm, D_MODEL),
                   _full((D_MODEL, D_MODEL)), _full((1, D_MODEL)), _full((1, D_MODEL))],
        out_shape=[jax.ShapeDtypeStruct((s, 512), BF16)] * 4
        + [jax.ShapeDtypeStruct((s, D_MODEL), F32), jax.ShapeDtypeStruct((D_MODEL, D_MODEL), F32),
           jax.ShapeDtypeStruct((1, D_MODEL), F32), jax.ShapeDtypeStruct((1, D_MODEL), F32)],
        compiler_params=_params(("arbitrary",), VMEM_LIMIT),
    )(o_sb, g_sb, o_mla, g_mla, x, target, ada, w_out_bf)


def _head_mask(lane, hh):
    return jnp.where((lane >= HALF_LANES) if hh else (lane < HALF_LANES), 1.0, 0.0)


def _pick_lane(packed, lane, which):
    return jnp.sum(jnp.where(lane == which, packed, 0.0), axis=1, keepdims=True)


def _sb_bwd_call(kstart, q, k, v, do, rfin):
    s = q.shape[0]
    bq = _row_tile(s, 256)
    nq = s // bq
    nh = SB_GROUP
    width = nh * HALF_LANES

    def body(ks_ref, q_ref, k_ref, v_ref, do_ref, r_ref, dq_ref, dk_ref, dv_ref):
        hp, i = pl.program_id(0), pl.program_id(1)

        @pl.when(i == 0)
        def _():
            dk_ref[...] = jnp.zeros_like(dk_ref)
            dv_ref[...] = jnp.zeros_like(dv_ref)

        lane = lax.broadcasted_iota(I32, (bq, LANES), 1)
        row = lax.broadcasted_iota(I32, (bq, bq), 0)
        col = lax.broadcasted_iota(I32, (bq, bq), 1)
        upto = jnp.where(row <= col, 1.0, 0.0).astype(BF16)
        before = jnp.where(row < col, 1.0, 0.0).astype(BF16)
        masks = [_head_mask(lane, hh).astype(BF16) for hh in range(2)]
        qms = [q_ref[:, _pair(hh)] * jnp.asarray(SB_SCALE, BF16) * masks[hh % 2] for hh in range(nh)]
        doms = [do_ref[:, _pair(hh)] * masks[hh % 2] for hh in range(nh)]
        totals = [_pick_lane(r_ref[:, _pair(hh)], lane, HALF_LANES * (hh % 2)) for hh in range(nh)]
        strict = col < row

        def walk(blocks, state):
            chains = [(kb, diagonal, hh) for kb, diagonal in blocks for hh in range(nh)]
            keys = lambda kb: pl.ds(pl.multiple_of(kb * bq, bq), bq)
            cut = lambda x, diagonal: jnp.where(strict, x, 0.0) if diagonal else x
            zs = [_dot_nt(qms[hh], k_ref[keys(kb), _pair(hh)]) for kb, _, hh in chains]
            dws = [_dot_nt(doms[hh], v_ref[keys(kb), _pair(hh)]) for kb, _, hh in chains]
            pairs = []
            for z, (_, diagonal, _) in zip(zs, chains):
                ls, lk = _log_sigmoid_pair(z)
                pairs.append((ls, cut(lk, diagonal)))
            incls = [_split_dot(lk, upto) for _, lk in pairs]
            pres = [st[0] for st in state]
            ws, gs = [], []
            for (ls, lk), incl, dw, (_, diagonal, hh) in zip(pairs, incls, dws, chains):
                w = cut(jnp.exp(ls + ((totals[hh] - pres[hh]) - incl)), diagonal)
                ws.append(w.astype(BF16))
                gs.append(w * dw)
                pres[hh] = pres[hh] + jnp.sum(lk, axis=1, keepdims=True)
            gsums = [_dot(g.astype(BF16), before) for g in gs]
            gpres = [st[1] for st in state]
            dzs = []
            for (ls, _), g, gsum, (_, diagonal, hh) in zip(pairs, gs, gsums, chains):
                dzs.append(cut(g - jnp.exp(ls) * (g + (gpres[hh] + gsum)), diagonal).astype(BF16))
                gpres[hh] = gpres[hh] + jnp.sum(g, axis=1, keepdims=True)
            dqs = [st[2] for st in state]
            dk_parts, dv_parts = [], []
            for dzb, w, (kb, _, hh) in zip(dzs, ws, chains):
                dk_parts.append(_dot_tn(dzb, qms[hh]))
                dv_parts.append(_dot_tn(w, doms[hh]))
                dqs[hh] = dqs[hh] + _dot(dzb, k_ref[keys(kb), _pair(hh)])
            for b, (kb, _) in enumerate(blocks):
                for pair in range(nh // 2):
                    c0 = b * nh + 2 * pair
                    dk_ref[keys(kb), _pair(2 * pair)] += dk_parts[c0] + dk_parts[c0 + 1]
                    dv_ref[keys(kb), _pair(2 * pair)] += dv_parts[c0] + dv_parts[c0 + 1]
            return tuple(zip(pres, gpres, dqs))

        def finish(state):
            for pair in range(nh // 2):
                both = jnp.where(lane < HALF_LANES, state[2 * pair][2], state[2 * pair + 1][2])
                dq_ref[:, _pair(2 * pair)] = (both * SB_SCALE).astype(BF16)

        zero = ((jnp.zeros((bq, 1), F32), jnp.zeros((bq, 1), F32), jnp.zeros((bq, LANES), F32)),) * nh

        @pl.when(i == 0)
        def _():
            finish(walk([(0, True)], zero))

        @pl.when(i > 0)
        def _():
            state = lax.fori_loop(ks_ref[hp, i], i - 1, lambda kb, st: walk([(kb, False)], st), zero)
            finish(walk([(i - 1, False), (i, True)], state))

    return pl.pallas_call(
        body, name="sb_bwd",
        grid_spec=pltpu.PrefetchScalarGridSpec(
            num_scalar_prefetch=1, grid=(SB_HEADS // nh, nq),
            in_specs=[pl.BlockSpec((bq, width), lambda h, i, ks: (i, h)),
                      pl.BlockSpec((s, width), lambda h, i, ks: (0, h), pipeline_mode=pl.Buffered(1)),
                      pl.BlockSpec((s, width), lambda h, i, ks: (0, h), pipeline_mode=pl.Buffered(1)),
                      pl.BlockSpec((bq, width), lambda h, i, ks: (i, h)),
                      pl.BlockSpec((bq, width), lambda h, i, ks: (i, h))],
            out_specs=[pl.BlockSpec((bq, width), lambda h, i, ks: (i, h)),
                       pl.BlockSpec((s, width), lambda h, i, ks: (0, h), pipeline_mode=pl.Buffered(1)),
                       pl.BlockSpec((s, width), lambda h, i, ks: (0, h), pipeline_mode=pl.Buffered(1))]),
        out_shape=[jax.ShapeDtypeStruct((s, SB_WIDTH), BF16), jax.ShapeDtypeStruct((s, SB_WIDTH), F32),
                   jax.ShapeDtypeStruct((s, SB_WIDTH), F32)],
        compiler_params=_params(("arbitrary", "arbitrary"), VMEM_LIMIT),
    )(kstart, q, k, v, do, rfin)


def _mla_bwd_call(q, k, v, do, o, lse):
    s = q.shape[0]
    bq = _row_tile(s, MLA_BWD_BQ)
    bk = _row_tile(s, MLA_BWD_BK)
    nq = s // bq
    assert bk % bq == 0
    step = min(bk // 2, MLA_BWD_STEP)
    nsub = bk // step
    assert nsub % 2 == 0

    def body(q_ref, k_ref, v_ref, do_ref, o_ref, lse_ref, dq_ref, dk_ref, dv_ref, dom_ref, s_ref, dp_ref, pb_ref,
             ds_ref):
        i = pl.program_id(1)

        @pl.when(i == 0)
        def _():
            dk_ref[...] = jnp.zeros_like(dk_ref)
            dv_ref[...] = jnp.zeros_like(dv_ref)

        lane = lax.broadcasted_iota(I32, (bq, LANES), 1)
        row = lax.broadcasted_iota(I32, (step, bq), 1)
        col = lax.broadcasted_iota(I32, (step, bq), 0)
        n_full = (i * bq) // bk
        do2 = do_ref[...]
        prod = do2.astype(F32) * o_ref[...]
        ones = jnp.ones((8, LANES), BF16)
        deltas, lses = [], []
        for hh in range(2):
            head = _head_mask(lane, hh)
            dom_ref[hh] = do2 * head.astype(BF16)
            part = prod * head
            hi = part.astype(BF16)
            lo = (part - hi.astype(F32)).astype(BF16)
            deltas.append((_dot_nt(ones, hi) + _dot_nt(ones, lo))[0:1])
            lses.append(lse_ref[hh:hh + 1, :])

        def keys(g):
            return pl.ds(pl.multiple_of(g * step, step), step)

        def heads():
            return [(hh, slice(hh * HEAD_PAD, (hh + 1) * HEAD_PAD)) for hh in range(2)]

        def put_products(g, slot, qlo=0):
            vblk = v_ref[keys(g), :]
            for hh, cols in heads():
                s_ref[slot, hh, :, qlo:] = _dot_nt(k_ref[keys(g), cols], q_ref[qlo:, cols])
                dp_ref[slot, hh, :, qlo:] = _dot_nt(vblk, dom_ref[hh, qlo:, :])

        def add_grads(dqs, g, slot, qlo=0):
            rows = keys(g)
            new, dv_parts = [], []
            for hh, cols in heads():
                ds = ds_ref[slot, hh, :, qlo:]
                dk_ref[rows, cols] += _dot(ds, q_ref[qlo:, cols])
                dv_parts.append(_dot(pb_ref[slot, hh, :, qlo:], dom_ref[hh, qlo:, :]))
                upd = dqs[hh][:, qlo:] + _dot_tn(k_ref[rows, cols], ds)
                new.append(upd if qlo == 0 else jnp.concatenate([dqs[hh][:, :qlo], upd], axis=1))
            dv_ref[rows, :] += dv_parts[0] + dv_parts[1]
            return tuple(new)

        def substep(g, slot, dqs, masked, prefetch, qlo=0, next_qlo=0, prev_qlo=0, first=False):
            if prefetch:
                put_products(g + 1, 1 - slot, next_qlo)
            if not first:
                dqs = add_grads(dqs, g - 1, 1 - slot, prev_qlo)
            for hh, _ in heads():
                p = jnp.exp2(s_ref[slot, hh, :, qlo:] - lses[hh][:, qlo:])
                if masked:
                    p = jnp.where(col[:, qlo:] + g * step <= row[:, qlo:] + i * bq, p, 0.0)
                ds_ref[slot, hh, :, qlo:] = (p * (dp_ref[slot, hh, :, qlo:] - deltas[hh][:, qlo:])).astype(BF16)
                pb_ref[slot, hh, :, qlo:] = p.astype(BF16)
            return dqs

        def first_query(t, masked):
            return t * step if (masked and bk == bq and 0 <= t < nsub) else 0

        def chunk(kb, dqs, masked, first=False):
            for t in range(nsub):
                last = masked and t == nsub - 1
                dqs = substep(nsub * kb + t, t % 2, dqs, masked, not last, first_query(t, masked),
                              first_query(t + 1, masked), first_query(t - 1, masked), first and t == 0)
            return dqs

        def finish(dqs):
            dqs = add_grads(dqs, nsub * n_full + nsub - 1, 1, first_query(nsub - 1, True))
            dq_ref[:, :HEAD_PAD] = dqs[0].T * MLA_SCALE
            dq_ref[:, HEAD_PAD:] = dqs[1].T * MLA_SCALE

        put_products(0, 0)
        zero = jnp.zeros((HEAD_PAD, bq), F32)

        @pl.when(n_full == 0)
        def _():
            finish(chunk(0, (zero, zero), True, first=True))

        @pl.when(n_full > 0)
        def _():
            dqs = chunk(0, (zero, zero), False, first=True)
            dqs = lax.fori_loop(1, n_full, lambda kb, dqs: chunk(kb, dqs, False), dqs)
            finish(chunk(n_full, dqs, True))

    return pl.pallas_call(
        body, name="mla_bwd",
        grid=(4, nq),
        in_specs=[pl.BlockSpec((bq, 2 * HEAD_PAD), lambda h, i: (i, h)),
                  pl.BlockSpec((s, 2 * HEAD_PAD), lambda h, i: (0, h)),
                  pl.BlockSpec((s, LANES), lambda h, i: (0, h)),
                  pl.BlockSpec((bq, LANES), lambda h, i: (i, h)),
                  pl.BlockSpec((bq, LANES), lambda h, i: (i, h)),
                  pl.BlockSpec((None, 8, bq), lambda h, i: (h, 0, i))],
        out_specs=[pl.BlockSpec((bq, 2 * HEAD_PAD), lambda h, i: (i, h)),
                   pl.BlockSpec((s, 2 * HEAD_PAD), lambda h, i: (0, h), pipeline_mode=pl.Buffered(1)),
                   pl.BlockSpec((s, LANES), lambda h, i: (0, h), pipeline_mode=pl.Buffered(1))],
        out_shape=[jax.ShapeDtypeStruct((s, MLA_PAD_WIDTH), F32), jax.ShapeDtypeStruct((s, MLA_PAD_WIDTH), F32),
                   jax.ShapeDtypeStruct((s, MLA_WIDTH), F32)],
        scratch_shapes=[pltpu.VMEM((2, bq, LANES), BF16),
                        pltpu.VMEM((2, 2, step, bq), F32), pltpu.VMEM((2, 2, step, bq), F32),
                        pltpu.VMEM((2, 2, step, bq), BF16), pltpu.VMEM((2, 2, step, bq), BF16)],
        compiler_params=_params(("arbitrary", "arbitrary"), VMEM_LIMIT),
    )(q, k, v, do, o, lse)


def _rms_bwd(d_out, inp, r, weight, n):
    normed = inp * r
    gw = d_out * weight
    d_in = r * (gw - normed * (jnp.sum(gw * normed, axis=-1, keepdims=True) * (1.0 / n)))
    return d_in, d_out * normed


def _mla_prep_bwd_call(dq, dk, dv, q0, k0, cqn, ckvn, c_q, c_kv, cos_t, sin_t,
                       q_lora_norm, kv_lora_norm, qhn_pad, khn_pad, w_uq_bf, w_uk_bf, w_uv_bf):
    s = dq.shape[0]
    tm = _row_tile(s, 512)

    def body(dq_ref, dk_ref, dv_ref, q0_ref, k0_ref, cqn_ref, ckvn_ref, cq_ref, ckv_ref,
             cos_ref, sin_ref, qln_ref, kvln_ref, qhn_ref, khn_ref, wuq_ref, wuk_ref, wuv_ref,
             dcq_ref, dckv_ref, dkr_ref, gwuq_ref, gwuk_ref, gwuv_ref, gqln_ref, gkvln_ref, gqhn_ref, gkhn_ref,
             dq0_ref, dk0_ref, tmp_ref):
        @pl.when(pl.program_id(0) == 0)
        def _():
            for ref in (gwuq_ref, gwuk_ref, gwuv_ref, gqln_ref, gkvln_ref, gqhn_ref, gkhn_ref):
                ref[...] = jnp.zeros_like(ref)

        cos_t, sin_t = cos_ref[...], sin_ref[...]
        lane = lax.broadcasted_iota(I32, (tm, LANES), 1)
        rope_lanes = jnp.logical_or(lane < ROPE_HALF,
                                    jnp.logical_and(lane >= HALF_LANES, lane < HALF_LANES + ROPE_HALF))
        heads = [slice(h * HEAD_PAD, (h + 1) * HEAD_PAD) for h in range(MLA_HEADS)]

        def head_norm_bwd(d_ref, x0_ref, w_ref, out_ref, scale):
            w = w_ref[...]
            inv = [lax.rsqrt(jnp.sum(x0_ref[:, cols] * x0_ref[:, cols], axis=-1, keepdims=True)
                             * (1.0 / MLA_QK_DIM) + EPS) for cols in heads]
            for cols in heads:
                tmp_ref[:, cols] = _rope_adjoint(d_ref[:, cols] * scale, cos_t, sin_t)
            dots = [jnp.sum(tmp_ref[:, cols] * w * (x0_ref[:, cols] * r), axis=-1, keepdims=True)
                    for cols, r in zip(heads, inv)]
            g_w = jnp.zeros((1, LANES), F32)
            rope_sum = jnp.zeros((tm, LANES), F32)
            for cols, r, dot in zip(heads, inv, dots):
                normed = x0_ref[:, cols] * r
                d_n = tmp_ref[:, cols]
                d_x0 = r * (d_n * w - normed * (dot * (1.0 / MLA_QK_DIM)))
                out_ref[:, cols] = d_x0.astype(BF16)
                g_w = g_w + jnp.sum(d_n * normed, axis=0, keepdims=True)
                rope_sum = rope_sum + jnp.where(rope_lanes, d_x0, 0.0)
            return g_w, rope_sum

        g_qhn, _ = head_norm_bwd(dq_ref, q0_ref, qhn_ref, dq0_ref, 1.0)
        g_khn, d_kr = head_norm_bwd(dk_ref, k0_ref, khn_ref, dk0_ref, LN2)
        cqn, ckvn = cqn_ref[...], ckvn_ref[...]
        d_q0b, d_k0b, dvb = dq0_ref[...], dk0_ref[...], dv_ref[...].astype(BF16)
        d_cqn = _dot_nt(d_q0b, wuq_ref[...])
        gwuq_ref[...] += _dot_tn(cqn, d_q0b)
        d_ckvn = _dot_nt(d_k0b, wuk_ref[...]) + _dot_nt(dvb, wuv_ref[...])
        gwuk_ref[...] += _dot_tn(ckvn, d_k0b)
        gwuv_ref[...] += _dot_tn(ckvn, dvb)
        gqhn_ref[...] += g_qhn
        gkhn_ref[...] += g_khn
        dkr_ref[...] = d_kr.astype(BF16)
        cq = cq_ref[...]
        rcq = lax.rsqrt(jnp.mean(cq * cq, axis=-1, keepdims=True) + EPS)
        d_cq, gl = _rms_bwd(d_cqn, cq, rcq, qln_ref[...], Q_LORA_RANK)
        dcq_ref[...] = d_cq.astype(BF16)
        gqln_ref[...] += jnp.sum(gl, axis=0, keepdims=True)
        ckv = ckv_ref[...]
        rckv = lax.rsqrt(jnp.mean(ckv * ckv, axis=-1, keepdims=True) + EPS)
        d_ckv, gl = _rms_bwd(d_ckvn, ckv, rckv, kvln_ref[...], KV_LORA_RANK)
        dckv_ref[...] = d_ckv.astype(BF16)
        gkvln_ref[...] += jnp.sum(gl, axis=0, keepdims=True)

    return pl.pallas_call(
        body, name="mla_prep_bwd",
        grid=(s // tm,),
        in_specs=[_rows(tm, MLA_PAD_WIDTH), _rows(tm, MLA_PAD_WIDTH), _rows(tm, MLA_WIDTH),
                  _rows(tm, MLA_PAD_WIDTH), _rows(tm, MLA_PAD_WIDTH),
                  _rows(tm, Q_LORA_RANK), _rows(tm, KV_LORA_RANK), _rows(tm, Q_LORA_RANK), _rows(tm, KV_LORA_RANK),
                  _rows(tm, LANES), _rows(tm, LANES),
                  _full((1, Q_LORA_RANK)), _full((1, KV_LORA_RANK)), _full((1, LANES)), _full((1, LANES)),
                  _full((Q_LORA_RANK, MLA_PAD_WIDTH)), _full((KV_LORA_RANK, MLA_PAD_WIDTH)),
                  _full((KV_LORA_RANK, MLA_WIDTH))],
        out_specs=[_rows(tm, Q_LORA_RANK), _rows(tm, KV_LORA_RANK), _rows(tm, LANES),
                   _full((Q_LORA_RANK, MLA_PAD_WIDTH)), _full((KV_LORA_RANK, MLA_PAD_WIDTH)),
                   _full((KV_LORA_RANK, MLA_WIDTH)),
                   _full((1, Q_LORA_RANK)), _full((1, KV_LORA_RANK)), _full((1, LANES)), _full((1, LANES))],
        out_shape=[jax.ShapeDtypeStruct((s, Q_LORA_RANK), BF16), jax.ShapeDtypeStruct((s, KV_LORA_RANK), BF16),
                   jax.ShapeDtypeStruct((s, LANES), BF16),
                   jax.ShapeDtypeStruct((Q_LORA_RANK, MLA_PAD_WIDTH), F32),
                   jax.ShapeDtypeStruct((KV_LORA_RANK, MLA_PAD_WIDTH), F32),
                   jax.ShapeDtypeStruct((KV_LORA_RANK, MLA_WIDTH), F32),
                   jax.ShapeDtypeStruct((1, Q_LORA_RANK), F32), jax.ShapeDtypeStruct((1, KV_LORA_RANK), F32),
                   jax.ShapeDtypeStruct((1, LANES), F32), jax.ShapeDtypeStruct((1, LANES), F32)],
        scratch_shapes=[pltpu.VMEM((tm, MLA_PAD_WIDTH), BF16), pltpu.VMEM((tm, MLA_PAD_WIDTH), BF16),
                        pltpu.VMEM((tm, MLA_PAD_WIDTH), F32)],
        compiler_params=_params(("arbitrary",), VMEM_LIMIT),
    )(dq, dk, dv, q0, k0, cqn, ckvn, c_q, c_kv, cos_t, sin_t,
      q_lora_norm, kv_lora_norm, qhn_pad, khn_pad, w_uq_bf, w_uk_bf, w_uv_bf)


def _dh_call(pieces, hb, x, dy, ada, norm_w, w_in_bf):
    s = x.shape[0]
    tm = _row_tile(s, 512)
    widths = [p.shape[1] for p in pieces]
    offsets = [sum(widths[:j]) for j in range(len(widths))]
    assert offsets[-1] + widths[-1] == IN_COLS_PAD
    n = len(pieces)

    def body(*refs):
        p_refs = refs[:n]
        (hb_ref, x_ref, dy_ref, sh_ref, sc_ref, nw_ref, w_ref, gx_ref, gw_ref, dsh_ref, dsc_ref, gnw_ref,
         dp_ref, acc_ref) = refs[n:]

        @pl.when(pl.program_id(0) == 0)
        def _():
            acc_ref[...] = jnp.zeros_like(acc_ref)
            dsh_ref[...] = jnp.zeros_like(dsh_ref)
            dsc_ref[...] = jnp.zeros_like(dsc_ref)
            gnw_ref[...] = jnp.zeros_like(gnw_ref)

        for p_ref, c0, width in zip(p_refs, offsets, widths):
            dp_ref[:, c0:c0 + width] = p_ref[...].astype(BF16)
        acc_ref[...] += _dot_tn(hb_ref[...], dp_ref[...])

        @pl.when(pl.program_id(0) == pl.num_programs(0) - 1)
        def _():
            gw_ref[...] = acc_ref[...].astype(BF16)

        dh = _dot_nt(dp_ref[...], w_ref[...])
        xx = x_ref[...]
        r0 = lax.rsqrt(jnp.mean(xx * xx, axis=-1, keepdims=True) + EPS)
        xn = xx * r0
        nw = nw_ref[...]
        dsh_ref[...] += jnp.sum(dh, axis=0, keepdims=True)
        dsc_ref[...] += jnp.sum(dh * (xn * nw), axis=0, keepdims=True)
        dn = dh * (1.0 + sc_ref[...])
        gnw_ref[...] += jnp.sum(dn * xn, axis=0, keepdims=True)
        dxn = dn * nw
        gx_ref[...] = dy_ref[...] + r0 * (dxn - xn * jnp.mean(dxn * xn, axis=-1, keepdims=True))

    return pl.pallas_call(
        body, name="in_proj_bwd",
        grid=(s // tm,),
        in_specs=[_rows(tm, w) for w in widths]
        + [_rows(tm, D_MODEL), _rows(tm, D_MODEL), _rows(tm, D_MODEL),
           _ada_part(0), _ada_part(1), _full((1, D_MODEL)),
           pl.BlockSpec((D_MODEL, IN_COLS_PAD), lambda i: (0, 0), pipeline_mode=pl.Buffered(1))],
        out_specs=[_rows(tm, D_MODEL),
                   pl.BlockSpec((D_MODEL, IN_COLS_PAD), lambda i: (0, 0), pipeline_mode=pl.Buffered(1)),
                   _full((1, D_MODEL)), _full((1, D_MODEL)), _full((1, D_MODEL))],
        out_shape=[jax.ShapeDtypeStruct((s, D_MODEL), F32), jax.ShapeDtypeStruct((D_MODEL, IN_COLS_PAD), BF16),
                   jax.ShapeDtypeStruct((1, D_MODEL), F32), jax.ShapeDtypeStruct((1, D_MODEL), F32),
                   jax.ShapeDtypeStruct((1, D_MODEL), F32)],
        scratch_shapes=[pltpu.VMEM((tm, IN_COLS_PAD), BF16), pltpu.VMEM((D_MODEL, IN_COLS_PAD), F32)],
        compiler_params=_params(("arbitrary",), VMEM_LIMIT),
    )(*pieces, hb, x, dy, ada, ada, norm_w, w_in_bf)


def _adamw(g, w, m, v):
    m = ADAM_B1 * m + (1.0 - ADAM_B1) * g
    v = ADAM_B2 * v + (1.0 - ADAM_B2) * (g * g)
    m_hat = m / (1.0 - ADAM_B1 ** ADAM_STEP)
    v_hat = v / (1.0 - ADAM_B2 ** ADAM_STEP)
    delta = -ADAM_LR * (m_hat / (jnp.sqrt(v_hat) + ADAM_EPS) + ADAM_WD * w)
    return delta, m, v


def _adam_shard_call(name, parts, w, m, v):
    r, c = w.shape
    tr = r if r <= 512 else 256

    def body(p_ref, w_ref, m_ref, v_ref, g_ref, d_ref, nm_ref, nv_ref):
        g = ((p_ref[0].astype(F32) + p_ref[1].astype(F32)) + p_ref[2].astype(F32)) + p_ref[3].astype(F32)
        g_ref[...] = g
        d_ref[...], nm_ref[...], nv_ref[...] = _adamw(g, w_ref[...], m_ref[...], v_ref[...])

    blk = pl.BlockSpec((tr, c), lambda i: (i, 0))
    return pl.pallas_call(
        body, name=name,
        grid=(r // tr,),
        in_specs=[pl.BlockSpec((4, tr, c), lambda i: (0, i, 0)), blk, blk, blk],
        out_specs=[blk] * 4,
        out_shape=[jax.ShapeDtypeStruct((r, c), F32)] * 4,
        compiler_params=_params(("arbitrary",), VMEM_LIMIT),
    )(parts, w, m, v)


def _adam_shard_transposed_call(name, parts, w_t, m_t, v_t):
    c, r = w_t.shape
    assert parts.shape[1:] == (r, c)
    tr = r if r <= 512 else 256
    c_pad = -(-c // LANES) * LANES

    def body(p_ref, w_ref, m_ref, v_ref, g_ref, d_ref, nm_ref, nv_ref, pad_ref):
        @pl.when(pl.program_id(0) == 0)
        def _():
            pad_ref[...] = jnp.zeros_like(pad_ref)

        pad_ref[:, :c] = ((p_ref[0].astype(F32) + p_ref[1].astype(F32)) + p_ref[2].astype(F32)) + p_ref[3].astype(F32)
        g = pad_ref[...].T[:c]
        g_ref[...] = g
        d_ref[...], nm_ref[...], nv_ref[...] = _adamw(g, w_ref[...], m_ref[...], v_ref[...])

    blk = pl.BlockSpec((c, tr), lambda i: (0, i))
    return pl.pallas_call(
        body, name=name,
        grid=(r // tr,),
        in_specs=[pl.BlockSpec((4, tr, c), lambda i: (0, i, 0)), blk, blk, blk],
        out_specs=[blk] * 4,
        out_shape=[jax.ShapeDtypeStruct((c, r), F32)] * 4,
        scratch_shapes=[pltpu.VMEM((tr, c_pad), F32)],
        compiler_params=_params(("arbitrary",), VMEM_LIMIT),
    )(parts, w_t, m_t, v_t)


def _adam_vectors_call(packs, offsets, vectors):
    nv = len(vectors)

    def body(*refs):
        p_ref, ins, outs = refs[0], refs[1:1 + 3 * nv], refs[1 + 3 * nv:]
        for j, off in enumerate(offsets):
            n = ins[3 * j].shape[1]
            span = -(-n // LANES) * LANES
            g = p_ref[0, :, off:off + span]
            for b in range(1, 8):
                g = g + p_ref[b, :, off:off + span]
            g = g[:, :n]
            outs[j][...] = g
            outs[nv + j][...], outs[2 * nv + j][...], outs[3 * nv + j][...] = _adamw(
                g, ins[3 * j][...], ins[3 * j + 1][...], ins[3 * j + 2][...])

    flat = [a for t in vectors for a in t]
    res = pl.pallas_call(
        body, name="adam_vectors",
        out_shape=[jax.ShapeDtypeStruct(t[0].shape, F32) for _ in range(4) for t in vectors],
    )(packs, *flat)
    return [res[k * nv:(k + 1) * nv] for k in range(4)]


ROPE_HALF = MLA_ROPE_DIM // 2
NOPE_A = MLA_NOPE_DIM - ROPE_HALF


def _zeros_like_lanes(t, n):
    return jnp.zeros(t.shape[:-1] + (n,), t.dtype)


def _to_head_lanes(t):
    nope, rope = t[..., :MLA_NOPE_DIM], t[..., MLA_NOPE_DIM:]
    return jnp.concatenate([rope[..., :ROPE_HALF], nope[..., :NOPE_A], rope[..., ROPE_HALF:], nope[..., NOPE_A:],
                            _zeros_like_lanes(t, HEAD_PAD - MLA_QK_DIM)], axis=-1)


def _from_head_lanes(t):
    return jnp.concatenate([t[..., ROPE_HALF:HALF_LANES], t[..., HALF_LANES + ROPE_HALF:MLA_QK_DIM],
                            t[..., :ROPE_HALF], t[..., HALF_LANES:HALF_LANES + ROPE_HALF]], axis=-1)


def _nope_to_head_lanes(t):
    return jnp.concatenate([_zeros_like_lanes(t, ROPE_HALF), t[..., :NOPE_A], _zeros_like_lanes(t, ROPE_HALF),
                            t[..., NOPE_A:], _zeros_like_lanes(t, HEAD_PAD - MLA_QK_DIM)], axis=-1)


def _rope_to_head_lanes(t):
    return jnp.concatenate([t[..., :ROPE_HALF], _zeros_like_lanes(t, HALF_LANES - ROPE_HALF), t[..., ROPE_HALF:],
                            _zeros_like_lanes(t, HALF_LANES - ROPE_HALF)], axis=-1)


def _rope_lane_freq():
    inv_freq = (ROPE_THETA ** (-jnp.arange(0, MLA_ROPE_DIM, 2, dtype=F32) / MLA_ROPE_DIM))[None]
    return _rope_to_head_lanes(jnp.concatenate([-inv_freq, inv_freq], axis=1))


def _unshard_cols(g):
    return jnp.transpose(g, (1, 0, 2)).reshape(g.shape[1], 4 * g.shape[2])


def _shard_cols(g):
    r, c4 = g.shape
    return jnp.transpose(g.reshape(r, 4, c4 // 4), (1, 0, 2))


def kernel(x, c, positions, w_ada, b_ada, norm_w, w_in, q_lora_norm, w_uq, kv_lora_norm, w_ukv, q_head_norm, k_head_norm, w_out, loss_target, m_w_ada, m_b_ada, m_norm_w, m_w_in, m_q_lora_norm, m_w_uq, m_kv_lora_norm, m_w_ukv, m_q_head_norm, m_k_head_norm, m_w_out, v_w_ada, v_b_ada, v_norm_w, v_w_in, v_q_lora_norm, v_w_uq, v_kv_lora_norm, v_w_ukv, v_q_head_norm, v_k_head_norm, v_w_out):
    chip = 2 * lax.axis_index("x") + lax.axis_index("y")
    me8 = 2 * chip + lax.axis_index("c")
    ada_cols = w_ada.shape[2]
    c_all = _allgather_rows_call(c)[:, 0, :]
    ada_part = _ada_call(c_all, w_ada[0], lax.dynamic_slice_in_dim(b_ada, chip * ada_cols, ada_cols, axis=1))
    ada_g, win_g, wuq_g, wukv_g, wout_g, cos_t, sin_t = _gather_call(
        [ada_part[None]] + [w.astype(BF16) for w in (w_in, w_uq, w_ukv, w_out)], [False, True, True, True, True],
        positions[0].astype(F32).reshape(-1, LANES), _rope_lane_freq())
    ada = lax.dynamic_slice_in_dim(ada_g, me8, 1, axis=1).reshape(1, 4 * ada_cols)
    (sq_sum, grad_x, g_w_in, g_w_uq, g_w_ukv, g_w_out, d_ada, g_norm_w, g_qln, g_kvln, g_qhn, g_khn) = _local_step(
        x[0], ada, (cos_t, sin_t), loss_target[0], norm_w, win_g,
        q_lora_norm, _unshard_cols(wuq_g), kv_lora_norm, _unshard_cols(wukv_g), q_head_norm, k_head_norm,
        wout_g.reshape(D_MODEL, D_MODEL))

    grads = [g.astype(BF16) for g in (g_w_in, _shard_cols(g_w_uq), _shard_cols(g_w_ukv),
                                      g_w_out.reshape(4, D_MODEL // 4, D_MODEL))]
    pieces = [d_ada, g_norm_w, g_qln, g_kvln, g_qhn, g_khn, (0.5 * sq_sum / D_MODEL).reshape(1, 1)]
    spans = [-(-p.shape[1] // LANES) * LANES for p in pieces]
    starts = [sum(spans[:j]) for j in range(len(spans))]
    small = jnp.concatenate([jnp.pad(p, ((0, 0), (0, sp - p.shape[1]))) for p, sp in zip(pieces, spans)], axis=1)
    core = lax.axis_index("c").astype(I32).reshape(1)
    chip_halves = _sum_halves_call(core, grads, _swap_halves_call(grads))
    names = ["adam_w_in", "adam_w_uq", "adam_w_ukv", "adam_w_out"]
    shard_w = [(w_in, m_w_in, v_w_in), (w_uq, m_w_uq, v_w_uq), (w_ukv, m_w_ukv, v_w_ukv),
               (w_out, m_w_out, v_w_out)]
    early = [a for a in (3, 2) if shard_w[a][0].shape[2] % LANES == 0]
    parts, packs, res_ada, res_early = _exchange_call(
        chip_halves, small, starts[0], jnp.pad(c_all, ((0, 8), (0, 0))), w_ada[0], m_w_ada[0], v_w_ada[0],
        [(a,) + tuple(t[0] for t in shard_w[a]) for a in early])
    loss = jnp.sum(packs[:, 0, starts[-1]])

    res = {names[a]: r for a, r in zip(early, res_early)}
    for a, (name, p_g, (w, m, v)) in enumerate(zip(names, parts, shard_w)):
        if a in early:
            continue
        if w.shape[2] % LANES:
            res_t = _adam_shard_transposed_call(name, p_g, w[0].T, m[0].T, v[0].T)
            res[name] = [t.T for t in res_t]
        else:
            res[name] = _adam_shard_call(name, p_g, w[0], m[0], v[0])
    vectors = [(b_ada, m_b_ada, v_b_ada), (norm_w, m_norm_w, v_norm_w), (q_lora_norm, m_q_lora_norm, v_q_lora_norm),
               (kv_lora_norm, m_kv_lora_norm, v_kv_lora_norm), (q_head_norm, m_q_head_norm, v_q_head_norm),
               (k_head_norm, m_k_head_norm, v_k_head_norm)]
    vec_out = _adam_vectors_call(packs, starts[:len(vectors)], vectors)

    def ordered(kind):
        big = lambda name: res[name][kind][None]
        return [res_ada[kind][None], vec_out[kind][0], vec_out[kind][1], big("adam_w_in"), vec_out[kind][2],
                big("adam_w_uq"), vec_out[kind][3], big("adam_w_ukv"), vec_out[kind][4], vec_out[kind][5],
                big("adam_w_out")]

    return (loss, grad_x[None], *ordered(0), *ordered(1), *ordered(2), *ordered(3))


def _local_step(x2, ada, rope_tables, tgt, norm_w, w_in_shards, q_lora_norm, w_uq_full,
                kv_lora_norm, w_ukv_full, q_head_norm, k_head_norm, w_out_full):
    in_shard = w_in_shards.shape[2]
    ckv_tail = C_GM - 3 * in_shard
    assert 0 <= ckv_tail and ckv_tail + MLA_ROPE_DIM + MLA_WIDTH == in_shard
    last = w_in_shards[3]
    w_in_bf = jnp.concatenate(
        [w_in_shards[0], w_in_shards[1], w_in_shards[2], last[:, :ckv_tail], last[:, ckv_tail + MLA_ROPE_DIM:],
         _rope_to_head_lanes(last[:, ckv_tail:ckv_tail + MLA_ROPE_DIM])], axis=1).astype(BF16)
    w_uq_bf = _to_head_lanes(w_uq_full.reshape(Q_LORA_RANK, MLA_HEADS, MLA_QK_DIM)).reshape(
        Q_LORA_RANK, MLA_PAD_WIDTH).astype(BF16)
    w_ukv_heads = w_ukv_full.reshape(KV_LORA_RANK, MLA_HEADS, 2 * MLA_NOPE_DIM)
    w_uk_bf = _nope_to_head_lanes(w_ukv_heads[:, :, :MLA_NOPE_DIM]).reshape(KV_LORA_RANK, MLA_PAD_WIDTH).astype(BF16)
    w_uv_bf = w_ukv_heads[:, :, MLA_NOPE_DIM:].reshape(KV_LORA_RANK, MLA_WIDTH).astype(BF16)
    w_out_bf = w_out_full.astype(BF16)
    qhn_pad, khn_pad = _to_head_lanes(q_head_norm), _to_head_lanes(k_head_norm)
    cos_t, sin_t = rope_tables

    hb, q_sb, k_sb, v_sb, g_sb, c_q, c_kv, g_mla, q_m, k_m, v_m, cqn, ckvn, q0, k0 = _pre_call(
        x2, ada, norm_w, w_in_bf, cos_t, sin_t, q_lora_norm, kv_lora_norm, qhn_pad, khn_pad,
        w_uq_bf, w_uk_bf, w_uv_bf)
    o_sb, r_sb, kstart = _sb_fwd_call(q_sb, k_sb, v_sb)
    o_mla, lse = _mla_fwd_call(q_m, k_m, v_m)
    do_sb, do_mla, dg_sb, dg_mla, dy, g_w_out, d_gate, sq = _out_call(
        o_sb, g_sb, o_mla, g_mla, x2, tgt, ada, w_out_bf)

    dq_sb, dk_sb, dv_sb = _sb_bwd_call(kstart, q_sb, k_sb, v_sb, do_sb, r_sb)
    dq_m, dk_m, dv_m = _mla_bwd_call(q_m, k_m, v_m, do_mla, o_mla, lse)
    (d_cq, d_ckv, d_kr, g_wuq_pad, g_wuk_pad, g_wuv, g_qln, g_kvln, g_qhn, g_khn) = _mla_prep_bwd_call(
        dq_m, dk_m, dv_m, q0, k0, cqn, ckvn, c_q, c_kv, cos_t, sin_t,
        q_lora_norm, kv_lora_norm, qhn_pad, khn_pad, w_uq_bf, w_uk_bf, w_uv_bf)
    grad_x, g_win_pad, d_shift, d_scale, g_norm_w = _dh_call(
        [dq_sb, dk_sb, dv_sb, dg_sb, d_cq, d_ckv, dg_mla, d_kr], hb, x2, dy, ada, norm_w, w_in_bf)

    g_kr = g_win_pad[:, C_KR:]
    g_last = jnp.concatenate([g_win_pad[:, 3 * in_shard:C_GM], g_kr[:, :ROPE_HALF],
                              g_kr[:, HALF_LANES:HALF_LANES + ROPE_HALF], g_win_pad[:, C_GM:C_KR]], axis=1)
    g_w_in = jnp.stack([g_win_pad[:, j * in_shard:(j + 1) * in_shard] for j in range(3)] + [g_last])
    g_w_uq = _from_head_lanes(g_wuq_pad.reshape(Q_LORA_RANK, MLA_HEADS, HEAD_PAD)).reshape(Q_LORA_RANK, -1)
    g_w_ukv = jnp.concatenate(
        [_from_head_lanes(g_wuk_pad.reshape(KV_LORA_RANK, MLA_HEADS, HEAD_PAD))[:, :, :MLA_NOPE_DIM],
         g_wuv.reshape(KV_LORA_RANK, MLA_HEADS, MLA_NOPE_DIM)], axis=2).reshape(KV_LORA_RANK, -1)
    d_ada = jnp.concatenate([d_shift, d_scale, d_gate], axis=1)
    return (jnp.sum(sq), grad_x, g_w_in, g_w_uq, g_w_ukv, g_w_out, d_ada, g_norm_w, g_qln, g_kvln,
            _from_head_lanes(g_qhn), _from_head_lanes(g_khn))
```

```python
import math

import jax
import jax.numpy as jnp
from jax import lax
from jax.experimental import pallas as pl
from jax.experimental.pallas import tpu as pltpu

F32 = jnp.float32
BF16 = jnp.bfloat16
I32 = jnp.int32

D_MODEL = 1024
SB_HEADS = 8
SB_WIDTH = 512
MLA_HEADS = 8
MLA_QK_DIM = 96
MLA_NOPE_DIM = 64
MLA_ROPE_DIM = 32
MLA_WIDTH = 512
Q_LORA_RANK = 384
KV_LORA_RANK = 256
ROPE_THETA = 10000.0
EPS = 1e-6
LANES = 128
HALF_LANES = LANES // 2
HEAD_PAD = 128
MLA_PAD_WIDTH = MLA_HEADS * HEAD_PAD

C_Q, C_K, C_V, C_G = 0, 512, 1024, 1536
C_CQ, C_CKV, C_GM, C_KR = 2048, 2432, 2688, 3200
IN_COLS_PAD = 3328

ADAM_LR = 0.001
ADAM_B1 = 0.9
ADAM_B2 = 0.999
ADAM_EPS = 1e-08
ADAM_WD = 0.01
ADAM_STEP = 10

SB_SCALE = 0.125
SB_GROUP = 4
MLA_SCALE = 1.0 / math.sqrt(MLA_QK_DIM)
LN2 = math.log(2.0)
MLA_SCALE_LOG2 = MLA_SCALE / LN2
MLA_BQ = 1024
MLA_BWD_BQ = 1024
MLA_BK = 1024
MLA_BWD_BK = 1024
MLA_STEP = 512
MLA_BWD_STEP = 256
SB_DEAD = -104.0
MASK_NEG = -1e30

VMEM_LIMIT = 56 * 1024 * 1024
MESH = pl.DeviceIdType.MESH


def _dot(a, b):
    return jnp.dot(a, b, preferred_element_type=F32)


def _dot_nt(a, b):
    return lax.dot_general(a, b, (((1,), (1,)), ((), ())), preferred_element_type=F32)


def _dot_tn(a, b):
    return lax.dot_general(a, b, (((0,), (0,)), ((), ())), preferred_element_type=F32)


def _sigmoid(x):
    return 1.0 / (1.0 + jnp.exp(-x))


def _split_dot(a, m):
    hi = a.astype(BF16)
    lo = (a - hi.astype(F32)).astype(BF16)
    return _dot(hi, m) + _dot(lo, m)


def _params(sem, vmem=None):
    return pltpu.CompilerParams(dimension_semantics=sem, vmem_limit_bytes=vmem)


def _row_tile(s, want):
    return min(want, s)


def _hbm_spec():
    return pl.BlockSpec(memory_space=pltpu.HBM)


def _allgather_rows_call(row):
    def body(in_ref, out_ref, send_sems, recv_sems, loc_sem):
        x, y, c = lax.axis_index("x"), lax.axis_index("y"), lax.axis_index("c")
        flips = [(fx, fy, fc) for fx in (0, 1) for fy in (0, 1) for fc in (0, 1)][1:]

        def peer(r):
            fx, fy, fc = flips[r]
            return ((1 - x) if fx else x, (1 - y) if fy else y, (1 - c) if fc else c)

        def copy(r, slot):
            return pltpu.make_async_remote_copy(
                src_ref=in_ref, dst_ref=out_ref.at[slot], send_sem=send_sems.at[r], recv_sem=recv_sems.at[r],
                device_id=peer(r), device_id_type=MESH)

        local = pltpu.make_async_copy(in_ref, out_ref.at[4 * x + 2 * y + c], loc_sem)
        local.start()
        sends = [copy(r, 4 * x + 2 * y + c) for r in range(7)]
        for cp in sends:
            cp.start()
        for r in range(7):
            px, py, pc = peer(r)
            copy(r, 4 * px + 2 * py + pc).wait_recv()
        for cp in sends:
            cp.wait_send()
        local.wait()

    return pl.pallas_call(
        body, name="gather_rows",
        out_shape=jax.ShapeDtypeStruct((8,) + row.shape, row.dtype),
        in_specs=[_hbm_spec()], out_specs=_hbm_spec(),
        scratch_shapes=[pltpu.SemaphoreType.DMA((7,)), pltpu.SemaphoreType.DMA((7,)), pltpu.SemaphoreType.DMA],
    )(row)


def _gather_call(shards, split, positions, lane_freq):
    n = len(shards)
    halves = [s.shape[1] // 2 for s in shards]
    s_len = positions.shape[0] * LANES
    table_rows = 4 * LANES
    n_chunks = s_len // table_rows
    assert n_chunks * table_rows == s_len and n_chunks >= 2

    def body(*refs):
        ins, pos_ref, frq_ref = refs[:n], refs[n], refs[n + 1]
        outs, cos_out, sin_out = refs[n + 2:2 * n + 2], refs[2 * n + 2], refs[2 * n + 3]
        ici_send, ici_recv, d2d_send, d2d_recv, loc_sems, cos_buf, sin_buf, tab_sems = refs[2 * n + 4:]
        x, y, c = lax.axis_index("x"), lax.axis_index("y"), lax.axis_index("c")
        me = 2 * x + y
        peers = [(1 - x, y), (x, 1 - y), (1 - x, 1 - y)]

        def rows(a, which):
            return pl.ds(pl.multiple_of(which * halves[a], 16), halves[a])

        def ici(a, j, slot):
            px, py = peers[j]
            src, dst = ins[a].at[0], outs[a].at[slot]
            if split[a]:
                src, dst = src.at[rows(a, c)], dst.at[rows(a, c)]
            return pltpu.make_async_remote_copy(
                src_ref=src, dst_ref=dst,
                send_sem=ici_send.at[3 * a + j], recv_sem=ici_recv.at[3 * a + j],
                device_id=(px, py, c), device_id_type=MESH)

        def d2d(a, j, which):
            px, py = peers[j]
            piece = outs[a].at[2 * px + py, rows(a, which)]
            return pltpu.make_async_remote_copy(
                src_ref=piece, dst_ref=piece,
                send_sem=d2d_send.at[3 * a + j], recv_sem=d2d_recv.at[3 * a + j],
                device_id=(x, y, 1 - c), device_id_type=MESH)

        local = [pltpu.make_async_copy(ins[a].at[0], outs[a].at[me], loc_sems.at[a]) for a in range(n)]
        for cp in local:
            cp.start()
        sends = [ici(a, j, me) for a in range(n) for j in range(3)]
        for cp in sends:
            cp.start()

        def table_copies(t, slot):
            rows = pl.ds(pl.multiple_of(t * table_rows, table_rows), table_rows)
            return [pltpu.make_async_copy(buf.at[slot], out.at[rows], tab_sems.at[k, slot])
                    for k, (buf, out) in enumerate(((cos_buf, cos_out), (sin_buf, sin_out)))]

        def tables(t, _):
            slot = t % 2

            @pl.when(t >= 2)
            def _():
                for cp in table_copies(t - 2, slot):
                    cp.wait()

            for b in range(table_rows // LANES):
                lane_pos = pos_ref[pl.ds(t * (table_rows // LANES) + b, 1), :]
                ang = jnp.broadcast_to(lane_pos, (LANES, LANES)).T * frq_ref[...]
                cos_buf[slot, b * LANES:(b + 1) * LANES, :] = jnp.cos(ang)
                sin_buf[slot, b * LANES:(b + 1) * LANES, :] = jnp.sin(ang)
            for cp in table_copies(t, slot):
                cp.start()
            return 0

        lax.fori_loop(0, n_chunks, tables, 0)
        for t in (n_chunks - 2, n_chunks - 1):
            for cp in table_copies(t, t % 2):
                cp.wait()
        for a in range(n):
            for j in range(3):
                px, py = peers[j]
                ici(a, j, 2 * px + py).wait_recv()
                if split[a]:
                    cp = d2d(a, j, c)
                    cp.start()
                    sends.append(cp)
        for a in range(n):
            for j in range(3):
                if split[a]:
                    d2d(a, j, 1 - c).wait_recv()
        for cp in sends:
            cp.wait_send()
        for cp in local:
            cp.wait()

    return pl.pallas_call(
        body, name="gather_weights",
        out_shape=[jax.ShapeDtypeStruct((4,) + s.shape[1:], s.dtype) for s in shards]
        + [jax.ShapeDtypeStruct((s_len, LANES), F32)] * 2,
        in_specs=[_hbm_spec() for _ in shards] + [pl.BlockSpec(memory_space=pltpu.VMEM)] * 2,
        out_specs=[_hbm_spec() for _ in range(n + 2)],
        scratch_shapes=[pltpu.SemaphoreType.DMA((3 * n,)), pltpu.SemaphoreType.DMA((3 * n,)),
                        pltpu.SemaphoreType.DMA((3 * n,)), pltpu.SemaphoreType.DMA((3 * n,)),
                        pltpu.SemaphoreType.DMA((n,)),
                        pltpu.VMEM((2, table_rows, LANES), F32), pltpu.VMEM((2, table_rows, LANES), F32),
                        pltpu.SemaphoreType.DMA((2, 2))],
    )(*shards, positions, lane_freq)


def _swap_halves_call(grads):
    n = len(grads)
    halves = [g.shape[1] // 2 for g in grads]

    def body(*refs):
        g_in, hs, send_sems, recv_sems = refs[:n], refs[n:2 * n], refs[2 * n], refs[2 * n + 1]
        x, y, c = lax.axis_index("x"), lax.axis_index("y"), lax.axis_index("c")
        copies = []
        for a in range(n):
            theirs = pl.ds(pl.multiple_of((1 - c) * halves[a], 16), halves[a])
            copies.append(pltpu.make_async_remote_copy(
                src_ref=g_in[a].at[:, theirs], dst_ref=hs[a], send_sem=send_sems.at[a], recv_sem=recv_sems.at[a],
                device_id=(x, y, 1 - c), device_id_type=MESH))
        for cp in copies:
            cp.start()
        for cp in copies:
            cp.wait()

    return pl.pallas_call(
        body, name="swap_halves",
        out_shape=[jax.ShapeDtypeStruct((4, h, g.shape[2]), g.dtype) for g, h in zip(grads, halves)],
        in_specs=[_hbm_spec() for _ in grads], out_specs=[_hbm_spec() for _ in grads],
        scratch_shapes=[pltpu.SemaphoreType.DMA((n,)), pltpu.SemaphoreType.DMA((n,))],
    )(*grads)


def _sum_halves_call(core, grads, halves):
    n = len(grads)

    def body(core_ref, *refs):
        for g_ref, h_ref, o_ref in zip(refs[:n], refs[n:2 * n], refs[2 * n:]):
            o_ref[...] = (g_ref[...].astype(F32) + h_ref[...].astype(F32)).astype(o_ref.dtype)

    whole = lambda h: pl.BlockSpec(h.shape, lambda i, core_ref: (0, 0, 0))
    return pl.pallas_call(
        body, name="sum_halves",
        grid_spec=pltpu.PrefetchScalarGridSpec(
            num_scalar_prefetch=1, grid=(1,),
            in_specs=[pl.BlockSpec(h.shape, lambda i, core_ref: (0, core_ref[0], 0)) for h in halves]
            + [whole(h) for h in halves],
            out_specs=[whole(h) for h in halves]),
        out_shape=[jax.ShapeDtypeStruct(h.shape, h.dtype) for h in halves],
        compiler_params=_params(("arbitrary",), VMEM_LIMIT),
    )(core, *grads, *halves)


def _exchange_call(chip_halves, small, ada_lane0, c_all, w_ada, m_ada, v_ada):
    n = len(chip_halves)
    halves = [h.shape[1] for h in chip_halves]
    d_model, cols = w_ada.shape
    tr = 256

    def body(*refs):
        g_in, small_in, c_in, wmv_in = refs[:n], refs[n], refs[n + 1], refs[n + 2:n + 5]
        parts, packs, ada_out = refs[n + 5:2 * n + 5], refs[2 * n + 5], refs[2 * n + 6:2 * n + 10]
        (ici_send, ici_recv, d2d_send, d2d_recv, sm_send, sm_recv, loc_sems,
         c_buf, d_buf, wmv_buf, res_buf, ada_sems) = refs[2 * n + 10:]
        x, y, c = lax.axis_index("x"), lax.axis_index("y"), lax.axis_index("c")
        me = 2 * x + y
        me8 = 4 * x + 2 * y + c
        sibling = (x, y, 1 - c)
        peers = [(1 - x, y), (x, 1 - y), (1 - x, 1 - y)]
        flips = [(fx, fy, fc) for fx in (0, 1) for fy in (0, 1) for fc in (0, 1)][1:]

        def rows(a, which):
            return pl.ds(pl.multiple_of(which * halves[a], 16), halves[a])

        def ici(a, j, src_slot, dst_slot):
            px, py = peers[j]
            return pltpu.make_async_remote_copy(
                src_ref=g_in[a].at[src_slot], dst_ref=parts[a].at[dst_slot, rows(a, c)],
                send_sem=ici_send.at[3 * a + j], recv_sem=ici_recv.at[3 * a + j],
                device_id=(px, py, c), device_id_type=MESH)

        def d2d(a, rel, chip, which):
            piece = parts[a].at[chip, rows(a, which)]
            return pltpu.make_async_remote_copy(
                src_ref=piece, dst_ref=piece,
                send_sem=d2d_send.at[4 * a + rel], recv_sem=d2d_recv.at[4 * a + rel],
                device_id=sibling, device_id_type=MESH)

        def flipped(r):
            fx, fy, fc = flips[r]
            return ((1 - x) if fx else x, (1 - y) if fy else y, (1 - c) if fc else c)

        def sm(r, slot):
            return pltpu.make_async_remote_copy(
                src_ref=small_in, dst_ref=packs.at[slot],
                send_sem=sm_send.at[r], recv_sem=sm_recv.at[r],
                device_id=flipped(r), device_id_type=MESH)

        def peer8(r):
            px, py, pc = flipped(r)
            return 4 * px + 2 * py + pc

        def adam_ada():
            lanes = pl.ds(pl.multiple_of(ada_lane0 + me * cols, LANES), cols)
            d_rows = pltpu.make_async_copy(packs.at[:, 0, lanes], d_buf.at[pl.ds(0, 8), :], ada_sems.at[4])
            d_buf[8:, :] = jnp.zeros((8, cols), F32)
            d_rows.start()
            for cp in [d_rows] + ada_loads:
                cp.wait()
            cc = c_buf[...]
            sc = cc * _sigmoid(cc)
            dd = d_buf[...]
            sc_hi = sc.astype(BF16)
            sc_lo = (sc - sc_hi.astype(F32)).astype(BF16)
            dd_hi = dd.astype(BF16)
            dd_lo = (dd - dd_hi.astype(F32)).astype(BF16)
            for t in range(d_model // tr):
                rws = slice(t * tr, (t + 1) * tr)
                g = _dot_tn(sc_hi[:, rws], dd_hi) + (_dot_tn(sc_hi[:, rws], dd_lo) + _dot_tn(sc_lo[:, rws], dd_hi))
                res_buf[0, rws, :] = g
                res_buf[1, rws, :], res_buf[2, rws, :], res_buf[3, rws, :] = _adamw(
                    g, wmv_buf[0, rws, :], wmv_buf[1, rws, :], wmv_buf[2, rws, :])
            stores = [pltpu.make_async_copy(res_buf.at[k], ada_out[k], ada_sems.at[5 + k]) for k in range(4)]
            for cp in stores:
                cp.start()
            return stores

        ada_loads = [pltpu.make_async_copy(c_in, c_buf, ada_sems.at[0])]
        ada_loads += [pltpu.make_async_copy(wmv_in[k], wmv_buf.at[k], ada_sems.at[1 + k]) for k in range(3)]
        for cp in ada_loads:
            cp.start()
        local = [pltpu.make_async_copy(g_in[a].at[me], parts[a].at[me, rows(a, c)], loc_sems.at[a])
                 for a in range(n)]
        local.append(pltpu.make_async_copy(small_in, packs.at[me8], loc_sems.at[n]))
        for cp in local:
            cp.start()
        sends = []
        for r in range(7):
            sends.append(sm(r, me8))
        for a in range(n):
            for j in range(3):
                px, py = peers[j]
                sends.append(ici(a, j, 2 * px + py, me))
        for cp in sends:
            cp.start()
        for a in range(n):
            local[a].wait()
            cp = d2d(a, 0, me, c)
            cp.start()
            sends.append(cp)
        for r in range(7):
            sm(r, peer8(r)).wait_recv()
        local[n].wait()
        ada_stores = adam_ada()
        for a in range(n):
            for j in range(3):
                px, py = peers[j]
                ici(a, j, me, 2 * px + py).wait_recv()
                cp = d2d(a, 1 + j, 2 * px + py, c)
                cp.start()
                sends.append(cp)
        for a in range(n):
            d2d(a, 0, me, 1 - c).wait_recv()
            for j in range(3):
                px, py = peers[j]
                d2d(a, 1 + j, 2 * px + py, 1 - c).wait_recv()
        for cp in sends:
            cp.wait_send()
        for cp in ada_stores:
            cp.wait()

    out_shape = ([jax.ShapeDtypeStruct((4, 2 * h.shape[1], h.shape[2]), h.dtype) for h in chip_halves]
                 + [jax.ShapeDtypeStruct((8,) + small.shape, small.dtype)]
                 + [jax.ShapeDtypeStruct((d_model, cols), F32)] * 4)
    res = pl.pallas_call(
        body, name="exchange_grads",
        out_shape=out_shape,
        in_specs=[_hbm_spec() for _ in range(n + 5)],
        out_specs=[_hbm_spec() for _ in range(n + 5)],
        scratch_shapes=[pltpu.SemaphoreType.DMA((3 * n,)), pltpu.SemaphoreType.DMA((3 * n,)),
                        pltpu.SemaphoreType.DMA((4 * n,)), pltpu.SemaphoreType.DMA((4 * n,)),
                        pltpu.SemaphoreType.DMA((7,)), pltpu.SemaphoreType.DMA((7,)),
                        pltpu.SemaphoreType.DMA((n + 1,)),
                        pltpu.VMEM(c_all.shape, F32), pltpu.VMEM((16, cols), F32),
                        pltpu.VMEM((3, d_model, cols), F32), pltpu.VMEM((4, d_model, cols), F32),
                        pltpu.SemaphoreType.DMA((9,))],
        compiler_params=pltpu.CompilerParams(vmem_limit_bytes=VMEM_LIMIT),
    )(*chip_halves, small, c_all, w_ada, m_ada, v_ada)
    return res[:n], res[n], res[n + 1:]


def _ada_call(c_all, w_ada_cols, b_ada_cols):
    def body(c_ref, w_ref, b_ref, o_ref):
        cc = c_ref[...]
        o_ref[...] = _dot((cc * _sigmoid(cc)).astype(BF16), w_ref[...].astype(BF16)) + b_ref[...]

    return pl.pallas_call(
        body, name="ada_fwd",
        out_shape=jax.ShapeDtypeStruct((c_all.shape[0], w_ada_cols.shape[1]), F32),
        compiler_params=pltpu.CompilerParams(vmem_limit_bytes=VMEM_LIMIT),
    )(c_all, w_ada_cols, b_ada_cols)


def _ada_part(j):
    return pl.BlockSpec((1, D_MODEL), lambda i: (0, j))


def _full(shape):
    return pl.BlockSpec(shape, lambda i: (0,) * len(shape))


def _rows(tm, width):
    return pl.BlockSpec((tm, width), lambda i: (i, 0))


def _rope(t, cos_t, sin_t):
    return t * cos_t + pltpu.roll(t, HALF_LANES, 1) * sin_t


def _rope_adjoint(d, cos_t, sin_t):
    return d * cos_t + pltpu.roll(d * sin_t, HALF_LANES, 1)


def _pre_call(x, ada, norm_w, w_in_bf, cos_t, sin_t, q_lora_norm, kv_lora_norm, qhn_pad, khn_pad,
              w_uq_bf, w_uk_bf, w_uv_bf):
    s = x.shape[0]
    tm = _row_tile(s, 512)
    out_defs = [(D_MODEL, BF16), (512, BF16), (512, BF16), (512, BF16), (512, F32),
                (Q_LORA_RANK, F32), (KV_LORA_RANK, F32), (512, F32),
                (MLA_PAD_WIDTH, BF16), (MLA_PAD_WIDTH, BF16), (MLA_WIDTH, BF16),
                (Q_LORA_RANK, BF16), (KV_LORA_RANK, BF16), (MLA_PAD_WIDTH, F32), (MLA_PAD_WIDTH, F32)]

    def body(x_ref, sh_ref, sc_ref, nw_ref, w_ref, cos_ref, sin_ref, qln_ref, kvln_ref, qhn_ref, khn_ref,
             wuq_ref, wuk_ref, wuv_ref,
             hb_ref, qsb_ref, ksb_ref, vsb_ref, gsb_ref, cq_ref, ckv_ref, gm_ref,
             q_ref, k_ref, v_ref, cqn_ref, ckvn_ref, q0_ref, k0_ref):
        xx = x_ref[...]
        r0 = lax.rsqrt(jnp.mean(xx * xx, axis=-1, keepdims=True) + EPS)
        hb = ((xx * r0 * nw_ref[...]) * (1.0 + sc_ref[...]) + sh_ref[...]).astype(BF16)
        hb_ref[...] = hb

        def proj(c0, width):
            return _dot(hb, w_ref[:, c0:c0 + width])

        cq = proj(C_CQ, Q_LORA_RANK)
        ckv = proj(C_CKV, KV_LORA_RANK)
        kr = proj(C_KR, LANES)
        cq_ref[...] = cq
        ckv_ref[...] = ckv
        cqn = (cq * lax.rsqrt(jnp.mean(cq * cq, axis=-1, keepdims=True) + EPS) * qln_ref[...]).astype(BF16)
        cqn_ref[...] = cqn
        ckvn = (ckv * lax.rsqrt(jnp.mean(ckv * ckv, axis=-1, keepdims=True) + EPS) * kvln_ref[...]).astype(BF16)
        ckvn_ref[...] = ckvn
        qsb_ref[...] = proj(C_Q, 512).astype(BF16)
        v_ref[...] = _dot(ckvn, wuv_ref[...]).astype(BF16)
        q0_ref[...] = _dot(cqn, wuq_ref[...])
        k0_ref[...] = _dot(ckvn, wuk_ref[...])
        ksb_ref[...] = proj(C_K, 512).astype(BF16)
        cos_t, sin_t = cos_ref[...], sin_ref[...]
        heads = [slice(h * HEAD_PAD, (h + 1) * HEAD_PAD) for h in range(MLA_HEADS)]
        for cols in heads:
            k0_ref[:, cols] = k0_ref[:, cols] + kr

        def inv_rms(ref):
            sums = [jnp.sum(ref[:, cols] * ref[:, cols], axis=-1, keepdims=True) for cols in heads]
            return [lax.rsqrt(t * (1.0 / MLA_QK_DIM) + EPS) for t in sums]

        rqs = inv_rms(q0_ref)
        vsb_ref[...] = proj(C_V, 512).astype(BF16)
        rks = inv_rms(k0_ref)
        gsb_ref[...] = proj(C_G, 512)
        for cols, rq, rk in zip(heads, rqs, rks):
            q_ref[:, cols] = (_rope(q0_ref[:, cols] * rq * qhn_ref[...], cos_t, sin_t) * MLA_SCALE_LOG2).astype(BF16)
            k_ref[:, cols] = _rope(k0_ref[:, cols] * rk * khn_ref[...], cos_t, sin_t).astype(BF16)
        gm_ref[...] = proj(C_GM, 512)

    return pl.pallas_call(
        body, name="pre_proj",
        grid=(s // tm,),
        in_specs=[_rows(tm, D_MODEL), _ada_part(0), _ada_part(1), _full((1, D_MODEL)),
                  pl.BlockSpec((D_MODEL, IN_COLS_PAD), lambda i: (0, 0), pipeline_mode=pl.Buffered(1)),
                  _rows(tm, LANES), _rows(tm, LANES),
                  _full((1, Q_LORA_RANK)), _full((1, KV_LORA_RANK)), _full((1, LANES)), _full((1, LANES)),
                  _full((Q_LORA_RANK, MLA_PAD_WIDTH)), _full((KV_LORA_RANK, MLA_PAD_WIDTH)),
                  _full((KV_LORA_RANK, MLA_WIDTH))],
        out_specs=[_rows(tm, w) for w, _ in out_defs],
        out_shape=[jax.ShapeDtypeStruct((s, w), dt) for w, dt in out_defs],
        compiler_params=_params(("arbitrary",), VMEM_LIMIT),
    )(x, ada, ada, norm_w, w_in_bf, cos_t, sin_t, q_lora_norm, kv_lora_norm, qhn_pad, khn_pad,
      w_uq_bf, w_uk_bf, w_uv_bf)


def _log_sigmoid_pair(z):
    ls = jnp.minimum(z, 0.0) - jnp.log(1.0 + jnp.exp(-jnp.abs(z)))
    return ls, ls - z


def _pair(hh):
    return slice((hh // 2) * LANES, (hh // 2 + 1) * LANES)


def _sb_fwd_call(q, k, v):
    s = q.shape[0]
    bq = _row_tile(s, 256)
    nq = s // bq
    nh = SB_GROUP
    width = nh * HALF_LANES

    def body(q_ref, k_ref, v_ref, o_ref, r_ref, ks_ref):
        hp, i = pl.program_id(0), pl.program_id(1)
        lane = lax.broadcasted_iota(I32, (bq, LANES), 1)
        row = lax.broadcasted_iota(I32, (bq, bq), 0)
        col = lax.broadcasted_iota(I32, (bq, bq), 1)
        strict = col < row
        later = jnp.where(row > col, 1.0, 0.0).astype(BF16)
        masks = [_head_mask(lane, hh).astype(BF16) for hh in range(2)]
        qms = [q_ref[:, _pair(hh)] * jnp.asarray(SB_SCALE, BF16) * masks[hh % 2] for hh in range(nh)]

        def walk(blocks, state):
            chains = [(kb, diagonal, hh) for kb, diagonal in blocks for hh in range(nh)]
            keys = lambda kb: pl.ds(pl.multiple_of(kb * bq, bq), bq)
            zs = [_dot_nt(qms[hh], k_ref[keys(kb), _pair(hh)]) for kb, _, hh in chains]
            pairs = []
            for z, (_, diagonal, _) in zip(zs, chains):
                ls, lk = _log_sigmoid_pair(z)
                pairs.append((ls, jnp.where(strict, lk, 0.0) if diagonal else lk))
            sums = [_split_dot(lk, later) for _, lk in pairs]
            runs = [st[0] for st in state]
            ws = []
            for (ls, lk), after, (_, diagonal, hh) in zip(pairs, sums, chains):
                w = jnp.exp(ls + (after + runs[hh]))
                ws.append((jnp.where(strict, w, 0.0) if diagonal else w).astype(BF16))
                runs[hh] = runs[hh] + jnp.sum(lk, axis=1, keepdims=True)
            accs = [st[1] for st in state]
            for w, (kb, _, hh) in zip(ws, chains):
                accs[hh] = accs[hh] + _dot(w, v_ref[keys(kb), _pair(hh)])
            return tuple(zip(runs, accs))

        def alive(state):
            top = jnp.max(state[0][0])
            for st in state[1:]:
                top = jnp.maximum(top, jnp.max(st[0]))
            return (top > SB_DEAD).astype(I32)

        def finish(state, first):
            ks_ref[hp, i] = first
            for pair in range(nh // 2):
                o_ref[:, _pair(2 * pair)] = jnp.where(lane < HALF_LANES, state[2 * pair][1], state[2 * pair + 1][1])
                r_ref[:, _pair(2 * pair)] = jnp.where(lane < HALF_LANES, state[2 * pair][0], state[2 * pair + 1][0])

        zero = ((jnp.zeros((bq, 1), F32), jnp.zeros((bq, LANES), F32)),) * nh

        @pl.when(i == 0)
        def _():
            finish(walk([(0, True)], zero), 0)

        @pl.when(i > 0)
        def _():
            state = walk([(i, True), (i - 1, False)], zero)

            def cond(carry):
                return jnp.logical_and(carry[0] >= 0, carry[1] > 0)

            def step(carry):
                state = walk([(carry[0], False)], carry[2])
                return carry[0] - 1, alive(state), state

            kb, _, state = lax.while_loop(cond, step, (i - 2, alive(state), state))
            finish(state, kb + 1)

    return pl.pallas_call(
        body, name="sb_fwd",
        grid=(SB_HEADS // nh, nq),
        in_specs=[pl.BlockSpec((bq, width), lambda h, i: (i, h)),
                  pl.BlockSpec((s, width), lambda h, i: (0, h)),
                  pl.BlockSpec((s, width), lambda h, i: (0, h))],
        out_specs=[pl.BlockSpec((bq, width), lambda h, i: (i, h)),
                   pl.BlockSpec((bq, width), lambda h, i: (i, h)),
                   pl.BlockSpec(memory_space=pltpu.SMEM)],
        out_shape=[jax.ShapeDtypeStruct((s, SB_WIDTH), F32), jax.ShapeDtypeStruct((s, SB_WIDTH), F32),
                   jax.ShapeDtypeStruct((SB_HEADS // nh, nq), I32)],
        compiler_params=_params(("arbitrary", "arbitrary"), VMEM_LIMIT),
    )(q, k, v)


def _mla_fwd_call(q, k, v):
    s = q.shape[0]
    bq = _row_tile(s, MLA_BQ)
    bk = _row_tile(s, MLA_BK)
    nq = s // bq
    assert bk % bq == 0
    step = min(bk // 2, MLA_STEP)
    nsub = bk // step
    assert nsub % 2 == 0

    def body(q_ref, k_ref, v_ref, o_ref, lse_ref, p_ref, s_ref):
        i = pl.program_id(1)
        lane = lax.broadcasted_iota(I32, (bq, LANES), 1)
        row = lax.broadcasted_iota(I32, (step, bq), 1)
        col = lax.broadcasted_iota(I32, (step, bq), 0)
        n_full = (i * bq) // bk

        def keys(g):
            return pl.ds(pl.multiple_of(g * step, step), step)

        def join(left, right, qlo):
            return right if qlo == 0 else jnp.concatenate([left[:, :qlo], right], axis=1)

        def put_scores(g, slot, qlo=0):
            for hh in range(2):
                cols = slice(hh * HEAD_PAD, (hh + 1) * HEAD_PAD)
                s_ref[slot, hh, :, qlo:] = _dot_nt(k_ref[keys(g), cols], q_ref[qlo:, cols])

        def add_pv(carry, g, slot, qlo=0):
            vblk = v_ref[keys(g), :]
            out = []
            for hh, (m, l, acc, alpha) in enumerate(carry):
                upd = alpha[:, qlo:] * acc[:, qlo:] + _dot_tn(vblk, p_ref[slot, hh, :, qlo:])
                out.append((m, l, join(acc, upd, qlo), alpha))
            return tuple(out)

        def substep(g, slot, carry, masked, prefetch, qlo=0, next_qlo=0, prev_qlo=0, first=False):
            if prefetch:
                put_scores(g + 1, 1 - slot, next_qlo)
            if not first:
                carry = add_pv(carry, g - 1, 1 - slot, prev_qlo)
            new = []
            for hh in range(2):
                m, l, acc, _ = carry[hh]
                sc = s_ref[slot, hh, :, qlo:]
                if masked:
                    sc = jnp.where(col[:, qlo:] + g * step <= row[:, qlo:] + i * bq, sc, MASK_NEG)
                m_new = jnp.maximum(m[:, qlo:], jnp.max(sc, axis=0, keepdims=True))
                p = jnp.exp2(sc - m_new)
                alpha = jnp.exp2(m[:, qlo:] - m_new)
                l_new = alpha * l[:, qlo:] + jnp.sum(p, axis=0, keepdims=True)
                p_ref[slot, hh, :, qlo:] = p.astype(BF16)
                new.append((join(m, m_new, qlo), join(l, l_new, qlo), acc, join(jnp.ones_like(m), alpha, qlo)))
            return tuple(new)

        def first_query(t, masked):
            return t * step if (masked and bk == bq and 0 <= t < nsub) else 0

        def chunk(kb, carry, masked, first=False):
            for t in range(nsub):
                last = masked and t == nsub - 1
                carry = substep(nsub * kb + t, t % 2, carry, masked, not last, first_query(t, masked),
                                first_query(t + 1, masked), first_query(t - 1, masked), first and t == 0)
            return carry

        def finish(carry):
            (m0, l0, a0, _), (m1, l1, a1, _) = add_pv(carry, nsub * n_full + nsub - 1, 1,
                                                      first_query(nsub - 1, True))
            o_ref[...] = jnp.where(lane < HALF_LANES, (a0 / l0).T, (a1 / l1).T)
            sub = lax.broadcasted_iota(I32, (8, bq), 0)
            lse_ref[...] = jnp.where(sub == 0, m0 + jnp.log2(l0), jnp.where(sub == 1, m1 + jnp.log2(l1), 0.0))

        put_scores(0, 0)
        one = (jnp.full((1, bq), MASK_NEG, F32), jnp.zeros((1, bq), F32), jnp.zeros((LANES, bq), F32),
               jnp.ones((1, bq), F32))

        @pl.when(n_full == 0)
        def _():
            finish(chunk(0, (one, one), True, first=True))

        @pl.when(n_full > 0)
        def _():
            carry = chunk(0, (one, one), False, first=True)
            carry = lax.fori_loop(1, n_full, lambda kb, cr: chunk(kb, cr, False), carry)
            finish(chunk(n_full, carry, True))

    return pl.pallas_call(
        body, name="mla_fwd",
        grid=(4, nq),
        in_specs=[pl.BlockSpec((bq, 2 * HEAD_PAD), lambda h, i: (i, h)),
                  pl.BlockSpec((s, 2 * HEAD_PAD), lambda h, i: (0, h)),
                  pl.BlockSpec((s, LANES), lambda h, i: (0, h))],
        out_specs=[pl.BlockSpec((bq, LANES), lambda h, i: (i, h)),
                   pl.BlockSpec((None, 8, bq), lambda h, i: (h, 0, i))],
        out_shape=[jax.ShapeDtypeStruct((s, MLA_WIDTH), F32), jax.ShapeDtypeStruct((4, 8, s), F32)],
        scratch_shapes=[pltpu.VMEM((2, 2, step, bq), BF16), pltpu.VMEM((2, 2, step, bq), F32)],
        compiler_params=_params(("arbitrary", "arbitrary"), VMEM_LIMIT),
    )(q, k, v)


def _out_call(o_sb, g_sb, o_mla, g_mla, x, target, ada, w_out_bf):
    s = x.shape[0]
    tm = _row_tile(s, 512)

    n_steps = s // tm
    depth = 3

    def body(osb_ref, gsb_ref, oml_ref, gml_ref, x_hbm, t_hbm, gate_ref, w_ref,
             dosb_ref, doml_ref, dgsb_ref, dgml_ref, dy_ref, gw_ref, dgate_ref, sq_ref, x_buf, t_buf, ring_sems):
        i = pl.program_id(0)

        def fetch(step):
            rows = pl.ds(pl.multiple_of(step * tm, tm), tm)
            slot = step % depth
            return [pltpu.make_async_copy(hbm.at[rows], buf.at[slot], ring_sems.at[k, slot])
                    for k, (hbm, buf) in enumerate(((x_hbm, x_buf), (t_hbm, t_buf)))]

        @pl.when(i == 0)
        def _():
            gw_ref[...] = jnp.zeros_like(gw_ref)
            dgate_ref[...] = jnp.zeros_like(dgate_ref)
            sq_ref[...] = jnp.zeros_like(sq_ref)
            for step in range(min(depth - 1, n_steps)):
                for cp in fetch(step):
                    cp.start()

        @pl.when(i + depth - 1 < n_steps)
        def _():
            for cp in fetch(i + depth - 1):
                cp.start()

        for cp in fetch(i):
            cp.wait()
        x_ref, t_ref = x_buf.at[i % depth], t_buf.at[i % depth]

        g_s, g_m = gsb_ref[...], gml_ref[...]
        sig_s, sig_m = _sigmoid(g_s), _sigmoid(g_m)
        silu_s, silu_m = g_s * sig_s, g_m * sig_m
        o_s, o_m = osb_ref[...], oml_ref[...]
        mixed = jnp.concatenate([o_s * silu_s, o_m * silu_m], axis=1).astype(BF16)
        u = _dot(mixed, w_ref[...])
        gate_v = gate_ref[...]
        err = x_ref[...] + gate_v * u - t_ref[...]
        sq_ref[...] += jnp.sum(err * err, axis=0, keepdims=True)
        dy = err * (1.0 / D_MODEL)
        dy_ref[...] = dy
        dgate_ref[...] += jnp.sum(dy * u, axis=0, keepdims=True)
        du = (dy * gate_v).astype(BF16)
        gw_ref[...] += _dot_tn(mixed, du)
        dmix = _dot_nt(du, w_ref[...])
        dm_s, dm_m = dmix[:, :SB_WIDTH], dmix[:, SB_WIDTH:]
        dosb_ref[...] = (dm_s * silu_s).astype(BF16)
        doml_ref[...] = (dm_m * silu_m).astype(BF16)
        dgsb_ref[...] = (dm_s * o_s * (sig_s * (1.0 + g_s * (1.0 - sig_s)))).astype(BF16)
        dgml_ref[...] = (dm_m * o_m * (sig_m * (1.0 + g_m * (1.0 - sig_m)))).astype(BF16)

    return pl.pallas_call(
        body, name="out_proj_loss",
        grid=(s // tm,),
        in_specs=[_rows(tm, 512), _rows(tm, 512), _rows(tm, 512), _rows(tm, 512),
                  _hbm_spec(), _hbm_spec(), _ada_part(2), _full((D_MODEL, D_MODEL))],
        out_specs=[_rows(tm, 512), _rows(tm, 512), _rows(tm, 512), _rows(tm, 512), _rows(tm, D_MODEL),
                   _full((D_MODEL, D_MODEL)), _full((1, D_MODEL)), _full((1, D_MODEL))],
        out_shape=[jax.ShapeDtypeStruct((s, 512), BF16)] * 4
        + [jax.ShapeDtypeStruct((s, D_MODEL), F32), jax.ShapeDtypeStruct((D_MODEL, D_MODEL), F32),
           jax.ShapeDtypeStruct((1, D_MODEL), F32), jax.ShapeDtypeStruct((1, D_MODEL), F32)],
        scratch_shapes=[pltpu.VMEM((depth, tm, D_MODEL), F32), pltpu.VMEM((depth, tm, D_MODEL), F32),
                        pltpu.SemaphoreType.DMA((2, depth))],
        compiler_params=_params(("arbitrary",), VMEM_LIMIT),
    )(o_sb, g_sb, o_mla, g_mla, x, target, ada, w_out_bf)


def _head_mask(lane, hh):
    return jnp.where((lane >= HALF_LANES) if hh else (lane < HALF_LANES), 1.0, 0.0)


def _pick_lane(packed, lane, which):
    return jnp.sum(jnp.where(lane == which, packed, 0.0), axis=1, keepdims=True)


def _sb_bwd_call(kstart, q, k, v, do, rfin):
    s = q.shape[0]
    bq = _row_tile(s, 256)
    nq = s // bq
    nh = SB_GROUP
    width = nh * HALF_LANES

    def body(ks_ref, q_ref, k_ref, v_ref, do_ref, r_ref, dq_ref, dk_ref, dv_ref):
        hp, i = pl.program_id(0), pl.program_id(1)

        @pl.when(i == 0)
        def _():
            dk_ref[...] = jnp.zeros_like(dk_ref)
            dv_ref[...] = jnp.zeros_like(dv_ref)

        lane = lax.broadcasted_iota(I32, (bq, LANES), 1)
        row = lax.broadcasted_iota(I32, (bq, bq), 0)
        col = lax.broadcasted_iota(I32, (bq, bq), 1)
        upto = jnp.where(row <= col, 1.0, 0.0).astype(BF16)
        before = jnp.where(row < col, 1.0, 0.0).astype(BF16)
        masks = [_head_mask(lane, hh).astype(BF16) for hh in range(2)]
        qms = [q_ref[:, _pair(hh)] * jnp.asarray(SB_SCALE, BF16) * masks[hh % 2] for hh in range(nh)]
        doms = [do_ref[:, _pair(hh)] * masks[hh % 2] for hh in range(nh)]
        totals = [_pick_lane(r_ref[:, _pair(hh)], lane, HALF_LANES * (hh % 2)) for hh in range(nh)]
        strict = col < row

        def walk(blocks, state):
            chains = [(kb, diagonal, hh) for kb, diagonal in blocks for hh in range(nh)]
            keys = lambda kb: pl.ds(pl.multiple_of(kb * bq, bq), bq)
            cut = lambda x, diagonal: jnp.where(strict, x, 0.0) if diagonal else x
            zs = [_dot_nt(qms[hh], k_ref[keys(kb), _pair(hh)]) for kb, _, hh in chains]
            dws = [_dot_nt(doms[hh], v_ref[keys(kb), _pair(hh)]) for kb, _, hh in chains]
            pairs = []
            for z, (_, diagonal, _) in zip(zs, chains):
                ls, lk = _log_sigmoid_pair(z)
                pairs.append((ls, cut(lk, diagonal)))
            incls = [_split_dot(lk, upto) for _, lk in pairs]
            pres = [st[0] for st in state]
            ws, gs = [], []
            for (ls, lk), incl, dw, (_, diagonal, hh) in zip(pairs, incls, dws, chains):
                w = cut(jnp.exp(ls + ((totals[hh] - pres[hh]) - incl)), diagonal)
                ws.append(w.astype(BF16))
                gs.append(w * dw)
                pres[hh] = pres[hh] + jnp.sum(lk, axis=1, keepdims=True)
            gsums = [_dot(g.astype(BF16), before) for g in gs]
            gpres = [st[1] for st in state]
            dzs = []
            for (ls, _), g, gsum, (_, diagonal, hh) in zip(pairs, gs, gsums, chains):
                dzs.append(cut(g - jnp.exp(ls) * (g + (gpres[hh] + gsum)), diagonal).astype(BF16))
                gpres[hh] = gpres[hh] + jnp.sum(g, axis=1, keepdims=True)
            dqs = [st[2] for st in state]
            dk_parts, dv_parts = [], []
            for dzb, w, (kb, _, hh) in zip(dzs, ws, chains):
                dk_parts.append(_dot_tn(dzb, qms[hh]))
                dv_parts.append(_dot_tn(w, doms[hh]))
                dqs[hh] = dqs[hh] + _dot(dzb, k_ref[keys(kb), _pair(hh)])
            for b, (kb, _) in enumerate(blocks):
                for pair in range(nh // 2):
                    c0 = b * nh + 2 * pair
                    dk_ref[keys(kb), _pair(2 * pair)] += dk_parts[c0] + dk_parts[c0 + 1]
                    dv_ref[keys(kb), _pair(2 * pair)] += dv_parts[c0] + dv_parts[c0 + 1]
            return tuple(zip(pres, gpres, dqs))

        def finish(state):
            for pair in range(nh // 2):
                both = jnp.where(lane < HALF_LANES, state[2 * pair][2], state[2 * pair + 1][2])
                dq_ref[:, _pair(2 * pair)] = (both * SB_SCALE).astype(BF16)

        zero = ((jnp.zeros((bq, 1), F32), jnp.zeros((bq, 1), F32), jnp.zeros((bq, LANES), F32)),) * nh

        @pl.when(i == 0)
        def _():
            finish(walk([(0, True)], zero))

        @pl.when(i > 0)
        def _():
            state = lax.fori_loop(ks_ref[hp, i], i - 1, lambda kb, st: walk([(kb, False)], st), zero)
            finish(walk([(i - 1, False), (i, True)], state))

    return pl.pallas_call(
        body, name="sb_bwd",
        grid_spec=pltpu.PrefetchScalarGridSpec(
            num_scalar_prefetch=1, grid=(SB_HEADS // nh, nq),
            in_specs=[pl.BlockSpec((bq, width), lambda h, i, ks: (i, h)),
                      pl.BlockSpec((s, width), lambda h, i, ks: (0, h), pipeline_mode=pl.Buffered(1)),
                      pl.BlockSpec((s, width), lambda h, i, ks: (0, h), pipeline_mode=pl.Buffered(1)),
                      pl.BlockSpec((bq, width), lambda h, i, ks: (i, h)),
                      pl.BlockSpec((bq, width), lambda h, i, ks: (i, h))],
            out_specs=[pl.BlockSpec((bq, width), lambda h, i, ks: (i, h)),
                       pl.BlockSpec((s, width), lambda h, i, ks: (0, h), pipeline_mode=pl.Buffered(1)),
                       pl.BlockSpec((s, width), lambda h, i, ks: (0, h), pipeline_mode=pl.Buffered(1))]),
        out_shape=[jax.ShapeDtypeStruct((s, SB_WIDTH), BF16), jax.ShapeDtypeStruct((s, SB_WIDTH), F32),
                   jax.ShapeDtypeStruct((s, SB_WIDTH), F32)],
        compiler_params=_params(("arbitrary", "arbitrary"), VMEM_LIMIT),
    )(kstart, q, k, v, do, rfin)


def _mla_bwd_call(q, k, v, do, o, lse):
    s = q.shape[0]
    bq = _row_tile(s, MLA_BWD_BQ)
    bk = _row_tile(s, MLA_BWD_BK)
    nq = s // bq
    assert bk % bq == 0
    step = min(bk // 2, MLA_BWD_STEP)
    nsub = bk // step
    assert nsub % 2 == 0

    def body(q_ref, k_ref, v_ref, do_ref, o_ref, lse_ref, dq_ref, dk_ref, dv_ref, dom_ref, s_ref, dp_ref, pb_ref,
             ds_ref):
        i = pl.program_id(1)

        @pl.when(i == 0)
        def _():
            dk_ref[...] = jnp.zeros_like(dk_ref)
            dv_ref[...] = jnp.zeros_like(dv_ref)

        lane = lax.broadcasted_iota(I32, (bq, LANES), 1)
        row = lax.broadcasted_iota(I32, (step, bq), 1)
        col = lax.broadcasted_iota(I32, (step, bq), 0)
        n_full = (i * bq) // bk
        do2 = do_ref[...]
        prod = do2.astype(F32) * o_ref[...]
        ones = jnp.ones((8, LANES), BF16)
        deltas, lses = [], []
        for hh in range(2):
            head = _head_mask(lane, hh)
            dom_ref[hh] = do2 * head.astype(BF16)
            part = prod * head
            hi = part.astype(BF16)
            lo = (part - hi.astype(F32)).astype(BF16)
            deltas.append((_dot_nt(ones, hi) + _dot_nt(ones, lo))[0:1])
            lses.append(lse_ref[hh:hh + 1, :])

        def keys(g):
            return pl.ds(pl.multiple_of(g * step, step), step)

        def heads():
            return [(hh, slice(hh * HEAD_PAD, (hh + 1) * HEAD_PAD)) for hh in range(2)]

        def put_products(g, slot, qlo=0):
            vblk = v_ref[keys(g), :]
            for hh, cols in heads():
                s_ref[slot, hh, :, qlo:] = _dot_nt(k_ref[keys(g), cols], q_ref[qlo:, cols])
                dp_ref[slot, hh, :, qlo:] = _dot_nt(vblk, dom_ref[hh, qlo:, :])

        def add_grads(dqs, g, slot, qlo=0):
            rows = keys(g)
            new, dv_parts = [], []
            for hh, cols in heads():
                ds = ds_ref[slot, hh, :, qlo:]
                dk_ref[rows, cols] += _dot(ds, q_ref[qlo:, cols])
                dv_parts.append(_dot(pb_ref[slot, hh, :, qlo:], dom_ref[hh, qlo:, :]))
                upd = dqs[hh][:, qlo:] + _dot_tn(k_ref[rows, cols], ds)
                new.append(upd if qlo == 0 else jnp.concatenate([dqs[hh][:, :qlo], upd], axis=1))
            dv_ref[rows, :] += dv_parts[0] + dv_parts[1]
            return tuple(new)

        def substep(g, slot, dqs, masked, prefetch, qlo=0, next_qlo=0, prev_qlo=0, first=False):
            if prefetch:
                put_products(g + 1, 1 - slot, next_qlo)
            if not first:
                dqs = add_grads(dqs, g - 1, 1 - slot, prev_qlo)
            for hh, _ in heads():
                p = jnp.exp2(s_ref[slot, hh, :, qlo:] - lses[hh][:, qlo:])
                if masked:
                    p = jnp.where(col[:, qlo:] + g * step <= row[:, qlo:] + i * bq, p, 0.0)
                ds_ref[slot, hh, :, qlo:] = (p * (dp_ref[slot, hh, :, qlo:] - deltas[hh][:, qlo:])).astype(BF16)
                pb_ref[slot, hh, :, qlo:] = p.astype(BF16)
            return dqs

        def first_query(t, masked):
            return t * step if (masked and bk == bq and 0 <= t < nsub) else 0

        def chunk(kb, dqs, masked, first=False):
            for t in range(nsub):
                last = masked and t == nsub - 1
                dqs = substep(nsub * kb + t, t % 2, dqs, masked, not last, first_query(t, masked),
                              first_query(t + 1, masked), first_query(t - 1, masked), first and t == 0)
            return dqs

        def finish(dqs):
            dqs = add_grads(dqs, nsub * n_full + nsub - 1, 1, first_query(nsub - 1, True))
            dq_ref[:, :HEAD_PAD] = dqs[0].T * MLA_SCALE
            dq_ref[:, HEAD_PAD:] = dqs[1].T * MLA_SCALE

        put_products(0, 0)
        zero = jnp.zeros((HEAD_PAD, bq), F32)

        @pl.when(n_full == 0)
        def _():
            finish(chunk(0, (zero, zero), True, first=True))

        @pl.when(n_full > 0)
        def _():
            dqs = chunk(0, (zero, zero), False, first=True)
            dqs = lax.fori_loop(1, n_full, lambda kb, dqs: chunk(kb, dqs, False), dqs)
            finish(chunk(n_full, dqs, True))

    return pl.pallas_call(
        body, name="mla_bwd",
        grid=(4, nq),
        in_specs=[pl.BlockSpec((bq, 2 * HEAD_PAD), lambda h, i: (i, h)),
                  pl.BlockSpec((s, 2 * HEAD_PAD), lambda h, i: (0, h)),
                  pl.BlockSpec((s, LANES), lambda h, i: (0, h)),
                  pl.BlockSpec((bq, LANES), lambda h, i: (i, h)),
                  pl.BlockSpec((bq, LANES), lambda h, i: (i, h)),
                  pl.BlockSpec((None, 8, bq), lambda h, i: (h, 0, i))],
        out_specs=[pl.BlockSpec((bq, 2 * HEAD_PAD), lambda h, i: (i, h)),
                   pl.BlockSpec((s, 2 * HEAD_PAD), lambda h, i: (0, h), pipeline_mode=pl.Buffered(1)),
                   pl.BlockSpec((s, LANES), lambda h, i: (0, h), pipeline_mode=pl.Buffered(1))],
        out_shape=[jax.ShapeDtypeStruct((s, MLA_PAD_WIDTH), F32), jax.ShapeDtypeStruct((s, MLA_PAD_WIDTH), F32),
                   jax.ShapeDtypeStruct((s, MLA_WIDTH), F32)],
        scratch_shapes=[pltpu.VMEM((2, bq, LANES), BF16),
                        pltpu.VMEM((2, 2, step, bq), F32), pltpu.VMEM((2, 2, step, bq), F32),
                        pltpu.VMEM((2, 2, step, bq), BF16), pltpu.VMEM((2, 2, step, bq), BF16)],
        compiler_params=_params(("arbitrary", "arbitrary"), VMEM_LIMIT),
    )(q, k, v, do, o, lse)


def _rms_bwd(d_out, inp, r, weight, n):
    normed = inp * r
    gw = d_out * weight
    d_in = r * (gw - normed * (jnp.sum(gw * normed, axis=-1, keepdims=True) * (1.0 / n)))
    return d_in, d_out * normed


def _mla_prep_bwd_call(dq, dk, dv, q0, k0, cqn, ckvn, c_q, c_kv, cos_t, sin_t,
                       q_lora_norm, kv_lora_norm, qhn_pad, khn_pad, w_uq_bf, w_uk_bf, w_uv_bf):
    s = dq.shape[0]
    tm = _row_tile(s, 512)

    def body(dq_ref, dk_ref, dv_ref, q0_ref, k0_ref, cqn_ref, ckvn_ref, cq_ref, ckv_ref,
             cos_ref, sin_ref, qln_ref, kvln_ref, qhn_ref, khn_ref, wuq_ref, wuk_ref, wuv_ref,
             dcq_ref, dckv_ref, dkr_ref, gwuq_ref, gwuk_ref, gwuv_ref, gqln_ref, gkvln_ref, gqhn_ref, gkhn_ref,
             dq0_ref, dk0_ref, tmp_ref):
        @pl.when(pl.program_id(0) == 0)
        def _():
            for ref in (gwuq_ref, gwuk_ref, gwuv_ref, gqln_ref, gkvln_ref, gqhn_ref, gkhn_ref):
                ref[...] = jnp.zeros_like(ref)

        cos_t, sin_t = cos_ref[...], sin_ref[...]
        lane = lax.broadcasted_iota(I32, (tm, LANES), 1)
        rope_lanes = jnp.logical_or(lane < ROPE_HALF,
                                    jnp.logical_and(lane >= HALF_LANES, lane < HALF_LANES + ROPE_HALF))
        heads = [slice(h * HEAD_PAD, (h + 1) * HEAD_PAD) for h in range(MLA_HEADS)]

        def head_norm_bwd(d_ref, x0_ref, w_ref, out_ref, scale):
            w = w_ref[...]
            inv = [lax.rsqrt(jnp.sum(x0_ref[:, cols] * x0_ref[:, cols], axis=-1, keepdims=True)
                             * (1.0 / MLA_QK_DIM) + EPS) for cols in heads]
            for cols in heads:
                tmp_ref[:, cols] = _rope_adjoint(d_ref[:, cols] * scale, cos_t, sin_t)
            dots = [jnp.sum(tmp_ref[:, cols] * w * (x0_ref[:, cols] * r), axis=-1, keepdims=True)
                    for cols, r in zip(heads, inv)]
            g_w = jnp.zeros((1, LANES), F32)
            rope_sum = jnp.zeros((tm, LANES), F32)
            for cols, r, dot in zip(heads, inv, dots):
                normed = x0_ref[:, cols] * r
                d_n = tmp_ref[:, cols]
                d_x0 = r * (d_n * w - normed * (dot * (1.0 / MLA_QK_DIM)))
                out_ref[:, cols] = d_x0.astype(BF16)
                g_w = g_w + jnp.sum(d_n * normed, axis=0, keepdims=True)
                rope_sum = rope_sum + jnp.where(rope_lanes, d_x0, 0.0)
            return g_w, rope_sum

        g_qhn, _ = head_norm_bwd(dq_ref, q0_ref, qhn_ref, dq0_ref, 1.0)
        g_khn, d_kr = head_norm_bwd(dk_ref, k0_ref, khn_ref, dk0_ref, LN2)
        cqn, ckvn = cqn_ref[...], ckvn_ref[...]
        d_q0b, d_k0b, dvb = dq0_ref[...], dk0_ref[...], dv_ref[...].astype(BF16)
        d_cqn = _dot_nt(d_q0b, wuq_ref[...])
        gwuq_ref[...] += _dot_tn(cqn, d_q0b)
        d_ckvn = _dot_nt(d_k0b, wuk_ref[...]) + _dot_nt(dvb, wuv_ref[...])
        gwuk_ref[...] += _dot_tn(ckvn, d_k0b)
        gwuv_ref[...] += _dot_tn(ckvn, dvb)
        gqhn_ref[...] += g_qhn
        gkhn_ref[...] += g_khn
        dkr_ref[...] = d_kr.astype(BF16)
        cq = cq_ref[...]
        rcq = lax.rsqrt(jnp.mean(cq * cq, axis=-1, keepdims=True) + EPS)
        d_cq, gl = _rms_bwd(d_cqn, cq, rcq, qln_ref[...], Q_LORA_RANK)
        dcq_ref[...] = d_cq.astype(BF16)
        gqln_ref[...] += jnp.sum(gl, axis=0, keepdims=True)
        ckv = ckv_ref[...]
        rckv = lax.rsqrt(jnp.mean(ckv * ckv, axis=-1, keepdims=True) + EPS)
        d_ckv, gl = _rms_bwd(d_ckvn, ckv, rckv, kvln_ref[...], KV_LORA_RANK)
        dckv_ref[...] = d_ckv.astype(BF16)
        gkvln_ref[...] += jnp.sum(gl, axis=0, keepdims=True)

    return pl.pallas_call(
        body, name="mla_prep_bwd",
        grid=(s // tm,),
        in_specs=[_rows(tm, MLA_PAD_WIDTH), _rows(tm, MLA_PAD_WIDTH), _rows(tm, MLA_WIDTH),
                  _rows(tm, MLA_PAD_WIDTH), _rows(tm, MLA_PAD_WIDTH),
                  _rows(tm, Q_LORA_RANK), _rows(tm, KV_LORA_RANK), _rows(tm, Q_LORA_RANK), _rows(tm, KV_LORA_RANK),
                  _rows(tm, LANES), _rows(tm, LANES),
                  _full((1, Q_LORA_RANK)), _full((1, KV_LORA_RANK)), _full((1, LANES)), _full((1, LANES)),
                  _full((Q_LORA_RANK, MLA_PAD_WIDTH)), _full((KV_LORA_RANK, MLA_PAD_WIDTH)),
                  _full((KV_LORA_RANK, MLA_WIDTH))],
        out_specs=[_rows(tm, Q_LORA_RANK), _rows(tm, KV_LORA_RANK), _rows(tm, LANES),
                   _full((Q_LORA_RANK, MLA_PAD_WIDTH)), _full((KV_LORA_RANK, MLA_PAD_WIDTH)),
                   _full((KV_LORA_RANK, MLA_WIDTH)),
                   _full((1, Q_LORA_RANK)), _full((1, KV_LORA_RANK)), _full((1, LANES)), _full((1, LANES))],
        out_shape=[jax.ShapeDtypeStruct((s, Q_LORA_RANK), BF16), jax.ShapeDtypeStruct((s, KV_LORA_RANK), BF16),
                   jax.ShapeDtypeStruct((s, LANES), BF16),
                   jax.ShapeDtypeStruct((Q_LORA_RANK, MLA_PAD_WIDTH), F32),
                   jax.ShapeDtypeStruct((KV_LORA_RANK, MLA_PAD_WIDTH), F32),
                   jax.ShapeDtypeStruct((KV_LORA_RANK, MLA_WIDTH), F32),
                   jax.ShapeDtypeStruct((1, Q_LORA_RANK), F32), jax.ShapeDtypeStruct((1, KV_LORA_RANK), F32),
                   jax.ShapeDtypeStruct((1, LANES), F32), jax.ShapeDtypeStruct((1, LANES), F32)],
        scratch_shapes=[pltpu.VMEM((tm, MLA_PAD_WIDTH), BF16), pltpu.VMEM((tm, MLA_PAD_WIDTH), BF16),
                        pltpu.VMEM((tm, MLA_PAD_WIDTH), F32)],
        compiler_params=_params(("arbitrary",), VMEM_LIMIT),
    )(dq, dk, dv, q0, k0, cqn, ckvn, c_q, c_kv, cos_t, sin_t,
      q_lora_norm, kv_lora_norm, qhn_pad, khn_pad, w_uq_bf, w_uk_bf, w_uv_bf)


def _dh_call(pieces, hb, x, dy, ada, norm_w, w_in_bf):
    s = x.shape[0]
    tm = _row_tile(s, 512)
    widths = [p.shape[1] for p in pieces]
    offsets = [sum(widths[:j]) for j in range(len(widths))]
    assert offsets[-1] + widths[-1] == IN_COLS_PAD
    n = len(pieces)

    def body(*refs):
        p_refs = refs[:n]
        (hb_ref, x_ref, dy_ref, sh_ref, sc_ref, nw_ref, w_ref, gx_ref, gw_ref, dsh_ref, dsc_ref, gnw_ref,
         dp_ref, acc_ref) = refs[n:]

        @pl.when(pl.program_id(0) == 0)
        def _():
            acc_ref[...] = jnp.zeros_like(acc_ref)
            dsh_ref[...] = jnp.zeros_like(dsh_ref)
            dsc_ref[...] = jnp.zeros_like(dsc_ref)
            gnw_ref[...] = jnp.zeros_like(gnw_ref)

        for p_ref, c0, width in zip(p_refs, offsets, widths):
            dp_ref[:, c0:c0 + width] = p_ref[...].astype(BF16)
        acc_ref[...] += _dot_tn(hb_ref[...], dp_ref[...])

        @pl.when(pl.program_id(0) == pl.num_programs(0) - 1)
        def _():
            gw_ref[...] = acc_ref[...].astype(BF16)

        dh = _dot_nt(dp_ref[...], w_ref[...])
        xx = x_ref[...]
        r0 = lax.rsqrt(jnp.mean(xx * xx, axis=-1, keepdims=True) + EPS)
        xn = xx * r0
        nw = nw_ref[...]
        dsh_ref[...] += jnp.sum(dh, axis=0, keepdims=True)
        dsc_ref[...] += jnp.sum(dh * (xn * nw), axis=0, keepdims=True)
        dn = dh * (1.0 + sc_ref[...])
        gnw_ref[...] += jnp.sum(dn * xn, axis=0, keepdims=True)
        dxn = dn * nw
        gx_ref[...] = dy_ref[...] + r0 * (dxn - xn * jnp.mean(dxn * xn, axis=-1, keepdims=True))

    return pl.pallas_call(
        body, name="in_proj_bwd",
        grid=(s // tm,),
        in_specs=[_rows(tm, w) for w in widths]
        + [_rows(tm, D_MODEL), _rows(tm, D_MODEL), _rows(tm, D_MODEL),
           _ada_part(0), _ada_part(1), _full((1, D_MODEL)),
           pl.BlockSpec((D_MODEL, IN_COLS_PAD), lambda i: (0, 0), pipeline_mode=pl.Buffered(1))],
        out_specs=[_rows(tm, D_MODEL),
                   pl.BlockSpec((D_MODEL, IN_COLS_PAD), lambda i: (0, 0), pipeline_mode=pl.Buffered(1)),
                   _full((1, D_MODEL)), _full((1, D_MODEL)), _full((1, D_MODEL))],
        out_shape=[jax.ShapeDtypeStruct((s, D_MODEL), F32), jax.ShapeDtypeStruct((D_MODEL, IN_COLS_PAD), BF16),
                   jax.ShapeDtypeStruct((1, D_MODEL), F32), jax.ShapeDtypeStruct((1, D_MODEL), F32),
                   jax.ShapeDtypeStruct((1, D_MODEL), F32)],
        scratch_shapes=[pltpu.VMEM((tm, IN_COLS_PAD), BF16), pltpu.VMEM((D_MODEL, IN_COLS_PAD), F32)],
        compiler_params=_params(("arbitrary",), VMEM_LIMIT),
    )(*pieces, hb, x, dy, ada, ada, norm_w, w_in_bf)


def _adamw(g, w, m, v):
    m = ADAM_B1 * m + (1.0 - ADAM_B1) * g
    v = ADAM_B2 * v + (1.0 - ADAM_B2) * (g * g)
    m_hat = m / (1.0 - ADAM_B1 ** ADAM_STEP)
    v_hat = v / (1.0 - ADAM_B2 ** ADAM_STEP)
    delta = -ADAM_LR * (m_hat / (jnp.sqrt(v_hat) + ADAM_EPS) + ADAM_WD * w)
    return delta, m, v


def _adam_shard_call(name, parts, w, m, v):
    r, c = w.shape
    tr = r if r <= 512 else 256

    def body(p_ref, w_ref, m_ref, v_ref, g_ref, d_ref, nm_ref, nv_ref):
        g = ((p_ref[0].astype(F32) + p_ref[1].astype(F32)) + p_ref[2].astype(F32)) + p_ref[3].astype(F32)
        g_ref[...] = g
        d_ref[...], nm_ref[...], nv_ref[...] = _adamw(g, w_ref[...], m_ref[...], v_ref[...])

    blk = pl.BlockSpec((tr, c), lambda i: (i, 0))
    return pl.pallas_call(
        body, name=name,
        grid=(r // tr,),
        in_specs=[pl.BlockSpec((4, tr, c), lambda i: (0, i, 0)), blk, blk, blk],
        out_specs=[blk] * 4,
        out_shape=[jax.ShapeDtypeStruct((r, c), F32)] * 4,
        compiler_params=_params(("arbitrary",), VMEM_LIMIT),
    )(parts, w, m, v)


def _adam_shard_transposed_call(name, parts, w_t, m_t, v_t):
    c, r = w_t.shape
    assert parts.shape[1:] == (r, c)
    tr = r if r <= 512 else 256
    c_pad = -(-c // LANES) * LANES

    def body(p_ref, w_ref, m_ref, v_ref, g_ref, d_ref, nm_ref, nv_ref, pad_ref):
        @pl.when(pl.program_id(0) == 0)
        def _():
            pad_ref[...] = jnp.zeros_like(pad_ref)

        pad_ref[:, :c] = ((p_ref[0].astype(F32) + p_ref[1].astype(F32)) + p_ref[2].astype(F32)) + p_ref[3].astype(F32)
        g = pad_ref[...].T[:c]
        g_ref[...] = g
        d_ref[...], nm_ref[...], nv_ref[...] = _adamw(g, w_ref[...], m_ref[...], v_ref[...])

    blk = pl.BlockSpec((c, tr), lambda i: (0, i))
    return pl.pallas_call(
        body, name=name,
        grid=(r // tr,),
        in_specs=[pl.BlockSpec((4, tr, c), lambda i: (0, i, 0)), blk, blk, blk],
        out_specs=[blk] * 4,
        out_shape=[jax.ShapeDtypeStruct((c, r), F32)] * 4,
        scratch_shapes=[pltpu.VMEM((tr, c_pad), F32)],
        compiler_params=_params(("arbitrary",), VMEM_LIMIT),
    )(parts, w_t, m_t, v_t)


def _adam_vectors_call(packs, offsets, vectors):
    nv = len(vectors)

    def body(*refs):
        p_ref, ins, outs = refs[0], refs[1:1 + 3 * nv], refs[1 + 3 * nv:]
        for j, off in enumerate(offsets):
            n = ins[3 * j].shape[1]
            span = -(-n // LANES) * LANES
            g = p_ref[0, :, off:off + span]
            for b in range(1, 8):
                g = g + p_ref[b, :, off:off + span]
            g = g[:, :n]
            outs[j][...] = g
            outs[nv + j][...], outs[2 * nv + j][...], outs[3 * nv + j][...] = _adamw(
                g, ins[3 * j][...], ins[3 * j + 1][...], ins[3 * j + 2][...])

    flat = [a for t in vectors for a in t]
    res = pl.pallas_call(
        body, name="adam_vectors",
        out_shape=[jax.ShapeDtypeStruct(t[0].shape, F32) for _ in range(4) for t in vectors],
    )(packs, *flat)
    return [res[k * nv:(k + 1) * nv] for k in range(4)]


ROPE_HALF = MLA_ROPE_DIM // 2
NOPE_A = MLA_NOPE_DIM - ROPE_HALF


def _zeros_like_lanes(t, n):
    return jnp.zeros(t.shape[:-1] + (n,), t.dtype)


def _to_head_lanes(t):
    nope, rope = t[..., :MLA_NOPE_DIM], t[..., MLA_NOPE_DIM:]
    return jnp.concatenate([rope[..., :ROPE_HALF], nope[..., :NOPE_A], rope[..., ROPE_HALF:], nope[..., NOPE_A:],
                            _zeros_like_lanes(t, HEAD_PAD - MLA_QK_DIM)], axis=-1)


def _from_head_lanes(t):
    return jnp.concatenate([t[..., ROPE_HALF:HALF_LANES], t[..., HALF_LANES + ROPE_HALF:MLA_QK_DIM],
                            t[..., :ROPE_HALF], t[..., HALF_LANES:HALF_LANES + ROPE_HALF]], axis=-1)


def _nope_to_head_lanes(t):
    return jnp.concatenate([_zeros_like_lanes(t, ROPE_HALF), t[..., :NOPE_A], _zeros_like_lanes(t, ROPE_HALF),
                            t[..., NOPE_A:], _zeros_like_lanes(t, HEAD_PAD - MLA_QK_DIM)], axis=-1)


def _rope_to_head_lanes(t):
    return jnp.concatenate([t[..., :ROPE_HALF], _zeros_like_lanes(t, HALF_LANES - ROPE_HALF), t[..., ROPE_HALF:],
                            _zeros_like_lanes(t, HALF_LANES - ROPE_HALF)], axis=-1)


def _rope_lane_freq():
    inv_freq = (ROPE_THETA ** (-jnp.arange(0, MLA_ROPE_DIM, 2, dtype=F32) / MLA_ROPE_DIM))[None]
    return _rope_to_head_lanes(jnp.concatenate([-inv_freq, inv_freq], axis=1))


def _unshard_cols(g):
    return jnp.transpose(g, (1, 0, 2)).reshape(g.shape[1], 4 * g.shape[2])


def _shard_cols(g):
    r, c4 = g.shape
    return jnp.transpose(g.reshape(r, 4, c4 // 4), (1, 0, 2))


def kernel(x, c, positions, w_ada, b_ada, norm_w, w_in, q_lora_norm, w_uq, kv_lora_norm, w_ukv, q_head_norm, k_head_norm, w_out, loss_target, m_w_ada, m_b_ada, m_norm_w, m_w_in, m_q_lora_norm, m_w_uq, m_kv_lora_norm, m_w_ukv, m_q_head_norm, m_k_head_norm, m_w_out, v_w_ada, v_b_ada, v_norm_w, v_w_in, v_q_lora_norm, v_w_uq, v_kv_lora_norm, v_w_ukv, v_q_head_norm, v_k_head_norm, v_w_out):
    chip = 2 * lax.axis_index("x") + lax.axis_index("y")
    me8 = 2 * chip + lax.axis_index("c")
    ada_cols = w_ada.shape[2]
    c_all = _allgather_rows_call(c)[:, 0, :]
    ada_part = _ada_call(c_all, w_ada[0], lax.dynamic_slice_in_dim(b_ada, chip * ada_cols, ada_cols, axis=1))
    ada_g, win_g, wuq_g, wukv_g, wout_g, cos_t, sin_t = _gather_call(
        [ada_part[None]] + [w.astype(BF16) for w in (w_in, w_uq, w_ukv, w_out)], [False, True, True, True, True],
        positions[0].astype(F32).reshape(-1, LANES), _rope_lane_freq())
    ada = lax.dynamic_slice_in_dim(ada_g, me8, 1, axis=1).reshape(1, 4 * ada_cols)
    (sq_sum, grad_x, g_w_in, g_w_uq, g_w_ukv, g_w_out, d_ada, g_norm_w, g_qln, g_kvln, g_qhn, g_khn) = _local_step(
        x[0], ada, (cos_t, sin_t), loss_target[0], norm_w, win_g,
        q_lora_norm, _unshard_cols(wuq_g), kv_lora_norm, _unshard_cols(wukv_g), q_head_norm, k_head_norm,
        wout_g.reshape(D_MODEL, D_MODEL))

    grads = [g.astype(BF16) for g in (g_w_in, _shard_cols(g_w_uq), _shard_cols(g_w_ukv),
                                      g_w_out.reshape(4, D_MODEL // 4, D_MODEL))]
    pieces = [d_ada, g_norm_w, g_qln, g_kvln, g_qhn, g_khn, (0.5 * sq_sum / D_MODEL).reshape(1, 1)]
    spans = [-(-p.shape[1] // LANES) * LANES for p in pieces]
    starts = [sum(spans[:j]) for j in range(len(spans))]
    small = jnp.concatenate([jnp.pad(p, ((0, 0), (0, sp - p.shape[1]))) for p, sp in zip(pieces, spans)], axis=1)
    core = lax.axis_index("c").astype(I32).reshape(1)
    chip_halves = _sum_halves_call(core, grads, _swap_halves_call(grads))
    parts, packs, res_ada = _exchange_call(chip_halves, small, starts[0], jnp.pad(c_all, ((0, 8), (0, 0))),
                                           w_ada[0], m_w_ada[0], v_w_ada[0])
    loss = jnp.sum(packs[:, 0, starts[-1]])

    names = ["adam_w_in", "adam_w_uq", "adam_w_ukv", "adam_w_out"]
    shard_w = [(w_in, m_w_in, v_w_in), (w_uq, m_w_uq, v_w_uq), (w_ukv, m_w_ukv, v_w_ukv),
               (w_out, m_w_out, v_w_out)]
    res = {}
    for name, p_g, (w, m, v) in zip(names, parts, shard_w):
        if w.shape[2] % LANES:
            res_t = _adam_shard_transposed_call(name, p_g, w[0].T, m[0].T, v[0].T)
            res[name] = [t.T for t in res_t]
        else:
            res[name] = _adam_shard_call(name, p_g, w[0], m[0], v[0])
    vectors = [(b_ada, m_b_ada, v_b_ada), (norm_w, m_norm_w, v_norm_w), (q_lora_norm, m_q_lora_norm, v_q_lora_norm),
               (kv_lora_norm, m_kv_lora_norm, v_kv_lora_norm), (q_head_norm, m_q_head_norm, v_q_head_norm),
               (k_head_norm, m_k_head_norm, v_k_head_norm)]
    vec_out = _adam_vectors_call(packs, starts[:len(vectors)], vectors)

    def ordered(kind):
        big = lambda name: res[name][kind][None]
        return [res_ada[kind][None], vec_out[kind][0], vec_out[kind][1], big("adam_w_in"), vec_out[kind][2],
                big("adam_w_uq"), vec_out[kind][3], big("adam_w_ukv"), vec_out[kind][4], vec_out[kind][5],
                big("adam_w_out")]

    return (loss, grad_x[None], *ordered(0), *ordered(1), *ordered(2), *ordered(3))


def _local_step(x2, ada, rope_tables, tgt, norm_w, w_in_shards, q_lora_norm, w_uq_full,
                kv_lora_norm, w_ukv_full, q_head_norm, k_head_norm, w_out_full):
    in_shard = w_in_shards.shape[2]
    ckv_tail = C_GM - 3 * in_shard
    assert 0 <= ckv_tail and ckv_tail + MLA_ROPE_DIM + MLA_WIDTH == in_shard
    last = w_in_shards[3]
    w_in_bf = jnp.concatenate(
        [w_in_shards[0], w_in_shards[1], w_in_shards[2], last[:, :ckv_tail], last[:, ckv_tail + MLA_ROPE_DIM:],
         _rope_to_head_lanes(last[:, ckv_tail:ckv_tail + MLA_ROPE_DIM])], axis=1).astype(BF16)
    w_uq_bf = _to_head_lanes(w_uq_full.reshape(Q_LORA_RANK, MLA_HEADS, MLA_QK_DIM)).reshape(
        Q_LORA_RANK, MLA_PAD_WIDTH).astype(BF16)
    w_ukv_heads = w_ukv_full.reshape(KV_LORA_RANK, MLA_HEADS, 2 * MLA_NOPE_DIM)
    w_uk_bf = _nope_to_head_lanes(w_ukv_heads[:, :, :MLA_NOPE_DIM]).reshape(KV_LORA_RANK, MLA_PAD_WIDTH).astype(BF16)
    w_uv_bf = w_ukv_heads[:, :, MLA_NOPE_DIM:].reshape(KV_LORA_RANK, MLA_WIDTH).astype(BF16)
    w_out_bf = w_out_full.astype(BF16)
    qhn_pad, khn_pad = _to_head_lanes(q_head_norm), _to_head_lanes(k_head_norm)
    cos_t, sin_t = rope_tables

    hb, q_sb, k_sb, v_sb, g_sb, c_q, c_kv, g_mla, q_m, k_m, v_m, cqn, ckvn, q0, k0 = _pre_call(
        x2, ada, norm_w, w_in_bf, cos_t, sin_t, q_lora_norm, kv_lora_norm, qhn_pad, khn_pad,
        w_uq_bf, w_uk_bf, w_uv_bf)
    o_sb, r_sb, kstart = _sb_fwd_call(q_sb, k_sb, v_sb)
    o_mla, lse = _mla_fwd_call(q_m, k_m, v_m)
    do_sb, do_mla, dg_sb, dg_mla, dy, g_w_out, d_gate, sq = _out_call(
        o_sb, g_sb, o_mla, g_mla, x2, tgt, ada, w_out_bf)

    dq_sb, dk_sb, dv_sb = _sb_bwd_call(kstart, q_sb, k_sb, v_sb, do_sb, r_sb)
    dq_m, dk_m, dv_m = _mla_bwd_call(q_m, k_m, v_m, do_mla, o_mla, lse)
    (d_cq, d_ckv, d_kr, g_wuq_pad, g_wuk_pad, g_wuv, g_qln, g_kvln, g_qhn, g_khn) = _mla_prep_bwd_call(
        dq_m, dk_m, dv_m, q0, k0, cqn, ckvn, c_q, c_kv, cos_t, sin_t,
        q_lora_norm, kv_lora_norm, qhn_pad, khn_pad, w_uq_bf, w_uk_bf, w_uv_bf)
    grad_x, g_win_pad, d_shift, d_scale, g_norm_w = _dh_call(
        [dq_sb, dk_sb, dv_sb, dg_sb, d_cq, d_ckv, dg_mla, d_kr], hb, x2, dy, ada, norm_w, w_in_bf)

    g_kr = g_win_pad[:, C_KR:]
    g_last = jnp.concatenate([g_win_pad[:, 3 * in_shard:C_GM], g_kr[:, :ROPE_HALF],
                              g_kr[:, HALF_LANES:HALF_LANES + ROPE_HALF], g_win_pad[:, C_GM:C_KR]], axis=1)
    g_w_in = jnp.stack([g_win_pad[:, j * in_shard:(j + 1) * in_shard] for j in range(3)] + [g_last])
    g_w_uq = _from_head_lanes(g_wuq_pad.reshape(Q_LORA_RANK, MLA_HEADS, HEAD_PAD)).reshape(Q_LORA_RANK, -1)
    g_w_ukv = jnp.concatenate(
        [_from_head_lanes(g_wuk_pad.reshape(KV_LORA_RANK, MLA_HEADS, HEAD_PAD))[:, :, :MLA_NOPE_DIM],
         g_wuv.reshape(KV_LORA_RANK, MLA_HEADS, MLA_NOPE_DIM)], axis=2).reshape(KV_LORA_RANK, -1)
    d_ada = jnp.concatenate([d_shift, d_scale, d_gate], axis=1)
    return (jnp.sum(sq), grad_x, g_w_in, g_w_uq, g_w_ukv, g_w_out, d_ada, g_norm_w, g_qln, g_kvln,
            _from_head_lanes(g_qhn), _from_head_lanes(g_khn))
```

```python
import math

import jax
import jax.numpy as jnp
from jax import lax
from jax.experimental import pallas as pl
from jax.experimental.pallas import tpu as pltpu

F32 = jnp.float32
BF16 = jnp.bfloat16
I32 = jnp.int32

D_MODEL = 1024
SB_HEADS = 8
SB_WIDTH = 512
MLA_HEADS = 8
MLA_QK_DIM = 96
MLA_NOPE_DIM = 64
MLA_ROPE_DIM = 32
MLA_WIDTH = 512
Q_LORA_RANK = 384
KV_LORA_RANK = 256
ROPE_THETA = 10000.0
EPS = 1e-6
LANES = 128
HALF_LANES = LANES // 2
HEAD_PAD = 128
MLA_PAD_WIDTH = MLA_HEADS * HEAD_PAD

C_Q, C_K, C_V, C_G = 0, 512, 1024, 1536
C_CQ, C_CKV, C_GM, C_KR = 2048, 2432, 2688, 3200
IN_COLS_PAD = 3328

ADAM_LR = 0.001
ADAM_B1 = 0.9
ADAM_B2 = 0.999
ADAM_EPS = 1e-08
ADAM_WD = 0.01
ADAM_STEP = 10

SB_SCALE = 0.125
SB_GROUP = 4
MLA_SCALE = 1.0 / math.sqrt(MLA_QK_DIM)
LN2 = math.log(2.0)
MLA_SCALE_LOG2 = MLA_SCALE / LN2
MLA_BQ = 1024
MLA_BWD_BQ = 1024
MLA_BK = 1024
MLA_BWD_BK = 1024
MLA_STEP = 512
MLA_BWD_STEP = 256
SB_DEAD = -104.0
MASK_NEG = -1e30

VMEM_LIMIT = 56 * 1024 * 1024
MESH = pl.DeviceIdType.MESH


def _dot(a, b):
    return jnp.dot(a, b, preferred_element_type=F32)


def _dot_nt(a, b):
    return lax.dot_general(a, b, (((1,), (1,)), ((), ())), preferred_element_type=F32)


def _dot_tn(a, b):
    return lax.dot_general(a, b, (((0,), (0,)), ((), ())), preferred_element_type=F32)


def _sigmoid(x):
    return 1.0 / (1.0 + jnp.exp(-x))


def _split_dot(a, m):
    hi = a.astype(BF16)
    lo = (a - hi.astype(F32)).astype(BF16)
    return _dot(hi, m) + _dot(lo, m)


def _params(sem, vmem=None):
    return pltpu.CompilerParams(dimension_semantics=sem, vmem_limit_bytes=vmem)


def _row_tile(s, want):
    return min(want, s)


def _hbm_spec():
    return pl.BlockSpec(memory_space=pltpu.HBM)


def _allgather_rows_call(row):
    def body(in_ref, out_ref, send_sems, recv_sems, loc_sem):
        x, y, c = lax.axis_index("x"), lax.axis_index("y"), lax.axis_index("c")
        flips = [(fx, fy, fc) for fx in (0, 1) for fy in (0, 1) for fc in (0, 1)][1:]

        def peer(r):
            fx, fy, fc = flips[r]
            return ((1 - x) if fx else x, (1 - y) if fy else y, (1 - c) if fc else c)

        def copy(r, slot):
            return pltpu.make_async_remote_copy(
                src_ref=in_ref, dst_ref=out_ref.at[slot], send_sem=send_sems.at[r], recv_sem=recv_sems.at[r],
                device_id=peer(r), device_id_type=MESH)

        local = pltpu.make_async_copy(in_ref, out_ref.at[4 * x + 2 * y + c], loc_sem)
        local.start()
        sends = [copy(r, 4 * x + 2 * y + c) for r in range(7)]
        for cp in sends:
            cp.start()
        for r in range(7):
            px, py, pc = peer(r)
            copy(r, 4 * px + 2 * py + pc).wait_recv()
        for cp in sends:
            cp.wait_send()
        local.wait()

    return pl.pallas_call(
        body, name="gather_rows",
        out_shape=jax.ShapeDtypeStruct((8,) + row.shape, row.dtype),
        in_specs=[_hbm_spec()], out_specs=_hbm_spec(),
        scratch_shapes=[pltpu.SemaphoreType.DMA((7,)), pltpu.SemaphoreType.DMA((7,)), pltpu.SemaphoreType.DMA],
    )(row)


def _gather_call(shards, split, positions, lane_freq):
    n = len(shards)
    halves = [s.shape[1] // 2 for s in shards]
    s_len = positions.shape[0] * LANES
    table_rows = 4 * LANES
    n_chunks = s_len // table_rows
    assert n_chunks * table_rows == s_len and n_chunks >= 2

    def body(*refs):
        ins, pos_ref, frq_ref = refs[:n], refs[n], refs[n + 1]
        outs, cos_out, sin_out = refs[n + 2:2 * n + 2], refs[2 * n + 2], refs[2 * n + 3]
        ici_send, ici_recv, d2d_send, d2d_recv, loc_sems, cos_buf, sin_buf, tab_sems = refs[2 * n + 4:]
        x, y, c = lax.axis_index("x"), lax.axis_index("y"), lax.axis_index("c")
        me = 2 * x + y
        peers = [(1 - x, y), (x, 1 - y), (1 - x, 1 - y)]

        def rows(a, which):
            return pl.ds(pl.multiple_of(which * halves[a], 16), halves[a])

        def ici(a, j, slot):
            px, py = peers[j]
            src, dst = ins[a].at[0], outs[a].at[slot]
            if split[a]:
                src, dst = src.at[rows(a, c)], dst.at[rows(a, c)]
            return pltpu.make_async_remote_copy(
                src_ref=src, dst_ref=dst,
                send_sem=ici_send.at[3 * a + j], recv_sem=ici_recv.at[3 * a + j],
                device_id=(px, py, c), device_id_type=MESH)

        def d2d(a, j, which):
            px, py = peers[j]
            piece = outs[a].at[2 * px + py, rows(a, which)]
            return pltpu.make_async_remote_copy(
                src_ref=piece, dst_ref=piece,
                send_sem=d2d_send.at[3 * a + j], recv_sem=d2d_recv.at[3 * a + j],
                device_id=(x, y, 1 - c), device_id_type=MESH)

        local = [pltpu.make_async_copy(ins[a].at[0], outs[a].at[me], loc_sems.at[a]) for a in range(n)]
        for cp in local:
            cp.start()
        sends = [ici(a, j, me) for a in range(n) for j in range(3)]
        for cp in sends:
            cp.start()

        def table_copies(t, slot):
            rows = pl.ds(pl.multiple_of(t * table_rows, table_rows), table_rows)
            return [pltpu.make_async_copy(buf.at[slot], out.at[rows], tab_sems.at[k, slot])
                    for k, (buf, out) in enumerate(((cos_buf, cos_out), (sin_buf, sin_out)))]

        def tables(t, _):
            slot = t % 2

            @pl.when(t >= 2)
            def _():
                for cp in table_copies(t - 2, slot):
                    cp.wait()

            for b in range(table_rows // LANES):
                lane_pos = pos_ref[pl.ds(t * (table_rows // LANES) + b, 1), :]
                ang = jnp.broadcast_to(lane_pos, (LANES, LANES)).T * frq_ref[...]
                cos_buf[slot, b * LANES:(b + 1) * LANES, :] = jnp.cos(ang)
                sin_buf[slot, b * LANES:(b + 1) * LANES, :] = jnp.sin(ang)
            for cp in table_copies(t, slot):
                cp.start()
            return 0

        lax.fori_loop(0, n_chunks, tables, 0)
        for t in (n_chunks - 2, n_chunks - 1):
            for cp in table_copies(t, t % 2):
                cp.wait()
        for a in range(n):
            for j in range(3):
                px, py = peers[j]
                ici(a, j, 2 * px + py).wait_recv()
                if split[a]:
                    cp = d2d(a, j, c)
                    cp.start()
                    sends.append(cp)
        for a in range(n):
            for j in range(3):
                if split[a]:
                    d2d(a, j, 1 - c).wait_recv()
        for cp in sends:
            cp.wait_send()
        for cp in local:
            cp.wait()

    return pl.pallas_call(
        body, name="gather_weights",
        out_shape=[jax.ShapeDtypeStruct((4,) + s.shape[1:], s.dtype) for s in shards]
        + [jax.ShapeDtypeStruct((s_len, LANES), F32)] * 2,
        in_specs=[_hbm_spec() for _ in shards] + [pl.BlockSpec(memory_space=pltpu.VMEM)] * 2,
        out_specs=[_hbm_spec() for _ in range(n + 2)],
        scratch_shapes=[pltpu.SemaphoreType.DMA((3 * n,)), pltpu.SemaphoreType.DMA((3 * n,)),
                        pltpu.SemaphoreType.DMA((3 * n,)), pltpu.SemaphoreType.DMA((3 * n,)),
                        pltpu.SemaphoreType.DMA((n,)),
                        pltpu.VMEM((2, table_rows, LANES), F32), pltpu.VMEM((2, table_rows, LANES), F32),
                        pltpu.SemaphoreType.DMA((2, 2))],
    )(*shards, positions, lane_freq)


def _swap_halves_call(grads):
    n = len(grads)
    halves = [g.shape[1] // 2 for g in grads]

    def body(*refs):
        g_in, hs, send_sems, recv_sems = refs[:n], refs[n:2 * n], refs[2 * n], refs[2 * n + 1]
        x, y, c = lax.axis_index("x"), lax.axis_index("y"), lax.axis_index("c")
        copies = []
        for a in range(n):
            theirs = pl.ds(pl.multiple_of((1 - c) * halves[a], 16), halves[a])
            copies.append(pltpu.make_async_remote_copy(
                src_ref=g_in[a].at[:, theirs], dst_ref=hs[a], send_sem=send_sems.at[a], recv_sem=recv_sems.at[a],
                device_id=(x, y, 1 - c), device_id_type=MESH))
        for cp in copies:
            cp.start()
        for cp in copies:
            cp.wait()

    return pl.pallas_call(
        body, name="swap_halves",
        out_shape=[jax.ShapeDtypeStruct((4, h, g.shape[2]), g.dtype) for g, h in zip(grads, halves)],
        in_specs=[_hbm_spec() for _ in grads], out_specs=[_hbm_spec() for _ in grads],
        scratch_shapes=[pltpu.SemaphoreType.DMA((n,)), pltpu.SemaphoreType.DMA((n,))],
    )(*grads)


def _sum_halves_call(core, grads, halves):
    n = len(grads)

    def body(core_ref, *refs):
        for g_ref, h_ref, o_ref in zip(refs[:n], refs[n:2 * n], refs[2 * n:]):
            o_ref[...] = (g_ref[...].astype(F32) + h_ref[...].astype(F32)).astype(o_ref.dtype)

    whole = lambda h: pl.BlockSpec(h.shape, lambda i, core_ref: (0, 0, 0))
    return pl.pallas_call(
        body, name="sum_halves",
        grid_spec=pltpu.PrefetchScalarGridSpec(
            num_scalar_prefetch=1, grid=(1,),
            in_specs=[pl.BlockSpec(h.shape, lambda i, core_ref: (0, core_ref[0], 0)) for h in halves]
            + [whole(h) for h in halves],
            out_specs=[whole(h) for h in halves]),
        out_shape=[jax.ShapeDtypeStruct(h.shape, h.dtype) for h in halves],
        compiler_params=_params(("arbitrary",), VMEM_LIMIT),
    )(core, *grads, *halves)


def _exchange_call(chip_halves, small, ada_lane0, c_all, w_ada, m_ada, v_ada):
    n = len(chip_halves)
    halves = [h.shape[1] for h in chip_halves]
    d_model, cols = w_ada.shape
    tr = 256

    def body(*refs):
        g_in, small_in, c_in, wmv_in = refs[:n], refs[n], refs[n + 1], refs[n + 2:n + 5]
        parts, packs, ada_out = refs[n + 5:2 * n + 5], refs[2 * n + 5], refs[2 * n + 6:2 * n + 10]
        (ici_send, ici_recv, d2d_send, d2d_recv, sm_send, sm_recv, loc_sems,
         c_buf, d_buf, wmv_buf, res_buf, ada_sems) = refs[2 * n + 10:]
        x, y, c = lax.axis_index("x"), lax.axis_index("y"), lax.axis_index("c")
        me = 2 * x + y
        me8 = 4 * x + 2 * y + c
        sibling = (x, y, 1 - c)
        peers = [(1 - x, y), (x, 1 - y), (1 - x, 1 - y)]
        flips = [(fx, fy, fc) for fx in (0, 1) for fy in (0, 1) for fc in (0, 1)][1:]

        def rows(a, which):
            return pl.ds(pl.multiple_of(which * halves[a], 16), halves[a])

        def ici(a, j, src_slot, dst_slot):
            px, py = peers[j]
            return pltpu.make_async_remote_copy(
                src_ref=g_in[a].at[src_slot], dst_ref=parts[a].at[dst_slot, rows(a, c)],
                send_sem=ici_send.at[3 * a + j], recv_sem=ici_recv.at[3 * a + j],
                device_id=(px, py, c), device_id_type=MESH)

        def d2d(a, rel, chip, which):
            piece = parts[a].at[chip, rows(a, which)]
            return pltpu.make_async_remote_copy(
                src_ref=piece, dst_ref=piece,
                send_sem=d2d_send.at[4 * a + rel], recv_sem=d2d_recv.at[4 * a + rel],
                device_id=sibling, device_id_type=MESH)

        def flipped(r):
            fx, fy, fc = flips[r]
            return ((1 - x) if fx else x, (1 - y) if fy else y, (1 - c) if fc else c)

        def sm(r, slot):
            return pltpu.make_async_remote_copy(
                src_ref=small_in, dst_ref=packs.at[slot],
                send_sem=sm_send.at[r], recv_sem=sm_recv.at[r],
                device_id=flipped(r), device_id_type=MESH)

        def peer8(r):
            px, py, pc = flipped(r)
            return 4 * px + 2 * py + pc

        def adam_ada():
            lanes = pl.ds(pl.multiple_of(ada_lane0 + me * cols, LANES), cols)
            d_rows = pltpu.make_async_copy(packs.at[:, 0, lanes], d_buf.at[pl.ds(0, 8), :], ada_sems.at[4])
            d_buf[8:, :] = jnp.zeros((8, cols), F32)
            d_rows.start()
            for cp in [d_rows] + ada_loads:
                cp.wait()
            cc = c_buf[...]
            sc = cc * _sigmoid(cc)
            dd = d_buf[...]
            sc_hi = sc.astype(BF16)
            sc_lo = (sc - sc_hi.astype(F32)).astype(BF16)
            dd_hi = dd.astype(BF16)
            dd_lo = (dd - dd_hi.astype(F32)).astype(BF16)
            for t in range(d_model // tr):
                rws = slice(t * tr, (t + 1) * tr)
                g = _dot_tn(sc_hi[:, rws], dd_hi) + (_dot_tn(sc_hi[:, rws], dd_lo) + _dot_tn(sc_lo[:, rws], dd_hi))
                res_buf[0, rws, :] = g
                res_buf[1, rws, :], res_buf[2, rws, :], res_buf[3, rws, :] = _adamw(
                    g, wmv_buf[0, rws, :], wmv_buf[1, rws, :], wmv_buf[2, rws, :])
            stores = [pltpu.make_async_copy(res_buf.at[k], ada_out[k], ada_sems.at[5 + k]) for k in range(4)]
            for cp in stores:
                cp.start()
            return stores

        ada_loads = [pltpu.make_async_copy(c_in, c_buf, ada_sems.at[0])]
        ada_loads += [pltpu.make_async_copy(wmv_in[k], wmv_buf.at[k], ada_sems.at[1 + k]) for k in range(3)]
        for cp in ada_loads:
            cp.start()
        local = [pltpu.make_async_copy(g_in[a].at[me], parts[a].at[me, rows(a, c)], loc_sems.at[a])
                 for a in range(n)]
        local.append(pltpu.make_async_copy(small_in, packs.at[me8], loc_sems.at[n]))
        for cp in local:
            cp.start()
        sends = []
        for r in range(7):
            sends.append(sm(r, me8))
        for a in range(n):
            for j in range(3):
                px, py = peers[j]
                sends.append(ici(a, j, 2 * px + py, me))
        for cp in sends:
            cp.start()
        for a in range(n):
            local[a].wait()
            cp = d2d(a, 0, me, c)
            cp.start()
            sends.append(cp)
        for r in range(7):
            sm(r, peer8(r)).wait_recv()
        local[n].wait()
        ada_stores = adam_ada()
        for a in range(n):
            for j in range(3):
                px, py = peers[j]
                ici(a, j, me, 2 * px + py).wait_recv()
                cp = d2d(a, 1 + j, 2 * px + py, c)
                cp.start()
                sends.append(cp)
        for a in range(n):
            d2d(a, 0, me, 1 - c).wait_recv()
            for j in range(3):
                px, py = peers[j]
                d2d(a, 1 + j, 2 * px + py, 1 - c).wait_recv()
        for cp in sends:
            cp.wait_send()
        for cp in ada_stores:
            cp.wait()

    out_shape = ([jax.ShapeDtypeStruct((4, 2 * h.shape[1], h.shape[2]), h.dtype) for h in chip_halves]
                 + [jax.ShapeDtypeStruct((8,) + small.shape, small.dtype)]
                 + [jax.ShapeDtypeStruct((d_model, cols), F32)] * 4)
    res = pl.pallas_call(
        body, name="exchange_grads",
        out_shape=out_shape,
        in_specs=[_hbm_spec() for _ in range(n + 5)],
        out_specs=[_hbm_spec() for _ in range(n + 5)],
        scratch_shapes=[pltpu.SemaphoreType.DMA((3 * n,)), pltpu.SemaphoreType.DMA((3 * n,)),
                        pltpu.SemaphoreType.DMA((4 * n,)), pltpu.SemaphoreType.DMA((4 * n,)),
                        pltpu.SemaphoreType.DMA((7,)), pltpu.SemaphoreType.DMA((7,)),
                        pltpu.SemaphoreType.DMA((n + 1,)),
                        pltpu.VMEM(c_all.shape, F32), pltpu.VMEM((16, cols), F32),
                        pltpu.VMEM((3, d_model, cols), F32), pltpu.VMEM((4, d_model, cols), F32),
                        pltpu.SemaphoreType.DMA((9,))],
        compiler_params=pltpu.CompilerParams(vmem_limit_bytes=VMEM_LIMIT),
    )(*chip_halves, small, c_all, w_ada, m_ada, v_ada)
    return res[:n], res[n], res[n + 1:]


def _ada_call(c_all, w_ada_cols, b_ada_cols):
    def body(c_ref, w_ref, b_ref, o_ref):
        cc = c_ref[...]
        o_ref[...] = _dot((cc * _sigmoid(cc)).astype(BF16), w_ref[...].astype(BF16)) + b_ref[...]

    return pl.pallas_call(
        body, name="ada_fwd",
        out_shape=jax.ShapeDtypeStruct((c_all.shape[0], w_ada_cols.shape[1]), F32),
        compiler_params=pltpu.CompilerParams(vmem_limit_bytes=VMEM_LIMIT),
    )(c_all, w_ada_cols, b_ada_cols)


def _ada_part(j):
    return pl.BlockSpec((1, D_MODEL), lambda i: (0, j))


def _full(shape):
    return pl.BlockSpec(shape, lambda i: (0,) * len(shape))


def _rows(tm, width):
    return pl.BlockSpec((tm, width), lambda i: (i, 0))


def _rope(t, cos_t, sin_t):
    return t * cos_t + pltpu.roll(t, HALF_LANES, 1) * sin_t


def _rope_adjoint(d, cos_t, sin_t):
    return d * cos_t + pltpu.roll(d * sin_t, HALF_LANES, 1)


def _pre_call(x, ada, norm_w, w_in_bf, cos_t, sin_t, q_lora_norm, kv_lora_norm, qhn_pad, khn_pad,
              w_uq_bf, w_uk_bf, w_uv_bf):
    s = x.shape[0]
    tm = _row_tile(s, 512)
    out_defs = [(D_MODEL, BF16), (512, BF16), (512, BF16), (512, BF16), (512, F32),
                (Q_LORA_RANK, F32), (KV_LORA_RANK, F32), (512, F32),
                (MLA_PAD_WIDTH, BF16), (MLA_PAD_WIDTH, BF16), (MLA_WIDTH, BF16),
                (Q_LORA_RANK, BF16), (KV_LORA_RANK, BF16), (MLA_PAD_WIDTH, F32), (MLA_PAD_WIDTH, F32)]

    def body(x_ref, sh_ref, sc_ref, nw_ref, w_ref, cos_ref, sin_ref, qln_ref, kvln_ref, qhn_ref, khn_ref,
             wuq_ref, wuk_ref, wuv_ref,
             hb_ref, qsb_ref, ksb_ref, vsb_ref, gsb_ref, cq_ref, ckv_ref, gm_ref,
             q_ref, k_ref, v_ref, cqn_ref, ckvn_ref, q0_ref, k0_ref):
        xx = x_ref[...]
        r0 = lax.rsqrt(jnp.mean(xx * xx, axis=-1, keepdims=True) + EPS)
        hb = ((xx * r0 * nw_ref[...]) * (1.0 + sc_ref[...]) + sh_ref[...]).astype(BF16)
        hb_ref[...] = hb

        def proj(c0, width):
            return _dot(hb, w_ref[:, c0:c0 + width])

        cq = proj(C_CQ, Q_LORA_RANK)
        ckv = proj(C_CKV, KV_LORA_RANK)
        kr = proj(C_KR, LANES)
        cq_ref[...] = cq
        ckv_ref[...] = ckv
        cqn = (cq * lax.rsqrt(jnp.mean(cq * cq, axis=-1, keepdims=True) + EPS) * qln_ref[...]).astype(BF16)
        cqn_ref[...] = cqn
        ckvn = (ckv * lax.rsqrt(jnp.mean(ckv * ckv, axis=-1, keepdims=True) + EPS) * kvln_ref[...]).astype(BF16)
        ckvn_ref[...] = ckvn
        qsb_ref[...] = proj(C_Q, 512).astype(BF16)
        v_ref[...] = _dot(ckvn, wuv_ref[...]).astype(BF16)
        q0_ref[...] = _dot(cqn, wuq_ref[...])
        k0_ref[...] = _dot(ckvn, wuk_ref[...])
        ksb_ref[...] = proj(C_K, 512).astype(BF16)
        cos_t, sin_t = cos_ref[...], sin_ref[...]
        heads = [slice(h * HEAD_PAD, (h + 1) * HEAD_PAD) for h in range(MLA_HEADS)]
        for cols in heads:
            k0_ref[:, cols] = k0_ref[:, cols] + kr

        def inv_rms(ref):
            sums = [jnp.sum(ref[:, cols] * ref[:, cols], axis=-1, keepdims=True) for cols in heads]
            return [lax.rsqrt(t * (1.0 / MLA_QK_DIM) + EPS) for t in sums]

        rqs = inv_rms(q0_ref)
        vsb_ref[...] = proj(C_V, 512).astype(BF16)
        rks = inv_rms(k0_ref)
        gsb_ref[...] = proj(C_G, 512)
        for cols, rq, rk in zip(heads, rqs, rks):
            q_ref[:, cols] = (_rope(q0_ref[:, cols] * rq * qhn_ref[...], cos_t, sin_t) * MLA_SCALE_LOG2).astype(BF16)
            k_ref[:, cols] = _rope(k0_ref[:, cols] * rk * khn_ref[...], cos_t, sin_t).astype(BF16)
        gm_ref[...] = proj(C_GM, 512)

    return pl.pallas_call(
        body, name="pre_proj",
        grid=(s // tm,),
        in_specs=[_rows(tm, D_MODEL), _ada_part(0), _ada_part(1), _full((1, D_MODEL)),
                  pl.BlockSpec((D_MODEL, IN_COLS_PAD), lambda i: (0, 0), pipeline_mode=pl.Buffered(1)),
                  _rows(tm, LANES), _rows(tm, LANES),
                  _full((1, Q_LORA_RANK)), _full((1, KV_LORA_RANK)), _full((1, LANES)), _full((1, LANES)),
                  _full((Q_LORA_RANK, MLA_PAD_WIDTH)), _full((KV_LORA_RANK, MLA_PAD_WIDTH)),
                  _full((KV_LORA_RANK, MLA_WIDTH))],
        out_specs=[_rows(tm, w) for w, _ in out_defs],
        out_shape=[jax.ShapeDtypeStruct((s, w), dt) for w, dt in out_defs],
        compiler_params=_params(("arbitrary",), VMEM_LIMIT),
    )(x, ada, ada, norm_w, w_in_bf, cos_t, sin_t, q_lora_norm, kv_lora_norm, qhn_pad, khn_pad,
      w_uq_bf, w_uk_bf, w_uv_bf)


def _log_sigmoid_pair(z):
    ls = jnp.minimum(z, 0.0) - jnp.log(1.0 + jnp.exp(-jnp.abs(z)))
    return ls, ls - z


def _pair(hh):
    return slice((hh // 2) * LANES, (hh // 2 + 1) * LANES)


def _sb_fwd_call(q, k, v):
    s = q.shape[0]
    bq = _row_tile(s, 256)
    nq = s // bq
    nh = SB_GROUP
    width = nh * HALF_LANES

    def body(q_ref, k_ref, v_ref, o_ref, r_ref, ks_ref):
        hp, i = pl.program_id(0), pl.program_id(1)
        lane = lax.broadcasted_iota(I32, (bq, LANES), 1)
        row = lax.broadcasted_iota(I32, (bq, bq), 0)
        col = lax.broadcasted_iota(I32, (bq, bq), 1)
        strict = col < row
        later = jnp.where(row > col, 1.0, 0.0).astype(BF16)
        masks = [_head_mask(lane, hh).astype(BF16) for hh in range(2)]
        qms = [q_ref[:, _pair(hh)] * jnp.asarray(SB_SCALE, BF16) * masks[hh % 2] for hh in range(nh)]

        def walk(blocks, state):
            chains = [(kb, diagonal, hh) for kb, diagonal in blocks for hh in range(nh)]
            keys = lambda kb: pl.ds(pl.multiple_of(kb * bq, bq), bq)
            zs = [_dot_nt(qms[hh], k_ref[keys(kb), _pair(hh)]) for kb, _, hh in chains]
            pairs = []
            for z, (_, diagonal, _) in zip(zs, chains):
                ls, lk = _log_sigmoid_pair(z)
                pairs.append((ls, jnp.where(strict, lk, 0.0) if diagonal else lk))
            sums = [_split_dot(lk, later) for _, lk in pairs]
            runs = [st[0] for st in state]
            ws = []
            for (ls, lk), after, (_, diagonal, hh) in zip(pairs, sums, chains):
                w = jnp.exp(ls + (after + runs[hh]))
                ws.append((jnp.where(strict, w, 0.0) if diagonal else w).astype(BF16))
                runs[hh] = runs[hh] + jnp.sum(lk, axis=1, keepdims=True)
            accs = [st[1] for st in state]
            for w, (kb, _, hh) in zip(ws, chains):
                accs[hh] = accs[hh] + _dot(w, v_ref[keys(kb), _pair(hh)])
            return tuple(zip(runs, accs))

        def alive(state):
            top = jnp.max(state[0][0])
            for st in state[1:]:
                top = jnp.maximum(top, jnp.max(st[0]))
            return (top > SB_DEAD).astype(I32)

        def finish(state, first):
            ks_ref[hp, i] = first
            for pair in range(nh // 2):
                o_ref[:, _pair(2 * pair)] = jnp.where(lane < HALF_LANES, state[2 * pair][1], state[2 * pair + 1][1])
                r_ref[:, _pair(2 * pair)] = jnp.where(lane < HALF_LANES, state[2 * pair][0], state[2 * pair + 1][0])

        zero = ((jnp.zeros((bq, 1), F32), jnp.zeros((bq, LANES), F32)),) * nh

        @pl.when(i == 0)
        def _():
            finish(walk([(0, True)], zero), 0)

        @pl.when(i > 0)
        def _():
            state = walk([(i, True), (i - 1, False)], zero)

            def cond(carry):
                return jnp.logical_and(carry[0] >= 0, carry[1] > 0)

            def step(carry):
                state = walk([(carry[0], False)], carry[2])
                return carry[0] - 1, alive(state), state

            kb, _, state = lax.while_loop(cond, step, (i - 2, alive(state), state))
            finish(state, kb + 1)

    return pl.pallas_call(
        body, name="sb_fwd",
        grid=(SB_HEADS // nh, nq),
        in_specs=[pl.BlockSpec((bq, width), lambda h, i: (i, h)),
                  pl.BlockSpec((s, width), lambda h, i: (0, h)),
                  pl.BlockSpec((s, width), lambda h, i: (0, h))],
        out_specs=[pl.BlockSpec((bq, width), lambda h, i: (i, h)),
                   pl.BlockSpec((bq, width), lambda h, i: (i, h)),
                   pl.BlockSpec(memory_space=pltpu.SMEM)],
        out_shape=[jax.ShapeDtypeStruct((s, SB_WIDTH), F32), jax.ShapeDtypeStruct((s, SB_WIDTH), F32),
                   jax.ShapeDtypeStruct((SB_HEADS // nh, nq), I32)],
        compiler_params=_params(("arbitrary", "arbitrary"), VMEM_LIMIT),
    )(q, k, v)


def _mla_fwd_call(q, k, v):
    s = q.shape[0]
    bq = _row_tile(s, MLA_BQ)
    bk = _row_tile(s, MLA_BK)
    nq = s // bq
    assert bk % bq == 0
    step = min(bk // 2, MLA_STEP)
    nsub = bk // step
    assert nsub % 2 == 0

    def body(q_ref, k_ref, v_ref, o_ref, lse_ref, p_ref, s_ref):
        i = pl.program_id(1)
        lane = lax.broadcasted_iota(I32, (bq, LANES), 1)
        row = lax.broadcasted_iota(I32, (step, bq), 1)
        col = lax.broadcasted_iota(I32, (step, bq), 0)
        n_full = (i * bq) // bk

        def keys(g):
            return pl.ds(pl.multiple_of(g * step, step), step)

        def join(left, right, qlo):
            return right if qlo == 0 else jnp.concatenate([left[:, :qlo], right], axis=1)

        def put_scores(g, slot, qlo=0):
            for hh in range(2):
                cols = slice(hh * HEAD_PAD, (hh + 1) * HEAD_PAD)
                s_ref[slot, hh, :, qlo:] = _dot_nt(k_ref[keys(g), cols], q_ref[qlo:, cols])

        def add_pv(carry, g, slot, qlo=0):
            vblk = v_ref[keys(g), :]
            out = []
            for hh, (m, l, acc, alpha) in enumerate(carry):
                upd = alpha[:, qlo:] * acc[:, qlo:] + _dot_tn(vblk, p_ref[slot, hh, :, qlo:])
                out.append((m, l, join(acc, upd, qlo), alpha))
            return tuple(out)

        def substep(g, slot, carry, masked, prefetch, qlo=0, next_qlo=0, prev_qlo=0, first=False):
            if prefetch:
                put_scores(g + 1, 1 - slot, next_qlo)
            if not first:
                carry = add_pv(carry, g - 1, 1 - slot, prev_qlo)
            new = []
            for hh in range(2):
                m, l, acc, _ = carry[hh]
                sc = s_ref[slot, hh, :, qlo:]
                if masked:
                    sc = jnp.where(col[:, qlo:] + g * step <= row[:, qlo:] + i * bq, sc, MASK_NEG)
                m_new = jnp.maximum(m[:, qlo:], jnp.max(sc, axis=0, keepdims=True))
                p = jnp.exp2(sc - m_new)
                alpha = jnp.exp2(m[:, qlo:] - m_new)
                l_new = alpha * l[:, qlo:] + jnp.sum(p, axis=0, keepdims=True)
                p_ref[slot, hh, :, qlo:] = p.astype(BF16)
                new.append((join(m, m_new, qlo), join(l, l_new, qlo), acc, join(jnp.ones_like(m), alpha, qlo)))
            return tuple(new)

        def first_query(t, masked):
            return t * step if (masked and bk == bq and 0 <= t < nsub) else 0

        def chunk(kb, carry, masked, first=False):
            for t in range(nsub):
                last = masked and t == nsub - 1
                carry = substep(nsub * kb + t, t % 2, carry, masked, not last, first_query(t, masked),
                                first_query(t + 1, masked), first_query(t - 1, masked), first and t == 0)
            return carry

        def finish(carry):
            (m0, l0, a0, _), (m1, l1, a1, _) = add_pv(carry, nsub * n_full + nsub - 1, 1,
                                                      first_query(nsub - 1, True))
            o_ref[...] = jnp.where(lane < HALF_LANES, (a0 / l0).T, (a1 / l1).T)
            sub = lax.broadcasted_iota(I32, (8, bq), 0)
            lse_ref[...] = jnp.where(sub == 0, m0 + jnp.log2(l0), jnp.where(sub == 1, m1 + jnp.log2(l1), 0.0))

        put_scores(0, 0)
        one = (jnp.full((1, bq), MASK_NEG, F32), jnp.zeros((1, bq), F32), jnp.zeros((LANES, bq), F32),
               jnp.ones((1, bq), F32))

        @pl.when(n_full == 0)
        def _():
            finish(chunk(0, (one, one), True, first=True))

        @pl.when(n_full > 0)
        def _():
            carry = chunk(0, (one, one), False, first=True)
            carry = lax.fori_loop(1, n_full, lambda kb, cr: chunk(kb, cr, False), carry)
            finish(chunk(n_full, carry, True))

    return pl.pallas_call(
        body, name="mla_fwd",
        grid=(4, nq),
        in_specs=[pl.BlockSpec((bq, 2 * HEAD_PAD), lambda h, i: (i, h)),
                  pl.BlockSpec((s, 2 * HEAD_PAD), lambda h, i: (0, h)),
                  pl.BlockSpec((s, LANES), lambda h, i: (0, h))],
        out_specs=[pl.BlockSpec((bq, LANES), lambda h, i: (i, h)),
                   pl.BlockSpec((None, 8, bq), lambda h, i: (h, 0, i))],
        out_shape=[jax.ShapeDtypeStruct((s, MLA_WIDTH), F32), jax.ShapeDtypeStruct((4, 8, s), F32)],
        scratch_shapes=[pltpu.VMEM((2, 2, step, bq), BF16), pltpu.VMEM((2, 2, step, bq), F32)],
        compiler_params=_params(("arbitrary", "arbitrary"), VMEM_LIMIT),
    )(q, k, v)


def _out_call(o_sb, g_sb, o_mla, g_mla, x, target, ada, w_out_bf):
    s = x.shape[0]
    tm = _row_tile(s, 512)

    n_steps = s // tm
    depth = 3

    def body(osb_hbm, gsb_hbm, oml_hbm, gml_hbm, x_hbm, t_hbm, gate_ref, w_ref,
             dosb_ref, doml_ref, dgsb_ref, dgml_ref, dy_ref, gw_ref, dgate_ref, sq_ref,
             osb_buf, gsb_buf, oml_buf, gml_buf, x_buf, t_buf, ring_sems):
        i = pl.program_id(0)
        streams = ((osb_hbm, osb_buf), (gsb_hbm, gsb_buf), (oml_hbm, oml_buf), (gml_hbm, gml_buf),
                   (x_hbm, x_buf), (t_hbm, t_buf))

        def fetch(step):
            rows = pl.ds(pl.multiple_of(step * tm, tm), tm)
            slot = step % depth
            return [pltpu.make_async_copy(hbm.at[rows], buf.at[slot], ring_sems.at[k, slot])
                    for k, (hbm, buf) in enumerate(streams)]

        @pl.when(i == 0)
        def _():
            gw_ref[...] = jnp.zeros_like(gw_ref)
            dgate_ref[...] = jnp.zeros_like(dgate_ref)
            sq_ref[...] = jnp.zeros_like(sq_ref)
            for step in range(min(depth - 1, n_steps)):
                for cp in fetch(step):
                    cp.start()

        @pl.when(i + depth - 1 < n_steps)
        def _():
            for cp in fetch(i + depth - 1):
                cp.start()

        for cp in fetch(i):
            cp.wait()
        osb_ref, gsb_ref, oml_ref, gml_ref, x_ref, t_ref = [buf.at[i % depth] for _, buf in streams]

        g_s, g_m = gsb_ref[...], gml_ref[...]
        sig_s, sig_m = _sigmoid(g_s), _sigmoid(g_m)
        silu_s, silu_m = g_s * sig_s, g_m * sig_m
        o_s, o_m = osb_ref[...], oml_ref[...]
        mixed = jnp.concatenate([o_s * silu_s, o_m * silu_m], axis=1).astype(BF16)
        u = _dot(mixed, w_ref[...])
        gate_v = gate_ref[...]
        err = x_ref[...] + gate_v * u - t_ref[...]
        sq_ref[...] += jnp.sum(err * err, axis=0, keepdims=True)
        dy = err * (1.0 / D_MODEL)
        dy_ref[...] = dy
        dgate_ref[...] += jnp.sum(dy * u, axis=0, keepdims=True)
        du = (dy * gate_v).astype(BF16)
        gw_ref[...] += _dot_tn(mixed, du)
        dmix = _dot_nt(du, w_ref[...])
        dm_s, dm_m = dmix[:, :SB_WIDTH], dmix[:, SB_WIDTH:]
        dosb_ref[...] = (dm_s * silu_s).astype(BF16)
        doml_ref[...] = (dm_m * silu_m).astype(BF16)
        dgsb_ref[...] = (dm_s * o_s * (sig_s * (1.0 + g_s * (1.0 - sig_s)))).astype(BF16)
        dgml_ref[...] = (dm_m * o_m * (sig_m * (1.0 + g_m * (1.0 - sig_m)))).astype(BF16)

    return pl.pallas_call(
        body, name="out_proj_loss",
        grid=(s // tm,),
        in_specs=[_hbm_spec() for _ in range(6)] + [_ada_part(2), _full((D_MODEL, D_MODEL))],
        out_specs=[_rows(tm, 512), _rows(tm, 512), _rows(tm, 512), _rows(tm, 512), _rows(tm, D_MODEL),
                   _full((D_MODEL, D_MODEL)), _full((1, D_MODEL)), _full((1, D_MODEL))],
        out_shape=[jax.ShapeDtypeStruct((s, 512), BF16)] * 4
        + [jax.ShapeDtypeStruct((s, D_MODEL), F32), jax.ShapeDtypeStruct((D_MODEL, D_MODEL), F32),
           jax.ShapeDtypeStruct((1, D_MODEL), F32), jax.ShapeDtypeStruct((1, D_MODEL), F32)],
        scratch_shapes=[pltpu.VMEM((depth, tm, a.shape[1]), a.dtype) for a in (o_sb, g_sb, o_mla, g_mla, x, target)]
        + [pltpu.SemaphoreType.DMA((6, depth))],
        compiler_params=_params(("arbitrary",), VMEM_LIMIT),
    )(o_sb, g_sb, o_mla, g_mla, x, target, ada, w_out_bf)


def _head_mask(lane, hh):
    return jnp.where((lane >= HALF_LANES) if hh else (lane < HALF_LANES), 1.0, 0.0)


def _pick_lane(packed, lane, which):
    return jnp.sum(jnp.where(lane == which, packed, 0.0), axis=1, keepdims=True)


def _sb_bwd_call(kstart, q, k, v, do, rfin):
    s = q.shape[0]
    bq = _row_tile(s, 256)
    nq = s // bq
    nh = SB_GROUP
    width = nh * HALF_LANES

    def body(ks_ref, q_ref, k_ref, v_ref, do_ref, r_ref, dq_ref, dk_ref, dv_ref):
        hp, i = pl.program_id(0), pl.program_id(1)

        @pl.when(i == 0)
        def _():
            dk_ref[...] = jnp.zeros_like(dk_ref)
            dv_ref[...] = jnp.zeros_like(dv_ref)

        lane = lax.broadcasted_iota(I32, (bq, LANES), 1)
        row = lax.broadcasted_iota(I32, (bq, bq), 0)
        col = lax.broadcasted_iota(I32, (bq, bq), 1)
        upto = jnp.where(row <= col, 1.0, 0.0).astype(BF16)
        before = jnp.where(row < col, 1.0, 0.0).astype(BF16)
        masks = [_head_mask(lane, hh).astype(BF16) for hh in range(2)]
        qms = [q_ref[:, _pair(hh)] * jnp.asarray(SB_SCALE, BF16) * masks[hh % 2] for hh in range(nh)]
        doms = [do_ref[:, _pair(hh)] * masks[hh % 2] for hh in range(nh)]
        totals = [_pick_lane(r_ref[:, _pair(hh)], lane, HALF_LANES * (hh % 2)) for hh in range(nh)]
        strict = col < row

        def walk(blocks, state):
            chains = [(kb, diagonal, hh) for kb, diagonal in blocks for hh in range(nh)]
            keys = lambda kb: pl.ds(pl.multiple_of(kb * bq, bq), bq)
            cut = lambda x, diagonal: jnp.where(strict, x, 0.0) if diagonal else x
            zs = [_dot_nt(qms[hh], k_ref[keys(kb), _pair(hh)]) for kb, _, hh in chains]
            dws = [_dot_nt(doms[hh], v_ref[keys(kb), _pair(hh)]) for kb, _, hh in chains]
            pairs = []
            for z, (_, diagonal, _) in zip(zs, chains):
                ls, lk = _log_sigmoid_pair(z)
                pairs.append((ls, cut(lk, diagonal)))
            incls = [_split_dot(lk, upto) for _, lk in pairs]
            pres = [st[0] for st in state]
            ws, gs = [], []
            for (ls, lk), incl, dw, (_, diagonal, hh) in zip(pairs, incls, dws, chains):
                w = cut(jnp.exp(ls + ((totals[hh] - pres[hh]) - incl)), diagonal)
                ws.append(w.astype(BF16))
                gs.append(w * dw)
                pres[hh] = pres[hh] + jnp.sum(lk, axis=1, keepdims=True)
            gsums = [_dot(g.astype(BF16), before) for g in gs]
            gpres = [st[1] for st in state]
            dzs = []
            for (ls, _), g, gsum, (_, diagonal, hh) in zip(pairs, gs, gsums, chains):
                dzs.append(cut(g - jnp.exp(ls) * (g + (gpres[hh] + gsum)), diagonal).astype(BF16))
                gpres[hh] = gpres[hh] + jnp.sum(g, axis=1, keepdims=True)
            dqs = [st[2] for st in state]
            dk_parts, dv_parts = [], []
            for dzb, w, (kb, _, hh) in zip(dzs, ws, chains):
                dk_parts.append(_dot_tn(dzb, qms[hh]))
                dv_parts.append(_dot_tn(w, doms[hh]))
                dqs[hh] = dqs[hh] + _dot(dzb, k_ref[keys(kb), _pair(hh)])
            for b, (kb, _) in enumerate(blocks):
                for pair in range(nh // 2):
                    c0 = b * nh + 2 * pair
                    dk_ref[keys(kb), _pair(2 * pair)] += dk_parts[c0] + dk_parts[c0 + 1]
                    dv_ref[keys(kb), _pair(2 * pair)] += dv_parts[c0] + dv_parts[c0 + 1]
            return tuple(zip(pres, gpres, dqs))

        def finish(state):
            for pair in range(nh // 2):
                both = jnp.where(lane < HALF_LANES, state[2 * pair][2], state[2 * pair + 1][2])
                dq_ref[:, _pair(2 * pair)] = (both * SB_SCALE).astype(BF16)

        zero = ((jnp.zeros((bq, 1), F32), jnp.zeros((bq, 1), F32), jnp.zeros((bq, LANES), F32)),) * nh

        @pl.when(i == 0)
        def _():
            finish(walk([(0, True)], zero))

        @pl.when(i > 0)
        def _():
            state = lax.fori_loop(ks_ref[hp, i], i - 1, lambda kb, st: walk([(kb, False)], st), zero)
            finish(walk([(i - 1, False), (i, True)], state))

    return pl.pallas_call(
        body, name="sb_bwd",
        grid_spec=pltpu.PrefetchScalarGridSpec(
            num_scalar_prefetch=1, grid=(SB_HEADS // nh, nq),
            in_specs=[pl.BlockSpec((bq, width), lambda h, i, ks: (i, h)),
                      pl.BlockSpec((s, width), lambda h, i, ks: (0, h), pipeline_mode=pl.Buffered(1)),
                      pl.BlockSpec((s, width), lambda h, i, ks: (0, h), pipeline_mode=pl.Buffered(1)),
                      pl.BlockSpec((bq, width), lambda h, i, ks: (i, h)),
                      pl.BlockSpec((bq, width), lambda h, i, ks: (i, h))],
            out_specs=[pl.BlockSpec((bq, width), lambda h, i, ks: (i, h)),
                       pl.BlockSpec((s, width), lambda h, i, ks: (0, h), pipeline_mode=pl.Buffered(1)),
                       pl.BlockSpec((s, width), lambda h, i, ks: (0, h), pipeline_mode=pl.Buffered(1))]),
        out_shape=[jax.ShapeDtypeStruct((s, SB_WIDTH), BF16), jax.ShapeDtypeStruct((s, SB_WIDTH), F32),
                   jax.ShapeDtypeStruct((s, SB_WIDTH), F32)],
        compiler_params=_params(("arbitrary", "arbitrary"), VMEM_LIMIT),
    )(kstart, q, k, v, do, rfin)


def _mla_bwd_call(q, k, v, do, o, lse):
    s = q.shape[0]
    bq = _row_tile(s, MLA_BWD_BQ)
    bk = _row_tile(s, MLA_BWD_BK)
    nq = s // bq
    assert bk % bq == 0
    step = min(bk // 2, MLA_BWD_STEP)
    nsub = bk // step
    assert nsub % 2 == 0

    def body(q_ref, k_ref, v_ref, do_ref, o_ref, lse_ref, dq_ref, dk_ref, dv_ref, dom_ref, s_ref, dp_ref, pb_ref,
             ds_ref):
        i = pl.program_id(1)

        @pl.when(i == 0)
        def _():
            dk_ref[...] = jnp.zeros_like(dk_ref)
            dv_ref[...] = jnp.zeros_like(dv_ref)

        lane = lax.broadcasted_iota(I32, (bq, LANES), 1)
        row = lax.broadcasted_iota(I32, (step, bq), 1)
        col = lax.broadcasted_iota(I32, (step, bq), 0)
        n_full = (i * bq) // bk
        do2 = do_ref[...]
        prod = do2.astype(F32) * o_ref[...]
        ones = jnp.ones((8, LANES), BF16)
        deltas, lses = [], []
        for hh in range(2):
            head = _head_mask(lane, hh)
            dom_ref[hh] = do2 * head.astype(BF16)
            part = prod * head
            hi = part.astype(BF16)
            lo = (part - hi.astype(F32)).astype(BF16)
            deltas.append((_dot_nt(ones, hi) + _dot_nt(ones, lo))[0:1])
            lses.append(lse_ref[hh:hh + 1, :])

        def keys(g):
            return pl.ds(pl.multiple_of(g * step, step), step)

        def heads():
            return [(hh, slice(hh * HEAD_PAD, (hh + 1) * HEAD_PAD)) for hh in range(2)]

        def put_products(g, slot, qlo=0):
            vblk = v_ref[keys(g), :]
            for hh, cols in heads():
                s_ref[slot, hh, :, qlo:] = _dot_nt(k_ref[keys(g), cols], q_ref[qlo:, cols])
                dp_ref[slot, hh, :, qlo:] = _dot_nt(vblk, dom_ref[hh, qlo:, :])

        def add_grads(dqs, g, slot, qlo=0):
            rows = keys(g)
            new, dv_parts = [], []
            for hh, cols in heads():
                ds = ds_ref[slot, hh, :, qlo:]
                dk_ref[rows, cols] += _dot(ds, q_ref[qlo:, cols])
                dv_parts.append(_dot(pb_ref[slot, hh, :, qlo:], dom_ref[hh, qlo:, :]))
                upd = dqs[hh][:, qlo:] + _dot_tn(k_ref[rows, cols], ds)
                new.append(upd if qlo == 0 else jnp.concatenate([dqs[hh][:, :qlo], upd], axis=1))
            dv_ref[rows, :] += dv_parts[0] + dv_parts[1]
            return tuple(new)

        def substep(g, slot, dqs, masked, prefetch, qlo=0, next_qlo=0, prev_qlo=0, first=False):
            if prefetch:
                put_products(g + 1, 1 - slot, next_qlo)
            if not first:
                dqs = add_grads(dqs, g - 1, 1 - slot, prev_qlo)
            for hh, _ in heads():
                p = jnp.exp2(s_ref[slot, hh, :, qlo:] - lses[hh][:, qlo:])
                if masked:
                    p = jnp.where(col[:, qlo:] + g * step <= row[:, qlo:] + i * bq, p, 0.0)
                ds_ref[slot, hh, :, qlo:] = (p * (dp_ref[slot, hh, :, qlo:] - deltas[hh][:, qlo:])).astype(BF16)
                pb_ref[slot, hh, :, qlo:] = p.astype(BF16)
            return dqs

        def first_query(t, masked):
            return t * step if (masked and bk == bq and 0 <= t < nsub) else 0

        def chunk(kb, dqs, masked, first=False):
            for t in range(nsub):
                last = masked and t == nsub - 1
                dqs = substep(nsub * kb + t, t % 2, dqs, masked, not last, first_query(t, masked),
                              first_query(t + 1, masked), first_query(t - 1, masked), first and t == 0)
            return dqs

        def finish(dqs):
            dqs = add_grads(dqs, nsub * n_full + nsub - 1, 1, first_query(nsub - 1, True))
            dq_ref[:, :HEAD_PAD] = dqs[0].T * MLA_SCALE
            dq_ref[:, HEAD_PAD:] = dqs[1].T * MLA_SCALE

        put_products(0, 0)
        zero = jnp.zeros((HEAD_PAD, bq), F32)

        @pl.when(n_full == 0)
        def _():
            finish(chunk(0, (zero, zero), True, first=True))

        @pl.when(n_full > 0)
        def _():
            dqs = chunk(0, (zero, zero), False, first=True)
            dqs = lax.fori_loop(1, n_full, lambda kb, dqs: chunk(kb, dqs, False), dqs)
            finish(chunk(n_full, dqs, True))

    return pl.pallas_call(
        body, name="mla_bwd",
        grid=(4, nq),
        in_specs=[pl.BlockSpec((bq, 2 * HEAD_PAD), lambda h, i: (i, h)),
                  pl.BlockSpec((s, 2 * HEAD_PAD), lambda h, i: (0, h)),
                  pl.BlockSpec((s, LANES), lambda h, i: (0, h)),
                  pl.BlockSpec((bq, LANES), lambda h, i: (i, h)),
                  pl.BlockSpec((bq, LANES), lambda h, i: (i, h)),
                  pl.BlockSpec((None, 8, bq), lambda h, i: (h, 0, i))],
        out_specs=[pl.BlockSpec((bq, 2 * HEAD_PAD), lambda h, i: (i, h)),
                   pl.BlockSpec((s, 2 * HEAD_PAD), lambda h, i: (0, h), pipeline_mode=pl.Buffered(1)),
                   pl.BlockSpec((s, LANES), lambda h, i: (0, h), pipeline_mode=pl.Buffered(1))],
        out_shape=[jax.ShapeDtypeStruct((s, MLA_PAD_WIDTH), F32), jax.ShapeDtypeStruct((s, MLA_PAD_WIDTH), F32),
                   jax.ShapeDtypeStruct((s, MLA_WIDTH), F32)],
        scratch_shapes=[pltpu.VMEM((2, bq, LANES), BF16),
                        pltpu.VMEM((2, 2, step, bq), F32), pltpu.VMEM((2, 2, step, bq), F32),
                        pltpu.VMEM((2, 2, step, bq), BF16), pltpu.VMEM((2, 2, step, bq), BF16)],
        compiler_params=_params(("arbitrary", "arbitrary"), VMEM_LIMIT),
    )(q, k, v, do, o, lse)


def _rms_bwd(d_out, inp, r, weight, n):
    normed = inp * r
    gw = d_out * weight
    d_in = r * (gw - normed * (jnp.sum(gw * normed, axis=-1, keepdims=True) * (1.0 / n)))
    return d_in, d_out * normed


def _mla_prep_bwd_call(dq, dk, dv, q0, k0, cqn, ckvn, c_q, c_kv, cos_t, sin_t,
                       q_lora_norm, kv_lora_norm, qhn_pad, khn_pad, w_uq_bf, w_uk_bf, w_uv_bf):
    s = dq.shape[0]
    tm = _row_tile(s, 512)

    def body(dq_ref, dk_ref, dv_ref, q0_ref, k0_ref, cqn_ref, ckvn_ref, cq_ref, ckv_ref,
             cos_ref, sin_ref, qln_ref, kvln_ref, qhn_ref, khn_ref, wuq_ref, wuk_ref, wuv_ref,
             dcq_ref, dckv_ref, dkr_ref, gwuq_ref, gwuk_ref, gwuv_ref, gqln_ref, gkvln_ref, gqhn_ref, gkhn_ref,
             dq0_ref, dk0_ref, tmp_ref):
        @pl.when(pl.program_id(0) == 0)
        def _():
            for ref in (gwuq_ref, gwuk_ref, gwuv_ref, gqln_ref, gkvln_ref, gqhn_ref, gkhn_ref):
                ref[...] = jnp.zeros_like(ref)

        cos_t, sin_t = cos_ref[...], sin_ref[...]
        lane = lax.broadcasted_iota(I32, (tm, LANES), 1)
        rope_lanes = jnp.logical_or(lane < ROPE_HALF,
                                    jnp.logical_and(lane >= HALF_LANES, lane < HALF_LANES + ROPE_HALF))
        heads = [slice(h * HEAD_PAD, (h + 1) * HEAD_PAD) for h in range(MLA_HEADS)]

        def head_norm_bwd(d_ref, x0_ref, w_ref, out_ref, scale):
            w = w_ref[...]
            inv = [lax.rsqrt(jnp.sum(x0_ref[:, cols] * x0_ref[:, cols], axis=-1, keepdims=True)
                             * (1.0 / MLA_QK_DIM) + EPS) for cols in heads]
            for cols in heads:
                tmp_ref[:, cols] = _rope_adjoint(d_ref[:, cols] * scale, cos_t, sin_t)
            dots = [jnp.sum(tmp_ref[:, cols] * w * (x0_ref[:, cols] * r), axis=-1, keepdims=True)
                    for cols, r in zip(heads, inv)]
            g_w = jnp.zeros((1, LANES), F32)
            rope_sum = jnp.zeros((tm, LANES), F32)
            for cols, r, dot in zip(heads, inv, dots):
                normed = x0_ref[:, cols] * r
                d_n = tmp_ref[:, cols]
                d_x0 = r * (d_n * w - normed * (dot * (1.0 / MLA_QK_DIM)))
                out_ref[:, cols] = d_x0.astype(BF16)
                g_w = g_w + jnp.sum(d_n * normed, axis=0, keepdims=True)
                rope_sum = rope_sum + jnp.where(rope_lanes, d_x0, 0.0)
            return g_w, rope_sum

        g_qhn, _ = head_norm_bwd(dq_ref, q0_ref, qhn_ref, dq0_ref, 1.0)
        g_khn, d_kr = head_norm_bwd(dk_ref, k0_ref, khn_ref, dk0_ref, LN2)
        cqn, ckvn = cqn_ref[...], ckvn_ref[...]
        d_q0b, d_k0b, dvb = dq0_ref[...], dk0_ref[...], dv_ref[...].astype(BF16)
        d_cqn = _dot_nt(d_q0b, wuq_ref[...])
        gwuq_ref[...] += _dot_tn(cqn, d_q0b)
        d_ckvn = _dot_nt(d_k0b, wuk_ref[...]) + _dot_nt(dvb, wuv_ref[...])
        gwuk_ref[...] += _dot_tn(ckvn, d_k0b)
        gwuv_ref[...] += _dot_tn(ckvn, dvb)
        gqhn_ref[...] += g_qhn
        gkhn_ref[...] += g_khn
        dkr_ref[...] = d_kr.astype(BF16)
        cq = cq_ref[...]
        rcq = lax.rsqrt(jnp.mean(cq * cq, axis=-1, keepdims=True) + EPS)
        d_cq, gl = _rms_bwd(d_cqn, cq, rcq, qln_ref[...], Q_LORA_RANK)
        dcq_ref[...] = d_cq.astype(BF16)
        gqln_ref[...] += jnp.sum(gl, axis=0, keepdims=True)
        ckv = ckv_ref[...]
        rckv = lax.rsqrt(jnp.mean(ckv * ckv, axis=-1, keepdims=True) + EPS)
        d_ckv, gl = _rms_bwd(d_ckvn, ckv, rckv, kvln_ref[...], KV_LORA_RANK)
        dckv_ref[...] = d_ckv.astype(BF16)
        gkvln_ref[...] += jnp.sum(gl, axis=0, keepdims=True)

    return pl.pallas_call(
        body, name="mla_prep_bwd",
        grid=(s // tm,),
        in_specs=[_rows(tm, MLA_PAD_WIDTH), _rows(tm, MLA_PAD_WIDTH), _rows(tm, MLA_WIDTH),
                  _rows(tm, MLA_PAD_WIDTH), _rows(tm, MLA_PAD_WIDTH),
                  _rows(tm, Q_LORA_RANK), _rows(tm, KV_LORA_RANK), _rows(tm, Q_LORA_RANK), _rows(tm, KV_LORA_RANK),
                  _rows(tm, LANES), _rows(tm, LANES),
                  _full((1, Q_LORA_RANK)), _full((1, KV_LORA_RANK)), _full((1, LANES)), _full((1, LANES)),
                  _full((Q_LORA_RANK, MLA_PAD_WIDTH)), _full((KV_LORA_RANK, MLA_PAD_WIDTH)),
                  _full((KV_LORA_RANK, MLA_WIDTH))],
        out_specs=[_rows(tm, Q_LORA_RANK), _rows(tm, KV_LORA_RANK), _rows(tm, LANES),
                   _full((Q_LORA_RANK, MLA_PAD_WIDTH)), _full((KV_LORA_RANK, MLA_PAD_WIDTH)),
                   _full((KV_LORA_RANK, MLA_WIDTH)),
                   _full((1, Q_LORA_RANK)), _full((1, KV_LORA_RANK)), _full((1, LANES)), _full((1, LANES))],
        out_shape=[jax.ShapeDtypeStruct((s, Q_LORA_RANK), BF16), jax.ShapeDtypeStruct((s, KV_LORA_RANK), BF16),
                   jax.ShapeDtypeStruct((s, LANES), BF16),
                   jax.ShapeDtypeStruct((Q_LORA_RANK, MLA_PAD_WIDTH), F32),
                   jax.ShapeDtypeStruct((KV_LORA_RANK, MLA_PAD_WIDTH), F32),
                   jax.ShapeDtypeStruct((KV_LORA_RANK, MLA_WIDTH), F32),
                   jax.ShapeDtypeStruct((1, Q_LORA_RANK), F32), jax.ShapeDtypeStruct((1, KV_LORA_RANK), F32),
                   jax.ShapeDtypeStruct((1, LANES), F32), jax.ShapeDtypeStruct((1, LANES), F32)],
        scratch_shapes=[pltpu.VMEM((tm, MLA_PAD_WIDTH), BF16), pltpu.VMEM((tm, MLA_PAD_WIDTH), BF16),
                        pltpu.VMEM((tm, MLA_PAD_WIDTH), F32)],
        compiler_params=_params(("arbitrary",), VMEM_LIMIT),
    )(dq, dk, dv, q0, k0, cqn, ckvn, c_q, c_kv, cos_t, sin_t,
      q_lora_norm, kv_lora_norm, qhn_pad, khn_pad, w_uq_bf, w_uk_bf, w_uv_bf)


def _dh_call(pieces, hb, x, dy, ada, norm_w, w_in_bf):
    s = x.shape[0]
    tm = _row_tile(s, 512)
    widths = [p.shape[1] for p in pieces]
    offsets = [sum(widths[:j]) for j in range(len(widths))]
    assert offsets[-1] + widths[-1] == IN_COLS_PAD
    n = len(pieces)

    def body(*refs):
        p_refs = refs[:n]
        (hb_ref, x_ref, dy_ref, sh_ref, sc_ref, nw_ref, w_ref, gx_ref, gw_ref, dsh_ref, dsc_ref, gnw_ref,
         dp_ref, acc_ref) = refs[n:]

        @pl.when(pl.program_id(0) == 0)
        def _():
            acc_ref[...] = jnp.zeros_like(acc_ref)
            dsh_ref[...] = jnp.zeros_like(dsh_ref)
            dsc_ref[...] = jnp.zeros_like(dsc_ref)
            gnw_ref[...] = jnp.zeros_like(gnw_ref)

        for p_ref, c0, width in zip(p_refs, offsets, widths):
            dp_ref[:, c0:c0 + width] = p_ref[...].astype(BF16)
        acc_ref[...] += _dot_tn(hb_ref[...], dp_ref[...])

        @pl.when(pl.program_id(0) == pl.num_programs(0) - 1)
        def _():
            gw_ref[...] = acc_ref[...].astype(BF16)

        dh = _dot_nt(dp_ref[...], w_ref[...])
        xx = x_ref[...]
        r0 = lax.rsqrt(jnp.mean(xx * xx, axis=-1, keepdims=True) + EPS)
        xn = xx * r0
        nw = nw_ref[...]
        dsh_ref[...] += jnp.sum(dh, axis=0, keepdims=True)
        dsc_ref[...] += jnp.sum(dh * (xn * nw), axis=0, keepdims=True)
        dn = dh * (1.0 + sc_ref[...])
        gnw_ref[...] += jnp.sum(dn * xn, axis=0, keepdims=True)
        dxn = dn * nw
        gx_ref[...] = dy_ref[...] + r0 * (dxn - xn * jnp.mean(dxn * xn, axis=-1, keepdims=True))

    return pl.pallas_call(
        body, name="in_proj_bwd",
        grid=(s // tm,),
        in_specs=[_rows(tm, w) for w in widths]
        + [_rows(tm, D_MODEL), _rows(tm, D_MODEL), _rows(tm, D_MODEL),
           _ada_part(0), _ada_part(1), _full((1, D_MODEL)),
           pl.BlockSpec((D_MODEL, IN_COLS_PAD), lambda i: (0, 0), pipeline_mode=pl.Buffered(1))],
        out_specs=[_rows(tm, D_MODEL),
                   pl.BlockSpec((D_MODEL, IN_COLS_PAD), lambda i: (0, 0), pipeline_mode=pl.Buffered(1)),
                   _full((1, D_MODEL)), _full((1, D_MODEL)), _full((1, D_MODEL))],
        out_shape=[jax.ShapeDtypeStruct((s, D_MODEL), F32), jax.ShapeDtypeStruct((D_MODEL, IN_COLS_PAD), BF16),
                   jax.ShapeDtypeStruct((1, D_MODEL), F32), jax.ShapeDtypeStruct((1, D_MODEL), F32),
                   jax.ShapeDtypeStruct((1, D_MODEL), F32)],
        scratch_shapes=[pltpu.VMEM((tm, IN_COLS_PAD), BF16), pltpu.VMEM((D_MODEL, IN_COLS_PAD), F32)],
        compiler_params=_params(("arbitrary",), VMEM_LIMIT),
    )(*pieces, hb, x, dy, ada, ada, norm_w, w_in_bf)


def _adamw(g, w, m, v):
    m = ADAM_B1 * m + (1.0 - ADAM_B1) * g
    v = ADAM_B2 * v + (1.0 - ADAM_B2) * (g * g)
    m_hat = m / (1.0 - ADAM_B1 ** ADAM_STEP)
    v_hat = v / (1.0 - ADAM_B2 ** ADAM_STEP)
    delta = -ADAM_LR * (m_hat / (jnp.sqrt(v_hat) + ADAM_EPS) + ADAM_WD * w)
    return delta, m, v


def _adam_shard_call(name, parts, w, m, v):
    r, c = w.shape
    tr = r if r <= 512 else 256

    def body(p_ref, w_ref, m_ref, v_ref, g_ref, d_ref, nm_ref, nv_ref):
        g = ((p_ref[0].astype(F32) + p_ref[1].astype(F32)) + p_ref[2].astype(F32)) + p_ref[3].astype(F32)
        g_ref[...] = g
        d_ref[...], nm_ref[...], nv_ref[...] = _adamw(g, w_ref[...], m_ref[...], v_ref[...])

    blk = pl.BlockSpec((tr, c), lambda i: (i, 0))
    return pl.pallas_call(
        body, name=name,
        grid=(r // tr,),
        in_specs=[pl.BlockSpec((4, tr, c), lambda i: (0, i, 0)), blk, blk, blk],
        out_specs=[blk] * 4,
        out_shape=[jax.ShapeDtypeStruct((r, c), F32)] * 4,
        compiler_params=_params(("arbitrary",), VMEM_LIMIT),
    )(parts, w, m, v)


def _adam_shard_transposed_call(name, parts, w_t, m_t, v_t):
    c, r = w_t.shape
    assert parts.shape[1:] == (r, c)
    tr = r if r <= 512 else 256
    c_pad = -(-c // LANES) * LANES

    def body(p_ref, w_ref, m_ref, v_ref, g_ref, d_ref, nm_ref, nv_ref, pad_ref):
        @pl.when(pl.program_id(0) == 0)
        def _():
            pad_ref[...] = jnp.zeros_like(pad_ref)

        pad_ref[:, :c] = ((p_ref[0].astype(F32) + p_ref[1].astype(F32)) + p_ref[2].astype(F32)) + p_ref[3].astype(F32)
        g = pad_ref[...].T[:c]
        g_ref[...] = g
        d_ref[...], nm_ref[...], nv_ref[...] = _adamw(g, w_ref[...], m_ref[...], v_ref[...])

    blk = pl.BlockSpec((c, tr), lambda i: (0, i))
    return pl.pallas_call(
        body, name=name,
        grid=(r // tr,),
        in_specs=[pl.BlockSpec((4, tr, c), lambda i: (0, i, 0)), blk, blk, blk],
        out_specs=[blk] * 4,
        out_shape=[jax.ShapeDtypeStruct((c, r), F32)] * 4,
        scratch_shapes=[pltpu.VMEM((tr, c_pad), F32)],
        compiler_params=_params(("arbitrary",), VMEM_LIMIT),
    )(parts, w_t, m_t, v_t)


def _adam_vectors_call(packs, offsets, vectors):
    nv = len(vectors)

    def body(*refs):
        p_ref, ins, outs = refs[0], refs[1:1 + 3 * nv], refs[1 + 3 * nv:]
        for j, off in enumerate(offsets):
            n = ins[3 * j].shape[1]
            span = -(-n // LANES) * LANES
            g = p_ref[0, :, off:off + span]
            for b in range(1, 8):
                g = g + p_ref[b, :, off:off + span]
            g = g[:, :n]
            outs[j][...] = g
            outs[nv + j][...], outs[2 * nv + j][...], outs[3 * nv + j][...] = _adamw(
                g, ins[3 * j][...], ins[3 * j + 1][...], ins[3 * j + 2][...])

    flat = [a for t in vectors for a in t]
    res = pl.pallas_call(
        body, name="adam_vectors",
        out_shape=[jax.ShapeDtypeStruct(t[0].shape, F32) for _ in range(4) for t in vectors],
    )(packs, *flat)
    return [res[k * nv:(k + 1) * nv] for k in range(4)]


ROPE_HALF = MLA_ROPE_DIM // 2
NOPE_A = MLA_NOPE_DIM - ROPE_HALF


def _zeros_like_lanes(t, n):
    return jnp.zeros(t.shape[:-1] + (n,), t.dtype)


def _to_head_lanes(t):
    nope, rope = t[..., :MLA_NOPE_DIM], t[..., MLA_NOPE_DIM:]
    return jnp.concatenate([rope[..., :ROPE_HALF], nope[..., :NOPE_A], rope[..., ROPE_HALF:], nope[..., NOPE_A:],
                            _zeros_like_lanes(t, HEAD_PAD - MLA_QK_DIM)], axis=-1)


def _from_head_lanes(t):
    return jnp.concatenate([t[..., ROPE_HALF:HALF_LANES], t[..., HALF_LANES + ROPE_HALF:MLA_QK_DIM],
                            t[..., :ROPE_HALF], t[..., HALF_LANES:HALF_LANES + ROPE_HALF]], axis=-1)


def _nope_to_head_lanes(t):
    return jnp.concatenate([_zeros_like_lanes(t, ROPE_HALF), t[..., :NOPE_A], _zeros_like_lanes(t, ROPE_HALF),
                            t[..., NOPE_A:], _zeros_like_lanes(t, HEAD_PAD - MLA_QK_DIM)], axis=-1)


def _rope_to_head_lanes(t):
    return jnp.concatenate([t[..., :ROPE_HALF], _zeros_like_lanes(t, HALF_LANES - ROPE_HALF), t[..., ROPE_HALF:],
                            _zeros_like_lanes(t, HALF_LANES - ROPE_HALF)], axis=-1)


def _rope_lane_freq():
    inv_freq = (ROPE_THETA ** (-jnp.arange(0, MLA_ROPE_DIM, 2, dtype=F32) / MLA_ROPE_DIM))[None]
    return _rope_to_head_lanes(jnp.concatenate([-inv_freq, inv_freq], axis=1))


def _unshard_cols(g):
    return jnp.transpose(g, (1, 0, 2)).reshape(g.shape[1], 4 * g.shape[2])


def _shard_cols(g):
    r, c4 = g.shape
    return jnp.transpose(g.reshape(r, 4, c4 // 4), (1, 0, 2))


def kernel(x, c, positions, w_ada, b_ada, norm_w, w_in, q_lora_norm, w_uq, kv_lora_norm, w_ukv, q_head_norm, k_head_norm, w_out, loss_target, m_w_ada, m_b_ada, m_norm_w, m_w_in, m_q_lora_norm, m_w_uq, m_kv_lora_norm, m_w_ukv, m_q_head_norm, m_k_head_norm, m_w_out, v_w_ada, v_b_ada, v_norm_w, v_w_in, v_q_lora_norm, v_w_uq, v_kv_lora_norm, v_w_ukv, v_q_head_norm, v_k_head_norm, v_w_out):
    chip = 2 * lax.axis_index("x") + lax.axis_index("y")
    me8 = 2 * chip + lax.axis_index("c")
    ada_cols = w_ada.shape[2]
    c_all = _allgather_rows_call(c)[:, 0, :]
    ada_part = _ada_call(c_all, w_ada[0], lax.dynamic_slice_in_dim(b_ada, chip * ada_cols, ada_cols, axis=1))
    ada_g, win_g, wuq_g, wukv_g, wout_g, cos_t, sin_t = _gather_call(
        [ada_part[None]] + [w.astype(BF16) for w in (w_in, w_uq, w_ukv, w_out)], [False, True, True, True, True],
        positions[0].astype(F32).reshape(-1, LANES), _rope_lane_freq())
    ada = lax.dynamic_slice_in_dim(ada_g, me8, 1, axis=1).reshape(1, 4 * ada_cols)
    (sq_sum, grad_x, g_w_in, g_w_uq, g_w_ukv, g_w_out, d_ada, g_norm_w, g_qln, g_kvln, g_qhn, g_khn) = _local_step(
        x[0], ada, (cos_t, sin_t), loss_target[0], norm_w, win_g,
        q_lora_norm, _unshard_cols(wuq_g), kv_lora_norm, _unshard_cols(wukv_g), q_head_norm, k_head_norm,
        wout_g.reshape(D_MODEL, D_MODEL))

    grads = [g.astype(BF16) for g in (g_w_in, _shard_cols(g_w_uq), _shard_cols(g_w_ukv),
                                      g_w_out.reshape(4, D_MODEL // 4, D_MODEL))]
    pieces = [d_ada, g_norm_w, g_qln, g_kvln, g_qhn, g_khn, (0.5 * sq_sum / D_MODEL).reshape(1, 1)]
    spans = [-(-p.shape[1] // LANES) * LANES for p in pieces]
    starts = [sum(spans[:j]) for j in range(len(spans))]
    small = jnp.concatenate([jnp.pad(p, ((0, 0), (0, sp - p.shape[1]))) for p, sp in zip(pieces, spans)], axis=1)
    core = lax.axis_index("c").astype(I32).reshape(1)
    chip_halves = _sum_halves_call(core, grads, _swap_halves_call(grads))
    parts, packs, res_ada = _exchange_call(chip_halves, small, starts[0], jnp.pad(c_all, ((0, 8), (0, 0))),
                                           w_ada[0], m_w_ada[0], v_w_ada[0])
    loss = jnp.sum(packs[:, 0, starts[-1]])

    names = ["adam_w_in", "adam_w_uq", "adam_w_ukv", "adam_w_out"]
    shard_w = [(w_in, m_w_in, v_w_in), (w_uq, m_w_uq, v_w_uq), (w_ukv, m_w_ukv, v_w_ukv),
               (w_out, m_w_out, v_w_out)]
    res = {}
    for name, p_g, (w, m, v) in zip(names, parts, shard_w):
        if w.shape[2] % LANES:
            res_t = _adam_shard_transposed_call(name, p_g, w[0].T, m[0].T, v[0].T)
            res[name] = [t.T for t in res_t]
        else:
            res[name] = _adam_shard_call(name, p_g, w[0], m[0], v[0])
    vectors = [(b_ada, m_b_ada, v_b_ada), (norm_w, m_norm_w, v_norm_w), (q_lora_norm, m_q_lora_norm, v_q_lora_norm),
               (kv_lora_norm, m_kv_lora_norm, v_kv_lora_norm), (q_head_norm, m_q_head_norm, v_q_head_norm),
               (k_head_norm, m_k_head_norm, v_k_head_norm)]
    vec_out = _adam_vectors_call(packs, starts[:len(vectors)], vectors)

    def ordered(kind):
        big = lambda name: res[name][kind][None]
        return [res_ada[kind][None], vec_out[kind][0], vec_out[kind][1], big("adam_w_in"), vec_out[kind][2],
                big("adam_w_uq"), vec_out[kind][3], big("adam_w_ukv"), vec_out[kind][4], vec_out[kind][5],
                big("adam_w_out")]

    return (loss, grad_x[None], *ordered(0), *ordered(1), *ordered(2), *ordered(3))


def _local_step(x2, ada, rope_tables, tgt, norm_w, w_in_shards, q_lora_norm, w_uq_full,
                kv_lora_norm, w_ukv_full, q_head_norm, k_head_norm, w_out_full):
    in_shard = w_in_shards.shape[2]
    ckv_tail = C_GM - 3 * in_shard
    assert 0 <= ckv_tail and ckv_tail + MLA_ROPE_DIM + MLA_WIDTH == in_shard
    last = w_in_shards[3]
    w_in_bf = jnp.concatenate(
        [w_in_shards[0], w_in_shards[1], w_in_shards[2], last[:, :ckv_tail], last[:, ckv_tail + MLA_ROPE_DIM:],
         _rope_to_head_lanes(last[:, ckv_tail:ckv_tail + MLA_ROPE_DIM])], axis=1).astype(BF16)
    w_uq_bf = _to_head_lanes(w_uq_full.reshape(Q_LORA_RANK, MLA_HEADS, MLA_QK_DIM)).reshape(
        Q_LORA_RANK, MLA_PAD_WIDTH).astype(BF16)
    w_ukv_heads = w_ukv_full.reshape(KV_LORA_RANK, MLA_HEADS, 2 * MLA_NOPE_DIM)
    w_uk_bf = _nope_to_head_lanes(w_ukv_heads[:, :, :MLA_NOPE_DIM]).reshape(KV_LORA_RANK, MLA_PAD_WIDTH).astype(BF16)
    w_uv_bf = w_ukv_heads[:, :, MLA_NOPE_DIM:].reshape(KV_LORA_RANK, MLA_WIDTH).astype(BF16)
    w_out_bf = w_out_full.astype(BF16)
    qhn_pad, khn_pad = _to_head_lanes(q_head_norm), _to_head_lanes(k_head_norm)
    cos_t, sin_t = rope_tables

    hb, q_sb, k_sb, v_sb, g_sb, c_q, c_kv, g_mla, q_m, k_m, v_m, cqn, ckvn, q0, k0 = _pre_call(
        x2, ada, norm_w, w_in_bf, cos_t, sin_t, q_lora_norm, kv_lora_norm, qhn_pad, khn_pad,
        w_uq_bf, w_uk_bf, w_uv_bf)
    o_sb, r_sb, kstart = _sb_fwd_call(q_sb, k_sb, v_sb)
    o_mla, lse = _mla_fwd_call(q_m, k_m, v_m)
    do_sb, do_mla, dg_sb, dg_mla, dy, g_w_out, d_gate, sq = _out_call(
        o_sb, g_sb, o_mla, g_mla, x2, tgt, ada, w_out_bf)

    dq_sb, dk_sb, dv_sb = _sb_bwd_call(kstart, q_sb, k_sb, v_sb, do_sb, r_sb)
    dq_m, dk_m, dv_m = _mla_bwd_call(q_m, k_m, v_m, do_mla, o_mla, lse)
    (d_cq, d_ckv, d_kr, g_wuq_pad, g_wuk_pad, g_wuv, g_qln, g_kvln, g_qhn, g_khn) = _mla_prep_bwd_call(
        dq_m, dk_m, dv_m, q0, k0, cqn, ckvn, c_q, c_kv, cos_t, sin_t,
        q_lora_norm, kv_lora_norm, qhn_pad, khn_pad, w_uq_bf, w_uk_bf, w_uv_bf)
    grad_x, g_win_pad, d_shift, d_scale, g_norm_w = _dh_call(
        [dq_sb, dk_sb, dv_sb, dg_sb, d_cq, d_ckv, dg_mla, d_kr], hb, x2, dy, ada, norm_w, w_in_bf)

    g_kr = g_win_pad[:, C_KR:]
    g_last = jnp.concatenate([g_win_pad[:, 3 * in_shard:C_GM], g_kr[:, :ROPE_HALF],
                              g_kr[:, HALF_LANES:HALF_LANES + ROPE_HALF], g_win_pad[:, C_GM:C_KR]], axis=1)
    g_w_in = jnp.stack([g_win_pad[:, j * in_shard:(j + 1) * in_shard] for j in range(3)] + [g_last])
    g_w_uq = _from_head_lanes(g_wuq_pad.reshape(Q_LORA_RANK, MLA_HEADS, HEAD_PAD)).reshape(Q_LORA_RANK, -1)
    g_w_ukv = jnp.concatenate(
        [_from_head_lanes(g_wuk_pad.reshape(KV_LORA_RANK, MLA_HEADS, HEAD_PAD))[:, :, :MLA_NOPE_DIM],
         g_wuv.reshape(KV_LORA_RANK, MLA_HEADS, MLA_NOPE_DIM)], axis=2).reshape(KV_LORA_RANK, -1)
    d_ada = jnp.concatenate([d_shift, d_scale, d_gate], axis=1)
    return (jnp.sum(sq), grad_x, g_w_in, g_w_uq, g_w_ukv, g_w_out, d_ada, g_norm_w, g_qln, g_kvln,
            _from_head_lanes(g_qhn), _from_head_lanes(g_khn))
```
